```python
import jax, jax.numpy as jnp
from jax import lax
import numpy as np

D_MODEL = 1024
BATCH = 8
SEQ = 4096
DEPTH = 2

GRID_W = 64
Q_BLOCK = 128
HEAD_DIM = 64
EPS = 1e-6
ROPE_THETA = 10000.0

A_HEADS = 8
A_KV_HEADS = 2
A_GROUPS = A_HEADS // A_KV_HEADS
A_Q = A_HEADS * HEAD_DIM
A_KV = A_KV_HEADS * HEAD_DIM
A_OUT = A_Q

B_HEADS = 8
B_NOPE = 64
B_ROPE = 32
B_V = 64
B_Q_LORA = 256
B_KV_LORA = 128
B_OUT = B_HEADS * B_V

C_HEADS = 16
C_KV_HEADS = 4
C_GROUPS = C_HEADS // C_KV_HEADS
C_Q = C_HEADS * HEAD_DIM
C_KV = C_KV_HEADS * HEAD_DIM
WINDOW = 128

EVEN_IN = A_Q + 2 * A_KV + A_OUT + B_Q_LORA + B_KV_LORA + B_ROPE + B_OUT
EVEN_MIX = A_OUT + B_OUT
ODD_IN = C_Q + 2 * C_KV + C_Q
ODD_MIX = C_Q
N_EVEN = (DEPTH + 1) // 2
N_ODD = DEPTH // 2

kernel_name = "hybrid_gqa_mla_swa_adaln_encoder"


def rms_norm(x, g):
    xf = x.astype(jnp.float32)
    y = xf * lax.rsqrt(jnp.mean(xf * xf, axis=-1, keepdims=True) + EPS)
    return (y * g.astype(jnp.float32)).astype(x.dtype)


def rope_cos_sin(pos, dim):
    inv = ROPE_THETA ** (-jnp.arange(0, dim, 2, dtype=jnp.float32) / dim)
    ang = pos.astype(jnp.float32)[:, None] * inv[None, :]
    return jnp.cos(ang), jnp.sin(ang)


def apply_rope(x, cos, sin):
    half = x.shape[-1] // 2
    x1, x2 = x[..., :half], x[..., half:]
    cos = cos.astype(x.dtype)
    sin = sin.astype(x.dtype)
    return jnp.concatenate([x1 * cos - x2 * sin, x1 * sin + x2 * cos], axis=-1)


def axial_rope(x, cos_r, sin_r, cos_c, sin_c):
    half = HEAD_DIM // 2
    xr = apply_rope(x[..., :half], cos_r[:, None, :], sin_r[:, None, :])
    xc = apply_rope(x[..., half:], cos_c[:, None, :], sin_c[:, None, :])
    return jnp.concatenate([xr, xc], axis=-1)


def to_blocks(t):
    b, s = t.shape[:2]
    return jnp.moveaxis(t.reshape((b, s // Q_BLOCK, Q_BLOCK) + t.shape[2:]), 1, 0)


def from_blocks(t):
    nb, b = t.shape[:2]
    return jnp.moveaxis(t, 0, 1).reshape((b, nb * Q_BLOCK) + t.shape[3:])


def dense_gqa_attention(q, k, v):
    b, s = q.shape[:2]
    scale = HEAD_DIM ** -0.5

    def block(qi):
        sc = jnp.einsum('bqkgd,bskd->bkgqs', qi, k).astype(jnp.float32) * scale
        p = jax.nn.softmax(sc, axis=-1).astype(v.dtype)
        return jnp.einsum('bkgqs,bskd->bqkgd', p, v)

    o = from_blocks(lax.map(block, to_blocks(q)))
    return o.reshape(b, s, A_HEADS * HEAD_DIM)


def mla_attention(q_lat, q_rope, c_kv, k_rope, w_uv):
    b, s = q_lat.shape[:2]
    scale = (B_NOPE + B_ROPE) ** -0.5

    def block(args):
        ql, qr = args
        sc = (jnp.einsum('bqhc,bsc->bhqs', ql, c_kv)
              + jnp.einsum('bqhr,bsr->bhqs', qr, k_rope)).astype(jnp.float32) * scale
        p = jax.nn.softmax(sc, axis=-1).astype(c_kv.dtype)
        return jnp.einsum('bhqs,bsc->bqhc', p, c_kv)

    o_lat = from_blocks(lax.map(block, (to_blocks(q_lat), to_blocks(q_rope))))
    o = jnp.einsum('bshc,chd->bshd', o_lat, w_uv)
    return o.reshape(b, s, B_HEADS * B_V)


def windowed_sink_attention(q, k, v, sink, slopes):
    b, s = q.shape[:2]
    nb = s // Q_BLOCK
    scale = HEAD_DIM ** -0.5
    qb = q.reshape(b, nb, Q_BLOCK, C_KV_HEADS, C_GROUPS, HEAD_DIM)

    def neighbours(t):
        tb = t.reshape(b, nb, Q_BLOCK, C_KV_HEADS, HEAD_DIM)
        tp = jnp.pad(tb, ((0, 0), (1, 1), (0, 0), (0, 0), (0, 0)))
        return jnp.concatenate([tp[:, :-2], tp[:, 1:-1], tp[:, 2:]], axis=2)

    kb, vb = neighbours(k), neighbours(v)
    rel = jnp.arange(3 * Q_BLOCK)[None, :] - Q_BLOCK - jnp.arange(Q_BLOCK)[:, None]
    key_pos = (jnp.arange(nb)[:, None] - 1) * Q_BLOCK + jnp.arange(3 * Q_BLOCK)[None, :]
    valid = (jnp.abs(rel) <= WINDOW)[None] & ((key_pos >= 0) & (key_pos < s))[:, None, :]
    bias = -slopes.reshape(C_KV_HEADS, C_GROUPS)[:, :, None, None] * jnp.abs(rel).astype(jnp.float32)
    sc = jnp.einsum('bnqkgd,bnskd->bnkgqs', qb, kb).astype(jnp.float32) * scale + bias
    sc = jnp.where(valid[None, :, None, None], sc, -jnp.inf)
    sink_l = sink.astype(jnp.float32).reshape(C_KV_HEADS, C_GROUPS)[None, None, :, :, None, None]
    m = jnp.maximum(jnp.max(sc, axis=-1, keepdims=True), sink_l)
    p = jnp.exp(sc - m)
    denom = jnp.sum(p, axis=-1, keepdims=True) + jnp.exp(sink_l - m)
    o = jnp.einsum('bnkgqs,bnskd->bnqkgd', (p / denom).astype(v.dtype), vb)
    return o.reshape(b, s, C_HEADS * HEAD_DIM)


def even_mixer(h, w_in, q_norm_a, k_norm_a, q_lora_norm, kv_lora_norm, w_uq, w_uk, w_uv, w_out,
               cos_r, sin_r, cos_c, sin_c, cos_t, sin_t):
    b, s, _ = h.shape
    proj = h @ w_in
    splits = list(np.cumsum([A_Q, A_KV, A_KV, A_OUT, B_Q_LORA, B_KV_LORA, B_ROPE]))
    qa, ka, va, ga, cq, ckv, kr, gb = jnp.split(proj, splits, axis=-1)
    qa = axial_rope(rms_norm(qa.reshape(b, s, A_HEADS, HEAD_DIM), q_norm_a), cos_r, sin_r, cos_c, sin_c)
    ka = axial_rope(rms_norm(ka.reshape(b, s, A_KV_HEADS, HEAD_DIM), k_norm_a), cos_r, sin_r, cos_c, sin_c)
    oa = dense_gqa_attention(qa.reshape(b, s, A_KV_HEADS, A_GROUPS, HEAD_DIM), ka,
                             va.reshape(b, s, A_KV_HEADS, HEAD_DIM))
    oa = oa * jax.nn.silu(ga)
    cq = rms_norm(cq, q_lora_norm)
    ckv = rms_norm(ckv, kv_lora_norm)
    qb = (cq @ w_uq).reshape(b, s, B_HEADS, B_NOPE + B_ROPE)
    q_nope, q_rope = qb[..., :B_NOPE], qb[..., B_NOPE:]
    q_rope = apply_rope(q_rope, cos_t[:, None, :], sin_t[:, None, :])
    k_rope = apply_rope(kr, cos_t, sin_t)
    q_lat = jnp.einsum('bshd,chd->bshc', q_nope, w_uk)
    ob = mla_attention(q_lat, q_rope, ckv, k_rope, w_uv) * jax.nn.silu(gb)
    return jnp.concatenate([oa, ob], axis=-1) @ w_out


def odd_mixer(h, w_in, sink, w_out, slopes):
    b, s, _ = h.shape
    proj = h @ w_in
    qc, kc, vc, gc = jnp.split(proj, [C_Q, C_Q + C_KV, C_Q + 2 * C_KV], axis=-1)
    oc = windowed_sink_attention(qc.reshape(b, s, C_KV_HEADS, C_GROUPS, HEAD_DIM),
                                 kc.reshape(b, s, C_KV_HEADS, HEAD_DIM),
                                 vc.reshape(b, s, C_KV_HEADS, HEAD_DIM), sink, slopes)
    return (oc * jax.nn.silu(gc)) @ w_out


def _fwd_setup_inputs(seed: int = 0) -> dict:
    key = jax.random.key(seed)
    ks = jax.random.split(key, 20)
    f32 = jnp.float32
    nrm = lambda k, shape, s: jax.random.normal(k, shape, f32) * s
    gain = lambda k, shape: 1.0 + 0.02 * jax.random.normal(k, shape, f32)
    return {
        "x": nrm(ks[0], (BATCH, SEQ, D_MODEL), 1.0),
        "c": nrm(ks[1], (BATCH, D_MODEL), 1.0),
        "norm_w": gain(ks[2], (DEPTH, D_MODEL)),
        "ada_w": nrm(ks[3], (DEPTH, D_MODEL, 3 * D_MODEL), 0.02),
        "ada_b": nrm(ks[4], (DEPTH, 3 * D_MODEL), 0.02),
        "even_w_in": nrm(ks[5], (N_EVEN, D_MODEL, EVEN_IN), D_MODEL ** -0.5),
        "a_q_norm": gain(ks[6], (N_EVEN, HEAD_DIM)),
        "a_k_norm": gain(ks[7], (N_EVEN, HEAD_DIM)),
        "b_q_lora_norm": gain(ks[8], (N_EVEN, B_Q_LORA)),
        "b_kv_lora_norm": gain(ks[9], (N_EVEN, B_KV_LORA)),
        "b_w_uq": nrm(ks[10], (N_EVEN, B_Q_LORA, B_HEADS * (B_NOPE + B_ROPE)), B_Q_LORA ** -0.5),
        "b_w_uk": nrm(ks[11], (N_EVEN, B_KV_LORA, B_HEADS, B_NOPE), B_KV_LORA ** -0.5),
        "b_w_uv": nrm(ks[12], (N_EVEN, B_KV_LORA, B_HEADS, B_V), B_KV_LORA ** -0.5),
        "even_w_out": nrm(ks[13], (N_EVEN, EVEN_MIX, D_MODEL), EVEN_MIX ** -0.5),
        "odd_w_in": nrm(ks[14], (N_ODD, D_MODEL, ODD_IN), D_MODEL ** -0.5),
        "c_sink": nrm(ks[15], (N_ODD, C_HEADS), 0.5),
        "odd_w_out": nrm(ks[16], (N_ODD, ODD_MIX, D_MODEL), ODD_MIX ** -0.5),
        "final_norm": gain(ks[17], (D_MODEL,)),
    }


def _fwd_reference(x, c, norm_w, ada_w, ada_b, even_w_in, a_q_norm, a_k_norm, b_q_lora_norm,
              b_kv_lora_norm, b_w_uq, b_w_uk, b_w_uv, even_w_out, odd_w_in, c_sink, odd_w_out,
              final_norm):
    s = x.shape[1]
    rows = s // GRID_W
    row = jnp.repeat(jnp.arange(rows), GRID_W)
    col = jnp.tile(jnp.arange(GRID_W), rows)
    tok = jnp.arange(s)
    cos_r, sin_r = rope_cos_sin(row, HEAD_DIM // 2)
    cos_c, sin_c = rope_cos_sin(col, HEAD_DIM // 2)
    cos_t, sin_t = rope_cos_sin(tok, B_ROPE)
    slopes = 2.0 ** (-8.0 * jnp.arange(1, C_HEADS + 1, dtype=jnp.float32) / C_HEADS)
    c_act = jax.nn.silu(c)
    for layer in range(DEPTH):
        mod = c_act @ ada_w[layer] + ada_b[layer]
        shift, scale, gate = jnp.split(mod, 3, axis=-1)
        h = rms_norm(x, norm_w[layer]) * (1.0 + scale[:, None, :]) + shift[:, None, :]
        if layer % 2 == 0:
            i = layer // 2
            y = even_mixer(h, even_w_in[i], a_q_norm[i], a_k_norm[i], b_q_lora_norm[i],
                           b_kv_lora_norm[i], b_w_uq[i], b_w_uk[i], b_w_uv[i], even_w_out[i],
                           cos_r, sin_r, cos_c, sin_c, cos_t, sin_t)
        else:
            i = layer // 2
            y = odd_mixer(h, odd_w_in[i], c_sink[i], odd_w_out[i], slopes)
        x = x + gate[:, None, :] * y
    return rms_norm(x, final_norm)


import jax as _jax
import jax.numpy as _jnp

TWIN_FORMAT = 'train_step'
FWD_PARAMS = ['x', 'c', 'norm_w', 'ada_w', 'ada_b', 'even_w_in', 'a_q_norm', 'a_k_norm', 'b_q_lora_norm', 'b_kv_lora_norm', 'b_w_uq', 'b_w_uk', 'b_w_uv', 'even_w_out', 'odd_w_in', 'c_sink', 'odd_w_out', 'final_norm']
TWIN_WEIGHTS = ['norm_w', 'ada_w', 'ada_b', 'even_w_in', 'a_q_norm', 'a_k_norm', 'b_q_lora_norm', 'b_kv_lora_norm', 'b_w_uq', 'b_w_uk', 'b_w_uv', 'even_w_out', 'odd_w_in', 'c_sink', 'odd_w_out', 'final_norm']
TWIN_DIFF_INPUT = 'x'
TWIN_INPUTS = ['x', 'c', 'norm_w', 'ada_w', 'ada_b', 'even_w_in', 'a_q_norm', 'a_k_norm', 'b_q_lora_norm', 'b_kv_lora_norm', 'b_w_uq', 'b_w_uk', 'b_w_uv', 'even_w_out', 'odd_w_in', 'c_sink', 'odd_w_out', 'final_norm', 'loss_target', 'm_norm_w', 'm_ada_w', 'm_ada_b', 'm_even_w_in', 'm_a_q_norm', 'm_a_k_norm', 'm_b_q_lora_norm', 'm_b_kv_lora_norm', 'm_b_w_uq', 'm_b_w_uk', 'm_b_w_uv', 'm_even_w_out', 'm_odd_w_in', 'm_c_sink', 'm_odd_w_out', 'm_final_norm', 'v_norm_w', 'v_ada_w', 'v_ada_b', 'v_even_w_in', 'v_a_q_norm', 'v_a_k_norm', 'v_b_q_lora_norm', 'v_b_kv_lora_norm', 'v_b_w_uq', 'v_b_w_uk', 'v_b_w_uv', 'v_even_w_out', 'v_odd_w_in', 'v_c_sink', 'v_odd_w_out', 'v_final_norm']
TWIN_OUTPUTS = ['loss', 'grad_x', 'grad_norm_w', 'grad_ada_w', 'grad_ada_b', 'grad_even_w_in', 'grad_a_q_norm', 'grad_a_k_norm', 'grad_b_q_lora_norm', 'grad_b_kv_lora_norm', 'grad_b_w_uq', 'grad_b_w_uk', 'grad_b_w_uv', 'grad_even_w_out', 'grad_odd_w_in', 'grad_c_sink', 'grad_odd_w_out', 'grad_final_norm', 'delta_norm_w', 'delta_ada_w', 'delta_ada_b', 'delta_even_w_in', 'delta_a_q_norm', 'delta_a_k_norm', 'delta_b_q_lora_norm', 'delta_b_kv_lora_norm', 'delta_b_w_uq', 'delta_b_w_uk', 'delta_b_w_uv', 'delta_even_w_out', 'delta_odd_w_in', 'delta_c_sink', 'delta_odd_w_out', 'delta_final_norm', 'new_m_norm_w', 'new_m_ada_w', 'new_m_ada_b', 'new_m_even_w_in', 'new_m_a_q_norm', 'new_m_a_k_norm', 'new_m_b_q_lora_norm', 'new_m_b_kv_lora_norm', 'new_m_b_w_uq', 'new_m_b_w_uk', 'new_m_b_w_uv', 'new_m_even_w_out', 'new_m_odd_w_in', 'new_m_c_sink', 'new_m_odd_w_out', 'new_m_final_norm', 'new_v_norm_w', 'new_v_ada_w', 'new_v_ada_b', 'new_v_even_w_in', 'new_v_a_q_norm', 'new_v_a_k_norm', 'new_v_b_q_lora_norm', 'new_v_b_kv_lora_norm', 'new_v_b_w_uq', 'new_v_b_w_uk', 'new_v_b_w_uv', 'new_v_even_w_out', 'new_v_odd_w_in', 'new_v_c_sink', 'new_v_odd_w_out', 'new_v_final_norm']
TWIN_LEAF_KINDS = {'loss': 'loss', 'grad_x': 'grad_x', 'grad_norm_w': 'grad_w', 'grad_ada_w': 'grad_w', 'grad_ada_b': 'grad_w', 'grad_even_w_in': 'grad_w', 'grad_a_q_norm': 'grad_w', 'grad_a_k_norm': 'grad_w', 'grad_b_q_lora_norm': 'grad_w', 'grad_b_kv_lora_norm': 'grad_w', 'grad_b_w_uq': 'grad_w', 'grad_b_w_uk': 'grad_w', 'grad_b_w_uv': 'grad_w', 'grad_even_w_out': 'grad_w', 'grad_odd_w_in': 'grad_w', 'grad_c_sink': 'grad_w', 'grad_odd_w_out': 'grad_w', 'grad_final_norm': 'grad_w', 'delta_norm_w': 'delta_w', 'delta_ada_w': 'delta_w', 'delta_ada_b': 'delta_w', 'delta_even_w_in': 'delta_w', 'delta_a_q_norm': 'delta_w', 'delta_a_k_norm': 'delta_w', 'delta_b_q_lora_norm': 'delta_w', 'delta_b_kv_lora_norm': 'delta_w', 'delta_b_w_uq': 'delta_w', 'delta_b_w_uk': 'delta_w', 'delta_b_w_uv': 'delta_w', 'delta_even_w_out': 'delta_w', 'delta_odd_w_in': 'delta_w', 'delta_c_sink': 'delta_w', 'delta_odd_w_out': 'delta_w', 'delta_final_norm': 'delta_w', 'new_m_norm_w': 'new_m', 'new_m_ada_w': 'new_m', 'new_m_ada_b': 'new_m', 'new_m_even_w_in': 'new_m', 'new_m_a_q_norm': 'new_m', 'new_m_a_k_norm': 'new_m', 'new_m_b_q_lora_norm': 'new_m', 'new_m_b_kv_lora_norm': 'new_m', 'new_m_b_w_uq': 'new_m', 'new_m_b_w_uk': 'new_m', 'new_m_b_w_uv': 'new_m', 'new_m_even_w_out': 'new_m', 'new_m_odd_w_in': 'new_m', 'new_m_c_sink': 'new_m', 'new_m_odd_w_out': 'new_m', 'new_m_final_norm': 'new_m', 'new_v_norm_w': 'new_v', 'new_v_ada_w': 'new_v', 'new_v_ada_b': 'new_v', 'new_v_even_w_in': 'new_v', 'new_v_a_q_norm': 'new_v', 'new_v_a_k_norm': 'new_v', 'new_v_b_q_lora_norm': 'new_v', 'new_v_b_kv_lora_norm': 'new_v', 'new_v_b_w_uq': 'new_v', 'new_v_b_w_uk': 'new_v', 'new_v_b_w_uv': 'new_v', 'new_v_even_w_out': 'new_v', 'new_v_odd_w_in': 'new_v', 'new_v_c_sink': 'new_v', 'new_v_odd_w_out': 'new_v', 'new_v_final_norm': 'new_v'}


def _forward(args):
    return _fwd_reference(*[args[k] for k in FWD_PARAMS])


def _output_shape():
    def fwd():
        inp = _fwd_setup_inputs(0)
        return _fwd_reference(*[inp[k] for k in FWD_PARAMS])
    out = _jax.eval_shape(fwd)
    return out.shape, out.dtype

N_MICROBATCH = 1
ADAM_LR = 0.001
ADAM_B1 = 0.9
ADAM_B2 = 0.999
ADAM_EPS = 1e-08
ADAM_WD = 0.01
ADAM_STEP = 10
PER_EXAMPLE_BATCH_AXIS = {'x': 0, 'c': 0, 'loss_target': 0}
SHARED_INPUTS = []
_WEIGHT_DTYPES = {'norm_w': _jnp.float32, 'ada_w': _jnp.float32, 'ada_b': _jnp.float32, 'even_w_in': _jnp.float32, 'a_q_norm': _jnp.float32, 'a_k_norm': _jnp.float32, 'b_q_lora_norm': _jnp.float32, 'b_kv_lora_norm': _jnp.float32, 'b_w_uq': _jnp.float32, 'b_w_uk': _jnp.float32, 'b_w_uv': _jnp.float32, 'even_w_out': _jnp.float32, 'odd_w_in': _jnp.float32, 'c_sink': _jnp.float32, 'odd_w_out': _jnp.float32, 'final_norm': _jnp.float32}
MOMENT_SCALE = {'norm_w': 2.682123e-02, 'ada_w': 2.716515e-02, 'ada_b': 4.522126e-02, 'even_w_in': 1.854859e-02, 'a_q_norm': 1.144208e-02, 'a_k_norm': 1.235742e-02, 'b_q_lora_norm': 8.150409e-03, 'b_kv_lora_norm': 3.736320e-02, 'b_w_uq': 4.551363e-03, 'b_w_uk': 4.721110e-03, 'b_w_uv': 1.714791e-02, 'even_w_out': 1.826841e-02, 'odd_w_in': 2.421479e-02, 'c_sink': 4.511270e-02, 'odd_w_out': 2.307845e-02, 'final_norm': 3.202355e+01}


def _to_microbatches(a, axis):
    t = _jnp.moveaxis(a, axis, 0)
    t = t.reshape((N_MICROBATCH, t.shape[0] // N_MICROBATCH) + t.shape[1:])
    return _jnp.moveaxis(t, 1, axis + 1)


def setup_inputs(seed: int = 0) -> dict:
    inp = _fwd_setup_inputs(seed)
    key = _jax.random.fold_in(_jax.random.key(seed), 7919)
    shape, _ = _output_shape()
    out = dict(inp)
    out["loss_target"] = _jax.random.normal(_jax.random.fold_in(key, 0), shape, _jnp.float32)
    for i, name in enumerate(TWIN_WEIGHTS):
        w = inp[name].astype(_jnp.float32)
        if MOMENT_SCALE is None:
            s = _jnp.sqrt(_jnp.mean(_jnp.square(w)) + 1e-30)
        else:
            s = MOMENT_SCALE[name]
        km, kv = _jax.random.split(_jax.random.fold_in(key, i + 1))
        out[name] = w
        out["m_" + name] = s * _jax.random.normal(km, w.shape, _jnp.float32)
        out["v_" + name] = (s * s) * _jax.random.uniform(kv, w.shape, _jnp.float32, 0.5, 1.5)
    if N_MICROBATCH > 1:
        for name, axis in PER_EXAMPLE_BATCH_AXIS.items():
            out[name] = _to_microbatches(out[name], axis)
    return {'x': out['x'], 'c': out['c'], 'norm_w': out['norm_w'], 'ada_w': out['ada_w'], 'ada_b': out['ada_b'], 'even_w_in': out['even_w_in'], 'a_q_norm': out['a_q_norm'], 'a_k_norm': out['a_k_norm'], 'b_q_lora_norm': out['b_q_lora_norm'], 'b_kv_lora_norm': out['b_kv_lora_norm'], 'b_w_uq': out['b_w_uq'], 'b_w_uk': out['b_w_uk'], 'b_w_uv': out['b_w_uv'], 'even_w_out': out['even_w_out'], 'odd_w_in': out['odd_w_in'], 'c_sink': out['c_sink'], 'odd_w_out': out['odd_w_out'], 'final_norm': out['final_norm'], 'loss_target': out['loss_target'], 'm_norm_w': out['m_norm_w'], 'm_ada_w': out['m_ada_w'], 'm_ada_b': out['m_ada_b'], 'm_even_w_in': out['m_even_w_in'], 'm_a_q_norm': out['m_a_q_norm'], 'm_a_k_norm': out['m_a_k_norm'], 'm_b_q_lora_norm': out['m_b_q_lora_norm'], 'm_b_kv_lora_norm': out['m_b_kv_lora_norm'], 'm_b_w_uq': out['m_b_w_uq'], 'm_b_w_uk': out['m_b_w_uk'], 'm_b_w_uv': out['m_b_w_uv'], 'm_even_w_out': out['m_even_w_out'], 'm_odd_w_in': out['m_odd_w_in'], 'm_c_sink': out['m_c_sink'], 'm_odd_w_out': out['m_odd_w_out'], 'm_final_norm': out['m_final_norm'], 'v_norm_w': out['v_norm_w'], 'v_ada_w': out['v_ada_w'], 'v_ada_b': out['v_ada_b'], 'v_even_w_in': out['v_even_w_in'], 'v_a_q_norm': out['v_a_q_norm'], 'v_a_k_norm': out['v_a_k_norm'], 'v_b_q_lora_norm': out['v_b_q_lora_norm'], 'v_b_kv_lora_norm': out['v_b_kv_lora_norm'], 'v_b_w_uq': out['v_b_w_uq'], 'v_b_w_uk': out['v_b_w_uk'], 'v_b_w_uv': out['v_b_w_uv'], 'v_even_w_out': out['v_even_w_out'], 'v_odd_w_in': out['v_odd_w_in'], 'v_c_sink': out['v_c_sink'], 'v_odd_w_out': out['v_odd_w_out'], 'v_final_norm': out['v_final_norm']}


def _loss(weights, diff, rest, loss_target):
    with _jax.named_scope("forward"):
        args = {**rest, TWIN_DIFF_INPUT: diff, **{k: w.astype(_WEIGHT_DTYPES[k]) for k, w in weights.items()}}
        y = _forward(args)
    with _jax.named_scope("loss_head"):
        err = _jnp.square(y.astype(_jnp.float32) - loss_target)
        return 0.5 * _jnp.sum(_jnp.mean(err, axis=-1)) if err.ndim else 0.5 * err


def _adamw(w, g, m, v):
    m = ADAM_B1 * m + (1.0 - ADAM_B1) * g
    v = ADAM_B2 * v + (1.0 - ADAM_B2) * _jnp.square(g)
    m_hat = m / (1.0 - ADAM_B1 ** ADAM_STEP)
    v_hat = v / (1.0 - ADAM_B2 ** ADAM_STEP)
    delta = -ADAM_LR * (m_hat / (_jnp.sqrt(v_hat) + ADAM_EPS) + ADAM_WD * w)
    return delta, m, v


def reference(x, c, norm_w, ada_w, ada_b, even_w_in, a_q_norm, a_k_norm, b_q_lora_norm, b_kv_lora_norm, b_w_uq, b_w_uk, b_w_uv, even_w_out, odd_w_in, c_sink, odd_w_out, final_norm, loss_target, m_norm_w, m_ada_w, m_ada_b, m_even_w_in, m_a_q_norm, m_a_k_norm, m_b_q_lora_norm, m_b_kv_lora_norm, m_b_w_uq, m_b_w_uk, m_b_w_uv, m_even_w_out, m_odd_w_in, m_c_sink, m_odd_w_out, m_final_norm, v_norm_w, v_ada_w, v_ada_b, v_even_w_in, v_a_q_norm, v_a_k_norm, v_b_q_lora_norm, v_b_kv_lora_norm, v_b_w_uq, v_b_w_uk, v_b_w_uv, v_even_w_out, v_odd_w_in, v_c_sink, v_odd_w_out, v_final_norm):
    given = dict(x=x, c=c, norm_w=norm_w, ada_w=ada_w, ada_b=ada_b, even_w_in=even_w_in, a_q_norm=a_q_norm, a_k_norm=a_k_norm, b_q_lora_norm=b_q_lora_norm, b_kv_lora_norm=b_kv_lora_norm, b_w_uq=b_w_uq, b_w_uk=b_w_uk, b_w_uv=b_w_uv, even_w_out=even_w_out, odd_w_in=odd_w_in, c_sink=c_sink, odd_w_out=odd_w_out, final_norm=final_norm, loss_target=loss_target, m_norm_w=m_norm_w, m_ada_w=m_ada_w, m_ada_b=m_ada_b, m_even_w_in=m_even_w_in, m_a_q_norm=m_a_q_norm, m_a_k_norm=m_a_k_norm, m_b_q_lora_norm=m_b_q_lora_norm, m_b_kv_lora_norm=m_b_kv_lora_norm, m_b_w_uq=m_b_w_uq, m_b_w_uk=m_b_w_uk, m_b_w_uv=m_b_w_uv, m_even_w_out=m_even_w_out, m_odd_w_in=m_odd_w_in, m_c_sink=m_c_sink, m_odd_w_out=m_odd_w_out, m_final_norm=m_final_norm, v_norm_w=v_norm_w, v_ada_w=v_ada_w, v_ada_b=v_ada_b, v_even_w_in=v_even_w_in, v_a_q_norm=v_a_q_norm, v_a_k_norm=v_a_k_norm, v_b_q_lora_norm=v_b_q_lora_norm, v_b_kv_lora_norm=v_b_kv_lora_norm, v_b_w_uq=v_b_w_uq, v_b_w_uk=v_b_w_uk, v_b_w_uv=v_b_w_uv, v_even_w_out=v_even_w_out, v_odd_w_in=v_odd_w_in, v_c_sink=v_c_sink, v_odd_w_out=v_odd_w_out, v_final_norm=v_final_norm)
    weights = {n: given[n] for n in TWIN_WEIGHTS}
    shared = {n: given[n] for n in SHARED_INPUTS}
    per_example = {n: given[n] for n in ['x', 'c']}
    grad_fn = _jax.value_and_grad(_loss, argnums=(0, 1))

    def one_microbatch(ex, loss_target):
        ex = dict(ex)
        diff = ex.pop(TWIN_DIFF_INPUT)
        return grad_fn(weights, diff, {**shared, **ex}, loss_target)

    if N_MICROBATCH == 1:
        loss, (grad_w, grad_x) = one_microbatch(per_example, given["loss_target"])
    else:
        def body(carry, xs):
            loss_sum, grad_sum = carry
            l_k, (gw_k, gx_k) = one_microbatch(xs[0], xs[1])
            with _jax.named_scope("update"):
                return (loss_sum + l_k, _jax.tree.map(_jnp.add, grad_sum, gw_k)), gx_k

        init = (_jnp.zeros((), _jnp.float32), _jax.tree.map(_jnp.zeros_like, weights))
        (loss, grad_w), grad_x = _jax.lax.scan(body, init, (per_example, given["loss_target"]))
    with _jax.named_scope("update"):
        delta_w, new_m, new_v = {}, {}, {}
        for n in TWIN_WEIGHTS:
            delta_w[n], new_m[n], new_v[n] = _adamw(weights[n], grad_w[n], given["m_" + n], given["v_" + n])
    return (loss, grad_x, *[grad_w[n] for n in TWIN_WEIGHTS], *[delta_w[n] for n in TWIN_WEIGHTS],
            *[new_m[n] for n in TWIN_WEIGHTS], *[new_v[n] for n in TWIN_WEIGHTS])
```

```python
import functools

import jax
import jax.numpy as jnp
from jax import lax
from jax.experimental import pallas as pl
from jax.experimental.pallas import tpu as pltpu

F32 = jnp.float32
MXU = jnp.bfloat16
EPS = 1e-6
ROPE_THETA = 10000.0
GRID_W = 64
HD = 64
N_DEV = 8
MESH_AXES = ("x", "y", "c")

A_HEADS, A_KV = 8, 2
B_HEADS, B_NOPE, B_ROPE, B_Q_LORA, B_KV_LORA = 8, 64, 32, 256, 128
B_QK = B_KV_LORA + B_ROPE
C_HEADS, C_KV = 16, 4
WINDOW = 128

ADAM_LR, ADAM_B1, ADAM_B2, ADAM_EPS, ADAM_WD, ADAM_STEP = 0.001, 0.9, 0.999, 1e-08, 0.01, 10

ROW_TILE = 256
VMEM_LIMIT = 56 * 1024 * 1024

E_QA, E_KA, E_VA, E_GA, E_CQ, E_CKV, E_GB, E_KR = (
    (0, 512), (512, 640), (640, 768), (768, 1280), (1280, 1536), (1536, 1664), (1664, 2176), (2176, 2208))
EVEN_IN = 2208
O_Q, O_K, O_V, O_G = (0, 1024), (1024, 1280), (1280, 1536), (1536, 2560)
ODD_IN = 2560


def _mm(a, b):
    return jnp.dot(a.astype(MXU), b.astype(MXU), preferred_element_type=F32)


def _mm_nt(a, b):
    return lax.dot_general(a.astype(MXU), b.astype(MXU), (((1,), (1,)), ((), ())), preferred_element_type=F32)


def _mm_tn(a, b):
    return lax.dot_general(a.astype(MXU), b.astype(MXU), (((0,), (0,)), ((), ())), preferred_element_type=F32)


def _sigmoid(z):
    return 1.0 / (1.0 + jnp.exp(-z))


def _silu(z):
    return z * _sigmoid(z)


def _dsilu(z):
    s = _sigmoid(z)
    return s * (1.0 + z * (1.0 - s))


def _rms(x):
    return lax.rsqrt(jnp.mean(x * x, axis=-1, keepdims=True) + EPS)


def _swap_halves(y, group):
    n = y.shape[-1]
    half = group // 2
    fwd = pltpu.roll(y, half, 1)
    if n == group:
        return fwd
    back = pltpu.roll(y, n - half, 1)
    lane = lax.broadcasted_iota(jnp.int32, y.shape, 1)
    return jnp.where((lane % group) < half, back, fwd)


def _rope(y, cos, sin, group):
    return y * cos + _swap_halves(y, group) * sin


def _rope_t(d, cos, sin, group):
    return d * cos - _swap_halves(d, group) * sin


def _rms_bwd(dy, x, g):
    r = _rms(x)
    xhat = x * r
    dxhat = dy * g
    dx = r * (dxhat - xhat * jnp.mean(dxhat * xhat, axis=-1, keepdims=True))
    return dx, dy * xhat


def _params(sem, vmem=VMEM_LIMIT):
    return pltpu.CompilerParams(dimension_semantics=sem, vmem_limit_bytes=vmem)


def _row_spec(tm, w):
    return pl.BlockSpec((tm, w), lambda i: (i, 0))


def _full_spec(shape):
    nd = len(shape)
    return pl.BlockSpec(shape, lambda i: (0,) * nd)


def _head_spec(h, tm, w):
    return pl.BlockSpec((h, tm, w), lambda i: (0, i, 0))


def _me():
    return lax.axis_index("x"), lax.axis_index("y"), lax.axis_index("c")


def _flat(p):
    return 4 * p[0] + 2 * p[1] + p[2]


def _peer(me, k):
    x, y, c = me
    return (1 - x if k & 4 else x, 1 - y if k & 2 else y, 1 - c if k & 1 else c)


MESH_ID = pl.DeviceIdType.MESH


def all_gather_rows(x_shard, out_dtype, name):
    r, n = x_shard.shape

    def body(x_ref, out_ref, xs_ref, send_sems, recv_sems, local_sem):
        me = _me()
        x, y, c = me
        sibling = (x, y, 1 - c)
        chips = [(1 - x, y), (x, 1 - y), (1 - x, 1 - y)]
        xs_ref[...] = x_ref[...].astype(out_dtype)

        def rows(p):
            return out_ref.at[pl.ds(pl.multiple_of(_flat(p) * r, 16), r), :]

        def copy(k, block, to, src=None):
            return pltpu.make_async_remote_copy(
                src_ref=rows(block) if src is None else src, dst_ref=rows(block),
                send_sem=send_sems.at[k], recv_sem=recv_sems.at[k], device_id=to, device_id_type=MESH_ID)

        mine = pltpu.make_async_copy(xs_ref, rows(me), local_sem)
        mine.start()
        first = [copy(0, me, sibling, src=xs_ref)]
        first += [copy(1 + j, me, (*chip, c), src=xs_ref) for j, chip in enumerate(chips)]
        for cp in first:
            cp.start()
        passed = [copy(4 + j, (*chip, c), sibling) for j, chip in enumerate(chips)]
        for j, chip in enumerate(chips):
            copy(1 + j, (*chip, c), me).wait_recv()
            passed[j].start()
        copy(0, sibling, me).wait_recv()
        for j, chip in enumerate(chips):
            copy(4 + j, (*chip, 1 - c), me).wait_recv()
        for cp in first + passed:
            cp.wait_send()
        mine.wait()

    return pl.pallas_call(
        body, name=name,
        out_shape=jax.ShapeDtypeStruct((N_DEV * r, n), out_dtype),
        in_specs=[pl.BlockSpec(memory_space=pltpu.VMEM)],
        out_specs=pl.BlockSpec(memory_space=pltpu.VMEM),
        scratch_shapes=[pltpu.VMEM((r, n), out_dtype), pltpu.SemaphoreType.DMA((7,)),
                        pltpu.SemaphoreType.DMA((7,)), pltpu.SemaphoreType.DMA],
        compiler_params=pltpu.CompilerParams(vmem_limit_bytes=VMEM_LIMIT),
    )(x_shard)


def scatter_blocks(blocks, name):
    _, r, n = blocks.shape

    def body(g_ref, land_ref, send_sems, recv_sems, local_sem):
        me = _me()
        mi = _flat(me)
        local = pltpu.make_async_copy(g_ref.at[mi], land_ref.at[mi], local_sem)
        local.start()
        copies = []
        for k in range(1, N_DEV):
            peer = _peer(me, k)
            pi = _flat(peer)
            copies.append(pltpu.make_async_remote_copy(
                src_ref=g_ref.at[pi], dst_ref=land_ref.at[mi],
                send_sem=send_sems.at[k - 1], recv_sem=recv_sems.at[k - 1], device_id=peer, device_id_type=MESH_ID))
        for cp in copies:
            cp.start()
        for k in range(1, N_DEV):
            pi = _flat(_peer(me, k))
            pltpu.make_async_remote_copy(
                src_ref=g_ref.at[pi], dst_ref=land_ref.at[pi],
                send_sem=send_sems.at[k - 1], recv_sem=recv_sems.at[k - 1], device_id=_peer(me, k),
                device_id_type=MESH_ID).wait_recv()
        for cp in copies:
            cp.wait_send()
        local.wait()

    return pl.pallas_call(
        body, name=name,
        out_shape=jax.ShapeDtypeStruct(blocks.shape, blocks.dtype),
        in_specs=[pl.BlockSpec(memory_space=pl.ANY)],
        out_specs=pl.BlockSpec(memory_space=pl.ANY),
        scratch_shapes=[pltpu.SemaphoreType.DMA((7,)), pltpu.SemaphoreType.DMA((7,)), pltpu.SemaphoreType.DMA],
    )(blocks)


def ada_forward(c8, ada_w, bias_cols):
    d = c8.shape[1]
    w = ada_w.shape[2]

    def body(c_ref, w_ref, b_ref, call_ref, modp_ref, part_ref, s1, r1, s2, r2):
        me = _me()
        mi = _flat(me)
        call_ref[mi] = c_ref[...]
        gather = []
        for k in range(1, N_DEV):
            gather.append(pltpu.make_async_remote_copy(
                src_ref=c_ref, dst_ref=call_ref.at[mi], send_sem=s1.at[k - 1], recv_sem=r1.at[k - 1],
                device_id=_peer(me, k), device_id_type=MESH_ID))
        for cp in gather:
            cp.start()
        for k in range(1, N_DEV):
            pltpu.make_async_remote_copy(
                src_ref=c_ref, dst_ref=call_ref.at[_flat(_peer(me, k))], send_sem=s1.at[k - 1],
                recv_sem=r1.at[k - 1], device_id=_peer(me, k), device_id_type=MESH_ID).wait_recv()
        ca = _silu(call_ref[...].reshape(N_DEV * 8, d))
        for l in range(2):
            part = _mm(ca, w_ref[l]) + b_ref[l]
            for b in range(N_DEV):
                part_ref[b, l] = part[8 * b:8 * b + 8, :]
        modp_ref[mi] = part_ref[mi]
        spread = []
        for k in range(1, N_DEV):
            peer = _peer(me, k)
            spread.append(pltpu.make_async_remote_copy(
                src_ref=part_ref.at[_flat(peer)], dst_ref=modp_ref.at[mi], send_sem=s2.at[k - 1],
                recv_sem=r2.at[k - 1], device_id=peer, device_id_type=MESH_ID))
        for cp in spread:
            cp.start()
        for k in range(1, N_DEV):
            pi = _flat(_peer(me, k))
            pltpu.make_async_remote_copy(
                src_ref=part_ref.at[pi], dst_ref=modp_ref.at[pi], send_sem=s2.at[k - 1],
                recv_sem=r2.at[k - 1], device_id=_peer(me, k), device_id_type=MESH_ID).wait_recv()
        for cp in gather + spread:
            cp.wait_send()

    vm = pl.BlockSpec(memory_space=pltpu.VMEM)
    return pl.pallas_call(
        body, name="ada_forward",
        out_shape=(jax.ShapeDtypeStruct((N_DEV, 8, d), F32), jax.ShapeDtypeStruct((N_DEV, 2, 8, w), F32)),
        in_specs=[vm, vm, vm], out_specs=(vm, vm),
        scratch_shapes=[pltpu.VMEM((N_DEV, 2, 8, w), F32)] + [pltpu.SemaphoreType.DMA((7,))] * 4,
        compiler_params=pltpu.CompilerParams(vmem_limit_bytes=VMEM_LIMIT),
    )(c8, ada_w, bias_cols)


def _modulated(x, mod_ref, nw_ref):
    xn = x * _rms(x)
    g1 = nw_ref[...] * (1.0 + mod_ref[1:2, :])
    return xn, g1, xn * g1 + mod_ref[0:1, :]


def even_in_forward(x, mod, nw, w_in, qn, kn, qln, kvln, w_uq, w_uk, cos_a, sin_a, cos_t, sin_t):
    s, d = x.shape
    tm = min(ROW_TILE, s)

    def body(x_ref, mod_ref, nw_ref, w_ref, qn_ref, kn_ref, qln_ref, kvln_ref, uq_ref, uk_ref,
             ca_ref, sa_ref, ct_ref, st_ref,
             qa_o, ka_o, va_o, qb_o, kb_o, qa_raw_o, ka_raw_o, cq_raw_o, ckv_raw_o, ga_o, gb_o):
        _, _, h = _modulated(x_ref[...], mod_ref, nw_ref)
        h = h.astype(MXU)

        def proj(cols):
            return jnp.dot(h, w_ref[:, cols[0]:cols[1]], preferred_element_type=F32)

        ca, sa, ct, st = ca_ref[...], sa_ref[...], ct_ref[...], st_ref[...]
        qa = proj(E_QA)
        qa_raw_o[...] = qa
        for hh in range(A_HEADS):
            xh = qa[:, HD * hh:HD * hh + HD]
            qa_o[hh] = _rope(xh * _rms(xh) * qn_ref[...], ca, sa, 32).astype(MXU)
        ka = proj(E_KA)
        ka_raw_o[...] = ka
        va = proj(E_VA)
        for g in range(A_KV):
            xh = ka[:, HD * g:HD * g + HD]
            ka_o[g] = _rope(xh * _rms(xh) * kn_ref[...], ca, sa, 32).astype(MXU)
            va_o[g] = va[:, HD * g:HD * g + HD].astype(MXU)
        ga_o[...] = proj(E_GA)
        gb_o[...] = proj(E_GB)
        cq = proj(E_CQ)
        cq_raw_o[...] = cq
        qb = _mm(cq * _rms(cq) * qln_ref[...], uq_ref[...])
        for hh in range(B_HEADS):
            base = (B_NOPE + B_ROPE) * hh
            qb_o[hh, :, 0:B_KV_LORA] = _mm_nt(qb[:, base:base + B_NOPE], uk_ref[hh]).astype(MXU)
            qb_o[hh, :, B_KV_LORA:B_QK] = _rope(qb[:, base + B_NOPE:base + B_NOPE + B_ROPE], ct, st, 32).astype(MXU)
        ckv = proj(E_CKV)
        ckv_raw_o[...] = ckv
        kb_o[0, :, 0:B_KV_LORA] = (ckv * _rms(ckv) * kvln_ref[...]).astype(MXU)
        kb_o[0, :, B_KV_LORA:B_QK] = _rope(proj(E_KR), ct, st, 32).astype(MXU)

    sd = jax.ShapeDtypeStruct
    outs = (sd((A_HEADS, s, HD), MXU), sd((A_KV, s, HD), MXU), sd((A_KV, s, HD), MXU),
            sd((B_HEADS, s, B_QK), MXU), sd((1, s, B_QK), MXU),
            sd((s, 512), F32), sd((s, 128), F32), sd((s, B_Q_LORA), F32), sd((s, B_KV_LORA), F32),
            sd((s, 512), F32), sd((s, 512), F32))
    out_specs = (_head_spec(A_HEADS, tm, HD), _head_spec(A_KV, tm, HD), _head_spec(A_KV, tm, HD),
                 _head_spec(B_HEADS, tm, B_QK), _head_spec(1, tm, B_QK),
                 _row_spec(tm, 512), _row_spec(tm, 128), _row_spec(tm, B_Q_LORA), _row_spec(tm, B_KV_LORA),
                 _row_spec(tm, 512), _row_spec(tm, 512))
    in_specs = [_row_spec(tm, d), _full_spec(mod.shape), _full_spec(nw.shape), _full_spec(w_in.shape),
                _full_spec(qn.shape), _full_spec(kn.shape), _full_spec(qln.shape), _full_spec(kvln.shape),
                _full_spec(w_uq.shape), _full_spec(w_uk.shape),
                _row_spec(tm, HD), _row_spec(tm, HD), _row_spec(tm, B_ROPE), _row_spec(tm, B_ROPE)]
    return pl.pallas_call(
        body, name="even_in_forward", grid=(s // tm,), out_shape=outs, in_specs=in_specs, out_specs=out_specs,
        compiler_params=_params(("parallel",)),
    )(x, mod, nw, w_in, qn, kn, qln, kvln, w_uq, w_uk, cos_a, sin_a, cos_t, sin_t)


def odd_in_forward(x, mod, nw, w_in):
    s, d = x.shape
    tm = min(ROW_TILE, s)

    def body(x_ref, mod_ref, nw_ref, w_ref, q_o, k_o, v_o, g_o):
        _, _, h = _modulated(x_ref[...], mod_ref, nw_ref)
        h = h.astype(MXU)

        def proj(cols):
            return jnp.dot(h, w_ref[:, cols[0]:cols[1]], preferred_element_type=F32)

        q = proj(O_Q)
        for hh in range(C_HEADS):
            q_o[hh] = q[:, HD * hh:HD * hh + HD].astype(MXU)
        k = proj(O_K)
        v = proj(O_V)
        for g in range(C_KV):
            k_o[g] = k[:, HD * g:HD * g + HD].astype(MXU)
            v_o[g] = v[:, HD * g:HD * g + HD].astype(MXU)
        g_o[...] = proj(O_G)

    sd = jax.ShapeDtypeStruct
    return pl.pallas_call(
        body, name="odd_in_forward", grid=(s // tm,),
        out_shape=(sd((C_HEADS, s, HD), MXU), sd((C_KV, s, HD), MXU), sd((C_KV, s, HD), MXU), sd((s, 1024), F32)),
        in_specs=[_row_spec(tm, d), _full_spec(mod.shape), _full_spec(nw.shape), _full_spec(w_in.shape)],
        out_specs=(_head_spec(C_HEADS, tm, HD), _head_spec(C_KV, tm, HD), _head_spec(C_KV, tm, HD), _row_spec(tm, 1024)),
        compiler_params=_params(("parallel",)),
    )(x, mod, nw, w_in)


def latent_out_forward(o_lat, w_uv):
    s = o_lat.shape[0]
    tm = min(ROW_TILE, s)

    def body(o_ref, uv_ref, out_ref):
        for hh in range(B_HEADS):
            out_ref[:, HD * hh:HD * hh + HD] = _mm(o_ref[:, B_KV_LORA * hh:B_KV_LORA * (hh + 1)], uv_ref[hh])

    return pl.pallas_call(
        body, name="latent_out_forward", grid=(s // tm,),
        out_shape=jax.ShapeDtypeStruct((s, B_HEADS * HD), F32),
        in_specs=[_row_spec(tm, o_lat.shape[1]), _full_spec(w_uv.shape)],
        out_specs=_row_spec(tm, B_HEADS * HD),
        compiler_params=_params(("parallel",)),
    )(o_lat, w_uv)


def mixer_out_forward(x, mod, pairs, w_out, name):
    s, d = x.shape
    tm = min(ROW_TILE, s)
    n = len(pairs)
    widths = [o.shape[1] for o, _ in pairs]

    def body(*refs):
        x_ref, mod_ref, w_ref = refs[:3]
        pr = refs[3:3 + 2 * n]
        xo_ref, y_ref = refs[3 + 2 * n:]
        y = jnp.zeros((tm, d), F32)
        r0 = 0
        for i in range(n):
            mix = pr[2 * i][...] * _silu(pr[2 * i + 1][...])
            y = y + _mm(mix, w_ref[r0:r0 + widths[i], :])
            r0 += widths[i]
        y_ref[...] = y
        xo_ref[...] = x_ref[...] + mod_ref[2:3, :] * y

    flat = [a for p in pairs for a in p]
    sd = jax.ShapeDtypeStruct
    return pl.pallas_call(
        body, name=name, grid=(s // tm,),
        out_shape=(sd((s, d), F32), sd((s, d), F32)),
        in_specs=[_row_spec(tm, d), _full_spec(mod.shape), _full_spec(w_out.shape)]
        + [_row_spec(tm, a.shape[1]) for a in flat],
        out_specs=(_row_spec(tm, d), _row_spec(tm, d)),
        compiler_params=_params(("parallel",)),
    )(x, mod, w_out, *flat)


def _band_terms(i_blk, k_blk, hpg, t, slope_ref):
    rowi = lax.broadcasted_iota(jnp.int32, (t, t), 0)
    coli = lax.broadcasted_iota(jnp.int32, (t, t), 1)
    rel = (k_blk * t + coli) - (i_blk * t + rowi)
    arel = jnp.abs(rel)
    valid = (arel <= WINDOW)[None]
    bias = -slope_ref[...] * arel.astype(F32)[None]
    return bias, valid


def flash_forward(q, k, v, *, scale, dv, tq, tk, name, banded=False, sink=None, slopes=None):
    hq, s, dq = q.shape
    g_kv = k.shape[0]
    hpg = hq // g_kv
    nq = s // tq
    nk = 3 if banded else s // tk
    m_rows = hpg * tq
    has_v = v is not None

    def body(*refs):
        it = iter(refs)
        q_ref, k_ref = next(it), next(it)
        v_ref = next(it) if has_v else None
        sink_ref = next(it) if banded else None
        slope_ref = next(it) if banded else None
        o_ref, lse_ref, m_s, l_s, acc_s = next(it), next(it), next(it), next(it), next(it)
        i = pl.program_id(1)
        j = pl.program_id(2)

        @pl.when(j == 0)
        def _():
            if banded:
                m_s[...] = jnp.broadcast_to(sink_ref[...], (hpg, tq, 1)).reshape(m_rows, 1)
                l_s[...] = jnp.ones((m_rows, 1), F32)
            else:
                m_s[...] = jnp.full((m_rows, 1), -jnp.inf, F32)
                l_s[...] = jnp.zeros((m_rows, 1), F32)
            acc_s[...] = jnp.zeros((m_rows, dv), F32)

        def step():
            qq = q_ref[...].reshape(m_rows, dq)
            kk = k_ref[0]
            vv = v_ref[0] if has_v else kk[:, :dv]
            sc = _mm_nt(qq, kk) * scale
            if banded:
                bias, valid = _band_terms(i, i - 1 + j, hpg, tq, slope_ref)
                sc = jnp.where(valid, sc.reshape(hpg, tq, tk) + bias, -jnp.inf).reshape(m_rows, tk)
            m_old = m_s[...]
            m_new = jnp.maximum(m_old, jnp.max(sc, axis=-1, keepdims=True))
            alpha = jnp.exp(m_old - m_new)
            p = jnp.exp(sc - m_new)
            l_s[...] = alpha * l_s[...] + jnp.sum(p, axis=-1, keepdims=True)
            acc_s[...] = alpha * acc_s[...] + _mm(p, vv)
            m_s[...] = m_new

        if banded:
            kb = i - 1 + j
            pl.when((kb >= 0) & (kb < nq))(step)
        else:
            step()

        @pl.when(j == nk - 1)
        def _():
            l = l_s[...]
            o = acc_s[...] / l
            for hh in range(hpg):
                o_ref[:, dv * hh:dv * hh + dv] = o[tq * hh:tq * hh + tq, :]
            lse_ref[...] = (m_s[...] + jnp.log(l)).reshape(hpg, tq, 1)

    if banded:
        k_map = lambda g, i, j: (g, jnp.clip(i - 1 + j, 0, nq - 1), 0)
    else:
        k_map = lambda g, i, j: (g, j, 0)
    in_specs = [pl.BlockSpec((hpg, tq, dq), lambda g, i, j: (g, i, 0)), pl.BlockSpec((1, tk, k.shape[2]), k_map)]
    args = [q, k]
    if has_v:
        in_specs.append(pl.BlockSpec((1, tk, dv), k_map))
        args.append(v)
    if banded:
        in_specs += [pl.BlockSpec((hpg, 1, 1), lambda g, i, j: (g, 0, 0))] * 2
        args += [sink, slopes]
    sd = jax.ShapeDtypeStruct
    return pl.pallas_call(
        body, name=name, grid=(g_kv, nq, nk),
        out_shape=(sd((s, hq * dv), F32), sd((hq, s, 1), F32)),
        in_specs=in_specs,
        out_specs=(pl.BlockSpec((tq, hpg * dv), lambda g, i, j: (i, g)),
                   pl.BlockSpec((hpg, tq, 1), lambda g, i, j: (g, i, 0))),
        scratch_shapes=[pltpu.VMEM((m_rows, 1), F32), pltpu.VMEM((m_rows, 1), F32), pltpu.VMEM((m_rows, dv), F32)],
        compiler_params=_params(("parallel", "parallel", "arbitrary")),
    )(*args)


def flash_backward(q, k, v, do, lse, delta, *, scale, dv, tq, tk, gq, name, banded=False, slopes=None):
    hq, s, dq = q.shape
    g_kv, _, dk_w = k.shape
    hpg = hq // gq
    nq = s // tq
    nkb = s // tk
    n_in = 3 if banded else nq
    m_rows = hpg * tq
    has_v = v is not None

    def body(*refs):
        it = iter(refs)
        q_ref, k_ref = next(it), next(it)
        v_ref = next(it) if has_v else None
        do_ref, lse_ref, dl_ref = next(it), next(it), next(it)
        slope_ref = next(it) if banded else None
        dq_ref, dk_ref, dv_ref = next(it), next(it), next(it)
        kj = pl.program_id(1)
        jj = pl.program_id(2)
        qi = kj - 1 + jj if banded else jj

        @pl.when((kj == 0) & (jj == 0))
        def _():
            dq_ref[...] = jnp.zeros(dq_ref.shape, F32)

        @pl.when(jj == 0)
        def _():
            dk_ref[...] = jnp.zeros(dk_ref.shape, F32)
            dv_ref[...] = jnp.zeros(dv_ref.shape, F32)

        def step():
            qq = q_ref[...].reshape(m_rows, dq)
            kk = k_ref[0]
            vv = v_ref[0] if has_v else kk[:, :dv]
            dd = jnp.concatenate([do_ref[:, dv * hh:dv * hh + dv] for hh in range(hpg)], axis=0)
            sc = _mm_nt(qq, kk) * scale
            if banded:
                bias, valid = _band_terms(qi, kj, hpg, tq, slope_ref)
                sc = jnp.where(valid, sc.reshape(hpg, tq, tk) + bias, -jnp.inf).reshape(m_rows, tk)
            p = jnp.exp(sc - lse_ref[...].reshape(m_rows, 1))
            dp = _mm_nt(dd, vv)
            ds = p * (dp - dl_ref[...].reshape(m_rows, 1)) * scale
            dv_ref[0] += _mm_tn(p, dd)
            dk_ref[0] += _mm_tn(ds, qq)
            dq4 = _mm(ds, kk)
            r0 = pl.multiple_of(qi * tq, tq)
            for hh in range(hpg):
                dq_ref[pl.ds(r0, tq), dq * hh:dq * hh + dq] += dq4[tq * hh:tq * hh + tq, :]

        if banded:
            pl.when((qi >= 0) & (qi < nq))(step)
        else:
            step()

    kv_of = lambda g: g * g_kv // gq
    if banded:
        q_map = lambda g, kj, jj: (g, jnp.clip(kj - 1 + jj, 0, nq - 1), 0)
        do_map = lambda g, kj, jj: (jnp.clip(kj - 1 + jj, 0, nq - 1), g)
    else:
        q_map = lambda g, kj, jj: (g, jj, 0)
        do_map = lambda g, kj, jj: (jj, g)
    k_map = lambda g, kj, jj: (kv_of(g), kj, 0)
    in_specs = [pl.BlockSpec((hpg, tq, dq), q_map), pl.BlockSpec((1, tk, dk_w), k_map)]
    args = [q, k]
    if has_v:
        in_specs.append(pl.BlockSpec((1, tk, dv), k_map))
        args.append(v)
    in_specs += [pl.BlockSpec((tq, hpg * dv), do_map), pl.BlockSpec((hpg, tq, 1), q_map), pl.BlockSpec((hpg, tq, 1), q_map)]
    args += [do, lse, delta]
    if banded:
        in_specs.append(pl.BlockSpec((hpg, 1, 1), lambda g, kj, jj: (g, 0, 0)))
        args.append(slopes)
    sd = jax.ShapeDtypeStruct
    return pl.pallas_call(
        body, name=name, grid=(gq, nkb, n_in),
        out_shape=(sd((s, hq * dq), F32), sd((gq, s, dk_w), F32), sd((gq, s, dv), F32)),
        in_specs=in_specs,
        out_specs=(pl.BlockSpec((s, hpg * dq), lambda g, kj, jj: (0, g)),
                   pl.BlockSpec((1, tk, dk_w), lambda g, kj, jj: (g, kj, 0)),
                   pl.BlockSpec((1, tk, dv), lambda g, kj, jj: (g, kj, 0))),
        compiler_params=_params(("parallel", "arbitrary", "arbitrary")),
    )(*args)


def loss_head(x, target, fnw):
    s, d = x.shape
    tm = min(ROW_TILE, s)

    def body(x_ref, t_ref, w_ref, lp_ref, dx_ref, dw_ref):
        @pl.when(pl.program_id(0) == 0)
        def _():
            lp_ref[...] = jnp.zeros(lp_ref.shape, F32)
            dw_ref[...] = jnp.zeros(dw_ref.shape, F32)

        x = x_ref[...]
        g = w_ref[...]
        err = x * _rms(x) * g - t_ref[...]
        lp_ref[...] += jnp.sum(err * err, axis=0, keepdims=True)
        dx, dg = _rms_bwd(err * (1.0 / d), x, g)
        dx_ref[...] = dx
        dw_ref[...] += jnp.sum(dg, axis=0, keepdims=True)

    sd = jax.ShapeDtypeStruct
    return pl.pallas_call(
        body, name="loss_head", grid=(s // tm,),
        out_shape=(sd((1, d), F32), sd((s, d), F32), sd((1, d), F32)),
        in_specs=[_row_spec(tm, d), _row_spec(tm, d), _full_spec(fnw.shape)],
        out_specs=(_full_spec((1, d)), _row_spec(tm, d), _full_spec((1, d))),
        compiler_params=_params(("arbitrary",)),
    )(x, target, fnw)


def mixer_out_backward(dx, y, mod, pairs, w_out_t, delta_heads, name, lse=None, sink=None):
    s, d = dx.shape
    tm = min(ROW_TILE, s)
    n = len(pairs)
    widths = [o.shape[1] for o, _ in pairs]
    n_delta = sum(1 for h in delta_heads if h)
    with_sink = lse is not None

    def body(*refs):
        it = iter(refs)
        dx_ref, y_ref, mod_ref, wt_ref = next(it), next(it), next(it), next(it)
        pr = [next(it) for _ in range(2 * n)]
        lse_ref = next(it) if with_sink else None
        sink_ref = next(it) if with_sink else None
        outs = [next(it) for _ in range(2 * n)]
        dl_refs = [next(it) for _ in range(n_delta)]
        dgate_ref, dw_ref = next(it), next(it)
        dsink_ref = next(it) if with_sink else None

        @pl.when(pl.program_id(0) == 0)
        def _():
            dgate_ref[...] = jnp.zeros(dgate_ref.shape, F32)
            dw_ref[...] = jnp.zeros(dw_ref.shape, F32)
            if with_sink:
                dsink_ref[...] = jnp.zeros(dsink_ref.shape, F32)

        dxo = dx_ref[...]
        dgate_ref[...] += jnp.sum(dxo * y_ref[...], axis=0, keepdims=True)
        dy = (dxo * mod_ref[2:3, :]).astype(MXU)
        dmix = jnp.dot(dy, wt_ref[...], preferred_element_type=F32)
        r0 = 0
        di = 0
        for i in range(n):
            o = pr[2 * i][...]
            g = pr[2 * i + 1][...]
            dm = dmix[:, r0:r0 + widths[i]]
            sg = _sigmoid(g)
            act = g * sg
            do = dm * act
            outs[2 * i][...] = do.astype(MXU)
            outs[2 * i + 1][...] = (dm * o * (sg * (1.0 + g * (1.0 - sg)))).astype(MXU)
            dw_ref[r0:r0 + widths[i], :] += _mm_tn(o * act, dy)
            if delta_heads[i]:
                prod = do * o
                for hh in range(delta_heads[i]):
                    dlt = jnp.sum(prod[:, HD * hh:HD * hh + HD], axis=-1, keepdims=True)
                    dl_refs[di][hh] = dlt
                    if with_sink:
                        ps = jnp.exp(sink_ref[hh] - lse_ref[hh])
                        dsink_ref[hh] += -jnp.sum(ps * dlt, axis=0, keepdims=True)
                di += 1
            r0 += widths[i]

    flat = [a for p in pairs for a in p]
    sd = jax.ShapeDtypeStruct
    in_specs = [_row_spec(tm, d), _row_spec(tm, d), _full_spec(mod.shape), _full_spec(w_out_t.shape)]
    in_specs += [_row_spec(tm, a.shape[1]) for a in flat]
    args = [dx, y, mod, w_out_t] + flat
    if with_sink:
        nh = lse.shape[0]
        in_specs += [_head_spec(nh, tm, 1), _full_spec(sink.shape)]
        args += [lse, sink]
    out_shape = [sd((s, a.shape[1]), MXU) for a in flat]
    out_specs = [_row_spec(tm, a.shape[1]) for a in flat]
    for h in delta_heads:
        if h:
            out_shape.append(sd((h, s, 1), F32))
            out_specs.append(_head_spec(h, tm, 1))
    out_shape += [sd((1, d), F32), sd((sum(widths), d), F32)]
    out_specs += [_full_spec((1, d)), _full_spec((sum(widths), d))]
    if with_sink:
        out_shape.append(sd((lse.shape[0], 1, 1), F32))
        out_specs.append(_full_spec((lse.shape[0], 1, 1)))
    return pl.pallas_call(
        body, name=name, grid=(s // tm,), out_shape=tuple(out_shape), in_specs=in_specs, out_specs=tuple(out_specs),
        compiler_params=_params(("arbitrary",)),
    )(*args)


def latent_out_backward(d_ob, o_lat, w_uv):
    s = o_lat.shape[0]
    tm = min(ROW_TILE, s)

    def body(d_ref, o_ref, uv_ref, dol_ref, dl_ref, duv_ref):
        @pl.when(pl.program_id(0) == 0)
        def _():
            duv_ref[...] = jnp.zeros(duv_ref.shape, F32)

        for hh in range(B_HEADS):
            dh = d_ref[:, HD * hh:HD * hh + HD]
            ol = o_ref[:, B_KV_LORA * hh:B_KV_LORA * (hh + 1)]
            dol = _mm_nt(dh, uv_ref[hh])
            dol_ref[:, B_KV_LORA * hh:B_KV_LORA * (hh + 1)] = dol.astype(MXU)
            dl_ref[hh] = jnp.sum(dol * ol, axis=-1, keepdims=True)
            duv_ref[hh] += _mm_tn(ol, dh)

    sd = jax.ShapeDtypeStruct
    return pl.pallas_call(
        body, name="latent_out_backward", grid=(s // tm,),
        out_shape=(sd(o_lat.shape, MXU), sd((B_HEADS, s, 1), F32), sd(w_uv.shape, F32)),
        in_specs=[_row_spec(tm, d_ob.shape[1]), _row_spec(tm, o_lat.shape[1]), _full_spec(w_uv.shape)],
        out_specs=(_row_spec(tm, o_lat.shape[1]), _head_spec(B_HEADS, tm, 1), _full_spec(w_uv.shape)),
        compiler_params=_params(("arbitrary",)),
    )(d_ob, o_lat, w_uv)


def even_prep_backward(dqa, dka, dva, dqb, dkb, dvb, qa_raw, ka_raw, cq_raw, ckv_raw,
                       qn, kn, qln, kvln, w_uq, w_uq_t, w_uk, cos_a, sin_a, cos_t, sin_t):
    s = qa_raw.shape[0]
    tm = min(ROW_TILE, s)
    qb_w = B_HEADS * (B_NOPE + B_ROPE)

    def body(dqa_ref, dka_ref, dva_ref, dqb_ref, dkb_ref, dvb_ref, qa_ref, ka_ref, cq_ref, ckv_ref,
             qn_ref, kn_ref, qln_ref, kvln_ref, uq_ref, uqt_ref, uk_ref, ca_ref, sa_ref, ct_ref, st_ref,
             pqa, pka, pva, pcq, pckv, pkr, gqn, gkn, gqln, gkvln, guq, guk, dqb_s):
        @pl.when(pl.program_id(0) == 0)
        def _():
            for r in (gqn, gkn, gqln, gkvln, guq, guk):
                r[...] = jnp.zeros(r.shape, F32)

        ca, sa, ct, st = ca_ref[...], sa_ref[...], ct_ref[...], st_ref[...]
        acc_q = jnp.zeros((1, HD), F32)
        for hh in range(A_HEADS):
            dyn = _rope_t(dqa_ref[:, HD * hh:HD * hh + HD], ca, sa, 32)
            dx, dg = _rms_bwd(dyn, qa_ref[:, HD * hh:HD * hh + HD], qn_ref[...])
            pqa[:, HD * hh:HD * hh + HD] = dx.astype(MXU)
            acc_q = acc_q + jnp.sum(dg, axis=0, keepdims=True)
        gqn[...] += acc_q
        acc_k = jnp.zeros((1, HD), F32)
        for g in range(A_KV):
            dyn = _rope_t(dka_ref[g], ca, sa, 32)
            dx, dg = _rms_bwd(dyn, ka_ref[:, HD * g:HD * g + HD], kn_ref[...])
            pka[:, HD * g:HD * g + HD] = dx.astype(MXU)
            acc_k = acc_k + jnp.sum(dg, axis=0, keepdims=True)
            pva[:, HD * g:HD * g + HD] = dva_ref[g].astype(MXU)
        gkn[...] += acc_k
        cq_raw = cq_ref[...]
        cq_n = cq_raw * _rms(cq_raw) * qln_ref[...]
        qb = _mm(cq_n, uq_ref[...])
        for hh in range(B_HEADS):
            base = (B_NOPE + B_ROPE) * hh
            dlat = dqb_ref[:, B_QK * hh:B_QK * hh + B_KV_LORA]
            dqb_s[:, base:base + B_NOPE] = _mm(dlat, uk_ref[hh])
            guk[hh] += _mm_tn(dlat, qb[:, base:base + B_NOPE])
            dqb_s[:, base + B_NOPE:base + B_NOPE + B_ROPE] = _rope_t(
                dqb_ref[:, B_QK * hh + B_KV_LORA:B_QK * (hh + 1)], ct, st, 32)
        dqb_all = dqb_s[...]
        guq[...] += _mm_tn(cq_n, dqb_all)
        dx, dg = _rms_bwd(_mm(dqb_all, uqt_ref[...]), cq_raw, qln_ref[...])
        pcq[...] = dx.astype(MXU)
        gqln[...] += jnp.sum(dg, axis=0, keepdims=True)
        dkb_sum = dkb_ref[0] + dkb_ref[1]
        dckv = dkb_sum[:, 0:B_KV_LORA] + dvb_ref[0] + dvb_ref[1]
        dx, dg = _rms_bwd(dckv, ckv_ref[...], kvln_ref[...])
        pckv[...] = dx.astype(MXU)
        gkvln[...] += jnp.sum(dg, axis=0, keepdims=True)
        pkr[...] = _rope_t(dkb_sum[:, B_KV_LORA:B_QK], ct, st, 32).astype(MXU)

    sd = jax.ShapeDtypeStruct
    args = [dqa, dka, dva, dqb, dkb, dvb, qa_raw, ka_raw, cq_raw, ckv_raw,
            qn, kn, qln, kvln, w_uq, w_uq_t, w_uk, cos_a, sin_a, cos_t, sin_t]
    in_specs = [_row_spec(tm, 512), _head_spec(A_KV, tm, HD), _head_spec(A_KV, tm, HD),
                _row_spec(tm, B_HEADS * B_QK), _head_spec(2, tm, B_QK), _head_spec(2, tm, B_KV_LORA),
                _row_spec(tm, 512), _row_spec(tm, 128), _row_spec(tm, B_Q_LORA), _row_spec(tm, B_KV_LORA),
                _full_spec(qn.shape), _full_spec(kn.shape), _full_spec(qln.shape), _full_spec(kvln.shape),
                _full_spec(w_uq.shape), _full_spec(w_uq_t.shape), _full_spec(w_uk.shape),
                _row_spec(tm, HD), _row_spec(tm, HD), _row_spec(tm, B_ROPE), _row_spec(tm, B_ROPE)]
    out_shape = (sd((s, 512), MXU), sd((s, 128), MXU), sd((s, 128), MXU), sd((s, B_Q_LORA), MXU),
                 sd((s, B_KV_LORA), MXU), sd((s, B_ROPE), MXU),
                 sd(qn.shape, F32), sd(kn.shape, F32), sd(qln.shape, F32), sd(kvln.shape, F32),
                 sd(w_uq.shape, F32), sd(w_uk.shape, F32))
    out_specs = (_row_spec(tm, 512), _row_spec(tm, 128), _row_spec(tm, 128), _row_spec(tm, B_Q_LORA),
                 _row_spec(tm, B_KV_LORA), _row_spec(tm, B_ROPE),
                 _full_spec(qn.shape), _full_spec(kn.shape), _full_spec(qln.shape), _full_spec(kvln.shape),
                 _full_spec(w_uq.shape), _full_spec(w_uk.shape))
    return pl.pallas_call(
        body, name="even_prep_backward", grid=(s // tm,), out_shape=out_shape, in_specs=in_specs, out_specs=out_specs,
        scratch_shapes=[pltpu.VMEM((tm, qb_w), F32)],
        compiler_params=_params(("arbitrary",)),
    )(*args)


def in_proj_backward(x, mod, nw, dx_out, pieces, w_in_t, name):
    s, d = x.shape
    tm = min(ROW_TILE, s)
    n_cols = w_in_t.shape[0]
    n = len(pieces)
    cols = [c for _, c in pieces]

    def body(*refs):
        x_ref, mod_ref, nw_ref, dxo_ref, wt_ref = refs[:5]
        p_refs = refs[5:5 + n]
        dx_ref, dw_ref, dv_ref, acc_ref = refs[5 + n:]
        i = pl.program_id(0)

        @pl.when(i == 0)
        def _():
            dw_ref[...] = jnp.zeros(dw_ref.shape, F32)
            acc_ref[...] = jnp.zeros(acc_ref.shape, F32)

        xn, g1, h = _modulated(x_ref[...], mod_ref, nw_ref)
        hb = h.astype(MXU)
        dh = jnp.zeros((tm, d), F32)
        for pr, (c0, c1) in zip(p_refs, cols):
            pc = pr[...].astype(MXU)
            dh = dh + jnp.dot(pc, wt_ref[c0:c1, :], preferred_element_type=F32)
            dw_ref[:, c0:c1] += _mm_tn(hb, pc)
        acc_ref[0:1, :] += jnp.sum(dh, axis=0, keepdims=True)
        acc_ref[1:2, :] += jnp.sum(dh * xn, axis=0, keepdims=True)
        dxn = dh * g1
        x = x_ref[...]
        r = _rms(x)
        dx_ref[...] = dxo_ref[...] + r * (dxn - xn * jnp.mean(dxn * xn, axis=-1, keepdims=True))

        @pl.when(i == pl.num_programs(0) - 1)
        def _():
            dg1 = acc_ref[1:2, :]
            dv_ref[0:1, :] = acc_ref[0:1, :]
            dv_ref[1:2, :] = dg1 * nw_ref[...]
            dv_ref[2:3, :] = dg1 * (1.0 + mod_ref[1:2, :])
            dv_ref[3:4, :] = jnp.zeros((1, d), F32)

    arrs = [a for a, _ in pieces]
    sd = jax.ShapeDtypeStruct
    return pl.pallas_call(
        body, name=name, grid=(s // tm,),
        out_shape=(sd((s, d), F32), sd((d, n_cols), F32), sd((4, d), F32)),
        in_specs=[_row_spec(tm, d), _full_spec(mod.shape), _full_spec(nw.shape), _row_spec(tm, d),
                  _full_spec(w_in_t.shape)] + [_row_spec(tm, a.shape[1]) for a in arrs],
        out_specs=(_row_spec(tm, d), _full_spec((d, n_cols)), _full_spec((4, d))),
        scratch_shapes=[pltpu.VMEM((8, d), F32)],
        compiler_params=_params(("arbitrary",)),
    )(x, mod, nw, dx_out, w_in_t, *arrs)


def ada_weight_grad(c_all, dmod_cols):
    d = c_all.shape[1]
    w = dmod_cols.shape[2]

    def body(c_ref, dm_ref, out_ref):
        ca = _silu(c_ref[...])
        for l in range(2):
            out_ref[l] = _mm_tn(ca, dm_ref[l])

    return pl.pallas_call(
        body, name="ada_weight_grad",
        out_shape=jax.ShapeDtypeStruct((2, d, w), F32),
        compiler_params=pltpu.CompilerParams(vmem_limit_bytes=VMEM_LIMIT),
    )(c_all, dmod_cols)


def adamw_rows(g_slots, w, m, v, name):
    n, r, lanes = g_slots.shape
    tr = r
    for cand in (2048, 1752, 1536, 1104, 1024, 552, 512):
        if r % cand == 0 and cand <= r:
            tr = cand
            break
    c1 = 1.0 - ADAM_B1 ** ADAM_STEP
    c2 = 1.0 - ADAM_B2 ** ADAM_STEP

    def body(g_ref, w_ref, m_ref, v_ref, go, do, mo, vo):
        g = g_ref[0]
        for k in range(1, n):
            g = g + g_ref[k]
        m_new = ADAM_B1 * m_ref[...] + (1.0 - ADAM_B1) * g
        v_new = ADAM_B2 * v_ref[...] + (1.0 - ADAM_B2) * (g * g)
        m_hat = m_new / c1
        v_hat = v_new / c2
        go[...] = g
        do[...] = -ADAM_LR * (m_hat / (jnp.sqrt(v_hat) + ADAM_EPS) + ADAM_WD * w_ref[...])
        mo[...] = m_new
        vo[...] = v_new

    row = pl.BlockSpec((tr, lanes), lambda i: (i, 0))
    sd = jax.ShapeDtypeStruct((r, lanes), F32)
    return pl.pallas_call(
        body, name=name, grid=(r // tr,), out_shape=(sd, sd, sd, sd),
        in_specs=[pl.BlockSpec((n, tr, lanes), lambda i: (0, i, 0)), row, row, row],
        out_specs=(row, row, row, row),
        compiler_params=_params(("parallel",)),
    )(g_slots, w, m, v)


def _rope_tables(s):
    def cs(pos, dim):
        inv = ROPE_THETA ** (-jnp.arange(0, dim, 2, dtype=F32) / dim)
        ang = pos.astype(F32)[:, None] * inv[None, :]
        return jnp.cos(ang), jnp.sin(ang)

    rows = s // GRID_W
    row = jnp.repeat(jnp.arange(rows), GRID_W)
    col = jnp.tile(jnp.arange(GRID_W), rows)
    cr, sr = cs(row, HD // 2)
    cc, sc = cs(col, HD // 2)
    ct, st = cs(jnp.arange(s), B_ROPE)
    cos_a = jnp.concatenate([cr, cr, cc, cc], axis=-1)
    sin_a = jnp.concatenate([-sr, sr, -sc, sc], axis=-1)
    return cos_a, sin_a, jnp.concatenate([ct, ct], axis=-1), jnp.concatenate([-st, st], axis=-1)


def _rows128(a):
    return a.reshape(-1, 128)


def _even_cols_to_kernel(w):
    return jnp.concatenate([w[:, :1664], w[:, 1696:], w[:, 1664:1696]], axis=1)


def _even_cols_to_reference(w):
    return jnp.concatenate([w[:, :1664], w[:, 2176:], w[:, 1664:2176]], axis=1)


def _col_blocks(w):
    d, n8 = w.shape
    n = n8 // N_DEV
    return w.reshape(d, N_DEV, n).transpose(1, 0, 2).reshape(N_DEV, d * n // 128, 128)


def _from_col_blocks(p, d):
    n = p.shape[1] * 128 // d
    return p.reshape(N_DEV, d, n).transpose(1, 0, 2).reshape(d, N_DEV * n)


def _pad_rows(flat, rows):
    return jnp.pad(flat, (0, rows * 128 - flat.shape[0])).reshape(rows, 128)


def kernel(x, c, norm_w, ada_w, ada_b, even_w_in, a_q_norm, a_k_norm, b_q_lora_norm, b_kv_lora_norm, b_w_uq, b_w_uk, b_w_uv, even_w_out, odd_w_in, c_sink, odd_w_out, final_norm, loss_target, m_norm_w, m_ada_w, m_ada_b, m_even_w_in, m_a_q_norm, m_a_k_norm, m_b_q_lora_norm, m_b_kv_lora_norm, m_b_w_uq, m_b_w_uk, m_b_w_uv, m_even_w_out, m_odd_w_in, m_c_sink, m_odd_w_out, m_final_norm, v_norm_w, v_ada_w, v_ada_b, v_even_w_in, v_a_q_norm, v_a_k_norm, v_b_q_lora_norm, v_b_kv_lora_norm, v_b_w_uq, v_b_w_uk, v_b_w_uv, v_even_w_out, v_odd_w_in, v_c_sink, v_odd_w_out, v_final_norm):
    s, d = x.shape[1], x.shape[2]
    x0 = x[0]
    target = loss_target[0]
    me_flat = 4 * lax.axis_index("x") + 2 * lax.axis_index("y") + lax.axis_index("c")

    big = [even_w_in, odd_w_in, even_w_out, odd_w_out, b_w_uq]
    big_rows = [w.size // 128 for w in big]
    offs = [0]
    for r in big_rows:
        offs.append(offs[-1] + r)
    r_big = offs[-1]
    pack = jnp.concatenate([_rows128(w) for w in big], axis=0)
    gathered = all_gather_rows(pack, MXU, "gather_weights").reshape(N_DEV, r_big, 128)
    seg = [gathered[:, offs[i]:offs[i + 1]] for i in range(5)]
    w_in_e = _even_cols_to_kernel(_from_col_blocks(seg[0], d))
    w_in_o = _from_col_blocks(seg[1], d)
    w_out_e = seg[2].reshape(1024, d)
    w_out_o = seg[3].reshape(1024, d)
    w_uq = _from_col_blocks(seg[4], B_Q_LORA)
    w_uk = jnp.transpose(b_w_uk[0], (1, 0, 2)).astype(MXU)
    w_uv = jnp.transpose(b_w_uv[0], (1, 0, 2)).astype(MXU)

    wcols = ada_w.shape[2]
    bias_cols = lax.dynamic_slice_in_dim(ada_b.reshape(2, N_DEV, wcols), me_flat, 1, axis=1)
    call, modp = ada_forward(jnp.broadcast_to(c, (8, d)), ada_w, bias_cols)
    c_all = call[:, 0, :]
    mod = jnp.transpose(modp[:, :, 0, :], (1, 0, 2)).reshape(2, 3, d)
    mod_e, mod_o = mod[0], mod[1]
    nw_e, nw_o = norm_w[0:1], norm_w[1:2]

    cos_a, sin_a, cos_t, sin_t = _rope_tables(s)
    slopes = (2.0 ** (-8.0 * jnp.arange(1, C_HEADS + 1, dtype=F32) / C_HEADS)).reshape(C_HEADS, 1, 1)
    sink = c_sink.reshape(C_HEADS, 1, 1)

    (qa, ka, va, qb, kb, qa_raw, ka_raw, cq_raw, ckv_raw, ga, gb) = even_in_forward(
        x0, mod_e, nw_e, w_in_e, a_q_norm, a_k_norm, b_q_lora_norm, b_kv_lora_norm, w_uq, w_uk,
        cos_a, sin_a, cos_t, sin_t)
    tk_dense = min(512, s)
    oa, lse_a = flash_forward(qa, ka, va, scale=HD ** -0.5, dv=HD, tq=min(256, s), tk=tk_dense, name="attn_a_fwd")
    scale_b = (B_NOPE + B_ROPE) ** -0.5
    o_lat, lse_b = flash_forward(qb, kb, None, scale=scale_b, dv=B_KV_LORA, tq=min(128, s), tk=tk_dense,
                                 name="attn_b_fwd")
    ob = latent_out_forward(o_lat, w_uv)
    x1, y_e = mixer_out_forward(x0, mod_e, [(oa, ga), (ob, gb)], w_out_e, "even_out_fwd")

    qc, kc, vc, gc = odd_in_forward(x1, mod_o, nw_o, w_in_o)
    oc, lse_c = flash_forward(qc, kc, vc, scale=HD ** -0.5, dv=HD, tq=WINDOW, tk=WINDOW, name="attn_c_fwd",
                              banded=True, sink=sink, slopes=slopes)
    x2, y_o = mixer_out_forward(x1, mod_o, [(oc, gc)], w_out_o, "odd_out_fwd")

    loss_lanes, dx2, d_final = loss_head(x2, target, final_norm.reshape(1, d))
    loss = lax.psum(0.5 * jnp.sum(loss_lanes) / d, MESH_AXES)

    doc, dgc, delta_c, dgate_o, dw_out_o, dsink = mixer_out_backward(
        dx2, y_o, mod_o, [(oc, gc)], w_out_o.T, [C_HEADS], "odd_out_bwd", lse=lse_c, sink=sink)
    dqc, dkc, dvc = flash_backward(qc, kc, vc, doc, lse_c, delta_c, scale=HD ** -0.5, dv=HD, tq=WINDOW, tk=WINDOW,
                                   gq=C_KV, name="attn_c_bwd", banded=True, slopes=slopes)
    to_rows = lambda t: jnp.transpose(t, (1, 0, 2)).reshape(s, -1)
    dx1, dw_in_o, dvec_o = in_proj_backward(
        x1, mod_o, nw_o, dx2, [(dqc, O_Q), (to_rows(dkc), O_K), (to_rows(dvc), O_V), (dgc, O_G)], w_in_o.T,
        "odd_in_bwd")

    doa, dga, dob, dgb, delta_a, dgate_e, dw_out_e = mixer_out_backward(
        dx1, y_e, mod_e, [(oa, ga), (ob, gb)], w_out_e.T, [A_HEADS, 0], "even_out_bwd")
    d_olat, delta_b, dw_uv = latent_out_backward(dob, o_lat, w_uv)
    dqb, dkb, dvb = flash_backward(qb, kb, None, d_olat, lse_b, delta_b, scale=scale_b, dv=B_KV_LORA,
                                   tq=min(256, s), tk=tk_dense, gq=2, name="attn_b_bwd")
    dqa, dka, dva = flash_backward(qa, ka, va, doa, lse_a, delta_a, scale=HD ** -0.5, dv=HD,
                                   tq=min(256, s), tk=tk_dense, gq=A_KV, name="attn_a_bwd")
    (pqa, pka, pva, pcq, pckv, pkr, g_qn, g_kn, g_qln, g_kvln, dw_uq, dw_uk) = even_prep_backward(
        dqa, dka, dva, dqb, dkb, dvb, qa_raw, ka_raw, cq_raw, ckv_raw,
        a_q_norm, a_k_norm, b_q_lora_norm, b_kv_lora_norm, w_uq, w_uq.T, w_uk, cos_a, sin_a, cos_t, sin_t)
    dx0, dw_in_e, dvec_e = in_proj_backward(
        x0, mod_e, nw_e, dx1,
        [(pqa, E_QA), (pka, E_KA), (pva, E_VA), (dga, E_GA), (pcq, E_CQ), (pckv, E_CKV), (dgb, E_GB), (pkr, E_KR)],
        w_in_e.T, "even_in_bwd")

    dmod = jnp.stack([jnp.concatenate([dvec_e[0], dvec_e[1], dgate_e[0]]),
                      jnp.concatenate([dvec_o[0], dvec_o[1], dgate_o[0]])])
    d_norm_w = jnp.stack([dvec_e[2], dvec_o[2]])
    small_names = ["norm_w", "ada_b", "a_q_norm", "a_k_norm", "b_q_lora_norm", "b_kv_lora_norm", "b_w_uk", "b_w_uv",
                   "c_sink", "final_norm"]
    small_w = [norm_w, ada_b, a_q_norm, a_k_norm, b_q_lora_norm, b_kv_lora_norm, b_w_uk, b_w_uv, c_sink, final_norm]
    small_m = [m_norm_w, m_ada_b, m_a_q_norm, m_a_k_norm, m_b_q_lora_norm, m_b_kv_lora_norm, m_b_w_uk, m_b_w_uv,
               m_c_sink, m_final_norm]
    small_v = [v_norm_w, v_ada_b, v_a_q_norm, v_a_k_norm, v_b_q_lora_norm, v_b_kv_lora_norm, v_b_w_uk, v_b_w_uv,
               v_c_sink, v_final_norm]
    small_g = [d_norm_w, dmod, g_qn, g_kn, g_qln, g_kvln, jnp.transpose(dw_uk, (1, 0, 2)), jnp.transpose(dw_uv, (1, 0, 2)),
               dsink, d_final]
    sizes = [w.size for w in small_w]
    n_small = sum(sizes)
    r_small = -(-n_small // (128 * 8)) * 8
    flat_pack = lambda arrs: _pad_rows(jnp.concatenate([a.reshape(-1) for a in arrs]), r_small)
    g_small_all = all_gather_rows(flat_pack(small_g), F32, "gather_small_grads").reshape(N_DEV, r_small, 128)
    sm = adamw_rows(g_small_all, flat_pack(small_w), flat_pack(small_m), flat_pack(small_v), "adamw_small")

    def unpack_small(packed):
        flat = packed.reshape(-1)
        out, o = {}, 0
        for nm, w, sz in zip(small_names, small_w, sizes):
            out[nm] = flat[o:o + sz].reshape(w.shape)
            o += sz
        return out

    sm = [unpack_small(p) for p in sm]

    dmod_all = g_small_all.reshape(N_DEV, -1)[:, sizes[0]:sizes[0] + sizes[1]].reshape(N_DEV, 2, N_DEV, wcols)
    dmod_cols = lax.dynamic_slice_in_dim(dmod_all, me_flat, 1, axis=2)[:, :, 0, :]
    pad16 = lambda a: jnp.concatenate([a, jnp.zeros_like(a)], axis=0)
    g_ada_w = ada_weight_grad(pad16(c_all), jnp.transpose(pad16(dmod_cols), (1, 0, 2)))
    ada = adamw_rows(_rows128(g_ada_w)[None], _rows128(ada_w), _rows128(m_ada_w), _rows128(v_ada_w), "adamw_ada_w")
    ada = [p.reshape(ada_w.shape) for p in ada]

    g_blocks = jnp.concatenate([
        _col_blocks(_even_cols_to_reference(dw_in_e)), _col_blocks(dw_in_o),
        dw_out_e.reshape(N_DEV, -1, 128), dw_out_o.reshape(N_DEV, -1, 128), _col_blocks(dw_uq)], axis=1)
    landed = scatter_blocks(g_blocks, "scatter_weight_grads")
    big_m = [m_even_w_in, m_odd_w_in, m_even_w_out, m_odd_w_out, m_b_w_uq]
    big_v = [v_even_w_in, v_odd_w_in, v_even_w_out, v_odd_w_out, v_b_w_uq]
    cat = lambda arrs: jnp.concatenate([_rows128(a) for a in arrs], axis=0)
    bg = adamw_rows(landed, pack, cat(big_m), cat(big_v), "adamw_big")
    big_names = ["even_w_in", "odd_w_in", "even_w_out", "odd_w_out", "b_w_uq"]
    bg = [{nm: p[offs[i]:offs[i + 1]].reshape(w.shape) for i, (nm, w) in enumerate(zip(big_names, big))} for p in bg]

    order = ["norm_w", "ada_w", "ada_b", "even_w_in", "a_q_norm", "a_k_norm", "b_q_lora_norm", "b_kv_lora_norm",
             "b_w_uq", "b_w_uk", "b_w_uv", "even_w_out", "odd_w_in", "c_sink", "odd_w_out", "final_norm"]

    def pick(kind):
        out = []
        for nm in order:
            if nm == "ada_w":
                out.append(ada[kind])
            elif nm in big_names:
                out.append(bg[kind][nm])
            else:
                out.append(sm[kind][nm])
        return out

    return (loss, dx0[None], *pick(0), *pick(1), *pick(2), *pick(3))
```

```python
import jax
import jax.numpy as jnp
from jax import lax
from jax.experimental import pallas as pl
from jax.experimental.pallas import tpu as pltpu

F32 = jnp.float32
MXU = jnp.bfloat16
EPS = 1e-6
ROPE_THETA = 10000.0
GRID_W = 64
HD = 64
N_DEV = 8
MESH_AXES = ("x", "y", "c")

A_HEADS, A_KV = 8, 2
B_HEADS, B_NOPE, B_ROPE, B_Q_LORA, B_KV_LORA = 8, 64, 32, 256, 128
B_QK = B_KV_LORA + B_ROPE
C_HEADS, C_KV = 16, 4
WINDOW = 128

ADAM_LR, ADAM_B1, ADAM_B2, ADAM_EPS, ADAM_WD, ADAM_STEP = 0.001, 0.9, 0.999, 1e-08, 0.01, 10

ROW_TILE = 256
VMEM_LIMIT = 56 * 1024 * 1024

E_QA, E_KA, E_VA, E_GA, E_CQ, E_CKV, E_GB, E_KR = (
    (0, 512), (512, 640), (640, 768), (768, 1280), (1280, 1536), (1536, 1664), (1664, 2176), (2176, 2208))
EVEN_IN = 2208
O_Q, O_K, O_V, O_G = (0, 1024), (1024, 1280), (1280, 1536), (1536, 2560)
ODD_IN = 2560


def _mm(a, b):
    return jnp.dot(a.astype(MXU), b.astype(MXU), preferred_element_type=F32)


def _mm_nt(a, b):
    return lax.dot_general(a.astype(MXU), b.astype(MXU), (((1,), (1,)), ((), ())), preferred_element_type=F32)


def _mm_tn(a, b):
    return lax.dot_general(a.astype(MXU), b.astype(MXU), (((0,), (0,)), ((), ())), preferred_element_type=F32)


def _group_sums_t(prod, group):
    tm, w = prod.shape
    sel = (lax.broadcasted_iota(jnp.int32, (w, 128), 0) // group
           == lax.broadcasted_iota(jnp.int32, (w, 128), 1)).astype(MXU)
    hi = prod.astype(MXU)
    lo = prod - hi.astype(F32)
    return (_mm(hi, sel) + _mm(lo, sel)).T


def _sigmoid(z):
    return 1.0 / (1.0 + jnp.exp(-z))


def _silu(z):
    return z * _sigmoid(z)


def _rms(x):
    return lax.rsqrt(jnp.mean(x * x, axis=-1, keepdims=True) + EPS)


def _swap_halves(y, group):
    n = y.shape[-1]
    half = group // 2
    fwd = pltpu.roll(y, half, 1)
    if n == group:
        return fwd
    back = pltpu.roll(y, n - half, 1)
    lane = lax.broadcasted_iota(jnp.int32, y.shape, 1)
    return jnp.where((lane % group) < half, back, fwd)


def _rope(y, cos, sin, group):
    return y * cos + _swap_halves(y, group) * sin


def _rope_t(d, cos, sin, group):
    return d * cos - _swap_halves(d, group) * sin


def _rms_bwd(dy, x, g):
    r = _rms(x)
    xhat = x * r
    dxhat = dy * g
    dx = r * (dxhat - xhat * jnp.mean(dxhat * xhat, axis=-1, keepdims=True))
    return dx, dy * xhat


def _params(sem, vmem=VMEM_LIMIT):
    return pltpu.CompilerParams(dimension_semantics=sem, vmem_limit_bytes=vmem)


def _row_spec(tm, w):
    return pl.BlockSpec((tm, w), lambda i: (i, 0))


def _full_spec(shape):
    nd = len(shape)
    return pl.BlockSpec(shape, lambda i: (0,) * nd)


def _head_spec(h, tm, w):
    return pl.BlockSpec((h, tm, w), lambda i: (0, i, 0))


def _headt_spec(h, w, tm):
    return pl.BlockSpec((h, w, tm), lambda i: (0, 0, i))


def _rows_spec(h, tm):
    return pl.BlockSpec((h, tm), lambda i: (0, i))


def _me():
    return lax.axis_index("x"), lax.axis_index("y"), lax.axis_index("c")


def _flat(p):
    return 4 * p[0] + 2 * p[1] + p[2]


def _peer(me, k):
    x, y, c = me
    return (1 - x if k & 4 else x, 1 - y if k & 2 else y, 1 - c if k & 1 else c)


MESH_ID = pl.DeviceIdType.MESH


def all_gather_rows(x_shard, out_dtype, name):
    r, n = x_shard.shape

    def body(x_ref, out_ref, xs_ref, send_sems, recv_sems, local_sem):
        me = _me()
        x, y, c = me
        sibling = (x, y, 1 - c)
        chips = [(1 - x, y), (x, 1 - y), (1 - x, 1 - y)]
        xs_ref[...] = x_ref[...].astype(out_dtype)

        def rows(p):
            return out_ref.at[pl.ds(pl.multiple_of(_flat(p) * r, 16), r), :]

        def copy(k, block, to, src=None):
            return pltpu.make_async_remote_copy(
                src_ref=rows(block) if src is None else src, dst_ref=rows(block),
                send_sem=send_sems.at[k], recv_sem=recv_sems.at[k], device_id=to, device_id_type=MESH_ID)

        mine = pltpu.make_async_copy(xs_ref, rows(me), local_sem)
        mine.start()
        first = [copy(0, me, sibling, src=xs_ref)]
        first += [copy(1 + j, me, (*chip, c), src=xs_ref) for j, chip in enumerate(chips)]
        for cp in first:
            cp.start()
        passed = [copy(4 + j, (*chip, c), sibling) for j, chip in enumerate(chips)]
        for j, chip in enumerate(chips):
            copy(1 + j, (*chip, c), me).wait_recv()
            passed[j].start()
        copy(0, sibling, me).wait_recv()
        for j, chip in enumerate(chips):
            copy(4 + j, (*chip, 1 - c), me).wait_recv()
        for cp in first + passed:
            cp.wait_send()
        mine.wait()

    return pl.pallas_call(
        body, name=name,
        out_shape=jax.ShapeDtypeStruct((N_DEV * r, n), out_dtype),
        in_specs=[pl.BlockSpec(memory_space=pltpu.VMEM)],
        out_specs=pl.BlockSpec(memory_space=pltpu.VMEM),
        scratch_shapes=[pltpu.VMEM((r, n), out_dtype), pltpu.SemaphoreType.DMA((7,)),
                        pltpu.SemaphoreType.DMA((7,)), pltpu.SemaphoreType.DMA],
        compiler_params=pltpu.CompilerParams(vmem_limit_bytes=VMEM_LIMIT),
    )(x_shard)


def scatter_blocks(blocks, name):
    _, r, n = blocks.shape

    def body(g_ref, land_ref, send_sems, recv_sems, local_sem):
        me = _me()
        mi = _flat(me)
        local = pltpu.make_async_copy(g_ref.at[mi], land_ref.at[mi], local_sem)
        local.start()
        copies = []
        for k in range(1, N_DEV):
            peer = _peer(me, k)
            pi = _flat(peer)
            copies.append(pltpu.make_async_remote_copy(
                src_ref=g_ref.at[pi], dst_ref=land_ref.at[mi],
                send_sem=send_sems.at[k - 1], recv_sem=recv_sems.at[k - 1], device_id=peer, device_id_type=MESH_ID))
        for cp in copies:
            cp.start()
        for k in range(1, N_DEV):
            pi = _flat(_peer(me, k))
            pltpu.make_async_remote_copy(
                src_ref=g_ref.at[pi], dst_ref=land_ref.at[pi],
                send_sem=send_sems.at[k - 1], recv_sem=recv_sems.at[k - 1], device_id=_peer(me, k),
                device_id_type=MESH_ID).wait_recv()
        for cp in copies:
            cp.wait_send()
        local.wait()

    return pl.pallas_call(
        body, name=name,
        out_shape=jax.ShapeDtypeStruct(blocks.shape, blocks.dtype),
        in_specs=[pl.BlockSpec(memory_space=pl.ANY)],
        out_specs=pl.BlockSpec(memory_space=pl.ANY),
        scratch_shapes=[pltpu.SemaphoreType.DMA((7,)), pltpu.SemaphoreType.DMA((7,)), pltpu.SemaphoreType.DMA],
    )(blocks)


def ada_forward(c8, ada_w, bias_cols):
    d = c8.shape[1]
    w = ada_w.shape[2]

    def body(c_ref, w_ref, b_ref, call_ref, modp_ref, part_ref, s1, r1, s2, r2):
        me = _me()
        mi = _flat(me)
        call_ref[mi] = c_ref[...]
        gather = []
        for k in range(1, N_DEV):
            gather.append(pltpu.make_async_remote_copy(
                src_ref=c_ref, dst_ref=call_ref.at[mi], send_sem=s1.at[k - 1], recv_sem=r1.at[k - 1],
                device_id=_peer(me, k), device_id_type=MESH_ID))
        for cp in gather:
            cp.start()
        for k in range(1, N_DEV):
            pltpu.make_async_remote_copy(
                src_ref=c_ref, dst_ref=call_ref.at[_flat(_peer(me, k))], send_sem=s1.at[k - 1],
                recv_sem=r1.at[k - 1], device_id=_peer(me, k), device_id_type=MESH_ID).wait_recv()
        ca = _silu(call_ref[...].reshape(N_DEV * 8, d))
        for l in range(2):
            part = _mm(ca, w_ref[l]) + b_ref[l]
            for b in range(N_DEV):
                part_ref[b, l] = part[8 * b:8 * b + 8, :]
        modp_ref[mi] = part_ref[mi]
        spread = []
        for k in range(1, N_DEV):
            peer = _peer(me, k)
            spread.append(pltpu.make_async_remote_copy(
                src_ref=part_ref.at[_flat(peer)], dst_ref=modp_ref.at[mi], send_sem=s2.at[k - 1],
                recv_sem=r2.at[k - 1], device_id=peer, device_id_type=MESH_ID))
        for cp in spread:
            cp.start()
        for k in range(1, N_DEV):
            pi = _flat(_peer(me, k))
            pltpu.make_async_remote_copy(
                src_ref=part_ref.at[pi], dst_ref=modp_ref.at[pi], send_sem=s2.at[k - 1],
                recv_sem=r2.at[k - 1], device_id=_peer(me, k), device_id_type=MESH_ID).wait_recv()
        for cp in gather + spread:
            cp.wait_send()

    vm = pl.BlockSpec(memory_space=pltpu.VMEM)
    return pl.pallas_call(
        body, name="ada_forward",
        out_shape=(jax.ShapeDtypeStruct((N_DEV, 8, d), F32), jax.ShapeDtypeStruct((N_DEV, 2, 8, w), F32)),
        in_specs=[vm, vm, vm], out_specs=(vm, vm),
        scratch_shapes=[pltpu.VMEM((N_DEV, 2, 8, w), F32)] + [pltpu.SemaphoreType.DMA((7,))] * 4,
        compiler_params=pltpu.CompilerParams(vmem_limit_bytes=VMEM_LIMIT),
    )(c8, ada_w, bias_cols)


def _modulated(x, mod_ref, nw_ref):
    xn = x * _rms(x)
    g1 = nw_ref[...] * (1.0 + mod_ref[1:2, :])
    return xn, g1, xn * g1 + mod_ref[0:1, :]


def even_in_forward(x, mod, nw, w_in, qn, kn, qln, kvln, w_uq, w_uk, cos_a, sin_a, cos_t, sin_t):
    s, d = x.shape
    tm = min(ROW_TILE, s)

    def body(x_ref, mod_ref, nw_ref, w_ref, qn_ref, kn_ref, qln_ref, kvln_ref, uq_ref, uk_ref,
             ca_ref, sa_ref, ct_ref, st_ref,
             qa_o, ka_o, va_o, qb_o, kb_o, kat_o, vat_o, kbt_o, qa_raw_o, ka_raw_o, cq_raw_o, ckv_raw_o, ga_o, gb_o):
        _, _, h = _modulated(x_ref[...], mod_ref, nw_ref)
        h = h.astype(MXU)

        def proj(cols):
            return jnp.dot(h, w_ref[:, cols[0]:cols[1]], preferred_element_type=F32)

        ca, sa, ct, st = ca_ref[...], sa_ref[...], ct_ref[...], st_ref[...]
        qa = proj(E_QA)
        qa_raw_o[...] = qa
        for hh in range(A_HEADS):
            xh = qa[:, HD * hh:HD * hh + HD]
            qa_o[hh] = _rope(xh * _rms(xh) * qn_ref[...], ca, sa, 32).astype(MXU)
        ka = proj(E_KA)
        ka_raw_o[...] = ka
        va = proj(E_VA)
        for g in range(A_KV):
            xh = ka[:, HD * g:HD * g + HD]
            kr = _rope(xh * _rms(xh) * kn_ref[...], ca, sa, 32)
            vh = va[:, HD * g:HD * g + HD]
            ka_o[g] = kr.astype(MXU)
            va_o[g] = vh.astype(MXU)
            kat_o[g] = kr.T.astype(MXU)
            vat_o[g] = vh.T.astype(MXU)
        ga_o[...] = proj(E_GA)
        gb_o[...] = proj(E_GB)
        cq = proj(E_CQ)
        cq_raw_o[...] = cq
        qb = _mm(cq * _rms(cq) * qln_ref[...], uq_ref[...])
        for hh in range(B_HEADS):
            base = (B_NOPE + B_ROPE) * hh
            qb_o[hh, :, 0:B_KV_LORA] = _mm_nt(qb[:, base:base + B_NOPE], uk_ref[hh]).astype(MXU)
            qb_o[hh, :, B_KV_LORA:B_QK] = _rope(qb[:, base + B_NOPE:base + B_NOPE + B_ROPE], ct, st, 32).astype(MXU)
        ckv = proj(E_CKV)
        ckv_raw_o[...] = ckv
        ckv_n = ckv * _rms(ckv) * kvln_ref[...]
        k_rope = _rope(proj(E_KR), ct, st, 32)
        kb_o[0, :, 0:B_KV_LORA] = ckv_n.astype(MXU)
        kb_o[0, :, B_KV_LORA:B_QK] = k_rope.astype(MXU)
        kbt_o[0, 0:B_KV_LORA, :] = ckv_n.T.astype(MXU)
        kbt_o[0, B_KV_LORA:B_QK, :] = k_rope.T.astype(MXU)

    sd = jax.ShapeDtypeStruct
    outs = (sd((A_HEADS, s, HD), MXU), sd((A_KV, s, HD), MXU), sd((A_KV, s, HD), MXU),
            sd((B_HEADS, s, B_QK), MXU), sd((1, s, B_QK), MXU),
            sd((A_KV, HD, s), MXU), sd((A_KV, HD, s), MXU), sd((1, B_QK, s), MXU),
            sd((s, 512), F32), sd((s, 128), F32), sd((s, B_Q_LORA), F32), sd((s, B_KV_LORA), F32),
            sd((s, 512), F32), sd((s, 512), F32))
    out_specs = (_head_spec(A_HEADS, tm, HD), _head_spec(A_KV, tm, HD), _head_spec(A_KV, tm, HD),
                 _head_spec(B_HEADS, tm, B_QK), _head_spec(1, tm, B_QK),
                 _headt_spec(A_KV, HD, tm), _headt_spec(A_KV, HD, tm), _headt_spec(1, B_QK, tm),
                 _row_spec(tm, 512), _row_spec(tm, 128), _row_spec(tm, B_Q_LORA), _row_spec(tm, B_KV_LORA),
                 _row_spec(tm, 512), _row_spec(tm, 512))
    in_specs = [_row_spec(tm, d), _full_spec(mod.shape), _full_spec(nw.shape), _full_spec(w_in.shape),
                _full_spec(qn.shape), _full_spec(kn.shape), _full_spec(qln.shape), _full_spec(kvln.shape),
                _full_spec(w_uq.shape), _full_spec(w_uk.shape),
                _row_spec(tm, HD), _row_spec(tm, HD), _row_spec(tm, B_ROPE), _row_spec(tm, B_ROPE)]
    return pl.pallas_call(
        body, name="even_in_forward", grid=(s // tm,), out_shape=outs, in_specs=in_specs, out_specs=out_specs,
        compiler_params=_params(("parallel",)),
    )(x, mod, nw, w_in, qn, kn, qln, kvln, w_uq, w_uk, cos_a, sin_a, cos_t, sin_t)


def odd_in_forward(x, mod, nw, w_in):
    s, d = x.shape
    tm = min(ROW_TILE, s)

    def body(x_ref, mod_ref, nw_ref, w_ref, q_o, k_o, v_o, kt_o, vt_o, g_o):
        _, _, h = _modulated(x_ref[...], mod_ref, nw_ref)
        h = h.astype(MXU)

        def proj(cols):
            return jnp.dot(h, w_ref[:, cols[0]:cols[1]], preferred_element_type=F32)

        q = proj(O_Q)
        for hh in range(C_HEADS):
            q_o[hh] = q[:, HD * hh:HD * hh + HD].astype(MXU)
        k = proj(O_K)
        v = proj(O_V)
        for g in range(C_KV):
            kh = k[:, HD * g:HD * g + HD]
            vh = v[:, HD * g:HD * g + HD]
            k_o[g] = kh.astype(MXU)
            v_o[g] = vh.astype(MXU)
            kt_o[g] = kh.T.astype(MXU)
            vt_o[g] = vh.T.astype(MXU)
        g_o[...] = proj(O_G)

    sd = jax.ShapeDtypeStruct
    return pl.pallas_call(
        body, name="odd_in_forward", grid=(s // tm,),
        out_shape=(sd((C_HEADS, s, HD), MXU), sd((C_KV, s, HD), MXU), sd((C_KV, s, HD), MXU),
                   sd((C_KV, HD, s), MXU), sd((C_KV, HD, s), MXU), sd((s, 1024), F32)),
        in_specs=[_row_spec(tm, d), _full_spec(mod.shape), _full_spec(nw.shape), _full_spec(w_in.shape)],
        out_specs=(_head_spec(C_HEADS, tm, HD), _head_spec(C_KV, tm, HD), _head_spec(C_KV, tm, HD),
                   _headt_spec(C_KV, HD, tm), _headt_spec(C_KV, HD, tm), _row_spec(tm, 1024)),
        compiler_params=_params(("parallel",)),
    )(x, mod, nw, w_in)


def latent_out_forward(o_lat, w_uv):
    s = o_lat.shape[0]
    tm = min(ROW_TILE, s)

    def body(o_ref, uv_ref, out_ref):
        for hh in range(B_HEADS):
            out_ref[:, HD * hh:HD * hh + HD] = _mm(o_ref[:, B_KV_LORA * hh:B_KV_LORA * (hh + 1)], uv_ref[hh])

    return pl.pallas_call(
        body, name="latent_out_forward", grid=(s // tm,),
        out_shape=jax.ShapeDtypeStruct((s, B_HEADS * HD), F32),
        in_specs=[_row_spec(tm, o_lat.shape[1]), _full_spec(w_uv.shape)],
        out_specs=_row_spec(tm, B_HEADS * HD),
        compiler_params=_params(("parallel",)),
    )(o_lat, w_uv)


def mixer_out_forward(x, mod, pairs, w_out, name):
    s, d = x.shape
    tm = min(ROW_TILE, s)
    n = len(pairs)
    widths = [o.shape[1] for o, _ in pairs]

    def body(*refs):
        x_ref, mod_ref, w_ref = refs[:3]
        pr = refs[3:3 + 2 * n]
        xo_ref, y_ref = refs[3 + 2 * n:]
        y = jnp.zeros((tm, d), F32)
        r0 = 0
        for i in range(n):
            mix = pr[2 * i][...] * _silu(pr[2 * i + 1][...])
            y = y + _mm(mix, w_ref[r0:r0 + widths[i], :])
            r0 += widths[i]
        y_ref[...] = y
        xo_ref[...] = x_ref[...] + mod_ref[2:3, :] * y

    flat = [a for p in pairs for a in p]
    sd = jax.ShapeDtypeStruct
    return pl.pallas_call(
        body, name=name, grid=(s // tm,),
        out_shape=(sd((s, d), F32), sd((s, d), F32)),
        in_specs=[_row_spec(tm, d), _full_spec(mod.shape), _full_spec(w_out.shape)]
        + [_row_spec(tm, a.shape[1]) for a in flat],
        out_specs=(_row_spec(tm, d), _row_spec(tm, d)),
        compiler_params=_params(("parallel",)),
    )(x, mod, w_out, *flat)


LOG2E = 1.4426950408889634
ONES_ROWS = 16


def _band_bias_t(q_blk, k_blk, hpg, t, slope_ref):
    keyi = lax.broadcasted_iota(jnp.int32, (t, t), 0)
    qryi = lax.broadcasted_iota(jnp.int32, (t, t), 1)
    arel = jnp.abs((k_blk * t + keyi) - (q_blk * t + qryi))
    base = jnp.where(arel <= WINDOW, arel.astype(F32) * (-LOG2E), -jnp.inf)
    return jnp.concatenate([base * slope_ref[hh] for hh in range(hpg)], axis=1)


def _col_max8(s3):
    m8 = jnp.max(s3, axis=0)
    return jnp.broadcast_to(jnp.max(m8, axis=0, keepdims=True), m8.shape)


def flash_forward(q, k, vt, *, scale, dv, tq, tk, name, banded=False, sink2=None, slopes=None):
    hq, s, dq = q.shape
    g_kv = k.shape[0]
    hpg = hq // g_kv
    nq = s // tq
    nk = 3 if banded else s // tk
    m_cols = hpg * tq
    c = scale * LOG2E
    dvp = dv + ONES_ROWS

    def body(*refs):
        it = iter(refs)
        q_ref, k_ref, vt_ref = next(it), next(it), next(it)
        sink_ref = next(it) if banded else None
        slope_ref = next(it) if banded else None
        o_ref, lse_ref, m_s, acc_s = next(it), next(it), next(it), next(it)
        i = pl.program_id(1)
        j = pl.program_id(2)

        @pl.when(j == 0)
        def _():
            if banded:
                m_s[...] = jnp.concatenate([jnp.broadcast_to(sink_ref[hh], (8, tq)) for hh in range(hpg)], axis=1)
                acc_s[0:dv, :] = jnp.zeros((dv, m_cols), F32)
                acc_s[dv:dvp, :] = jnp.ones((ONES_ROWS, m_cols), F32)
            else:
                m_s[...] = jnp.full((8, m_cols), -jnp.inf, F32)
                acc_s[...] = jnp.zeros((dvp, m_cols), F32)

        def step():
            qq = q_ref[...].reshape(m_cols, dq)
            st = _mm_nt(k_ref[0], qq)
            m_old = m_s[...]
            if banded:
                s3 = (st * c + _band_bias_t(i, i - 1 + j, hpg, tq, slope_ref)).reshape(tk // 8, 8, m_cols)
                m_new = jnp.maximum(m_old, _col_max8(s3))
                p = jnp.exp2(s3 - m_new[None])
            else:
                s3 = st.reshape(tk // 8, 8, m_cols)
                m_new = jnp.maximum(m_old, _col_max8(s3) * c)
                p = jnp.exp2(s3 * c - m_new[None])
            alpha = jnp.exp2(m_old - m_new)
            vte = jnp.concatenate([vt_ref[0], jnp.ones((ONES_ROWS, tk), vt_ref.dtype)], axis=0)
            pv = _mm(vte, p.reshape(tk, m_cols))
            acc_s[...] = (acc_s[...].reshape(dvp // 8, 8, m_cols) * alpha[None]).reshape(dvp, m_cols) + pv
            m_s[...] = m_new

        if banded:
            kb = i - 1 + j
            pl.when((kb >= 0) & (kb < nq))(step)
        else:
            step()

        @pl.when(j == nk - 1)
        def _():
            l = acc_s[dv:dv + 1, :]
            ot = acc_s[0:dv, :] / l
            lse = m_s[0:1, :] + jnp.log2(l)
            for hh in range(hpg):
                o_ref[:, dv * hh:dv * hh + dv] = ot[:, tq * hh:tq * hh + tq].T
                lse_ref[hh] = lse[:, tq * hh:tq * hh + tq]

    if banded:
        kblk = lambda i, j: jnp.clip(i - 1 + j, 0, nq - 1)
    else:
        kblk = lambda i, j: j
    in_specs = [pl.BlockSpec((hpg, tq, dq), lambda g, i, j: (g, i, 0)),
                pl.BlockSpec((1, tk, k.shape[2]), lambda g, i, j: (g, kblk(i, j), 0)),
                pl.BlockSpec((1, dv, tk), lambda g, i, j: (g, 0, kblk(i, j)))]
    args = [q, k, vt]
    if banded:
        in_specs += [pl.BlockSpec((hpg, 1, 1), lambda g, i, j: (g, 0, 0))] * 2
        args += [sink2, slopes]
    sd = jax.ShapeDtypeStruct
    return pl.pallas_call(
        body, name=name, grid=(g_kv, nq, nk),
        out_shape=(sd((s, hq * dv), F32), sd((hq, 1, s), F32)),
        in_specs=in_specs,
        out_specs=(pl.BlockSpec((tq, hpg * dv), lambda g, i, j: (i, g)),
                   pl.BlockSpec((hpg, 1, tq), lambda g, i, j: (g, 0, i))),
        scratch_shapes=[pltpu.VMEM((8, m_cols), F32), pltpu.VMEM((dvp, m_cols), F32)],
        compiler_params=_params(("parallel", "parallel", "arbitrary")),
    )(*args)


def flash_backward(q, k, kt, v, do, lse, delta, *, scale, dv, tq, tk, gq, name, banded=False, slopes=None):
    hq, s, dq = q.shape
    g_kv = k.shape[0]
    hpg = hq // gq
    nq = s // tq
    nkb = s // tk
    n_in = 3 if banded else nq
    m_cols = hpg * tq
    c = scale * LOG2E
    has_v = v is not None

    def body(*refs):
        it = iter(refs)
        q_ref, k_ref, kt_ref = next(it), next(it), next(it)
        v_ref = next(it) if has_v else None
        do_ref, lse_ref, dl_ref = next(it), next(it), next(it)
        slope_ref = next(it) if banded else None
        dq_ref, dk_ref, dv_ref, dqt_s = next(it), next(it), next(it), next(it)
        kj = pl.program_id(1)
        jj = pl.program_id(2)
        qi = kj - 1 + jj if banded else jj

        @pl.when((kj == 0) & (jj == 0))
        def _():
            dqt_s[...] = jnp.zeros(dqt_s.shape, F32)

        @pl.when(jj == 0)
        def _():
            dk_ref[...] = jnp.zeros(dk_ref.shape, F32)
            dv_ref[...] = jnp.zeros(dv_ref.shape, F32)

        def step():
            qq = q_ref[...].reshape(m_cols, dq)
            kk = k_ref[0]
            vv = v_ref[0] if has_v else kk[:, :dv]
            dd = jnp.concatenate([do_ref[:, dv * hh:dv * hh + dv] for hh in range(hpg)], axis=0)
            lse_row = jnp.concatenate([lse_ref[hh] for hh in range(hpg)], axis=1)
            dl_row = jnp.concatenate([dl_ref[hh] for hh in range(hpg)], axis=1)
            st = _mm_nt(kk, qq) * c
            if banded:
                st = st + _band_bias_t(qi, kj, hpg, tq, slope_ref)
            p = jnp.exp2(st - lse_row)
            ds = p * (_mm_nt(vv, dd) - dl_row) * scale
            dv_ref[0] += _mm(p, dd)
            dk_ref[0] += _mm(ds, qq)
            dqt = _mm(kt_ref[0], ds)
            for hh in range(hpg):
                dqt_s[qi, dq * hh:dq * hh + dq, :] += dqt[:, tq * hh:tq * hh + tq]

        if banded:
            pl.when((qi >= 0) & (qi < nq))(step)
        else:
            step()

        @pl.when((kj == nkb - 1) & (jj == n_in - 1))
        def _():
            def emit(t, carry):
                r0 = pl.multiple_of(t * tq, tq)
                for hh in range(hpg):
                    dq_ref[pl.ds(r0, tq), dq * hh:dq * hh + dq] = dqt_s[t, dq * hh:dq * hh + dq, :].T
                return carry

            lax.fori_loop(0, nq, emit, 0)

    kv_of = lambda g: g * g_kv // gq
    if banded:
        qblk = lambda kj, jj: jnp.clip(kj - 1 + jj, 0, nq - 1)
    else:
        qblk = lambda kj, jj: jj
    in_specs = [pl.BlockSpec((hpg, tq, dq), lambda g, kj, jj: (g, qblk(kj, jj), 0)),
                pl.BlockSpec((1, tk, dq), lambda g, kj, jj: (kv_of(g), kj, 0)),
                pl.BlockSpec((1, dq, tk), lambda g, kj, jj: (kv_of(g), 0, kj))]
    args = [q, k, kt]
    if has_v:
        in_specs.append(pl.BlockSpec((1, tk, dv), lambda g, kj, jj: (kv_of(g), kj, 0)))
        args.append(v)
    row_map = lambda g, kj, jj: (g, 0, qblk(kj, jj))
    in_specs += [pl.BlockSpec((tq, hpg * dv), lambda g, kj, jj: (qblk(kj, jj), g)),
                 pl.BlockSpec((hpg, 1, tq), row_map), pl.BlockSpec((hpg, 1, tq), row_map)]
    args += [do, lse, delta]
    if banded:
        in_specs.append(pl.BlockSpec((hpg, 1, 1), lambda g, kj, jj: (g, 0, 0)))
        args.append(slopes)
    sd = jax.ShapeDtypeStruct
    return pl.pallas_call(
        body, name=name, grid=(gq, nkb, n_in),
        out_shape=(sd((s, hq * dq), F32), sd((gq, s, dq), F32), sd((gq, s, dv), F32)),
        in_specs=in_specs,
        out_specs=(pl.BlockSpec((s, hpg * dq), lambda g, kj, jj: (0, g)),
                   pl.BlockSpec((1, tk, dq), lambda g, kj, jj: (g, kj, 0)),
                   pl.BlockSpec((1, tk, dv), lambda g, kj, jj: (g, kj, 0))),
        scratch_shapes=[pltpu.VMEM((nq, hpg * dq, tq), F32)],
        compiler_params=_params(("parallel", "arbitrary", "arbitrary")),
    )(*args)


def loss_head(x, target, fnw):
    s, d = x.shape
    tm = min(ROW_TILE, s)

    def body(x_ref, t_ref, w_ref, lp_ref, dx_ref, dw_ref):
        @pl.when(pl.program_id(0) == 0)
        def _():
            lp_ref[...] = jnp.zeros(lp_ref.shape, F32)
            dw_ref[...] = jnp.zeros(dw_ref.shape, F32)

        x = x_ref[...]
        g = w_ref[...]
        err = x * _rms(x) * g - t_ref[...]
        lp_ref[...] += jnp.sum(err * err, axis=0, keepdims=True)
        dx, dg = _rms_bwd(err * (1.0 / d), x, g)
        dx_ref[...] = dx
        dw_ref[...] += jnp.sum(dg, axis=0, keepdims=True)

    sd = jax.ShapeDtypeStruct
    return pl.pallas_call(
        body, name="loss_head", grid=(s // tm,),
        out_shape=(sd((1, d), F32), sd((s, d), F32), sd((1, d), F32)),
        in_specs=[_row_spec(tm, d), _row_spec(tm, d), _full_spec(fnw.shape)],
        out_specs=(_full_spec((1, d)), _row_spec(tm, d), _full_spec((1, d))),
        compiler_params=_params(("arbitrary",)),
    )(x, target, fnw)


def mixer_out_backward(dx, y, mod, pairs, w_out_t, delta_heads, name, lse=None, sink=None):
    s, d = dx.shape
    tm = min(ROW_TILE, s)
    n = len(pairs)
    widths = [o.shape[1] for o, _ in pairs]
    n_delta = sum(1 for h in delta_heads if h)
    with_sink = lse is not None

    def body(*refs):
        it = iter(refs)
        dx_ref, y_ref, mod_ref, wt_ref = next(it), next(it), next(it), next(it)
        pr = [next(it) for _ in range(2 * n)]
        lse_ref = next(it) if with_sink else None
        sink_ref = next(it) if with_sink else None
        outs = [next(it) for _ in range(2 * n)]
        dl_refs = [next(it) for _ in range(n_delta)]
        dgate_ref, dw_ref = next(it), next(it)
        dsink_ref = next(it) if with_sink else None

        @pl.when(pl.program_id(0) == 0)
        def _():
            dgate_ref[...] = jnp.zeros(dgate_ref.shape, F32)
            dw_ref[...] = jnp.zeros(dw_ref.shape, F32)
            if with_sink:
                dsink_ref[...] = jnp.zeros(dsink_ref.shape, F32)

        dxo = dx_ref[...]
        dgate_ref[...] += jnp.sum(dxo * y_ref[...], axis=0, keepdims=True)
        dy = (dxo * mod_ref[2:3, :]).astype(MXU)
        dmix = jnp.dot(dy, wt_ref[...], preferred_element_type=F32)
        r0 = 0
        di = 0
        for i in range(n):
            o = pr[2 * i][...]
            g = pr[2 * i + 1][...]
            dm = dmix[:, r0:r0 + widths[i]]
            sg = _sigmoid(g)
            act = g * sg
            do = dm * act
            outs[2 * i][...] = do.astype(MXU)
            outs[2 * i + 1][...] = (dm * o * (sg * (1.0 + g * (1.0 - sg)))).astype(MXU)
            dw_ref[r0:r0 + widths[i], :] += _mm_tn(o * act, dy)
            if delta_heads[i]:
                dlt = _group_sums_t(do * o, HD)[0:delta_heads[i], :]
                dl_refs[di][...] = dlt
                if with_sink:
                    ps = jnp.exp2(sink_ref[...] - lse_ref[...])
                    dsink_ref[...] += -jnp.sum(ps * dlt, axis=1, keepdims=True)
                di += 1
            r0 += widths[i]

    flat = [a for p in pairs for a in p]
    sd = jax.ShapeDtypeStruct
    in_specs = [_row_spec(tm, d), _row_spec(tm, d), _full_spec(mod.shape), _full_spec(w_out_t.shape)]
    in_specs += [_row_spec(tm, a.shape[1]) for a in flat]
    args = [dx, y, mod, w_out_t] + flat
    if with_sink:
        nh = lse.shape[0]
        in_specs += [_rows_spec(nh, tm), _full_spec(sink.shape)]
        args += [lse, sink]
    out_shape = [sd((s, a.shape[1]), MXU) for a in flat]
    out_specs = [_row_spec(tm, a.shape[1]) for a in flat]
    for h in delta_heads:
        if h:
            out_shape.append(sd((h, s), F32))
            out_specs.append(_rows_spec(h, tm))
    out_shape += [sd((1, d), F32), sd((sum(widths), d), F32)]
    out_specs += [_full_spec((1, d)), _full_spec((sum(widths), d))]
    if with_sink:
        out_shape.append(sd((lse.shape[0], 1), F32))
        out_specs.append(_full_spec((lse.shape[0], 1)))
    return pl.pallas_call(
        body, name=name, grid=(s // tm,), out_shape=tuple(out_shape), in_specs=in_specs, out_specs=tuple(out_specs),
        compiler_params=_params(("arbitrary",)),
    )(*args)


def latent_out_backward(d_ob, o_lat, w_uv):
    s = o_lat.shape[0]
    tm = min(ROW_TILE, s)

    def body(d_ref, o_ref, uv_ref, dol_ref, dl_ref, duv_ref, prod_s):
        @pl.when(pl.program_id(0) == 0)
        def _():
            duv_ref[...] = jnp.zeros(duv_ref.shape, F32)

        for hh in range(B_HEADS):
            dh = d_ref[:, HD * hh:HD * hh + HD]
            ol = o_ref[:, B_KV_LORA * hh:B_KV_LORA * (hh + 1)]
            dol = _mm_nt(dh, uv_ref[hh])
            dol_ref[:, B_KV_LORA * hh:B_KV_LORA * (hh + 1)] = dol.astype(MXU)
            prod_s[:, B_KV_LORA * hh:B_KV_LORA * (hh + 1)] = dol * ol
            duv_ref[hh] += _mm_tn(ol, dh)
        dl_ref[...] = _group_sums_t(prod_s[...], B_KV_LORA)[0:B_HEADS, :]

    sd = jax.ShapeDtypeStruct
    return pl.pallas_call(
        body, name="latent_out_backward", grid=(s // tm,),
        out_shape=(sd(o_lat.shape, MXU), sd((B_HEADS, s), F32), sd(w_uv.shape, F32)),
        in_specs=[_row_spec(tm, d_ob.shape[1]), _row_spec(tm, o_lat.shape[1]), _full_spec(w_uv.shape)],
        out_specs=(_row_spec(tm, o_lat.shape[1]), _rows_spec(B_HEADS, tm), _full_spec(w_uv.shape)),
        scratch_shapes=[pltpu.VMEM((tm, o_lat.shape[1]), F32)],
        compiler_params=_params(("arbitrary",)),
    )(d_ob, o_lat, w_uv)


def even_prep_backward(dqa, dka, dva, dqb, dkb, dvb, qa_raw, ka_raw, cq_raw, ckv_raw,
                       qn, kn, qln, kvln, w_uq, w_uq_t, w_uk, cos_a, sin_a, cos_t, sin_t):
    s = qa_raw.shape[0]
    tm = min(ROW_TILE, s)
    qb_w = B_HEADS * (B_NOPE + B_ROPE)

    def body(dqa_ref, dka_ref, dva_ref, dqb_ref, dkb_ref, dvb_ref, qa_ref, ka_ref, cq_ref, ckv_ref,
             qn_ref, kn_ref, qln_ref, kvln_ref, uq_ref, uqt_ref, uk_ref, ca_ref, sa_ref, ct_ref, st_ref,
             pqa, pka, pva, pcq, pckv, pkr, gqn, gkn, gqln, gkvln, guq, guk, dqb_s):
        @pl.when(pl.program_id(0) == 0)
        def _():
            for r in (gqn, gkn, gqln, gkvln, guq, guk):
                r[...] = jnp.zeros(r.shape, F32)

        ca, sa, ct, st = ca_ref[...], sa_ref[...], ct_ref[...], st_ref[...]
        acc_q = jnp.zeros((1, HD), F32)
        for hh in range(A_HEADS):
            dyn = _rope_t(dqa_ref[:, HD * hh:HD * hh + HD], ca, sa, 32)
            dx, dg = _rms_bwd(dyn, qa_ref[:, HD * hh:HD * hh + HD], qn_ref[...])
            pqa[:, HD * hh:HD * hh + HD] = dx.astype(MXU)
            acc_q = acc_q + jnp.sum(dg, axis=0, keepdims=True)
        gqn[...] += acc_q
        acc_k = jnp.zeros((1, HD), F32)
        for g in range(A_KV):
            dyn = _rope_t(dka_ref[g], ca, sa, 32)
            dx, dg = _rms_bwd(dyn, ka_ref[:, HD * g:HD * g + HD], kn_ref[...])
            pka[:, HD * g:HD * g + HD] = dx.astype(MXU)
            acc_k = acc_k + jnp.sum(dg, axis=0, keepdims=True)
            pva[:, HD * g:HD * g + HD] = dva_ref[g].astype(MXU)
        gkn[...] += acc_k
        cq_raw = cq_ref[...]
        cq_n = cq_raw * _rms(cq_raw) * qln_ref[...]
        qb = _mm(cq_n, uq_ref[...])
        for hh in range(B_HEADS):
            base = (B_NOPE + B_ROPE) * hh
            dlat = dqb_ref[:, B_QK * hh:B_QK * hh + B_KV_LORA]
            dqb_s[:, base:base + B_NOPE] = _mm(dlat, uk_ref[hh])
            guk[hh] += _mm_tn(dlat, qb[:, base:base + B_NOPE])
            dqb_s[:, base + B_NOPE:base + B_NOPE + B_ROPE] = _rope_t(
                dqb_ref[:, B_QK * hh + B_KV_LORA:B_QK * (hh + 1)], ct, st, 32)
        dqb_all = dqb_s[...]
        guq[...] += _mm_tn(cq_n, dqb_all)
        dx, dg = _rms_bwd(_mm(dqb_all, uqt_ref[...]), cq_raw, qln_ref[...])
        pcq[...] = dx.astype(MXU)
        gqln[...] += jnp.sum(dg, axis=0, keepdims=True)
        dkb_sum = dkb_ref[0] + dkb_ref[1]
        dckv = dkb_sum[:, 0:B_KV_LORA] + dvb_ref[0] + dvb_ref[1]
        dx, dg = _rms_bwd(dckv, ckv_ref[...], kvln_ref[...])
        pckv[...] = dx.astype(MXU)
        gkvln[...] += jnp.sum(dg, axis=0, keepdims=True)
        pkr[...] = _rope_t(dkb_sum[:, B_KV_LORA:B_QK], ct, st, 32).astype(MXU)

    sd = jax.ShapeDtypeStruct
    args = [dqa, dka, dva, dqb, dkb, dvb, qa_raw, ka_raw, cq_raw, ckv_raw,
            qn, kn, qln, kvln, w_uq, w_uq_t, w_uk, cos_a, sin_a, cos_t, sin_t]
    in_specs = [_row_spec(tm, 512), _head_spec(A_KV, tm, HD), _head_spec(A_KV, tm, HD),
                _row_spec(tm, B_HEADS * B_QK), _head_spec(2, tm, B_QK), _head_spec(2, tm, B_KV_LORA),
                _row_spec(tm, 512), _row_spec(tm, 128), _row_spec(tm, B_Q_LORA), _row_spec(tm, B_KV_LORA),
                _full_spec(qn.shape), _full_spec(kn.shape), _full_spec(qln.shape), _full_spec(kvln.shape),
                _full_spec(w_uq.shape), _full_spec(w_uq_t.shape), _full_spec(w_uk.shape),
                _row_spec(tm, HD), _row_spec(tm, HD), _row_spec(tm, B_ROPE), _row_spec(tm, B_ROPE)]
    out_shape = (sd((s, 512), MXU), sd((s, 128), MXU), sd((s, 128), MXU), sd((s, B_Q_LORA), MXU),
                 sd((s, B_KV_LORA), MXU), sd((s, B_ROPE), MXU),
                 sd(qn.shape, F32), sd(kn.shape, F32), sd(qln.shape, F32), sd(kvln.shape, F32),
                 sd(w_uq.shape, F32), sd(w_uk.shape, F32))
    out_specs = (_row_spec(tm, 512), _row_spec(tm, 128), _row_spec(tm, 128), _row_spec(tm, B_Q_LORA),
                 _row_spec(tm, B_KV_LORA), _row_spec(tm, B_ROPE),
                 _full_spec(qn.shape), _full_spec(kn.shape), _full_spec(qln.shape), _full_spec(kvln.shape),
                 _full_spec(w_uq.shape), _full_spec(w_uk.shape))
    return pl.pallas_call(
        body, name="even_prep_backward", grid=(s // tm,), out_shape=out_shape, in_specs=in_specs, out_specs=out_specs,
        scratch_shapes=[pltpu.VMEM((tm, qb_w), F32)],
        compiler_params=_params(("arbitrary",)),
    )(*args)


def in_proj_backward(x, mod, nw, dx_out, pieces, w_in_t, name):
    s, d = x.shape
    tm = min(ROW_TILE, s)
    n_cols = w_in_t.shape[0]
    n = len(pieces)
    cols = [c for _, c in pieces]

    def body(*refs):
        x_ref, mod_ref, nw_ref, dxo_ref, wt_ref = refs[:5]
        p_refs = refs[5:5 + n]
        dx_ref, dw_ref, dv_ref, acc_ref = refs[5 + n:]
        i = pl.program_id(0)

        @pl.when(i == 0)
        def _():
            dw_ref[...] = jnp.zeros(dw_ref.shape, F32)
            acc_ref[...] = jnp.zeros(acc_ref.shape, F32)

        xn, g1, h = _modulated(x_ref[...], mod_ref, nw_ref)
        hb = h.astype(MXU)
        dh = jnp.zeros((tm, d), F32)
        for pr, (c0, c1) in zip(p_refs, cols):
            pc = pr[...].astype(MXU)
            dh = dh + jnp.dot(pc, wt_ref[c0:c1, :], preferred_element_type=F32)
            dw_ref[:, c0:c1] += _mm_tn(hb, pc)
        acc_ref[0:1, :] += jnp.sum(dh, axis=0, keepdims=True)
        acc_ref[1:2, :] += jnp.sum(dh * xn, axis=0, keepdims=True)
        dxn = dh * g1
        x = x_ref[...]
        r = _rms(x)
        dx_ref[...] = dxo_ref[...] + r * (dxn - xn * jnp.mean(dxn * xn, axis=-1, keepdims=True))

        @pl.when(i == pl.num_programs(0) - 1)
        def _():
            dg1 = acc_ref[1:2, :]
            dv_ref[0:1, :] = acc_ref[0:1, :]
            dv_ref[1:2, :] = dg1 * nw_ref[...]
            dv_ref[2:3, :] = dg1 * (1.0 + mod_ref[1:2, :])
            dv_ref[3:4, :] = jnp.zeros((1, d), F32)

    arrs = [a for a, _ in pieces]
    sd = jax.ShapeDtypeStruct
    return pl.pallas_call(
        body, name=name, grid=(s // tm,),
        out_shape=(sd((s, d), F32), sd((d, n_cols), F32), sd((4, d), F32)),
        in_specs=[_row_spec(tm, d), _full_spec(mod.shape), _full_spec(nw.shape), _row_spec(tm, d),
                  _full_spec(w_in_t.shape)] + [_row_spec(tm, a.shape[1]) for a in arrs],
        out_specs=(_row_spec(tm, d), _full_spec((d, n_cols)), _full_spec((4, d))),
        scratch_shapes=[pltpu.VMEM((8, d), F32)],
        compiler_params=_params(("arbitrary",)),
    )(x, mod, nw, dx_out, w_in_t, *arrs)


def ada_weight_grad(c_all, dmod_cols):
    d = c_all.shape[1]
    w = dmod_cols.shape[2]

    def body(c_ref, dm_ref, out_ref):
        ca = _silu(c_ref[...])
        for l in range(2):
            out_ref[l] = _mm_tn(ca, dm_ref[l])

    return pl.pallas_call(
        body, name="ada_weight_grad",
        out_shape=jax.ShapeDtypeStruct((2, d, w), F32),
        compiler_params=pltpu.CompilerParams(vmem_limit_bytes=VMEM_LIMIT),
    )(c_all, dmod_cols)


def adamw_rows(g_slots, w, m, v, name):
    n, r, lanes = g_slots.shape
    tr = r
    for cand in (2048, 1752, 1536, 1104, 1024, 552, 512):
        if r % cand == 0 and cand <= r:
            tr = cand
            break
    c1 = 1.0 - ADAM_B1 ** ADAM_STEP
    c2 = 1.0 - ADAM_B2 ** ADAM_STEP

    def body(g_ref, w_ref, m_ref, v_ref, go, do, mo, vo):
        g = g_ref[0]
        for k in range(1, n):
            g = g + g_ref[k]
        m_new = ADAM_B1 * m_ref[...] + (1.0 - ADAM_B1) * g
        v_new = ADAM_B2 * v_ref[...] + (1.0 - ADAM_B2) * (g * g)
        m_hat = m_new / c1
        v_hat = v_new / c2
        go[...] = g
        do[...] = -ADAM_LR * (m_hat / (jnp.sqrt(v_hat) + ADAM_EPS) + ADAM_WD * w_ref[...])
        mo[...] = m_new
        vo[...] = v_new

    row = pl.BlockSpec((tr, lanes), lambda i: (i, 0))
    sd = jax.ShapeDtypeStruct((r, lanes), F32)
    return pl.pallas_call(
        body, name=name, grid=(r // tr,), out_shape=(sd, sd, sd, sd),
        in_specs=[pl.BlockSpec((n, tr, lanes), lambda i: (0, i, 0)), row, row, row],
        out_specs=(row, row, row, row),
        compiler_params=_params(("parallel",)),
    )(g_slots, w, m, v)


def _rope_tables(s):
    def cs(pos, dim):
        inv = ROPE_THETA ** (-jnp.arange(0, dim, 2, dtype=F32) / dim)
        ang = pos.astype(F32)[:, None] * inv[None, :]
        return jnp.cos(ang), jnp.sin(ang)

    rows = s // GRID_W
    row = jnp.repeat(jnp.arange(rows), GRID_W)
    col = jnp.tile(jnp.arange(GRID_W), rows)
    cr, sr = cs(row, HD // 2)
    cc, sc = cs(col, HD // 2)
    ct, st = cs(jnp.arange(s), B_ROPE)
    cos_a = jnp.concatenate([cr, cr, cc, cc], axis=-1)
    sin_a = jnp.concatenate([-sr, sr, -sc, sc], axis=-1)
    return cos_a, sin_a, jnp.concatenate([ct, ct], axis=-1), jnp.concatenate([-st, st], axis=-1)


def _rows128(a):
    return a.reshape(-1, 128)


def _even_cols_to_kernel(w):
    return jnp.concatenate([w[:, :1664], w[:, 1696:], w[:, 1664:1696]], axis=1)


def _even_cols_to_reference(w):
    return jnp.concatenate([w[:, :1664], w[:, 2176:], w[:, 1664:2176]], axis=1)


def _col_blocks(w):
    d, n8 = w.shape
    n = n8 // N_DEV
    return w.reshape(d, N_DEV, n).transpose(1, 0, 2).reshape(N_DEV, d * n // 128, 128)


def _from_col_blocks(p, d):
    n = p.shape[1] * 128 // d
    return p.reshape(N_DEV, d, n).transpose(1, 0, 2).reshape(d, N_DEV * n)


def _pad_rows(flat, rows):
    return jnp.pad(flat, (0, rows * 128 - flat.shape[0])).reshape(rows, 128)


def kernel(x, c, norm_w, ada_w, ada_b, even_w_in, a_q_norm, a_k_norm, b_q_lora_norm, b_kv_lora_norm, b_w_uq, b_w_uk, b_w_uv, even_w_out, odd_w_in, c_sink, odd_w_out, final_norm, loss_target, m_norm_w, m_ada_w, m_ada_b, m_even_w_in, m_a_q_norm, m_a_k_norm, m_b_q_lora_norm, m_b_kv_lora_norm, m_b_w_uq, m_b_w_uk, m_b_w_uv, m_even_w_out, m_odd_w_in, m_c_sink, m_odd_w_out, m_final_norm, v_norm_w, v_ada_w, v_ada_b, v_even_w_in, v_a_q_norm, v_a_k_norm, v_b_q_lora_norm, v_b_kv_lora_norm, v_b_w_uq, v_b_w_uk, v_b_w_uv, v_even_w_out, v_odd_w_in, v_c_sink, v_odd_w_out, v_final_norm):
    s, d = x.shape[1], x.shape[2]
    x0 = x[0]
    target = loss_target[0]
    me_flat = 4 * lax.axis_index("x") + 2 * lax.axis_index("y") + lax.axis_index("c")

    big = [even_w_in, odd_w_in, even_w_out, odd_w_out, b_w_uq]
    big_rows = [w.size // 128 for w in big]
    offs = [0]
    for r in big_rows:
        offs.append(offs[-1] + r)
    r_big = offs[-1]
    pack = jnp.concatenate([_rows128(w) for w in big], axis=0)
    gathered = all_gather_rows(pack, MXU, "gather_weights").reshape(N_DEV, r_big, 128)
    seg = [gathered[:, offs[i]:offs[i + 1]] for i in range(5)]
    w_in_e = _even_cols_to_kernel(_from_col_blocks(seg[0], d))
    w_in_o = _from_col_blocks(seg[1], d)
    w_out_e = seg[2].reshape(1024, d)
    w_out_o = seg[3].reshape(1024, d)
    w_uq = _from_col_blocks(seg[4], B_Q_LORA)
    w_uk = jnp.transpose(b_w_uk[0], (1, 0, 2)).astype(MXU)
    w_uv = jnp.transpose(b_w_uv[0], (1, 0, 2)).astype(MXU)

    wcols = ada_w.shape[2]
    bias_cols = lax.dynamic_slice_in_dim(ada_b.reshape(2, N_DEV, wcols), me_flat, 1, axis=1)
    call, modp = ada_forward(jnp.broadcast_to(c, (8, d)), ada_w, bias_cols)
    c_all = call[:, 0, :]
    mod = jnp.transpose(modp[:, :, 0, :], (1, 0, 2)).reshape(2, 3, d)
    mod_e, mod_o = mod[0], mod[1]
    nw_e, nw_o = norm_w[0:1], norm_w[1:2]

    cos_a, sin_a, cos_t, sin_t = _rope_tables(s)
    slopes = (2.0 ** (-8.0 * jnp.arange(1, C_HEADS + 1, dtype=F32) / C_HEADS)).reshape(C_HEADS, 1, 1)
    sink2 = c_sink.reshape(C_HEADS, 1, 1) * LOG2E

    (qa, ka, va, qb, kb, kat, vat, kbt, qa_raw, ka_raw, cq_raw, ckv_raw, ga, gb) = even_in_forward(
        x0, mod_e, nw_e, w_in_e, a_q_norm, a_k_norm, b_q_lora_norm, b_kv_lora_norm, w_uq, w_uk,
        cos_a, sin_a, cos_t, sin_t)
    tk_dense = min(512, s)
    oa, lse_a = flash_forward(qa, ka, vat, scale=HD ** -0.5, dv=HD, tq=min(256, s), tk=tk_dense, name="attn_a_fwd")
    scale_b = (B_NOPE + B_ROPE) ** -0.5
    o_lat, lse_b = flash_forward(qb, kb, kbt, scale=scale_b, dv=B_KV_LORA, tq=min(128, s), tk=tk_dense,
                                 name="attn_b_fwd")
    ob = latent_out_forward(o_lat, w_uv)
    x1, y_e = mixer_out_forward(x0, mod_e, [(oa, ga), (ob, gb)], w_out_e, "even_out_fwd")

    qc, kc, vc, kct, vct, gc = odd_in_forward(x1, mod_o, nw_o, w_in_o)
    oc, lse_c = flash_forward(qc, kc, vct, scale=HD ** -0.5, dv=HD, tq=WINDOW, tk=WINDOW, name="attn_c_fwd",
                              banded=True, sink2=sink2, slopes=slopes)
    x2, y_o = mixer_out_forward(x1, mod_o, [(oc, gc)], w_out_o, "odd_out_fwd")

    loss_lanes, dx2, d_final = loss_head(x2, target, final_norm.reshape(1, d))
    loss = lax.psum(0.5 * jnp.sum(loss_lanes) / d, MESH_AXES)

    doc, dgc, delta_c, dgate_o, dw_out_o, dsink = mixer_out_backward(
        dx2, y_o, mod_o, [(oc, gc)], w_out_o.T, [C_HEADS], "odd_out_bwd", lse=lse_c.reshape(C_HEADS, s),
        sink=sink2.reshape(C_HEADS, 1))
    rows3 = lambda t: t.reshape(t.shape[0], 1, s)
    dqc, dkc, dvc = flash_backward(qc, kc, kct, vc, doc, lse_c, rows3(delta_c), scale=HD ** -0.5, dv=HD, tq=WINDOW,
                                   tk=WINDOW, gq=C_KV, name="attn_c_bwd", banded=True, slopes=slopes)
    to_rows = lambda t: jnp.transpose(t, (1, 0, 2)).reshape(s, -1)
    dx1, dw_in_o, dvec_o = in_proj_backward(
        x1, mod_o, nw_o, dx2, [(dqc, O_Q), (to_rows(dkc), O_K), (to_rows(dvc), O_V), (dgc, O_G)], w_in_o.T,
        "odd_in_bwd")

    doa, dga, dob, dgb, delta_a, dgate_e, dw_out_e = mixer_out_backward(
        dx1, y_e, mod_e, [(oa, ga), (ob, gb)], w_out_e.T, [A_HEADS, 0], "even_out_bwd")
    d_olat, delta_b, dw_uv = latent_out_backward(dob, o_lat, w_uv)
    dqb, dkb, dvb = flash_backward(qb, kb, kbt, None, d_olat, lse_b, rows3(delta_b), scale=scale_b, dv=B_KV_LORA,
                                   tq=min(256, s), tk=tk_dense, gq=2, name="attn_b_bwd")
    dqa, dka, dva = flash_backward(qa, ka, kat, va, doa, lse_a, rows3(delta_a), scale=HD ** -0.5, dv=HD,
                                   tq=min(256, s), tk=tk_dense, gq=A_KV, name="attn_a_bwd")
    (pqa, pka, pva, pcq, pckv, pkr, g_qn, g_kn, g_qln, g_kvln, dw_uq, dw_uk) = even_prep_backward(
        dqa, dka, dva, dqb, dkb, dvb, qa_raw, ka_raw, cq_raw, ckv_raw,
        a_q_norm, a_k_norm, b_q_lora_norm, b_kv_lora_norm, w_uq, w_uq.T, w_uk, cos_a, sin_a, cos_t, sin_t)
    dx0, dw_in_e, dvec_e = in_proj_backward(
        x0, mod_e, nw_e, dx1,
        [(pqa, E_QA), (pka, E_KA), (pva, E_VA), (dga, E_GA), (pcq, E_CQ), (pckv, E_CKV), (dgb, E_GB), (pkr, E_KR)],
        w_in_e.T, "even_in_bwd")

    dmod = jnp.stack([jnp.concatenate([dvec_e[0], dvec_e[1], dgate_e[0]]),
                      jnp.concatenate([dvec_o[0], dvec_o[1], dgate_o[0]])])
    d_norm_w = jnp.stack([dvec_e[2], dvec_o[2]])
    small_names = ["norm_w", "ada_b", "a_q_norm", "a_k_norm", "b_q_lora_norm", "b_kv_lora_norm", "b_w_uk", "b_w_uv",
                   "c_sink", "final_norm"]
    small_w = [norm_w, ada_b, a_q_norm, a_k_norm, b_q_lora_norm, b_kv_lora_norm, b_w_uk, b_w_uv, c_sink, final_norm]
    small_m = [m_norm_w, m_ada_b, m_a_q_norm, m_a_k_norm, m_b_q_lora_norm, m_b_kv_lora_norm, m_b_w_uk, m_b_w_uv,
               m_c_sink, m_final_norm]
    small_v = [v_norm_w, v_ada_b, v_a_q_norm, v_a_k_norm, v_b_q_lora_norm, v_b_kv_lora_norm, v_b_w_uk, v_b_w_uv,
               v_c_sink, v_final_norm]
    small_g = [d_norm_w, dmod, g_qn, g_kn, g_qln, g_kvln, jnp.transpose(dw_uk, (1, 0, 2)), jnp.transpose(dw_uv, (1, 0, 2)),
               dsink, d_final]
    sizes = [w.size for w in small_w]
    n_small = sum(sizes)
    r_small = -(-n_small // (128 * 8)) * 8
    flat_pack = lambda arrs: _pad_rows(jnp.concatenate([a.reshape(-1) for a in arrs]), r_small)
    g_small_all = all_gather_rows(flat_pack(small_g), F32, "gather_small_grads").reshape(N_DEV, r_small, 128)
    sm = adamw_rows(g_small_all, flat_pack(small_w), flat_pack(small_m), flat_pack(small_v), "adamw_small")

    def unpack_small(packed):
        flat = packed.reshape(-1)
        out, o = {}, 0
        for nm, w, sz in zip(small_names, small_w, sizes):
            out[nm] = flat[o:o + sz].reshape(w.shape)
            o += sz
        return out

    sm = [unpack_small(p) for p in sm]

    dmod_all = g_small_all.reshape(N_DEV, -1)[:, sizes[0]:sizes[0] + sizes[1]].reshape(N_DEV, 2, N_DEV, wcols)
    dmod_cols = lax.dynamic_slice_in_dim(dmod_all, me_flat, 1, axis=2)[:, :, 0, :]
    pad16 = lambda a: jnp.concatenate([a, jnp.zeros_like(a)], axis=0)
    g_ada_w = ada_weight_grad(pad16(c_all), jnp.transpose(pad16(dmod_cols), (1, 0, 2)))
    ada = adamw_rows(_rows128(g_ada_w)[None], _rows128(ada_w), _rows128(m_ada_w), _rows128(v_ada_w), "adamw_ada_w")
    ada = [p.reshape(ada_w.shape) for p in ada]

    g_blocks = jnp.concatenate([
        _col_blocks(_even_cols_to_reference(dw_in_e)), _col_blocks(dw_in_o),
        dw_out_e.reshape(N_DEV, -1, 128), dw_out_o.reshape(N_DEV, -1, 128), _col_blocks(dw_uq)], axis=1)
    landed = scatter_blocks(g_blocks, "scatter_weight_grads")
    big_m = [m_even_w_in, m_odd_w_in, m_even_w_out, m_odd_w_out, m_b_w_uq]
    big_v = [v_even_w_in, v_odd_w_in, v_even_w_out, v_odd_w_out, v_b_w_uq]
    cat = lambda arrs: jnp.concatenate([_rows128(a) for a in arrs], axis=0)
    bg = adamw_rows(landed, pack, cat(big_m), cat(big_v), "adamw_big")
    big_names = ["even_w_in", "odd_w_in", "even_w_out", "odd_w_out", "b_w_uq"]
    bg = [{nm: p[offs[i]:offs[i + 1]].reshape(w.shape) for i, (nm, w) in enumerate(zip(big_names, big))} for p in bg]

    order = ["norm_w", "ada_w", "ada_b", "even_w_in", "a_q_norm", "a_k_norm", "b_q_lora_norm", "b_kv_lora_norm",
             "b_w_uq", "b_w_uk", "b_w_uv", "even_w_out", "odd_w_in", "c_sink", "odd_w_out", "final_norm"]

    def pick(kind):
        out = []
        for nm in order:
            if nm == "ada_w":
                out.append(ada[kind])
            elif nm in big_names:
                out.append(bg[kind][nm])
            else:
                out.append(sm[kind][nm])
        return out

    return (loss, dx0[None], *pick(0), *pick(1), *pick(2), *pick(3))
```

```python
import jax
import jax.numpy as jnp
from jax import lax
from jax.experimental import pallas as pl
from jax.experimental.pallas import tpu as pltpu

F32 = jnp.float32
MXU = jnp.bfloat16
EPS = 1e-6
ROPE_THETA = 10000.0
GRID_W = 64
HD = 64
N_DEV = 8
MESH_AXES = ("x", "y", "c")

A_HEADS, A_KV = 8, 2
B_HEADS, B_NOPE, B_ROPE, B_Q_LORA, B_KV_LORA = 8, 64, 32, 256, 128
B_QK = B_KV_LORA + B_ROPE
C_HEADS, C_KV = 16, 4
WINDOW = 128

ADAM_LR, ADAM_B1, ADAM_B2, ADAM_EPS, ADAM_WD, ADAM_STEP = 0.001, 0.9, 0.999, 1e-08, 0.01, 10

ROW_TILE = 256
VMEM_LIMIT = 56 * 1024 * 1024

E_QA, E_KA, E_VA, E_GA, E_CQ, E_CKV, E_GB, E_KR = (
    (0, 512), (512, 640), (640, 768), (768, 1280), (1280, 1536), (1536, 1664), (1664, 2176), (2176, 2208))
EVEN_IN = 2208
O_Q, O_K, O_V, O_G = (0, 1024), (1024, 1280), (1280, 1536), (1536, 2560)
ODD_IN = 2560


def _mm(a, b):
    return jnp.dot(a.astype(MXU), b.astype(MXU), preferred_element_type=F32)


def _mm_nt(a, b):
    return lax.dot_general(a.astype(MXU), b.astype(MXU), (((1,), (1,)), ((), ())), preferred_element_type=F32)


def _mm_tn(a, b):
    return lax.dot_general(a.astype(MXU), b.astype(MXU), (((0,), (0,)), ((), ())), preferred_element_type=F32)


def _group_sums_t(prod, group):
    tm, w = prod.shape
    sel = (lax.broadcasted_iota(jnp.int32, (w, 128), 0) // group
           == lax.broadcasted_iota(jnp.int32, (w, 128), 1)).astype(MXU)
    hi = prod.astype(MXU)
    lo = prod - hi.astype(F32)
    return (_mm(hi, sel) + _mm(lo, sel)).T


def _sigmoid(z):
    return 1.0 / (1.0 + jnp.exp(-z))


def _silu(z):
    return z * _sigmoid(z)


def _rms(x):
    return lax.rsqrt(jnp.mean(x * x, axis=-1, keepdims=True) + EPS)


def _swap_halves(y, group):
    n = y.shape[-1]
    half = group // 2
    fwd = pltpu.roll(y, half, 1)
    if n == group:
        return fwd
    back = pltpu.roll(y, n - half, 1)
    lane = lax.broadcasted_iota(jnp.int32, y.shape, 1)
    return jnp.where((lane % group) < half, back, fwd)


def _rope(y, cos, sin, group):
    return y * cos + _swap_halves(y, group) * sin


def _rope_t(d, cos, sin, group):
    return d * cos - _swap_halves(d, group) * sin


def _rms_bwd(dy, x, g):
    r = _rms(x)
    xhat = x * r
    dxhat = dy * g
    dx = r * (dxhat - xhat * jnp.mean(dxhat * xhat, axis=-1, keepdims=True))
    return dx, dy * xhat


def _params(sem, vmem=VMEM_LIMIT):
    return pltpu.CompilerParams(dimension_semantics=sem, vmem_limit_bytes=vmem)


def _row_spec(tm, w):
    return pl.BlockSpec((tm, w), lambda i: (i, 0))


def _full_spec(shape):
    nd = len(shape)
    return pl.BlockSpec(shape, lambda i: (0,) * nd)


def _head_spec(h, tm, w):
    return pl.BlockSpec((h, tm, w), lambda i: (0, i, 0))


def _headt_spec(h, w, tm):
    return pl.BlockSpec((h, w, tm), lambda i: (0, 0, i))


def _rows_spec(h, tm):
    return pl.BlockSpec((h, tm), lambda i: (0, i))


def _me():
    return lax.axis_index("x"), lax.axis_index("y"), lax.axis_index("c")


def _flat(p):
    return 4 * p[0] + 2 * p[1] + p[2]


def _peer(me, k):
    x, y, c = me
    return (1 - x if k & 4 else x, 1 - y if k & 2 else y, 1 - c if k & 1 else c)


MESH_ID = pl.DeviceIdType.MESH


def all_gather_rows(x_shard, out_dtype, name):
    r, n = x_shard.shape

    def body(x_ref, out_ref, xs_ref, send_sems, recv_sems, local_sem):
        me = _me()
        x, y, c = me
        sibling = (x, y, 1 - c)
        chips = [(1 - x, y), (x, 1 - y), (1 - x, 1 - y)]
        xs_ref[...] = x_ref[...].astype(out_dtype)

        def rows(p):
            return out_ref.at[pl.ds(pl.multiple_of(_flat(p) * r, 16), r), :]

        def copy(k, block, to, src=None):
            return pltpu.make_async_remote_copy(
                src_ref=rows(block) if src is None else src, dst_ref=rows(block),
                send_sem=send_sems.at[k], recv_sem=recv_sems.at[k], device_id=to, device_id_type=MESH_ID)

        mine = pltpu.make_async_copy(xs_ref, rows(me), local_sem)
        mine.start()
        first = [copy(0, me, sibling, src=xs_ref)]
        first += [copy(1 + j, me, (*chip, c), src=xs_ref) for j, chip in enumerate(chips)]
        for cp in first:
            cp.start()
        passed = [copy(4 + j, (*chip, c), sibling) for j, chip in enumerate(chips)]
        for j, chip in enumerate(chips):
            copy(1 + j, (*chip, c), me).wait_recv()
            passed[j].start()
        copy(0, sibling, me).wait_recv()
        for j, chip in enumerate(chips):
            copy(4 + j, (*chip, 1 - c), me).wait_recv()
        for cp in first + passed:
            cp.wait_send()
        mine.wait()

    return pl.pallas_call(
        body, name=name,
        out_shape=jax.ShapeDtypeStruct((N_DEV * r, n), out_dtype),
        in_specs=[pl.BlockSpec(memory_space=pltpu.VMEM)],
        out_specs=pl.BlockSpec(memory_space=pltpu.VMEM),
        scratch_shapes=[pltpu.VMEM((r, n), out_dtype), pltpu.SemaphoreType.DMA((7,)),
                        pltpu.SemaphoreType.DMA((7,)), pltpu.SemaphoreType.DMA],
        compiler_params=pltpu.CompilerParams(vmem_limit_bytes=VMEM_LIMIT),
    )(x_shard)


def scatter_blocks(blocks, name):
    _, r, n = blocks.shape

    def body(g_ref, land_ref, send_sems, recv_sems, local_sem):
        me = _me()
        mi = _flat(me)
        local = pltpu.make_async_copy(g_ref.at[mi], land_ref.at[mi], local_sem)
        local.start()
        copies = []
        for k in range(1, N_DEV):
            peer = _peer(me, k)
            pi = _flat(peer)
            copies.append(pltpu.make_async_remote_copy(
                src_ref=g_ref.at[pi], dst_ref=land_ref.at[mi],
                send_sem=send_sems.at[k - 1], recv_sem=recv_sems.at[k - 1], device_id=peer, device_id_type=MESH_ID))
        for cp in copies:
            cp.start()
        for k in range(1, N_DEV):
            pi = _flat(_peer(me, k))
            pltpu.make_async_remote_copy(
                src_ref=g_ref.at[pi], dst_ref=land_ref.at[pi],
                send_sem=send_sems.at[k - 1], recv_sem=recv_sems.at[k - 1], device_id=_peer(me, k),
                device_id_type=MESH_ID).wait_recv()
        for cp in copies:
            cp.wait_send()
        local.wait()

    return pl.pallas_call(
        body, name=name,
        out_shape=jax.ShapeDtypeStruct(blocks.shape, blocks.dtype),
        in_specs=[pl.BlockSpec(memory_space=pl.ANY)],
        out_specs=pl.BlockSpec(memory_space=pl.ANY),
        scratch_shapes=[pltpu.SemaphoreType.DMA((7,)), pltpu.SemaphoreType.DMA((7,)), pltpu.SemaphoreType.DMA],
    )(blocks)


def ada_forward(c8, ada_w, bias_cols):
    d = c8.shape[1]
    w = ada_w.shape[2]

    def body(c_ref, w_ref, b_ref, call_ref, modp_ref, part_ref, s1, r1, s2, r2):
        me = _me()
        mi = _flat(me)
        call_ref[mi] = c_ref[...]
        gather = []
        for k in range(1, N_DEV):
            gather.append(pltpu.make_async_remote_copy(
                src_ref=c_ref, dst_ref=call_ref.at[mi], send_sem=s1.at[k - 1], recv_sem=r1.at[k - 1],
                device_id=_peer(me, k), device_id_type=MESH_ID))
        for cp in gather:
            cp.start()
        for k in range(1, N_DEV):
            pltpu.make_async_remote_copy(
                src_ref=c_ref, dst_ref=call_ref.at[_flat(_peer(me, k))], send_sem=s1.at[k - 1],
                recv_sem=r1.at[k - 1], device_id=_peer(me, k), device_id_type=MESH_ID).wait_recv()
        ca = _silu(call_ref[...].reshape(N_DEV * 8, d))
        for l in range(2):
            part = _mm(ca, w_ref[l]) + b_ref[l]
            for b in range(N_DEV):
                part_ref[b, l] = part[8 * b:8 * b + 8, :]
        modp_ref[mi] = part_ref[mi]
        spread = []
        for k in range(1, N_DEV):
            peer = _peer(me, k)
            spread.append(pltpu.make_async_remote_copy(
                src_ref=part_ref.at[_flat(peer)], dst_ref=modp_ref.at[mi], send_sem=s2.at[k - 1],
                recv_sem=r2.at[k - 1], device_id=peer, device_id_type=MESH_ID))
        for cp in spread:
            cp.start()
        for k in range(1, N_DEV):
            pi = _flat(_peer(me, k))
            pltpu.make_async_remote_copy(
                src_ref=part_ref.at[pi], dst_ref=modp_ref.at[pi], send_sem=s2.at[k - 1],
                recv_sem=r2.at[k - 1], device_id=_peer(me, k), device_id_type=MESH_ID).wait_recv()
        for cp in gather + spread:
            cp.wait_send()

    vm = pl.BlockSpec(memory_space=pltpu.VMEM)
    return pl.pallas_call(
        body, name="ada_forward",
        out_shape=(jax.ShapeDtypeStruct((N_DEV, 8, d), F32), jax.ShapeDtypeStruct((N_DEV, 2, 8, w), F32)),
        in_specs=[vm, vm, vm], out_specs=(vm, vm),
        scratch_shapes=[pltpu.VMEM((N_DEV, 2, 8, w), F32)] + [pltpu.SemaphoreType.DMA((7,))] * 4,
        compiler_params=pltpu.CompilerParams(vmem_limit_bytes=VMEM_LIMIT),
    )(c8, ada_w, bias_cols)


def _modulated(x, mod_ref, nw_ref):
    xn = x * _rms(x)
    g1 = nw_ref[...] * (1.0 + mod_ref[1:2, :])
    return xn, g1, xn * g1 + mod_ref[0:1, :]


def even_in_forward(x, mod, nw, w_in, qn, kn, qln, kvln, w_uq, w_uk, cos_a, sin_a, cos_t, sin_t):
    s, d = x.shape
    tm = min(ROW_TILE, s)

    def body(x_ref, mod_ref, nw_ref, w_ref, qn_ref, kn_ref, qln_ref, kvln_ref, uq_ref, uk_ref,
             ca_ref, sa_ref, ct_ref, st_ref,
             qa_o, ka_o, va_o, qb_o, kb_o, kat_o, vat_o, kbt_o, qa_raw_o, ka_raw_o, cq_raw_o, ckv_raw_o, ga_o, gb_o):
        _, _, h = _modulated(x_ref[...], mod_ref, nw_ref)
        h = h.astype(MXU)

        def proj(cols):
            return jnp.dot(h, w_ref[:, cols[0]:cols[1]], preferred_element_type=F32)

        ca, sa, ct, st = ca_ref[...], sa_ref[...], ct_ref[...], st_ref[...]
        qa = proj(E_QA)
        qa_raw_o[...] = qa
        for hh in range(A_HEADS):
            xh = qa[:, HD * hh:HD * hh + HD]
            qa_o[hh] = _rope(xh * _rms(xh) * qn_ref[...], ca, sa, 32).astype(MXU)
        ka = proj(E_KA)
        ka_raw_o[...] = ka
        va = proj(E_VA)
        for g in range(A_KV):
            xh = ka[:, HD * g:HD * g + HD]
            kr = _rope(xh * _rms(xh) * kn_ref[...], ca, sa, 32)
            vh = va[:, HD * g:HD * g + HD]
            ka_o[g] = kr.astype(MXU)
            va_o[g] = vh.astype(MXU)
            kat_o[g] = kr.T.astype(MXU)
            vat_o[g] = vh.T.astype(MXU)
        ga_o[...] = proj(E_GA)
        gb_o[...] = proj(E_GB)
        cq = proj(E_CQ)
        cq_raw_o[...] = cq
        qb = _mm(cq * _rms(cq) * qln_ref[...], uq_ref[...])
        for hh in range(B_HEADS):
            base = (B_NOPE + B_ROPE) * hh
            qb_o[hh, :, 0:B_KV_LORA] = _mm_nt(qb[:, base:base + B_NOPE], uk_ref[hh]).astype(MXU)
            qb_o[hh, :, B_KV_LORA:B_QK] = _rope(qb[:, base + B_NOPE:base + B_NOPE + B_ROPE], ct, st, 32).astype(MXU)
        ckv = proj(E_CKV)
        ckv_raw_o[...] = ckv
        ckv_n = ckv * _rms(ckv) * kvln_ref[...]
        k_rope = _rope(proj(E_KR), ct, st, 32)
        kb_o[0, :, 0:B_KV_LORA] = ckv_n.astype(MXU)
        kb_o[0, :, B_KV_LORA:B_QK] = k_rope.astype(MXU)
        kbt_o[0, 0:B_KV_LORA, :] = ckv_n.T.astype(MXU)
        kbt_o[0, B_KV_LORA:B_QK, :] = k_rope.T.astype(MXU)

    sd = jax.ShapeDtypeStruct
    outs = (sd((A_HEADS, s, HD), MXU), sd((A_KV, s, HD), MXU), sd((A_KV, s, HD), MXU),
            sd((B_HEADS, s, B_QK), MXU), sd((1, s, B_QK), MXU),
            sd((A_KV, HD, s), MXU), sd((A_KV, HD, s), MXU), sd((1, B_QK, s), MXU),
            sd((s, 512), F32), sd((s, 128), F32), sd((s, B_Q_LORA), F32), sd((s, B_KV_LORA), F32),
            sd((s, 512), F32), sd((s, 512), F32))
    out_specs = (_head_spec(A_HEADS, tm, HD), _head_spec(A_KV, tm, HD), _head_spec(A_KV, tm, HD),
                 _head_spec(B_HEADS, tm, B_QK), _head_spec(1, tm, B_QK),
                 _headt_spec(A_KV, HD, tm), _headt_spec(A_KV, HD, tm), _headt_spec(1, B_QK, tm),
                 _row_spec(tm, 512), _row_spec(tm, 128), _row_spec(tm, B_Q_LORA), _row_spec(tm, B_KV_LORA),
                 _row_spec(tm, 512), _row_spec(tm, 512))
    in_specs = [_row_spec(tm, d), _full_spec(mod.shape), _full_spec(nw.shape), _full_spec(w_in.shape),
                _full_spec(qn.shape), _full_spec(kn.shape), _full_spec(qln.shape), _full_spec(kvln.shape),
                _full_spec(w_uq.shape), _full_spec(w_uk.shape),
                _row_spec(tm, HD), _row_spec(tm, HD), _row_spec(tm, B_ROPE), _row_spec(tm, B_ROPE)]
    return pl.pallas_call(
        body, name="even_in_forward", grid=(s // tm,), out_shape=outs, in_specs=in_specs, out_specs=out_specs,
        compiler_params=_params(("parallel",)),
    )(x, mod, nw, w_in, qn, kn, qln, kvln, w_uq, w_uk, cos_a, sin_a, cos_t, sin_t)


def odd_in_forward(x, mod, nw, w_in):
    s, d = x.shape
    tm = min(ROW_TILE, s)

    def body(x_ref, mod_ref, nw_ref, w_ref, q_o, k_o, v_o, kt_o, vt_o, g_o):
        _, _, h = _modulated(x_ref[...], mod_ref, nw_ref)
        h = h.astype(MXU)

        def proj(cols):
            return jnp.dot(h, w_ref[:, cols[0]:cols[1]], preferred_element_type=F32)

        q = proj(O_Q)
        for hh in range(C_HEADS):
            q_o[hh] = q[:, HD * hh:HD * hh + HD].astype(MXU)
        k = proj(O_K)
        v = proj(O_V)
        for g in range(C_KV):
            kh = k[:, HD * g:HD * g + HD]
            vh = v[:, HD * g:HD * g + HD]
            k_o[g] = kh.astype(MXU)
            v_o[g] = vh.astype(MXU)
            kt_o[g] = kh.T.astype(MXU)
            vt_o[g] = vh.T.astype(MXU)
        g_o[...] = proj(O_G)

    sd = jax.ShapeDtypeStruct
    return pl.pallas_call(
        body, name="odd_in_forward", grid=(s // tm,),
        out_shape=(sd((C_HEADS, s, HD), MXU), sd((C_KV, s, HD), MXU), sd((C_KV, s, HD), MXU),
                   sd((C_KV, HD, s), MXU), sd((C_KV, HD, s), MXU), sd((s, 1024), F32)),
        in_specs=[_row_spec(tm, d), _full_spec(mod.shape), _full_spec(nw.shape), _full_spec(w_in.shape)],
        out_specs=(_head_spec(C_HEADS, tm, HD), _head_spec(C_KV, tm, HD), _head_spec(C_KV, tm, HD),
                   _headt_spec(C_KV, HD, tm), _headt_spec(C_KV, HD, tm), _row_spec(tm, 1024)),
        compiler_params=_params(("parallel",)),
    )(x, mod, nw, w_in)


def latent_out_forward(o_lat, w_uv):
    s = o_lat.shape[0]
    tm = min(ROW_TILE, s)

    def body(o_ref, uv_ref, out_ref):
        for hh in range(B_HEADS):
            out_ref[:, HD * hh:HD * hh + HD] = _mm(o_ref[:, B_KV_LORA * hh:B_KV_LORA * (hh + 1)], uv_ref[hh])

    return pl.pallas_call(
        body, name="latent_out_forward", grid=(s // tm,),
        out_shape=jax.ShapeDtypeStruct((s, B_HEADS * HD), F32),
        in_specs=[_row_spec(tm, o_lat.shape[1]), _full_spec(w_uv.shape)],
        out_specs=_row_spec(tm, B_HEADS * HD),
        compiler_params=_params(("parallel",)),
    )(o_lat, w_uv)


def mixer_out_forward(x, mod, pairs, w_out, name):
    s, d = x.shape
    tm = min(ROW_TILE, s)
    n = len(pairs)
    widths = [o.shape[1] for o, _ in pairs]

    def body(*refs):
        x_ref, mod_ref, w_ref = refs[:3]
        pr = refs[3:3 + 2 * n]
        xo_ref, y_ref = refs[3 + 2 * n:]
        y = jnp.zeros((tm, d), F32)
        r0 = 0
        for i in range(n):
            mix = pr[2 * i][...] * _silu(pr[2 * i + 1][...])
            y = y + _mm(mix, w_ref[r0:r0 + widths[i], :])
            r0 += widths[i]
        y_ref[...] = y
        xo_ref[...] = x_ref[...] + mod_ref[2:3, :] * y

    flat = [a for p in pairs for a in p]
    sd = jax.ShapeDtypeStruct
    return pl.pallas_call(
        body, name=name, grid=(s // tm,),
        out_shape=(sd((s, d), F32), sd((s, d), F32)),
        in_specs=[_row_spec(tm, d), _full_spec(mod.shape), _full_spec(w_out.shape)]
        + [_row_spec(tm, a.shape[1]) for a in flat],
        out_specs=(_row_spec(tm, d), _row_spec(tm, d)),
        compiler_params=_params(("parallel",)),
    )(x, mod, w_out, *flat)


LOG2E = 1.4426950408889634
ONES_ROWS = 16


def _col_max8(s3):
    m8 = jnp.max(s3, axis=0)
    return jnp.broadcast_to(jnp.max(m8, axis=0, keepdims=True), m8.shape)


def _with_ones(vt, n):
    return jnp.concatenate([vt, jnp.ones((ONES_ROWS, n), vt.dtype)], axis=0)


def flash_forward(q, k, vt, *, scale, dv, tq, tk, name):
    hq, s, dq = q.shape
    g_kv = k.shape[0]
    hpg = hq // g_kv
    nq = s // tq
    nk = s // tk
    m_cols = hpg * tq
    c = scale * LOG2E
    dvp = dv + ONES_ROWS

    def body(q_ref, k_ref, vt_ref, o_ref, lse_ref, m_s, acc_s):
        j = pl.program_id(2)

        @pl.when(j == 0)
        def _():
            m_s[...] = jnp.full((8, m_cols), -jnp.inf, F32)
            acc_s[...] = jnp.zeros((dvp, m_cols), F32)

        qq = q_ref[...].reshape(m_cols, dq)
        s3 = _mm_nt(k_ref[0], qq).reshape(tk // 8, 8, m_cols)
        m_old = m_s[...]
        m_new = jnp.maximum(m_old, _col_max8(s3) * c)
        p = jnp.exp2(s3 * c - m_new[None])
        alpha = jnp.exp2(m_old - m_new)
        pv = _mm(_with_ones(vt_ref[0], tk), p.reshape(tk, m_cols))
        acc_s[...] = (acc_s[...].reshape(dvp // 8, 8, m_cols) * alpha[None]).reshape(dvp, m_cols) + pv
        m_s[...] = m_new

        @pl.when(j == nk - 1)
        def _():
            l = acc_s[dv:dv + 1, :]
            ot = acc_s[0:dv, :] / l
            lse = m_s[0:1, :] + jnp.log2(l)
            for hh in range(hpg):
                o_ref[:, dv * hh:dv * hh + dv] = ot[:, tq * hh:tq * hh + tq].T
                lse_ref[hh] = lse[:, tq * hh:tq * hh + tq]

    sd = jax.ShapeDtypeStruct
    return pl.pallas_call(
        body, name=name, grid=(g_kv, nq, nk),
        out_shape=(sd((s, hq * dv), F32), sd((hq, 1, s), F32)),
        in_specs=[pl.BlockSpec((hpg, tq, dq), lambda g, i, j: (g, i, 0)),
                  pl.BlockSpec((1, tk, k.shape[2]), lambda g, i, j: (g, j, 0)),
                  pl.BlockSpec((1, dv, tk), lambda g, i, j: (g, 0, j))],
        out_specs=(pl.BlockSpec((tq, hpg * dv), lambda g, i, j: (i, g)),
                   pl.BlockSpec((hpg, 1, tq), lambda g, i, j: (g, 0, i))),
        scratch_shapes=[pltpu.VMEM((8, m_cols), F32), pltpu.VMEM((dvp, m_cols), F32)],
        compiler_params=_params(("parallel", "parallel", "arbitrary")),
    )(q, k, vt)


def _window_bias_t(i, nq, hpg, slope_ref):
    t = WINDOW
    r = lax.broadcasted_iota(jnp.int32, (3 * t, t), 0)
    cq = lax.broadcasted_iota(jnp.int32, (3 * t, t), 1)
    arel = jnp.abs(r - t - cq)
    ok = (arel <= WINDOW) & ((r >= t) | (i > 0)) & ((r < 2 * t) | (i < nq - 1))
    base = jnp.where(ok, arel.astype(F32) * (-LOG2E), -jnp.inf)
    return jnp.concatenate([base * slope_ref[hh] for hh in range(hpg)], axis=1)


def _neighbour_specs(block, axis, nq, head_of):
    def spec(off):
        def index(g, i):
            idx = [head_of(g), 0, 0]
            idx[axis] = jnp.clip(i + off, 0, nq - 1)
            return tuple(idx)
        return pl.BlockSpec(block, index)
    return [spec(-1), spec(0), spec(1)]


def window_forward(q, k, vt, sink2, slopes, name):
    hq, s, d = q.shape
    g_kv = k.shape[0]
    hpg = hq // g_kv
    t = WINDOW
    nq = s // t
    m_cols = hpg * t
    c = (d ** -0.5) * LOG2E

    def body(q_ref, kp, ko, kn, vp, vo, vn, sink_ref, slope_ref, o_ref, lse_ref):
        i = pl.program_id(1)
        qq = q_ref[...].reshape(m_cols, d)
        kk = jnp.concatenate([kp[0], ko[0], kn[0]], axis=0)
        st = _mm_nt(kk, qq) * c + _window_bias_t(i, nq, hpg, slope_ref)
        sink_row = jnp.concatenate([jnp.broadcast_to(sink_ref[hh], (8, t)) for hh in range(hpg)], axis=1)
        s3 = st.reshape(3 * t // 8, 8, m_cols)
        m8 = jnp.maximum(_col_max8(s3), sink_row)
        p = jnp.exp2(s3 - m8[None]).reshape(3 * t, m_cols)
        vte = _with_ones(jnp.concatenate([vp[0], vo[0], vn[0]], axis=1), 3 * t)
        acc = _mm(vte, p)
        l = acc[d:d + 1, :] + jnp.exp2(sink_row[0:1, :] - m8[0:1, :])
        ot = acc[0:d, :] / l
        lse = m8[0:1, :] + jnp.log2(l)
        for hh in range(hpg):
            o_ref[:, d * hh:d * hh + d] = ot[:, t * hh:t * hh + t].T
            lse_ref[hh] = lse[:, t * hh:t * hh + t]

    head = lambda g: g
    sd = jax.ShapeDtypeStruct
    return pl.pallas_call(
        body, name=name, grid=(g_kv, nq),
        out_shape=(sd((s, hq * d), F32), sd((hq, 1, s), F32)),
        in_specs=[pl.BlockSpec((hpg, t, d), lambda g, i: (g, i, 0))]
        + _neighbour_specs((1, t, d), 1, nq, head) + _neighbour_specs((1, d, t), 2, nq, head)
        + [pl.BlockSpec((hpg, 1, 1), lambda g, i: (g, 0, 0))] * 2,
        out_specs=(pl.BlockSpec((t, hpg * d), lambda g, i: (i, g)),
                   pl.BlockSpec((hpg, 1, t), lambda g, i: (g, 0, i))),
        compiler_params=_params(("parallel", "parallel")),
    )(q, k, k, k, vt, vt, vt, sink2, slopes)


def window_backward(q, k, kt, v, do, lse, delta, slopes, name):
    hq, s, d = q.shape
    g_kv = k.shape[0]
    hpg = hq // g_kv
    t = WINDOW
    nq = s // t
    m_cols = hpg * t
    scale = d ** -0.5
    c = scale * LOG2E

    def body(q_ref, kp, ko, kn, ktp, kto, ktn, vp, vo, vn, do_ref, lse_ref, dl_ref, slope_ref,
             dq_ref, dk_ref, dv_ref):
        i = pl.program_id(1)

        @pl.when(i == 0)
        def _():
            dk_ref[...] = jnp.zeros(dk_ref.shape, F32)
            dv_ref[...] = jnp.zeros(dv_ref.shape, F32)

        qq = q_ref[...].reshape(m_cols, d)
        kk = jnp.concatenate([kp[0], ko[0], kn[0]], axis=0)
        vv = jnp.concatenate([vp[0], vo[0], vn[0]], axis=0)
        kkt = jnp.concatenate([ktp[0], kto[0], ktn[0]], axis=1)
        dd = jnp.concatenate([do_ref[:, d * hh:d * hh + d] for hh in range(hpg)], axis=0)
        lse_row = jnp.concatenate([lse_ref[hh] for hh in range(hpg)], axis=1)
        dl_row = jnp.concatenate([dl_ref[hh] for hh in range(hpg)], axis=1)
        st = _mm_nt(kk, qq) * c + _window_bias_t(i, nq, hpg, slope_ref)
        p = jnp.exp2(st - lse_row)
        ds = p * (_mm_nt(vv, dd) - dl_row) * scale
        dv_part = _mm(p, dd)
        dk_part = _mm(ds, qq)
        for b in range(3):
            r0 = pl.multiple_of(jnp.clip(i - 1 + b, 0, nq - 1) * t, t)
            dv_ref[0, pl.ds(r0, t), :] += dv_part[t * b:t * b + t, :]
            dk_ref[0, pl.ds(r0, t), :] += dk_part[t * b:t * b + t, :]
        dqt = _mm(kkt, ds)
        for hh in range(hpg):
            dq_ref[:, d * hh:d * hh + d] = dqt[:, t * hh:t * hh + t].T

    head = lambda g: g
    row_map = lambda g, i: (g, 0, i)
    sd = jax.ShapeDtypeStruct
    return pl.pallas_call(
        body, name=name, grid=(g_kv, nq),
        out_shape=(sd((s, hq * d), F32), sd((g_kv, s, d), F32), sd((g_kv, s, d), F32)),
        in_specs=[pl.BlockSpec((hpg, t, d), lambda g, i: (g, i, 0))]
        + _neighbour_specs((1, t, d), 1, nq, head) + _neighbour_specs((1, d, t), 2, nq, head)
        + _neighbour_specs((1, t, d), 1, nq, head)
        + [pl.BlockSpec((t, hpg * d), lambda g, i: (i, g)), pl.BlockSpec((hpg, 1, t), row_map),
           pl.BlockSpec((hpg, 1, t), row_map), pl.BlockSpec((hpg, 1, 1), lambda g, i: (g, 0, 0))],
        out_specs=(pl.BlockSpec((t, hpg * d), lambda g, i: (i, g)),
                   pl.BlockSpec((1, s, d), lambda g, i: (g, 0, 0)),
                   pl.BlockSpec((1, s, d), lambda g, i: (g, 0, 0))),
        compiler_params=_params(("parallel", "arbitrary")),
    )(q, k, k, k, kt, kt, kt, v, v, v, do, lse, delta, slopes)


def flash_backward(q, k, kt, v, do, lse, delta, *, scale, dv, tq, tk, gq, name):
    hq, s, dq = q.shape
    g_kv = k.shape[0]
    hpg = hq // gq
    nq = s // tq
    nkb = s // tk
    m_cols = hpg * tq
    c = scale * LOG2E
    has_v = v is not None

    def body(*refs):
        it = iter(refs)
        q_ref, k_ref, kt_ref = next(it), next(it), next(it)
        v_ref = next(it) if has_v else None
        do_ref, lse_ref, dl_ref = next(it), next(it), next(it)
        dq_ref, dk_ref, dv_ref, dqt_s = next(it), next(it), next(it), next(it)
        kj = pl.program_id(1)
        qi = pl.program_id(2)

        @pl.when((kj == 0) & (qi == 0))
        def _():
            dqt_s[...] = jnp.zeros(dqt_s.shape, F32)

        @pl.when(qi == 0)
        def _():
            dk_ref[...] = jnp.zeros(dk_ref.shape, F32)
            dv_ref[...] = jnp.zeros(dv_ref.shape, F32)

        qq = q_ref[...].reshape(m_cols, dq)
        kk = k_ref[0]
        vv = v_ref[0] if has_v else kk[:, :dv]
        dd = jnp.concatenate([do_ref[:, dv * hh:dv * hh + dv] for hh in range(hpg)], axis=0)
        lse_row = jnp.concatenate([lse_ref[hh] for hh in range(hpg)], axis=1)
        dl_row = jnp.concatenate([dl_ref[hh] for hh in range(hpg)], axis=1)
        p = jnp.exp2(_mm_nt(kk, qq) * c - lse_row)
        ds = p * (_mm_nt(vv, dd) - dl_row) * scale
        dv_ref[0] += _mm(p, dd)
        dk_ref[0] += _mm(ds, qq)
        dqt = _mm(kt_ref[0], ds)
        for hh in range(hpg):
            dqt_s[qi, dq * hh:dq * hh + dq, :] += dqt[:, tq * hh:tq * hh + tq]

        @pl.when((kj == nkb - 1) & (qi == nq - 1))
        def _():
            def emit(t, carry):
                r0 = pl.multiple_of(t * tq, tq)
                for hh in range(hpg):
                    dq_ref[pl.ds(r0, tq), dq * hh:dq * hh + dq] = dqt_s[t, dq * hh:dq * hh + dq, :].T
                return carry

            lax.fori_loop(0, nq, emit, 0)

    kv_of = lambda g: g * g_kv // gq
    in_specs = [pl.BlockSpec((hpg, tq, dq), lambda g, kj, qi: (g, qi, 0)),
                pl.BlockSpec((1, tk, dq), lambda g, kj, qi: (kv_of(g), kj, 0)),
                pl.BlockSpec((1, dq, tk), lambda g, kj, qi: (kv_of(g), 0, kj))]
    args = [q, k, kt]
    if has_v:
        in_specs.append(pl.BlockSpec((1, tk, dv), lambda g, kj, qi: (kv_of(g), kj, 0)))
        args.append(v)
    row_map = lambda g, kj, qi: (g, 0, qi)
    in_specs += [pl.BlockSpec((tq, hpg * dv), lambda g, kj, qi: (qi, g)),
                 pl.BlockSpec((hpg, 1, tq), row_map), pl.BlockSpec((hpg, 1, tq), row_map)]
    args += [do, lse, delta]
    sd = jax.ShapeDtypeStruct
    return pl.pallas_call(
        body, name=name, grid=(gq, nkb, nq),
        out_shape=(sd((s, hq * dq), F32), sd((gq, s, dq), F32), sd((gq, s, dv), F32)),
        in_specs=in_specs,
        out_specs=(pl.BlockSpec((s, hpg * dq), lambda g, kj, jj: (0, g)),
                   pl.BlockSpec((1, tk, dq), lambda g, kj, jj: (g, kj, 0)),
                   pl.BlockSpec((1, tk, dv), lambda g, kj, jj: (g, kj, 0))),
        scratch_shapes=[pltpu.VMEM((nq, hpg * dq, tq), F32)],
        compiler_params=_params(("parallel", "arbitrary", "arbitrary")),
    )(*args)


def loss_head(x, target, fnw):
    s, d = x.shape
    tm = min(ROW_TILE, s)

    def body(x_ref, t_ref, w_ref, lp_ref, dx_ref, dw_ref):
        @pl.when(pl.program_id(0) == 0)
        def _():
            lp_ref[...] = jnp.zeros(lp_ref.shape, F32)
            dw_ref[...] = jnp.zeros(dw_ref.shape, F32)

        x = x_ref[...]
        g = w_ref[...]
        err = x * _rms(x) * g - t_ref[...]
        lp_ref[...] += jnp.sum(err * err, axis=0, keepdims=True)
        dx, dg = _rms_bwd(err * (1.0 / d), x, g)
        dx_ref[...] = dx
        dw_ref[...] += jnp.sum(dg, axis=0, keepdims=True)

    sd = jax.ShapeDtypeStruct
    return pl.pallas_call(
        body, name="loss_head", grid=(s // tm,),
        out_shape=(sd((1, d), F32), sd((s, d), F32), sd((1, d), F32)),
        in_specs=[_row_spec(tm, d), _row_spec(tm, d), _full_spec(fnw.shape)],
        out_specs=(_full_spec((1, d)), _row_spec(tm, d), _full_spec((1, d))),
        compiler_params=_params(("arbitrary",)),
    )(x, target, fnw)


def mixer_out_backward(dx, y, mod, pairs, w_out_t, delta_heads, name, lse=None, sink=None):
    s, d = dx.shape
    tm = min(ROW_TILE, s)
    n = len(pairs)
    widths = [o.shape[1] for o, _ in pairs]
    n_delta = sum(1 for h in delta_heads if h)
    with_sink = lse is not None

    def body(*refs):
        it = iter(refs)
        dx_ref, y_ref, mod_ref, wt_ref = next(it), next(it), next(it), next(it)
        pr = [next(it) for _ in range(2 * n)]
        lse_ref = next(it) if with_sink else None
        sink_ref = next(it) if with_sink else None
        outs = [next(it) for _ in range(2 * n)]
        dl_refs = [next(it) for _ in range(n_delta)]
        dgate_ref, dw_ref = next(it), next(it)
        dsink_ref = next(it) if with_sink else None

        @pl.when(pl.program_id(0) == 0)
        def _():
            dgate_ref[...] = jnp.zeros(dgate_ref.shape, F32)
            dw_ref[...] = jnp.zeros(dw_ref.shape, F32)
            if with_sink:
                dsink_ref[...] = jnp.zeros(dsink_ref.shape, F32)

        dxo = dx_ref[...]
        dgate_ref[...] += jnp.sum(dxo * y_ref[...], axis=0, keepdims=True)
        dy = (dxo * mod_ref[2:3, :]).astype(MXU)
        dmix = jnp.dot(dy, wt_ref[...], preferred_element_type=F32)
        r0 = 0
        di = 0
        for i in range(n):
            o = pr[2 * i][...]
            g = pr[2 * i + 1][...]
            dm = dmix[:, r0:r0 + widths[i]]
            sg = _sigmoid(g)
            act = g * sg
            do = dm * act
            outs[2 * i][...] = do.astype(MXU)
            outs[2 * i + 1][...] = (dm * o * (sg * (1.0 + g * (1.0 - sg)))).astype(MXU)
            dw_ref[r0:r0 + widths[i], :] += _mm_tn(o * act, dy)
            if delta_heads[i]:
                dlt = _group_sums_t(do * o, HD)[0:delta_heads[i], :]
                dl_refs[di][...] = dlt
                if with_sink:
                    ps = jnp.exp2(sink_ref[...] - lse_ref[...])
                    dsink_ref[...] += -jnp.sum(ps * dlt, axis=1, keepdims=True)
                di += 1
            r0 += widths[i]

    flat = [a for p in pairs for a in p]
    sd = jax.ShapeDtypeStruct
    in_specs = [_row_spec(tm, d), _row_spec(tm, d), _full_spec(mod.shape), _full_spec(w_out_t.shape)]
    in_specs += [_row_spec(tm, a.shape[1]) for a in flat]
    args = [dx, y, mod, w_out_t] + flat
    if with_sink:
        nh = lse.shape[0]
        in_specs += [_rows_spec(nh, tm), _full_spec(sink.shape)]
        args += [lse, sink]
    out_shape = [sd((s, a.shape[1]), MXU) for a in flat]
    out_specs = [_row_spec(tm, a.shape[1]) for a in flat]
    for h in delta_heads:
        if h:
            out_shape.append(sd((h, s), F32))
            out_specs.append(_rows_spec(h, tm))
    out_shape += [sd((1, d), F32), sd((sum(widths), d), F32)]
    out_specs += [_full_spec((1, d)), _full_spec((sum(widths), d))]
    if with_sink:
        out_shape.append(sd((lse.shape[0], 1), F32))
        out_specs.append(_full_spec((lse.shape[0], 1)))
    return pl.pallas_call(
        body, name=name, grid=(s // tm,), out_shape=tuple(out_shape), in_specs=in_specs, out_specs=tuple(out_specs),
        compiler_params=_params(("arbitrary",)),
    )(*args)


def latent_out_backward(d_ob, o_lat, w_uv):
    s = o_lat.shape[0]
    tm = min(ROW_TILE, s)

    def body(d_ref, o_ref, uv_ref, dol_ref, dl_ref, duv_ref, prod_s):
        @pl.when(pl.program_id(0) == 0)
        def _():
            duv_ref[...] = jnp.zeros(duv_ref.shape, F32)

        for hh in range(B_HEADS):
            dh = d_ref[:, HD * hh:HD * hh + HD]
            ol = o_ref[:, B_KV_LORA * hh:B_KV_LORA * (hh + 1)]
            dol = _mm_nt(dh, uv_ref[hh])
            dol_ref[:, B_KV_LORA * hh:B_KV_LORA * (hh + 1)] = dol.astype(MXU)
            prod_s[:, B_KV_LORA * hh:B_KV_LORA * (hh + 1)] = dol * ol
            duv_ref[hh] += _mm_tn(ol, dh)
        dl_ref[...] = _group_sums_t(prod_s[...], B_KV_LORA)[0:B_HEADS, :]

    sd = jax.ShapeDtypeStruct
    return pl.pallas_call(
        body, name="latent_out_backward", grid=(s // tm,),
        out_shape=(sd(o_lat.shape, MXU), sd((B_HEADS, s), F32), sd(w_uv.shape, F32)),
        in_specs=[_row_spec(tm, d_ob.shape[1]), _row_spec(tm, o_lat.shape[1]), _full_spec(w_uv.shape)],
        out_specs=(_row_spec(tm, o_lat.shape[1]), _rows_spec(B_HEADS, tm), _full_spec(w_uv.shape)),
        scratch_shapes=[pltpu.VMEM((tm, o_lat.shape[1]), F32)],
        compiler_params=_params(("arbitrary",)),
    )(d_ob, o_lat, w_uv)


def even_prep_backward(dqa, dka, dva, dqb, dkb, dvb, qa_raw, ka_raw, cq_raw, ckv_raw,
                       qn, kn, qln, kvln, w_uq, w_uq_t, w_uk, cos_a, sin_a, cos_t, sin_t):
    s = qa_raw.shape[0]
    tm = min(ROW_TILE, s)
    qb_w = B_HEADS * (B_NOPE + B_ROPE)

    def body(dqa_ref, dka_ref, dva_ref, dqb_ref, dkb_ref, dvb_ref, qa_ref, ka_ref, cq_ref, ckv_ref,
             qn_ref, kn_ref, qln_ref, kvln_ref, uq_ref, uqt_ref, uk_ref, ca_ref, sa_ref, ct_ref, st_ref,
             pqa, pka, pva, pcq, pckv, pkr, gqn, gkn, gqln, gkvln, guq, guk, dqb_s):
        @pl.when(pl.program_id(0) == 0)
        def _():
            for r in (gqn, gkn, gqln, gkvln, guq, guk):
                r[...] = jnp.zeros(r.shape, F32)

        ca, sa, ct, st = ca_ref[...], sa_ref[...], ct_ref[...], st_ref[...]
        acc_q = jnp.zeros((1, HD), F32)
        for hh in range(A_HEADS):
            dyn = _rope_t(dqa_ref[:, HD * hh:HD * hh + HD], ca, sa, 32)
            dx, dg = _rms_bwd(dyn, qa_ref[:, HD * hh:HD * hh + HD], qn_ref[...])
            pqa[:, HD * hh:HD * hh + HD] = dx.astype(MXU)
            acc_q = acc_q + jnp.sum(dg, axis=0, keepdims=True)
        gqn[...] += acc_q
        acc_k = jnp.zeros((1, HD), F32)
        for g in range(A_KV):
            dyn = _rope_t(dka_ref[g], ca, sa, 32)
            dx, dg = _rms_bwd(dyn, ka_ref[:, HD * g:HD * g + HD], kn_ref[...])
            pka[:, HD * g:HD * g + HD] = dx.astype(MXU)
            acc_k = acc_k + jnp.sum(dg, axis=0, keepdims=True)
            pva[:, HD * g:HD * g + HD] = dva_ref[g].astype(MXU)
        gkn[...] += acc_k
        cq_raw = cq_ref[...]
        cq_n = cq_raw * _rms(cq_raw) * qln_ref[...]
        qb = _mm(cq_n, uq_ref[...])
        for hh in range(B_HEADS):
            base = (B_NOPE + B_ROPE) * hh
            dlat = dqb_ref[:, B_QK * hh:B_QK * hh + B_KV_LORA]
            dqb_s[:, base:base + B_NOPE] = _mm(dlat, uk_ref[hh])
            guk[hh] += _mm_tn(dlat, qb[:, base:base + B_NOPE])
            dqb_s[:, base + B_NOPE:base + B_NOPE + B_ROPE] = _rope_t(
                dqb_ref[:, B_QK * hh + B_KV_LORA:B_QK * (hh + 1)], ct, st, 32)
        dqb_all = dqb_s[...]
        guq[...] += _mm_tn(cq_n, dqb_all)
        dx, dg = _rms_bwd(_mm(dqb_all, uqt_ref[...]), cq_raw, qln_ref[...])
        pcq[...] = dx.astype(MXU)
        gqln[...] += jnp.sum(dg, axis=0, keepdims=True)
        dkb_sum = dkb_ref[0] + dkb_ref[1]
        dckv = dkb_sum[:, 0:B_KV_LORA] + dvb_ref[0] + dvb_ref[1]
        dx, dg = _rms_bwd(dckv, ckv_ref[...], kvln_ref[...])
        pckv[...] = dx.astype(MXU)
        gkvln[...] += jnp.sum(dg, axis=0, keepdims=True)
        pkr[...] = _rope_t(dkb_sum[:, B_KV_LORA:B_QK], ct, st, 32).astype(MXU)

    sd = jax.ShapeDtypeStruct
    args = [dqa, dka, dva, dqb, dkb, dvb, qa_raw, ka_raw, cq_raw, ckv_raw,
            qn, kn, qln, kvln, w_uq, w_uq_t, w_uk, cos_a, sin_a, cos_t, sin_t]
    in_specs = [_row_spec(tm, 512), _head_spec(A_KV, tm, HD), _head_spec(A_KV, tm, HD),
                _row_spec(tm, B_HEADS * B_QK), _head_spec(2, tm, B_QK), _head_spec(2, tm, B_KV_LORA),
                _row_spec(tm, 512), _row_spec(tm, 128), _row_spec(tm, B_Q_LORA), _row_spec(tm, B_KV_LORA),
                _full_spec(qn.shape), _full_spec(kn.shape), _full_spec(qln.shape), _full_spec(kvln.shape),
                _full_spec(w_uq.shape), _full_spec(w_uq_t.shape), _full_spec(w_uk.shape),
                _row_spec(tm, HD), _row_spec(tm, HD), _row_spec(tm, B_ROPE), _row_spec(tm, B_ROPE)]
    out_shape = (sd((s, 512), MXU), sd((s, 128), MXU), sd((s, 128), MXU), sd((s, B_Q_LORA), MXU),
                 sd((s, B_KV_LORA), MXU), sd((s, B_ROPE), MXU),
                 sd(qn.shape, F32), sd(kn.shape, F32), sd(qln.shape, F32), sd(kvln.shape, F32),
                 sd(w_uq.shape, F32), sd(w_uk.shape, F32))
    out_specs = (_row_spec(tm, 512), _row_spec(tm, 128), _row_spec(tm, 128), _row_spec(tm, B_Q_LORA),
                 _row_spec(tm, B_KV_LORA), _row_spec(tm, B_ROPE),
                 _full_spec(qn.shape), _full_spec(kn.shape), _full_spec(qln.shape), _full_spec(kvln.shape),
                 _full_spec(w_uq.shape), _full_spec(w_uk.shape))
    return pl.pallas_call(
        body, name="even_prep_backward", grid=(s // tm,), out_shape=out_shape, in_specs=in_specs, out_specs=out_specs,
        scratch_shapes=[pltpu.VMEM((tm, qb_w), F32)],
        compiler_params=_params(("arbitrary",)),
    )(*args)


def in_proj_backward(x, mod, nw, dx_out, pieces, w_in_t, name):
    s, d = x.shape
    tm = min(ROW_TILE, s)
    n_cols = w_in_t.shape[0]
    n = len(pieces)
    cols = [c for _, c in pieces]

    def body(*refs):
        x_ref, mod_ref, nw_ref, dxo_ref, wt_ref = refs[:5]
        p_refs = refs[5:5 + n]
        dx_ref, dw_ref, dv_ref, acc_ref = refs[5 + n:]
        i = pl.program_id(0)

        @pl.when(i == 0)
        def _():
            dw_ref[...] = jnp.zeros(dw_ref.shape, F32)
            acc_ref[...] = jnp.zeros(acc_ref.shape, F32)

        xn, g1, h = _modulated(x_ref[...], mod_ref, nw_ref)
        hb = h.astype(MXU)
        dh = jnp.zeros((tm, d), F32)
        for pr, (c0, c1) in zip(p_refs, cols):
            pc = pr[...].astype(MXU)
            dh = dh + jnp.dot(pc, wt_ref[c0:c1, :], preferred_element_type=F32)
            dw_ref[:, c0:c1] += _mm_tn(hb, pc)
        acc_ref[0:1, :] += jnp.sum(dh, axis=0, keepdims=True)
        acc_ref[1:2, :] += jnp.sum(dh * xn, axis=0, keepdims=True)
        dxn = dh * g1
        x = x_ref[...]
        r = _rms(x)
        dx_ref[...] = dxo_ref[...] + r * (dxn - xn * jnp.mean(dxn * xn, axis=-1, keepdims=True))

        @pl.when(i == pl.num_programs(0) - 1)
        def _():
            dg1 = acc_ref[1:2, :]
            dv_ref[0:1, :] = acc_ref[0:1, :]
            dv_ref[1:2, :] = dg1 * nw_ref[...]
            dv_ref[2:3, :] = dg1 * (1.0 + mod_ref[1:2, :])
            dv_ref[3:4, :] = jnp.zeros((1, d), F32)

    arrs = [a for a, _ in pieces]
    sd = jax.ShapeDtypeStruct
    return pl.pallas_call(
        body, name=name, grid=(s // tm,),
        out_shape=(sd((s, d), F32), sd((d, n_cols), F32), sd((4, d), F32)),
        in_specs=[_row_spec(tm, d), _full_spec(mod.shape), _full_spec(nw.shape), _row_spec(tm, d),
                  _full_spec(w_in_t.shape)] + [_row_spec(tm, a.shape[1]) for a in arrs],
        out_specs=(_row_spec(tm, d), _full_spec((d, n_cols)), _full_spec((4, d))),
        scratch_shapes=[pltpu.VMEM((8, d), F32)],
        compiler_params=_params(("arbitrary",)),
    )(x, mod, nw, dx_out, w_in_t, *arrs)


def ada_weight_grad(c_all, dmod_cols):
    d = c_all.shape[1]
    w = dmod_cols.shape[2]

    def body(c_ref, dm_ref, out_ref):
        ca = _silu(c_ref[...])
        for l in range(2):
            out_ref[l] = _mm_tn(ca, dm_ref[l])

    return pl.pallas_call(
        body, name="ada_weight_grad",
        out_shape=jax.ShapeDtypeStruct((2, d, w), F32),
        compiler_params=pltpu.CompilerParams(vmem_limit_bytes=VMEM_LIMIT),
    )(c_all, dmod_cols)


def adamw_rows(g_slots, w, m, v, name):
    n, r, lanes = g_slots.shape
    tr = r
    for cand in (2048, 1752, 1536, 1104, 1024, 552, 512):
        if r % cand == 0 and cand <= r:
            tr = cand
            break
    c1 = 1.0 - ADAM_B1 ** ADAM_STEP
    c2 = 1.0 - ADAM_B2 ** ADAM_STEP

    def body(g_ref, w_ref, m_ref, v_ref, go, do, mo, vo):
        g = g_ref[0]
        for k in range(1, n):
            g = g + g_ref[k]
        m_new = ADAM_B1 * m_ref[...] + (1.0 - ADAM_B1) * g
        v_new = ADAM_B2 * v_ref[...] + (1.0 - ADAM_B2) * (g * g)
        m_hat = m_new / c1
        v_hat = v_new / c2
        go[...] = g
        do[...] = -ADAM_LR * (m_hat / (jnp.sqrt(v_hat) + ADAM_EPS) + ADAM_WD * w_ref[...])
        mo[...] = m_new
        vo[...] = v_new

    row = pl.BlockSpec((tr, lanes), lambda i: (i, 0))
    sd = jax.ShapeDtypeStruct((r, lanes), F32)
    return pl.pallas_call(
        body, name=name, grid=(r // tr,), out_shape=(sd, sd, sd, sd),
        in_specs=[pl.BlockSpec((n, tr, lanes), lambda i: (0, i, 0)), row, row, row],
        out_specs=(row, row, row, row),
        compiler_params=_params(("parallel",)),
    )(g_slots, w, m, v)


def _rope_tables(s):
    def cs(pos, dim):
        inv = ROPE_THETA ** (-jnp.arange(0, dim, 2, dtype=F32) / dim)
        ang = pos.astype(F32)[:, None] * inv[None, :]
        return jnp.cos(ang), jnp.sin(ang)

    rows = s // GRID_W
    row = jnp.repeat(jnp.arange(rows), GRID_W)
    col = jnp.tile(jnp.arange(GRID_W), rows)
    cr, sr = cs(row, HD // 2)
    cc, sc = cs(col, HD // 2)
    ct, st = cs(jnp.arange(s), B_ROPE)
    cos_a = jnp.concatenate([cr, cr, cc, cc], axis=-1)
    sin_a = jnp.concatenate([-sr, sr, -sc, sc], axis=-1)
    return cos_a, sin_a, jnp.concatenate([ct, ct], axis=-1), jnp.concatenate([-st, st], axis=-1)


def _rows128(a):
    return a.reshape(-1, 128)


def _even_cols_to_kernel(w):
    return jnp.concatenate([w[:, :1664], w[:, 1696:], w[:, 1664:1696]], axis=1)


def _even_cols_to_reference(w):
    return jnp.concatenate([w[:, :1664], w[:, 2176:], w[:, 1664:2176]], axis=1)


def _col_blocks(w):
    d, n8 = w.shape
    n = n8 // N_DEV
    return w.reshape(d, N_DEV, n).transpose(1, 0, 2).reshape(N_DEV, d * n // 128, 128)


def _from_col_blocks(p, d):
    n = p.shape[1] * 128 // d
    return p.reshape(N_DEV, d, n).transpose(1, 0, 2).reshape(d, N_DEV * n)


def _pad_rows(flat, rows):
    return jnp.pad(flat, (0, rows * 128 - flat.shape[0])).reshape(rows, 128)


def kernel(x, c, norm_w, ada_w, ada_b, even_w_in, a_q_norm, a_k_norm, b_q_lora_norm, b_kv_lora_norm, b_w_uq, b_w_uk, b_w_uv, even_w_out, odd_w_in, c_sink, odd_w_out, final_norm, loss_target, m_norm_w, m_ada_w, m_ada_b, m_even_w_in, m_a_q_norm, m_a_k_norm, m_b_q_lora_norm, m_b_kv_lora_norm, m_b_w_uq, m_b_w_uk, m_b_w_uv, m_even_w_out, m_odd_w_in, m_c_sink, m_odd_w_out, m_final_norm, v_norm_w, v_ada_w, v_ada_b, v_even_w_in, v_a_q_norm, v_a_k_norm, v_b_q_lora_norm, v_b_kv_lora_norm, v_b_w_uq, v_b_w_uk, v_b_w_uv, v_even_w_out, v_odd_w_in, v_c_sink, v_odd_w_out, v_final_norm):
    s, d = x.shape[1], x.shape[2]
    x0 = x[0]
    target = loss_target[0]
    me_flat = 4 * lax.axis_index("x") + 2 * lax.axis_index("y") + lax.axis_index("c")

    big = [even_w_in, odd_w_in, even_w_out, odd_w_out, b_w_uq]
    big_rows = [w.size // 128 for w in big]
    offs = [0]
    for r in big_rows:
        offs.append(offs[-1] + r)
    r_big = offs[-1]
    pack = jnp.concatenate([_rows128(w) for w in big], axis=0)
    gathered = all_gather_rows(pack, MXU, "gather_weights").reshape(N_DEV, r_big, 128)
    seg = [gathered[:, offs[i]:offs[i + 1]] for i in range(5)]
    w_in_e = _even_cols_to_kernel(_from_col_blocks(seg[0], d))
    w_in_o = _from_col_blocks(seg[1], d)
    w_out_e = seg[2].reshape(1024, d)
    w_out_o = seg[3].reshape(1024, d)
    w_uq = _from_col_blocks(seg[4], B_Q_LORA)
    w_uk = jnp.transpose(b_w_uk[0], (1, 0, 2)).astype(MXU)
    w_uv = jnp.transpose(b_w_uv[0], (1, 0, 2)).astype(MXU)

    wcols = ada_w.shape[2]
    bias_cols = lax.dynamic_slice_in_dim(ada_b.reshape(2, N_DEV, wcols), me_flat, 1, axis=1)
    call, modp = ada_forward(jnp.broadcast_to(c, (8, d)), ada_w, bias_cols)
    c_all = call[:, 0, :]
    mod = jnp.transpose(modp[:, :, 0, :], (1, 0, 2)).reshape(2, 3, d)
    mod_e, mod_o = mod[0], mod[1]
    nw_e, nw_o = norm_w[0:1], norm_w[1:2]

    cos_a, sin_a, cos_t, sin_t = _rope_tables(s)
    slopes = (2.0 ** (-8.0 * jnp.arange(1, C_HEADS + 1, dtype=F32) / C_HEADS)).reshape(C_HEADS, 1, 1)
    sink2 = c_sink.reshape(C_HEADS, 1, 1) * LOG2E

    (qa, ka, va, qb, kb, kat, vat, kbt, qa_raw, ka_raw, cq_raw, ckv_raw, ga, gb) = even_in_forward(
        x0, mod_e, nw_e, w_in_e, a_q_norm, a_k_norm, b_q_lora_norm, b_kv_lora_norm, w_uq, w_uk,
        cos_a, sin_a, cos_t, sin_t)
    tk_dense = min(512, s)
    oa, lse_a = flash_forward(qa, ka, vat, scale=HD ** -0.5, dv=HD, tq=min(256, s), tk=tk_dense, name="attn_a_fwd")
    scale_b = (B_NOPE + B_ROPE) ** -0.5
    o_lat, lse_b = flash_forward(qb, kb, kbt, scale=scale_b, dv=B_KV_LORA, tq=min(128, s), tk=tk_dense,
                                 name="attn_b_fwd")
    ob = latent_out_forward(o_lat, w_uv)
    x1, y_e = mixer_out_forward(x0, mod_e, [(oa, ga), (ob, gb)], w_out_e, "even_out_fwd")

    qc, kc, vc, kct, vct, gc = odd_in_forward(x1, mod_o, nw_o, w_in_o)
    oc, lse_c = window_forward(qc, kc, vct, sink2, slopes, "attn_c_fwd")
    x2, y_o = mixer_out_forward(x1, mod_o, [(oc, gc)], w_out_o, "odd_out_fwd")

    loss_lanes, dx2, d_final = loss_head(x2, target, final_norm.reshape(1, d))
    loss = lax.psum(0.5 * jnp.sum(loss_lanes) / d, MESH_AXES)

    doc, dgc, delta_c, dgate_o, dw_out_o, dsink = mixer_out_backward(
        dx2, y_o, mod_o, [(oc, gc)], w_out_o.T, [C_HEADS], "odd_out_bwd", lse=lse_c.reshape(C_HEADS, s),
        sink=sink2.reshape(C_HEADS, 1))
    rows3 = lambda t: t.reshape(t.shape[0], 1, s)
    dqc, dkc, dvc = window_backward(qc, kc, kct, vc, doc, lse_c, rows3(delta_c), slopes, "attn_c_bwd")
    to_rows = lambda t: jnp.transpose(t, (1, 0, 2)).reshape(s, -1)
    dx1, dw_in_o, dvec_o = in_proj_backward(
        x1, mod_o, nw_o, dx2, [(dqc, O_Q), (to_rows(dkc), O_K), (to_rows(dvc), O_V), (dgc, O_G)], w_in_o.T,
        "odd_in_bwd")

    doa, dga, dob, dgb, delta_a, dgate_e, dw_out_e = mixer_out_backward(
        dx1, y_e, mod_e, [(oa, ga), (ob, gb)], w_out_e.T, [A_HEADS, 0], "even_out_bwd")
    d_olat, delta_b, dw_uv = latent_out_backward(dob, o_lat, w_uv)
    dqb, dkb, dvb = flash_backward(qb, kb, kbt, None, d_olat, lse_b, rows3(delta_b), scale=scale_b, dv=B_KV_LORA,
                                   tq=min(256, s), tk=tk_dense, gq=2, name="attn_b_bwd")
    dqa, dka, dva = flash_backward(qa, ka, kat, va, doa, lse_a, rows3(delta_a), scale=HD ** -0.5, dv=HD,
                                   tq=min(256, s), tk=tk_dense, gq=A_KV, name="attn_a_bwd")
    (pqa, pka, pva, pcq, pckv, pkr, g_qn, g_kn, g_qln, g_kvln, dw_uq, dw_uk) = even_prep_backward(
        dqa, dka, dva, dqb, dkb, dvb, qa_raw, ka_raw, cq_raw, ckv_raw,
        a_q_norm, a_k_norm, b_q_lora_norm, b_kv_lora_norm, w_uq, w_uq.T, w_uk, cos_a, sin_a, cos_t, sin_t)
    dx0, dw_in_e, dvec_e = in_proj_backward(
        x0, mod_e, nw_e, dx1,
        [(pqa, E_QA), (pka, E_KA), (pva, E_VA), (dga, E_GA), (pcq, E_CQ), (pckv, E_CKV), (dgb, E_GB), (pkr, E_KR)],
        w_in_e.T, "even_in_bwd")

    dmod = jnp.stack([jnp.concatenate([dvec_e[0], dvec_e[1], dgate_e[0]]),
                      jnp.concatenate([dvec_o[0], dvec_o[1], dgate_o[0]])])
    d_norm_w = jnp.stack([dvec_e[2], dvec_o[2]])
    small_names = ["norm_w", "ada_b", "a_q_norm", "a_k_norm", "b_q_lora_norm", "b_kv_lora_norm", "b_w_uk", "b_w_uv",
                   "c_sink", "final_norm"]
    small_w = [norm_w, ada_b, a_q_norm, a_k_norm, b_q_lora_norm, b_kv_lora_norm, b_w_uk, b_w_uv, c_sink, final_norm]
    small_m = [m_norm_w, m_ada_b, m_a_q_norm, m_a_k_norm, m_b_q_lora_norm, m_b_kv_lora_norm, m_b_w_uk, m_b_w_uv,
               m_c_sink, m_final_norm]
    small_v = [v_norm_w, v_ada_b, v_a_q_norm, v_a_k_norm, v_b_q_lora_norm, v_b_kv_lora_norm, v_b_w_uk, v_b_w_uv,
               v_c_sink, v_final_norm]
    small_g = [d_norm_w, dmod, g_qn, g_kn, g_qln, g_kvln, jnp.transpose(dw_uk, (1, 0, 2)), jnp.transpose(dw_uv, (1, 0, 2)),
               dsink, d_final]
    sizes = [w.size for w in small_w]
    n_small = sum(sizes)
    r_small = -(-n_small // (128 * 8)) * 8
    flat_pack = lambda arrs: _pad_rows(jnp.concatenate([a.reshape(-1) for a in arrs]), r_small)
    g_small_all = all_gather_rows(flat_pack(small_g), F32, "gather_small_grads").reshape(N_DEV, r_small, 128)
    sm = adamw_rows(g_small_all, flat_pack(small_w), flat_pack(small_m), flat_pack(small_v), "adamw_small")

    def unpack_small(packed):
        flat = packed.reshape(-1)
        out, o = {}, 0
        for nm, w, sz in zip(small_names, small_w, sizes):
            out[nm] = flat[o:o + sz].reshape(w.shape)
            o += sz
        return out

    sm = [unpack_small(p) for p in sm]

    dmod_all = g_small_all.reshape(N_DEV, -1)[:, sizes[0]:sizes[0] + sizes[1]].reshape(N_DEV, 2, N_DEV, wcols)
    dmod_cols = lax.dynamic_slice_in_dim(dmod_all, me_flat, 1, axis=2)[:, :, 0, :]
    pad16 = lambda a: jnp.concatenate([a, jnp.zeros_like(a)], axis=0)
    g_ada_w = ada_weight_grad(pad16(c_all), jnp.transpose(pad16(dmod_cols), (1, 0, 2)))
    ada = adamw_rows(_rows128(g_ada_w)[None], _rows128(ada_w), _rows128(m_ada_w), _rows128(v_ada_w), "adamw_ada_w")
    ada = [p.reshape(ada_w.shape) for p in ada]

    g_blocks = jnp.concatenate([
        _col_blocks(_even_cols_to_reference(dw_in_e)), _col_blocks(dw_in_o),
        dw_out_e.reshape(N_DEV, -1, 128), dw_out_o.reshape(N_DEV, -1, 128), _col_blocks(dw_uq)], axis=1)
    landed = scatter_blocks(g_blocks, "scatter_weight_grads")
    big_m = [m_even_w_in, m_odd_w_in, m_even_w_out, m_odd_w_out, m_b_w_uq]
    big_v = [v_even_w_in, v_odd_w_in, v_even_w_out, v_odd_w_out, v_b_w_uq]
    cat = lambda arrs: jnp.concatenate([_rows128(a) for a in arrs], axis=0)
    bg = adamw_rows(landed, pack, cat(big_m), cat(big_v), "adamw_big")
    big_names = ["even_w_in", "odd_w_in", "even_w_out", "odd_w_out", "b_w_uq"]
    bg = [{nm: p[offs[i]:offs[i + 1]].reshape(w.shape) for i, (nm, w) in enumerate(zip(big_names, big))} for p in bg]

    order = ["norm_w", "ada_w", "ada_b", "even_w_in", "a_q_norm", "a_k_norm", "b_q_lora_norm", "b_kv_lora_norm",
             "b_w_uq", "b_w_uk", "b_w_uv", "even_w_out", "odd_w_in", "c_sink", "odd_w_out", "final_norm"]

    def pick(kind):
        out = []
        for nm in order:
            if nm == "ada_w":
                out.append(ada[kind])
            elif nm in big_names:
                out.append(bg[kind][nm])
            else:
                out.append(sm[kind][nm])
        return out

    return (loss, dx0[None], *pick(0), *pick(1), *pick(2), *pick(3))
```

```python
import functools

import jax
import jax.numpy as jnp
from jax import lax
from jax.experimental import pallas as pl
from jax.experimental.pallas import tpu as pltpu

F32 = jnp.float32
MXU = jnp.bfloat16
EPS = 1e-6
ROPE_THETA = 10000.0
GRID_W = 64
HD = 64
N_DEV = 8
MESH_AXES = ("x", "y", "c")

A_HEADS, A_KV = 8, 2
B_HEADS, B_NOPE, B_ROPE, B_Q_LORA, B_KV_LORA = 8, 64, 32, 256, 128
B_QK = B_KV_LORA + B_ROPE
C_HEADS, C_KV = 16, 4
WINDOW = 128

ADAM_LR, ADAM_B1, ADAM_B2, ADAM_EPS, ADAM_WD, ADAM_STEP = 0.001, 0.9, 0.999, 1e-08, 0.01, 10

ROW_TILE = 256
VMEM_LIMIT = 56 * 1024 * 1024

E_QA, E_KA, E_VA, E_GA, E_CQ, E_CKV, E_GB, E_KR = (
    (0, 512), (512, 640), (640, 768), (768, 1280), (1280, 1536), (1536, 1664), (1664, 2176), (2176, 2208))
EVEN_IN = 2208
O_Q, O_K, O_V, O_G = (0, 1024), (1024, 1280), (1280, 1536), (1536, 2560)
ODD_IN = 2560


def _mm(a, b):
    return jnp.dot(a.astype(MXU), b.astype(MXU), preferred_element_type=F32)


def _mm_nt(a, b):
    return lax.dot_general(a.astype(MXU), b.astype(MXU), (((1,), (1,)), ((), ())), preferred_element_type=F32)


def _mm_tn(a, b):
    return lax.dot_general(a.astype(MXU), b.astype(MXU), (((0,), (0,)), ((), ())), preferred_element_type=F32)


def _group_sums_t(prod, group):
    tm, w = prod.shape
    sel = (lax.broadcasted_iota(jnp.int32, (w, 128), 0) // group
           == lax.broadcasted_iota(jnp.int32, (w, 128), 1)).astype(MXU)
    hi = prod.astype(MXU)
    lo = prod - hi.astype(F32)
    return (_mm(hi, sel) + _mm(lo, sel)).T


def _sigmoid(z):
    return 1.0 / (1.0 + jnp.exp(-z))


def _silu(z):
    return z * _sigmoid(z)


def _rms(x):
    return lax.rsqrt(jnp.mean(x * x, axis=-1, keepdims=True) + EPS)


def _swap_halves(y, group):
    n = y.shape[-1]
    half = group // 2
    fwd = pltpu.roll(y, half, 1)
    if n == group:
        return fwd
    back = pltpu.roll(y, n - half, 1)
    lane = lax.broadcasted_iota(jnp.int32, y.shape, 1)
    return jnp.where((lane % group) < half, back, fwd)


def _rope(y, cos, sin, group):
    return y * cos + _swap_halves(y, group) * sin


def _rope_t(d, cos, sin, group):
    return d * cos - _swap_halves(d, group) * sin


def _rms_bwd(dy, x, g):
    r = _rms(x)
    xhat = x * r
    dxhat = dy * g
    dx = r * (dxhat - xhat * jnp.mean(dxhat * xhat, axis=-1, keepdims=True))
    return dx, dy * xhat


def _params(sem, vmem=VMEM_LIMIT):
    return pltpu.CompilerParams(dimension_semantics=sem, vmem_limit_bytes=vmem)


def _row_spec(tm, w):
    return pl.BlockSpec((tm, w), lambda i: (i, 0))


def _full_spec(shape):
    nd = len(shape)
    return pl.BlockSpec(shape, lambda i: (0,) * nd)


def _head_spec(h, tm, w):
    return pl.BlockSpec((h, tm, w), lambda i: (0, i, 0))


def _headt_spec(h, w, tm):
    return pl.BlockSpec((h, w, tm), lambda i: (0, 0, i))


def _rows_spec(h, tm):
    return pl.BlockSpec((h, tm), lambda i: (0, i))


def _me():
    return lax.axis_index("x"), lax.axis_index("y"), lax.axis_index("c")


def _flat(p):
    return 4 * p[0] + 2 * p[1] + p[2]


def _peer(me, k):
    x, y, c = me
    return (1 - x if k & 4 else x, 1 - y if k & 2 else y, 1 - c if k & 1 else c)


MESH_ID = pl.DeviceIdType.MESH


def all_gather_rows(x_shard, out_dtype, name):
    r, n = x_shard.shape

    def body(x_ref, out_ref, xs_ref, send_sems, recv_sems, local_sem):
        me = _me()
        x, y, c = me
        sibling = (x, y, 1 - c)
        chips = [(1 - x, y), (x, 1 - y), (1 - x, 1 - y)]
        xs_ref[...] = x_ref[...].astype(out_dtype)

        def rows(p):
            return out_ref.at[pl.ds(pl.multiple_of(_flat(p) * r, 16), r), :]

        def copy(k, block, to, src=None):
            return pltpu.make_async_remote_copy(
                src_ref=rows(block) if src is None else src, dst_ref=rows(block),
                send_sem=send_sems.at[k], recv_sem=recv_sems.at[k], device_id=to, device_id_type=MESH_ID)

        mine = pltpu.make_async_copy(xs_ref, rows(me), local_sem)
        mine.start()
        first = [copy(0, me, sibling, src=xs_ref)]
        first += [copy(1 + j, me, (*chip, c), src=xs_ref) for j, chip in enumerate(chips)]
        for cp in first:
            cp.start()
        passed = [copy(4 + j, (*chip, c), sibling) for j, chip in enumerate(chips)]
        for j, chip in enumerate(chips):
            copy(1 + j, (*chip, c), me).wait_recv()
            passed[j].start()
        copy(0, sibling, me).wait_recv()
        for j, chip in enumerate(chips):
            copy(4 + j, (*chip, 1 - c), me).wait_recv()
        for cp in first + passed:
            cp.wait_send()
        mine.wait()

    return pl.pallas_call(
        body, name=name,
        out_shape=jax.ShapeDtypeStruct((N_DEV * r, n), out_dtype),
        in_specs=[pl.BlockSpec(memory_space=pltpu.VMEM)],
        out_specs=pl.BlockSpec(memory_space=pltpu.VMEM),
        scratch_shapes=[pltpu.VMEM((r, n), out_dtype), pltpu.SemaphoreType.DMA((7,)),
                        pltpu.SemaphoreType.DMA((7,)), pltpu.SemaphoreType.DMA],
        compiler_params=pltpu.CompilerParams(vmem_limit_bytes=VMEM_LIMIT),
    )(x_shard)


class Exchange:
    SEMS = [pltpu.SemaphoreType.DMA((N_DEV - 1,)), pltpu.SemaphoreType.DMA((N_DEV - 1,)), pltpu.SemaphoreType.DMA]
    HBM = pl.BlockSpec(memory_space=pl.ANY)

    def __init__(self, src, scatter):
        self.src = src
        self.scatter = scatter
        rows = src.shape[-2:]
        self.land_shape = jax.ShapeDtypeStruct((N_DEV,) + tuple(rows), src.dtype)

    def _copies(self, src_ref, land_ref, send_sems, recv_sems, local_sem):
        me = _me()
        mi = _flat(me)
        pick = (lambda p: src_ref.at[_flat(p)]) if self.scatter else (lambda p: src_ref)
        local = pltpu.make_async_copy(pick(me), land_ref.at[mi], local_sem)
        sends, recvs = [], []
        for k in range(1, N_DEV):
            peer = _peer(me, k)
            sems = dict(send_sem=send_sems.at[k - 1], recv_sem=recv_sems.at[k - 1], device_id=peer,
                        device_id_type=MESH_ID)
            sends.append(pltpu.make_async_remote_copy(src_ref=pick(peer), dst_ref=land_ref.at[mi], **sems))
            recvs.append(pltpu.make_async_remote_copy(src_ref=pick(peer), dst_ref=land_ref.at[_flat(peer)], **sems))
        return local, sends, recvs

    def start(self, *refs):
        local, sends, _ = self._copies(*refs)
        local.start()
        for cp in sends:
            cp.start()

    def wait(self, *refs):
        local, sends, recvs = self._copies(*refs)
        for cp in recvs:
            cp.wait_recv()
        for cp in sends:
            cp.wait_send()
        local.wait()


def exchange_blocks(ex, name):
    def body(src_ref, land_ref, *sems):
        ex.start(src_ref, land_ref, *sems)
        ex.wait(src_ref, land_ref, *sems)

    return pl.pallas_call(
        body, name=name, out_shape=ex.land_shape, in_specs=[Exchange.HBM], out_specs=Exchange.HBM,
        scratch_shapes=list(Exchange.SEMS),
    )(ex.src)


def ada_forward(c8, ada_w, bias_cols):
    d = c8.shape[1]
    w = ada_w.shape[2]

    def body(c_ref, w_ref, b_ref, call_ref, modp_ref, part_ref, s1, r1, s2, r2):
        me = _me()
        mi = _flat(me)
        call_ref[mi] = c_ref[...]
        gather = []
        for k in range(1, N_DEV):
            gather.append(pltpu.make_async_remote_copy(
                src_ref=c_ref, dst_ref=call_ref.at[mi], send_sem=s1.at[k - 1], recv_sem=r1.at[k - 1],
                device_id=_peer(me, k), device_id_type=MESH_ID))
        for cp in gather:
            cp.start()
        for k in range(1, N_DEV):
            pltpu.make_async_remote_copy(
                src_ref=c_ref, dst_ref=call_ref.at[_flat(_peer(me, k))], send_sem=s1.at[k - 1],
                recv_sem=r1.at[k - 1], device_id=_peer(me, k), device_id_type=MESH_ID).wait_recv()
        ca = _silu(call_ref[...].reshape(N_DEV * 8, d))
        for l in range(2):
            part = _mm(ca, w_ref[l]) + b_ref[l]
            for b in range(N_DEV):
                part_ref[b, l] = part[8 * b:8 * b + 8, :]
        modp_ref[mi] = part_ref[mi]
        spread = []
        for k in range(1, N_DEV):
            peer = _peer(me, k)
            spread.append(pltpu.make_async_remote_copy(
                src_ref=part_ref.at[_flat(peer)], dst_ref=modp_ref.at[mi], send_sem=s2.at[k - 1],
                recv_sem=r2.at[k - 1], device_id=peer, device_id_type=MESH_ID))
        for cp in spread:
            cp.start()
        for k in range(1, N_DEV):
            pi = _flat(_peer(me, k))
            pltpu.make_async_remote_copy(
                src_ref=part_ref.at[pi], dst_ref=modp_ref.at[pi], send_sem=s2.at[k - 1],
                recv_sem=r2.at[k - 1], device_id=_peer(me, k), device_id_type=MESH_ID).wait_recv()
        for cp in gather + spread:
            cp.wait_send()

    vm = pl.BlockSpec(memory_space=pltpu.VMEM)
    return pl.pallas_call(
        body, name="ada_forward",
        out_shape=(jax.ShapeDtypeStruct((N_DEV, 8, d), F32), jax.ShapeDtypeStruct((N_DEV, 2, 8, w), F32)),
        in_specs=[vm, vm, vm], out_specs=(vm, vm),
        scratch_shapes=[pltpu.VMEM((N_DEV, 2, 8, w), F32)] + [pltpu.SemaphoreType.DMA((7,))] * 4,
        compiler_params=pltpu.CompilerParams(vmem_limit_bytes=VMEM_LIMIT),
    )(c8, ada_w, bias_cols)


def _modulated(x, mod_ref, nw_ref):
    xn = x * _rms(x)
    g1 = nw_ref[...] * (1.0 + mod_ref[1:2, :])
    return xn, g1, xn * g1 + mod_ref[0:1, :]


def even_in_forward(x, mod, nw, w_in, qn, kn, qln, kvln, w_uq, w_uk, cos_a, sin_a, cos_t, sin_t):
    s, d = x.shape
    tm = min(ROW_TILE, s)

    def body(x_ref, mod_ref, nw_ref, w_ref, qn_ref, kn_ref, qln_ref, kvln_ref, uq_ref, uk_ref,
             ca_ref, sa_ref, ct_ref, st_ref,
             qa_o, ka_o, va_o, qb_o, kb_o, kat_o, vat_o, kbt_o, qa_raw_o, ka_raw_o, cq_raw_o, ckv_raw_o, ga_o, gb_o):
        _, _, h = _modulated(x_ref[...], mod_ref, nw_ref)
        h = h.astype(MXU)

        def proj(cols):
            return jnp.dot(h, w_ref[:, cols[0]:cols[1]], preferred_element_type=F32)

        ca, sa, ct, st = ca_ref[...], sa_ref[...], ct_ref[...], st_ref[...]
        qa = proj(E_QA)
        qa_raw_o[...] = qa
        for hh in range(A_HEADS):
            xh = qa[:, HD * hh:HD * hh + HD]
            qa_o[hh] = _rope(xh * _rms(xh) * qn_ref[...], ca, sa, 32).astype(MXU)
        ka = proj(E_KA)
        ka_raw_o[...] = ka
        va = proj(E_VA)
        for g in range(A_KV):
            xh = ka[:, HD * g:HD * g + HD]
            kr = _rope(xh * _rms(xh) * kn_ref[...], ca, sa, 32)
            vh = va[:, HD * g:HD * g + HD]
            ka_o[g] = kr.astype(MXU)
            va_o[g] = vh.astype(MXU)
            kat_o[g] = kr.T.astype(MXU)
            vat_o[g] = vh.T.astype(MXU)
        ga_o[...] = proj(E_GA)
        gb_o[...] = proj(E_GB)
        cq = proj(E_CQ)
        cq_raw_o[...] = cq
        qb = _mm(cq * _rms(cq) * qln_ref[...], uq_ref[...])
        for hh in range(B_HEADS):
            base = (B_NOPE + B_ROPE) * hh
            qb_o[hh, :, 0:B_KV_LORA] = _mm_nt(qb[:, base:base + B_NOPE], uk_ref[hh]).astype(MXU)
            qb_o[hh, :, B_KV_LORA:B_QK] = _rope(qb[:, base + B_NOPE:base + B_NOPE + B_ROPE], ct, st, 32).astype(MXU)
        ckv = proj(E_CKV)
        ckv_raw_o[...] = ckv
        ckv_n = ckv * _rms(ckv) * kvln_ref[...]
        k_rope = _rope(proj(E_KR), ct, st, 32)
        kb_o[0, :, 0:B_KV_LORA] = ckv_n.astype(MXU)
        kb_o[0, :, B_KV_LORA:B_QK] = k_rope.astype(MXU)
        kbt_o[0, 0:B_KV_LORA, :] = ckv_n.T.astype(MXU)
        kbt_o[0, B_KV_LORA:B_QK, :] = k_rope.T.astype(MXU)

    sd = jax.ShapeDtypeStruct
    outs = (sd((A_HEADS, s, HD), MXU), sd((A_KV, s, HD), MXU), sd((A_KV, s, HD), MXU),
            sd((B_HEADS, s, B_QK), MXU), sd((1, s, B_QK), MXU),
            sd((A_KV, HD, s), MXU), sd((A_KV, HD, s), MXU), sd((1, B_QK, s), MXU),
            sd((s, 512), F32), sd((s, 128), F32), sd((s, B_Q_LORA), F32), sd((s, B_KV_LORA), F32),
            sd((s, 512), F32), sd((s, 512), F32))
    out_specs = (_head_spec(A_HEADS, tm, HD), _head_spec(A_KV, tm, HD), _head_spec(A_KV, tm, HD),
                 _head_spec(B_HEADS, tm, B_QK), _head_spec(1, tm, B_QK),
                 _headt_spec(A_KV, HD, tm), _headt_spec(A_KV, HD, tm), _headt_spec(1, B_QK, tm),
                 _row_spec(tm, 512), _row_spec(tm, 128), _row_spec(tm, B_Q_LORA), _row_spec(tm, B_KV_LORA),
                 _row_spec(tm, 512), _row_spec(tm, 512))
    in_specs = [_row_spec(tm, d), _full_spec(mod.shape), _full_spec(nw.shape), _full_spec(w_in.shape),
                _full_spec(qn.shape), _full_spec(kn.shape), _full_spec(qln.shape), _full_spec(kvln.shape),
                _full_spec(w_uq.shape), _full_spec(w_uk.shape),
                _row_spec(tm, HD), _row_spec(tm, HD), _row_spec(tm, B_ROPE), _row_spec(tm, B_ROPE)]
    return pl.pallas_call(
        body, name="even_in_forward", grid=(s // tm,), out_shape=outs, in_specs=in_specs, out_specs=out_specs,
        compiler_params=_params(("parallel",)),
    )(x, mod, nw, w_in, qn, kn, qln, kvln, w_uq, w_uk, cos_a, sin_a, cos_t, sin_t)


def odd_in_forward(x, mod, nw, w_in):
    s, d = x.shape
    tm = min(ROW_TILE, s)

    def body(x_ref, mod_ref, nw_ref, w_ref, q_o, k_o, v_o, kt_o, vt_o, g_o):
        _, _, h = _modulated(x_ref[...], mod_ref, nw_ref)
        h = h.astype(MXU)

        def proj(cols):
            return jnp.dot(h, w_ref[:, cols[0]:cols[1]], preferred_element_type=F32)

        q = proj(O_Q)
        for hh in range(C_HEADS):
            q_o[hh] = q[:, HD * hh:HD * hh + HD].astype(MXU)
        k = proj(O_K)
        v = proj(O_V)
        for g in range(C_KV):
            kh = k[:, HD * g:HD * g + HD]
            vh = v[:, HD * g:HD * g + HD]
            k_o[g] = kh.astype(MXU)
            v_o[g] = vh.astype(MXU)
            kt_o[g] = kh.T.astype(MXU)
            vt_o[g] = vh.T.astype(MXU)
        g_o[...] = proj(O_G)

    sd = jax.ShapeDtypeStruct
    return pl.pallas_call(
        body, name="odd_in_forward", grid=(s // tm,),
        out_shape=(sd((C_HEADS, s, HD), MXU), sd((C_KV, s, HD), MXU), sd((C_KV, s, HD), MXU),
                   sd((C_KV, HD, s), MXU), sd((C_KV, HD, s), MXU), sd((s, 1024), F32)),
        in_specs=[_row_spec(tm, d), _full_spec(mod.shape), _full_spec(nw.shape), _full_spec(w_in.shape)],
        out_specs=(_head_spec(C_HEADS, tm, HD), _head_spec(C_KV, tm, HD), _head_spec(C_KV, tm, HD),
                   _headt_spec(C_KV, HD, tm), _headt_spec(C_KV, HD, tm), _row_spec(tm, 1024)),
        compiler_params=_params(("parallel",)),
    )(x, mod, nw, w_in)


def latent_out_forward(o_lat, w_uv):
    s = o_lat.shape[0]
    tm = min(ROW_TILE, s)

    def body(o_ref, uv_ref, out_ref):
        for hh in range(B_HEADS):
            out_ref[:, HD * hh:HD * hh + HD] = _mm(o_ref[:, B_KV_LORA * hh:B_KV_LORA * (hh + 1)], uv_ref[hh])

    return pl.pallas_call(
        body, name="latent_out_forward", grid=(s // tm,),
        out_shape=jax.ShapeDtypeStruct((s, B_HEADS * HD), F32),
        in_specs=[_row_spec(tm, o_lat.shape[1]), _full_spec(w_uv.shape)],
        out_specs=_row_spec(tm, B_HEADS * HD),
        compiler_params=_params(("parallel",)),
    )(o_lat, w_uv)


def mixer_out_forward(x, mod, pairs, w_out, name):
    s, d = x.shape
    tm = min(ROW_TILE, s)
    n = len(pairs)
    widths = [o.shape[1] for o, _ in pairs]

    def body(*refs):
        x_ref, mod_ref, w_ref = refs[:3]
        pr = refs[3:3 + 2 * n]
        xo_ref, y_ref = refs[3 + 2 * n:]
        y = jnp.zeros((tm, d), F32)
        r0 = 0
        for i in range(n):
            mix = pr[2 * i][...] * _silu(pr[2 * i + 1][...])
            y = y + _mm(mix, w_ref[r0:r0 + widths[i], :])
            r0 += widths[i]
        y_ref[...] = y
        xo_ref[...] = x_ref[...] + mod_ref[2:3, :] * y

    flat = [a for p in pairs for a in p]
    sd = jax.ShapeDtypeStruct
    return pl.pallas_call(
        body, name=name, grid=(s // tm,),
        out_shape=(sd((s, d), F32), sd((s, d), F32)),
        in_specs=[_row_spec(tm, d), _full_spec(mod.shape), _full_spec(w_out.shape)]
        + [_row_spec(tm, a.shape[1]) for a in flat],
        out_specs=(_row_spec(tm, d), _row_spec(tm, d)),
        compiler_params=_params(("parallel",)),
    )(x, mod, w_out, *flat)


LOG2E = 1.4426950408889634
ONES_ROWS = 16


def _col_max8(s3):
    m8 = jnp.max(s3, axis=0)
    return jnp.broadcast_to(jnp.max(m8, axis=0, keepdims=True), m8.shape)


def _with_ones(vt, n):
    return jnp.concatenate([vt, jnp.ones((ONES_ROWS, n), vt.dtype)], axis=0)


def _grid_edges(grid):
    ids = [pl.program_id(a) for a in range(len(grid))]
    first = functools.reduce(jnp.logical_and, [i == 0 for i in ids])
    last = functools.reduce(jnp.logical_and, [i == n - 1 for i, n in zip(ids, grid)])
    return first, last


def flash_forward(q, k, vt, *, scale, dv, tq, tk, name, exchange=None):
    hq, s, dq = q.shape
    g_kv = k.shape[0]
    hpg = hq // g_kv
    nq = s // tq
    nk = s // tk
    grid = (g_kv, nq, nk)
    m_cols = hpg * tq
    c = scale * LOG2E
    dvp = dv + ONES_ROWS

    def body(*refs):
        if exchange is None:
            q_ref, k_ref, vt_ref, o_ref, lse_ref, m_s, acc_s = refs
        else:
            q_ref, k_ref, vt_ref, xs_ref, o_ref, lse_ref, land_ref, m_s, acc_s, *sems = refs
            first, last = _grid_edges(grid)
            pl.when(first)(lambda: exchange.start(xs_ref, land_ref, *sems))
        j = pl.program_id(2)

        @pl.when(j == 0)
        def _():
            m_s[...] = jnp.full((8, m_cols), -jnp.inf, F32)
            acc_s[...] = jnp.zeros((dvp, m_cols), F32)

        qq = q_ref[...].reshape(m_cols, dq)
        s3 = _mm_nt(k_ref[0], qq).reshape(tk // 8, 8, m_cols)
        m_old = m_s[...]
        m_new = jnp.maximum(m_old, _col_max8(s3) * c)
        p = jnp.exp2(s3 * c - m_new[None])
        alpha = jnp.exp2(m_old - m_new)
        pv = _mm(_with_ones(vt_ref[0], tk), p.reshape(tk, m_cols))
        acc_s[...] = (acc_s[...].reshape(dvp // 8, 8, m_cols) * alpha[None]).reshape(dvp, m_cols) + pv
        m_s[...] = m_new

        @pl.when(j == nk - 1)
        def _():
            l = acc_s[dv:dv + 1, :]
            ot = acc_s[0:dv, :] / l
            lse = m_s[0:1, :] + jnp.log2(l)
            for hh in range(hpg):
                o_ref[:, dv * hh:dv * hh + dv] = ot[:, tq * hh:tq * hh + tq].T
                lse_ref[hh] = lse[:, tq * hh:tq * hh + tq]

        if exchange is not None:
            pl.when(last)(lambda: exchange.wait(xs_ref, land_ref, *sems))

    sd = jax.ShapeDtypeStruct
    hosted = exchange is not None
    return pl.pallas_call(
        body, name=name, grid=grid,
        out_shape=(sd((s, hq * dv), F32), sd((hq, 1, s), F32)) + ((exchange.land_shape,) if hosted else ()),
        in_specs=[pl.BlockSpec((hpg, tq, dq), lambda g, i, j: (g, i, 0)),
                  pl.BlockSpec((1, tk, k.shape[2]), lambda g, i, j: (g, j, 0)),
                  pl.BlockSpec((1, dv, tk), lambda g, i, j: (g, 0, j))] + ([Exchange.HBM] if hosted else []),
        out_specs=(pl.BlockSpec((tq, hpg * dv), lambda g, i, j: (i, g)),
                   pl.BlockSpec((hpg, 1, tq), lambda g, i, j: (g, 0, i))) + ((Exchange.HBM,) if hosted else ()),
        scratch_shapes=[pltpu.VMEM((8, m_cols), F32), pltpu.VMEM((dvp, m_cols), F32)]
        + (list(Exchange.SEMS) if hosted else []),
        compiler_params=_params(("arbitrary",) * 3 if hosted else ("parallel", "parallel", "arbitrary")),
    )(q, k, vt, *([exchange.src] if hosted else []))


def _window_bias_t(i, nq, hpg, slope_ref):
    t = WINDOW
    r = lax.broadcasted_iota(jnp.int32, (3 * t, t), 0)
    cq = lax.broadcasted_iota(jnp.int32, (3 * t, t), 1)
    arel = jnp.abs(r - t - cq)
    ok = (arel <= WINDOW) & ((r >= t) | (i > 0)) & ((r < 2 * t) | (i < nq - 1))
    base = jnp.where(ok, arel.astype(F32) * (-LOG2E), -jnp.inf)
    return jnp.concatenate([base * slope_ref[hh] for hh in range(hpg)], axis=1)


def _neighbour_specs(block, axis, nq, head_of):
    def spec(off):
        def index(g, i):
            idx = [head_of(g), 0, 0]
            idx[axis] = jnp.clip(i + off, 0, nq - 1)
            return tuple(idx)
        return pl.BlockSpec(block, index)
    return [spec(-1), spec(0), spec(1)]


def window_forward(q, k, vt, sink2, slopes, name):
    hq, s, d = q.shape
    g_kv = k.shape[0]
    hpg = hq // g_kv
    t = WINDOW
    nq = s // t
    m_cols = hpg * t
    c = (d ** -0.5) * LOG2E

    def body(q_ref, kp, ko, kn, vp, vo, vn, sink_ref, slope_ref, o_ref, lse_ref):
        i = pl.program_id(1)
        qq = q_ref[...].reshape(m_cols, d)
        kk = jnp.concatenate([kp[0], ko[0], kn[0]], axis=0)
        st = _mm_nt(kk, qq) * c + _window_bias_t(i, nq, hpg, slope_ref)
        sink_row = jnp.concatenate([jnp.broadcast_to(sink_ref[hh], (8, t)) for hh in range(hpg)], axis=1)
        s3 = st.reshape(3 * t // 8, 8, m_cols)
        m8 = jnp.maximum(_col_max8(s3), sink_row)
        p = jnp.exp2(s3 - m8[None]).reshape(3 * t, m_cols)
        vte = _with_ones(jnp.concatenate([vp[0], vo[0], vn[0]], axis=1), 3 * t)
        acc = _mm(vte, p)
        l = acc[d:d + 1, :] + jnp.exp2(sink_row[0:1, :] - m8[0:1, :])
        ot = acc[0:d, :] / l
        lse = m8[0:1, :] + jnp.log2(l)
        for hh in range(hpg):
            o_ref[:, d * hh:d * hh + d] = ot[:, t * hh:t * hh + t].T
            lse_ref[hh] = lse[:, t * hh:t * hh + t]

    head = lambda g: g
    sd = jax.ShapeDtypeStruct
    return pl.pallas_call(
        body, name=name, grid=(g_kv, nq),
        out_shape=(sd((s, hq * d), F32), sd((hq, 1, s), F32)),
        in_specs=[pl.BlockSpec((hpg, t, d), lambda g, i: (g, i, 0))]
        + _neighbour_specs((1, t, d), 1, nq, head) + _neighbour_specs((1, d, t), 2, nq, head)
        + [pl.BlockSpec((hpg, 1, 1), lambda g, i: (g, 0, 0))] * 2,
        out_specs=(pl.BlockSpec((t, hpg * d), lambda g, i: (i, g)),
                   pl.BlockSpec((hpg, 1, t), lambda g, i: (g, 0, i))),
        compiler_params=_params(("parallel", "parallel")),
    )(q, k, k, k, vt, vt, vt, sink2, slopes)


def window_backward(q, k, kt, v, do, lse, delta, slopes, name):
    hq, s, d = q.shape
    g_kv = k.shape[0]
    hpg = hq // g_kv
    t = WINDOW
    nq = s // t
    m_cols = hpg * t
    scale = d ** -0.5
    c = scale * LOG2E

    def body(q_ref, kp, ko, kn, ktp, kto, ktn, vp, vo, vn, do_ref, lse_ref, dl_ref, slope_ref,
             dq_ref, dk_ref, dv_ref):
        i = pl.program_id(1)

        @pl.when(i == 0)
        def _():
            dk_ref[...] = jnp.zeros(dk_ref.shape, F32)
            dv_ref[...] = jnp.zeros(dv_ref.shape, F32)

        qq = q_ref[...].reshape(m_cols, d)
        kk = jnp.concatenate([kp[0], ko[0], kn[0]], axis=0)
        vv = jnp.concatenate([vp[0], vo[0], vn[0]], axis=0)
        kkt = jnp.concatenate([ktp[0], kto[0], ktn[0]], axis=1)
        dd = jnp.concatenate([do_ref[:, d * hh:d * hh + d] for hh in range(hpg)], axis=0)
        lse_row = jnp.concatenate([lse_ref[hh] for hh in range(hpg)], axis=1)
        dl_row = jnp.concatenate([dl_ref[hh] for hh in range(hpg)], axis=1)
        st = _mm_nt(kk, qq) * c + _window_bias_t(i, nq, hpg, slope_ref)
        p = jnp.exp2(st - lse_row)
        ds = p * (_mm_nt(vv, dd) - dl_row) * scale
        dv_part = _mm(p, dd)
        dk_part = _mm(ds, qq)
        for b in range(3):
            r0 = pl.multiple_of(jnp.clip(i - 1 + b, 0, nq - 1) * t, t)
            dv_ref[0, pl.ds(r0, t), :] += dv_part[t * b:t * b + t, :]
            dk_ref[0, pl.ds(r0, t), :] += dk_part[t * b:t * b + t, :]
        dqt = _mm(kkt, ds)
        for hh in range(hpg):
            dq_ref[:, d * hh:d * hh + d] = dqt[:, t * hh:t * hh + t].T

    head = lambda g: g
    row_map = lambda g, i: (g, 0, i)
    sd = jax.ShapeDtypeStruct
    return pl.pallas_call(
        body, name=name, grid=(g_kv, nq),
        out_shape=(sd((s, hq * d), F32), sd((g_kv, s, d), F32), sd((g_kv, s, d), F32)),
        in_specs=[pl.BlockSpec((hpg, t, d), lambda g, i: (g, i, 0))]
        + _neighbour_specs((1, t, d), 1, nq, head) + _neighbour_specs((1, d, t), 2, nq, head)
        + _neighbour_specs((1, t, d), 1, nq, head)
        + [pl.BlockSpec((t, hpg * d), lambda g, i: (i, g)), pl.BlockSpec((hpg, 1, t), row_map),
           pl.BlockSpec((hpg, 1, t), row_map), pl.BlockSpec((hpg, 1, 1), lambda g, i: (g, 0, 0))],
        out_specs=(pl.BlockSpec((t, hpg * d), lambda g, i: (i, g)),
                   pl.BlockSpec((1, s, d), lambda g, i: (g, 0, 0)),
                   pl.BlockSpec((1, s, d), lambda g, i: (g, 0, 0))),
        compiler_params=_params(("parallel", "arbitrary")),
    )(q, k, k, k, kt, kt, kt, v, v, v, do, lse, delta, slopes)


def flash_backward(q, k, kt, v, do, lse, delta, *, scale, dv, tq, tk, gq, name, exchange=None):
    hq, s, dq = q.shape
    g_kv = k.shape[0]
    hpg = hq // gq
    nq = s // tq
    nkb = s // tk
    grid = (gq, nkb, nq)
    hosted = exchange is not None
    m_cols = hpg * tq
    c = scale * LOG2E
    has_v = v is not None

    def body(*refs):
        it = iter(refs)
        q_ref, k_ref, kt_ref = next(it), next(it), next(it)
        v_ref = next(it) if has_v else None
        do_ref, lse_ref, dl_ref = next(it), next(it), next(it)
        xs_ref = next(it) if hosted else None
        dq_ref, dk_ref, dv_ref = next(it), next(it), next(it)
        land_ref = next(it) if hosted else None
        dqt_s = next(it)
        sems = list(it)
        kj = pl.program_id(1)
        qi = pl.program_id(2)
        if hosted:
            first, last = _grid_edges(grid)
            pl.when(first)(lambda: exchange.start(xs_ref, land_ref, *sems))

        @pl.when((kj == 0) & (qi == 0))
        def _():
            dqt_s[...] = jnp.zeros(dqt_s.shape, F32)

        @pl.when(qi == 0)
        def _():
            dk_ref[...] = jnp.zeros(dk_ref.shape, F32)
            dv_ref[...] = jnp.zeros(dv_ref.shape, F32)

        qq = q_ref[...].reshape(m_cols, dq)
        kk = k_ref[0]
        vv = v_ref[0] if has_v else kk[:, :dv]
        dd = jnp.concatenate([do_ref[:, dv * hh:dv * hh + dv] for hh in range(hpg)], axis=0)
        lse_row = jnp.concatenate([lse_ref[hh] for hh in range(hpg)], axis=1)
        dl_row = jnp.concatenate([dl_ref[hh] for hh in range(hpg)], axis=1)
        p = jnp.exp2(_mm_nt(kk, qq) * c - lse_row)
        ds = p * (_mm_nt(vv, dd) - dl_row) * scale
        dv_ref[0] += _mm(p, dd)
        dk_ref[0] += _mm(ds, qq)
        dqt = _mm(kt_ref[0], ds)
        for hh in range(hpg):
            dqt_s[qi, dq * hh:dq * hh + dq, :] += dqt[:, tq * hh:tq * hh + tq]

        @pl.when((kj == nkb - 1) & (qi == nq - 1))
        def _():
            def emit(t, carry):
                r0 = pl.multiple_of(t * tq, tq)
                for hh in range(hpg):
                    dq_ref[pl.ds(r0, tq), dq * hh:dq * hh + dq] = dqt_s[t, dq * hh:dq * hh + dq, :].T
                return carry

            lax.fori_loop(0, nq, emit, 0)

        if hosted:
            pl.when(last)(lambda: exchange.wait(xs_ref, land_ref, *sems))

    kv_of = lambda g: g * g_kv // gq
    in_specs = [pl.BlockSpec((hpg, tq, dq), lambda g, kj, qi: (g, qi, 0)),
                pl.BlockSpec((1, tk, dq), lambda g, kj, qi: (kv_of(g), kj, 0)),
                pl.BlockSpec((1, dq, tk), lambda g, kj, qi: (kv_of(g), 0, kj))]
    args = [q, k, kt]
    if has_v:
        in_specs.append(pl.BlockSpec((1, tk, dv), lambda g, kj, qi: (kv_of(g), kj, 0)))
        args.append(v)
    row_map = lambda g, kj, qi: (g, 0, qi)
    in_specs += [pl.BlockSpec((tq, hpg * dv), lambda g, kj, qi: (qi, g)),
                 pl.BlockSpec((hpg, 1, tq), row_map), pl.BlockSpec((hpg, 1, tq), row_map)]
    args += [do, lse, delta]
    if hosted:
        in_specs.append(Exchange.HBM)
        args.append(exchange.src)
    sd = jax.ShapeDtypeStruct
    return pl.pallas_call(
        body, name=name, grid=grid,
        out_shape=(sd((s, hq * dq), F32), sd((gq, s, dq), F32), sd((gq, s, dv), F32))
        + ((exchange.land_shape,) if hosted else ()),
        in_specs=in_specs,
        out_specs=(pl.BlockSpec((s, hpg * dq), lambda g, kj, qi: (0, g)),
                   pl.BlockSpec((1, tk, dq), lambda g, kj, qi: (g, kj, 0)),
                   pl.BlockSpec((1, tk, dv), lambda g, kj, qi: (g, kj, 0))) + ((Exchange.HBM,) if hosted else ()),
        scratch_shapes=[pltpu.VMEM((nq, hpg * dq, tq), F32)] + (list(Exchange.SEMS) if hosted else []),
        compiler_params=_params(("arbitrary",) * 3 if hosted else ("parallel", "arbitrary", "arbitrary")),
    )(*args)


def loss_head(x, target, fnw):
    s, d = x.shape
    tm = min(ROW_TILE, s)

    def body(x_ref, t_ref, w_ref, lp_ref, dx_ref, dw_ref):
        @pl.when(pl.program_id(0) == 0)
        def _():
            lp_ref[...] = jnp.zeros(lp_ref.shape, F32)
            dw_ref[...] = jnp.zeros(dw_ref.shape, F32)

        x = x_ref[...]
        g = w_ref[...]
        err = x * _rms(x) * g - t_ref[...]
        lp_ref[...] += jnp.sum(err * err, axis=0, keepdims=True)
        dx, dg = _rms_bwd(err * (1.0 / d), x, g)
        dx_ref[...] = dx
        dw_ref[...] += jnp.sum(dg, axis=0, keepdims=True)

    sd = jax.ShapeDtypeStruct
    return pl.pallas_call(
        body, name="loss_head", grid=(s // tm,),
        out_shape=(sd((1, d), F32), sd((s, d), F32), sd((1, d), F32)),
        in_specs=[_row_spec(tm, d), _row_spec(tm, d), _full_spec(fnw.shape)],
        out_specs=(_full_spec((1, d)), _row_spec(tm, d), _full_spec((1, d))),
        compiler_params=_params(("arbitrary",)),
    )(x, target, fnw)


def mixer_out_backward(dx, y, mod, pairs, w_out_t, delta_heads, name, lse=None, sink=None):
    s, d = dx.shape
    tm = min(ROW_TILE, s)
    n = len(pairs)
    widths = [o.shape[1] for o, _ in pairs]
    n_delta = sum(1 for h in delta_heads if h)
    with_sink = lse is not None

    def body(*refs):
        it = iter(refs)
        dx_ref, y_ref, mod_ref, wt_ref = next(it), next(it), next(it), next(it)
        pr = [next(it) for _ in range(2 * n)]
        lse_ref = next(it) if with_sink else None
        sink_ref = next(it) if with_sink else None
        outs = [next(it) for _ in range(2 * n)]
        dl_refs = [next(it) for _ in range(n_delta)]
        dgate_ref, dw_ref = next(it), next(it)
        dsink_ref = next(it) if with_sink else None

        @pl.when(pl.program_id(0) == 0)
        def _():
            dgate_ref[...] = jnp.zeros(dgate_ref.shape, F32)
            dw_ref[...] = jnp.zeros(dw_ref.shape, F32)
            if with_sink:
                dsink_ref[...] = jnp.zeros(dsink_ref.shape, F32)

        dxo = dx_ref[...]
        dgate_ref[...] += jnp.sum(dxo * y_ref[...], axis=0, keepdims=True)
        dy = (dxo * mod_ref[2:3, :]).astype(MXU)
        dmix = jnp.dot(dy, wt_ref[...], preferred_element_type=F32)
        r0 = 0
        di = 0
        for i in range(n):
            o = pr[2 * i][...]
            g = pr[2 * i + 1][...]
            dm = dmix[:, r0:r0 + widths[i]]
            sg = _sigmoid(g)
            act = g * sg
            do = dm * act
            outs[2 * i][...] = do.astype(MXU)
            outs[2 * i + 1][...] = (dm * o * (sg * (1.0 + g * (1.0 - sg)))).astype(MXU)
            dw_ref[r0:r0 + widths[i], :] += _mm_tn(o * act, dy)
            if delta_heads[i]:
                dlt = _group_sums_t(do * o, HD)[0:delta_heads[i], :]
                dl_refs[di][...] = dlt
                if with_sink:
                    ps = jnp.exp2(sink_ref[...] - lse_ref[...])
                    dsink_ref[...] += -jnp.sum(ps * dlt, axis=1, keepdims=True)
                di += 1
            r0 += widths[i]

    flat = [a for p in pairs for a in p]
    sd = jax.ShapeDtypeStruct
    in_specs = [_row_spec(tm, d), _row_spec(tm, d), _full_spec(mod.shape), _full_spec(w_out_t.shape)]
    in_specs += [_row_spec(tm, a.shape[1]) for a in flat]
    args = [dx, y, mod, w_out_t] + flat
    if with_sink:
        nh = lse.shape[0]
        in_specs += [_rows_spec(nh, tm), _full_spec(sink.shape)]
        args += [lse, sink]
    out_shape = [sd((s, a.shape[1]), MXU) for a in flat]
    out_specs = [_row_spec(tm, a.shape[1]) for a in flat]
    for h in delta_heads:
        if h:
            out_shape.append(sd((h, s), F32))
            out_specs.append(_rows_spec(h, tm))
    out_shape += [sd((1, d), F32), sd((sum(widths), d), F32)]
    out_specs += [_full_spec((1, d)), _full_spec((sum(widths), d))]
    if with_sink:
        out_shape.append(sd((lse.shape[0], 1), F32))
        out_specs.append(_full_spec((lse.shape[0], 1)))
    return pl.pallas_call(
        body, name=name, grid=(s // tm,), out_shape=tuple(out_shape), in_specs=in_specs, out_specs=tuple(out_specs),
        compiler_params=_params(("arbitrary",)),
    )(*args)


def latent_out_backward(d_ob, o_lat, w_uv):
    s = o_lat.shape[0]
    tm = min(ROW_TILE, s)

    def body(d_ref, o_ref, uv_ref, dol_ref, dl_ref, duv_ref, prod_s):
        @pl.when(pl.program_id(0) == 0)
        def _():
            duv_ref[...] = jnp.zeros(duv_ref.shape, F32)

        for hh in range(B_HEADS):
            dh = d_ref[:, HD * hh:HD * hh + HD]
            ol = o_ref[:, B_KV_LORA * hh:B_KV_LORA * (hh + 1)]
            dol = _mm_nt(dh, uv_ref[hh])
            dol_ref[:, B_KV_LORA * hh:B_KV_LORA * (hh + 1)] = dol.astype(MXU)
            prod_s[:, B_KV_LORA * hh:B_KV_LORA * (hh + 1)] = dol * ol
            duv_ref[hh] += _mm_tn(ol, dh)
        dl_ref[...] = _group_sums_t(prod_s[...], B_KV_LORA)[0:B_HEADS, :]

    sd = jax.ShapeDtypeStruct
    return pl.pallas_call(
        body, name="latent_out_backward", grid=(s // tm,),
        out_shape=(sd(o_lat.shape, MXU), sd((B_HEADS, s), F32), sd(w_uv.shape, F32)),
        in_specs=[_row_spec(tm, d_ob.shape[1]), _row_spec(tm, o_lat.shape[1]), _full_spec(w_uv.shape)],
        out_specs=(_row_spec(tm, o_lat.shape[1]), _rows_spec(B_HEADS, tm), _full_spec(w_uv.shape)),
        scratch_shapes=[pltpu.VMEM((tm, o_lat.shape[1]), F32)],
        compiler_params=_params(("arbitrary",)),
    )(d_ob, o_lat, w_uv)


def even_prep_backward(dqa, dka, dva, dqb, dkb, dvb, qa_raw, ka_raw, cq_raw, ckv_raw,
                       qn, kn, qln, kvln, w_uq, w_uq_t, w_uk, cos_a, sin_a, cos_t, sin_t):
    s = qa_raw.shape[0]
    tm = min(ROW_TILE, s)
    qb_w = B_HEADS * (B_NOPE + B_ROPE)

    def body(dqa_ref, dka_ref, dva_ref, dqb_ref, dkb_ref, dvb_ref, qa_ref, ka_ref, cq_ref, ckv_ref,
             qn_ref, kn_ref, qln_ref, kvln_ref, uq_ref, uqt_ref, uk_ref, ca_ref, sa_ref, ct_ref, st_ref,
             pqa, pka, pva, pcq, pckv, pkr, gqn, gkn, gqln, gkvln, guq, guk, dqb_s):
        @pl.when(pl.program_id(0) == 0)
        def _():
            for r in (gqn, gkn, gqln, gkvln, guq, guk):
                r[...] = jnp.zeros(r.shape, F32)

        ca, sa, ct, st = ca_ref[...], sa_ref[...], ct_ref[...], st_ref[...]
        acc_q = jnp.zeros((1, HD), F32)
        for hh in range(A_HEADS):
            dyn = _rope_t(dqa_ref[:, HD * hh:HD * hh + HD], ca, sa, 32)
            dx, dg = _rms_bwd(dyn, qa_ref[:, HD * hh:HD * hh + HD], qn_ref[...])
            pqa[:, HD * hh:HD * hh + HD] = dx.astype(MXU)
            acc_q = acc_q + jnp.sum(dg, axis=0, keepdims=True)
        gqn[...] += acc_q
        acc_k = jnp.zeros((1, HD), F32)
        for g in range(A_KV):
            dyn = _rope_t(dka_ref[g], ca, sa, 32)
            dx, dg = _rms_bwd(dyn, ka_ref[:, HD * g:HD * g + HD], kn_ref[...])
            pka[:, HD * g:HD * g + HD] = dx.astype(MXU)
            acc_k = acc_k + jnp.sum(dg, axis=0, keepdims=True)
            pva[:, HD * g:HD * g + HD] = dva_ref[g].astype(MXU)
        gkn[...] += acc_k
        cq_raw = cq_ref[...]
        cq_n = cq_raw * _rms(cq_raw) * qln_ref[...]
        qb = _mm(cq_n, uq_ref[...])
        for hh in range(B_HEADS):
            base = (B_NOPE + B_ROPE) * hh
            dlat = dqb_ref[:, B_QK * hh:B_QK * hh + B_KV_LORA]
            dqb_s[:, base:base + B_NOPE] = _mm(dlat, uk_ref[hh])
            guk[hh] += _mm_tn(dlat, qb[:, base:base + B_NOPE])
            dqb_s[:, base + B_NOPE:base + B_NOPE + B_ROPE] = _rope_t(
                dqb_ref[:, B_QK * hh + B_KV_LORA:B_QK * (hh + 1)], ct, st, 32)
        dqb_all = dqb_s[...]
        guq[...] += _mm_tn(cq_n, dqb_all)
        dx, dg = _rms_bwd(_mm(dqb_all, uqt_ref[...]), cq_raw, qln_ref[...])
        pcq[...] = dx.astype(MXU)
        gqln[...] += jnp.sum(dg, axis=0, keepdims=True)
        dkb_sum = dkb_ref[0] + dkb_ref[1]
        dckv = dkb_sum[:, 0:B_KV_LORA] + dvb_ref[0] + dvb_ref[1]
        dx, dg = _rms_bwd(dckv, ckv_ref[...], kvln_ref[...])
        pckv[...] = dx.astype(MXU)
        gkvln[...] += jnp.sum(dg, axis=0, keepdims=True)
        pkr[...] = _rope_t(dkb_sum[:, B_KV_LORA:B_QK], ct, st, 32).astype(MXU)

    sd = jax.ShapeDtypeStruct
    args = [dqa, dka, dva, dqb, dkb, dvb, qa_raw, ka_raw, cq_raw, ckv_raw,
            qn, kn, qln, kvln, w_uq, w_uq_t, w_uk, cos_a, sin_a, cos_t, sin_t]
    in_specs = [_row_spec(tm, 512), _head_spec(A_KV, tm, HD), _head_spec(A_KV, tm, HD),
                _row_spec(tm, B_HEADS * B_QK), _head_spec(2, tm, B_QK), _head_spec(2, tm, B_KV_LORA),
                _row_spec(tm, 512), _row_spec(tm, 128), _row_spec(tm, B_Q_LORA), _row_spec(tm, B_KV_LORA),
                _full_spec(qn.shape), _full_spec(kn.shape), _full_spec(qln.shape), _full_spec(kvln.shape),
                _full_spec(w_uq.shape), _full_spec(w_uq_t.shape), _full_spec(w_uk.shape),
                _row_spec(tm, HD), _row_spec(tm, HD), _row_spec(tm, B_ROPE), _row_spec(tm, B_ROPE)]
    out_shape = (sd((s, 512), MXU), sd((s, 128), MXU), sd((s, 128), MXU), sd((s, B_Q_LORA), MXU),
                 sd((s, B_KV_LORA), MXU), sd((s, B_ROPE), MXU),
                 sd(qn.shape, F32), sd(kn.shape, F32), sd(qln.shape, F32), sd(kvln.shape, F32),
                 sd(w_uq.shape, F32), sd(w_uk.shape, F32))
    out_specs = (_row_spec(tm, 512), _row_spec(tm, 128), _row_spec(tm, 128), _row_spec(tm, B_Q_LORA),
                 _row_spec(tm, B_KV_LORA), _row_spec(tm, B_ROPE),
                 _full_spec(qn.shape), _full_spec(kn.shape), _full_spec(qln.shape), _full_spec(kvln.shape),
                 _full_spec(w_uq.shape), _full_spec(w_uk.shape))
    return pl.pallas_call(
        body, name="even_prep_backward", grid=(s // tm,), out_shape=out_shape, in_specs=in_specs, out_specs=out_specs,
        scratch_shapes=[pltpu.VMEM((tm, qb_w), F32)],
        compiler_params=_params(("arbitrary",)),
    )(*args)


def in_proj_backward(x, mod, nw, dx_out, pieces, w_in_t, name):
    s, d = x.shape
    tm = min(ROW_TILE, s)
    n_cols = w_in_t.shape[0]
    n = len(pieces)
    cols = [c for _, c in pieces]

    def body(*refs):
        x_ref, mod_ref, nw_ref, dxo_ref, wt_ref = refs[:5]
        p_refs = refs[5:5 + n]
        dx_ref, dw_ref, dv_ref, acc_ref = refs[5 + n:]
        i = pl.program_id(0)

        @pl.when(i == 0)
        def _():
            dw_ref[...] = jnp.zeros(dw_ref.shape, F32)
            acc_ref[...] = jnp.zeros(acc_ref.shape, F32)

        xn, g1, h = _modulated(x_ref[...], mod_ref, nw_ref)
        hb = h.astype(MXU)
        dh = jnp.zeros((tm, d), F32)
        for pr, (c0, c1) in zip(p_refs, cols):
            pc = pr[...].astype(MXU)
            dh = dh + jnp.dot(pc, wt_ref[c0:c1, :], preferred_element_type=F32)
            dw_ref[:, c0:c1] += _mm_tn(hb, pc)
        acc_ref[0:1, :] += jnp.sum(dh, axis=0, keepdims=True)
        acc_ref[1:2, :] += jnp.sum(dh * xn, axis=0, keepdims=True)
        dxn = dh * g1
        x = x_ref[...]
        r = _rms(x)
        dx_ref[...] = dxo_ref[...] + r * (dxn - xn * jnp.mean(dxn * xn, axis=-1, keepdims=True))

        @pl.when(i == pl.num_programs(0) - 1)
        def _():
            dg1 = acc_ref[1:2, :]
            dv_ref[0:1, :] = acc_ref[0:1, :]
            dv_ref[1:2, :] = dg1 * nw_ref[...]
            dv_ref[2:3, :] = dg1 * (1.0 + mod_ref[1:2, :])
            dv_ref[3:4, :] = jnp.zeros((1, d), F32)

    arrs = [a for a, _ in pieces]
    sd = jax.ShapeDtypeStruct
    return pl.pallas_call(
        body, name=name, grid=(s // tm,),
        out_shape=(sd((s, d), F32), sd((d, n_cols), F32), sd((4, d), F32)),
        in_specs=[_row_spec(tm, d), _full_spec(mod.shape), _full_spec(nw.shape), _row_spec(tm, d),
                  _full_spec(w_in_t.shape)] + [_row_spec(tm, a.shape[1]) for a in arrs],
        out_specs=(_row_spec(tm, d), _full_spec((d, n_cols)), _full_spec((4, d))),
        scratch_shapes=[pltpu.VMEM((8, d), F32)],
        compiler_params=_params(("arbitrary",)),
    )(x, mod, nw, dx_out, w_in_t, *arrs)


def ada_weight_grad(c_all, dmod_cols):
    d = c_all.shape[1]
    w = dmod_cols.shape[2]

    def body(c_ref, dm_ref, out_ref):
        ca = _silu(c_ref[...])
        for l in range(2):
            out_ref[l] = _mm_tn(ca, dm_ref[l])

    return pl.pallas_call(
        body, name="ada_weight_grad",
        out_shape=jax.ShapeDtypeStruct((2, d, w), F32),
        compiler_params=pltpu.CompilerParams(vmem_limit_bytes=VMEM_LIMIT),
    )(c_all, dmod_cols)


def adamw_rows(g_slots, w, m, v, name):
    n, r, lanes = g_slots.shape
    tr = max(t for t in range(16, min(r, 2048) + 1, 16) if r % t == 0) if r % 16 == 0 else r
    c1 = 1.0 - ADAM_B1 ** ADAM_STEP
    c2 = 1.0 - ADAM_B2 ** ADAM_STEP

    def body(g_ref, w_ref, m_ref, v_ref, go, do, mo, vo):
        g = g_ref[0].astype(F32)
        for k in range(1, n):
            g = g + g_ref[k].astype(F32)
        m_new = ADAM_B1 * m_ref[...] + (1.0 - ADAM_B1) * g
        v_new = ADAM_B2 * v_ref[...] + (1.0 - ADAM_B2) * (g * g)
        m_hat = m_new / c1
        v_hat = v_new / c2
        go[...] = g
        do[...] = -ADAM_LR * (m_hat / (jnp.sqrt(v_hat) + ADAM_EPS) + ADAM_WD * w_ref[...])
        mo[...] = m_new
        vo[...] = v_new

    row = pl.BlockSpec((tr, lanes), lambda i: (i, 0))
    sd = jax.ShapeDtypeStruct((r, lanes), F32)
    return pl.pallas_call(
        body, name=name, grid=(r // tr,), out_shape=(sd, sd, sd, sd),
        in_specs=[pl.BlockSpec((n, tr, lanes), lambda i: (0, i, 0)), row, row, row],
        out_specs=(row, row, row, row),
        compiler_params=_params(("parallel",)),
    )(g_slots, w, m, v)


def _rope_tables(s):
    def cs(pos, dim):
        inv = ROPE_THETA ** (-jnp.arange(0, dim, 2, dtype=F32) / dim)
        ang = pos.astype(F32)[:, None] * inv[None, :]
        return jnp.cos(ang), jnp.sin(ang)

    rows = s // GRID_W
    row = jnp.repeat(jnp.arange(rows), GRID_W)
    col = jnp.tile(jnp.arange(GRID_W), rows)
    cr, sr = cs(row, HD // 2)
    cc, sc = cs(col, HD // 2)
    ct, st = cs(jnp.arange(s), B_ROPE)
    cos_a = jnp.concatenate([cr, cr, cc, cc], axis=-1)
    sin_a = jnp.concatenate([-sr, sr, -sc, sc], axis=-1)
    return cos_a, sin_a, jnp.concatenate([ct, ct], axis=-1), jnp.concatenate([-st, st], axis=-1)


def _rows128(a):
    return a.reshape(-1, 128)


def _even_cols_to_kernel(w):
    return jnp.concatenate([w[:, :1664], w[:, 1696:], w[:, 1664:1696]], axis=1)


def _even_cols_to_reference(w):
    return jnp.concatenate([w[:, :1664], w[:, 2176:], w[:, 1664:2176]], axis=1)


def _col_blocks(w):
    d, n8 = w.shape
    n = n8 // N_DEV
    return w.reshape(d, N_DEV, n).transpose(1, 0, 2).reshape(N_DEV, d * n // 128, 128)


def _from_col_blocks(p, d):
    n = p.shape[1] * 128 // d
    return p.reshape(N_DEV, d, n).transpose(1, 0, 2).reshape(d, N_DEV * n)


def _pad_rows(flat, rows):
    return jnp.pad(flat, (0, rows * 128 - flat.shape[0])).reshape(rows, 128)


def kernel(x, c, norm_w, ada_w, ada_b, even_w_in, a_q_norm, a_k_norm, b_q_lora_norm, b_kv_lora_norm, b_w_uq, b_w_uk, b_w_uv, even_w_out, odd_w_in, c_sink, odd_w_out, final_norm, loss_target, m_norm_w, m_ada_w, m_ada_b, m_even_w_in, m_a_q_norm, m_a_k_norm, m_b_q_lora_norm, m_b_kv_lora_norm, m_b_w_uq, m_b_w_uk, m_b_w_uv, m_even_w_out, m_odd_w_in, m_c_sink, m_odd_w_out, m_final_norm, v_norm_w, v_ada_w, v_ada_b, v_even_w_in, v_a_q_norm, v_a_k_norm, v_b_q_lora_norm, v_b_kv_lora_norm, v_b_w_uq, v_b_w_uk, v_b_w_uv, v_even_w_out, v_odd_w_in, v_c_sink, v_odd_w_out, v_final_norm):
    s, d = x.shape[1], x.shape[2]
    x0 = x[0]
    target = loss_target[0]
    me_flat = 4 * lax.axis_index("x") + 2 * lax.axis_index("y") + lax.axis_index("c")

    cat = lambda arrs: jnp.concatenate([_rows128(a) for a in arrs], axis=0)

    def split_rows(p, like):
        out, o = [], 0
        for a in like:
            r = a.size // 128
            out.append(p[..., o:o + r, :])
            o += r
        return out

    first_w = [even_w_in, b_w_uq]
    later_w = [odd_w_in, even_w_out, odd_w_out]
    g_first = all_gather_rows(cat(first_w), MXU, "gather_first_weights")
    seg_in_e, seg_uq = split_rows(g_first.reshape(N_DEV, -1, 128), first_w)
    w_in_e = _even_cols_to_kernel(_from_col_blocks(seg_in_e, d))
    w_uq = _from_col_blocks(seg_uq, B_Q_LORA)
    later_exchange = Exchange(cat(later_w).astype(MXU), scatter=False)
    w_uk = jnp.transpose(b_w_uk[0], (1, 0, 2)).astype(MXU)
    w_uv = jnp.transpose(b_w_uv[0], (1, 0, 2)).astype(MXU)

    wcols = ada_w.shape[2]
    bias_cols = lax.dynamic_slice_in_dim(ada_b.reshape(2, N_DEV, wcols), me_flat, 1, axis=1)
    call, modp = ada_forward(jnp.broadcast_to(c, (8, d)), ada_w, bias_cols)
    c_all = call[:, 0, :]
    mod = jnp.transpose(modp[:, :, 0, :], (1, 0, 2)).reshape(2, 3, d)
    mod_e, mod_o = mod[0], mod[1]
    nw_e, nw_o = norm_w[0:1], norm_w[1:2]

    cos_a, sin_a, cos_t, sin_t = _rope_tables(s)
    slopes = (2.0 ** (-8.0 * jnp.arange(1, C_HEADS + 1, dtype=F32) / C_HEADS)).reshape(C_HEADS, 1, 1)
    sink2 = c_sink.reshape(C_HEADS, 1, 1) * LOG2E

    (qa, ka, va, qb, kb, kat, vat, kbt, qa_raw, ka_raw, cq_raw, ckv_raw, ga, gb) = even_in_forward(
        x0, mod_e, nw_e, w_in_e, a_q_norm, a_k_norm, b_q_lora_norm, b_kv_lora_norm, w_uq, w_uk,
        cos_a, sin_a, cos_t, sin_t)
    tk_dense = min(512, s)
    oa, lse_a, g_later = flash_forward(qa, ka, vat, scale=HD ** -0.5, dv=HD, tq=min(256, s), tk=tk_dense,
                                       name="attn_a_fwd", exchange=later_exchange)
    seg_in_o, seg_out_e, seg_out_o = split_rows(g_later, later_w)
    w_in_o = _from_col_blocks(seg_in_o, d)
    w_out_e = seg_out_e.reshape(1024, d)
    w_out_o = seg_out_o.reshape(1024, d)
    scale_b = (B_NOPE + B_ROPE) ** -0.5
    o_lat, lse_b = flash_forward(qb, kb, kbt, scale=scale_b, dv=B_KV_LORA, tq=min(128, s), tk=tk_dense,
                                 name="attn_b_fwd")
    ob = latent_out_forward(o_lat, w_uv)
    x1, y_e = mixer_out_forward(x0, mod_e, [(oa, ga), (ob, gb)], w_out_e, "even_out_fwd")

    qc, kc, vc, kct, vct, gc = odd_in_forward(x1, mod_o, nw_o, w_in_o)
    oc, lse_c = window_forward(qc, kc, vct, sink2, slopes, "attn_c_fwd")
    x2, y_o = mixer_out_forward(x1, mod_o, [(oc, gc)], w_out_o, "odd_out_fwd")

    loss_lanes, dx2, d_final = loss_head(x2, target, final_norm.reshape(1, d))
    loss = lax.psum(0.5 * jnp.sum(loss_lanes) / d, MESH_AXES)

    doc, dgc, delta_c, dgate_o, dw_out_o, dsink = mixer_out_backward(
        dx2, y_o, mod_o, [(oc, gc)], w_out_o.T, [C_HEADS], "odd_out_bwd", lse=lse_c.reshape(C_HEADS, s),
        sink=sink2.reshape(C_HEADS, 1))
    rows3 = lambda t: t.reshape(t.shape[0], 1, s)
    dqc, dkc, dvc = window_backward(qc, kc, kct, vc, doc, lse_c, rows3(delta_c), slopes, "attn_c_bwd")
    to_rows = lambda t: jnp.transpose(t, (1, 0, 2)).reshape(s, -1)
    dx1, dw_in_o, dvec_o = in_proj_backward(
        x1, mod_o, nw_o, dx2, [(dqc, O_Q), (to_rows(dkc), O_K), (to_rows(dvc), O_V), (dgc, O_G)], w_in_o.T,
        "odd_in_bwd")

    doa, dga, dob, dgb, delta_a, dgate_e, dw_out_e = mixer_out_backward(
        dx1, y_e, mod_e, [(oa, ga), (ob, gb)], w_out_e.T, [A_HEADS, 0], "even_out_bwd")
    d_olat, delta_b, dw_uv = latent_out_backward(dob, o_lat, w_uv)
    row_blocks = lambda w: w.astype(MXU).reshape(N_DEV, -1, 128)
    scatter_odd = Exchange(jnp.concatenate([_col_blocks(dw_in_o.astype(MXU)), row_blocks(dw_out_o)], axis=1), True)
    scatter_out_e = Exchange(row_blocks(dw_out_e), True)
    dqb, dkb, dvb, landed_odd = flash_backward(
        qb, kb, kbt, None, d_olat, lse_b, rows3(delta_b), scale=scale_b, dv=B_KV_LORA,
        tq=min(256, s), tk=tk_dense, gq=2, name="attn_b_bwd", exchange=scatter_odd)
    dqa, dka, dva, landed_out_e = flash_backward(
        qa, ka, kat, va, doa, lse_a, rows3(delta_a), scale=HD ** -0.5, dv=HD,
        tq=min(256, s), tk=tk_dense, gq=A_KV, name="attn_a_bwd", exchange=scatter_out_e)
    (pqa, pka, pva, pcq, pckv, pkr, g_qn, g_kn, g_qln, g_kvln, dw_uq, dw_uk) = even_prep_backward(
        dqa, dka, dva, dqb, dkb, dvb, qa_raw, ka_raw, cq_raw, ckv_raw,
        a_q_norm, a_k_norm, b_q_lora_norm, b_kv_lora_norm, w_uq, w_uq.T, w_uk, cos_a, sin_a, cos_t, sin_t)
    dx0, dw_in_e, dvec_e = in_proj_backward(
        x0, mod_e, nw_e, dx1,
        [(pqa, E_QA), (pka, E_KA), (pva, E_VA), (dga, E_GA), (pcq, E_CQ), (pckv, E_CKV), (dgb, E_GB), (pkr, E_KR)],
        w_in_e.T, "even_in_bwd")

    dmod = jnp.stack([jnp.concatenate([dvec_e[0], dvec_e[1], dgate_e[0]]),
                      jnp.concatenate([dvec_o[0], dvec_o[1], dgate_o[0]])])
    d_norm_w = jnp.stack([dvec_e[2], dvec_o[2]])
    small_names = ["norm_w", "ada_b", "a_q_norm", "a_k_norm", "b_q_lora_norm", "b_kv_lora_norm", "b_w_uk", "b_w_uv",
                   "c_sink", "final_norm"]
    small_w = [norm_w, ada_b, a_q_norm, a_k_norm, b_q_lora_norm, b_kv_lora_norm, b_w_uk, b_w_uv, c_sink, final_norm]
    small_m = [m_norm_w, m_ada_b, m_a_q_norm, m_a_k_norm, m_b_q_lora_norm, m_b_kv_lora_norm, m_b_w_uk, m_b_w_uv,
               m_c_sink, m_final_norm]
    small_v = [v_norm_w, v_ada_b, v_a_q_norm, v_a_k_norm, v_b_q_lora_norm, v_b_kv_lora_norm, v_b_w_uk, v_b_w_uv,
               v_c_sink, v_final_norm]
    small_g = [d_norm_w, dmod, g_qn, g_kn, g_qln, g_kvln, jnp.transpose(dw_uk, (1, 0, 2)), jnp.transpose(dw_uv, (1, 0, 2)),
               dsink, d_final]
    sizes = [w.size for w in small_w]
    n_small = sum(sizes)
    r_small = -(-n_small // (128 * 8)) * 8
    flat_pack = lambda arrs: _pad_rows(jnp.concatenate([a.reshape(-1) for a in arrs]), r_small)
    g_small_all = all_gather_rows(flat_pack(small_g), F32, "gather_small_grads").reshape(N_DEV, r_small, 128)
    sm = adamw_rows(g_small_all, flat_pack(small_w), flat_pack(small_m), flat_pack(small_v), "adamw_small")

    def unpack_small(packed):
        flat = packed.reshape(-1)
        out, o = {}, 0
        for nm, w, sz in zip(small_names, small_w, sizes):
            out[nm] = flat[o:o + sz].reshape(w.shape)
            o += sz
        return out

    sm = [unpack_small(p) for p in sm]

    dmod_all = g_small_all.reshape(N_DEV, -1)[:, sizes[0]:sizes[0] + sizes[1]].reshape(N_DEV, 2, N_DEV, wcols)
    dmod_cols = lax.dynamic_slice_in_dim(dmod_all, me_flat, 1, axis=2)[:, :, 0, :]
    pad16 = lambda a: jnp.concatenate([a, jnp.zeros_like(a)], axis=0)
    g_ada_w = ada_weight_grad(pad16(c_all), jnp.transpose(pad16(dmod_cols), (1, 0, 2)))
    ada = adamw_rows(_rows128(g_ada_w)[None], _rows128(ada_w), _rows128(m_ada_w), _rows128(v_ada_w), "adamw_ada_w")
    ada = [p.reshape(ada_w.shape) for p in ada]

    scatter_in_e = Exchange(jnp.concatenate([_col_blocks(_even_cols_to_reference(dw_in_e).astype(MXU)),
                                             _col_blocks(dw_uq.astype(MXU))], axis=1), True)
    landed_in_e = exchange_blocks(scatter_in_e, "scatter_first_weight_grads")
    bg = [{}, {}, {}, {}]
    for landed, names, ws, ms, vs in (
            (landed_odd, ["odd_w_in", "odd_w_out"], [odd_w_in, odd_w_out], [m_odd_w_in, m_odd_w_out],
             [v_odd_w_in, v_odd_w_out]),
            (landed_out_e, ["even_w_out"], [even_w_out], [m_even_w_out], [v_even_w_out]),
            (landed_in_e, ["even_w_in", "b_w_uq"], [even_w_in, b_w_uq], [m_even_w_in, m_b_w_uq],
             [v_even_w_in, v_b_w_uq])):
        res = adamw_rows(landed, cat(ws), cat(ms), cat(vs), "adamw_" + names[0])
        for kind, p in enumerate(res):
            for nm, w, piece in zip(names, ws, split_rows(p, ws)):
                bg[kind][nm] = piece.reshape(w.shape)
    big_names = ["even_w_in", "odd_w_in", "even_w_out", "odd_w_out", "b_w_uq"]

    order = ["norm_w", "ada_w", "ada_b", "even_w_in", "a_q_norm", "a_k_norm", "b_q_lora_norm", "b_kv_lora_norm",
             "b_w_uq", "b_w_uk", "b_w_uv", "even_w_out", "odd_w_in", "c_sink", "odd_w_out", "final_norm"]

    def pick(kind):
        out = []
        for nm in order:
            if nm == "ada_w":
                out.append(ada[kind])
            elif nm in big_names:
                out.append(bg[kind][nm])
            else:
                out.append(sm[kind][nm])
        return out

    return (loss, dx0[None], *pick(0), *pick(1), *pick(2), *pick(3))
```

```python
import functools

import jax
import jax.numpy as jnp
import numpy as np
from jax import lax
from jax.experimental import pallas as pl
from jax.experimental.pallas import tpu as pltpu

F32 = jnp.float32
MXU = jnp.bfloat16
EPS = 1e-6
ROPE_THETA = 10000.0
GRID_W = 64
HD = 64
N_DEV = 8
MESH_AXES = ("x", "y", "c")

A_HEADS, A_KV = 8, 2
B_HEADS, B_NOPE, B_ROPE, B_Q_LORA, B_KV_LORA = 8, 64, 32, 256, 128
B_QK = B_KV_LORA + B_ROPE
C_HEADS, C_KV = 16, 4
WINDOW = 128

ADAM_LR, ADAM_B1, ADAM_B2, ADAM_EPS, ADAM_WD, ADAM_STEP = 0.001, 0.9, 0.999, 1e-08, 0.01, 10

ROW_TILE = 256
VMEM_LIMIT = 56 * 1024 * 1024

E_QA, E_KA, E_VA, E_GA, E_CQ, E_CKV, E_GB, E_KR = (
    (0, 512), (512, 640), (640, 768), (768, 1280), (1280, 1536), (1536, 1664), (1664, 2176), (2176, 2208))
EVEN_IN = 2208
O_Q, O_K, O_V, O_G = (0, 1024), (1024, 1280), (1280, 1536), (1536, 2560)
ODD_IN = 2560


def _mm(a, b):
    return jnp.dot(a.astype(MXU), b.astype(MXU), preferred_element_type=F32)


def _mm_nt(a, b):
    return lax.dot_general(a.astype(MXU), b.astype(MXU), (((1,), (1,)), ((), ())), preferred_element_type=F32)


def _mm_tn(a, b):
    return lax.dot_general(a.astype(MXU), b.astype(MXU), (((0,), (0,)), ((), ())), preferred_element_type=F32)


def _group_sums_t(prod, group):
    tm, w = prod.shape
    sel = (lax.broadcasted_iota(jnp.int32, (w, 128), 0) // group
           == lax.broadcasted_iota(jnp.int32, (w, 128), 1)).astype(MXU)
    hi = prod.astype(MXU)
    lo = prod - hi.astype(F32)
    return (_mm(hi, sel) + _mm(lo, sel)).T


def _sigmoid(z):
    return 1.0 / (1.0 + jnp.exp(-z))


def _silu(z):
    return z * _sigmoid(z)


def _rms(x):
    return lax.rsqrt(jnp.mean(x * x, axis=-1, keepdims=True) + EPS)


def _swap_halves(y, group):
    n = y.shape[-1]
    half = group // 2
    fwd = pltpu.roll(y, half, 1)
    if n == group:
        return fwd
    back = pltpu.roll(y, n - half, 1)
    lane = lax.broadcasted_iota(jnp.int32, y.shape, 1)
    return jnp.where((lane % group) < half, back, fwd)


def _rope(y, cos, sin, group):
    return y * cos + _swap_halves(y, group) * sin


def _rope_t(d, cos, sin, group):
    return d * cos - _swap_halves(d, group) * sin


def _rms_bwd(dy, x, g):
    r = _rms(x)
    xhat = x * r
    dxhat = dy * g
    dx = r * (dxhat - xhat * jnp.mean(dxhat * xhat, axis=-1, keepdims=True))
    return dx, dy * xhat


def _params(sem, vmem=VMEM_LIMIT):
    return pltpu.CompilerParams(dimension_semantics=sem, vmem_limit_bytes=vmem)


def _row_spec(tm, w):
    return pl.BlockSpec((tm, w), lambda i: (i, 0))


def _full_spec(shape):
    nd = len(shape)
    return pl.BlockSpec(shape, lambda i: (0,) * nd)


def _head_spec(h, tm, w):
    return pl.BlockSpec((h, tm, w), lambda i: (0, i, 0))


def _headt_spec(h, w, tm):
    return pl.BlockSpec((h, w, tm), lambda i: (0, 0, i))


def _rows_spec(h, tm):
    return pl.BlockSpec((h, tm), lambda i: (0, i))


def _me():
    return lax.axis_index("x"), lax.axis_index("y"), lax.axis_index("c")


def _flat(p):
    return 4 * p[0] + 2 * p[1] + p[2]


def _peer(me, k):
    x, y, c = me
    return (1 - x if k & 4 else x, 1 - y if k & 2 else y, 1 - c if k & 1 else c)


MESH_ID = pl.DeviceIdType.MESH


def all_gather_rows(x_shard, out_dtype, name):
    r, n = x_shard.shape

    def body(x_ref, out_ref, xs_ref, send_sems, recv_sems, local_sem):
        me = _me()
        x, y, c = me
        sibling = (x, y, 1 - c)
        chips = [(1 - x, y), (x, 1 - y), (1 - x, 1 - y)]
        xs_ref[...] = x_ref[...].astype(out_dtype)

        def rows(p):
            return out_ref.at[pl.ds(pl.multiple_of(_flat(p) * r, 16), r), :]

        def copy(k, block, to, src=None):
            return pltpu.make_async_remote_copy(
                src_ref=rows(block) if src is None else src, dst_ref=rows(block),
                send_sem=send_sems.at[k], recv_sem=recv_sems.at[k], device_id=to, device_id_type=MESH_ID)

        mine = pltpu.make_async_copy(xs_ref, rows(me), local_sem)
        mine.start()
        first = [copy(0, me, sibling, src=xs_ref)]
        first += [copy(1 + j, me, (*chip, c), src=xs_ref) for j, chip in enumerate(chips)]
        for cp in first:
            cp.start()
        passed = [copy(4 + j, (*chip, c), sibling) for j, chip in enumerate(chips)]
        for j, chip in enumerate(chips):
            copy(1 + j, (*chip, c), me).wait_recv()
            passed[j].start()
        copy(0, sibling, me).wait_recv()
        for j, chip in enumerate(chips):
            copy(4 + j, (*chip, 1 - c), me).wait_recv()
        for cp in first + passed:
            cp.wait_send()
        mine.wait()

    return pl.pallas_call(
        body, name=name,
        out_shape=jax.ShapeDtypeStruct((N_DEV * r, n), out_dtype),
        in_specs=[pl.BlockSpec(memory_space=pltpu.VMEM)],
        out_specs=pl.BlockSpec(memory_space=pltpu.VMEM),
        scratch_shapes=[pltpu.VMEM((r, n), out_dtype), pltpu.SemaphoreType.DMA((7,)),
                        pltpu.SemaphoreType.DMA((7,)), pltpu.SemaphoreType.DMA],
        compiler_params=pltpu.CompilerParams(vmem_limit_bytes=VMEM_LIMIT),
    )(x_shard)


class Exchange:
    SEMS = [pltpu.SemaphoreType.DMA((N_DEV - 1,)), pltpu.SemaphoreType.DMA((N_DEV - 1,)), pltpu.SemaphoreType.DMA]
    HBM = pl.BlockSpec(memory_space=pl.ANY)

    def __init__(self, src, scatter):
        self.src = src
        self.scatter = scatter
        rows = src.shape[-2:]
        self.land_shape = jax.ShapeDtypeStruct((N_DEV,) + tuple(rows), src.dtype)

    def _copies(self, src_ref, land_ref, send_sems, recv_sems, local_sem):
        me = _me()
        mi = _flat(me)
        pick = (lambda p: src_ref.at[_flat(p)]) if self.scatter else (lambda p: src_ref)
        local = pltpu.make_async_copy(pick(me), land_ref.at[mi], local_sem)
        sends, recvs = [], []
        for k in range(1, N_DEV):
            peer = _peer(me, k)
            sems = dict(send_sem=send_sems.at[k - 1], recv_sem=recv_sems.at[k - 1], device_id=peer,
                        device_id_type=MESH_ID)
            sends.append(pltpu.make_async_remote_copy(src_ref=pick(peer), dst_ref=land_ref.at[mi], **sems))
            recvs.append(pltpu.make_async_remote_copy(src_ref=pick(peer), dst_ref=land_ref.at[_flat(peer)], **sems))
        return local, sends, recvs

    def start(self, *refs):
        local, sends, _ = self._copies(*refs)
        local.start()
        for cp in sends:
            cp.start()

    def wait(self, *refs):
        local, sends, recvs = self._copies(*refs)
        for cp in recvs:
            cp.wait_recv()
        for cp in sends:
            cp.wait_send()
        local.wait()


def exchange_blocks(ex, name):
    def body(src_ref, land_ref, *sems):
        ex.start(src_ref, land_ref, *sems)
        ex.wait(src_ref, land_ref, *sems)

    return pl.pallas_call(
        body, name=name, out_shape=ex.land_shape, in_specs=[Exchange.HBM], out_specs=Exchange.HBM,
        scratch_shapes=list(Exchange.SEMS),
    )(ex.src)


def ada_forward(c8, ada_w, bias_cols):
    d = c8.shape[1]
    w = ada_w.shape[2]

    def body(c_ref, w_ref, b_ref, call_ref, modp_ref, part_ref, s1, r1, s2, r2):
        me = _me()
        mi = _flat(me)
        call_ref[mi] = c_ref[...]
        gather = []
        for k in range(1, N_DEV):
            gather.append(pltpu.make_async_remote_copy(
                src_ref=c_ref, dst_ref=call_ref.at[mi], send_sem=s1.at[k - 1], recv_sem=r1.at[k - 1],
                device_id=_peer(me, k), device_id_type=MESH_ID))
        for cp in gather:
            cp.start()
        for k in range(1, N_DEV):
            pltpu.make_async_remote_copy(
                src_ref=c_ref, dst_ref=call_ref.at[_flat(_peer(me, k))], send_sem=s1.at[k - 1],
                recv_sem=r1.at[k - 1], device_id=_peer(me, k), device_id_type=MESH_ID).wait_recv()
        ca = _silu(call_ref[...].reshape(N_DEV * 8, d))
        for l in range(2):
            part = _mm(ca, w_ref[l]) + b_ref[l]
            for b in range(N_DEV):
                part_ref[b, l] = part[8 * b:8 * b + 8, :]
        modp_ref[mi] = part_ref[mi]
        spread = []
        for k in range(1, N_DEV):
            peer = _peer(me, k)
            spread.append(pltpu.make_async_remote_copy(
                src_ref=part_ref.at[_flat(peer)], dst_ref=modp_ref.at[mi], send_sem=s2.at[k - 1],
                recv_sem=r2.at[k - 1], device_id=peer, device_id_type=MESH_ID))
        for cp in spread:
            cp.start()
        for k in range(1, N_DEV):
            pi = _flat(_peer(me, k))
            pltpu.make_async_remote_copy(
                src_ref=part_ref.at[pi], dst_ref=modp_ref.at[pi], send_sem=s2.at[k - 1],
                recv_sem=r2.at[k - 1], device_id=_peer(me, k), device_id_type=MESH_ID).wait_recv()
        for cp in gather + spread:
            cp.wait_send()

    vm = pl.BlockSpec(memory_space=pltpu.VMEM)
    return pl.pallas_call(
        body, name="ada_forward",
        out_shape=(jax.ShapeDtypeStruct((N_DEV, 8, d), F32), jax.ShapeDtypeStruct((N_DEV, 2, 8, w), F32)),
        in_specs=[vm, vm, vm], out_specs=(vm, vm),
        scratch_shapes=[pltpu.VMEM((N_DEV, 2, 8, w), F32)] + [pltpu.SemaphoreType.DMA((7,))] * 4,
        compiler_params=pltpu.CompilerParams(vmem_limit_bytes=VMEM_LIMIT),
    )(c8, ada_w, bias_cols)


def _modulated(x, mod_ref, nw_ref):
    xn = x * _rms(x)
    g1 = nw_ref[...] * (1.0 + mod_ref[1:2, :])
    return xn, g1, xn * g1 + mod_ref[0:1, :]


def even_in_forward(x, mod, nw, w_in, qn, kn, qln, kvln, w_uq, w_uk, cos_a, sin_a, cos_t, sin_t):
    s, d = x.shape
    tm = min(ROW_TILE, s)

    def body(x_ref, mod_ref, nw_ref, w_ref, qn_ref, kn_ref, qln_ref, kvln_ref, uq_ref, uk_ref,
             ca_ref, sa_ref, ct_ref, st_ref,
             qa_o, ka_o, va_o, qb_o, kb_o, kat_o, vat_o, kbt_o, qa_raw_o, ka_raw_o, cq_raw_o, ckv_raw_o, ga_o, gb_o):
        _, _, h = _modulated(x_ref[...], mod_ref, nw_ref)
        h = h.astype(MXU)

        def proj(cols):
            return jnp.dot(h, w_ref[:, cols[0]:cols[1]], preferred_element_type=F32)

        ca, sa, ct, st = ca_ref[...], sa_ref[...], ct_ref[...], st_ref[...]
        qa = proj(E_QA)
        qa_raw_o[...] = qa
        for hh in range(A_HEADS):
            xh = qa[:, HD * hh:HD * hh + HD]
            qa_o[hh] = _rope(xh * _rms(xh) * qn_ref[...], ca, sa, 32).astype(MXU)
        ka = proj(E_KA)
        ka_raw_o[...] = ka
        va = proj(E_VA)
        for g in range(A_KV):
            xh = ka[:, HD * g:HD * g + HD]
            kr = _rope(xh * _rms(xh) * kn_ref[...], ca, sa, 32)
            vh = va[:, HD * g:HD * g + HD]
            ka_o[g] = kr.astype(MXU)
            va_o[g] = vh.astype(MXU)
            kat_o[g] = kr.T.astype(MXU)
            vat_o[g] = vh.T.astype(MXU)
        ga_o[...] = proj(E_GA)
        gb_o[...] = proj(E_GB)
        cq = proj(E_CQ)
        cq_raw_o[...] = cq
        qb = _mm(cq * _rms(cq) * qln_ref[...], uq_ref[...])
        for hh in range(B_HEADS):
            base = (B_NOPE + B_ROPE) * hh
            qb_o[hh, :, 0:B_KV_LORA] = _mm_nt(qb[:, base:base + B_NOPE], uk_ref[hh]).astype(MXU)
            qb_o[hh, :, B_KV_LORA:B_QK] = _rope(qb[:, base + B_NOPE:base + B_NOPE + B_ROPE], ct, st, 32).astype(MXU)
        ckv = proj(E_CKV)
        ckv_raw_o[...] = ckv
        ckv_n = ckv * _rms(ckv) * kvln_ref[...]
        k_rope = _rope(proj(E_KR), ct, st, 32)
        kb_o[0, :, 0:B_KV_LORA] = ckv_n.astype(MXU)
        kb_o[0, :, B_KV_LORA:B_QK] = k_rope.astype(MXU)
        kbt_o[0, 0:B_KV_LORA, :] = ckv_n.T.astype(MXU)
        kbt_o[0, B_KV_LORA:B_QK, :] = k_rope.T.astype(MXU)

    sd = jax.ShapeDtypeStruct
    outs = (sd((A_HEADS, s, HD), MXU), sd((A_KV, s, HD), MXU), sd((A_KV, s, HD), MXU),
            sd((B_HEADS, s, B_QK), MXU), sd((1, s, B_QK), MXU),
            sd((A_KV, HD, s), MXU), sd((A_KV, HD, s), MXU), sd((1, B_QK, s), MXU),
            sd((s, 512), F32), sd((s, 128), F32), sd((s, B_Q_LORA), F32), sd((s, B_KV_LORA), F32),
            sd((s, 512), F32), sd((s, 512), F32))
    out_specs = (_head_spec(A_HEADS, tm, HD), _head_spec(A_KV, tm, HD), _head_spec(A_KV, tm, HD),
                 _head_spec(B_HEADS, tm, B_QK), _head_spec(1, tm, B_QK),
                 _headt_spec(A_KV, HD, tm), _headt_spec(A_KV, HD, tm), _headt_spec(1, B_QK, tm),
                 _row_spec(tm, 512), _row_spec(tm, 128), _row_spec(tm, B_Q_LORA), _row_spec(tm, B_KV_LORA),
                 _row_spec(tm, 512), _row_spec(tm, 512))
    in_specs = [_row_spec(tm, d), _full_spec(mod.shape), _full_spec(nw.shape), _full_spec(w_in.shape),
                _full_spec(qn.shape), _full_spec(kn.shape), _full_spec(qln.shape), _full_spec(kvln.shape),
                _full_spec(w_uq.shape), _full_spec(w_uk.shape),
                _row_spec(tm, HD), _row_spec(tm, HD), _row_spec(tm, B_ROPE), _row_spec(tm, B_ROPE)]
    return pl.pallas_call(
        body, name="even_in_forward", grid=(s // tm,), out_shape=outs, in_specs=in_specs, out_specs=out_specs,
        compiler_params=_params(("parallel",)),
    )(x, mod, nw, w_in, qn, kn, qln, kvln, w_uq, w_uk, cos_a, sin_a, cos_t, sin_t)


def odd_in_forward(x, mod, nw, w_in):
    s, d = x.shape
    tm = min(ROW_TILE, s)

    def body(x_ref, mod_ref, nw_ref, w_ref, q_o, k_o, v_o, kt_o, vt_o, g_o):
        _, _, h = _modulated(x_ref[...], mod_ref, nw_ref)
        h = h.astype(MXU)

        def proj(cols):
            return jnp.dot(h, w_ref[:, cols[0]:cols[1]], preferred_element_type=F32)

        q = proj(O_Q)
        for hh in range(C_HEADS):
            q_o[hh] = q[:, HD * hh:HD * hh + HD].astype(MXU)
        k = proj(O_K)
        v = proj(O_V)
        for g in range(C_KV):
            kh = k[:, HD * g:HD * g + HD]
            vh = v[:, HD * g:HD * g + HD]
            k_o[g] = kh.astype(MXU)
            v_o[g] = vh.astype(MXU)
            kt_o[g] = kh.T.astype(MXU)
            vt_o[g] = vh.T.astype(MXU)
        g_o[...] = proj(O_G)

    sd = jax.ShapeDtypeStruct
    return pl.pallas_call(
        body, name="odd_in_forward", grid=(s // tm,),
        out_shape=(sd((C_HEADS, s, HD), MXU), sd((C_KV, s, HD), MXU), sd((C_KV, s, HD), MXU),
                   sd((C_KV, HD, s), MXU), sd((C_KV, HD, s), MXU), sd((s, 1024), F32)),
        in_specs=[_row_spec(tm, d), _full_spec(mod.shape), _full_spec(nw.shape), _full_spec(w_in.shape)],
        out_specs=(_head_spec(C_HEADS, tm, HD), _head_spec(C_KV, tm, HD), _head_spec(C_KV, tm, HD),
                   _headt_spec(C_KV, HD, tm), _headt_spec(C_KV, HD, tm), _row_spec(tm, 1024)),
        compiler_params=_params(("parallel",)),
    )(x, mod, nw, w_in)


def latent_out_forward(o_lat, w_uv):
    s = o_lat.shape[0]
    tm = min(ROW_TILE, s)

    def body(o_ref, uv_ref, out_ref):
        for hh in range(B_HEADS):
            out_ref[:, HD * hh:HD * hh + HD] = _mm(o_ref[:, B_KV_LORA * hh:B_KV_LORA * (hh + 1)], uv_ref[hh])

    return pl.pallas_call(
        body, name="latent_out_forward", grid=(s // tm,),
        out_shape=jax.ShapeDtypeStruct((s, B_HEADS * HD), F32),
        in_specs=[_row_spec(tm, o_lat.shape[1]), _full_spec(w_uv.shape)],
        out_specs=_row_spec(tm, B_HEADS * HD),
        compiler_params=_params(("parallel",)),
    )(o_lat, w_uv)


def mixer_out_forward(x, mod, pairs, w_out, name):
    s, d = x.shape
    tm = min(ROW_TILE, s)
    n = len(pairs)
    widths = [o.shape[1] for o, _ in pairs]

    def body(*refs):
        x_ref, mod_ref, w_ref = refs[:3]
        pr = refs[3:3 + 2 * n]
        xo_ref, y_ref = refs[3 + 2 * n:]
        y = jnp.zeros((tm, d), F32)
        r0 = 0
        for i in range(n):
            mix = pr[2 * i][...] * _silu(pr[2 * i + 1][...])
            y = y + _mm(mix, w_ref[r0:r0 + widths[i], :])
            r0 += widths[i]
        y_ref[...] = y
        xo_ref[...] = x_ref[...] + mod_ref[2:3, :] * y

    flat = [a for p in pairs for a in p]
    sd = jax.ShapeDtypeStruct
    return pl.pallas_call(
        body, name=name, grid=(s // tm,),
        out_shape=(sd((s, d), F32), sd((s, d), F32)),
        in_specs=[_row_spec(tm, d), _full_spec(mod.shape), _full_spec(w_out.shape)]
        + [_row_spec(tm, a.shape[1]) for a in flat],
        out_specs=(_row_spec(tm, d), _row_spec(tm, d)),
        compiler_params=_params(("parallel",)),
    )(x, mod, w_out, *flat)


LOG2E = 1.4426950408889634
ONES_ROWS = 16


def _col_max8(s3):
    m8 = jnp.max(s3, axis=0)
    return jnp.broadcast_to(jnp.max(m8, axis=0, keepdims=True), m8.shape)


def _with_ones(vt, n):
    return jnp.concatenate([vt, jnp.ones((ONES_ROWS, n), vt.dtype)], axis=0)


def _grid_edges(grid):
    ids = [pl.program_id(a) for a in range(len(grid))]
    first = functools.reduce(jnp.logical_and, [i == 0 for i in ids])
    last = functools.reduce(jnp.logical_and, [i == n - 1 for i, n in zip(ids, grid)])
    return first, last


def flash_forward(q, k, vt, *, scale, dv, tq, tk, nsub, name, exchange=None):
    hq, s, dq = q.shape
    g_kv = k.shape[0]
    hpg = hq // g_kv
    nq = s // tq
    tkk = tk * nsub
    nk = s // tkk
    grid = (g_kv, nq, nk)
    m_cols = hpg * tq
    c = scale * LOG2E
    dvp = dv + ONES_ROWS

    def body(*refs):
        if exchange is None:
            q_ref, k_ref, vt_ref, o_ref, lse_ref, m_s, acc_s = refs
        else:
            q_ref, k_ref, vt_ref, xs_ref, o_ref, lse_ref, land_ref, m_s, acc_s, *sems = refs
            first, last = _grid_edges(grid)
            pl.when(first)(lambda: exchange.start(xs_ref, land_ref, *sems))
        j = pl.program_id(2)

        @pl.when(j == 0)
        def _():
            m_s[...] = jnp.full((8, m_cols), -jnp.inf, F32)
            acc_s[...] = jnp.zeros((dvp, m_cols), F32)

        qq = q_ref[...].reshape(m_cols, dq)
        sts = [_mm_nt(k_ref[0, tk * u:tk * (u + 1), :], qq).reshape(tk // 8, 8, m_cols)
               for u in range(nsub)]
        m_run = m_s[...]
        acc = acc_s[...]
        for u in range(nsub):
            m_new = jnp.maximum(m_run, _col_max8(sts[u]) * c)
            p = jnp.exp2(sts[u] * c - m_new[None])
            alpha = jnp.exp2(m_run - m_new)
            pv = _mm(_with_ones(vt_ref[0, 0:dv, tk * u:tk * (u + 1)], tk), p.reshape(tk, m_cols))
            acc = (acc.reshape(dvp // 8, 8, m_cols) * alpha[None]).reshape(dvp, m_cols) + pv
            m_run = m_new
        acc_s[...] = acc
        m_s[...] = m_run

        @pl.when(j == nk - 1)
        def _():
            l = acc_s[dv:dv + 1, :]
            ot = acc_s[0:dv, :] / l
            lse = m_s[0:1, :] + jnp.log2(l)
            for hh in range(hpg):
                o_ref[:, dv * hh:dv * hh + dv] = ot[:, tq * hh:tq * hh + tq].T
                lse_ref[hh] = lse[:, tq * hh:tq * hh + tq]

        if exchange is not None:
            pl.when(last)(lambda: exchange.wait(xs_ref, land_ref, *sems))

    sd = jax.ShapeDtypeStruct
    hosted = exchange is not None
    return pl.pallas_call(
        body, name=name, grid=grid,
        out_shape=(sd((s, hq * dv), F32), sd((hq, 1, s), F32)) + ((exchange.land_shape,) if hosted else ()),
        in_specs=[pl.BlockSpec((hpg, tq, dq), lambda g, i, j: (g, i, 0)),
                  pl.BlockSpec((1, tkk, k.shape[2]), lambda g, i, j: (g, j, 0)),
                  pl.BlockSpec((1, dv, tkk), lambda g, i, j: (g, 0, j))] + ([Exchange.HBM] if hosted else []),
        out_specs=(pl.BlockSpec((tq, hpg * dv), lambda g, i, j: (i, g)),
                   pl.BlockSpec((hpg, 1, tq), lambda g, i, j: (g, 0, i))) + ((Exchange.HBM,) if hosted else ()),
        scratch_shapes=[pltpu.VMEM((8, m_cols), F32), pltpu.VMEM((dvp, m_cols), F32)]
        + (list(Exchange.SEMS) if hosted else []),
        compiler_params=_params(("arbitrary",) * 3 if hosted else ("parallel", "parallel", "arbitrary")),
    )(q, k, vt, *([exchange.src] if hosted else []))


def _window_bias_t(i, nq, hpg, slope_ref):
    t = WINDOW
    r = lax.broadcasted_iota(jnp.int32, (3 * t, t), 0)
    cq = lax.broadcasted_iota(jnp.int32, (3 * t, t), 1)
    arel = jnp.abs(r - t - cq)
    ok = (arel <= WINDOW) & ((r >= t) | (i > 0)) & ((r < 2 * t) | (i < nq - 1))
    base = jnp.where(ok, arel.astype(F32) * (-LOG2E), -jnp.inf)
    return jnp.concatenate([base * slope_ref[hh] for hh in range(hpg)], axis=1)


def _neighbour_specs(block, axis, nq, head_of):
    def spec(off):
        def index(g, i):
            idx = [head_of(g), 0, 0]
            idx[axis] = jnp.clip(i + off, 0, nq - 1)
            return tuple(idx)
        return pl.BlockSpec(block, index)
    return [spec(-1), spec(0), spec(1)]


def window_forward(q, k, vt, sink2, slopes, name):
    hq, s, d = q.shape
    g_kv = k.shape[0]
    hpg = hq // g_kv
    t = WINDOW
    nq = s // t
    m_cols = hpg * t
    c = (d ** -0.5) * LOG2E

    def body(q_ref, kp, ko, kn, vp, vo, vn, sink_ref, slope_ref, o_ref, lse_ref):
        i = pl.program_id(1)
        qq = q_ref[...].reshape(m_cols, d)
        kk = jnp.concatenate([kp[0], ko[0], kn[0]], axis=0)
        st = _mm_nt(kk, qq) * c + _window_bias_t(i, nq, hpg, slope_ref)
        sink_row = jnp.concatenate([jnp.broadcast_to(sink_ref[hh], (8, t)) for hh in range(hpg)], axis=1)
        s3 = st.reshape(3 * t // 8, 8, m_cols)
        m8 = jnp.maximum(_col_max8(s3), sink_row)
        p = jnp.exp2(s3 - m8[None]).reshape(3 * t, m_cols)
        vte = _with_ones(jnp.concatenate([vp[0], vo[0], vn[0]], axis=1), 3 * t)
        acc = _mm(vte, p)
        l = acc[d:d + 1, :] + jnp.exp2(sink_row[0:1, :] - m8[0:1, :])
        ot = acc[0:d, :] / l
        lse = m8[0:1, :] + jnp.log2(l)
        for hh in range(hpg):
            o_ref[:, d * hh:d * hh + d] = ot[:, t * hh:t * hh + t].T
            lse_ref[hh] = lse[:, t * hh:t * hh + t]

    head = lambda g: g
    sd = jax.ShapeDtypeStruct
    return pl.pallas_call(
        body, name=name, grid=(g_kv, nq),
        out_shape=(sd((s, hq * d), F32), sd((hq, 1, s), F32)),
        in_specs=[pl.BlockSpec((hpg, t, d), lambda g, i: (g, i, 0))]
        + _neighbour_specs((1, t, d), 1, nq, head) + _neighbour_specs((1, d, t), 2, nq, head)
        + [pl.BlockSpec((hpg, 1, 1), lambda g, i: (g, 0, 0))] * 2,
        out_specs=(pl.BlockSpec((t, hpg * d), lambda g, i: (i, g)),
                   pl.BlockSpec((hpg, 1, t), lambda g, i: (g, 0, i))),
        compiler_params=_params(("parallel", "parallel")),
    )(q, k, k, k, vt, vt, vt, sink2, slopes)


def window_backward(q, k, kt, v, do, lse, delta, slopes, name):
    hq, s, d = q.shape
    g_kv = k.shape[0]
    hpg = hq // g_kv
    t = WINDOW
    nq = s // t
    m_cols = hpg * t
    scale = d ** -0.5
    c = scale * LOG2E

    def body(q_ref, kp, ko, kn, ktp, kto, ktn, vp, vo, vn, do_ref, lse_ref, dl_ref, slope_ref,
             dq_ref, dk_ref, dv_ref):
        i = pl.program_id(1)

        @pl.when(i == 0)
        def _():
            dk_ref[...] = jnp.zeros(dk_ref.shape, F32)
            dv_ref[...] = jnp.zeros(dv_ref.shape, F32)

        qq = q_ref[...].reshape(m_cols, d)
        kk = jnp.concatenate([kp[0], ko[0], kn[0]], axis=0)
        vv = jnp.concatenate([vp[0], vo[0], vn[0]], axis=0)
        kkt = jnp.concatenate([ktp[0], kto[0], ktn[0]], axis=1)
        dd = jnp.concatenate([do_ref[:, d * hh:d * hh + d] for hh in range(hpg)], axis=0)
        lse_row = jnp.concatenate([lse_ref[hh] for hh in range(hpg)], axis=1)
        dl_row = jnp.concatenate([dl_ref[hh] for hh in range(hpg)], axis=1)
        st = _mm_nt(kk, qq) * c + _window_bias_t(i, nq, hpg, slope_ref)
        p = jnp.exp2(st - lse_row)
        ds = p * (_mm_nt(vv, dd) - dl_row) * scale
        dv_part = _mm(p, dd)
        dk_part = _mm(ds, qq)
        for b in range(3):
            r0 = pl.multiple_of(jnp.clip(i - 1 + b, 0, nq - 1) * t, t)
            dv_ref[0, pl.ds(r0, t), :] += dv_part[t * b:t * b + t, :]
            dk_ref[0, pl.ds(r0, t), :] += dk_part[t * b:t * b + t, :]
        dqt = _mm(kkt, ds)
        for hh in range(hpg):
            dq_ref[:, d * hh:d * hh + d] = dqt[:, t * hh:t * hh + t].T

    head = lambda g: g
    row_map = lambda g, i: (g, 0, i)
    sd = jax.ShapeDtypeStruct
    return pl.pallas_call(
        body, name=name, grid=(g_kv, nq),
        out_shape=(sd((s, hq * d), F32), sd((g_kv, s, d), F32), sd((g_kv, s, d), F32)),
        in_specs=[pl.BlockSpec((hpg, t, d), lambda g, i: (g, i, 0))]
        + _neighbour_specs((1, t, d), 1, nq, head) + _neighbour_specs((1, d, t), 2, nq, head)
        + _neighbour_specs((1, t, d), 1, nq, head)
        + [pl.BlockSpec((t, hpg * d), lambda g, i: (i, g)), pl.BlockSpec((hpg, 1, t), row_map),
           pl.BlockSpec((hpg, 1, t), row_map), pl.BlockSpec((hpg, 1, 1), lambda g, i: (g, 0, 0))],
        out_specs=(pl.BlockSpec((t, hpg * d), lambda g, i: (i, g)),
                   pl.BlockSpec((1, s, d), lambda g, i: (g, 0, 0)),
                   pl.BlockSpec((1, s, d), lambda g, i: (g, 0, 0))),
        compiler_params=_params(("parallel", "arbitrary")),
    )(q, k, k, k, kt, kt, kt, v, v, v, do, lse, delta, slopes)


def flash_backward(q, k, kt, v, do, lse, delta, *, scale, dv, tq, tk, nsub, gq, name, exchange=None):
    hq, s, dq = q.shape
    g_kv = k.shape[0]
    hpg = hq // gq
    nq = s // tq
    tqq = tq * nsub
    nqs = s // tqq
    nkb = s // tk
    grid = (gq, nkb, nqs)
    hosted = exchange is not None
    m_cols = hpg * tq
    c = scale * LOG2E
    has_v = v is not None

    def body(*refs):
        it = iter(refs)
        q_ref, k_ref, kt_ref = next(it), next(it), next(it)
        v_ref = next(it) if has_v else None
        do_ref, lse_ref, dl_ref = next(it), next(it), next(it)
        xs_ref = next(it) if hosted else None
        dq_ref, dk_ref, dv_ref = next(it), next(it), next(it)
        land_ref = next(it) if hosted else None
        dqt_s = next(it)
        sems = list(it)
        kj = pl.program_id(1)
        qi = pl.program_id(2)
        if hosted:
            first, last = _grid_edges(grid)
            pl.when(first)(lambda: exchange.start(xs_ref, land_ref, *sems))

        @pl.when((kj == 0) & (qi == 0))
        def _():
            dqt_s[...] = jnp.zeros(dqt_s.shape, F32)

        @pl.when(qi == 0)
        def _():
            dk_ref[...] = jnp.zeros(dk_ref.shape, F32)
            dv_ref[...] = jnp.zeros(dv_ref.shape, F32)

        kk = k_ref[0]
        vv = v_ref[0] if has_v else kk[:, :dv]
        qqs, dds, sts, dps = [], [], [], []
        for u in range(nsub):
            rows = slice(tq * u, tq * (u + 1))
            qqs.append(q_ref[:, rows, :].reshape(m_cols, dq))
            dds.append(jnp.concatenate([do_ref[rows, dv * hh:dv * hh + dv] for hh in range(hpg)], axis=0))
            sts.append(_mm_nt(kk, qqs[u]))
            dps.append(_mm_nt(vv, dds[u]))
        dv_acc = dv_ref[0]
        dk_acc = dk_ref[0]
        for u in range(nsub):
            rows = slice(tq * u, tq * (u + 1))
            lse_row = jnp.concatenate([lse_ref[hh, :, rows] for hh in range(hpg)], axis=1)
            dl_row = jnp.concatenate([dl_ref[hh, :, rows] for hh in range(hpg)], axis=1)
            p = jnp.exp2(sts[u] * c - lse_row)
            ds = p * (dps[u] - dl_row) * scale
            dv_acc = dv_acc + _mm(p, dds[u])
            dk_acc = dk_acc + _mm(ds, qqs[u])
            dqt = _mm(kt_ref[0], ds)
            for hh in range(hpg):
                dqt_s[qi * nsub + u, dq * hh:dq * hh + dq, :] += dqt[:, tq * hh:tq * hh + tq]
        dv_ref[0] = dv_acc
        dk_ref[0] = dk_acc

        @pl.when((kj == nkb - 1) & (qi == nqs - 1))
        def _():
            def emit(t, carry):
                r0 = pl.multiple_of(t * tq, tq)
                for hh in range(hpg):
                    dq_ref[pl.ds(r0, tq), dq * hh:dq * hh + dq] = dqt_s[t, dq * hh:dq * hh + dq, :].T
                return carry

            lax.fori_loop(0, nq, emit, 0)

        if hosted:
            pl.when(last)(lambda: exchange.wait(xs_ref, land_ref, *sems))

    kv_of = lambda g: g * g_kv // gq
    in_specs = [pl.BlockSpec((hpg, tqq, dq), lambda g, kj, qi: (g, qi, 0)),
                pl.BlockSpec((1, tk, dq), lambda g, kj, qi: (kv_of(g), kj, 0)),
                pl.BlockSpec((1, dq, tk), lambda g, kj, qi: (kv_of(g), 0, kj))]
    args = [q, k, kt]
    if has_v:
        in_specs.append(pl.BlockSpec((1, tk, dv), lambda g, kj, qi: (kv_of(g), kj, 0)))
        args.append(v)
    row_map = lambda g, kj, qi: (g, 0, qi)
    in_specs += [pl.BlockSpec((tqq, hpg * dv), lambda g, kj, qi: (qi, g)),
                 pl.BlockSpec((hpg, 1, tqq), row_map), pl.BlockSpec((hpg, 1, tqq), row_map)]
    args += [do, lse, delta]
    if hosted:
        in_specs.append(Exchange.HBM)
        args.append(exchange.src)
    sd = jax.ShapeDtypeStruct
    return pl.pallas_call(
        body, name=name, grid=grid,
        out_shape=(sd((s, hq * dq), F32), sd((gq, s, dq), F32), sd((gq, s, dv), F32))
        + ((exchange.land_shape,) if hosted else ()),
        in_specs=in_specs,
        out_specs=(pl.BlockSpec((s, hpg * dq), lambda g, kj, qi: (0, g)),
                   pl.BlockSpec((1, tk, dq), lambda g, kj, qi: (g, kj, 0)),
                   pl.BlockSpec((1, tk, dv), lambda g, kj, qi: (g, kj, 0))) + ((Exchange.HBM,) if hosted else ()),
        scratch_shapes=[pltpu.VMEM((nq, hpg * dq, tq), F32)] + (list(Exchange.SEMS) if hosted else []),
        compiler_params=_params(("arbitrary",) * 3 if hosted else ("parallel", "arbitrary", "arbitrary")),
    )(*args)


def loss_head(x, target, fnw):
    s, d = x.shape
    tm = min(ROW_TILE, s)

    def body(x_ref, t_ref, w_ref, lp_ref, dx_ref, dw_ref):
        @pl.when(pl.program_id(0) == 0)
        def _():
            lp_ref[...] = jnp.zeros(lp_ref.shape, F32)
            dw_ref[...] = jnp.zeros(dw_ref.shape, F32)

        x = x_ref[...]
        g = w_ref[...]
        err = x * _rms(x) * g - t_ref[...]
        lp_ref[...] += jnp.sum(err * err, axis=0, keepdims=True)
        dx, dg = _rms_bwd(err * (1.0 / d), x, g)
        dx_ref[...] = dx
        dw_ref[...] += jnp.sum(dg, axis=0, keepdims=True)

    sd = jax.ShapeDtypeStruct
    return pl.pallas_call(
        body, name="loss_head", grid=(s // tm,),
        out_shape=(sd((1, d), F32), sd((s, d), F32), sd((1, d), F32)),
        in_specs=[_row_spec(tm, d), _row_spec(tm, d), _full_spec(fnw.shape)],
        out_specs=(_full_spec((1, d)), _row_spec(tm, d), _full_spec((1, d))),
        compiler_params=_params(("arbitrary",)),
    )(x, target, fnw)


def mixer_out_backward(dx, y, mod, pairs, w_out_t, delta_heads, name, lse=None, sink=None):
    s, d = dx.shape
    tm = min(ROW_TILE, s)
    n = len(pairs)
    widths = [o.shape[1] for o, _ in pairs]
    n_delta = sum(1 for h in delta_heads if h)
    with_sink = lse is not None

    def body(*refs):
        it = iter(refs)
        dx_ref, y_ref, mod_ref, wt_ref = next(it), next(it), next(it), next(it)
        pr = [next(it) for _ in range(2 * n)]
        lse_ref = next(it) if with_sink else None
        sink_ref = next(it) if with_sink else None
        outs = [next(it) for _ in range(2 * n)]
        dl_refs = [next(it) for _ in range(n_delta)]
        dgate_ref, dw_ref = next(it), next(it)
        dsink_ref = next(it) if with_sink else None

        @pl.when(pl.program_id(0) == 0)
        def _():
            dgate_ref[...] = jnp.zeros(dgate_ref.shape, F32)
            dw_ref[...] = jnp.zeros(dw_ref.shape, F32)
            if with_sink:
                dsink_ref[...] = jnp.zeros(dsink_ref.shape, F32)

        dxo = dx_ref[...]
        dgate_ref[...] += jnp.sum(dxo * y_ref[...], axis=0, keepdims=True)
        dy = (dxo * mod_ref[2:3, :]).astype(MXU)
        dmix = jnp.dot(dy, wt_ref[...], preferred_element_type=F32)
        r0 = 0
        di = 0
        for i in range(n):
            o = pr[2 * i][...]
            g = pr[2 * i + 1][...]
            dm = dmix[:, r0:r0 + widths[i]]
            sg = _sigmoid(g)
            act = g * sg
            do = dm * act
            outs[2 * i][...] = do.astype(MXU)
            outs[2 * i + 1][...] = (dm * o * (sg * (1.0 + g * (1.0 - sg)))).astype(MXU)
            dw_ref[r0:r0 + widths[i], :] += _mm_tn(o * act, dy)
            if delta_heads[i]:
                dlt = _group_sums_t(do * o, HD)[0:delta_heads[i], :]
                dl_refs[di][...] = dlt
                if with_sink:
                    ps = jnp.exp2(sink_ref[...] - lse_ref[...])
                    dsink_ref[...] += -jnp.sum(ps * dlt, axis=1, keepdims=True)
                di += 1
            r0 += widths[i]

    flat = [a for p in pairs for a in p]
    sd = jax.ShapeDtypeStruct
    in_specs = [_row_spec(tm, d), _row_spec(tm, d), _full_spec(mod.shape), _full_spec(w_out_t.shape)]
    in_specs += [_row_spec(tm, a.shape[1]) for a in flat]
    args = [dx, y, mod, w_out_t] + flat
    if with_sink:
        nh = lse.shape[0]
        in_specs += [_rows_spec(nh, tm), _full_spec(sink.shape)]
        args += [lse, sink]
    out_shape = [sd((s, a.shape[1]), MXU) for a in flat]
    out_specs = [_row_spec(tm, a.shape[1]) for a in flat]
    for h in delta_heads:
        if h:
            out_shape.append(sd((h, s), F32))
            out_specs.append(_rows_spec(h, tm))
    out_shape += [sd((1, d), F32), sd((sum(widths), d), F32)]
    out_specs += [_full_spec((1, d)), _full_spec((sum(widths), d))]
    if with_sink:
        out_shape.append(sd((lse.shape[0], 1), F32))
        out_specs.append(_full_spec((lse.shape[0], 1)))
    return pl.pallas_call(
        body, name=name, grid=(s // tm,), out_shape=tuple(out_shape), in_specs=in_specs, out_specs=tuple(out_specs),
        compiler_params=_params(("arbitrary",)),
    )(*args)


def latent_out_backward(d_ob, o_lat, w_uv):
    s = o_lat.shape[0]
    tm = min(ROW_TILE, s)

    def body(d_ref, o_ref, uv_ref, dol_ref, dl_ref, duv_ref, prod_s):
        @pl.when(pl.program_id(0) == 0)
        def _():
            duv_ref[...] = jnp.zeros(duv_ref.shape, F32)

        for hh in range(B_HEADS):
            dh = d_ref[:, HD * hh:HD * hh + HD]
            ol = o_ref[:, B_KV_LORA * hh:B_KV_LORA * (hh + 1)]
            dol = _mm_nt(dh, uv_ref[hh])
            dol_ref[:, B_KV_LORA * hh:B_KV_LORA * (hh + 1)] = dol.astype(MXU)
            prod_s[:, B_KV_LORA * hh:B_KV_LORA * (hh + 1)] = dol * ol
            duv_ref[hh] += _mm_tn(ol, dh)
        dl_ref[...] = _group_sums_t(prod_s[...], B_KV_LORA)[0:B_HEADS, :]

    sd = jax.ShapeDtypeStruct
    return pl.pallas_call(
        body, name="latent_out_backward", grid=(s // tm,),
        out_shape=(sd(o_lat.shape, MXU), sd((B_HEADS, s), F32), sd(w_uv.shape, F32)),
        in_specs=[_row_spec(tm, d_ob.shape[1]), _row_spec(tm, o_lat.shape[1]), _full_spec(w_uv.shape)],
        out_specs=(_row_spec(tm, o_lat.shape[1]), _rows_spec(B_HEADS, tm), _full_spec(w_uv.shape)),
        scratch_shapes=[pltpu.VMEM((tm, o_lat.shape[1]), F32)],
        compiler_params=_params(("arbitrary",)),
    )(d_ob, o_lat, w_uv)


def even_prep_backward(dqa, dka, dva, dqb, dkb, dvb, qa_raw, ka_raw, cq_raw, ckv_raw,
                       qn, kn, qln, kvln, w_uq, w_uq_t, w_uk, cos_a, sin_a, cos_t, sin_t):
    s = qa_raw.shape[0]
    tm = min(ROW_TILE, s)
    qb_w = B_HEADS * (B_NOPE + B_ROPE)

    def body(dqa_ref, dka_ref, dva_ref, dqb_ref, dkb_ref, dvb_ref, qa_ref, ka_ref, cq_ref, ckv_ref,
             qn_ref, kn_ref, qln_ref, kvln_ref, uq_ref, uqt_ref, uk_ref, ca_ref, sa_ref, ct_ref, st_ref,
             pqa, pka, pva, pcq, pckv, pkr, gqn, gkn, gqln, gkvln, guq, guk, dqb_s):
        @pl.when(pl.program_id(0) == 0)
        def _():
            for r in (gqn, gkn, gqln, gkvln, guq, guk):
                r[...] = jnp.zeros(r.shape, F32)

        ca, sa, ct, st = ca_ref[...], sa_ref[...], ct_ref[...], st_ref[...]
        acc_q = jnp.zeros((1, HD), F32)
        for hh in range(A_HEADS):
            dyn = _rope_t(dqa_ref[:, HD * hh:HD * hh + HD], ca, sa, 32)
            dx, dg = _rms_bwd(dyn, qa_ref[:, HD * hh:HD * hh + HD], qn_ref[...])
            pqa[:, HD * hh:HD * hh + HD] = dx.astype(MXU)
            acc_q = acc_q + jnp.sum(dg, axis=0, keepdims=True)
        gqn[...] += acc_q
        acc_k = jnp.zeros((1, HD), F32)
        for g in range(A_KV):
            dyn = _rope_t(dka_ref[g], ca, sa, 32)
            dx, dg = _rms_bwd(dyn, ka_ref[:, HD * g:HD * g + HD], kn_ref[...])
            pka[:, HD * g:HD * g + HD] = dx.astype(MXU)
            acc_k = acc_k + jnp.sum(dg, axis=0, keepdims=True)
            pva[:, HD * g:HD * g + HD] = dva_ref[g].astype(MXU)
        gkn[...] += acc_k
        cq_raw = cq_ref[...]
        cq_n = cq_raw * _rms(cq_raw) * qln_ref[...]
        qb = _mm(cq_n, uq_ref[...])
        for hh in range(B_HEADS):
            base = (B_NOPE + B_ROPE) * hh
            dlat = dqb_ref[:, B_QK * hh:B_QK * hh + B_KV_LORA]
            dqb_s[:, base:base + B_NOPE] = _mm(dlat, uk_ref[hh])
            guk[hh] += _mm_tn(dlat, qb[:, base:base + B_NOPE])
            dqb_s[:, base + B_NOPE:base + B_NOPE + B_ROPE] = _rope_t(
                dqb_ref[:, B_QK * hh + B_KV_LORA:B_QK * (hh + 1)], ct, st, 32)
        dqb_all = dqb_s[...]
        guq[...] += _mm_tn(cq_n, dqb_all)
        dx, dg = _rms_bwd(_mm(dqb_all, uqt_ref[...]), cq_raw, qln_ref[...])
        pcq[...] = dx.astype(MXU)
        gqln[...] += jnp.sum(dg, axis=0, keepdims=True)
        dkb_sum = dkb_ref[0] + dkb_ref[1]
        dckv = dkb_sum[:, 0:B_KV_LORA] + dvb_ref[0] + dvb_ref[1]
        dx, dg = _rms_bwd(dckv, ckv_ref[...], kvln_ref[...])
        pckv[...] = dx.astype(MXU)
        gkvln[...] += jnp.sum(dg, axis=0, keepdims=True)
        pkr[...] = _rope_t(dkb_sum[:, B_KV_LORA:B_QK], ct, st, 32).astype(MXU)

    sd = jax.ShapeDtypeStruct
    args = [dqa, dka, dva, dqb, dkb, dvb, qa_raw, ka_raw, cq_raw, ckv_raw,
            qn, kn, qln, kvln, w_uq, w_uq_t, w_uk, cos_a, sin_a, cos_t, sin_t]
    in_specs = [_row_spec(tm, 512), _head_spec(A_KV, tm, HD), _head_spec(A_KV, tm, HD),
                _row_spec(tm, B_HEADS * B_QK), _head_spec(2, tm, B_QK), _head_spec(2, tm, B_KV_LORA),
                _row_spec(tm, 512), _row_spec(tm, 128), _row_spec(tm, B_Q_LORA), _row_spec(tm, B_KV_LORA),
                _full_spec(qn.shape), _full_spec(kn.shape), _full_spec(qln.shape), _full_spec(kvln.shape),
                _full_spec(w_uq.shape), _full_spec(w_uq_t.shape), _full_spec(w_uk.shape),
                _row_spec(tm, HD), _row_spec(tm, HD), _row_spec(tm, B_ROPE), _row_spec(tm, B_ROPE)]
    out_shape = (sd((s, 512), MXU), sd((s, 128), MXU), sd((s, 128), MXU), sd((s, B_Q_LORA), MXU),
                 sd((s, B_KV_LORA), MXU), sd((s, B_ROPE), MXU),
                 sd(qn.shape, F32), sd(kn.shape, F32), sd(qln.shape, F32), sd(kvln.shape, F32),
                 sd(w_uq.shape, F32), sd(w_uk.shape, F32))
    out_specs = (_row_spec(tm, 512), _row_spec(tm, 128), _row_spec(tm, 128), _row_spec(tm, B_Q_LORA),
                 _row_spec(tm, B_KV_LORA), _row_spec(tm, B_ROPE),
                 _full_spec(qn.shape), _full_spec(kn.shape), _full_spec(qln.shape), _full_spec(kvln.shape),
                 _full_spec(w_uq.shape), _full_spec(w_uk.shape))
    return pl.pallas_call(
        body, name="even_prep_backward", grid=(s // tm,), out_shape=out_shape, in_specs=in_specs, out_specs=out_specs,
        scratch_shapes=[pltpu.VMEM((tm, qb_w), F32)],
        compiler_params=_params(("arbitrary",)),
    )(*args)


def in_proj_backward(x, mod, nw, dx_out, pieces, w_in_t, name):
    s, d = x.shape
    tm = min(ROW_TILE, s)
    n_cols = w_in_t.shape[0]
    n = len(pieces)
    cols = [c for _, c in pieces]

    def body(*refs):
        x_ref, mod_ref, nw_ref, dxo_ref, wt_ref = refs[:5]
        p_refs = refs[5:5 + n]
        dx_ref, dw_ref, dv_ref, acc_ref = refs[5 + n:]
        i = pl.program_id(0)

        @pl.when(i == 0)
        def _():
            dw_ref[...] = jnp.zeros(dw_ref.shape, F32)
            acc_ref[...] = jnp.zeros(acc_ref.shape, F32)

        xn, g1, h = _modulated(x_ref[...], mod_ref, nw_ref)
        hb = h.astype(MXU)
        dh = jnp.zeros((tm, d), F32)
        for pr, (c0, c1) in zip(p_refs, cols):
            pc = pr[...].astype(MXU)
            dh = dh + jnp.dot(pc, wt_ref[c0:c1, :], preferred_element_type=F32)
            dw_ref[:, c0:c1] += _mm_tn(hb, pc)
        acc_ref[0:1, :] += jnp.sum(dh, axis=0, keepdims=True)
        acc_ref[1:2, :] += jnp.sum(dh * xn, axis=0, keepdims=True)
        dxn = dh * g1
        x = x_ref[...]
        r = _rms(x)
        dx_ref[...] = dxo_ref[...] + r * (dxn - xn * jnp.mean(dxn * xn, axis=-1, keepdims=True))

        @pl.when(i == pl.num_programs(0) - 1)
        def _():
            dg1 = acc_ref[1:2, :]
            dv_ref[0:1, :] = acc_ref[0:1, :]
            dv_ref[1:2, :] = dg1 * nw_ref[...]
            dv_ref[2:3, :] = dg1 * (1.0 + mod_ref[1:2, :])
            dv_ref[3:4, :] = jnp.zeros((1, d), F32)

    arrs = [a for a, _ in pieces]
    sd = jax.ShapeDtypeStruct
    return pl.pallas_call(
        body, name=name, grid=(s // tm,),
        out_shape=(sd((s, d), F32), sd((d, n_cols), F32), sd((4, d), F32)),
        in_specs=[_row_spec(tm, d), _full_spec(mod.shape), _full_spec(nw.shape), _row_spec(tm, d),
                  _full_spec(w_in_t.shape)] + [_row_spec(tm, a.shape[1]) for a in arrs],
        out_specs=(_row_spec(tm, d), _full_spec((d, n_cols)), _full_spec((4, d))),
        scratch_shapes=[pltpu.VMEM((8, d), F32)],
        compiler_params=_params(("arbitrary",)),
    )(x, mod, nw, dx_out, w_in_t, *arrs)


def ada_weight_grad(c_all, dmod_cols):
    d = c_all.shape[1]
    w = dmod_cols.shape[2]

    def body(c_ref, dm_ref, out_ref):
        ca = _silu(c_ref[...])
        for l in range(2):
            out_ref[l] = _mm_tn(ca, dm_ref[l])

    return pl.pallas_call(
        body, name="ada_weight_grad",
        out_shape=jax.ShapeDtypeStruct((2, d, w), F32),
        compiler_params=pltpu.CompilerParams(vmem_limit_bytes=VMEM_LIMIT),
    )(c_all, dmod_cols)


def adamw_rows(g_slots, w, m, v, name):
    n, r, lanes = g_slots.shape
    tr = max(t for t in range(16, min(r, 2048) + 1, 16) if r % t == 0) if r % 16 == 0 else r
    c1 = 1.0 - ADAM_B1 ** ADAM_STEP
    c2 = 1.0 - ADAM_B2 ** ADAM_STEP

    def body(g_ref, w_ref, m_ref, v_ref, go, do, mo, vo):
        g = g_ref[0].astype(F32)
        for k in range(1, n):
            g = g + g_ref[k].astype(F32)
        m_new = ADAM_B1 * m_ref[...] + (1.0 - ADAM_B1) * g
        v_new = ADAM_B2 * v_ref[...] + (1.0 - ADAM_B2) * (g * g)
        m_hat = m_new / c1
        v_hat = v_new / c2
        go[...] = g
        do[...] = -ADAM_LR * (m_hat / (jnp.sqrt(v_hat) + ADAM_EPS) + ADAM_WD * w_ref[...])
        mo[...] = m_new
        vo[...] = v_new

    row = pl.BlockSpec((tr, lanes), lambda i: (i, 0))
    sd = jax.ShapeDtypeStruct((r, lanes), F32)
    return pl.pallas_call(
        body, name=name, grid=(r // tr,), out_shape=(sd, sd, sd, sd),
        in_specs=[pl.BlockSpec((n, tr, lanes), lambda i: (0, i, 0)), row, row, row],
        out_specs=(row, row, row, row),
        compiler_params=_params(("parallel",)),
    )(g_slots, w, m, v)


def _rope_tables(s):
    def cs(pos, dim):
        inv = ROPE_THETA ** (-np.arange(0, dim, 2, dtype=np.float32) / dim)
        ang = pos.astype(np.float32)[:, None] * inv.astype(np.float32)[None, :]
        return np.cos(ang), np.sin(ang)

    rows = s // GRID_W
    row = np.repeat(np.arange(rows), GRID_W)
    col = np.tile(np.arange(GRID_W), rows)
    cr, sr = cs(row, HD // 2)
    cc, sc = cs(col, HD // 2)
    ct, st = cs(np.arange(s), B_ROPE)
    tables = (np.concatenate([cr, cr, cc, cc], axis=-1), np.concatenate([-sr, sr, -sc, sc], axis=-1),
              np.concatenate([ct, ct], axis=-1), np.concatenate([-st, st], axis=-1))
    return tuple(jnp.asarray(t, F32) for t in tables)


def _rows128(a):
    return a.reshape(-1, 128)


def _even_cols_to_kernel(w):
    return jnp.concatenate([w[:, :1664], w[:, 1696:], w[:, 1664:1696]], axis=1)


def _even_cols_to_reference(w):
    return jnp.concatenate([w[:, :1664], w[:, 2176:], w[:, 1664:2176]], axis=1)


def _col_blocks(w):
    d, n8 = w.shape
    n = n8 // N_DEV
    return w.reshape(d, N_DEV, n).transpose(1, 0, 2).reshape(N_DEV, d * n // 128, 128)


def _from_col_blocks(p, d):
    n = p.shape[1] * 128 // d
    return p.reshape(N_DEV, d, n).transpose(1, 0, 2).reshape(d, N_DEV * n)


def _pad_rows(flat, rows):
    return jnp.pad(flat, (0, rows * 128 - flat.shape[0])).reshape(rows, 128)


def kernel(x, c, norm_w, ada_w, ada_b, even_w_in, a_q_norm, a_k_norm, b_q_lora_norm, b_kv_lora_norm, b_w_uq, b_w_uk, b_w_uv, even_w_out, odd_w_in, c_sink, odd_w_out, final_norm, loss_target, m_norm_w, m_ada_w, m_ada_b, m_even_w_in, m_a_q_norm, m_a_k_norm, m_b_q_lora_norm, m_b_kv_lora_norm, m_b_w_uq, m_b_w_uk, m_b_w_uv, m_even_w_out, m_odd_w_in, m_c_sink, m_odd_w_out, m_final_norm, v_norm_w, v_ada_w, v_ada_b, v_even_w_in, v_a_q_norm, v_a_k_norm, v_b_q_lora_norm, v_b_kv_lora_norm, v_b_w_uq, v_b_w_uk, v_b_w_uv, v_even_w_out, v_odd_w_in, v_c_sink, v_odd_w_out, v_final_norm):
    s, d = x.shape[1], x.shape[2]
    x0 = x[0]
    target = loss_target[0]
    me_flat = 4 * lax.axis_index("x") + 2 * lax.axis_index("y") + lax.axis_index("c")

    cat = lambda arrs: jnp.concatenate([_rows128(a) for a in arrs], axis=0)

    def split_rows(p, like):
        out, o = [], 0
        for a in like:
            r = a.size // 128
            out.append(p[..., o:o + r, :])
            o += r
        return out

    first_w = [even_w_in, b_w_uq]
    later_w = [odd_w_in, even_w_out, odd_w_out]
    g_first = all_gather_rows(cat(first_w), MXU, "gather_first_weights")
    seg_in_e, seg_uq = split_rows(g_first.reshape(N_DEV, -1, 128), first_w)
    w_in_e = _even_cols_to_kernel(_from_col_blocks(seg_in_e, d))
    w_uq = _from_col_blocks(seg_uq, B_Q_LORA)
    later_exchange = Exchange(cat(later_w).astype(MXU), scatter=False)
    w_uk = jnp.transpose(b_w_uk[0], (1, 0, 2)).astype(MXU)
    w_uv = jnp.transpose(b_w_uv[0], (1, 0, 2)).astype(MXU)

    wcols = ada_w.shape[2]
    bias_cols = lax.dynamic_slice_in_dim(ada_b.reshape(2, N_DEV, wcols), me_flat, 1, axis=1)
    call, modp = ada_forward(jnp.broadcast_to(c, (8, d)), ada_w, bias_cols)
    c_all = call[:, 0, :]
    mod = jnp.transpose(modp[:, :, 0, :], (1, 0, 2)).reshape(2, 3, d)
    mod_e, mod_o = mod[0], mod[1]
    nw_e, nw_o = norm_w[0:1], norm_w[1:2]

    cos_a, sin_a, cos_t, sin_t = _rope_tables(s)
    slopes = (2.0 ** (-8.0 * jnp.arange(1, C_HEADS + 1, dtype=F32) / C_HEADS)).reshape(C_HEADS, 1, 1)
    sink2 = c_sink.reshape(C_HEADS, 1, 1) * LOG2E

    (qa, ka, va, qb, kb, kat, vat, kbt, qa_raw, ka_raw, cq_raw, ckv_raw, ga, gb) = even_in_forward(
        x0, mod_e, nw_e, w_in_e, a_q_norm, a_k_norm, b_q_lora_norm, b_kv_lora_norm, w_uq, w_uk,
        cos_a, sin_a, cos_t, sin_t)
    tk_dense = min(512, s)
    tq_dense = min(256, s)
    fwd_sub = min(8, s // tk_dense)
    bwd_sub = min(4, s // tq_dense)
    oa, lse_a, g_later = flash_forward(qa, ka, vat, scale=HD ** -0.5, dv=HD, tq=tq_dense, tk=tk_dense, nsub=fwd_sub,
                                       name="attn_a_fwd", exchange=later_exchange)
    seg_in_o, seg_out_e, seg_out_o = split_rows(g_later, later_w)
    w_in_o = _from_col_blocks(seg_in_o, d)
    w_out_e = seg_out_e.reshape(1024, d)
    w_out_o = seg_out_o.reshape(1024, d)
    scale_b = (B_NOPE + B_ROPE) ** -0.5
    o_lat, lse_b = flash_forward(qb, kb, kbt, scale=scale_b, dv=B_KV_LORA, tq=min(128, s), tk=tk_dense, nsub=fwd_sub,
                                 name="attn_b_fwd")
    ob = latent_out_forward(o_lat, w_uv)
    x1, y_e = mixer_out_forward(x0, mod_e, [(oa, ga), (ob, gb)], w_out_e, "even_out_fwd")

    qc, kc, vc, kct, vct, gc = odd_in_forward(x1, mod_o, nw_o, w_in_o)
    oc, lse_c = window_forward(qc, kc, vct, sink2, slopes, "attn_c_fwd")
    x2, y_o = mixer_out_forward(x1, mod_o, [(oc, gc)], w_out_o, "odd_out_fwd")

    loss_lanes, dx2, d_final = loss_head(x2, target, final_norm.reshape(1, d))
    loss_part = (0.5 / d) * jnp.sum(loss_lanes)

    doc, dgc, delta_c, dgate_o, dw_out_o, dsink = mixer_out_backward(
        dx2, y_o, mod_o, [(oc, gc)], w_out_o.T, [C_HEADS], "odd_out_bwd", lse=lse_c.reshape(C_HEADS, s),
        sink=sink2.reshape(C_HEADS, 1))
    rows3 = lambda t: t.reshape(t.shape[0], 1, s)
    dqc, dkc, dvc = window_backward(qc, kc, kct, vc, doc, lse_c, rows3(delta_c), slopes, "attn_c_bwd")
    to_rows = lambda t: jnp.transpose(t, (1, 0, 2)).reshape(s, -1)
    dx1, dw_in_o, dvec_o = in_proj_backward(
        x1, mod_o, nw_o, dx2, [(dqc, O_Q), (to_rows(dkc), O_K), (to_rows(dvc), O_V), (dgc, O_G)], w_in_o.T,
        "odd_in_bwd")

    doa, dga, dob, dgb, delta_a, dgate_e, dw_out_e = mixer_out_backward(
        dx1, y_e, mod_e, [(oa, ga), (ob, gb)], w_out_e.T, [A_HEADS, 0], "even_out_bwd")
    d_olat, delta_b, dw_uv = latent_out_backward(dob, o_lat, w_uv)
    row_blocks = lambda w: w.astype(MXU).reshape(N_DEV, -1, 128)
    scatter_odd = Exchange(jnp.concatenate([_col_blocks(dw_in_o.astype(MXU)), row_blocks(dw_out_o)], axis=1), True)
    scatter_out_e = Exchange(row_blocks(dw_out_e), True)
    dqb, dkb, dvb, landed_odd = flash_backward(
        qb, kb, kbt, None, d_olat, lse_b, rows3(delta_b), scale=scale_b, dv=B_KV_LORA,
        tq=tq_dense, tk=tk_dense, nsub=bwd_sub, gq=2, name="attn_b_bwd", exchange=scatter_odd)
    dqa, dka, dva, landed_out_e = flash_backward(
        qa, ka, kat, va, doa, lse_a, rows3(delta_a), scale=HD ** -0.5, dv=HD,
        tq=tq_dense, tk=tk_dense, nsub=bwd_sub, gq=A_KV, name="attn_a_bwd", exchange=scatter_out_e)
    (pqa, pka, pva, pcq, pckv, pkr, g_qn, g_kn, g_qln, g_kvln, dw_uq, dw_uk) = even_prep_backward(
        dqa, dka, dva, dqb, dkb, dvb, qa_raw, ka_raw, cq_raw, ckv_raw,
        a_q_norm, a_k_norm, b_q_lora_norm, b_kv_lora_norm, w_uq, w_uq.T, w_uk, cos_a, sin_a, cos_t, sin_t)
    dx0, dw_in_e, dvec_e = in_proj_backward(
        x0, mod_e, nw_e, dx1,
        [(pqa, E_QA), (pka, E_KA), (pva, E_VA), (dga, E_GA), (pcq, E_CQ), (pckv, E_CKV), (dgb, E_GB), (pkr, E_KR)],
        w_in_e.T, "even_in_bwd")

    dmod = jnp.stack([jnp.concatenate([dvec_e[0], dvec_e[1], dgate_e[0]]),
                      jnp.concatenate([dvec_o[0], dvec_o[1], dgate_o[0]])])
    d_norm_w = jnp.stack([dvec_e[2], dvec_o[2]])
    small_names = ["norm_w", "ada_b", "a_q_norm", "a_k_norm", "b_q_lora_norm", "b_kv_lora_norm", "b_w_uk", "b_w_uv",
                   "c_sink", "final_norm"]
    small_w = [norm_w, ada_b, a_q_norm, a_k_norm, b_q_lora_norm, b_kv_lora_norm, b_w_uk, b_w_uv, c_sink, final_norm]
    small_m = [m_norm_w, m_ada_b, m_a_q_norm, m_a_k_norm, m_b_q_lora_norm, m_b_kv_lora_norm, m_b_w_uk, m_b_w_uv,
               m_c_sink, m_final_norm]
    small_v = [v_norm_w, v_ada_b, v_a_q_norm, v_a_k_norm, v_b_q_lora_norm, v_b_kv_lora_norm, v_b_w_uk, v_b_w_uv,
               v_c_sink, v_final_norm]
    small_g = [d_norm_w, dmod, g_qn, g_kn, g_qln, g_kvln, jnp.transpose(dw_uk, (1, 0, 2)), jnp.transpose(dw_uv, (1, 0, 2)),
               dsink, d_final]
    sizes = [w.size for w in small_w]
    n_small = sum(sizes)
    r_small = -(-(n_small + 1) // (128 * 8)) * 8
    flat_pack = lambda arrs: _pad_rows(jnp.concatenate([a.reshape(-1) for a in arrs]), r_small)
    g_small_all = all_gather_rows(flat_pack(small_g + [loss_part]), F32, "gather_small_grads").reshape(
        N_DEV, r_small, 128)
    sm = adamw_rows(g_small_all, flat_pack(small_w), flat_pack(small_m), flat_pack(small_v), "adamw_small")
    loss = sm[0].reshape(-1)[n_small]

    def unpack_small(packed):
        flat = packed.reshape(-1)
        out, o = {}, 0
        for nm, w, sz in zip(small_names, small_w, sizes):
            out[nm] = flat[o:o + sz].reshape(w.shape)
            o += sz
        return out

    sm = [unpack_small(p) for p in sm]

    dmod_all = g_small_all.reshape(N_DEV, -1)[:, sizes[0]:sizes[0] + sizes[1]].reshape(N_DEV, 2, N_DEV, wcols)
    dmod_cols = lax.dynamic_slice_in_dim(dmod_all, me_flat, 1, axis=2)[:, :, 0, :]
    pad16 = lambda a: jnp.concatenate([a, jnp.zeros_like(a)], axis=0)
    g_ada_w = ada_weight_grad(pad16(c_all), jnp.transpose(pad16(dmod_cols), (1, 0, 2)))
    ada = adamw_rows(_rows128(g_ada_w)[None], _rows128(ada_w), _rows128(m_ada_w), _rows128(v_ada_w), "adamw_ada_w")
    ada = [p.reshape(ada_w.shape) for p in ada]

    scatter_in_e = Exchange(jnp.concatenate([_col_blocks(_even_cols_to_reference(dw_in_e).astype(MXU)),
                                             _col_blocks(dw_uq.astype(MXU))], axis=1), True)
    landed_in_e = exchange_blocks(scatter_in_e, "scatter_first_weight_grads")
    bg = [{}, {}, {}, {}]
    for landed, names, ws, ms, vs in (
            (landed_odd, ["odd_w_in", "odd_w_out"], [odd_w_in, odd_w_out], [m_odd_w_in, m_odd_w_out],
             [v_odd_w_in, v_odd_w_out]),
            (landed_out_e, ["even_w_out"], [even_w_out], [m_even_w_out], [v_even_w_out]),
            (landed_in_e, ["even_w_in", "b_w_uq"], [even_w_in, b_w_uq], [m_even_w_in, m_b_w_uq],
             [v_even_w_in, v_b_w_uq])):
        res = adamw_rows(landed, cat(ws), cat(ms), cat(vs), "adamw_" + names[0])
        for kind, p in enumerate(res):
            for nm, w, piece in zip(names, ws, split_rows(p, ws)):
                bg[kind][nm] = piece.reshape(w.shape)
    big_names = ["even_w_in", "odd_w_in", "even_w_out", "odd_w_out", "b_w_uq"]

    order = ["norm_w", "ada_w", "ada_b", "even_w_in", "a_q_norm", "a_k_norm", "b_q_lora_norm", "b_kv_lora_norm",
             "b_w_uq", "b_w_uk", "b_w_uv", "even_w_out", "odd_w_in", "c_sink", "odd_w_out", "final_norm"]

    def pick(kind):
        out = []
        for nm in order:
            if nm == "ada_w":
                out.append(ada[kind])
            elif nm in big_names:
                out.append(bg[kind][nm])
            else:
                out.append(sm[kind][nm])
        return out

    return (loss, dx0[None], *pick(0), *pick(1), *pick(2), *pick(3))
```

```python
import functools

import jax
import jax.numpy as jnp
import numpy as np
from jax import lax
from jax.experimental import pallas as pl
from jax.experimental.pallas import tpu as pltpu

F32 = jnp.float32
MXU = jnp.bfloat16
EPS = 1e-6
ROPE_THETA = 10000.0
GRID_W = 64
HD = 64
N_DEV = 8
MESH_AXES = ("x", "y", "c")

A_HEADS, A_KV = 8, 2
B_HEADS, B_NOPE, B_ROPE, B_Q_LORA, B_KV_LORA = 8, 64, 32, 256, 128
B_QK = B_KV_LORA + B_ROPE
C_HEADS, C_KV = 16, 4
WINDOW = 128

ADAM_LR, ADAM_B1, ADAM_B2, ADAM_EPS, ADAM_WD, ADAM_STEP = 0.001, 0.9, 0.999, 1e-08, 0.01, 10

ROW_TILE = 256
ADAM_TILE = 2048 * 128
VMEM_LIMIT = 56 * 1024 * 1024

E_QA, E_KA, E_VA, E_GA, E_CQ, E_CKV, E_GB, E_KR = (
    (0, 512), (512, 640), (640, 768), (768, 1280), (1280, 1536), (1536, 1664), (1664, 2176), (2176, 2208))
EVEN_IN = 2208
O_Q, O_K, O_V, O_G = (0, 1024), (1024, 1280), (1280, 1536), (1536, 2560)
ODD_IN = 2560


def _mm(a, b):
    return jnp.dot(a.astype(MXU), b.astype(MXU), preferred_element_type=F32)


def _mm_nt(a, b):
    return lax.dot_general(a.astype(MXU), b.astype(MXU), (((1,), (1,)), ((), ())), preferred_element_type=F32)


def _mm_tn(a, b):
    return lax.dot_general(a.astype(MXU), b.astype(MXU), (((0,), (0,)), ((), ())), preferred_element_type=F32)


def _group_sums_t(prod, group):
    tm, w = prod.shape
    sel = (lax.broadcasted_iota(jnp.int32, (w, 128), 0) // group
           == lax.broadcasted_iota(jnp.int32, (w, 128), 1)).astype(MXU)
    hi = prod.astype(MXU)
    lo = prod - hi.astype(F32)
    return (_mm(hi, sel) + _mm(lo, sel)).T


def _sigmoid(z):
    return 1.0 / (1.0 + jnp.exp(-z))


def _silu(z):
    return z * _sigmoid(z)


def _rms(x):
    return lax.rsqrt(jnp.mean(x * x, axis=-1, keepdims=True) + EPS)


def _swap_halves(y, group):
    n = y.shape[-1]
    half = group // 2
    fwd = pltpu.roll(y, half, 1)
    if n == group:
        return fwd
    back = pltpu.roll(y, n - half, 1)
    lane = lax.broadcasted_iota(jnp.int32, y.shape, 1)
    return jnp.where((lane % group) < half, back, fwd)


def _rope(y, cos, sin, group):
    return y * cos + _swap_halves(y, group) * sin


def _rope_t(d, cos, sin, group):
    return d * cos - _swap_halves(d, group) * sin


def _rms_bwd(dy, x, g):
    r = _rms(x)
    xhat = x * r
    dxhat = dy * g
    dx = r * (dxhat - xhat * jnp.mean(dxhat * xhat, axis=-1, keepdims=True))
    return dx, dy * xhat


def _params(sem, vmem=VMEM_LIMIT):
    return pltpu.CompilerParams(dimension_semantics=sem, vmem_limit_bytes=vmem)


def _row_spec(tm, w):
    return pl.BlockSpec((tm, w), lambda i: (i, 0))


def _full_spec(shape):
    nd = len(shape)
    return pl.BlockSpec(shape, lambda i: (0,) * nd)


def _head_spec(h, tm, w):
    return pl.BlockSpec((h, tm, w), lambda i: (0, i, 0))


def _headt_spec(h, w, tm):
    return pl.BlockSpec((h, w, tm), lambda i: (0, 0, i))


def _rows_spec(h, tm):
    return pl.BlockSpec((h, tm), lambda i: (0, i))


def _me():
    return lax.axis_index("x"), lax.axis_index("y"), lax.axis_index("c")


def _flat(p):
    return 4 * p[0] + 2 * p[1] + p[2]


def _peer(me, k):
    x, y, c = me
    return (1 - x if k & 4 else x, 1 - y if k & 2 else y, 1 - c if k & 1 else c)


MESH_ID = pl.DeviceIdType.MESH


def all_gather_slots(shards, name):
    n = len(shards)

    def body(*refs):
        x_refs, out_refs = refs[:n], refs[n:2 * n]
        send_sems, recv_sems, local_sems = refs[2 * n:]
        me = _me()
        x, y, c = me
        sibling = (x, y, 1 - c)
        chips = [(1 - x, y), (x, 1 - y), (1 - x, 1 - y)]

        def copy(a, k, block, to, src=None):
            slot = out_refs[a].at[_flat(block)]
            return pltpu.make_async_remote_copy(
                src_ref=slot if src is None else src, dst_ref=slot, send_sem=send_sems.at[7 * a + k],
                recv_sem=recv_sems.at[7 * a + k], device_id=to, device_id_type=MESH_ID)

        mine = [pltpu.make_async_copy(x_refs[a], out_refs[a].at[_flat(me)], local_sems.at[a]) for a in range(n)]
        for cp in mine:
            cp.start()
        first = [copy(a, 0, me, sibling, src=x_refs[a]) for a in range(n)]
        first += [copy(a, 1 + j, me, (*chip, c), src=x_refs[a]) for a in range(n) for j, chip in enumerate(chips)]
        for cp in first:
            cp.start()
        passed = []
        for a in range(n):
            for j, chip in enumerate(chips):
                copy(a, 1 + j, (*chip, c), me).wait_recv()
                passed.append(copy(a, 4 + j, (*chip, c), sibling))
                passed[-1].start()
        for a in range(n):
            copy(a, 0, sibling, me).wait_recv()
            for j, chip in enumerate(chips):
                copy(a, 4 + j, (*chip, 1 - c), me).wait_recv()
        for cp in first + passed:
            cp.wait_send()
        for cp in mine:
            cp.wait()

    vm = pl.BlockSpec(memory_space=pltpu.VMEM)
    return pl.pallas_call(
        body, name=name,
        out_shape=tuple(jax.ShapeDtypeStruct((N_DEV,) + a.shape, a.dtype) for a in shards),
        in_specs=[vm] * n, out_specs=(vm,) * n,
        scratch_shapes=[pltpu.SemaphoreType.DMA((7 * n,)), pltpu.SemaphoreType.DMA((7 * n,)),
                        pltpu.SemaphoreType.DMA((n,))],
        compiler_params=pltpu.CompilerParams(vmem_limit_bytes=VMEM_LIMIT),
    )(*shards)


class Exchange:
    HBM = pl.BlockSpec(memory_space=pl.ANY)

    def __init__(self, srcs, scatter):
        self.srcs = list(srcs)
        self.scatter = scatter
        self.n = len(self.srcs)
        self.land_shapes = tuple(jax.ShapeDtypeStruct((N_DEV,) + tuple(a.shape[-2:]), a.dtype) for a in self.srcs)
        self.in_specs = [Exchange.HBM] * self.n
        self.out_specs = (Exchange.HBM,) * self.n
        self.sems = [pltpu.SemaphoreType.DMA((N_DEV - 1,)), pltpu.SemaphoreType.DMA((N_DEV - 1,)),
                     pltpu.SemaphoreType.DMA] * self.n

    def _copies(self, src_refs, land_refs, sems):
        me = _me()
        mi = _flat(me)
        local, sends, recvs = [], [], []
        for a, (src_ref, land_ref) in enumerate(zip(src_refs, land_refs)):
            send_sems, recv_sems, local_sem = sems[3 * a:3 * a + 3]
            pick = (lambda p, r=src_ref: r.at[_flat(p)]) if self.scatter else (lambda p, r=src_ref: r)
            local.append(pltpu.make_async_copy(pick(me), land_ref.at[mi], local_sem))
            for k in range(1, N_DEV):
                peer = _peer(me, k)
                pair = dict(send_sem=send_sems.at[k - 1], recv_sem=recv_sems.at[k - 1], device_id=peer,
                            device_id_type=MESH_ID)
                sends.append(pltpu.make_async_remote_copy(src_ref=pick(peer), dst_ref=land_ref.at[mi], **pair))
                recvs.append(pltpu.make_async_remote_copy(src_ref=pick(peer), dst_ref=land_ref.at[_flat(peer)],
                                                          **pair))
        return local, sends, recvs

    def start(self, src_refs, land_refs, sems):
        local, sends, _ = self._copies(src_refs, land_refs, sems)
        for cp in local + sends:
            cp.start()

    def wait(self, src_refs, land_refs, sems):
        local, sends, recvs = self._copies(src_refs, land_refs, sems)
        for cp in recvs:
            cp.wait_recv()
        for cp in sends:
            cp.wait_send()
        for cp in local:
            cp.wait()


def exchange_blocks(ex, name):
    def body(*refs):
        src_refs, land_refs, sems = refs[:ex.n], refs[ex.n:2 * ex.n], refs[2 * ex.n:]
        ex.start(src_refs, land_refs, sems)
        ex.wait(src_refs, land_refs, sems)

    return pl.pallas_call(
        body, name=name, out_shape=ex.land_shapes, in_specs=ex.in_specs, out_specs=ex.out_specs,
        scratch_shapes=list(ex.sems),
    )(*ex.srcs)


def ada_forward(c8, ada_w, bias_cols):
    d = c8.shape[1]
    w = ada_w.shape[2]

    def body(c_ref, w_ref, b_ref, call_ref, modp_ref, part_ref, s1, r1, s2, r2):
        me = _me()
        mi = _flat(me)
        call_ref[mi] = c_ref[...]
        gather = []
        for k in range(1, N_DEV):
            gather.append(pltpu.make_async_remote_copy(
                src_ref=c_ref, dst_ref=call_ref.at[mi], send_sem=s1.at[k - 1], recv_sem=r1.at[k - 1],
                device_id=_peer(me, k), device_id_type=MESH_ID))
        for cp in gather:
            cp.start()
        for k in range(1, N_DEV):
            pltpu.make_async_remote_copy(
                src_ref=c_ref, dst_ref=call_ref.at[_flat(_peer(me, k))], send_sem=s1.at[k - 1],
                recv_sem=r1.at[k - 1], device_id=_peer(me, k), device_id_type=MESH_ID).wait_recv()
        ca = _silu(call_ref[...].reshape(N_DEV * 8, d))
        for l in range(2):
            part = _mm(ca, w_ref[l]) + b_ref[l]
            for b in range(N_DEV):
                part_ref[b, l] = part[8 * b:8 * b + 8, :]
        modp_ref[mi] = part_ref[mi]
        spread = []
        for k in range(1, N_DEV):
            peer = _peer(me, k)
            spread.append(pltpu.make_async_remote_copy(
                src_ref=part_ref.at[_flat(peer)], dst_ref=modp_ref.at[mi], send_sem=s2.at[k - 1],
                recv_sem=r2.at[k - 1], device_id=peer, device_id_type=MESH_ID))
        for cp in spread:
            cp.start()
        for k in range(1, N_DEV):
            pi = _flat(_peer(me, k))
            pltpu.make_async_remote_copy(
                src_ref=part_ref.at[pi], dst_ref=modp_ref.at[pi], send_sem=s2.at[k - 1],
                recv_sem=r2.at[k - 1], device_id=_peer(me, k), device_id_type=MESH_ID).wait_recv()
        for cp in gather + spread:
            cp.wait_send()

    vm = pl.BlockSpec(memory_space=pltpu.VMEM)
    return pl.pallas_call(
        body, name="ada_forward",
        out_shape=(jax.ShapeDtypeStruct((N_DEV, 8, d), F32), jax.ShapeDtypeStruct((N_DEV, 2, 8, w), F32)),
        in_specs=[vm, vm, vm], out_specs=(vm, vm),
        scratch_shapes=[pltpu.VMEM((N_DEV, 2, 8, w), F32)] + [pltpu.SemaphoreType.DMA((7,))] * 4,
        compiler_params=pltpu.CompilerParams(vmem_limit_bytes=VMEM_LIMIT),
    )(c8, ada_w, bias_cols)


def _modulated(x, mod_ref, nw_ref):
    xn = x * _rms(x)
    g1 = nw_ref[...] * (1.0 + mod_ref[1:2, :])
    return xn, g1, xn * g1 + mod_ref[0:1, :]


def even_in_forward(x, mod, nw, w_in, qn, kn, qln, kvln, w_uq, w_uk, cos_a, sin_a, cos_t, sin_t):
    s, d = x.shape
    tm = min(ROW_TILE, s)

    def body(x_ref, mod_ref, nw_ref, w_ref, qn_ref, kn_ref, qln_ref, kvln_ref, uq_ref, uk_ref,
             ca_ref, sa_ref, ct_ref, st_ref,
             qa_o, ka_o, va_o, qb_o, kb_o, kat_o, vat_o, kbt_o, qa_raw_o, ka_raw_o, cq_raw_o, ckv_raw_o, ga_o, gb_o):
        _, _, h = _modulated(x_ref[...], mod_ref, nw_ref)
        h = h.astype(MXU)

        def proj(cols):
            return _mm_nt(h, w_ref[cols[0]:cols[1], :])

        ca, sa, ct, st = ca_ref[...], sa_ref[...], ct_ref[...], st_ref[...]
        qa = proj(E_QA)
        qa_raw_o[...] = qa
        for hh in range(A_HEADS):
            xh = qa[:, HD * hh:HD * hh + HD]
            qa_o[hh] = _rope(xh * _rms(xh) * qn_ref[...], ca, sa, 32).astype(MXU)
        ka = proj(E_KA)
        ka_raw_o[...] = ka
        va = proj(E_VA)
        for g in range(A_KV):
            xh = ka[:, HD * g:HD * g + HD]
            kr = _rope(xh * _rms(xh) * kn_ref[...], ca, sa, 32)
            vh = va[:, HD * g:HD * g + HD]
            ka_o[g] = kr.astype(MXU)
            va_o[g] = vh.astype(MXU)
            kat_o[g] = kr.T.astype(MXU)
            vat_o[g] = vh.T.astype(MXU)
        ga_o[...] = proj(E_GA)
        gb_o[...] = proj(E_GB)
        cq = proj(E_CQ)
        cq_raw_o[...] = cq
        qb = _mm_nt(cq * _rms(cq) * qln_ref[...], uq_ref[...])
        for hh in range(B_HEADS):
            base = (B_NOPE + B_ROPE) * hh
            qb_o[hh, :, 0:B_KV_LORA] = _mm_nt(qb[:, base:base + B_NOPE], uk_ref[hh]).astype(MXU)
            qb_o[hh, :, B_KV_LORA:B_QK] = _rope(qb[:, base + B_NOPE:base + B_NOPE + B_ROPE], ct, st, 32).astype(MXU)
        ckv = proj(E_CKV)
        ckv_raw_o[...] = ckv
        ckv_n = ckv * _rms(ckv) * kvln_ref[...]
        k_rope = _rope(proj(E_KR), ct, st, 32)
        kb_o[0, :, 0:B_KV_LORA] = ckv_n.astype(MXU)
        kb_o[0, :, B_KV_LORA:B_QK] = k_rope.astype(MXU)
        kbt_o[0, 0:B_KV_LORA, :] = ckv_n.T.astype(MXU)
        kbt_o[0, B_KV_LORA:B_QK, :] = k_rope.T.astype(MXU)

    sd = jax.ShapeDtypeStruct
    outs = (sd((A_HEADS, s, HD), MXU), sd((A_KV, s, HD), MXU), sd((A_KV, s, HD), MXU),
            sd((B_HEADS, s, B_QK), MXU), sd((1, s, B_QK), MXU),
            sd((A_KV, HD, s), MXU), sd((A_KV, HD, s), MXU), sd((1, B_QK, s), MXU),
            sd((s, 512), F32), sd((s, 128), F32), sd((s, B_Q_LORA), F32), sd((s, B_KV_LORA), F32),
            sd((s, 512), F32), sd((s, 512), F32))
    out_specs = (_head_spec(A_HEADS, tm, HD), _head_spec(A_KV, tm, HD), _head_spec(A_KV, tm, HD),
                 _head_spec(B_HEADS, tm, B_QK), _head_spec(1, tm, B_QK),
                 _headt_spec(A_KV, HD, tm), _headt_spec(A_KV, HD, tm), _headt_spec(1, B_QK, tm),
                 _row_spec(tm, 512), _row_spec(tm, 128), _row_spec(tm, B_Q_LORA), _row_spec(tm, B_KV_LORA),
                 _row_spec(tm, 512), _row_spec(tm, 512))
    in_specs = [_row_spec(tm, d), _full_spec(mod.shape), _full_spec(nw.shape), _full_spec(w_in.shape),
                _full_spec(qn.shape), _full_spec(kn.shape), _full_spec(qln.shape), _full_spec(kvln.shape),
                _full_spec(w_uq.shape), _full_spec(w_uk.shape),
                _row_spec(tm, HD), _row_spec(tm, HD), _row_spec(tm, B_ROPE), _row_spec(tm, B_ROPE)]
    return pl.pallas_call(
        body, name="even_in_forward", grid=(s // tm,), out_shape=outs, in_specs=in_specs, out_specs=out_specs,
        compiler_params=_params(("parallel",)),
    )(x, mod, nw, w_in, qn, kn, qln, kvln, w_uq, w_uk, cos_a, sin_a, cos_t, sin_t)


def odd_in_forward(x, mod, nw, w_in):
    s, d = x.shape
    tm = min(ROW_TILE, s)

    def body(x_ref, mod_ref, nw_ref, w_ref, q_o, k_o, v_o, kt_o, vt_o, g_o):
        _, _, h = _modulated(x_ref[...], mod_ref, nw_ref)
        h = h.astype(MXU)

        def proj(cols):
            return _mm_nt(h, w_ref[cols[0]:cols[1], :])

        q = proj(O_Q)
        for hh in range(C_HEADS):
            q_o[hh] = q[:, HD * hh:HD * hh + HD].astype(MXU)
        k = proj(O_K)
        v = proj(O_V)
        for g in range(C_KV):
            kh = k[:, HD * g:HD * g + HD]
            vh = v[:, HD * g:HD * g + HD]
            k_o[g] = kh.astype(MXU)
            v_o[g] = vh.astype(MXU)
            kt_o[g] = kh.T.astype(MXU)
            vt_o[g] = vh.T.astype(MXU)
        g_o[...] = proj(O_G)

    sd = jax.ShapeDtypeStruct
    return pl.pallas_call(
        body, name="odd_in_forward", grid=(s // tm,),
        out_shape=(sd((C_HEADS, s, HD), MXU), sd((C_KV, s, HD), MXU), sd((C_KV, s, HD), MXU),
                   sd((C_KV, HD, s), MXU), sd((C_KV, HD, s), MXU), sd((s, 1024), F32)),
        in_specs=[_row_spec(tm, d), _full_spec(mod.shape), _full_spec(nw.shape), _full_spec(w_in.shape)],
        out_specs=(_head_spec(C_HEADS, tm, HD), _head_spec(C_KV, tm, HD), _head_spec(C_KV, tm, HD),
                   _headt_spec(C_KV, HD, tm), _headt_spec(C_KV, HD, tm), _row_spec(tm, 1024)),
        compiler_params=_params(("parallel",)),
    )(x, mod, nw, w_in)


def latent_out_forward(o_lat, w_uv):
    s = o_lat.shape[0]
    tm = min(ROW_TILE, s)

    def body(o_ref, uv_ref, out_ref):
        for hh in range(B_HEADS):
            out_ref[:, HD * hh:HD * hh + HD] = _mm(o_ref[:, B_KV_LORA * hh:B_KV_LORA * (hh + 1)], uv_ref[hh])

    return pl.pallas_call(
        body, name="latent_out_forward", grid=(s // tm,),
        out_shape=jax.ShapeDtypeStruct((s, B_HEADS * HD), F32),
        in_specs=[_row_spec(tm, o_lat.shape[1]), _full_spec(w_uv.shape)],
        out_specs=_row_spec(tm, B_HEADS * HD),
        compiler_params=_params(("parallel",)),
    )(o_lat, w_uv)


def mixer_out_forward(x, mod, pairs, w_out, name):
    s, d = x.shape
    tm = min(ROW_TILE, s)
    n = len(pairs)
    widths = [o.shape[1] for o, _ in pairs]

    def body(*refs):
        x_ref, mod_ref, w_ref = refs[:3]
        pr = refs[3:3 + 2 * n]
        xo_ref, y_ref = refs[3 + 2 * n:]
        y = jnp.zeros((tm, d), F32)
        r0 = 0
        for i in range(n):
            mix = pr[2 * i][...] * _silu(pr[2 * i + 1][...])
            y = y + _mm(mix, w_ref[r0:r0 + widths[i], :])
            r0 += widths[i]
        y_ref[...] = y
        xo_ref[...] = x_ref[...] + mod_ref[2:3, :] * y

    flat = [a for p in pairs for a in p]
    sd = jax.ShapeDtypeStruct
    return pl.pallas_call(
        body, name=name, grid=(s // tm,),
        out_shape=(sd((s, d), F32), sd((s, d), F32)),
        in_specs=[_row_spec(tm, d), _full_spec(mod.shape), _full_spec(w_out.shape)]
        + [_row_spec(tm, a.shape[1]) for a in flat],
        out_specs=(_row_spec(tm, d), _row_spec(tm, d)),
        compiler_params=_params(("parallel",)),
    )(x, mod, w_out, *flat)


LOG2E = 1.4426950408889634
ONES_ROWS = 16


def _col_max8(s3):
    m8 = jnp.max(s3, axis=0)
    return jnp.broadcast_to(jnp.max(m8, axis=0, keepdims=True), m8.shape)


def _with_ones(vt, n):
    return jnp.concatenate([vt, jnp.ones((ONES_ROWS, n), vt.dtype)], axis=0)


def _grid_edges(grid):
    ids = [pl.program_id(a) for a in range(len(grid))]
    first = functools.reduce(jnp.logical_and, [i == 0 for i in ids])
    last = functools.reduce(jnp.logical_and, [i == n - 1 for i, n in zip(ids, grid)])
    return first, last


def flash_forward(q, k, vt, *, scale, dv, tq, tk, nsub, name, exchange=None):
    hq, s, dq = q.shape
    g_kv = k.shape[0]
    hpg = hq // g_kv
    nq = s // tq
    tkk = tk * nsub
    nk = s // tkk
    grid = (g_kv, nq, nk)
    hosted = exchange is not None
    m_cols = hpg * tq
    c = scale * LOG2E
    dvp = dv + ONES_ROWS

    def body(*refs):
        nx =exchange.n if hosted else 0
        q_ref, k_ref, vt_ref = refs[:3]
        xs_refs = refs[3:3 + nx]
        o_ref, lse_ref = refs[3 + nx:5 + nx]
        land_refs = refs[5 + nx:5 + 2 * nx]
        m_s, acc_s = refs[5 + 2 * nx:7 + 2 * nx]
        sems = refs[7 + 2 * nx:]
        if hosted:
            first, last = _grid_edges(grid)
            pl.when(first)(lambda: exchange.start(xs_refs, land_refs, sems))
        j = pl.program_id(2)

        @pl.when(j == 0)
        def _():
            m_s[...] = jnp.full((8, m_cols), -jnp.inf, F32)
            acc_s[...] = jnp.zeros((dvp, m_cols), F32)

        qq = q_ref[...].reshape(m_cols, dq)
        sts = [_mm_nt(k_ref[0, tk * u:tk * (u + 1), :], qq).reshape(tk // 8, 8, m_cols)
               for u in range(nsub)]
        m_run = m_s[...]
        acc = acc_s[...]
        for u in range(nsub):
            m_new = jnp.maximum(m_run, _col_max8(sts[u]) * c)
            p = jnp.exp2(sts[u] * c - m_new[None])
            alpha = jnp.exp2(m_run - m_new)
            pv = _mm(_with_ones(vt_ref[0, 0:dv, tk * u:tk * (u + 1)], tk), p.reshape(tk, m_cols))
            acc = (acc.reshape(dvp // 8, 8, m_cols) * alpha[None]).reshape(dvp, m_cols) + pv
            m_run = m_new
        acc_s[...] = acc
        m_s[...] = m_run

        @pl.when(j == nk - 1)
        def _():
            l = acc_s[dv:dv + 1, :]
            ot = acc_s[0:dv, :] / l
            lse = m_s[0:1, :] + jnp.log2(l)
            for hh in range(hpg):
                o_ref[:, dv * hh:dv * hh + dv] = ot[:, tq * hh:tq * hh + tq].T
                lse_ref[hh] = lse[:, tq * hh:tq * hh + tq]

        if hosted:
            pl.when(last)(lambda: exchange.wait(xs_refs, land_refs, sems))

    sd = jax.ShapeDtypeStruct
    return pl.pallas_call(
        body, name=name, grid=grid,
        out_shape=(sd((s, hq * dv), F32), sd((hq, 1, s), F32)) + (exchange.land_shapes if hosted else ()),
        in_specs=[pl.BlockSpec((hpg, tq, dq), lambda g, i, j: (g, i, 0)),
                  pl.BlockSpec((1, tkk, k.shape[2]), lambda g, i, j: (g, j, 0)),
                  pl.BlockSpec((1, dv, tkk), lambda g, i, j: (g, 0, j))] + (exchange.in_specs if hosted else []),
        out_specs=(pl.BlockSpec((tq, hpg * dv), lambda g, i, j: (i, g)),
                   pl.BlockSpec((hpg, 1, tq), lambda g, i, j: (g, 0, i))) + (exchange.out_specs if hosted else ()),
        scratch_shapes=[pltpu.VMEM((8, m_cols), F32), pltpu.VMEM((dvp, m_cols), F32)]
        + (list(exchange.sems) if hosted else []),
        compiler_params=_params(("arbitrary",) * 3 if hosted else ("parallel", "parallel", "arbitrary")),
    )(q, k, vt, *(exchange.srcs if hosted else []))


def _window_bias_t(i, nq, hpg, slope_ref):
    t = WINDOW
    r = lax.broadcasted_iota(jnp.int32, (3 * t, t), 0)
    cq = lax.broadcasted_iota(jnp.int32, (3 * t, t), 1)
    arel = jnp.abs(r - t - cq)
    ok = (arel <= WINDOW) & ((r >= t) | (i > 0)) & ((r < 2 * t) | (i < nq - 1))
    base = jnp.where(ok, arel.astype(F32) * (-LOG2E), -jnp.inf)
    return jnp.concatenate([base * slope_ref[hh] for hh in range(hpg)], axis=1)


def _neighbour_specs(block, axis, nq, head_of):
    def spec(off):
        def index(g, i):
            idx = [head_of(g), 0, 0]
            idx[axis] = jnp.clip(i + off, 0, nq - 1)
            return tuple(idx)
        return pl.BlockSpec(block, index)
    return [spec(-1), spec(0), spec(1)]


def window_forward(q, k, vt, sink2, slopes, name):
    hq, s, d = q.shape
    g_kv = k.shape[0]
    hpg = hq // g_kv
    t = WINDOW
    nq = s // t
    m_cols = hpg * t
    c = (d ** -0.5) * LOG2E

    def body(q_ref, kp, ko, kn, vp, vo, vn, sink_ref, slope_ref, o_ref, lse_ref):
        i = pl.program_id(1)
        qq = q_ref[...].reshape(m_cols, d)
        kk = jnp.concatenate([kp[0], ko[0], kn[0]], axis=0)
        st = _mm_nt(kk, qq) * c + _window_bias_t(i, nq, hpg, slope_ref)
        sink_row = jnp.concatenate([jnp.broadcast_to(sink_ref[hh], (8, t)) for hh in range(hpg)], axis=1)
        s3 = st.reshape(3 * t // 8, 8, m_cols)
        m8 = jnp.maximum(_col_max8(s3), sink_row)
        p = jnp.exp2(s3 - m8[None]).reshape(3 * t, m_cols)
        vte = _with_ones(jnp.concatenate([vp[0], vo[0], vn[0]], axis=1), 3 * t)
        acc = _mm(vte, p)
        l = acc[d:d + 1, :] + jnp.exp2(sink_row[0:1, :] - m8[0:1, :])
        ot = acc[0:d, :] / l
        lse = m8[0:1, :] + jnp.log2(l)
        for hh in range(hpg):
            o_ref[:, d * hh:d * hh + d] = ot[:, t * hh:t * hh + t].T
            lse_ref[hh] = lse[:, t * hh:t * hh + t]

    head = lambda g: g
    sd = jax.ShapeDtypeStruct
    return pl.pallas_call(
        body, name=name, grid=(g_kv, nq),
        out_shape=(sd((s, hq * d), F32), sd((hq, 1, s), F32)),
        in_specs=[pl.BlockSpec((hpg, t, d), lambda g, i: (g, i, 0))]
        + _neighbour_specs((1, t, d), 1, nq, head) + _neighbour_specs((1, d, t), 2, nq, head)
        + [pl.BlockSpec((hpg, 1, 1), lambda g, i: (g, 0, 0))] * 2,
        out_specs=(pl.BlockSpec((t, hpg * d), lambda g, i: (i, g)),
                   pl.BlockSpec((hpg, 1, t), lambda g, i: (g, 0, i))),
        compiler_params=_params(("parallel", "parallel")),
    )(q, k, k, k, vt, vt, vt, sink2, slopes)


def window_backward(q, k, kt, v, do, lse, delta, slopes, name):
    hq, s, d = q.shape
    g_kv = k.shape[0]
    hpg = hq // g_kv
    t = WINDOW
    nq = s // t
    m_cols = hpg * t
    scale = d ** -0.5
    c = scale * LOG2E

    def body(q_ref, kp, ko, kn, ktp, kto, ktn, vp, vo, vn, do_ref, lse_ref, dl_ref, slope_ref,
             dq_ref, dk_ref, dv_ref):
        i = pl.program_id(1)

        @pl.when(i == 0)
        def _():
            dk_ref[...] = jnp.zeros(dk_ref.shape, F32)
            dv_ref[...] = jnp.zeros(dv_ref.shape, F32)

        qq = q_ref[...].reshape(m_cols, d)
        kk = jnp.concatenate([kp[0], ko[0], kn[0]], axis=0)
        vv = jnp.concatenate([vp[0], vo[0], vn[0]], axis=0)
        kkt = jnp.concatenate([ktp[0], kto[0], ktn[0]], axis=1)
        dd = jnp.concatenate([do_ref[:, d * hh:d * hh + d] for hh in range(hpg)], axis=0)
        lse_row = jnp.concatenate([lse_ref[hh] for hh in range(hpg)], axis=1)
        dl_row = jnp.concatenate([dl_ref[hh] for hh in range(hpg)], axis=1)
        st = _mm_nt(kk, qq) * c + _window_bias_t(i, nq, hpg, slope_ref)
        p = jnp.exp2(st - lse_row)
        ds = p * (_mm_nt(vv, dd) - dl_row) * scale
        dv_part = _mm(p, dd)
        dk_part = _mm(ds, qq)
        for b in range(3):
            r0 = pl.multiple_of(jnp.clip(i - 1 + b, 0, nq - 1) * t, t)
            dv_ref[0, pl.ds(r0, t), :] += dv_part[t * b:t * b + t, :]
            dk_ref[0, pl.ds(r0, t), :] += dk_part[t * b:t * b + t, :]
        dqt = _mm(kkt, ds)
        for hh in range(hpg):
            dq_ref[:, d * hh:d * hh + d] = dqt[:, t * hh:t * hh + t].T

    head = lambda g: g
    row_map = lambda g, i: (g, 0, i)
    sd = jax.ShapeDtypeStruct
    return pl.pallas_call(
        body, name=name, grid=(g_kv, nq),
        out_shape=(sd((s, hq * d), F32), sd((g_kv, s, d), F32), sd((g_kv, s, d), F32)),
        in_specs=[pl.BlockSpec((hpg, t, d), lambda g, i: (g, i, 0))]
        + _neighbour_specs((1, t, d), 1, nq, head) + _neighbour_specs((1, d, t), 2, nq, head)
        + _neighbour_specs((1, t, d), 1, nq, head)
        + [pl.BlockSpec((t, hpg * d), lambda g, i: (i, g)), pl.BlockSpec((hpg, 1, t), row_map),
           pl.BlockSpec((hpg, 1, t), row_map), pl.BlockSpec((hpg, 1, 1), lambda g, i: (g, 0, 0))],
        out_specs=(pl.BlockSpec((t, hpg * d), lambda g, i: (i, g)),
                   pl.BlockSpec((1, s, d), lambda g, i: (g, 0, 0)),
                   pl.BlockSpec((1, s, d), lambda g, i: (g, 0, 0))),
        compiler_params=_params(("parallel", "arbitrary")),
    )(q, k, k, k, kt, kt, kt, v, v, v, do, lse, delta, slopes)


def flash_backward(q, k, kt, v, do, lse, delta, *, scale, dv, tq, tk, nsub, gq, name, exchange=None):
    hq, s, dq = q.shape
    g_kv = k.shape[0]
    hpg = hq // gq
    nq = s // tq
    tqq = tq * nsub
    nqs = s // tqq
    nkb = s // tk
    grid = (gq, nkb, nqs)
    hosted = exchange is not None
    m_cols = hpg * tq
    c = scale * LOG2E
    has_v = v is not None

    def body(*refs):
        it = iter(refs)
        q_ref, k_ref, kt_ref = next(it), next(it), next(it)
        v_ref = next(it) if has_v else None
        do_ref, lse_ref, dl_ref = next(it), next(it), next(it)
        nx = exchange.n if hosted else 0
        xs_refs = [next(it) for _ in range(nx)]
        dq_ref, dk_ref, dv_ref = next(it), next(it), next(it)
        land_refs = [next(it) for _ in range(nx)]
        dqt_s = next(it)
        sems = list(it)
        kj = pl.program_id(1)
        qi = pl.program_id(2)
        if hosted:
            first, last = _grid_edges(grid)
            pl.when(first)(lambda: exchange.start(xs_refs, land_refs, sems))

        @pl.when((kj == 0) & (qi == 0))
        def _():
            dqt_s[...] = jnp.zeros(dqt_s.shape, F32)

        @pl.when(qi == 0)
        def _():
            dk_ref[...] = jnp.zeros(dk_ref.shape, F32)
            dv_ref[...] = jnp.zeros(dv_ref.shape, F32)

        kk = k_ref[0]
        vv = v_ref[0] if has_v else kk[:, :dv]
        qqs, dds, sts, dps = [], [], [], []
        for u in range(nsub):
            rows = slice(tq * u, tq * (u + 1))
            qqs.append(q_ref[:, rows, :].reshape(m_cols, dq))
            dds.append(jnp.concatenate([do_ref[rows, dv * hh:dv * hh + dv] for hh in range(hpg)], axis=0))
            sts.append(_mm_nt(kk, qqs[u]))
            dps.append(_mm_nt(vv, dds[u]))
        dv_acc = dv_ref[0]
        dk_acc = dk_ref[0]
        for u in range(nsub):
            rows = slice(tq * u, tq * (u + 1))
            lse_row = jnp.concatenate([lse_ref[hh, :, rows] for hh in range(hpg)], axis=1)
            dl_row = jnp.concatenate([dl_ref[hh, :, rows] for hh in range(hpg)], axis=1)
            p = jnp.exp2(sts[u] * c - lse_row)
            ds = p * (dps[u] - dl_row) * scale
            dv_acc = dv_acc + _mm(p, dds[u])
            dk_acc = dk_acc + _mm(ds, qqs[u])
            dqt = _mm(kt_ref[0], ds)
            for hh in range(hpg):
                dqt_s[qi * nsub + u, dq * hh:dq * hh + dq, :] += dqt[:, tq * hh:tq * hh + tq]
        dv_ref[0] = dv_acc
        dk_ref[0] = dk_acc

        @pl.when((kj == nkb - 1) & (qi == nqs - 1))
        def _():
            def emit(t, carry):
                r0 = pl.multiple_of(t * tq, tq)
                for hh in range(hpg):
                    dq_ref[pl.ds(r0, tq), dq * hh:dq * hh + dq] = dqt_s[t, dq * hh:dq * hh + dq, :].T
                return carry

            lax.fori_loop(0, nq, emit, 0)

        if hosted:
            pl.when(last)(lambda: exchange.wait(xs_refs, land_refs, sems))

    kv_of = lambda g: g * g_kv // gq
    in_specs = [pl.BlockSpec((hpg, tqq, dq), lambda g, kj, qi: (g, qi, 0)),
                pl.BlockSpec((1, tk, dq), lambda g, kj, qi: (kv_of(g), kj, 0)),
                pl.BlockSpec((1, dq, tk), lambda g, kj, qi: (kv_of(g), 0, kj))]
    args = [q, k, kt]
    if has_v:
        in_specs.append(pl.BlockSpec((1, tk, dv), lambda g, kj, qi: (kv_of(g), kj, 0)))
        args.append(v)
    row_map = lambda g, kj, qi: (g, 0, qi)
    in_specs += [pl.BlockSpec((tqq, hpg * dv), lambda g, kj, qi: (qi, g)),
                 pl.BlockSpec((hpg, 1, tqq), row_map), pl.BlockSpec((hpg, 1, tqq), row_map)]
    args += [do, lse, delta]
    if hosted:
        in_specs += exchange.in_specs
        args += exchange.srcs
    sd = jax.ShapeDtypeStruct
    return pl.pallas_call(
        body, name=name, grid=grid,
        out_shape=(sd((s, hq * dq), F32), sd((gq, s, dq), F32), sd((gq, s, dv), F32))
        + (exchange.land_shapes if hosted else ()),
        in_specs=in_specs,
        out_specs=(pl.BlockSpec((s, hpg * dq), lambda g, kj, qi: (0, g)),
                   pl.BlockSpec((1, tk, dq), lambda g, kj, qi: (g, kj, 0)),
                   pl.BlockSpec((1, tk, dv), lambda g, kj, qi: (g, kj, 0))) + (exchange.out_specs if hosted else ()),
        scratch_shapes=[pltpu.VMEM((nq, hpg * dq, tq), F32)] + (list(exchange.sems) if hosted else []),
        compiler_params=_params(("arbitrary",) * 3 if hosted else ("parallel", "arbitrary", "arbitrary")),
    )(*args)


def loss_head(x, target, fnw):
    s, d = x.shape
    tm = min(ROW_TILE, s)

    def body(x_ref, t_ref, w_ref, lp_ref, dx_ref, dw_ref):
        @pl.when(pl.program_id(0) == 0)
        def _():
            lp_ref[...] = jnp.zeros(lp_ref.shape, F32)
            dw_ref[...] = jnp.zeros(dw_ref.shape, F32)

        x = x_ref[...]
        g = w_ref[...]
        err = x * _rms(x) * g - t_ref[...]
        lp_ref[...] += jnp.sum(err * err, axis=0, keepdims=True)
        dx, dg = _rms_bwd(err * (1.0 / d), x, g)
        dx_ref[...] = dx
        dw_ref[...] += jnp.sum(dg, axis=0, keepdims=True)

    sd = jax.ShapeDtypeStruct
    return pl.pallas_call(
        body, name="loss_head", grid=(s // tm,),
        out_shape=(sd((1, d), F32), sd((s, d), F32), sd((1, d), F32)),
        in_specs=[_row_spec(tm, d), _row_spec(tm, d), _full_spec(fnw.shape)],
        out_specs=(_full_spec((1, d)), _row_spec(tm, d), _full_spec((1, d))),
        compiler_params=_params(("arbitrary",)),
    )(x, target, fnw)


def mixer_out_backward(dx, y, mod, pairs, w_out, delta_heads, name, lse=None, sink=None):
    s, d = dx.shape
    tm = min(ROW_TILE, s)
    n = len(pairs)
    widths = [o.shape[1] for o, _ in pairs]
    n_delta = sum(1 for h in delta_heads if h)
    with_sink = lse is not None

    def body(*refs):
        it = iter(refs)
        dx_ref, y_ref, mod_ref, wt_ref = next(it), next(it), next(it), next(it)
        pr = [next(it) for _ in range(2 * n)]
        lse_ref = next(it) if with_sink else None
        sink_ref = next(it) if with_sink else None
        outs = [next(it) for _ in range(2 * n)]
        dl_refs = [next(it) for _ in range(n_delta)]
        dgate_ref, dw_ref = next(it), next(it)
        dsink_ref = next(it) if with_sink else None

        @pl.when(pl.program_id(0) == 0)
        def _():
            dgate_ref[...] = jnp.zeros(dgate_ref.shape, F32)
            dw_ref[...] = jnp.zeros(dw_ref.shape, F32)
            if with_sink:
                dsink_ref[...] = jnp.zeros(dsink_ref.shape, F32)

        dxo = dx_ref[...]
        dgate_ref[...] += jnp.sum(dxo * y_ref[...], axis=0, keepdims=True)
        dy = (dxo * mod_ref[2:3, :]).astype(MXU)
        dmix = _mm_nt(dy, wt_ref[...])
        r0 = 0
        di = 0
        for i in range(n):
            o = pr[2 * i][...]
            g = pr[2 * i + 1][...]
            dm = dmix[:, r0:r0 + widths[i]]
            sg = _sigmoid(g)
            act = g * sg
            do = dm * act
            outs[2 * i][...] = do.astype(MXU)
            outs[2 * i + 1][...] = (dm * o * (sg * (1.0 + g * (1.0 - sg)))).astype(MXU)
            dw_ref[r0:r0 + widths[i], :] += _mm_tn(o * act, dy)
            if delta_heads[i]:
                dlt = _group_sums_t(do * o, HD)[0:delta_heads[i], :]
                dl_refs[di][...] = dlt
                if with_sink:
                    ps = jnp.exp2(sink_ref[...] - lse_ref[...])
                    dsink_ref[...] += -jnp.sum(ps * dlt, axis=1, keepdims=True)
                di += 1
            r0 += widths[i]

    flat = [a for p in pairs for a in p]
    sd = jax.ShapeDtypeStruct
    in_specs = [_row_spec(tm, d), _row_spec(tm, d), _full_spec(mod.shape), _full_spec(w_out.shape)]
    in_specs += [_row_spec(tm, a.shape[1]) for a in flat]
    args = [dx, y, mod, w_out] + flat
    if with_sink:
        nh = lse.shape[0]
        in_specs += [_rows_spec(nh, tm), _full_spec(sink.shape)]
        args += [lse, sink]
    out_shape = [sd((s, a.shape[1]), MXU) for a in flat]
    out_specs = [_row_spec(tm, a.shape[1]) for a in flat]
    for h in delta_heads:
        if h:
            out_shape.append(sd((h, s), F32))
            out_specs.append(_rows_spec(h, tm))
    out_shape += [sd((1, d), F32), sd((sum(widths), d), F32)]
    out_specs += [_full_spec((1, d)), _full_spec((sum(widths), d))]
    if with_sink:
        out_shape.append(sd((lse.shape[0], 1), F32))
        out_specs.append(_full_spec((lse.shape[0], 1)))
    return pl.pallas_call(
        body, name=name, grid=(s // tm,), out_shape=tuple(out_shape), in_specs=in_specs, out_specs=tuple(out_specs),
        compiler_params=_params(("arbitrary",)),
    )(*args)


def latent_out_backward(d_ob, o_lat, w_uv):
    s = o_lat.shape[0]
    tm = min(ROW_TILE, s)

    def body(d_ref, o_ref, uv_ref, dol_ref, dl_ref, duv_ref, prod_s):
        @pl.when(pl.program_id(0) == 0)
        def _():
            duv_ref[...] = jnp.zeros(duv_ref.shape, F32)

        for hh in range(B_HEADS):
            dh = d_ref[:, HD * hh:HD * hh + HD]
            ol = o_ref[:, B_KV_LORA * hh:B_KV_LORA * (hh + 1)]
            dol = _mm_nt(dh, uv_ref[hh])
            dol_ref[:, B_KV_LORA * hh:B_KV_LORA * (hh + 1)] = dol.astype(MXU)
            prod_s[:, B_KV_LORA * hh:B_KV_LORA * (hh + 1)] = dol * ol
            duv_ref[hh] += _mm_tn(ol, dh)
        dl_ref[...] = _group_sums_t(prod_s[...], B_KV_LORA)[0:B_HEADS, :]

    sd = jax.ShapeDtypeStruct
    return pl.pallas_call(
        body, name="latent_out_backward", grid=(s // tm,),
        out_shape=(sd(o_lat.shape, MXU), sd((B_HEADS, s), F32), sd(w_uv.shape, F32)),
        in_specs=[_row_spec(tm, d_ob.shape[1]), _row_spec(tm, o_lat.shape[1]), _full_spec(w_uv.shape)],
        out_specs=(_row_spec(tm, o_lat.shape[1]), _rows_spec(B_HEADS, tm), _full_spec(w_uv.shape)),
        scratch_shapes=[pltpu.VMEM((tm, o_lat.shape[1]), F32)],
        compiler_params=_params(("arbitrary",)),
    )(d_ob, o_lat, w_uv)


def even_prep_backward(dqa, dka, dva, dqb, dkb, dvb, qa_raw, ka_raw, cq_raw, ckv_raw,
                       qn, kn, qln, kvln, w_uq_t, w_uk, cos_a, sin_a, cos_t, sin_t):
    s = qa_raw.shape[0]
    tm = min(ROW_TILE, s)
    qb_w = B_HEADS * (B_NOPE + B_ROPE)

    def body(dqa_ref, dka_ref, dva_ref, dqb_ref, dkb_ref, dvb_ref, qa_ref, ka_ref, cq_ref, ckv_ref,
             qn_ref, kn_ref, qln_ref, kvln_ref, uqt_ref, uk_ref, ca_ref, sa_ref, ct_ref, st_ref,
             pqa, pka, pva, pcq, pckv, pkr, gqn, gkn, gqln, gkvln, guq, guk, dqb_s):
        @pl.when(pl.program_id(0) == 0)
        def _():
            for r in (gqn, gkn, gqln, gkvln, guq, guk):
                r[...] = jnp.zeros(r.shape, F32)

        ca, sa, ct, st = ca_ref[...], sa_ref[...], ct_ref[...], st_ref[...]
        acc_q = jnp.zeros((1, HD), F32)
        for hh in range(A_HEADS):
            dyn = _rope_t(dqa_ref[:, HD * hh:HD * hh + HD], ca, sa, 32)
            dx, dg = _rms_bwd(dyn, qa_ref[:, HD * hh:HD * hh + HD], qn_ref[...])
            pqa[:, HD * hh:HD * hh + HD] = dx.astype(MXU)
            acc_q = acc_q + jnp.sum(dg, axis=0, keepdims=True)
        gqn[...] += acc_q
        acc_k = jnp.zeros((1, HD), F32)
        for g in range(A_KV):
            dyn = _rope_t(dka_ref[g], ca, sa, 32)
            dx, dg = _rms_bwd(dyn, ka_ref[:, HD * g:HD * g + HD], kn_ref[...])
            pka[:, HD * g:HD * g + HD] = dx.astype(MXU)
            acc_k = acc_k + jnp.sum(dg, axis=0, keepdims=True)
            pva[:, HD * g:HD * g + HD] = dva_ref[g].astype(MXU)
        gkn[...] += acc_k
        cq_raw = cq_ref[...]
        cq_n = cq_raw * _rms(cq_raw) * qln_ref[...]
        qb = _mm_nt(cq_n, uqt_ref[...])
        for hh in range(B_HEADS):
            base = (B_NOPE + B_ROPE) * hh
            dlat = dqb_ref[:, B_QK * hh:B_QK * hh + B_KV_LORA]
            dqb_s[:, base:base + B_NOPE] = _mm(dlat, uk_ref[hh])
            guk[hh] += _mm_tn(dlat, qb[:, base:base + B_NOPE])
            dqb_s[:, base + B_NOPE:base + B_NOPE + B_ROPE] = _rope_t(
                dqb_ref[:, B_QK * hh + B_KV_LORA:B_QK * (hh + 1)], ct, st, 32)
        dqb_all = dqb_s[...]
        guq[...] += _mm_tn(dqb_all, cq_n)
        dx, dg = _rms_bwd(_mm(dqb_all, uqt_ref[...]), cq_raw, qln_ref[...])
        pcq[...] = dx.astype(MXU)
        gqln[...] += jnp.sum(dg, axis=0, keepdims=True)
        dkb_sum = dkb_ref[0] + dkb_ref[1]
        dckv = dkb_sum[:, 0:B_KV_LORA] + dvb_ref[0] + dvb_ref[1]
        dx, dg = _rms_bwd(dckv, ckv_ref[...], kvln_ref[...])
        pckv[...] = dx.astype(MXU)
        gkvln[...] += jnp.sum(dg, axis=0, keepdims=True)
        pkr[...] = _rope_t(dkb_sum[:, B_KV_LORA:B_QK], ct, st, 32).astype(MXU)

    sd = jax.ShapeDtypeStruct
    args = [dqa, dka, dva, dqb, dkb, dvb, qa_raw, ka_raw, cq_raw, ckv_raw,
            qn, kn, qln, kvln, w_uq_t, w_uk, cos_a, sin_a, cos_t, sin_t]
    in_specs = [_row_spec(tm, 512), _head_spec(A_KV, tm, HD), _head_spec(A_KV, tm, HD),
                _row_spec(tm, B_HEADS * B_QK), _head_spec(2, tm, B_QK), _head_spec(2, tm, B_KV_LORA),
                _row_spec(tm, 512), _row_spec(tm, 128), _row_spec(tm, B_Q_LORA), _row_spec(tm, B_KV_LORA),
                _full_spec(qn.shape), _full_spec(kn.shape), _full_spec(qln.shape), _full_spec(kvln.shape),
                _full_spec(w_uq_t.shape), _full_spec(w_uk.shape),
                _row_spec(tm, HD), _row_spec(tm, HD), _row_spec(tm, B_ROPE), _row_spec(tm, B_ROPE)]
    out_shape = (sd((s, 512), MXU), sd((s, 128), MXU), sd((s, 128), MXU), sd((s, B_Q_LORA), MXU),
                 sd((s, B_KV_LORA), MXU), sd((s, B_ROPE), MXU),
                 sd(qn.shape, F32), sd(kn.shape, F32), sd(qln.shape, F32), sd(kvln.shape, F32),
                 sd(w_uq_t.shape, F32), sd(w_uk.shape, F32))
    out_specs = (_row_spec(tm, 512), _row_spec(tm, 128), _row_spec(tm, 128), _row_spec(tm, B_Q_LORA),
                 _row_spec(tm, B_KV_LORA), _row_spec(tm, B_ROPE),
                 _full_spec(qn.shape), _full_spec(kn.shape), _full_spec(qln.shape), _full_spec(kvln.shape),
                 _full_spec(w_uq_t.shape), _full_spec(w_uk.shape))
    return pl.pallas_call(
        body, name="even_prep_backward", grid=(s // tm,), out_shape=out_shape, in_specs=in_specs, out_specs=out_specs,
        scratch_shapes=[pltpu.VMEM((tm, qb_w), F32)],
        compiler_params=_params(("arbitrary",)),
    )(*args)


def in_proj_backward(x, mod, nw, dx_out, pieces, w_in_t, name):
    s, d = x.shape
    tm = min(ROW_TILE, s)
    n_cols = w_in_t.shape[0]
    n = len(pieces)
    cols = [c for _, c in pieces]

    def body(*refs):
        x_ref, mod_ref, nw_ref, dxo_ref, wt_ref = refs[:5]
        p_refs = refs[5:5 + n]
        dx_ref, dw_ref, dv_ref, acc_ref = refs[5 + n:]
        i = pl.program_id(0)

        @pl.when(i == 0)
        def _():
            dw_ref[...] = jnp.zeros(dw_ref.shape, F32)
            acc_ref[...] = jnp.zeros(acc_ref.shape, F32)

        xn, g1, h = _modulated(x_ref[...], mod_ref, nw_ref)
        hb = h.astype(MXU)
        dh = jnp.zeros((tm, d), F32)
        for pr, (c0, c1) in zip(p_refs, cols):
            pc = pr[...].astype(MXU)
            dh = dh + jnp.dot(pc, wt_ref[c0:c1, :], preferred_element_type=F32)
            dw_ref[c0:c1, :] += _mm_tn(pc, hb)
        acc_ref[0:1, :] += jnp.sum(dh, axis=0, keepdims=True)
        acc_ref[1:2, :] += jnp.sum(dh * xn, axis=0, keepdims=True)
        dxn = dh * g1
        x = x_ref[...]
        r = _rms(x)
        dx_ref[...] = dxo_ref[...] + r * (dxn - xn * jnp.mean(dxn * xn, axis=-1, keepdims=True))

        @pl.when(i == pl.num_programs(0) - 1)
        def _():
            dg1 = acc_ref[1:2, :]
            dv_ref[0:1, :] = acc_ref[0:1, :]
            dv_ref[1:2, :] = dg1 * nw_ref[...]
            dv_ref[2:3, :] = dg1 * (1.0 + mod_ref[1:2, :])
            dv_ref[3:4, :] = jnp.zeros((1, d), F32)

    arrs = [a for a, _ in pieces]
    sd = jax.ShapeDtypeStruct
    return pl.pallas_call(
        body, name=name, grid=(s // tm,),
        out_shape=(sd((s, d), F32), sd((n_cols, d), F32), sd((4, d), F32)),
        in_specs=[_row_spec(tm, d), _full_spec(mod.shape), _full_spec(nw.shape), _row_spec(tm, d),
                  _full_spec(w_in_t.shape)] + [_row_spec(tm, a.shape[1]) for a in arrs],
        out_specs=(_row_spec(tm, d), _full_spec((n_cols, d)), _full_spec((4, d))),
        scratch_shapes=[pltpu.VMEM((8, d), F32)],
        compiler_params=_params(("arbitrary",)),
    )(x, mod, nw, dx_out, w_in_t, *arrs)


def ada_weight_grad(c_all, dmod_cols):
    d = c_all.shape[1]
    w = dmod_cols.shape[2]

    def body(c_ref, dm_ref, out_ref):
        ca = _silu(c_ref[...])
        for l in range(2):
            out_ref[l] = _mm_tn(ca, dm_ref[l])

    return pl.pallas_call(
        body, name="ada_weight_grad",
        out_shape=jax.ShapeDtypeStruct((2, d, w), F32),
        compiler_params=pltpu.CompilerParams(vmem_limit_bytes=VMEM_LIMIT),
    )(c_all, dmod_cols)


def adamw_rows(g_slots, w, m, v, name):
    n, r, lanes = g_slots.shape
    fits = [t for t in range(16, r + 1, 16) if r % t == 0 and t * lanes <= ADAM_TILE]
    tr = max(fits) if fits else r
    c1 = 1.0 - ADAM_B1 ** ADAM_STEP
    c2 = 1.0 - ADAM_B2 ** ADAM_STEP

    def body(g_ref, w_ref, m_ref, v_ref, go, do, mo, vo):
        g = g_ref[0].astype(F32)
        for k in range(1, n):
            g = g + g_ref[k].astype(F32)
        m_new = ADAM_B1 * m_ref[...] + (1.0 - ADAM_B1) * g
        v_new = ADAM_B2 * v_ref[...] + (1.0 - ADAM_B2) * (g * g)
        m_hat = m_new / c1
        v_hat = v_new / c2
        go[...] = g
        do[...] = -ADAM_LR * (m_hat / (jnp.sqrt(v_hat) + ADAM_EPS) + ADAM_WD * w_ref[...])
        mo[...] = m_new
        vo[...] = v_new

    row = pl.BlockSpec((tr, lanes), lambda i: (i, 0))
    sd = jax.ShapeDtypeStruct((r, lanes), F32)
    return pl.pallas_call(
        body, name=name, grid=(r // tr,), out_shape=(sd, sd, sd, sd),
        in_specs=[pl.BlockSpec((n, tr, lanes), lambda i: (0, i, 0)), row, row, row],
        out_specs=(row, row, row, row),
        compiler_params=_params(("parallel",)),
    )(g_slots, w, m, v)


def _rope_tables(s):
    def cs(pos, dim):
        inv = ROPE_THETA ** (-np.arange(0, dim, 2, dtype=np.float32) / dim)
        ang = pos.astype(np.float32)[:, None] * inv.astype(np.float32)[None, :]
        return np.cos(ang), np.sin(ang)

    rows = s // GRID_W
    row = np.repeat(np.arange(rows), GRID_W)
    col = np.tile(np.arange(GRID_W), rows)
    cr, sr = cs(row, HD // 2)
    cc, sc = cs(col, HD // 2)
    ct, st = cs(np.arange(s), B_ROPE)
    tables = (np.concatenate([cr, cr, cc, cc], axis=-1), np.concatenate([-sr, sr, -sc, sc], axis=-1),
              np.concatenate([ct, ct], axis=-1), np.concatenate([-st, st], axis=-1))
    return tuple(jnp.asarray(t, F32) for t in tables)


def _rows128(a):
    return a.reshape(-1, 128)


def _even_rows_to_kernel(wt):
    return jnp.concatenate([wt[:1664], wt[1696:], wt[1664:1696]], axis=0)


def _even_rows_to_reference(wt):
    return jnp.concatenate([wt[:1664], wt[2176:], wt[1664:2176]], axis=0)


def _shard_t(w):
    return jnp.transpose(w[0])


def _unshard_t(wt, like):
    return jnp.transpose(wt)[None].reshape(like.shape)


def _pad_rows(flat, rows):
    return jnp.pad(flat, (0, rows * 128 - flat.shape[0])).reshape(rows, 128)


def kernel(x, c, norm_w, ada_w, ada_b, even_w_in, a_q_norm, a_k_norm, b_q_lora_norm, b_kv_lora_norm, b_w_uq, b_w_uk, b_w_uv, even_w_out, odd_w_in, c_sink, odd_w_out, final_norm, loss_target, m_norm_w, m_ada_w, m_ada_b, m_even_w_in, m_a_q_norm, m_a_k_norm, m_b_q_lora_norm, m_b_kv_lora_norm, m_b_w_uq, m_b_w_uk, m_b_w_uv, m_even_w_out, m_odd_w_in, m_c_sink, m_odd_w_out, m_final_norm, v_norm_w, v_ada_w, v_ada_b, v_even_w_in, v_a_q_norm, v_a_k_norm, v_b_q_lora_norm, v_b_kv_lora_norm, v_b_w_uq, v_b_w_uk, v_b_w_uv, v_even_w_out, v_odd_w_in, v_c_sink, v_odd_w_out, v_final_norm):
    s, d = x.shape[1], x.shape[2]
    x0 = x[0]
    target = loss_target[0]
    me_flat = 4 * lax.axis_index("x") + 2 * lax.axis_index("y") + lax.axis_index("c")

    g_in_e, g_uq = all_gather_slots([_shard_t(even_w_in).astype(MXU), _shard_t(b_w_uq).astype(MXU)],
                                    "gather_first_weights")
    wt_in_e = _even_rows_to_kernel(g_in_e.reshape(-1, d))
    wt_uq = g_uq.reshape(-1, B_Q_LORA)
    later_exchange = Exchange([_shard_t(odd_w_in).astype(MXU), even_w_out[0].astype(MXU),
                               odd_w_out[0].astype(MXU)], scatter=False)
    w_uk = jnp.transpose(b_w_uk[0], (1, 0, 2)).astype(MXU)
    w_uv = jnp.transpose(b_w_uv[0], (1, 0, 2)).astype(MXU)

    wcols = ada_w.shape[2]
    bias_cols = lax.dynamic_slice_in_dim(ada_b.reshape(2, N_DEV, wcols), me_flat, 1, axis=1)
    call, modp = ada_forward(jnp.broadcast_to(c, (8, d)), ada_w, bias_cols)
    c_all = call[:, 0, :]
    mod = jnp.transpose(modp[:, :, 0, :], (1, 0, 2)).reshape(2, 3, d)
    mod_e, mod_o = mod[0], mod[1]
    nw_e, nw_o = norm_w[0:1], norm_w[1:2]

    cos_a, sin_a, cos_t, sin_t = _rope_tables(s)
    slopes = (2.0 ** (-8.0 * jnp.arange(1, C_HEADS + 1, dtype=F32) / C_HEADS)).reshape(C_HEADS, 1, 1)
    sink2 = c_sink.reshape(C_HEADS, 1, 1) * LOG2E

    (qa, ka, va, qb, kb, kat, vat, kbt, qa_raw, ka_raw, cq_raw, ckv_raw, ga, gb) = even_in_forward(
        x0, mod_e, nw_e, wt_in_e, a_q_norm, a_k_norm, b_q_lora_norm, b_kv_lora_norm, wt_uq, w_uk,
        cos_a, sin_a, cos_t, sin_t)
    tk_dense = min(512, s)
    tq_dense = min(256, s)
    fwd_sub = min(8, s // tk_dense)
    bwd_sub = min(4, s // tq_dense)
    oa, lse_a, g_in_o, g_out_e, g_out_o = flash_forward(
        qa, ka, vat, scale=HD ** -0.5, dv=HD, tq=tq_dense, tk=tk_dense, nsub=fwd_sub, name="attn_a_fwd",
        exchange=later_exchange)
    wt_in_o = g_in_o.reshape(-1, d)
    w_out_e = g_out_e.reshape(-1, d)
    w_out_o = g_out_o.reshape(-1, d)
    scale_b = (B_NOPE + B_ROPE) ** -0.5
    o_lat, lse_b = flash_forward(qb, kb, kbt, scale=scale_b, dv=B_KV_LORA, tq=min(128, s), tk=tk_dense, nsub=fwd_sub,
                                 name="attn_b_fwd")
    ob = latent_out_forward(o_lat, w_uv)
    x1, y_e = mixer_out_forward(x0, mod_e, [(oa, ga), (ob, gb)], w_out_e, "even_out_fwd")

    qc, kc, vc, kct, vct, gc = odd_in_forward(x1, mod_o, nw_o, wt_in_o)
    oc, lse_c = window_forward(qc, kc, vct, sink2, slopes, "attn_c_fwd")
    x2, y_o = mixer_out_forward(x1, mod_o, [(oc, gc)], w_out_o, "odd_out_fwd")

    loss_lanes, dx2, d_final = loss_head(x2, target, final_norm.reshape(1, d))
    loss_part = (0.5 / d) * jnp.sum(loss_lanes)

    doc, dgc, delta_c, dgate_o, dw_out_o, dsink = mixer_out_backward(
        dx2, y_o, mod_o, [(oc, gc)], w_out_o, [C_HEADS], "odd_out_bwd", lse=lse_c.reshape(C_HEADS, s),
        sink=sink2.reshape(C_HEADS, 1))
    rows3 = lambda t: t.reshape(t.shape[0], 1, s)
    dqc, dkc, dvc = window_backward(qc, kc, kct, vc, doc, lse_c, rows3(delta_c), slopes, "attn_c_bwd")
    to_rows = lambda t: jnp.transpose(t, (1, 0, 2)).reshape(s, -1)
    dx1, dwt_in_o, dvec_o = in_proj_backward(
        x1, mod_o, nw_o, dx2, [(dqc, O_Q), (to_rows(dkc), O_K), (to_rows(dvc), O_V), (dgc, O_G)], wt_in_o,
        "odd_in_bwd")

    doa, dga, dob, dgb, delta_a, dgate_e, dw_out_e = mixer_out_backward(
        dx1, y_e, mod_e, [(oa, ga), (ob, gb)], w_out_e, [A_HEADS, 0], "even_out_bwd")
    d_olat, delta_b, dw_uv = latent_out_backward(dob, o_lat, w_uv)
    blocks = lambda g: g.astype(MXU).reshape(N_DEV, g.shape[0] // N_DEV, g.shape[1])
    scatter_odd = Exchange([blocks(dwt_in_o), blocks(dw_out_o)], True)
    scatter_out_e = Exchange([blocks(dw_out_e)], True)
    dqb, dkb, dvb, l_in_o, l_out_o = flash_backward(
        qb, kb, kbt, None, d_olat, lse_b, rows3(delta_b), scale=scale_b, dv=B_KV_LORA,
        tq=tq_dense, tk=tk_dense, nsub=bwd_sub, gq=2, name="attn_b_bwd", exchange=scatter_odd)
    dqa, dka, dva, l_out_e = flash_backward(
        qa, ka, kat, va, doa, lse_a, rows3(delta_a), scale=HD ** -0.5, dv=HD,
        tq=tq_dense, tk=tk_dense, nsub=bwd_sub, gq=A_KV, name="attn_a_bwd", exchange=scatter_out_e)
    (pqa, pka, pva, pcq, pckv, pkr, g_qn, g_kn, g_qln, g_kvln, dwt_uq, dw_uk) = even_prep_backward(
        dqa, dka, dva, dqb, dkb, dvb, qa_raw, ka_raw, cq_raw, ckv_raw,
        a_q_norm, a_k_norm, b_q_lora_norm, b_kv_lora_norm, wt_uq, w_uk, cos_a, sin_a, cos_t, sin_t)
    dx0, dwt_in_e, dvec_e = in_proj_backward(
        x0, mod_e, nw_e, dx1,
        [(pqa, E_QA), (pka, E_KA), (pva, E_VA), (dga, E_GA), (pcq, E_CQ), (pckv, E_CKV), (dgb, E_GB), (pkr, E_KR)],
        wt_in_e, "even_in_bwd")

    dmod = jnp.stack([jnp.concatenate([dvec_e[0], dvec_e[1], dgate_e[0]]),
                      jnp.concatenate([dvec_o[0], dvec_o[1], dgate_o[0]])])
    d_norm_w = jnp.stack([dvec_e[2], dvec_o[2]])
    small_names = ["norm_w", "ada_b", "a_q_norm", "a_k_norm", "b_q_lora_norm", "b_kv_lora_norm", "b_w_uk", "b_w_uv",
                   "c_sink", "final_norm"]
    small_w = [norm_w, ada_b, a_q_norm, a_k_norm, b_q_lora_norm, b_kv_lora_norm, b_w_uk, b_w_uv, c_sink, final_norm]
    small_m = [m_norm_w, m_ada_b, m_a_q_norm, m_a_k_norm, m_b_q_lora_norm, m_b_kv_lora_norm, m_b_w_uk, m_b_w_uv,
               m_c_sink, m_final_norm]
    small_v = [v_norm_w, v_ada_b, v_a_q_norm, v_a_k_norm, v_b_q_lora_norm, v_b_kv_lora_norm, v_b_w_uk, v_b_w_uv,
               v_c_sink, v_final_norm]
    small_g = [d_norm_w, dmod, g_qn, g_kn, g_qln, g_kvln, jnp.transpose(dw_uk, (1, 0, 2)), jnp.transpose(dw_uv, (1, 0, 2)),
               dsink, d_final]
    sizes = [w.size for w in small_w]
    n_small = sum(sizes)
    r_small = -(-(n_small + 1) // (128 * 8)) * 8
    flat_pack = lambda arrs: _pad_rows(jnp.concatenate([a.reshape(-1) for a in arrs]), r_small)
    (g_small_all,) = all_gather_slots([flat_pack(small_g + [loss_part])], "gather_small_grads")
    sm = adamw_rows(g_small_all, flat_pack(small_w), flat_pack(small_m), flat_pack(small_v), "adamw_small")
    loss = sm[0].reshape(-1)[n_small]

    def unpack_small(packed):
        flat = packed.reshape(-1)
        out, o = {}, 0
        for nm, w, sz in zip(small_names, small_w, sizes):
            out[nm] = flat[o:o + sz].reshape(w.shape)
            o += sz
        return out

    sm = [unpack_small(p) for p in sm]

    dmod_all = g_small_all.reshape(N_DEV, -1)[:, sizes[0]:sizes[0] + sizes[1]].reshape(N_DEV, 2, N_DEV, wcols)
    dmod_cols = lax.dynamic_slice_in_dim(dmod_all, me_flat, 1, axis=2)[:, :, 0, :]
    pad16 = lambda a: jnp.concatenate([a, jnp.zeros_like(a)], axis=0)
    g_ada_w = ada_weight_grad(pad16(c_all), jnp.transpose(pad16(dmod_cols), (1, 0, 2)))
    ada = adamw_rows(_rows128(g_ada_w)[None], _rows128(ada_w), _rows128(m_ada_w), _rows128(v_ada_w), "adamw_ada_w")
    ada = [p.reshape(ada_w.shape) for p in ada]

    l_in_e, l_uq = exchange_blocks(
        Exchange([blocks(_even_rows_to_reference(dwt_in_e)), blocks(dwt_uq)], True), "scatter_first_weight_grads")
    bg = [{}, {}, {}, {}]
    for nm, landed, w, m, v, transposed in (
            ("even_w_in", l_in_e, even_w_in, m_even_w_in, v_even_w_in, True),
            ("b_w_uq", l_uq, b_w_uq, m_b_w_uq, v_b_w_uq, True),
            ("odd_w_in", l_in_o, odd_w_in, m_odd_w_in, v_odd_w_in, True),
            ("even_w_out", l_out_e, even_w_out, m_even_w_out, v_even_w_out, False),
            ("odd_w_out", l_out_o, odd_w_out, m_odd_w_out, v_odd_w_out, False)):
        view = _shard_t if transposed else (lambda a: a[0])
        res = adamw_rows(landed, view(w), view(m), view(v), "adamw_" + nm)
        for kind, p in enumerate(res):
            bg[kind][nm] = _unshard_t(p, w) if transposed else p[None]
    big_names = ["even_w_in", "odd_w_in", "even_w_out", "odd_w_out", "b_w_uq"]

    order = ["norm_w", "ada_w", "ada_b", "even_w_in", "a_q_norm", "a_k_norm", "b_q_lora_norm", "b_kv_lora_norm",
             "b_w_uq", "b_w_uk", "b_w_uv", "even_w_out", "odd_w_in", "c_sink", "odd_w_out", "final_norm"]

    def pick(kind):
        out = []
        for nm in order:
            if nm == "ada_w":
                out.append(ada[kind])
            elif nm in big_names:
                out.append(bg[kind][nm])
            else:
                out.append(sm[kind][nm])
        return out

    return (loss, dx0[None], *pick(0), *pick(1), *pick(2), *pick(3))
```

```python
import functools

import jax
import jax.numpy as jnp
import numpy as np
from jax import lax
from jax.experimental import pallas as pl
from jax.experimental.pallas import tpu as pltpu

F32 = jnp.float32
MXU = jnp.bfloat16
EPS = 1e-6
ROPE_THETA = 10000.0
GRID_W = 64
HD = 64
N_DEV = 8
MESH_AXES = ("x", "y", "c")

A_HEADS, A_KV = 8, 2
B_HEADS, B_NOPE, B_ROPE, B_Q_LORA, B_KV_LORA = 8, 64, 32, 256, 128
B_QK = B_KV_LORA + B_ROPE
C_HEADS, C_KV = 16, 4
WINDOW = 128

ADAM_LR, ADAM_B1, ADAM_B2, ADAM_EPS, ADAM_WD, ADAM_STEP = 0.001, 0.9, 0.999, 1e-08, 0.01, 10

ROW_TILE = 256
ADAM_TILE = 2048 * 128
VMEM_LIMIT = 56 * 1024 * 1024

E_QA, E_KA, E_VA, E_GA, E_CQ, E_CKV, E_GB, E_KR = (
    (0, 512), (512, 640), (640, 768), (768, 1280), (1280, 1536), (1536, 1664), (1664, 2176), (2176, 2208))
EVEN_IN = 2208
O_Q, O_K, O_V, O_G = (0, 1024), (1024, 1280), (1280, 1536), (1536, 2560)
ODD_IN = 2560


def _mm(a, b):
    return jnp.dot(a.astype(MXU), b.astype(MXU), preferred_element_type=F32)


def _mm_nt(a, b):
    return lax.dot_general(a.astype(MXU), b.astype(MXU), (((1,), (1,)), ((), ())), preferred_element_type=F32)


def _mm_tn(a, b):
    return lax.dot_general(a.astype(MXU), b.astype(MXU), (((0,), (0,)), ((), ())), preferred_element_type=F32)


def _group_sums_t(prod, group):
    tm, w = prod.shape
    sel = (lax.broadcasted_iota(jnp.int32, (w, 128), 0) // group
           == lax.broadcasted_iota(jnp.int32, (w, 128), 1)).astype(MXU)
    hi = prod.astype(MXU)
    lo = prod - hi.astype(F32)
    return (_mm(hi, sel) + _mm(lo, sel)).T


def _sigmoid(z):
    return 1.0 / (1.0 + jnp.exp(-z))


def _silu(z):
    return z * _sigmoid(z)


def _rms(x):
    return lax.rsqrt(jnp.mean(x * x, axis=-1, keepdims=True) + EPS)


def _swap_halves(y, group):
    n = y.shape[-1]
    half = group // 2
    fwd = pltpu.roll(y, half, 1)
    if n == group:
        return fwd
    back = pltpu.roll(y, n - half, 1)
    lane = lax.broadcasted_iota(jnp.int32, y.shape, 1)
    return jnp.where((lane % group) < half, back, fwd)


def _rope(y, cos, sin, group):
    return y * cos + _swap_halves(y, group) * sin


def _rope_t(d, cos, sin, group):
    return d * cos - _swap_halves(d, group) * sin


def _rms_bwd(dy, x, g):
    r = _rms(x)
    xhat = x * r
    dxhat = dy * g
    dx = r * (dxhat - xhat * jnp.mean(dxhat * xhat, axis=-1, keepdims=True))
    return dx, dy * xhat


def _params(sem, vmem=VMEM_LIMIT):
    return pltpu.CompilerParams(dimension_semantics=sem, vmem_limit_bytes=vmem)


def _row_spec(tm, w):
    return pl.BlockSpec((tm, w), lambda i: (i, 0))


def _full_spec(shape):
    nd = len(shape)
    return pl.BlockSpec(shape, lambda i: (0,) * nd)


def _head_spec(h, tm, w):
    return pl.BlockSpec((h, tm, w), lambda i: (0, i, 0))


def _headt_spec(h, w, tm):
    return pl.BlockSpec((h, w, tm), lambda i: (0, 0, i))


def _rows_spec(h, tm):
    return pl.BlockSpec((h, tm), lambda i: (0, i))


def _me():
    return lax.axis_index("x"), lax.axis_index("y"), lax.axis_index("c")


def _flat(p):
    return 4 * p[0] + 2 * p[1] + p[2]


def _peer(me, k):
    x, y, c = me
    return (1 - x if k & 4 else x, 1 - y if k & 2 else y, 1 - c if k & 1 else c)


MESH_ID = pl.DeviceIdType.MESH


def all_gather_slots(shards, name):
    n = len(shards)

    def body(*refs):
        x_refs, out_refs = refs[:n], refs[n:2 * n]
        send_sems, recv_sems, local_sems = refs[2 * n:]
        me = _me()
        x, y, c = me
        sibling = (x, y, 1 - c)
        chips = [(1 - x, y), (x, 1 - y), (1 - x, 1 - y)]

        def copy(a, k, block, to, src=None):
            slot = out_refs[a].at[_flat(block)]
            return pltpu.make_async_remote_copy(
                src_ref=slot if src is None else src, dst_ref=slot, send_sem=send_sems.at[7 * a + k],
                recv_sem=recv_sems.at[7 * a + k], device_id=to, device_id_type=MESH_ID)

        mine = [pltpu.make_async_copy(x_refs[a], out_refs[a].at[_flat(me)], local_sems.at[a]) for a in range(n)]
        for cp in mine:
            cp.start()
        first = [copy(a, 0, me, sibling, src=x_refs[a]) for a in range(n)]
        first += [copy(a, 1 + j, me, (*chip, c), src=x_refs[a]) for a in range(n) for j, chip in enumerate(chips)]
        for cp in first:
            cp.start()
        passed = []
        for a in range(n):
            for j, chip in enumerate(chips):
                copy(a, 1 + j, (*chip, c), me).wait_recv()
                passed.append(copy(a, 4 + j, (*chip, c), sibling))
                passed[-1].start()
        for a in range(n):
            copy(a, 0, sibling, me).wait_recv()
            for j, chip in enumerate(chips):
                copy(a, 4 + j, (*chip, 1 - c), me).wait_recv()
        for cp in first + passed:
            cp.wait_send()
        for cp in mine:
            cp.wait()

    vm = pl.BlockSpec(memory_space=pltpu.VMEM)
    return pl.pallas_call(
        body, name=name,
        out_shape=tuple(jax.ShapeDtypeStruct((N_DEV,) + a.shape, a.dtype) for a in shards),
        in_specs=[vm] * n, out_specs=(vm,) * n,
        scratch_shapes=[pltpu.SemaphoreType.DMA((7 * n,)), pltpu.SemaphoreType.DMA((7 * n,)),
                        pltpu.SemaphoreType.DMA((n,))],
        compiler_params=pltpu.CompilerParams(vmem_limit_bytes=VMEM_LIMIT),
    )(*shards)


class Exchange:
    HBM = pl.BlockSpec(memory_space=pl.ANY)

    def __init__(self, srcs, scatter):
        self.srcs = list(srcs)
        self.scatter = scatter
        self.n = len(self.srcs)
        self.land_shapes = tuple(jax.ShapeDtypeStruct((N_DEV,) + tuple(a.shape[-2:]), a.dtype) for a in self.srcs)
        self.in_specs = [Exchange.HBM] * self.n
        self.out_specs = (Exchange.HBM,) * self.n
        self.sems = [pltpu.SemaphoreType.DMA((N_DEV - 1,)), pltpu.SemaphoreType.DMA((N_DEV - 1,)),
                     pltpu.SemaphoreType.DMA] * self.n

    def _copies(self, src_refs, land_refs, sems):
        me = _me()
        mi = _flat(me)
        local, sends, recvs = [], [], []
        for a, (src_ref, land_ref) in enumerate(zip(src_refs, land_refs)):
            send_sems, recv_sems, local_sem = sems[3 * a:3 * a + 3]
            pick = (lambda p, r=src_ref: r.at[_flat(p)]) if self.scatter else (lambda p, r=src_ref: r)
            local.append(pltpu.make_async_copy(pick(me), land_ref.at[mi], local_sem))
            for k in range(1, N_DEV):
                peer = _peer(me, k)
                pair = dict(send_sem=send_sems.at[k - 1], recv_sem=recv_sems.at[k - 1], device_id=peer,
                            device_id_type=MESH_ID)
                sends.append(pltpu.make_async_remote_copy(src_ref=pick(peer), dst_ref=land_ref.at[mi], **pair))
                recvs.append(pltpu.make_async_remote_copy(src_ref=pick(peer), dst_ref=land_ref.at[_flat(peer)],
                                                          **pair))
        return local, sends, recvs

    def start(self, src_refs, land_refs, sems):
        local, sends, _ = self._copies(src_refs, land_refs, sems)
        for cp in local + sends:
            cp.start()

    def wait(self, src_refs, land_refs, sems):
        local, sends, recvs = self._copies(src_refs, land_refs, sems)
        for cp in recvs:
            cp.wait_recv()
        for cp in sends:
            cp.wait_send()
        for cp in local:
            cp.wait()


def exchange_blocks(ex, name):
    def body(*refs):
        src_refs, land_refs, sems = refs[:ex.n], refs[ex.n:2 * ex.n], refs[2 * ex.n:]
        ex.start(src_refs, land_refs, sems)
        ex.wait(src_refs, land_refs, sems)

    return pl.pallas_call(
        body, name=name, out_shape=ex.land_shapes, in_specs=ex.in_specs, out_specs=ex.out_specs,
        scratch_shapes=list(ex.sems),
    )(*ex.srcs)


def ada_forward(c8, ada_w, bias_cols):
    d = c8.shape[1]
    w = ada_w.shape[2]

    def body(c_ref, w_ref, b_ref, call_ref, modp_ref, part_ref, s1, r1, s2, r2):
        me = _me()
        mi = _flat(me)
        call_ref[mi] = c_ref[...]
        gather = []
        for k in range(1, N_DEV):
            gather.append(pltpu.make_async_remote_copy(
                src_ref=c_ref, dst_ref=call_ref.at[mi], send_sem=s1.at[k - 1], recv_sem=r1.at[k - 1],
                device_id=_peer(me, k), device_id_type=MESH_ID))
        for cp in gather:
            cp.start()
        for k in range(1, N_DEV):
            pltpu.make_async_remote_copy(
                src_ref=c_ref, dst_ref=call_ref.at[_flat(_peer(me, k))], send_sem=s1.at[k - 1],
                recv_sem=r1.at[k - 1], device_id=_peer(me, k), device_id_type=MESH_ID).wait_recv()
        ca = _silu(call_ref[...].reshape(N_DEV * 8, d))
        for l in range(2):
            part = _mm(ca, w_ref[l]) + b_ref[l]
            for b in range(N_DEV):
                part_ref[b, l] = part[8 * b:8 * b + 8, :]
        modp_ref[mi] = part_ref[mi]
        spread = []
        for k in range(1, N_DEV):
            peer = _peer(me, k)
            spread.append(pltpu.make_async_remote_copy(
                src_ref=part_ref.at[_flat(peer)], dst_ref=modp_ref.at[mi], send_sem=s2.at[k - 1],
                recv_sem=r2.at[k - 1], device_id=peer, device_id_type=MESH_ID))
        for cp in spread:
            cp.start()
        for k in range(1, N_DEV):
            pi = _flat(_peer(me, k))
            pltpu.make_async_remote_copy(
                src_ref=part_ref.at[pi], dst_ref=modp_ref.at[pi], send_sem=s2.at[k - 1],
                recv_sem=r2.at[k - 1], device_id=_peer(me, k), device_id_type=MESH_ID).wait_recv()
        for cp in gather + spread:
            cp.wait_send()

    vm = pl.BlockSpec(memory_space=pltpu.VMEM)
    return pl.pallas_call(
        body, name="ada_forward",
        out_shape=(jax.ShapeDtypeStruct((N_DEV, 8, d), F32), jax.ShapeDtypeStruct((N_DEV, 2, 8, w), F32)),
        in_specs=[vm, vm, vm], out_specs=(vm, vm),
        scratch_shapes=[pltpu.VMEM((N_DEV, 2, 8, w), F32)] + [pltpu.SemaphoreType.DMA((7,))] * 4,
        compiler_params=pltpu.CompilerParams(vmem_limit_bytes=VMEM_LIMIT),
    )(c8, ada_w, bias_cols)


def _modulated(x, mod_ref, nw_ref):
    xn = x * _rms(x)
    g1 = nw_ref[...] * (1.0 + mod_ref[1:2, :])
    return xn, g1, xn * g1 + mod_ref[0:1, :]


def even_in_forward(x, mod, nw, w_in, qn, kn, qln, kvln, w_uq, w_uk, cos_a, sin_a, cos_t, sin_t):
    s, d = x.shape
    tm = min(ROW_TILE, s)

    def body(x_ref, mod_ref, nw_ref, w_ref, qn_ref, kn_ref, qln_ref, kvln_ref, uq_ref, uk_ref,
             ca_ref, sa_ref, ct_ref, st_ref,
             qa_o, ka_o, va_o, qb_o, kb_o, kat_o, vat_o, kbt_o, qa_raw_o, ka_raw_o, cq_raw_o, ckv_raw_o, ga_o, gb_o):
        _, _, h = _modulated(x_ref[...], mod_ref, nw_ref)
        h = h.astype(MXU)

        def proj(cols):
            return _mm_nt(h, w_ref[cols[0]:cols[1], :])

        ca, sa, ct, st = ca_ref[...], sa_ref[...], ct_ref[...], st_ref[...]
        qa = proj(E_QA)
        qa_raw_o[...] = qa
        for hh in range(A_HEADS):
            xh = qa[:, HD * hh:HD * hh + HD]
            qa_o[hh] = _rope(xh * _rms(xh) * qn_ref[...], ca, sa, 32).astype(MXU)
        ka = proj(E_KA)
        ka_raw_o[...] = ka
        va = proj(E_VA)
        for g in range(A_KV):
            xh = ka[:, HD * g:HD * g + HD]
            kr = _rope(xh * _rms(xh) * kn_ref[...], ca, sa, 32)
            vh = va[:, HD * g:HD * g + HD]
            ka_o[g] = kr.astype(MXU)
            va_o[g] = vh.astype(MXU)
            kat_o[g] = kr.T.astype(MXU)
            vat_o[g] = vh.T.astype(MXU)
        ga_o[...] = proj(E_GA)
        gb_o[...] = proj(E_GB)
        cq = proj(E_CQ)
        cq_raw_o[...] = cq
        qb = _mm_nt(cq * _rms(cq) * qln_ref[...], uq_ref[...])
        for hh in range(B_HEADS):
            base = (B_NOPE + B_ROPE) * hh
            qb_o[hh, :, 0:B_KV_LORA] = _mm_nt(qb[:, base:base + B_NOPE], uk_ref[hh]).astype(MXU)
            qb_o[hh, :, B_KV_LORA:B_QK] = _rope(qb[:, base + B_NOPE:base + B_NOPE + B_ROPE], ct, st, 32).astype(MXU)
        ckv = proj(E_CKV)
        ckv_raw_o[...] = ckv
        ckv_n = ckv * _rms(ckv) * kvln_ref[...]
        k_rope = _rope(proj(E_KR), ct, st, 32)
        kb_o[0, :, 0:B_KV_LORA] = ckv_n.astype(MXU)
        kb_o[0, :, B_KV_LORA:B_QK] = k_rope.astype(MXU)
        kbt_o[0, 0:B_KV_LORA, :] = ckv_n.T.astype(MXU)
        kbt_o[0, B_KV_LORA:B_QK, :] = k_rope.T.astype(MXU)

    sd = jax.ShapeDtypeStruct
    outs = (sd((A_HEADS, s, HD), MXU), sd((A_KV, s, HD), MXU), sd((A_KV, s, HD), MXU),
            sd((B_HEADS, s, B_QK), MXU), sd((1, s, B_QK), MXU),
            sd((A_KV, HD, s), MXU), sd((A_KV, HD, s), MXU), sd((1, B_QK, s), MXU),
            sd((s, 512), F32), sd((s, 128), F32), sd((s, B_Q_LORA), F32), sd((s, B_KV_LORA), F32),
            sd((s, 512), F32), sd((s, 512), F32))
    out_specs = (_head_spec(A_HEADS, tm, HD), _head_spec(A_KV, tm, HD), _head_spec(A_KV, tm, HD),
                 _head_spec(B_HEADS, tm, B_QK), _head_spec(1, tm, B_QK),
                 _headt_spec(A_KV, HD, tm), _headt_spec(A_KV, HD, tm), _headt_spec(1, B_QK, tm),
                 _row_spec(tm, 512), _row_spec(tm, 128), _row_spec(tm, B_Q_LORA), _row_spec(tm, B_KV_LORA),
                 _row_spec(tm, 512), _row_spec(tm, 512))
    in_specs = [_row_spec(tm, d), _full_spec(mod.shape), _full_spec(nw.shape), _full_spec(w_in.shape),
                _full_spec(qn.shape), _full_spec(kn.shape), _full_spec(qln.shape), _full_spec(kvln.shape),
                _full_spec(w_uq.shape), _full_spec(w_uk.shape),
                _row_spec(tm, HD), _row_spec(tm, HD), _row_spec(tm, B_ROPE), _row_spec(tm, B_ROPE)]
    return pl.pallas_call(
        body, name="even_in_forward", grid=(s // tm,), out_shape=outs, in_specs=in_specs, out_specs=out_specs,
        compiler_params=_params(("parallel",)),
    )(x, mod, nw, w_in, qn, kn, qln, kvln, w_uq, w_uk, cos_a, sin_a, cos_t, sin_t)


def odd_in_forward(x, mod, nw, w_in):
    s, d = x.shape
    tm = min(ROW_TILE, s)

    def body(x_ref, mod_ref, nw_ref, w_ref, q_o, k_o, v_o, kt_o, vt_o, g_o):
        _, _, h = _modulated(x_ref[...], mod_ref, nw_ref)
        h = h.astype(MXU)

        def proj(cols):
            return _mm_nt(h, w_ref[cols[0]:cols[1], :])

        q = proj(O_Q)
        for hh in range(C_HEADS):
            q_o[hh] = q[:, HD * hh:HD * hh + HD].astype(MXU)
        k = proj(O_K)
        v = proj(O_V)
        for g in range(C_KV):
            kh = k[:, HD * g:HD * g + HD]
            vh = v[:, HD * g:HD * g + HD]
            k_o[g] = kh.astype(MXU)
            v_o[g] = vh.astype(MXU)
            kt_o[g] = kh.T.astype(MXU)
            vt_o[g] = vh.T.astype(MXU)
        g_o[...] = proj(O_G)

    sd = jax.ShapeDtypeStruct
    return pl.pallas_call(
        body, name="odd_in_forward", grid=(s // tm,),
        out_shape=(sd((C_HEADS, s, HD), MXU), sd((C_KV, s, HD), MXU), sd((C_KV, s, HD), MXU),
                   sd((C_KV, HD, s), MXU), sd((C_KV, HD, s), MXU), sd((s, 1024), F32)),
        in_specs=[_row_spec(tm, d), _full_spec(mod.shape), _full_spec(nw.shape), _full_spec(w_in.shape)],
        out_specs=(_head_spec(C_HEADS, tm, HD), _head_spec(C_KV, tm, HD), _head_spec(C_KV, tm, HD),
                   _headt_spec(C_KV, HD, tm), _headt_spec(C_KV, HD, tm), _row_spec(tm, 1024)),
        compiler_params=_params(("parallel",)),
    )(x, mod, nw, w_in)


def latent_out_forward(o_lat, w_uv):
    s = o_lat.shape[0]
    tm = min(ROW_TILE, s)

    def body(o_ref, uv_ref, out_ref):
        for hh in range(B_HEADS):
            out_ref[:, HD * hh:HD * hh + HD] = _mm(o_ref[:, B_KV_LORA * hh:B_KV_LORA * (hh + 1)], uv_ref[hh])

    return pl.pallas_call(
        body, name="latent_out_forward", grid=(s // tm,),
        out_shape=jax.ShapeDtypeStruct((s, B_HEADS * HD), F32),
        in_specs=[_row_spec(tm, o_lat.shape[1]), _full_spec(w_uv.shape)],
        out_specs=_row_spec(tm, B_HEADS * HD),
        compiler_params=_params(("parallel",)),
    )(o_lat, w_uv)


def mixer_out_forward(x, mod, pairs, w_out, name):
    s, d = x.shape
    tm = min(ROW_TILE, s)
    n = len(pairs)
    widths = [o.shape[1] for o, _ in pairs]

    def body(*refs):
        x_ref, mod_ref, w_ref = refs[:3]
        pr = refs[3:3 + 2 * n]
        xo_ref, y_ref = refs[3 + 2 * n:]
        y = jnp.zeros((tm, d), F32)
        r0 = 0
        for i in range(n):
            mix = pr[2 * i][...] * _silu(pr[2 * i + 1][...])
            y = y + _mm(mix, w_ref[r0:r0 + widths[i], :])
            r0 += widths[i]
        y_ref[...] = y
        xo_ref[...] = x_ref[...] + mod_ref[2:3, :] * y

    flat = [a for p in pairs for a in p]
    sd = jax.ShapeDtypeStruct
    return pl.pallas_call(
        body, name=name, grid=(s // tm,),
        out_shape=(sd((s, d), F32), sd((s, d), F32)),
        in_specs=[_row_spec(tm, d), _full_spec(mod.shape), _full_spec(w_out.shape)]
        + [_row_spec(tm, a.shape[1]) for a in flat],
        out_specs=(_row_spec(tm, d), _row_spec(tm, d)),
        compiler_params=_params(("parallel",)),
    )(x, mod, w_out, *flat)


LOG2E = 1.4426950408889634
ONES_ROWS = 16


def _col_max8(s3):
    m8 = jnp.max(s3, axis=0)
    return jnp.broadcast_to(jnp.max(m8, axis=0, keepdims=True), m8.shape)


def _with_ones(vt, n):
    return jnp.concatenate([vt, jnp.ones((ONES_ROWS, n), vt.dtype)], axis=0)


def _grid_edges(grid):
    ids = [pl.program_id(a) for a in range(len(grid))]
    first = functools.reduce(jnp.logical_and, [i == 0 for i in ids])
    last = functools.reduce(jnp.logical_and, [i == n - 1 for i, n in zip(ids, grid)])
    return first, last


def flash_forward(q, k, vt, *, scale, dv, tq, tk, nsub, name, exchange=None):
    hq, s, dq = q.shape
    g_kv = k.shape[0]
    hpg = hq // g_kv
    nq = s // tq
    tkk = tk * nsub
    nk = s // tkk
    grid = (g_kv, nq, nk)
    hosted = exchange is not None
    m_cols = hpg * tq
    c = scale * LOG2E
    dvp = dv + ONES_ROWS

    def body(*refs):
        nx =exchange.n if hosted else 0
        q_ref, k_ref, vt_ref = refs[:3]
        xs_refs = refs[3:3 + nx]
        o_ref, lse_ref = refs[3 + nx:5 + nx]
        land_refs = refs[5 + nx:5 + 2 * nx]
        m_s, acc_s = refs[5 + 2 * nx:7 + 2 * nx]
        sems = refs[7 + 2 * nx:]
        if hosted:
            first, last = _grid_edges(grid)
            pl.when(first)(lambda: exchange.start(xs_refs, land_refs, sems))
        j = pl.program_id(2)

        @pl.when(j == 0)
        def _():
            m_s[...] = jnp.full((8, m_cols), -jnp.inf, F32)
            acc_s[...] = jnp.zeros((dvp, m_cols), F32)

        qq = q_ref[...].reshape(m_cols, dq)
        sts = [_mm_nt(k_ref[0, tk * u:tk * (u + 1), :], qq).reshape(tk // 8, 8, m_cols)
               for u in range(nsub)]
        m_run = m_s[...]
        acc = acc_s[...]
        for u in range(nsub):
            m_new = jnp.maximum(m_run, _col_max8(sts[u]) * c)
            p = jnp.exp2(sts[u] * c - m_new[None])
            alpha = jnp.exp2(m_run - m_new)
            pv = _mm(_with_ones(vt_ref[0, 0:dv, tk * u:tk * (u + 1)], tk), p.reshape(tk, m_cols))
            acc = (acc.reshape(dvp // 8, 8, m_cols) * alpha[None]).reshape(dvp, m_cols) + pv
            m_run = m_new
        acc_s[...] = acc
        m_s[...] = m_run

        @pl.when(j == nk - 1)
        def _():
            l = acc_s[dv:dv + 1, :]
            ot = acc_s[0:dv, :] / l
            lse = m_s[0:1, :] + jnp.log2(l)
            for hh in range(hpg):
                o_ref[:, dv * hh:dv * hh + dv] = ot[:, tq * hh:tq * hh + tq].T
                lse_ref[hh] = lse[:, tq * hh:tq * hh + tq]

        if hosted:
            pl.when(last)(lambda: exchange.wait(xs_refs, land_refs, sems))

    sd = jax.ShapeDtypeStruct
    return pl.pallas_call(
        body, name=name, grid=grid,
        out_shape=(sd((s, hq * dv), F32), sd((hq, 1, s), F32)) + (exchange.land_shapes if hosted else ()),
        in_specs=[pl.BlockSpec((hpg, tq, dq), lambda g, i, j: (g, i, 0)),
                  pl.BlockSpec((1, tkk, k.shape[2]), lambda g, i, j: (g, j, 0)),
                  pl.BlockSpec((1, dv, tkk), lambda g, i, j: (g, 0, j))] + (exchange.in_specs if hosted else []),
        out_specs=(pl.BlockSpec((tq, hpg * dv), lambda g, i, j: (i, g)),
                   pl.BlockSpec((hpg, 1, tq), lambda g, i, j: (g, 0, i))) + (exchange.out_specs if hosted else ()),
        scratch_shapes=[pltpu.VMEM((8, m_cols), F32), pltpu.VMEM((dvp, m_cols), F32)]
        + (list(exchange.sems) if hosted else []),
        compiler_params=_params(("arbitrary",) * 3 if hosted else ("parallel", "parallel", "arbitrary")),
    )(q, k, vt, *(exchange.srcs if hosted else []))


def _window_bias_t(hpg, slope_ref):
    t = WINDOW
    r = lax.broadcasted_iota(jnp.int32, (3 * t, t), 0)
    cq = lax.broadcasted_iota(jnp.int32, (3 * t, t), 1)
    arel = jnp.abs(r - t - cq)
    base = jnp.where(arel <= WINDOW, arel.astype(F32) * (-LOG2E), -jnp.inf)
    return jnp.concatenate([base * slope_ref[hh] for hh in range(hpg)], axis=1)


def _window_edges_t(bias, no_before, no_after):
    t = WINDOW
    r = lax.broadcasted_iota(jnp.int32, bias.shape, 0)
    out = ((r < t) & no_before) | ((r >= 2 * t) & no_after)
    return jnp.where(out, -jnp.inf, bias)


def _window_specs(kind, nb, nblk, d):
    t = WINDOW
    before = lambda i: jnp.clip(i * nb - 1, 0, nblk - 1)
    after = lambda i: jnp.clip((i + 1) * nb, 0, nblk - 1)
    if kind == "rows":
        return [pl.BlockSpec((1, t, d), lambda g, i: (g, before(i), 0)),
                pl.BlockSpec((1, nb * t, d), lambda g, i: (g, i, 0)),
                pl.BlockSpec((1, t, d), lambda g, i: (g, after(i), 0))]
    return [pl.BlockSpec((1, d, t), lambda g, i: (g, 0, before(i))),
            pl.BlockSpec((1, d, nb * t), lambda g, i: (g, 0, i)),
            pl.BlockSpec((1, d, t), lambda g, i: (g, 0, after(i)))]


def window_forward(q, k, vt, sink2, slopes, nb, name):
    hq, s, d = q.shape
    g_kv = k.shape[0]
    hpg = hq // g_kv
    t = WINDOW
    nblk = s // t
    steps = nblk // nb
    m_cols = hpg * t
    c = (d ** -0.5) * LOG2E

    def body(q_ref, kp, ko, kn, vp, vo, vn, sink_ref, slope_ref, o_ref, lse_ref):
        i = pl.program_id(1)
        kk_all = jnp.concatenate([kp[0], ko[0], kn[0]], axis=0)
        vt_all = jnp.concatenate([vp[0], vo[0], vn[0]], axis=1)
        bias = _window_bias_t(hpg, slope_ref)
        sink_row = jnp.concatenate([jnp.broadcast_to(sink_ref[hh], (8, t)) for hh in range(hpg)], axis=1)
        sts = []
        for u in range(nb):
            qq = q_ref[:, t * u:t * (u + 1), :].reshape(m_cols, d)
            b_u = bias
            if u == 0 or u == nb - 1:
                b_u = _window_edges_t(bias, (i == 0) if u == 0 else False,
                                      (i == steps - 1) if u == nb - 1 else False)
            sts.append(_mm_nt(kk_all[t * u:t * (u + 3), :], qq) * c + b_u)
        for u in range(nb):
            s3 = sts[u].reshape(3 * t // 8, 8, m_cols)
            m8 = jnp.maximum(_col_max8(s3), sink_row)
            p = jnp.exp2(s3 - m8[None]).reshape(3 * t, m_cols)
            acc = _mm(_with_ones(vt_all[:, t * u:t * (u + 3)], 3 * t), p)
            l = acc[d:d + 1, :] + jnp.exp2(sink_row[0:1, :] - m8[0:1, :])
            ot = acc[0:d, :] / l
            lse = m8[0:1, :] + jnp.log2(l)
            for hh in range(hpg):
                o_ref[t * u:t * (u + 1), d * hh:d * hh + d] = ot[:, t * hh:t * hh + t].T
                lse_ref[hh, :, t * u:t * (u + 1)] = lse[:, t * hh:t * hh + t]

    sd = jax.ShapeDtypeStruct
    return pl.pallas_call(
        body, name=name, grid=(g_kv, steps),
        out_shape=(sd((s, hq * d), F32), sd((hq, 1, s), F32)),
        in_specs=[pl.BlockSpec((hpg, nb * t, d), lambda g, i: (g, i, 0))]
        + _window_specs("rows", nb, nblk, d) + _window_specs("cols", nb, nblk, d)
        + [pl.BlockSpec((hpg, 1, 1), lambda g, i: (g, 0, 0))] * 2,
        out_specs=(pl.BlockSpec((nb * t, hpg * d), lambda g, i: (i, g)),
                   pl.BlockSpec((hpg, 1, nb * t), lambda g, i: (g, 0, i))),
        compiler_params=_params(("parallel", "parallel")),
    )(q, k, k, k, vt, vt, vt, sink2, slopes)


def window_backward(q, k, kt, v, do, lse, delta, slopes, nb, name):
    hq, s, d = q.shape
    g_kv = k.shape[0]
    hpg = hq // g_kv
    t = WINDOW
    nblk = s // t
    steps = nblk // nb
    m_cols = hpg * t
    scale = d ** -0.5
    c = scale * LOG2E

    def body(q_ref, kp, ko, kn, ktp, kto, ktn, vp, vo, vn, do_ref, lse_ref, dl_ref, slope_ref,
             dq_ref, dk_ref, dv_ref, dk_s, dv_s):
        i = pl.program_id(1)

        @pl.when(i == 0)
        def _():
            dk_ref[...] = jnp.zeros(dk_ref.shape, F32)
            dv_ref[...] = jnp.zeros(dv_ref.shape, F32)

        dk_s[...] = jnp.zeros(dk_s.shape, F32)
        dv_s[...] = jnp.zeros(dv_s.shape, F32)
        kk_all = jnp.concatenate([kp[0], ko[0], kn[0]], axis=0)
        vv_all = jnp.concatenate([vp[0], vo[0], vn[0]], axis=0)
        kkt_all = jnp.concatenate([ktp[0], kto[0], ktn[0]], axis=1)
        bias = _window_bias_t(hpg, slope_ref)
        qqs, dds, sts, dps = [], [], [], []
        for u in range(nb):
            rows = slice(t * u, t * (u + 1))
            keys = slice(t * u, t * (u + 3))
            qqs.append(q_ref[:, rows, :].reshape(m_cols, d))
            dds.append(jnp.concatenate([do_ref[rows, d * hh:d * hh + d] for hh in range(hpg)], axis=0))
            b_u = bias
            if u == 0 or u == nb - 1:
                b_u = _window_edges_t(bias, (i == 0) if u == 0 else False,
                                      (i == steps - 1) if u == nb - 1 else False)
            sts.append(_mm_nt(kk_all[keys, :], qqs[u]) * c + b_u)
            dps.append(_mm_nt(vv_all[keys, :], dds[u]))
        for u in range(nb):
            rows = slice(t * u, t * (u + 1))
            keys = slice(t * u, t * (u + 3))
            lse_row = jnp.concatenate([lse_ref[hh, :, rows] for hh in range(hpg)], axis=1)
            dl_row = jnp.concatenate([dl_ref[hh, :, rows] for hh in range(hpg)], axis=1)
            p = jnp.exp2(sts[u] - lse_row)
            ds = p * (dps[u] - dl_row) * scale
            dv_s[keys, :] += _mm(p, dds[u])
            dk_s[keys, :] += _mm(ds, qqs[u])
            dqt = _mm(kkt_all[:, keys], ds)
            for hh in range(hpg):
                dq_ref[rows, d * hh:d * hh + d] = dqt[:, t * hh:t * hh + t].T
        tq = nb * t
        for src, r0, n in ((0, jnp.clip(i * nb - 1, 0, nblk - 1) * t, t), (t, i * tq, tq),
                           (t + tq, jnp.clip((i + 1) * nb, 0, nblk - 1) * t, t)):
            dst = pl.ds(pl.multiple_of(r0, t), n)
            dk_ref[0, dst, :] += dk_s[src:src + n, :]
            dv_ref[0, dst, :] += dv_s[src:src + n, :]

    row_map = lambda g, i: (g, 0, i)
    sd = jax.ShapeDtypeStruct
    return pl.pallas_call(
        body, name=name, grid=(g_kv, steps),
        out_shape=(sd((s, hq * d), F32), sd((g_kv, s, d), F32), sd((g_kv, s, d), F32)),
        in_specs=[pl.BlockSpec((hpg, nb * t, d), lambda g, i: (g, i, 0))]
        + _window_specs("rows", nb, nblk, d) + _window_specs("cols", nb, nblk, d) + _window_specs("rows", nb, nblk, d)
        + [pl.BlockSpec((nb * t, hpg * d), lambda g, i: (i, g)), pl.BlockSpec((hpg, 1, nb * t), row_map),
           pl.BlockSpec((hpg, 1, nb * t), row_map), pl.BlockSpec((hpg, 1, 1), lambda g, i: (g, 0, 0))],
        out_specs=(pl.BlockSpec((nb * t, hpg * d), lambda g, i: (i, g)),
                   pl.BlockSpec((1, s, d), lambda g, i: (g, 0, 0)),
                   pl.BlockSpec((1, s, d), lambda g, i: (g, 0, 0))),
        scratch_shapes=[pltpu.VMEM(((nb + 2) * t, d), F32), pltpu.VMEM(((nb + 2) * t, d), F32)],
        compiler_params=_params(("parallel", "arbitrary")),
    )(q, k, k, k, kt, kt, kt, v, v, v, do, lse, delta, slopes)


def flash_backward(q, k, kt, v, do, lse, delta, *, scale, dv, tq, tk, nsub, gq, name, exchange=None):
    hq, s, dq = q.shape
    g_kv = k.shape[0]
    hpg = hq // gq
    nq = s // tq
    tqq = tq * nsub
    nqs = s // tqq
    nkb = s // tk
    grid = (gq, nkb, nqs)
    hosted = exchange is not None
    m_cols = hpg * tq
    c = scale * LOG2E
    has_v = v is not None

    def body(*refs):
        it = iter(refs)
        q_ref, k_ref, kt_ref = next(it), next(it), next(it)
        v_ref = next(it) if has_v else None
        do_ref, lse_ref, dl_ref = next(it), next(it), next(it)
        nx = exchange.n if hosted else 0
        xs_refs = [next(it) for _ in range(nx)]
        dq_ref, dk_ref, dv_ref = next(it), next(it), next(it)
        land_refs = [next(it) for _ in range(nx)]
        dqt_s = next(it)
        sems = list(it)
        kj = pl.program_id(1)
        qi = pl.program_id(2)
        if hosted:
            first, last = _grid_edges(grid)
            pl.when(first)(lambda: exchange.start(xs_refs, land_refs, sems))

        @pl.when((kj == 0) & (qi == 0))
        def _():
            dqt_s[...] = jnp.zeros(dqt_s.shape, F32)

        @pl.when(qi == 0)
        def _():
            dk_ref[...] = jnp.zeros(dk_ref.shape, F32)
            dv_ref[...] = jnp.zeros(dv_ref.shape, F32)

        kk = k_ref[0]
        vv = v_ref[0] if has_v else kk[:, :dv]
        qqs, dds, sts, dps = [], [], [], []
        for u in range(nsub):
            rows = slice(tq * u, tq * (u + 1))
            qqs.append(q_ref[:, rows, :].reshape(m_cols, dq))
            dds.append(jnp.concatenate([do_ref[rows, dv * hh:dv * hh + dv] for hh in range(hpg)], axis=0))
            sts.append(_mm_nt(kk, qqs[u]))
            dps.append(_mm_nt(vv, dds[u]))
        dv_acc = dv_ref[0]
        dk_acc = dk_ref[0]
        for u in range(nsub):
            rows = slice(tq * u, tq * (u + 1))
            lse_row = jnp.concatenate([lse_ref[hh, :, rows] for hh in range(hpg)], axis=1)
            dl_row = jnp.concatenate([dl_ref[hh, :, rows] for hh in range(hpg)], axis=1)
            p = jnp.exp2(sts[u] * c - lse_row)
            ds = p * (dps[u] - dl_row) * scale
            dv_acc = dv_acc + _mm(p, dds[u])
            dk_acc = dk_acc + _mm(ds, qqs[u])
            dqt = _mm(kt_ref[0], ds)
            for hh in range(hpg):
                dqt_s[qi * nsub + u, dq * hh:dq * hh + dq, :] += dqt[:, tq * hh:tq * hh + tq]
        dv_ref[0] = dv_acc
        dk_ref[0] = dk_acc

        @pl.when((kj == nkb - 1) & (qi == nqs - 1))
        def _():
            def emit(t, carry):
                r0 = pl.multiple_of(t * tq, tq)
                for hh in range(hpg):
                    dq_ref[pl.ds(r0, tq), dq * hh:dq * hh + dq] = dqt_s[t, dq * hh:dq * hh + dq, :].T
                return carry

            lax.fori_loop(0, nq, emit, 0)

        if hosted:
            pl.when(last)(lambda: exchange.wait(xs_refs, land_refs, sems))

    kv_of = lambda g: g * g_kv // gq
    in_specs = [pl.BlockSpec((hpg, tqq, dq), lambda g, kj, qi: (g, qi, 0)),
                pl.BlockSpec((1, tk, dq), lambda g, kj, qi: (kv_of(g), kj, 0)),
                pl.BlockSpec((1, dq, tk), lambda g, kj, qi: (kv_of(g), 0, kj))]
    args = [q, k, kt]
    if has_v:
        in_specs.append(pl.BlockSpec((1, tk, dv), lambda g, kj, qi: (kv_of(g), kj, 0)))
        args.append(v)
    row_map = lambda g, kj, qi: (g, 0, qi)
    in_specs += [pl.BlockSpec((tqq, hpg * dv), lambda g, kj, qi: (qi, g)),
                 pl.BlockSpec((hpg, 1, tqq), row_map), pl.BlockSpec((hpg, 1, tqq), row_map)]
    args += [do, lse, delta]
    if hosted:
        in_specs += exchange.in_specs
        args += exchange.srcs
    sd = jax.ShapeDtypeStruct
    return pl.pallas_call(
        body, name=name, grid=grid,
        out_shape=(sd((s, hq * dq), F32), sd((gq, s, dq), F32), sd((gq, s, dv), F32))
        + (exchange.land_shapes if hosted else ()),
        in_specs=in_specs,
        out_specs=(pl.BlockSpec((s, hpg * dq), lambda g, kj, qi: (0, g)),
                   pl.BlockSpec((1, tk, dq), lambda g, kj, qi: (g, kj, 0)),
                   pl.BlockSpec((1, tk, dv), lambda g, kj, qi: (g, kj, 0))) + (exchange.out_specs if hosted else ()),
        scratch_shapes=[pltpu.VMEM((nq, hpg * dq, tq), F32)] + (list(exchange.sems) if hosted else []),
        compiler_params=_params(("arbitrary",) * 3 if hosted else ("parallel", "arbitrary", "arbitrary")),
    )(*args)


def loss_head(x, target, fnw):
    s, d = x.shape
    tm = min(ROW_TILE, s)

    def body(x_ref, t_ref, w_ref, lp_ref, dx_ref, dw_ref):
        @pl.when(pl.program_id(0) == 0)
        def _():
            lp_ref[...] = jnp.zeros(lp_ref.shape, F32)
            dw_ref[...] = jnp.zeros(dw_ref.shape, F32)

        x = x_ref[...]
        g = w_ref[...]
        err = x * _rms(x) * g - t_ref[...]
        lp_ref[...] += jnp.sum(err * err, axis=0, keepdims=True)
        dx, dg = _rms_bwd(err * (1.0 / d), x, g)
        dx_ref[...] = dx
        dw_ref[...] += jnp.sum(dg, axis=0, keepdims=True)

    sd = jax.ShapeDtypeStruct
    return pl.pallas_call(
        body, name="loss_head", grid=(s // tm,),
        out_shape=(sd((1, d), F32), sd((s, d), F32), sd((1, d), F32)),
        in_specs=[_row_spec(tm, d), _row_spec(tm, d), _full_spec(fnw.shape)],
        out_specs=(_full_spec((1, d)), _row_spec(tm, d), _full_spec((1, d))),
        compiler_params=_params(("arbitrary",)),
    )(x, target, fnw)


def mixer_out_backward(dx, y, mod, pairs, w_out, delta_heads, name, lse=None, sink=None):
    s, d = dx.shape
    tm = min(ROW_TILE, s)
    n = len(pairs)
    widths = [o.shape[1] for o, _ in pairs]
    n_delta = sum(1 for h in delta_heads if h)
    with_sink = lse is not None

    def body(*refs):
        it = iter(refs)
        dx_ref, y_ref, mod_ref, wt_ref = next(it), next(it), next(it), next(it)
        pr = [next(it) for _ in range(2 * n)]
        lse_ref = next(it) if with_sink else None
        sink_ref = next(it) if with_sink else None
        outs = [next(it) for _ in range(2 * n)]
        dl_refs = [next(it) for _ in range(n_delta)]
        dgate_ref, dw_ref = next(it), next(it)
        dsink_ref = next(it) if with_sink else None

        @pl.when(pl.program_id(0) == 0)
        def _():
            dgate_ref[...] = jnp.zeros(dgate_ref.shape, F32)
            dw_ref[...] = jnp.zeros(dw_ref.shape, F32)
            if with_sink:
                dsink_ref[...] = jnp.zeros(dsink_ref.shape, F32)

        dxo = dx_ref[...]
        dgate_ref[...] += jnp.sum(dxo * y_ref[...], axis=0, keepdims=True)
        dy = (dxo * mod_ref[2:3, :]).astype(MXU)
        dmix = _mm_nt(dy, wt_ref[...])
        r0 = 0
        di = 0
        for i in range(n):
            o = pr[2 * i][...]
            g = pr[2 * i + 1][...]
            dm = dmix[:, r0:r0 + widths[i]]
            sg = _sigmoid(g)
            act = g * sg
            do = dm * act
            outs[2 * i][...] = do.astype(MXU)
            outs[2 * i + 1][...] = (dm * o * (sg * (1.0 + g * (1.0 - sg)))).astype(MXU)
            dw_ref[r0:r0 + widths[i], :] += _mm_tn(o * act, dy)
            if delta_heads[i]:
                dlt = _group_sums_t(do * o, HD)[0:delta_heads[i], :]
                dl_refs[di][...] = dlt
                if with_sink:
                    ps = jnp.exp2(sink_ref[...] - lse_ref[...])
                    dsink_ref[...] += -jnp.sum(ps * dlt, axis=1, keepdims=True)
                di += 1
            r0 += widths[i]

    flat = [a for p in pairs for a in p]
    sd = jax.ShapeDtypeStruct
    in_specs = [_row_spec(tm, d), _row_spec(tm, d), _full_spec(mod.shape), _full_spec(w_out.shape)]
    in_specs += [_row_spec(tm, a.shape[1]) for a in flat]
    args = [dx, y, mod, w_out] + flat
    if with_sink:
        nh = lse.shape[0]
        in_specs += [_rows_spec(nh, tm), _full_spec(sink.shape)]
        args += [lse, sink]
    out_shape = [sd((s, a.shape[1]), MXU) for a in flat]
    out_specs = [_row_spec(tm, a.shape[1]) for a in flat]
    for h in delta_heads:
        if h:
            out_shape.append(sd((h, s), F32))
            out_specs.append(_rows_spec(h, tm))
    out_shape += [sd((1, d), F32), sd((sum(widths), d), F32)]
    out_specs += [_full_spec((1, d)), _full_spec((sum(widths), d))]
    if with_sink:
        out_shape.append(sd((lse.shape[0], 1), F32))
        out_specs.append(_full_spec((lse.shape[0], 1)))
    return pl.pallas_call(
        body, name=name, grid=(s // tm,), out_shape=tuple(out_shape), in_specs=in_specs, out_specs=tuple(out_specs),
        compiler_params=_params(("arbitrary",)),
    )(*args)


def latent_out_backward(d_ob, o_lat, w_uv):
    s = o_lat.shape[0]
    tm = min(ROW_TILE, s)

    def body(d_ref, o_ref, uv_ref, dol_ref, dl_ref, duv_ref, prod_s):
        @pl.when(pl.program_id(0) == 0)
        def _():
            duv_ref[...] = jnp.zeros(duv_ref.shape, F32)

        for hh in range(B_HEADS):
            dh = d_ref[:, HD * hh:HD * hh + HD]
            ol = o_ref[:, B_KV_LORA * hh:B_KV_LORA * (hh + 1)]
            dol = _mm_nt(dh, uv_ref[hh])
            dol_ref[:, B_KV_LORA * hh:B_KV_LORA * (hh + 1)] = dol.astype(MXU)
            prod_s[:, B_KV_LORA * hh:B_KV_LORA * (hh + 1)] = dol * ol
            duv_ref[hh] += _mm_tn(ol, dh)
        dl_ref[...] = _group_sums_t(prod_s[...], B_KV_LORA)[0:B_HEADS, :]

    sd = jax.ShapeDtypeStruct
    return pl.pallas_call(
        body, name="latent_out_backward", grid=(s // tm,),
        out_shape=(sd(o_lat.shape, MXU), sd((B_HEADS, s), F32), sd(w_uv.shape, F32)),
        in_specs=[_row_spec(tm, d_ob.shape[1]), _row_spec(tm, o_lat.shape[1]), _full_spec(w_uv.shape)],
        out_specs=(_row_spec(tm, o_lat.shape[1]), _rows_spec(B_HEADS, tm), _full_spec(w_uv.shape)),
        scratch_shapes=[pltpu.VMEM((tm, o_lat.shape[1]), F32)],
        compiler_params=_params(("arbitrary",)),
    )(d_ob, o_lat, w_uv)


def even_prep_backward(dqa, dka, dva, dqb, dkb, dvb, qa_raw, ka_raw, cq_raw, ckv_raw,
                       qn, kn, qln, kvln, w_uq_t, w_uk, cos_a, sin_a, cos_t, sin_t):
    s = qa_raw.shape[0]
    tm = min(ROW_TILE, s)
    qb_w = B_HEADS * (B_NOPE + B_ROPE)

    def body(dqa_ref, dka_ref, dva_ref, dqb_ref, dkb_ref, dvb_ref, qa_ref, ka_ref, cq_ref, ckv_ref,
             qn_ref, kn_ref, qln_ref, kvln_ref, uqt_ref, uk_ref, ca_ref, sa_ref, ct_ref, st_ref,
             pqa, pka, pva, pcq, pckv, pkr, gqn, gkn, gqln, gkvln, guq, guk, dqb_s):
        @pl.when(pl.program_id(0) == 0)
        def _():
            for r in (gqn, gkn, gqln, gkvln, guq, guk):
                r[...] = jnp.zeros(r.shape, F32)

        ca, sa, ct, st = ca_ref[...], sa_ref[...], ct_ref[...], st_ref[...]
        acc_q = jnp.zeros((1, HD), F32)
        for hh in range(A_HEADS):
            dyn = _rope_t(dqa_ref[:, HD * hh:HD * hh + HD], ca, sa, 32)
            dx, dg = _rms_bwd(dyn, qa_ref[:, HD * hh:HD * hh + HD], qn_ref[...])
            pqa[:, HD * hh:HD * hh + HD] = dx.astype(MXU)
            acc_q = acc_q + jnp.sum(dg, axis=0, keepdims=True)
        gqn[...] += acc_q
        acc_k = jnp.zeros((1, HD), F32)
        for g in range(A_KV):
            dyn = _rope_t(dka_ref[g], ca, sa, 32)
            dx, dg = _rms_bwd(dyn, ka_ref[:, HD * g:HD * g + HD], kn_ref[...])
            pka[:, HD * g:HD * g + HD] = dx.astype(MXU)
            acc_k = acc_k + jnp.sum(dg, axis=0, keepdims=True)
            pva[:, HD * g:HD * g + HD] = dva_ref[g].astype(MXU)
        gkn[...] += acc_k
        cq_raw = cq_ref[...]
        cq_n = cq_raw * _rms(cq_raw) * qln_ref[...]
        qb = _mm_nt(cq_n, uqt_ref[...])
        for hh in range(B_HEADS):
            base = (B_NOPE + B_ROPE) * hh
            dlat = dqb_ref[:, B_QK * hh:B_QK * hh + B_KV_LORA]
            dqb_s[:, base:base + B_NOPE] = _mm(dlat, uk_ref[hh])
            guk[hh] += _mm_tn(dlat, qb[:, base:base + B_NOPE])
            dqb_s[:, base + B_NOPE:base + B_NOPE + B_ROPE] = _rope_t(
                dqb_ref[:, B_QK * hh + B_KV_LORA:B_QK * (hh + 1)], ct, st, 32)
        dqb_all = dqb_s[...]
        guq[...] += _mm_tn(dqb_all, cq_n)
        dx, dg = _rms_bwd(_mm(dqb_all, uqt_ref[...]), cq_raw, qln_ref[...])
        pcq[...] = dx.astype(MXU)
        gqln[...] += jnp.sum(dg, axis=0, keepdims=True)
        dkb_sum = dkb_ref[0] + dkb_ref[1]
        dckv = dkb_sum[:, 0:B_KV_LORA] + dvb_ref[0] + dvb_ref[1]
        dx, dg = _rms_bwd(dckv, ckv_ref[...], kvln_ref[...])
        pckv[...] = dx.astype(MXU)
        gkvln[...] += jnp.sum(dg, axis=0, keepdims=True)
        pkr[...] = _rope_t(dkb_sum[:, B_KV_LORA:B_QK], ct, st, 32).astype(MXU)

    sd = jax.ShapeDtypeStruct
    args = [dqa, dka, dva, dqb, dkb, dvb, qa_raw, ka_raw, cq_raw, ckv_raw,
            qn, kn, qln, kvln, w_uq_t, w_uk, cos_a, sin_a, cos_t, sin_t]
    in_specs = [_row_spec(tm, 512), _head_spec(A_KV, tm, HD), _head_spec(A_KV, tm, HD),
                _row_spec(tm, B_HEADS * B_QK), _head_spec(2, tm, B_QK), _head_spec(2, tm, B_KV_LORA),
                _row_spec(tm, 512), _row_spec(tm, 128), _row_spec(tm, B_Q_LORA), _row_spec(tm, B_KV_LORA),
                _full_spec(qn.shape), _full_spec(kn.shape), _full_spec(qln.shape), _full_spec(kvln.shape),
                _full_spec(w_uq_t.shape), _full_spec(w_uk.shape),
                _row_spec(tm, HD), _row_spec(tm, HD), _row_spec(tm, B_ROPE), _row_spec(tm, B_ROPE)]
    out_shape = (sd((s, 512), MXU), sd((s, 128), MXU), sd((s, 128), MXU), sd((s, B_Q_LORA), MXU),
                 sd((s, B_KV_LORA), MXU), sd((s, B_ROPE), MXU),
                 sd(qn.shape, F32), sd(kn.shape, F32), sd(qln.shape, F32), sd(kvln.shape, F32),
                 sd(w_uq_t.shape, F32), sd(w_uk.shape, F32))
    out_specs = (_row_spec(tm, 512), _row_spec(tm, 128), _row_spec(tm, 128), _row_spec(tm, B_Q_LORA),
                 _row_spec(tm, B_KV_LORA), _row_spec(tm, B_ROPE),
                 _full_spec(qn.shape), _full_spec(kn.shape), _full_spec(qln.shape), _full_spec(kvln.shape),
                 _full_spec(w_uq_t.shape), _full_spec(w_uk.shape))
    return pl.pallas_call(
        body, name="even_prep_backward", grid=(s // tm,), out_shape=out_shape, in_specs=in_specs, out_specs=out_specs,
        scratch_shapes=[pltpu.VMEM((tm, qb_w), F32)],
        compiler_params=_params(("arbitrary",)),
    )(*args)


def in_proj_backward(x, mod, nw, dx_out, pieces, w_in_t, name):
    s, d = x.shape
    tm = min(ROW_TILE, s)
    n_cols = w_in_t.shape[0]
    n = len(pieces)
    cols = [c for _, c in pieces]

    def body(*refs):
        x_ref, mod_ref, nw_ref, dxo_ref, wt_ref = refs[:5]
        p_refs = refs[5:5 + n]
        dx_ref, dw_ref, dv_ref, acc_ref = refs[5 + n:]
        i = pl.program_id(0)

        @pl.when(i == 0)
        def _():
            dw_ref[...] = jnp.zeros(dw_ref.shape, F32)
            acc_ref[...] = jnp.zeros(acc_ref.shape, F32)

        xn, g1, h = _modulated(x_ref[...], mod_ref, nw_ref)
        hb = h.astype(MXU)
        dh = jnp.zeros((tm, d), F32)
        for pr, (c0, c1) in zip(p_refs, cols):
            pc = pr[...].astype(MXU)
            dh = dh + jnp.dot(pc, wt_ref[c0:c1, :], preferred_element_type=F32)
            dw_ref[c0:c1, :] += _mm_tn(pc, hb)
        acc_ref[0:1, :] += jnp.sum(dh, axis=0, keepdims=True)
        acc_ref[1:2, :] += jnp.sum(dh * xn, axis=0, keepdims=True)
        dxn = dh * g1
        x = x_ref[...]
        r = _rms(x)
        dx_ref[...] = dxo_ref[...] + r * (dxn - xn * jnp.mean(dxn * xn, axis=-1, keepdims=True))

        @pl.when(i == pl.num_programs(0) - 1)
        def _():
            dg1 = acc_ref[1:2, :]
            dv_ref[0:1, :] = acc_ref[0:1, :]
            dv_ref[1:2, :] = dg1 * nw_ref[...]
            dv_ref[2:3, :] = dg1 * (1.0 + mod_ref[1:2, :])
            dv_ref[3:4, :] = jnp.zeros((1, d), F32)

    arrs = [a for a, _ in pieces]
    sd = jax.ShapeDtypeStruct
    return pl.pallas_call(
        body, name=name, grid=(s // tm,),
        out_shape=(sd((s, d), F32), sd((n_cols, d), F32), sd((4, d), F32)),
        in_specs=[_row_spec(tm, d), _full_spec(mod.shape), _full_spec(nw.shape), _row_spec(tm, d),
                  _full_spec(w_in_t.shape)] + [_row_spec(tm, a.shape[1]) for a in arrs],
        out_specs=(_row_spec(tm, d), _full_spec((n_cols, d)), _full_spec((4, d))),
        scratch_shapes=[pltpu.VMEM((8, d), F32)],
        compiler_params=_params(("arbitrary",)),
    )(x, mod, nw, dx_out, w_in_t, *arrs)


def ada_weight_grad(c_all, dmod_cols):
    d = c_all.shape[1]
    w = dmod_cols.shape[2]

    def body(c_ref, dm_ref, out_ref):
        ca = _silu(c_ref[...])
        for l in range(2):
            out_ref[l] = _mm_tn(ca, dm_ref[l])

    return pl.pallas_call(
        body, name="ada_weight_grad",
        out_shape=jax.ShapeDtypeStruct((2, d, w), F32),
        compiler_params=pltpu.CompilerParams(vmem_limit_bytes=VMEM_LIMIT),
    )(c_all, dmod_cols)


def adamw_rows(g_slots, w, m, v, name):
    n, r, lanes = g_slots.shape
    fits = [t for t in range(16, r + 1, 16) if r % t == 0 and t * lanes <= ADAM_TILE]
    tr = max(fits) if fits else r
    c1 = 1.0 - ADAM_B1 ** ADAM_STEP
    c2 = 1.0 - ADAM_B2 ** ADAM_STEP

    def body(g_ref, w_ref, m_ref, v_ref, go, do, mo, vo):
        g = g_ref[0].astype(F32)
        for k in range(1, n):
            g = g + g_ref[k].astype(F32)
        m_new = ADAM_B1 * m_ref[...] + (1.0 - ADAM_B1) * g
        v_new = ADAM_B2 * v_ref[...] + (1.0 - ADAM_B2) * (g * g)
        m_hat = m_new / c1
        v_hat = v_new / c2
        go[...] = g
        do[...] = -ADAM_LR * (m_hat / (jnp.sqrt(v_hat) + ADAM_EPS) + ADAM_WD * w_ref[...])
        mo[...] = m_new
        vo[...] = v_new

    row = pl.BlockSpec((tr, lanes), lambda i: (i, 0))
    sd = jax.ShapeDtypeStruct((r, lanes), F32)
    return pl.pallas_call(
        body, name=name, grid=(r // tr,), out_shape=(sd, sd, sd, sd),
        in_specs=[pl.BlockSpec((n, tr, lanes), lambda i: (0, i, 0)), row, row, row],
        out_specs=(row, row, row, row),
        compiler_params=_params(("parallel",)),
    )(g_slots, w, m, v)


def _rope_tables(s):
    def cs(pos, dim):
        inv = ROPE_THETA ** (-np.arange(0, dim, 2, dtype=np.float32) / dim)
        ang = pos.astype(np.float32)[:, None] * inv.astype(np.float32)[None, :]
        return np.cos(ang), np.sin(ang)

    rows = s // GRID_W
    row = np.repeat(np.arange(rows), GRID_W)
    col = np.tile(np.arange(GRID_W), rows)
    cr, sr = cs(row, HD // 2)
    cc, sc = cs(col, HD // 2)
    ct, st = cs(np.arange(s), B_ROPE)
    tables = (np.concatenate([cr, cr, cc, cc], axis=-1), np.concatenate([-sr, sr, -sc, sc], axis=-1),
              np.concatenate([ct, ct], axis=-1), np.concatenate([-st, st], axis=-1))
    return tuple(jnp.asarray(t, F32) for t in tables)


def _rows128(a):
    return a.reshape(-1, 128)


def _even_rows_to_kernel(wt):
    return jnp.concatenate([wt[:1664], wt[1696:], wt[1664:1696]], axis=0)


def _even_rows_to_reference(wt):
    return jnp.concatenate([wt[:1664], wt[2176:], wt[1664:2176]], axis=0)


def _shard_t(w):
    return jnp.transpose(w[0])


def _unshard_t(wt, like):
    return jnp.transpose(wt)[None].reshape(like.shape)


def _pad_rows(flat, rows):
    return jnp.pad(flat, (0, rows * 128 - flat.shape[0])).reshape(rows, 128)


def kernel(x, c, norm_w, ada_w, ada_b, even_w_in, a_q_norm, a_k_norm, b_q_lora_norm, b_kv_lora_norm, b_w_uq, b_w_uk, b_w_uv, even_w_out, odd_w_in, c_sink, odd_w_out, final_norm, loss_target, m_norm_w, m_ada_w, m_ada_b, m_even_w_in, m_a_q_norm, m_a_k_norm, m_b_q_lora_norm, m_b_kv_lora_norm, m_b_w_uq, m_b_w_uk, m_b_w_uv, m_even_w_out, m_odd_w_in, m_c_sink, m_odd_w_out, m_final_norm, v_norm_w, v_ada_w, v_ada_b, v_even_w_in, v_a_q_norm, v_a_k_norm, v_b_q_lora_norm, v_b_kv_lora_norm, v_b_w_uq, v_b_w_uk, v_b_w_uv, v_even_w_out, v_odd_w_in, v_c_sink, v_odd_w_out, v_final_norm):
    s, d = x.shape[1], x.shape[2]
    x0 = x[0]
    target = loss_target[0]
    me_flat = 4 * lax.axis_index("x") + 2 * lax.axis_index("y") + lax.axis_index("c")

    g_in_e, g_uq = all_gather_slots([_shard_t(even_w_in).astype(MXU), _shard_t(b_w_uq).astype(MXU)],
                                    "gather_first_weights")
    wt_in_e = _even_rows_to_kernel(g_in_e.reshape(-1, d))
    wt_uq = g_uq.reshape(-1, B_Q_LORA)
    later_exchange = Exchange([_shard_t(odd_w_in).astype(MXU), even_w_out[0].astype(MXU),
                               odd_w_out[0].astype(MXU)], scatter=False)
    w_uk = jnp.transpose(b_w_uk[0], (1, 0, 2)).astype(MXU)
    w_uv = jnp.transpose(b_w_uv[0], (1, 0, 2)).astype(MXU)

    wcols = ada_w.shape[2]
    bias_cols = lax.dynamic_slice_in_dim(ada_b.reshape(2, N_DEV, wcols), me_flat, 1, axis=1)
    call, modp = ada_forward(jnp.broadcast_to(c, (8, d)), ada_w, bias_cols)
    c_all = call[:, 0, :]
    mod = jnp.transpose(modp[:, :, 0, :], (1, 0, 2)).reshape(2, 3, d)
    mod_e, mod_o = mod[0], mod[1]
    nw_e, nw_o = norm_w[0:1], norm_w[1:2]

    cos_a, sin_a, cos_t, sin_t = _rope_tables(s)
    slopes = (2.0 ** (-8.0 * jnp.arange(1, C_HEADS + 1, dtype=F32) / C_HEADS)).reshape(C_HEADS, 1, 1)
    sink2 = c_sink.reshape(C_HEADS, 1, 1) * LOG2E

    (qa, ka, va, qb, kb, kat, vat, kbt, qa_raw, ka_raw, cq_raw, ckv_raw, ga, gb) = even_in_forward(
        x0, mod_e, nw_e, wt_in_e, a_q_norm, a_k_norm, b_q_lora_norm, b_kv_lora_norm, wt_uq, w_uk,
        cos_a, sin_a, cos_t, sin_t)
    tk_dense = min(512, s)
    tq_dense = min(256, s)
    fwd_sub = min(8, s // tk_dense)
    bwd_sub = min(4, s // tq_dense)
    oa, lse_a, g_in_o, g_out_e, g_out_o = flash_forward(
        qa, ka, vat, scale=HD ** -0.5, dv=HD, tq=tq_dense, tk=tk_dense, nsub=fwd_sub, name="attn_a_fwd",
        exchange=later_exchange)
    wt_in_o = g_in_o.reshape(-1, d)
    w_out_e = g_out_e.reshape(-1, d)
    w_out_o = g_out_o.reshape(-1, d)
    scale_b = (B_NOPE + B_ROPE) ** -0.5
    o_lat, lse_b = flash_forward(qb, kb, kbt, scale=scale_b, dv=B_KV_LORA, tq=min(128, s), tk=tk_dense, nsub=fwd_sub,
                                 name="attn_b_fwd")
    ob = latent_out_forward(o_lat, w_uv)
    x1, y_e = mixer_out_forward(x0, mod_e, [(oa, ga), (ob, gb)], w_out_e, "even_out_fwd")

    qc, kc, vc, kct, vct, gc = odd_in_forward(x1, mod_o, nw_o, wt_in_o)
    win_sub = min(8, s // WINDOW)
    oc, lse_c = window_forward(qc, kc, vct, sink2, slopes, win_sub, "attn_c_fwd")
    x2, y_o = mixer_out_forward(x1, mod_o, [(oc, gc)], w_out_o, "odd_out_fwd")

    loss_lanes, dx2, d_final = loss_head(x2, target, final_norm.reshape(1, d))
    loss_part = (0.5 / d) * jnp.sum(loss_lanes)

    doc, dgc, delta_c, dgate_o, dw_out_o, dsink = mixer_out_backward(
        dx2, y_o, mod_o, [(oc, gc)], w_out_o, [C_HEADS], "odd_out_bwd", lse=lse_c.reshape(C_HEADS, s),
        sink=sink2.reshape(C_HEADS, 1))
    rows3 = lambda t: t.reshape(t.shape[0], 1, s)
    dqc, dkc, dvc = window_backward(qc, kc, kct, vc, doc, lse_c, rows3(delta_c), slopes, win_sub, "attn_c_bwd")
    to_rows = lambda t: jnp.transpose(t, (1, 0, 2)).reshape(s, -1)
    dx1, dwt_in_o, dvec_o = in_proj_backward(
        x1, mod_o, nw_o, dx2, [(dqc, O_Q), (to_rows(dkc), O_K), (to_rows(dvc), O_V), (dgc, O_G)], wt_in_o,
        "odd_in_bwd")

    doa, dga, dob, dgb, delta_a, dgate_e, dw_out_e = mixer_out_backward(
        dx1, y_e, mod_e, [(oa, ga), (ob, gb)], w_out_e, [A_HEADS, 0], "even_out_bwd")
    d_olat, delta_b, dw_uv = latent_out_backward(dob, o_lat, w_uv)
    blocks = lambda g: g.astype(MXU).reshape(N_DEV, g.shape[0] // N_DEV, g.shape[1])
    scatter_odd = Exchange([blocks(dwt_in_o), blocks(dw_out_o)], True)
    scatter_out_e = Exchange([blocks(dw_out_e)], True)
    dqb, dkb, dvb, l_in_o, l_out_o = flash_backward(
        qb, kb, kbt, None, d_olat, lse_b, rows3(delta_b), scale=scale_b, dv=B_KV_LORA,
        tq=tq_dense, tk=tk_dense, nsub=bwd_sub, gq=2, name="attn_b_bwd", exchange=scatter_odd)
    dqa, dka, dva, l_out_e = flash_backward(
        qa, ka, kat, va, doa, lse_a, rows3(delta_a), scale=HD ** -0.5, dv=HD,
        tq=tq_dense, tk=tk_dense, nsub=bwd_sub, gq=A_KV, name="attn_a_bwd", exchange=scatter_out_e)
    (pqa, pka, pva, pcq, pckv, pkr, g_qn, g_kn, g_qln, g_kvln, dwt_uq, dw_uk) = even_prep_backward(
        dqa, dka, dva, dqb, dkb, dvb, qa_raw, ka_raw, cq_raw, ckv_raw,
        a_q_norm, a_k_norm, b_q_lora_norm, b_kv_lora_norm, wt_uq, w_uk, cos_a, sin_a, cos_t, sin_t)
    dx0, dwt_in_e, dvec_e = in_proj_backward(
        x0, mod_e, nw_e, dx1,
        [(pqa, E_QA), (pka, E_KA), (pva, E_VA), (dga, E_GA), (pcq, E_CQ), (pckv, E_CKV), (dgb, E_GB), (pkr, E_KR)],
        wt_in_e, "even_in_bwd")

    dmod = jnp.stack([jnp.concatenate([dvec_e[0], dvec_e[1], dgate_e[0]]),
                      jnp.concatenate([dvec_o[0], dvec_o[1], dgate_o[0]])])
    d_norm_w = jnp.stack([dvec_e[2], dvec_o[2]])
    small_names = ["norm_w", "ada_b", "a_q_norm", "a_k_norm", "b_q_lora_norm", "b_kv_lora_norm", "b_w_uk", "b_w_uv",
                   "c_sink", "final_norm"]
    small_w = [norm_w, ada_b, a_q_norm, a_k_norm, b_q_lora_norm, b_kv_lora_norm, b_w_uk, b_w_uv, c_sink, final_norm]
    small_m = [m_norm_w, m_ada_b, m_a_q_norm, m_a_k_norm, m_b_q_lora_norm, m_b_kv_lora_norm, m_b_w_uk, m_b_w_uv,
               m_c_sink, m_final_norm]
    small_v = [v_norm_w, v_ada_b, v_a_q_norm, v_a_k_norm, v_b_q_lora_norm, v_b_kv_lora_norm, v_b_w_uk, v_b_w_uv,
               v_c_sink, v_final_norm]
    small_g = [d_norm_w, dmod, g_qn, g_kn, g_qln, g_kvln, jnp.transpose(dw_uk, (1, 0, 2)), jnp.transpose(dw_uv, (1, 0, 2)),
               dsink, d_final]
    sizes = [w.size for w in small_w]
    n_small = sum(sizes)
    r_small = -(-(n_small + 1) // (128 * 8)) * 8
    flat_pack = lambda arrs: _pad_rows(jnp.concatenate([a.reshape(-1) for a in arrs]), r_small)
    (g_small_all,) = all_gather_slots([flat_pack(small_g + [loss_part])], "gather_small_grads")
    sm = adamw_rows(g_small_all, flat_pack(small_w), flat_pack(small_m), flat_pack(small_v), "adamw_small")
    loss = sm[0].reshape(-1)[n_small]

    def unpack_small(packed):
        flat = packed.reshape(-1)
        out, o = {}, 0
        for nm, w, sz in zip(small_names, small_w, sizes):
            out[nm] = flat[o:o + sz].reshape(w.shape)
            o += sz
        return out

    sm = [unpack_small(p) for p in sm]

    dmod_all = g_small_all.reshape(N_DEV, -1)[:, sizes[0]:sizes[0] + sizes[1]].reshape(N_DEV, 2, N_DEV, wcols)
    dmod_cols = lax.dynamic_slice_in_dim(dmod_all, me_flat, 1, axis=2)[:, :, 0, :]
    pad16 = lambda a: jnp.concatenate([a, jnp.zeros_like(a)], axis=0)
    g_ada_w = ada_weight_grad(pad16(c_all), jnp.transpose(pad16(dmod_cols), (1, 0, 2)))
    ada = adamw_rows(_rows128(g_ada_w)[None], _rows128(ada_w), _rows128(m_ada_w), _rows128(v_ada_w), "adamw_ada_w")
    ada = [p.reshape(ada_w.shape) for p in ada]

    l_in_e, l_uq = exchange_blocks(
        Exchange([blocks(_even_rows_to_reference(dwt_in_e)), blocks(dwt_uq)], True), "scatter_first_weight_grads")
    bg = [{}, {}, {}, {}]
    for nm, landed, w, m, v, transposed in (
            ("even_w_in", l_in_e, even_w_in, m_even_w_in, v_even_w_in, True),
            ("b_w_uq", l_uq, b_w_uq, m_b_w_uq, v_b_w_uq, True),
            ("odd_w_in", l_in_o, odd_w_in, m_odd_w_in, v_odd_w_in, True),
            ("even_w_out", l_out_e, even_w_out, m_even_w_out, v_even_w_out, False),
            ("odd_w_out", l_out_o, odd_w_out, m_odd_w_out, v_odd_w_out, False)):
        view = _shard_t if transposed else (lambda a: a[0])
        res = adamw_rows(landed, view(w), view(m), view(v), "adamw_" + nm)
        for kind, p in enumerate(res):
            bg[kind][nm] = _unshard_t(p, w) if transposed else p[None]
    big_names = ["even_w_in", "odd_w_in", "even_w_out", "odd_w_out", "b_w_uq"]

    order = ["norm_w", "ada_w", "ada_b", "even_w_in", "a_q_norm", "a_k_norm", "b_q_lora_norm", "b_kv_lora_norm",
             "b_w_uq", "b_w_uk", "b_w_uv", "even_w_out", "odd_w_in", "c_sink", "odd_w_out", "final_norm"]

    def pick(kind):
        out = []
        for nm in order:
            if nm == "ada_w":
                out.append(ada[kind])
            elif nm in big_names:
                out.append(bg[kind][nm])
            else:
                out.append(sm[kind][nm])
        return out

    return (loss, dx0[None], *pick(0), *pick(1), *pick(2), *pick(3))
```

```python
import functools

import jax
import jax.numpy as jnp
import numpy as np
from jax import lax
from jax.experimental import pallas as pl
from jax.experimental.pallas import tpu as pltpu

F32 = jnp.float32
MXU = jnp.bfloat16
EPS = 1e-6
ROPE_THETA = 10000.0
GRID_W = 64
HD = 64
N_DEV = 8
MESH_AXES = ("x", "y", "c")

A_HEADS, A_KV = 8, 2
B_HEADS, B_NOPE, B_ROPE, B_Q_LORA, B_KV_LORA = 8, 64, 32, 256, 128
B_QK = B_KV_LORA + B_ROPE
C_HEADS, C_KV = 16, 4
WINDOW = 128

ADAM_LR, ADAM_B1, ADAM_B2, ADAM_EPS, ADAM_WD, ADAM_STEP = 0.001, 0.9, 0.999, 1e-08, 0.01, 10

ROW_TILE = 256
ADAM_TILE = 2048 * 128
VMEM_LIMIT = 56 * 1024 * 1024

E_QA, E_KA, E_VA, E_GA, E_CQ, E_CKV, E_GB, E_KR = (
    (0, 512), (512, 640), (640, 768), (768, 1280), (1280, 1536), (1536, 1664), (1664, 2176), (2176, 2208))
EVEN_IN = 2208
O_Q, O_K, O_V, O_G = (0, 1024), (1024, 1280), (1280, 1536), (1536, 2560)
ODD_IN = 2560


def _mm(a, b):
    return jnp.dot(a.astype(MXU), b.astype(MXU), preferred_element_type=F32)


def _mm_nt(a, b):
    return lax.dot_general(a.astype(MXU), b.astype(MXU), (((1,), (1,)), ((), ())), preferred_element_type=F32)


def _mm_tn(a, b):
    return lax.dot_general(a.astype(MXU), b.astype(MXU), (((0,), (0,)), ((), ())), preferred_element_type=F32)


def _group_sums_t(prod, group):
    tm, w = prod.shape
    sel = (lax.broadcasted_iota(jnp.int32, (w, 128), 0) // group
           == lax.broadcasted_iota(jnp.int32, (w, 128), 1)).astype(MXU)
    hi = prod.astype(MXU)
    lo = prod - hi.astype(F32)
    return (_mm(hi, sel) + _mm(lo, sel)).T


def _sigmoid(z):
    return 1.0 / (1.0 + jnp.exp(-z))


def _silu(z):
    return z * _sigmoid(z)


def _rms(x):
    return lax.rsqrt(jnp.mean(x * x, axis=-1, keepdims=True) + EPS)


def _swap_halves(y, group):
    n = y.shape[-1]
    half = group // 2
    fwd = pltpu.roll(y, half, 1)
    if n == group:
        return fwd
    back = pltpu.roll(y, n - half, 1)
    lane = lax.broadcasted_iota(jnp.int32, y.shape, 1)
    return jnp.where((lane % group) < half, back, fwd)


def _rope(y, cos, sin, group):
    return y * cos + _swap_halves(y, group) * sin


def _rope_t(d, cos, sin, group):
    return d * cos - _swap_halves(d, group) * sin


def _rms_bwd(dy, x, g):
    r = _rms(x)
    xhat = x * r
    dxhat = dy * g
    dx = r * (dxhat - xhat * jnp.mean(dxhat * xhat, axis=-1, keepdims=True))
    return dx, dy * xhat


def _group_mean(v, bd, group):
    hi = v.astype(MXU)
    lo = v - hi.astype(F32)
    return (_mm(hi, bd[...]) + _mm(lo, bd[...])) * (1.0 / group)


def _head_norm(x, g, bd, group):
    return x * lax.rsqrt(_group_mean(x * x, bd, group) + EPS) * g


def _head_norm_bwd(dy, x, g, bd, group):
    r = lax.rsqrt(_group_mean(x * x, bd, group) + EPS)
    xhat = x * r
    dxhat = dy * g
    dx = r * (dxhat - xhat * _group_mean(dxhat * xhat, bd, group))
    return dx, dy * xhat


def _params(sem, vmem=VMEM_LIMIT):
    return pltpu.CompilerParams(dimension_semantics=sem, vmem_limit_bytes=vmem)


def _row_spec(tm, w):
    return pl.BlockSpec((tm, w), lambda i: (i, 0))


def _full_spec(shape):
    nd = len(shape)
    return pl.BlockSpec(shape, lambda i: (0,) * nd)


def _head_spec(h, tm, w):
    return pl.BlockSpec((h, tm, w), lambda i: (0, i, 0))


def _headt_spec(h, w, tm):
    return pl.BlockSpec((h, w, tm), lambda i: (0, 0, i))


def _rows_spec(h, tm):
    return pl.BlockSpec((h, tm), lambda i: (0, i))


def _me():
    return lax.axis_index("x"), lax.axis_index("y"), lax.axis_index("c")


def _flat(p):
    return 4 * p[0] + 2 * p[1] + p[2]


def _peer(me, k):
    x, y, c = me
    return (1 - x if k & 4 else x, 1 - y if k & 2 else y, 1 - c if k & 1 else c)


MESH_ID = pl.DeviceIdType.MESH


def all_gather_slots(shards, name):
    n = len(shards)

    def body(*refs):
        x_refs, out_refs = refs[:n], refs[n:2 * n]
        send_sems, recv_sems, local_sems = refs[2 * n:]
        me = _me()
        x, y, c = me
        sibling = (x, y, 1 - c)
        chips = [(1 - x, y), (x, 1 - y), (1 - x, 1 - y)]

        def copy(a, k, block, to, src=None):
            slot = out_refs[a].at[_flat(block)]
            return pltpu.make_async_remote_copy(
                src_ref=slot if src is None else src, dst_ref=slot, send_sem=send_sems.at[7 * a + k],
                recv_sem=recv_sems.at[7 * a + k], device_id=to, device_id_type=MESH_ID)

        mine = [pltpu.make_async_copy(x_refs[a], out_refs[a].at[_flat(me)], local_sems.at[a]) for a in range(n)]
        for cp in mine:
            cp.start()
        first = [copy(a, 0, me, sibling, src=x_refs[a]) for a in range(n)]
        first += [copy(a, 1 + j, me, (*chip, c), src=x_refs[a]) for a in range(n) for j, chip in enumerate(chips)]
        for cp in first:
            cp.start()
        passed = []
        for a in range(n):
            for j, chip in enumerate(chips):
                copy(a, 1 + j, (*chip, c), me).wait_recv()
                passed.append(copy(a, 4 + j, (*chip, c), sibling))
                passed[-1].start()
        for a in range(n):
            copy(a, 0, sibling, me).wait_recv()
            for j, chip in enumerate(chips):
                copy(a, 4 + j, (*chip, 1 - c), me).wait_recv()
        for cp in first + passed:
            cp.wait_send()
        for cp in mine:
            cp.wait()

    vm = pl.BlockSpec(memory_space=pltpu.VMEM)
    return pl.pallas_call(
        body, name=name,
        out_shape=tuple(jax.ShapeDtypeStruct((N_DEV,) + a.shape, a.dtype) for a in shards),
        in_specs=[vm] * n, out_specs=(vm,) * n,
        scratch_shapes=[pltpu.SemaphoreType.DMA((7 * n,)), pltpu.SemaphoreType.DMA((7 * n,)),
                        pltpu.SemaphoreType.DMA((n,))],
        compiler_params=pltpu.CompilerParams(vmem_limit_bytes=VMEM_LIMIT),
    )(*shards)


class Exchange:
    HBM = pl.BlockSpec(memory_space=pl.ANY)

    def __init__(self, srcs, scatter):
        self.srcs = list(srcs)
        self.scatter = scatter
        self.n = len(self.srcs)
        self.land_shapes = tuple(jax.ShapeDtypeStruct((N_DEV,) + tuple(a.shape[-2:]), a.dtype) for a in self.srcs)
        self.in_specs = [Exchange.HBM] * self.n
        self.out_specs = (Exchange.HBM,) * self.n
        self.sems = [pltpu.SemaphoreType.DMA((N_DEV - 1,)), pltpu.SemaphoreType.DMA((N_DEV - 1,)),
                     pltpu.SemaphoreType.DMA] * self.n

    def _copies(self, src_refs, land_refs, sems):
        me = _me()
        mi = _flat(me)
        local, sends, recvs = [], [], []
        for a, (src_ref, land_ref) in enumerate(zip(src_refs, land_refs)):
            send_sems, recv_sems, local_sem = sems[3 * a:3 * a + 3]
            pick = (lambda p, r=src_ref: r.at[_flat(p)]) if self.scatter else (lambda p, r=src_ref: r)
            local.append(pltpu.make_async_copy(pick(me), land_ref.at[mi], local_sem))
            for k in range(1, N_DEV):
                peer = _peer(me, k)
                pair = dict(send_sem=send_sems.at[k - 1], recv_sem=recv_sems.at[k - 1], device_id=peer,
                            device_id_type=MESH_ID)
                sends.append(pltpu.make_async_remote_copy(src_ref=pick(peer), dst_ref=land_ref.at[mi], **pair))
                recvs.append(pltpu.make_async_remote_copy(src_ref=pick(peer), dst_ref=land_ref.at[_flat(peer)],
                                                          **pair))
        return local, sends, recvs

    def start(self, src_refs, land_refs, sems):
        local, sends, _ = self._copies(src_refs, land_refs, sems)
        for cp in local + sends:
            cp.start()

    def wait(self, src_refs, land_refs, sems):
        local, sends, recvs = self._copies(src_refs, land_refs, sems)
        for cp in recvs:
            cp.wait_recv()
        for cp in sends:
            cp.wait_send()
        for cp in local:
            cp.wait()


def exchange_blocks(ex, name):
    def body(*refs):
        src_refs, land_refs, sems = refs[:ex.n], refs[ex.n:2 * ex.n], refs[2 * ex.n:]
        ex.start(src_refs, land_refs, sems)
        ex.wait(src_refs, land_refs, sems)

    return pl.pallas_call(
        body, name=name, out_shape=ex.land_shapes, in_specs=ex.in_specs, out_specs=ex.out_specs,
        scratch_shapes=list(ex.sems),
    )(*ex.srcs)


def ada_forward(c8, ada_w, bias_cols):
    d = c8.shape[1]
    w = ada_w.shape[2]

    def body(c_ref, w_ref, b_ref, call_ref, modp_ref, part_ref, s1, r1, s2, r2):
        me = _me()
        mi = _flat(me)
        call_ref[mi] = c_ref[...]
        gather = []
        for k in range(1, N_DEV):
            gather.append(pltpu.make_async_remote_copy(
                src_ref=c_ref, dst_ref=call_ref.at[mi], send_sem=s1.at[k - 1], recv_sem=r1.at[k - 1],
                device_id=_peer(me, k), device_id_type=MESH_ID))
        for cp in gather:
            cp.start()
        for k in range(1, N_DEV):
            pltpu.make_async_remote_copy(
                src_ref=c_ref, dst_ref=call_ref.at[_flat(_peer(me, k))], send_sem=s1.at[k - 1],
                recv_sem=r1.at[k - 1], device_id=_peer(me, k), device_id_type=MESH_ID).wait_recv()
        ca = _silu(call_ref[...].reshape(N_DEV * 8, d))
        for l in range(2):
            part = _mm(ca, w_ref[l]) + b_ref[l]
            for b in range(N_DEV):
                part_ref[b, l] = part[8 * b:8 * b + 8, :]
        modp_ref[mi] = part_ref[mi]
        spread = []
        for k in range(1, N_DEV):
            peer = _peer(me, k)
            spread.append(pltpu.make_async_remote_copy(
                src_ref=part_ref.at[_flat(peer)], dst_ref=modp_ref.at[mi], send_sem=s2.at[k - 1],
                recv_sem=r2.at[k - 1], device_id=peer, device_id_type=MESH_ID))
        for cp in spread:
            cp.start()
        for k in range(1, N_DEV):
            pi = _flat(_peer(me, k))
            pltpu.make_async_remote_copy(
                src_ref=part_ref.at[pi], dst_ref=modp_ref.at[pi], send_sem=s2.at[k - 1],
                recv_sem=r2.at[k - 1], device_id=_peer(me, k), device_id_type=MESH_ID).wait_recv()
        for cp in gather + spread:
            cp.wait_send()

    vm = pl.BlockSpec(memory_space=pltpu.VMEM)
    return pl.pallas_call(
        body, name="ada_forward",
        out_shape=(jax.ShapeDtypeStruct((N_DEV, 8, d), F32), jax.ShapeDtypeStruct((N_DEV, 2, 8, w), F32)),
        in_specs=[vm, vm, vm], out_specs=(vm, vm),
        scratch_shapes=[pltpu.VMEM((N_DEV, 2, 8, w), F32)] + [pltpu.SemaphoreType.DMA((7,))] * 4,
        compiler_params=pltpu.CompilerParams(vmem_limit_bytes=VMEM_LIMIT),
    )(c8, ada_w, bias_cols)


def _modulated(x, mod_ref, nw_ref):
    xn = x * _rms(x)
    g1 = nw_ref[...] * (1.0 + mod_ref[1:2, :])
    return xn, g1, xn * g1 + mod_ref[0:1, :]


def even_in_forward(x, mod, nw, w_in_t, gq, gk, qln, kvln, w_uq_t, uk_bd, bd, cos_a, sin_a, cos_t, sin_t):
    s, d = x.shape
    tm = min(ROW_TILE, s)
    n_nope = B_HEADS * B_NOPE

    def body(x_ref, mod_ref, nw_ref, w_ref, gq_ref, gk_ref, qln_ref, kvln_ref, uq_ref, ukbd_ref, bd_ref,
             ca_ref, sa_ref, ct_ref, st_ref,
             qa_o, ka_o, va_o, qb_o, kb_o, kat_o, vat_o, kbt_o, qa_raw_o, ka_raw_o, cq_raw_o, ckv_raw_o, ga_o, gb_o):
        _, _, h = _modulated(x_ref[...], mod_ref, nw_ref)
        h = h.astype(MXU)

        def proj(cols):
            return _mm_nt(h, w_ref[cols[0]:cols[1], :])

        ca, sa, ct, st = ca_ref[...], sa_ref[...], ct_ref[...], st_ref[...]
        wide = lambda t, n: jnp.concatenate([t] * n, axis=1)
        qa = proj(E_QA)
        qa_raw_o[...] = qa
        qr = _rope(_head_norm(qa, gq_ref[...], bd_ref, HD), wide(ca, 4), wide(sa, 4), 32)
        for hh in range(A_HEADS):
            qa_o[hh] = qr[:, HD * hh:HD * hh + HD].astype(MXU)
        ka = proj(E_KA)
        ka_raw_o[...] = ka
        kr = _rope(_head_norm(ka, gk_ref[...], bd_ref[0:128, 0:128], HD), ca, sa, 32)
        va = proj(E_VA)
        krt, vat = kr.T, va.T
        for g in range(A_KV):
            ka_o[g] = kr[:, HD * g:HD * g + HD].astype(MXU)
            va_o[g] = va[:, HD * g:HD * g + HD].astype(MXU)
            kat_o[g] = krt[HD * g:HD * g + HD, :].astype(MXU)
            vat_o[g] = vat[HD * g:HD * g + HD, :].astype(MXU)
        ga_o[...] = proj(E_GA)
        gb_o[...] = proj(E_GB)
        cq = proj(E_CQ)
        cq_raw_o[...] = cq
        qb = _mm_nt(cq * _rms(cq) * qln_ref[...], uq_ref[...])
        q_lat = _mm(qb[:, 0:n_nope], ukbd_ref[...])
        q_rope = _rope(qb[:, n_nope:], wide(ct, 2), wide(st, 2), 32)
        for hh in range(B_HEADS):
            qb_o[hh, :, 0:B_KV_LORA] = q_lat[:, B_KV_LORA * hh:B_KV_LORA * (hh + 1)].astype(MXU)
            qb_o[hh, :, B_KV_LORA:B_QK] = q_rope[:, B_ROPE * hh:B_ROPE * (hh + 1)].astype(MXU)
        ckv = proj(E_CKV)
        ckv_raw_o[...] = ckv
        ckv_n = ckv * _rms(ckv) * kvln_ref[...]
        k_rope = _rope(proj(E_KR), ct[:, 0:B_ROPE], st[:, 0:B_ROPE], 32)
        kb_o[0, :, 0:B_KV_LORA] = ckv_n.astype(MXU)
        kb_o[0, :, B_KV_LORA:B_QK] = k_rope.astype(MXU)
        kbt_o[0, 0:B_KV_LORA, :] = ckv_n.T.astype(MXU)
        kbt_o[0, B_KV_LORA:B_QK, :] = k_rope.T.astype(MXU)

    sd = jax.ShapeDtypeStruct
    outs = (sd((A_HEADS, s, HD), MXU), sd((A_KV, s, HD), MXU), sd((A_KV, s, HD), MXU),
            sd((B_HEADS, s, B_QK), MXU), sd((1, s, B_QK), MXU),
            sd((A_KV, HD, s), MXU), sd((A_KV, HD, s), MXU), sd((1, B_QK, s), MXU),
            sd((s, 512), F32), sd((s, 128), F32), sd((s, B_Q_LORA), F32), sd((s, B_KV_LORA), F32),
            sd((s, 512), F32), sd((s, 512), F32))
    out_specs = (_head_spec(A_HEADS, tm, HD), _head_spec(A_KV, tm, HD), _head_spec(A_KV, tm, HD),
                 _head_spec(B_HEADS, tm, B_QK), _head_spec(1, tm, B_QK),
                 _headt_spec(A_KV, HD, tm), _headt_spec(A_KV, HD, tm), _headt_spec(1, B_QK, tm),
                 _row_spec(tm, 512), _row_spec(tm, 128), _row_spec(tm, B_Q_LORA), _row_spec(tm, B_KV_LORA),
                 _row_spec(tm, 512), _row_spec(tm, 512))
    consts = [mod, nw, w_in_t, gq, gk, qln, kvln, w_uq_t, uk_bd, bd]
    return pl.pallas_call(
        body, name="even_in_forward", grid=(s // tm,), out_shape=outs,
        in_specs=[_row_spec(tm, d)] + [_full_spec(a.shape) for a in consts] + [_row_spec(tm, 128)] * 4,
        out_specs=out_specs, compiler_params=_params(("parallel",)),
    )(x, *consts, cos_a, sin_a, cos_t, sin_t)


def odd_in_forward(x, mod, nw, w_in):
    s, d = x.shape
    tm = min(ROW_TILE, s)

    def body(x_ref, mod_ref, nw_ref, w_ref, q_o, k_o, v_o, kt_o, vt_o, g_o):
        _, _, h = _modulated(x_ref[...], mod_ref, nw_ref)
        h = h.astype(MXU)

        def proj(cols):
            return _mm_nt(h, w_ref[cols[0]:cols[1], :])

        q = proj(O_Q)
        for hh in range(C_HEADS):
            q_o[hh] = q[:, HD * hh:HD * hh + HD].astype(MXU)
        k = proj(O_K)
        v = proj(O_V)
        for g in range(C_KV):
            kh = k[:, HD * g:HD * g + HD]
            vh = v[:, HD * g:HD * g + HD]
            k_o[g] = kh.astype(MXU)
            v_o[g] = vh.astype(MXU)
            kt_o[g] = kh.T.astype(MXU)
            vt_o[g] = vh.T.astype(MXU)
        g_o[...] = proj(O_G)

    sd = jax.ShapeDtypeStruct
    return pl.pallas_call(
        body, name="odd_in_forward", grid=(s // tm,),
        out_shape=(sd((C_HEADS, s, HD), MXU), sd((C_KV, s, HD), MXU), sd((C_KV, s, HD), MXU),
                   sd((C_KV, HD, s), MXU), sd((C_KV, HD, s), MXU), sd((s, 1024), F32)),
        in_specs=[_row_spec(tm, d), _full_spec(mod.shape), _full_spec(nw.shape), _full_spec(w_in.shape)],
        out_specs=(_head_spec(C_HEADS, tm, HD), _head_spec(C_KV, tm, HD), _head_spec(C_KV, tm, HD),
                   _headt_spec(C_KV, HD, tm), _headt_spec(C_KV, HD, tm), _row_spec(tm, 1024)),
        compiler_params=_params(("parallel",)),
    )(x, mod, nw, w_in)


def latent_out_forward(o_lat, w_uv):
    s = o_lat.shape[0]
    tm = min(ROW_TILE, s)

    def body(o_ref, uv_ref, out_ref):
        for hh in range(B_HEADS):
            out_ref[:, HD * hh:HD * hh + HD] = _mm(o_ref[:, B_KV_LORA * hh:B_KV_LORA * (hh + 1)], uv_ref[hh])

    return pl.pallas_call(
        body, name="latent_out_forward", grid=(s // tm,),
        out_shape=jax.ShapeDtypeStruct((s, B_HEADS * HD), F32),
        in_specs=[_row_spec(tm, o_lat.shape[1]), _full_spec(w_uv.shape)],
        out_specs=_row_spec(tm, B_HEADS * HD),
        compiler_params=_params(("parallel",)),
    )(o_lat, w_uv)


def mixer_out_forward(x, mod, pairs, w_out, name):
    s, d = x.shape
    tm = min(ROW_TILE, s)
    n = len(pairs)
    widths = [o.shape[1] for o, _ in pairs]

    def body(*refs):
        x_ref, mod_ref, w_ref = refs[:3]
        pr = refs[3:3 + 2 * n]
        xo_ref, y_ref = refs[3 + 2 * n:]
        y = jnp.zeros((tm, d), F32)
        r0 = 0
        for i in range(n):
            mix = pr[2 * i][...] * _silu(pr[2 * i + 1][...])
            y = y + _mm(mix, w_ref[r0:r0 + widths[i], :])
            r0 += widths[i]
        y_ref[...] = y
        xo_ref[...] = x_ref[...] + mod_ref[2:3, :] * y

    flat = [a for p in pairs for a in p]
    sd = jax.ShapeDtypeStruct
    return pl.pallas_call(
        body, name=name, grid=(s // tm,),
        out_shape=(sd((s, d), F32), sd((s, d), F32)),
        in_specs=[_row_spec(tm, d), _full_spec(mod.shape), _full_spec(w_out.shape)]
        + [_row_spec(tm, a.shape[1]) for a in flat],
        out_specs=(_row_spec(tm, d), _row_spec(tm, d)),
        compiler_params=_params(("parallel",)),
    )(x, mod, w_out, *flat)


LOG2E = 1.4426950408889634
ONES_ROWS = 16


def _col_max8(s3):
    m8 = jnp.max(s3, axis=0)
    return jnp.broadcast_to(jnp.max(m8, axis=0, keepdims=True), m8.shape)


def _with_ones(vt, n):
    return jnp.concatenate([vt, jnp.ones((ONES_ROWS, n), vt.dtype)], axis=0)


def _grid_edges(grid):
    ids = [pl.program_id(a) for a in range(len(grid))]
    first = functools.reduce(jnp.logical_and, [i == 0 for i in ids])
    last = functools.reduce(jnp.logical_and, [i == n - 1 for i, n in zip(ids, grid)])
    return first, last


def flash_forward(q, k, vt, *, scale, dv, tq, tk, nsub, name, exchange=None):
    hq, s, dq = q.shape
    g_kv = k.shape[0]
    hpg = hq // g_kv
    nq = s // tq
    tkk = tk * nsub
    nk = s // tkk
    grid = (g_kv, nq, nk)
    hosted = exchange is not None
    m_cols = hpg * tq
    c = scale * LOG2E
    dvp = dv + ONES_ROWS

    def body(*refs):
        nx =exchange.n if hosted else 0
        q_ref, k_ref, vt_ref = refs[:3]
        xs_refs = refs[3:3 + nx]
        o_ref, lse_ref = refs[3 + nx:5 + nx]
        land_refs = refs[5 + nx:5 + 2 * nx]
        m_s, acc_s = refs[5 + 2 * nx:7 + 2 * nx]
        sems = refs[7 + 2 * nx:]
        if hosted:
            first, last = _grid_edges(grid)
            pl.when(first)(lambda: exchange.start(xs_refs, land_refs, sems))
        j = pl.program_id(2)

        @pl.when(j == 0)
        def _():
            m_s[...] = jnp.full((8, m_cols), -jnp.inf, F32)
            acc_s[...] = jnp.zeros((dvp, m_cols), F32)

        qq = q_ref[...].reshape(m_cols, dq)
        sts = [_mm_nt(k_ref[0, tk * u:tk * (u + 1), :], qq).reshape(tk // 8, 8, m_cols)
               for u in range(nsub)]
        m_run = m_s[...]
        acc = acc_s[...]
        for u in range(nsub):
            m_new = jnp.maximum(m_run, _col_max8(sts[u]) * c)
            p = jnp.exp2(sts[u] * c - m_new[None])
            alpha = jnp.exp2(m_run - m_new)
            pv = _mm(_with_ones(vt_ref[0, 0:dv, tk * u:tk * (u + 1)], tk), p.reshape(tk, m_cols))
            acc = (acc.reshape(dvp // 8, 8, m_cols) * alpha[None]).reshape(dvp, m_cols) + pv
            m_run = m_new
        acc_s[...] = acc
        m_s[...] = m_run

        @pl.when(j == nk - 1)
        def _():
            l = acc_s[dv:dv + 1, :]
            ot = acc_s[0:dv, :] / l
            lse = m_s[0:1, :] + jnp.log2(l)
            for hh in range(hpg):
                o_ref[:, dv * hh:dv * hh + dv] = ot[:, tq * hh:tq * hh + tq].T
                lse_ref[hh] = lse[:, tq * hh:tq * hh + tq]

        if hosted:
            pl.when(last)(lambda: exchange.wait(xs_refs, land_refs, sems))

    sd = jax.ShapeDtypeStruct
    return pl.pallas_call(
        body, name=name, grid=grid,
        out_shape=(sd((s, hq * dv), F32), sd((hq, 1, s), F32)) + (exchange.land_shapes if hosted else ()),
        in_specs=[pl.BlockSpec((hpg, tq, dq), lambda g, i, j: (g, i, 0)),
                  pl.BlockSpec((1, tkk, k.shape[2]), lambda g, i, j: (g, j, 0)),
                  pl.BlockSpec((1, dv, tkk), lambda g, i, j: (g, 0, j))] + (exchange.in_specs if hosted else []),
        out_specs=(pl.BlockSpec((tq, hpg * dv), lambda g, i, j: (i, g)),
                   pl.BlockSpec((hpg, 1, tq), lambda g, i, j: (g, 0, i))) + (exchange.out_specs if hosted else ()),
        scratch_shapes=[pltpu.VMEM((8, m_cols), F32), pltpu.VMEM((dvp, m_cols), F32)]
        + (list(exchange.sems) if hosted else []),
        compiler_params=_params(("arbitrary",) * 3 if hosted else ("parallel", "parallel", "arbitrary")),
    )(q, k, vt, *(exchange.srcs if hosted else []))


def _window_bias_t(hpg, slope_ref):
    t = WINDOW
    r = lax.broadcasted_iota(jnp.int32, (3 * t, t), 0)
    cq = lax.broadcasted_iota(jnp.int32, (3 * t, t), 1)
    arel = jnp.abs(r - t - cq)
    base = jnp.where(arel <= WINDOW, arel.astype(F32) * (-LOG2E), -jnp.inf)
    return jnp.concatenate([base * slope_ref[hh] for hh in range(hpg)], axis=1)


def _window_edges_t(bias, no_before, no_after):
    t = WINDOW
    r = lax.broadcasted_iota(jnp.int32, bias.shape, 0)
    out = ((r < t) & no_before) | ((r >= 2 * t) & no_after)
    return jnp.where(out, -jnp.inf, bias)


def _window_specs(kind, nb, nblk, d):
    t = WINDOW
    before = lambda i: jnp.clip(i * nb - 1, 0, nblk - 1)
    after = lambda i: jnp.clip((i + 1) * nb, 0, nblk - 1)
    if kind == "rows":
        return [pl.BlockSpec((1, t, d), lambda g, i: (g, before(i), 0)),
                pl.BlockSpec((1, nb * t, d), lambda g, i: (g, i, 0)),
                pl.BlockSpec((1, t, d), lambda g, i: (g, after(i), 0))]
    return [pl.BlockSpec((1, d, t), lambda g, i: (g, 0, before(i))),
            pl.BlockSpec((1, d, nb * t), lambda g, i: (g, 0, i)),
            pl.BlockSpec((1, d, t), lambda g, i: (g, 0, after(i)))]


def window_forward(q, k, vt, sink2, slopes, nb, name):
    hq, s, d = q.shape
    g_kv = k.shape[0]
    hpg = hq // g_kv
    t = WINDOW
    nblk = s // t
    steps = nblk // nb
    m_cols = hpg * t
    c = (d ** -0.5) * LOG2E

    def body(q_ref, kp, ko, kn, vp, vo, vn, sink_ref, slope_ref, o_ref, lse_ref):
        i = pl.program_id(1)
        kk_all = jnp.concatenate([kp[0], ko[0], kn[0]], axis=0)
        vt_all = jnp.concatenate([vp[0], vo[0], vn[0]], axis=1)
        bias = _window_bias_t(hpg, slope_ref)
        sink_row = jnp.concatenate([jnp.broadcast_to(sink_ref[hh], (8, t)) for hh in range(hpg)], axis=1)
        sts = []
        for u in range(nb):
            qq = q_ref[:, t * u:t * (u + 1), :].reshape(m_cols, d)
            b_u = bias
            if u == 0 or u == nb - 1:
                b_u = _window_edges_t(bias, (i == 0) if u == 0 else False,
                                      (i == steps - 1) if u == nb - 1 else False)
            sts.append(_mm_nt(kk_all[t * u:t * (u + 3), :], qq) * c + b_u)
        for u in range(nb):
            s3 = sts[u].reshape(3 * t // 8, 8, m_cols)
            m8 = jnp.maximum(_col_max8(s3), sink_row)
            p = jnp.exp2(s3 - m8[None]).reshape(3 * t, m_cols)
            acc = _mm(_with_ones(vt_all[:, t * u:t * (u + 3)], 3 * t), p)
            l = acc[d:d + 1, :] + jnp.exp2(sink_row[0:1, :] - m8[0:1, :])
            ot = acc[0:d, :] / l
            lse = m8[0:1, :] + jnp.log2(l)
            for hh in range(hpg):
                o_ref[t * u:t * (u + 1), d * hh:d * hh + d] = ot[:, t * hh:t * hh + t].T
                lse_ref[hh, :, t * u:t * (u + 1)] = lse[:, t * hh:t * hh + t]

    sd = jax.ShapeDtypeStruct
    return pl.pallas_call(
        body, name=name, grid=(g_kv, steps),
        out_shape=(sd((s, hq * d), F32), sd((hq, 1, s), F32)),
        in_specs=[pl.BlockSpec((hpg, nb * t, d), lambda g, i: (g, i, 0))]
        + _window_specs("rows", nb, nblk, d) + _window_specs("cols", nb, nblk, d)
        + [pl.BlockSpec((hpg, 1, 1), lambda g, i: (g, 0, 0))] * 2,
        out_specs=(pl.BlockSpec((nb * t, hpg * d), lambda g, i: (i, g)),
                   pl.BlockSpec((hpg, 1, nb * t), lambda g, i: (g, 0, i))),
        compiler_params=_params(("parallel", "parallel")),
    )(q, k, k, k, vt, vt, vt, sink2, slopes)


def window_backward(q, k, kt, v, do, lse, delta, slopes, nb, name):
    hq, s, d = q.shape
    g_kv = k.shape[0]
    hpg = hq // g_kv
    t = WINDOW
    nblk = s // t
    steps = nblk // nb
    m_cols = hpg * t
    scale = d ** -0.5
    c = scale * LOG2E

    def body(q_ref, kp, ko, kn, ktp, kto, ktn, vp, vo, vn, do_ref, lse_ref, dl_ref, slope_ref,
             dq_ref, dk_ref, dv_ref, dk_s, dv_s):
        i = pl.program_id(1)

        @pl.when(i == 0)
        def _():
            dk_ref[...] = jnp.zeros(dk_ref.shape, F32)
            dv_ref[...] = jnp.zeros(dv_ref.shape, F32)

        dk_s[...] = jnp.zeros(dk_s.shape, F32)
        dv_s[...] = jnp.zeros(dv_s.shape, F32)
        kk_all = jnp.concatenate([kp[0], ko[0], kn[0]], axis=0)
        vv_all = jnp.concatenate([vp[0], vo[0], vn[0]], axis=0)
        kkt_all = jnp.concatenate([ktp[0], kto[0], ktn[0]], axis=1)
        bias = _window_bias_t(hpg, slope_ref)
        qqs, dds, sts, dps = [], [], [], []
        for u in range(nb):
            rows = slice(t * u, t * (u + 1))
            keys = slice(t * u, t * (u + 3))
            qqs.append(q_ref[:, rows, :].reshape(m_cols, d))
            dds.append(jnp.concatenate([do_ref[rows, d * hh:d * hh + d] for hh in range(hpg)], axis=0))
            b_u = bias
            if u == 0 or u == nb - 1:
                b_u = _window_edges_t(bias, (i == 0) if u == 0 else False,
                                      (i == steps - 1) if u == nb - 1 else False)
            sts.append(_mm_nt(kk_all[keys, :], qqs[u]) * c + b_u)
            dps.append(_mm_nt(vv_all[keys, :], dds[u]))
        for u in range(nb):
            rows = slice(t * u, t * (u + 1))
            keys = slice(t * u, t * (u + 3))
            lse_row = jnp.concatenate([lse_ref[hh, :, rows] for hh in range(hpg)], axis=1)
            dl_row = jnp.concatenate([dl_ref[hh, :, rows] for hh in range(hpg)], axis=1)
            p = jnp.exp2(sts[u] - lse_row)
            ds = p * (dps[u] - dl_row) * scale
            dv_s[keys, :] += _mm(p, dds[u])
            dk_s[keys, :] += _mm(ds, qqs[u])
            dqt = _mm(kkt_all[:, keys], ds)
            for hh in range(hpg):
                dq_ref[rows, d * hh:d * hh + d] = dqt[:, t * hh:t * hh + t].T
        tq = nb * t
        for src, r0, n in ((0, jnp.clip(i * nb - 1, 0, nblk - 1) * t, t), (t, i * tq, tq),
                           (t + tq, jnp.clip((i + 1) * nb, 0, nblk - 1) * t, t)):
            dst = pl.ds(pl.multiple_of(r0, t), n)
            dk_ref[0, dst, :] += dk_s[src:src + n, :]
            dv_ref[0, dst, :] += dv_s[src:src + n, :]

    row_map = lambda g, i: (g, 0, i)
    sd = jax.ShapeDtypeStruct
    return pl.pallas_call(
        body, name=name, grid=(g_kv, steps),
        out_shape=(sd((s, hq * d), F32), sd((g_kv, s, d), F32), sd((g_kv, s, d), F32)),
        in_specs=[pl.BlockSpec((hpg, nb * t, d), lambda g, i: (g, i, 0))]
        + _window_specs("rows", nb, nblk, d) + _window_specs("cols", nb, nblk, d) + _window_specs("rows", nb, nblk, d)
        + [pl.BlockSpec((nb * t, hpg * d), lambda g, i: (i, g)), pl.BlockSpec((hpg, 1, nb * t), row_map),
           pl.BlockSpec((hpg, 1, nb * t), row_map), pl.BlockSpec((hpg, 1, 1), lambda g, i: (g, 0, 0))],
        out_specs=(pl.BlockSpec((nb * t, hpg * d), lambda g, i: (i, g)),
                   pl.BlockSpec((1, s, d), lambda g, i: (g, 0, 0)),
                   pl.BlockSpec((1, s, d), lambda g, i: (g, 0, 0))),
        scratch_shapes=[pltpu.VMEM(((nb + 2) * t, d), F32), pltpu.VMEM(((nb + 2) * t, d), F32)],
        compiler_params=_params(("parallel", "arbitrary")),
    )(q, k, k, k, kt, kt, kt, v, v, v, do, lse, delta, slopes)


def flash_backward(q, k, kt, v, do, lse, delta, *, scale, dv, tq, tk, nsub, gq, name, split=None, exchange=None):
    hq, s, dq = q.shape
    g_kv = k.shape[0]
    hpg = hq // gq
    nq = s // tq
    tqq = tq * nsub
    nqs = s // tqq
    nkb = s // tk
    grid = (gq, nkb, nqs)
    hosted = exchange is not None
    m_cols = hpg * tq
    c = scale * LOG2E
    has_v = v is not None

    def body(*refs):
        it = iter(refs)
        q_ref, k_ref, kt_ref = next(it), next(it), next(it)
        v_ref = next(it) if has_v else None
        do_ref, lse_ref, dl_ref = next(it), next(it), next(it)
        nx = exchange.n if hosted else 0
        xs_refs = [next(it) for _ in range(nx)]
        dq_ref, dk_ref, dv_ref = next(it), next(it), next(it)
        land_refs = [next(it) for _ in range(nx)]
        dqt_s = next(it)
        sems = list(it)
        kj = pl.program_id(1)
        qi = pl.program_id(2)
        if hosted:
            first, last = _grid_edges(grid)
            pl.when(first)(lambda: exchange.start(xs_refs, land_refs, sems))

        @pl.when((kj == 0) & (qi == 0))
        def _():
            dqt_s[...] = jnp.zeros(dqt_s.shape, F32)

        @pl.when(qi == 0)
        def _():
            dk_ref[...] = jnp.zeros(dk_ref.shape, F32)
            dv_ref[...] = jnp.zeros(dv_ref.shape, F32)

        kk = k_ref[0]
        vv = v_ref[0] if has_v else kk[:, :dv]
        qqs, dds, sts, dps = [], [], [], []
        for u in range(nsub):
            rows = slice(tq * u, tq * (u + 1))
            qqs.append(q_ref[:, rows, :].reshape(m_cols, dq))
            dds.append(jnp.concatenate([do_ref[rows, dv * hh:dv * hh + dv] for hh in range(hpg)], axis=0))
            sts.append(_mm_nt(kk, qqs[u]))
            dps.append(_mm_nt(vv, dds[u]))
        dv_acc = dv_ref[0]
        dk_acc = dk_ref[0]
        for u in range(nsub):
            rows = slice(tq * u, tq * (u + 1))
            lse_row = jnp.concatenate([lse_ref[hh, :, rows] for hh in range(hpg)], axis=1)
            dl_row = jnp.concatenate([dl_ref[hh, :, rows] for hh in range(hpg)], axis=1)
            p = jnp.exp2(sts[u] * c - lse_row)
            ds = p * (dps[u] - dl_row) * scale
            dv_acc = dv_acc + _mm(p, dds[u])
            dk_acc = dk_acc + _mm(ds, qqs[u])
            dqt = _mm(kt_ref[0], ds)
            for hh in range(hpg):
                dqt_s[qi * nsub + u, dq * hh:dq * hh + dq, :] += dqt[:, tq * hh:tq * hh + tq]
        dv_ref[0] = dv_acc
        dk_ref[0] = dk_acc

        @pl.when((kj == nkb - 1) & (qi == nqs - 1))
        def _():
            def emit(t, carry):
                r0 = pl.multiple_of(t * tq, tq)
                for hh in range(hpg):
                    blk = dqt_s[t, dq * hh:dq * hh + dq, :].T
                    if split is None:
                        dq_ref[pl.ds(r0, tq), dq * hh:dq * hh + dq] = blk
                    else:
                        rest = dq - split
                        dq_ref[pl.ds(r0, tq), split * hh:split * (hh + 1)] = blk[:, 0:split]
                        dq_ref[pl.ds(r0, tq), hpg * split + rest * hh:hpg * split + rest * (hh + 1)] = blk[:, split:]
                return carry

            lax.fori_loop(0, nq, emit, 0)

        if hosted:
            pl.when(last)(lambda: exchange.wait(xs_refs, land_refs, sems))

    kv_of = lambda g: g * g_kv // gq
    in_specs = [pl.BlockSpec((hpg, tqq, dq), lambda g, kj, qi: (g, qi, 0)),
                pl.BlockSpec((1, tk, dq), lambda g, kj, qi: (kv_of(g), kj, 0)),
                pl.BlockSpec((1, dq, tk), lambda g, kj, qi: (kv_of(g), 0, kj))]
    args = [q, k, kt]
    if has_v:
        in_specs.append(pl.BlockSpec((1, tk, dv), lambda g, kj, qi: (kv_of(g), kj, 0)))
        args.append(v)
    row_map = lambda g, kj, qi: (g, 0, qi)
    in_specs += [pl.BlockSpec((tqq, hpg * dv), lambda g, kj, qi: (qi, g)),
                 pl.BlockSpec((hpg, 1, tqq), row_map), pl.BlockSpec((hpg, 1, tqq), row_map)]
    args += [do, lse, delta]
    if hosted:
        in_specs += exchange.in_specs
        args += exchange.srcs
    sd = jax.ShapeDtypeStruct
    return pl.pallas_call(
        body, name=name, grid=grid,
        out_shape=(sd((s, hq * dq), F32), sd((gq, s, dq), F32), sd((gq, s, dv), F32))
        + (exchange.land_shapes if hosted else ()),
        in_specs=in_specs,
        out_specs=(pl.BlockSpec((s, hpg * dq), lambda g, kj, qi: (0, g)),
                   pl.BlockSpec((1, tk, dq), lambda g, kj, qi: (g, kj, 0)),
                   pl.BlockSpec((1, tk, dv), lambda g, kj, qi: (g, kj, 0))) + (exchange.out_specs if hosted else ()),
        scratch_shapes=[pltpu.VMEM((nq, hpg * dq, tq), F32)] + (list(exchange.sems) if hosted else []),
        compiler_params=_params(("arbitrary",) * 3 if hosted else ("parallel", "arbitrary", "arbitrary")),
    )(*args)


def loss_head(x, target, fnw):
    s, d = x.shape
    tm = min(ROW_TILE, s)

    def body(x_ref, t_ref, w_ref, lp_ref, dx_ref, dw_ref):
        @pl.when(pl.program_id(0) == 0)
        def _():
            lp_ref[...] = jnp.zeros(lp_ref.shape, F32)
            dw_ref[...] = jnp.zeros(dw_ref.shape, F32)

        x = x_ref[...]
        g = w_ref[...]
        err = x * _rms(x) * g - t_ref[...]
        lp_ref[...] += jnp.sum(err * err, axis=0, keepdims=True)
        dx, dg = _rms_bwd(err * (1.0 / d), x, g)
        dx_ref[...] = dx
        dw_ref[...] += jnp.sum(dg, axis=0, keepdims=True)

    sd = jax.ShapeDtypeStruct
    return pl.pallas_call(
        body, name="loss_head", grid=(s // tm,),
        out_shape=(sd((1, d), F32), sd((s, d), F32), sd((1, d), F32)),
        in_specs=[_row_spec(tm, d), _row_spec(tm, d), _full_spec(fnw.shape)],
        out_specs=(_full_spec((1, d)), _row_spec(tm, d), _full_spec((1, d))),
        compiler_params=_params(("arbitrary",)),
    )(x, target, fnw)


def mixer_out_backward(dx, y, mod, pairs, w_out, delta_heads, name, lse=None, sink=None):
    s, d = dx.shape
    tm = min(ROW_TILE, s)
    n = len(pairs)
    widths = [o.shape[1] for o, _ in pairs]
    n_delta = sum(1 for h in delta_heads if h)
    with_sink = lse is not None

    def body(*refs):
        it = iter(refs)
        dx_ref, y_ref, mod_ref, wt_ref = next(it), next(it), next(it), next(it)
        pr = [next(it) for _ in range(2 * n)]
        lse_ref = next(it) if with_sink else None
        sink_ref = next(it) if with_sink else None
        outs = [next(it) for _ in range(2 * n)]
        dl_refs = [next(it) for _ in range(n_delta)]
        dgate_ref, dw_ref = next(it), next(it)
        dsink_ref = next(it) if with_sink else None

        @pl.when(pl.program_id(0) == 0)
        def _():
            dgate_ref[...] = jnp.zeros(dgate_ref.shape, F32)
            dw_ref[...] = jnp.zeros(dw_ref.shape, F32)
            if with_sink:
                dsink_ref[...] = jnp.zeros(dsink_ref.shape, F32)

        dxo = dx_ref[...]
        dgate_ref[...] += jnp.sum(dxo * y_ref[...], axis=0, keepdims=True)
        dy = (dxo * mod_ref[2:3, :]).astype(MXU)
        dmix = _mm_nt(dy, wt_ref[...])
        r0 = 0
        di = 0
        for i in range(n):
            o = pr[2 * i][...]
            g = pr[2 * i + 1][...]
            dm = dmix[:, r0:r0 + widths[i]]
            sg = _sigmoid(g)
            act = g * sg
            do = dm * act
            outs[2 * i][...] = do.astype(MXU)
            outs[2 * i + 1][...] = (dm * o * (sg * (1.0 + g * (1.0 - sg)))).astype(MXU)
            dw_ref[r0:r0 + widths[i], :] += _mm_tn(o * act, dy)
            if delta_heads[i]:
                dlt = _group_sums_t(do * o, HD)[0:delta_heads[i], :]
                dl_refs[di][...] = dlt
                if with_sink:
                    ps = jnp.exp2(sink_ref[...] - lse_ref[...])
                    dsink_ref[...] += -jnp.sum(ps * dlt, axis=1, keepdims=True)
                di += 1
            r0 += widths[i]

    flat = [a for p in pairs for a in p]
    sd = jax.ShapeDtypeStruct
    in_specs = [_row_spec(tm, d), _row_spec(tm, d), _full_spec(mod.shape), _full_spec(w_out.shape)]
    in_specs += [_row_spec(tm, a.shape[1]) for a in flat]
    args = [dx, y, mod, w_out] + flat
    if with_sink:
        nh = lse.shape[0]
        in_specs += [_rows_spec(nh, tm), _full_spec(sink.shape)]
        args += [lse, sink]
    out_shape = [sd((s, a.shape[1]), MXU) for a in flat]
    out_specs = [_row_spec(tm, a.shape[1]) for a in flat]
    for h in delta_heads:
        if h:
            out_shape.append(sd((h, s), F32))
            out_specs.append(_rows_spec(h, tm))
    out_shape += [sd((1, d), F32), sd((sum(widths), d), F32)]
    out_specs += [_full_spec((1, d)), _full_spec((sum(widths), d))]
    if with_sink:
        out_shape.append(sd((lse.shape[0], 1), F32))
        out_specs.append(_full_spec((lse.shape[0], 1)))
    return pl.pallas_call(
        body, name=name, grid=(s // tm,), out_shape=tuple(out_shape), in_specs=in_specs, out_specs=tuple(out_specs),
        compiler_params=_params(("arbitrary",)),
    )(*args)


def latent_out_backward(d_ob, o_lat, w_uv):
    s = o_lat.shape[0]
    tm = min(ROW_TILE, s)

    def body(d_ref, o_ref, uv_ref, dol_ref, dl_ref, duv_ref, prod_s):
        @pl.when(pl.program_id(0) == 0)
        def _():
            duv_ref[...] = jnp.zeros(duv_ref.shape, F32)

        for hh in range(B_HEADS):
            dh = d_ref[:, HD * hh:HD * hh + HD]
            ol = o_ref[:, B_KV_LORA * hh:B_KV_LORA * (hh + 1)]
            dol = _mm_nt(dh, uv_ref[hh])
            dol_ref[:, B_KV_LORA * hh:B_KV_LORA * (hh + 1)] = dol.astype(MXU)
            prod_s[:, B_KV_LORA * hh:B_KV_LORA * (hh + 1)] = dol * ol
            duv_ref[hh] += _mm_tn(ol, dh)
        dl_ref[...] = _group_sums_t(prod_s[...], B_KV_LORA)[0:B_HEADS, :]

    sd = jax.ShapeDtypeStruct
    return pl.pallas_call(
        body, name="latent_out_backward", grid=(s // tm,),
        out_shape=(sd(o_lat.shape, MXU), sd((B_HEADS, s), F32), sd(w_uv.shape, F32)),
        in_specs=[_row_spec(tm, d_ob.shape[1]), _row_spec(tm, o_lat.shape[1]), _full_spec(w_uv.shape)],
        out_specs=(_row_spec(tm, o_lat.shape[1]), _rows_spec(B_HEADS, tm), _full_spec(w_uv.shape)),
        scratch_shapes=[pltpu.VMEM((tm, o_lat.shape[1]), F32)],
        compiler_params=_params(("arbitrary",)),
    )(d_ob, o_lat, w_uv)


def even_prep_backward(dqa, dka, dva, dqb, dkb, dvb, qa_raw, ka_raw, cq_raw, ckv_raw,
                       gq, gk, qln, kvln, w_uq_t, uk_bd, bd, cos_a, sin_a, cos_t, sin_t):
    s = qa_raw.shape[0]
    tm = min(ROW_TILE, s)
    half_lat = B_KV_LORA * B_HEADS // 2
    half_w = dqb.shape[1] // 2

    def body(dqa_ref, dka_ref, dva_ref, dqb_ref, dkb_ref, dvb_ref, qa_ref, ka_ref, cq_ref, ckv_ref,
             gq_ref, gk_ref, qln_ref, kvln_ref, uqt_ref, ukbd_ref, bd_ref, ca_ref, sa_ref, ct_ref, st_ref,
             pqa, pka, pva, pcq, pckv, pkr, gqn, gkn, gqln, gkvln, guq, guk):
        @pl.when(pl.program_id(0) == 0)
        def _():
            for r in (gqn, gkn, gqln, gkvln, guq, guk):
                r[...] = jnp.zeros(r.shape, F32)

        ca, sa, ct, st = ca_ref[...], sa_ref[...], ct_ref[...], st_ref[...]
        wide = lambda t, n: jnp.concatenate([t] * n, axis=1)
        rows = lambda a: jnp.sum(a, axis=0, keepdims=True)
        dx, dg = _head_norm_bwd(_rope_t(dqa_ref[...], wide(ca, 4), wide(sa, 4), 32), qa_ref[...], gq_ref[...],
                                bd_ref, HD)
        pqa[...] = dx.astype(MXU)
        gqn[...] += rows(dg)
        dk_all = jnp.concatenate([dka_ref[g] for g in range(A_KV)], axis=1)
        dx, dg = _head_norm_bwd(_rope_t(dk_all, ca, sa, 32), ka_ref[...], gk_ref[...], bd_ref[0:128, 0:128], HD)
        pka[...] = dx.astype(MXU)
        gkn[...] += rows(dg)
        pva[...] = jnp.concatenate([dva_ref[g] for g in range(A_KV)], axis=1).astype(MXU)
        cq_raw = cq_ref[...]
        cq_n = cq_raw * _rms(cq_raw) * qln_ref[...]
        qb = _mm_nt(cq_n, uqt_ref[...])
        d_lat = jnp.concatenate([dqb_ref[:, 0:half_lat], dqb_ref[:, half_w:half_w + half_lat]], axis=1)
        d_rope = jnp.concatenate([dqb_ref[:, half_lat:half_w], dqb_ref[:, half_w + half_lat:]], axis=1)
        for hh in range(B_HEADS):
            guk[hh] += _mm_tn(d_lat[:, B_KV_LORA * hh:B_KV_LORA * (hh + 1)], qb[:, B_NOPE * hh:B_NOPE * (hh + 1)])
        dqb_all = jnp.concatenate([_mm_nt(d_lat, ukbd_ref[...]),
                                   _rope_t(d_rope, wide(ct, 2), wide(st, 2), 32)], axis=1)
        guq[...] += _mm_tn(dqb_all, cq_n)
        dx, dg = _rms_bwd(_mm(dqb_all, uqt_ref[...]), cq_raw, qln_ref[...])
        pcq[...] = dx.astype(MXU)
        gqln[...] += rows(dg)
        dkb_sum = dkb_ref[0] + dkb_ref[1]
        dckv = dkb_sum[:, 0:B_KV_LORA] + dvb_ref[0] + dvb_ref[1]
        dx, dg = _rms_bwd(dckv, ckv_ref[...], kvln_ref[...])
        pckv[...] = dx.astype(MXU)
        gkvln[...] += rows(dg)
        pkr[...] = _rope_t(dkb_sum[:, B_KV_LORA:B_QK], ct[:, 0:B_ROPE], st[:, 0:B_ROPE], 32).astype(MXU)

    sd = jax.ShapeDtypeStruct
    consts = [gq, gk, qln, kvln, w_uq_t, uk_bd, bd]
    in_specs = [_row_spec(tm, 512), _head_spec(A_KV, tm, HD), _head_spec(A_KV, tm, HD),
                _row_spec(tm, dqb.shape[1]), _head_spec(2, tm, B_QK), _head_spec(2, tm, B_KV_LORA),
                _row_spec(tm, 512), _row_spec(tm, 128), _row_spec(tm, B_Q_LORA), _row_spec(tm, B_KV_LORA)]
    in_specs += [_full_spec(a.shape) for a in consts] + [_row_spec(tm, 128)] * 4
    small = [sd(gq.shape, F32), sd(gk.shape, F32), sd(qln.shape, F32), sd(kvln.shape, F32), sd(w_uq_t.shape, F32),
             sd((B_HEADS, B_KV_LORA, B_NOPE), F32)]
    out_shape = (sd((s, 512), MXU), sd((s, 128), MXU), sd((s, 128), MXU), sd((s, B_Q_LORA), MXU),
                 sd((s, B_KV_LORA), MXU), sd((s, B_ROPE), MXU), *small)
    out_specs = (_row_spec(tm, 512), _row_spec(tm, 128), _row_spec(tm, 128), _row_spec(tm, B_Q_LORA),
                 _row_spec(tm, B_KV_LORA), _row_spec(tm, B_ROPE), *[_full_spec(a.shape) for a in small])
    return pl.pallas_call(
        body, name="even_prep_backward", grid=(s // tm,), out_shape=out_shape, in_specs=in_specs, out_specs=out_specs,
        compiler_params=_params(("arbitrary",)),
    )(dqa, dka, dva, dqb, dkb, dvb, qa_raw, ka_raw, cq_raw, ckv_raw, *consts, cos_a, sin_a, cos_t, sin_t)


def in_proj_backward(x, mod, nw, dx_out, pieces, w_in_t, name):
    s, d = x.shape
    tm = min(ROW_TILE, s)
    n_cols = w_in_t.shape[0]
    n = len(pieces)
    cols = [c for _, c in pieces]

    def body(*refs):
        x_ref, mod_ref, nw_ref, dxo_ref, wt_ref = refs[:5]
        p_refs = refs[5:5 + n]
        dx_ref, dw_ref, dv_ref, acc_ref = refs[5 + n:]
        i = pl.program_id(0)

        @pl.when(i == 0)
        def _():
            dw_ref[...] = jnp.zeros(dw_ref.shape, F32)
            acc_ref[...] = jnp.zeros(acc_ref.shape, F32)

        xn, g1, h = _modulated(x_ref[...], mod_ref, nw_ref)
        hb = h.astype(MXU)
        dh = jnp.zeros((tm, d), F32)
        for pr, (c0, c1) in zip(p_refs, cols):
            pc = pr[...].astype(MXU)
            dh = dh + jnp.dot(pc, wt_ref[c0:c1, :], preferred_element_type=F32)
            dw_ref[c0:c1, :] += _mm_tn(pc, hb)
        acc_ref[0:1, :] += jnp.sum(dh, axis=0, keepdims=True)
        acc_ref[1:2, :] += jnp.sum(dh * xn, axis=0, keepdims=True)
        dxn = dh * g1
        x = x_ref[...]
        r = _rms(x)
        dx_ref[...] = dxo_ref[...] + r * (dxn - xn * jnp.mean(dxn * xn, axis=-1, keepdims=True))

        @pl.when(i == pl.num_programs(0) - 1)
        def _():
            dg1 = acc_ref[1:2, :]
            dv_ref[0:1, :] = acc_ref[0:1, :]
            dv_ref[1:2, :] = dg1 * nw_ref[...]
            dv_ref[2:3, :] = dg1 * (1.0 + mod_ref[1:2, :])
            dv_ref[3:4, :] = jnp.zeros((1, d), F32)

    arrs = [a for a, _ in pieces]
    sd = jax.ShapeDtypeStruct
    return pl.pallas_call(
        body, name=name, grid=(s // tm,),
        out_shape=(sd((s, d), F32), sd((n_cols, d), F32), sd((4, d), F32)),
        in_specs=[_row_spec(tm, d), _full_spec(mod.shape), _full_spec(nw.shape), _row_spec(tm, d),
                  _full_spec(w_in_t.shape)] + [_row_spec(tm, a.shape[1]) for a in arrs],
        out_specs=(_row_spec(tm, d), _full_spec((n_cols, d)), _full_spec((4, d))),
        scratch_shapes=[pltpu.VMEM((8, d), F32)],
        compiler_params=_params(("arbitrary",)),
    )(x, mod, nw, dx_out, w_in_t, *arrs)


def ada_weight_grad(c_all, dmod_cols):
    d = c_all.shape[1]
    w = dmod_cols.shape[2]

    def body(c_ref, dm_ref, out_ref):
        ca = _silu(c_ref[...])
        for l in range(2):
            out_ref[l] = _mm_tn(ca, dm_ref[l])

    return pl.pallas_call(
        body, name="ada_weight_grad",
        out_shape=jax.ShapeDtypeStruct((2, d, w), F32),
        compiler_params=pltpu.CompilerParams(vmem_limit_bytes=VMEM_LIMIT),
    )(c_all, dmod_cols)


def adamw_rows(g_slots, w, m, v, name):
    n, r, lanes = g_slots.shape
    fits = [t for t in range(16, r + 1, 16) if r % t == 0 and t * lanes <= ADAM_TILE]
    tr = max(fits) if fits else r
    c1 = 1.0 - ADAM_B1 ** ADAM_STEP
    c2 = 1.0 - ADAM_B2 ** ADAM_STEP

    def body(g_ref, w_ref, m_ref, v_ref, go, do, mo, vo):
        g = g_ref[0].astype(F32)
        for k in range(1, n):
            g = g + g_ref[k].astype(F32)
        m_new = ADAM_B1 * m_ref[...] + (1.0 - ADAM_B1) * g
        v_new = ADAM_B2 * v_ref[...] + (1.0 - ADAM_B2) * (g * g)
        m_hat = m_new / c1
        v_hat = v_new / c2
        go[...] = g
        do[...] = -ADAM_LR * (m_hat / (jnp.sqrt(v_hat) + ADAM_EPS) + ADAM_WD * w_ref[...])
        mo[...] = m_new
        vo[...] = v_new

    row = pl.BlockSpec((tr, lanes), lambda i: (i, 0))
    sd = jax.ShapeDtypeStruct((r, lanes), F32)
    return pl.pallas_call(
        body, name=name, grid=(r // tr,), out_shape=(sd, sd, sd, sd),
        in_specs=[pl.BlockSpec((n, tr, lanes), lambda i: (0, i, 0)), row, row, row],
        out_specs=(row, row, row, row),
        compiler_params=_params(("parallel",)),
    )(g_slots, w, m, v)


def _rope_tables(s):
    def cs(pos, dim):
        inv = ROPE_THETA ** (-np.arange(0, dim, 2, dtype=np.float32) / dim)
        ang = pos.astype(np.float32)[:, None] * inv.astype(np.float32)[None, :]
        return np.cos(ang), np.sin(ang)

    rows = s // GRID_W
    row = np.repeat(np.arange(rows), GRID_W)
    col = np.tile(np.arange(GRID_W), rows)
    cr, sr = cs(row, HD // 2)
    cc, sc = cs(col, HD // 2)
    ct, st = cs(np.arange(s), B_ROPE)
    tables = (np.concatenate([cr, cr, cc, cc] * 2, axis=-1), np.concatenate([-sr, sr, -sc, sc] * 2, axis=-1),
              np.concatenate([ct, ct] * 4, axis=-1), np.concatenate([-st, st] * 4, axis=-1))
    return tuple(jnp.asarray(t, F32) for t in tables)


def _rows128(a):
    return a.reshape(-1, 128)


def _even_rows_to_kernel(wt):
    return jnp.concatenate([wt[:1664], wt[1696:], wt[1664:1696]], axis=0)


def _even_rows_to_reference(wt):
    return jnp.concatenate([wt[:1664], wt[2176:], wt[1664:2176]], axis=0)


def _uq_rows_to_kernel(wt):
    r = wt.reshape(B_HEADS, B_NOPE + B_ROPE, -1)
    return jnp.concatenate([r[:, :B_NOPE].reshape(B_HEADS * B_NOPE, -1), r[:, B_NOPE:].reshape(B_HEADS * B_ROPE, -1)])


def _uq_rows_to_reference(wt):
    nope = wt[:B_HEADS * B_NOPE].reshape(B_HEADS, B_NOPE, -1)
    rope = wt[B_HEADS * B_NOPE:].reshape(B_HEADS, B_ROPE, -1)
    return jnp.concatenate([nope, rope], axis=1).reshape(B_HEADS * (B_NOPE + B_ROPE), -1)


def _shard_t(w):
    return jnp.transpose(w[0])


def _unshard_t(wt, like):
    return jnp.transpose(wt)[None].reshape(like.shape)


def _pad_rows(flat, rows):
    return jnp.pad(flat, (0, rows * 128 - flat.shape[0])).reshape(rows, 128)


def kernel(x, c, norm_w, ada_w, ada_b, even_w_in, a_q_norm, a_k_norm, b_q_lora_norm, b_kv_lora_norm, b_w_uq, b_w_uk, b_w_uv, even_w_out, odd_w_in, c_sink, odd_w_out, final_norm, loss_target, m_norm_w, m_ada_w, m_ada_b, m_even_w_in, m_a_q_norm, m_a_k_norm, m_b_q_lora_norm, m_b_kv_lora_norm, m_b_w_uq, m_b_w_uk, m_b_w_uv, m_even_w_out, m_odd_w_in, m_c_sink, m_odd_w_out, m_final_norm, v_norm_w, v_ada_w, v_ada_b, v_even_w_in, v_a_q_norm, v_a_k_norm, v_b_q_lora_norm, v_b_kv_lora_norm, v_b_w_uq, v_b_w_uk, v_b_w_uv, v_even_w_out, v_odd_w_in, v_c_sink, v_odd_w_out, v_final_norm):
    s, d = x.shape[1], x.shape[2]
    x0 = x[0]
    target = loss_target[0]
    me_flat = 4 * lax.axis_index("x") + 2 * lax.axis_index("y") + lax.axis_index("c")

    g_in_e, g_uq = all_gather_slots([_shard_t(even_w_in).astype(MXU), _shard_t(b_w_uq).astype(MXU)],
                                    "gather_first_weights")
    wt_in_e = _even_rows_to_kernel(g_in_e.reshape(-1, d))
    wt_uq = _uq_rows_to_kernel(g_uq.reshape(-1, B_Q_LORA))
    later_exchange = Exchange([_shard_t(odd_w_in).astype(MXU), even_w_out[0].astype(MXU),
                               odd_w_out[0].astype(MXU)], scatter=False)
    uk_bd = (jnp.eye(B_HEADS, dtype=F32)[:, None, :, None] * jnp.transpose(b_w_uk[0], (1, 2, 0))[:, :, None, :]
             ).reshape(B_HEADS * B_NOPE, B_HEADS * B_KV_LORA).astype(MXU)
    head_bd = jnp.asarray(np.kron(np.eye(A_HEADS), np.ones((HD, HD))), MXU)
    gq_full, gk_full = jnp.tile(a_q_norm, (1, A_HEADS)), jnp.tile(a_k_norm, (1, A_KV))
    w_uv = jnp.transpose(b_w_uv[0], (1, 0, 2)).astype(MXU)

    wcols = ada_w.shape[2]
    bias_cols = lax.dynamic_slice_in_dim(ada_b.reshape(2, N_DEV, wcols), me_flat, 1, axis=1)
    call, modp = ada_forward(jnp.broadcast_to(c, (8, d)), ada_w, bias_cols)
    c_all = call[:, 0, :]
    mod = jnp.transpose(modp[:, :, 0, :], (1, 0, 2)).reshape(2, 3, d)
    mod_e, mod_o = mod[0], mod[1]
    nw_e, nw_o = norm_w[0:1], norm_w[1:2]

    cos_a, sin_a, cos_t, sin_t = _rope_tables(s)
    slopes = (2.0 ** (-8.0 * jnp.arange(1, C_HEADS + 1, dtype=F32) / C_HEADS)).reshape(C_HEADS, 1, 1)
    sink2 = c_sink.reshape(C_HEADS, 1, 1) * LOG2E

    (qa, ka, va, qb, kb, kat, vat, kbt, qa_raw, ka_raw, cq_raw, ckv_raw, ga, gb) = even_in_forward(
        x0, mod_e, nw_e, wt_in_e, gq_full, gk_full, b_q_lora_norm, b_kv_lora_norm, wt_uq, uk_bd, head_bd,
        cos_a, sin_a, cos_t, sin_t)
    tk_dense = min(512, s)
    tq_dense = min(256, s)
    fwd_sub = min(8, s // tk_dense)
    bwd_sub = min(4, s // tq_dense)
    oa, lse_a, g_in_o, g_out_e, g_out_o = flash_forward(
        qa, ka, vat, scale=HD ** -0.5, dv=HD, tq=tq_dense, tk=tk_dense, nsub=fwd_sub, name="attn_a_fwd",
        exchange=later_exchange)
    wt_in_o = g_in_o.reshape(-1, d)
    w_out_e = g_out_e.reshape(-1, d)
    w_out_o = g_out_o.reshape(-1, d)
    scale_b = (B_NOPE + B_ROPE) ** -0.5
    o_lat, lse_b = flash_forward(qb, kb, kbt, scale=scale_b, dv=B_KV_LORA, tq=min(128, s), tk=tk_dense, nsub=fwd_sub,
                                 name="attn_b_fwd")
    ob = latent_out_forward(o_lat, w_uv)
    x1, y_e = mixer_out_forward(x0, mod_e, [(oa, ga), (ob, gb)], w_out_e, "even_out_fwd")

    qc, kc, vc, kct, vct, gc = odd_in_forward(x1, mod_o, nw_o, wt_in_o)
    win_sub = min(8, s // WINDOW)
    oc, lse_c = window_forward(qc, kc, vct, sink2, slopes, win_sub, "attn_c_fwd")
    x2, y_o = mixer_out_forward(x1, mod_o, [(oc, gc)], w_out_o, "odd_out_fwd")

    loss_lanes, dx2, d_final = loss_head(x2, target, final_norm.reshape(1, d))
    loss_part = (0.5 / d) * jnp.sum(loss_lanes)

    doc, dgc, delta_c, dgate_o, dw_out_o, dsink = mixer_out_backward(
        dx2, y_o, mod_o, [(oc, gc)], w_out_o, [C_HEADS], "odd_out_bwd", lse=lse_c.reshape(C_HEADS, s),
        sink=sink2.reshape(C_HEADS, 1))
    rows3 = lambda t: t.reshape(t.shape[0], 1, s)
    dqc, dkc, dvc = window_backward(qc, kc, kct, vc, doc, lse_c, rows3(delta_c), slopes, win_sub, "attn_c_bwd")
    to_rows = lambda t: jnp.transpose(t, (1, 0, 2)).reshape(s, -1)
    dx1, dwt_in_o, dvec_o = in_proj_backward(
        x1, mod_o, nw_o, dx2, [(dqc, O_Q), (to_rows(dkc), O_K), (to_rows(dvc), O_V), (dgc, O_G)], wt_in_o,
        "odd_in_bwd")

    doa, dga, dob, dgb, delta_a, dgate_e, dw_out_e = mixer_out_backward(
        dx1, y_e, mod_e, [(oa, ga), (ob, gb)], w_out_e, [A_HEADS, 0], "even_out_bwd")
    d_olat, delta_b, dw_uv = latent_out_backward(dob, o_lat, w_uv)
    blocks = lambda g: g.astype(MXU).reshape(N_DEV, g.shape[0] // N_DEV, g.shape[1])
    scatter_odd = Exchange([blocks(dwt_in_o), blocks(dw_out_o)], True)
    scatter_out_e = Exchange([blocks(dw_out_e)], True)
    dqb, dkb, dvb, l_in_o, l_out_o = flash_backward(
        qb, kb, kbt, None, d_olat, lse_b, rows3(delta_b), scale=scale_b, dv=B_KV_LORA,
        tq=tq_dense, tk=tk_dense, nsub=bwd_sub, gq=2, name="attn_b_bwd", split=B_KV_LORA, exchange=scatter_odd)
    dqa, dka, dva, l_out_e = flash_backward(
        qa, ka, kat, va, doa, lse_a, rows3(delta_a), scale=HD ** -0.5, dv=HD,
        tq=tq_dense, tk=tk_dense, nsub=bwd_sub, gq=A_KV, name="attn_a_bwd", exchange=scatter_out_e)
    (pqa, pka, pva, pcq, pckv, pkr, g_qn, g_kn, g_qln, g_kvln, dwt_uq, dw_uk) = even_prep_backward(
        dqa, dka, dva, dqb, dkb, dvb, qa_raw, ka_raw, cq_raw, ckv_raw,
        gq_full, gk_full, b_q_lora_norm, b_kv_lora_norm, wt_uq, uk_bd, head_bd, cos_a, sin_a, cos_t, sin_t)
    g_qn = jnp.sum(g_qn.reshape(A_HEADS, HD), axis=0)
    g_kn = jnp.sum(g_kn.reshape(A_KV, HD), axis=0)
    dx0, dwt_in_e, dvec_e = in_proj_backward(
        x0, mod_e, nw_e, dx1,
        [(pqa, E_QA), (pka, E_KA), (pva, E_VA), (dga, E_GA), (pcq, E_CQ), (pckv, E_CKV), (dgb, E_GB), (pkr, E_KR)],
        wt_in_e, "even_in_bwd")

    dmod = jnp.stack([jnp.concatenate([dvec_e[0], dvec_e[1], dgate_e[0]]),
                      jnp.concatenate([dvec_o[0], dvec_o[1], dgate_o[0]])])
    d_norm_w = jnp.stack([dvec_e[2], dvec_o[2]])
    small_names = ["norm_w", "ada_b", "a_q_norm", "a_k_norm", "b_q_lora_norm", "b_kv_lora_norm", "b_w_uk", "b_w_uv",
                   "c_sink", "final_norm"]
    small_w = [norm_w, ada_b, a_q_norm, a_k_norm, b_q_lora_norm, b_kv_lora_norm, b_w_uk, b_w_uv, c_sink, final_norm]
    small_m = [m_norm_w, m_ada_b, m_a_q_norm, m_a_k_norm, m_b_q_lora_norm, m_b_kv_lora_norm, m_b_w_uk, m_b_w_uv,
               m_c_sink, m_final_norm]
    small_v = [v_norm_w, v_ada_b, v_a_q_norm, v_a_k_norm, v_b_q_lora_norm, v_b_kv_lora_norm, v_b_w_uk, v_b_w_uv,
               v_c_sink, v_final_norm]
    small_g = [d_norm_w, dmod, g_qn, g_kn, g_qln, g_kvln, jnp.transpose(dw_uk, (1, 0, 2)), jnp.transpose(dw_uv, (1, 0, 2)),
               dsink, d_final]
    sizes = [w.size for w in small_w]
    n_small = sum(sizes)
    r_small = -(-(n_small + 1) // (128 * 8)) * 8
    flat_pack = lambda arrs: _pad_rows(jnp.concatenate([a.reshape(-1) for a in arrs]), r_small)
    (g_small_all,) = all_gather_slots([flat_pack(small_g + [loss_part])], "gather_small_grads")
    sm = adamw_rows(g_small_all, flat_pack(small_w), flat_pack(small_m), flat_pack(small_v), "adamw_small")
    loss = sm[0].reshape(-1)[n_small]

    def unpack_small(packed):
        flat = packed.reshape(-1)
        out, o = {}, 0
        for nm, w, sz in zip(small_names, small_w, sizes):
            out[nm] = flat[o:o + sz].reshape(w.shape)
            o += sz
        return out

    sm = [unpack_small(p) for p in sm]

    dmod_all = g_small_all.reshape(N_DEV, -1)[:, sizes[0]:sizes[0] + sizes[1]].reshape(N_DEV, 2, N_DEV, wcols)
    dmod_cols = lax.dynamic_slice_in_dim(dmod_all, me_flat, 1, axis=2)[:, :, 0, :]
    pad16 = lambda a: jnp.concatenate([a, jnp.zeros_like(a)], axis=0)
    g_ada_w = ada_weight_grad(pad16(c_all), jnp.transpose(pad16(dmod_cols), (1, 0, 2)))
    ada = adamw_rows(_rows128(g_ada_w)[None], _rows128(ada_w), _rows128(m_ada_w), _rows128(v_ada_w), "adamw_ada_w")
    ada = [p.reshape(ada_w.shape) for p in ada]

    l_in_e, l_uq = exchange_blocks(
        Exchange([blocks(_even_rows_to_reference(dwt_in_e)), blocks(_uq_rows_to_reference(dwt_uq))], True),
        "scatter_first_weight_grads")
    bg = [{}, {}, {}, {}]
    for nm, landed, w, m, v, transposed in (
            ("even_w_in", l_in_e, even_w_in, m_even_w_in, v_even_w_in, True),
            ("b_w_uq", l_uq, b_w_uq, m_b_w_uq, v_b_w_uq, True),
            ("odd_w_in", l_in_o, odd_w_in, m_odd_w_in, v_odd_w_in, True),
            ("even_w_out", l_out_e, even_w_out, m_even_w_out, v_even_w_out, False),
            ("odd_w_out", l_out_o, odd_w_out, m_odd_w_out, v_odd_w_out, False)):
        view = _shard_t if transposed else (lambda a: a[0])
        res = adamw_rows(landed, view(w), view(m), view(v), "adamw_" + nm)
        for kind, p in enumerate(res):
            bg[kind][nm] = _unshard_t(p, w) if transposed else p[None]
    big_names = ["even_w_in", "odd_w_in", "even_w_out", "odd_w_out", "b_w_uq"]

    order = ["norm_w", "ada_w", "ada_b", "even_w_in", "a_q_norm", "a_k_norm", "b_q_lora_norm", "b_kv_lora_norm",
             "b_w_uq", "b_w_uk", "b_w_uv", "even_w_out", "odd_w_in", "c_sink", "odd_w_out", "final_norm"]

    def pick(kind):
        out = []
        for nm in order:
            if nm == "ada_w":
                out.append(ada[kind])
            elif nm in big_names:
                out.append(bg[kind][nm])
            else:
                out.append(sm[kind][nm])
        return out

    return (loss, dx0[None], *pick(0), *pick(1), *pick(2), *pick(3))
```

```python
import functools

import jax
import jax.numpy as jnp
import numpy as np
from jax import lax
from jax.experimental import pallas as pl
from jax.experimental.pallas import tpu as pltpu

F32 = jnp.float32
MXU = jnp.bfloat16
EPS = 1e-6
ROPE_THETA = 10000.0
GRID_W = 64
HD = 64
N_DEV = 8

A_HEADS, A_KV = 8, 2
B_HEADS, B_NOPE, B_ROPE, B_Q_LORA, B_KV_LORA = 8, 64, 32, 256, 128
B_QK = B_KV_LORA + B_ROPE
C_HEADS, C_KV = 16, 4
WINDOW = 128

ADAM_LR, ADAM_B1, ADAM_B2, ADAM_EPS, ADAM_WD, ADAM_STEP = 0.001, 0.9, 0.999, 1e-08, 0.01, 10

ROW_TILE = 512
ADAM_TILE = 2048 * 128
VMEM_LIMIT = 56 * 1024 * 1024

E_QA, E_KA, E_VA, E_GA, E_CQ, E_CKV, E_GB, E_KR = (
    (0, 512), (512, 640), (640, 768), (768, 1280), (1280, 1536), (1536, 1664), (1664, 2176), (2176, 2208))
EVEN_IN = 2208
O_Q, O_K, O_V, O_G = (0, 1024), (1024, 1280), (1280, 1536), (1536, 2560)
ODD_IN = 2560


def _mm(a, b):
    return jnp.dot(a.astype(MXU), b.astype(MXU), preferred_element_type=F32)


def _mm_nt(a, b):
    return lax.dot_general(a.astype(MXU), b.astype(MXU), (((1,), (1,)), ((), ())), preferred_element_type=F32)


def _mm_tn(a, b):
    return lax.dot_general(a.astype(MXU), b.astype(MXU), (((0,), (0,)), ((), ())), preferred_element_type=F32)


def _group_sums_t(prod, group):
    tm, w = prod.shape
    sel = (lax.broadcasted_iota(jnp.int32, (w, 128), 0) // group
           == lax.broadcasted_iota(jnp.int32, (w, 128), 1)).astype(MXU)
    hi = prod.astype(MXU)
    lo = prod - hi.astype(F32)
    return (_mm(hi, sel) + _mm(lo, sel)).T


def _sigmoid(z):
    return 1.0 / (1.0 + jnp.exp(-z))


def _silu(z):
    return z * _sigmoid(z)


def _rms(x):
    return lax.rsqrt(jnp.mean(x * x, axis=-1, keepdims=True) + EPS)


def _swap_halves(y, group):
    n = y.shape[-1]
    half = group // 2
    fwd = pltpu.roll(y, half, 1)
    if n == group:
        return fwd
    back = pltpu.roll(y, n - half, 1)
    lane = lax.broadcasted_iota(jnp.int32, y.shape, 1)
    return jnp.where((lane % group) < half, back, fwd)


def _rope(y, cos, sin, group):
    return y * cos + _swap_halves(y, group) * sin


def _rope_t(d, cos, sin, group):
    return d * cos - _swap_halves(d, group) * sin


def _rms_bwd(dy, x, g):
    r = _rms(x)
    xhat = x * r
    dxhat = dy * g
    dx = r * (dxhat - xhat * jnp.mean(dxhat * xhat, axis=-1, keepdims=True))
    return dx, dy * xhat


def _group_mean(v, bd, group):
    hi = v.astype(MXU)
    lo = v - hi.astype(F32)
    return (_mm(hi, bd[...]) + _mm(lo, bd[...])) * (1.0 / group)


def _head_norm(x, g, bd, group):
    return x * lax.rsqrt(_group_mean(x * x, bd, group) + EPS) * g


def _head_norm_bwd(dy, x, g, bd, group):
    r = lax.rsqrt(_group_mean(x * x, bd, group) + EPS)
    xhat = x * r
    dxhat = dy * g
    dx = r * (dxhat - xhat * _group_mean(dxhat * xhat, bd, group))
    return dx, dy * xhat


def _params(sem, vmem=VMEM_LIMIT):
    return pltpu.CompilerParams(dimension_semantics=sem, vmem_limit_bytes=vmem)


def _row_spec(tm, w):
    return pl.BlockSpec((tm, w), lambda i: (i, 0))


def _full_spec(shape):
    nd = len(shape)
    return pl.BlockSpec(shape, lambda i: (0,) * nd)


def _head_spec(h, tm, w):
    return pl.BlockSpec((h, tm, w), lambda i: (0, i, 0))


def _headt_spec(h, w, tm):
    return pl.BlockSpec((h, w, tm), lambda i: (0, 0, i))


def _rows_spec(h, tm):
    return pl.BlockSpec((h, tm), lambda i: (0, i))


def _me():
    return lax.axis_index("x"), lax.axis_index("y"), lax.axis_index("c")


def _flat(p):
    return 4 * p[0] + 2 * p[1] + p[2]


def _peer(me, k):
    x, y, c = me
    return (1 - x if k & 4 else x, 1 - y if k & 2 else y, 1 - c if k & 1 else c)


MESH_ID = pl.DeviceIdType.MESH


class Gather:
    VMEM = pl.BlockSpec(memory_space=pltpu.VMEM)

    def __init__(self, shards):
        self.shards = list(shards)
        self.n = len(self.shards)
        self.out_shapes = tuple(jax.ShapeDtypeStruct((N_DEV,) + a.shape, a.dtype) for a in self.shards)
        self.in_specs = [Gather.VMEM] * self.n
        self.out_specs = (Gather.VMEM,) * self.n
        self.sems = [pltpu.SemaphoreType.DMA((7 * self.n,)), pltpu.SemaphoreType.DMA((7 * self.n,)),
                     pltpu.SemaphoreType.DMA((self.n,))]

    def _plan(self, x_refs, out_refs, sems):
        send_sems, recv_sems, local_sems = sems
        me = _me()
        x, y, c = me
        chips = [(1 - x, y), (x, 1 - y), (1 - x, 1 - y)]

        def copy(a, k, block, to, src=None):
            slot = out_refs[a].at[_flat(block)]
            return pltpu.make_async_remote_copy(
                src_ref=slot if src is None else src, dst_ref=slot, send_sem=send_sems.at[7 * a + k],
                recv_sem=recv_sems.at[7 * a + k], device_id=to, device_id_type=MESH_ID)

        mine = [pltpu.make_async_copy(x_refs[a], out_refs[a].at[_flat(me)], local_sems.at[a]) for a in range(self.n)]
        first = [copy(a, 0, me, (x, y, 1 - c), src=x_refs[a]) for a in range(self.n)]
        first += [copy(a, 1 + j, me, (*chip, c), src=x_refs[a]) for a in range(self.n) for j, chip in enumerate(chips)]
        return me, chips, copy, mine, first

    def start(self, x_refs, out_refs, sems):
        _, _, _, mine, first = self._plan(x_refs, out_refs, sems)
        for cp in mine + first:
            cp.start()

    def finish(self, x_refs, out_refs, sems):
        me, chips, copy, mine, first = self._plan(x_refs, out_refs, sems)
        x, y, c = me
        sibling = (x, y, 1 - c)
        passed = []
        for a in range(self.n):
            for j, chip in enumerate(chips):
                copy(a, 1 + j, (*chip, c), me).wait_recv()
                passed.append(copy(a, 4 + j, (*chip, c), sibling))
                passed[-1].start()
        for a in range(self.n):
            copy(a, 0, sibling, me).wait_recv()
            for j, chip in enumerate(chips):
                copy(a, 4 + j, (*chip, 1 - c), me).wait_recv()
        for cp in first + passed:
            cp.wait_send()
        for cp in mine:
            cp.wait()


def scatter_and_gather(ex, gather, name):
    def body(*refs):
        it = iter(refs)
        src_refs = [next(it) for _ in range(ex.n)]
        x_refs = [next(it) for _ in range(gather.n)]
        land_refs = [next(it) for _ in range(ex.n)]
        out_refs = [next(it) for _ in range(gather.n)]
        ex_sems = [next(it) for _ in range(len(ex.sems))]
        g_sems = list(it)
        ex.start(src_refs, land_refs, ex_sems)
        gather.start(x_refs, out_refs, g_sems)
        gather.finish(x_refs, out_refs, g_sems)
        ex.wait(src_refs, land_refs, ex_sems)

    res = pl.pallas_call(
        body, name=name, out_shape=ex.land_shapes + gather.out_shapes,
        in_specs=ex.in_specs + gather.in_specs, out_specs=ex.out_specs + gather.out_specs,
        scratch_shapes=list(ex.sems) + list(gather.sems),
        compiler_params=pltpu.CompilerParams(vmem_limit_bytes=VMEM_LIMIT),
    )(*ex.srcs, *gather.shards)
    return res[:ex.n], res[ex.n:]


class Exchange:
    HBM = pl.BlockSpec(memory_space=pl.ANY)

    def __init__(self, srcs, scatter):
        self.srcs = list(srcs)
        self.scatter = scatter
        self.n = len(self.srcs)
        self.land_shapes = tuple(jax.ShapeDtypeStruct((N_DEV,) + tuple(a.shape[-2:]), a.dtype) for a in self.srcs)
        self.in_specs = [Exchange.HBM] * self.n
        self.out_specs = (Exchange.HBM,) * self.n
        self.sems = [pltpu.SemaphoreType.DMA((N_DEV - 1,)), pltpu.SemaphoreType.DMA((N_DEV - 1,)),
                     pltpu.SemaphoreType.DMA] * self.n

    def _copies(self, src_refs, land_refs, sems):
        me = _me()
        mi = _flat(me)
        local, sends, recvs = [], [], []
        for a, (src_ref, land_ref) in enumerate(zip(src_refs, land_refs)):
            send_sems, recv_sems, local_sem = sems[3 * a:3 * a + 3]
            pick = (lambda p, r=src_ref: r.at[_flat(p)]) if self.scatter else (lambda p, r=src_ref: r)
            local.append(pltpu.make_async_copy(pick(me), land_ref.at[mi], local_sem))
            for k in range(1, N_DEV):
                peer = _peer(me, k)
                pair = dict(send_sem=send_sems.at[k - 1], recv_sem=recv_sems.at[k - 1], device_id=peer,
                            device_id_type=MESH_ID)
                sends.append(pltpu.make_async_remote_copy(src_ref=pick(peer), dst_ref=land_ref.at[mi], **pair))
                recvs.append(pltpu.make_async_remote_copy(src_ref=pick(peer), dst_ref=land_ref.at[_flat(peer)],
                                                          **pair))
        return local, sends, recvs

    def start(self, src_refs, land_refs, sems):
        local, sends, _ = self._copies(src_refs, land_refs, sems)
        for cp in local + sends:
            cp.start()

    def wait(self, src_refs, land_refs, sems):
        local, sends, recvs = self._copies(src_refs, land_refs, sems)
        for cp in recvs:
            cp.wait_recv()
        for cp in sends:
            cp.wait_send()
        for cp in local:
            cp.wait()


def ada_forward(c8, ada_w, bias_cols, gather):
    d = c8.shape[1]
    w = ada_w.shape[2]
    ng = gather.n

    def body(*refs):
        c_ref, w_ref, b_ref = refs[:3]
        gx_refs = refs[3:3 + ng]
        call_ref, modp_ref = refs[3 + ng:5 + ng]
        gout_refs = refs[5 + ng:5 + 2 * ng]
        part_ref, s1, r1, s2, r2 = refs[5 + 2 * ng:10 + 2 * ng]
        g_sems = refs[10 + 2 * ng:]
        gather.start(gx_refs, gout_refs, g_sems)
        me = _me()
        mi = _flat(me)
        call_ref[mi] = c_ref[...]
        rows_out = []
        for k in range(1, N_DEV):
            rows_out.append(pltpu.make_async_remote_copy(
                src_ref=c_ref, dst_ref=call_ref.at[mi], send_sem=s1.at[k - 1], recv_sem=r1.at[k - 1],
                device_id=_peer(me, k), device_id_type=MESH_ID))
        for cp in rows_out:
            cp.start()
        for k in range(1, N_DEV):
            pltpu.make_async_remote_copy(
                src_ref=c_ref, dst_ref=call_ref.at[_flat(_peer(me, k))], send_sem=s1.at[k - 1],
                recv_sem=r1.at[k - 1], device_id=_peer(me, k), device_id_type=MESH_ID).wait_recv()
        ca = _silu(call_ref[...].reshape(N_DEV * 8, d))
        for l in range(2):
            part = _mm(ca, w_ref[l]) + b_ref[l]
            for b in range(N_DEV):
                part_ref[b, l] = part[8 * b:8 * b + 8, :]
        modp_ref[mi] = part_ref[mi]
        spread = []
        for k in range(1, N_DEV):
            peer = _peer(me, k)
            spread.append(pltpu.make_async_remote_copy(
                src_ref=part_ref.at[_flat(peer)], dst_ref=modp_ref.at[mi], send_sem=s2.at[k - 1],
                recv_sem=r2.at[k - 1], device_id=peer, device_id_type=MESH_ID))
        for cp in spread:
            cp.start()
        for k in range(1, N_DEV):
            pi = _flat(_peer(me, k))
            pltpu.make_async_remote_copy(
                src_ref=part_ref.at[pi], dst_ref=modp_ref.at[pi], send_sem=s2.at[k - 1],
                recv_sem=r2.at[k - 1], device_id=_peer(me, k), device_id_type=MESH_ID).wait_recv()
        for cp in rows_out + spread:
            cp.wait_send()
        gather.finish(gx_refs, gout_refs, g_sems)

    vm = pl.BlockSpec(memory_space=pltpu.VMEM)
    res = pl.pallas_call(
        body, name="ada_forward",
        out_shape=(jax.ShapeDtypeStruct((N_DEV, 8, d), F32), jax.ShapeDtypeStruct((N_DEV, 2, 8, w), F32))
        + gather.out_shapes,
        in_specs=[vm, vm, vm] + gather.in_specs, out_specs=(vm, vm) + gather.out_specs,
        scratch_shapes=[pltpu.VMEM((N_DEV, 2, 8, w), F32)] + [pltpu.SemaphoreType.DMA((7,))] * 4 + list(gather.sems),
        compiler_params=pltpu.CompilerParams(vmem_limit_bytes=VMEM_LIMIT),
    )(c8, ada_w, bias_cols, *gather.shards)
    return res[0], res[1], res[2:]


def _modulated(x, mod_ref, nw_ref):
    xn = x * _rms(x)
    g1 = nw_ref[...] * (1.0 + mod_ref[1:2, :])
    return xn, g1, xn * g1 + mod_ref[0:1, :]


def even_in_forward(x, mod, nw, w_in_t, gq, gk, qln, kvln, w_uq_t, uk_bd, bd, cos_a, sin_a, cos_t, sin_t):
    s, d = x.shape
    tm = min(ROW_TILE, s)
    n_nope = B_HEADS * B_NOPE

    def body(x_ref, mod_ref, nw_ref, w_ref, gq_ref, gk_ref, qln_ref, kvln_ref, uq_ref, ukbd_ref, bd_ref,
             ca_ref, sa_ref, ct_ref, st_ref,
             qa_o, ka_o, va_o, qb_o, kb_o, kat_o, vat_o, kbt_o, qa_raw_o, ka_raw_o, cq_raw_o, ckv_raw_o, ga_o, gb_o):
        _, _, h = _modulated(x_ref[...], mod_ref, nw_ref)
        h = h.astype(MXU)

        def proj(cols):
            return _mm_nt(h, w_ref[cols[0]:cols[1], :])

        ca, sa, ct, st = ca_ref[...], sa_ref[...], ct_ref[...], st_ref[...]
        wide = lambda t, n: jnp.concatenate([t] * n, axis=1)
        qa = proj(E_QA)
        qa_raw_o[...] = qa
        qr = _rope(_head_norm(qa, gq_ref[...], bd_ref, HD), wide(ca, 4), wide(sa, 4), 32)
        for hh in range(A_HEADS):
            qa_o[hh] = qr[:, HD * hh:HD * hh + HD].astype(MXU)
        ka = proj(E_KA)
        ka_raw_o[...] = ka
        kr = _rope(_head_norm(ka, gk_ref[...], bd_ref[0:128, 0:128], HD), ca, sa, 32)
        va = proj(E_VA)
        krt, vat = kr.T, va.T
        for g in range(A_KV):
            ka_o[g] = kr[:, HD * g:HD * g + HD].astype(MXU)
            va_o[g] = va[:, HD * g:HD * g + HD].astype(MXU)
            kat_o[g] = krt[HD * g:HD * g + HD, :].astype(MXU)
            vat_o[g] = vat[HD * g:HD * g + HD, :].astype(MXU)
        ga_o[...] = proj(E_GA)
        gb_o[...] = proj(E_GB)
        cq = proj(E_CQ)
        cq_raw_o[...] = cq
        qb = _mm_nt(cq * _rms(cq) * qln_ref[...], uq_ref[...])
        q_lat = _mm(qb[:, 0:n_nope], ukbd_ref[...])
        q_rope = _rope(qb[:, n_nope:], wide(ct, 2), wide(st, 2), 32)
        for hh in range(B_HEADS):
            qb_o[hh, :, 0:B_KV_LORA] = q_lat[:, B_KV_LORA * hh:B_KV_LORA * (hh + 1)].astype(MXU)
            qb_o[hh, :, B_KV_LORA:B_QK] = q_rope[:, B_ROPE * hh:B_ROPE * (hh + 1)].astype(MXU)
        ckv = proj(E_CKV)
        ckv_raw_o[...] = ckv
        ckv_n = ckv * _rms(ckv) * kvln_ref[...]
        k_rope = _rope(proj(E_KR), ct[:, 0:B_ROPE], st[:, 0:B_ROPE], 32)
        kb_o[0, :, 0:B_KV_LORA] = ckv_n.astype(MXU)
        kb_o[0, :, B_KV_LORA:B_QK] = k_rope.astype(MXU)
        kbt_o[0, 0:B_KV_LORA, :] = ckv_n.T.astype(MXU)
        kbt_o[0, B_KV_LORA:B_QK, :] = k_rope.T.astype(MXU)

    sd = jax.ShapeDtypeStruct
    outs = (sd((A_HEADS, s, HD), MXU), sd((A_KV, s, HD), MXU), sd((A_KV, s, HD), MXU),
            sd((B_HEADS, s, B_QK), MXU), sd((1, s, B_QK), MXU),
            sd((A_KV, HD, s), MXU), sd((A_KV, HD, s), MXU), sd((1, B_QK, s), MXU),
            sd((s, 512), F32), sd((s, 128), F32), sd((s, B_Q_LORA), F32), sd((s, B_KV_LORA), F32),
            sd((s, 512), F32), sd((s, 512), F32))
    out_specs = (_head_spec(A_HEADS, tm, HD), _head_spec(A_KV, tm, HD), _head_spec(A_KV, tm, HD),
                 _head_spec(B_HEADS, tm, B_QK), _head_spec(1, tm, B_QK),
                 _headt_spec(A_KV, HD, tm), _headt_spec(A_KV, HD, tm), _headt_spec(1, B_QK, tm),
                 _row_spec(tm, 512), _row_spec(tm, 128), _row_spec(tm, B_Q_LORA), _row_spec(tm, B_KV_LORA),
                 _row_spec(tm, 512), _row_spec(tm, 512))
    consts = [mod, nw, w_in_t, gq, gk, qln, kvln, w_uq_t, uk_bd, bd]
    return pl.pallas_call(
        body, name="even_in_forward", grid=(s // tm,), out_shape=outs,
        in_specs=[_row_spec(tm, d)] + [_full_spec(a.shape) for a in consts] + [_row_spec(tm, 128)] * 4,
        out_specs=out_specs, compiler_params=_params(("parallel",)),
    )(x, *consts, cos_a, sin_a, cos_t, sin_t)


def odd_in_forward(x, mod, nw, w_in):
    s, d = x.shape
    tm = min(ROW_TILE, s)

    def body(x_ref, mod_ref, nw_ref, w_ref, q_o, k_o, v_o, kt_o, vt_o, g_o):
        _, _, h = _modulated(x_ref[...], mod_ref, nw_ref)
        h = h.astype(MXU)

        def proj(cols):
            return _mm_nt(h, w_ref[cols[0]:cols[1], :])

        q = proj(O_Q)
        for hh in range(C_HEADS):
            q_o[hh] = q[:, HD * hh:HD * hh + HD].astype(MXU)
        k = proj(O_K)
        v = proj(O_V)
        for g in range(C_KV):
            kh = k[:, HD * g:HD * g + HD]
            vh = v[:, HD * g:HD * g + HD]
            k_o[g] = kh.astype(MXU)
            v_o[g] = vh.astype(MXU)
            kt_o[g] = kh.T.astype(MXU)
            vt_o[g] = vh.T.astype(MXU)
        g_o[...] = proj(O_G)

    sd = jax.ShapeDtypeStruct
    return pl.pallas_call(
        body, name="odd_in_forward", grid=(s // tm,),
        out_shape=(sd((C_HEADS, s, HD), MXU), sd((C_KV, s, HD), MXU), sd((C_KV, s, HD), MXU),
                   sd((C_KV, HD, s), MXU), sd((C_KV, HD, s), MXU), sd((s, 1024), F32)),
        in_specs=[_row_spec(tm, d), _full_spec(mod.shape), _full_spec(nw.shape), _full_spec(w_in.shape)],
        out_specs=(_head_spec(C_HEADS, tm, HD), _head_spec(C_KV, tm, HD), _head_spec(C_KV, tm, HD),
                   _headt_spec(C_KV, HD, tm), _headt_spec(C_KV, HD, tm), _row_spec(tm, 1024)),
        compiler_params=_params(("parallel",)),
    )(x, mod, nw, w_in)


def latent_out_forward(o_lat, w_uv):
    s = o_lat.shape[0]
    tm = min(ROW_TILE, s)

    def body(o_ref, uv_ref, out_ref):
        for hh in range(B_HEADS):
            out_ref[:, HD * hh:HD * hh + HD] = _mm(o_ref[:, B_KV_LORA * hh:B_KV_LORA * (hh + 1)], uv_ref[hh])

    return pl.pallas_call(
        body, name="latent_out_forward", grid=(s // tm,),
        out_shape=jax.ShapeDtypeStruct((s, B_HEADS * HD), F32),
        in_specs=[_row_spec(tm, o_lat.shape[1]), _full_spec(w_uv.shape)],
        out_specs=_row_spec(tm, B_HEADS * HD),
        compiler_params=_params(("parallel",)),
    )(o_lat, w_uv)


def mixer_out_forward(x, mod, pairs, w_out, name):
    s, d = x.shape
    tm = min(ROW_TILE, s)
    n = len(pairs)
    widths = [o.shape[1] for o, _ in pairs]

    def body(*refs):
        x_ref, mod_ref, w_ref = refs[:3]
        pr = refs[3:3 + 2 * n]
        xo_ref, y_ref = refs[3 + 2 * n:]
        y = jnp.zeros((tm, d), F32)
        r0 = 0
        for i in range(n):
            mix = pr[2 * i][...] * _silu(pr[2 * i + 1][...])
            y = y + _mm(mix, w_ref[r0:r0 + widths[i], :])
            r0 += widths[i]
        y_ref[...] = y
        xo_ref[...] = x_ref[...] + mod_ref[2:3, :] * y

    flat = [a for p in pairs for a in p]
    sd = jax.ShapeDtypeStruct
    return pl.pallas_call(
        body, name=name, grid=(s // tm,),
        out_shape=(sd((s, d), F32), sd((s, d), F32)),
        in_specs=[_row_spec(tm, d), _full_spec(mod.shape), _full_spec(w_out.shape)]
        + [_row_spec(tm, a.shape[1]) for a in flat],
        out_specs=(_row_spec(tm, d), _row_spec(tm, d)),
        compiler_params=_params(("parallel",)),
    )(x, mod, w_out, *flat)


LOG2E = 1.4426950408889634
ONES_ROWS = 16


def _col_max8(s3):
    m8 = jnp.max(s3, axis=0)
    return jnp.broadcast_to(jnp.max(m8, axis=0, keepdims=True), m8.shape)


def _with_ones(vt, n):
    return jnp.concatenate([vt, jnp.ones((ONES_ROWS, n), vt.dtype)], axis=0)


def _grid_edges(grid):
    ids = [pl.program_id(a) for a in range(len(grid))]
    first = functools.reduce(jnp.logical_and, [i == 0 for i in ids])
    last = functools.reduce(jnp.logical_and, [i == n - 1 for i, n in zip(ids, grid)])
    return first, last


def flash_forward(q, k, vt, *, scale, dv, tq, tk, nsub, name, exchange=None):
    hq, s, dq = q.shape
    g_kv = k.shape[0]
    hpg = hq // g_kv
    nq = s // tq
    tkk = tk * nsub
    nk = s // tkk
    grid = (g_kv, nq, nk)
    hosted = exchange is not None
    m_cols = hpg * tq
    c = scale * LOG2E
    dvp = dv + ONES_ROWS

    def body(*refs):
        nx =exchange.n if hosted else 0
        q_ref, k_ref, vt_ref = refs[:3]
        xs_refs = refs[3:3 + nx]
        o_ref, lse_ref = refs[3 + nx:5 + nx]
        land_refs = refs[5 + nx:5 + 2 * nx]
        m_s, acc_s = refs[5 + 2 * nx:7 + 2 * nx]
        sems = refs[7 + 2 * nx:]
        if hosted:
            first, last = _grid_edges(grid)
            pl.when(first)(lambda: exchange.start(xs_refs, land_refs, sems))
        j = pl.program_id(2)

        @pl.when(j == 0)
        def _():
            m_s[...] = jnp.full((8, m_cols), -jnp.inf, F32)
            acc_s[...] = jnp.zeros((dvp, m_cols), F32)

        qq = q_ref[...].reshape(m_cols, dq)
        sts = [_mm_nt(k_ref[0, tk * u:tk * (u + 1), :], qq).reshape(tk // 8, 8, m_cols)
               for u in range(nsub)]
        m_run = m_s[...]
        acc = acc_s[...]
        for u in range(nsub):
            m_new = jnp.maximum(m_run, _col_max8(sts[u]) * c)
            p = jnp.exp2(sts[u] * c - m_new[None])
            alpha = jnp.exp2(m_run - m_new)
            pv = _mm(_with_ones(vt_ref[0, 0:dv, tk * u:tk * (u + 1)], tk), p.reshape(tk, m_cols))
            acc = (acc.reshape(dvp // 8, 8, m_cols) * alpha[None]).reshape(dvp, m_cols) + pv
            m_run = m_new
        acc_s[...] = acc
        m_s[...] = m_run

        @pl.when(j == nk - 1)
        def _():
            l = acc_s[dv:dv + 1, :]
            ot = acc_s[0:dv, :] / l
            lse = m_s[0:1, :] + jnp.log2(l)
            for hh in range(hpg):
                o_ref[:, dv * hh:dv * hh + dv] = ot[:, tq * hh:tq * hh + tq].T
                lse_ref[hh] = lse[:, tq * hh:tq * hh + tq]

        if hosted:
            pl.when(last)(lambda: exchange.wait(xs_refs, land_refs, sems))

    sd = jax.ShapeDtypeStruct
    return pl.pallas_call(
        body, name=name, grid=grid,
        out_shape=(sd((s, hq * dv), F32), sd((hq, 1, s), F32)) + (exchange.land_shapes if hosted else ()),
        in_specs=[pl.BlockSpec((hpg, tq, dq), lambda g, i, j: (g, i, 0)),
                  pl.BlockSpec((1, tkk, k.shape[2]), lambda g, i, j: (g, j, 0)),
                  pl.BlockSpec((1, dv, tkk), lambda g, i, j: (g, 0, j))] + (exchange.in_specs if hosted else []),
        out_specs=(pl.BlockSpec((tq, hpg * dv), lambda g, i, j: (i, g)),
                   pl.BlockSpec((hpg, 1, tq), lambda g, i, j: (g, 0, i))) + (exchange.out_specs if hosted else ()),
        scratch_shapes=[pltpu.VMEM((8, m_cols), F32), pltpu.VMEM((dvp, m_cols), F32)]
        + (list(exchange.sems) if hosted else []),
        compiler_params=_params(("arbitrary",) * 3 if hosted else ("parallel", "parallel", "arbitrary")),
    )(q, k, vt, *(exchange.srcs if hosted else []))


def _window_bias_t(hpg, slope_ref):
    t = WINDOW
    r = lax.broadcasted_iota(jnp.int32, (3 * t, t), 0)
    cq = lax.broadcasted_iota(jnp.int32, (3 * t, t), 1)
    arel = jnp.abs(r - t - cq)
    base = jnp.where(arel <= WINDOW, arel.astype(F32) * (-LOG2E), -jnp.inf)
    return jnp.concatenate([base * slope_ref[hh] for hh in range(hpg)], axis=1)


def _window_edges_t(bias, no_before, no_after):
    t = WINDOW
    r = lax.broadcasted_iota(jnp.int32, bias.shape, 0)
    out = ((r < t) & no_before) | ((r >= 2 * t) & no_after)
    return jnp.where(out, -jnp.inf, bias)


def _window_specs(kind, nb, nblk, d):
    t = WINDOW
    before = lambda i: jnp.clip(i * nb - 1, 0, nblk - 1)
    after = lambda i: jnp.clip((i + 1) * nb, 0, nblk - 1)
    if kind == "rows":
        return [pl.BlockSpec((1, t, d), lambda g, i: (g, before(i), 0)),
                pl.BlockSpec((1, nb * t, d), lambda g, i: (g, i, 0)),
                pl.BlockSpec((1, t, d), lambda g, i: (g, after(i), 0))]
    return [pl.BlockSpec((1, d, t), lambda g, i: (g, 0, before(i))),
            pl.BlockSpec((1, d, nb * t), lambda g, i: (g, 0, i)),
            pl.BlockSpec((1, d, t), lambda g, i: (g, 0, after(i)))]


def window_forward(q, k, vt, sink2, slopes, nb, name):
    hq, s, d = q.shape
    g_kv = k.shape[0]
    hpg = hq // g_kv
    t = WINDOW
    nblk = s // t
    steps = nblk // nb
    m_cols = hpg * t
    c = (d ** -0.5) * LOG2E

    def body(q_ref, kp, ko, kn, vp, vo, vn, sink_ref, slope_ref, o_ref, lse_ref):
        i = pl.program_id(1)
        kk_all = jnp.concatenate([kp[0], ko[0], kn[0]], axis=0)
        vt_all = jnp.concatenate([vp[0], vo[0], vn[0]], axis=1)
        bias = _window_bias_t(hpg, slope_ref)
        sink_row = jnp.concatenate([jnp.broadcast_to(sink_ref[hh], (8, t)) for hh in range(hpg)], axis=1)
        sts = []
        for u in range(nb):
            qq = q_ref[:, t * u:t * (u + 1), :].reshape(m_cols, d)
            b_u = bias
            if u == 0 or u == nb - 1:
                b_u = _window_edges_t(bias, (i == 0) if u == 0 else False,
                                      (i == steps - 1) if u == nb - 1 else False)
            sts.append(_mm_nt(kk_all[t * u:t * (u + 3), :], qq) * c + b_u)
        for u in range(nb):
            s3 = sts[u].reshape(3 * t // 8, 8, m_cols)
            m8 = jnp.maximum(_col_max8(s3), sink_row)
            p = jnp.exp2(s3 - m8[None]).reshape(3 * t, m_cols)
            acc = _mm(_with_ones(vt_all[:, t * u:t * (u + 3)], 3 * t), p)
            l = acc[d:d + 1, :] + jnp.exp2(sink_row[0:1, :] - m8[0:1, :])
            ot = acc[0:d, :] / l
            lse = m8[0:1, :] + jnp.log2(l)
            for hh in range(hpg):
                o_ref[t * u:t * (u + 1), d * hh:d * hh + d] = ot[:, t * hh:t * hh + t].T
                lse_ref[hh, :, t * u:t * (u + 1)] = lse[:, t * hh:t * hh + t]

    sd = jax.ShapeDtypeStruct
    return pl.pallas_call(
        body, name=name, grid=(g_kv, steps),
        out_shape=(sd((s, hq * d), F32), sd((hq, 1, s), F32)),
        in_specs=[pl.BlockSpec((hpg, nb * t, d), lambda g, i: (g, i, 0))]
        + _window_specs("rows", nb, nblk, d) + _window_specs("cols", nb, nblk, d)
        + [pl.BlockSpec((hpg, 1, 1), lambda g, i: (g, 0, 0))] * 2,
        out_specs=(pl.BlockSpec((nb * t, hpg * d), lambda g, i: (i, g)),
                   pl.BlockSpec((hpg, 1, nb * t), lambda g, i: (g, 0, i))),
        compiler_params=_params(("parallel", "parallel")),
    )(q, k, k, k, vt, vt, vt, sink2, slopes)


def window_backward(q, k, kt, v, do, lse, delta, slopes, nb, name):
    hq, s, d = q.shape
    g_kv = k.shape[0]
    hpg = hq // g_kv
    t = WINDOW
    nblk = s // t
    steps = nblk // nb
    m_cols = hpg * t
    scale = d ** -0.5
    c = scale * LOG2E

    def body(q_ref, kp, ko, kn, ktp, kto, ktn, vp, vo, vn, do_ref, lse_ref, dl_ref, slope_ref,
             dq_ref, dk_ref, dv_ref, dk_s, dv_s):
        i = pl.program_id(1)

        @pl.when(i == 0)
        def _():
            dk_ref[...] = jnp.zeros(dk_ref.shape, F32)
            dv_ref[...] = jnp.zeros(dv_ref.shape, F32)

        dk_s[...] = jnp.zeros(dk_s.shape, F32)
        dv_s[...] = jnp.zeros(dv_s.shape, F32)
        kk_all = jnp.concatenate([kp[0], ko[0], kn[0]], axis=0)
        vv_all = jnp.concatenate([vp[0], vo[0], vn[0]], axis=0)
        kkt_all = jnp.concatenate([ktp[0], kto[0], ktn[0]], axis=1)
        bias = _window_bias_t(hpg, slope_ref)
        qqs, dds, sts, dps = [], [], [], []
        for u in range(nb):
            rows = slice(t * u, t * (u + 1))
            keys = slice(t * u, t * (u + 3))
            qqs.append(q_ref[:, rows, :].reshape(m_cols, d))
            dds.append(jnp.concatenate([do_ref[rows, d * hh:d * hh + d] for hh in range(hpg)], axis=0))
            b_u = bias
            if u == 0 or u == nb - 1:
                b_u = _window_edges_t(bias, (i == 0) if u == 0 else False,
                                      (i == steps - 1) if u == nb - 1 else False)
            sts.append(_mm_nt(kk_all[keys, :], qqs[u]) * c + b_u)
            dps.append(_mm_nt(vv_all[keys, :], dds[u]))
        for u in range(nb):
            rows = slice(t * u, t * (u + 1))
            keys = slice(t * u, t * (u + 3))
            lse_row = jnp.concatenate([lse_ref[hh, :, rows] for hh in range(hpg)], axis=1)
            dl_row = jnp.concatenate([dl_ref[hh, :, rows] for hh in range(hpg)], axis=1)
            p = jnp.exp2(sts[u] - lse_row)
            ds = p * (dps[u] - dl_row) * scale
            dv_s[keys, :] += _mm(p, dds[u])
            dk_s[keys, :] += _mm(ds, qqs[u])
            dqt = _mm(kkt_all[:, keys], ds)
            for hh in range(hpg):
                dq_ref[rows, d * hh:d * hh + d] = dqt[:, t * hh:t * hh + t].T
        tq = nb * t
        for src, r0, n in ((0, jnp.clip(i * nb - 1, 0, nblk - 1) * t, t), (t, i * tq, tq),
                           (t + tq, jnp.clip((i + 1) * nb, 0, nblk - 1) * t, t)):
            dst = pl.ds(pl.multiple_of(r0, t), n)
            dk_ref[0, dst, :] += dk_s[src:src + n, :]
            dv_ref[0, dst, :] += dv_s[src:src + n, :]

    row_map = lambda g, i: (g, 0, i)
    sd = jax.ShapeDtypeStruct
    return pl.pallas_call(
        body, name=name, grid=(g_kv, steps),
        out_shape=(sd((s, hq * d), F32), sd((g_kv, s, d), F32), sd((g_kv, s, d), F32)),
        in_specs=[pl.BlockSpec((hpg, nb * t, d), lambda g, i: (g, i, 0))]
        + _window_specs("rows", nb, nblk, d) + _window_specs("cols", nb, nblk, d) + _window_specs("rows", nb, nblk, d)
        + [pl.BlockSpec((nb * t, hpg * d), lambda g, i: (i, g)), pl.BlockSpec((hpg, 1, nb * t), row_map),
           pl.BlockSpec((hpg, 1, nb * t), row_map), pl.BlockSpec((hpg, 1, 1), lambda g, i: (g, 0, 0))],
        out_specs=(pl.BlockSpec((nb * t, hpg * d), lambda g, i: (i, g)),
                   pl.BlockSpec((1, s, d), lambda g, i: (g, 0, 0)),
                   pl.BlockSpec((1, s, d), lambda g, i: (g, 0, 0))),
        scratch_shapes=[pltpu.VMEM(((nb + 2) * t, d), F32), pltpu.VMEM(((nb + 2) * t, d), F32)],
        compiler_params=_params(("parallel", "arbitrary")),
    )(q, k, k, k, kt, kt, kt, v, v, v, do, lse, delta, slopes)


def flash_backward(q, k, kt, v, do, lse, delta, *, scale, dv, tq, tk, nsub, gq, name, split=None, exchange=None):
    hq, s, dq = q.shape
    g_kv = k.shape[0]
    hpg = hq // gq
    nq = s // tq
    tqq = tq * nsub
    nqs = s // tqq
    nkb = s // tk
    grid = (gq, nkb, nqs)
    hosted = exchange is not None
    m_cols = hpg * tq
    c = scale * LOG2E
    has_v = v is not None

    def body(*refs):
        it = iter(refs)
        q_ref, k_ref, kt_ref = next(it), next(it), next(it)
        v_ref = next(it) if has_v else None
        do_ref, lse_ref, dl_ref = next(it), next(it), next(it)
        nx = exchange.n if hosted else 0
        xs_refs = [next(it) for _ in range(nx)]
        dq_ref, dk_ref, dv_ref = next(it), next(it), next(it)
        land_refs = [next(it) for _ in range(nx)]
        dqt_s = next(it)
        sems = list(it)
        kj = pl.program_id(1)
        qi = pl.program_id(2)
        if hosted:
            first, last = _grid_edges(grid)
            pl.when(first)(lambda: exchange.start(xs_refs, land_refs, sems))

        @pl.when((kj == 0) & (qi == 0))
        def _():
            dqt_s[...] = jnp.zeros(dqt_s.shape, F32)

        @pl.when(qi == 0)
        def _():
            dk_ref[...] = jnp.zeros(dk_ref.shape, F32)
            dv_ref[...] = jnp.zeros(dv_ref.shape, F32)

        kk = k_ref[0]
        vv = v_ref[0] if has_v else kk[:, :dv]
        qqs, dds, sts, dps = [], [], [], []
        for u in range(nsub):
            rows = slice(tq * u, tq * (u + 1))
            qqs.append(q_ref[:, rows, :].reshape(m_cols, dq))
            dds.append(jnp.concatenate([do_ref[rows, dv * hh:dv * hh + dv] for hh in range(hpg)], axis=0))
            sts.append(_mm_nt(kk, qqs[u]))
            dps.append(_mm_nt(vv, dds[u]))
        dv_acc = dv_ref[0]
        dk_acc = dk_ref[0]
        for u in range(nsub):
            rows = slice(tq * u, tq * (u + 1))
            lse_row = jnp.concatenate([lse_ref[hh, :, rows] for hh in range(hpg)], axis=1)
            dl_row = jnp.concatenate([dl_ref[hh, :, rows] for hh in range(hpg)], axis=1)
            p = jnp.exp2(sts[u] * c - lse_row)
            ds = p * (dps[u] - dl_row) * scale
            dv_acc = dv_acc + _mm(p, dds[u])
            dk_acc = dk_acc + _mm(ds, qqs[u])
            dqt = _mm(kt_ref[0], ds)
            for hh in range(hpg):
                dqt_s[qi * nsub + u, dq * hh:dq * hh + dq, :] += dqt[:, tq * hh:tq * hh + tq]
        dv_ref[0] = dv_acc
        dk_ref[0] = dk_acc

        @pl.when((kj == nkb - 1) & (qi == nqs - 1))
        def _():
            def emit(t, carry):
                r0 = pl.multiple_of(t * tq, tq)
                for hh in range(hpg):
                    blk = dqt_s[t, dq * hh:dq * hh + dq, :].T
                    if split is None:
                        dq_ref[pl.ds(r0, tq), dq * hh:dq * hh + dq] = blk
                    else:
                        rest = dq - split
                        dq_ref[pl.ds(r0, tq), split * hh:split * (hh + 1)] = blk[:, 0:split]
                        dq_ref[pl.ds(r0, tq), hpg * split + rest * hh:hpg * split + rest * (hh + 1)] = blk[:, split:]
                return carry

            lax.fori_loop(0, nq, emit, 0)

        if hosted:
            pl.when(last)(lambda: exchange.wait(xs_refs, land_refs, sems))

    kv_of = lambda g: g * g_kv // gq
    in_specs = [pl.BlockSpec((hpg, tqq, dq), lambda g, kj, qi: (g, qi, 0)),
                pl.BlockSpec((1, tk, dq), lambda g, kj, qi: (kv_of(g), kj, 0)),
                pl.BlockSpec((1, dq, tk), lambda g, kj, qi: (kv_of(g), 0, kj))]
    args = [q, k, kt]
    if has_v:
        in_specs.append(pl.BlockSpec((1, tk, dv), lambda g, kj, qi: (kv_of(g), kj, 0)))
        args.append(v)
    row_map = lambda g, kj, qi: (g, 0, qi)
    in_specs += [pl.BlockSpec((tqq, hpg * dv), lambda g, kj, qi: (qi, g)),
                 pl.BlockSpec((hpg, 1, tqq), row_map), pl.BlockSpec((hpg, 1, tqq), row_map)]
    args += [do, lse, delta]
    if hosted:
        in_specs += exchange.in_specs
        args += exchange.srcs
    sd = jax.ShapeDtypeStruct
    return pl.pallas_call(
        body, name=name, grid=grid,
        out_shape=(sd((s, hq * dq), F32), sd((gq, s, dq), F32), sd((gq, s, dv), F32))
        + (exchange.land_shapes if hosted else ()),
        in_specs=in_specs,
        out_specs=(pl.BlockSpec((s, hpg * dq), lambda g, kj, qi: (0, g)),
                   pl.BlockSpec((1, tk, dq), lambda g, kj, qi: (g, kj, 0)),
                   pl.BlockSpec((1, tk, dv), lambda g, kj, qi: (g, kj, 0))) + (exchange.out_specs if hosted else ()),
        scratch_shapes=[pltpu.VMEM((nq, hpg * dq, tq), F32)] + (list(exchange.sems) if hosted else []),
        compiler_params=_params(("arbitrary",) * 3 if hosted else ("parallel", "arbitrary", "arbitrary")),
    )(*args)


def loss_head(x, target, fnw):
    s, d = x.shape
    tm = min(ROW_TILE, s)

    def body(x_ref, t_ref, w_ref, lp_ref, dx_ref, dw_ref):
        @pl.when(pl.program_id(0) == 0)
        def _():
            lp_ref[...] = jnp.zeros(lp_ref.shape, F32)
            dw_ref[...] = jnp.zeros(dw_ref.shape, F32)

        x = x_ref[...]
        g = w_ref[...]
        err = x * _rms(x) * g - t_ref[...]
        lp_ref[...] += jnp.sum(err * err, axis=0, keepdims=True)
        dx, dg = _rms_bwd(err * (1.0 / d), x, g)
        dx_ref[...] = dx
        dw_ref[...] += jnp.sum(dg, axis=0, keepdims=True)

    sd = jax.ShapeDtypeStruct
    return pl.pallas_call(
        body, name="loss_head", grid=(s // tm,),
        out_shape=(sd((1, d), F32), sd((s, d), F32), sd((1, d), F32)),
        in_specs=[_row_spec(tm, d), _row_spec(tm, d), _full_spec(fnw.shape)],
        out_specs=(_full_spec((1, d)), _row_spec(tm, d), _full_spec((1, d))),
        compiler_params=_params(("arbitrary",)),
    )(x, target, fnw)


def mixer_out_backward(dx, y, mod, pairs, w_out, delta_heads, name, lse=None, sink=None):
    s, d = dx.shape
    tm = min(ROW_TILE, s)
    n = len(pairs)
    widths = [o.shape[1] for o, _ in pairs]
    n_delta = sum(1 for h in delta_heads if h)
    with_sink = lse is not None

    def body(*refs):
        it = iter(refs)
        dx_ref, y_ref, mod_ref, wt_ref = next(it), next(it), next(it), next(it)
        pr = [next(it) for _ in range(2 * n)]
        lse_ref = next(it) if with_sink else None
        sink_ref = next(it) if with_sink else None
        outs = [next(it) for _ in range(2 * n)]
        dl_refs = [next(it) for _ in range(n_delta)]
        dgate_ref, dw_ref = next(it), next(it)
        dsink_ref = next(it) if with_sink else None

        @pl.when(pl.program_id(0) == 0)
        def _():
            dgate_ref[...] = jnp.zeros(dgate_ref.shape, F32)
            dw_ref[...] = jnp.zeros(dw_ref.shape, F32)
            if with_sink:
                dsink_ref[...] = jnp.zeros(dsink_ref.shape, F32)

        dxo = dx_ref[...]
        dgate_ref[...] += jnp.sum(dxo * y_ref[...], axis=0, keepdims=True)
        dy = (dxo * mod_ref[2:3, :]).astype(MXU)
        dmix = _mm_nt(dy, wt_ref[...])
        r0 = 0
        di = 0
        for i in range(n):
            o = pr[2 * i][...]
            g = pr[2 * i + 1][...]
            dm = dmix[:, r0:r0 + widths[i]]
            sg = _sigmoid(g)
            act = g * sg
            do = dm * act
            outs[2 * i][...] = do.astype(MXU)
            outs[2 * i + 1][...] = (dm * o * (sg * (1.0 + g * (1.0 - sg)))).astype(MXU)
            dw_ref[r0:r0 + widths[i], :] += _mm_tn(o * act, dy)
            if delta_heads[i]:
                dlt = _group_sums_t(do * o, HD)[0:delta_heads[i], :]
                dl_refs[di][...] = dlt
                if with_sink:
                    ps = jnp.exp2(sink_ref[...] - lse_ref[...])
                    dsink_ref[...] += -jnp.sum(ps * dlt, axis=1, keepdims=True)
                di += 1
            r0 += widths[i]

    flat = [a for p in pairs for a in p]
    sd = jax.ShapeDtypeStruct
    in_specs = [_row_spec(tm, d), _row_spec(tm, d), _full_spec(mod.shape), _full_spec(w_out.shape)]
    in_specs += [_row_spec(tm, a.shape[1]) for a in flat]
    args = [dx, y, mod, w_out] + flat
    if with_sink:
        nh = lse.shape[0]
        in_specs += [_rows_spec(nh, tm), _full_spec(sink.shape)]
        args += [lse, sink]
    out_shape = [sd((s, a.shape[1]), MXU) for a in flat]
    out_specs = [_row_spec(tm, a.shape[1]) for a in flat]
    for h in delta_heads:
        if h:
            out_shape.append(sd((h, s), F32))
            out_specs.append(_rows_spec(h, tm))
    out_shape += [sd((1, d), F32), sd((sum(widths), d), F32)]
    out_specs += [_full_spec((1, d)), _full_spec((sum(widths), d))]
    if with_sink:
        out_shape.append(sd((lse.shape[0], 1), F32))
        out_specs.append(_full_spec((lse.shape[0], 1)))
    return pl.pallas_call(
        body, name=name, grid=(s // tm,), out_shape=tuple(out_shape), in_specs=in_specs, out_specs=tuple(out_specs),
        compiler_params=_params(("arbitrary",)),
    )(*args)


def latent_out_backward(d_ob, o_lat, w_uv):
    s = o_lat.shape[0]
    tm = min(ROW_TILE, s)

    def body(d_ref, o_ref, uv_ref, dol_ref, dl_ref, duv_ref, prod_s):
        @pl.when(pl.program_id(0) == 0)
        def _():
            duv_ref[...] = jnp.zeros(duv_ref.shape, F32)

        for hh in range(B_HEADS):
            dh = d_ref[:, HD * hh:HD * hh + HD]
            ol = o_ref[:, B_KV_LORA * hh:B_KV_LORA * (hh + 1)]
            dol = _mm_nt(dh, uv_ref[hh])
            dol_ref[:, B_KV_LORA * hh:B_KV_LORA * (hh + 1)] = dol.astype(MXU)
            prod_s[:, B_KV_LORA * hh:B_KV_LORA * (hh + 1)] = dol * ol
            duv_ref[hh] += _mm_tn(ol, dh)
        dl_ref[...] = _group_sums_t(prod_s[...], B_KV_LORA)[0:B_HEADS, :]

    sd = jax.ShapeDtypeStruct
    return pl.pallas_call(
        body, name="latent_out_backward", grid=(s // tm,),
        out_shape=(sd(o_lat.shape, MXU), sd((B_HEADS, s), F32), sd(w_uv.shape, F32)),
        in_specs=[_row_spec(tm, d_ob.shape[1]), _row_spec(tm, o_lat.shape[1]), _full_spec(w_uv.shape)],
        out_specs=(_row_spec(tm, o_lat.shape[1]), _rows_spec(B_HEADS, tm), _full_spec(w_uv.shape)),
        scratch_shapes=[pltpu.VMEM((tm, o_lat.shape[1]), F32)],
        compiler_params=_params(("arbitrary",)),
    )(d_ob, o_lat, w_uv)


def even_prep_backward(dqa, dka, dva, dqb, dkb, dvb, qa_raw, ka_raw, cq_raw, ckv_raw,
                       gq, gk, qln, kvln, w_uq_t, uk_bd, bd, cos_a, sin_a, cos_t, sin_t):
    s = qa_raw.shape[0]
    tm = min(ROW_TILE, s)
    half_lat = B_KV_LORA * B_HEADS // 2
    half_w = dqb.shape[1] // 2

    def body(dqa_ref, dka_ref, dva_ref, dqb_ref, dkb_ref, dvb_ref, qa_ref, ka_ref, cq_ref, ckv_ref,
             gq_ref, gk_ref, qln_ref, kvln_ref, uqt_ref, ukbd_ref, bd_ref, ca_ref, sa_ref, ct_ref, st_ref,
             pqa, pka, pva, pcq, pckv, pkr, gqn, gkn, gqln, gkvln, guq, guk):
        @pl.when(pl.program_id(0) == 0)
        def _():
            for r in (gqn, gkn, gqln, gkvln, guq, guk):
                r[...] = jnp.zeros(r.shape, F32)

        ca, sa, ct, st = ca_ref[...], sa_ref[...], ct_ref[...], st_ref[...]
        wide = lambda t, n: jnp.concatenate([t] * n, axis=1)
        rows = lambda a: jnp.sum(a, axis=0, keepdims=True)
        dx, dg = _head_norm_bwd(_rope_t(dqa_ref[...], wide(ca, 4), wide(sa, 4), 32), qa_ref[...], gq_ref[...],
                                bd_ref, HD)
        pqa[...] = dx.astype(MXU)
        gqn[...] += rows(dg)
        dk_all = jnp.concatenate([dka_ref[g] for g in range(A_KV)], axis=1)
        dx, dg = _head_norm_bwd(_rope_t(dk_all, ca, sa, 32), ka_ref[...], gk_ref[...], bd_ref[0:128, 0:128], HD)
        pka[...] = dx.astype(MXU)
        gkn[...] += rows(dg)
        pva[...] = jnp.concatenate([dva_ref[g] for g in range(A_KV)], axis=1).astype(MXU)
        cq_raw = cq_ref[...]
        cq_n = cq_raw * _rms(cq_raw) * qln_ref[...]
        qb = _mm_nt(cq_n, uqt_ref[...])
        d_lat = jnp.concatenate([dqb_ref[:, 0:half_lat], dqb_ref[:, half_w:half_w + half_lat]], axis=1)
        d_rope = jnp.concatenate([dqb_ref[:, half_lat:half_w], dqb_ref[:, half_w + half_lat:]], axis=1)
        for hh in range(B_HEADS):
            guk[hh] += _mm_tn(d_lat[:, B_KV_LORA * hh:B_KV_LORA * (hh + 1)], qb[:, B_NOPE * hh:B_NOPE * (hh + 1)])
        dqb_all = jnp.concatenate([_mm_nt(d_lat, ukbd_ref[...]),
                                   _rope_t(d_rope, wide(ct, 2), wide(st, 2), 32)], axis=1)
        guq[...] += _mm_tn(dqb_all, cq_n)
        dx, dg = _rms_bwd(_mm(dqb_all, uqt_ref[...]), cq_raw, qln_ref[...])
        pcq[...] = dx.astype(MXU)
        gqln[...] += rows(dg)
        dkb_sum = dkb_ref[0] + dkb_ref[1]
        dckv = dkb_sum[:, 0:B_KV_LORA] + dvb_ref[0] + dvb_ref[1]
        dx, dg = _rms_bwd(dckv, ckv_ref[...], kvln_ref[...])
        pckv[...] = dx.astype(MXU)
        gkvln[...] += rows(dg)
        pkr[...] = _rope_t(dkb_sum[:, B_KV_LORA:B_QK], ct[:, 0:B_ROPE], st[:, 0:B_ROPE], 32).astype(MXU)

    sd = jax.ShapeDtypeStruct
    consts = [gq, gk, qln, kvln, w_uq_t, uk_bd, bd]
    in_specs = [_row_spec(tm, 512), _head_spec(A_KV, tm, HD), _head_spec(A_KV, tm, HD),
                _row_spec(tm, dqb.shape[1]), _head_spec(2, tm, B_QK), _head_spec(2, tm, B_KV_LORA),
                _row_spec(tm, 512), _row_spec(tm, 128), _row_spec(tm, B_Q_LORA), _row_spec(tm, B_KV_LORA)]
    in_specs += [_full_spec(a.shape) for a in consts] + [_row_spec(tm, 128)] * 4
    small = [sd(gq.shape, F32), sd(gk.shape, F32), sd(qln.shape, F32), sd(kvln.shape, F32), sd(w_uq_t.shape, F32),
             sd((B_HEADS, B_KV_LORA, B_NOPE), F32)]
    out_shape = (sd((s, 512), MXU), sd((s, 128), MXU), sd((s, 128), MXU), sd((s, B_Q_LORA), MXU),
                 sd((s, B_KV_LORA), MXU), sd((s, B_ROPE), MXU), *small)
    out_specs = (_row_spec(tm, 512), _row_spec(tm, 128), _row_spec(tm, 128), _row_spec(tm, B_Q_LORA),
                 _row_spec(tm, B_KV_LORA), _row_spec(tm, B_ROPE), *[_full_spec(a.shape) for a in small])
    return pl.pallas_call(
        body, name="even_prep_backward", grid=(s // tm,), out_shape=out_shape, in_specs=in_specs, out_specs=out_specs,
        compiler_params=_params(("arbitrary",)),
    )(dqa, dka, dva, dqb, dkb, dvb, qa_raw, ka_raw, cq_raw, ckv_raw, *consts, cos_a, sin_a, cos_t, sin_t)


def in_proj_backward(x, mod, nw, dx_out, pieces, w_in_t, name):
    s, d = x.shape
    tm = min(ROW_TILE, s)
    n_cols = w_in_t.shape[0]
    n = len(pieces)
    cols = [c for _, c in pieces]

    def body(*refs):
        x_ref, mod_ref, nw_ref, dxo_ref, wt_ref = refs[:5]
        p_refs = refs[5:5 + n]
        dx_ref, dw_ref, dv_ref, acc_ref = refs[5 + n:]
        i = pl.program_id(0)

        @pl.when(i == 0)
        def _():
            dw_ref[...] = jnp.zeros(dw_ref.shape, F32)
            acc_ref[...] = jnp.zeros(acc_ref.shape, F32)

        xn, g1, h = _modulated(x_ref[...], mod_ref, nw_ref)
        hb = h.astype(MXU)
        dh = jnp.zeros((tm, d), F32)
        for pr, (c0, c1) in zip(p_refs, cols):
            pc = pr[...].astype(MXU)
            dh = dh + jnp.dot(pc, wt_ref[c0:c1, :], preferred_element_type=F32)
            dw_ref[c0:c1, :] += _mm_tn(pc, hb)
        acc_ref[0:1, :] += jnp.sum(dh, axis=0, keepdims=True)
        acc_ref[1:2, :] += jnp.sum(dh * xn, axis=0, keepdims=True)
        dxn = dh * g1
        x = x_ref[...]
        r = _rms(x)
        dx_ref[...] = dxo_ref[...] + r * (dxn - xn * jnp.mean(dxn * xn, axis=-1, keepdims=True))

        @pl.when(i == pl.num_programs(0) - 1)
        def _():
            dg1 = acc_ref[1:2, :]
            dv_ref[0:1, :] = acc_ref[0:1, :]
            dv_ref[1:2, :] = dg1 * nw_ref[...]
            dv_ref[2:3, :] = dg1 * (1.0 + mod_ref[1:2, :])
            dv_ref[3:4, :] = jnp.zeros((1, d), F32)

    arrs = [a for a, _ in pieces]
    sd = jax.ShapeDtypeStruct
    return pl.pallas_call(
        body, name=name, grid=(s // tm,),
        out_shape=(sd((s, d), F32), sd((n_cols, d), F32), sd((4, d), F32)),
        in_specs=[_row_spec(tm, d), _full_spec(mod.shape), _full_spec(nw.shape), _row_spec(tm, d),
                  _full_spec(w_in_t.shape)] + [_row_spec(tm, a.shape[1]) for a in arrs],
        out_specs=(_row_spec(tm, d), _full_spec((n_cols, d)), _full_spec((4, d))),
        scratch_shapes=[pltpu.VMEM((8, d), F32)],
        compiler_params=_params(("arbitrary",)),
    )(x, mod, nw, dx_out, w_in_t, *arrs)


def ada_weight_grad(c_all, dmod_cols):
    d = c_all.shape[1]
    w = dmod_cols.shape[2]

    def body(c_ref, dm_ref, out_ref):
        ca = _silu(c_ref[...])
        for l in range(2):
            out_ref[l] = _mm_tn(ca, dm_ref[l])

    return pl.pallas_call(
        body, name="ada_weight_grad",
        out_shape=jax.ShapeDtypeStruct((2, d, w), F32),
        compiler_params=pltpu.CompilerParams(vmem_limit_bytes=VMEM_LIMIT),
    )(c_all, dmod_cols)


def _slot_sum(g_ref):
    g = g_ref[0].astype(F32)
    for k in range(1, g_ref.shape[0]):
        g = g + g_ref[k].astype(F32)
    return g


def _adamw_math(g, w, m, v):
    m_new = ADAM_B1 * m + (1.0 - ADAM_B1) * g
    v_new = ADAM_B2 * v + (1.0 - ADAM_B2) * (g * g)
    m_hat = m_new / (1.0 - ADAM_B1 ** ADAM_STEP)
    v_hat = v_new / (1.0 - ADAM_B2 ** ADAM_STEP)
    return -ADAM_LR * (m_hat / (jnp.sqrt(v_hat) + ADAM_EPS) + ADAM_WD * w), m_new, v_new


def adamw_small(g_alls, ws, ms, vs, loss_all):
    n = len(ws)

    def body(*refs):
        g_refs, w_refs, m_refs, v_refs = (refs[i * n:(i + 1) * n] for i in range(4))
        loss_ref = refs[4 * n]
        outs = refs[4 * n + 1:]
        for i in range(n):
            g = _slot_sum(g_refs[i])
            outs[i][...] = g
            outs[n + i][...], outs[2 * n + i][...], outs[3 * n + i][...] = _adamw_math(
                g, w_refs[i][...], m_refs[i][...], v_refs[i][...])
        outs[4 * n][...] = _slot_sum(loss_ref)

    sds = [jax.ShapeDtypeStruct(w.shape, F32) for w in ws]
    res = pl.pallas_call(
        body, name="adamw_small", out_shape=tuple(sds * 4) + (jax.ShapeDtypeStruct(loss_all.shape[1:], F32),),
        compiler_params=pltpu.CompilerParams(vmem_limit_bytes=VMEM_LIMIT),
    )(*g_alls, *ws, *ms, *vs, loss_all)
    return [res[i * n:(i + 1) * n] for i in range(4)], res[4 * n]


def adamw_rows(g_slots, w, m, v, name):
    n, r, lanes = g_slots.shape
    fits = [t for t in range(16, r + 1, 16) if r % t == 0 and t * lanes <= ADAM_TILE]
    tr = max(fits) if fits else r
    def body(g_ref, w_ref, m_ref, v_ref, go, do, mo, vo):
        g = _slot_sum(g_ref)
        go[...] = g
        do[...], mo[...], vo[...] = _adamw_math(g, w_ref[...], m_ref[...], v_ref[...])

    row = pl.BlockSpec((tr, lanes), lambda i: (i, 0))
    sd = jax.ShapeDtypeStruct((r, lanes), F32)
    return pl.pallas_call(
        body, name=name, grid=(r // tr,), out_shape=(sd, sd, sd, sd),
        in_specs=[pl.BlockSpec((n, tr, lanes), lambda i: (0, i, 0)), row, row, row],
        out_specs=(row, row, row, row),
        compiler_params=_params(("parallel",)),
    )(g_slots, w, m, v)


def _rope_tables(s):
    def cs(pos, dim):
        inv = ROPE_THETA ** (-np.arange(0, dim, 2, dtype=np.float32) / dim)
        ang = pos.astype(np.float32)[:, None] * inv.astype(np.float32)[None, :]
        return np.cos(ang), np.sin(ang)

    rows = s // GRID_W
    row = np.repeat(np.arange(rows), GRID_W)
    col = np.tile(np.arange(GRID_W), rows)
    cr, sr = cs(row, HD // 2)
    cc, sc = cs(col, HD // 2)
    ct, st = cs(np.arange(s), B_ROPE)
    tables = (np.concatenate([cr, cr, cc, cc] * 2, axis=-1), np.concatenate([-sr, sr, -sc, sc] * 2, axis=-1),
              np.concatenate([ct, ct] * 4, axis=-1), np.concatenate([-st, st] * 4, axis=-1))
    return tuple(jnp.asarray(t, F32) for t in tables)


def _even_rows_to_kernel(wt):
    return jnp.concatenate([wt[:1664], wt[1696:], wt[1664:1696]], axis=0)


def _even_rows_to_reference(wt):
    return jnp.concatenate([wt[:1664], wt[2176:], wt[1664:2176]], axis=0)


def _uq_rows_to_kernel(wt):
    r = wt.reshape(B_HEADS, B_NOPE + B_ROPE, -1)
    return jnp.concatenate([r[:, :B_NOPE].reshape(B_HEADS * B_NOPE, -1), r[:, B_NOPE:].reshape(B_HEADS * B_ROPE, -1)])


def _uq_rows_to_reference(wt):
    nope = wt[:B_HEADS * B_NOPE].reshape(B_HEADS, B_NOPE, -1)
    rope = wt[B_HEADS * B_NOPE:].reshape(B_HEADS, B_ROPE, -1)
    return jnp.concatenate([nope, rope], axis=1).reshape(B_HEADS * (B_NOPE + B_ROPE), -1)


def _shard_t(w):
    return jnp.transpose(w[0])


def _unshard_t(wt, like):
    return jnp.transpose(wt)[None].reshape(like.shape)


def kernel(x, c, norm_w, ada_w, ada_b, even_w_in, a_q_norm, a_k_norm, b_q_lora_norm, b_kv_lora_norm, b_w_uq, b_w_uk, b_w_uv, even_w_out, odd_w_in, c_sink, odd_w_out, final_norm, loss_target, m_norm_w, m_ada_w, m_ada_b, m_even_w_in, m_a_q_norm, m_a_k_norm, m_b_q_lora_norm, m_b_kv_lora_norm, m_b_w_uq, m_b_w_uk, m_b_w_uv, m_even_w_out, m_odd_w_in, m_c_sink, m_odd_w_out, m_final_norm, v_norm_w, v_ada_w, v_ada_b, v_even_w_in, v_a_q_norm, v_a_k_norm, v_b_q_lora_norm, v_b_kv_lora_norm, v_b_w_uq, v_b_w_uk, v_b_w_uv, v_even_w_out, v_odd_w_in, v_c_sink, v_odd_w_out, v_final_norm):
    s, d = x.shape[1], x.shape[2]
    x0 = x[0]
    target = loss_target[0]
    me_flat = 4 * lax.axis_index("x") + 2 * lax.axis_index("y") + lax.axis_index("c")

    wcols = ada_w.shape[2]
    bias_cols = lax.dynamic_slice_in_dim(ada_b.reshape(2, N_DEV, wcols), me_flat, 1, axis=1)
    call, modp, (g_in_e, g_uq) = ada_forward(
        jnp.broadcast_to(c, (8, d)), ada_w, bias_cols,
        Gather([_shard_t(even_w_in).astype(MXU), _shard_t(b_w_uq).astype(MXU)]))
    wt_in_e = _even_rows_to_kernel(g_in_e.reshape(-1, d))
    wt_uq = _uq_rows_to_kernel(g_uq.reshape(-1, B_Q_LORA))
    later_exchange = Exchange([_shard_t(odd_w_in).astype(MXU), even_w_out[0].astype(MXU),
                               odd_w_out[0].astype(MXU)], scatter=False)
    uk_bd = (jnp.eye(B_HEADS, dtype=F32)[:, None, :, None] * jnp.transpose(b_w_uk[0], (1, 2, 0))[:, :, None, :]
             ).reshape(B_HEADS * B_NOPE, B_HEADS * B_KV_LORA).astype(MXU)
    head_bd = jnp.asarray(np.kron(np.eye(A_HEADS), np.ones((HD, HD))), MXU)
    gq_full, gk_full = jnp.tile(a_q_norm, (1, A_HEADS)), jnp.tile(a_k_norm, (1, A_KV))
    w_uv = jnp.transpose(b_w_uv[0], (1, 0, 2)).astype(MXU)

    c_all = call[:, 0, :]
    mod = jnp.transpose(modp[:, :, 0, :], (1, 0, 2)).reshape(2, 3, d)
    mod_e, mod_o = mod[0], mod[1]
    nw_e, nw_o = norm_w[0:1], norm_w[1:2]

    cos_a, sin_a, cos_t, sin_t = _rope_tables(s)
    slopes = (2.0 ** (-8.0 * jnp.arange(1, C_HEADS + 1, dtype=F32) / C_HEADS)).reshape(C_HEADS, 1, 1)
    sink2 = c_sink.reshape(C_HEADS, 1, 1) * LOG2E

    (qa, ka, va, qb, kb, kat, vat, kbt, qa_raw, ka_raw, cq_raw, ckv_raw, ga, gb) = even_in_forward(
        x0, mod_e, nw_e, wt_in_e, gq_full, gk_full, b_q_lora_norm, b_kv_lora_norm, wt_uq, uk_bd, head_bd,
        cos_a, sin_a, cos_t, sin_t)
    tk_dense = min(512, s)
    tq_dense = min(256, s)
    fwd_sub = min(8, s // tk_dense)
    bwd_sub = min(4, s // tq_dense)
    oa, lse_a, g_in_o, g_out_e, g_out_o = flash_forward(
        qa, ka, vat, scale=HD ** -0.5, dv=HD, tq=tq_dense, tk=tk_dense, nsub=fwd_sub, name="attn_a_fwd",
        exchange=later_exchange)
    wt_in_o = g_in_o.reshape(-1, d)
    w_out_e = g_out_e.reshape(-1, d)
    w_out_o = g_out_o.reshape(-1, d)
    scale_b = (B_NOPE + B_ROPE) ** -0.5
    o_lat, lse_b = flash_forward(qb, kb, kbt, scale=scale_b, dv=B_KV_LORA, tq=min(128, s), tk=tk_dense, nsub=fwd_sub,
                                 name="attn_b_fwd")
    ob = latent_out_forward(o_lat, w_uv)
    x1, y_e = mixer_out_forward(x0, mod_e, [(oa, ga), (ob, gb)], w_out_e, "even_out_fwd")

    qc, kc, vc, kct, vct, gc = odd_in_forward(x1, mod_o, nw_o, wt_in_o)
    win_sub = min(8, s // WINDOW)
    oc, lse_c = window_forward(qc, kc, vct, sink2, slopes, win_sub, "attn_c_fwd")
    x2, y_o = mixer_out_forward(x1, mod_o, [(oc, gc)], w_out_o, "odd_out_fwd")

    loss_lanes, dx2, d_final = loss_head(x2, target, final_norm.reshape(1, d))
    loss_part = (0.5 / d) * jnp.sum(loss_lanes)

    doc, dgc, delta_c, dgate_o, dw_out_o, dsink = mixer_out_backward(
        dx2, y_o, mod_o, [(oc, gc)], w_out_o, [C_HEADS], "odd_out_bwd", lse=lse_c.reshape(C_HEADS, s),
        sink=sink2.reshape(C_HEADS, 1))
    rows3 = lambda t: t.reshape(t.shape[0], 1, s)
    dqc, dkc, dvc = window_backward(qc, kc, kct, vc, doc, lse_c, rows3(delta_c), slopes, win_sub, "attn_c_bwd")
    to_rows = lambda t: jnp.transpose(t, (1, 0, 2)).reshape(s, -1)
    dx1, dwt_in_o, dvec_o = in_proj_backward(
        x1, mod_o, nw_o, dx2, [(dqc, O_Q), (to_rows(dkc), O_K), (to_rows(dvc), O_V), (dgc, O_G)], wt_in_o,
        "odd_in_bwd")

    doa, dga, dob, dgb, delta_a, dgate_e, dw_out_e = mixer_out_backward(
        dx1, y_e, mod_e, [(oa, ga), (ob, gb)], w_out_e, [A_HEADS, 0], "even_out_bwd")
    d_olat, delta_b, dw_uv = latent_out_backward(dob, o_lat, w_uv)
    blocks = lambda g: g.astype(MXU).reshape(N_DEV, g.shape[0] // N_DEV, g.shape[1])
    scatter_odd = Exchange([blocks(dwt_in_o), blocks(dw_out_o)], True)
    scatter_out_e = Exchange([blocks(dw_out_e)], True)
    dqb, dkb, dvb, l_in_o, l_out_o = flash_backward(
        qb, kb, kbt, None, d_olat, lse_b, rows3(delta_b), scale=scale_b, dv=B_KV_LORA,
        tq=tq_dense, tk=tk_dense, nsub=bwd_sub, gq=2, name="attn_b_bwd", split=B_KV_LORA, exchange=scatter_odd)
    dqa, dka, dva, l_out_e = flash_backward(
        qa, ka, kat, va, doa, lse_a, rows3(delta_a), scale=HD ** -0.5, dv=HD,
        tq=tq_dense, tk=tk_dense, nsub=bwd_sub, gq=A_KV, name="attn_a_bwd", exchange=scatter_out_e)
    (pqa, pka, pva, pcq, pckv, pkr, g_qn, g_kn, g_qln, g_kvln, dwt_uq, dw_uk) = even_prep_backward(
        dqa, dka, dva, dqb, dkb, dvb, qa_raw, ka_raw, cq_raw, ckv_raw,
        gq_full, gk_full, b_q_lora_norm, b_kv_lora_norm, wt_uq, uk_bd, head_bd, cos_a, sin_a, cos_t, sin_t)
    g_qn = jnp.sum(g_qn.reshape(A_HEADS, HD), axis=0)
    g_kn = jnp.sum(g_kn.reshape(A_KV, HD), axis=0)
    dx0, dwt_in_e, dvec_e = in_proj_backward(
        x0, mod_e, nw_e, dx1,
        [(pqa, E_QA), (pka, E_KA), (pva, E_VA), (dga, E_GA), (pcq, E_CQ), (pckv, E_CKV), (dgb, E_GB), (pkr, E_KR)],
        wt_in_e, "even_in_bwd")

    dmod = jnp.stack([jnp.concatenate([dvec_e[0], dvec_e[1], dgate_e[0]]),
                      jnp.concatenate([dvec_o[0], dvec_o[1], dgate_o[0]])])
    d_norm_w = jnp.stack([dvec_e[2], dvec_o[2]])
    small_names = ["norm_w", "ada_b", "a_q_norm", "a_k_norm", "b_q_lora_norm", "b_kv_lora_norm", "b_w_uk", "b_w_uv",
                   "c_sink", "final_norm"]
    small_w = [norm_w, ada_b, a_q_norm, a_k_norm, b_q_lora_norm, b_kv_lora_norm, b_w_uk, b_w_uv, c_sink, final_norm]
    small_m = [m_norm_w, m_ada_b, m_a_q_norm, m_a_k_norm, m_b_q_lora_norm, m_b_kv_lora_norm, m_b_w_uk, m_b_w_uv,
               m_c_sink, m_final_norm]
    small_v = [v_norm_w, v_ada_b, v_a_q_norm, v_a_k_norm, v_b_q_lora_norm, v_b_kv_lora_norm, v_b_w_uk, v_b_w_uv,
               v_c_sink, v_final_norm]
    small_g = [d_norm_w, dmod, g_qn, g_kn, g_qln, g_kvln, jnp.transpose(dw_uk, (1, 0, 2)), jnp.transpose(dw_uv, (1, 0, 2)),
               dsink, d_final]
    flat2 = lambda a: a.reshape((1, -1)) if a.size == a.shape[-1] else a.reshape(a.shape[-3:] if a.ndim > 3 else a.shape)
    kshape = [flat2(w).shape for w in small_w]
    (l_in_e, l_uq), g_all = scatter_and_gather(
        Exchange([blocks(_even_rows_to_reference(dwt_in_e)), blocks(_uq_rows_to_reference(dwt_uq))], True),
        Gather([g.reshape(sh) for g, sh in zip(small_g, kshape)] + [jnp.full((8, 128), loss_part, F32)]),
        "last_exchanges")
    sm_out, loss_sum = adamw_small(g_all[:-1], [flat2(a) for a in small_w], [flat2(a) for a in small_m],
                                   [flat2(a) for a in small_v], g_all[-1])
    loss = loss_sum[0, 0]
    sm = [{nm: p.reshape(w.shape) for nm, w, p in zip(small_names, small_w, outs)} for outs in sm_out]

    dmod_all = g_all[1].reshape(N_DEV, 2, N_DEV, wcols)
    dmod_cols = lax.dynamic_slice_in_dim(dmod_all, me_flat, 1, axis=2)[:, :, 0, :]
    pad16 = lambda a: jnp.concatenate([a, jnp.zeros_like(a)], axis=0)
    g_ada_w = ada_weight_grad(pad16(c_all), jnp.transpose(pad16(dmod_cols), (1, 0, 2)))
    rows_of = lambda a: a.reshape(-1, wcols)
    ada = adamw_rows(rows_of(g_ada_w)[None], rows_of(ada_w), rows_of(m_ada_w), rows_of(v_ada_w), "adamw_ada_w")
    ada = [p.reshape(ada_w.shape) for p in ada]

    bg = [{}, {}, {}, {}]
    for nm, landed, w, m, v, transposed in (
            ("even_w_in", l_in_e, even_w_in, m_even_w_in, v_even_w_in, True),
            ("b_w_uq", l_uq, b_w_uq, m_b_w_uq, v_b_w_uq, True),
            ("odd_w_in", l_in_o, odd_w_in, m_odd_w_in, v_odd_w_in, True),
            ("even_w_out", l_out_e, even_w_out, m_even_w_out, v_even_w_out, False),
            ("odd_w_out", l_out_o, odd_w_out, m_odd_w_out, v_odd_w_out, False)):
        view = _shard_t if transposed else (lambda a: a[0])
        res = adamw_rows(landed, view(w), view(m), view(v), "adamw_" + nm)
        for kind, p in enumerate(res):
            bg[kind][nm] = _unshard_t(p, w) if transposed else p[None]
    big_names = ["even_w_in", "odd_w_in", "even_w_out", "odd_w_out", "b_w_uq"]

    order = ["norm_w", "ada_w", "ada_b", "even_w_in", "a_q_norm", "a_k_norm", "b_q_lora_norm", "b_kv_lora_norm",
             "b_w_uq", "b_w_uk", "b_w_uv", "even_w_out", "odd_w_in", "c_sink", "odd_w_out", "final_norm"]

    def pick(kind):
        out = []
        for nm in order:
            if nm == "ada_w":
                out.append(ada[kind])
            elif nm in big_names:
                out.append(bg[kind][nm])
            else:
                out.append(sm[kind][nm])
        return out

    return (loss, dx0[None], *pick(0), *pick(1), *pick(2), *pick(3))
```

```python
import functools

import jax
import jax.numpy as jnp
import numpy as np
from jax import lax
from jax.experimental import pallas as pl
from jax.experimental.pallas import tpu as pltpu

F32 = jnp.float32
MXU = jnp.bfloat16
EPS = 1e-6
ROPE_THETA = 10000.0
GRID_W = 64
HD = 64
N_DEV = 8

A_HEADS, A_KV = 8, 2
B_HEADS, B_NOPE, B_ROPE, B_Q_LORA, B_KV_LORA = 8, 64, 32, 256, 128
B_QK = B_KV_LORA + B_ROPE
C_HEADS, C_KV = 16, 4
WINDOW = 128

ADAM_LR, ADAM_B1, ADAM_B2, ADAM_EPS, ADAM_WD, ADAM_STEP = 0.001, 0.9, 0.999, 1e-08, 0.01, 10

ROW_TILE = 512
ADAM_TILE = 2048 * 128
VMEM_LIMIT = 56 * 1024 * 1024

E_QA, E_KA, E_VA, E_GA, E_CQ, E_CKV, E_GB, E_KR = (
    (0, 512), (512, 640), (640, 768), (768, 1280), (1280, 1536), (1536, 1664), (1664, 2176), (2176, 2208))
EVEN_IN = 2208
O_Q, O_K, O_V, O_G = (0, 1024), (1024, 1280), (1280, 1536), (1536, 2560)
ODD_IN = 2560


def _mm(a, b):
    return jnp.dot(a.astype(MXU), b.astype(MXU), preferred_element_type=F32)


def _mm_nt(a, b):
    return lax.dot_general(a.astype(MXU), b.astype(MXU), (((1,), (1,)), ((), ())), preferred_element_type=F32)


def _mm_tn(a, b):
    return lax.dot_general(a.astype(MXU), b.astype(MXU), (((0,), (0,)), ((), ())), preferred_element_type=F32)


def _group_sums_t(prod, group):
    tm, w = prod.shape
    sel = (lax.broadcasted_iota(jnp.int32, (w, 128), 0) // group
           == lax.broadcasted_iota(jnp.int32, (w, 128), 1)).astype(MXU)
    hi = prod.astype(MXU)
    lo = prod - hi.astype(F32)
    return (_mm(hi, sel) + _mm(lo, sel)).T


def _sigmoid(z):
    return 1.0 / (1.0 + jnp.exp(-z))


def _silu(z):
    return z * _sigmoid(z)


def _rms(x):
    return lax.rsqrt(jnp.mean(x * x, axis=-1, keepdims=True) + EPS)


def _swap_halves(y, group):
    n = y.shape[-1]
    half = group // 2
    fwd = pltpu.roll(y, half, 1)
    if n == group:
        return fwd
    back = pltpu.roll(y, n - half, 1)
    lane = lax.broadcasted_iota(jnp.int32, y.shape, 1)
    return jnp.where((lane % group) < half, back, fwd)


def _rope(y, cos, sin, group):
    return y * cos + _swap_halves(y, group) * sin


def _rope_t(d, cos, sin, group):
    return d * cos - _swap_halves(d, group) * sin


def _rms_bwd(dy, x, g):
    r = _rms(x)
    xhat = x * r
    dxhat = dy * g
    dx = r * (dxhat - xhat * jnp.mean(dxhat * xhat, axis=-1, keepdims=True))
    return dx, dy * xhat


def _group_mean(v, bd, group):
    hi = v.astype(MXU)
    lo = v - hi.astype(F32)
    return (_mm(hi, bd[...]) + _mm(lo, bd[...])) * (1.0 / group)


def _head_norm(x, g, bd, group):
    return x * lax.rsqrt(_group_mean(x * x, bd, group) + EPS) * g


def _head_norm_bwd(dy, x, g, bd, group):
    r = lax.rsqrt(_group_mean(x * x, bd, group) + EPS)
    xhat = x * r
    dxhat = dy * g
    dx = r * (dxhat - xhat * _group_mean(dxhat * xhat, bd, group))
    return dx, dy * xhat


def _params(sem, vmem=VMEM_LIMIT):
    return pltpu.CompilerParams(dimension_semantics=sem, vmem_limit_bytes=vmem)


def _row_spec(tm, w):
    return pl.BlockSpec((tm, w), lambda i: (i, 0))


def _full_spec(shape):
    nd = len(shape)
    return pl.BlockSpec(shape, lambda i: (0,) * nd)


def _head_spec(h, tm, w):
    return pl.BlockSpec((h, tm, w), lambda i: (0, i, 0))


def _headt_spec(h, w, tm):
    return pl.BlockSpec((h, w, tm), lambda i: (0, 0, i))


def _rows_spec(h, tm):
    return pl.BlockSpec((h, tm), lambda i: (0, i))


def _me():
    return lax.axis_index("x"), lax.axis_index("y"), lax.axis_index("c")


def _flat(p):
    return 4 * p[0] + 2 * p[1] + p[2]


def _peer(me, k):
    x, y, c = me
    return (1 - x if k & 4 else x, 1 - y if k & 2 else y, 1 - c if k & 1 else c)


MESH_ID = pl.DeviceIdType.MESH


class Gather:
    VMEM = pl.BlockSpec(memory_space=pltpu.VMEM)

    def __init__(self, shards):
        self.shards = list(shards)
        self.n = len(self.shards)
        self.out_shapes = tuple(jax.ShapeDtypeStruct((N_DEV,) + a.shape, a.dtype) for a in self.shards)
        self.in_specs = [Gather.VMEM] * self.n
        self.out_specs = (Gather.VMEM,) * self.n
        self.sems = [pltpu.SemaphoreType.DMA((7 * self.n,)), pltpu.SemaphoreType.DMA((7 * self.n,)),
                     pltpu.SemaphoreType.DMA((self.n,))]

    def _plan(self, x_refs, out_refs, sems):
        send_sems, recv_sems, local_sems = sems
        me = _me()
        x, y, c = me
        chips = [(1 - x, y), (x, 1 - y), (1 - x, 1 - y)]

        def copy(a, k, block, to, src=None):
            slot = out_refs[a].at[_flat(block)]
            return pltpu.make_async_remote_copy(
                src_ref=slot if src is None else src, dst_ref=slot, send_sem=send_sems.at[7 * a + k],
                recv_sem=recv_sems.at[7 * a + k], device_id=to, device_id_type=MESH_ID)

        mine = [pltpu.make_async_copy(x_refs[a], out_refs[a].at[_flat(me)], local_sems.at[a]) for a in range(self.n)]
        first = [copy(a, 0, me, (x, y, 1 - c), src=x_refs[a]) for a in range(self.n)]
        first += [copy(a, 1 + j, me, (*chip, c), src=x_refs[a]) for a in range(self.n) for j, chip in enumerate(chips)]
        return me, chips, copy, mine, first

    def start(self, x_refs, out_refs, sems):
        _, _, _, mine, first = self._plan(x_refs, out_refs, sems)
        for cp in mine + first:
            cp.start()

    def finish(self, x_refs, out_refs, sems):
        me, chips, copy, mine, first = self._plan(x_refs, out_refs, sems)
        x, y, c = me
        sibling = (x, y, 1 - c)
        passed = []
        for a in range(self.n):
            for j, chip in enumerate(chips):
                copy(a, 1 + j, (*chip, c), me).wait_recv()
                passed.append(copy(a, 4 + j, (*chip, c), sibling))
                passed[-1].start()
        for a in range(self.n):
            copy(a, 0, sibling, me).wait_recv()
            for j, chip in enumerate(chips):
                copy(a, 4 + j, (*chip, 1 - c), me).wait_recv()
        for cp in first + passed:
            cp.wait_send()
        for cp in mine:
            cp.wait()


def all_gather_slots(gather, name):
    def body(*refs):
        x_refs, out_refs, sems = refs[:gather.n], refs[gather.n:2 * gather.n], refs[2 * gather.n:]
        gather.start(x_refs, out_refs, sems)
        gather.finish(x_refs, out_refs, sems)

    return pl.pallas_call(
        body, name=name, out_shape=gather.out_shapes, in_specs=gather.in_specs, out_specs=gather.out_specs,
        scratch_shapes=list(gather.sems), compiler_params=pltpu.CompilerParams(vmem_limit_bytes=VMEM_LIMIT),
    )(*gather.shards)


class Exchange:
    HBM = pl.BlockSpec(memory_space=pl.ANY)

    def __init__(self, srcs, scatter):
        self.srcs = list(srcs)
        self.scatter = scatter
        self.n = len(self.srcs)
        self.land_shapes = tuple(jax.ShapeDtypeStruct((N_DEV,) + tuple(a.shape[-2:]), a.dtype) for a in self.srcs)
        self.in_specs = [Exchange.HBM] * self.n
        self.out_specs = (Exchange.HBM,) * self.n
        self.sems = [pltpu.SemaphoreType.DMA((N_DEV - 1,)), pltpu.SemaphoreType.DMA((N_DEV - 1,)),
                     pltpu.SemaphoreType.DMA] * self.n

    def _copies(self, src_refs, land_refs, sems):
        me = _me()
        mi = _flat(me)
        local, sends, recvs = [], [], []
        for a, (src_ref, land_ref) in enumerate(zip(src_refs, land_refs)):
            send_sems, recv_sems, local_sem = sems[3 * a:3 * a + 3]
            pick = (lambda p, r=src_ref: r.at[_flat(p)]) if self.scatter else (lambda p, r=src_ref: r)
            local.append(pltpu.make_async_copy(pick(me), land_ref.at[mi], local_sem))
            for k in range(1, N_DEV):
                peer = _peer(me, k)
                pair = dict(send_sem=send_sems.at[k - 1], recv_sem=recv_sems.at[k - 1], device_id=peer,
                            device_id_type=MESH_ID)
                sends.append(pltpu.make_async_remote_copy(src_ref=pick(peer), dst_ref=land_ref.at[mi], **pair))
                recvs.append(pltpu.make_async_remote_copy(src_ref=pick(peer), dst_ref=land_ref.at[_flat(peer)],
                                                          **pair))
        return local, sends, recvs

    def start(self, src_refs, land_refs, sems):
        local, sends, _ = self._copies(src_refs, land_refs, sems)
        for cp in local + sends:
            cp.start()

    def wait(self, src_refs, land_refs, sems):
        local, sends, recvs = self._copies(src_refs, land_refs, sems)
        for cp in recvs:
            cp.wait_recv()
        for cp in sends:
            cp.wait_send()
        for cp in local:
            cp.wait()


def ada_forward(c8, ada_w, bias_cols, gather):
    d = c8.shape[1]
    w = ada_w.shape[2]
    ng = gather.n

    def body(*refs):
        c_ref, w_ref, b_ref = refs[:3]
        gx_refs = refs[3:3 + ng]
        call_ref, modp_ref = refs[3 + ng:5 + ng]
        gout_refs = refs[5 + ng:5 + 2 * ng]
        part_ref, s1, r1, s2, r2 = refs[5 + 2 * ng:10 + 2 * ng]
        g_sems = refs[10 + 2 * ng:]
        gather.start(gx_refs, gout_refs, g_sems)
        me = _me()
        mi = _flat(me)
        call_ref[mi] = c_ref[...]
        rows_out = []
        for k in range(1, N_DEV):
            rows_out.append(pltpu.make_async_remote_copy(
                src_ref=c_ref, dst_ref=call_ref.at[mi], send_sem=s1.at[k - 1], recv_sem=r1.at[k - 1],
                device_id=_peer(me, k), device_id_type=MESH_ID))
        for cp in rows_out:
            cp.start()
        for k in range(1, N_DEV):
            pltpu.make_async_remote_copy(
                src_ref=c_ref, dst_ref=call_ref.at[_flat(_peer(me, k))], send_sem=s1.at[k - 1],
                recv_sem=r1.at[k - 1], device_id=_peer(me, k), device_id_type=MESH_ID).wait_recv()
        ca = _silu(call_ref[...].reshape(N_DEV * 8, d))
        for l in range(2):
            part = _mm(ca, w_ref[l]) + b_ref[l]
            for b in range(N_DEV):
                part_ref[b, l] = part[8 * b:8 * b + 8, :]
        modp_ref[mi] = part_ref[mi]
        spread = []
        for k in range(1, N_DEV):
            peer = _peer(me, k)
            spread.append(pltpu.make_async_remote_copy(
                src_ref=part_ref.at[_flat(peer)], dst_ref=modp_ref.at[mi], send_sem=s2.at[k - 1],
                recv_sem=r2.at[k - 1], device_id=peer, device_id_type=MESH_ID))
        for cp in spread:
            cp.start()
        for k in range(1, N_DEV):
            pi = _flat(_peer(me, k))
            pltpu.make_async_remote_copy(
                src_ref=part_ref.at[pi], dst_ref=modp_ref.at[pi], send_sem=s2.at[k - 1],
                recv_sem=r2.at[k - 1], device_id=_peer(me, k), device_id_type=MESH_ID).wait_recv()
        for cp in rows_out + spread:
            cp.wait_send()
        gather.finish(gx_refs, gout_refs, g_sems)

    vm = pl.BlockSpec(memory_space=pltpu.VMEM)
    res = pl.pallas_call(
        body, name="ada_forward",
        out_shape=(jax.ShapeDtypeStruct((N_DEV, 8, d), F32), jax.ShapeDtypeStruct((N_DEV, 2, 8, w), F32))
        + gather.out_shapes,
        in_specs=[vm, vm, vm] + gather.in_specs, out_specs=(vm, vm) + gather.out_specs,
        scratch_shapes=[pltpu.VMEM((N_DEV, 2, 8, w), F32)] + [pltpu.SemaphoreType.DMA((7,))] * 4 + list(gather.sems),
        compiler_params=pltpu.CompilerParams(vmem_limit_bytes=VMEM_LIMIT),
    )(c8, ada_w, bias_cols, *gather.shards)
    return res[0], res[1], res[2:]


def _modulated(x, mod_ref, nw_ref):
    xn = x * _rms(x)
    g1 = nw_ref[...] * (1.0 + mod_ref[1:2, :])
    return xn, g1, xn * g1 + mod_ref[0:1, :]


def even_in_forward(x, mod, nw, w_in_t, gq, gk, qln, kvln, w_uq_t, uk_bd, bd, cos_a, sin_a, cos_t, sin_t):
    s, d = x.shape
    tm = min(ROW_TILE, s)
    n_nope = B_HEADS * B_NOPE

    def body(x_ref, mod_ref, nw_ref, w_ref, gq_ref, gk_ref, qln_ref, kvln_ref, uq_ref, ukbd_ref, bd_ref,
             ca_ref, sa_ref, ct_ref, st_ref,
             qa_o, ka_o, va_o, qb_o, kb_o, kat_o, vat_o, kbt_o, qa_raw_o, ka_raw_o, cq_raw_o, ckv_raw_o, ga_o, gb_o):
        _, _, h = _modulated(x_ref[...], mod_ref, nw_ref)
        h = h.astype(MXU)

        def proj(cols):
            return _mm_nt(h, w_ref[cols[0]:cols[1], :])

        ca, sa, ct, st = ca_ref[...], sa_ref[...], ct_ref[...], st_ref[...]
        wide = lambda t, n: jnp.concatenate([t] * n, axis=1)
        qa = proj(E_QA)
        qa_raw_o[...] = qa
        qr = _rope(_head_norm(qa, gq_ref[...], bd_ref, HD), wide(ca, 4), wide(sa, 4), 32)
        for hh in range(A_HEADS):
            qa_o[hh] = qr[:, HD * hh:HD * hh + HD].astype(MXU)
        ka = proj(E_KA)
        ka_raw_o[...] = ka
        kr = _rope(_head_norm(ka, gk_ref[...], bd_ref[0:128, 0:128], HD), ca, sa, 32)
        va = proj(E_VA)
        krt, vat = kr.T, va.T
        for g in range(A_KV):
            ka_o[g] = kr[:, HD * g:HD * g + HD].astype(MXU)
            va_o[g] = va[:, HD * g:HD * g + HD].astype(MXU)
            kat_o[g] = krt[HD * g:HD * g + HD, :].astype(MXU)
            vat_o[g] = vat[HD * g:HD * g + HD, :].astype(MXU)
        ga_o[...] = proj(E_GA)
        gb_o[...] = proj(E_GB)
        cq = proj(E_CQ)
        cq_raw_o[...] = cq
        qb = _mm_nt(cq * _rms(cq) * qln_ref[...], uq_ref[...])
        q_lat = _mm(qb[:, 0:n_nope], ukbd_ref[...])
        q_rope = _rope(qb[:, n_nope:], wide(ct, 2), wide(st, 2), 32)
        for hh in range(B_HEADS):
            qb_o[hh, :, 0:B_KV_LORA] = q_lat[:, B_KV_LORA * hh:B_KV_LORA * (hh + 1)].astype(MXU)
            qb_o[hh, :, B_KV_LORA:B_QK] = q_rope[:, B_ROPE * hh:B_ROPE * (hh + 1)].astype(MXU)
        ckv = proj(E_CKV)
        ckv_raw_o[...] = ckv
        ckv_n = ckv * _rms(ckv) * kvln_ref[...]
        k_rope = _rope(proj(E_KR), ct[:, 0:B_ROPE], st[:, 0:B_ROPE], 32)
        kb_o[0, :, 0:B_KV_LORA] = ckv_n.astype(MXU)
        kb_o[0, :, B_KV_LORA:B_QK] = k_rope.astype(MXU)
        kbt_o[0, 0:B_KV_LORA, :] = ckv_n.T.astype(MXU)
        kbt_o[0, B_KV_LORA:B_QK, :] = k_rope.T.astype(MXU)

    sd = jax.ShapeDtypeStruct
    outs = (sd((A_HEADS, s, HD), MXU), sd((A_KV, s, HD), MXU), sd((A_KV, s, HD), MXU),
            sd((B_HEADS, s, B_QK), MXU), sd((1, s, B_QK), MXU),
            sd((A_KV, HD, s), MXU), sd((A_KV, HD, s), MXU), sd((1, B_QK, s), MXU),
            sd((s, 512), F32), sd((s, 128), F32), sd((s, B_Q_LORA), F32), sd((s, B_KV_LORA), F32),
            sd((s, 512), F32), sd((s, 512), F32))
    out_specs = (_head_spec(A_HEADS, tm, HD), _head_spec(A_KV, tm, HD), _head_spec(A_KV, tm, HD),
                 _head_spec(B_HEADS, tm, B_QK), _head_spec(1, tm, B_QK),
                 _headt_spec(A_KV, HD, tm), _headt_spec(A_KV, HD, tm), _headt_spec(1, B_QK, tm),
                 _row_spec(tm, 512), _row_spec(tm, 128), _row_spec(tm, B_Q_LORA), _row_spec(tm, B_KV_LORA),
                 _row_spec(tm, 512), _row_spec(tm, 512))
    consts = [mod, nw, w_in_t, gq, gk, qln, kvln, w_uq_t, uk_bd, bd]
    return pl.pallas_call(
        body, name="even_in_forward", grid=(s // tm,), out_shape=outs,
        in_specs=[_row_spec(tm, d)] + [_full_spec(a.shape) for a in consts] + [_row_spec(tm, 128)] * 4,
        out_specs=out_specs, compiler_params=_params(("parallel",)),
    )(x, *consts, cos_a, sin_a, cos_t, sin_t)


def odd_in_forward(x, mod, nw, w_in):
    s, d = x.shape
    tm = min(ROW_TILE, s)

    def body(x_ref, mod_ref, nw_ref, w_ref, q_o, k_o, v_o, kt_o, vt_o, g_o):
        _, _, h = _modulated(x_ref[...], mod_ref, nw_ref)
        h = h.astype(MXU)

        def proj(cols):
            return _mm_nt(h, w_ref[cols[0]:cols[1], :])

        q = proj(O_Q)
        for hh in range(C_HEADS):
            q_o[hh] = q[:, HD * hh:HD * hh + HD].astype(MXU)
        k = proj(O_K)
        v = proj(O_V)
        for g in range(C_KV):
            kh = k[:, HD * g:HD * g + HD]
            vh = v[:, HD * g:HD * g + HD]
            k_o[g] = kh.astype(MXU)
            v_o[g] = vh.astype(MXU)
            kt_o[g] = kh.T.astype(MXU)
            vt_o[g] = vh.T.astype(MXU)
        g_o[...] = proj(O_G)

    sd = jax.ShapeDtypeStruct
    return pl.pallas_call(
        body, name="odd_in_forward", grid=(s // tm,),
        out_shape=(sd((C_HEADS, s, HD), MXU), sd((C_KV, s, HD), MXU), sd((C_KV, s, HD), MXU),
                   sd((C_KV, HD, s), MXU), sd((C_KV, HD, s), MXU), sd((s, 1024), F32)),
        in_specs=[_row_spec(tm, d), _full_spec(mod.shape), _full_spec(nw.shape), _full_spec(w_in.shape)],
        out_specs=(_head_spec(C_HEADS, tm, HD), _head_spec(C_KV, tm, HD), _head_spec(C_KV, tm, HD),
                   _headt_spec(C_KV, HD, tm), _headt_spec(C_KV, HD, tm), _row_spec(tm, 1024)),
        compiler_params=_params(("parallel",)),
    )(x, mod, nw, w_in)


def latent_out_forward(o_lat, w_uv):
    s = o_lat.shape[0]
    tm = min(ROW_TILE, s)

    def body(o_ref, uv_ref, out_ref):
        for hh in range(B_HEADS):
            out_ref[:, HD * hh:HD * hh + HD] = _mm(o_ref[:, B_KV_LORA * hh:B_KV_LORA * (hh + 1)], uv_ref[hh])

    return pl.pallas_call(
        body, name="latent_out_forward", grid=(s // tm,),
        out_shape=jax.ShapeDtypeStruct((s, B_HEADS * HD), F32),
        in_specs=[_row_spec(tm, o_lat.shape[1]), _full_spec(w_uv.shape)],
        out_specs=_row_spec(tm, B_HEADS * HD),
        compiler_params=_params(("parallel",)),
    )(o_lat, w_uv)


def mixer_out_forward(x, mod, pairs, w_out, name):
    s, d = x.shape
    tm = min(ROW_TILE, s)
    n = len(pairs)
    widths = [o.shape[1] for o, _ in pairs]

    def body(*refs):
        x_ref, mod_ref, w_ref = refs[:3]
        pr = refs[3:3 + 2 * n]
        xo_ref, y_ref = refs[3 + 2 * n:]
        y = jnp.zeros((tm, d), F32)
        r0 = 0
        for i in range(n):
            mix = pr[2 * i][...] * _silu(pr[2 * i + 1][...])
            y = y + _mm(mix, w_ref[r0:r0 + widths[i], :])
            r0 += widths[i]
        y_ref[...] = y
        xo_ref[...] = x_ref[...] + mod_ref[2:3, :] * y

    flat = [a for p in pairs for a in p]
    sd = jax.ShapeDtypeStruct
    return pl.pallas_call(
        body, name=name, grid=(s // tm,),
        out_shape=(sd((s, d), F32), sd((s, d), F32)),
        in_specs=[_row_spec(tm, d), _full_spec(mod.shape), _full_spec(w_out.shape)]
        + [_row_spec(tm, a.shape[1]) for a in flat],
        out_specs=(_row_spec(tm, d), _row_spec(tm, d)),
        compiler_params=_params(("parallel",)),
    )(x, mod, w_out, *flat)


LOG2E = 1.4426950408889634
ONES_ROWS = 16


def _col_max8(s3):
    m8 = jnp.max(s3, axis=0)
    return jnp.broadcast_to(jnp.max(m8, axis=0, keepdims=True), m8.shape)


def _with_ones(vt, n):
    return jnp.concatenate([vt, jnp.ones((ONES_ROWS, n), vt.dtype)], axis=0)


def _grid_edges(grid):
    ids = [pl.program_id(a) for a in range(len(grid))]
    first = functools.reduce(jnp.logical_and, [i == 0 for i in ids])
    last = functools.reduce(jnp.logical_and, [i == n - 1 for i, n in zip(ids, grid)])
    return first, last


def flash_forward(q, k, vt, *, scale, dv, tq, tk, nsub, name, exchange=None):
    hq, s, dq = q.shape
    g_kv = k.shape[0]
    hpg = hq // g_kv
    nq = s // tq
    tkk = tk * nsub
    nk = s // tkk
    grid = (g_kv, nq, nk)
    hosted = exchange is not None
    m_cols = hpg * tq
    c = scale * LOG2E
    dvp = dv + ONES_ROWS

    def body(*refs):
        nx =exchange.n if hosted else 0
        q_ref, k_ref, vt_ref = refs[:3]
        xs_refs = refs[3:3 + nx]
        o_ref, lse_ref = refs[3 + nx:5 + nx]
        land_refs = refs[5 + nx:5 + 2 * nx]
        m_s, acc_s = refs[5 + 2 * nx:7 + 2 * nx]
        sems = refs[7 + 2 * nx:]
        if hosted:
            first, last = _grid_edges(grid)
            pl.when(first)(lambda: exchange.start(xs_refs, land_refs, sems))
        j = pl.program_id(2)

        @pl.when(j == 0)
        def _():
            m_s[...] = jnp.full((8, m_cols), -jnp.inf, F32)
            acc_s[...] = jnp.zeros((dvp, m_cols), F32)

        qq = q_ref[...].reshape(m_cols, dq)
        sts = [_mm_nt(k_ref[0, tk * u:tk * (u + 1), :], qq).reshape(tk // 8, 8, m_cols)
               for u in range(nsub)]
        m_run = m_s[...]
        acc = acc_s[...]
        for u in range(nsub):
            m_new = jnp.maximum(m_run, _col_max8(sts[u]) * c)
            p = jnp.exp2(sts[u] * c - m_new[None])
            alpha = jnp.exp2(m_run - m_new)
            pv = _mm(_with_ones(vt_ref[0, 0:dv, tk * u:tk * (u + 1)], tk), p.reshape(tk, m_cols))
            acc = (acc.reshape(dvp // 8, 8, m_cols) * alpha[None]).reshape(dvp, m_cols) + pv
            m_run = m_new
        acc_s[...] = acc
        m_s[...] = m_run

        @pl.when(j == nk - 1)
        def _():
            l = acc_s[dv:dv + 1, :]
            ot = acc_s[0:dv, :] / l
            lse = m_s[0:1, :] + jnp.log2(l)
            for hh in range(hpg):
                o_ref[:, dv * hh:dv * hh + dv] = ot[:, tq * hh:tq * hh + tq].T
                lse_ref[hh] = lse[:, tq * hh:tq * hh + tq]

        if hosted:
            pl.when(last)(lambda: exchange.wait(xs_refs, land_refs, sems))

    sd = jax.ShapeDtypeStruct
    return pl.pallas_call(
        body, name=name, grid=grid,
        out_shape=(sd((s, hq * dv), F32), sd((hq, 1, s), F32)) + (exchange.land_shapes if hosted else ()),
        in_specs=[pl.BlockSpec((hpg, tq, dq), lambda g, i, j: (g, i, 0)),
                  pl.BlockSpec((1, tkk, k.shape[2]), lambda g, i, j: (g, j, 0)),
                  pl.BlockSpec((1, dv, tkk), lambda g, i, j: (g, 0, j))] + (exchange.in_specs if hosted else []),
        out_specs=(pl.BlockSpec((tq, hpg * dv), lambda g, i, j: (i, g)),
                   pl.BlockSpec((hpg, 1, tq), lambda g, i, j: (g, 0, i))) + (exchange.out_specs if hosted else ()),
        scratch_shapes=[pltpu.VMEM((8, m_cols), F32), pltpu.VMEM((dvp, m_cols), F32)]
        + (list(exchange.sems) if hosted else []),
        compiler_params=_params(("arbitrary",) * 3 if hosted else ("parallel", "parallel", "arbitrary")),
    )(q, k, vt, *(exchange.srcs if hosted else []))


def _window_bias_t(hpg, slope_ref):
    t = WINDOW
    r = lax.broadcasted_iota(jnp.int32, (3 * t, t), 0)
    cq = lax.broadcasted_iota(jnp.int32, (3 * t, t), 1)
    arel = jnp.abs(r - t - cq)
    base = jnp.where(arel <= WINDOW, arel.astype(F32) * (-LOG2E), -jnp.inf)
    return jnp.concatenate([base * slope_ref[hh] for hh in range(hpg)], axis=1)


def _window_edges_t(bias, no_before, no_after):
    t = WINDOW
    r = lax.broadcasted_iota(jnp.int32, bias.shape, 0)
    out = ((r < t) & no_before) | ((r >= 2 * t) & no_after)
    return jnp.where(out, -jnp.inf, bias)


def _window_specs(kind, nb, nblk, d):
    t = WINDOW
    before = lambda i: jnp.clip(i * nb - 1, 0, nblk - 1)
    after = lambda i: jnp.clip((i + 1) * nb, 0, nblk - 1)
    if kind == "rows":
        return [pl.BlockSpec((1, t, d), lambda g, i: (g, before(i), 0)),
                pl.BlockSpec((1, nb * t, d), lambda g, i: (g, i, 0)),
                pl.BlockSpec((1, t, d), lambda g, i: (g, after(i), 0))]
    return [pl.BlockSpec((1, d, t), lambda g, i: (g, 0, before(i))),
            pl.BlockSpec((1, d, nb * t), lambda g, i: (g, 0, i)),
            pl.BlockSpec((1, d, t), lambda g, i: (g, 0, after(i)))]


def window_forward(q, k, vt, sink2, slopes, nb, name):
    hq, s, d = q.shape
    g_kv = k.shape[0]
    hpg = hq // g_kv
    t = WINDOW
    nblk = s // t
    steps = nblk // nb
    m_cols = hpg * t
    c = (d ** -0.5) * LOG2E

    def body(q_ref, kp, ko, kn, vp, vo, vn, sink_ref, slope_ref, o_ref, lse_ref):
        i = pl.program_id(1)
        kk_all = jnp.concatenate([kp[0], ko[0], kn[0]], axis=0)
        vt_all = jnp.concatenate([vp[0], vo[0], vn[0]], axis=1)
        bias = _window_bias_t(hpg, slope_ref)
        sink_row = jnp.concatenate([jnp.broadcast_to(sink_ref[hh], (8, t)) for hh in range(hpg)], axis=1)
        sts = []
        for u in range(nb):
            qq = q_ref[:, t * u:t * (u + 1), :].reshape(m_cols, d)
            b_u = bias
            if u == 0 or u == nb - 1:
                b_u = _window_edges_t(bias, (i == 0) if u == 0 else False,
                                      (i == steps - 1) if u == nb - 1 else False)
            sts.append(_mm_nt(kk_all[t * u:t * (u + 3), :], qq) * c + b_u)
        for u in range(nb):
            s3 = sts[u].reshape(3 * t // 8, 8, m_cols)
            m8 = jnp.maximum(_col_max8(s3), sink_row)
            p = jnp.exp2(s3 - m8[None]).reshape(3 * t, m_cols)
            acc = _mm(_with_ones(vt_all[:, t * u:t * (u + 3)], 3 * t), p)
            l = acc[d:d + 1, :] + jnp.exp2(sink_row[0:1, :] - m8[0:1, :])
            ot = acc[0:d, :] / l
            lse = m8[0:1, :] + jnp.log2(l)
            for hh in range(hpg):
                o_ref[t * u:t * (u + 1), d * hh:d * hh + d] = ot[:, t * hh:t * hh + t].T
                lse_ref[hh, :, t * u:t * (u + 1)] = lse[:, t * hh:t * hh + t]

    sd = jax.ShapeDtypeStruct
    return pl.pallas_call(
        body, name=name, grid=(g_kv, steps),
        out_shape=(sd((s, hq * d), F32), sd((hq, 1, s), F32)),
        in_specs=[pl.BlockSpec((hpg, nb * t, d), lambda g, i: (g, i, 0))]
        + _window_specs("rows", nb, nblk, d) + _window_specs("cols", nb, nblk, d)
        + [pl.BlockSpec((hpg, 1, 1), lambda g, i: (g, 0, 0))] * 2,
        out_specs=(pl.BlockSpec((nb * t, hpg * d), lambda g, i: (i, g)),
                   pl.BlockSpec((hpg, 1, nb * t), lambda g, i: (g, 0, i))),
        compiler_params=_params(("parallel", "parallel")),
    )(q, k, k, k, vt, vt, vt, sink2, slopes)


def window_backward(q, k, kt, v, do, lse, delta, slopes, nb, name):
    hq, s, d = q.shape
    g_kv = k.shape[0]
    hpg = hq // g_kv
    t = WINDOW
    nblk = s // t
    steps = nblk // nb
    m_cols = hpg * t
    scale = d ** -0.5
    c = scale * LOG2E

    def body(q_ref, kp, ko, kn, ktp, kto, ktn, vp, vo, vn, do_ref, lse_ref, dl_ref, slope_ref,
             dq_ref, dk_ref, dv_ref, dk_s, dv_s):
        i = pl.program_id(1)

        @pl.when(i == 0)
        def _():
            dk_ref[...] = jnp.zeros(dk_ref.shape, F32)
            dv_ref[...] = jnp.zeros(dv_ref.shape, F32)

        dk_s[...] = jnp.zeros(dk_s.shape, F32)
        dv_s[...] = jnp.zeros(dv_s.shape, F32)
        kk_all = jnp.concatenate([kp[0], ko[0], kn[0]], axis=0)
        vv_all = jnp.concatenate([vp[0], vo[0], vn[0]], axis=0)
        kkt_all = jnp.concatenate([ktp[0], kto[0], ktn[0]], axis=1)
        bias = _window_bias_t(hpg, slope_ref)
        qqs, dds, sts, dps = [], [], [], []
        for u in range(nb):
            rows = slice(t * u, t * (u + 1))
            keys = slice(t * u, t * (u + 3))
            qqs.append(q_ref[:, rows, :].reshape(m_cols, d))
            dds.append(jnp.concatenate([do_ref[rows, d * hh:d * hh + d] for hh in range(hpg)], axis=0))
            b_u = bias
            if u == 0 or u == nb - 1:
                b_u = _window_edges_t(bias, (i == 0) if u == 0 else False,
                                      (i == steps - 1) if u == nb - 1 else False)
            sts.append(_mm_nt(kk_all[keys, :], qqs[u]) * c + b_u)
            dps.append(_mm_nt(vv_all[keys, :], dds[u]))
        for u in range(nb):
            rows = slice(t * u, t * (u + 1))
            keys = slice(t * u, t * (u + 3))
            lse_row = jnp.concatenate([lse_ref[hh, :, rows] for hh in range(hpg)], axis=1)
            dl_row = jnp.concatenate([dl_ref[hh, :, rows] for hh in range(hpg)], axis=1)
            p = jnp.exp2(sts[u] - lse_row)
            ds = p * (dps[u] - dl_row) * scale
            dv_s[keys, :] += _mm(p, dds[u])
            dk_s[keys, :] += _mm(ds, qqs[u])
            dqt = _mm(kkt_all[:, keys], ds)
            for hh in range(hpg):
                dq_ref[rows, d * hh:d * hh + d] = dqt[:, t * hh:t * hh + t].T
        tq = nb * t
        for src, r0, n in ((0, jnp.clip(i * nb - 1, 0, nblk - 1) * t, t), (t, i * tq, tq),
                           (t + tq, jnp.clip((i + 1) * nb, 0, nblk - 1) * t, t)):
            dst = pl.ds(pl.multiple_of(r0, t), n)
            dk_ref[0, dst, :] += dk_s[src:src + n, :]
            dv_ref[0, dst, :] += dv_s[src:src + n, :]

    row_map = lambda g, i: (g, 0, i)
    sd = jax.ShapeDtypeStruct
    return pl.pallas_call(
        body, name=name, grid=(g_kv, steps),
        out_shape=(sd((s, hq * d), F32), sd((g_kv, s, d), F32), sd((g_kv, s, d), F32)),
        in_specs=[pl.BlockSpec((hpg, nb * t, d), lambda g, i: (g, i, 0))]
        + _window_specs("rows", nb, nblk, d) + _window_specs("cols", nb, nblk, d) + _window_specs("rows", nb, nblk, d)
        + [pl.BlockSpec((nb * t, hpg * d), lambda g, i: (i, g)), pl.BlockSpec((hpg, 1, nb * t), row_map),
           pl.BlockSpec((hpg, 1, nb * t), row_map), pl.BlockSpec((hpg, 1, 1), lambda g, i: (g, 0, 0))],
        out_specs=(pl.BlockSpec((nb * t, hpg * d), lambda g, i: (i, g)),
                   pl.BlockSpec((1, s, d), lambda g, i: (g, 0, 0)),
                   pl.BlockSpec((1, s, d), lambda g, i: (g, 0, 0))),
        scratch_shapes=[pltpu.VMEM(((nb + 2) * t, d), F32), pltpu.VMEM(((nb + 2) * t, d), F32)],
        compiler_params=_params(("parallel", "arbitrary")),
    )(q, k, k, k, kt, kt, kt, v, v, v, do, lse, delta, slopes)


def flash_backward(q, k, kt, v, do, lse, delta, *, scale, dv, tq, tk, nsub, gq, name, split=None, exchange=None):
    hq, s, dq = q.shape
    g_kv = k.shape[0]
    hpg = hq // gq
    nq = s // tq
    tqq = tq * nsub
    nqs = s // tqq
    nkb = s // tk
    grid = (gq, nkb, nqs)
    hosted = exchange is not None
    m_cols = hpg * tq
    c = scale * LOG2E
    has_v = v is not None

    def body(*refs):
        it = iter(refs)
        q_ref, k_ref, kt_ref = next(it), next(it), next(it)
        v_ref = next(it) if has_v else None
        do_ref, lse_ref, dl_ref = next(it), next(it), next(it)
        nx = exchange.n if hosted else 0
        xs_refs = [next(it) for _ in range(nx)]
        dq_ref, dk_ref, dv_ref = next(it), next(it), next(it)
        land_refs = [next(it) for _ in range(nx)]
        dqt_s = next(it)
        sems = list(it)
        kj = pl.program_id(1)
        qi = pl.program_id(2)
        if hosted:
            first, last = _grid_edges(grid)
            pl.when(first)(lambda: exchange.start(xs_refs, land_refs, sems))

        @pl.when((kj == 0) & (qi == 0))
        def _():
            dqt_s[...] = jnp.zeros(dqt_s.shape, F32)

        @pl.when(qi == 0)
        def _():
            dk_ref[...] = jnp.zeros(dk_ref.shape, F32)
            dv_ref[...] = jnp.zeros(dv_ref.shape, F32)

        kk = k_ref[0]
        vv = v_ref[0] if has_v else kk[:, :dv]
        qqs, dds, sts, dps = [], [], [], []
        for u in range(nsub):
            rows = slice(tq * u, tq * (u + 1))
            qqs.append(q_ref[:, rows, :].reshape(m_cols, dq))
            dds.append(jnp.concatenate([do_ref[rows, dv * hh:dv * hh + dv] for hh in range(hpg)], axis=0))
            sts.append(_mm_nt(kk, qqs[u]))
            dps.append(_mm_nt(vv, dds[u]))
        dv_acc = dv_ref[0]
        dk_acc = dk_ref[0]
        for u in range(nsub):
            rows = slice(tq * u, tq * (u + 1))
            lse_row = jnp.concatenate([lse_ref[hh, :, rows] for hh in range(hpg)], axis=1)
            dl_row = jnp.concatenate([dl_ref[hh, :, rows] for hh in range(hpg)], axis=1)
            p = jnp.exp2(sts[u] * c - lse_row)
            ds = p * (dps[u] - dl_row) * scale
            dv_acc = dv_acc + _mm(p, dds[u])
            dk_acc = dk_acc + _mm(ds, qqs[u])
            dqt = _mm(kt_ref[0], ds)
            for hh in range(hpg):
                dqt_s[qi * nsub + u, dq * hh:dq * hh + dq, :] += dqt[:, tq * hh:tq * hh + tq]
        dv_ref[0] = dv_acc
        dk_ref[0] = dk_acc

        @pl.when((kj == nkb - 1) & (qi == nqs - 1))
        def _():
            def emit(t, carry):
                r0 = pl.multiple_of(t * tq, tq)
                for hh in range(hpg):
                    blk = dqt_s[t, dq * hh:dq * hh + dq, :].T
                    if split is None:
                        dq_ref[pl.ds(r0, tq), dq * hh:dq * hh + dq] = blk
                    else:
                        rest = dq - split
                        dq_ref[pl.ds(r0, tq), split * hh:split * (hh + 1)] = blk[:, 0:split]
                        dq_ref[pl.ds(r0, tq), hpg * split + rest * hh:hpg * split + rest * (hh + 1)] = blk[:, split:]
                return carry

            lax.fori_loop(0, nq, emit, 0)

        if hosted:
            pl.when(last)(lambda: exchange.wait(xs_refs, land_refs, sems))

    kv_of = lambda g: g * g_kv // gq
    in_specs = [pl.BlockSpec((hpg, tqq, dq), lambda g, kj, qi: (g, qi, 0)),
                pl.BlockSpec((1, tk, dq), lambda g, kj, qi: (kv_of(g), kj, 0)),
                pl.BlockSpec((1, dq, tk), lambda g, kj, qi: (kv_of(g), 0, kj))]
    args = [q, k, kt]
    if has_v:
        in_specs.append(pl.BlockSpec((1, tk, dv), lambda g, kj, qi: (kv_of(g), kj, 0)))
        args.append(v)
    row_map = lambda g, kj, qi: (g, 0, qi)
    in_specs += [pl.BlockSpec((tqq, hpg * dv), lambda g, kj, qi: (qi, g)),
                 pl.BlockSpec((hpg, 1, tqq), row_map), pl.BlockSpec((hpg, 1, tqq), row_map)]
    args += [do, lse, delta]
    if hosted:
        in_specs += exchange.in_specs
        args += exchange.srcs
    sd = jax.ShapeDtypeStruct
    return pl.pallas_call(
        body, name=name, grid=grid,
        out_shape=(sd((s, hq * dq), F32), sd((gq, s, dq), F32), sd((gq, s, dv), F32))
        + (exchange.land_shapes if hosted else ()),
        in_specs=in_specs,
        out_specs=(pl.BlockSpec((s, hpg * dq), lambda g, kj, qi: (0, g)),
                   pl.BlockSpec((1, tk, dq), lambda g, kj, qi: (g, kj, 0)),
                   pl.BlockSpec((1, tk, dv), lambda g, kj, qi: (g, kj, 0))) + (exchange.out_specs if hosted else ()),
        scratch_shapes=[pltpu.VMEM((nq, hpg * dq, tq), F32)] + (list(exchange.sems) if hosted else []),
        compiler_params=_params(("arbitrary",) * 3 if hosted else ("parallel", "arbitrary", "arbitrary")),
    )(*args)


def loss_head(x, target, fnw):
    s, d = x.shape
    tm = min(ROW_TILE, s)

    def body(x_ref, t_ref, w_ref, lp_ref, dx_ref, dw_ref):
        @pl.when(pl.program_id(0) == 0)
        def _():
            lp_ref[...] = jnp.zeros(lp_ref.shape, F32)
            dw_ref[...] = jnp.zeros(dw_ref.shape, F32)

        x = x_ref[...]
        g = w_ref[...]
        err = x * _rms(x) * g - t_ref[...]
        lp_ref[...] += jnp.sum(err * err, axis=0, keepdims=True)
        dx, dg = _rms_bwd(err * (1.0 / d), x, g)
        dx_ref[...] = dx
        dw_ref[...] += jnp.sum(dg, axis=0, keepdims=True)

    sd = jax.ShapeDtypeStruct
    return pl.pallas_call(
        body, name="loss_head", grid=(s // tm,),
        out_shape=(sd((1, d), F32), sd((s, d), F32), sd((1, d), F32)),
        in_specs=[_row_spec(tm, d), _row_spec(tm, d), _full_spec(fnw.shape)],
        out_specs=(_full_spec((1, d)), _row_spec(tm, d), _full_spec((1, d))),
        compiler_params=_params(("arbitrary",)),
    )(x, target, fnw)


def mixer_out_backward(dx, y, mod, pairs, w_out, delta_heads, name, lse=None, sink=None):
    s, d = dx.shape
    tm = min(ROW_TILE, s)
    n = len(pairs)
    widths = [o.shape[1] for o, _ in pairs]
    n_delta = sum(1 for h in delta_heads if h)
    with_sink = lse is not None

    def body(*refs):
        it = iter(refs)
        dx_ref, y_ref, mod_ref, wt_ref = next(it), next(it), next(it), next(it)
        pr = [next(it) for _ in range(2 * n)]
        lse_ref = next(it) if with_sink else None
        sink_ref = next(it) if with_sink else None
        outs = [next(it) for _ in range(2 * n)]
        dl_refs = [next(it) for _ in range(n_delta)]
        dgate_ref, dw_ref = next(it), next(it)
        dsink_ref = next(it) if with_sink else None

        @pl.when(pl.program_id(0) == 0)
        def _():
            dgate_ref[...] = jnp.zeros(dgate_ref.shape, F32)
            dw_ref[...] = jnp.zeros(dw_ref.shape, F32)
            if with_sink:
                dsink_ref[...] = jnp.zeros(dsink_ref.shape, F32)

        dxo = dx_ref[...]
        dgate_ref[...] += jnp.sum(dxo * y_ref[...], axis=0, keepdims=True)
        dy = (dxo * mod_ref[2:3, :]).astype(MXU)
        dmix = _mm_nt(dy, wt_ref[...])
        r0 = 0
        di = 0
        for i in range(n):
            o = pr[2 * i][...]
            g = pr[2 * i + 1][...]
            dm = dmix[:, r0:r0 + widths[i]]
            sg = _sigmoid(g)
            act = g * sg
            do = dm * act
            outs[2 * i][...] = do.astype(MXU)
            outs[2 * i + 1][...] = (dm * o * (sg * (1.0 + g * (1.0 - sg)))).astype(MXU)
            dw_ref[r0:r0 + widths[i], :] += _mm_tn(o * act, dy)
            if delta_heads[i]:
                dlt = _group_sums_t(do * o, HD)[0:delta_heads[i], :]
                dl_refs[di][...] = dlt
                if with_sink:
                    ps = jnp.exp2(sink_ref[...] - lse_ref[...])
                    dsink_ref[...] += -jnp.sum(ps * dlt, axis=1, keepdims=True)
                di += 1
            r0 += widths[i]

    flat = [a for p in pairs for a in p]
    sd = jax.ShapeDtypeStruct
    in_specs = [_row_spec(tm, d), _row_spec(tm, d), _full_spec(mod.shape), _full_spec(w_out.shape)]
    in_specs += [_row_spec(tm, a.shape[1]) for a in flat]
    args = [dx, y, mod, w_out] + flat
    if with_sink:
        nh = lse.shape[0]
        in_specs += [_rows_spec(nh, tm), _full_spec(sink.shape)]
        args += [lse, sink]
    out_shape = [sd((s, a.shape[1]), MXU) for a in flat]
    out_specs = [_row_spec(tm, a.shape[1]) for a in flat]
    for h in delta_heads:
        if h:
            out_shape.append(sd((h, s), F32))
            out_specs.append(_rows_spec(h, tm))
    out_shape += [sd((1, d), F32), sd((sum(widths), d), F32)]
    out_specs += [_full_spec((1, d)), _full_spec((sum(widths), d))]
    if with_sink:
        out_shape.append(sd((lse.shape[0], 1), F32))
        out_specs.append(_full_spec((lse.shape[0], 1)))
    return pl.pallas_call(
        body, name=name, grid=(s // tm,), out_shape=tuple(out_shape), in_specs=in_specs, out_specs=tuple(out_specs),
        compiler_params=_params(("arbitrary",)),
    )(*args)


def latent_out_backward(d_ob, o_lat, w_uv):
    s = o_lat.shape[0]
    tm = min(ROW_TILE, s)

    def body(d_ref, o_ref, uv_ref, dol_ref, dl_ref, duv_ref, prod_s):
        @pl.when(pl.program_id(0) == 0)
        def _():
            duv_ref[...] = jnp.zeros(duv_ref.shape, F32)

        for hh in range(B_HEADS):
            dh = d_ref[:, HD * hh:HD * hh + HD]
            ol = o_ref[:, B_KV_LORA * hh:B_KV_LORA * (hh + 1)]
            dol = _mm_nt(dh, uv_ref[hh])
            dol_ref[:, B_KV_LORA * hh:B_KV_LORA * (hh + 1)] = dol.astype(MXU)
            prod_s[:, B_KV_LORA * hh:B_KV_LORA * (hh + 1)] = dol * ol
            duv_ref[hh] += _mm_tn(ol, dh)
        dl_ref[...] = _group_sums_t(prod_s[...], B_KV_LORA)[0:B_HEADS, :]

    sd = jax.ShapeDtypeStruct
    return pl.pallas_call(
        body, name="latent_out_backward", grid=(s // tm,),
        out_shape=(sd(o_lat.shape, MXU), sd((B_HEADS, s), F32), sd(w_uv.shape, F32)),
        in_specs=[_row_spec(tm, d_ob.shape[1]), _row_spec(tm, o_lat.shape[1]), _full_spec(w_uv.shape)],
        out_specs=(_row_spec(tm, o_lat.shape[1]), _rows_spec(B_HEADS, tm), _full_spec(w_uv.shape)),
        scratch_shapes=[pltpu.VMEM((tm, o_lat.shape[1]), F32)],
        compiler_params=_params(("arbitrary",)),
    )(d_ob, o_lat, w_uv)


def even_prep_backward(dqa, dka, dva, dqb, dkb, dvb, qa_raw, ka_raw, cq_raw, ckv_raw,
                       gq, gk, qln, kvln, w_uq_t, uk_bd, bd, cos_a, sin_a, cos_t, sin_t):
    s = qa_raw.shape[0]
    tm = min(ROW_TILE, s)
    half_lat = B_KV_LORA * B_HEADS // 2
    half_w = dqb.shape[1] // 2

    def body(dqa_ref, dka_ref, dva_ref, dqb_ref, dkb_ref, dvb_ref, qa_ref, ka_ref, cq_ref, ckv_ref,
             gq_ref, gk_ref, qln_ref, kvln_ref, uqt_ref, ukbd_ref, bd_ref, ca_ref, sa_ref, ct_ref, st_ref,
             pqa, pka, pva, pcq, pckv, pkr, gqn, gkn, gqln, gkvln, guq, guk):
        @pl.when(pl.program_id(0) == 0)
        def _():
            for r in (gqn, gkn, gqln, gkvln, guq, guk):
                r[...] = jnp.zeros(r.shape, F32)

        ca, sa, ct, st = ca_ref[...], sa_ref[...], ct_ref[...], st_ref[...]
        wide = lambda t, n: jnp.concatenate([t] * n, axis=1)
        rows = lambda a: jnp.sum(a, axis=0, keepdims=True)
        dx, dg = _head_norm_bwd(_rope_t(dqa_ref[...], wide(ca, 4), wide(sa, 4), 32), qa_ref[...], gq_ref[...],
                                bd_ref, HD)
        pqa[...] = dx.astype(MXU)
        gqn[...] += rows(dg)
        dk_all = jnp.concatenate([dka_ref[g] for g in range(A_KV)], axis=1)
        dx, dg = _head_norm_bwd(_rope_t(dk_all, ca, sa, 32), ka_ref[...], gk_ref[...], bd_ref[0:128, 0:128], HD)
        pka[...] = dx.astype(MXU)
        gkn[...] += rows(dg)
        pva[...] = jnp.concatenate([dva_ref[g] for g in range(A_KV)], axis=1).astype(MXU)
        cq_raw = cq_ref[...]
        cq_n = cq_raw * _rms(cq_raw) * qln_ref[...]
        qb = _mm_nt(cq_n, uqt_ref[...])
        d_lat = jnp.concatenate([dqb_ref[:, 0:half_lat], dqb_ref[:, half_w:half_w + half_lat]], axis=1)
        d_rope = jnp.concatenate([dqb_ref[:, half_lat:half_w], dqb_ref[:, half_w + half_lat:]], axis=1)
        for hh in range(B_HEADS):
            guk[hh] += _mm_tn(d_lat[:, B_KV_LORA * hh:B_KV_LORA * (hh + 1)], qb[:, B_NOPE * hh:B_NOPE * (hh + 1)])
        dqb_all = jnp.concatenate([_mm_nt(d_lat, ukbd_ref[...]),
                                   _rope_t(d_rope, wide(ct, 2), wide(st, 2), 32)], axis=1)
        guq[...] += _mm_tn(dqb_all, cq_n)
        dx, dg = _rms_bwd(_mm(dqb_all, uqt_ref[...]), cq_raw, qln_ref[...])
        pcq[...] = dx.astype(MXU)
        gqln[...] += rows(dg)
        dkb_sum = dkb_ref[0] + dkb_ref[1]
        dckv = dkb_sum[:, 0:B_KV_LORA] + dvb_ref[0] + dvb_ref[1]
        dx, dg = _rms_bwd(dckv, ckv_ref[...], kvln_ref[...])
        pckv[...] = dx.astype(MXU)
        gkvln[...] += rows(dg)
        pkr[...] = _rope_t(dkb_sum[:, B_KV_LORA:B_QK], ct[:, 0:B_ROPE], st[:, 0:B_ROPE], 32).astype(MXU)

    sd = jax.ShapeDtypeStruct
    consts = [gq, gk, qln, kvln, w_uq_t, uk_bd, bd]
    in_specs = [_row_spec(tm, 512), _head_spec(A_KV, tm, HD), _head_spec(A_KV, tm, HD),
                _row_spec(tm, dqb.shape[1]), _head_spec(2, tm, B_QK), _head_spec(2, tm, B_KV_LORA),
                _row_spec(tm, 512), _row_spec(tm, 128), _row_spec(tm, B_Q_LORA), _row_spec(tm, B_KV_LORA)]
    in_specs += [_full_spec(a.shape) for a in consts] + [_row_spec(tm, 128)] * 4
    small = [sd(gq.shape, F32), sd(gk.shape, F32), sd(qln.shape, F32), sd(kvln.shape, F32), sd(w_uq_t.shape, F32),
             sd((B_HEADS, B_KV_LORA, B_NOPE), F32)]
    out_shape = (sd((s, 512), MXU), sd((s, 128), MXU), sd((s, 128), MXU), sd((s, B_Q_LORA), MXU),
                 sd((s, B_KV_LORA), MXU), sd((s, B_ROPE), MXU), *small)
    out_specs = (_row_spec(tm, 512), _row_spec(tm, 128), _row_spec(tm, 128), _row_spec(tm, B_Q_LORA),
                 _row_spec(tm, B_KV_LORA), _row_spec(tm, B_ROPE), *[_full_spec(a.shape) for a in small])
    return pl.pallas_call(
        body, name="even_prep_backward", grid=(s // tm,), out_shape=out_shape, in_specs=in_specs, out_specs=out_specs,
        compiler_params=_params(("arbitrary",)),
    )(dqa, dka, dva, dqb, dkb, dvb, qa_raw, ka_raw, cq_raw, ckv_raw, *consts, cos_a, sin_a, cos_t, sin_t)


def in_proj_backward(x, mod, nw, pieces, name, *, dx_out=None, w_in_t=None, dw_rows=None, exchange=None):
    s, d = x.shape
    tm = min(ROW_TILE, s)
    grid = (s // tm,)
    n = len(pieces)
    cols = [c for _, c in pieces]
    want_dx = w_in_t is not None
    want_dw = dw_rows is not None
    n_cols = sum(c1 - c0 for c0, c1 in cols)
    hosted = exchange is not None
    nx = exchange.n if hosted else 0

    def body(*refs):
        it = iter(refs)
        x_ref, mod_ref, nw_ref = next(it), next(it), next(it)
        dxo_ref, wt_ref = (next(it), next(it)) if want_dx else (None, None)
        p_refs = [next(it) for _ in range(n)]
        xs_refs = [next(it) for _ in range(nx)]
        dx_ref, dv_ref = (next(it), next(it)) if want_dx else (None, None)
        dw_ref = next(it) if want_dw else None
        land_refs = [next(it) for _ in range(nx)]
        acc_ref = next(it) if want_dx else None
        dw_acc = next(it) if want_dw else None
        sems = list(it)
        first, last = _grid_edges(grid)
        if hosted:
            pl.when(first)(lambda: exchange.start(xs_refs, land_refs, sems))

        @pl.when(first)
        def _():
            if want_dw:
                dw_acc[...] = jnp.zeros(dw_acc.shape, F32)
            if want_dx:
                acc_ref[...] = jnp.zeros(acc_ref.shape, F32)

        xn, g1, h = _modulated(x_ref[...], mod_ref, nw_ref)
        hb = h.astype(MXU)
        dh = jnp.zeros((tm, d), F32)
        for k, (pr, (c0, c1)) in enumerate(zip(p_refs, cols)):
            pc = pr[...].astype(MXU)
            if want_dx:
                dh = dh + jnp.dot(pc, wt_ref[c0:c1, :], preferred_element_type=F32)
            if want_dw:
                r0, r1 = dw_rows[k]
                dw_acc[r0:r1, :] += _mm_tn(pc, hb)
        if want_dx:
            acc_ref[0:1, :] += jnp.sum(dh, axis=0, keepdims=True)
            acc_ref[1:2, :] += jnp.sum(dh * xn, axis=0, keepdims=True)
            dxn = dh * g1
            x = x_ref[...]
            dx_ref[...] = dxo_ref[...] + _rms(x) * (dxn - xn * jnp.mean(dxn * xn, axis=-1, keepdims=True))

        @pl.when(last)
        def _():
            if want_dx:
                dg1 = acc_ref[1:2, :]
                dv_ref[0:1, :] = acc_ref[0:1, :]
                dv_ref[1:2, :] = dg1 * nw_ref[...]
                dv_ref[2:3, :] = dg1 * (1.0 + mod_ref[1:2, :])
                dv_ref[3:4, :] = jnp.zeros((1, d), F32)
            if want_dw:
                dw_ref[...] = dw_acc[...].astype(MXU)

        if hosted:
            pl.when(last)(lambda: exchange.wait(xs_refs, land_refs, sems))

    arrs = [a for a, _ in pieces]
    sd = jax.ShapeDtypeStruct
    args = [x, mod, nw] + ([dx_out, w_in_t] if want_dx else []) + arrs + (exchange.srcs if hosted else [])
    in_specs = [_row_spec(tm, d), _full_spec(mod.shape), _full_spec(nw.shape)]
    in_specs += [_row_spec(tm, d), _full_spec(w_in_t.shape)] if want_dx else []
    in_specs += [_row_spec(tm, a.shape[1]) for a in arrs] + (exchange.in_specs if hosted else [])
    out_shape, out_specs, scratch = [], [], []
    if want_dx:
        out_shape += [sd((s, d), F32), sd((4, d), F32)]
        out_specs += [_row_spec(tm, d), _full_spec((4, d))]
        scratch.append(pltpu.VMEM((8, d), F32))
    if want_dw:
        out_shape.append(sd((n_cols, d), MXU))
        out_specs.append(_full_spec((n_cols, d)))
        scratch.append(pltpu.VMEM((n_cols, d), F32))
    if hosted:
        out_shape += list(exchange.land_shapes)
        out_specs += list(exchange.out_specs)
        scratch += list(exchange.sems)
    return pl.pallas_call(
        body, name=name, grid=grid, out_shape=tuple(out_shape), in_specs=in_specs, out_specs=tuple(out_specs),
        scratch_shapes=scratch, compiler_params=_params(("arbitrary",)),
    )(*args)


def ada_weight_grad(c_all, dmod_cols):
    d = c_all.shape[1]
    w = dmod_cols.shape[2]

    def body(c_ref, dm_ref, out_ref):
        ca = _silu(c_ref[...])
        for l in range(2):
            out_ref[l] = _mm_tn(ca, dm_ref[l])

    return pl.pallas_call(
        body, name="ada_weight_grad",
        out_shape=jax.ShapeDtypeStruct((2, d, w), F32),
        compiler_params=pltpu.CompilerParams(vmem_limit_bytes=VMEM_LIMIT),
    )(c_all, dmod_cols)


def _slot_sum(g_ref):
    g = g_ref[0].astype(F32)
    for k in range(1, g_ref.shape[0]):
        g = g + g_ref[k].astype(F32)
    return g


def _adamw_math(g, w, m, v):
    m_new = ADAM_B1 * m + (1.0 - ADAM_B1) * g
    v_new = ADAM_B2 * v + (1.0 - ADAM_B2) * (g * g)
    m_hat = m_new / (1.0 - ADAM_B1 ** ADAM_STEP)
    v_hat = v_new / (1.0 - ADAM_B2 ** ADAM_STEP)
    return -ADAM_LR * (m_hat / (jnp.sqrt(v_hat) + ADAM_EPS) + ADAM_WD * w), m_new, v_new


def adamw_small(g_alls, ws, ms, vs, loss_all):
    n = len(ws)

    def body(*refs):
        g_refs, w_refs, m_refs, v_refs = (refs[i * n:(i + 1) * n] for i in range(4))
        loss_ref = refs[4 * n]
        outs = refs[4 * n + 1:]
        for i in range(n):
            g = _slot_sum(g_refs[i])
            outs[i][...] = g
            outs[n + i][...], outs[2 * n + i][...], outs[3 * n + i][...] = _adamw_math(
                g, w_refs[i][...], m_refs[i][...], v_refs[i][...])
        outs[4 * n][...] = _slot_sum(loss_ref)

    sds = [jax.ShapeDtypeStruct(w.shape, F32) for w in ws]
    res = pl.pallas_call(
        body, name="adamw_small", out_shape=tuple(sds * 4) + (jax.ShapeDtypeStruct(loss_all.shape[1:], F32),),
        compiler_params=pltpu.CompilerParams(vmem_limit_bytes=VMEM_LIMIT),
    )(*g_alls, *ws, *ms, *vs, loss_all)
    return [res[i * n:(i + 1) * n] for i in range(4)], res[4 * n]


def adamw_rows(g_slots, w, m, v, name):
    n, r, lanes = g_slots.shape
    fits = [t for t in range(16, r + 1, 16) if r % t == 0 and t * lanes <= ADAM_TILE]
    tr = max(fits) if fits else r
    def body(g_ref, w_ref, m_ref, v_ref, go, do, mo, vo):
        g = _slot_sum(g_ref)
        go[...] = g
        do[...], mo[...], vo[...] = _adamw_math(g, w_ref[...], m_ref[...], v_ref[...])

    row = pl.BlockSpec((tr, lanes), lambda i: (i, 0))
    sd = jax.ShapeDtypeStruct((r, lanes), F32)
    return pl.pallas_call(
        body, name=name, grid=(r // tr,), out_shape=(sd, sd, sd, sd),
        in_specs=[pl.BlockSpec((n, tr, lanes), lambda i: (0, i, 0)), row, row, row],
        out_specs=(row, row, row, row),
        compiler_params=_params(("parallel",)),
    )(g_slots, w, m, v)


def _rope_tables(s):
    def cs(pos, dim):
        inv = ROPE_THETA ** (-np.arange(0, dim, 2, dtype=np.float32) / dim)
        ang = pos.astype(np.float32)[:, None] * inv.astype(np.float32)[None, :]
        return np.cos(ang), np.sin(ang)

    rows = s // GRID_W
    row = np.repeat(np.arange(rows), GRID_W)
    col = np.tile(np.arange(GRID_W), rows)
    cr, sr = cs(row, HD // 2)
    cc, sc = cs(col, HD // 2)
    ct, st = cs(np.arange(s), B_ROPE)
    tables = (np.concatenate([cr, cr, cc, cc] * 2, axis=-1), np.concatenate([-sr, sr, -sc, sc] * 2, axis=-1),
              np.concatenate([ct, ct] * 4, axis=-1), np.concatenate([-st, st] * 4, axis=-1))
    return tuple(jnp.asarray(t, F32) for t in tables)


def _even_rows_to_kernel(wt):
    return jnp.concatenate([wt[:1664], wt[1696:], wt[1664:1696]], axis=0)


def _uq_rows_to_kernel(wt):
    r = wt.reshape(B_HEADS, B_NOPE + B_ROPE, -1)
    return jnp.concatenate([r[:, :B_NOPE].reshape(B_HEADS * B_NOPE, -1), r[:, B_NOPE:].reshape(B_HEADS * B_ROPE, -1)])


def _uq_rows_to_reference(wt):
    nope = wt[:B_HEADS * B_NOPE].reshape(B_HEADS, B_NOPE, -1)
    rope = wt[B_HEADS * B_NOPE:].reshape(B_HEADS, B_ROPE, -1)
    return jnp.concatenate([nope, rope], axis=1).reshape(B_HEADS * (B_NOPE + B_ROPE), -1)


def _shard_t(w):
    return jnp.transpose(w[0])


def _unshard_t(wt, like):
    return jnp.transpose(wt)[None].reshape(like.shape)


def kernel(x, c, norm_w, ada_w, ada_b, even_w_in, a_q_norm, a_k_norm, b_q_lora_norm, b_kv_lora_norm, b_w_uq, b_w_uk, b_w_uv, even_w_out, odd_w_in, c_sink, odd_w_out, final_norm, loss_target, m_norm_w, m_ada_w, m_ada_b, m_even_w_in, m_a_q_norm, m_a_k_norm, m_b_q_lora_norm, m_b_kv_lora_norm, m_b_w_uq, m_b_w_uk, m_b_w_uv, m_even_w_out, m_odd_w_in, m_c_sink, m_odd_w_out, m_final_norm, v_norm_w, v_ada_w, v_ada_b, v_even_w_in, v_a_q_norm, v_a_k_norm, v_b_q_lora_norm, v_b_kv_lora_norm, v_b_w_uq, v_b_w_uk, v_b_w_uv, v_even_w_out, v_odd_w_in, v_c_sink, v_odd_w_out, v_final_norm):
    s, d = x.shape[1], x.shape[2]
    x0 = x[0]
    target = loss_target[0]
    me_flat = 4 * lax.axis_index("x") + 2 * lax.axis_index("y") + lax.axis_index("c")

    wcols = ada_w.shape[2]
    bias_cols = lax.dynamic_slice_in_dim(ada_b.reshape(2, N_DEV, wcols), me_flat, 1, axis=1)
    call, modp, (g_in_e, g_uq) = ada_forward(
        jnp.broadcast_to(c, (8, d)), ada_w, bias_cols,
        Gather([_shard_t(even_w_in).astype(MXU), _shard_t(b_w_uq).astype(MXU)]))
    wt_in_e = _even_rows_to_kernel(g_in_e.reshape(-1, d))
    wt_uq = _uq_rows_to_kernel(g_uq.reshape(-1, B_Q_LORA))
    later_exchange = Exchange([_shard_t(odd_w_in).astype(MXU), even_w_out[0].astype(MXU),
                               odd_w_out[0].astype(MXU)], scatter=False)
    uk_bd = (jnp.eye(B_HEADS, dtype=F32)[:, None, :, None] * jnp.transpose(b_w_uk[0], (1, 2, 0))[:, :, None, :]
             ).reshape(B_HEADS * B_NOPE, B_HEADS * B_KV_LORA).astype(MXU)
    head_bd = jnp.asarray(np.kron(np.eye(A_HEADS), np.ones((HD, HD))), MXU)
    gq_full, gk_full = jnp.tile(a_q_norm, (1, A_HEADS)), jnp.tile(a_k_norm, (1, A_KV))
    w_uv = jnp.transpose(b_w_uv[0], (1, 0, 2)).astype(MXU)

    c_all = call[:, 0, :]
    mod = jnp.transpose(modp[:, :, 0, :], (1, 0, 2)).reshape(2, 3, d)
    mod_e, mod_o = mod[0], mod[1]
    nw_e, nw_o = norm_w[0:1], norm_w[1:2]

    cos_a, sin_a, cos_t, sin_t = _rope_tables(s)
    slopes = (2.0 ** (-8.0 * jnp.arange(1, C_HEADS + 1, dtype=F32) / C_HEADS)).reshape(C_HEADS, 1, 1)
    sink2 = c_sink.reshape(C_HEADS, 1, 1) * LOG2E

    (qa, ka, va, qb, kb, kat, vat, kbt, qa_raw, ka_raw, cq_raw, ckv_raw, ga, gb) = even_in_forward(
        x0, mod_e, nw_e, wt_in_e, gq_full, gk_full, b_q_lora_norm, b_kv_lora_norm, wt_uq, uk_bd, head_bd,
        cos_a, sin_a, cos_t, sin_t)
    tk_dense = min(512, s)
    tq_dense = min(256, s)
    fwd_sub = min(8, s // tk_dense)
    bwd_sub = min(4, s // tq_dense)
    oa, lse_a, g_in_o, g_out_e, g_out_o = flash_forward(
        qa, ka, vat, scale=HD ** -0.5, dv=HD, tq=tq_dense, tk=tk_dense, nsub=fwd_sub, name="attn_a_fwd",
        exchange=later_exchange)
    wt_in_o = g_in_o.reshape(-1, d)
    w_out_e = g_out_e.reshape(-1, d)
    w_out_o = g_out_o.reshape(-1, d)
    scale_b = (B_NOPE + B_ROPE) ** -0.5
    o_lat, lse_b = flash_forward(qb, kb, kbt, scale=scale_b, dv=B_KV_LORA, tq=min(128, s), tk=tk_dense, nsub=fwd_sub,
                                 name="attn_b_fwd")
    ob = latent_out_forward(o_lat, w_uv)
    x1, y_e = mixer_out_forward(x0, mod_e, [(oa, ga), (ob, gb)], w_out_e, "even_out_fwd")

    qc, kc, vc, kct, vct, gc = odd_in_forward(x1, mod_o, nw_o, wt_in_o)
    win_sub = min(8, s // WINDOW)
    oc, lse_c = window_forward(qc, kc, vct, sink2, slopes, win_sub, "attn_c_fwd")
    x2, y_o = mixer_out_forward(x1, mod_o, [(oc, gc)], w_out_o, "odd_out_fwd")

    loss_lanes, dx2, d_final = loss_head(x2, target, final_norm.reshape(1, d))
    loss_part = (0.5 / d) * jnp.sum(loss_lanes)

    doc, dgc, delta_c, dgate_o, dw_out_o, dsink = mixer_out_backward(
        dx2, y_o, mod_o, [(oc, gc)], w_out_o, [C_HEADS], "odd_out_bwd", lse=lse_c.reshape(C_HEADS, s),
        sink=sink2.reshape(C_HEADS, 1))
    rows3 = lambda t: t.reshape(t.shape[0], 1, s)
    dqc, dkc, dvc = window_backward(qc, kc, kct, vc, doc, lse_c, rows3(delta_c), slopes, win_sub, "attn_c_bwd")
    to_rows = lambda t: jnp.transpose(t, (1, 0, 2)).reshape(s, -1)
    dx1, dvec_o, dwt_in_o = in_proj_backward(
        x1, mod_o, nw_o, [(dqc, O_Q), (to_rows(dkc), O_K), (to_rows(dvc), O_V), (dgc, O_G)], "odd_in_bwd",
        dx_out=dx2, w_in_t=wt_in_o, dw_rows=[O_Q, O_K, O_V, O_G])

    doa, dga, dob, dgb, delta_a, dgate_e, dw_out_e = mixer_out_backward(
        dx1, y_e, mod_e, [(oa, ga), (ob, gb)], w_out_e, [A_HEADS, 0], "even_out_bwd")
    d_olat, delta_b, dw_uv = latent_out_backward(dob, o_lat, w_uv)
    blocks = lambda g: g.astype(MXU).reshape(N_DEV, g.shape[0] // N_DEV, g.shape[1])
    even_pieces = lambda: [(pqa, E_QA), (pka, E_KA), (pva, E_VA), (dga, E_GA), (pcq, E_CQ), (pckv, E_CKV),
                           (dgb, E_GB), (pkr, E_KR)]
    scatter_odd = Exchange([blocks(dwt_in_o), blocks(dw_out_o)], True)
    scatter_out_e = Exchange([blocks(dw_out_e)], True)
    dqb, dkb, dvb, l_in_o, l_out_o = flash_backward(
        qb, kb, kbt, None, d_olat, lse_b, rows3(delta_b), scale=scale_b, dv=B_KV_LORA,
        tq=tq_dense, tk=tk_dense, nsub=bwd_sub, gq=2, name="attn_b_bwd", split=B_KV_LORA, exchange=scatter_odd)
    dqa, dka, dva, l_out_e = flash_backward(
        qa, ka, kat, va, doa, lse_a, rows3(delta_a), scale=HD ** -0.5, dv=HD,
        tq=tq_dense, tk=tk_dense, nsub=bwd_sub, gq=A_KV, name="attn_a_bwd", exchange=scatter_out_e)
    (pqa, pka, pva, pcq, pckv, pkr, g_qn, g_kn, g_qln, g_kvln, dwt_uq, dw_uk) = even_prep_backward(
        dqa, dka, dva, dqb, dkb, dvb, qa_raw, ka_raw, cq_raw, ckv_raw,
        gq_full, gk_full, b_q_lora_norm, b_kv_lora_norm, wt_uq, uk_bd, head_bd, cos_a, sin_a, cos_t, sin_t)
    g_qn = jnp.sum(g_qn.reshape(A_HEADS, HD), axis=0)
    g_kn = jnp.sum(g_kn.reshape(A_KV, HD), axis=0)
    (dwt_in_e,) = in_proj_backward(
        x0, mod_e, nw_e, even_pieces(), "even_in_bwd_dw",
        dw_rows=[E_QA, E_KA, E_VA, E_GA, E_CQ, E_CKV, (1696, 2208), (1664, 1696)])
    dx0, dvec_e, l_in_e, l_uq = in_proj_backward(
        x0, mod_e, nw_e, even_pieces(), "even_in_bwd_dx", dx_out=dx1, w_in_t=wt_in_e,
        exchange=Exchange([blocks(dwt_in_e), blocks(_uq_rows_to_reference(dwt_uq))], True))

    dmod = jnp.stack([jnp.concatenate([dvec_e[0], dvec_e[1], dgate_e[0]]),
                      jnp.concatenate([dvec_o[0], dvec_o[1], dgate_o[0]])])
    d_norm_w = jnp.stack([dvec_e[2], dvec_o[2]])
    small_names = ["norm_w", "ada_b", "a_q_norm", "a_k_norm", "b_q_lora_norm", "b_kv_lora_norm", "b_w_uk", "b_w_uv",
                   "c_sink", "final_norm"]
    small_w = [norm_w, ada_b, a_q_norm, a_k_norm, b_q_lora_norm, b_kv_lora_norm, b_w_uk, b_w_uv, c_sink, final_norm]
    small_m = [m_norm_w, m_ada_b, m_a_q_norm, m_a_k_norm, m_b_q_lora_norm, m_b_kv_lora_norm, m_b_w_uk, m_b_w_uv,
               m_c_sink, m_final_norm]
    small_v = [v_norm_w, v_ada_b, v_a_q_norm, v_a_k_norm, v_b_q_lora_norm, v_b_kv_lora_norm, v_b_w_uk, v_b_w_uv,
               v_c_sink, v_final_norm]
    small_g = [d_norm_w, dmod, g_qn, g_kn, g_qln, g_kvln, jnp.transpose(dw_uk, (1, 0, 2)), jnp.transpose(dw_uv, (1, 0, 2)),
               dsink, d_final]
    flat2 = lambda a: a.reshape((1, -1)) if a.size == a.shape[-1] else a.reshape(a.shape[-3:] if a.ndim > 3 else a.shape)
    kshape = [flat2(w).shape for w in small_w]
    g_all = all_gather_slots(
        Gather([g.reshape(sh) for g, sh in zip(small_g, kshape)] + [jnp.full((8, 128), loss_part, F32)]),
        "gather_small_grads")
    sm_out, loss_sum = adamw_small(g_all[:-1], [flat2(a) for a in small_w], [flat2(a) for a in small_m],
                                   [flat2(a) for a in small_v], g_all[-1])
    loss = loss_sum[0, 0]
    sm = [{nm: p.reshape(w.shape) for nm, w, p in zip(small_names, small_w, outs)} for outs in sm_out]

    dmod_all = g_all[1].reshape(N_DEV, 2, N_DEV, wcols)
    dmod_cols = lax.dynamic_slice_in_dim(dmod_all, me_flat, 1, axis=2)[:, :, 0, :]
    pad16 = lambda a: jnp.concatenate([a, jnp.zeros_like(a)], axis=0)
    g_ada_w = ada_weight_grad(pad16(c_all), jnp.transpose(pad16(dmod_cols), (1, 0, 2)))
    rows_of = lambda a: a.reshape(-1, wcols)
    ada = adamw_rows(rows_of(g_ada_w)[None], rows_of(ada_w), rows_of(m_ada_w), rows_of(v_ada_w), "adamw_ada_w")
    ada = [p.reshape(ada_w.shape) for p in ada]

    bg = [{}, {}, {}, {}]
    for nm, landed, w, m, v, transposed in (
            ("even_w_in", l_in_e, even_w_in, m_even_w_in, v_even_w_in, True),
            ("b_w_uq", l_uq, b_w_uq, m_b_w_uq, v_b_w_uq, True),
            ("odd_w_in", l_in_o, odd_w_in, m_odd_w_in, v_odd_w_in, True),
            ("even_w_out", l_out_e, even_w_out, m_even_w_out, v_even_w_out, False),
            ("odd_w_out", l_out_o, odd_w_out, m_odd_w_out, v_odd_w_out, False)):
        view = _shard_t if transposed else (lambda a: a[0])
        res = adamw_rows(landed, view(w), view(m), view(v), "adamw_" + nm)
        for kind, p in enumerate(res):
            bg[kind][nm] = _unshard_t(p, w) if transposed else p[None]
    big_names = ["even_w_in", "odd_w_in", "even_w_out", "odd_w_out", "b_w_uq"]

    order = ["norm_w", "ada_w", "ada_b", "even_w_in", "a_q_norm", "a_k_norm", "b_q_lora_norm", "b_kv_lora_norm",
             "b_w_uq", "b_w_uk", "b_w_uv", "even_w_out", "odd_w_in", "c_sink", "odd_w_out", "final_norm"]

    def pick(kind):
        out = []
        for nm in order:
            if nm == "ada_w":
                out.append(ada[kind])
            elif nm in big_names:
                out.append(bg[kind][nm])
            else:
                out.append(sm[kind][nm])
        return out

    return (loss, dx0[None], *pick(0), *pick(1), *pick(2), *pick(3))
```

```python
import functools

import jax
import jax.numpy as jnp
import numpy as np
from jax import lax
from jax.experimental import pallas as pl
from jax.experimental.pallas import tpu as pltpu

F32 = jnp.float32
MXU = jnp.bfloat16
EPS = 1e-6
ROPE_THETA = 10000.0
GRID_W = 64
HD = 64
N_DEV = 8

A_HEADS, A_KV = 8, 2
B_HEADS, B_NOPE, B_ROPE, B_Q_LORA, B_KV_LORA = 8, 64, 32, 256, 128
B_QK = B_KV_LORA + B_ROPE
C_HEADS, C_KV = 16, 4
WINDOW = 128

ADAM_LR, ADAM_B1, ADAM_B2, ADAM_EPS, ADAM_WD, ADAM_STEP = 0.001, 0.9, 0.999, 1e-08, 0.01, 10

ROW_TILE = 512
ADAM_TILE = 2048 * 128

LOG2E = 1.4426950408889634
SCALE_A = HD ** -0.5
SCALE_B = (B_NOPE + B_ROPE) ** -0.5
SCALE2_A, SCALE2_B = SCALE_A * LOG2E, SCALE_B * LOG2E
VMEM_LIMIT = 56 * 1024 * 1024

E_QA, E_KA, E_VA, E_GA, E_CQ, E_CKV, E_GB, E_KR = (
    (0, 512), (512, 640), (640, 768), (768, 1280), (1280, 1536), (1536, 1664), (1664, 2176), (2176, 2208))
O_Q, O_K, O_V, O_G = (0, 1024), (1024, 1280), (1280, 1536), (1536, 2560)


def _mm(a, b):
    return jnp.dot(a.astype(MXU), b.astype(MXU), preferred_element_type=F32)


def _mm_nt(a, b):
    return lax.dot_general(a.astype(MXU), b.astype(MXU), (((1,), (1,)), ((), ())), preferred_element_type=F32)


def _mm_tn(a, b):
    return lax.dot_general(a.astype(MXU), b.astype(MXU), (((0,), (0,)), ((), ())), preferred_element_type=F32)


def _group_sums_t(prod, group):
    tm, w = prod.shape
    sel = (lax.broadcasted_iota(jnp.int32, (w, 128), 0) // group
           == lax.broadcasted_iota(jnp.int32, (w, 128), 1)).astype(MXU)
    hi = prod.astype(MXU)
    lo = prod - hi.astype(F32)
    return (_mm(hi, sel) + _mm(lo, sel)).T


def _sigmoid(z):
    return 1.0 / (1.0 + jnp.exp(-z))


def _silu(z):
    return z * _sigmoid(z)


def _rms(x):
    return lax.rsqrt(jnp.mean(x * x, axis=-1, keepdims=True) + EPS)


def _swap_halves(y, group):
    n = y.shape[-1]
    half = group // 2
    fwd = pltpu.roll(y, half, 1)
    if n == group:
        return fwd
    back = pltpu.roll(y, n - half, 1)
    lane = lax.broadcasted_iota(jnp.int32, y.shape, 1)
    return jnp.where((lane % group) < half, back, fwd)


def _rope(y, cos, sin, group):
    return y * cos + _swap_halves(y, group) * sin


def _rope_t(d, cos, sin, group):
    return d * cos - _swap_halves(d, group) * sin


def _rms_bwd(dy, x, g):
    r = _rms(x)
    xhat = x * r
    dxhat = dy * g
    dx = r * (dxhat - xhat * jnp.mean(dxhat * xhat, axis=-1, keepdims=True))
    return dx, dy * xhat


def _group_mean(v, bd, group):
    hi = v.astype(MXU)
    lo = v - hi.astype(F32)
    return (_mm(hi, bd[...]) + _mm(lo, bd[...])) * (1.0 / group)


def _head_norm(x, g, bd, group):
    return x * lax.rsqrt(_group_mean(x * x, bd, group) + EPS) * g


def _head_norm_bwd(dy, x, g, bd, group):
    r = lax.rsqrt(_group_mean(x * x, bd, group) + EPS)
    xhat = x * r
    dxhat = dy * g
    dx = r * (dxhat - xhat * _group_mean(dxhat * xhat, bd, group))
    return dx, dy * xhat


def _params(sem, vmem=VMEM_LIMIT):
    return pltpu.CompilerParams(dimension_semantics=sem, vmem_limit_bytes=vmem)


def _row_spec(tm, w):
    return pl.BlockSpec((tm, w), lambda i: (i, 0))


def _full_spec(shape):
    nd = len(shape)
    return pl.BlockSpec(shape, lambda i: (0,) * nd)


def _head_spec(h, tm, w):
    return pl.BlockSpec((h, tm, w), lambda i: (0, i, 0))


def _headt_spec(h, w, tm):
    return pl.BlockSpec((h, w, tm), lambda i: (0, 0, i))


def _rows_spec(h, tm):
    return pl.BlockSpec((h, tm), lambda i: (0, i))


def _me():
    return lax.axis_index("x"), lax.axis_index("y"), lax.axis_index("c")


def _flat(p):
    return 4 * p[0] + 2 * p[1] + p[2]


def _peer(me, k):
    x, y, c = me
    return (1 - x if k & 4 else x, 1 - y if k & 2 else y, 1 - c if k & 1 else c)


MESH_ID = pl.DeviceIdType.MESH


class Gather:
    VMEM = pl.BlockSpec(memory_space=pltpu.VMEM)

    def __init__(self, shards):
        self.shards = list(shards)
        self.n = len(self.shards)
        self.out_shapes = tuple(jax.ShapeDtypeStruct((N_DEV,) + a.shape, a.dtype) for a in self.shards)
        self.in_specs = [Gather.VMEM] * self.n
        self.out_specs = (Gather.VMEM,) * self.n
        self.sems = [pltpu.SemaphoreType.DMA((7 * self.n,)), pltpu.SemaphoreType.DMA((7 * self.n,)),
                     pltpu.SemaphoreType.DMA((self.n,))]

    def _plan(self, x_refs, out_refs, sems):
        send_sems, recv_sems, local_sems = sems
        me = _me()
        x, y, c = me
        chips = [(1 - x, y), (x, 1 - y), (1 - x, 1 - y)]

        def copy(a, k, block, to, src=None):
            slot = out_refs[a].at[_flat(block)]
            return pltpu.make_async_remote_copy(
                src_ref=slot if src is None else src, dst_ref=slot, send_sem=send_sems.at[7 * a + k],
                recv_sem=recv_sems.at[7 * a + k], device_id=to, device_id_type=MESH_ID)

        mine = [pltpu.make_async_copy(x_refs[a], out_refs[a].at[_flat(me)], local_sems.at[a]) for a in range(self.n)]
        first = [copy(a, 0, me, (x, y, 1 - c), src=x_refs[a]) for a in range(self.n)]
        first += [copy(a, 1 + j, me, (*chip, c), src=x_refs[a]) for a in range(self.n) for j, chip in enumerate(chips)]
        return me, chips, copy, mine, first

    def start(self, x_refs, out_refs, sems):
        _, _, _, mine, first = self._plan(x_refs, out_refs, sems)
        for cp in mine + first:
            cp.start()

    def finish(self, x_refs, out_refs, sems):
        me, chips, copy, mine, first = self._plan(x_refs, out_refs, sems)
        x, y, c = me
        sibling = (x, y, 1 - c)
        passed = []
        for a in range(self.n):
            for j, chip in enumerate(chips):
                copy(a, 1 + j, (*chip, c), me).wait_recv()
                passed.append(copy(a, 4 + j, (*chip, c), sibling))
                passed[-1].start()
        for a in range(self.n):
            copy(a, 0, sibling, me).wait_recv()
            for j, chip in enumerate(chips):
                copy(a, 4 + j, (*chip, 1 - c), me).wait_recv()
        for cp in first + passed:
            cp.wait_send()
        for cp in mine:
            cp.wait()


def all_gather_slots(gather, name):
    def body(*refs):
        x_refs, out_refs, sems = refs[:gather.n], refs[gather.n:2 * gather.n], refs[2 * gather.n:]
        gather.start(x_refs, out_refs, sems)
        gather.finish(x_refs, out_refs, sems)

    return pl.pallas_call(
        body, name=name, out_shape=gather.out_shapes, in_specs=gather.in_specs, out_specs=gather.out_specs,
        scratch_shapes=list(gather.sems), compiler_params=pltpu.CompilerParams(vmem_limit_bytes=VMEM_LIMIT),
    )(*gather.shards)


class Exchange:
    HBM = pl.BlockSpec(memory_space=pl.ANY)

    def __init__(self, srcs, scatter):
        self.srcs = list(srcs)
        self.scatter = scatter
        self.n = len(self.srcs)
        self.land_shapes = tuple(jax.ShapeDtypeStruct((N_DEV,) + tuple(a.shape[-2:]), a.dtype) for a in self.srcs)
        self.in_specs = [Exchange.HBM] * self.n
        self.out_specs = (Exchange.HBM,) * self.n
        self.sems = [pltpu.SemaphoreType.DMA((N_DEV - 1,)), pltpu.SemaphoreType.DMA((N_DEV - 1,)),
                     pltpu.SemaphoreType.DMA] * self.n

    def _copies(self, src_refs, land_refs, sems):
        me = _me()
        mi = _flat(me)
        local, sends, recvs = [], [], []
        for a, (src_ref, land_ref) in enumerate(zip(src_refs, land_refs)):
            send_sems, recv_sems, local_sem = sems[3 * a:3 * a + 3]
            pick = (lambda p, r=src_ref: r.at[_flat(p)]) if self.scatter else (lambda p, r=src_ref: r)
            local.append(pltpu.make_async_copy(pick(me), land_ref.at[mi], local_sem))
            for k in range(1, N_DEV):
                peer = _peer(me, k)
                pair = dict(send_sem=send_sems.at[k - 1], recv_sem=recv_sems.at[k - 1], device_id=peer,
                            device_id_type=MESH_ID)
                sends.append(pltpu.make_async_remote_copy(src_ref=pick(peer), dst_ref=land_ref.at[mi], **pair))
                recvs.append(pltpu.make_async_remote_copy(src_ref=pick(peer), dst_ref=land_ref.at[_flat(peer)],
                                                          **pair))
        return local, sends, recvs

    def start(self, src_refs, land_refs, sems):
        local, sends, _ = self._copies(src_refs, land_refs, sems)
        for cp in local + sends:
            cp.start()

    def wait(self, src_refs, land_refs, sems):
        local, sends, recvs = self._copies(src_refs, land_refs, sems)
        for cp in recvs:
            cp.wait_recv()
        for cp in sends:
            cp.wait_send()
        for cp in local:
            cp.wait()


def ada_forward(c8, ada_w, bias_cols, gather):
    d = c8.shape[1]
    w = ada_w.shape[2]
    ng = gather.n

    def body(*refs):
        c_ref, w_ref, b_ref = refs[:3]
        gx_refs = refs[3:3 + ng]
        call_ref, modp_ref = refs[3 + ng:5 + ng]
        gout_refs = refs[5 + ng:5 + 2 * ng]
        part_ref, s1, r1, s2, r2 = refs[5 + 2 * ng:10 + 2 * ng]
        g_sems = refs[10 + 2 * ng:]
        gather.start(gx_refs, gout_refs, g_sems)
        me = _me()
        mi = _flat(me)
        call_ref[mi] = c_ref[...]
        rows_out = []
        for k in range(1, N_DEV):
            rows_out.append(pltpu.make_async_remote_copy(
                src_ref=c_ref, dst_ref=call_ref.at[mi], send_sem=s1.at[k - 1], recv_sem=r1.at[k - 1],
                device_id=_peer(me, k), device_id_type=MESH_ID))
        for cp in rows_out:
            cp.start()
        for k in range(1, N_DEV):
            pltpu.make_async_remote_copy(
                src_ref=c_ref, dst_ref=call_ref.at[_flat(_peer(me, k))], send_sem=s1.at[k - 1],
                recv_sem=r1.at[k - 1], device_id=_peer(me, k), device_id_type=MESH_ID).wait_recv()
        ca = _silu(call_ref[...].reshape(N_DEV * 8, d))
        for l in range(2):
            part = _mm(ca, w_ref[l]) + b_ref[l]
            for b in range(N_DEV):
                part_ref[b, l] = part[8 * b:8 * b + 8, :]
        modp_ref[mi] = part_ref[mi]
        spread = []
        for k in range(1, N_DEV):
            peer = _peer(me, k)
            spread.append(pltpu.make_async_remote_copy(
                src_ref=part_ref.at[_flat(peer)], dst_ref=modp_ref.at[mi], send_sem=s2.at[k - 1],
                recv_sem=r2.at[k - 1], device_id=peer, device_id_type=MESH_ID))
        for cp in spread:
            cp.start()
        for k in range(1, N_DEV):
            pi = _flat(_peer(me, k))
            pltpu.make_async_remote_copy(
                src_ref=part_ref.at[pi], dst_ref=modp_ref.at[pi], send_sem=s2.at[k - 1],
                recv_sem=r2.at[k - 1], device_id=_peer(me, k), device_id_type=MESH_ID).wait_recv()
        for cp in rows_out + spread:
            cp.wait_send()
        gather.finish(gx_refs, gout_refs, g_sems)

    vm = pl.BlockSpec(memory_space=pltpu.VMEM)
    res = pl.pallas_call(
        body, name="ada_forward",
        out_shape=(jax.ShapeDtypeStruct((N_DEV, 8, d), F32), jax.ShapeDtypeStruct((N_DEV, 2, 8, w), F32))
        + gather.out_shapes,
        in_specs=[vm, vm, vm] + gather.in_specs, out_specs=(vm, vm) + gather.out_specs,
        scratch_shapes=[pltpu.VMEM((N_DEV, 2, 8, w), F32)] + [pltpu.SemaphoreType.DMA((7,))] * 4 + list(gather.sems),
        compiler_params=pltpu.CompilerParams(vmem_limit_bytes=VMEM_LIMIT),
    )(c8, ada_w, bias_cols, *gather.shards)
    return res[0], res[1], res[2:]


def _modulated(x, mod_ref, nw_ref):
    xn = x * _rms(x)
    g1 = nw_ref[...] * (1.0 + mod_ref[1:2, :])
    return xn, g1, xn * g1 + mod_ref[0:1, :]


def even_in_forward(x, mod, nw, w_in_t, gq, gk, qln, kvln, w_uq_t, uk_bd, bd, cos_a, sin_a, cos_t, sin_t):
    s, d = x.shape
    tm = min(ROW_TILE, s)
    n_nope = B_HEADS * B_NOPE

    def body(x_ref, mod_ref, nw_ref, w_ref, gq_ref, gk_ref, qln_ref, kvln_ref, uq_ref, ukbd_ref, bd_ref,
             ca_ref, sa_ref, ct_ref, st_ref,
             qa_o, ka_o, va_o, qb_o, kb_o, kat_o, vat_o, kbt_o, qa_raw_o, ka_raw_o, cq_raw_o, ckv_raw_o, ga_o, gb_o):
        _, _, h = _modulated(x_ref[...], mod_ref, nw_ref)
        h = h.astype(MXU)

        def proj(cols):
            return _mm_nt(h, w_ref[cols[0]:cols[1], :])

        ca, sa, ct, st = ca_ref[...], sa_ref[...], ct_ref[...], st_ref[...]
        wide = lambda t, n: jnp.concatenate([t] * n, axis=1)
        qa = proj(E_QA)
        qa_raw_o[...] = qa
        qr = _rope(_head_norm(qa, gq_ref[...], bd_ref, HD), wide(ca, 4), wide(sa, 4), 32) * SCALE2_A
        for hh in range(A_HEADS):
            qa_o[hh] = qr[:, HD * hh:HD * hh + HD].astype(MXU)
        ka = proj(E_KA)
        ka_raw_o[...] = ka
        kr = _rope(_head_norm(ka, gk_ref[...], bd_ref[0:128, 0:128], HD), ca, sa, 32)
        va = proj(E_VA)
        krt, vat = kr.T, va.T
        for g in range(A_KV):
            ka_o[g] = kr[:, HD * g:HD * g + HD].astype(MXU)
            va_o[g] = va[:, HD * g:HD * g + HD].astype(MXU)
            kat_o[g] = krt[HD * g:HD * g + HD, :].astype(MXU)
            vat_o[g] = vat[HD * g:HD * g + HD, :].astype(MXU)
        ga_o[...] = proj(E_GA)
        gb_o[...] = proj(E_GB)
        cq = proj(E_CQ)
        cq_raw_o[...] = cq
        qb = _mm_nt(cq * _rms(cq) * qln_ref[...], uq_ref[...])
        q_lat = _mm(qb[:, 0:n_nope], ukbd_ref[...]) * SCALE2_B
        q_rope = _rope(qb[:, n_nope:], wide(ct, 2), wide(st, 2), 32) * SCALE2_B
        for hh in range(B_HEADS):
            qb_o[hh, :, 0:B_KV_LORA] = q_lat[:, B_KV_LORA * hh:B_KV_LORA * (hh + 1)].astype(MXU)
            qb_o[hh, :, B_KV_LORA:B_QK] = q_rope[:, B_ROPE * hh:B_ROPE * (hh + 1)].astype(MXU)
        ckv = proj(E_CKV)
        ckv_raw_o[...] = ckv
        ckv_n = ckv * _rms(ckv) * kvln_ref[...]
        k_rope = _rope(proj(E_KR), ct[:, 0:B_ROPE], st[:, 0:B_ROPE], 32)
        kb_o[0, :, 0:B_KV_LORA] = ckv_n.astype(MXU)
        kb_o[0, :, B_KV_LORA:B_QK] = k_rope.astype(MXU)
        kbt_o[0, 0:B_KV_LORA, :] = ckv_n.T.astype(MXU)
        kbt_o[0, B_KV_LORA:B_QK, :] = k_rope.T.astype(MXU)

    sd = jax.ShapeDtypeStruct
    outs = (sd((A_HEADS, s, HD), MXU), sd((A_KV, s, HD), MXU), sd((A_KV, s, HD), MXU),
            sd((B_HEADS, s, B_QK), MXU), sd((1, s, B_QK), MXU),
            sd((A_KV, HD, s), MXU), sd((A_KV, HD, s), MXU), sd((1, B_QK, s), MXU),
            sd((s, 512), F32), sd((s, 128), F32), sd((s, B_Q_LORA), F32), sd((s, B_KV_LORA), F32),
            sd((s, 512), F32), sd((s, 512), F32))
    out_specs = (_head_spec(A_HEADS, tm, HD), _head_spec(A_KV, tm, HD), _head_spec(A_KV, tm, HD),
                 _head_spec(B_HEADS, tm, B_QK), _head_spec(1, tm, B_QK),
                 _headt_spec(A_KV, HD, tm), _headt_spec(A_KV, HD, tm), _headt_spec(1, B_QK, tm),
                 _row_spec(tm, 512), _row_spec(tm, 128), _row_spec(tm, B_Q_LORA), _row_spec(tm, B_KV_LORA),
                 _row_spec(tm, 512), _row_spec(tm, 512))
    consts = [mod, nw, w_in_t, gq, gk, qln, kvln, w_uq_t, uk_bd, bd]
    return pl.pallas_call(
        body, name="even_in_forward", grid=(s // tm,), out_shape=outs,
        in_specs=[_row_spec(tm, d)] + [_full_spec(a.shape) for a in consts] + [_row_spec(tm, 128)] * 4,
        out_specs=out_specs, compiler_params=_params(("parallel",)),
    )(x, *consts, cos_a, sin_a, cos_t, sin_t)


def odd_in_forward(x, mod, nw, w_in):
    s, d = x.shape
    tm = min(ROW_TILE, s)

    def body(x_ref, mod_ref, nw_ref, w_ref, q_o, k_o, v_o, kt_o, vt_o, g_o):
        _, _, h = _modulated(x_ref[...], mod_ref, nw_ref)
        h = h.astype(MXU)

        def proj(cols):
            return _mm_nt(h, w_ref[cols[0]:cols[1], :])

        q = proj(O_Q) * SCALE2_A
        for hh in range(C_HEADS):
            q_o[hh] = q[:, HD * hh:HD * hh + HD].astype(MXU)
        k = proj(O_K)
        v = proj(O_V)
        for g in range(C_KV):
            kh = k[:, HD * g:HD * g + HD]
            vh = v[:, HD * g:HD * g + HD]
            k_o[g] = kh.astype(MXU)
            v_o[g] = vh.astype(MXU)
            kt_o[g] = kh.T.astype(MXU)
            vt_o[g] = vh.T.astype(MXU)
        g_o[...] = proj(O_G)

    sd = jax.ShapeDtypeStruct
    return pl.pallas_call(
        body, name="odd_in_forward", grid=(s // tm,),
        out_shape=(sd((C_HEADS, s, HD), MXU), sd((C_KV, s, HD), MXU), sd((C_KV, s, HD), MXU),
                   sd((C_KV, HD, s), MXU), sd((C_KV, HD, s), MXU), sd((s, 1024), F32)),
        in_specs=[_row_spec(tm, d), _full_spec(mod.shape), _full_spec(nw.shape), _full_spec(w_in.shape)],
        out_specs=(_head_spec(C_HEADS, tm, HD), _head_spec(C_KV, tm, HD), _head_spec(C_KV, tm, HD),
                   _headt_spec(C_KV, HD, tm), _headt_spec(C_KV, HD, tm), _row_spec(tm, 1024)),
        compiler_params=_params(("parallel",)),
    )(x, mod, nw, w_in)


def latent_out_forward(o_lat, w_uv):
    s = o_lat.shape[0]
    tm = min(ROW_TILE, s)

    def body(o_ref, uv_ref, out_ref):
        for hh in range(B_HEADS):
            out_ref[:, HD * hh:HD * hh + HD] = _mm(o_ref[:, B_KV_LORA * hh:B_KV_LORA * (hh + 1)], uv_ref[hh])

    return pl.pallas_call(
        body, name="latent_out_forward", grid=(s // tm,),
        out_shape=jax.ShapeDtypeStruct((s, B_HEADS * HD), F32),
        in_specs=[_row_spec(tm, o_lat.shape[1]), _full_spec(w_uv.shape)],
        out_specs=_row_spec(tm, B_HEADS * HD),
        compiler_params=_params(("parallel",)),
    )(o_lat, w_uv)


def mixer_out_forward(x, mod, pairs, w_out, name):
    s, d = x.shape
    tm = min(ROW_TILE, s)
    n = len(pairs)
    widths = [o.shape[1] for o, _ in pairs]

    def body(*refs):
        x_ref, mod_ref, w_ref = refs[:3]
        pr = refs[3:3 + 2 * n]
        xo_ref, y_ref = refs[3 + 2 * n:]
        y = jnp.zeros((tm, d), F32)
        r0 = 0
        for i in range(n):
            mix = pr[2 * i][...] * _silu(pr[2 * i + 1][...])
            y = y + _mm(mix, w_ref[r0:r0 + widths[i], :])
            r0 += widths[i]
        y_ref[...] = y
        xo_ref[...] = x_ref[...] + mod_ref[2:3, :] * y

    flat = [a for p in pairs for a in p]
    sd = jax.ShapeDtypeStruct
    return pl.pallas_call(
        body, name=name, grid=(s // tm,),
        out_shape=(sd((s, d), F32), sd((s, d), F32)),
        in_specs=[_row_spec(tm, d), _full_spec(mod.shape), _full_spec(w_out.shape)]
        + [_row_spec(tm, a.shape[1]) for a in flat],
        out_specs=(_row_spec(tm, d), _row_spec(tm, d)),
        compiler_params=_params(("parallel",)),
    )(x, mod, w_out, *flat)


ONES_ROWS = 16
AHEAD = 2


def _col_max8(s3):
    m8 = jnp.max(s3, axis=0)
    return jnp.broadcast_to(jnp.max(m8, axis=0, keepdims=True), m8.shape)


def _with_ones(vt, n):
    return jnp.concatenate([vt, jnp.ones((ONES_ROWS, n), vt.dtype)], axis=0)


def _grid_edges(grid):
    ids = [pl.program_id(a) for a in range(len(grid))]
    first = functools.reduce(jnp.logical_and, [i == 0 for i in ids])
    last = functools.reduce(jnp.logical_and, [i == n - 1 for i, n in zip(ids, grid)])
    return first, last


def flash_forward(q, k, vt, *, dv, tq, tk, nsub, name, exchange=None):
    hq, s, dq = q.shape
    g_kv = k.shape[0]
    hpg = hq // g_kv
    nq = s // tq
    tkk = tk * nsub
    nk = s // tkk
    grid = (g_kv, nq, nk)
    hosted = exchange is not None
    m_cols = hpg * tq
    dvp = dv + ONES_ROWS

    def body(*refs):
        nx = exchange.n if hosted else 0
        q_ref, k_ref, vt_ref = refs[:3]
        xs_refs = refs[3:3 + nx]
        o_ref, lse_ref = refs[3 + nx:5 + nx]
        land_refs = refs[5 + nx:5 + 2 * nx]
        m_s, acc_s = refs[5 + 2 * nx:7 + 2 * nx]
        sems = refs[7 + 2 * nx:]
        if hosted:
            first, last = _grid_edges(grid)
            pl.when(first)(lambda: exchange.start(xs_refs, land_refs, sems))
        j = pl.program_id(2)

        @pl.when(j == 0)
        def _():
            m_s[...] = jnp.full((8, m_cols), -jnp.inf, F32)
            acc_s[...] = jnp.zeros((dvp, m_cols), F32)

        qq = q_ref[...].reshape(m_cols, dq)
        score = lambda u: _mm_nt(k_ref[0, tk * u:tk * (u + 1), :], qq).reshape(tk // 8, 8, m_cols)
        sts = {u: score(u) for u in range(min(AHEAD, nsub))}
        m_run = m_s[...]
        acc = acc_s[...]
        for u in range(nsub):
            if u + AHEAD < nsub:
                sts[u + AHEAD] = score(u + AHEAD)
            st = sts.pop(u)
            m_new = jnp.maximum(m_run, _col_max8(st))
            p = jnp.exp2(st - m_new[None])
            alpha = jnp.exp2(m_run - m_new)
            pv = _mm(_with_ones(vt_ref[0, 0:dv, tk * u:tk * (u + 1)], tk), p.reshape(tk, m_cols))
            acc = (acc.reshape(dvp // 8, 8, m_cols) * alpha[None]).reshape(dvp, m_cols) + pv
            m_run = m_new
        acc_s[...] = acc
        m_s[...] = m_run

        @pl.when(j == nk - 1)
        def _():
            l = acc_s[dv:dv + 1, :]
            ot = acc_s[0:dv, :] / l
            lse = m_s[0:1, :] + jnp.log2(l)
            for hh in range(hpg):
                o_ref[:, dv * hh:dv * hh + dv] = ot[:, tq * hh:tq * hh + tq].T
                lse_ref[hh] = lse[:, tq * hh:tq * hh + tq]

        if hosted:
            pl.when(last)(lambda: exchange.wait(xs_refs, land_refs, sems))

    sd = jax.ShapeDtypeStruct
    return pl.pallas_call(
        body, name=name, grid=grid,
        out_shape=(sd((s, hq * dv), F32), sd((hq, 1, s), F32)) + (exchange.land_shapes if hosted else ()),
        in_specs=[pl.BlockSpec((hpg, tq, dq), lambda g, i, j: (g, i, 0)),
                  pl.BlockSpec((1, tkk, k.shape[2]), lambda g, i, j: (g, j, 0)),
                  pl.BlockSpec((1, dv, tkk), lambda g, i, j: (g, 0, j))] + (exchange.in_specs if hosted else []),
        out_specs=(pl.BlockSpec((tq, hpg * dv), lambda g, i, j: (i, g)),
                   pl.BlockSpec((hpg, 1, tq), lambda g, i, j: (g, 0, i))) + (exchange.out_specs if hosted else ()),
        scratch_shapes=[pltpu.VMEM((8, m_cols), F32), pltpu.VMEM((dvp, m_cols), F32)]
        + (list(exchange.sems) if hosted else []),
        compiler_params=_params(("arbitrary",) * 3 if hosted else ("parallel", "parallel", "arbitrary")),
    )(q, k, vt, *(exchange.srcs if hosted else []))


def _window_bias_t(hpg, slope_ref):
    t = WINDOW
    r = lax.broadcasted_iota(jnp.int32, (3 * t, t), 0)
    cq = lax.broadcasted_iota(jnp.int32, (3 * t, t), 1)
    arel = jnp.abs(r - t - cq)
    base = jnp.where(arel <= WINDOW, arel.astype(F32) * (-LOG2E), -jnp.inf)
    return jnp.concatenate([base * slope_ref[hh] for hh in range(hpg)], axis=1)


def _window_edges_t(bias, no_before, no_after):
    t = WINDOW
    r = lax.broadcasted_iota(jnp.int32, bias.shape, 0)
    out = ((r < t) & no_before) | ((r >= 2 * t) & no_after)
    return jnp.where(out, -jnp.inf, bias)


def _window_specs(kind, nb, nblk, d):
    t = WINDOW
    before = lambda i: jnp.clip(i * nb - 1, 0, nblk - 1)
    after = lambda i: jnp.clip((i + 1) * nb, 0, nblk - 1)
    if kind == "rows":
        return [pl.BlockSpec((1, t, d), lambda g, i: (g, before(i), 0)),
                pl.BlockSpec((1, nb * t, d), lambda g, i: (g, i, 0)),
                pl.BlockSpec((1, t, d), lambda g, i: (g, after(i), 0))]
    return [pl.BlockSpec((1, d, t), lambda g, i: (g, 0, before(i))),
            pl.BlockSpec((1, d, nb * t), lambda g, i: (g, 0, i)),
            pl.BlockSpec((1, d, t), lambda g, i: (g, 0, after(i)))]


def window_forward(q, k, vt, sink2, slopes, nb, name):
    hq, s, d = q.shape
    g_kv = k.shape[0]
    hpg = hq // g_kv
    t = WINDOW
    nblk = s // t
    steps = nblk // nb
    m_cols = hpg * t

    def body(q_ref, kp, ko, kn, vp, vo, vn, sink_ref, slope_ref, o_ref, lse_ref):
        i = pl.program_id(1)
        kk_all = jnp.concatenate([kp[0], ko[0], kn[0]], axis=0)
        vt_all = jnp.concatenate([vp[0], vo[0], vn[0]], axis=1)
        bias = _window_bias_t(hpg, slope_ref)
        sink_row = jnp.concatenate([jnp.broadcast_to(sink_ref[hh], (8, t)) for hh in range(hpg)], axis=1)
        sts = []
        for u in range(nb):
            qq = q_ref[:, t * u:t * (u + 1), :].reshape(m_cols, d)
            b_u = bias
            if u == 0 or u == nb - 1:
                b_u = _window_edges_t(bias, (i == 0) if u == 0 else False,
                                      (i == steps - 1) if u == nb - 1 else False)
            sts.append(_mm_nt(kk_all[t * u:t * (u + 3), :], qq) + b_u)
        for u in range(nb):
            s3 = sts[u].reshape(3 * t // 8, 8, m_cols)
            m8 = jnp.maximum(_col_max8(s3), sink_row)
            p = jnp.exp2(s3 - m8[None]).reshape(3 * t, m_cols)
            acc = _mm(_with_ones(vt_all[:, t * u:t * (u + 3)], 3 * t), p)
            l = acc[d:d + 1, :] + jnp.exp2(sink_row[0:1, :] - m8[0:1, :])
            ot = acc[0:d, :] / l
            lse = m8[0:1, :] + jnp.log2(l)
            for hh in range(hpg):
                o_ref[t * u:t * (u + 1), d * hh:d * hh + d] = ot[:, t * hh:t * hh + t].T
                lse_ref[hh, :, t * u:t * (u + 1)] = lse[:, t * hh:t * hh + t]

    sd = jax.ShapeDtypeStruct
    return pl.pallas_call(
        body, name=name, grid=(g_kv, steps),
        out_shape=(sd((s, hq * d), F32), sd((hq, 1, s), F32)),
        in_specs=[pl.BlockSpec((hpg, nb * t, d), lambda g, i: (g, i, 0))]
        + _window_specs("rows", nb, nblk, d) + _window_specs("cols", nb, nblk, d)
        + [pl.BlockSpec((hpg, 1, 1), lambda g, i: (g, 0, 0))] * 2,
        out_specs=(pl.BlockSpec((nb * t, hpg * d), lambda g, i: (i, g)),
                   pl.BlockSpec((hpg, 1, nb * t), lambda g, i: (g, 0, i))),
        compiler_params=_params(("parallel", "parallel")),
    )(q, k, k, k, vt, vt, vt, sink2, slopes)


def window_backward(q, k, kt, v, do, lse, delta, slopes, nb, name):
    hq, s, d = q.shape
    g_kv = k.shape[0]
    hpg = hq // g_kv
    t = WINDOW
    nblk = s // t
    steps = nblk // nb
    m_cols = hpg * t

    def body(q_ref, kp, ko, kn, ktp, kto, ktn, vp, vo, vn, do_ref, lse_ref, dl_ref, slope_ref,
             dq_ref, dk_ref, dv_ref, dk_s, dv_s):
        i = pl.program_id(1)

        @pl.when(i == 0)
        def _():
            dk_ref[...] = jnp.zeros(dk_ref.shape, F32)
            dv_ref[...] = jnp.zeros(dv_ref.shape, F32)

        dk_s[...] = jnp.zeros(dk_s.shape, F32)
        dv_s[...] = jnp.zeros(dv_s.shape, F32)
        kk_all = jnp.concatenate([kp[0], ko[0], kn[0]], axis=0)
        vv_all = jnp.concatenate([vp[0], vo[0], vn[0]], axis=0)
        kkt_all = jnp.concatenate([ktp[0], kto[0], ktn[0]], axis=1)
        bias = _window_bias_t(hpg, slope_ref)
        qqs, dds, sts, dps = [], [], [], []
        for u in range(nb):
            rows = slice(t * u, t * (u + 1))
            keys = slice(t * u, t * (u + 3))
            qqs.append(q_ref[:, rows, :].reshape(m_cols, d))
            dds.append(jnp.concatenate([do_ref[rows, d * hh:d * hh + d] for hh in range(hpg)], axis=0))
            b_u = bias
            if u == 0 or u == nb - 1:
                b_u = _window_edges_t(bias, (i == 0) if u == 0 else False,
                                      (i == steps - 1) if u == nb - 1 else False)
            sts.append(_mm_nt(kk_all[keys, :], qqs[u]) + b_u)
            dps.append(_mm_nt(vv_all[keys, :], dds[u]))
        for u in range(nb):
            rows = slice(t * u, t * (u + 1))
            keys = slice(t * u, t * (u + 3))
            lse_row = jnp.concatenate([lse_ref[hh, :, rows] for hh in range(hpg)], axis=1)
            dl_row = jnp.concatenate([dl_ref[hh, :, rows] for hh in range(hpg)], axis=1)
            p = jnp.exp2(sts[u] - lse_row)
            ds = p * (dps[u] - dl_row) * SCALE_A
            dv_s[keys, :] += _mm(p, dds[u])
            dk_s[keys, :] += _mm(ds, qqs[u])
            dqt = _mm(kkt_all[:, keys], ds)
            for hh in range(hpg):
                dq_ref[rows, d * hh:d * hh + d] = dqt[:, t * hh:t * hh + t].T
        tq = nb * t
        for src, r0, n in ((0, jnp.clip(i * nb - 1, 0, nblk - 1) * t, t), (t, i * tq, tq),
                           (t + tq, jnp.clip((i + 1) * nb, 0, nblk - 1) * t, t)):
            dst = pl.ds(pl.multiple_of(r0, t), n)
            dk_ref[0, dst, :] += dk_s[src:src + n, :] * (1.0 / SCALE2_A)
            dv_ref[0, dst, :] += dv_s[src:src + n, :]

    row_map = lambda g, i: (g, 0, i)
    sd = jax.ShapeDtypeStruct
    return pl.pallas_call(
        body, name=name, grid=(g_kv, steps),
        out_shape=(sd((s, hq * d), F32), sd((g_kv, s, d), F32), sd((g_kv, s, d), F32)),
        in_specs=[pl.BlockSpec((hpg, nb * t, d), lambda g, i: (g, i, 0))]
        + _window_specs("rows", nb, nblk, d) + _window_specs("cols", nb, nblk, d) + _window_specs("rows", nb, nblk, d)
        + [pl.BlockSpec((nb * t, hpg * d), lambda g, i: (i, g)), pl.BlockSpec((hpg, 1, nb * t), row_map),
           pl.BlockSpec((hpg, 1, nb * t), row_map), pl.BlockSpec((hpg, 1, 1), lambda g, i: (g, 0, 0))],
        out_specs=(pl.BlockSpec((nb * t, hpg * d), lambda g, i: (i, g)),
                   pl.BlockSpec((1, s, d), lambda g, i: (g, 0, 0)),
                   pl.BlockSpec((1, s, d), lambda g, i: (g, 0, 0))),
        scratch_shapes=[pltpu.VMEM(((nb + 2) * t, d), F32), pltpu.VMEM(((nb + 2) * t, d), F32)],
        compiler_params=_params(("parallel", "arbitrary")),
    )(q, k, k, k, kt, kt, kt, v, v, v, do, lse, delta, slopes)


def flash_backward(q, k, kt, v, do, lse, delta, *, scale, dv, tq, tk, nsub, gq, name, split=None, exchange=None):
    hq, s, dq = q.shape
    g_kv = k.shape[0]
    hpg = hq // gq
    nq = s // tq
    tqq = tq * nsub
    nqs = s // tqq
    nkb = s // tk
    grid = (gq, nkb, nqs)
    hosted = exchange is not None
    m_cols = hpg * tq
    c = scale * LOG2E
    has_v = v is not None

    def body(*refs):
        it = iter(refs)
        q_ref, k_ref, kt_ref = next(it), next(it), next(it)
        v_ref = next(it) if has_v else None
        do_ref, lse_ref, dl_ref = next(it), next(it), next(it)
        nx = exchange.n if hosted else 0
        xs_refs = [next(it) for _ in range(nx)]
        dq_ref, dk_ref, dv_ref = next(it), next(it), next(it)
        land_refs = [next(it) for _ in range(nx)]
        dqt_s = next(it)
        sems = list(it)
        kj = pl.program_id(1)
        qi = pl.program_id(2)
        if hosted:
            first, last = _grid_edges(grid)
            pl.when(first)(lambda: exchange.start(xs_refs, land_refs, sems))

        @pl.when((kj == 0) & (qi == 0))
        def _():
            dqt_s[...] = jnp.zeros(dqt_s.shape, F32)

        @pl.when(qi == 0)
        def _():
            dk_ref[...] = jnp.zeros(dk_ref.shape, F32)
            dv_ref[...] = jnp.zeros(dv_ref.shape, F32)

        kk = k_ref[0]
        vv = v_ref[0] if has_v else kk[:, :dv]
        qqs, dds, sts, dps = [], [], [], []
        for u in range(nsub):
            rows = slice(tq * u, tq * (u + 1))
            qqs.append(q_ref[:, rows, :].reshape(m_cols, dq))
            dds.append(jnp.concatenate([do_ref[rows, dv * hh:dv * hh + dv] for hh in range(hpg)], axis=0))
            sts.append(_mm_nt(kk, qqs[u]))
            dps.append(_mm_nt(vv, dds[u]))
        dv_acc = dv_ref[0]
        dk_acc = dk_ref[0]
        for u in range(nsub):
            rows = slice(tq * u, tq * (u + 1))
            lse_row = jnp.concatenate([lse_ref[hh, :, rows] for hh in range(hpg)], axis=1)
            dl_row = jnp.concatenate([dl_ref[hh, :, rows] for hh in range(hpg)], axis=1)
            p = jnp.exp2(sts[u] - lse_row)
            ds = p * (dps[u] - dl_row) * scale
            dv_acc = dv_acc + _mm(p, dds[u])
            dk_acc = dk_acc + _mm(ds, qqs[u])
            dqt = _mm(kt_ref[0], ds)
            for hh in range(hpg):
                dqt_s[qi * nsub + u, dq * hh:dq * hh + dq, :] += dqt[:, tq * hh:tq * hh + tq]
        dv_ref[0] = dv_acc
        dk_ref[0] = jnp.where(qi == nqs - 1, dk_acc * (1.0 / c), dk_acc)

        @pl.when((kj == nkb - 1) & (qi == nqs - 1))
        def _():
            def emit(t, carry):
                r0 = pl.multiple_of(t * tq, tq)
                for hh in range(hpg):
                    blk = dqt_s[t, dq * hh:dq * hh + dq, :].T
                    if split is None:
                        dq_ref[pl.ds(r0, tq), dq * hh:dq * hh + dq] = blk
                    else:
                        rest = dq - split
                        dq_ref[pl.ds(r0, tq), split * hh:split * (hh + 1)] = blk[:, 0:split]
                        dq_ref[pl.ds(r0, tq), hpg * split + rest * hh:hpg * split + rest * (hh + 1)] = blk[:, split:]
                return carry

            lax.fori_loop(0, nq, emit, 0)

        if hosted:
            pl.when(last)(lambda: exchange.wait(xs_refs, land_refs, sems))

    kv_of = lambda g: g * g_kv // gq
    in_specs = [pl.BlockSpec((hpg, tqq, dq), lambda g, kj, qi: (g, qi, 0)),
                pl.BlockSpec((1, tk, dq), lambda g, kj, qi: (kv_of(g), kj, 0)),
                pl.BlockSpec((1, dq, tk), lambda g, kj, qi: (kv_of(g), 0, kj))]
    args = [q, k, kt]
    if has_v:
        in_specs.append(pl.BlockSpec((1, tk, dv), lambda g, kj, qi: (kv_of(g), kj, 0)))
        args.append(v)
    row_map = lambda g, kj, qi: (g, 0, qi)
    in_specs += [pl.BlockSpec((tqq, hpg * dv), lambda g, kj, qi: (qi, g)),
                 pl.BlockSpec((hpg, 1, tqq), row_map), pl.BlockSpec((hpg, 1, tqq), row_map)]
    args += [do, lse, delta]
    if hosted:
        in_specs += exchange.in_specs
        args += exchange.srcs
    sd = jax.ShapeDtypeStruct
    return pl.pallas_call(
        body, name=name, grid=grid,
        out_shape=(sd((s, hq * dq), F32), sd((gq, s, dq), F32), sd((gq, s, dv), F32))
        + (exchange.land_shapes if hosted else ()),
        in_specs=in_specs,
        out_specs=(pl.BlockSpec((s, hpg * dq), lambda g, kj, qi: (0, g)),
                   pl.BlockSpec((1, tk, dq), lambda g, kj, qi: (g, kj, 0)),
                   pl.BlockSpec((1, tk, dv), lambda g, kj, qi: (g, kj, 0))) + (exchange.out_specs if hosted else ()),
        scratch_shapes=[pltpu.VMEM((nq, hpg * dq, tq), F32)] + (list(exchange.sems) if hosted else []),
        compiler_params=_params(("arbitrary",) * 3 if hosted else ("parallel", "arbitrary", "arbitrary")),
    )(*args)


def loss_head(x, target, fnw):
    s, d = x.shape
    tm = min(ROW_TILE, s)

    def body(x_ref, t_ref, w_ref, lp_ref, dx_ref, dw_ref):
        @pl.when(pl.program_id(0) == 0)
        def _():
            lp_ref[...] = jnp.zeros(lp_ref.shape, F32)
            dw_ref[...] = jnp.zeros(dw_ref.shape, F32)

        x = x_ref[...]
        g = w_ref[...]
        err = x * _rms(x) * g - t_ref[...]
        lp_ref[...] += jnp.sum(err * err, axis=0, keepdims=True)
        dx, dg = _rms_bwd(err * (1.0 / d), x, g)
        dx_ref[...] = dx
        dw_ref[...] += jnp.sum(dg, axis=0, keepdims=True)

    sd = jax.ShapeDtypeStruct
    return pl.pallas_call(
        body, name="loss_head", grid=(s // tm,),
        out_shape=(sd((1, d), F32), sd((s, d), F32), sd((1, d), F32)),
        in_specs=[_row_spec(tm, d), _row_spec(tm, d), _full_spec(fnw.shape)],
        out_specs=(_full_spec((1, d)), _row_spec(tm, d), _full_spec((1, d))),
        compiler_params=_params(("arbitrary",)),
    )(x, target, fnw)


def mixer_out_backward(dx, y, mod, pairs, w_out, delta_heads, name, lse=None, sink=None):
    s, d = dx.shape
    tm = min(ROW_TILE, s)
    n = len(pairs)
    widths = [o.shape[1] for o, _ in pairs]
    n_delta = sum(1 for h in delta_heads if h)
    with_sink = lse is not None

    def body(*refs):
        it = iter(refs)
        dx_ref, y_ref, mod_ref, wt_ref = next(it), next(it), next(it), next(it)
        pr = [next(it) for _ in range(2 * n)]
        lse_ref = next(it) if with_sink else None
        sink_ref = next(it) if with_sink else None
        outs = [next(it) for _ in range(2 * n)]
        dl_refs = [next(it) for _ in range(n_delta)]
        dgate_ref, dw_ref = next(it), next(it)
        dsink_ref = next(it) if with_sink else None

        @pl.when(pl.program_id(0) == 0)
        def _():
            dgate_ref[...] = jnp.zeros(dgate_ref.shape, F32)
            dw_ref[...] = jnp.zeros(dw_ref.shape, F32)
            if with_sink:
                dsink_ref[...] = jnp.zeros(dsink_ref.shape, F32)

        dxo = dx_ref[...]
        dgate_ref[...] += jnp.sum(dxo * y_ref[...], axis=0, keepdims=True)
        dy = (dxo * mod_ref[2:3, :]).astype(MXU)
        dmix = _mm_nt(dy, wt_ref[...])
        r0 = 0
        di = 0
        for i in range(n):
            o = pr[2 * i][...]
            g = pr[2 * i + 1][...]
            dm = dmix[:, r0:r0 + widths[i]]
            sg = _sigmoid(g)
            act = g * sg
            do = dm * act
            outs[2 * i][...] = do.astype(MXU)
            outs[2 * i + 1][...] = (dm * o * (sg * (1.0 + g * (1.0 - sg)))).astype(MXU)
            dw_ref[r0:r0 + widths[i], :] += _mm_tn(o * act, dy)
            if delta_heads[i]:
                dlt = _group_sums_t(do * o, HD)[0:delta_heads[i], :]
                dl_refs[di][...] = dlt
                if with_sink:
                    ps = jnp.exp2(sink_ref[...] - lse_ref[...])
                    dsink_ref[...] += -jnp.sum(ps * dlt, axis=1, keepdims=True)
                di += 1
            r0 += widths[i]

    flat = [a for p in pairs for a in p]
    sd = jax.ShapeDtypeStruct
    in_specs = [_row_spec(tm, d), _row_spec(tm, d), _full_spec(mod.shape), _full_spec(w_out.shape)]
    in_specs += [_row_spec(tm, a.shape[1]) for a in flat]
    args = [dx, y, mod, w_out] + flat
    if with_sink:
        nh = lse.shape[0]
        in_specs += [_rows_spec(nh, tm), _full_spec(sink.shape)]
        args += [lse, sink]
    out_shape = [sd((s, a.shape[1]), MXU) for a in flat]
    out_specs = [_row_spec(tm, a.shape[1]) for a in flat]
    for h in delta_heads:
        if h:
            out_shape.append(sd((h, s), F32))
            out_specs.append(_rows_spec(h, tm))
    out_shape += [sd((1, d), F32), sd((sum(widths), d), F32)]
    out_specs += [_full_spec((1, d)), _full_spec((sum(widths), d))]
    if with_sink:
        out_shape.append(sd((lse.shape[0], 1), F32))
        out_specs.append(_full_spec((lse.shape[0], 1)))
    return pl.pallas_call(
        body, name=name, grid=(s // tm,), out_shape=tuple(out_shape), in_specs=in_specs, out_specs=tuple(out_specs),
        compiler_params=_params(("arbitrary",)),
    )(*args)


def latent_out_backward(d_ob, o_lat, w_uv):
    s = o_lat.shape[0]
    tm = min(ROW_TILE, s)

    def body(d_ref, o_ref, uv_ref, dol_ref, dl_ref, duv_ref, prod_s):
        @pl.when(pl.program_id(0) == 0)
        def _():
            duv_ref[...] = jnp.zeros(duv_ref.shape, F32)

        for hh in range(B_HEADS):
            dh = d_ref[:, HD * hh:HD * hh + HD]
            ol = o_ref[:, B_KV_LORA * hh:B_KV_LORA * (hh + 1)]
            dol = _mm_nt(dh, uv_ref[hh])
            dol_ref[:, B_KV_LORA * hh:B_KV_LORA * (hh + 1)] = dol.astype(MXU)
            prod_s[:, B_KV_LORA * hh:B_KV_LORA * (hh + 1)] = dol * ol
            duv_ref[hh] += _mm_tn(ol, dh)
        dl_ref[...] = _group_sums_t(prod_s[...], B_KV_LORA)[0:B_HEADS, :]

    sd = jax.ShapeDtypeStruct
    return pl.pallas_call(
        body, name="latent_out_backward", grid=(s // tm,),
        out_shape=(sd(o_lat.shape, MXU), sd((B_HEADS, s), F32), sd(w_uv.shape, F32)),
        in_specs=[_row_spec(tm, d_ob.shape[1]), _row_spec(tm, o_lat.shape[1]), _full_spec(w_uv.shape)],
        out_specs=(_row_spec(tm, o_lat.shape[1]), _rows_spec(B_HEADS, tm), _full_spec(w_uv.shape)),
        scratch_shapes=[pltpu.VMEM((tm, o_lat.shape[1]), F32)],
        compiler_params=_params(("arbitrary",)),
    )(d_ob, o_lat, w_uv)


def even_prep_backward(dqa, dka, dva, dqb, dkb, dvb, qa_raw, ka_raw, cq_raw, ckv_raw,
                       gq, gk, qln, kvln, w_uq_t, uk_bd, bd, cos_a, sin_a, cos_t, sin_t):
    s = qa_raw.shape[0]
    tm = min(ROW_TILE, s)
    half_lat = B_KV_LORA * B_HEADS // 2
    half_w = dqb.shape[1] // 2

    def body(dqa_ref, dka_ref, dva_ref, dqb_ref, dkb_ref, dvb_ref, qa_ref, ka_ref, cq_ref, ckv_ref,
             gq_ref, gk_ref, qln_ref, kvln_ref, uqt_ref, ukbd_ref, bd_ref, ca_ref, sa_ref, ct_ref, st_ref,
             pqa, pka, pva, pcq, pckv, pkr, gqn, gkn, gqln, gkvln, guq, guk):
        @pl.when(pl.program_id(0) == 0)
        def _():
            for r in (gqn, gkn, gqln, gkvln, guq, guk):
                r[...] = jnp.zeros(r.shape, F32)

        ca, sa, ct, st = ca_ref[...], sa_ref[...], ct_ref[...], st_ref[...]
        wide = lambda t, n: jnp.concatenate([t] * n, axis=1)
        rows = lambda a: jnp.sum(a, axis=0, keepdims=True)
        dx, dg = _head_norm_bwd(_rope_t(dqa_ref[...], wide(ca, 4), wide(sa, 4), 32), qa_ref[...], gq_ref[...],
                                bd_ref, HD)
        pqa[...] = dx.astype(MXU)
        gqn[...] += rows(dg)
        dk_all = jnp.concatenate([dka_ref[g] for g in range(A_KV)], axis=1)
        dx, dg = _head_norm_bwd(_rope_t(dk_all, ca, sa, 32), ka_ref[...], gk_ref[...], bd_ref[0:128, 0:128], HD)
        pka[...] = dx.astype(MXU)
        gkn[...] += rows(dg)
        pva[...] = jnp.concatenate([dva_ref[g] for g in range(A_KV)], axis=1).astype(MXU)
        cq_raw = cq_ref[...]
        cq_n = cq_raw * _rms(cq_raw) * qln_ref[...]
        qb = _mm_nt(cq_n, uqt_ref[...])
        d_lat = jnp.concatenate([dqb_ref[:, 0:half_lat], dqb_ref[:, half_w:half_w + half_lat]], axis=1)
        d_rope = jnp.concatenate([dqb_ref[:, half_lat:half_w], dqb_ref[:, half_w + half_lat:]], axis=1)
        for hh in range(B_HEADS):
            guk[hh] += _mm_tn(d_lat[:, B_KV_LORA * hh:B_KV_LORA * (hh + 1)], qb[:, B_NOPE * hh:B_NOPE * (hh + 1)])
        dqb_all = jnp.concatenate([_mm_nt(d_lat, ukbd_ref[...]),
                                   _rope_t(d_rope, wide(ct, 2), wide(st, 2), 32)], axis=1)
        guq[...] += _mm_tn(dqb_all, cq_n)
        dx, dg = _rms_bwd(_mm(dqb_all, uqt_ref[...]), cq_raw, qln_ref[...])
        pcq[...] = dx.astype(MXU)
        gqln[...] += rows(dg)
        dkb_sum = dkb_ref[0] + dkb_ref[1]
        dckv = dkb_sum[:, 0:B_KV_LORA] + dvb_ref[0] + dvb_ref[1]
        dx, dg = _rms_bwd(dckv, ckv_ref[...], kvln_ref[...])
        pckv[...] = dx.astype(MXU)
        gkvln[...] += rows(dg)
        pkr[...] = _rope_t(dkb_sum[:, B_KV_LORA:B_QK], ct[:, 0:B_ROPE], st[:, 0:B_ROPE], 32).astype(MXU)

    sd = jax.ShapeDtypeStruct
    consts = [gq, gk, qln, kvln, w_uq_t, uk_bd, bd]
    in_specs = [_row_spec(tm, 512), _head_spec(A_KV, tm, HD), _head_spec(A_KV, tm, HD),
                _row_spec(tm, dqb.shape[1]), _head_spec(2, tm, B_QK), _head_spec(2, tm, B_KV_LORA),
                _row_spec(tm, 512), _row_spec(tm, 128), _row_spec(tm, B_Q_LORA), _row_spec(tm, B_KV_LORA)]
    in_specs += [_full_spec(a.shape) for a in consts] + [_row_spec(tm, 128)] * 4
    small = [sd(gq.shape, F32), sd(gk.shape, F32), sd(qln.shape, F32), sd(kvln.shape, F32), sd(w_uq_t.shape, F32),
             sd((B_HEADS, B_KV_LORA, B_NOPE), F32)]
    out_shape = (sd((s, 512), MXU), sd((s, 128), MXU), sd((s, 128), MXU), sd((s, B_Q_LORA), MXU),
                 sd((s, B_KV_LORA), MXU), sd((s, B_ROPE), MXU), *small)
    out_specs = (_row_spec(tm, 512), _row_spec(tm, 128), _row_spec(tm, 128), _row_spec(tm, B_Q_LORA),
                 _row_spec(tm, B_KV_LORA), _row_spec(tm, B_ROPE), *[_full_spec(a.shape) for a in small])
    return pl.pallas_call(
        body, name="even_prep_backward", grid=(s // tm,), out_shape=out_shape, in_specs=in_specs, out_specs=out_specs,
        compiler_params=_params(("arbitrary",)),
    )(dqa, dka, dva, dqb, dkb, dvb, qa_raw, ka_raw, cq_raw, ckv_raw, *consts, cos_a, sin_a, cos_t, sin_t)


def in_proj_backward(x, mod, nw, pieces, name, *, dx_out=None, w_in_t=None, dw_rows=None, exchange=None):
    s, d = x.shape
    tm = min(ROW_TILE, s)
    grid = (s // tm,)
    n = len(pieces)
    cols = [c for _, c in pieces]
    want_dx = w_in_t is not None
    want_dw = dw_rows is not None
    n_cols = sum(c1 - c0 for c0, c1 in cols)
    hosted = exchange is not None
    nx = exchange.n if hosted else 0

    def body(*refs):
        it = iter(refs)
        x_ref, mod_ref, nw_ref = next(it), next(it), next(it)
        dxo_ref, wt_ref = (next(it), next(it)) if want_dx else (None, None)
        p_refs = [next(it) for _ in range(n)]
        xs_refs = [next(it) for _ in range(nx)]
        dx_ref, dv_ref = (next(it), next(it)) if want_dx else (None, None)
        dw_ref = next(it) if want_dw else None
        land_refs = [next(it) for _ in range(nx)]
        acc_ref = next(it) if want_dx else None
        dw_acc = next(it) if want_dw else None
        sems = list(it)
        first, last = _grid_edges(grid)
        if hosted:
            pl.when(first)(lambda: exchange.start(xs_refs, land_refs, sems))

        @pl.when(first)
        def _():
            if want_dw:
                dw_acc[...] = jnp.zeros(dw_acc.shape, F32)
            if want_dx:
                acc_ref[...] = jnp.zeros(acc_ref.shape, F32)

        xn, g1, h = _modulated(x_ref[...], mod_ref, nw_ref)
        hb = h.astype(MXU)
        dh = jnp.zeros((tm, d), F32)
        for k, (pr, (c0, c1)) in enumerate(zip(p_refs, cols)):
            pc = pr[...].astype(MXU)
            if want_dx:
                dh = dh + jnp.dot(pc, wt_ref[c0:c1, :], preferred_element_type=F32)
            if want_dw:
                r0, r1 = dw_rows[k]
                dw_acc[r0:r1, :] += _mm_tn(pc, hb)
        if want_dx:
            acc_ref[0:1, :] += jnp.sum(dh, axis=0, keepdims=True)
            acc_ref[1:2, :] += jnp.sum(dh * xn, axis=0, keepdims=True)
            dxn = dh * g1
            x = x_ref[...]
            dx_ref[...] = dxo_ref[...] + _rms(x) * (dxn - xn * jnp.mean(dxn * xn, axis=-1, keepdims=True))

        @pl.when(last)
        def _():
            if want_dx:
                dg1 = acc_ref[1:2, :]
                dv_ref[0:1, :] = acc_ref[0:1, :]
                dv_ref[1:2, :] = dg1 * nw_ref[...]
                dv_ref[2:3, :] = dg1 * (1.0 + mod_ref[1:2, :])
                dv_ref[3:4, :] = jnp.zeros((1, d), F32)
            if want_dw:
                dw_ref[...] = dw_acc[...].astype(MXU)

        if hosted:
            pl.when(last)(lambda: exchange.wait(xs_refs, land_refs, sems))

    arrs = [a for a, _ in pieces]
    sd = jax.ShapeDtypeStruct
    args = [x, mod, nw] + ([dx_out, w_in_t] if want_dx else []) + arrs + (exchange.srcs if hosted else [])
    in_specs = [_row_spec(tm, d), _full_spec(mod.shape), _full_spec(nw.shape)]
    in_specs += [_row_spec(tm, d), _full_spec(w_in_t.shape)] if want_dx else []
    in_specs += [_row_spec(tm, a.shape[1]) for a in arrs] + (exchange.in_specs if hosted else [])
    out_shape, out_specs, scratch = [], [], []
    if want_dx:
        out_shape += [sd((s, d), F32), sd((4, d), F32)]
        out_specs += [_row_spec(tm, d), _full_spec((4, d))]
        scratch.append(pltpu.VMEM((8, d), F32))
    if want_dw:
        out_shape.append(sd((n_cols, d), MXU))
        out_specs.append(_full_spec((n_cols, d)))
        scratch.append(pltpu.VMEM((n_cols, d), F32))
    if hosted:
        out_shape += list(exchange.land_shapes)
        out_specs += list(exchange.out_specs)
        scratch += list(exchange.sems)
    return pl.pallas_call(
        body, name=name, grid=grid, out_shape=tuple(out_shape), in_specs=in_specs, out_specs=tuple(out_specs),
        scratch_shapes=scratch, compiler_params=_params(("arbitrary",)),
    )(*args)


def ada_weight_grad(c_all, dmod_cols):
    d = c_all.shape[1]
    w = dmod_cols.shape[2]

    def body(c_ref, dm_ref, out_ref):
        ca = _silu(c_ref[...])
        for l in range(2):
            out_ref[l] = _mm_tn(ca, dm_ref[l])

    return pl.pallas_call(
        body, name="ada_weight_grad",
        out_shape=jax.ShapeDtypeStruct((2, d, w), F32),
        compiler_params=pltpu.CompilerParams(vmem_limit_bytes=VMEM_LIMIT),
    )(c_all, dmod_cols)


def _slot_sum(g_ref):
    g = g_ref[0].astype(F32)
    for k in range(1, g_ref.shape[0]):
        g = g + g_ref[k].astype(F32)
    return g


def _adamw_math(g, w, m, v):
    m_new = ADAM_B1 * m + (1.0 - ADAM_B1) * g
    v_new = ADAM_B2 * v + (1.0 - ADAM_B2) * (g * g)
    m_hat = m_new / (1.0 - ADAM_B1 ** ADAM_STEP)
    v_hat = v_new / (1.0 - ADAM_B2 ** ADAM_STEP)
    return -ADAM_LR * (m_hat / (jnp.sqrt(v_hat) + ADAM_EPS) + ADAM_WD * w), m_new, v_new


def adamw_small(g_alls, ws, ms, vs, loss_all):
    n = len(ws)

    def body(*refs):
        g_refs, w_refs, m_refs, v_refs = (refs[i * n:(i + 1) * n] for i in range(4))
        loss_ref = refs[4 * n]
        outs = refs[4 * n + 1:]
        for i in range(n):
            g = _slot_sum(g_refs[i])
            outs[i][...] = g
            outs[n + i][...], outs[2 * n + i][...], outs[3 * n + i][...] = _adamw_math(
                g, w_refs[i][...], m_refs[i][...], v_refs[i][...])
        outs[4 * n][...] = _slot_sum(loss_ref)

    sds = [jax.ShapeDtypeStruct(w.shape, F32) for w in ws]
    res = pl.pallas_call(
        body, name="adamw_small", out_shape=tuple(sds * 4) + (jax.ShapeDtypeStruct(loss_all.shape[1:], F32),),
        compiler_params=pltpu.CompilerParams(vmem_limit_bytes=VMEM_LIMIT),
    )(*g_alls, *ws, *ms, *vs, loss_all)
    return [res[i * n:(i + 1) * n] for i in range(4)], res[4 * n]


def adamw_rows(g_slots, w, m, v, name):
    n, r, lanes = g_slots.shape
    fits = [t for t in range(16, r + 1, 16) if r % t == 0 and t * lanes <= ADAM_TILE]
    tr = max(fits) if fits else r
    def body(g_ref, w_ref, m_ref, v_ref, go, do, mo, vo):
        g = _slot_sum(g_ref)
        go[...] = g
        do[...], mo[...], vo[...] = _adamw_math(g, w_ref[...], m_ref[...], v_ref[...])

    row = pl.BlockSpec((tr, lanes), lambda i: (i, 0))
    sd = jax.ShapeDtypeStruct((r, lanes), F32)
    return pl.pallas_call(
        body, name=name, grid=(r // tr,), out_shape=(sd, sd, sd, sd),
        in_specs=[pl.BlockSpec((n, tr, lanes), lambda i: (0, i, 0)), row, row, row],
        out_specs=(row, row, row, row),
        compiler_params=_params(("parallel",)),
    )(g_slots, w, m, v)


def _rope_tables(s):
    def cs(pos, dim):
        inv = ROPE_THETA ** (-np.arange(0, dim, 2, dtype=np.float32) / dim)
        ang = pos.astype(np.float32)[:, None] * inv.astype(np.float32)[None, :]
        return np.cos(ang), np.sin(ang)

    rows = s // GRID_W
    row = np.repeat(np.arange(rows), GRID_W)
    col = np.tile(np.arange(GRID_W), rows)
    cr, sr = cs(row, HD // 2)
    cc, sc = cs(col, HD // 2)
    ct, st = cs(np.arange(s), B_ROPE)
    tables = (np.concatenate([cr, cr, cc, cc] * 2, axis=-1), np.concatenate([-sr, sr, -sc, sc] * 2, axis=-1),
              np.concatenate([ct, ct] * 4, axis=-1), np.concatenate([-st, st] * 4, axis=-1))
    return tuple(jnp.asarray(t, F32) for t in tables)


def _even_rows_to_kernel(wt):
    return jnp.concatenate([wt[:1664], wt[1696:], wt[1664:1696]], axis=0)


def _uq_rows_to_kernel(wt):
    r = wt.reshape(B_HEADS, B_NOPE + B_ROPE, -1)
    return jnp.concatenate([r[:, :B_NOPE].reshape(B_HEADS * B_NOPE, -1), r[:, B_NOPE:].reshape(B_HEADS * B_ROPE, -1)])


def _uq_rows_to_reference(wt):
    nope = wt[:B_HEADS * B_NOPE].reshape(B_HEADS, B_NOPE, -1)
    rope = wt[B_HEADS * B_NOPE:].reshape(B_HEADS, B_ROPE, -1)
    return jnp.concatenate([nope, rope], axis=1).reshape(B_HEADS * (B_NOPE + B_ROPE), -1)


def _shard_t(w):
    return jnp.transpose(w[0])


def _unshard_t(wt, like):
    return jnp.transpose(wt)[None].reshape(like.shape)


def kernel(x, c, norm_w, ada_w, ada_b, even_w_in, a_q_norm, a_k_norm, b_q_lora_norm, b_kv_lora_norm, b_w_uq, b_w_uk, b_w_uv, even_w_out, odd_w_in, c_sink, odd_w_out, final_norm, loss_target, m_norm_w, m_ada_w, m_ada_b, m_even_w_in, m_a_q_norm, m_a_k_norm, m_b_q_lora_norm, m_b_kv_lora_norm, m_b_w_uq, m_b_w_uk, m_b_w_uv, m_even_w_out, m_odd_w_in, m_c_sink, m_odd_w_out, m_final_norm, v_norm_w, v_ada_w, v_ada_b, v_even_w_in, v_a_q_norm, v_a_k_norm, v_b_q_lora_norm, v_b_kv_lora_norm, v_b_w_uq, v_b_w_uk, v_b_w_uv, v_even_w_out, v_odd_w_in, v_c_sink, v_odd_w_out, v_final_norm):
    s, d = x.shape[1], x.shape[2]
    x0 = x[0]
    target = loss_target[0]
    me_flat = 4 * lax.axis_index("x") + 2 * lax.axis_index("y") + lax.axis_index("c")

    wcols = ada_w.shape[2]
    bias_cols = lax.dynamic_slice_in_dim(ada_b.reshape(2, N_DEV, wcols), me_flat, 1, axis=1)
    call, modp, (g_in_e, g_uq) = ada_forward(
        jnp.broadcast_to(c, (8, d)), ada_w, bias_cols,
        Gather([_shard_t(even_w_in).astype(MXU), _shard_t(b_w_uq).astype(MXU)]))
    wt_in_e = _even_rows_to_kernel(g_in_e.reshape(-1, d))
    wt_uq = _uq_rows_to_kernel(g_uq.reshape(-1, B_Q_LORA))
    later_exchange = Exchange([_shard_t(odd_w_in).astype(MXU), even_w_out[0].astype(MXU),
                               odd_w_out[0].astype(MXU)], scatter=False)
    uk_bd = (jnp.eye(B_HEADS, dtype=F32)[:, None, :, None] * jnp.transpose(b_w_uk[0], (1, 2, 0))[:, :, None, :]
             ).reshape(B_HEADS * B_NOPE, B_HEADS * B_KV_LORA).astype(MXU)
    head_bd = jnp.asarray(np.kron(np.eye(A_HEADS), np.ones((HD, HD))), MXU)
    gq_full, gk_full = jnp.tile(a_q_norm, (1, A_HEADS)), jnp.tile(a_k_norm, (1, A_KV))
    w_uv = jnp.transpose(b_w_uv[0], (1, 0, 2)).astype(MXU)

    c_all = call[:, 0, :]
    mod = jnp.transpose(modp[:, :, 0, :], (1, 0, 2)).reshape(2, 3, d)
    mod_e, mod_o = mod[0], mod[1]
    nw_e, nw_o = norm_w[0:1], norm_w[1:2]

    cos_a, sin_a, cos_t, sin_t = _rope_tables(s)
    slopes = (2.0 ** (-8.0 * jnp.arange(1, C_HEADS + 1, dtype=F32) / C_HEADS)).reshape(C_HEADS, 1, 1)
    sink2 = c_sink.reshape(C_HEADS, 1, 1) * LOG2E

    (qa, ka, va, qb, kb, kat, vat, kbt, qa_raw, ka_raw, cq_raw, ckv_raw, ga, gb) = even_in_forward(
        x0, mod_e, nw_e, wt_in_e, gq_full, gk_full, b_q_lora_norm, b_kv_lora_norm, wt_uq, uk_bd, head_bd,
        cos_a, sin_a, cos_t, sin_t)
    tk_dense = min(512, s)
    tq_dense = min(256, s)
    fwd_sub = min(8, s // tk_dense)
    bwd_sub = min(4, s // tq_dense)
    oa, lse_a, g_in_o, g_out_e, g_out_o = flash_forward(
        qa, ka, vat, dv=HD, tq=tq_dense, tk=tk_dense, nsub=fwd_sub, name="attn_a_fwd",
        exchange=later_exchange)
    wt_in_o = g_in_o.reshape(-1, d)
    w_out_e = g_out_e.reshape(-1, d)
    w_out_o = g_out_o.reshape(-1, d)
    o_lat, lse_b = flash_forward(qb, kb, kbt, dv=B_KV_LORA, tq=min(128, s), tk=tk_dense, nsub=fwd_sub,
                                 name="attn_b_fwd")
    ob = latent_out_forward(o_lat, w_uv)
    x1, y_e = mixer_out_forward(x0, mod_e, [(oa, ga), (ob, gb)], w_out_e, "even_out_fwd")

    qc, kc, vc, kct, vct, gc = odd_in_forward(x1, mod_o, nw_o, wt_in_o)
    win_sub = min(8, s // WINDOW)
    oc, lse_c = window_forward(qc, kc, vct, sink2, slopes, win_sub, "attn_c_fwd")
    x2, y_o = mixer_out_forward(x1, mod_o, [(oc, gc)], w_out_o, "odd_out_fwd")

    loss_lanes, dx2, d_final = loss_head(x2, target, final_norm.reshape(1, d))
    loss_part = (0.5 / d) * jnp.sum(loss_lanes)

    doc, dgc, delta_c, dgate_o, dw_out_o, dsink = mixer_out_backward(
        dx2, y_o, mod_o, [(oc, gc)], w_out_o, [C_HEADS], "odd_out_bwd", lse=lse_c.reshape(C_HEADS, s),
        sink=sink2.reshape(C_HEADS, 1))
    rows3 = lambda t: t.reshape(t.shape[0], 1, s)
    dqc, dkc, dvc = window_backward(qc, kc, kct, vc, doc, lse_c, rows3(delta_c), slopes, win_sub, "attn_c_bwd")
    to_rows = lambda t: jnp.transpose(t, (1, 0, 2)).reshape(s, -1)
    dx1, dvec_o, dwt_in_o = in_proj_backward(
        x1, mod_o, nw_o, [(dqc, O_Q), (to_rows(dkc), O_K), (to_rows(dvc), O_V), (dgc, O_G)], "odd_in_bwd",
        dx_out=dx2, w_in_t=wt_in_o, dw_rows=[O_Q, O_K, O_V, O_G])

    doa, dga, dob, dgb, delta_a, dgate_e, dw_out_e = mixer_out_backward(
        dx1, y_e, mod_e, [(oa, ga), (ob, gb)], w_out_e, [A_HEADS, 0], "even_out_bwd")
    d_olat, delta_b, dw_uv = latent_out_backward(dob, o_lat, w_uv)
    blocks = lambda g: g.astype(MXU).reshape(N_DEV, g.shape[0] // N_DEV, g.shape[1])
    even_pieces = lambda: [(pqa, E_QA), (pka, E_KA), (pva, E_VA), (dga, E_GA), (pcq, E_CQ), (pckv, E_CKV),
                           (dgb, E_GB), (pkr, E_KR)]
    scatter_odd = Exchange([blocks(dwt_in_o), blocks(dw_out_o)], True)
    scatter_out_e = Exchange([blocks(dw_out_e)], True)
    dqb, dkb, dvb, l_in_o, l_out_o = flash_backward(
        qb, kb, kbt, None, d_olat, lse_b, rows3(delta_b), scale=SCALE_B, dv=B_KV_LORA,
        tq=tq_dense, tk=tk_dense, nsub=bwd_sub, gq=2, name="attn_b_bwd", split=B_KV_LORA, exchange=scatter_odd)
    dqa, dka, dva, l_out_e = flash_backward(
        qa, ka, kat, va, doa, lse_a, rows3(delta_a), scale=SCALE_A, dv=HD,
        tq=tq_dense, tk=tk_dense, nsub=bwd_sub, gq=A_KV, name="attn_a_bwd", exchange=scatter_out_e)
    (pqa, pka, pva, pcq, pckv, pkr, g_qn, g_kn, g_qln, g_kvln, dwt_uq, dw_uk) = even_prep_backward(
        dqa, dka, dva, dqb, dkb, dvb, qa_raw, ka_raw, cq_raw, ckv_raw,
        gq_full, gk_full, b_q_lora_norm, b_kv_lora_norm, wt_uq, uk_bd, head_bd, cos_a, sin_a, cos_t, sin_t)
    g_qn = jnp.sum(g_qn.reshape(A_HEADS, HD), axis=0)
    g_kn = jnp.sum(g_kn.reshape(A_KV, HD), axis=0)
    (dwt_in_e,) = in_proj_backward(
        x0, mod_e, nw_e, even_pieces(), "even_in_bwd_dw",
        dw_rows=[E_QA, E_KA, E_VA, E_GA, E_CQ, E_CKV, (1696, 2208), (1664, 1696)])
    dx0, dvec_e, l_in_e, l_uq = in_proj_backward(
        x0, mod_e, nw_e, even_pieces(), "even_in_bwd_dx", dx_out=dx1, w_in_t=wt_in_e,
        exchange=Exchange([blocks(dwt_in_e), blocks(_uq_rows_to_reference(dwt_uq))], True))

    dmod = jnp.stack([jnp.concatenate([dvec_e[0], dvec_e[1], dgate_e[0]]),
                      jnp.concatenate([dvec_o[0], dvec_o[1], dgate_o[0]])])
    d_norm_w = jnp.stack([dvec_e[2], dvec_o[2]])
    small_names = ["norm_w", "ada_b", "a_q_norm", "a_k_norm", "b_q_lora_norm", "b_kv_lora_norm", "b_w_uk", "b_w_uv",
                   "c_sink", "final_norm"]
    small_w = [norm_w, ada_b, a_q_norm, a_k_norm, b_q_lora_norm, b_kv_lora_norm, b_w_uk, b_w_uv, c_sink, final_norm]
    small_m = [m_norm_w, m_ada_b, m_a_q_norm, m_a_k_norm, m_b_q_lora_norm, m_b_kv_lora_norm, m_b_w_uk, m_b_w_uv,
               m_c_sink, m_final_norm]
    small_v = [v_norm_w, v_ada_b, v_a_q_norm, v_a_k_norm, v_b_q_lora_norm, v_b_kv_lora_norm, v_b_w_uk, v_b_w_uv,
               v_c_sink, v_final_norm]
    small_g = [d_norm_w, dmod, g_qn, g_kn, g_qln, g_kvln, jnp.transpose(dw_uk, (1, 0, 2)), jnp.transpose(dw_uv, (1, 0, 2)),
               dsink, d_final]
    flat2 = lambda a: a.reshape((1, -1)) if a.size == a.shape[-1] else a.reshape(a.shape[-3:] if a.ndim > 3 else a.shape)
    kshape = [flat2(w).shape for w in small_w]
    g_all = all_gather_slots(
        Gather([g.reshape(sh) for g, sh in zip(small_g, kshape)] + [jnp.full((8, 128), loss_part, F32)]),
        "gather_small_grads")
    sm_out, loss_sum = adamw_small(g_all[:-1], [flat2(a) for a in small_w], [flat2(a) for a in small_m],
                                   [flat2(a) for a in small_v], g_all[-1])
    loss = loss_sum[0, 0]
    sm = [{nm: p.reshape(w.shape) for nm, w, p in zip(small_names, small_w, outs)} for outs in sm_out]

    dmod_all = g_all[1].reshape(N_DEV, 2, N_DEV, wcols)
    dmod_cols = lax.dynamic_slice_in_dim(dmod_all, me_flat, 1, axis=2)[:, :, 0, :]
    pad16 = lambda a: jnp.concatenate([a, jnp.zeros_like(a)], axis=0)
    g_ada_w = ada_weight_grad(pad16(c_all), jnp.transpose(pad16(dmod_cols), (1, 0, 2)))
    rows_of = lambda a: a.reshape(-1, wcols)
    ada = adamw_rows(rows_of(g_ada_w)[None], rows_of(ada_w), rows_of(m_ada_w), rows_of(v_ada_w), "adamw_ada_w")
    ada = [p.reshape(ada_w.shape) for p in ada]

    bg = [{}, {}, {}, {}]
    for nm, landed, w, m, v, transposed in (
            ("even_w_in", l_in_e, even_w_in, m_even_w_in, v_even_w_in, True),
            ("b_w_uq", l_uq, b_w_uq, m_b_w_uq, v_b_w_uq, True),
            ("odd_w_in", l_in_o, odd_w_in, m_odd_w_in, v_odd_w_in, True),
            ("even_w_out", l_out_e, even_w_out, m_even_w_out, v_even_w_out, False),
            ("odd_w_out", l_out_o, odd_w_out, m_odd_w_out, v_odd_w_out, False)):
        view = _shard_t if transposed else (lambda a: a[0])
        res = adamw_rows(landed, view(w), view(m), view(v), "adamw_" + nm)
        for kind, p in enumerate(res):
            bg[kind][nm] = _unshard_t(p, w) if transposed else p[None]
    big_names = ["even_w_in", "odd_w_in", "even_w_out", "odd_w_out", "b_w_uq"]

    order = ["norm_w", "ada_w", "ada_b", "even_w_in", "a_q_norm", "a_k_norm", "b_q_lora_norm", "b_kv_lora_norm",
             "b_w_uq", "b_w_uk", "b_w_uv", "even_w_out", "odd_w_in", "c_sink", "odd_w_out", "final_norm"]

    def pick(kind):
        out = []
        for nm in order:
            if nm == "ada_w":
                out.append(ada[kind])
            elif nm in big_names:
                out.append(bg[kind][nm])
            else:
                out.append(sm[kind][nm])
        return out

    return (loss, dx0[None], *pick(0), *pick(1), *pick(2), *pick(3))
```

```python
import functools

import jax
import jax.numpy as jnp
import numpy as np
from jax import lax
from jax.experimental import pallas as pl
from jax.experimental.pallas import tpu as pltpu

F32 = jnp.float32
MXU = jnp.bfloat16
EPS = 1e-6
ROPE_THETA = 10000.0
GRID_W = 64
HD = 64
N_DEV = 8

A_HEADS, A_KV = 8, 2
B_HEADS, B_NOPE, B_ROPE, B_Q_LORA, B_KV_LORA = 8, 64, 32, 256, 128
B_QK = B_KV_LORA + B_ROPE
C_HEADS, C_KV = 16, 4
WINDOW = 128

ADAM_LR, ADAM_B1, ADAM_B2, ADAM_EPS, ADAM_WD, ADAM_STEP = 0.001, 0.9, 0.999, 1e-08, 0.01, 10

ROW_TILE = 512
ADAM_TILE = 2048 * 128

LOG2E = 1.4426950408889634
SCALE_A = HD ** -0.5
SCALE_B = (B_NOPE + B_ROPE) ** -0.5
SCALE2_A, SCALE2_B = SCALE_A * LOG2E, SCALE_B * LOG2E
VMEM_LIMIT = 56 * 1024 * 1024

E_QA, E_KA, E_VA, E_GA, E_CQ, E_CKV, E_GB, E_KR = (
    (0, 512), (512, 640), (640, 768), (768, 1280), (1280, 1536), (1536, 1664), (1664, 2176), (2176, 2208))
O_Q, O_K, O_V, O_G = (0, 1024), (1024, 1280), (1280, 1536), (1536, 2560)


def _mm(a, b):
    return jnp.dot(a.astype(MXU), b.astype(MXU), preferred_element_type=F32)


def _mm_nt(a, b):
    return lax.dot_general(a.astype(MXU), b.astype(MXU), (((1,), (1,)), ((), ())), preferred_element_type=F32)


def _mm_tn(a, b):
    return lax.dot_general(a.astype(MXU), b.astype(MXU), (((0,), (0,)), ((), ())), preferred_element_type=F32)


def _group_sums_t(prod, group):
    tm, w = prod.shape
    sel = (lax.broadcasted_iota(jnp.int32, (w, 128), 0) // group
           == lax.broadcasted_iota(jnp.int32, (w, 128), 1)).astype(MXU)
    hi = prod.astype(MXU)
    lo = prod - hi.astype(F32)
    return (_mm(hi, sel) + _mm(lo, sel)).T


def _sigmoid(z):
    return 1.0 / (1.0 + jnp.exp(-z))


def _silu(z):
    return z * _sigmoid(z)


def _rms(x):
    return lax.rsqrt(jnp.mean(x * x, axis=-1, keepdims=True) + EPS)


def _swap_halves(y, group):
    n = y.shape[-1]
    half = group // 2
    fwd = pltpu.roll(y, half, 1)
    if n == group:
        return fwd
    back = pltpu.roll(y, n - half, 1)
    lane = lax.broadcasted_iota(jnp.int32, y.shape, 1)
    return jnp.where((lane % group) < half, back, fwd)


def _rope(y, cos, sin, group):
    return y * cos + _swap_halves(y, group) * sin


def _rope_t(d, cos, sin, group):
    return d * cos - _swap_halves(d, group) * sin


def _rms_bwd(dy, x, g):
    r = _rms(x)
    xhat = x * r
    dxhat = dy * g
    dx = r * (dxhat - xhat * jnp.mean(dxhat * xhat, axis=-1, keepdims=True))
    return dx, dy * xhat


def _group_mean(v, bd, group):
    hi = v.astype(MXU)
    lo = v - hi.astype(F32)
    return (_mm(hi, bd[...]) + _mm(lo, bd[...])) * (1.0 / group)


def _head_norm(x, g, bd, group):
    return x * lax.rsqrt(_group_mean(x * x, bd, group) + EPS) * g


def _head_norm_bwd(dy, x, g, bd, group):
    r = lax.rsqrt(_group_mean(x * x, bd, group) + EPS)
    xhat = x * r
    dxhat = dy * g
    dx = r * (dxhat - xhat * _group_mean(dxhat * xhat, bd, group))
    return dx, dy * xhat


def _params(sem, vmem=VMEM_LIMIT):
    return pltpu.CompilerParams(dimension_semantics=sem, vmem_limit_bytes=vmem)


def _row_spec(tm, w):
    return pl.BlockSpec((tm, w), lambda i: (i, 0))


def _full_spec(shape):
    nd = len(shape)
    return pl.BlockSpec(shape, lambda i: (0,) * nd)


def _head_spec(h, tm, w):
    return pl.BlockSpec((h, tm, w), lambda i: (0, i, 0))


def _headt_spec(h, w, tm):
    return pl.BlockSpec((h, w, tm), lambda i: (0, 0, i))


def _rows_spec(h, tm):
    return pl.BlockSpec((h, tm), lambda i: (0, i))


def _me():
    return lax.axis_index("x"), lax.axis_index("y"), lax.axis_index("c")


def _flat(p):
    return 4 * p[0] + 2 * p[1] + p[2]


def _peer(me, k):
    x, y, c = me
    return (1 - x if k & 4 else x, 1 - y if k & 2 else y, 1 - c if k & 1 else c)


MESH_ID = pl.DeviceIdType.MESH


class Gather:
    VMEM = pl.BlockSpec(memory_space=pltpu.VMEM)

    def __init__(self, shards):
        self.shards = list(shards)
        self.n = len(self.shards)
        self.out_shapes = tuple(jax.ShapeDtypeStruct((N_DEV,) + a.shape, a.dtype) for a in self.shards)
        self.in_specs = [Gather.VMEM] * self.n
        self.out_specs = (Gather.VMEM,) * self.n
        self.sems = [pltpu.SemaphoreType.DMA((7 * self.n,)), pltpu.SemaphoreType.DMA((7 * self.n,)),
                     pltpu.SemaphoreType.DMA((self.n,))]

    def _plan(self, x_refs, out_refs, sems):
        send_sems, recv_sems, local_sems = sems
        me = _me()
        x, y, c = me
        chips = [(1 - x, y), (x, 1 - y), (1 - x, 1 - y)]

        def copy(a, k, block, to, src=None):
            slot = out_refs[a].at[_flat(block)]
            return pltpu.make_async_remote_copy(
                src_ref=slot if src is None else src, dst_ref=slot, send_sem=send_sems.at[7 * a + k],
                recv_sem=recv_sems.at[7 * a + k], device_id=to, device_id_type=MESH_ID)

        mine = [pltpu.make_async_copy(x_refs[a], out_refs[a].at[_flat(me)], local_sems.at[a]) for a in range(self.n)]
        first = [copy(a, 0, me, (x, y, 1 - c), src=x_refs[a]) for a in range(self.n)]
        first += [copy(a, 1 + j, me, (*chip, c), src=x_refs[a]) for a in range(self.n) for j, chip in enumerate(chips)]
        return me, chips, copy, mine, first

    def start(self, x_refs, out_refs, sems):
        _, _, _, mine, first = self._plan(x_refs, out_refs, sems)
        for cp in mine + first:
            cp.start()

    def finish(self, x_refs, out_refs, sems):
        me, chips, copy, mine, first = self._plan(x_refs, out_refs, sems)
        x, y, c = me
        sibling = (x, y, 1 - c)
        passed = []
        for a in range(self.n):
            for j, chip in enumerate(chips):
                copy(a, 1 + j, (*chip, c), me).wait_recv()
                passed.append(copy(a, 4 + j, (*chip, c), sibling))
                passed[-1].start()
        for a in range(self.n):
            copy(a, 0, sibling, me).wait_recv()
            for j, chip in enumerate(chips):
                copy(a, 4 + j, (*chip, 1 - c), me).wait_recv()
        for cp in first + passed:
            cp.wait_send()
        for cp in mine:
            cp.wait()


def all_gather_slots(gather, name):
    def body(*refs):
        x_refs, out_refs, sems = refs[:gather.n], refs[gather.n:2 * gather.n], refs[2 * gather.n:]
        gather.start(x_refs, out_refs, sems)
        gather.finish(x_refs, out_refs, sems)

    return pl.pallas_call(
        body, name=name, out_shape=gather.out_shapes, in_specs=gather.in_specs, out_specs=gather.out_specs,
        scratch_shapes=list(gather.sems), compiler_params=pltpu.CompilerParams(vmem_limit_bytes=VMEM_LIMIT),
    )(*gather.shards)


class Exchange:
    HBM = pl.BlockSpec(memory_space=pl.ANY)

    def __init__(self, srcs, scatter):
        self.srcs = list(srcs)
        self.scatter = scatter
        self.n = len(self.srcs)
        self.land_shapes = tuple(jax.ShapeDtypeStruct((N_DEV,) + tuple(a.shape[-2:]), a.dtype) for a in self.srcs)
        self.in_specs = [Exchange.HBM] * self.n
        self.out_specs = (Exchange.HBM,) * self.n
        self.sems = [pltpu.SemaphoreType.DMA((N_DEV - 1,)), pltpu.SemaphoreType.DMA((N_DEV - 1,)),
                     pltpu.SemaphoreType.DMA] * self.n

    def _copies(self, src_refs, land_refs, sems):
        me = _me()
        mi = _flat(me)
        local, sends, recvs = [], [], []
        for a, (src_ref, land_ref) in enumerate(zip(src_refs, land_refs)):
            send_sems, recv_sems, local_sem = sems[3 * a:3 * a + 3]
            pick = (lambda p, r=src_ref: r.at[_flat(p)]) if self.scatter else (lambda p, r=src_ref: r)
            local.append(pltpu.make_async_copy(pick(me), land_ref.at[mi], local_sem))
            for k in range(1, N_DEV):
                peer = _peer(me, k)
                pair = dict(send_sem=send_sems.at[k - 1], recv_sem=recv_sems.at[k - 1], device_id=peer,
                            device_id_type=MESH_ID)
                sends.append(pltpu.make_async_remote_copy(src_ref=pick(peer), dst_ref=land_ref.at[mi], **pair))
                recvs.append(pltpu.make_async_remote_copy(src_ref=pick(peer), dst_ref=land_ref.at[_flat(peer)],
                                                          **pair))
        return local, sends, recvs

    def start(self, src_refs, land_refs, sems):
        local, sends, _ = self._copies(src_refs, land_refs, sems)
        for cp in local + sends:
            cp.start()

    def wait(self, src_refs, land_refs, sems):
        local, sends, recvs = self._copies(src_refs, land_refs, sems)
        for cp in recvs:
            cp.wait_recv()
        for cp in sends:
            cp.wait_send()
        for cp in local:
            cp.wait()


def ada_forward(c8, ada_w, bias_cols, gather):
    d = c8.shape[1]
    w = ada_w.shape[2]
    ng = gather.n

    def body(*refs):
        c_ref, w_ref, b_ref = refs[:3]
        gx_refs = refs[3:3 + ng]
        call_ref, modp_ref = refs[3 + ng:5 + ng]
        gout_refs = refs[5 + ng:5 + 2 * ng]
        part_ref, s1, r1, s2, r2 = refs[5 + 2 * ng:10 + 2 * ng]
        g_sems = refs[10 + 2 * ng:]
        gather.start(gx_refs, gout_refs, g_sems)
        me = _me()
        mi = _flat(me)
        call_ref[mi] = c_ref[...]
        rows_out = []
        for k in range(1, N_DEV):
            rows_out.append(pltpu.make_async_remote_copy(
                src_ref=c_ref, dst_ref=call_ref.at[mi], send_sem=s1.at[k - 1], recv_sem=r1.at[k - 1],
                device_id=_peer(me, k), device_id_type=MESH_ID))
        for cp in rows_out:
            cp.start()
        for k in range(1, N_DEV):
            pltpu.make_async_remote_copy(
                src_ref=c_ref, dst_ref=call_ref.at[_flat(_peer(me, k))], send_sem=s1.at[k - 1],
                recv_sem=r1.at[k - 1], device_id=_peer(me, k), device_id_type=MESH_ID).wait_recv()
        ca = _silu(call_ref[...].reshape(N_DEV * 8, d))
        for l in range(2):
            part = _mm(ca, w_ref[l]) + b_ref[l]
            for b in range(N_DEV):
                part_ref[b, l] = part[8 * b:8 * b + 8, :]
        modp_ref[mi] = part_ref[mi]
        spread = []
        for k in range(1, N_DEV):
            peer = _peer(me, k)
            spread.append(pltpu.make_async_remote_copy(
                src_ref=part_ref.at[_flat(peer)], dst_ref=modp_ref.at[mi], send_sem=s2.at[k - 1],
                recv_sem=r2.at[k - 1], device_id=peer, device_id_type=MESH_ID))
        for cp in spread:
            cp.start()
        for k in range(1, N_DEV):
            pi = _flat(_peer(me, k))
            pltpu.make_async_remote_copy(
                src_ref=part_ref.at[pi], dst_ref=modp_ref.at[pi], send_sem=s2.at[k - 1],
                recv_sem=r2.at[k - 1], device_id=_peer(me, k), device_id_type=MESH_ID).wait_recv()
        for cp in rows_out + spread:
            cp.wait_send()
        gather.finish(gx_refs, gout_refs, g_sems)

    vm = pl.BlockSpec(memory_space=pltpu.VMEM)
    res = pl.pallas_call(
        body, name="ada_forward",
        out_shape=(jax.ShapeDtypeStruct((N_DEV, 8, d), F32), jax.ShapeDtypeStruct((N_DEV, 2, 8, w), F32))
        + gather.out_shapes,
        in_specs=[vm, vm, vm] + gather.in_specs, out_specs=(vm, vm) + gather.out_specs,
        scratch_shapes=[pltpu.VMEM((N_DEV, 2, 8, w), F32)] + [pltpu.SemaphoreType.DMA((7,))] * 4 + list(gather.sems),
        compiler_params=pltpu.CompilerParams(vmem_limit_bytes=VMEM_LIMIT),
    )(c8, ada_w, bias_cols, *gather.shards)
    return res[0], res[1], res[2:]


def _modulated(x, mod_ref, nw_ref):
    xn = x * _rms(x)
    g1 = nw_ref[...] * (1.0 + mod_ref[1:2, :])
    return xn, g1, xn * g1 + mod_ref[0:1, :]


def even_in_forward(x, mod, nw, w_in_t, gq, gk, qln, kvln, w_uq_t, uk_bd, bd, cos_a, sin_a, cos_t, sin_t):
    s, d = x.shape
    tm = min(ROW_TILE, s)
    n_nope = B_HEADS * B_NOPE

    def body(x_ref, mod_ref, nw_ref, w_ref, gq_ref, gk_ref, qln_ref, kvln_ref, uq_ref, ukbd_ref, bd_ref,
             ca_ref, sa_ref, ct_ref, st_ref,
             qa_o, ka_o, va_o, qb_o, kb_o, kat_o, vat_o, kbt_o, qa_raw_o, ka_raw_o, cq_raw_o, ckv_raw_o, ga_o, gb_o):
        _, _, h = _modulated(x_ref[...], mod_ref, nw_ref)
        h = h.astype(MXU)

        def proj(cols):
            return _mm_nt(h, w_ref[cols[0]:cols[1], :])

        ca, sa, ct, st = ca_ref[...], sa_ref[...], ct_ref[...], st_ref[...]
        wide = lambda t, n: jnp.concatenate([t] * n, axis=1)
        qa = proj(E_QA)
        qa_raw_o[...] = qa
        qr = _rope(_head_norm(qa, gq_ref[...], bd_ref, HD), wide(ca, 4), wide(sa, 4), 32) * SCALE2_A
        for hh in range(A_HEADS):
            qa_o[hh] = qr[:, HD * hh:HD * hh + HD].astype(MXU)
        ka = proj(E_KA)
        ka_raw_o[...] = ka
        kr = _rope(_head_norm(ka, gk_ref[...], bd_ref[0:128, 0:128], HD), ca, sa, 32)
        va = proj(E_VA)
        krt, vat = kr.T, va.T
        for g in range(A_KV):
            ka_o[g] = kr[:, HD * g:HD * g + HD].astype(MXU)
            va_o[g] = va[:, HD * g:HD * g + HD].astype(MXU)
            kat_o[g] = krt[HD * g:HD * g + HD, :].astype(MXU)
            vat_o[g] = vat[HD * g:HD * g + HD, :].astype(MXU)
        ga_o[...] = proj(E_GA)
        gb_o[...] = proj(E_GB)
        cq = proj(E_CQ)
        cq_raw_o[...] = cq
        qb = _mm_nt(cq * _rms(cq) * qln_ref[...], uq_ref[...])
        q_lat = _mm(qb[:, 0:n_nope], ukbd_ref[...]) * SCALE2_B
        q_rope = _rope(qb[:, n_nope:], wide(ct, 2), wide(st, 2), 32) * SCALE2_B
        for hh in range(B_HEADS):
            qb_o[hh, :, 0:B_KV_LORA] = q_lat[:, B_KV_LORA * hh:B_KV_LORA * (hh + 1)].astype(MXU)
            qb_o[hh, :, B_KV_LORA:B_QK] = q_rope[:, B_ROPE * hh:B_ROPE * (hh + 1)].astype(MXU)
        ckv = proj(E_CKV)
        ckv_raw_o[...] = ckv
        ckv_n = ckv * _rms(ckv) * kvln_ref[...]
        k_rope = _rope(proj(E_KR), ct[:, 0:B_ROPE], st[:, 0:B_ROPE], 32)
        kb_o[0, :, 0:B_KV_LORA] = ckv_n.astype(MXU)
        kb_o[0, :, B_KV_LORA:B_QK] = k_rope.astype(MXU)
        kbt_o[0, 0:B_KV_LORA, :] = ckv_n.T.astype(MXU)
        kbt_o[0, B_KV_LORA:B_QK, :] = k_rope.T.astype(MXU)

    sd = jax.ShapeDtypeStruct
    outs = (sd((A_HEADS, s, HD), MXU), sd((A_KV, s, HD), MXU), sd((A_KV, s, HD), MXU),
            sd((B_HEADS, s, B_QK), MXU), sd((1, s, B_QK), MXU),
            sd((A_KV, HD, s), MXU), sd((A_KV, HD, s), MXU), sd((1, B_QK, s), MXU),
            sd((s, 512), F32), sd((s, 128), F32), sd((s, B_Q_LORA), F32), sd((s, B_KV_LORA), F32),
            sd((s, 512), F32), sd((s, 512), F32))
    out_specs = (_head_spec(A_HEADS, tm, HD), _head_spec(A_KV, tm, HD), _head_spec(A_KV, tm, HD),
                 _head_spec(B_HEADS, tm, B_QK), _head_spec(1, tm, B_QK),
                 _headt_spec(A_KV, HD, tm), _headt_spec(A_KV, HD, tm), _headt_spec(1, B_QK, tm),
                 _row_spec(tm, 512), _row_spec(tm, 128), _row_spec(tm, B_Q_LORA), _row_spec(tm, B_KV_LORA),
                 _row_spec(tm, 512), _row_spec(tm, 512))
    consts = [mod, nw, w_in_t, gq, gk, qln, kvln, w_uq_t, uk_bd, bd]
    return pl.pallas_call(
        body, name="even_in_forward", grid=(s // tm,), out_shape=outs,
        in_specs=[_row_spec(tm, d)] + [_full_spec(a.shape) for a in consts] + [_row_spec(tm, 128)] * 4,
        out_specs=out_specs, compiler_params=_params(("parallel",)),
    )(x, *consts, cos_a, sin_a, cos_t, sin_t)


def odd_in_forward(x, mod, nw, w_in):
    s, d = x.shape
    tm = min(ROW_TILE, s)

    def body(x_ref, mod_ref, nw_ref, w_ref, q_o, k_o, v_o, kt_o, vt_o, g_o):
        _, _, h = _modulated(x_ref[...], mod_ref, nw_ref)
        h = h.astype(MXU)

        def proj(cols):
            return _mm_nt(h, w_ref[cols[0]:cols[1], :])

        q = proj(O_Q) * SCALE2_A
        for hh in range(C_HEADS):
            q_o[hh] = q[:, HD * hh:HD * hh + HD].astype(MXU)
        k = proj(O_K)
        v = proj(O_V)
        for g in range(C_KV):
            kh = k[:, HD * g:HD * g + HD]
            vh = v[:, HD * g:HD * g + HD]
            k_o[g] = kh.astype(MXU)
            v_o[g] = vh.astype(MXU)
            kt_o[g] = kh.T.astype(MXU)
            vt_o[g] = vh.T.astype(MXU)
        g_o[...] = proj(O_G)

    sd = jax.ShapeDtypeStruct
    return pl.pallas_call(
        body, name="odd_in_forward", grid=(s // tm,),
        out_shape=(sd((C_HEADS, s, HD), MXU), sd((C_KV, s, HD), MXU), sd((C_KV, s, HD), MXU),
                   sd((C_KV, HD, s), MXU), sd((C_KV, HD, s), MXU), sd((s, 1024), F32)),
        in_specs=[_row_spec(tm, d), _full_spec(mod.shape), _full_spec(nw.shape), _full_spec(w_in.shape)],
        out_specs=(_head_spec(C_HEADS, tm, HD), _head_spec(C_KV, tm, HD), _head_spec(C_KV, tm, HD),
                   _headt_spec(C_KV, HD, tm), _headt_spec(C_KV, HD, tm), _row_spec(tm, 1024)),
        compiler_params=_params(("parallel",)),
    )(x, mod, nw, w_in)


def latent_out_forward(o_lat, w_uv):
    s = o_lat.shape[0]
    tm = min(ROW_TILE, s)

    def body(o_ref, uv_ref, out_ref):
        for hh in range(B_HEADS):
            out_ref[:, HD * hh:HD * hh + HD] = _mm(o_ref[:, B_KV_LORA * hh:B_KV_LORA * (hh + 1)], uv_ref[hh])

    return pl.pallas_call(
        body, name="latent_out_forward", grid=(s // tm,),
        out_shape=jax.ShapeDtypeStruct((s, B_HEADS * HD), F32),
        in_specs=[_row_spec(tm, o_lat.shape[1]), _full_spec(w_uv.shape)],
        out_specs=_row_spec(tm, B_HEADS * HD),
        compiler_params=_params(("parallel",)),
    )(o_lat, w_uv)


def mixer_out_forward(x, mod, pairs, w_out, name):
    s, d = x.shape
    tm = min(ROW_TILE, s)
    n = len(pairs)
    widths = [o.shape[1] for o, _ in pairs]

    def body(*refs):
        x_ref, mod_ref, w_ref = refs[:3]
        pr = refs[3:3 + 2 * n]
        xo_ref, y_ref = refs[3 + 2 * n:]
        y = jnp.zeros((tm, d), F32)
        r0 = 0
        for i in range(n):
            mix = pr[2 * i][...] * _silu(pr[2 * i + 1][...])
            y = y + _mm(mix, w_ref[r0:r0 + widths[i], :])
            r0 += widths[i]
        y_ref[...] = y
        xo_ref[...] = x_ref[...] + mod_ref[2:3, :] * y

    flat = [a for p in pairs for a in p]
    sd = jax.ShapeDtypeStruct
    return pl.pallas_call(
        body, name=name, grid=(s // tm,),
        out_shape=(sd((s, d), F32), sd((s, d), F32)),
        in_specs=[_row_spec(tm, d), _full_spec(mod.shape), _full_spec(w_out.shape)]
        + [_row_spec(tm, a.shape[1]) for a in flat],
        out_specs=(_row_spec(tm, d), _row_spec(tm, d)),
        compiler_params=_params(("parallel",)),
    )(x, mod, w_out, *flat)


ONES_ROWS = 16
AHEAD = 2


def _col_max8(s3):
    m8 = jnp.max(s3, axis=0)
    return jnp.broadcast_to(jnp.max(m8, axis=0, keepdims=True), m8.shape)


def _with_ones(vt, n):
    return jnp.concatenate([vt, jnp.ones((ONES_ROWS, n), vt.dtype)], axis=0)


def _grid_edges(grid):
    ids = [pl.program_id(a) for a in range(len(grid))]
    first = functools.reduce(jnp.logical_and, [i == 0 for i in ids])
    last = functools.reduce(jnp.logical_and, [i == n - 1 for i, n in zip(ids, grid)])
    return first, last


def flash_forward(q, k, vt, *, dv, tq, tk, nsub, name, exchange=None):
    hq, s, dq = q.shape
    g_kv = k.shape[0]
    hpg = hq // g_kv
    nq = s // tq
    tkk = tk * nsub
    nk = s // tkk
    grid = (g_kv, nq, nk)
    hosted = exchange is not None
    m_cols = hpg * tq
    dvp = dv + ONES_ROWS

    def body(*refs):
        nx = exchange.n if hosted else 0
        q_ref, k_ref, vt_ref = refs[:3]
        xs_refs = refs[3:3 + nx]
        o_ref, lse_ref = refs[3 + nx:5 + nx]
        land_refs = refs[5 + nx:5 + 2 * nx]
        m_s, acc_s = refs[5 + 2 * nx:7 + 2 * nx]
        sems = refs[7 + 2 * nx:]
        if hosted:
            first, last = _grid_edges(grid)
            pl.when(first)(lambda: exchange.start(xs_refs, land_refs, sems))
        j = pl.program_id(2)

        @pl.when(j == 0)
        def _():
            m_s[...] = jnp.full((8, m_cols), -jnp.inf, F32)
            acc_s[...] = jnp.zeros((dvp, m_cols), F32)

        qq = q_ref[...].reshape(m_cols, dq)
        score = lambda u: _mm_nt(k_ref[0, tk * u:tk * (u + 1), :], qq).reshape(tk // 8, 8, m_cols)
        sts = {u: score(u) for u in range(min(AHEAD, nsub))}
        m_run = m_s[...]
        acc = acc_s[...]
        for u in range(nsub):
            if u + AHEAD < nsub:
                sts[u + AHEAD] = score(u + AHEAD)
            st = sts.pop(u)
            m_new = jnp.maximum(m_run, _col_max8(st))
            p = jnp.exp2(st - m_new[None])
            alpha = jnp.exp2(m_run - m_new)
            pv = _mm(_with_ones(vt_ref[0, 0:dv, tk * u:tk * (u + 1)], tk), p.reshape(tk, m_cols))
            acc = (acc.reshape(dvp // 8, 8, m_cols) * alpha[None]).reshape(dvp, m_cols) + pv
            m_run = m_new
        acc_s[...] = acc
        m_s[...] = m_run

        @pl.when(j == nk - 1)
        def _():
            l = acc_s[dv:dv + 1, :]
            ot = acc_s[0:dv, :] / l
            lse = m_s[0:1, :] + jnp.log2(l)
            for hh in range(hpg):
                o_ref[:, dv * hh:dv * hh + dv] = ot[:, tq * hh:tq * hh + tq].T
                lse_ref[hh] = lse[:, tq * hh:tq * hh + tq]

        if hosted:
            pl.when(last)(lambda: exchange.wait(xs_refs, land_refs, sems))

    sd = jax.ShapeDtypeStruct
    return pl.pallas_call(
        body, name=name, grid=grid,
        out_shape=(sd((s, hq * dv), F32), sd((hq, 1, s), F32)) + (exchange.land_shapes if hosted else ()),
        in_specs=[pl.BlockSpec((hpg, tq, dq), lambda g, i, j: (g, i, 0)),
                  pl.BlockSpec((1, tkk, k.shape[2]), lambda g, i, j: (g, j, 0)),
                  pl.BlockSpec((1, dv, tkk), lambda g, i, j: (g, 0, j))] + (exchange.in_specs if hosted else []),
        out_specs=(pl.BlockSpec((tq, hpg * dv), lambda g, i, j: (i, g)),
                   pl.BlockSpec((hpg, 1, tq), lambda g, i, j: (g, 0, i))) + (exchange.out_specs if hosted else ()),
        scratch_shapes=[pltpu.VMEM((8, m_cols), F32), pltpu.VMEM((dvp, m_cols), F32)]
        + (list(exchange.sems) if hosted else []),
        compiler_params=_params(("arbitrary",) * 3 if hosted else ("parallel", "parallel", "arbitrary")),
    )(q, k, vt, *(exchange.srcs if hosted else []))


def _window_bias_t(hpg, slope_ref):
    t = WINDOW
    r = lax.broadcasted_iota(jnp.int32, (3 * t, t), 0)
    cq = lax.broadcasted_iota(jnp.int32, (3 * t, t), 1)
    arel = jnp.abs(r - t - cq)
    base = jnp.where(arel <= WINDOW, arel.astype(F32) * (-LOG2E), -jnp.inf)
    return jnp.concatenate([base * slope_ref[hh] for hh in range(hpg)], axis=1)


def _window_edges_t(bias, no_before, no_after):
    t = WINDOW
    r = lax.broadcasted_iota(jnp.int32, bias.shape, 0)
    out = ((r < t) & no_before) | ((r >= 2 * t) & no_after)
    return jnp.where(out, -jnp.inf, bias)


def _window_specs(kind, nb, nblk, d):
    t = WINDOW
    before = lambda i: jnp.clip(i * nb - 1, 0, nblk - 1)
    after = lambda i: jnp.clip((i + 1) * nb, 0, nblk - 1)
    if kind == "rows":
        return [pl.BlockSpec((1, t, d), lambda g, i: (g, before(i), 0)),
                pl.BlockSpec((1, nb * t, d), lambda g, i: (g, i, 0)),
                pl.BlockSpec((1, t, d), lambda g, i: (g, after(i), 0))]
    return [pl.BlockSpec((1, d, t), lambda g, i: (g, 0, before(i))),
            pl.BlockSpec((1, d, nb * t), lambda g, i: (g, 0, i)),
            pl.BlockSpec((1, d, t), lambda g, i: (g, 0, after(i)))]


def window_forward(q, k, vt, sink2, slopes, nb, name):
    hq, s, d = q.shape
    g_kv = k.shape[0]
    hpg = hq // g_kv
    t = WINDOW
    nblk = s // t
    steps = nblk // nb
    m_cols = hpg * t

    def body(q_ref, kp, ko, kn, vp, vo, vn, sink_ref, slope_ref, o_ref, lse_ref):
        i = pl.program_id(1)
        kk_all = jnp.concatenate([kp[0], ko[0], kn[0]], axis=0)
        vt_all = jnp.concatenate([vp[0], vo[0], vn[0]], axis=1)
        bias = _window_bias_t(hpg, slope_ref)
        sink_row = jnp.concatenate([jnp.broadcast_to(sink_ref[hh], (8, t)) for hh in range(hpg)], axis=1)
        sts = []
        for u in range(nb):
            qq = q_ref[:, t * u:t * (u + 1), :].reshape(m_cols, d)
            b_u = bias
            if u == 0 or u == nb - 1:
                b_u = _window_edges_t(bias, (i == 0) if u == 0 else False,
                                      (i == steps - 1) if u == nb - 1 else False)
            sts.append(_mm_nt(kk_all[t * u:t * (u + 3), :], qq) + b_u)
        for u in range(nb):
            s3 = sts[u].reshape(3 * t // 8, 8, m_cols)
            m8 = jnp.maximum(_col_max8(s3), sink_row)
            p = jnp.exp2(s3 - m8[None]).reshape(3 * t, m_cols)
            acc = _mm(_with_ones(vt_all[:, t * u:t * (u + 3)], 3 * t), p)
            l = acc[d:d + 1, :] + jnp.exp2(sink_row[0:1, :] - m8[0:1, :])
            ot = acc[0:d, :] / l
            lse = m8[0:1, :] + jnp.log2(l)
            for hh in range(hpg):
                o_ref[t * u:t * (u + 1), d * hh:d * hh + d] = ot[:, t * hh:t * hh + t].T
                lse_ref[hh, :, t * u:t * (u + 1)] = lse[:, t * hh:t * hh + t]

    sd = jax.ShapeDtypeStruct
    return pl.pallas_call(
        body, name=name, grid=(g_kv, steps),
        out_shape=(sd((s, hq * d), F32), sd((hq, 1, s), F32)),
        in_specs=[pl.BlockSpec((hpg, nb * t, d), lambda g, i: (g, i, 0))]
        + _window_specs("rows", nb, nblk, d) + _window_specs("cols", nb, nblk, d)
        + [pl.BlockSpec((hpg, 1, 1), lambda g, i: (g, 0, 0))] * 2,
        out_specs=(pl.BlockSpec((nb * t, hpg * d), lambda g, i: (i, g)),
                   pl.BlockSpec((hpg, 1, nb * t), lambda g, i: (g, 0, i))),
        compiler_params=_params(("parallel", "parallel")),
    )(q, k, k, k, vt, vt, vt, sink2, slopes)


def window_backward(q, k, kt, v, do, lse, delta, slopes, nb, name):
    hq, s, d = q.shape
    g_kv = k.shape[0]
    hpg = hq // g_kv
    t = WINDOW
    nblk = s // t
    steps = nblk // nb
    m_cols = hpg * t

    def body(q_ref, kp, ko, kn, ktp, kto, ktn, vp, vo, vn, do_ref, lse_ref, dl_ref, slope_ref,
             dq_ref, dk_ref, dv_ref, dk_s, dv_s):
        i = pl.program_id(1)

        @pl.when(i == 0)
        def _():
            dk_ref[...] = jnp.zeros(dk_ref.shape, F32)
            dv_ref[...] = jnp.zeros(dv_ref.shape, F32)

        dk_s[...] = jnp.zeros(dk_s.shape, F32)
        dv_s[...] = jnp.zeros(dv_s.shape, F32)
        kk_all = jnp.concatenate([kp[0], ko[0], kn[0]], axis=0)
        vv_all = jnp.concatenate([vp[0], vo[0], vn[0]], axis=0)
        kkt_all = jnp.concatenate([ktp[0], kto[0], ktn[0]], axis=1)
        bias = _window_bias_t(hpg, slope_ref)
        qqs, dds, sts, dps = [], [], [], []
        for u in range(nb):
            rows = slice(t * u, t * (u + 1))
            keys = slice(t * u, t * (u + 3))
            qqs.append(q_ref[:, rows, :].reshape(m_cols, d))
            dds.append(jnp.concatenate([do_ref[rows, d * hh:d * hh + d] for hh in range(hpg)], axis=0))
            b_u = bias
            if u == 0 or u == nb - 1:
                b_u = _window_edges_t(bias, (i == 0) if u == 0 else False,
                                      (i == steps - 1) if u == nb - 1 else False)
            sts.append(_mm_nt(kk_all[keys, :], qqs[u]) + b_u)
            dps.append(_mm_nt(vv_all[keys, :], dds[u]))
        for u in range(nb):
            rows = slice(t * u, t * (u + 1))
            keys = slice(t * u, t * (u + 3))
            lse_row = jnp.concatenate([lse_ref[hh, :, rows] for hh in range(hpg)], axis=1)
            dl_row = jnp.concatenate([dl_ref[hh, :, rows] for hh in range(hpg)], axis=1)
            p = jnp.exp2(sts[u] - lse_row)
            ds = p * (dps[u] - dl_row) * SCALE_A
            dv_s[keys, :] += _mm(p, dds[u])
            dk_s[keys, :] += _mm(ds, qqs[u])
            dqt = _mm(kkt_all[:, keys], ds)
            for hh in range(hpg):
                dq_ref[rows, d * hh:d * hh + d] = dqt[:, t * hh:t * hh + t].T
        tq = nb * t
        for src, r0, n in ((0, jnp.clip(i * nb - 1, 0, nblk - 1) * t, t), (t, i * tq, tq),
                           (t + tq, jnp.clip((i + 1) * nb, 0, nblk - 1) * t, t)):
            dst = pl.ds(pl.multiple_of(r0, t), n)
            dk_ref[0, dst, :] += dk_s[src:src + n, :] * (1.0 / SCALE2_A)
            dv_ref[0, dst, :] += dv_s[src:src + n, :]

    row_map = lambda g, i: (g, 0, i)
    sd = jax.ShapeDtypeStruct
    return pl.pallas_call(
        body, name=name, grid=(g_kv, steps),
        out_shape=(sd((s, hq * d), F32), sd((g_kv, s, d), F32), sd((g_kv, s, d), F32)),
        in_specs=[pl.BlockSpec((hpg, nb * t, d), lambda g, i: (g, i, 0))]
        + _window_specs("rows", nb, nblk, d) + _window_specs("cols", nb, nblk, d) + _window_specs("rows", nb, nblk, d)
        + [pl.BlockSpec((nb * t, hpg * d), lambda g, i: (i, g)), pl.BlockSpec((hpg, 1, nb * t), row_map),
           pl.BlockSpec((hpg, 1, nb * t), row_map), pl.BlockSpec((hpg, 1, 1), lambda g, i: (g, 0, 0))],
        out_specs=(pl.BlockSpec((nb * t, hpg * d), lambda g, i: (i, g)),
                   pl.BlockSpec((1, s, d), lambda g, i: (g, 0, 0)),
                   pl.BlockSpec((1, s, d), lambda g, i: (g, 0, 0))),
        scratch_shapes=[pltpu.VMEM(((nb + 2) * t, d), F32), pltpu.VMEM(((nb + 2) * t, d), F32)],
        compiler_params=_params(("parallel", "arbitrary")),
    )(q, k, k, k, kt, kt, kt, v, v, v, do, lse, delta, slopes)


def flash_backward(q, k, kt, v, do, lse, delta, *, scale, dv, tq, tk, nsub, gq, name, split=None, exchange=None):
    hq, s, dq = q.shape
    g_kv = k.shape[0]
    hpg = hq // gq
    nq = s // tq
    tqq = tq * nsub
    nqs = s // tqq
    nkb = s // tk
    grid = (gq, nkb, nqs)
    hosted = exchange is not None
    m_cols = hpg * tq
    c = scale * LOG2E
    has_v = v is not None

    def body(*refs):
        it = iter(refs)
        q_ref, k_ref, kt_ref = next(it), next(it), next(it)
        v_ref = next(it) if has_v else None
        do_ref, lse_ref, dl_ref = next(it), next(it), next(it)
        nx = exchange.n if hosted else 0
        xs_refs = [next(it) for _ in range(nx)]
        dq_ref, dk_ref, dv_ref = next(it), next(it), next(it)
        land_refs = [next(it) for _ in range(nx)]
        dqt_s = next(it)
        sems = list(it)
        kj = pl.program_id(1)
        qi = pl.program_id(2)
        if hosted:
            first, last = _grid_edges(grid)
            pl.when(first)(lambda: exchange.start(xs_refs, land_refs, sems))

        @pl.when((kj == 0) & (qi == 0))
        def _():
            dqt_s[...] = jnp.zeros(dqt_s.shape, F32)

        @pl.when(qi == 0)
        def _():
            dk_ref[...] = jnp.zeros(dk_ref.shape, F32)
            dv_ref[...] = jnp.zeros(dv_ref.shape, F32)

        kk = k_ref[0]
        vv = v_ref[0] if has_v else kk[:, :dv]
        qqs, dds, sts, dps = {}, {}, {}, {}

        def issue(u):
            rows = slice(tq * u, tq * (u + 1))
            qqs[u] = q_ref[:, rows, :].reshape(m_cols, dq)
            dds[u] = jnp.concatenate([do_ref[rows, dv * hh:dv * hh + dv] for hh in range(hpg)], axis=0)
            sts[u] = _mm_nt(kk, qqs[u])
            dps[u] = _mm_nt(vv, dds[u])

        for u in range(min(AHEAD, nsub)):
            issue(u)
        dv_acc = dv_ref[0]
        dk_acc = dk_ref[0]
        for u in range(nsub):
            if u + AHEAD < nsub:
                issue(u + AHEAD)
            rows = slice(tq * u, tq * (u + 1))
            lse_row = jnp.concatenate([lse_ref[hh, :, rows] for hh in range(hpg)], axis=1)
            dl_row = jnp.concatenate([dl_ref[hh, :, rows] for hh in range(hpg)], axis=1)
            p = jnp.exp2(sts[u] - lse_row)
            ds = p * (dps[u] - dl_row) * scale
            dv_acc = dv_acc + _mm(p, dds[u])
            dk_acc = dk_acc + _mm(ds, qqs[u])
            dqt = _mm(kt_ref[0], ds)
            for hh in range(hpg):
                dqt_s[qi * nsub + u, dq * hh:dq * hh + dq, :] += dqt[:, tq * hh:tq * hh + tq]
        dv_ref[0] = dv_acc
        dk_ref[0] = jnp.where(qi == nqs - 1, dk_acc * (1.0 / c), dk_acc)

        @pl.when((kj == nkb - 1) & (qi == nqs - 1))
        def _():
            def emit(t, carry):
                r0 = pl.multiple_of(t * tq, tq)
                for hh in range(hpg):
                    blk = dqt_s[t, dq * hh:dq * hh + dq, :].T
                    if split is None:
                        dq_ref[pl.ds(r0, tq), dq * hh:dq * hh + dq] = blk
                    else:
                        rest = dq - split
                        dq_ref[pl.ds(r0, tq), split * hh:split * (hh + 1)] = blk[:, 0:split]
                        dq_ref[pl.ds(r0, tq), hpg * split + rest * hh:hpg * split + rest * (hh + 1)] = blk[:, split:]
                return carry

            lax.fori_loop(0, nq, emit, 0)

        if hosted:
            pl.when(last)(lambda: exchange.wait(xs_refs, land_refs, sems))

    kv_of = lambda g: g * g_kv // gq
    in_specs = [pl.BlockSpec((hpg, tqq, dq), lambda g, kj, qi: (g, qi, 0)),
                pl.BlockSpec((1, tk, dq), lambda g, kj, qi: (kv_of(g), kj, 0)),
                pl.BlockSpec((1, dq, tk), lambda g, kj, qi: (kv_of(g), 0, kj))]
    args = [q, k, kt]
    if has_v:
        in_specs.append(pl.BlockSpec((1, tk, dv), lambda g, kj, qi: (kv_of(g), kj, 0)))
        args.append(v)
    row_map = lambda g, kj, qi: (g, 0, qi)
    in_specs += [pl.BlockSpec((tqq, hpg * dv), lambda g, kj, qi: (qi, g)),
                 pl.BlockSpec((hpg, 1, tqq), row_map), pl.BlockSpec((hpg, 1, tqq), row_map)]
    args += [do, lse, delta]
    if hosted:
        in_specs += exchange.in_specs
        args += exchange.srcs
    sd = jax.ShapeDtypeStruct
    return pl.pallas_call(
        body, name=name, grid=grid,
        out_shape=(sd((s, hq * dq), F32), sd((gq, s, dq), F32), sd((gq, s, dv), F32))
        + (exchange.land_shapes if hosted else ()),
        in_specs=in_specs,
        out_specs=(pl.BlockSpec((s, hpg * dq), lambda g, kj, qi: (0, g)),
                   pl.BlockSpec((1, tk, dq), lambda g, kj, qi: (g, kj, 0)),
                   pl.BlockSpec((1, tk, dv), lambda g, kj, qi: (g, kj, 0))) + (exchange.out_specs if hosted else ()),
        scratch_shapes=[pltpu.VMEM((nq, hpg * dq, tq), F32)] + (list(exchange.sems) if hosted else []),
        compiler_params=_params(("arbitrary",) * 3 if hosted else ("parallel", "arbitrary", "arbitrary")),
    )(*args)


def loss_head(x, target, fnw):
    s, d = x.shape
    tm = min(ROW_TILE, s)

    def body(x_ref, t_ref, w_ref, lp_ref, dx_ref, dw_ref):
        @pl.when(pl.program_id(0) == 0)
        def _():
            lp_ref[...] = jnp.zeros(lp_ref.shape, F32)
            dw_ref[...] = jnp.zeros(dw_ref.shape, F32)

        x = x_ref[...]
        g = w_ref[...]
        err = x * _rms(x) * g - t_ref[...]
        lp_ref[...] += jnp.sum(err * err, axis=0, keepdims=True)
        dx, dg = _rms_bwd(err * (1.0 / d), x, g)
        dx_ref[...] = dx
        dw_ref[...] += jnp.sum(dg, axis=0, keepdims=True)

    sd = jax.ShapeDtypeStruct
    return pl.pallas_call(
        body, name="loss_head", grid=(s // tm,),
        out_shape=(sd((1, d), F32), sd((s, d), F32), sd((1, d), F32)),
        in_specs=[_row_spec(tm, d), _row_spec(tm, d), _full_spec(fnw.shape)],
        out_specs=(_full_spec((1, d)), _row_spec(tm, d), _full_spec((1, d))),
        compiler_params=_params(("arbitrary",)),
    )(x, target, fnw)


def mixer_out_backward(dx, y, mod, pairs, w_out, delta_heads, name, lse=None, sink=None):
    s, d = dx.shape
    tm = min(ROW_TILE, s)
    n = len(pairs)
    widths = [o.shape[1] for o, _ in pairs]
    n_delta = sum(1 for h in delta_heads if h)
    with_sink = lse is not None

    def body(*refs):
        it = iter(refs)
        dx_ref, y_ref, mod_ref, wt_ref = next(it), next(it), next(it), next(it)
        pr = [next(it) for _ in range(2 * n)]
        lse_ref = next(it) if with_sink else None
        sink_ref = next(it) if with_sink else None
        outs = [next(it) for _ in range(2 * n)]
        dl_refs = [next(it) for _ in range(n_delta)]
        dgate_ref, dw_ref = next(it), next(it)
        dsink_ref = next(it) if with_sink else None

        @pl.when(pl.program_id(0) == 0)
        def _():
            dgate_ref[...] = jnp.zeros(dgate_ref.shape, F32)
            dw_ref[...] = jnp.zeros(dw_ref.shape, F32)
            if with_sink:
                dsink_ref[...] = jnp.zeros(dsink_ref.shape, F32)

        dxo = dx_ref[...]
        dgate_ref[...] += jnp.sum(dxo * y_ref[...], axis=0, keepdims=True)
        dy = (dxo * mod_ref[2:3, :]).astype(MXU)
        dmix = _mm_nt(dy, wt_ref[...])
        r0 = 0
        di = 0
        for i in range(n):
            o = pr[2 * i][...]
            g = pr[2 * i + 1][...]
            dm = dmix[:, r0:r0 + widths[i]]
            sg = _sigmoid(g)
            act = g * sg
            do = dm * act
            outs[2 * i][...] = do.astype(MXU)
            outs[2 * i + 1][...] = (dm * o * (sg * (1.0 + g * (1.0 - sg)))).astype(MXU)
            dw_ref[r0:r0 + widths[i], :] += _mm_tn(o * act, dy)
            if delta_heads[i]:
                dlt = _group_sums_t(do * o, HD)[0:delta_heads[i], :]
                dl_refs[di][...] = dlt
                if with_sink:
                    ps = jnp.exp2(sink_ref[...] - lse_ref[...])
                    dsink_ref[...] += -jnp.sum(ps * dlt, axis=1, keepdims=True)
                di += 1
            r0 += widths[i]

    flat = [a for p in pairs for a in p]
    sd = jax.ShapeDtypeStruct
    in_specs = [_row_spec(tm, d), _row_spec(tm, d), _full_spec(mod.shape), _full_spec(w_out.shape)]
    in_specs += [_row_spec(tm, a.shape[1]) for a in flat]
    args = [dx, y, mod, w_out] + flat
    if with_sink:
        nh = lse.shape[0]
        in_specs += [_rows_spec(nh, tm), _full_spec(sink.shape)]
        args += [lse, sink]
    out_shape = [sd((s, a.shape[1]), MXU) for a in flat]
    out_specs = [_row_spec(tm, a.shape[1]) for a in flat]
    for h in delta_heads:
        if h:
            out_shape.append(sd((h, s), F32))
            out_specs.append(_rows_spec(h, tm))
    out_shape += [sd((1, d), F32), sd((sum(widths), d), F32)]
    out_specs += [_full_spec((1, d)), _full_spec((sum(widths), d))]
    if with_sink:
        out_shape.append(sd((lse.shape[0], 1), F32))
        out_specs.append(_full_spec((lse.shape[0], 1)))
    return pl.pallas_call(
        body, name=name, grid=(s // tm,), out_shape=tuple(out_shape), in_specs=in_specs, out_specs=tuple(out_specs),
        compiler_params=_params(("arbitrary",)),
    )(*args)


def latent_out_backward(d_ob, o_lat, w_uv):
    s = o_lat.shape[0]
    tm = min(ROW_TILE, s)

    def body(d_ref, o_ref, uv_ref, dol_ref, dl_ref, duv_ref, prod_s):
        @pl.when(pl.program_id(0) == 0)
        def _():
            duv_ref[...] = jnp.zeros(duv_ref.shape, F32)

        for hh in range(B_HEADS):
            dh = d_ref[:, HD * hh:HD * hh + HD]
            ol = o_ref[:, B_KV_LORA * hh:B_KV_LORA * (hh + 1)]
            dol = _mm_nt(dh, uv_ref[hh])
            dol_ref[:, B_KV_LORA * hh:B_KV_LORA * (hh + 1)] = dol.astype(MXU)
            prod_s[:, B_KV_LORA * hh:B_KV_LORA * (hh + 1)] = dol * ol
            duv_ref[hh] += _mm_tn(ol, dh)
        dl_ref[...] = _group_sums_t(prod_s[...], B_KV_LORA)[0:B_HEADS, :]

    sd = jax.ShapeDtypeStruct
    return pl.pallas_call(
        body, name="latent_out_backward", grid=(s // tm,),
        out_shape=(sd(o_lat.shape, MXU), sd((B_HEADS, s), F32), sd(w_uv.shape, F32)),
        in_specs=[_row_spec(tm, d_ob.shape[1]), _row_spec(tm, o_lat.shape[1]), _full_spec(w_uv.shape)],
        out_specs=(_row_spec(tm, o_lat.shape[1]), _rows_spec(B_HEADS, tm), _full_spec(w_uv.shape)),
        scratch_shapes=[pltpu.VMEM((tm, o_lat.shape[1]), F32)],
        compiler_params=_params(("arbitrary",)),
    )(d_ob, o_lat, w_uv)


def even_prep_backward(dqa, dka, dva, dqb, dkb, dvb, qa_raw, ka_raw, cq_raw, ckv_raw,
                       gq, gk, qln, kvln, w_uq_t, uk_bd, bd, cos_a, sin_a, cos_t, sin_t):
    s = qa_raw.shape[0]
    tm = min(ROW_TILE, s)
    half_lat = B_KV_LORA * B_HEADS // 2
    half_w = dqb.shape[1] // 2

    def body(dqa_ref, dka_ref, dva_ref, dqb_ref, dkb_ref, dvb_ref, qa_ref, ka_ref, cq_ref, ckv_ref,
             gq_ref, gk_ref, qln_ref, kvln_ref, uqt_ref, ukbd_ref, bd_ref, ca_ref, sa_ref, ct_ref, st_ref,
             pqa, pka, pva, pcq, pckv, pkr, gqn, gkn, gqln, gkvln, guq, guk):
        @pl.when(pl.program_id(0) == 0)
        def _():
            for r in (gqn, gkn, gqln, gkvln, guq, guk):
                r[...] = jnp.zeros(r.shape, F32)

        ca, sa, ct, st = ca_ref[...], sa_ref[...], ct_ref[...], st_ref[...]
        wide = lambda t, n: jnp.concatenate([t] * n, axis=1)
        rows = lambda a: jnp.sum(a, axis=0, keepdims=True)
        dx, dg = _head_norm_bwd(_rope_t(dqa_ref[...], wide(ca, 4), wide(sa, 4), 32), qa_ref[...], gq_ref[...],
                                bd_ref, HD)
        pqa[...] = dx.astype(MXU)
        gqn[...] += rows(dg)
        dk_all = jnp.concatenate([dka_ref[g] for g in range(A_KV)], axis=1)
        dx, dg = _head_norm_bwd(_rope_t(dk_all, ca, sa, 32), ka_ref[...], gk_ref[...], bd_ref[0:128, 0:128], HD)
        pka[...] = dx.astype(MXU)
        gkn[...] += rows(dg)
        pva[...] = jnp.concatenate([dva_ref[g] for g in range(A_KV)], axis=1).astype(MXU)
        cq_raw = cq_ref[...]
        cq_n = cq_raw * _rms(cq_raw) * qln_ref[...]
        qb = _mm_nt(cq_n, uqt_ref[...])
        d_lat = jnp.concatenate([dqb_ref[:, 0:half_lat], dqb_ref[:, half_w:half_w + half_lat]], axis=1)
        d_rope = jnp.concatenate([dqb_ref[:, half_lat:half_w], dqb_ref[:, half_w + half_lat:]], axis=1)
        for hh in range(B_HEADS):
            guk[hh] += _mm_tn(d_lat[:, B_KV_LORA * hh:B_KV_LORA * (hh + 1)], qb[:, B_NOPE * hh:B_NOPE * (hh + 1)])
        dqb_all = jnp.concatenate([_mm_nt(d_lat, ukbd_ref[...]),
                                   _rope_t(d_rope, wide(ct, 2), wide(st, 2), 32)], axis=1)
        guq[...] += _mm_tn(dqb_all, cq_n)
        dx, dg = _rms_bwd(_mm(dqb_all, uqt_ref[...]), cq_raw, qln_ref[...])
        pcq[...] = dx.astype(MXU)
        gqln[...] += rows(dg)
        dkb_sum = dkb_ref[0] + dkb_ref[1]
        dckv = dkb_sum[:, 0:B_KV_LORA] + dvb_ref[0] + dvb_ref[1]
        dx, dg = _rms_bwd(dckv, ckv_ref[...], kvln_ref[...])
        pckv[...] = dx.astype(MXU)
        gkvln[...] += rows(dg)
        pkr[...] = _rope_t(dkb_sum[:, B_KV_LORA:B_QK], ct[:, 0:B_ROPE], st[:, 0:B_ROPE], 32).astype(MXU)

    sd = jax.ShapeDtypeStruct
    consts = [gq, gk, qln, kvln, w_uq_t, uk_bd, bd]
    in_specs = [_row_spec(tm, 512), _head_spec(A_KV, tm, HD), _head_spec(A_KV, tm, HD),
                _row_spec(tm, dqb.shape[1]), _head_spec(2, tm, B_QK), _head_spec(2, tm, B_KV_LORA),
                _row_spec(tm, 512), _row_spec(tm, 128), _row_spec(tm, B_Q_LORA), _row_spec(tm, B_KV_LORA)]
    in_specs += [_full_spec(a.shape) for a in consts] + [_row_spec(tm, 128)] * 4
    small = [sd(gq.shape, F32), sd(gk.shape, F32), sd(qln.shape, F32), sd(kvln.shape, F32), sd(w_uq_t.shape, F32),
             sd((B_HEADS, B_KV_LORA, B_NOPE), F32)]
    out_shape = (sd((s, 512), MXU), sd((s, 128), MXU), sd((s, 128), MXU), sd((s, B_Q_LORA), MXU),
                 sd((s, B_KV_LORA), MXU), sd((s, B_ROPE), MXU), *small)
    out_specs = (_row_spec(tm, 512), _row_spec(tm, 128), _row_spec(tm, 128), _row_spec(tm, B_Q_LORA),
                 _row_spec(tm, B_KV_LORA), _row_spec(tm, B_ROPE), *[_full_spec(a.shape) for a in small])
    return pl.pallas_call(
        body, name="even_prep_backward", grid=(s // tm,), out_shape=out_shape, in_specs=in_specs, out_specs=out_specs,
        compiler_params=_params(("arbitrary",)),
    )(dqa, dka, dva, dqb, dkb, dvb, qa_raw, ka_raw, cq_raw, ckv_raw, *consts, cos_a, sin_a, cos_t, sin_t)


def in_proj_backward(x, mod, nw, pieces, name, *, dx_out=None, w_in_t=None, dw_rows=None, exchange=None):
    s, d = x.shape
    tm = min(ROW_TILE, s)
    grid = (s // tm,)
    n = len(pieces)
    cols = [c for _, c in pieces]
    want_dx = w_in_t is not None
    want_dw = dw_rows is not None
    n_cols = sum(c1 - c0 for c0, c1 in cols)
    hosted = exchange is not None
    nx = exchange.n if hosted else 0

    def body(*refs):
        it = iter(refs)
        x_ref, mod_ref, nw_ref = next(it), next(it), next(it)
        dxo_ref, wt_ref = (next(it), next(it)) if want_dx else (None, None)
        p_refs = [next(it) for _ in range(n)]
        xs_refs = [next(it) for _ in range(nx)]
        dx_ref, dv_ref = (next(it), next(it)) if want_dx else (None, None)
        dw_ref = next(it) if want_dw else None
        land_refs = [next(it) for _ in range(nx)]
        acc_ref = next(it) if want_dx else None
        dw_acc = next(it) if want_dw else None
        sems = list(it)
        first, last = _grid_edges(grid)
        if hosted:
            pl.when(first)(lambda: exchange.start(xs_refs, land_refs, sems))

        @pl.when(first)
        def _():
            if want_dw:
                dw_acc[...] = jnp.zeros(dw_acc.shape, F32)
            if want_dx:
                acc_ref[...] = jnp.zeros(acc_ref.shape, F32)

        xn, g1, h = _modulated(x_ref[...], mod_ref, nw_ref)
        hb = h.astype(MXU)
        dh = jnp.zeros((tm, d), F32)
        for k, (pr, (c0, c1)) in enumerate(zip(p_refs, cols)):
            if len(pr.shape) == 3:
                pc = jnp.concatenate([pr[g] for g in range(pr.shape[0])], axis=1).astype(MXU)
            else:
                pc = pr[...].astype(MXU)
            if want_dx:
                dh = dh + jnp.dot(pc, wt_ref[c0:c1, :], preferred_element_type=F32)
            if want_dw:
                r0, r1 = dw_rows[k]
                dw_acc[r0:r1, :] += _mm_tn(pc, hb)
        if want_dx:
            acc_ref[0:1, :] += jnp.sum(dh, axis=0, keepdims=True)
            acc_ref[1:2, :] += jnp.sum(dh * xn, axis=0, keepdims=True)
            dxn = dh * g1
            x = x_ref[...]
            dx_ref[...] = dxo_ref[...] + _rms(x) * (dxn - xn * jnp.mean(dxn * xn, axis=-1, keepdims=True))

        @pl.when(last)
        def _():
            if want_dx:
                dg1 = acc_ref[1:2, :]
                dv_ref[0:1, :] = acc_ref[0:1, :]
                dv_ref[1:2, :] = dg1 * nw_ref[...]
                dv_ref[2:3, :] = dg1 * (1.0 + mod_ref[1:2, :])
                dv_ref[3:4, :] = jnp.zeros((1, d), F32)
            if want_dw:
                dw_ref[...] = dw_acc[...].astype(MXU)

        if hosted:
            pl.when(last)(lambda: exchange.wait(xs_refs, land_refs, sems))

    arrs = [a for a, _ in pieces]
    sd = jax.ShapeDtypeStruct
    args = [x, mod, nw] + ([dx_out, w_in_t] if want_dx else []) + arrs + (exchange.srcs if hosted else [])
    in_specs = [_row_spec(tm, d), _full_spec(mod.shape), _full_spec(nw.shape)]
    in_specs += [_row_spec(tm, d), _full_spec(w_in_t.shape)] if want_dx else []
    in_specs += [_row_spec(tm, a.shape[1]) if a.ndim == 2 else _head_spec(a.shape[0], tm, a.shape[2]) for a in arrs]
    in_specs += exchange.in_specs if hosted else []
    out_shape, out_specs, scratch = [], [], []
    if want_dx:
        out_shape += [sd((s, d), F32), sd((4, d), F32)]
        out_specs += [_row_spec(tm, d), _full_spec((4, d))]
        scratch.append(pltpu.VMEM((8, d), F32))
    if want_dw:
        out_shape.append(sd((n_cols, d), MXU))
        out_specs.append(_full_spec((n_cols, d)))
        scratch.append(pltpu.VMEM((n_cols, d), F32))
    if hosted:
        out_shape += list(exchange.land_shapes)
        out_specs += list(exchange.out_specs)
        scratch += list(exchange.sems)
    return pl.pallas_call(
        body, name=name, grid=grid, out_shape=tuple(out_shape), in_specs=in_specs, out_specs=tuple(out_specs),
        scratch_shapes=scratch, compiler_params=_params(("arbitrary",)),
    )(*args)


def ada_weight_grad(c_all, dmod_cols):
    d = c_all.shape[1]
    w = dmod_cols.shape[2]

    def body(c_ref, dm_ref, out_ref):
        ca = _silu(c_ref[...])
        for l in range(2):
            out_ref[l] = _mm_tn(ca, dm_ref[l])

    return pl.pallas_call(
        body, name="ada_weight_grad",
        out_shape=jax.ShapeDtypeStruct((2, d, w), F32),
        compiler_params=pltpu.CompilerParams(vmem_limit_bytes=VMEM_LIMIT),
    )(c_all, dmod_cols)


def _slot_sum(g_ref):
    g = g_ref[0].astype(F32)
    for k in range(1, g_ref.shape[0]):
        g = g + g_ref[k].astype(F32)
    return g


def _adamw_math(g, w, m, v):
    m_new = ADAM_B1 * m + (1.0 - ADAM_B1) * g
    v_new = ADAM_B2 * v + (1.0 - ADAM_B2) * (g * g)
    m_hat = m_new / (1.0 - ADAM_B1 ** ADAM_STEP)
    v_hat = v_new / (1.0 - ADAM_B2 ** ADAM_STEP)
    return -ADAM_LR * (m_hat / (jnp.sqrt(v_hat) + ADAM_EPS) + ADAM_WD * w), m_new, v_new


def adamw_small(g_alls, ws, ms, vs, loss_all):
    n = len(ws)

    def body(*refs):
        g_refs, w_refs, m_refs, v_refs = (refs[i * n:(i + 1) * n] for i in range(4))
        loss_ref = refs[4 * n]
        outs = refs[4 * n + 1:]
        for i in range(n):
            g = _slot_sum(g_refs[i])
            outs[i][...] = g
            outs[n + i][...], outs[2 * n + i][...], outs[3 * n + i][...] = _adamw_math(
                g, w_refs[i][...], m_refs[i][...], v_refs[i][...])
        outs[4 * n][...] = _slot_sum(loss_ref)

    sds = [jax.ShapeDtypeStruct(w.shape, F32) for w in ws]
    res = pl.pallas_call(
        body, name="adamw_small", out_shape=tuple(sds * 4) + (jax.ShapeDtypeStruct(loss_all.shape[1:], F32),),
        compiler_params=pltpu.CompilerParams(vmem_limit_bytes=VMEM_LIMIT),
    )(*g_alls, *ws, *ms, *vs, loss_all)
    return [res[i * n:(i + 1) * n] for i in range(4)], res[4 * n]


def adamw_rows(g_slots, w, m, v, name):
    n, r, lanes = g_slots.shape
    fits = [t for t in range(16, r + 1, 16) if r % t == 0 and t * lanes <= ADAM_TILE]
    tr = max(fits) if fits else r
    def body(g_ref, w_ref, m_ref, v_ref, go, do, mo, vo):
        g = _slot_sum(g_ref)
        go[...] = g
        do[...], mo[...], vo[...] = _adamw_math(g, w_ref[...], m_ref[...], v_ref[...])

    row = pl.BlockSpec((tr, lanes), lambda i: (i, 0))
    sd = jax.ShapeDtypeStruct((r, lanes), F32)
    return pl.pallas_call(
        body, name=name, grid=(r // tr,), out_shape=(sd, sd, sd, sd),
        in_specs=[pl.BlockSpec((n, tr, lanes), lambda i: (0, i, 0)), row, row, row],
        out_specs=(row, row, row, row),
        compiler_params=_params(("parallel",)),
    )(g_slots, w, m, v)


def _rope_tables(s):
    def cs(pos, dim):
        inv = ROPE_THETA ** (-np.arange(0, dim, 2, dtype=np.float32) / dim)
        ang = pos.astype(np.float32)[:, None] * inv.astype(np.float32)[None, :]
        return np.cos(ang), np.sin(ang)

    rows = s // GRID_W
    row = np.repeat(np.arange(rows), GRID_W)
    col = np.tile(np.arange(GRID_W), rows)
    cr, sr = cs(row, HD // 2)
    cc, sc = cs(col, HD // 2)
    ct, st = cs(np.arange(s), B_ROPE)
    tables = (np.concatenate([cr, cr, cc, cc] * 2, axis=-1), np.concatenate([-sr, sr, -sc, sc] * 2, axis=-1),
              np.concatenate([ct, ct] * 4, axis=-1), np.concatenate([-st, st] * 4, axis=-1))
    return tuple(jnp.asarray(t, F32) for t in tables)


def _even_rows_to_kernel(wt):
    return jnp.concatenate([wt[:1664], wt[1696:], wt[1664:1696]], axis=0)


def _uq_rows_to_kernel(wt):
    r = wt.reshape(B_HEADS, B_NOPE + B_ROPE, -1)
    return jnp.concatenate([r[:, :B_NOPE].reshape(B_HEADS * B_NOPE, -1), r[:, B_NOPE:].reshape(B_HEADS * B_ROPE, -1)])


def _uq_rows_to_reference(wt):
    nope = wt[:B_HEADS * B_NOPE].reshape(B_HEADS, B_NOPE, -1)
    rope = wt[B_HEADS * B_NOPE:].reshape(B_HEADS, B_ROPE, -1)
    return jnp.concatenate([nope, rope], axis=1).reshape(B_HEADS * (B_NOPE + B_ROPE), -1)


def _shard_t(w):
    return jnp.transpose(w[0])


def _unshard_t(wt, like):
    return jnp.transpose(wt)[None].reshape(like.shape)


def kernel(x, c, norm_w, ada_w, ada_b, even_w_in, a_q_norm, a_k_norm, b_q_lora_norm, b_kv_lora_norm, b_w_uq, b_w_uk, b_w_uv, even_w_out, odd_w_in, c_sink, odd_w_out, final_norm, loss_target, m_norm_w, m_ada_w, m_ada_b, m_even_w_in, m_a_q_norm, m_a_k_norm, m_b_q_lora_norm, m_b_kv_lora_norm, m_b_w_uq, m_b_w_uk, m_b_w_uv, m_even_w_out, m_odd_w_in, m_c_sink, m_odd_w_out, m_final_norm, v_norm_w, v_ada_w, v_ada_b, v_even_w_in, v_a_q_norm, v_a_k_norm, v_b_q_lora_norm, v_b_kv_lora_norm, v_b_w_uq, v_b_w_uk, v_b_w_uv, v_even_w_out, v_odd_w_in, v_c_sink, v_odd_w_out, v_final_norm):
    s, d = x.shape[1], x.shape[2]
    x0 = x[0]
    target = loss_target[0]
    me_flat = 4 * lax.axis_index("x") + 2 * lax.axis_index("y") + lax.axis_index("c")

    wcols = ada_w.shape[2]
    bias_cols = lax.dynamic_slice_in_dim(ada_b.reshape(2, N_DEV, wcols), me_flat, 1, axis=1)
    call, modp, (g_in_e, g_uq) = ada_forward(
        jnp.broadcast_to(c, (8, d)), ada_w, bias_cols,
        Gather([_shard_t(even_w_in).astype(MXU), _shard_t(b_w_uq).astype(MXU)]))
    wt_in_e = _even_rows_to_kernel(g_in_e.reshape(-1, d))
    wt_uq = _uq_rows_to_kernel(g_uq.reshape(-1, B_Q_LORA))
    later_exchange = Exchange([_shard_t(odd_w_in).astype(MXU), even_w_out[0].astype(MXU),
                               odd_w_out[0].astype(MXU)], scatter=False)
    uk_bd = (jnp.eye(B_HEADS, dtype=F32)[:, None, :, None] * jnp.transpose(b_w_uk[0], (1, 2, 0))[:, :, None, :]
             ).reshape(B_HEADS * B_NOPE, B_HEADS * B_KV_LORA).astype(MXU)
    head_bd = jnp.asarray(np.kron(np.eye(A_HEADS), np.ones((HD, HD))), MXU)
    gq_full, gk_full = jnp.tile(a_q_norm, (1, A_HEADS)), jnp.tile(a_k_norm, (1, A_KV))
    w_uv = jnp.transpose(b_w_uv[0], (1, 0, 2)).astype(MXU)

    c_all = call[:, 0, :]
    mod = jnp.transpose(modp[:, :, 0, :], (1, 0, 2)).reshape(2, 3, d)
    mod_e, mod_o = mod[0], mod[1]
    nw_e, nw_o = norm_w[0:1], norm_w[1:2]

    cos_a, sin_a, cos_t, sin_t = _rope_tables(s)
    slopes = (2.0 ** (-8.0 * jnp.arange(1, C_HEADS + 1, dtype=F32) / C_HEADS)).reshape(C_HEADS, 1, 1)
    sink2 = c_sink.reshape(C_HEADS, 1, 1) * LOG2E

    (qa, ka, va, qb, kb, kat, vat, kbt, qa_raw, ka_raw, cq_raw, ckv_raw, ga, gb) = even_in_forward(
        x0, mod_e, nw_e, wt_in_e, gq_full, gk_full, b_q_lora_norm, b_kv_lora_norm, wt_uq, uk_bd, head_bd,
        cos_a, sin_a, cos_t, sin_t)
    tk_dense = min(512, s)
    tq_dense = min(256, s)
    fwd_sub = min(8, s // tk_dense)
    bwd_sub_a = min(16, s // tq_dense)
    bwd_sub_b = min(8, s // tq_dense)
    oa, lse_a, g_in_o, g_out_e, g_out_o = flash_forward(
        qa, ka, vat, dv=HD, tq=tq_dense, tk=tk_dense, nsub=fwd_sub, name="attn_a_fwd",
        exchange=later_exchange)
    wt_in_o = g_in_o.reshape(-1, d)
    w_out_e = g_out_e.reshape(-1, d)
    w_out_o = g_out_o.reshape(-1, d)
    o_lat, lse_b = flash_forward(qb, kb, kbt, dv=B_KV_LORA, tq=min(128, s), tk=tk_dense, nsub=fwd_sub,
                                 name="attn_b_fwd")
    ob = latent_out_forward(o_lat, w_uv)
    x1, y_e = mixer_out_forward(x0, mod_e, [(oa, ga), (ob, gb)], w_out_e, "even_out_fwd")

    qc, kc, vc, kct, vct, gc = odd_in_forward(x1, mod_o, nw_o, wt_in_o)
    win_sub = min(8, s // WINDOW)
    oc, lse_c = window_forward(qc, kc, vct, sink2, slopes, win_sub, "attn_c_fwd")
    x2, y_o = mixer_out_forward(x1, mod_o, [(oc, gc)], w_out_o, "odd_out_fwd")

    loss_lanes, dx2, d_final = loss_head(x2, target, final_norm.reshape(1, d))
    loss_part = (0.5 / d) * jnp.sum(loss_lanes)

    doc, dgc, delta_c, dgate_o, dw_out_o, dsink = mixer_out_backward(
        dx2, y_o, mod_o, [(oc, gc)], w_out_o, [C_HEADS], "odd_out_bwd", lse=lse_c.reshape(C_HEADS, s),
        sink=sink2.reshape(C_HEADS, 1))
    rows3 = lambda t: t.reshape(t.shape[0], 1, s)
    dqc, dkc, dvc = window_backward(qc, kc, kct, vc, doc, lse_c, rows3(delta_c), slopes, win_sub, "attn_c_bwd")
    dx1, dvec_o, dwt_in_o = in_proj_backward(
        x1, mod_o, nw_o, [(dqc, O_Q), (dkc, O_K), (dvc, O_V), (dgc, O_G)], "odd_in_bwd",
        dx_out=dx2, w_in_t=wt_in_o, dw_rows=[O_Q, O_K, O_V, O_G])

    doa, dga, dob, dgb, delta_a, dgate_e, dw_out_e = mixer_out_backward(
        dx1, y_e, mod_e, [(oa, ga), (ob, gb)], w_out_e, [A_HEADS, 0], "even_out_bwd")
    d_olat, delta_b, dw_uv = latent_out_backward(dob, o_lat, w_uv)
    blocks = lambda g: g.astype(MXU).reshape(N_DEV, g.shape[0] // N_DEV, g.shape[1])
    even_pieces = lambda: [(pqa, E_QA), (pka, E_KA), (pva, E_VA), (dga, E_GA), (pcq, E_CQ), (pckv, E_CKV),
                           (dgb, E_GB), (pkr, E_KR)]
    scatter_odd = Exchange([blocks(dwt_in_o), blocks(dw_out_o)], True)
    scatter_out_e = Exchange([blocks(dw_out_e)], True)
    dqb, dkb, dvb, l_in_o, l_out_o = flash_backward(
        qb, kb, kbt, None, d_olat, lse_b, rows3(delta_b), scale=SCALE_B, dv=B_KV_LORA,
        tq=tq_dense, tk=tk_dense, nsub=bwd_sub_b, gq=2, name="attn_b_bwd", split=B_KV_LORA, exchange=scatter_odd)
    dqa, dka, dva, l_out_e = flash_backward(
        qa, ka, kat, va, doa, lse_a, rows3(delta_a), scale=SCALE_A, dv=HD,
        tq=tq_dense, tk=tk_dense, nsub=bwd_sub_a, gq=A_KV, name="attn_a_bwd", exchange=scatter_out_e)
    (pqa, pka, pva, pcq, pckv, pkr, g_qn, g_kn, g_qln, g_kvln, dwt_uq, dw_uk) = even_prep_backward(
        dqa, dka, dva, dqb, dkb, dvb, qa_raw, ka_raw, cq_raw, ckv_raw,
        gq_full, gk_full, b_q_lora_norm, b_kv_lora_norm, wt_uq, uk_bd, head_bd, cos_a, sin_a, cos_t, sin_t)
    g_qn = jnp.sum(g_qn.reshape(A_HEADS, HD), axis=0)
    g_kn = jnp.sum(g_kn.reshape(A_KV, HD), axis=0)
    (dwt_in_e,) = in_proj_backward(
        x0, mod_e, nw_e, even_pieces(), "even_in_bwd_dw",
        dw_rows=[E_QA, E_KA, E_VA, E_GA, E_CQ, E_CKV, (1696, 2208), (1664, 1696)])
    dx0, dvec_e, l_in_e, l_uq = in_proj_backward(
        x0, mod_e, nw_e, even_pieces(), "even_in_bwd_dx", dx_out=dx1, w_in_t=wt_in_e,
        exchange=Exchange([blocks(dwt_in_e), blocks(_uq_rows_to_reference(dwt_uq))], True))

    dmod = jnp.stack([jnp.concatenate([dvec_e[0], dvec_e[1], dgate_e[0]]),
                      jnp.concatenate([dvec_o[0], dvec_o[1], dgate_o[0]])])
    d_norm_w = jnp.stack([dvec_e[2], dvec_o[2]])
    small_names = ["norm_w", "ada_b", "a_q_norm", "a_k_norm", "b_q_lora_norm", "b_kv_lora_norm", "b_w_uk", "b_w_uv",
                   "c_sink", "final_norm"]
    small_w = [norm_w, ada_b, a_q_norm, a_k_norm, b_q_lora_norm, b_kv_lora_norm, b_w_uk, b_w_uv, c_sink, final_norm]
    small_m = [m_norm_w, m_ada_b, m_a_q_norm, m_a_k_norm, m_b_q_lora_norm, m_b_kv_lora_norm, m_b_w_uk, m_b_w_uv,
               m_c_sink, m_final_norm]
    small_v = [v_norm_w, v_ada_b, v_a_q_norm, v_a_k_norm, v_b_q_lora_norm, v_b_kv_lora_norm, v_b_w_uk, v_b_w_uv,
               v_c_sink, v_final_norm]
    small_g = [d_norm_w, dmod, g_qn, g_kn, g_qln, g_kvln, jnp.transpose(dw_uk, (1, 0, 2)), jnp.transpose(dw_uv, (1, 0, 2)),
               dsink, d_final]
    flat2 = lambda a: a.reshape((1, -1)) if a.size == a.shape[-1] else a.reshape(a.shape[-3:] if a.ndim > 3 else a.shape)
    kshape = [flat2(w).shape for w in small_w]
    g_all = all_gather_slots(
        Gather([g.reshape(sh) for g, sh in zip(small_g, kshape)] + [jnp.full((8, 128), loss_part, F32)]),
        "gather_small_grads")
    sm_out, loss_sum = adamw_small(g_all[:-1], [flat2(a) for a in small_w], [flat2(a) for a in small_m],
                                   [flat2(a) for a in small_v], g_all[-1])
    loss = loss_sum[0, 0]
    sm = [{nm: p.reshape(w.shape) for nm, w, p in zip(small_names, small_w, outs)} for outs in sm_out]

    dmod_all = g_all[1].reshape(N_DEV, 2, N_DEV, wcols)
    dmod_cols = lax.dynamic_slice_in_dim(dmod_all, me_flat, 1, axis=2)[:, :, 0, :]
    pad16 = lambda a: jnp.concatenate([a, jnp.zeros_like(a)], axis=0)
    g_ada_w = ada_weight_grad(pad16(c_all), jnp.transpose(pad16(dmod_cols), (1, 0, 2)))
    rows_of = lambda a: a.reshape(-1, wcols)
    ada = adamw_rows(rows_of(g_ada_w)[None], rows_of(ada_w), rows_of(m_ada_w), rows_of(v_ada_w), "adamw_ada_w")
    ada = [p.reshape(ada_w.shape) for p in ada]

    bg = [{}, {}, {}, {}]
    for nm, landed, w, m, v, transposed in (
            ("even_w_in", l_in_e, even_w_in, m_even_w_in, v_even_w_in, True),
            ("b_w_uq", l_uq, b_w_uq, m_b_w_uq, v_b_w_uq, True),
            ("odd_w_in", l_in_o, odd_w_in, m_odd_w_in, v_odd_w_in, True),
            ("even_w_out", l_out_e, even_w_out, m_even_w_out, v_even_w_out, False),
            ("odd_w_out", l_out_o, odd_w_out, m_odd_w_out, v_odd_w_out, False)):
        view = _shard_t if transposed else (lambda a: a[0])
        res = adamw_rows(landed, view(w), view(m), view(v), "adamw_" + nm)
        for kind, p in enumerate(res):
            bg[kind][nm] = _unshard_t(p, w) if transposed else p[None]
    big_names = ["even_w_in", "odd_w_in", "even_w_out", "odd_w_out", "b_w_uq"]

    order = ["norm_w", "ada_w", "ada_b", "even_w_in", "a_q_norm", "a_k_norm", "b_q_lora_norm", "b_kv_lora_norm",
             "b_w_uq", "b_w_uk", "b_w_uv", "even_w_out", "odd_w_in", "c_sink", "odd_w_out", "final_norm"]

    def pick(kind):
        out = []
        for nm in order:
            if nm == "ada_w":
                out.append(ada[kind])
            elif nm in big_names:
                out.append(bg[kind][nm])
            else:
                out.append(sm[kind][nm])
        return out

    return (loss, dx0[None], *pick(0), *pick(1), *pick(2), *pick(3))
```

```python
import functools

import jax
import jax.numpy as jnp
import numpy as np
from jax import lax
from jax.experimental import pallas as pl
from jax.experimental.pallas import tpu as pltpu

F32 = jnp.float32
MXU = jnp.bfloat16
EPS = 1e-6
ROPE_THETA = 10000.0
GRID_W = 64
HD = 64
N_DEV = 8

A_HEADS, A_KV = 8, 2
B_HEADS, B_NOPE, B_ROPE, B_Q_LORA, B_KV_LORA = 8, 64, 32, 256, 128
B_QK = B_KV_LORA + B_ROPE
C_HEADS, C_KV = 16, 4
WINDOW = 128

ADAM_LR, ADAM_B1, ADAM_B2, ADAM_EPS, ADAM_WD, ADAM_STEP = 0.001, 0.9, 0.999, 1e-08, 0.01, 10

ROW_TILE = 512
ADAM_TILE = 2048 * 128

LOG2E = 1.4426950408889634
SCALE_A = HD ** -0.5
SCALE_B = (B_NOPE + B_ROPE) ** -0.5
SCALE2_A, SCALE2_B = SCALE_A * LOG2E, SCALE_B * LOG2E
VMEM_LIMIT = 56 * 1024 * 1024

E_QA, E_KA, E_VA, E_GA, E_CQ, E_CKV, E_GB, E_KR = (
    (0, 512), (512, 640), (640, 768), (768, 1280), (1280, 1536), (1536, 1664), (1664, 2176), (2176, 2208))
O_Q, O_K, O_V, O_G = (0, 1024), (1024, 1280), (1280, 1536), (1536, 2560)


def _mm(a, b):
    return jnp.dot(a.astype(MXU), b.astype(MXU), preferred_element_type=F32)


def _mm_nt(a, b):
    return lax.dot_general(a.astype(MXU), b.astype(MXU), (((1,), (1,)), ((), ())), preferred_element_type=F32)


def _mm_tn(a, b):
    return lax.dot_general(a.astype(MXU), b.astype(MXU), (((0,), (0,)), ((), ())), preferred_element_type=F32)


def _group_sums_t(prod, group):
    tm, w = prod.shape
    sel = (lax.broadcasted_iota(jnp.int32, (w, 128), 0) // group
           == lax.broadcasted_iota(jnp.int32, (w, 128), 1)).astype(MXU)
    hi = prod.astype(MXU)
    lo = prod - hi.astype(F32)
    return (_mm(hi, sel) + _mm(lo, sel)).T


def _sigmoid(z):
    return 1.0 / (1.0 + jnp.exp(-z))


def _silu(z):
    return z * _sigmoid(z)


def _rms(x):
    return lax.rsqrt(jnp.mean(x * x, axis=-1, keepdims=True) + EPS)


def _swap_halves(y, group):
    n = y.shape[-1]
    half = group // 2
    fwd = pltpu.roll(y, half, 1)
    if n == group:
        return fwd
    back = pltpu.roll(y, n - half, 1)
    lane = lax.broadcasted_iota(jnp.int32, y.shape, 1)
    return jnp.where((lane % group) < half, back, fwd)


def _rope(y, cos, sin, group):
    return y * cos + _swap_halves(y, group) * sin


def _rope_t(d, cos, sin, group):
    return d * cos - _swap_halves(d, group) * sin


def _rms_bwd(dy, x, g):
    r = _rms(x)
    xhat = x * r
    dxhat = dy * g
    dx = r * (dxhat - xhat * jnp.mean(dxhat * xhat, axis=-1, keepdims=True))
    return dx, dy * xhat


def _group_mean(v, bd, group):
    hi = v.astype(MXU)
    lo = v - hi.astype(F32)
    return (_mm(hi, bd[...]) + _mm(lo, bd[...])) * (1.0 / group)


def _head_norm(x, g, bd, group):
    return x * lax.rsqrt(_group_mean(x * x, bd, group) + EPS) * g


def _head_norm_bwd(dy, x, g, bd, group):
    r = lax.rsqrt(_group_mean(x * x, bd, group) + EPS)
    xhat = x * r
    dxhat = dy * g
    dx = r * (dxhat - xhat * _group_mean(dxhat * xhat, bd, group))
    return dx, dy * xhat


def _params(sem, vmem=VMEM_LIMIT):
    return pltpu.CompilerParams(dimension_semantics=sem, vmem_limit_bytes=vmem)


def _row_spec(tm, w):
    return pl.BlockSpec((tm, w), lambda i: (i, 0))


def _full_spec(shape):
    nd = len(shape)
    return pl.BlockSpec(shape, lambda i: (0,) * nd)


def _head_spec(h, tm, w):
    return pl.BlockSpec((h, tm, w), lambda i: (0, i, 0))


def _headt_spec(h, w, tm):
    return pl.BlockSpec((h, w, tm), lambda i: (0, 0, i))


def _rows_spec(h, tm):
    return pl.BlockSpec((h, tm), lambda i: (0, i))


def _me():
    return lax.axis_index("x"), lax.axis_index("y"), lax.axis_index("c")


def _flat(p):
    return 4 * p[0] + 2 * p[1] + p[2]


def _peer(me, k):
    x, y, c = me
    return (1 - x if k & 4 else x, 1 - y if k & 2 else y, 1 - c if k & 1 else c)


MESH_ID = pl.DeviceIdType.MESH


class Gather:
    VMEM = pl.BlockSpec(memory_space=pltpu.VMEM)

    def __init__(self, shards):
        self.shards = list(shards)
        self.n = len(self.shards)
        self.out_shapes = tuple(jax.ShapeDtypeStruct((N_DEV,) + a.shape, a.dtype) for a in self.shards)
        self.in_specs = [Gather.VMEM] * self.n
        self.out_specs = (Gather.VMEM,) * self.n
        self.sems = [pltpu.SemaphoreType.DMA((7 * self.n,)), pltpu.SemaphoreType.DMA((7 * self.n,)),
                     pltpu.SemaphoreType.DMA((self.n,))]

    def _plan(self, x_refs, out_refs, sems):
        send_sems, recv_sems, local_sems = sems
        me = _me()
        x, y, c = me
        chips = [(1 - x, y), (x, 1 - y), (1 - x, 1 - y)]

        def copy(a, k, block, to, src=None):
            slot = out_refs[a].at[_flat(block)]
            return pltpu.make_async_remote_copy(
                src_ref=slot if src is None else src, dst_ref=slot, send_sem=send_sems.at[7 * a + k],
                recv_sem=recv_sems.at[7 * a + k], device_id=to, device_id_type=MESH_ID)

        mine = [pltpu.make_async_copy(x_refs[a], out_refs[a].at[_flat(me)], local_sems.at[a]) for a in range(self.n)]
        first = [copy(a, 0, me, (x, y, 1 - c), src=x_refs[a]) for a in range(self.n)]
        first += [copy(a, 1 + j, me, (*chip, c), src=x_refs[a]) for a in range(self.n) for j, chip in enumerate(chips)]
        return me, chips, copy, mine, first

    def start(self, x_refs, out_refs, sems):
        _, _, _, mine, first = self._plan(x_refs, out_refs, sems)
        for cp in mine + first:
            cp.start()

    def finish(self, x_refs, out_refs, sems):
        me, chips, copy, mine, first = self._plan(x_refs, out_refs, sems)
        x, y, c = me
        sibling = (x, y, 1 - c)
        passed = []
        for a in range(self.n):
            for j, chip in enumerate(chips):
                copy(a, 1 + j, (*chip, c), me).wait_recv()
                passed.append(copy(a, 4 + j, (*chip, c), sibling))
                passed[-1].start()
        for a in range(self.n):
            copy(a, 0, sibling, me).wait_recv()
            for j, chip in enumerate(chips):
                copy(a, 4 + j, (*chip, 1 - c), me).wait_recv()
        for cp in first + passed:
            cp.wait_send()
        for cp in mine:
            cp.wait()


def all_gather_slots(gather, name):
    def body(*refs):
        x_refs, out_refs, sems = refs[:gather.n], refs[gather.n:2 * gather.n], refs[2 * gather.n:]
        gather.start(x_refs, out_refs, sems)
        gather.finish(x_refs, out_refs, sems)

    return pl.pallas_call(
        body, name=name, out_shape=gather.out_shapes, in_specs=gather.in_specs, out_specs=gather.out_specs,
        scratch_shapes=list(gather.sems), compiler_params=pltpu.CompilerParams(vmem_limit_bytes=VMEM_LIMIT),
    )(*gather.shards)


class Exchange:
    HBM = pl.BlockSpec(memory_space=pl.ANY)

    def __init__(self, srcs, scatter):
        self.srcs = list(srcs)
        self.scatter = scatter
        self.n = len(self.srcs)
        self.land_shapes = tuple(jax.ShapeDtypeStruct((N_DEV,) + tuple(a.shape[-2:]), a.dtype) for a in self.srcs)
        self.in_specs = [Exchange.HBM] * self.n
        self.out_specs = (Exchange.HBM,) * self.n
        self.sems = [pltpu.SemaphoreType.DMA((N_DEV - 1,)), pltpu.SemaphoreType.DMA((N_DEV - 1,)),
                     pltpu.SemaphoreType.DMA] * self.n

    def _copies(self, src_refs, land_refs, sems):
        me = _me()
        mi = _flat(me)
        local, sends, recvs = [], [], []
        for a, (src_ref, land_ref) in enumerate(zip(src_refs, land_refs)):
            send_sems, recv_sems, local_sem = sems[3 * a:3 * a + 3]
            pick = (lambda p, r=src_ref: r.at[_flat(p)]) if self.scatter else (lambda p, r=src_ref: r)
            local.append(pltpu.make_async_copy(pick(me), land_ref.at[mi], local_sem))
            for k in range(1, N_DEV):
                peer = _peer(me, k)
                pair = dict(send_sem=send_sems.at[k - 1], recv_sem=recv_sems.at[k - 1], device_id=peer,
                            device_id_type=MESH_ID)
                sends.append(pltpu.make_async_remote_copy(src_ref=pick(peer), dst_ref=land_ref.at[mi], **pair))
                recvs.append(pltpu.make_async_remote_copy(src_ref=pick(peer), dst_ref=land_ref.at[_flat(peer)],
                                                          **pair))
        return local, sends, recvs

    def start(self, src_refs, land_refs, sems):
        local, sends, _ = self._copies(src_refs, land_refs, sems)
        for cp in local + sends:
            cp.start()

    def wait(self, src_refs, land_refs, sems):
        local, sends, recvs = self._copies(src_refs, land_refs, sems)
        for cp in recvs:
            cp.wait_recv()
        for cp in sends:
            cp.wait_send()
        for cp in local:
            cp.wait()


def ada_forward(c8, ada_w, bias_cols, gather):
    d = c8.shape[1]
    w = ada_w.shape[2]
    ng = gather.n

    def body(*refs):
        c_ref, w_ref, b_ref = refs[:3]
        gx_refs = refs[3:3 + ng]
        call_ref, modp_ref = refs[3 + ng:5 + ng]
        gout_refs = refs[5 + ng:5 + 2 * ng]
        part_ref, s1, r1, s2, r2 = refs[5 + 2 * ng:10 + 2 * ng]
        g_sems = refs[10 + 2 * ng:]
        gather.start(gx_refs, gout_refs, g_sems)
        me = _me()
        mi = _flat(me)
        call_ref[mi] = c_ref[...]
        rows_out = []
        for k in range(1, N_DEV):
            rows_out.append(pltpu.make_async_remote_copy(
                src_ref=c_ref, dst_ref=call_ref.at[mi], send_sem=s1.at[k - 1], recv_sem=r1.at[k - 1],
                device_id=_peer(me, k), device_id_type=MESH_ID))
        for cp in rows_out:
            cp.start()
        for k in range(1, N_DEV):
            pltpu.make_async_remote_copy(
                src_ref=c_ref, dst_ref=call_ref.at[_flat(_peer(me, k))], send_sem=s1.at[k - 1],
                recv_sem=r1.at[k - 1], device_id=_peer(me, k), device_id_type=MESH_ID).wait_recv()
        ca = _silu(call_ref[...].reshape(N_DEV * 8, d))
        for l in range(2):
            part = _mm(ca, w_ref[l]) + b_ref[l]
            for b in range(N_DEV):
                part_ref[b, l] = part[8 * b:8 * b + 8, :]
        modp_ref[mi] = part_ref[mi]
        spread = []
        for k in range(1, N_DEV):
            peer = _peer(me, k)
            spread.append(pltpu.make_async_remote_copy(
                src_ref=part_ref.at[_flat(peer)], dst_ref=modp_ref.at[mi], send_sem=s2.at[k - 1],
                recv_sem=r2.at[k - 1], device_id=peer, device_id_type=MESH_ID))
        for cp in spread:
            cp.start()
        for k in range(1, N_DEV):
            pi = _flat(_peer(me, k))
            pltpu.make_async_remote_copy(
                src_ref=part_ref.at[pi], dst_ref=modp_ref.at[pi], send_sem=s2.at[k - 1],
                recv_sem=r2.at[k - 1], device_id=_peer(me, k), device_id_type=MESH_ID).wait_recv()
        for cp in rows_out + spread:
            cp.wait_send()
        gather.finish(gx_refs, gout_refs, g_sems)

    vm = pl.BlockSpec(memory_space=pltpu.VMEM)
    res = pl.pallas_call(
        body, name="ada_forward",
        out_shape=(jax.ShapeDtypeStruct((N_DEV, 8, d), F32), jax.ShapeDtypeStruct((N_DEV, 2, 8, w), F32))
        + gather.out_shapes,
        in_specs=[vm, vm, vm] + gather.in_specs, out_specs=(vm, vm) + gather.out_specs,
        scratch_shapes=[pltpu.VMEM((N_DEV, 2, 8, w), F32)] + [pltpu.SemaphoreType.DMA((7,))] * 4 + list(gather.sems),
        compiler_params=pltpu.CompilerParams(vmem_limit_bytes=VMEM_LIMIT),
    )(c8, ada_w, bias_cols, *gather.shards)
    return res[0], res[1], res[2:]


def _modulated(x, mod_ref, nw_ref):
    xn = x * _rms(x)
    g1 = nw_ref[...] * (1.0 + mod_ref[1:2, :])
    return xn, g1, xn * g1 + mod_ref[0:1, :]


def even_in_forward(x, mod, nw, w_in_t, gq, gk, qln, kvln, w_uq_t, uk_bd, bd, cos_a, sin_a, cos_t, sin_t):
    s, d = x.shape
    tm = min(ROW_TILE, s)
    n_nope = B_HEADS * B_NOPE

    def body(x_ref, mod_ref, nw_ref, w_ref, gq_ref, gk_ref, qln_ref, kvln_ref, uq_ref, ukbd_ref, bd_ref,
             ca_ref, sa_ref, ct_ref, st_ref,
             qa_o, ka_o, va_o, qb_o, kb_o, kat_o, vat_o, kbt_o, qa_raw_o, ka_raw_o, cq_raw_o, ckv_raw_o, ga_o, gb_o):
        _, _, h = _modulated(x_ref[...], mod_ref, nw_ref)
        h = h.astype(MXU)

        def proj(cols):
            return _mm_nt(h, w_ref[cols[0]:cols[1], :])

        ca, sa, ct, st = ca_ref[...], sa_ref[...], ct_ref[...], st_ref[...]
        wide = lambda t, n: jnp.concatenate([t] * n, axis=1)
        qa = proj(E_QA)
        qa_raw_o[...] = qa
        qr = _rope(_head_norm(qa, gq_ref[...], bd_ref, HD), wide(ca, 4), wide(sa, 4), 32) * SCALE2_A
        for hh in range(A_HEADS):
            qa_o[hh] = qr[:, HD * hh:HD * hh + HD].astype(MXU)
        ka = proj(E_KA)
        ka_raw_o[...] = ka
        kr = _rope(_head_norm(ka, gk_ref[...], bd_ref[0:128, 0:128], HD), ca, sa, 32)
        va = proj(E_VA)
        krt, vat = kr.T, va.T
        for g in range(A_KV):
            ka_o[g] = kr[:, HD * g:HD * g + HD].astype(MXU)
            va_o[g] = va[:, HD * g:HD * g + HD].astype(MXU)
            kat_o[g] = krt[HD * g:HD * g + HD, :].astype(MXU)
            vat_o[g] = vat[HD * g:HD * g + HD, :].astype(MXU)
        ga_o[...] = proj(E_GA)
        gb_o[...] = proj(E_GB)
        cq = proj(E_CQ)
        cq_raw_o[...] = cq
        qb = _mm_nt(cq * _rms(cq) * qln_ref[...], uq_ref[...])
        q_lat = _mm(qb[:, 0:n_nope], ukbd_ref[...]) * SCALE2_B
        q_rope = _rope(qb[:, n_nope:], wide(ct, 2), wide(st, 2), 32) * SCALE2_B
        for hh in range(B_HEADS):
            qb_o[hh, :, 0:B_KV_LORA] = q_lat[:, B_KV_LORA * hh:B_KV_LORA * (hh + 1)].astype(MXU)
            qb_o[hh, :, B_KV_LORA:B_QK] = q_rope[:, B_ROPE * hh:B_ROPE * (hh + 1)].astype(MXU)
        ckv = proj(E_CKV)
        ckv_raw_o[...] = ckv
        ckv_n = ckv * _rms(ckv) * kvln_ref[...]
        k_rope = _rope(proj(E_KR), ct[:, 0:B_ROPE], st[:, 0:B_ROPE], 32)
        kb_o[0, :, 0:B_KV_LORA] = ckv_n.astype(MXU)
        kb_o[0, :, B_KV_LORA:B_QK] = k_rope.astype(MXU)
        kbt_o[0, 0:B_KV_LORA, :] = ckv_n.T.astype(MXU)
        kbt_o[0, B_KV_LORA:B_QK, :] = k_rope.T.astype(MXU)

    sd = jax.ShapeDtypeStruct
    outs = (sd((A_HEADS, s, HD), MXU), sd((A_KV, s, HD), MXU), sd((A_KV, s, HD), MXU),
            sd((B_HEADS, s, B_QK), MXU), sd((1, s, B_QK), MXU),
            sd((A_KV, HD, s), MXU), sd((A_KV, HD, s), MXU), sd((1, B_QK, s), MXU),
            sd((s, 512), F32), sd((s, 128), F32), sd((s, B_Q_LORA), F32), sd((s, B_KV_LORA), F32),
            sd((s, 512), F32), sd((s, 512), F32))
    out_specs = (_head_spec(A_HEADS, tm, HD), _head_spec(A_KV, tm, HD), _head_spec(A_KV, tm, HD),
                 _head_spec(B_HEADS, tm, B_QK), _head_spec(1, tm, B_QK),
                 _headt_spec(A_KV, HD, tm), _headt_spec(A_KV, HD, tm), _headt_spec(1, B_QK, tm),
                 _row_spec(tm, 512), _row_spec(tm, 128), _row_spec(tm, B_Q_LORA), _row_spec(tm, B_KV_LORA),
                 _row_spec(tm, 512), _row_spec(tm, 512))
    consts = [mod, nw, w_in_t, gq, gk, qln, kvln, w_uq_t, uk_bd, bd]
    return pl.pallas_call(
        body, name="even_in_forward", grid=(s // tm,), out_shape=outs,
        in_specs=[_row_spec(tm, d)] + [_full_spec(a.shape) for a in consts] + [_row_spec(tm, 128)] * 4,
        out_specs=out_specs, compiler_params=_params(("parallel",)),
    )(x, *consts, cos_a, sin_a, cos_t, sin_t)


def odd_in_forward(x, mod, nw, w_in):
    s, d = x.shape
    tm = min(ROW_TILE, s)

    def body(x_ref, mod_ref, nw_ref, w_ref, q_o, k_o, v_o, kt_o, vt_o, g_o):
        _, _, h = _modulated(x_ref[...], mod_ref, nw_ref)
        h = h.astype(MXU)

        def proj(cols):
            return _mm_nt(h, w_ref[cols[0]:cols[1], :])

        q = proj(O_Q) * SCALE2_A
        for hh in range(C_HEADS):
            q_o[hh] = q[:, HD * hh:HD * hh + HD].astype(MXU)
        k = proj(O_K)
        v = proj(O_V)
        for g in range(C_KV):
            kh = k[:, HD * g:HD * g + HD]
            vh = v[:, HD * g:HD * g + HD]
            k_o[g] = kh.astype(MXU)
            v_o[g] = vh.astype(MXU)
            kt_o[g] = kh.T.astype(MXU)
            vt_o[g] = vh.T.astype(MXU)
        g_o[...] = proj(O_G)

    sd = jax.ShapeDtypeStruct
    return pl.pallas_call(
        body, name="odd_in_forward", grid=(s // tm,),
        out_shape=(sd((C_HEADS, s, HD), MXU), sd((C_KV, s, HD), MXU), sd((C_KV, s, HD), MXU),
                   sd((C_KV, HD, s), MXU), sd((C_KV, HD, s), MXU), sd((s, 1024), F32)),
        in_specs=[_row_spec(tm, d), _full_spec(mod.shape), _full_spec(nw.shape), _full_spec(w_in.shape)],
        out_specs=(_head_spec(C_HEADS, tm, HD), _head_spec(C_KV, tm, HD), _head_spec(C_KV, tm, HD),
                   _headt_spec(C_KV, HD, tm), _headt_spec(C_KV, HD, tm), _row_spec(tm, 1024)),
        compiler_params=_params(("parallel",)),
    )(x, mod, nw, w_in)


def latent_out_forward(o_lat, w_uv):
    s = o_lat.shape[0]
    tm = min(ROW_TILE, s)

    def body(o_ref, uv_ref, out_ref):
        for hh in range(B_HEADS):
            out_ref[:, HD * hh:HD * hh + HD] = _mm(o_ref[:, B_KV_LORA * hh:B_KV_LORA * (hh + 1)], uv_ref[hh])

    return pl.pallas_call(
        body, name="latent_out_forward", grid=(s // tm,),
        out_shape=jax.ShapeDtypeStruct((s, B_HEADS * HD), F32),
        in_specs=[_row_spec(tm, o_lat.shape[1]), _full_spec(w_uv.shape)],
        out_specs=_row_spec(tm, B_HEADS * HD),
        compiler_params=_params(("parallel",)),
    )(o_lat, w_uv)


def mixer_out_forward(x, mod, pairs, w_out, name):
    s, d = x.shape
    tm = min(ROW_TILE, s)
    n = len(pairs)
    widths = [o.shape[1] for o, _ in pairs]

    def body(*refs):
        x_ref, mod_ref, w_ref = refs[:3]
        pr = refs[3:3 + 2 * n]
        xo_ref, y_ref = refs[3 + 2 * n:]
        y = jnp.zeros((tm, d), F32)
        r0 = 0
        for i in range(n):
            mix = pr[2 * i][...] * _silu(pr[2 * i + 1][...])
            y = y + _mm(mix, w_ref[r0:r0 + widths[i], :])
            r0 += widths[i]
        y_ref[...] = y
        xo_ref[...] = x_ref[...] + mod_ref[2:3, :] * y

    flat = [a for p in pairs for a in p]
    sd = jax.ShapeDtypeStruct
    return pl.pallas_call(
        body, name=name, grid=(s // tm,),
        out_shape=(sd((s, d), F32), sd((s, d), F32)),
        in_specs=[_row_spec(tm, d), _full_spec(mod.shape), _full_spec(w_out.shape)]
        + [_row_spec(tm, a.shape[1]) for a in flat],
        out_specs=(_row_spec(tm, d), _row_spec(tm, d)),
        compiler_params=_params(("parallel",)),
    )(x, mod, w_out, *flat)


ONES_ROWS = 16
AHEAD = 2


def _col_max8(s3):
    m8 = jnp.max(s3, axis=0)
    return jnp.broadcast_to(jnp.max(m8, axis=0, keepdims=True), m8.shape)


def _with_ones(vt, n):
    return jnp.concatenate([vt, jnp.ones((ONES_ROWS, n), vt.dtype)], axis=0)


def _grid_edges(grid):
    ids = [pl.program_id(a) for a in range(len(grid))]
    first = functools.reduce(jnp.logical_and, [i == 0 for i in ids])
    last = functools.reduce(jnp.logical_and, [i == n - 1 for i, n in zip(ids, grid)])
    return first, last


def flash_forward(q, k, vt, *, dv, tq, tk, nsub, name, exchange=None):
    hq, s, dq = q.shape
    g_kv = k.shape[0]
    hpg = hq // g_kv
    nq = s // tq
    tkk = tk * nsub
    nk = s // tkk
    grid = (g_kv, nq, nk)
    hosted = exchange is not None
    m_cols = hpg * tq
    dvp = dv + ONES_ROWS

    def body(*refs):
        nx = exchange.n if hosted else 0
        q_ref, k_ref, vt_ref = refs[:3]
        xs_refs = refs[3:3 + nx]
        o_ref, lse_ref = refs[3 + nx:5 + nx]
        land_refs = refs[5 + nx:5 + 2 * nx]
        m_s, acc_s = refs[5 + 2 * nx:7 + 2 * nx]
        sems = refs[7 + 2 * nx:]
        if hosted:
            first, last = _grid_edges(grid)
            pl.when(first)(lambda: exchange.start(xs_refs, land_refs, sems))
        j = pl.program_id(2)

        @pl.when(j == 0)
        def _():
            m_s[...] = jnp.full((8, m_cols), -jnp.inf, F32)
            acc_s[...] = jnp.zeros((dvp, m_cols), F32)

        qq = q_ref[...].reshape(m_cols, dq)
        score = lambda u: _mm_nt(k_ref[0, tk * u:tk * (u + 1), :], qq).reshape(tk // 8, 8, m_cols)
        sts = {u: score(u) for u in range(min(AHEAD, nsub))}
        m_run = m_s[...]
        acc = acc_s[...]
        for u in range(nsub):
            if u + AHEAD < nsub:
                sts[u + AHEAD] = score(u + AHEAD)
            st = sts.pop(u)
            m_new = jnp.maximum(m_run, _col_max8(st))
            p = jnp.exp2(st - m_new[None])
            alpha = jnp.exp2(m_run - m_new)
            pv = _mm(_with_ones(vt_ref[0, 0:dv, tk * u:tk * (u + 1)], tk), p.reshape(tk, m_cols))
            acc = (acc.reshape(dvp // 8, 8, m_cols) * alpha[None]).reshape(dvp, m_cols) + pv
            m_run = m_new
        acc_s[...] = acc
        m_s[...] = m_run

        @pl.when(j == nk - 1)
        def _():
            l = acc_s[dv:dv + 1, :]
            ot = acc_s[0:dv, :] / l
            lse = m_s[0:1, :] + jnp.log2(l)
            for hh in range(hpg):
                o_ref[:, dv * hh:dv * hh + dv] = ot[:, tq * hh:tq * hh + tq].T
                lse_ref[hh] = lse[:, tq * hh:tq * hh + tq]

        if hosted:
            pl.when(last)(lambda: exchange.wait(xs_refs, land_refs, sems))

    sd = jax.ShapeDtypeStruct
    return pl.pallas_call(
        body, name=name, grid=grid,
        out_shape=(sd((s, hq * dv), F32), sd((hq, 1, s), F32)) + (exchange.land_shapes if hosted else ()),
        in_specs=[pl.BlockSpec((hpg, tq, dq), lambda g, i, j: (g, i, 0)),
                  pl.BlockSpec((1, tkk, k.shape[2]), lambda g, i, j: (g, j, 0)),
                  pl.BlockSpec((1, dv, tkk), lambda g, i, j: (g, 0, j))] + (exchange.in_specs if hosted else []),
        out_specs=(pl.BlockSpec((tq, hpg * dv), lambda g, i, j: (i, g)),
                   pl.BlockSpec((hpg, 1, tq), lambda g, i, j: (g, 0, i))) + (exchange.out_specs if hosted else ()),
        scratch_shapes=[pltpu.VMEM((8, m_cols), F32), pltpu.VMEM((dvp, m_cols), F32)]
        + (list(exchange.sems) if hosted else []),
        compiler_params=_params(("arbitrary",) * 3 if hosted else ("parallel", "parallel", "arbitrary")),
    )(q, k, vt, *(exchange.srcs if hosted else []))


def _window_bias_t(hpg, slope_ref):
    t = WINDOW
    r = lax.broadcasted_iota(jnp.int32, (3 * t, t), 0)
    cq = lax.broadcasted_iota(jnp.int32, (3 * t, t), 1)
    arel = jnp.abs(r - t - cq)
    base = jnp.where(arel <= WINDOW, arel.astype(F32) * (-LOG2E), -jnp.inf)
    return jnp.concatenate([base * slope_ref[hh] for hh in range(hpg)], axis=1)


def _window_edges_t(bias, no_before, no_after):
    t = WINDOW
    r = lax.broadcasted_iota(jnp.int32, bias.shape, 0)
    out = ((r < t) & no_before) | ((r >= 2 * t) & no_after)
    return jnp.where(out, -jnp.inf, bias)


def _window_specs(kind, nb, nblk, d):
    t = WINDOW
    before = lambda i: jnp.clip(i * nb - 1, 0, nblk - 1)
    after = lambda i: jnp.clip((i + 1) * nb, 0, nblk - 1)
    if kind == "rows":
        return [pl.BlockSpec((1, t, d), lambda g, i: (g, before(i), 0)),
                pl.BlockSpec((1, nb * t, d), lambda g, i: (g, i, 0)),
                pl.BlockSpec((1, t, d), lambda g, i: (g, after(i), 0))]
    return [pl.BlockSpec((1, d, t), lambda g, i: (g, 0, before(i))),
            pl.BlockSpec((1, d, nb * t), lambda g, i: (g, 0, i)),
            pl.BlockSpec((1, d, t), lambda g, i: (g, 0, after(i)))]


def window_forward(q, k, vt, sink2, slopes, nb, name):
    hq, s, d = q.shape
    g_kv = k.shape[0]
    hpg = hq // g_kv
    t = WINDOW
    nblk = s // t
    steps = nblk // nb
    m_cols = hpg * t

    def body(q_ref, kp, ko, kn, vp, vo, vn, sink_ref, slope_ref, o_ref, lse_ref):
        i = pl.program_id(1)
        kk_all = jnp.concatenate([kp[0], ko[0], kn[0]], axis=0)
        vt_all = jnp.concatenate([vp[0], vo[0], vn[0]], axis=1)
        bias = _window_bias_t(hpg, slope_ref)
        sink_row = jnp.concatenate([jnp.broadcast_to(sink_ref[hh], (8, t)) for hh in range(hpg)], axis=1)
        sts = []
        for u in range(nb):
            qq = q_ref[:, t * u:t * (u + 1), :].reshape(m_cols, d)
            b_u = bias
            if u == 0 or u == nb - 1:
                b_u = _window_edges_t(bias, (i == 0) if u == 0 else False,
                                      (i == steps - 1) if u == nb - 1 else False)
            sts.append(_mm_nt(kk_all[t * u:t * (u + 3), :], qq) + b_u)
        for u in range(nb):
            s3 = sts[u].reshape(3 * t // 8, 8, m_cols)
            m8 = jnp.maximum(_col_max8(s3), sink_row)
            p = jnp.exp2(s3 - m8[None]).reshape(3 * t, m_cols)
            acc = _mm(_with_ones(vt_all[:, t * u:t * (u + 3)], 3 * t), p)
            l = acc[d:d + 1, :] + jnp.exp2(sink_row[0:1, :] - m8[0:1, :])
            ot = acc[0:d, :] / l
            lse = m8[0:1, :] + jnp.log2(l)
            for hh in range(hpg):
                o_ref[t * u:t * (u + 1), d * hh:d * hh + d] = ot[:, t * hh:t * hh + t].T
                lse_ref[hh, :, t * u:t * (u + 1)] = lse[:, t * hh:t * hh + t]

    sd = jax.ShapeDtypeStruct
    return pl.pallas_call(
        body, name=name, grid=(g_kv, steps),
        out_shape=(sd((s, hq * d), F32), sd((hq, 1, s), F32)),
        in_specs=[pl.BlockSpec((hpg, nb * t, d), lambda g, i: (g, i, 0))]
        + _window_specs("rows", nb, nblk, d) + _window_specs("cols", nb, nblk, d)
        + [pl.BlockSpec((hpg, 1, 1), lambda g, i: (g, 0, 0))] * 2,
        out_specs=(pl.BlockSpec((nb * t, hpg * d), lambda g, i: (i, g)),
                   pl.BlockSpec((hpg, 1, nb * t), lambda g, i: (g, 0, i))),
        compiler_params=_params(("parallel", "parallel")),
    )(q, k, k, k, vt, vt, vt, sink2, slopes)


def window_backward(q, k, kt, v, do, lse, delta, slopes, nb, name):
    hq, s, d = q.shape
    g_kv = k.shape[0]
    hpg = hq // g_kv
    t = WINDOW
    nblk = s // t
    steps = nblk // nb
    m_cols = hpg * t

    def body(q_ref, kp, ko, kn, ktp, kto, ktn, vp, vo, vn, do_ref, lse_ref, dl_ref, slope_ref,
             dq_ref, dk_ref, dv_ref, dk_s, dv_s):
        i = pl.program_id(1)

        @pl.when(i == 0)
        def _():
            dk_ref[...] = jnp.zeros(dk_ref.shape, F32)
            dv_ref[...] = jnp.zeros(dv_ref.shape, F32)

        dk_s[...] = jnp.zeros(dk_s.shape, F32)
        dv_s[...] = jnp.zeros(dv_s.shape, F32)
        kk_all = jnp.concatenate([kp[0], ko[0], kn[0]], axis=0)
        vv_all = jnp.concatenate([vp[0], vo[0], vn[0]], axis=0)
        kkt_all = jnp.concatenate([ktp[0], kto[0], ktn[0]], axis=1)
        bias = _window_bias_t(hpg, slope_ref)
        qqs, dds, sts, dps = [], [], [], []
        for u in range(nb):
            rows = slice(t * u, t * (u + 1))
            keys = slice(t * u, t * (u + 3))
            qqs.append(q_ref[:, rows, :].reshape(m_cols, d))
            dds.append(jnp.concatenate([do_ref[rows, d * hh:d * hh + d] for hh in range(hpg)], axis=0))
            b_u = bias
            if u == 0 or u == nb - 1:
                b_u = _window_edges_t(bias, (i == 0) if u == 0 else False,
                                      (i == steps - 1) if u == nb - 1 else False)
            sts.append(_mm_nt(kk_all[keys, :], qqs[u]) + b_u)
            dps.append(_mm_nt(vv_all[keys, :], dds[u]))
        for u in range(nb):
            rows = slice(t * u, t * (u + 1))
            keys = slice(t * u, t * (u + 3))
            lse_row = jnp.concatenate([lse_ref[hh, :, rows] for hh in range(hpg)], axis=1)
            dl_row = jnp.concatenate([dl_ref[hh, :, rows] for hh in range(hpg)], axis=1)
            p = jnp.exp2(sts[u] - lse_row)
            ds = p * (dps[u] - dl_row) * SCALE_A
            dv_s[keys, :] += _mm(p, dds[u])
            dk_s[keys, :] += _mm(ds, qqs[u])
            dqt = _mm(kkt_all[:, keys], ds)
            for hh in range(hpg):
                dq_ref[rows, d * hh:d * hh + d] = dqt[:, t * hh:t * hh + t].T
        tq = nb * t
        for src, r0, n in ((0, jnp.clip(i * nb - 1, 0, nblk - 1) * t, t), (t, i * tq, tq),
                           (t + tq, jnp.clip((i + 1) * nb, 0, nblk - 1) * t, t)):
            dst = pl.ds(pl.multiple_of(r0, t), n)
            dk_ref[0, dst, :] += dk_s[src:src + n, :] * (1.0 / SCALE2_A)
            dv_ref[0, dst, :] += dv_s[src:src + n, :]

    row_map = lambda g, i: (g, 0, i)
    sd = jax.ShapeDtypeStruct
    return pl.pallas_call(
        body, name=name, grid=(g_kv, steps),
        out_shape=(sd((s, hq * d), F32), sd((g_kv, s, d), F32), sd((g_kv, s, d), F32)),
        in_specs=[pl.BlockSpec((hpg, nb * t, d), lambda g, i: (g, i, 0))]
        + _window_specs("rows", nb, nblk, d) + _window_specs("cols", nb, nblk, d) + _window_specs("rows", nb, nblk, d)
        + [pl.BlockSpec((nb * t, hpg * d), lambda g, i: (i, g)), pl.BlockSpec((hpg, 1, nb * t), row_map),
           pl.BlockSpec((hpg, 1, nb * t), row_map), pl.BlockSpec((hpg, 1, 1), lambda g, i: (g, 0, 0))],
        out_specs=(pl.BlockSpec((nb * t, hpg * d), lambda g, i: (i, g)),
                   pl.BlockSpec((1, s, d), lambda g, i: (g, 0, 0)),
                   pl.BlockSpec((1, s, d), lambda g, i: (g, 0, 0))),
        scratch_shapes=[pltpu.VMEM(((nb + 2) * t, d), F32), pltpu.VMEM(((nb + 2) * t, d), F32)],
        compiler_params=_params(("parallel", "arbitrary")),
    )(q, k, k, k, kt, kt, kt, v, v, v, do, lse, delta, slopes)


def flash_backward(q, k, kt, v, do, lse, delta, *, scale, dv, tq, tk, nsub, gq, name, split=None, exchange=None):
    hq, s, dq = q.shape
    g_kv = k.shape[0]
    hpg = hq // gq
    nq = s // tq
    tqq = tq * nsub
    nqs = s // tqq
    nkb = s // tk
    grid = (gq, nkb, nqs)
    hosted = exchange is not None
    m_cols = hpg * tq
    c = scale * LOG2E
    has_v = v is not None

    def body(*refs):
        it = iter(refs)
        q_ref, k_ref, kt_ref = next(it), next(it), next(it)
        v_ref = next(it) if has_v else None
        do_ref, lse_ref, dl_ref = next(it), next(it), next(it)
        nx = exchange.n if hosted else 0
        xs_refs = [next(it) for _ in range(nx)]
        dq_ref, dk_ref, dv_ref = next(it), next(it), next(it)
        land_refs = [next(it) for _ in range(nx)]
        dqt_s = next(it)
        sems = list(it)
        kj = pl.program_id(1)
        qi = pl.program_id(2)
        if hosted:
            first, last = _grid_edges(grid)
            pl.when(first)(lambda: exchange.start(xs_refs, land_refs, sems))

        @pl.when((kj == 0) & (qi == 0))
        def _():
            dqt_s[...] = jnp.zeros(dqt_s.shape, F32)

        @pl.when(qi == 0)
        def _():
            dk_ref[...] = jnp.zeros(dk_ref.shape, F32)
            dv_ref[...] = jnp.zeros(dv_ref.shape, F32)

        kk = k_ref[0]
        vv = v_ref[0] if has_v else kk[:, :dv]
        qqs, dds, sts, dps = {}, {}, {}, {}

        def issue(u):
            rows = slice(tq * u, tq * (u + 1))
            qqs[u] = q_ref[:, rows, :].reshape(m_cols, dq)
            dds[u] = jnp.concatenate([do_ref[rows, dv * hh:dv * hh + dv] for hh in range(hpg)], axis=0)
            sts[u] = _mm_nt(kk, qqs[u])
            dps[u] = _mm_nt(vv, dds[u])

        for u in range(min(AHEAD, nsub)):
            issue(u)
        dv_acc = dv_ref[0]
        dk_acc = dk_ref[0]
        for u in range(nsub):
            if u + AHEAD < nsub:
                issue(u + AHEAD)
            rows = slice(tq * u, tq * (u + 1))
            lse_row = jnp.concatenate([lse_ref[hh, :, rows] for hh in range(hpg)], axis=1)
            dl_row = jnp.concatenate([dl_ref[hh, :, rows] for hh in range(hpg)], axis=1)
            p = jnp.exp2(sts[u] - lse_row)
            ds = p * (dps[u] - dl_row) * scale
            dv_acc = dv_acc + _mm(p, dds[u])
            dk_acc = dk_acc + _mm(ds, qqs[u])
            dqt = _mm(kt_ref[0], ds)
            for hh in range(hpg):
                dqt_s[qi * nsub + u, dq * hh:dq * hh + dq, :] += dqt[:, tq * hh:tq * hh + tq]
        dv_ref[0] = dv_acc
        dk_ref[0] = jnp.where(qi == nqs - 1, dk_acc * (1.0 / c), dk_acc)

        @pl.when((kj == nkb - 1) & (qi == nqs - 1))
        def _():
            def emit(t, carry):
                r0 = pl.multiple_of(t * tq, tq)
                for hh in range(hpg):
                    blk = dqt_s[t, dq * hh:dq * hh + dq, :].T
                    if split is None:
                        dq_ref[pl.ds(r0, tq), dq * hh:dq * hh + dq] = blk
                    else:
                        rest = dq - split
                        dq_ref[pl.ds(r0, tq), split * hh:split * (hh + 1)] = blk[:, 0:split]
                        dq_ref[pl.ds(r0, tq), hpg * split + rest * hh:hpg * split + rest * (hh + 1)] = blk[:, split:]
                return carry

            lax.fori_loop(0, nq, emit, 0)

        if hosted:
            pl.when(last)(lambda: exchange.wait(xs_refs, land_refs, sems))

    kv_of = lambda g: g * g_kv // gq
    in_specs = [pl.BlockSpec((hpg, tqq, dq), lambda g, kj, qi: (g, qi, 0)),
                pl.BlockSpec((1, tk, dq), lambda g, kj, qi: (kv_of(g), kj, 0)),
                pl.BlockSpec((1, dq, tk), lambda g, kj, qi: (kv_of(g), 0, kj))]
    args = [q, k, kt]
    if has_v:
        in_specs.append(pl.BlockSpec((1, tk, dv), lambda g, kj, qi: (kv_of(g), kj, 0)))
        args.append(v)
    row_map = lambda g, kj, qi: (g, 0, qi)
    in_specs += [pl.BlockSpec((tqq, hpg * dv), lambda g, kj, qi: (qi, g)),
                 pl.BlockSpec((hpg, 1, tqq), row_map), pl.BlockSpec((hpg, 1, tqq), row_map)]
    args += [do, lse, delta]
    if hosted:
        in_specs += exchange.in_specs
        args += exchange.srcs
    sd = jax.ShapeDtypeStruct
    return pl.pallas_call(
        body, name=name, grid=grid,
        out_shape=(sd((s, hq * dq), F32), sd((gq, s, dq), F32), sd((gq, s, dv), F32))
        + (exchange.land_shapes if hosted else ()),
        in_specs=in_specs,
        out_specs=(pl.BlockSpec((s, hpg * dq), lambda g, kj, qi: (0, g)),
                   pl.BlockSpec((1, tk, dq), lambda g, kj, qi: (g, kj, 0)),
                   pl.BlockSpec((1, tk, dv), lambda g, kj, qi: (g, kj, 0))) + (exchange.out_specs if hosted else ()),
        scratch_shapes=[pltpu.VMEM((nq, hpg * dq, tq), F32)] + (list(exchange.sems) if hosted else []),
        compiler_params=_params(("arbitrary",) * 3 if hosted else ("parallel", "arbitrary", "arbitrary")),
    )(*args)


def loss_head(x, target, fnw):
    s, d = x.shape
    tm = min(ROW_TILE, s)

    def body(x_ref, t_ref, w_ref, lp_ref, dx_ref, dw_ref):
        @pl.when(pl.program_id(0) == 0)
        def _():
            lp_ref[...] = jnp.zeros(lp_ref.shape, F32)
            dw_ref[...] = jnp.zeros(dw_ref.shape, F32)

        x = x_ref[...]
        g = w_ref[...]
        err = x * _rms(x) * g - t_ref[...]
        lp_ref[...] += jnp.sum(err * err, axis=0, keepdims=True)
        dx, dg = _rms_bwd(err * (1.0 / d), x, g)
        dx_ref[...] = dx
        dw_ref[...] += jnp.sum(dg, axis=0, keepdims=True)

    sd = jax.ShapeDtypeStruct
    return pl.pallas_call(
        body, name="loss_head", grid=(s // tm,),
        out_shape=(sd((1, d), F32), sd((s, d), F32), sd((1, d), F32)),
        in_specs=[_row_spec(tm, d), _row_spec(tm, d), _full_spec(fnw.shape)],
        out_specs=(_full_spec((1, d)), _row_spec(tm, d), _full_spec((1, d))),
        compiler_params=_params(("arbitrary",)),
    )(x, target, fnw)


def mixer_out_backward(dx, y, mod, pairs, w_out, delta_heads, name, lse=None, sink=None):
    s, d = dx.shape
    tm = min(ROW_TILE, s)
    n = len(pairs)
    widths = [o.shape[1] for o, _ in pairs]
    n_delta = sum(1 for h in delta_heads if h)
    with_sink = lse is not None

    def body(*refs):
        it = iter(refs)
        dx_ref, y_ref, mod_ref, wt_ref = next(it), next(it), next(it), next(it)
        pr = [next(it) for _ in range(2 * n)]
        lse_ref = next(it) if with_sink else None
        sink_ref = next(it) if with_sink else None
        outs = [next(it) for _ in range(2 * n)]
        dl_refs = [next(it) for _ in range(n_delta)]
        dgate_ref, dw_ref = next(it), next(it)
        dsink_ref = next(it) if with_sink else None

        @pl.when(pl.program_id(0) == 0)
        def _():
            dgate_ref[...] = jnp.zeros(dgate_ref.shape, F32)
            dw_ref[...] = jnp.zeros(dw_ref.shape, F32)
            if with_sink:
                dsink_ref[...] = jnp.zeros(dsink_ref.shape, F32)

        dxo = dx_ref[...]
        dgate_ref[...] += jnp.sum(dxo * y_ref[...], axis=0, keepdims=True)
        dy = (dxo * mod_ref[2:3, :]).astype(MXU)
        dmix = _mm_nt(dy, wt_ref[...])
        r0 = 0
        di = 0
        for i in range(n):
            o = pr[2 * i][...]
            g = pr[2 * i + 1][...]
            dm = dmix[:, r0:r0 + widths[i]]
            sg = _sigmoid(g)
            act = g * sg
            do = dm * act
            outs[2 * i][...] = do.astype(MXU)
            outs[2 * i + 1][...] = (dm * o * (sg * (1.0 + g * (1.0 - sg)))).astype(MXU)
            dw_ref[r0:r0 + widths[i], :] += _mm_tn(o * act, dy)
            if delta_heads[i]:
                dlt = _group_sums_t(do * o, HD)[0:delta_heads[i], :]
                dl_refs[di][...] = dlt
                if with_sink:
                    ps = jnp.exp2(sink_ref[...] - lse_ref[...])
                    dsink_ref[...] += -jnp.sum(ps * dlt, axis=1, keepdims=True)
                di += 1
            r0 += widths[i]

    flat = [a for p in pairs for a in p]
    sd = jax.ShapeDtypeStruct
    in_specs = [_row_spec(tm, d), _row_spec(tm, d), _full_spec(mod.shape), _full_spec(w_out.shape)]
    in_specs += [_row_spec(tm, a.shape[1]) for a in flat]
    args = [dx, y, mod, w_out] + flat
    if with_sink:
        nh = lse.shape[0]
        in_specs += [_rows_spec(nh, tm), _full_spec(sink.shape)]
        args += [lse, sink]
    out_shape = [sd((s, a.shape[1]), MXU) for a in flat]
    out_specs = [_row_spec(tm, a.shape[1]) for a in flat]
    for h in delta_heads:
        if h:
            out_shape.append(sd((h, s), F32))
            out_specs.append(_rows_spec(h, tm))
    out_shape += [sd((1, d), F32), sd((sum(widths), d), F32)]
    out_specs += [_full_spec((1, d)), _full_spec((sum(widths), d))]
    if with_sink:
        out_shape.append(sd((lse.shape[0], 1), F32))
        out_specs.append(_full_spec((lse.shape[0], 1)))
    return pl.pallas_call(
        body, name=name, grid=(s // tm,), out_shape=tuple(out_shape), in_specs=in_specs, out_specs=tuple(out_specs),
        compiler_params=_params(("arbitrary",)),
    )(*args)


def latent_out_backward(d_ob, o_lat, w_uv):
    s = o_lat.shape[0]
    tm = min(ROW_TILE, s)

    def body(d_ref, o_ref, uv_ref, dol_ref, dl_ref, duv_ref, prod_s):
        @pl.when(pl.program_id(0) == 0)
        def _():
            duv_ref[...] = jnp.zeros(duv_ref.shape, F32)

        for hh in range(B_HEADS):
            dh = d_ref[:, HD * hh:HD * hh + HD]
            ol = o_ref[:, B_KV_LORA * hh:B_KV_LORA * (hh + 1)]
            dol = _mm_nt(dh, uv_ref[hh])
            dol_ref[:, B_KV_LORA * hh:B_KV_LORA * (hh + 1)] = dol.astype(MXU)
            prod_s[:, B_KV_LORA * hh:B_KV_LORA * (hh + 1)] = dol * ol
            duv_ref[:, HD * hh:HD * hh + HD] += _mm_tn(ol, dh)
        dl_ref[...] = _group_sums_t(prod_s[...], B_KV_LORA)[0:B_HEADS, :]

    sd = jax.ShapeDtypeStruct
    duv_shape = (B_KV_LORA, B_HEADS * HD)
    return pl.pallas_call(
        body, name="latent_out_backward", grid=(s // tm,),
        out_shape=(sd(o_lat.shape, MXU), sd((B_HEADS, s), F32), sd(duv_shape, F32)),
        in_specs=[_row_spec(tm, d_ob.shape[1]), _row_spec(tm, o_lat.shape[1]), _full_spec(w_uv.shape)],
        out_specs=(_row_spec(tm, o_lat.shape[1]), _rows_spec(B_HEADS, tm), _full_spec(duv_shape)),
        scratch_shapes=[pltpu.VMEM((tm, o_lat.shape[1]), F32)],
        compiler_params=_params(("arbitrary",)),
    )(d_ob, o_lat, w_uv)


def even_prep_backward(dqa, dka, dva, dqb, dkb, dvb, qa_raw, ka_raw, cq_raw, ckv_raw,
                       gq, gk, qln, kvln, w_uq_t, uk_bd, bd, cos_a, sin_a, cos_t, sin_t):
    s = qa_raw.shape[0]
    tm = min(ROW_TILE, s)
    half_lat = B_KV_LORA * B_HEADS // 2
    half_w = dqb.shape[1] // 2

    def body(dqa_ref, dka_ref, dva_ref, dqb_ref, dkb_ref, dvb_ref, qa_ref, ka_ref, cq_ref, ckv_ref,
             gq_ref, gk_ref, qln_ref, kvln_ref, uqt_ref, ukbd_ref, bd_ref, ca_ref, sa_ref, ct_ref, st_ref,
             pqa, pka, pva, pcq, pckv, pkr, gqn, gkn, gqln, gkvln, guq, guk):
        @pl.when(pl.program_id(0) == 0)
        def _():
            for r in (gqn, gkn, gqln, gkvln, guq, guk):
                r[...] = jnp.zeros(r.shape, F32)

        ca, sa, ct, st = ca_ref[...], sa_ref[...], ct_ref[...], st_ref[...]
        wide = lambda t, n: jnp.concatenate([t] * n, axis=1)
        rows = lambda a: jnp.sum(a, axis=0, keepdims=True)
        dx, dg = _head_norm_bwd(_rope_t(dqa_ref[...], wide(ca, 4), wide(sa, 4), 32), qa_ref[...], gq_ref[...],
                                bd_ref, HD)
        pqa[...] = dx.astype(MXU)
        gqn[...] += rows(dg)
        dk_all = jnp.concatenate([dka_ref[g] for g in range(A_KV)], axis=1)
        dx, dg = _head_norm_bwd(_rope_t(dk_all, ca, sa, 32), ka_ref[...], gk_ref[...], bd_ref[0:128, 0:128], HD)
        pka[...] = dx.astype(MXU)
        gkn[...] += rows(dg)
        pva[...] = jnp.concatenate([dva_ref[g] for g in range(A_KV)], axis=1).astype(MXU)
        cq_raw = cq_ref[...]
        cq_n = cq_raw * _rms(cq_raw) * qln_ref[...]
        qb = _mm_nt(cq_n, uqt_ref[...])
        d_lat = jnp.concatenate([dqb_ref[:, 0:half_lat], dqb_ref[:, half_w:half_w + half_lat]], axis=1)
        d_rope = jnp.concatenate([dqb_ref[:, half_lat:half_w], dqb_ref[:, half_w + half_lat:]], axis=1)
        for hh in range(B_HEADS):
            guk[:, B_NOPE * hh:B_NOPE * (hh + 1)] += _mm_tn(d_lat[:, B_KV_LORA * hh:B_KV_LORA * (hh + 1)],
                                                            qb[:, B_NOPE * hh:B_NOPE * (hh + 1)])
        dqb_all = jnp.concatenate([_mm_nt(d_lat, ukbd_ref[...]),
                                   _rope_t(d_rope, wide(ct, 2), wide(st, 2), 32)], axis=1)
        guq[...] += _mm_tn(dqb_all, cq_n)
        dx, dg = _rms_bwd(_mm(dqb_all, uqt_ref[...]), cq_raw, qln_ref[...])
        pcq[...] = dx.astype(MXU)
        gqln[...] += rows(dg)
        dkb_sum = dkb_ref[0] + dkb_ref[1]
        dckv = dkb_sum[:, 0:B_KV_LORA] + dvb_ref[0] + dvb_ref[1]
        dx, dg = _rms_bwd(dckv, ckv_ref[...], kvln_ref[...])
        pckv[...] = dx.astype(MXU)
        gkvln[...] += rows(dg)
        pkr[...] = _rope_t(dkb_sum[:, B_KV_LORA:B_QK], ct[:, 0:B_ROPE], st[:, 0:B_ROPE], 32).astype(MXU)

    sd = jax.ShapeDtypeStruct
    consts = [gq, gk, qln, kvln, w_uq_t, uk_bd, bd]
    in_specs = [_row_spec(tm, 512), _head_spec(A_KV, tm, HD), _head_spec(A_KV, tm, HD),
                _row_spec(tm, dqb.shape[1]), _head_spec(2, tm, B_QK), _head_spec(2, tm, B_KV_LORA),
                _row_spec(tm, 512), _row_spec(tm, 128), _row_spec(tm, B_Q_LORA), _row_spec(tm, B_KV_LORA)]
    in_specs += [_full_spec(a.shape) for a in consts] + [_row_spec(tm, 128)] * 4
    small = [sd(gq.shape, F32), sd(gk.shape, F32), sd(qln.shape, F32), sd(kvln.shape, F32), sd(w_uq_t.shape, F32),
             sd((B_KV_LORA, B_HEADS * B_NOPE), F32)]
    out_shape = (sd((s, 512), MXU), sd((s, 128), MXU), sd((s, 128), MXU), sd((s, B_Q_LORA), MXU),
                 sd((s, B_KV_LORA), MXU), sd((s, B_ROPE), MXU), *small)
    out_specs = (_row_spec(tm, 512), _row_spec(tm, 128), _row_spec(tm, 128), _row_spec(tm, B_Q_LORA),
                 _row_spec(tm, B_KV_LORA), _row_spec(tm, B_ROPE), *[_full_spec(a.shape) for a in small])
    return pl.pallas_call(
        body, name="even_prep_backward", grid=(s // tm,), out_shape=out_shape, in_specs=in_specs, out_specs=out_specs,
        compiler_params=_params(("arbitrary",)),
    )(dqa, dka, dva, dqb, dkb, dvb, qa_raw, ka_raw, cq_raw, ckv_raw, *consts, cos_a, sin_a, cos_t, sin_t)


def in_proj_backward(x, mod, nw, pieces, name, *, dx_out=None, w_in_t=None, dw_rows=None, exchange=None):
    s, d = x.shape
    tm = min(ROW_TILE, s)
    grid = (s // tm,)
    n = len(pieces)
    cols = [c for _, c in pieces]
    want_dx = w_in_t is not None
    want_dw = dw_rows is not None
    n_cols = sum(c1 - c0 for c0, c1 in cols)
    hosted = exchange is not None
    nx = exchange.n if hosted else 0

    def body(*refs):
        it = iter(refs)
        x_ref, mod_ref, nw_ref = next(it), next(it), next(it)
        dxo_ref, wt_ref = (next(it), next(it)) if want_dx else (None, None)
        p_refs = [next(it) for _ in range(n)]
        xs_refs = [next(it) for _ in range(nx)]
        dx_ref, dv_ref = (next(it), next(it)) if want_dx else (None, None)
        dw_ref = next(it) if want_dw else None
        land_refs = [next(it) for _ in range(nx)]
        acc_ref = next(it) if want_dx else None
        dw_acc = next(it) if want_dw else None
        sems = list(it)
        first, last = _grid_edges(grid)
        if hosted:
            pl.when(first)(lambda: exchange.start(xs_refs, land_refs, sems))

        @pl.when(first)
        def _():
            if want_dw:
                dw_acc[...] = jnp.zeros(dw_acc.shape, F32)
            if want_dx:
                acc_ref[...] = jnp.zeros(acc_ref.shape, F32)

        xn, g1, h = _modulated(x_ref[...], mod_ref, nw_ref)
        hb = h.astype(MXU)
        dh = jnp.zeros((tm, d), F32)
        for k, (pr, (c0, c1)) in enumerate(zip(p_refs, cols)):
            if len(pr.shape) == 3:
                pc = jnp.concatenate([pr[g] for g in range(pr.shape[0])], axis=1).astype(MXU)
            else:
                pc = pr[...].astype(MXU)
            if want_dx:
                dh = dh + jnp.dot(pc, wt_ref[c0:c1, :], preferred_element_type=F32)
            if want_dw:
                r0, r1 = dw_rows[k]
                dw_acc[r0:r1, :] += _mm_tn(pc, hb)
        if want_dx:
            acc_ref[0:1, :] += jnp.sum(dh, axis=0, keepdims=True)
            acc_ref[1:2, :] += jnp.sum(dh * xn, axis=0, keepdims=True)
            dxn = dh * g1
            x = x_ref[...]
            dx_ref[...] = dxo_ref[...] + _rms(x) * (dxn - xn * jnp.mean(dxn * xn, axis=-1, keepdims=True))

        @pl.when(last)
        def _():
            if want_dx:
                dg1 = acc_ref[1:2, :]
                dv_ref[0:1, :] = acc_ref[0:1, :]
                dv_ref[1:2, :] = dg1 * nw_ref[...]
                dv_ref[2:3, :] = dg1 * (1.0 + mod_ref[1:2, :])
                dv_ref[3:4, :] = jnp.zeros((1, d), F32)
            if want_dw:
                dw_ref[...] = dw_acc[...].astype(MXU)

        if hosted:
            pl.when(last)(lambda: exchange.wait(xs_refs, land_refs, sems))

    arrs = [a for a, _ in pieces]
    sd = jax.ShapeDtypeStruct
    args = [x, mod, nw] + ([dx_out, w_in_t] if want_dx else []) + arrs + (exchange.srcs if hosted else [])
    in_specs = [_row_spec(tm, d), _full_spec(mod.shape), _full_spec(nw.shape)]
    in_specs += [_row_spec(tm, d), _full_spec(w_in_t.shape)] if want_dx else []
    in_specs += [_row_spec(tm, a.shape[1]) if a.ndim == 2 else _head_spec(a.shape[0], tm, a.shape[2]) for a in arrs]
    in_specs += exchange.in_specs if hosted else []
    out_shape, out_specs, scratch = [], [], []
    if want_dx:
        out_shape += [sd((s, d), F32), sd((4, d), F32)]
        out_specs += [_row_spec(tm, d), _full_spec((4, d))]
        scratch.append(pltpu.VMEM((8, d), F32))
    if want_dw:
        out_shape.append(sd((n_cols, d), MXU))
        out_specs.append(_full_spec((n_cols, d)))
        scratch.append(pltpu.VMEM((n_cols, d), F32))
    if hosted:
        out_shape += list(exchange.land_shapes)
        out_specs += list(exchange.out_specs)
        scratch += list(exchange.sems)
    return pl.pallas_call(
        body, name=name, grid=grid, out_shape=tuple(out_shape), in_specs=in_specs, out_specs=tuple(out_specs),
        scratch_shapes=scratch, compiler_params=_params(("arbitrary",)),
    )(*args)


def ada_weight_grad(c_all, dmod_cols):
    d = c_all.shape[1]
    w = dmod_cols.shape[2]

    def body(c_ref, dm_ref, out_ref):
        ca = _silu(c_ref[...])
        for l in range(2):
            out_ref[l] = _mm_tn(ca, dm_ref[l])

    return pl.pallas_call(
        body, name="ada_weight_grad",
        out_shape=jax.ShapeDtypeStruct((2, d, w), F32),
        compiler_params=pltpu.CompilerParams(vmem_limit_bytes=VMEM_LIMIT),
    )(c_all, dmod_cols)


def _slot_sum(g_ref):
    g = g_ref[0].astype(F32)
    for k in range(1, g_ref.shape[0]):
        g = g + g_ref[k].astype(F32)
    return g


def _adamw_math(g, w, m, v):
    m_new = ADAM_B1 * m + (1.0 - ADAM_B1) * g
    v_new = ADAM_B2 * v + (1.0 - ADAM_B2) * (g * g)
    m_hat = m_new / (1.0 - ADAM_B1 ** ADAM_STEP)
    v_hat = v_new / (1.0 - ADAM_B2 ** ADAM_STEP)
    return -ADAM_LR * (m_hat / (jnp.sqrt(v_hat) + ADAM_EPS) + ADAM_WD * w), m_new, v_new


def adamw_small(g_alls, ws, ms, vs, loss_all):
    n = len(ws)

    def body(*refs):
        g_refs, w_refs, m_refs, v_refs = (refs[i * n:(i + 1) * n] for i in range(4))
        loss_ref = refs[4 * n]
        outs = refs[4 * n + 1:]
        for i in range(n):
            g = _slot_sum(g_refs[i])
            outs[i][...] = g
            outs[n + i][...], outs[2 * n + i][...], outs[3 * n + i][...] = _adamw_math(
                g, w_refs[i][...], m_refs[i][...], v_refs[i][...])
        outs[4 * n][...] = _slot_sum(loss_ref)

    sds = [jax.ShapeDtypeStruct(w.shape, F32) for w in ws]
    res = pl.pallas_call(
        body, name="adamw_small", out_shape=tuple(sds * 4) + (jax.ShapeDtypeStruct(loss_all.shape[1:], F32),),
        compiler_params=pltpu.CompilerParams(vmem_limit_bytes=VMEM_LIMIT),
    )(*g_alls, *ws, *ms, *vs, loss_all)
    return [res[i * n:(i + 1) * n] for i in range(4)], res[4 * n]


def adamw_rows(g_slots, w, m, v, name):
    n, r, lanes = g_slots.shape
    fits = [t for t in range(16, r + 1, 16) if r % t == 0 and t * lanes <= ADAM_TILE]
    tr = max(fits) if fits else r
    def body(g_ref, w_ref, m_ref, v_ref, go, do, mo, vo):
        g = _slot_sum(g_ref)
        go[...] = g
        do[...], mo[...], vo[...] = _adamw_math(g, w_ref[...], m_ref[...], v_ref[...])

    row = pl.BlockSpec((tr, lanes), lambda i: (i, 0))
    sd = jax.ShapeDtypeStruct((r, lanes), F32)
    return pl.pallas_call(
        body, name=name, grid=(r // tr,), out_shape=(sd, sd, sd, sd),
        in_specs=[pl.BlockSpec((n, tr, lanes), lambda i: (0, i, 0)), row, row, row],
        out_specs=(row, row, row, row),
        compiler_params=_params(("parallel",)),
    )(g_slots, w, m, v)


def _rope_tables(s):
    def cs(pos, dim):
        inv = ROPE_THETA ** (-np.arange(0, dim, 2, dtype=np.float32) / dim)
        ang = pos.astype(np.float32)[:, None] * inv.astype(np.float32)[None, :]
        return np.cos(ang), np.sin(ang)

    rows = s // GRID_W
    row = np.repeat(np.arange(rows), GRID_W)
    col = np.tile(np.arange(GRID_W), rows)
    cr, sr = cs(row, HD // 2)
    cc, sc = cs(col, HD // 2)
    ct, st = cs(np.arange(s), B_ROPE)
    tables = (np.concatenate([cr, cr, cc, cc] * 2, axis=-1), np.concatenate([-sr, sr, -sc, sc] * 2, axis=-1),
              np.concatenate([ct, ct] * 4, axis=-1), np.concatenate([-st, st] * 4, axis=-1))
    return tuple(jnp.asarray(t, F32) for t in tables)


def _even_rows_to_kernel(wt):
    return jnp.concatenate([wt[:1664], wt[1696:], wt[1664:1696]], axis=0)


def _uq_rows_to_kernel(wt):
    r = wt.reshape(B_HEADS, B_NOPE + B_ROPE, -1)
    return jnp.concatenate([r[:, :B_NOPE].reshape(B_HEADS * B_NOPE, -1), r[:, B_NOPE:].reshape(B_HEADS * B_ROPE, -1)])


def _uq_rows_to_reference(wt):
    nope = wt[:B_HEADS * B_NOPE].reshape(B_HEADS, B_NOPE, -1)
    rope = wt[B_HEADS * B_NOPE:].reshape(B_HEADS, B_ROPE, -1)
    return jnp.concatenate([nope, rope], axis=1).reshape(B_HEADS * (B_NOPE + B_ROPE), -1)


def _shard_t(w):
    return jnp.transpose(w[0])


def _unshard_t(wt, like):
    return jnp.transpose(wt)[None].reshape(like.shape)


def kernel(x, c, norm_w, ada_w, ada_b, even_w_in, a_q_norm, a_k_norm, b_q_lora_norm, b_kv_lora_norm, b_w_uq, b_w_uk, b_w_uv, even_w_out, odd_w_in, c_sink, odd_w_out, final_norm, loss_target, m_norm_w, m_ada_w, m_ada_b, m_even_w_in, m_a_q_norm, m_a_k_norm, m_b_q_lora_norm, m_b_kv_lora_norm, m_b_w_uq, m_b_w_uk, m_b_w_uv, m_even_w_out, m_odd_w_in, m_c_sink, m_odd_w_out, m_final_norm, v_norm_w, v_ada_w, v_ada_b, v_even_w_in, v_a_q_norm, v_a_k_norm, v_b_q_lora_norm, v_b_kv_lora_norm, v_b_w_uq, v_b_w_uk, v_b_w_uv, v_even_w_out, v_odd_w_in, v_c_sink, v_odd_w_out, v_final_norm):
    s, d = x.shape[1], x.shape[2]
    x0 = x[0]
    target = loss_target[0]
    me_flat = 4 * lax.axis_index("x") + 2 * lax.axis_index("y") + lax.axis_index("c")

    wcols = ada_w.shape[2]
    bias_cols = lax.dynamic_slice_in_dim(ada_b.reshape(2, N_DEV, wcols), me_flat, 1, axis=1)
    call, modp, (g_in_e, g_uq) = ada_forward(
        jnp.broadcast_to(c, (8, d)), ada_w, bias_cols,
        Gather([_shard_t(even_w_in).astype(MXU), _shard_t(b_w_uq).astype(MXU)]))
    wt_in_e = _even_rows_to_kernel(g_in_e.reshape(-1, d))
    wt_uq = _uq_rows_to_kernel(g_uq.reshape(-1, B_Q_LORA))
    later_exchange = Exchange([_shard_t(odd_w_in).astype(MXU), even_w_out[0].astype(MXU),
                               odd_w_out[0].astype(MXU)], scatter=False)
    uk_bd = (jnp.eye(B_HEADS, dtype=F32)[:, None, :, None] * jnp.transpose(b_w_uk[0], (1, 2, 0))[:, :, None, :]
             ).reshape(B_HEADS * B_NOPE, B_HEADS * B_KV_LORA).astype(MXU)
    head_bd = jnp.asarray(np.kron(np.eye(A_HEADS), np.ones((HD, HD))), MXU)
    gq_full, gk_full = jnp.tile(a_q_norm, (1, A_HEADS)), jnp.tile(a_k_norm, (1, A_KV))
    w_uv = jnp.transpose(b_w_uv[0], (1, 0, 2)).astype(MXU)

    c_all = call[:, 0, :]
    mod = jnp.transpose(modp[:, :, 0, :], (1, 0, 2)).reshape(2, 3, d)
    mod_e, mod_o = mod[0], mod[1]
    nw_e, nw_o = norm_w[0:1], norm_w[1:2]

    cos_a, sin_a, cos_t, sin_t = _rope_tables(s)
    slopes = (2.0 ** (-8.0 * jnp.arange(1, C_HEADS + 1, dtype=F32) / C_HEADS)).reshape(C_HEADS, 1, 1)
    sink2 = c_sink.reshape(C_HEADS, 1, 1) * LOG2E

    (qa, ka, va, qb, kb, kat, vat, kbt, qa_raw, ka_raw, cq_raw, ckv_raw, ga, gb) = even_in_forward(
        x0, mod_e, nw_e, wt_in_e, gq_full, gk_full, b_q_lora_norm, b_kv_lora_norm, wt_uq, uk_bd, head_bd,
        cos_a, sin_a, cos_t, sin_t)
    tk_dense = min(512, s)
    tq_dense = min(256, s)
    fwd_sub = min(8, s // tk_dense)
    bwd_sub_a = min(16, s // tq_dense)
    bwd_sub_b = min(8, s // tq_dense)
    oa, lse_a, g_in_o, g_out_e, g_out_o = flash_forward(
        qa, ka, vat, dv=HD, tq=tq_dense, tk=tk_dense, nsub=fwd_sub, name="attn_a_fwd",
        exchange=later_exchange)
    wt_in_o = g_in_o.reshape(-1, d)
    w_out_e = g_out_e.reshape(-1, d)
    w_out_o = g_out_o.reshape(-1, d)
    o_lat, lse_b = flash_forward(qb, kb, kbt, dv=B_KV_LORA, tq=min(128, s), tk=tk_dense, nsub=fwd_sub,
                                 name="attn_b_fwd")
    ob = latent_out_forward(o_lat, w_uv)
    x1, y_e = mixer_out_forward(x0, mod_e, [(oa, ga), (ob, gb)], w_out_e, "even_out_fwd")

    qc, kc, vc, kct, vct, gc = odd_in_forward(x1, mod_o, nw_o, wt_in_o)
    win_sub = min(8, s // WINDOW)
    oc, lse_c = window_forward(qc, kc, vct, sink2, slopes, win_sub, "attn_c_fwd")
    x2, y_o = mixer_out_forward(x1, mod_o, [(oc, gc)], w_out_o, "odd_out_fwd")

    loss_lanes, dx2, d_final = loss_head(x2, target, final_norm.reshape(1, d))
    loss_part = (0.5 / d) * jnp.sum(loss_lanes)

    doc, dgc, delta_c, dgate_o, dw_out_o, dsink = mixer_out_backward(
        dx2, y_o, mod_o, [(oc, gc)], w_out_o, [C_HEADS], "odd_out_bwd", lse=lse_c.reshape(C_HEADS, s),
        sink=sink2.reshape(C_HEADS, 1))
    rows3 = lambda t: t.reshape(t.shape[0], 1, s)
    dqc, dkc, dvc = window_backward(qc, kc, kct, vc, doc, lse_c, rows3(delta_c), slopes, win_sub, "attn_c_bwd")
    dx1, dvec_o, dwt_in_o = in_proj_backward(
        x1, mod_o, nw_o, [(dqc, O_Q), (dkc, O_K), (dvc, O_V), (dgc, O_G)], "odd_in_bwd",
        dx_out=dx2, w_in_t=wt_in_o, dw_rows=[O_Q, O_K, O_V, O_G])

    doa, dga, dob, dgb, delta_a, dgate_e, dw_out_e = mixer_out_backward(
        dx1, y_e, mod_e, [(oa, ga), (ob, gb)], w_out_e, [A_HEADS, 0], "even_out_bwd")
    d_olat, delta_b, dw_uv = latent_out_backward(dob, o_lat, w_uv)
    blocks = lambda g: g.astype(MXU).reshape(N_DEV, g.shape[0] // N_DEV, g.shape[1])
    even_pieces = lambda: [(pqa, E_QA), (pka, E_KA), (pva, E_VA), (dga, E_GA), (pcq, E_CQ), (pckv, E_CKV),
                           (dgb, E_GB), (pkr, E_KR)]
    scatter_odd = Exchange([blocks(dwt_in_o), blocks(dw_out_o)], True)
    scatter_out_e = Exchange([blocks(dw_out_e)], True)
    dqb, dkb, dvb, l_in_o, l_out_o = flash_backward(
        qb, kb, kbt, None, d_olat, lse_b, rows3(delta_b), scale=SCALE_B, dv=B_KV_LORA,
        tq=tq_dense, tk=tk_dense, nsub=bwd_sub_b, gq=2, name="attn_b_bwd", split=B_KV_LORA, exchange=scatter_odd)
    dqa, dka, dva, l_out_e = flash_backward(
        qa, ka, kat, va, doa, lse_a, rows3(delta_a), scale=SCALE_A, dv=HD,
        tq=tq_dense, tk=tk_dense, nsub=bwd_sub_a, gq=A_KV, name="attn_a_bwd", exchange=scatter_out_e)
    (pqa, pka, pva, pcq, pckv, pkr, g_qn, g_kn, g_qln, g_kvln, dwt_uq, dw_uk) = even_prep_backward(
        dqa, dka, dva, dqb, dkb, dvb, qa_raw, ka_raw, cq_raw, ckv_raw,
        gq_full, gk_full, b_q_lora_norm, b_kv_lora_norm, wt_uq, uk_bd, head_bd, cos_a, sin_a, cos_t, sin_t)
    g_qn = jnp.sum(g_qn.reshape(A_HEADS, HD), axis=0)
    g_kn = jnp.sum(g_kn.reshape(A_KV, HD), axis=0)
    dwt_in_e, l_uk, l_uv = in_proj_backward(
        x0, mod_e, nw_e, even_pieces(), "even_in_bwd_dw",
        dw_rows=[E_QA, E_KA, E_VA, E_GA, E_CQ, E_CKV, (1696, 2208), (1664, 1696)],
        exchange=Exchange([dw_uk.astype(MXU), dw_uv.astype(MXU)], scatter=False))
    dx0, dvec_e, l_in_e, l_uq = in_proj_backward(
        x0, mod_e, nw_e, even_pieces(), "even_in_bwd_dx", dx_out=dx1, w_in_t=wt_in_e,
        exchange=Exchange([blocks(dwt_in_e), blocks(_uq_rows_to_reference(dwt_uq))], True))

    dmod = jnp.stack([jnp.concatenate([dvec_e[0], dvec_e[1], dgate_e[0]]),
                      jnp.concatenate([dvec_o[0], dvec_o[1], dgate_o[0]])])
    d_norm_w = jnp.stack([dvec_e[2], dvec_o[2]])
    small_names = ["norm_w", "ada_b", "a_q_norm", "a_k_norm", "b_q_lora_norm", "b_kv_lora_norm", "b_w_uk", "b_w_uv",
                   "c_sink", "final_norm"]
    small_w = [norm_w, ada_b, a_q_norm, a_k_norm, b_q_lora_norm, b_kv_lora_norm, b_w_uk, b_w_uv, c_sink, final_norm]
    small_m = [m_norm_w, m_ada_b, m_a_q_norm, m_a_k_norm, m_b_q_lora_norm, m_b_kv_lora_norm, m_b_w_uk, m_b_w_uv,
               m_c_sink, m_final_norm]
    small_v = [v_norm_w, v_ada_b, v_a_q_norm, v_a_k_norm, v_b_q_lora_norm, v_b_kv_lora_norm, v_b_w_uk, v_b_w_uv,
               v_c_sink, v_final_norm]
    small_g = [d_norm_w, dmod, g_qn, g_kn, g_qln, g_kvln, None, None, dsink, d_final]
    flat2 = lambda a: a.reshape((1, -1)) if a.size == a.shape[-1] else a.reshape(a.shape[-3:] if a.ndim > 3 else a.shape)
    kshape = [flat2(w).shape for w in small_w]
    late = [i for i, g in enumerate(small_g) if g is not None]
    gathered = all_gather_slots(
        Gather([small_g[i].reshape(kshape[i]) for i in late] + [jnp.full((8, 128), loss_part, F32)]),
        "gather_small_grads")
    g_all = [None] * len(small_g)
    for i, g in zip(late, gathered):
        g_all[i] = g
    g_all[6], g_all[7] = (l.reshape((N_DEV,) + kshape[6]) for l in (l_uk, l_uv))
    sm_out, loss_sum = adamw_small(g_all, [flat2(a) for a in small_w], [flat2(a) for a in small_m],
                                   [flat2(a) for a in small_v], gathered[-1])
    loss = loss_sum[0, 0]
    sm = [{nm: p.reshape(w.shape) for nm, w, p in zip(small_names, small_w, outs)} for outs in sm_out]

    dmod_all = g_all[1].reshape(N_DEV, 2, N_DEV, wcols)
    dmod_cols = lax.dynamic_slice_in_dim(dmod_all, me_flat, 1, axis=2)[:, :, 0, :]
    pad16 = lambda a: jnp.concatenate([a, jnp.zeros_like(a)], axis=0)
    g_ada_w = ada_weight_grad(pad16(c_all), jnp.transpose(pad16(dmod_cols), (1, 0, 2)))
    rows_of = lambda a: a.reshape(-1, wcols)
    ada = adamw_rows(rows_of(g_ada_w)[None], rows_of(ada_w), rows_of(m_ada_w), rows_of(v_ada_w), "adamw_ada_w")
    ada = [p.reshape(ada_w.shape) for p in ada]

    bg = [{}, {}, {}, {}]
    for nm, landed, w, m, v, transposed in (
            ("even_w_in", l_in_e, even_w_in, m_even_w_in, v_even_w_in, True),
            ("b_w_uq", l_uq, b_w_uq, m_b_w_uq, v_b_w_uq, True),
            ("odd_w_in", l_in_o, odd_w_in, m_odd_w_in, v_odd_w_in, True),
            ("even_w_out", l_out_e, even_w_out, m_even_w_out, v_even_w_out, False),
            ("odd_w_out", l_out_o, odd_w_out, m_odd_w_out, v_odd_w_out, False)):
        view = _shard_t if transposed else (lambda a: a[0])
        res = adamw_rows(landed, view(w), view(m), view(v), "adamw_" + nm)
        for kind, p in enumerate(res):
            bg[kind][nm] = _unshard_t(p, w) if transposed else p[None]
    big_names = ["even_w_in", "odd_w_in", "even_w_out", "odd_w_out", "b_w_uq"]

    order = ["norm_w", "ada_w", "ada_b", "even_w_in", "a_q_norm", "a_k_norm", "b_q_lora_norm", "b_kv_lora_norm",
             "b_w_uq", "b_w_uk", "b_w_uv", "even_w_out", "odd_w_in", "c_sink", "odd_w_out", "final_norm"]

    def pick(kind):
        out = []
        for nm in order:
            if nm == "ada_w":
                out.append(ada[kind])
            elif nm in big_names:
                out.append(bg[kind][nm])
            else:
                out.append(sm[kind][nm])
        return out

    return (loss, dx0[None], *pick(0), *pick(1), *pick(2), *pick(3))
```

```python
import functools

import jax
import jax.numpy as jnp
import numpy as np
from jax import lax
from jax.experimental import pallas as pl
from jax.experimental.pallas import tpu as pltpu

F32 = jnp.float32
MXU = jnp.bfloat16
EPS = 1e-6
ROPE_THETA = 10000.0
GRID_W = 64
HD = 64
N_DEV = 8

A_HEADS, A_KV = 8, 2
B_HEADS, B_NOPE, B_ROPE, B_Q_LORA, B_KV_LORA = 8, 64, 32, 256, 128
B_QK = B_KV_LORA + B_ROPE
C_HEADS, C_KV = 16, 4
WINDOW = 128

ADAM_LR, ADAM_B1, ADAM_B2, ADAM_EPS, ADAM_WD, ADAM_STEP = 0.001, 0.9, 0.999, 1e-08, 0.01, 10

ROW_TILE = 512
ADAM_TILE = 2048 * 128

LOG2E = 1.4426950408889634
SCALE_A = HD ** -0.5
SCALE_B = (B_NOPE + B_ROPE) ** -0.5
SCALE2_A, SCALE2_B = SCALE_A * LOG2E, SCALE_B * LOG2E
VMEM_LIMIT = 56 * 1024 * 1024

E_QA, E_KA, E_VA, E_GA, E_CQ, E_CKV, E_GB, E_KR = (
    (0, 512), (512, 640), (640, 768), (768, 1280), (1280, 1536), (1536, 1664), (1664, 2176), (2176, 2208))
O_Q, O_K, O_V, O_G = (0, 1024), (1024, 1280), (1280, 1536), (1536, 2560)


def _mm(a, b):
    return jnp.dot(a.astype(MXU), b.astype(MXU), preferred_element_type=F32)


def _mm_nt(a, b):
    return lax.dot_general(a.astype(MXU), b.astype(MXU), (((1,), (1,)), ((), ())), preferred_element_type=F32)


def _mm_tn(a, b):
    return lax.dot_general(a.astype(MXU), b.astype(MXU), (((0,), (0,)), ((), ())), preferred_element_type=F32)


def _group_sums_t(prod, group):
    tm, w = prod.shape
    sel = (lax.broadcasted_iota(jnp.int32, (w, 128), 0) // group
           == lax.broadcasted_iota(jnp.int32, (w, 128), 1)).astype(MXU)
    hi = prod.astype(MXU)
    lo = prod - hi.astype(F32)
    return (_mm(hi, sel) + _mm(lo, sel)).T


def _sigmoid(z):
    return 1.0 / (1.0 + jnp.exp(-z))


def _silu(z):
    return z * _sigmoid(z)


def _rms(x):
    return lax.rsqrt(jnp.mean(x * x, axis=-1, keepdims=True) + EPS)


def _swap_halves(y, group):
    n = y.shape[-1]
    half = group // 2
    fwd = pltpu.roll(y, half, 1)
    if n == group:
        return fwd
    back = pltpu.roll(y, n - half, 1)
    lane = lax.broadcasted_iota(jnp.int32, y.shape, 1)
    return jnp.where((lane % group) < half, back, fwd)


def _rope(y, cos, sin, group):
    return y * cos + _swap_halves(y, group) * sin


def _rope_t(d, cos, sin, group):
    return d * cos - _swap_halves(d, group) * sin


def _rms_bwd(dy, x, g):
    r = _rms(x)
    xhat = x * r
    dxhat = dy * g
    dx = r * (dxhat - xhat * jnp.mean(dxhat * xhat, axis=-1, keepdims=True))
    return dx, dy * xhat


def _group_mean(v, bd, group):
    hi = v.astype(MXU)
    lo = v - hi.astype(F32)
    return (_mm(hi, bd[...]) + _mm(lo, bd[...])) * (1.0 / group)


def _head_norm(x, g, bd, group):
    return x * lax.rsqrt(_group_mean(x * x, bd, group) + EPS) * g


def _head_norm_bwd(dy, x, g, bd, group):
    r = lax.rsqrt(_group_mean(x * x, bd, group) + EPS)
    xhat = x * r
    dxhat = dy * g
    dx = r * (dxhat - xhat * _group_mean(dxhat * xhat, bd, group))
    return dx, dy * xhat


def _params(sem, vmem=VMEM_LIMIT):
    return pltpu.CompilerParams(dimension_semantics=sem, vmem_limit_bytes=vmem)


def _row_spec(tm, w):
    return pl.BlockSpec((tm, w), lambda i: (i, 0))


def _full_spec(shape):
    nd = len(shape)
    return pl.BlockSpec(shape, lambda i: (0,) * nd)


def _head_spec(h, tm, w):
    return pl.BlockSpec((h, tm, w), lambda i: (0, i, 0))


def _headt_spec(h, w, tm):
    return pl.BlockSpec((h, w, tm), lambda i: (0, 0, i))


def _rows_spec(h, tm):
    return pl.BlockSpec((h, tm), lambda i: (0, i))


def _me():
    return lax.axis_index("x"), lax.axis_index("y"), lax.axis_index("c")


def _flat(p):
    return 4 * p[0] + 2 * p[1] + p[2]


def _peer(me, k):
    x, y, c = me
    return (1 - x if k & 4 else x, 1 - y if k & 2 else y, 1 - c if k & 1 else c)


MESH_ID = pl.DeviceIdType.MESH


class Gather:
    VMEM = pl.BlockSpec(memory_space=pltpu.VMEM)

    def __init__(self, shards):
        self.shards = list(shards)
        self.n = len(self.shards)
        self.out_shapes = tuple(jax.ShapeDtypeStruct((N_DEV,) + a.shape, a.dtype) for a in self.shards)
        self.in_specs = [Gather.VMEM] * self.n
        self.out_specs = (Gather.VMEM,) * self.n
        self.sems = [pltpu.SemaphoreType.DMA((7 * self.n,)), pltpu.SemaphoreType.DMA((7 * self.n,)),
                     pltpu.SemaphoreType.DMA((self.n,))]

    def _plan(self, x_refs, out_refs, sems):
        send_sems, recv_sems, local_sems = sems
        me = _me()
        x, y, c = me
        chips = [(1 - x, y), (x, 1 - y), (1 - x, 1 - y)]

        def copy(a, k, block, to, src=None):
            slot = out_refs[a].at[_flat(block)]
            return pltpu.make_async_remote_copy(
                src_ref=slot if src is None else src, dst_ref=slot, send_sem=send_sems.at[7 * a + k],
                recv_sem=recv_sems.at[7 * a + k], device_id=to, device_id_type=MESH_ID)

        mine = [pltpu.make_async_copy(x_refs[a], out_refs[a].at[_flat(me)], local_sems.at[a]) for a in range(self.n)]
        first = [copy(a, 0, me, (x, y, 1 - c), src=x_refs[a]) for a in range(self.n)]
        first += [copy(a, 1 + j, me, (*chip, c), src=x_refs[a]) for a in range(self.n) for j, chip in enumerate(chips)]
        return me, chips, copy, mine, first

    def start(self, x_refs, out_refs, sems):
        _, _, _, mine, first = self._plan(x_refs, out_refs, sems)
        for cp in mine + first:
            cp.start()

    def forward(self, x_refs, out_refs, sems):
        me, chips, copy, _, _ = self._plan(x_refs, out_refs, sems)
        x, y, c = me
        for a in range(self.n):
            for j, chip in enumerate(chips):
                copy(a, 1 + j, (*chip, c), me).wait_recv()
                copy(a, 4 + j, (*chip, c), (x, y, 1 - c)).start()

    def drain(self, x_refs, out_refs, sems):
        me, chips, copy, mine, first = self._plan(x_refs, out_refs, sems)
        x, y, c = me
        sibling = (x, y, 1 - c)
        for a in range(self.n):
            copy(a, 0, sibling, me).wait_recv()
            for j, chip in enumerate(chips):
                copy(a, 4 + j, (*chip, 1 - c), me).wait_recv()
        for cp in first + [copy(a, 4 + j, (*chip, c), sibling) for a in range(self.n) for j, chip in enumerate(chips)]:
            cp.wait_send()
        for cp in mine:
            cp.wait()

    def finish(self, x_refs, out_refs, sems):
        self.forward(x_refs, out_refs, sems)
        self.drain(x_refs, out_refs, sems)


def all_gather_slots(gather, name):
    def body(*refs):
        x_refs, out_refs, sems = refs[:gather.n], refs[gather.n:2 * gather.n], refs[2 * gather.n:]
        gather.start(x_refs, out_refs, sems)
        gather.finish(x_refs, out_refs, sems)

    return pl.pallas_call(
        body, name=name, out_shape=gather.out_shapes, in_specs=gather.in_specs, out_specs=gather.out_specs,
        scratch_shapes=list(gather.sems), compiler_params=pltpu.CompilerParams(vmem_limit_bytes=VMEM_LIMIT),
    )(*gather.shards)


class Exchange:
    HBM = pl.BlockSpec(memory_space=pl.ANY)

    def __init__(self, srcs, scatter):
        self.srcs = list(srcs)
        self.scatter = scatter
        self.n = len(self.srcs)
        self.land_shapes = tuple(jax.ShapeDtypeStruct((N_DEV,) + tuple(a.shape[-2:]), a.dtype) for a in self.srcs)
        self.in_specs = [Exchange.HBM] * self.n
        self.out_specs = (Exchange.HBM,) * self.n
        self.sems = [pltpu.SemaphoreType.DMA((N_DEV - 1,)), pltpu.SemaphoreType.DMA((N_DEV - 1,)),
                     pltpu.SemaphoreType.DMA] * self.n

    def _copies(self, src_refs, land_refs, sems):
        me = _me()
        mi = _flat(me)
        local, sends, recvs = [], [], []
        for a, (src_ref, land_ref) in enumerate(zip(src_refs, land_refs)):
            send_sems, recv_sems, local_sem = sems[3 * a:3 * a + 3]
            pick = (lambda p, r=src_ref: r.at[_flat(p)]) if self.scatter else (lambda p, r=src_ref: r)
            local.append(pltpu.make_async_copy(pick(me), land_ref.at[mi], local_sem))
            for k in range(1, N_DEV):
                peer = _peer(me, k)
                pair = dict(send_sem=send_sems.at[k - 1], recv_sem=recv_sems.at[k - 1], device_id=peer,
                            device_id_type=MESH_ID)
                sends.append(pltpu.make_async_remote_copy(src_ref=pick(peer), dst_ref=land_ref.at[mi], **pair))
                recvs.append(pltpu.make_async_remote_copy(src_ref=pick(peer), dst_ref=land_ref.at[_flat(peer)],
                                                          **pair))
        return local, sends, recvs

    def start(self, src_refs, land_refs, sems):
        local, sends, _ = self._copies(src_refs, land_refs, sems)
        for cp in local + sends:
            cp.start()

    def wait(self, src_refs, land_refs, sems):
        local, sends, recvs = self._copies(src_refs, land_refs, sems)
        for cp in recvs:
            cp.wait_recv()
        for cp in sends:
            cp.wait_send()
        for cp in local:
            cp.wait()


def ada_forward(c8, ada_w, bias_cols, gather):
    d = c8.shape[1]
    w = ada_w.shape[2]
    ng = gather.n

    def body(*refs):
        c_ref, w_ref, b_ref = refs[:3]
        gx_refs = refs[3:3 + ng]
        call_ref, modp_ref = refs[3 + ng:5 + ng]
        gout_refs = refs[5 + ng:5 + 2 * ng]
        part_ref, s1, r1, s2, r2 = refs[5 + 2 * ng:10 + 2 * ng]
        g_sems = refs[10 + 2 * ng:]
        me = _me()
        mi = _flat(me)
        call_ref[mi] = c_ref[...]
        rows_out = []
        for k in range(1, N_DEV):
            rows_out.append(pltpu.make_async_remote_copy(
                src_ref=c_ref, dst_ref=call_ref.at[mi], send_sem=s1.at[k - 1], recv_sem=r1.at[k - 1],
                device_id=_peer(me, k), device_id_type=MESH_ID))
        for cp in rows_out:
            cp.start()
        gather.start(gx_refs, gout_refs, g_sems)
        for k in range(1, N_DEV):
            pltpu.make_async_remote_copy(
                src_ref=c_ref, dst_ref=call_ref.at[_flat(_peer(me, k))], send_sem=s1.at[k - 1],
                recv_sem=r1.at[k - 1], device_id=_peer(me, k), device_id_type=MESH_ID).wait_recv()
        ca = _silu(call_ref[...].reshape(N_DEV * 8, d))
        for l in range(2):
            part = _mm(ca, w_ref[l]) + b_ref[l]
            for b in range(N_DEV):
                part_ref[b, l] = part[8 * b:8 * b + 8, :]
        modp_ref[mi] = part_ref[mi]
        spread = []
        for k in range(1, N_DEV):
            peer = _peer(me, k)
            spread.append(pltpu.make_async_remote_copy(
                src_ref=part_ref.at[_flat(peer)], dst_ref=modp_ref.at[mi], send_sem=s2.at[k - 1],
                recv_sem=r2.at[k - 1], device_id=peer, device_id_type=MESH_ID))
        for cp in spread:
            cp.start()
        gather.forward(gx_refs, gout_refs, g_sems)
        for k in range(1, N_DEV):
            pi = _flat(_peer(me, k))
            pltpu.make_async_remote_copy(
                src_ref=part_ref.at[pi], dst_ref=modp_ref.at[pi], send_sem=s2.at[k - 1],
                recv_sem=r2.at[k - 1], device_id=_peer(me, k), device_id_type=MESH_ID).wait_recv()
        for cp in rows_out + spread:
            cp.wait_send()
        gather.drain(gx_refs, gout_refs, g_sems)

    vm = pl.BlockSpec(memory_space=pltpu.VMEM)
    res = pl.pallas_call(
        body, name="ada_forward",
        out_shape=(jax.ShapeDtypeStruct((N_DEV, 8, d), F32), jax.ShapeDtypeStruct((N_DEV, 2, 8, w), F32))
        + gather.out_shapes,
        in_specs=[vm, vm, vm] + gather.in_specs, out_specs=(vm, vm) + gather.out_specs,
        scratch_shapes=[pltpu.VMEM((N_DEV, 2, 8, w), F32)] + [pltpu.SemaphoreType.DMA((7,))] * 4 + list(gather.sems),
        compiler_params=pltpu.CompilerParams(vmem_limit_bytes=VMEM_LIMIT),
    )(c8, ada_w, bias_cols, *gather.shards)
    return res[0], res[1], res[2:]


def _modulated(x, mod_ref, nw_ref):
    xn = x * _rms(x)
    g1 = nw_ref[...] * (1.0 + mod_ref[1:2, :])
    return xn, g1, xn * g1 + mod_ref[0:1, :]


def even_in_forward(x, mod, nw, w_in_t, gq, gk, qln, kvln, w_uq_t, uk_bd, bd, cos_a, sin_a, cos_t, sin_t):
    s, d = x.shape
    tm = min(ROW_TILE, s)
    n_nope = B_HEADS * B_NOPE

    def body(x_ref, mod_ref, nw_ref, w_ref, gq_ref, gk_ref, qln_ref, kvln_ref, uq_ref, ukbd_ref, bd_ref,
             ca_ref, sa_ref, ct_ref, st_ref,
             qa_o, ka_o, va_o, qb_o, kb_o, kat_o, vat_o, kbt_o, qa_raw_o, ka_raw_o, cq_raw_o, ckv_raw_o, ga_o, gb_o):
        _, _, h = _modulated(x_ref[...], mod_ref, nw_ref)
        h = h.astype(MXU)

        def proj(cols):
            return _mm_nt(h, w_ref[cols[0]:cols[1], :])

        ca, sa, ct, st = ca_ref[...], sa_ref[...], ct_ref[...], st_ref[...]
        wide = lambda t, n: jnp.concatenate([t] * n, axis=1)
        qa = proj(E_QA)
        qa_raw_o[...] = qa
        qr = _rope(_head_norm(qa, gq_ref[...], bd_ref, HD), wide(ca, 4), wide(sa, 4), 32) * SCALE2_A
        for hh in range(A_HEADS):
            qa_o[hh] = qr[:, HD * hh:HD * hh + HD].astype(MXU)
        ka = proj(E_KA)
        ka_raw_o[...] = ka
        kr = _rope(_head_norm(ka, gk_ref[...], bd_ref[0:128, 0:128], HD), ca, sa, 32)
        va = proj(E_VA)
        krt, vat = kr.T, va.T
        for g in range(A_KV):
            ka_o[g] = kr[:, HD * g:HD * g + HD].astype(MXU)
            va_o[g] = va[:, HD * g:HD * g + HD].astype(MXU)
            kat_o[g] = krt[HD * g:HD * g + HD, :].astype(MXU)
            vat_o[g] = vat[HD * g:HD * g + HD, :].astype(MXU)
        ga_o[...] = proj(E_GA)
        gb_o[...] = proj(E_GB)
        cq = proj(E_CQ)
        cq_raw_o[...] = cq
        qb = _mm_nt(cq * _rms(cq) * qln_ref[...], uq_ref[...])
        q_lat = _mm(qb[:, 0:n_nope], ukbd_ref[...]) * SCALE2_B
        q_rope = _rope(qb[:, n_nope:], wide(ct, 2), wide(st, 2), 32) * SCALE2_B
        for hh in range(B_HEADS):
            qb_o[hh, :, 0:B_KV_LORA] = q_lat[:, B_KV_LORA * hh:B_KV_LORA * (hh + 1)].astype(MXU)
            qb_o[hh, :, B_KV_LORA:B_QK] = q_rope[:, B_ROPE * hh:B_ROPE * (hh + 1)].astype(MXU)
        ckv = proj(E_CKV)
        ckv_raw_o[...] = ckv
        ckv_n = ckv * _rms(ckv) * kvln_ref[...]
        k_rope = _rope(proj(E_KR), ct[:, 0:B_ROPE], st[:, 0:B_ROPE], 32)
        kb_o[0, :, 0:B_KV_LORA] = ckv_n.astype(MXU)
        kb_o[0, :, B_KV_LORA:B_QK] = k_rope.astype(MXU)
        kbt_o[0, 0:B_KV_LORA, :] = ckv_n.T.astype(MXU)
        kbt_o[0, B_KV_LORA:B_QK, :] = k_rope.T.astype(MXU)

    sd = jax.ShapeDtypeStruct
    outs = (sd((A_HEADS, s, HD), MXU), sd((A_KV, s, HD), MXU), sd((A_KV, s, HD), MXU),
            sd((B_HEADS, s, B_QK), MXU), sd((1, s, B_QK), MXU),
            sd((A_KV, HD, s), MXU), sd((A_KV, HD, s), MXU), sd((1, B_QK, s), MXU),
            sd((s, 512), F32), sd((s, 128), F32), sd((s, B_Q_LORA), F32), sd((s, B_KV_LORA), F32),
            sd((s, 512), F32), sd((s, 512), F32))
    out_specs = (_head_spec(A_HEADS, tm, HD), _head_spec(A_KV, tm, HD), _head_spec(A_KV, tm, HD),
                 _head_spec(B_HEADS, tm, B_QK), _head_spec(1, tm, B_QK),
                 _headt_spec(A_KV, HD, tm), _headt_spec(A_KV, HD, tm), _headt_spec(1, B_QK, tm),
                 _row_spec(tm, 512), _row_spec(tm, 128), _row_spec(tm, B_Q_LORA), _row_spec(tm, B_KV_LORA),
                 _row_spec(tm, 512), _row_spec(tm, 512))
    consts = [mod, nw, w_in_t, gq, gk, qln, kvln, w_uq_t, uk_bd, bd]
    return pl.pallas_call(
        body, name="even_in_forward", grid=(s // tm,), out_shape=outs,
        in_specs=[_row_spec(tm, d)] + [_full_spec(a.shape) for a in consts] + [_row_spec(tm, 128)] * 4,
        out_specs=out_specs, compiler_params=_params(("parallel",)),
    )(x, *consts, cos_a, sin_a, cos_t, sin_t)


def odd_in_forward(x, mod, nw, w_in):
    s, d = x.shape
    tm = min(ROW_TILE, s)

    def body(x_ref, mod_ref, nw_ref, w_ref, q_o, k_o, v_o, kt_o, vt_o, g_o):
        _, _, h = _modulated(x_ref[...], mod_ref, nw_ref)
        h = h.astype(MXU)

        def proj(cols):
            return _mm_nt(h, w_ref[cols[0]:cols[1], :])

        q = proj(O_Q) * SCALE2_A
        for hh in range(C_HEADS):
            q_o[hh] = q[:, HD * hh:HD * hh + HD].astype(MXU)
        k = proj(O_K)
        v = proj(O_V)
        for g in range(C_KV):
            kh = k[:, HD * g:HD * g + HD]
            vh = v[:, HD * g:HD * g + HD]
            k_o[g] = kh.astype(MXU)
            v_o[g] = vh.astype(MXU)
            kt_o[g] = kh.T.astype(MXU)
            vt_o[g] = vh.T.astype(MXU)
        g_o[...] = proj(O_G)

    sd = jax.ShapeDtypeStruct
    return pl.pallas_call(
        body, name="odd_in_forward", grid=(s // tm,),
        out_shape=(sd((C_HEADS, s, HD), MXU), sd((C_KV, s, HD), MXU), sd((C_KV, s, HD), MXU),
                   sd((C_KV, HD, s), MXU), sd((C_KV, HD, s), MXU), sd((s, 1024), F32)),
        in_specs=[_row_spec(tm, d), _full_spec(mod.shape), _full_spec(nw.shape), _full_spec(w_in.shape)],
        out_specs=(_head_spec(C_HEADS, tm, HD), _head_spec(C_KV, tm, HD), _head_spec(C_KV, tm, HD),
                   _headt_spec(C_KV, HD, tm), _headt_spec(C_KV, HD, tm), _row_spec(tm, 1024)),
        compiler_params=_params(("parallel",)),
    )(x, mod, nw, w_in)


def latent_out_forward(o_lat, w_uv):
    s = o_lat.shape[0]
    tm = min(ROW_TILE, s)

    def body(o_ref, uv_ref, out_ref):
        for hh in range(B_HEADS):
            out_ref[:, HD * hh:HD * hh + HD] = _mm(o_ref[:, B_KV_LORA * hh:B_KV_LORA * (hh + 1)], uv_ref[hh])

    return pl.pallas_call(
        body, name="latent_out_forward", grid=(s // tm,),
        out_shape=jax.ShapeDtypeStruct((s, B_HEADS * HD), F32),
        in_specs=[_row_spec(tm, o_lat.shape[1]), _full_spec(w_uv.shape)],
        out_specs=_row_spec(tm, B_HEADS * HD),
        compiler_params=_params(("parallel",)),
    )(o_lat, w_uv)


def mixer_out_forward(x, mod, pairs, w_out, name):
    s, d = x.shape
    tm = min(ROW_TILE, s)
    n = len(pairs)
    widths = [o.shape[1] for o, _ in pairs]

    def body(*refs):
        x_ref, mod_ref, w_ref = refs[:3]
        pr = refs[3:3 + 2 * n]
        xo_ref, y_ref = refs[3 + 2 * n:]
        y = jnp.zeros((tm, d), F32)
        r0 = 0
        for i in range(n):
            mix = pr[2 * i][...] * _silu(pr[2 * i + 1][...])
            y = y + _mm(mix, w_ref[r0:r0 + widths[i], :])
            r0 += widths[i]
        y_ref[...] = y
        xo_ref[...] = x_ref[...] + mod_ref[2:3, :] * y

    flat = [a for p in pairs for a in p]
    sd = jax.ShapeDtypeStruct
    return pl.pallas_call(
        body, name=name, grid=(s // tm,),
        out_shape=(sd((s, d), F32), sd((s, d), F32)),
        in_specs=[_row_spec(tm, d), _full_spec(mod.shape), _full_spec(w_out.shape)]
        + [_row_spec(tm, a.shape[1]) for a in flat],
        out_specs=(_row_spec(tm, d), _row_spec(tm, d)),
        compiler_params=_params(("parallel",)),
    )(x, mod, w_out, *flat)


ONES_ROWS = 16
AHEAD = 2


def _col_max8(s3):
    m8 = jnp.max(s3, axis=0)
    return jnp.broadcast_to(jnp.max(m8, axis=0, keepdims=True), m8.shape)


def _with_ones(vt, n):
    return jnp.concatenate([vt, jnp.ones((ONES_ROWS, n), vt.dtype)], axis=0)


def _grid_edges(grid):
    ids = [pl.program_id(a) for a in range(len(grid))]
    first = functools.reduce(jnp.logical_and, [i == 0 for i in ids])
    last = functools.reduce(jnp.logical_and, [i == n - 1 for i, n in zip(ids, grid)])
    return first, last


def flash_forward(q, k, vt, *, dv, tq, tk, nsub, name, exchange=None):
    hq, s, dq = q.shape
    g_kv = k.shape[0]
    hpg = hq // g_kv
    nq = s // tq
    tkk = tk * nsub
    nk = s // tkk
    grid = (g_kv, nq, nk)
    hosted = exchange is not None
    m_cols = hpg * tq
    dvp = dv + ONES_ROWS

    def body(*refs):
        nx = exchange.n if hosted else 0
        q_ref, k_ref, vt_ref = refs[:3]
        xs_refs = refs[3:3 + nx]
        o_ref, lse_ref = refs[3 + nx:5 + nx]
        land_refs = refs[5 + nx:5 + 2 * nx]
        m_s, acc_s = refs[5 + 2 * nx:7 + 2 * nx]
        sems = refs[7 + 2 * nx:]
        if hosted:
            first, last = _grid_edges(grid)
            pl.when(first)(lambda: exchange.start(xs_refs, land_refs, sems))
        j = pl.program_id(2)

        @pl.when(j == 0)
        def _():
            m_s[...] = jnp.full((8, m_cols), -jnp.inf, F32)
            acc_s[...] = jnp.zeros((dvp, m_cols), F32)

        qq = q_ref[...].reshape(m_cols, dq)
        score = lambda u: _mm_nt(k_ref[0, tk * u:tk * (u + 1), :], qq).reshape(tk // 8, 8, m_cols)
        sts = {u: score(u) for u in range(min(AHEAD, nsub))}
        m_run = m_s[...]
        acc = acc_s[...]
        for u in range(nsub):
            if u + AHEAD < nsub:
                sts[u + AHEAD] = score(u + AHEAD)
            st = sts.pop(u)
            m_new = jnp.maximum(m_run, _col_max8(st))
            p = jnp.exp2(st - m_new[None])
            alpha = jnp.exp2(m_run - m_new)
            pv = _mm(_with_ones(vt_ref[0, 0:dv, tk * u:tk * (u + 1)], tk), p.reshape(tk, m_cols))
            acc = (acc.reshape(dvp // 8, 8, m_cols) * alpha[None]).reshape(dvp, m_cols) + pv
            m_run = m_new
        acc_s[...] = acc
        m_s[...] = m_run

        @pl.when(j == nk - 1)
        def _():
            l = acc_s[dv:dv + 1, :]
            ot = acc_s[0:dv, :] / l
            lse = m_s[0:1, :] + jnp.log2(l)
            for hh in range(hpg):
                o_ref[:, dv * hh:dv * hh + dv] = ot[:, tq * hh:tq * hh + tq].T
                lse_ref[hh] = lse[:, tq * hh:tq * hh + tq]

        if hosted:
            pl.when(last)(lambda: exchange.wait(xs_refs, land_refs, sems))

    sd = jax.ShapeDtypeStruct
    return pl.pallas_call(
        body, name=name, grid=grid,
        out_shape=(sd((s, hq * dv), F32), sd((hq, 1, s), F32)) + (exchange.land_shapes if hosted else ()),
        in_specs=[pl.BlockSpec((hpg, tq, dq), lambda g, i, j: (g, i, 0)),
                  pl.BlockSpec((1, tkk, k.shape[2]), lambda g, i, j: (g, j, 0)),
                  pl.BlockSpec((1, dv, tkk), lambda g, i, j: (g, 0, j))] + (exchange.in_specs if hosted else []),
        out_specs=(pl.BlockSpec((tq, hpg * dv), lambda g, i, j: (i, g)),
                   pl.BlockSpec((hpg, 1, tq), lambda g, i, j: (g, 0, i))) + (exchange.out_specs if hosted else ()),
        scratch_shapes=[pltpu.VMEM((8, m_cols), F32), pltpu.VMEM((dvp, m_cols), F32)]
        + (list(exchange.sems) if hosted else []),
        compiler_params=_params(("arbitrary",) * 3 if hosted else ("parallel", "parallel", "arbitrary")),
    )(q, k, vt, *(exchange.srcs if hosted else []))


def _window_bias_t(hpg, slope_ref):
    t = WINDOW
    r = lax.broadcasted_iota(jnp.int32, (3 * t, t), 0)
    cq = lax.broadcasted_iota(jnp.int32, (3 * t, t), 1)
    arel = jnp.abs(r - t - cq)
    base = jnp.where(arel <= WINDOW, arel.astype(F32) * (-LOG2E), -jnp.inf)
    return jnp.concatenate([base * slope_ref[hh] for hh in range(hpg)], axis=1)


def _window_edges_t(bias, no_before, no_after):
    t = WINDOW
    r = lax.broadcasted_iota(jnp.int32, bias.shape, 0)
    out = ((r < t) & no_before) | ((r >= 2 * t) & no_after)
    return jnp.where(out, -jnp.inf, bias)


def _window_specs(kind, nb, nblk, d):
    t = WINDOW
    before = lambda i: jnp.clip(i * nb - 1, 0, nblk - 1)
    after = lambda i: jnp.clip((i + 1) * nb, 0, nblk - 1)
    if kind == "rows":
        return [pl.BlockSpec((1, t, d), lambda g, i: (g, before(i), 0)),
                pl.BlockSpec((1, nb * t, d), lambda g, i: (g, i, 0)),
                pl.BlockSpec((1, t, d), lambda g, i: (g, after(i), 0))]
    return [pl.BlockSpec((1, d, t), lambda g, i: (g, 0, before(i))),
            pl.BlockSpec((1, d, nb * t), lambda g, i: (g, 0, i)),
            pl.BlockSpec((1, d, t), lambda g, i: (g, 0, after(i)))]


def window_forward(q, k, vt, sink2, slopes, nb, name):
    hq, s, d = q.shape
    g_kv = k.shape[0]
    hpg = hq // g_kv
    t = WINDOW
    nblk = s // t
    steps = nblk // nb
    m_cols = hpg * t

    def body(q_ref, kp, ko, kn, vp, vo, vn, sink_ref, slope_ref, o_ref, lse_ref):
        i = pl.program_id(1)
        kk_all = jnp.concatenate([kp[0], ko[0], kn[0]], axis=0)
        vt_all = jnp.concatenate([vp[0], vo[0], vn[0]], axis=1)
        bias = _window_bias_t(hpg, slope_ref)
        sink_row = jnp.concatenate([jnp.broadcast_to(sink_ref[hh], (8, t)) for hh in range(hpg)], axis=1)
        sts = {}

        def score(u):
            qq = q_ref[:, t * u:t * (u + 1), :].reshape(m_cols, d)
            b_u = bias
            if u == 0 or u == nb - 1:
                b_u = _window_edges_t(bias, (i == 0) if u == 0 else False,
                                      (i == steps - 1) if u == nb - 1 else False)
            sts[u] = _mm_nt(kk_all[t * u:t * (u + 3), :], qq) + b_u

        for u in range(min(AHEAD, nb)):
            score(u)
        for u in range(nb):
            if u + AHEAD < nb:
                score(u + AHEAD)
            s3 = sts.pop(u).reshape(3 * t // 8, 8, m_cols)
            m8 = jnp.maximum(_col_max8(s3), sink_row)
            p = jnp.exp2(s3 - m8[None]).reshape(3 * t, m_cols)
            acc = _mm(_with_ones(vt_all[:, t * u:t * (u + 3)], 3 * t), p)
            l = acc[d:d + 1, :] + jnp.exp2(sink_row[0:1, :] - m8[0:1, :])
            ot = acc[0:d, :] / l
            lse = m8[0:1, :] + jnp.log2(l)
            for hh in range(hpg):
                o_ref[t * u:t * (u + 1), d * hh:d * hh + d] = ot[:, t * hh:t * hh + t].T
                lse_ref[hh, :, t * u:t * (u + 1)] = lse[:, t * hh:t * hh + t]

    sd = jax.ShapeDtypeStruct
    return pl.pallas_call(
        body, name=name, grid=(g_kv, steps),
        out_shape=(sd((s, hq * d), F32), sd((hq, 1, s), F32)),
        in_specs=[pl.BlockSpec((hpg, nb * t, d), lambda g, i: (g, i, 0))]
        + _window_specs("rows", nb, nblk, d) + _window_specs("cols", nb, nblk, d)
        + [pl.BlockSpec((hpg, 1, 1), lambda g, i: (g, 0, 0))] * 2,
        out_specs=(pl.BlockSpec((nb * t, hpg * d), lambda g, i: (i, g)),
                   pl.BlockSpec((hpg, 1, nb * t), lambda g, i: (g, 0, i))),
        compiler_params=_params(("parallel", "parallel")),
    )(q, k, k, k, vt, vt, vt, sink2, slopes)


def window_backward(q, k, kt, v, do, lse, delta, slopes, nb, name):
    hq, s, d = q.shape
    g_kv = k.shape[0]
    hpg = hq // g_kv
    t = WINDOW
    nblk = s // t
    steps = nblk // nb
    m_cols = hpg * t

    def body(q_ref, kp, ko, kn, ktp, kto, ktn, vp, vo, vn, do_ref, lse_ref, dl_ref, slope_ref,
             dq_ref, dk_ref, dv_ref, dk_s, dv_s):
        i = pl.program_id(1)

        @pl.when(i == 0)
        def _():
            dk_ref[...] = jnp.zeros(dk_ref.shape, F32)
            dv_ref[...] = jnp.zeros(dv_ref.shape, F32)

        dk_s[...] = jnp.zeros(dk_s.shape, F32)
        dv_s[...] = jnp.zeros(dv_s.shape, F32)
        kk_all = jnp.concatenate([kp[0], ko[0], kn[0]], axis=0)
        vv_all = jnp.concatenate([vp[0], vo[0], vn[0]], axis=0)
        kkt_all = jnp.concatenate([ktp[0], kto[0], ktn[0]], axis=1)
        bias = _window_bias_t(hpg, slope_ref)
        qqs, dds, sts, dps = {}, {}, {}, {}

        def issue(u):
            rows = slice(t * u, t * (u + 1))
            keys = slice(t * u, t * (u + 3))
            qqs[u] = q_ref[:, rows, :].reshape(m_cols, d)
            dds[u] = jnp.concatenate([do_ref[rows, d * hh:d * hh + d] for hh in range(hpg)], axis=0)
            b_u = bias
            if u == 0 or u == nb - 1:
                b_u = _window_edges_t(bias, (i == 0) if u == 0 else False,
                                      (i == steps - 1) if u == nb - 1 else False)
            sts[u] = _mm_nt(kk_all[keys, :], qqs[u]) + b_u
            dps[u] = _mm_nt(vv_all[keys, :], dds[u])

        for u in range(min(AHEAD, nb)):
            issue(u)
        for u in range(nb):
            if u + AHEAD < nb:
                issue(u + AHEAD)
            rows = slice(t * u, t * (u + 1))
            keys = slice(t * u, t * (u + 3))
            lse_row = jnp.concatenate([lse_ref[hh, :, rows] for hh in range(hpg)], axis=1)
            dl_row = jnp.concatenate([dl_ref[hh, :, rows] for hh in range(hpg)], axis=1)
            p = jnp.exp2(sts[u] - lse_row)
            ds = p * (dps[u] - dl_row) * SCALE_A
            dv_s[keys, :] += _mm(p, dds[u])
            dk_s[keys, :] += _mm(ds, qqs[u])
            dqt = _mm(kkt_all[:, keys], ds)
            for hh in range(hpg):
                dq_ref[rows, d * hh:d * hh + d] = dqt[:, t * hh:t * hh + t].T
        tq = nb * t
        for src, r0, n in ((0, jnp.clip(i * nb - 1, 0, nblk - 1) * t, t), (t, i * tq, tq),
                           (t + tq, jnp.clip((i + 1) * nb, 0, nblk - 1) * t, t)):
            dst = pl.ds(pl.multiple_of(r0, t), n)
            dk_ref[0, dst, :] += dk_s[src:src + n, :] * (1.0 / SCALE2_A)
            dv_ref[0, dst, :] += dv_s[src:src + n, :]

    row_map = lambda g, i: (g, 0, i)
    sd = jax.ShapeDtypeStruct
    return pl.pallas_call(
        body, name=name, grid=(g_kv, steps),
        out_shape=(sd((s, hq * d), F32), sd((g_kv, s, d), F32), sd((g_kv, s, d), F32)),
        in_specs=[pl.BlockSpec((hpg, nb * t, d), lambda g, i: (g, i, 0))]
        + _window_specs("rows", nb, nblk, d) + _window_specs("cols", nb, nblk, d) + _window_specs("rows", nb, nblk, d)
        + [pl.BlockSpec((nb * t, hpg * d), lambda g, i: (i, g)), pl.BlockSpec((hpg, 1, nb * t), row_map),
           pl.BlockSpec((hpg, 1, nb * t), row_map), pl.BlockSpec((hpg, 1, 1), lambda g, i: (g, 0, 0))],
        out_specs=(pl.BlockSpec((nb * t, hpg * d), lambda g, i: (i, g)),
                   pl.BlockSpec((1, s, d), lambda g, i: (g, 0, 0)),
                   pl.BlockSpec((1, s, d), lambda g, i: (g, 0, 0))),
        scratch_shapes=[pltpu.VMEM(((nb + 2) * t, d), F32), pltpu.VMEM(((nb + 2) * t, d), F32)],
        compiler_params=_params(("parallel", "arbitrary")),
    )(q, k, k, k, kt, kt, kt, v, v, v, do, lse, delta, slopes)


def flash_backward(q, k, kt, v, do, lse, delta, *, scale, dv, tq, tk, nsub, gq, name, split=None, exchange=None):
    hq, s, dq = q.shape
    g_kv = k.shape[0]
    hpg = hq // gq
    nq = s // tq
    tqq = tq * nsub
    nqs = s // tqq
    nkb = s // tk
    grid = (gq, nkb, nqs)
    hosted = exchange is not None
    m_cols = hpg * tq
    c = scale * LOG2E
    has_v = v is not None

    def body(*refs):
        it = iter(refs)
        q_ref, k_ref, kt_ref = next(it), next(it), next(it)
        v_ref = next(it) if has_v else None
        do_ref, lse_ref, dl_ref = next(it), next(it), next(it)
        nx = exchange.n if hosted else 0
        xs_refs = [next(it) for _ in range(nx)]
        dq_ref, dk_ref, dv_ref = next(it), next(it), next(it)
        land_refs = [next(it) for _ in range(nx)]
        dqt_s = next(it)
        sems = list(it)
        kj = pl.program_id(1)
        qi = pl.program_id(2)
        if hosted:
            first, last = _grid_edges(grid)
            pl.when(first)(lambda: exchange.start(xs_refs, land_refs, sems))

        @pl.when((kj == 0) & (qi == 0))
        def _():
            dqt_s[...] = jnp.zeros(dqt_s.shape, F32)

        @pl.when(qi == 0)
        def _():
            dk_ref[...] = jnp.zeros(dk_ref.shape, F32)
            dv_ref[...] = jnp.zeros(dv_ref.shape, F32)

        kk = k_ref[0]
        vv = v_ref[0] if has_v else kk[:, :dv]
        qqs, dds, sts, dps = {}, {}, {}, {}

        def issue(u):
            rows = slice(tq * u, tq * (u + 1))
            qqs[u] = q_ref[:, rows, :].reshape(m_cols, dq)
            dds[u] = jnp.concatenate([do_ref[rows, dv * hh:dv * hh + dv] for hh in range(hpg)], axis=0)
            sts[u] = _mm_nt(kk, qqs[u])
            dps[u] = _mm_nt(vv, dds[u])

        for u in range(min(AHEAD, nsub)):
            issue(u)
        dv_acc = dv_ref[0]
        dk_acc = dk_ref[0]
        for u in range(nsub):
            if u + AHEAD < nsub:
                issue(u + AHEAD)
            rows = slice(tq * u, tq * (u + 1))
            lse_row = jnp.concatenate([lse_ref[hh, :, rows] for hh in range(hpg)], axis=1)
            dl_row = jnp.concatenate([dl_ref[hh, :, rows] for hh in range(hpg)], axis=1)
            p = jnp.exp2(sts[u] - lse_row)
            ds = p * (dps[u] - dl_row) * scale
            dv_acc = dv_acc + _mm(p, dds[u])
            dk_acc = dk_acc + _mm(ds, qqs[u])
            dqt = _mm(kt_ref[0], ds)
            for hh in range(hpg):
                dqt_s[qi * nsub + u, dq * hh:dq * hh + dq, :] += dqt[:, tq * hh:tq * hh + tq]
        dv_ref[0] = dv_acc
        dk_ref[0] = jnp.where(qi == nqs - 1, dk_acc * (1.0 / c), dk_acc)

        @pl.when((kj == nkb - 1) & (qi == nqs - 1))
        def _():
            def emit(t, carry):
                r0 = pl.multiple_of(t * tq, tq)
                for hh in range(hpg):
                    blk = dqt_s[t, dq * hh:dq * hh + dq, :].T
                    if split is None:
                        dq_ref[pl.ds(r0, tq), dq * hh:dq * hh + dq] = blk
                    else:
                        rest = dq - split
                        dq_ref[pl.ds(r0, tq), split * hh:split * (hh + 1)] = blk[:, 0:split]
                        dq_ref[pl.ds(r0, tq), hpg * split + rest * hh:hpg * split + rest * (hh + 1)] = blk[:, split:]
                return carry

            lax.fori_loop(0, nq, emit, 0)

        if hosted:
            pl.when(last)(lambda: exchange.wait(xs_refs, land_refs, sems))

    kv_of = lambda g: g * g_kv // gq
    in_specs = [pl.BlockSpec((hpg, tqq, dq), lambda g, kj, qi: (g, qi, 0)),
                pl.BlockSpec((1, tk, dq), lambda g, kj, qi: (kv_of(g), kj, 0)),
                pl.BlockSpec((1, dq, tk), lambda g, kj, qi: (kv_of(g), 0, kj))]
    args = [q, k, kt]
    if has_v:
        in_specs.append(pl.BlockSpec((1, tk, dv), lambda g, kj, qi: (kv_of(g), kj, 0)))
        args.append(v)
    row_map = lambda g, kj, qi: (g, 0, qi)
    in_specs += [pl.BlockSpec((tqq, hpg * dv), lambda g, kj, qi: (qi, g)),
                 pl.BlockSpec((hpg, 1, tqq), row_map), pl.BlockSpec((hpg, 1, tqq), row_map)]
    args += [do, lse, delta]
    if hosted:
        in_specs += exchange.in_specs
        args += exchange.srcs
    sd = jax.ShapeDtypeStruct
    return pl.pallas_call(
        body, name=name, grid=grid,
        out_shape=(sd((s, hq * dq), F32), sd((gq, s, dq), F32), sd((gq, s, dv), F32))
        + (exchange.land_shapes if hosted else ()),
        in_specs=in_specs,
        out_specs=(pl.BlockSpec((s, hpg * dq), lambda g, kj, qi: (0, g)),
                   pl.BlockSpec((1, tk, dq), lambda g, kj, qi: (g, kj, 0)),
                   pl.BlockSpec((1, tk, dv), lambda g, kj, qi: (g, kj, 0))) + (exchange.out_specs if hosted else ()),
        scratch_shapes=[pltpu.VMEM((nq, hpg * dq, tq), F32)] + (list(exchange.sems) if hosted else []),
        compiler_params=_params(("arbitrary",) * 3 if hosted else ("parallel", "arbitrary", "arbitrary")),
    )(*args)


def loss_head(x, target, fnw):
    s, d = x.shape
    tm = min(ROW_TILE, s)

    def body(x_ref, t_ref, w_ref, lp_ref, dx_ref, dw_ref):
        @pl.when(pl.program_id(0) == 0)
        def _():
            lp_ref[...] = jnp.zeros(lp_ref.shape, F32)
            dw_ref[...] = jnp.zeros(dw_ref.shape, F32)

        x = x_ref[...]
        g = w_ref[...]
        err = x * _rms(x) * g - t_ref[...]
        lp_ref[...] += jnp.sum(err * err, axis=0, keepdims=True)
        dx, dg = _rms_bwd(err * (1.0 / d), x, g)
        dx_ref[...] = dx
        dw_ref[...] += jnp.sum(dg, axis=0, keepdims=True)

    sd = jax.ShapeDtypeStruct
    return pl.pallas_call(
        body, name="loss_head", grid=(s // tm,),
        out_shape=(sd((1, d), F32), sd((s, d), F32), sd((1, d), F32)),
        in_specs=[_row_spec(tm, d), _row_spec(tm, d), _full_spec(fnw.shape)],
        out_specs=(_full_spec((1, d)), _row_spec(tm, d), _full_spec((1, d))),
        compiler_params=_params(("arbitrary",)),
    )(x, target, fnw)


def mixer_out_backward(dx, y, mod, pairs, w_out, delta_heads, name, lse=None, sink=None):
    s, d = dx.shape
    tm = min(ROW_TILE, s)
    n = len(pairs)
    widths = [o.shape[1] for o, _ in pairs]
    n_delta = sum(1 for h in delta_heads if h)
    with_sink = lse is not None

    def body(*refs):
        it = iter(refs)
        dx_ref, y_ref, mod_ref, wt_ref = next(it), next(it), next(it), next(it)
        pr = [next(it) for _ in range(2 * n)]
        lse_ref = next(it) if with_sink else None
        sink_ref = next(it) if with_sink else None
        outs = [next(it) for _ in range(2 * n)]
        dl_refs = [next(it) for _ in range(n_delta)]
        dgate_ref, dw_ref = next(it), next(it)
        dsink_ref = next(it) if with_sink else None

        @pl.when(pl.program_id(0) == 0)
        def _():
            dgate_ref[...] = jnp.zeros(dgate_ref.shape, F32)
            dw_ref[...] = jnp.zeros(dw_ref.shape, F32)
            if with_sink:
                dsink_ref[...] = jnp.zeros(dsink_ref.shape, F32)

        dxo = dx_ref[...]
        dgate_ref[...] += jnp.sum(dxo * y_ref[...], axis=0, keepdims=True)
        dy = (dxo * mod_ref[2:3, :]).astype(MXU)
        dmix = _mm_nt(dy, wt_ref[...])
        r0 = 0
        di = 0
        for i in range(n):
            o = pr[2 * i][...]
            g = pr[2 * i + 1][...]
            dm = dmix[:, r0:r0 + widths[i]]
            sg = _sigmoid(g)
            act = g * sg
            do = dm * act
            outs[2 * i][...] = do.astype(MXU)
            outs[2 * i + 1][...] = (dm * o * (sg * (1.0 + g * (1.0 - sg)))).astype(MXU)
            dw_ref[r0:r0 + widths[i], :] += _mm_tn(o * act, dy)
            if delta_heads[i]:
                dlt = _group_sums_t(do * o, HD)[0:delta_heads[i], :]
                dl_refs[di][...] = dlt
                if with_sink:
                    ps = jnp.exp2(sink_ref[...] - lse_ref[...])
                    dsink_ref[...] += -jnp.sum(ps * dlt, axis=1, keepdims=True)
                di += 1
            r0 += widths[i]

    flat = [a for p in pairs for a in p]
    sd = jax.ShapeDtypeStruct
    in_specs = [_row_spec(tm, d), _row_spec(tm, d), _full_spec(mod.shape), _full_spec(w_out.shape)]
    in_specs += [_row_spec(tm, a.shape[1]) for a in flat]
    args = [dx, y, mod, w_out] + flat
    if with_sink:
        nh = lse.shape[0]
        in_specs += [_rows_spec(nh, tm), _full_spec(sink.shape)]
        args += [lse, sink]
    out_shape = [sd((s, a.shape[1]), MXU) for a in flat]
    out_specs = [_row_spec(tm, a.shape[1]) for a in flat]
    for h in delta_heads:
        if h:
            out_shape.append(sd((h, s), F32))
            out_specs.append(_rows_spec(h, tm))
    out_shape += [sd((1, d), F32), sd((sum(widths), d), F32)]
    out_specs += [_full_spec((1, d)), _full_spec((sum(widths), d))]
    if with_sink:
        out_shape.append(sd((lse.shape[0], 1), F32))
        out_specs.append(_full_spec((lse.shape[0], 1)))
    return pl.pallas_call(
        body, name=name, grid=(s // tm,), out_shape=tuple(out_shape), in_specs=in_specs, out_specs=tuple(out_specs),
        compiler_params=_params(("arbitrary",)),
    )(*args)


def latent_out_backward(d_ob, o_lat, w_uv):
    s = o_lat.shape[0]
    tm = min(ROW_TILE, s)

    def body(d_ref, o_ref, uv_ref, dol_ref, dl_ref, duv_ref, prod_s):
        @pl.when(pl.program_id(0) == 0)
        def _():
            duv_ref[...] = jnp.zeros(duv_ref.shape, F32)

        for hh in range(B_HEADS):
            dh = d_ref[:, HD * hh:HD * hh + HD]
            ol = o_ref[:, B_KV_LORA * hh:B_KV_LORA * (hh + 1)]
            dol = _mm_nt(dh, uv_ref[hh])
            dol_ref[:, B_KV_LORA * hh:B_KV_LORA * (hh + 1)] = dol.astype(MXU)
            prod_s[:, B_KV_LORA * hh:B_KV_LORA * (hh + 1)] = dol * ol
            duv_ref[:, HD * hh:HD * hh + HD] += _mm_tn(ol, dh)
        dl_ref[...] = _group_sums_t(prod_s[...], B_KV_LORA)[0:B_HEADS, :]

    sd = jax.ShapeDtypeStruct
    duv_shape = (B_KV_LORA, B_HEADS * HD)
    return pl.pallas_call(
        body, name="latent_out_backward", grid=(s // tm,),
        out_shape=(sd(o_lat.shape, MXU), sd((B_HEADS, s), F32), sd(duv_shape, F32)),
        in_specs=[_row_spec(tm, d_ob.shape[1]), _row_spec(tm, o_lat.shape[1]), _full_spec(w_uv.shape)],
        out_specs=(_row_spec(tm, o_lat.shape[1]), _rows_spec(B_HEADS, tm), _full_spec(duv_shape)),
        scratch_shapes=[pltpu.VMEM((tm, o_lat.shape[1]), F32)],
        compiler_params=_params(("arbitrary",)),
    )(d_ob, o_lat, w_uv)


def even_prep_backward(dqa, dka, dva, dqb, dkb, dvb, qa_raw, ka_raw, cq_raw, ckv_raw,
                       gq, gk, qln, kvln, w_uq_t, uk_bd, bd, cos_a, sin_a, cos_t, sin_t):
    s = qa_raw.shape[0]
    tm = min(ROW_TILE, s)
    half_lat = B_KV_LORA * B_HEADS // 2
    half_w = dqb.shape[1] // 2

    def body(dqa_ref, dka_ref, dva_ref, dqb_ref, dkb_ref, dvb_ref, qa_ref, ka_ref, cq_ref, ckv_ref,
             gq_ref, gk_ref, qln_ref, kvln_ref, uqt_ref, ukbd_ref, bd_ref, ca_ref, sa_ref, ct_ref, st_ref,
             pqa, pka, pva, pcq, pckv, pkr, gqn, gkn, gqln, gkvln, guq, guk):
        @pl.when(pl.program_id(0) == 0)
        def _():
            for r in (gqn, gkn, gqln, gkvln, guq, guk):
                r[...] = jnp.zeros(r.shape, F32)

        ca, sa, ct, st = ca_ref[...], sa_ref[...], ct_ref[...], st_ref[...]
        wide = lambda t, n: jnp.concatenate([t] * n, axis=1)
        rows = lambda a: jnp.sum(a, axis=0, keepdims=True)
        dx, dg = _head_norm_bwd(_rope_t(dqa_ref[...], wide(ca, 4), wide(sa, 4), 32), qa_ref[...], gq_ref[...],
                                bd_ref, HD)
        pqa[...] = dx.astype(MXU)
        gqn[...] += rows(dg)
        dk_all = jnp.concatenate([dka_ref[g] for g in range(A_KV)], axis=1)
        dx, dg = _head_norm_bwd(_rope_t(dk_all, ca, sa, 32), ka_ref[...], gk_ref[...], bd_ref[0:128, 0:128], HD)
        pka[...] = dx.astype(MXU)
        gkn[...] += rows(dg)
        pva[...] = jnp.concatenate([dva_ref[g] for g in range(A_KV)], axis=1).astype(MXU)
        cq_raw = cq_ref[...]
        cq_n = cq_raw * _rms(cq_raw) * qln_ref[...]
        qb = _mm_nt(cq_n, uqt_ref[...])
        d_lat = jnp.concatenate([dqb_ref[:, 0:half_lat], dqb_ref[:, half_w:half_w + half_lat]], axis=1)
        d_rope = jnp.concatenate([dqb_ref[:, half_lat:half_w], dqb_ref[:, half_w + half_lat:]], axis=1)
        for hh in range(B_HEADS):
            guk[:, B_NOPE * hh:B_NOPE * (hh + 1)] += _mm_tn(d_lat[:, B_KV_LORA * hh:B_KV_LORA * (hh + 1)],
                                                            qb[:, B_NOPE * hh:B_NOPE * (hh + 1)])
        dqb_all = jnp.concatenate([_mm_nt(d_lat, ukbd_ref[...]),
                                   _rope_t(d_rope, wide(ct, 2), wide(st, 2), 32)], axis=1)
        guq[...] += _mm_tn(dqb_all, cq_n)
        dx, dg = _rms_bwd(_mm(dqb_all, uqt_ref[...]), cq_raw, qln_ref[...])
        pcq[...] = dx.astype(MXU)
        gqln[...] += rows(dg)
        dkb_sum = dkb_ref[0] + dkb_ref[1]
        dckv = dkb_sum[:, 0:B_KV_LORA] + dvb_ref[0] + dvb_ref[1]
        dx, dg = _rms_bwd(dckv, ckv_ref[...], kvln_ref[...])
        pckv[...] = dx.astype(MXU)
        gkvln[...] += rows(dg)
        pkr[...] = _rope_t(dkb_sum[:, B_KV_LORA:B_QK], ct[:, 0:B_ROPE], st[:, 0:B_ROPE], 32).astype(MXU)

    sd = jax.ShapeDtypeStruct
    consts = [gq, gk, qln, kvln, w_uq_t, uk_bd, bd]
    in_specs = [_row_spec(tm, 512), _head_spec(A_KV, tm, HD), _head_spec(A_KV, tm, HD),
                _row_spec(tm, dqb.shape[1]), _head_spec(2, tm, B_QK), _head_spec(2, tm, B_KV_LORA),
                _row_spec(tm, 512), _row_spec(tm, 128), _row_spec(tm, B_Q_LORA), _row_spec(tm, B_KV_LORA)]
    in_specs += [_full_spec(a.shape) for a in consts] + [_row_spec(tm, 128)] * 4
    small = [sd(gq.shape, F32), sd(gk.shape, F32), sd(qln.shape, F32), sd(kvln.shape, F32), sd(w_uq_t.shape, F32),
             sd((B_KV_LORA, B_HEADS * B_NOPE), F32)]
    out_shape = (sd((s, 512), MXU), sd((s, 128), MXU), sd((s, 128), MXU), sd((s, B_Q_LORA), MXU),
                 sd((s, B_KV_LORA), MXU), sd((s, B_ROPE), MXU), *small)
    out_specs = (_row_spec(tm, 512), _row_spec(tm, 128), _row_spec(tm, 128), _row_spec(tm, B_Q_LORA),
                 _row_spec(tm, B_KV_LORA), _row_spec(tm, B_ROPE), *[_full_spec(a.shape) for a in small])
    return pl.pallas_call(
        body, name="even_prep_backward", grid=(s // tm,), out_shape=out_shape, in_specs=in_specs, out_specs=out_specs,
        compiler_params=_params(("arbitrary",)),
    )(dqa, dka, dva, dqb, dkb, dvb, qa_raw, ka_raw, cq_raw, ckv_raw, *consts, cos_a, sin_a, cos_t, sin_t)


def in_proj_backward(x, mod, nw, pieces, name, *, dx_out=None, w_in_t=None, dw_rows=None, exchange=None):
    s, d = x.shape
    tm = min(ROW_TILE, s)
    grid = (s // tm,)
    n = len(pieces)
    cols = [c for _, c in pieces]
    want_dx = w_in_t is not None
    want_dw = dw_rows is not None
    n_cols = sum(c1 - c0 for c0, c1 in cols)
    hosted = exchange is not None
    nx = exchange.n if hosted else 0

    def body(*refs):
        it = iter(refs)
        x_ref, mod_ref, nw_ref = next(it), next(it), next(it)
        dxo_ref, wt_ref = (next(it), next(it)) if want_dx else (None, None)
        p_refs = [next(it) for _ in range(n)]
        xs_refs = [next(it) for _ in range(nx)]
        dx_ref, dv_ref = (next(it), next(it)) if want_dx else (None, None)
        dw_ref = next(it) if want_dw else None
        land_refs = [next(it) for _ in range(nx)]
        acc_ref = next(it) if want_dx else None
        dw_acc = next(it) if want_dw else None
        sems = list(it)
        first, last = _grid_edges(grid)
        if hosted:
            pl.when(first)(lambda: exchange.start(xs_refs, land_refs, sems))

        @pl.when(first)
        def _():
            if want_dw:
                dw_acc[...] = jnp.zeros(dw_acc.shape, F32)
            if want_dx:
                acc_ref[...] = jnp.zeros(acc_ref.shape, F32)

        xn, g1, h = _modulated(x_ref[...], mod_ref, nw_ref)
        hb = h.astype(MXU)
        dh = jnp.zeros((tm, d), F32)
        for k, (pr, (c0, c1)) in enumerate(zip(p_refs, cols)):
            if len(pr.shape) == 3:
                pc = jnp.concatenate([pr[g] for g in range(pr.shape[0])], axis=1).astype(MXU)
            else:
                pc = pr[...].astype(MXU)
            if want_dx:
                dh = dh + jnp.dot(pc, wt_ref[c0:c1, :], preferred_element_type=F32)
            if want_dw:
                r0, r1 = dw_rows[k]
                dw_acc[r0:r1, :] += _mm_tn(pc, hb)
        if want_dx:
            acc_ref[0:1, :] += jnp.sum(dh, axis=0, keepdims=True)
            acc_ref[1:2, :] += jnp.sum(dh * xn, axis=0, keepdims=True)
            dxn = dh * g1
            x = x_ref[...]
            dx_ref[...] = dxo_ref[...] + _rms(x) * (dxn - xn * jnp.mean(dxn * xn, axis=-1, keepdims=True))

        @pl.when(last)
        def _():
            if want_dx:
                dg1 = acc_ref[1:2, :]
                dv_ref[0:1, :] = acc_ref[0:1, :]
                dv_ref[1:2, :] = dg1 * nw_ref[...]
                dv_ref[2:3, :] = dg1 * (1.0 + mod_ref[1:2, :])
                dv_ref[3:4, :] = jnp.zeros((1, d), F32)
            if want_dw:
                dw_ref[...] = dw_acc[...].astype(MXU)

        if hosted:
            pl.when(last)(lambda: exchange.wait(xs_refs, land_refs, sems))

    arrs = [a for a, _ in pieces]
    sd = jax.ShapeDtypeStruct
    args = [x, mod, nw] + ([dx_out, w_in_t] if want_dx else []) + arrs + (exchange.srcs if hosted else [])
    in_specs = [_row_spec(tm, d), _full_spec(mod.shape), _full_spec(nw.shape)]
    in_specs += [_row_spec(tm, d), _full_spec(w_in_t.shape)] if want_dx else []
    in_specs += [_row_spec(tm, a.shape[1]) if a.ndim == 2 else _head_spec(a.shape[0], tm, a.shape[2]) for a in arrs]
    in_specs += exchange.in_specs if hosted else []
    out_shape, out_specs, scratch = [], [], []
    if want_dx:
        out_shape += [sd((s, d), F32), sd((4, d), F32)]
        out_specs += [_row_spec(tm, d), _full_spec((4, d))]
        scratch.append(pltpu.VMEM((8, d), F32))
    if want_dw:
        out_shape.append(sd((n_cols, d), MXU))
        out_specs.append(_full_spec((n_cols, d)))
        scratch.append(pltpu.VMEM((n_cols, d), F32))
    if hosted:
        out_shape += list(exchange.land_shapes)
        out_specs += list(exchange.out_specs)
        scratch += list(exchange.sems)
    return pl.pallas_call(
        body, name=name, grid=grid, out_shape=tuple(out_shape), in_specs=in_specs, out_specs=tuple(out_specs),
        scratch_shapes=scratch, compiler_params=_params(("arbitrary",)),
    )(*args)


def ada_weight_grad(c_all, dmod_cols):
    d = c_all.shape[1]
    w = dmod_cols.shape[2]

    def body(c_ref, dm_ref, out_ref):
        ca = _silu(c_ref[...])
        for l in range(2):
            out_ref[l] = _mm_tn(ca, dm_ref[l])

    return pl.pallas_call(
        body, name="ada_weight_grad",
        out_shape=jax.ShapeDtypeStruct((2, d, w), F32),
        compiler_params=pltpu.CompilerParams(vmem_limit_bytes=VMEM_LIMIT),
    )(c_all, dmod_cols)


def _slot_sum(g_ref):
    g = g_ref[0].astype(F32)
    for k in range(1, g_ref.shape[0]):
        g = g + g_ref[k].astype(F32)
    return g


def _adamw_math(g, w, m, v):
    m_new = ADAM_B1 * m + (1.0 - ADAM_B1) * g
    v_new = ADAM_B2 * v + (1.0 - ADAM_B2) * (g * g)
    m_hat = m_new / (1.0 - ADAM_B1 ** ADAM_STEP)
    v_hat = v_new / (1.0 - ADAM_B2 ** ADAM_STEP)
    return -ADAM_LR * (m_hat / (jnp.sqrt(v_hat) + ADAM_EPS) + ADAM_WD * w), m_new, v_new


def adamw_small(g_alls, ws, ms, vs, loss_all):
    n = len(ws)

    def body(*refs):
        g_refs, w_refs, m_refs, v_refs = (refs[i * n:(i + 1) * n] for i in range(4))
        loss_ref = refs[4 * n]
        outs = refs[4 * n + 1:]
        for i in range(n):
            g = _slot_sum(g_refs[i])
            outs[i][...] = g
            outs[n + i][...], outs[2 * n + i][...], outs[3 * n + i][...] = _adamw_math(
                g, w_refs[i][...], m_refs[i][...], v_refs[i][...])
        outs[4 * n][...] = _slot_sum(loss_ref)

    sds = [jax.ShapeDtypeStruct(w.shape, F32) for w in ws]
    res = pl.pallas_call(
        body, name="adamw_small", out_shape=tuple(sds * 4) + (jax.ShapeDtypeStruct(loss_all.shape[1:], F32),),
        compiler_params=pltpu.CompilerParams(vmem_limit_bytes=VMEM_LIMIT),
    )(*g_alls, *ws, *ms, *vs, loss_all)
    return [res[i * n:(i + 1) * n] for i in range(4)], res[4 * n]


def adamw_rows(g_slots, w, m, v, name):
    n, r, lanes = g_slots.shape
    fits = [t for t in range(16, r + 1, 16) if r % t == 0 and t * lanes <= ADAM_TILE]
    tr = max(fits) if fits else r
    def body(g_ref, w_ref, m_ref, v_ref, go, do, mo, vo):
        g = _slot_sum(g_ref)
        go[...] = g
        do[...], mo[...], vo[...] = _adamw_math(g, w_ref[...], m_ref[...], v_ref[...])

    row = pl.BlockSpec((tr, lanes), lambda i: (i, 0))
    sd = jax.ShapeDtypeStruct((r, lanes), F32)
    return pl.pallas_call(
        body, name=name, grid=(r // tr,), out_shape=(sd, sd, sd, sd),
        in_specs=[pl.BlockSpec((n, tr, lanes), lambda i: (0, i, 0)), row, row, row],
        out_specs=(row, row, row, row),
        compiler_params=_params(("parallel",)),
    )(g_slots, w, m, v)


def _rope_tables(s):
    def cs(pos, dim):
        inv = ROPE_THETA ** (-np.arange(0, dim, 2, dtype=np.float32) / dim)
        ang = pos.astype(np.float32)[:, None] * inv.astype(np.float32)[None, :]
        return np.cos(ang), np.sin(ang)

    rows = s // GRID_W
    row = np.repeat(np.arange(rows), GRID_W)
    col = np.tile(np.arange(GRID_W), rows)
    cr, sr = cs(row, HD // 2)
    cc, sc = cs(col, HD // 2)
    ct, st = cs(np.arange(s), B_ROPE)
    tables = (np.concatenate([cr, cr, cc, cc] * 2, axis=-1), np.concatenate([-sr, sr, -sc, sc] * 2, axis=-1),
              np.concatenate([ct, ct] * 4, axis=-1), np.concatenate([-st, st] * 4, axis=-1))
    return tuple(jnp.asarray(t, F32) for t in tables)


def _even_rows_to_kernel(wt):
    return jnp.concatenate([wt[:1664], wt[1696:], wt[1664:1696]], axis=0)


def _uq_rows_to_kernel(wt):
    r = wt.reshape(B_HEADS, B_NOPE + B_ROPE, -1)
    return jnp.concatenate([r[:, :B_NOPE].reshape(B_HEADS * B_NOPE, -1), r[:, B_NOPE:].reshape(B_HEADS * B_ROPE, -1)])


def _uq_rows_to_reference(wt):
    nope = wt[:B_HEADS * B_NOPE].reshape(B_HEADS, B_NOPE, -1)
    rope = wt[B_HEADS * B_NOPE:].reshape(B_HEADS, B_ROPE, -1)
    return jnp.concatenate([nope, rope], axis=1).reshape(B_HEADS * (B_NOPE + B_ROPE), -1)


def _shard_t(w):
    return jnp.transpose(w[0])


def _unshard_t(wt, like):
    return jnp.transpose(wt)[None].reshape(like.shape)


def kernel(x, c, norm_w, ada_w, ada_b, even_w_in, a_q_norm, a_k_norm, b_q_lora_norm, b_kv_lora_norm, b_w_uq, b_w_uk, b_w_uv, even_w_out, odd_w_in, c_sink, odd_w_out, final_norm, loss_target, m_norm_w, m_ada_w, m_ada_b, m_even_w_in, m_a_q_norm, m_a_k_norm, m_b_q_lora_norm, m_b_kv_lora_norm, m_b_w_uq, m_b_w_uk, m_b_w_uv, m_even_w_out, m_odd_w_in, m_c_sink, m_odd_w_out, m_final_norm, v_norm_w, v_ada_w, v_ada_b, v_even_w_in, v_a_q_norm, v_a_k_norm, v_b_q_lora_norm, v_b_kv_lora_norm, v_b_w_uq, v_b_w_uk, v_b_w_uv, v_even_w_out, v_odd_w_in, v_c_sink, v_odd_w_out, v_final_norm):
    s, d = x.shape[1], x.shape[2]
    x0 = x[0]
    target = loss_target[0]
    me_flat = 4 * lax.axis_index("x") + 2 * lax.axis_index("y") + lax.axis_index("c")

    wcols = ada_w.shape[2]
    bias_cols = lax.dynamic_slice_in_dim(ada_b.reshape(2, N_DEV, wcols), me_flat, 1, axis=1)
    call, modp, (g_in_e, g_uq) = ada_forward(
        jnp.broadcast_to(c, (8, d)), ada_w, bias_cols,
        Gather([_shard_t(even_w_in).astype(MXU), _shard_t(b_w_uq).astype(MXU)]))
    wt_in_e = _even_rows_to_kernel(g_in_e.reshape(-1, d))
    wt_uq = _uq_rows_to_kernel(g_uq.reshape(-1, B_Q_LORA))
    later_exchange = Exchange([_shard_t(odd_w_in).astype(MXU), even_w_out[0].astype(MXU),
                               odd_w_out[0].astype(MXU)], scatter=False)
    uk_bd = (jnp.eye(B_HEADS, dtype=F32)[:, None, :, None] * jnp.transpose(b_w_uk[0], (1, 2, 0))[:, :, None, :]
             ).reshape(B_HEADS * B_NOPE, B_HEADS * B_KV_LORA).astype(MXU)
    head_bd = jnp.asarray(np.kron(np.eye(A_HEADS), np.ones((HD, HD))), MXU)
    gq_full, gk_full = jnp.tile(a_q_norm, (1, A_HEADS)), jnp.tile(a_k_norm, (1, A_KV))
    w_uv = jnp.transpose(b_w_uv[0], (1, 0, 2)).astype(MXU)

    c_all = call[:, 0, :]
    mod = jnp.transpose(modp[:, :, 0, :], (1, 0, 2)).reshape(2, 3, d)
    mod_e, mod_o = mod[0], mod[1]
    nw_e, nw_o = norm_w[0:1], norm_w[1:2]

    cos_a, sin_a, cos_t, sin_t = _rope_tables(s)
    slopes = (2.0 ** (-8.0 * jnp.arange(1, C_HEADS + 1, dtype=F32) / C_HEADS)).reshape(C_HEADS, 1, 1)
    sink2 = c_sink.reshape(C_HEADS, 1, 1) * LOG2E

    (qa, ka, va, qb, kb, kat, vat, kbt, qa_raw, ka_raw, cq_raw, ckv_raw, ga, gb) = even_in_forward(
        x0, mod_e, nw_e, wt_in_e, gq_full, gk_full, b_q_lora_norm, b_kv_lora_norm, wt_uq, uk_bd, head_bd,
        cos_a, sin_a, cos_t, sin_t)
    tk_dense = min(512, s)
    tq_dense = min(256, s)
    fwd_sub = min(8, s // tk_dense)
    bwd_sub_a = min(16, s // tq_dense)
    bwd_sub_b = min(8, s // tq_dense)
    oa, lse_a, g_in_o, g_out_e, g_out_o = flash_forward(
        qa, ka, vat, dv=HD, tq=tq_dense, tk=tk_dense, nsub=fwd_sub, name="attn_a_fwd",
        exchange=later_exchange)
    wt_in_o = g_in_o.reshape(-1, d)
    w_out_e = g_out_e.reshape(-1, d)
    w_out_o = g_out_o.reshape(-1, d)
    o_lat, lse_b = flash_forward(qb, kb, kbt, dv=B_KV_LORA, tq=min(128, s), tk=tk_dense, nsub=fwd_sub,
                                 name="attn_b_fwd")
    ob = latent_out_forward(o_lat, w_uv)
    x1, y_e = mixer_out_forward(x0, mod_e, [(oa, ga), (ob, gb)], w_out_e, "even_out_fwd")

    qc, kc, vc, kct, vct, gc = odd_in_forward(x1, mod_o, nw_o, wt_in_o)
    win_sub = min(8, s // WINDOW)
    oc, lse_c = window_forward(qc, kc, vct, sink2, slopes, win_sub, "attn_c_fwd")
    x2, y_o = mixer_out_forward(x1, mod_o, [(oc, gc)], w_out_o, "odd_out_fwd")

    loss_lanes, dx2, d_final = loss_head(x2, target, final_norm.reshape(1, d))
    loss_part = (0.5 / d) * jnp.sum(loss_lanes)

    doc, dgc, delta_c, dgate_o, dw_out_o, dsink = mixer_out_backward(
        dx2, y_o, mod_o, [(oc, gc)], w_out_o, [C_HEADS], "odd_out_bwd", lse=lse_c.reshape(C_HEADS, s),
        sink=sink2.reshape(C_HEADS, 1))
    rows3 = lambda t: t.reshape(t.shape[0], 1, s)
    dqc, dkc, dvc = window_backward(qc, kc, kct, vc, doc, lse_c, rows3(delta_c), slopes, win_sub, "attn_c_bwd")
    dx1, dvec_o, dwt_in_o = in_proj_backward(
        x1, mod_o, nw_o, [(dqc, O_Q), (dkc, O_K), (dvc, O_V), (dgc, O_G)], "odd_in_bwd",
        dx_out=dx2, w_in_t=wt_in_o, dw_rows=[O_Q, O_K, O_V, O_G])

    doa, dga, dob, dgb, delta_a, dgate_e, dw_out_e = mixer_out_backward(
        dx1, y_e, mod_e, [(oa, ga), (ob, gb)], w_out_e, [A_HEADS, 0], "even_out_bwd")
    d_olat, delta_b, dw_uv = latent_out_backward(dob, o_lat, w_uv)
    blocks = lambda g: g.astype(MXU).reshape(N_DEV, g.shape[0] // N_DEV, g.shape[1])
    even_pieces = lambda: [(pqa, E_QA), (pka, E_KA), (pva, E_VA), (dga, E_GA), (pcq, E_CQ), (pckv, E_CKV),
                           (dgb, E_GB), (pkr, E_KR)]
    scatter_odd = Exchange([blocks(dwt_in_o), blocks(dw_out_o)], True)
    scatter_out_e = Exchange([blocks(dw_out_e)], True)
    dqb, dkb, dvb, l_in_o, l_out_o = flash_backward(
        qb, kb, kbt, None, d_olat, lse_b, rows3(delta_b), scale=SCALE_B, dv=B_KV_LORA,
        tq=tq_dense, tk=tk_dense, nsub=bwd_sub_b, gq=2, name="attn_b_bwd", split=B_KV_LORA, exchange=scatter_odd)
    dqa, dka, dva, l_out_e = flash_backward(
        qa, ka, kat, va, doa, lse_a, rows3(delta_a), scale=SCALE_A, dv=HD,
        tq=tq_dense, tk=tk_dense, nsub=bwd_sub_a, gq=A_KV, name="attn_a_bwd", exchange=scatter_out_e)
    (pqa, pka, pva, pcq, pckv, pkr, g_qn, g_kn, g_qln, g_kvln, dwt_uq, dw_uk) = even_prep_backward(
        dqa, dka, dva, dqb, dkb, dvb, qa_raw, ka_raw, cq_raw, ckv_raw,
        gq_full, gk_full, b_q_lora_norm, b_kv_lora_norm, wt_uq, uk_bd, head_bd, cos_a, sin_a, cos_t, sin_t)
    g_qn = jnp.sum(g_qn.reshape(A_HEADS, HD), axis=0)
    g_kn = jnp.sum(g_kn.reshape(A_KV, HD), axis=0)
    dwt_in_e, l_uk, l_uv = in_proj_backward(
        x0, mod_e, nw_e, even_pieces(), "even_in_bwd_dw",
        dw_rows=[E_QA, E_KA, E_VA, E_GA, E_CQ, E_CKV, (1696, 2208), (1664, 1696)],
        exchange=Exchange([dw_uk.astype(MXU), dw_uv.astype(MXU)], scatter=False))
    dx0, dvec_e, l_in_e, l_uq = in_proj_backward(
        x0, mod_e, nw_e, even_pieces(), "even_in_bwd_dx", dx_out=dx1, w_in_t=wt_in_e,
        exchange=Exchange([blocks(dwt_in_e), blocks(_uq_rows_to_reference(dwt_uq))], True))

    dmod = jnp.stack([jnp.concatenate([dvec_e[0], dvec_e[1], dgate_e[0]]),
                      jnp.concatenate([dvec_o[0], dvec_o[1], dgate_o[0]])])
    d_norm_w = jnp.stack([dvec_e[2], dvec_o[2]])
    small_names = ["norm_w", "ada_b", "a_q_norm", "a_k_norm", "b_q_lora_norm", "b_kv_lora_norm", "b_w_uk", "b_w_uv",
                   "c_sink", "final_norm"]
    small_w = [norm_w, ada_b, a_q_norm, a_k_norm, b_q_lora_norm, b_kv_lora_norm, b_w_uk, b_w_uv, c_sink, final_norm]
    small_m = [m_norm_w, m_ada_b, m_a_q_norm, m_a_k_norm, m_b_q_lora_norm, m_b_kv_lora_norm, m_b_w_uk, m_b_w_uv,
               m_c_sink, m_final_norm]
    small_v = [v_norm_w, v_ada_b, v_a_q_norm, v_a_k_norm, v_b_q_lora_norm, v_b_kv_lora_norm, v_b_w_uk, v_b_w_uv,
               v_c_sink, v_final_norm]
    small_g = [d_norm_w, dmod, g_qn, g_kn, g_qln, g_kvln, None, None, dsink, d_final]
    flat2 = lambda a: a.reshape((1, -1)) if a.size == a.shape[-1] else a.reshape(a.shape[-3:] if a.ndim > 3 else a.shape)
    kshape = [flat2(w).shape for w in small_w]
    late = [i for i, g in enumerate(small_g) if g is not None]
    gathered = all_gather_slots(
        Gather([small_g[i].reshape(kshape[i]) for i in late] + [jnp.full((8, 128), loss_part, F32)]),
        "gather_small_grads")
    g_all = [None] * len(small_g)
    for i, g in zip(late, gathered):
        g_all[i] = g
    g_all[6], g_all[7] = (l.reshape((N_DEV,) + kshape[6]) for l in (l_uk, l_uv))
    sm_out, loss_sum = adamw_small(g_all, [flat2(a) for a in small_w], [flat2(a) for a in small_m],
                                   [flat2(a) for a in small_v], gathered[-1])
    loss = loss_sum[0, 0]
    sm = [{nm: p.reshape(w.shape) for nm, w, p in zip(small_names, small_w, outs)} for outs in sm_out]

    dmod_all = g_all[1].reshape(N_DEV, 2, N_DEV, wcols)
    dmod_cols = lax.dynamic_slice_in_dim(dmod_all, me_flat, 1, axis=2)[:, :, 0, :]
    pad16 = lambda a: jnp.concatenate([a, jnp.zeros_like(a)], axis=0)
    g_ada_w = ada_weight_grad(pad16(c_all), jnp.transpose(pad16(dmod_cols), (1, 0, 2)))
    rows_of = lambda a: a.reshape(-1, wcols)
    ada = adamw_rows(rows_of(g_ada_w)[None], rows_of(ada_w), rows_of(m_ada_w), rows_of(v_ada_w), "adamw_ada_w")
    ada = [p.reshape(ada_w.shape) for p in ada]

    bg = [{}, {}, {}, {}]
    for nm, landed, w, m, v, transposed in (
            ("even_w_in", l_in_e, even_w_in, m_even_w_in, v_even_w_in, True),
            ("b_w_uq", l_uq, b_w_uq, m_b_w_uq, v_b_w_uq, True),
            ("odd_w_in", l_in_o, odd_w_in, m_odd_w_in, v_odd_w_in, True),
            ("even_w_out", l_out_e, even_w_out, m_even_w_out, v_even_w_out, False),
            ("odd_w_out", l_out_o, odd_w_out, m_odd_w_out, v_odd_w_out, False)):
        view = _shard_t if transposed else (lambda a: a[0])
        res = adamw_rows(landed, view(w), view(m), view(v), "adamw_" + nm)
        for kind, p in enumerate(res):
            bg[kind][nm] = _unshard_t(p, w) if transposed else p[None]
    big_names = ["even_w_in", "odd_w_in", "even_w_out", "odd_w_out", "b_w_uq"]

    order = ["norm_w", "ada_w", "ada_b", "even_w_in", "a_q_norm", "a_k_norm", "b_q_lora_norm", "b_kv_lora_norm",
             "b_w_uq", "b_w_uk", "b_w_uv", "even_w_out", "odd_w_in", "c_sink", "odd_w_out", "final_norm"]

    def pick(kind):
        out = []
        for nm in order:
            if nm == "ada_w":
                out.append(ada[kind])
            elif nm in big_names:
                out.append(bg[kind][nm])
            else:
                out.append(sm[kind][nm])
        return out

    return (loss, dx0[None], *pick(0), *pick(1), *pick(2), *pick(3))
```

```python
import functools

import jax
import jax.numpy as jnp
import numpy as np
from jax import lax
from jax.experimental import pallas as pl
from jax.experimental.pallas import tpu as pltpu

F32 = jnp.float32
MXU = jnp.bfloat16
EPS = 1e-6
ROPE_THETA = 10000.0
GRID_W = 64
HD = 64
N_DEV = 8

A_HEADS, A_KV = 8, 2
B_HEADS, B_NOPE, B_ROPE, B_Q_LORA, B_KV_LORA = 8, 64, 32, 256, 128
B_QK = B_KV_LORA + B_ROPE
C_HEADS, C_KV = 16, 4
WINDOW = 128

ADAM_LR, ADAM_B1, ADAM_B2, ADAM_EPS, ADAM_WD, ADAM_STEP = 0.001, 0.9, 0.999, 1e-08, 0.01, 10

ROW_TILE = 512
ADAM_TILE = 2048 * 128

LOG2E = 1.4426950408889634
SCALE_A = HD ** -0.5
SCALE_B = (B_NOPE + B_ROPE) ** -0.5
SCALE2_A, SCALE2_B = SCALE_A * LOG2E, SCALE_B * LOG2E
VMEM_LIMIT = 56 * 1024 * 1024

E_QA, E_KA, E_VA, E_GA, E_CQ, E_CKV, E_GB, E_KR = (
    (0, 512), (512, 640), (640, 768), (768, 1280), (1280, 1536), (1536, 1664), (1664, 2176), (2176, 2208))
O_Q, O_K, O_V, O_G = (0, 1024), (1024, 1280), (1280, 1536), (1536, 2560)


def _mm(a, b):
    return jnp.dot(a.astype(MXU), b.astype(MXU), preferred_element_type=F32)


def _mm_nt(a, b):
    return lax.dot_general(a.astype(MXU), b.astype(MXU), (((1,), (1,)), ((), ())), preferred_element_type=F32)


def _mm_tn(a, b):
    return lax.dot_general(a.astype(MXU), b.astype(MXU), (((0,), (0,)), ((), ())), preferred_element_type=F32)


def _group_sums_t(prod, group):
    tm, w = prod.shape
    sel = (lax.broadcasted_iota(jnp.int32, (w, 128), 0) // group
           == lax.broadcasted_iota(jnp.int32, (w, 128), 1)).astype(MXU)
    hi = prod.astype(MXU)
    lo = prod - hi.astype(F32)
    return (_mm(hi, sel) + _mm(lo, sel)).T


def _sigmoid(z):
    return 1.0 / (1.0 + jnp.exp(-z))


def _silu(z):
    return z * _sigmoid(z)


def _rms(x):
    return lax.rsqrt(jnp.mean(x * x, axis=-1, keepdims=True) + EPS)


def _swap_halves(y, group):
    n = y.shape[-1]
    half = group // 2
    fwd = pltpu.roll(y, half, 1)
    if n == group:
        return fwd
    back = pltpu.roll(y, n - half, 1)
    lane = lax.broadcasted_iota(jnp.int32, y.shape, 1)
    return jnp.where((lane % group) < half, back, fwd)


def _rope(y, cos, sin, group):
    return y * cos + _swap_halves(y, group) * sin


def _rope_t(d, cos, sin, group):
    return d * cos - _swap_halves(d, group) * sin


def _rms_bwd(dy, x, g):
    r = _rms(x)
    xhat = x * r
    dxhat = dy * g
    dx = r * (dxhat - xhat * jnp.mean(dxhat * xhat, axis=-1, keepdims=True))
    return dx, dy * xhat


def _group_mean(v, bd, group):
    hi = v.astype(MXU)
    lo = v - hi.astype(F32)
    return (_mm(hi, bd[...]) + _mm(lo, bd[...])) * (1.0 / group)


def _head_norm(x, g, bd, group):
    return x * lax.rsqrt(_group_mean(x * x, bd, group) + EPS) * g


def _head_norm_bwd(dy, x, g, bd, group):
    r = lax.rsqrt(_group_mean(x * x, bd, group) + EPS)
    xhat = x * r
    dxhat = dy * g
    dx = r * (dxhat - xhat * _group_mean(dxhat * xhat, bd, group))
    return dx, dy * xhat


def _params(sem, vmem=VMEM_LIMIT):
    return pltpu.CompilerParams(dimension_semantics=sem, vmem_limit_bytes=vmem)


def _row_spec(tm, w):
    return pl.BlockSpec((tm, w), lambda i: (i, 0))


def _full_spec(shape):
    nd = len(shape)
    return pl.BlockSpec(shape, lambda i: (0,) * nd)


def _head_spec(h, tm, w):
    return pl.BlockSpec((h, tm, w), lambda i: (0, i, 0))


def _headt_spec(h, w, tm):
    return pl.BlockSpec((h, w, tm), lambda i: (0, 0, i))


def _rows_spec(h, tm):
    return pl.BlockSpec((h, tm), lambda i: (0, i))


def _me():
    return lax.axis_index("x"), lax.axis_index("y"), lax.axis_index("c")


def _flat(p):
    return 4 * p[0] + 2 * p[1] + p[2]


def _peer(me, k):
    x, y, c = me
    return (1 - x if k & 4 else x, 1 - y if k & 2 else y, 1 - c if k & 1 else c)


MESH_ID = pl.DeviceIdType.MESH


class Gather:
    VMEM = pl.BlockSpec(memory_space=pltpu.VMEM)

    def __init__(self, shards):
        self.shards = list(shards)
        self.n = len(self.shards)
        self.out_shapes = tuple(jax.ShapeDtypeStruct((N_DEV,) + a.shape, a.dtype) for a in self.shards)
        self.in_specs = [Gather.VMEM] * self.n
        self.out_specs = (Gather.VMEM,) * self.n
        self.sems = [pltpu.SemaphoreType.DMA((7 * self.n,)), pltpu.SemaphoreType.DMA((7 * self.n,)),
                     pltpu.SemaphoreType.DMA((self.n,))]

    def _plan(self, x_refs, out_refs, sems):
        send_sems, recv_sems, local_sems = sems
        me = _me()
        x, y, c = me
        chips = [(1 - x, y), (x, 1 - y), (1 - x, 1 - y)]

        def copy(a, k, block, to, src=None):
            slot = out_refs[a].at[_flat(block)]
            return pltpu.make_async_remote_copy(
                src_ref=slot if src is None else src, dst_ref=slot, send_sem=send_sems.at[7 * a + k],
                recv_sem=recv_sems.at[7 * a + k], device_id=to, device_id_type=MESH_ID)

        mine = [pltpu.make_async_copy(x_refs[a], out_refs[a].at[_flat(me)], local_sems.at[a]) for a in range(self.n)]
        first = [copy(a, 0, me, (x, y, 1 - c), src=x_refs[a]) for a in range(self.n)]
        first += [copy(a, 1 + j, me, (*chip, c), src=x_refs[a]) for a in range(self.n) for j, chip in enumerate(chips)]
        return me, chips, copy, mine, first

    def start(self, x_refs, out_refs, sems):
        _, _, _, mine, first = self._plan(x_refs, out_refs, sems)
        for cp in mine + first:
            cp.start()

    def forward(self, x_refs, out_refs, sems):
        me, chips, copy, _, _ = self._plan(x_refs, out_refs, sems)
        x, y, c = me
        for a in range(self.n):
            for j, chip in enumerate(chips):
                copy(a, 1 + j, (*chip, c), me).wait_recv()
                copy(a, 4 + j, (*chip, c), (x, y, 1 - c)).start()

    def drain(self, x_refs, out_refs, sems):
        me, chips, copy, mine, first = self._plan(x_refs, out_refs, sems)
        x, y, c = me
        sibling = (x, y, 1 - c)
        for a in range(self.n):
            copy(a, 0, sibling, me).wait_recv()
            for j, chip in enumerate(chips):
                copy(a, 4 + j, (*chip, 1 - c), me).wait_recv()
        for cp in first + [copy(a, 4 + j, (*chip, c), sibling) for a in range(self.n) for j, chip in enumerate(chips)]:
            cp.wait_send()
        for cp in mine:
            cp.wait()

    def finish(self, x_refs, out_refs, sems):
        self.forward(x_refs, out_refs, sems)
        self.drain(x_refs, out_refs, sems)


def all_gather_slots(gather, name):
    def body(*refs):
        x_refs, out_refs, sems = refs[:gather.n], refs[gather.n:2 * gather.n], refs[2 * gather.n:]
        gather.start(x_refs, out_refs, sems)
        gather.finish(x_refs, out_refs, sems)

    return pl.pallas_call(
        body, name=name, out_shape=gather.out_shapes, in_specs=gather.in_specs, out_specs=gather.out_specs,
        scratch_shapes=list(gather.sems), compiler_params=pltpu.CompilerParams(vmem_limit_bytes=VMEM_LIMIT),
    )(*gather.shards)


class Exchange:
    HBM = pl.BlockSpec(memory_space=pl.ANY)

    def __init__(self, srcs, scatter):
        self.srcs = list(srcs)
        self.scatter = scatter
        self.n = len(self.srcs)
        self.land_shapes = tuple(jax.ShapeDtypeStruct((N_DEV,) + tuple(a.shape[-2:]), a.dtype) for a in self.srcs)
        self.in_specs = [Exchange.HBM] * self.n
        self.out_specs = (Exchange.HBM,) * self.n
        self.sems = [pltpu.SemaphoreType.DMA((N_DEV - 1,)), pltpu.SemaphoreType.DMA((N_DEV - 1,)),
                     pltpu.SemaphoreType.DMA] * self.n

    def _copies(self, src_refs, land_refs, sems):
        me = _me()
        mi = _flat(me)
        local, sends, recvs = [], [], []
        for a, (src_ref, land_ref) in enumerate(zip(src_refs, land_refs)):
            send_sems, recv_sems, local_sem = sems[3 * a:3 * a + 3]
            pick = (lambda p, r=src_ref: r.at[_flat(p)]) if self.scatter else (lambda p, r=src_ref: r)
            local.append(pltpu.make_async_copy(pick(me), land_ref.at[mi], local_sem))
            for k in range(1, N_DEV):
                peer = _peer(me, k)
                pair = dict(send_sem=send_sems.at[k - 1], recv_sem=recv_sems.at[k - 1], device_id=peer,
                            device_id_type=MESH_ID)
                sends.append(pltpu.make_async_remote_copy(src_ref=pick(peer), dst_ref=land_ref.at[mi], **pair))
                recvs.append(pltpu.make_async_remote_copy(src_ref=pick(peer), dst_ref=land_ref.at[_flat(peer)],
                                                          **pair))
        return local, sends, recvs

    def start(self, src_refs, land_refs, sems):
        local, sends, _ = self._copies(src_refs, land_refs, sems)
        for cp in local + sends:
            cp.start()

    def wait(self, src_refs, land_refs, sems):
        local, sends, recvs = self._copies(src_refs, land_refs, sems)
        for cp in recvs:
            cp.wait_recv()
        for cp in sends:
            cp.wait_send()
        for cp in local:
            cp.wait()


def ada_forward(c8, ada_w, bias_cols, gather):
    d = c8.shape[1]
    w = ada_w.shape[2]
    ng = gather.n

    def body(*refs):
        c_ref, w_ref, b_ref = refs[:3]
        gx_refs = refs[3:3 + ng]
        call_ref, modp_ref = refs[3 + ng:5 + ng]
        gout_refs = refs[5 + ng:5 + 2 * ng]
        part_ref, s1, r1, s2, r2 = refs[5 + 2 * ng:10 + 2 * ng]
        g_sems = refs[10 + 2 * ng:]
        me = _me()
        mi = _flat(me)
        call_ref[mi] = c_ref[...]
        rows_out = []
        for k in range(1, N_DEV):
            rows_out.append(pltpu.make_async_remote_copy(
                src_ref=c_ref, dst_ref=call_ref.at[mi], send_sem=s1.at[k - 1], recv_sem=r1.at[k - 1],
                device_id=_peer(me, k), device_id_type=MESH_ID))
        for cp in rows_out:
            cp.start()
        gather.start(gx_refs, gout_refs, g_sems)
        for k in range(1, N_DEV):
            pltpu.make_async_remote_copy(
                src_ref=c_ref, dst_ref=call_ref.at[_flat(_peer(me, k))], send_sem=s1.at[k - 1],
                recv_sem=r1.at[k - 1], device_id=_peer(me, k), device_id_type=MESH_ID).wait_recv()
        ca = _silu(call_ref[...].reshape(N_DEV * 8, d))
        for l in range(2):
            part = _mm(ca, w_ref[l]) + b_ref[l]
            for b in range(N_DEV):
                part_ref[b, l] = part[8 * b:8 * b + 8, :]
        modp_ref[mi] = part_ref[mi]
        spread = []
        for k in range(1, N_DEV):
            peer = _peer(me, k)
            spread.append(pltpu.make_async_remote_copy(
                src_ref=part_ref.at[_flat(peer)], dst_ref=modp_ref.at[mi], send_sem=s2.at[k - 1],
                recv_sem=r2.at[k - 1], device_id=peer, device_id_type=MESH_ID))
        for cp in spread:
            cp.start()
        gather.forward(gx_refs, gout_refs, g_sems)
        for k in range(1, N_DEV):
            pi = _flat(_peer(me, k))
            pltpu.make_async_remote_copy(
                src_ref=part_ref.at[pi], dst_ref=modp_ref.at[pi], send_sem=s2.at[k - 1],
                recv_sem=r2.at[k - 1], device_id=_peer(me, k), device_id_type=MESH_ID).wait_recv()
        for cp in rows_out + spread:
            cp.wait_send()
        gather.drain(gx_refs, gout_refs, g_sems)

    vm = pl.BlockSpec(memory_space=pltpu.VMEM)
    res = pl.pallas_call(
        body, name="ada_forward",
        out_shape=(jax.ShapeDtypeStruct((N_DEV, 8, d), F32), jax.ShapeDtypeStruct((N_DEV, 2, 8, w), F32))
        + gather.out_shapes,
        in_specs=[vm, vm, vm] + gather.in_specs, out_specs=(vm, vm) + gather.out_specs,
        scratch_shapes=[pltpu.VMEM((N_DEV, 2, 8, w), F32)] + [pltpu.SemaphoreType.DMA((7,))] * 4 + list(gather.sems),
        compiler_params=pltpu.CompilerParams(vmem_limit_bytes=VMEM_LIMIT),
    )(c8, ada_w, bias_cols, *gather.shards)
    return res[0], res[1], res[2:]


def _modulated(x, mod_ref, nw_ref):
    xn = x * _rms(x)
    g1 = nw_ref[...] * (1.0 + mod_ref[1:2, :])
    return xn, g1, xn * g1 + mod_ref[0:1, :]


def even_in_forward(x, mod, nw, w_in_t, gq, gk, qln, kvln, w_uq_t, uk_bd, bd, cos_a, sin_a, cos_t, sin_t):
    s, d = x.shape
    tm = min(ROW_TILE, s)
    n_nope = B_HEADS * B_NOPE

    def body(x_ref, mod_ref, nw_ref, w_ref, gq_ref, gk_ref, qln_ref, kvln_ref, uq_ref, ukbd_ref, bd_ref,
             ca_ref, sa_ref, ct_ref, st_ref,
             qa_o, ka_o, va_o, qb_o, kb_o, kat_o, vat_o, kbt_o, qa_raw_o, ka_raw_o, cq_raw_o, ckv_raw_o, ga_o, gb_o):
        _, _, h = _modulated(x_ref[...], mod_ref, nw_ref)
        h = h.astype(MXU)

        def proj(cols):
            return _mm_nt(h, w_ref[cols[0]:cols[1], :])

        ca, sa, ct, st = ca_ref[...], sa_ref[...], ct_ref[...], st_ref[...]
        wide = lambda t, n: jnp.concatenate([t] * n, axis=1)
        qa = proj(E_QA)
        qa_raw_o[...] = qa
        qr = _rope(_head_norm(qa, gq_ref[...], bd_ref, HD), wide(ca, 4), wide(sa, 4), 32) * SCALE2_A
        for hh in range(A_HEADS):
            qa_o[hh] = qr[:, HD * hh:HD * hh + HD].astype(MXU)
        ka = proj(E_KA)
        ka_raw_o[...] = ka
        kr = _rope(_head_norm(ka, gk_ref[...], bd_ref[0:128, 0:128], HD), ca, sa, 32)
        va = proj(E_VA)
        krt, vat = kr.T, va.T
        for g in range(A_KV):
            ka_o[g] = kr[:, HD * g:HD * g + HD].astype(MXU)
            va_o[g] = va[:, HD * g:HD * g + HD].astype(MXU)
            kat_o[g] = krt[HD * g:HD * g + HD, :].astype(MXU)
            vat_o[g] = vat[HD * g:HD * g + HD, :].astype(MXU)
        ga_o[...] = proj(E_GA)
        gb_o[...] = proj(E_GB)
        cq = proj(E_CQ)
        cq_raw_o[...] = cq
        qb = _mm_nt(cq * _rms(cq) * qln_ref[...], uq_ref[...])
        q_lat = _mm(qb[:, 0:n_nope], ukbd_ref[...]) * SCALE2_B
        q_rope = _rope(qb[:, n_nope:], wide(ct, 2), wide(st, 2), 32) * SCALE2_B
        for hh in range(B_HEADS):
            qb_o[hh, :, 0:B_KV_LORA] = q_lat[:, B_KV_LORA * hh:B_KV_LORA * (hh + 1)].astype(MXU)
            qb_o[hh, :, B_KV_LORA:B_QK] = q_rope[:, B_ROPE * hh:B_ROPE * (hh + 1)].astype(MXU)
        ckv = proj(E_CKV)
        ckv_raw_o[...] = ckv
        ckv_n = ckv * _rms(ckv) * kvln_ref[...]
        k_rope = _rope(proj(E_KR), ct[:, 0:B_ROPE], st[:, 0:B_ROPE], 32)
        kb_o[0, :, 0:B_KV_LORA] = ckv_n.astype(MXU)
        kb_o[0, :, B_KV_LORA:B_QK] = k_rope.astype(MXU)
        kbt_o[0, 0:B_KV_LORA, :] = ckv_n.T.astype(MXU)
        kbt_o[0, B_KV_LORA:B_QK, :] = k_rope.T.astype(MXU)

    sd = jax.ShapeDtypeStruct
    outs = (sd((A_HEADS, s, HD), MXU), sd((A_KV, s, HD), MXU), sd((A_KV, s, HD), MXU),
            sd((B_HEADS, s, B_QK), MXU), sd((1, s, B_QK), MXU),
            sd((A_KV, HD, s), MXU), sd((A_KV, HD, s), MXU), sd((1, B_QK, s), MXU),
            sd((s, 512), F32), sd((s, 128), F32), sd((s, B_Q_LORA), F32), sd((s, B_KV_LORA), F32),
            sd((s, 512), F32), sd((s, 512), F32))
    out_specs = (_head_spec(A_HEADS, tm, HD), _head_spec(A_KV, tm, HD), _head_spec(A_KV, tm, HD),
                 _head_spec(B_HEADS, tm, B_QK), _head_spec(1, tm, B_QK),
                 _headt_spec(A_KV, HD, tm), _headt_spec(A_KV, HD, tm), _headt_spec(1, B_QK, tm),
                 _row_spec(tm, 512), _row_spec(tm, 128), _row_spec(tm, B_Q_LORA), _row_spec(tm, B_KV_LORA),
                 _row_spec(tm, 512), _row_spec(tm, 512))
    consts = [mod, nw, w_in_t, gq, gk, qln, kvln, w_uq_t, uk_bd, bd]
    return pl.pallas_call(
        body, name="even_in_forward", grid=(s // tm,), out_shape=outs,
        in_specs=[_row_spec(tm, d)] + [_full_spec(a.shape) for a in consts] + [_row_spec(tm, 128)] * 4,
        out_specs=out_specs, compiler_params=_params(("parallel",)),
    )(x, *consts, cos_a, sin_a, cos_t, sin_t)


def odd_in_forward(x, mod, nw, w_in):
    s, d = x.shape
    tm = min(ROW_TILE, s)

    def body(x_ref, mod_ref, nw_ref, w_ref, q_o, k_o, v_o, kt_o, vt_o, g_o):
        _, _, h = _modulated(x_ref[...], mod_ref, nw_ref)
        h = h.astype(MXU)

        def proj(cols):
            return _mm_nt(h, w_ref[cols[0]:cols[1], :])

        q = proj(O_Q) * SCALE2_A
        for hh in range(C_HEADS):
            q_o[hh] = q[:, HD * hh:HD * hh + HD].astype(MXU)
        k = proj(O_K)
        v = proj(O_V)
        for g in range(C_KV):
            kh = k[:, HD * g:HD * g + HD]
            vh = v[:, HD * g:HD * g + HD]
            k_o[g] = kh.astype(MXU)
            v_o[g] = vh.astype(MXU)
            kt_o[g] = kh.T.astype(MXU)
            vt_o[g] = vh.T.astype(MXU)
        g_o[...] = proj(O_G)

    sd = jax.ShapeDtypeStruct
    return pl.pallas_call(
        body, name="odd_in_forward", grid=(s // tm,),
        out_shape=(sd((C_HEADS, s, HD), MXU), sd((C_KV, s, HD), MXU), sd((C_KV, s, HD), MXU),
                   sd((C_KV, HD, s), MXU), sd((C_KV, HD, s), MXU), sd((s, 1024), F32)),
        in_specs=[_row_spec(tm, d), _full_spec(mod.shape), _full_spec(nw.shape), _full_spec(w_in.shape)],
        out_specs=(_head_spec(C_HEADS, tm, HD), _head_spec(C_KV, tm, HD), _head_spec(C_KV, tm, HD),
                   _headt_spec(C_KV, HD, tm), _headt_spec(C_KV, HD, tm), _row_spec(tm, 1024)),
        compiler_params=_params(("parallel",)),
    )(x, mod, nw, w_in)


def latent_out_forward(o_lat, w_uv):
    s = o_lat.shape[0]
    tm = min(ROW_TILE, s)

    def body(o_ref, uv_ref, out_ref):
        for hh in range(B_HEADS):
            out_ref[:, HD * hh:HD * hh + HD] = _mm(o_ref[:, B_KV_LORA * hh:B_KV_LORA * (hh + 1)], uv_ref[hh])

    return pl.pallas_call(
        body, name="latent_out_forward", grid=(s // tm,),
        out_shape=jax.ShapeDtypeStruct((s, B_HEADS * HD), F32),
        in_specs=[_row_spec(tm, o_lat.shape[1]), _full_spec(w_uv.shape)],
        out_specs=_row_spec(tm, B_HEADS * HD),
        compiler_params=_params(("parallel",)),
    )(o_lat, w_uv)


def mixer_out_forward(x, mod, pairs, w_out, name):
    s, d = x.shape
    tm = min(ROW_TILE, s)
    n = len(pairs)
    widths = [o.shape[1] for o, _ in pairs]

    def body(*refs):
        x_ref, mod_ref, w_ref = refs[:3]
        pr = refs[3:3 + 2 * n]
        xo_ref, y_ref = refs[3 + 2 * n:]
        y = jnp.zeros((tm, d), F32)
        r0 = 0
        for i in range(n):
            mix = pr[2 * i][...] * _silu(pr[2 * i + 1][...])
            y = y + _mm(mix, w_ref[r0:r0 + widths[i], :])
            r0 += widths[i]
        y_ref[...] = y.astype(y_ref.dtype)
        xo_ref[...] = x_ref[...] + mod_ref[2:3, :] * y

    flat = [a for p in pairs for a in p]
    sd = jax.ShapeDtypeStruct
    return pl.pallas_call(
        body, name=name, grid=(s // tm,),
        out_shape=(sd((s, d), F32), sd((s, d), MXU)),
        in_specs=[_row_spec(tm, d), _full_spec(mod.shape), _full_spec(w_out.shape)]
        + [_row_spec(tm, a.shape[1]) for a in flat],
        out_specs=(_row_spec(tm, d), _row_spec(tm, d)),
        compiler_params=_params(("parallel",)),
    )(x, mod, w_out, *flat)


ONES_ROWS = 16
AHEAD = 2


def _col_max8(s3):
    m8 = jnp.max(s3, axis=0)
    return jnp.broadcast_to(jnp.max(m8, axis=0, keepdims=True), m8.shape)


def _with_ones(vt, n):
    return jnp.concatenate([vt, jnp.ones((ONES_ROWS, n), vt.dtype)], axis=0)


def _grid_edges(grid):
    ids = [pl.program_id(a) for a in range(len(grid))]
    first = functools.reduce(jnp.logical_and, [i == 0 for i in ids])
    last = functools.reduce(jnp.logical_and, [i == n - 1 for i, n in zip(ids, grid)])
    return first, last


def flash_forward(q, k, vt, *, dv, tq, tk, nsub, name, exchange=None):
    hq, s, dq = q.shape
    g_kv = k.shape[0]
    hpg = hq // g_kv
    nq = s // tq
    tkk = tk * nsub
    nk = s // tkk
    grid = (g_kv, nq, nk)
    hosted = exchange is not None
    m_cols = hpg * tq
    dvp = dv + ONES_ROWS

    def body(*refs):
        nx = exchange.n if hosted else 0
        q_ref, k_ref, vt_ref = refs[:3]
        xs_refs = refs[3:3 + nx]
        o_ref, lse_ref = refs[3 + nx:5 + nx]
        land_refs = refs[5 + nx:5 + 2 * nx]
        m_s, acc_s = refs[5 + 2 * nx:7 + 2 * nx]
        sems = refs[7 + 2 * nx:]
        if hosted:
            first, last = _grid_edges(grid)
            pl.when(first)(lambda: exchange.start(xs_refs, land_refs, sems))
        j = pl.program_id(2)

        @pl.when(j == 0)
        def _():
            m_s[...] = jnp.full((8, m_cols), -jnp.inf, F32)
            acc_s[...] = jnp.zeros((dvp, m_cols), F32)

        qq = q_ref[...].reshape(m_cols, dq)
        score = lambda u: _mm_nt(k_ref[0, tk * u:tk * (u + 1), :], qq).reshape(tk // 8, 8, m_cols)
        sts = {u: score(u) for u in range(min(AHEAD, nsub))}
        m_run = m_s[...]
        acc = acc_s[...]
        for u in range(nsub):
            if u + AHEAD < nsub:
                sts[u + AHEAD] = score(u + AHEAD)
            st = sts.pop(u)
            m_new = jnp.maximum(m_run, _col_max8(st))
            p = jnp.exp2(st - m_new[None])
            alpha = jnp.exp2(m_run - m_new)
            pv = _mm(_with_ones(vt_ref[0, 0:dv, tk * u:tk * (u + 1)], tk), p.reshape(tk, m_cols))
            acc = (acc.reshape(dvp // 8, 8, m_cols) * alpha[None]).reshape(dvp, m_cols) + pv
            m_run = m_new
        acc_s[...] = acc
        m_s[...] = m_run

        @pl.when(j == nk - 1)
        def _():
            l = acc_s[dv:dv + 1, :]
            ot = acc_s[0:dv, :] / l
            lse = m_s[0:1, :] + jnp.log2(l)
            for hh in range(hpg):
                o_ref[:, dv * hh:dv * hh + dv] = ot[:, tq * hh:tq * hh + tq].T
                lse_ref[hh] = lse[:, tq * hh:tq * hh + tq]

        if hosted:
            pl.when(last)(lambda: exchange.wait(xs_refs, land_refs, sems))

    sd = jax.ShapeDtypeStruct
    return pl.pallas_call(
        body, name=name, grid=grid,
        out_shape=(sd((s, hq * dv), F32), sd((hq, 1, s), F32)) + (exchange.land_shapes if hosted else ()),
        in_specs=[pl.BlockSpec((hpg, tq, dq), lambda g, i, j: (g, i, 0)),
                  pl.BlockSpec((1, tkk, k.shape[2]), lambda g, i, j: (g, j, 0)),
                  pl.BlockSpec((1, dv, tkk), lambda g, i, j: (g, 0, j))] + (exchange.in_specs if hosted else []),
        out_specs=(pl.BlockSpec((tq, hpg * dv), lambda g, i, j: (i, g)),
                   pl.BlockSpec((hpg, 1, tq), lambda g, i, j: (g, 0, i))) + (exchange.out_specs if hosted else ()),
        scratch_shapes=[pltpu.VMEM((8, m_cols), F32), pltpu.VMEM((dvp, m_cols), F32)]
        + (list(exchange.sems) if hosted else []),
        compiler_params=_params(("arbitrary",) * 3 if hosted else ("parallel", "parallel", "arbitrary")),
    )(q, k, vt, *(exchange.srcs if hosted else []))


def _window_bias_t(hpg, slope_ref):
    t = WINDOW
    r = lax.broadcasted_iota(jnp.int32, (3 * t, t), 0)
    cq = lax.broadcasted_iota(jnp.int32, (3 * t, t), 1)
    arel = jnp.abs(r - t - cq)
    base = jnp.where(arel <= WINDOW, arel.astype(F32) * (-LOG2E), -jnp.inf)
    return jnp.concatenate([base * slope_ref[hh] for hh in range(hpg)], axis=1)


def _window_edges_t(bias, no_before, no_after):
    t = WINDOW
    r = lax.broadcasted_iota(jnp.int32, bias.shape, 0)
    out = ((r < t) & no_before) | ((r >= 2 * t) & no_after)
    return jnp.where(out, -jnp.inf, bias)


def _window_specs(kind, nb, nblk, d):
    t = WINDOW
    before = lambda i: jnp.clip(i * nb - 1, 0, nblk - 1)
    after = lambda i: jnp.clip((i + 1) * nb, 0, nblk - 1)
    if kind == "rows":
        return [pl.BlockSpec((1, t, d), lambda g, i: (g, before(i), 0)),
                pl.BlockSpec((1, nb * t, d), lambda g, i: (g, i, 0)),
                pl.BlockSpec((1, t, d), lambda g, i: (g, after(i), 0))]
    return [pl.BlockSpec((1, d, t), lambda g, i: (g, 0, before(i))),
            pl.BlockSpec((1, d, nb * t), lambda g, i: (g, 0, i)),
            pl.BlockSpec((1, d, t), lambda g, i: (g, 0, after(i)))]


def window_forward(q, k, vt, sink2, slopes, nb, name):
    hq, s, d = q.shape
    g_kv = k.shape[0]
    hpg = hq // g_kv
    t = WINDOW
    nblk = s // t
    steps = nblk // nb
    m_cols = hpg * t

    def body(q_ref, kp, ko, kn, vp, vo, vn, sink_ref, slope_ref, o_ref, lse_ref):
        i = pl.program_id(1)
        kk_all = jnp.concatenate([kp[0], ko[0], kn[0]], axis=0)
        vt_all = jnp.concatenate([vp[0], vo[0], vn[0]], axis=1)
        bias = _window_bias_t(hpg, slope_ref)
        sink_row = jnp.concatenate([jnp.broadcast_to(sink_ref[hh], (8, t)) for hh in range(hpg)], axis=1)
        sts = {}

        def score(u):
            qq = q_ref[:, t * u:t * (u + 1), :].reshape(m_cols, d)
            b_u = bias
            if u == 0 or u == nb - 1:
                b_u = _window_edges_t(bias, (i == 0) if u == 0 else False,
                                      (i == steps - 1) if u == nb - 1 else False)
            sts[u] = _mm_nt(kk_all[t * u:t * (u + 3), :], qq) + b_u

        for u in range(min(AHEAD, nb)):
            score(u)
        for u in range(nb):
            if u + AHEAD < nb:
                score(u + AHEAD)
            s3 = sts.pop(u).reshape(3 * t // 8, 8, m_cols)
            m8 = jnp.maximum(_col_max8(s3), sink_row)
            p = jnp.exp2(s3 - m8[None]).reshape(3 * t, m_cols)
            acc = _mm(_with_ones(vt_all[:, t * u:t * (u + 3)], 3 * t), p)
            l = acc[d:d + 1, :] + jnp.exp2(sink_row[0:1, :] - m8[0:1, :])
            ot = acc[0:d, :] / l
            lse = m8[0:1, :] + jnp.log2(l)
            for hh in range(hpg):
                o_ref[t * u:t * (u + 1), d * hh:d * hh + d] = ot[:, t * hh:t * hh + t].T
                lse_ref[hh, :, t * u:t * (u + 1)] = lse[:, t * hh:t * hh + t]

    sd = jax.ShapeDtypeStruct
    return pl.pallas_call(
        body, name=name, grid=(g_kv, steps),
        out_shape=(sd((s, hq * d), F32), sd((hq, 1, s), F32)),
        in_specs=[pl.BlockSpec((hpg, nb * t, d), lambda g, i: (g, i, 0))]
        + _window_specs("rows", nb, nblk, d) + _window_specs("cols", nb, nblk, d)
        + [pl.BlockSpec((hpg, 1, 1), lambda g, i: (g, 0, 0))] * 2,
        out_specs=(pl.BlockSpec((nb * t, hpg * d), lambda g, i: (i, g)),
                   pl.BlockSpec((hpg, 1, nb * t), lambda g, i: (g, 0, i))),
        compiler_params=_params(("parallel", "parallel")),
    )(q, k, k, k, vt, vt, vt, sink2, slopes)


def window_backward(q, k, kt, v, do, lse, delta, slopes, nb, name):
    hq, s, d = q.shape
    g_kv = k.shape[0]
    hpg = hq // g_kv
    t = WINDOW
    nblk = s // t
    steps = nblk // nb
    m_cols = hpg * t

    def body(q_ref, kp, ko, kn, ktp, kto, ktn, vp, vo, vn, do_ref, lse_ref, dl_ref, slope_ref,
             dq_ref, dk_ref, dv_ref, dk_s, dv_s):
        i = pl.program_id(1)

        @pl.when(i == 0)
        def _():
            dk_ref[...] = jnp.zeros(dk_ref.shape, F32)
            dv_ref[...] = jnp.zeros(dv_ref.shape, F32)

        dk_s[...] = jnp.zeros(dk_s.shape, F32)
        dv_s[...] = jnp.zeros(dv_s.shape, F32)
        kk_all = jnp.concatenate([kp[0], ko[0], kn[0]], axis=0)
        vv_all = jnp.concatenate([vp[0], vo[0], vn[0]], axis=0)
        kkt_all = jnp.concatenate([ktp[0], kto[0], ktn[0]], axis=1)
        bias = _window_bias_t(hpg, slope_ref)
        qqs, dds, sts, dps = {}, {}, {}, {}

        def issue(u):
            rows = slice(t * u, t * (u + 1))
            keys = slice(t * u, t * (u + 3))
            qqs[u] = q_ref[:, rows, :].reshape(m_cols, d)
            dds[u] = jnp.concatenate([do_ref[rows, d * hh:d * hh + d] for hh in range(hpg)], axis=0)
            b_u = bias
            if u == 0 or u == nb - 1:
                b_u = _window_edges_t(bias, (i == 0) if u == 0 else False,
                                      (i == steps - 1) if u == nb - 1 else False)
            sts[u] = _mm_nt(kk_all[keys, :], qqs[u]) + b_u
            dps[u] = _mm_nt(vv_all[keys, :], dds[u])

        for u in range(min(AHEAD, nb)):
            issue(u)
        for u in range(nb):
            if u + AHEAD < nb:
                issue(u + AHEAD)
            rows = slice(t * u, t * (u + 1))
            keys = slice(t * u, t * (u + 3))
            lse_row = jnp.concatenate([lse_ref[hh, :, rows] for hh in range(hpg)], axis=1)
            dl_row = jnp.concatenate([dl_ref[hh, :, rows] for hh in range(hpg)], axis=1)
            p = jnp.exp2(sts[u] - lse_row)
            ds = p * (dps[u] - dl_row) * SCALE_A
            dv_s[keys, :] += _mm(p, dds[u])
            dk_s[keys, :] += _mm(ds, qqs[u])
            dqt = _mm(kkt_all[:, keys], ds)
            for hh in range(hpg):
                dq_ref[rows, d * hh:d * hh + d] = dqt[:, t * hh:t * hh + t].T.astype(dq_ref.dtype)
        tq = nb * t
        for src, r0, n in ((0, jnp.clip(i * nb - 1, 0, nblk - 1) * t, t), (t, i * tq, tq),
                           (t + tq, jnp.clip((i + 1) * nb, 0, nblk - 1) * t, t)):
            dst = pl.ds(pl.multiple_of(r0, t), n)
            dk_ref[0, dst, :] += dk_s[src:src + n, :] * (1.0 / SCALE2_A)
            dv_ref[0, dst, :] += dv_s[src:src + n, :]

    row_map = lambda g, i: (g, 0, i)
    sd = jax.ShapeDtypeStruct
    return pl.pallas_call(
        body, name=name, grid=(g_kv, steps),
        out_shape=(sd((s, hq * d), MXU), sd((g_kv, s, d), F32), sd((g_kv, s, d), F32)),
        in_specs=[pl.BlockSpec((hpg, nb * t, d), lambda g, i: (g, i, 0))]
        + _window_specs("rows", nb, nblk, d) + _window_specs("cols", nb, nblk, d) + _window_specs("rows", nb, nblk, d)
        + [pl.BlockSpec((nb * t, hpg * d), lambda g, i: (i, g)), pl.BlockSpec((hpg, 1, nb * t), row_map),
           pl.BlockSpec((hpg, 1, nb * t), row_map), pl.BlockSpec((hpg, 1, 1), lambda g, i: (g, 0, 0))],
        out_specs=(pl.BlockSpec((nb * t, hpg * d), lambda g, i: (i, g)),
                   pl.BlockSpec((1, s, d), lambda g, i: (g, 0, 0)),
                   pl.BlockSpec((1, s, d), lambda g, i: (g, 0, 0))),
        scratch_shapes=[pltpu.VMEM(((nb + 2) * t, d), F32), pltpu.VMEM(((nb + 2) * t, d), F32)],
        compiler_params=_params(("parallel", "arbitrary")),
    )(q, k, k, k, kt, kt, kt, v, v, v, do, lse, delta, slopes)


def flash_backward(q, k, kt, v, do, lse, delta, *, scale, dv, tq, tk, nsub, gq, name, split=None, exchange=None):
    hq, s, dq = q.shape
    g_kv = k.shape[0]
    hpg = hq // gq
    nq = s // tq
    tqq = tq * nsub
    nqs = s // tqq
    nkb = s // tk
    grid = (gq, nkb, nqs)
    hosted = exchange is not None
    m_cols = hpg * tq
    c = scale * LOG2E
    has_v = v is not None

    def body(*refs):
        it = iter(refs)
        q_ref, k_ref, kt_ref = next(it), next(it), next(it)
        v_ref = next(it) if has_v else None
        do_ref, lse_ref, dl_ref = next(it), next(it), next(it)
        nx = exchange.n if hosted else 0
        xs_refs = [next(it) for _ in range(nx)]
        dq_ref, dk_ref, dv_ref = next(it), next(it), next(it)
        land_refs = [next(it) for _ in range(nx)]
        dqt_s = next(it)
        sems = list(it)
        kj = pl.program_id(1)
        qi = pl.program_id(2)
        if hosted:
            first, last = _grid_edges(grid)
            pl.when(first)(lambda: exchange.start(xs_refs, land_refs, sems))

        @pl.when((kj == 0) & (qi == 0))
        def _():
            dqt_s[...] = jnp.zeros(dqt_s.shape, F32)

        @pl.when(qi == 0)
        def _():
            dk_ref[...] = jnp.zeros(dk_ref.shape, F32)
            dv_ref[...] = jnp.zeros(dv_ref.shape, F32)

        kk = k_ref[0]
        vv = v_ref[0] if has_v else kk[:, :dv]
        qqs, dds, sts, dps = {}, {}, {}, {}

        def issue(u):
            rows = slice(tq * u, tq * (u + 1))
            qqs[u] = q_ref[:, rows, :].reshape(m_cols, dq)
            dds[u] = jnp.concatenate([do_ref[rows, dv * hh:dv * hh + dv] for hh in range(hpg)], axis=0)
            sts[u] = _mm_nt(kk, qqs[u])
            dps[u] = _mm_nt(vv, dds[u])

        for u in range(min(AHEAD, nsub)):
            issue(u)
        dv_acc = dv_ref[0]
        dk_acc = dk_ref[0]
        for u in range(nsub):
            if u + AHEAD < nsub:
                issue(u + AHEAD)
            rows = slice(tq * u, tq * (u + 1))
            lse_row = jnp.concatenate([lse_ref[hh, :, rows] for hh in range(hpg)], axis=1)
            dl_row = jnp.concatenate([dl_ref[hh, :, rows] for hh in range(hpg)], axis=1)
            p = jnp.exp2(sts[u] - lse_row)
            ds = p * (dps[u] - dl_row) * scale
            dv_acc = dv_acc + _mm(p, dds[u])
            dk_acc = dk_acc + _mm(ds, qqs[u])
            dqt = _mm(kt_ref[0], ds)
            for hh in range(hpg):
                dqt_s[qi * nsub + u, dq * hh:dq * hh + dq, :] += dqt[:, tq * hh:tq * hh + tq]
        dv_ref[0] = dv_acc
        dk_ref[0] = jnp.where(qi == nqs - 1, dk_acc * (1.0 / c), dk_acc)

        @pl.when((kj == nkb - 1) & (qi == nqs - 1))
        def _():
            def emit(t, carry):
                r0 = pl.multiple_of(t * tq, tq)
                for hh in range(hpg):
                    blk = dqt_s[t, dq * hh:dq * hh + dq, :].T
                    if split is None:
                        dq_ref[pl.ds(r0, tq), dq * hh:dq * hh + dq] = blk
                    else:
                        rest = dq - split
                        dq_ref[pl.ds(r0, tq), split * hh:split * (hh + 1)] = blk[:, 0:split]
                        dq_ref[pl.ds(r0, tq), hpg * split + rest * hh:hpg * split + rest * (hh + 1)] = blk[:, split:]
                return carry

            lax.fori_loop(0, nq, emit, 0)

        if hosted:
            pl.when(last)(lambda: exchange.wait(xs_refs, land_refs, sems))

    kv_of = lambda g: g * g_kv // gq
    in_specs = [pl.BlockSpec((hpg, tqq, dq), lambda g, kj, qi: (g, qi, 0)),
                pl.BlockSpec((1, tk, dq), lambda g, kj, qi: (kv_of(g), kj, 0)),
                pl.BlockSpec((1, dq, tk), lambda g, kj, qi: (kv_of(g), 0, kj))]
    args = [q, k, kt]
    if has_v:
        in_specs.append(pl.BlockSpec((1, tk, dv), lambda g, kj, qi: (kv_of(g), kj, 0)))
        args.append(v)
    row_map = lambda g, kj, qi: (g, 0, qi)
    in_specs += [pl.BlockSpec((tqq, hpg * dv), lambda g, kj, qi: (qi, g)),
                 pl.BlockSpec((hpg, 1, tqq), row_map), pl.BlockSpec((hpg, 1, tqq), row_map)]
    args += [do, lse, delta]
    if hosted:
        in_specs += exchange.in_specs
        args += exchange.srcs
    sd = jax.ShapeDtypeStruct
    return pl.pallas_call(
        body, name=name, grid=grid,
        out_shape=(sd((s, hq * dq), F32), sd((gq, s, dq), F32), sd((gq, s, dv), F32))
        + (exchange.land_shapes if hosted else ()),
        in_specs=in_specs,
        out_specs=(pl.BlockSpec((s, hpg * dq), lambda g, kj, qi: (0, g)),
                   pl.BlockSpec((1, tk, dq), lambda g, kj, qi: (g, kj, 0)),
                   pl.BlockSpec((1, tk, dv), lambda g, kj, qi: (g, kj, 0))) + (exchange.out_specs if hosted else ()),
        scratch_shapes=[pltpu.VMEM((nq, hpg * dq, tq), F32)] + (list(exchange.sems) if hosted else []),
        compiler_params=_params(("arbitrary",) * 3 if hosted else ("parallel", "arbitrary", "arbitrary")),
    )(*args)


def loss_head(x, target, fnw):
    s, d = x.shape
    tm = min(ROW_TILE, s)

    def body(x_ref, t_ref, w_ref, lp_ref, dx_ref, dw_ref):
        @pl.when(pl.program_id(0) == 0)
        def _():
            lp_ref[...] = jnp.zeros(lp_ref.shape, F32)
            dw_ref[...] = jnp.zeros(dw_ref.shape, F32)

        x = x_ref[...]
        g = w_ref[...]
        err = x * _rms(x) * g - t_ref[...]
        lp_ref[...] += jnp.sum(err * err, axis=0, keepdims=True)
        dx, dg = _rms_bwd(err * (1.0 / d), x, g)
        dx_ref[...] = dx
        dw_ref[...] += jnp.sum(dg, axis=0, keepdims=True)

    sd = jax.ShapeDtypeStruct
    return pl.pallas_call(
        body, name="loss_head", grid=(s // tm,),
        out_shape=(sd((1, d), F32), sd((s, d), F32), sd((1, d), F32)),
        in_specs=[_row_spec(tm, d), _row_spec(tm, d), _full_spec(fnw.shape)],
        out_specs=(_full_spec((1, d)), _row_spec(tm, d), _full_spec((1, d))),
        compiler_params=_params(("arbitrary",)),
    )(x, target, fnw)


def mixer_out_backward(dx, y, mod, pairs, w_out, delta_heads, name, lse=None, sink=None):
    s, d = dx.shape
    tm = min(ROW_TILE, s)
    n = len(pairs)
    widths = [o.shape[1] for o, _ in pairs]
    n_delta = sum(1 for h in delta_heads if h)
    with_sink = lse is not None

    def body(*refs):
        it = iter(refs)
        dx_ref, y_ref, mod_ref, wt_ref = next(it), next(it), next(it), next(it)
        pr = [next(it) for _ in range(2 * n)]
        lse_ref = next(it) if with_sink else None
        sink_ref = next(it) if with_sink else None
        outs = [next(it) for _ in range(2 * n)]
        dl_refs = [next(it) for _ in range(n_delta)]
        dgate_ref, dw_ref = next(it), next(it)
        dsink_ref = next(it) if with_sink else None

        @pl.when(pl.program_id(0) == 0)
        def _():
            dgate_ref[...] = jnp.zeros(dgate_ref.shape, F32)
            dw_ref[...] = jnp.zeros(dw_ref.shape, F32)
            if with_sink:
                dsink_ref[...] = jnp.zeros(dsink_ref.shape, F32)

        dxo = dx_ref[...]
        dgate_ref[...] += jnp.sum(dxo * y_ref[...].astype(F32), axis=0, keepdims=True)
        dy = (dxo * mod_ref[2:3, :]).astype(MXU)
        dmix = _mm_nt(dy, wt_ref[...])
        r0 = 0
        di = 0
        for i in range(n):
            o = pr[2 * i][...]
            g = pr[2 * i + 1][...]
            dm = dmix[:, r0:r0 + widths[i]]
            sg = _sigmoid(g)
            act = g * sg
            do = dm * act
            outs[2 * i][...] = do.astype(MXU)
            outs[2 * i + 1][...] = (dm * o * (sg * (1.0 + g * (1.0 - sg)))).astype(MXU)
            dw_ref[r0:r0 + widths[i], :] += _mm_tn(o * act, dy)
            if delta_heads[i]:
                dlt = _group_sums_t(do * o, HD)[0:delta_heads[i], :]
                dl_refs[di][...] = dlt
                if with_sink:
                    ps = jnp.exp2(sink_ref[...] - lse_ref[...])
                    dsink_ref[...] += -jnp.sum(ps * dlt, axis=1, keepdims=True)
                di += 1
            r0 += widths[i]

    flat = [a for p in pairs for a in p]
    sd = jax.ShapeDtypeStruct
    in_specs = [_row_spec(tm, d), _row_spec(tm, d), _full_spec(mod.shape), _full_spec(w_out.shape)]
    in_specs += [_row_spec(tm, a.shape[1]) for a in flat]
    args = [dx, y, mod, w_out] + flat
    if with_sink:
        nh = lse.shape[0]
        in_specs += [_rows_spec(nh, tm), _full_spec(sink.shape)]
        args += [lse, sink]
    out_shape = [sd((s, a.shape[1]), MXU) for a in flat]
    out_specs = [_row_spec(tm, a.shape[1]) for a in flat]
    for h in delta_heads:
        if h:
            out_shape.append(sd((h, s), F32))
            out_specs.append(_rows_spec(h, tm))
    out_shape += [sd((1, d), F32), sd((sum(widths), d), F32)]
    out_specs += [_full_spec((1, d)), _full_spec((sum(widths), d))]
    if with_sink:
        out_shape.append(sd((lse.shape[0], 1), F32))
        out_specs.append(_full_spec((lse.shape[0], 1)))
    return pl.pallas_call(
        body, name=name, grid=(s // tm,), out_shape=tuple(out_shape), in_specs=in_specs, out_specs=tuple(out_specs),
        compiler_params=_params(("arbitrary",)),
    )(*args)


def latent_out_backward(d_ob, o_lat, w_uv):
    s = o_lat.shape[0]
    tm = min(ROW_TILE, s)

    def body(d_ref, o_ref, uv_ref, dol_ref, dl_ref, duv_ref, prod_s):
        @pl.when(pl.program_id(0) == 0)
        def _():
            duv_ref[...] = jnp.zeros(duv_ref.shape, F32)

        for hh in range(B_HEADS):
            dh = d_ref[:, HD * hh:HD * hh + HD]
            ol = o_ref[:, B_KV_LORA * hh:B_KV_LORA * (hh + 1)]
            dol = _mm_nt(dh, uv_ref[hh])
            dol_ref[:, B_KV_LORA * hh:B_KV_LORA * (hh + 1)] = dol.astype(MXU)
            prod_s[:, B_KV_LORA * hh:B_KV_LORA * (hh + 1)] = dol * ol
            duv_ref[:, HD * hh:HD * hh + HD] += _mm_tn(ol, dh)
        dl_ref[...] = _group_sums_t(prod_s[...], B_KV_LORA)[0:B_HEADS, :]

    sd = jax.ShapeDtypeStruct
    duv_shape = (B_KV_LORA, B_HEADS * HD)
    return pl.pallas_call(
        body, name="latent_out_backward", grid=(s // tm,),
        out_shape=(sd(o_lat.shape, MXU), sd((B_HEADS, s), F32), sd(duv_shape, F32)),
        in_specs=[_row_spec(tm, d_ob.shape[1]), _row_spec(tm, o_lat.shape[1]), _full_spec(w_uv.shape)],
        out_specs=(_row_spec(tm, o_lat.shape[1]), _rows_spec(B_HEADS, tm), _full_spec(duv_shape)),
        scratch_shapes=[pltpu.VMEM((tm, o_lat.shape[1]), F32)],
        compiler_params=_params(("arbitrary",)),
    )(d_ob, o_lat, w_uv)


def even_prep_backward(dqa, dka, dva, dqb, dkb, dvb, qa_raw, ka_raw, cq_raw, ckv_raw,
                       gq, gk, qln, kvln, w_uq_t, uk_bd, bd, cos_a, sin_a, cos_t, sin_t):
    s = qa_raw.shape[0]
    tm = min(ROW_TILE, s)
    half_lat = B_KV_LORA * B_HEADS // 2
    half_w = dqb.shape[1] // 2

    def body(dqa_ref, dka_ref, dva_ref, dqb_ref, dkb_ref, dvb_ref, qa_ref, ka_ref, cq_ref, ckv_ref,
             gq_ref, gk_ref, qln_ref, kvln_ref, uqt_ref, ukbd_ref, bd_ref, ca_ref, sa_ref, ct_ref, st_ref,
             pqa, pka, pva, pcq, pckv, pkr, gqn, gkn, gqln, gkvln, guq, guk):
        @pl.when(pl.program_id(0) == 0)
        def _():
            for r in (gqn, gkn, gqln, gkvln, guq, guk):
                r[...] = jnp.zeros(r.shape, F32)

        ca, sa, ct, st = ca_ref[...], sa_ref[...], ct_ref[...], st_ref[...]
        wide = lambda t, n: jnp.concatenate([t] * n, axis=1)
        rows = lambda a: jnp.sum(a, axis=0, keepdims=True)
        dx, dg = _head_norm_bwd(_rope_t(dqa_ref[...], wide(ca, 4), wide(sa, 4), 32), qa_ref[...], gq_ref[...],
                                bd_ref, HD)
        pqa[...] = dx.astype(MXU)
        gqn[...] += rows(dg)
        dk_all = jnp.concatenate([dka_ref[g] for g in range(A_KV)], axis=1)
        dx, dg = _head_norm_bwd(_rope_t(dk_all, ca, sa, 32), ka_ref[...], gk_ref[...], bd_ref[0:128, 0:128], HD)
        pka[...] = dx.astype(MXU)
        gkn[...] += rows(dg)
        pva[...] = jnp.concatenate([dva_ref[g] for g in range(A_KV)], axis=1).astype(MXU)
        cq_raw = cq_ref[...]
        cq_n = cq_raw * _rms(cq_raw) * qln_ref[...]
        qb = _mm_nt(cq_n, uqt_ref[...])
        d_lat = jnp.concatenate([dqb_ref[:, 0:half_lat], dqb_ref[:, half_w:half_w + half_lat]], axis=1)
        d_rope = jnp.concatenate([dqb_ref[:, half_lat:half_w], dqb_ref[:, half_w + half_lat:]], axis=1)
        for hh in range(B_HEADS):
            guk[:, B_NOPE * hh:B_NOPE * (hh + 1)] += _mm_tn(d_lat[:, B_KV_LORA * hh:B_KV_LORA * (hh + 1)],
                                                            qb[:, B_NOPE * hh:B_NOPE * (hh + 1)])
        dqb_all = jnp.concatenate([_mm_nt(d_lat, ukbd_ref[...]),
                                   _rope_t(d_rope, wide(ct, 2), wide(st, 2), 32)], axis=1)
        guq[...] += _mm_tn(dqb_all, cq_n)
        dx, dg = _rms_bwd(_mm(dqb_all, uqt_ref[...]), cq_raw, qln_ref[...])
        pcq[...] = dx.astype(MXU)
        gqln[...] += rows(dg)
        dkb_sum = dkb_ref[0] + dkb_ref[1]
        dckv = dkb_sum[:, 0:B_KV_LORA] + dvb_ref[0] + dvb_ref[1]
        dx, dg = _rms_bwd(dckv, ckv_ref[...], kvln_ref[...])
        pckv[...] = dx.astype(MXU)
        gkvln[...] += rows(dg)
        pkr[...] = _rope_t(dkb_sum[:, B_KV_LORA:B_QK], ct[:, 0:B_ROPE], st[:, 0:B_ROPE], 32).astype(MXU)

    sd = jax.ShapeDtypeStruct
    consts = [gq, gk, qln, kvln, w_uq_t, uk_bd, bd]
    in_specs = [_row_spec(tm, 512), _head_spec(A_KV, tm, HD), _head_spec(A_KV, tm, HD),
                _row_spec(tm, dqb.shape[1]), _head_spec(2, tm, B_QK), _head_spec(2, tm, B_KV_LORA),
                _row_spec(tm, 512), _row_spec(tm, 128), _row_spec(tm, B_Q_LORA), _row_spec(tm, B_KV_LORA)]
    in_specs += [_full_spec(a.shape) for a in consts] + [_row_spec(tm, 128)] * 4
    small = [sd(gq.shape, F32), sd(gk.shape, F32), sd(qln.shape, F32), sd(kvln.shape, F32), sd(w_uq_t.shape, F32),
             sd((B_KV_LORA, B_HEADS * B_NOPE), F32)]
    out_shape = (sd((s, 512), MXU), sd((s, 128), MXU), sd((s, 128), MXU), sd((s, B_Q_LORA), MXU),
                 sd((s, B_KV_LORA), MXU), sd((s, B_ROPE), MXU), *small)
    out_specs = (_row_spec(tm, 512), _row_spec(tm, 128), _row_spec(tm, 128), _row_spec(tm, B_Q_LORA),
                 _row_spec(tm, B_KV_LORA), _row_spec(tm, B_ROPE), *[_full_spec(a.shape) for a in small])
    return pl.pallas_call(
        body, name="even_prep_backward", grid=(s // tm,), out_shape=out_shape, in_specs=in_specs, out_specs=out_specs,
        compiler_params=_params(("arbitrary",)),
    )(dqa, dka, dva, dqb, dkb, dvb, qa_raw, ka_raw, cq_raw, ckv_raw, *consts, cos_a, sin_a, cos_t, sin_t)


def in_proj_backward(x, mod, nw, pieces, name, *, dx_out=None, w_in_t=None, dw_rows=None, exchange=None):
    s, d = x.shape
    tm = min(ROW_TILE, s)
    grid = (s // tm,)
    n = len(pieces)
    cols = [c for _, c in pieces]
    want_dx = w_in_t is not None
    want_dw = dw_rows is not None
    n_cols = sum(c1 - c0 for c0, c1 in cols)
    hosted = exchange is not None
    nx = exchange.n if hosted else 0

    def body(*refs):
        it = iter(refs)
        x_ref, mod_ref, nw_ref = next(it), next(it), next(it)
        dxo_ref, wt_ref = (next(it), next(it)) if want_dx else (None, None)
        p_refs = [next(it) for _ in range(n)]
        xs_refs = [next(it) for _ in range(nx)]
        dx_ref, dv_ref = (next(it), next(it)) if want_dx else (None, None)
        dw_ref = next(it) if want_dw else None
        land_refs = [next(it) for _ in range(nx)]
        acc_ref = next(it) if want_dx else None
        dw_acc = next(it) if want_dw else None
        sems = list(it)
        first, last = _grid_edges(grid)
        if hosted:
            pl.when(first)(lambda: exchange.start(xs_refs, land_refs, sems))

        @pl.when(first)
        def _():
            if want_dw:
                dw_acc[...] = jnp.zeros(dw_acc.shape, F32)
            if want_dx:
                acc_ref[...] = jnp.zeros(acc_ref.shape, F32)

        xn, g1, h = _modulated(x_ref[...], mod_ref, nw_ref)
        hb = h.astype(MXU)
        dh = jnp.zeros((tm, d), F32)
        for k, (pr, (c0, c1)) in enumerate(zip(p_refs, cols)):
            if len(pr.shape) == 3:
                pc = jnp.concatenate([pr[g] for g in range(pr.shape[0])], axis=1).astype(MXU)
            else:
                pc = pr[...].astype(MXU)
            if want_dx:
                dh = dh + jnp.dot(pc, wt_ref[c0:c1, :], preferred_element_type=F32)
            if want_dw:
                r0, r1 = dw_rows[k]
                dw_acc[r0:r1, :] += _mm_tn(pc, hb)
        if want_dx:
            acc_ref[0:1, :] += jnp.sum(dh, axis=0, keepdims=True)
            acc_ref[1:2, :] += jnp.sum(dh * xn, axis=0, keepdims=True)
            dxn = dh * g1
            x = x_ref[...]
            dx_ref[...] = dxo_ref[...] + _rms(x) * (dxn - xn * jnp.mean(dxn * xn, axis=-1, keepdims=True))

        @pl.when(last)
        def _():
            if want_dx:
                dg1 = acc_ref[1:2, :]
                dv_ref[0:1, :] = acc_ref[0:1, :]
                dv_ref[1:2, :] = dg1 * nw_ref[...]
                dv_ref[2:3, :] = dg1 * (1.0 + mod_ref[1:2, :])
                dv_ref[3:4, :] = jnp.zeros((1, d), F32)
            if want_dw:
                dw_ref[...] = dw_acc[...].astype(MXU)

        if hosted:
            pl.when(last)(lambda: exchange.wait(xs_refs, land_refs, sems))

    arrs = [a for a, _ in pieces]
    sd = jax.ShapeDtypeStruct
    args = [x, mod, nw] + ([dx_out, w_in_t] if want_dx else []) + arrs + (exchange.srcs if hosted else [])
    in_specs = [_row_spec(tm, d), _full_spec(mod.shape), _full_spec(nw.shape)]
    in_specs += [_row_spec(tm, d), _full_spec(w_in_t.shape)] if want_dx else []
    in_specs += [_row_spec(tm, a.shape[1]) if a.ndim == 2 else _head_spec(a.shape[0], tm, a.shape[2]) for a in arrs]
    in_specs += exchange.in_specs if hosted else []
    out_shape, out_specs, scratch = [], [], []
    if want_dx:
        out_shape += [sd((s, d), F32), sd((4, d), F32)]
        out_specs += [_row_spec(tm, d), _full_spec((4, d))]
        scratch.append(pltpu.VMEM((8, d), F32))
    if want_dw:
        out_shape.append(sd((n_cols, d), MXU))
        out_specs.append(_full_spec((n_cols, d)))
        scratch.append(pltpu.VMEM((n_cols, d), F32))
    if hosted:
        out_shape += list(exchange.land_shapes)
        out_specs += list(exchange.out_specs)
        scratch += list(exchange.sems)
    return pl.pallas_call(
        body, name=name, grid=grid, out_shape=tuple(out_shape), in_specs=in_specs, out_specs=tuple(out_specs),
        scratch_shapes=scratch, compiler_params=_params(("arbitrary",)),
    )(*args)


def ada_weight_grad(c_all, dmod_cols):
    d = c_all.shape[1]
    w = dmod_cols.shape[2]

    def body(c_ref, dm_ref, out_ref):
        ca = _silu(c_ref[...])
        for l in range(2):
            out_ref[l] = _mm_tn(ca, dm_ref[l])

    return pl.pallas_call(
        body, name="ada_weight_grad",
        out_shape=jax.ShapeDtypeStruct((2, d, w), F32),
        compiler_params=pltpu.CompilerParams(vmem_limit_bytes=VMEM_LIMIT),
    )(c_all, dmod_cols)


def _slot_sum(g_ref):
    g = g_ref[0].astype(F32)
    for k in range(1, g_ref.shape[0]):
        g = g + g_ref[k].astype(F32)
    return g


def _adamw_math(g, w, m, v):
    m_new = ADAM_B1 * m + (1.0 - ADAM_B1) * g
    v_new = ADAM_B2 * v + (1.0 - ADAM_B2) * (g * g)
    m_hat = m_new / (1.0 - ADAM_B1 ** ADAM_STEP)
    v_hat = v_new / (1.0 - ADAM_B2 ** ADAM_STEP)
    return -ADAM_LR * (m_hat / (jnp.sqrt(v_hat) + ADAM_EPS) + ADAM_WD * w), m_new, v_new


def adamw_small(g_alls, ws, ms, vs, loss_all):
    n = len(ws)

    def body(*refs):
        g_refs, w_refs, m_refs, v_refs = (refs[i * n:(i + 1) * n] for i in range(4))
        loss_ref = refs[4 * n]
        outs = refs[4 * n + 1:]
        for i in range(n):
            g = _slot_sum(g_refs[i])
            outs[i][...] = g
            outs[n + i][...], outs[2 * n + i][...], outs[3 * n + i][...] = _adamw_math(
                g, w_refs[i][...], m_refs[i][...], v_refs[i][...])
        outs[4 * n][...] = _slot_sum(loss_ref)

    sds = [jax.ShapeDtypeStruct(w.shape, F32) for w in ws]
    res = pl.pallas_call(
        body, name="adamw_small", out_shape=tuple(sds * 4) + (jax.ShapeDtypeStruct(loss_all.shape[1:], F32),),
        compiler_params=pltpu.CompilerParams(vmem_limit_bytes=VMEM_LIMIT),
    )(*g_alls, *ws, *ms, *vs, loss_all)
    return [res[i * n:(i + 1) * n] for i in range(4)], res[4 * n]


def adamw_rows(g_slots, w, m, v, name):
    n, r, lanes = g_slots.shape
    fits = [t for t in range(16, r + 1, 16) if r % t == 0 and t * lanes <= ADAM_TILE]
    tr = max(fits) if fits else r
    def body(g_ref, w_ref, m_ref, v_ref, go, do, mo, vo):
        g = _slot_sum(g_ref)
        go[...] = g
        do[...], mo[...], vo[...] = _adamw_math(g, w_ref[...], m_ref[...], v_ref[...])

    row = pl.BlockSpec((tr, lanes), lambda i: (i, 0))
    sd = jax.ShapeDtypeStruct((r, lanes), F32)
    return pl.pallas_call(
        body, name=name, grid=(r // tr,), out_shape=(sd, sd, sd, sd),
        in_specs=[pl.BlockSpec((n, tr, lanes), lambda i: (0, i, 0)), row, row, row],
        out_specs=(row, row, row, row),
        compiler_params=_params(("parallel",)),
    )(g_slots, w, m, v)


def _rope_tables(s):
    def cs(pos, dim):
        inv = ROPE_THETA ** (-np.arange(0, dim, 2, dtype=np.float32) / dim)
        ang = pos.astype(np.float32)[:, None] * inv.astype(np.float32)[None, :]
        return np.cos(ang), np.sin(ang)

    rows = s // GRID_W
    row = np.repeat(np.arange(rows), GRID_W)
    col = np.tile(np.arange(GRID_W), rows)
    cr, sr = cs(row, HD // 2)
    cc, sc = cs(col, HD // 2)
    ct, st = cs(np.arange(s), B_ROPE)
    tables = (np.concatenate([cr, cr, cc, cc] * 2, axis=-1), np.concatenate([-sr, sr, -sc, sc] * 2, axis=-1),
              np.concatenate([ct, ct] * 4, axis=-1), np.concatenate([-st, st] * 4, axis=-1))
    return tuple(jnp.asarray(t, F32) for t in tables)


def _even_rows_to_kernel(wt):
    return jnp.concatenate([wt[:1664], wt[1696:], wt[1664:1696]], axis=0)


def _uq_rows_to_kernel(wt):
    r = wt.reshape(B_HEADS, B_NOPE + B_ROPE, -1)
    return jnp.concatenate([r[:, :B_NOPE].reshape(B_HEADS * B_NOPE, -1), r[:, B_NOPE:].reshape(B_HEADS * B_ROPE, -1)])


def _uq_rows_to_reference(wt):
    nope = wt[:B_HEADS * B_NOPE].reshape(B_HEADS, B_NOPE, -1)
    rope = wt[B_HEADS * B_NOPE:].reshape(B_HEADS, B_ROPE, -1)
    return jnp.concatenate([nope, rope], axis=1).reshape(B_HEADS * (B_NOPE + B_ROPE), -1)


def _shard_t(w):
    return jnp.transpose(w[0])


def _unshard_t(wt, like):
    return jnp.transpose(wt)[None].reshape(like.shape)


def kernel(x, c, norm_w, ada_w, ada_b, even_w_in, a_q_norm, a_k_norm, b_q_lora_norm, b_kv_lora_norm, b_w_uq, b_w_uk, b_w_uv, even_w_out, odd_w_in, c_sink, odd_w_out, final_norm, loss_target, m_norm_w, m_ada_w, m_ada_b, m_even_w_in, m_a_q_norm, m_a_k_norm, m_b_q_lora_norm, m_b_kv_lora_norm, m_b_w_uq, m_b_w_uk, m_b_w_uv, m_even_w_out, m_odd_w_in, m_c_sink, m_odd_w_out, m_final_norm, v_norm_w, v_ada_w, v_ada_b, v_even_w_in, v_a_q_norm, v_a_k_norm, v_b_q_lora_norm, v_b_kv_lora_norm, v_b_w_uq, v_b_w_uk, v_b_w_uv, v_even_w_out, v_odd_w_in, v_c_sink, v_odd_w_out, v_final_norm):
    s, d = x.shape[1], x.shape[2]
    x0 = x[0]
    target = loss_target[0]
    me_flat = 4 * lax.axis_index("x") + 2 * lax.axis_index("y") + lax.axis_index("c")

    wcols = ada_w.shape[2]
    bias_cols = lax.dynamic_slice_in_dim(ada_b.reshape(2, N_DEV, wcols), me_flat, 1, axis=1)
    call, modp, (g_in_e, g_uq) = ada_forward(
        jnp.broadcast_to(c, (8, d)), ada_w, bias_cols,
        Gather([_shard_t(even_w_in).astype(MXU), _shard_t(b_w_uq).astype(MXU)]))
    wt_in_e = _even_rows_to_kernel(g_in_e.reshape(-1, d))
    wt_uq = _uq_rows_to_kernel(g_uq.reshape(-1, B_Q_LORA))
    later_exchange = Exchange([_shard_t(odd_w_in).astype(MXU), even_w_out[0].astype(MXU),
                               odd_w_out[0].astype(MXU)], scatter=False)
    uk_bd = (jnp.eye(B_HEADS, dtype=F32)[:, None, :, None] * jnp.transpose(b_w_uk[0], (1, 2, 0))[:, :, None, :]
             ).reshape(B_HEADS * B_NOPE, B_HEADS * B_KV_LORA).astype(MXU)
    head_bd = jnp.asarray(np.kron(np.eye(A_HEADS), np.ones((HD, HD))), MXU)
    gq_full, gk_full = jnp.tile(a_q_norm, (1, A_HEADS)), jnp.tile(a_k_norm, (1, A_KV))
    w_uv = jnp.transpose(b_w_uv[0], (1, 0, 2)).astype(MXU)

    c_all = call[:, 0, :]
    mod = jnp.transpose(modp[:, :, 0, :], (1, 0, 2)).reshape(2, 3, d)
    mod_e, mod_o = mod[0], mod[1]
    nw_e, nw_o = norm_w[0:1], norm_w[1:2]

    cos_a, sin_a, cos_t, sin_t = _rope_tables(s)
    slopes = (2.0 ** (-8.0 * jnp.arange(1, C_HEADS + 1, dtype=F32) / C_HEADS)).reshape(C_HEADS, 1, 1)
    sink2 = c_sink.reshape(C_HEADS, 1, 1) * LOG2E

    (qa, ka, va, qb, kb, kat, vat, kbt, qa_raw, ka_raw, cq_raw, ckv_raw, ga, gb) = even_in_forward(
        x0, mod_e, nw_e, wt_in_e, gq_full, gk_full, b_q_lora_norm, b_kv_lora_norm, wt_uq, uk_bd, head_bd,
        cos_a, sin_a, cos_t, sin_t)
    tk_dense = min(512, s)
    tq_dense = min(256, s)
    fwd_sub = min(8, s // tk_dense)
    bwd_sub_a = min(16, s // tq_dense)
    bwd_sub_b = min(8, s // tq_dense)
    oa, lse_a, g_in_o, g_out_e, g_out_o = flash_forward(
        qa, ka, vat, dv=HD, tq=tq_dense, tk=tk_dense, nsub=fwd_sub, name="attn_a_fwd",
        exchange=later_exchange)
    wt_in_o = g_in_o.reshape(-1, d)
    w_out_e = g_out_e.reshape(-1, d)
    w_out_o = g_out_o.reshape(-1, d)
    o_lat, lse_b = flash_forward(qb, kb, kbt, dv=B_KV_LORA, tq=min(128, s), tk=tk_dense, nsub=fwd_sub,
                                 name="attn_b_fwd")
    ob = latent_out_forward(o_lat, w_uv)
    x1, y_e = mixer_out_forward(x0, mod_e, [(oa, ga), (ob, gb)], w_out_e, "even_out_fwd")

    qc, kc, vc, kct, vct, gc = odd_in_forward(x1, mod_o, nw_o, wt_in_o)
    win_sub = min(8, s // WINDOW)
    oc, lse_c = window_forward(qc, kc, vct, sink2, slopes, win_sub, "attn_c_fwd")
    x2, y_o = mixer_out_forward(x1, mod_o, [(oc, gc)], w_out_o, "odd_out_fwd")

    loss_lanes, dx2, d_final = loss_head(x2, target, final_norm.reshape(1, d))
    loss_part = (0.5 / d) * jnp.sum(loss_lanes)

    doc, dgc, delta_c, dgate_o, dw_out_o, dsink = mixer_out_backward(
        dx2, y_o, mod_o, [(oc, gc)], w_out_o, [C_HEADS], "odd_out_bwd", lse=lse_c.reshape(C_HEADS, s),
        sink=sink2.reshape(C_HEADS, 1))
    rows3 = lambda t: t.reshape(t.shape[0], 1, s)
    dqc, dkc, dvc = window_backward(qc, kc, kct, vc, doc, lse_c, rows3(delta_c), slopes, win_sub, "attn_c_bwd")
    dx1, dvec_o, dwt_in_o = in_proj_backward(
        x1, mod_o, nw_o, [(dqc, O_Q), (dkc, O_K), (dvc, O_V), (dgc, O_G)], "odd_in_bwd",
        dx_out=dx2, w_in_t=wt_in_o, dw_rows=[O_Q, O_K, O_V, O_G])

    doa, dga, dob, dgb, delta_a, dgate_e, dw_out_e = mixer_out_backward(
        dx1, y_e, mod_e, [(oa, ga), (ob, gb)], w_out_e, [A_HEADS, 0], "even_out_bwd")
    d_olat, delta_b, dw_uv = latent_out_backward(dob, o_lat, w_uv)
    blocks = lambda g: g.astype(MXU).reshape(N_DEV, g.shape[0] // N_DEV, g.shape[1])
    even_pieces = lambda: [(pqa, E_QA), (pka, E_KA), (pva, E_VA), (dga, E_GA), (pcq, E_CQ), (pckv, E_CKV),
                           (dgb, E_GB), (pkr, E_KR)]
    scatter_odd = Exchange([blocks(dwt_in_o), blocks(dw_out_o)], True)
    scatter_out_e = Exchange([blocks(dw_out_e)], True)
    dqb, dkb, dvb, l_in_o, l_out_o = flash_backward(
        qb, kb, kbt, None, d_olat, lse_b, rows3(delta_b), scale=SCALE_B, dv=B_KV_LORA,
        tq=tq_dense, tk=tk_dense, nsub=bwd_sub_b, gq=2, name="attn_b_bwd", split=B_KV_LORA, exchange=scatter_odd)
    dqa, dka, dva, l_out_e = flash_backward(
        qa, ka, kat, va, doa, lse_a, rows3(delta_a), scale=SCALE_A, dv=HD,
        tq=tq_dense, tk=tk_dense, nsub=bwd_sub_a, gq=A_KV, name="attn_a_bwd", exchange=scatter_out_e)
    (pqa, pka, pva, pcq, pckv, pkr, g_qn, g_kn, g_qln, g_kvln, dwt_uq, dw_uk) = even_prep_backward(
        dqa, dka, dva, dqb, dkb, dvb, qa_raw, ka_raw, cq_raw, ckv_raw,
        gq_full, gk_full, b_q_lora_norm, b_kv_lora_norm, wt_uq, uk_bd, head_bd, cos_a, sin_a, cos_t, sin_t)
    g_qn = jnp.sum(g_qn.reshape(A_HEADS, HD), axis=0)
    g_kn = jnp.sum(g_kn.reshape(A_KV, HD), axis=0)
    dwt_in_e, l_uk, l_uv = in_proj_backward(
        x0, mod_e, nw_e, even_pieces(), "even_in_bwd_dw",
        dw_rows=[E_QA, E_KA, E_VA, E_GA, E_CQ, E_CKV, (1696, 2208), (1664, 1696)],
        exchange=Exchange([dw_uk.astype(MXU), dw_uv.astype(MXU)], scatter=False))
    dx0, dvec_e, l_in_e, l_uq = in_proj_backward(
        x0, mod_e, nw_e, even_pieces(), "even_in_bwd_dx", dx_out=dx1, w_in_t=wt_in_e,
        exchange=Exchange([blocks(dwt_in_e), blocks(_uq_rows_to_reference(dwt_uq))], True))

    dmod = jnp.stack([jnp.concatenate([dvec_e[0], dvec_e[1], dgate_e[0]]),
                      jnp.concatenate([dvec_o[0], dvec_o[1], dgate_o[0]])])
    d_norm_w = jnp.stack([dvec_e[2], dvec_o[2]])
    small_names = ["norm_w", "ada_b", "a_q_norm", "a_k_norm", "b_q_lora_norm", "b_kv_lora_norm", "b_w_uk", "b_w_uv",
                   "c_sink", "final_norm"]
    small_w = [norm_w, ada_b, a_q_norm, a_k_norm, b_q_lora_norm, b_kv_lora_norm, b_w_uk, b_w_uv, c_sink, final_norm]
    small_m = [m_norm_w, m_ada_b, m_a_q_norm, m_a_k_norm, m_b_q_lora_norm, m_b_kv_lora_norm, m_b_w_uk, m_b_w_uv,
               m_c_sink, m_final_norm]
    small_v = [v_norm_w, v_ada_b, v_a_q_norm, v_a_k_norm, v_b_q_lora_norm, v_b_kv_lora_norm, v_b_w_uk, v_b_w_uv,
               v_c_sink, v_final_norm]
    small_g = [d_norm_w, dmod, g_qn, g_kn, g_qln, g_kvln, None, None, dsink, d_final]
    flat2 = lambda a: a.reshape((1, -1)) if a.size == a.shape[-1] else a.reshape(a.shape[-3:] if a.ndim > 3 else a.shape)
    kshape = [flat2(w).shape for w in small_w]
    late = [i for i, g in enumerate(small_g) if g is not None]
    gathered = all_gather_slots(
        Gather([small_g[i].reshape(kshape[i]) for i in late] + [jnp.full((8, 128), loss_part, F32)]),
        "gather_small_grads")
    g_all = [None] * len(small_g)
    for i, g in zip(late, gathered):
        g_all[i] = g
    g_all[6], g_all[7] = (l.reshape((N_DEV,) + kshape[6]) for l in (l_uk, l_uv))
    sm_out, loss_sum = adamw_small(g_all, [flat2(a) for a in small_w], [flat2(a) for a in small_m],
                                   [flat2(a) for a in small_v], gathered[-1])
    loss = loss_sum[0, 0]
    sm = [{nm: p.reshape(w.shape) for nm, w, p in zip(small_names, small_w, outs)} for outs in sm_out]

    dmod_all = g_all[1].reshape(N_DEV, 2, N_DEV, wcols)
    dmod_cols = lax.dynamic_slice_in_dim(dmod_all, me_flat, 1, axis=2)[:, :, 0, :]
    pad16 = lambda a: jnp.concatenate([a, jnp.zeros_like(a)], axis=0)
    g_ada_w = ada_weight_grad(pad16(c_all), jnp.transpose(pad16(dmod_cols), (1, 0, 2)))
    rows_of = lambda a: a.reshape(-1, wcols)
    ada = adamw_rows(rows_of(g_ada_w)[None], rows_of(ada_w), rows_of(m_ada_w), rows_of(v_ada_w), "adamw_ada_w")
    ada = [p.reshape(ada_w.shape) for p in ada]

    bg = [{}, {}, {}, {}]
    for nm, landed, w, m, v, transposed in (
            ("even_w_in", l_in_e, even_w_in, m_even_w_in, v_even_w_in, True),
            ("b_w_uq", l_uq, b_w_uq, m_b_w_uq, v_b_w_uq, True),
            ("odd_w_in", l_in_o, odd_w_in, m_odd_w_in, v_odd_w_in, True),
            ("even_w_out", l_out_e, even_w_out, m_even_w_out, v_even_w_out, False),
            ("odd_w_out", l_out_o, odd_w_out, m_odd_w_out, v_odd_w_out, False)):
        view = _shard_t if transposed else (lambda a: a[0])
        res = adamw_rows(landed, view(w), view(m), view(v), "adamw_" + nm)
        for kind, p in enumerate(res):
            bg[kind][nm] = _unshard_t(p, w) if transposed else p[None]
    big_names = ["even_w_in", "odd_w_in", "even_w_out", "odd_w_out", "b_w_uq"]

    order = ["norm_w", "ada_w", "ada_b", "even_w_in", "a_q_norm", "a_k_norm", "b_q_lora_norm", "b_kv_lora_norm",
             "b_w_uq", "b_w_uk", "b_w_uv", "even_w_out", "odd_w_in", "c_sink", "odd_w_out", "final_norm"]

    def pick(kind):
        out = []
        for nm in order:
            if nm == "ada_w":
                out.append(ada[kind])
            elif nm in big_names:
                out.append(bg[kind][nm])
            else:
                out.append(sm[kind][nm])
        return out

    return (loss, dx0[None], *pick(0), *pick(1), *pick(2), *pick(3))
```

```python
import functools

import jax
import jax.numpy as jnp
import numpy as np
from jax import lax
from jax.experimental import pallas as pl
from jax.experimental.pallas import tpu as pltpu

F32 = jnp.float32
MXU = jnp.bfloat16
EPS = 1e-6
ROPE_THETA = 10000.0
GRID_W = 64
HD = 64
N_DEV = 8

A_HEADS, A_KV = 8, 2
B_HEADS, B_NOPE, B_ROPE, B_Q_LORA, B_KV_LORA = 8, 64, 32, 256, 128
B_QK = B_KV_LORA + B_ROPE
C_HEADS, C_KV = 16, 4
WINDOW = 128

ADAM_LR, ADAM_B1, ADAM_B2, ADAM_EPS, ADAM_WD, ADAM_STEP = 0.001, 0.9, 0.999, 1e-08, 0.01, 10

ROW_TILE = 512
ADAM_TILE = 2048 * 128

LOG2E = 1.4426950408889634
SCALE_A = HD ** -0.5
SCALE_B = (B_NOPE + B_ROPE) ** -0.5
SCALE2_A, SCALE2_B = SCALE_A * LOG2E, SCALE_B * LOG2E
VMEM_LIMIT = 56 * 1024 * 1024

E_QA, E_KA, E_VA, E_GA, E_CQ, E_CKV, E_GB, E_KR = (
    (0, 512), (512, 640), (640, 768), (768, 1280), (1280, 1536), (1536, 1664), (1664, 2176), (2176, 2208))
O_Q, O_K, O_V, O_G = (0, 1024), (1024, 1280), (1280, 1536), (1536, 2560)


def _mm(a, b):
    return jnp.dot(a.astype(MXU), b.astype(MXU), preferred_element_type=F32)


def _mm_nt(a, b):
    return lax.dot_general(a.astype(MXU), b.astype(MXU), (((1,), (1,)), ((), ())), preferred_element_type=F32)


def _mm_tn(a, b):
    return lax.dot_general(a.astype(MXU), b.astype(MXU), (((0,), (0,)), ((), ())), preferred_element_type=F32)


def _group_sums_t(prod, group):
    tm, w = prod.shape
    sel = (lax.broadcasted_iota(jnp.int32, (w, 128), 0) // group
           == lax.broadcasted_iota(jnp.int32, (w, 128), 1)).astype(MXU)
    hi = prod.astype(MXU)
    lo = prod - hi.astype(F32)
    return (_mm(hi, sel) + _mm(lo, sel)).T


def _sigmoid(z):
    return 1.0 / (1.0 + jnp.exp(-z))


def _silu(z):
    return z * _sigmoid(z)


def _rms(x):
    return lax.rsqrt(jnp.mean(x * x, axis=-1, keepdims=True) + EPS)


def _swap_halves(y, group):
    n = y.shape[-1]
    half = group // 2
    fwd = pltpu.roll(y, half, 1)
    if n == group:
        return fwd
    back = pltpu.roll(y, n - half, 1)
    lane = lax.broadcasted_iota(jnp.int32, y.shape, 1)
    return jnp.where((lane % group) < half, back, fwd)


def _rope(y, cos, sin, group):
    return y * cos + _swap_halves(y, group) * sin


def _rope_t(d, cos, sin, group):
    return d * cos - _swap_halves(d, group) * sin


def _rms_bwd(dy, x, g):
    r = _rms(x)
    xhat = x * r
    dxhat = dy * g
    dx = r * (dxhat - xhat * jnp.mean(dxhat * xhat, axis=-1, keepdims=True))
    return dx, dy * xhat


def _group_mean(v, bd, group):
    hi = v.astype(MXU)
    lo = v - hi.astype(F32)
    return (_mm(hi, bd[...]) + _mm(lo, bd[...])) * (1.0 / group)


def _head_norm(x, g, bd, group):
    return x * lax.rsqrt(_group_mean(x * x, bd, group) + EPS) * g


def _head_norm_bwd(dy, x, g, bd, group):
    r = lax.rsqrt(_group_mean(x * x, bd, group) + EPS)
    xhat = x * r
    dxhat = dy * g
    dx = r * (dxhat - xhat * _group_mean(dxhat * xhat, bd, group))
    return dx, dy * xhat


def _params(sem, vmem=VMEM_LIMIT):
    return pltpu.CompilerParams(dimension_semantics=sem, vmem_limit_bytes=vmem)


def _row_spec(tm, w):
    return pl.BlockSpec((tm, w), lambda i: (i, 0))


def _full_spec(shape):
    nd = len(shape)
    return pl.BlockSpec(shape, lambda i: (0,) * nd)


def _head_spec(h, tm, w):
    return pl.BlockSpec((h, tm, w), lambda i: (0, i, 0))


def _headt_spec(h, w, tm):
    return pl.BlockSpec((h, w, tm), lambda i: (0, 0, i))


def _rows_spec(h, tm):
    return pl.BlockSpec((h, tm), lambda i: (0, i))


def _me():
    return lax.axis_index("x"), lax.axis_index("y"), lax.axis_index("c")


def _flat(p):
    return 4 * p[0] + 2 * p[1] + p[2]


def _peer(me, k):
    x, y, c = me
    return (1 - x if k & 4 else x, 1 - y if k & 2 else y, 1 - c if k & 1 else c)


MESH_ID = pl.DeviceIdType.MESH


class Gather:
    VMEM = pl.BlockSpec(memory_space=pltpu.VMEM)

    def __init__(self, shards):
        self.shards = list(shards)
        self.n = len(self.shards)
        self.out_shapes = tuple(jax.ShapeDtypeStruct((N_DEV,) + a.shape, a.dtype) for a in self.shards)
        self.in_specs = [Gather.VMEM] * self.n
        self.out_specs = (Gather.VMEM,) * self.n
        self.sems = [pltpu.SemaphoreType.DMA((7 * self.n,)), pltpu.SemaphoreType.DMA((7 * self.n,)),
                     pltpu.SemaphoreType.DMA((self.n,))]

    def _plan(self, x_refs, out_refs, sems):
        send_sems, recv_sems, local_sems = sems
        me = _me()
        x, y, c = me
        chips = [(1 - x, y), (x, 1 - y), (1 - x, 1 - y)]

        def copy(a, k, block, to, src=None):
            slot = out_refs[a].at[_flat(block)]
            return pltpu.make_async_remote_copy(
                src_ref=slot if src is None else src, dst_ref=slot, send_sem=send_sems.at[7 * a + k],
                recv_sem=recv_sems.at[7 * a + k], device_id=to, device_id_type=MESH_ID)

        mine = [pltpu.make_async_copy(x_refs[a], out_refs[a].at[_flat(me)], local_sems.at[a]) for a in range(self.n)]
        first = [copy(a, 0, me, (x, y, 1 - c), src=x_refs[a]) for a in range(self.n)]
        first += [copy(a, 1 + j, me, (*chip, c), src=x_refs[a]) for a in range(self.n) for j, chip in enumerate(chips)]
        return me, chips, copy, mine, first

    def start(self, x_refs, out_refs, sems):
        _, _, _, mine, first = self._plan(x_refs, out_refs, sems)
        for cp in mine + first:
            cp.start()

    def forward(self, x_refs, out_refs, sems):
        me, chips, copy, _, _ = self._plan(x_refs, out_refs, sems)
        x, y, c = me
        for a in range(self.n):
            for j, chip in enumerate(chips):
                copy(a, 1 + j, (*chip, c), me).wait_recv()
                copy(a, 4 + j, (*chip, c), (x, y, 1 - c)).start()

    def drain(self, x_refs, out_refs, sems):
        me, chips, copy, mine, first = self._plan(x_refs, out_refs, sems)
        x, y, c = me
        sibling = (x, y, 1 - c)
        for a in range(self.n):
            copy(a, 0, sibling, me).wait_recv()
            for j, chip in enumerate(chips):
                copy(a, 4 + j, (*chip, 1 - c), me).wait_recv()
        for cp in first + [copy(a, 4 + j, (*chip, c), sibling) for a in range(self.n) for j, chip in enumerate(chips)]:
            cp.wait_send()
        for cp in mine:
            cp.wait()

    def finish(self, x_refs, out_refs, sems):
        self.forward(x_refs, out_refs, sems)
        self.drain(x_refs, out_refs, sems)


def all_gather_slots(gather, name):
    def body(*refs):
        x_refs, out_refs, sems = refs[:gather.n], refs[gather.n:2 * gather.n], refs[2 * gather.n:]
        gather.start(x_refs, out_refs, sems)
        gather.finish(x_refs, out_refs, sems)

    return pl.pallas_call(
        body, name=name, out_shape=gather.out_shapes, in_specs=gather.in_specs, out_specs=gather.out_specs,
        scratch_shapes=list(gather.sems), compiler_params=pltpu.CompilerParams(vmem_limit_bytes=VMEM_LIMIT),
    )(*gather.shards)


class Exchange:
    HBM = pl.BlockSpec(memory_space=pl.ANY)

    def __init__(self, srcs, scatter):
        self.srcs = list(srcs)
        self.scatter = scatter
        self.n = len(self.srcs)
        self.land_shapes = tuple(jax.ShapeDtypeStruct((N_DEV,) + tuple(a.shape[-2:]), a.dtype) for a in self.srcs)
        self.in_specs = [Exchange.HBM] * self.n
        self.out_specs = (Exchange.HBM,) * self.n
        self.sems = [pltpu.SemaphoreType.DMA((N_DEV - 1,)), pltpu.SemaphoreType.DMA((N_DEV - 1,)),
                     pltpu.SemaphoreType.DMA] * self.n

    def _copies(self, src_refs, land_refs, sems):
        me = _me()
        mi = _flat(me)
        local, sends, recvs = [], [], []
        for a, (src_ref, land_ref) in enumerate(zip(src_refs, land_refs)):
            send_sems, recv_sems, local_sem = sems[3 * a:3 * a + 3]
            pick = (lambda p, r=src_ref: r.at[_flat(p)]) if self.scatter else (lambda p, r=src_ref: r)
            local.append(pltpu.make_async_copy(pick(me), land_ref.at[mi], local_sem))
            for k in range(1, N_DEV):
                peer = _peer(me, k)
                pair = dict(send_sem=send_sems.at[k - 1], recv_sem=recv_sems.at[k - 1], device_id=peer,
                            device_id_type=MESH_ID)
                sends.append(pltpu.make_async_remote_copy(src_ref=pick(peer), dst_ref=land_ref.at[mi], **pair))
                recvs.append(pltpu.make_async_remote_copy(src_ref=pick(peer), dst_ref=land_ref.at[_flat(peer)],
                                                          **pair))
        return local, sends, recvs

    def start(self, src_refs, land_refs, sems):
        local, sends, _ = self._copies(src_refs, land_refs, sems)
        for cp in local + sends:
            cp.start()

    def wait(self, src_refs, land_refs, sems):
        local, sends, recvs = self._copies(src_refs, land_refs, sems)
        for cp in recvs:
            cp.wait_recv()
        for cp in sends:
            cp.wait_send()
        for cp in local:
            cp.wait()


def ada_forward(c8, ada_w, bias_cols, gather):
    d = c8.shape[1]
    w = ada_w.shape[2]
    ng = gather.n

    def body(*refs):
        c_ref, w_ref, b_ref = refs[:3]
        gx_refs = refs[3:3 + ng]
        call_ref, modp_ref = refs[3 + ng:5 + ng]
        gout_refs = refs[5 + ng:5 + 2 * ng]
        part_ref, s1, r1, s2, r2 = refs[5 + 2 * ng:10 + 2 * ng]
        g_sems = refs[10 + 2 * ng:]
        me = _me()
        mi = _flat(me)
        call_ref[mi] = c_ref[...]
        rows_out = []
        for k in range(1, N_DEV):
            rows_out.append(pltpu.make_async_remote_copy(
                src_ref=c_ref, dst_ref=call_ref.at[mi], send_sem=s1.at[k - 1], recv_sem=r1.at[k - 1],
                device_id=_peer(me, k), device_id_type=MESH_ID))
        for cp in rows_out:
            cp.start()
        gather.start(gx_refs, gout_refs, g_sems)
        for k in range(1, N_DEV):
            pltpu.make_async_remote_copy(
                src_ref=c_ref, dst_ref=call_ref.at[_flat(_peer(me, k))], send_sem=s1.at[k - 1],
                recv_sem=r1.at[k - 1], device_id=_peer(me, k), device_id_type=MESH_ID).wait_recv()
        ca = _silu(call_ref[...].reshape(N_DEV * 8, d))
        for l in range(2):
            part = _mm(ca, w_ref[l]) + b_ref[l]
            for b in range(N_DEV):
                part_ref[b, l] = part[8 * b:8 * b + 8, :]
        modp_ref[mi] = part_ref[mi]
        spread = []
        for k in range(1, N_DEV):
            peer = _peer(me, k)
            spread.append(pltpu.make_async_remote_copy(
                src_ref=part_ref.at[_flat(peer)], dst_ref=modp_ref.at[mi], send_sem=s2.at[k - 1],
                recv_sem=r2.at[k - 1], device_id=peer, device_id_type=MESH_ID))
        for cp in spread:
            cp.start()
        gather.forward(gx_refs, gout_refs, g_sems)
        for k in range(1, N_DEV):
            pi = _flat(_peer(me, k))
            pltpu.make_async_remote_copy(
                src_ref=part_ref.at[pi], dst_ref=modp_ref.at[pi], send_sem=s2.at[k - 1],
                recv_sem=r2.at[k - 1], device_id=_peer(me, k), device_id_type=MESH_ID).wait_recv()
        for cp in rows_out + spread:
            cp.wait_send()
        gather.drain(gx_refs, gout_refs, g_sems)

    vm = pl.BlockSpec(memory_space=pltpu.VMEM)
    res = pl.pallas_call(
        body, name="ada_forward",
        out_shape=(jax.ShapeDtypeStruct((N_DEV, 8, d), F32), jax.ShapeDtypeStruct((N_DEV, 2, 8, w), F32))
        + gather.out_shapes,
        in_specs=[vm, vm, vm] + gather.in_specs, out_specs=(vm, vm) + gather.out_specs,
        scratch_shapes=[pltpu.VMEM((N_DEV, 2, 8, w), F32)] + [pltpu.SemaphoreType.DMA((7,))] * 4 + list(gather.sems),
        compiler_params=pltpu.CompilerParams(vmem_limit_bytes=VMEM_LIMIT),
    )(c8, ada_w, bias_cols, *gather.shards)
    return res[0], res[1], res[2:]


def _modulated(x, mod_ref, nw_ref):
    xn = x * _rms(x)
    g1 = nw_ref[...] * (1.0 + mod_ref[1:2, :])
    return xn, g1, xn * g1 + mod_ref[0:1, :]


def even_in_forward(x, mod, nw, w_in_t, gq, gk, qln, kvln, w_uq_t, uk_bd, bd, cos_a, sin_a, cos_t, sin_t):
    s, d = x.shape
    tm = min(ROW_TILE, s)
    n_nope = B_HEADS * B_NOPE

    def body(x_ref, mod_ref, nw_ref, w_ref, gq_ref, gk_ref, qln_ref, kvln_ref, uq_ref, ukbd_ref, bd_ref,
             ca_ref, sa_ref, ct_ref, st_ref,
             qa_o, ka_o, va_o, qb_o, kb_o, kat_o, vat_o, kbt_o, qa_raw_o, ka_raw_o, cq_raw_o, ckv_raw_o, ga_o, gb_o):
        _, _, h = _modulated(x_ref[...], mod_ref, nw_ref)
        h = h.astype(MXU)

        def proj(cols):
            return _mm_nt(h, w_ref[cols[0]:cols[1], :])

        ca, sa, ct, st = ca_ref[...], sa_ref[...], ct_ref[...], st_ref[...]
        wide = lambda t, n: jnp.concatenate([t] * n, axis=1)
        qa = proj(E_QA)
        qa_raw_o[...] = qa
        qr = _rope(_head_norm(qa, gq_ref[...], bd_ref, HD), wide(ca, 4), wide(sa, 4), 32) * SCALE2_A
        for hh in range(A_HEADS):
            qa_o[hh] = qr[:, HD * hh:HD * hh + HD].astype(MXU)
        ka = proj(E_KA)
        ka_raw_o[...] = ka
        kr = _rope(_head_norm(ka, gk_ref[...], bd_ref[0:128, 0:128], HD), ca, sa, 32)
        va = proj(E_VA)
        krt, vat = kr.T, va.T
        for g in range(A_KV):
            ka_o[g] = kr[:, HD * g:HD * g + HD].astype(MXU)
            va_o[g] = va[:, HD * g:HD * g + HD].astype(MXU)
            kat_o[g] = krt[HD * g:HD * g + HD, :].astype(MXU)
            vat_o[g] = vat[HD * g:HD * g + HD, :].astype(MXU)
        ga_o[...] = proj(E_GA).astype(MXU)
        gb_o[...] = proj(E_GB).astype(MXU)
        cq = proj(E_CQ)
        cq_raw_o[...] = cq
        qb = _mm_nt(cq * _rms(cq) * qln_ref[...], uq_ref[...])
        q_lat = _mm(qb[:, 0:n_nope], ukbd_ref[...]) * SCALE2_B
        q_rope = _rope(qb[:, n_nope:], wide(ct, 2), wide(st, 2), 32) * SCALE2_B
        for hh in range(B_HEADS):
            qb_o[hh, :, 0:B_KV_LORA] = q_lat[:, B_KV_LORA * hh:B_KV_LORA * (hh + 1)].astype(MXU)
            qb_o[hh, :, B_KV_LORA:B_QK] = q_rope[:, B_ROPE * hh:B_ROPE * (hh + 1)].astype(MXU)
        ckv = proj(E_CKV)
        ckv_raw_o[...] = ckv
        ckv_n = ckv * _rms(ckv) * kvln_ref[...]
        k_rope = _rope(proj(E_KR), ct[:, 0:B_ROPE], st[:, 0:B_ROPE], 32)
        kb_o[0, :, 0:B_KV_LORA] = ckv_n.astype(MXU)
        kb_o[0, :, B_KV_LORA:B_QK] = k_rope.astype(MXU)
        kbt_o[0, 0:B_KV_LORA, :] = ckv_n.T.astype(MXU)
        kbt_o[0, B_KV_LORA:B_QK, :] = k_rope.T.astype(MXU)

    sd = jax.ShapeDtypeStruct
    outs = (sd((A_HEADS, s, HD), MXU), sd((A_KV, s, HD), MXU), sd((A_KV, s, HD), MXU),
            sd((B_HEADS, s, B_QK), MXU), sd((1, s, B_QK), MXU),
            sd((A_KV, HD, s), MXU), sd((A_KV, HD, s), MXU), sd((1, B_QK, s), MXU),
            sd((s, 512), F32), sd((s, 128), F32), sd((s, B_Q_LORA), F32), sd((s, B_KV_LORA), F32),
            sd((s, 512), MXU), sd((s, 512), MXU))
    out_specs = (_head_spec(A_HEADS, tm, HD), _head_spec(A_KV, tm, HD), _head_spec(A_KV, tm, HD),
                 _head_spec(B_HEADS, tm, B_QK), _head_spec(1, tm, B_QK),
                 _headt_spec(A_KV, HD, tm), _headt_spec(A_KV, HD, tm), _headt_spec(1, B_QK, tm),
                 _row_spec(tm, 512), _row_spec(tm, 128), _row_spec(tm, B_Q_LORA), _row_spec(tm, B_KV_LORA),
                 _row_spec(tm, 512), _row_spec(tm, 512))
    consts = [mod, nw, w_in_t, gq, gk, qln, kvln, w_uq_t, uk_bd, bd]
    return pl.pallas_call(
        body, name="even_in_forward", grid=(s // tm,), out_shape=outs,
        in_specs=[_row_spec(tm, d)] + [_full_spec(a.shape) for a in consts] + [_row_spec(tm, 128)] * 4,
        out_specs=out_specs, compiler_params=_params(("parallel",)),
    )(x, *consts, cos_a, sin_a, cos_t, sin_t)


def odd_in_forward(x, mod, nw, w_in):
    s, d = x.shape
    tm = min(ROW_TILE, s)

    def body(x_ref, mod_ref, nw_ref, w_ref, q_o, k_o, v_o, kt_o, vt_o, g_o):
        _, _, h = _modulated(x_ref[...], mod_ref, nw_ref)
        h = h.astype(MXU)

        def proj(cols):
            return _mm_nt(h, w_ref[cols[0]:cols[1], :])

        q = proj(O_Q) * SCALE2_A
        for hh in range(C_HEADS):
            q_o[hh] = q[:, HD * hh:HD * hh + HD].astype(MXU)
        k = proj(O_K)
        v = proj(O_V)
        for g in range(C_KV):
            kh = k[:, HD * g:HD * g + HD]
            vh = v[:, HD * g:HD * g + HD]
            k_o[g] = kh.astype(MXU)
            v_o[g] = vh.astype(MXU)
            kt_o[g] = kh.T.astype(MXU)
            vt_o[g] = vh.T.astype(MXU)
        g_o[...] = proj(O_G).astype(MXU)

    sd = jax.ShapeDtypeStruct
    return pl.pallas_call(
        body, name="odd_in_forward", grid=(s // tm,),
        out_shape=(sd((C_HEADS, s, HD), MXU), sd((C_KV, s, HD), MXU), sd((C_KV, s, HD), MXU),
                   sd((C_KV, HD, s), MXU), sd((C_KV, HD, s), MXU), sd((s, 1024), MXU)),
        in_specs=[_row_spec(tm, d), _full_spec(mod.shape), _full_spec(nw.shape), _full_spec(w_in.shape)],
        out_specs=(_head_spec(C_HEADS, tm, HD), _head_spec(C_KV, tm, HD), _head_spec(C_KV, tm, HD),
                   _headt_spec(C_KV, HD, tm), _headt_spec(C_KV, HD, tm), _row_spec(tm, 1024)),
        compiler_params=_params(("parallel",)),
    )(x, mod, nw, w_in)


def latent_out_forward(o_lat, w_uv):
    s = o_lat.shape[0]
    tm = min(ROW_TILE, s)

    def body(o_ref, uv_ref, out_ref):
        for hh in range(B_HEADS):
            out_ref[:, HD * hh:HD * hh + HD] = _mm(o_ref[:, B_KV_LORA * hh:B_KV_LORA * (hh + 1)],
                                                   uv_ref[hh]).astype(MXU)

    return pl.pallas_call(
        body, name="latent_out_forward", grid=(s // tm,),
        out_shape=jax.ShapeDtypeStruct((s, B_HEADS * HD), MXU),
        in_specs=[_row_spec(tm, o_lat.shape[1]), _full_spec(w_uv.shape)],
        out_specs=_row_spec(tm, B_HEADS * HD),
        compiler_params=_params(("parallel",)),
    )(o_lat, w_uv)


def mixer_out_forward(x, mod, pairs, w_out, name):
    s, d = x.shape
    tm = min(ROW_TILE, s)
    n = len(pairs)
    widths = [o.shape[1] for o, _ in pairs]

    def body(*refs):
        x_ref, mod_ref, w_ref = refs[:3]
        pr = refs[3:3 + 2 * n]
        xo_ref, y_ref = refs[3 + 2 * n:]
        y = jnp.zeros((tm, d), F32)
        r0 = 0
        for i in range(n):
            mix = pr[2 * i][...].astype(F32) * _silu(pr[2 * i + 1][...].astype(F32))
            y = y + _mm(mix, w_ref[r0:r0 + widths[i], :])
            r0 += widths[i]
        y_ref[...] = y.astype(y_ref.dtype)
        xo_ref[...] = x_ref[...] + mod_ref[2:3, :] * y

    flat = [a for p in pairs for a in p]
    sd = jax.ShapeDtypeStruct
    return pl.pallas_call(
        body, name=name, grid=(s // tm,),
        out_shape=(sd((s, d), F32), sd((s, d), MXU)),
        in_specs=[_row_spec(tm, d), _full_spec(mod.shape), _full_spec(w_out.shape)]
        + [_row_spec(tm, a.shape[1]) for a in flat],
        out_specs=(_row_spec(tm, d), _row_spec(tm, d)),
        compiler_params=_params(("parallel",)),
    )(x, mod, w_out, *flat)


ONES_ROWS = 16
AHEAD = 2


def _col_max8(s3):
    m8 = jnp.max(s3, axis=0)
    return jnp.broadcast_to(jnp.max(m8, axis=0, keepdims=True), m8.shape)


def _with_ones(vt, n):
    return jnp.concatenate([vt, jnp.ones((ONES_ROWS, n), vt.dtype)], axis=0)


def _grid_edges(grid):
    ids = [pl.program_id(a) for a in range(len(grid))]
    first = functools.reduce(jnp.logical_and, [i == 0 for i in ids])
    last = functools.reduce(jnp.logical_and, [i == n - 1 for i, n in zip(ids, grid)])
    return first, last


def flash_forward(q, k, vt, *, dv, tq, tk, nsub, name, exchange=None):
    hq, s, dq = q.shape
    g_kv = k.shape[0]
    hpg = hq // g_kv
    nq = s // tq
    tkk = tk * nsub
    nk = s // tkk
    grid = (g_kv, nq, nk)
    hosted = exchange is not None
    m_cols = hpg * tq
    dvp = dv + ONES_ROWS

    def body(*refs):
        nx = exchange.n if hosted else 0
        q_ref, k_ref, vt_ref = refs[:3]
        xs_refs = refs[3:3 + nx]
        o_ref, lse_ref = refs[3 + nx:5 + nx]
        land_refs = refs[5 + nx:5 + 2 * nx]
        m_s, acc_s = refs[5 + 2 * nx:7 + 2 * nx]
        sems = refs[7 + 2 * nx:]
        if hosted:
            first, last = _grid_edges(grid)
            pl.when(first)(lambda: exchange.start(xs_refs, land_refs, sems))
        j = pl.program_id(2)

        @pl.when(j == 0)
        def _():
            m_s[...] = jnp.full((8, m_cols), -jnp.inf, F32)
            acc_s[...] = jnp.zeros((dvp, m_cols), F32)

        qq = q_ref[...].reshape(m_cols, dq)
        score = lambda u: _mm_nt(k_ref[0, tk * u:tk * (u + 1), :], qq).reshape(tk // 8, 8, m_cols)
        sts = {u: score(u) for u in range(min(AHEAD, nsub))}
        m_run = m_s[...]
        acc = acc_s[...]
        for u in range(nsub):
            if u + AHEAD < nsub:
                sts[u + AHEAD] = score(u + AHEAD)
            st = sts.pop(u)
            m_new = jnp.maximum(m_run, _col_max8(st))
            p = jnp.exp2(st - m_new[None])
            alpha = jnp.exp2(m_run - m_new)
            pv = _mm(_with_ones(vt_ref[0, 0:dv, tk * u:tk * (u + 1)], tk), p.reshape(tk, m_cols))
            acc = (acc.reshape(dvp // 8, 8, m_cols) * alpha[None]).reshape(dvp, m_cols) + pv
            m_run = m_new
        acc_s[...] = acc
        m_s[...] = m_run

        @pl.when(j == nk - 1)
        def _():
            l = acc_s[dv:dv + 1, :]
            ot = acc_s[0:dv, :] / l
            lse = m_s[0:1, :] + jnp.log2(l)
            for hh in range(hpg):
                o_ref[:, dv * hh:dv * hh + dv] = ot[:, tq * hh:tq * hh + tq].T.astype(MXU)
                lse_ref[hh] = lse[:, tq * hh:tq * hh + tq]

        if hosted:
            pl.when(last)(lambda: exchange.wait(xs_refs, land_refs, sems))

    sd = jax.ShapeDtypeStruct
    return pl.pallas_call(
        body, name=name, grid=grid,
        out_shape=(sd((s, hq * dv), MXU), sd((hq, 1, s), F32)) + (exchange.land_shapes if hosted else ()),
        in_specs=[pl.BlockSpec((hpg, tq, dq), lambda g, i, j: (g, i, 0)),
                  pl.BlockSpec((1, tkk, k.shape[2]), lambda g, i, j: (g, j, 0)),
                  pl.BlockSpec((1, dv, tkk), lambda g, i, j: (g, 0, j))] + (exchange.in_specs if hosted else []),
        out_specs=(pl.BlockSpec((tq, hpg * dv), lambda g, i, j: (i, g)),
                   pl.BlockSpec((hpg, 1, tq), lambda g, i, j: (g, 0, i))) + (exchange.out_specs if hosted else ()),
        scratch_shapes=[pltpu.VMEM((8, m_cols), F32), pltpu.VMEM((dvp, m_cols), F32)]
        + (list(exchange.sems) if hosted else []),
        compiler_params=_params(("arbitrary",) * 3 if hosted else ("parallel", "parallel", "arbitrary")),
    )(q, k, vt, *(exchange.srcs if hosted else []))


def _window_bias_t(hpg, slope_ref):
    t = WINDOW
    r = lax.broadcasted_iota(jnp.int32, (3 * t, t), 0)
    cq = lax.broadcasted_iota(jnp.int32, (3 * t, t), 1)
    arel = jnp.abs(r - t - cq)
    base = jnp.where(arel <= WINDOW, arel.astype(F32) * (-LOG2E), -jnp.inf)
    return jnp.concatenate([base * slope_ref[hh] for hh in range(hpg)], axis=1)


def _window_edges_t(bias, no_before, no_after):
    t = WINDOW
    r = lax.broadcasted_iota(jnp.int32, bias.shape, 0)
    out = ((r < t) & no_before) | ((r >= 2 * t) & no_after)
    return jnp.where(out, -jnp.inf, bias)


def _window_specs(kind, nb, nblk, d):
    t = WINDOW
    before = lambda i: jnp.clip(i * nb - 1, 0, nblk - 1)
    after = lambda i: jnp.clip((i + 1) * nb, 0, nblk - 1)
    if kind == "rows":
        return [pl.BlockSpec((1, t, d), lambda g, i: (g, before(i), 0)),
                pl.BlockSpec((1, nb * t, d), lambda g, i: (g, i, 0)),
                pl.BlockSpec((1, t, d), lambda g, i: (g, after(i), 0))]
    return [pl.BlockSpec((1, d, t), lambda g, i: (g, 0, before(i))),
            pl.BlockSpec((1, d, nb * t), lambda g, i: (g, 0, i)),
            pl.BlockSpec((1, d, t), lambda g, i: (g, 0, after(i)))]


def window_forward(q, k, vt, sink2, slopes, nb, name):
    hq, s, d = q.shape
    g_kv = k.shape[0]
    hpg = hq // g_kv
    t = WINDOW
    nblk = s // t
    steps = nblk // nb
    m_cols = hpg * t

    def body(q_ref, kp, ko, kn, vp, vo, vn, sink_ref, slope_ref, o_ref, lse_ref):
        i = pl.program_id(1)
        kk_all = jnp.concatenate([kp[0], ko[0], kn[0]], axis=0)
        vt_all = jnp.concatenate([vp[0], vo[0], vn[0]], axis=1)
        bias = _window_bias_t(hpg, slope_ref)
        sink_row = jnp.concatenate([jnp.broadcast_to(sink_ref[hh], (8, t)) for hh in range(hpg)], axis=1)
        sts = {}

        def score(u):
            qq = q_ref[:, t * u:t * (u + 1), :].reshape(m_cols, d)
            b_u = bias
            if u == 0 or u == nb - 1:
                b_u = _window_edges_t(bias, (i == 0) if u == 0 else False,
                                      (i == steps - 1) if u == nb - 1 else False)
            sts[u] = _mm_nt(kk_all[t * u:t * (u + 3), :], qq) + b_u

        for u in range(min(AHEAD, nb)):
            score(u)
        for u in range(nb):
            if u + AHEAD < nb:
                score(u + AHEAD)
            s3 = sts.pop(u).reshape(3 * t // 8, 8, m_cols)
            m8 = jnp.maximum(_col_max8(s3), sink_row)
            p = jnp.exp2(s3 - m8[None]).reshape(3 * t, m_cols)
            acc = _mm(_with_ones(vt_all[:, t * u:t * (u + 3)], 3 * t), p)
            l = acc[d:d + 1, :] + jnp.exp2(sink_row[0:1, :] - m8[0:1, :])
            ot = acc[0:d, :] / l
            lse = m8[0:1, :] + jnp.log2(l)
            for hh in range(hpg):
                o_ref[t * u:t * (u + 1), d * hh:d * hh + d] = ot[:, t * hh:t * hh + t].T.astype(MXU)
                lse_ref[hh, :, t * u:t * (u + 1)] = lse[:, t * hh:t * hh + t]

    sd = jax.ShapeDtypeStruct
    return pl.pallas_call(
        body, name=name, grid=(g_kv, steps),
        out_shape=(sd((s, hq * d), MXU), sd((hq, 1, s), F32)),
        in_specs=[pl.BlockSpec((hpg, nb * t, d), lambda g, i: (g, i, 0))]
        + _window_specs("rows", nb, nblk, d) + _window_specs("cols", nb, nblk, d)
        + [pl.BlockSpec((hpg, 1, 1), lambda g, i: (g, 0, 0))] * 2,
        out_specs=(pl.BlockSpec((nb * t, hpg * d), lambda g, i: (i, g)),
                   pl.BlockSpec((hpg, 1, nb * t), lambda g, i: (g, 0, i))),
        compiler_params=_params(("parallel", "parallel")),
    )(q, k, k, k, vt, vt, vt, sink2, slopes)


def window_backward(q, k, kt, v, do, lse, delta, slopes, nb, name):
    hq, s, d = q.shape
    g_kv = k.shape[0]
    hpg = hq // g_kv
    t = WINDOW
    nblk = s // t
    steps = nblk // nb
    m_cols = hpg * t

    def body(q_ref, kp, ko, kn, ktp, kto, ktn, vp, vo, vn, do_ref, lse_ref, dl_ref, slope_ref,
             dq_ref, dk_ref, dv_ref, dk_s, dv_s):
        i = pl.program_id(1)

        @pl.when(i == 0)
        def _():
            dk_ref[...] = jnp.zeros(dk_ref.shape, F32)
            dv_ref[...] = jnp.zeros(dv_ref.shape, F32)

        dk_s[...] = jnp.zeros(dk_s.shape, F32)
        dv_s[...] = jnp.zeros(dv_s.shape, F32)
        kk_all = jnp.concatenate([kp[0], ko[0], kn[0]], axis=0)
        vv_all = jnp.concatenate([vp[0], vo[0], vn[0]], axis=0)
        kkt_all = jnp.concatenate([ktp[0], kto[0], ktn[0]], axis=1)
        bias = _window_bias_t(hpg, slope_ref)
        qqs, dds, sts, dps = {}, {}, {}, {}

        def issue(u):
            rows = slice(t * u, t * (u + 1))
            keys = slice(t * u, t * (u + 3))
            qqs[u] = q_ref[:, rows, :].reshape(m_cols, d)
            dds[u] = jnp.concatenate([do_ref[rows, d * hh:d * hh + d] for hh in range(hpg)], axis=0)
            b_u = bias
            if u == 0 or u == nb - 1:
                b_u = _window_edges_t(bias, (i == 0) if u == 0 else False,
                                      (i == steps - 1) if u == nb - 1 else False)
            sts[u] = _mm_nt(kk_all[keys, :], qqs[u]) + b_u
            dps[u] = _mm_nt(vv_all[keys, :], dds[u])

        for u in range(min(AHEAD, nb)):
            issue(u)
        for u in range(nb):
            if u + AHEAD < nb:
                issue(u + AHEAD)
            rows = slice(t * u, t * (u + 1))
            keys = slice(t * u, t * (u + 3))
            lse_row = jnp.concatenate([lse_ref[hh, :, rows] for hh in range(hpg)], axis=1)
            dl_row = jnp.concatenate([dl_ref[hh, :, rows] for hh in range(hpg)], axis=1)
            p = jnp.exp2(sts[u] - lse_row)
            ds = p * (dps[u] - dl_row) * SCALE_A
            dv_s[keys, :] += _mm(p, dds[u])
            dk_s[keys, :] += _mm(ds, qqs[u])
            dqt = _mm(kkt_all[:, keys], ds)
            for hh in range(hpg):
                dq_ref[rows, d * hh:d * hh + d] = dqt[:, t * hh:t * hh + t].T.astype(dq_ref.dtype)
        tq = nb * t
        for src, r0, n in ((0, jnp.clip(i * nb - 1, 0, nblk - 1) * t, t), (t, i * tq, tq),
                           (t + tq, jnp.clip((i + 1) * nb, 0, nblk - 1) * t, t)):
            dst = pl.ds(pl.multiple_of(r0, t), n)
            dk_ref[0, dst, :] += dk_s[src:src + n, :] * (1.0 / SCALE2_A)
            dv_ref[0, dst, :] += dv_s[src:src + n, :]

    row_map = lambda g, i: (g, 0, i)
    sd = jax.ShapeDtypeStruct
    return pl.pallas_call(
        body, name=name, grid=(g_kv, steps),
        out_shape=(sd((s, hq * d), MXU), sd((g_kv, s, d), F32), sd((g_kv, s, d), F32)),
        in_specs=[pl.BlockSpec((hpg, nb * t, d), lambda g, i: (g, i, 0))]
        + _window_specs("rows", nb, nblk, d) + _window_specs("cols", nb, nblk, d) + _window_specs("rows", nb, nblk, d)
        + [pl.BlockSpec((nb * t, hpg * d), lambda g, i: (i, g)), pl.BlockSpec((hpg, 1, nb * t), row_map),
           pl.BlockSpec((hpg, 1, nb * t), row_map), pl.BlockSpec((hpg, 1, 1), lambda g, i: (g, 0, 0))],
        out_specs=(pl.BlockSpec((nb * t, hpg * d), lambda g, i: (i, g)),
                   pl.BlockSpec((1, s, d), lambda g, i: (g, 0, 0)),
                   pl.BlockSpec((1, s, d), lambda g, i: (g, 0, 0))),
        scratch_shapes=[pltpu.VMEM(((nb + 2) * t, d), F32), pltpu.VMEM(((nb + 2) * t, d), F32)],
        compiler_params=_params(("parallel", "arbitrary")),
    )(q, k, k, k, kt, kt, kt, v, v, v, do, lse, delta, slopes)


def flash_backward(q, k, kt, v, do, lse, delta, *, scale, dv, tq, tk, nsub, gq, name, split=None, exchange=None):
    hq, s, dq = q.shape
    g_kv = k.shape[0]
    hpg = hq // gq
    nq = s // tq
    tqq = tq * nsub
    nqs = s // tqq
    nkb = s // tk
    grid = (gq, nkb, nqs)
    hosted = exchange is not None
    m_cols = hpg * tq
    c = scale * LOG2E
    has_v = v is not None

    def body(*refs):
        it = iter(refs)
        q_ref, k_ref, kt_ref = next(it), next(it), next(it)
        v_ref = next(it) if has_v else None
        do_ref, lse_ref, dl_ref = next(it), next(it), next(it)
        nx = exchange.n if hosted else 0
        xs_refs = [next(it) for _ in range(nx)]
        dq_ref, dk_ref, dv_ref = next(it), next(it), next(it)
        land_refs = [next(it) for _ in range(nx)]
        dqt_s = next(it)
        sems = list(it)
        kj = pl.program_id(1)
        qi = pl.program_id(2)
        if hosted:
            first, last = _grid_edges(grid)
            pl.when(first)(lambda: exchange.start(xs_refs, land_refs, sems))

        @pl.when((kj == 0) & (qi == 0))
        def _():
            dqt_s[...] = jnp.zeros(dqt_s.shape, F32)

        @pl.when(qi == 0)
        def _():
            dk_ref[...] = jnp.zeros(dk_ref.shape, F32)
            dv_ref[...] = jnp.zeros(dv_ref.shape, F32)

        kk = k_ref[0]
        vv = v_ref[0] if has_v else kk[:, :dv]
        qqs, dds, sts, dps = {}, {}, {}, {}

        def issue(u):
            rows = slice(tq * u, tq * (u + 1))
            qqs[u] = q_ref[:, rows, :].reshape(m_cols, dq)
            dds[u] = jnp.concatenate([do_ref[rows, dv * hh:dv * hh + dv] for hh in range(hpg)], axis=0)
            sts[u] = _mm_nt(kk, qqs[u])
            dps[u] = _mm_nt(vv, dds[u])

        for u in range(min(AHEAD, nsub)):
            issue(u)
        dv_acc = dv_ref[0]
        dk_acc = dk_ref[0]
        for u in range(nsub):
            if u + AHEAD < nsub:
                issue(u + AHEAD)
            rows = slice(tq * u, tq * (u + 1))
            lse_row = jnp.concatenate([lse_ref[hh, :, rows] for hh in range(hpg)], axis=1)
            dl_row = jnp.concatenate([dl_ref[hh, :, rows] for hh in range(hpg)], axis=1)
            p = jnp.exp2(sts[u] - lse_row)
            ds = p * (dps[u] - dl_row) * scale
            dv_acc = dv_acc + _mm(p, dds[u])
            dk_acc = dk_acc + _mm(ds, qqs[u])
            dqt = _mm(kt_ref[0], ds)
            for hh in range(hpg):
                dqt_s[qi * nsub + u, dq * hh:dq * hh + dq, :] += dqt[:, tq * hh:tq * hh + tq]
        dv_ref[0] = dv_acc
        dk_ref[0] = jnp.where(qi == nqs - 1, dk_acc * (1.0 / c), dk_acc)

        @pl.when((kj == nkb - 1) & (qi == nqs - 1))
        def _():
            def emit(t, carry):
                r0 = pl.multiple_of(t * tq, tq)
                for hh in range(hpg):
                    blk = dqt_s[t, dq * hh:dq * hh + dq, :].T
                    if split is None:
                        dq_ref[pl.ds(r0, tq), dq * hh:dq * hh + dq] = blk
                    else:
                        rest = dq - split
                        dq_ref[pl.ds(r0, tq), split * hh:split * (hh + 1)] = blk[:, 0:split]
                        dq_ref[pl.ds(r0, tq), hpg * split + rest * hh:hpg * split + rest * (hh + 1)] = blk[:, split:]
                return carry

            lax.fori_loop(0, nq, emit, 0)

        if hosted:
            pl.when(last)(lambda: exchange.wait(xs_refs, land_refs, sems))

    kv_of = lambda g: g * g_kv // gq
    in_specs = [pl.BlockSpec((hpg, tqq, dq), lambda g, kj, qi: (g, qi, 0)),
                pl.BlockSpec((1, tk, dq), lambda g, kj, qi: (kv_of(g), kj, 0)),
                pl.BlockSpec((1, dq, tk), lambda g, kj, qi: (kv_of(g), 0, kj))]
    args = [q, k, kt]
    if has_v:
        in_specs.append(pl.BlockSpec((1, tk, dv), lambda g, kj, qi: (kv_of(g), kj, 0)))
        args.append(v)
    row_map = lambda g, kj, qi: (g, 0, qi)
    in_specs += [pl.BlockSpec((tqq, hpg * dv), lambda g, kj, qi: (qi, g)),
                 pl.BlockSpec((hpg, 1, tqq), row_map), pl.BlockSpec((hpg, 1, tqq), row_map)]
    args += [do, lse, delta]
    if hosted:
        in_specs += exchange.in_specs
        args += exchange.srcs
    sd = jax.ShapeDtypeStruct
    return pl.pallas_call(
        body, name=name, grid=grid,
        out_shape=(sd((s, hq * dq), F32), sd((gq, s, dq), F32), sd((gq, s, dv), F32))
        + (exchange.land_shapes if hosted else ()),
        in_specs=in_specs,
        out_specs=(pl.BlockSpec((s, hpg * dq), lambda g, kj, qi: (0, g)),
                   pl.BlockSpec((1, tk, dq), lambda g, kj, qi: (g, kj, 0)),
                   pl.BlockSpec((1, tk, dv), lambda g, kj, qi: (g, kj, 0))) + (exchange.out_specs if hosted else ()),
        scratch_shapes=[pltpu.VMEM((nq, hpg * dq, tq), F32)] + (list(exchange.sems) if hosted else []),
        compiler_params=_params(("arbitrary",) * 3 if hosted else ("parallel", "arbitrary", "arbitrary")),
    )(*args)


def loss_head(x, target, fnw):
    s, d = x.shape
    tm = min(ROW_TILE, s)

    def body(x_ref, t_ref, w_ref, lp_ref, dx_ref, dw_ref):
        @pl.when(pl.program_id(0) == 0)
        def _():
            lp_ref[...] = jnp.zeros(lp_ref.shape, F32)
            dw_ref[...] = jnp.zeros(dw_ref.shape, F32)

        x = x_ref[...]
        g = w_ref[...]
        err = x * _rms(x) * g - t_ref[...]
        lp_ref[...] += jnp.sum(err * err, axis=0, keepdims=True)
        dx, dg = _rms_bwd(err * (1.0 / d), x, g)
        dx_ref[...] = dx
        dw_ref[...] += jnp.sum(dg, axis=0, keepdims=True)

    sd = jax.ShapeDtypeStruct
    return pl.pallas_call(
        body, name="loss_head", grid=(s // tm,),
        out_shape=(sd((1, d), F32), sd((s, d), F32), sd((1, d), F32)),
        in_specs=[_row_spec(tm, d), _row_spec(tm, d), _full_spec(fnw.shape)],
        out_specs=(_full_spec((1, d)), _row_spec(tm, d), _full_spec((1, d))),
        compiler_params=_params(("arbitrary",)),
    )(x, target, fnw)


def mixer_out_backward(dx, y, mod, pairs, w_out, delta_heads, name, lse=None, sink=None):
    s, d = dx.shape
    tm = min(ROW_TILE, s)
    n = len(pairs)
    widths = [o.shape[1] for o, _ in pairs]
    n_delta = sum(1 for h in delta_heads if h)
    with_sink = lse is not None

    def body(*refs):
        it = iter(refs)
        dx_ref, y_ref, mod_ref, wt_ref = next(it), next(it), next(it), next(it)
        pr = [next(it) for _ in range(2 * n)]
        lse_ref = next(it) if with_sink else None
        sink_ref = next(it) if with_sink else None
        outs = [next(it) for _ in range(2 * n)]
        dl_refs = [next(it) for _ in range(n_delta)]
        dgate_ref, dw_ref = next(it), next(it)
        dsink_ref = next(it) if with_sink else None

        @pl.when(pl.program_id(0) == 0)
        def _():
            dgate_ref[...] = jnp.zeros(dgate_ref.shape, F32)
            dw_ref[...] = jnp.zeros(dw_ref.shape, F32)
            if with_sink:
                dsink_ref[...] = jnp.zeros(dsink_ref.shape, F32)

        dxo = dx_ref[...]
        dgate_ref[...] += jnp.sum(dxo * y_ref[...].astype(F32), axis=0, keepdims=True)
        dy = (dxo * mod_ref[2:3, :]).astype(MXU)
        dmix = _mm_nt(dy, wt_ref[...])
        r0 = 0
        di = 0
        for i in range(n):
            o = pr[2 * i][...].astype(F32)
            g = pr[2 * i + 1][...].astype(F32)
            dm = dmix[:, r0:r0 + widths[i]]
            sg = _sigmoid(g)
            act = g * sg
            do = dm * act
            outs[2 * i][...] = do.astype(MXU)
            outs[2 * i + 1][...] = (dm * o * (sg * (1.0 + g * (1.0 - sg)))).astype(MXU)
            dw_ref[r0:r0 + widths[i], :] += _mm_tn(o * act, dy)
            if delta_heads[i]:
                dlt = _group_sums_t(do * o, HD)[0:delta_heads[i], :]
                dl_refs[di][...] = dlt
                if with_sink:
                    ps = jnp.exp2(sink_ref[...] - lse_ref[...])
                    dsink_ref[...] += -jnp.sum(ps * dlt, axis=1, keepdims=True)
                di += 1
            r0 += widths[i]

    flat = [a for p in pairs for a in p]
    sd = jax.ShapeDtypeStruct
    in_specs = [_row_spec(tm, d), _row_spec(tm, d), _full_spec(mod.shape), _full_spec(w_out.shape)]
    in_specs += [_row_spec(tm, a.shape[1]) for a in flat]
    args = [dx, y, mod, w_out] + flat
    if with_sink:
        nh = lse.shape[0]
        in_specs += [_rows_spec(nh, tm), _full_spec(sink.shape)]
        args += [lse, sink]
    out_shape = [sd((s, a.shape[1]), MXU) for a in flat]
    out_specs = [_row_spec(tm, a.shape[1]) for a in flat]
    for h in delta_heads:
        if h:
            out_shape.append(sd((h, s), F32))
            out_specs.append(_rows_spec(h, tm))
    out_shape += [sd((1, d), F32), sd((sum(widths), d), F32)]
    out_specs += [_full_spec((1, d)), _full_spec((sum(widths), d))]
    if with_sink:
        out_shape.append(sd((lse.shape[0], 1), F32))
        out_specs.append(_full_spec((lse.shape[0], 1)))
    return pl.pallas_call(
        body, name=name, grid=(s // tm,), out_shape=tuple(out_shape), in_specs=in_specs, out_specs=tuple(out_specs),
        compiler_params=_params(("arbitrary",)),
    )(*args)


def latent_out_backward(d_ob, o_lat, w_uv):
    s = o_lat.shape[0]
    tm = min(ROW_TILE, s)

    def body(d_ref, o_ref, uv_ref, dol_ref, dl_ref, duv_ref, prod_s):
        @pl.when(pl.program_id(0) == 0)
        def _():
            duv_ref[...] = jnp.zeros(duv_ref.shape, F32)

        for hh in range(B_HEADS):
            dh = d_ref[:, HD * hh:HD * hh + HD]
            ol = o_ref[:, B_KV_LORA * hh:B_KV_LORA * (hh + 1)].astype(F32)
            dol = _mm_nt(dh, uv_ref[hh])
            dol_ref[:, B_KV_LORA * hh:B_KV_LORA * (hh + 1)] = dol.astype(MXU)
            prod_s[:, B_KV_LORA * hh:B_KV_LORA * (hh + 1)] = dol * ol
            duv_ref[:, HD * hh:HD * hh + HD] += _mm_tn(ol, dh)
        dl_ref[...] = _group_sums_t(prod_s[...], B_KV_LORA)[0:B_HEADS, :]

    sd = jax.ShapeDtypeStruct
    duv_shape = (B_KV_LORA, B_HEADS * HD)
    return pl.pallas_call(
        body, name="latent_out_backward", grid=(s // tm,),
        out_shape=(sd(o_lat.shape, MXU), sd((B_HEADS, s), F32), sd(duv_shape, F32)),
        in_specs=[_row_spec(tm, d_ob.shape[1]), _row_spec(tm, o_lat.shape[1]), _full_spec(w_uv.shape)],
        out_specs=(_row_spec(tm, o_lat.shape[1]), _rows_spec(B_HEADS, tm), _full_spec(duv_shape)),
        scratch_shapes=[pltpu.VMEM((tm, o_lat.shape[1]), F32)],
        compiler_params=_params(("arbitrary",)),
    )(d_ob, o_lat, w_uv)


def even_prep_backward(dqa, dka, dva, dqb, dkb, dvb, qa_raw, ka_raw, cq_raw, ckv_raw,
                       gq, gk, qln, kvln, w_uq_t, uk_bd, bd, cos_a, sin_a, cos_t, sin_t):
    s = qa_raw.shape[0]
    tm = min(ROW_TILE, s)
    half_lat = B_KV_LORA * B_HEADS // 2
    half_w = dqb.shape[1] // 2

    def body(dqa_ref, dka_ref, dva_ref, dqb_ref, dkb_ref, dvb_ref, qa_ref, ka_ref, cq_ref, ckv_ref,
             gq_ref, gk_ref, qln_ref, kvln_ref, uqt_ref, ukbd_ref, bd_ref, ca_ref, sa_ref, ct_ref, st_ref,
             pqa, pka, pva, pcq, pckv, pkr, gqn, gkn, gqln, gkvln, guq, guk):
        @pl.when(pl.program_id(0) == 0)
        def _():
            for r in (gqn, gkn, gqln, gkvln, guq, guk):
                r[...] = jnp.zeros(r.shape, F32)

        ca, sa, ct, st = ca_ref[...], sa_ref[...], ct_ref[...], st_ref[...]
        wide = lambda t, n: jnp.concatenate([t] * n, axis=1)
        rows = lambda a: jnp.sum(a, axis=0, keepdims=True)
        dx, dg = _head_norm_bwd(_rope_t(dqa_ref[...], wide(ca, 4), wide(sa, 4), 32), qa_ref[...], gq_ref[...],
                                bd_ref, HD)
        pqa[...] = dx.astype(MXU)
        gqn[...] += rows(dg)
        dk_all = jnp.concatenate([dka_ref[g] for g in range(A_KV)], axis=1)
        dx, dg = _head_norm_bwd(_rope_t(dk_all, ca, sa, 32), ka_ref[...], gk_ref[...], bd_ref[0:128, 0:128], HD)
        pka[...] = dx.astype(MXU)
        gkn[...] += rows(dg)
        pva[...] = jnp.concatenate([dva_ref[g] for g in range(A_KV)], axis=1).astype(MXU)
        cq_raw = cq_ref[...]
        cq_n = cq_raw * _rms(cq_raw) * qln_ref[...]
        qb = _mm_nt(cq_n, uqt_ref[...])
        d_lat = jnp.concatenate([dqb_ref[:, 0:half_lat], dqb_ref[:, half_w:half_w + half_lat]], axis=1)
        d_rope = jnp.concatenate([dqb_ref[:, half_lat:half_w], dqb_ref[:, half_w + half_lat:]], axis=1)
        for hh in range(B_HEADS):
            guk[:, B_NOPE * hh:B_NOPE * (hh + 1)] += _mm_tn(d_lat[:, B_KV_LORA * hh:B_KV_LORA * (hh + 1)],
                                                            qb[:, B_NOPE * hh:B_NOPE * (hh + 1)])
        dqb_all = jnp.concatenate([_mm_nt(d_lat, ukbd_ref[...]),
                                   _rope_t(d_rope, wide(ct, 2), wide(st, 2), 32)], axis=1)
        guq[...] += _mm_tn(dqb_all, cq_n)
        dx, dg = _rms_bwd(_mm(dqb_all, uqt_ref[...]), cq_raw, qln_ref[...])
        pcq[...] = dx.astype(MXU)
        gqln[...] += rows(dg)
        dkb_sum = dkb_ref[0] + dkb_ref[1]
        dckv = dkb_sum[:, 0:B_KV_LORA] + dvb_ref[0] + dvb_ref[1]
        dx, dg = _rms_bwd(dckv, ckv_ref[...], kvln_ref[...])
        pckv[...] = dx.astype(MXU)
        gkvln[...] += rows(dg)
        pkr[...] = _rope_t(dkb_sum[:, B_KV_LORA:B_QK], ct[:, 0:B_ROPE], st[:, 0:B_ROPE], 32).astype(MXU)

    sd = jax.ShapeDtypeStruct
    consts = [gq, gk, qln, kvln, w_uq_t, uk_bd, bd]
    in_specs = [_row_spec(tm, 512), _head_spec(A_KV, tm, HD), _head_spec(A_KV, tm, HD),
                _row_spec(tm, dqb.shape[1]), _head_spec(2, tm, B_QK), _head_spec(2, tm, B_KV_LORA),
                _row_spec(tm, 512), _row_spec(tm, 128), _row_spec(tm, B_Q_LORA), _row_spec(tm, B_KV_LORA)]
    in_specs += [_full_spec(a.shape) for a in consts] + [_row_spec(tm, 128)] * 4
    small = [sd(gq.shape, F32), sd(gk.shape, F32), sd(qln.shape, F32), sd(kvln.shape, F32), sd(w_uq_t.shape, F32),
             sd((B_KV_LORA, B_HEADS * B_NOPE), F32)]
    out_shape = (sd((s, 512), MXU), sd((s, 128), MXU), sd((s, 128), MXU), sd((s, B_Q_LORA), MXU),
                 sd((s, B_KV_LORA), MXU), sd((s, B_ROPE), MXU), *small)
    out_specs = (_row_spec(tm, 512), _row_spec(tm, 128), _row_spec(tm, 128), _row_spec(tm, B_Q_LORA),
                 _row_spec(tm, B_KV_LORA), _row_spec(tm, B_ROPE), *[_full_spec(a.shape) for a in small])
    return pl.pallas_call(
        body, name="even_prep_backward", grid=(s // tm,), out_shape=out_shape, in_specs=in_specs, out_specs=out_specs,
        compiler_params=_params(("arbitrary",)),
    )(dqa, dka, dva, dqb, dkb, dvb, qa_raw, ka_raw, cq_raw, ckv_raw, *consts, cos_a, sin_a, cos_t, sin_t)


def in_proj_backward(x, mod, nw, pieces, name, *, dx_out=None, w_in_t=None, dw_rows=None, exchange=None):
    s, d = x.shape
    tm = min(ROW_TILE, s)
    grid = (s // tm,)
    n = len(pieces)
    cols = [c for _, c in pieces]
    want_dx = w_in_t is not None
    want_dw = dw_rows is not None
    n_cols = sum(c1 - c0 for c0, c1 in cols)
    hosted = exchange is not None
    nx = exchange.n if hosted else 0

    def body(*refs):
        it = iter(refs)
        x_ref, mod_ref, nw_ref = next(it), next(it), next(it)
        dxo_ref, wt_ref = (next(it), next(it)) if want_dx else (None, None)
        p_refs = [next(it) for _ in range(n)]
        xs_refs = [next(it) for _ in range(nx)]
        dx_ref, dv_ref = (next(it), next(it)) if want_dx else (None, None)
        dw_ref = next(it) if want_dw else None
        land_refs = [next(it) for _ in range(nx)]
        acc_ref = next(it) if want_dx else None
        dw_acc = next(it) if want_dw else None
        sems = list(it)
        first, last = _grid_edges(grid)
        if hosted:
            pl.when(first)(lambda: exchange.start(xs_refs, land_refs, sems))

        @pl.when(first)
        def _():
            if want_dw:
                dw_acc[...] = jnp.zeros(dw_acc.shape, F32)
            if want_dx:
                acc_ref[...] = jnp.zeros(acc_ref.shape, F32)

        xn, g1, h = _modulated(x_ref[...], mod_ref, nw_ref)
        hb = h.astype(MXU)
        dh = jnp.zeros((tm, d), F32)
        for k, (pr, (c0, c1)) in enumerate(zip(p_refs, cols)):
            if len(pr.shape) == 3:
                pc = jnp.concatenate([pr[g] for g in range(pr.shape[0])], axis=1).astype(MXU)
            else:
                pc = pr[...].astype(MXU)
            if want_dx:
                dh = dh + jnp.dot(pc, wt_ref[c0:c1, :], preferred_element_type=F32)
            if want_dw:
                r0, r1 = dw_rows[k]
                dw_acc[r0:r1, :] += _mm_tn(pc, hb)
        if want_dx:
            acc_ref[0:1, :] += jnp.sum(dh, axis=0, keepdims=True)
            acc_ref[1:2, :] += jnp.sum(dh * xn, axis=0, keepdims=True)
            dxn = dh * g1
            x = x_ref[...]
            dx_ref[...] = dxo_ref[...] + _rms(x) * (dxn - xn * jnp.mean(dxn * xn, axis=-1, keepdims=True))

        @pl.when(last)
        def _():
            if want_dx:
                dg1 = acc_ref[1:2, :]
                dv_ref[0:1, :] = acc_ref[0:1, :]
                dv_ref[1:2, :] = dg1 * nw_ref[...]
                dv_ref[2:3, :] = dg1 * (1.0 + mod_ref[1:2, :])
                dv_ref[3:4, :] = jnp.zeros((1, d), F32)
            if want_dw:
                dw_ref[...] = dw_acc[...].astype(MXU)

        if hosted:
            pl.when(last)(lambda: exchange.wait(xs_refs, land_refs, sems))

    arrs = [a for a, _ in pieces]
    sd = jax.ShapeDtypeStruct
    args = [x, mod, nw] + ([dx_out, w_in_t] if want_dx else []) + arrs + (exchange.srcs if hosted else [])
    in_specs = [_row_spec(tm, d), _full_spec(mod.shape), _full_spec(nw.shape)]
    in_specs += [_row_spec(tm, d), _full_spec(w_in_t.shape)] if want_dx else []
    in_specs += [_row_spec(tm, a.shape[1]) if a.ndim == 2 else _head_spec(a.shape[0], tm, a.shape[2]) for a in arrs]
    in_specs += exchange.in_specs if hosted else []
    out_shape, out_specs, scratch = [], [], []
    if want_dx:
        out_shape += [sd((s, d), F32), sd((4, d), F32)]
        out_specs += [_row_spec(tm, d), _full_spec((4, d))]
        scratch.append(pltpu.VMEM((8, d), F32))
    if want_dw:
        out_shape.append(sd((n_cols, d), MXU))
        out_specs.append(_full_spec((n_cols, d)))
        scratch.append(pltpu.VMEM((n_cols, d), F32))
    if hosted:
        out_shape += list(exchange.land_shapes)
        out_specs += list(exchange.out_specs)
        scratch += list(exchange.sems)
    return pl.pallas_call(
        body, name=name, grid=grid, out_shape=tuple(out_shape), in_specs=in_specs, out_specs=tuple(out_specs),
        scratch_shapes=scratch, compiler_params=_params(("arbitrary",)),
    )(*args)


def ada_weight_grad(c_all, dmod_cols):
    d = c_all.shape[1]
    w = dmod_cols.shape[2]

    def body(c_ref, dm_ref, out_ref):
        ca = _silu(c_ref[...])
        for l in range(2):
            out_ref[l] = _mm_tn(ca, dm_ref[l])

    return pl.pallas_call(
        body, name="ada_weight_grad",
        out_shape=jax.ShapeDtypeStruct((2, d, w), F32),
        compiler_params=pltpu.CompilerParams(vmem_limit_bytes=VMEM_LIMIT),
    )(c_all, dmod_cols)


def _slot_sum(g_ref):
    g = g_ref[0].astype(F32)
    for k in range(1, g_ref.shape[0]):
        g = g + g_ref[k].astype(F32)
    return g


def _adamw_math(g, w, m, v):
    m_new = ADAM_B1 * m + (1.0 - ADAM_B1) * g
    v_new = ADAM_B2 * v + (1.0 - ADAM_B2) * (g * g)
    m_hat = m_new / (1.0 - ADAM_B1 ** ADAM_STEP)
    v_hat = v_new / (1.0 - ADAM_B2 ** ADAM_STEP)
    return -ADAM_LR * (m_hat / (jnp.sqrt(v_hat) + ADAM_EPS) + ADAM_WD * w), m_new, v_new


def adamw_small(g_alls, ws, ms, vs, loss_all):
    n = len(ws)

    def body(*refs):
        g_refs, w_refs, m_refs, v_refs = (refs[i * n:(i + 1) * n] for i in range(4))
        loss_ref = refs[4 * n]
        outs = refs[4 * n + 1:]
        for i in range(n):
            g = _slot_sum(g_refs[i])
            outs[i][...] = g
            outs[n + i][...], outs[2 * n + i][...], outs[3 * n + i][...] = _adamw_math(
                g, w_refs[i][...], m_refs[i][...], v_refs[i][...])
        outs[4 * n][...] = _slot_sum(loss_ref)

    sds = [jax.ShapeDtypeStruct(w.shape, F32) for w in ws]
    res = pl.pallas_call(
        body, name="adamw_small", out_shape=tuple(sds * 4) + (jax.ShapeDtypeStruct(loss_all.shape[1:], F32),),
        compiler_params=pltpu.CompilerParams(vmem_limit_bytes=VMEM_LIMIT),
    )(*g_alls, *ws, *ms, *vs, loss_all)
    return [res[i * n:(i + 1) * n] for i in range(4)], res[4 * n]


def adamw_rows(g_slots, w, m, v, name):
    n, r, lanes = g_slots.shape
    fits = [t for t in range(16, r + 1, 16) if r % t == 0 and t * lanes <= ADAM_TILE]
    tr = max(fits) if fits else r
    def body(g_ref, w_ref, m_ref, v_ref, go, do, mo, vo):
        g = _slot_sum(g_ref)
        go[...] = g
        do[...], mo[...], vo[...] = _adamw_math(g, w_ref[...], m_ref[...], v_ref[...])

    row = pl.BlockSpec((tr, lanes), lambda i: (i, 0))
    sd = jax.ShapeDtypeStruct((r, lanes), F32)
    return pl.pallas_call(
        body, name=name, grid=(r // tr,), out_shape=(sd, sd, sd, sd),
        in_specs=[pl.BlockSpec((n, tr, lanes), lambda i: (0, i, 0)), row, row, row],
        out_specs=(row, row, row, row),
        compiler_params=_params(("parallel",)),
    )(g_slots, w, m, v)


def _rope_tables(s):
    def cs(pos, dim):
        inv = ROPE_THETA ** (-np.arange(0, dim, 2, dtype=np.float32) / dim)
        ang = pos.astype(np.float32)[:, None] * inv.astype(np.float32)[None, :]
        return np.cos(ang), np.sin(ang)

    rows = s // GRID_W
    row = np.repeat(np.arange(rows), GRID_W)
    col = np.tile(np.arange(GRID_W), rows)
    cr, sr = cs(row, HD // 2)
    cc, sc = cs(col, HD // 2)
    ct, st = cs(np.arange(s), B_ROPE)
    tables = (np.concatenate([cr, cr, cc, cc] * 2, axis=-1), np.concatenate([-sr, sr, -sc, sc] * 2, axis=-1),
              np.concatenate([ct, ct] * 4, axis=-1), np.concatenate([-st, st] * 4, axis=-1))
    return tuple(jnp.asarray(t, F32) for t in tables)


def _even_rows_to_kernel(wt):
    return jnp.concatenate([wt[:1664], wt[1696:], wt[1664:1696]], axis=0)


def _uq_rows_to_kernel(wt):
    r = wt.reshape(B_HEADS, B_NOPE + B_ROPE, -1)
    return jnp.concatenate([r[:, :B_NOPE].reshape(B_HEADS * B_NOPE, -1), r[:, B_NOPE:].reshape(B_HEADS * B_ROPE, -1)])


def _uq_rows_to_reference(wt):
    nope = wt[:B_HEADS * B_NOPE].reshape(B_HEADS, B_NOPE, -1)
    rope = wt[B_HEADS * B_NOPE:].reshape(B_HEADS, B_ROPE, -1)
    return jnp.concatenate([nope, rope], axis=1).reshape(B_HEADS * (B_NOPE + B_ROPE), -1)


def _shard_t(w):
    return jnp.transpose(w[0])


def _unshard_t(wt, like):
    return jnp.transpose(wt)[None].reshape(like.shape)


def kernel(x, c, norm_w, ada_w, ada_b, even_w_in, a_q_norm, a_k_norm, b_q_lora_norm, b_kv_lora_norm, b_w_uq, b_w_uk, b_w_uv, even_w_out, odd_w_in, c_sink, odd_w_out, final_norm, loss_target, m_norm_w, m_ada_w, m_ada_b, m_even_w_in, m_a_q_norm, m_a_k_norm, m_b_q_lora_norm, m_b_kv_lora_norm, m_b_w_uq, m_b_w_uk, m_b_w_uv, m_even_w_out, m_odd_w_in, m_c_sink, m_odd_w_out, m_final_norm, v_norm_w, v_ada_w, v_ada_b, v_even_w_in, v_a_q_norm, v_a_k_norm, v_b_q_lora_norm, v_b_kv_lora_norm, v_b_w_uq, v_b_w_uk, v_b_w_uv, v_even_w_out, v_odd_w_in, v_c_sink, v_odd_w_out, v_final_norm):
    s, d = x.shape[1], x.shape[2]
    x0 = x[0]
    target = loss_target[0]
    me_flat = 4 * lax.axis_index("x") + 2 * lax.axis_index("y") + lax.axis_index("c")

    wcols = ada_w.shape[2]
    bias_cols = lax.dynamic_slice_in_dim(ada_b.reshape(2, N_DEV, wcols), me_flat, 1, axis=1)
    call, modp, (g_in_e, g_uq) = ada_forward(
        jnp.broadcast_to(c, (8, d)), ada_w, bias_cols,
        Gather([_shard_t(even_w_in).astype(MXU), _shard_t(b_w_uq).astype(MXU)]))
    wt_in_e = _even_rows_to_kernel(g_in_e.reshape(-1, d))
    wt_uq = _uq_rows_to_kernel(g_uq.reshape(-1, B_Q_LORA))
    later_exchange = Exchange([_shard_t(odd_w_in).astype(MXU), even_w_out[0].astype(MXU),
                               odd_w_out[0].astype(MXU)], scatter=False)
    uk_bd = (jnp.eye(B_HEADS, dtype=F32)[:, None, :, None] * jnp.transpose(b_w_uk[0], (1, 2, 0))[:, :, None, :]
             ).reshape(B_HEADS * B_NOPE, B_HEADS * B_KV_LORA).astype(MXU)
    head_bd = jnp.asarray(np.kron(np.eye(A_HEADS), np.ones((HD, HD))), MXU)
    gq_full, gk_full = jnp.tile(a_q_norm, (1, A_HEADS)), jnp.tile(a_k_norm, (1, A_KV))
    w_uv = jnp.transpose(b_w_uv[0], (1, 0, 2)).astype(MXU)

    c_all = call[:, 0, :]
    mod = jnp.transpose(modp[:, :, 0, :], (1, 0, 2)).reshape(2, 3, d)
    mod_e, mod_o = mod[0], mod[1]
    nw_e, nw_o = norm_w[0:1], norm_w[1:2]

    cos_a, sin_a, cos_t, sin_t = _rope_tables(s)
    slopes = (2.0 ** (-8.0 * jnp.arange(1, C_HEADS + 1, dtype=F32) / C_HEADS)).reshape(C_HEADS, 1, 1)
    sink2 = c_sink.reshape(C_HEADS, 1, 1) * LOG2E

    (qa, ka, va, qb, kb, kat, vat, kbt, qa_raw, ka_raw, cq_raw, ckv_raw, ga, gb) = even_in_forward(
        x0, mod_e, nw_e, wt_in_e, gq_full, gk_full, b_q_lora_norm, b_kv_lora_norm, wt_uq, uk_bd, head_bd,
        cos_a, sin_a, cos_t, sin_t)
    tk_dense = min(512, s)
    tq_dense = min(256, s)
    fwd_sub = min(8, s // tk_dense)
    bwd_sub_a = min(16, s // tq_dense)
    bwd_sub_b = min(8, s // tq_dense)
    oa, lse_a, g_in_o, g_out_e, g_out_o = flash_forward(
        qa, ka, vat, dv=HD, tq=tq_dense, tk=tk_dense, nsub=fwd_sub, name="attn_a_fwd",
        exchange=later_exchange)
    wt_in_o = g_in_o.reshape(-1, d)
    w_out_e = g_out_e.reshape(-1, d)
    w_out_o = g_out_o.reshape(-1, d)
    o_lat, lse_b = flash_forward(qb, kb, kbt, dv=B_KV_LORA, tq=min(128, s), tk=tk_dense, nsub=fwd_sub,
                                 name="attn_b_fwd")
    ob = latent_out_forward(o_lat, w_uv)
    x1, y_e = mixer_out_forward(x0, mod_e, [(oa, ga), (ob, gb)], w_out_e, "even_out_fwd")

    qc, kc, vc, kct, vct, gc = odd_in_forward(x1, mod_o, nw_o, wt_in_o)
    win_sub = min(8, s // WINDOW)
    oc, lse_c = window_forward(qc, kc, vct, sink2, slopes, win_sub, "attn_c_fwd")
    x2, y_o = mixer_out_forward(x1, mod_o, [(oc, gc)], w_out_o, "odd_out_fwd")

    loss_lanes, dx2, d_final = loss_head(x2, target, final_norm.reshape(1, d))
    loss_part = (0.5 / d) * jnp.sum(loss_lanes)

    doc, dgc, delta_c, dgate_o, dw_out_o, dsink = mixer_out_backward(
        dx2, y_o, mod_o, [(oc, gc)], w_out_o, [C_HEADS], "odd_out_bwd", lse=lse_c.reshape(C_HEADS, s),
        sink=sink2.reshape(C_HEADS, 1))
    rows3 = lambda t: t.reshape(t.shape[0], 1, s)
    dqc, dkc, dvc = window_backward(qc, kc, kct, vc, doc, lse_c, rows3(delta_c), slopes, win_sub, "attn_c_bwd")
    dx1, dvec_o, dwt_in_o = in_proj_backward(
        x1, mod_o, nw_o, [(dqc, O_Q), (dkc, O_K), (dvc, O_V), (dgc, O_G)], "odd_in_bwd",
        dx_out=dx2, w_in_t=wt_in_o, dw_rows=[O_Q, O_K, O_V, O_G])

    doa, dga, dob, dgb, delta_a, dgate_e, dw_out_e = mixer_out_backward(
        dx1, y_e, mod_e, [(oa, ga), (ob, gb)], w_out_e, [A_HEADS, 0], "even_out_bwd")
    d_olat, delta_b, dw_uv = latent_out_backward(dob, o_lat, w_uv)
    blocks = lambda g: g.astype(MXU).reshape(N_DEV, g.shape[0] // N_DEV, g.shape[1])
    even_pieces = lambda: [(pqa, E_QA), (pka, E_KA), (pva, E_VA), (dga, E_GA), (pcq, E_CQ), (pckv, E_CKV),
                           (dgb, E_GB), (pkr, E_KR)]
    scatter_odd = Exchange([blocks(dwt_in_o), blocks(dw_out_o)], True)
    scatter_out_e = Exchange([blocks(dw_out_e)], True)
    dqb, dkb, dvb, l_in_o, l_out_o = flash_backward(
        qb, kb, kbt, None, d_olat, lse_b, rows3(delta_b), scale=SCALE_B, dv=B_KV_LORA,
        tq=tq_dense, tk=tk_dense, nsub=bwd_sub_b, gq=2, name="attn_b_bwd", split=B_KV_LORA, exchange=scatter_odd)
    dqa, dka, dva, l_out_e = flash_backward(
        qa, ka, kat, va, doa, lse_a, rows3(delta_a), scale=SCALE_A, dv=HD,
        tq=tq_dense, tk=tk_dense, nsub=bwd_sub_a, gq=A_KV, name="attn_a_bwd", exchange=scatter_out_e)
    (pqa, pka, pva, pcq, pckv, pkr, g_qn, g_kn, g_qln, g_kvln, dwt_uq, dw_uk) = even_prep_backward(
        dqa, dka, dva, dqb, dkb, dvb, qa_raw, ka_raw, cq_raw, ckv_raw,
        gq_full, gk_full, b_q_lora_norm, b_kv_lora_norm, wt_uq, uk_bd, head_bd, cos_a, sin_a, cos_t, sin_t)
    g_qn = jnp.sum(g_qn.reshape(A_HEADS, HD), axis=0)
    g_kn = jnp.sum(g_kn.reshape(A_KV, HD), axis=0)
    dwt_in_e, l_uk, l_uv = in_proj_backward(
        x0, mod_e, nw_e, even_pieces(), "even_in_bwd_dw",
        dw_rows=[E_QA, E_KA, E_VA, E_GA, E_CQ, E_CKV, (1696, 2208), (1664, 1696)],
        exchange=Exchange([dw_uk.astype(MXU), dw_uv.astype(MXU)], scatter=False))
    dx0, dvec_e, l_in_e, l_uq = in_proj_backward(
        x0, mod_e, nw_e, even_pieces(), "even_in_bwd_dx", dx_out=dx1, w_in_t=wt_in_e,
        exchange=Exchange([blocks(dwt_in_e), blocks(_uq_rows_to_reference(dwt_uq))], True))

    dmod = jnp.stack([jnp.concatenate([dvec_e[0], dvec_e[1], dgate_e[0]]),
                      jnp.concatenate([dvec_o[0], dvec_o[1], dgate_o[0]])])
    d_norm_w = jnp.stack([dvec_e[2], dvec_o[2]])
    small_names = ["norm_w", "ada_b", "a_q_norm", "a_k_norm", "b_q_lora_norm", "b_kv_lora_norm", "b_w_uk", "b_w_uv",
                   "c_sink", "final_norm"]
    small_w = [norm_w, ada_b, a_q_norm, a_k_norm, b_q_lora_norm, b_kv_lora_norm, b_w_uk, b_w_uv, c_sink, final_norm]
    small_m = [m_norm_w, m_ada_b, m_a_q_norm, m_a_k_norm, m_b_q_lora_norm, m_b_kv_lora_norm, m_b_w_uk, m_b_w_uv,
               m_c_sink, m_final_norm]
    small_v = [v_norm_w, v_ada_b, v_a_q_norm, v_a_k_norm, v_b_q_lora_norm, v_b_kv_lora_norm, v_b_w_uk, v_b_w_uv,
               v_c_sink, v_final_norm]
    small_g = [d_norm_w, dmod, g_qn, g_kn, g_qln, g_kvln, None, None, dsink, d_final]
    flat2 = lambda a: a.reshape((1, -1)) if a.size == a.shape[-1] else a.reshape(a.shape[-3:] if a.ndim > 3 else a.shape)
    kshape = [flat2(w).shape for w in small_w]
    late = [i for i, g in enumerate(small_g) if g is not None]
    gathered = all_gather_slots(
        Gather([small_g[i].reshape(kshape[i]) for i in late] + [jnp.full((8, 128), loss_part, F32)]),
        "gather_small_grads")
    g_all = [None] * len(small_g)
    for i, g in zip(late, gathered):
        g_all[i] = g
    g_all[6], g_all[7] = (l.reshape((N_DEV,) + kshape[6]) for l in (l_uk, l_uv))
    sm_out, loss_sum = adamw_small(g_all, [flat2(a) for a in small_w], [flat2(a) for a in small_m],
                                   [flat2(a) for a in small_v], gathered[-1])
    loss = loss_sum[0, 0]
    sm = [{nm: p.reshape(w.shape) for nm, w, p in zip(small_names, small_w, outs)} for outs in sm_out]

    dmod_all = g_all[1].reshape(N_DEV, 2, N_DEV, wcols)
    dmod_cols = lax.dynamic_slice_in_dim(dmod_all, me_flat, 1, axis=2)[:, :, 0, :]
    pad16 = lambda a: jnp.concatenate([a, jnp.zeros_like(a)], axis=0)
    g_ada_w = ada_weight_grad(pad16(c_all), jnp.transpose(pad16(dmod_cols), (1, 0, 2)))
    rows_of = lambda a: a.reshape(-1, wcols)
    ada = adamw_rows(rows_of(g_ada_w)[None], rows_of(ada_w), rows_of(m_ada_w), rows_of(v_ada_w), "adamw_ada_w")
    ada = [p.reshape(ada_w.shape) for p in ada]

    bg = [{}, {}, {}, {}]
    for nm, landed, w, m, v, transposed in (
            ("even_w_in", l_in_e, even_w_in, m_even_w_in, v_even_w_in, True),
            ("b_w_uq", l_uq, b_w_uq, m_b_w_uq, v_b_w_uq, True),
            ("odd_w_in", l_in_o, odd_w_in, m_odd_w_in, v_odd_w_in, True),
            ("even_w_out", l_out_e, even_w_out, m_even_w_out, v_even_w_out, False),
            ("odd_w_out", l_out_o, odd_w_out, m_odd_w_out, v_odd_w_out, False)):
        view = _shard_t if transposed else (lambda a: a[0])
        res = adamw_rows(landed, view(w), view(m), view(v), "adamw_" + nm)
        for kind, p in enumerate(res):
            bg[kind][nm] = _unshard_t(p, w) if transposed else p[None]
    big_names = ["even_w_in", "odd_w_in", "even_w_out", "odd_w_out", "b_w_uq"]

    order = ["norm_w", "ada_w", "ada_b", "even_w_in", "a_q_norm", "a_k_norm", "b_q_lora_norm", "b_kv_lora_norm",
             "b_w_uq", "b_w_uk", "b_w_uv", "even_w_out", "odd_w_in", "c_sink", "odd_w_out", "final_norm"]

    def pick(kind):
        out = []
        for nm in order:
            if nm == "ada_w":
                out.append(ada[kind])
            elif nm in big_names:
                out.append(bg[kind][nm])
            else:
                out.append(sm[kind][nm])
        return out

    return (loss, dx0[None], *pick(0), *pick(1), *pick(2), *pick(3))
```

```python
import functools

import jax
import jax.numpy as jnp
import numpy as np
from jax import lax
from jax.experimental import pallas as pl
from jax.experimental.pallas import tpu as pltpu

F32 = jnp.float32
MXU = jnp.bfloat16
EPS = 1e-6
ROPE_THETA = 10000.0
GRID_W = 64
HD = 64
N_DEV = 8

A_HEADS, A_KV = 8, 2
B_HEADS, B_NOPE, B_ROPE, B_Q_LORA, B_KV_LORA = 8, 64, 32, 256, 128
B_QK = B_KV_LORA + B_ROPE
C_HEADS, C_KV = 16, 4
WINDOW = 128

ADAM_LR, ADAM_B1, ADAM_B2, ADAM_EPS, ADAM_WD, ADAM_STEP = 0.001, 0.9, 0.999, 1e-08, 0.01, 10

ROW_TILE = 512
ADAM_TILE = 2048 * 128

LOG2E = 1.4426950408889634
SCALE_A = HD ** -0.5
SCALE_B = (B_NOPE + B_ROPE) ** -0.5
SCALE2_A, SCALE2_B = SCALE_A * LOG2E, SCALE_B * LOG2E
VMEM_LIMIT = 56 * 1024 * 1024

E_QA, E_KA, E_VA, E_GA, E_CQ, E_CKV, E_GB, E_KR = (
    (0, 512), (512, 640), (640, 768), (768, 1280), (1280, 1536), (1536, 1664), (1664, 2176), (2176, 2208))
O_Q, O_K, O_V, O_G = (0, 1024), (1024, 1280), (1280, 1536), (1536, 2560)


def _mm(a, b):
    return jnp.dot(a.astype(MXU), b.astype(MXU), preferred_element_type=F32)


def _mm_nt(a, b):
    return lax.dot_general(a.astype(MXU), b.astype(MXU), (((1,), (1,)), ((), ())), preferred_element_type=F32)


def _mm_tn(a, b):
    return lax.dot_general(a.astype(MXU), b.astype(MXU), (((0,), (0,)), ((), ())), preferred_element_type=F32)


def _group_sums_t(prod, group):
    tm, w = prod.shape
    sel = (lax.broadcasted_iota(jnp.int32, (w, 128), 0) // group
           == lax.broadcasted_iota(jnp.int32, (w, 128), 1)).astype(MXU)
    hi = prod.astype(MXU)
    lo = prod - hi.astype(F32)
    return (_mm(hi, sel) + _mm(lo, sel)).T


def _sigmoid(z):
    return 1.0 / (1.0 + jnp.exp(-z))


def _silu(z):
    return z * _sigmoid(z)


def _rms(x):
    return lax.rsqrt(jnp.mean(x * x, axis=-1, keepdims=True) + EPS)


def _swap_halves(y, group):
    n = y.shape[-1]
    half = group // 2
    fwd = pltpu.roll(y, half, 1)
    if n == group:
        return fwd
    back = pltpu.roll(y, n - half, 1)
    lane = lax.broadcasted_iota(jnp.int32, y.shape, 1)
    return jnp.where((lane % group) < half, back, fwd)


def _rope(y, cos, sin, group):
    return y * cos + _swap_halves(y, group) * sin


def _rope_t(d, cos, sin, group):
    return d * cos - _swap_halves(d, group) * sin


def _rms_bwd(dy, x, g):
    r = _rms(x)
    xhat = x * r
    dxhat = dy * g
    dx = r * (dxhat - xhat * jnp.mean(dxhat * xhat, axis=-1, keepdims=True))
    return dx, dy * xhat


def _group_mean(v, bd, group):
    hi = v.astype(MXU)
    lo = v - hi.astype(F32)
    return (_mm(hi, bd[...]) + _mm(lo, bd[...])) * (1.0 / group)


def _head_norm(x, g, bd, group):
    return x * lax.rsqrt(_group_mean(x * x, bd, group) + EPS) * g


def _head_norm_bwd(dy, x, g, bd, group):
    r = lax.rsqrt(_group_mean(x * x, bd, group) + EPS)
    xhat = x * r
    dxhat = dy * g
    dx = r * (dxhat - xhat * _group_mean(dxhat * xhat, bd, group))
    return dx, dy * xhat


def _params(sem, vmem=VMEM_LIMIT):
    return pltpu.CompilerParams(dimension_semantics=sem, vmem_limit_bytes=vmem)


def _row_spec(tm, w):
    return pl.BlockSpec((tm, w), lambda i: (i, 0))


def _full_spec(shape):
    nd = len(shape)
    return pl.BlockSpec(shape, lambda i: (0,) * nd)


def _head_spec(h, tm, w):
    return pl.BlockSpec((h, tm, w), lambda i: (0, i, 0))


def _headt_spec(h, w, tm):
    return pl.BlockSpec((h, w, tm), lambda i: (0, 0, i))


def _rows_spec(h, tm):
    return pl.BlockSpec((h, tm), lambda i: (0, i))


def _me():
    return lax.axis_index("x"), lax.axis_index("y"), lax.axis_index("c")


def _flat(p):
    return 4 * p[0] + 2 * p[1] + p[2]


def _peer(me, k):
    x, y, c = me
    return (1 - x if k & 4 else x, 1 - y if k & 2 else y, 1 - c if k & 1 else c)


MESH_ID = pl.DeviceIdType.MESH


class Gather:
    VMEM = pl.BlockSpec(memory_space=pltpu.VMEM)

    def __init__(self, shards):
        self.shards = list(shards)
        self.n = len(self.shards)
        self.out_shapes = tuple(jax.ShapeDtypeStruct((N_DEV,) + a.shape, a.dtype) for a in self.shards)
        self.in_specs = [Gather.VMEM] * self.n
        self.out_specs = (Gather.VMEM,) * self.n
        self.sems = [pltpu.SemaphoreType.DMA((7 * self.n,)), pltpu.SemaphoreType.DMA((7 * self.n,)),
                     pltpu.SemaphoreType.DMA((self.n,))]

    def _plan(self, x_refs, out_refs, sems):
        send_sems, recv_sems, local_sems = sems
        me = _me()
        x, y, c = me
        chips = [(1 - x, y), (x, 1 - y), (1 - x, 1 - y)]

        def copy(a, k, block, to, src=None):
            slot = out_refs[a].at[_flat(block)]
            return pltpu.make_async_remote_copy(
                src_ref=slot if src is None else src, dst_ref=slot, send_sem=send_sems.at[7 * a + k],
                recv_sem=recv_sems.at[7 * a + k], device_id=to, device_id_type=MESH_ID)

        mine = [pltpu.make_async_copy(x_refs[a], out_refs[a].at[_flat(me)], local_sems.at[a]) for a in range(self.n)]
        first = [copy(a, 0, me, (x, y, 1 - c), src=x_refs[a]) for a in range(self.n)]
        first += [copy(a, 1 + j, me, (*chip, c), src=x_refs[a]) for a in range(self.n) for j, chip in enumerate(chips)]
        return me, chips, copy, mine, first

    def start(self, x_refs, out_refs, sems):
        _, _, _, mine, first = self._plan(x_refs, out_refs, sems)
        for cp in mine + first:
            cp.start()

    def forward(self, x_refs, out_refs, sems):
        me, chips, copy, _, _ = self._plan(x_refs, out_refs, sems)
        x, y, c = me
        for a in range(self.n):
            for j, chip in enumerate(chips):
                copy(a, 1 + j, (*chip, c), me).wait_recv()
                copy(a, 4 + j, (*chip, c), (x, y, 1 - c)).start()

    def drain(self, x_refs, out_refs, sems):
        me, chips, copy, mine, first = self._plan(x_refs, out_refs, sems)
        x, y, c = me
        sibling = (x, y, 1 - c)
        for a in range(self.n):
            copy(a, 0, sibling, me).wait_recv()
            for j, chip in enumerate(chips):
                copy(a, 4 + j, (*chip, 1 - c), me).wait_recv()
        for cp in first + [copy(a, 4 + j, (*chip, c), sibling) for a in range(self.n) for j, chip in enumerate(chips)]:
            cp.wait_send()
        for cp in mine:
            cp.wait()

    def finish(self, x_refs, out_refs, sems):
        self.forward(x_refs, out_refs, sems)
        self.drain(x_refs, out_refs, sems)


def all_gather_slots(gather, name):
    def body(*refs):
        x_refs, out_refs, sems = refs[:gather.n], refs[gather.n:2 * gather.n], refs[2 * gather.n:]
        gather.start(x_refs, out_refs, sems)
        gather.finish(x_refs, out_refs, sems)

    return pl.pallas_call(
        body, name=name, out_shape=gather.out_shapes, in_specs=gather.in_specs, out_specs=gather.out_specs,
        scratch_shapes=list(gather.sems), compiler_params=pltpu.CompilerParams(vmem_limit_bytes=VMEM_LIMIT),
    )(*gather.shards)


class Exchange:
    HBM = pl.BlockSpec(memory_space=pl.ANY)

    def __init__(self, srcs, scatter):
        self.srcs = list(srcs)
        self.scatter = scatter
        self.n = len(self.srcs)
        self.land_shapes = tuple(jax.ShapeDtypeStruct((N_DEV,) + tuple(a.shape[-2:]), a.dtype) for a in self.srcs)
        self.in_specs = [Exchange.HBM] * self.n
        self.out_specs = (Exchange.HBM,) * self.n
        self.sems = [pltpu.SemaphoreType.DMA((N_DEV - 1,)), pltpu.SemaphoreType.DMA((N_DEV - 1,)),
                     pltpu.SemaphoreType.DMA] * self.n

    def _copies(self, src_refs, land_refs, sems):
        me = _me()
        mi = _flat(me)
        local, sends, recvs = [], [], []
        for a, (src_ref, land_ref) in enumerate(zip(src_refs, land_refs)):
            send_sems, recv_sems, local_sem = sems[3 * a:3 * a + 3]
            pick = (lambda p, r=src_ref: r.at[_flat(p)]) if self.scatter else (lambda p, r=src_ref: r)
            local.append(pltpu.make_async_copy(pick(me), land_ref.at[mi], local_sem))
            for k in range(1, N_DEV):
                peer = _peer(me, k)
                pair = dict(send_sem=send_sems.at[k - 1], recv_sem=recv_sems.at[k - 1], device_id=peer,
                            device_id_type=MESH_ID)
                sends.append(pltpu.make_async_remote_copy(src_ref=pick(peer), dst_ref=land_ref.at[mi], **pair))
                recvs.append(pltpu.make_async_remote_copy(src_ref=pick(peer), dst_ref=land_ref.at[_flat(peer)],
                                                          **pair))
        return local, sends, recvs

    def start(self, src_refs, land_refs, sems):
        local, sends, _ = self._copies(src_refs, land_refs, sems)
        for cp in local + sends:
            cp.start()

    def wait(self, src_refs, land_refs, sems):
        local, sends, recvs = self._copies(src_refs, land_refs, sems)
        for cp in recvs:
            cp.wait_recv()
        for cp in sends:
            cp.wait_send()
        for cp in local:
            cp.wait()


def ada_forward(c8, ada_w, bias_cols, gather):
    d = c8.shape[1]
    w = ada_w.shape[2]
    ng = gather.n

    def body(*refs):
        c_ref, w_ref, b_ref = refs[:3]
        gx_refs = refs[3:3 + ng]
        call_ref, modp_ref = refs[3 + ng:5 + ng]
        gout_refs = refs[5 + ng:5 + 2 * ng]
        part_ref, s1, r1, s2, r2 = refs[5 + 2 * ng:10 + 2 * ng]
        g_sems = refs[10 + 2 * ng:]
        me = _me()
        mi = _flat(me)
        call_ref[mi] = c_ref[...]
        rows_out = []
        for k in range(1, N_DEV):
            rows_out.append(pltpu.make_async_remote_copy(
                src_ref=c_ref, dst_ref=call_ref.at[mi], send_sem=s1.at[k - 1], recv_sem=r1.at[k - 1],
                device_id=_peer(me, k), device_id_type=MESH_ID))
        for cp in rows_out:
            cp.start()
        gather.start(gx_refs, gout_refs, g_sems)
        for k in range(1, N_DEV):
            pltpu.make_async_remote_copy(
                src_ref=c_ref, dst_ref=call_ref.at[_flat(_peer(me, k))], send_sem=s1.at[k - 1],
                recv_sem=r1.at[k - 1], device_id=_peer(me, k), device_id_type=MESH_ID).wait_recv()
        ca = _silu(call_ref[...].reshape(N_DEV * 8, d))
        for l in range(2):
            part = _mm(ca, w_ref[l]) + b_ref[l]
            for b in range(N_DEV):
                part_ref[b, l] = part[8 * b:8 * b + 8, :]
        modp_ref[mi] = part_ref[mi]
        spread = []
        for k in range(1, N_DEV):
            peer = _peer(me, k)
            spread.append(pltpu.make_async_remote_copy(
                src_ref=part_ref.at[_flat(peer)], dst_ref=modp_ref.at[mi], send_sem=s2.at[k - 1],
                recv_sem=r2.at[k - 1], device_id=peer, device_id_type=MESH_ID))
        for cp in spread:
            cp.start()
        gather.forward(gx_refs, gout_refs, g_sems)
        for k in range(1, N_DEV):
            pi = _flat(_peer(me, k))
            pltpu.make_async_remote_copy(
                src_ref=part_ref.at[pi], dst_ref=modp_ref.at[pi], send_sem=s2.at[k - 1],
                recv_sem=r2.at[k - 1], device_id=_peer(me, k), device_id_type=MESH_ID).wait_recv()
        for cp in rows_out + spread:
            cp.wait_send()
        gather.drain(gx_refs, gout_refs, g_sems)

    vm = pl.BlockSpec(memory_space=pltpu.VMEM)
    res = pl.pallas_call(
        body, name="ada_forward",
        out_shape=(jax.ShapeDtypeStruct((N_DEV, 8, d), F32), jax.ShapeDtypeStruct((N_DEV, 2, 8, w), F32))
        + gather.out_shapes,
        in_specs=[vm, vm, vm] + gather.in_specs, out_specs=(vm, vm) + gather.out_specs,
        scratch_shapes=[pltpu.VMEM((N_DEV, 2, 8, w), F32)] + [pltpu.SemaphoreType.DMA((7,))] * 4 + list(gather.sems),
        compiler_params=pltpu.CompilerParams(vmem_limit_bytes=VMEM_LIMIT),
    )(c8, ada_w, bias_cols, *gather.shards)
    return res[0], res[1], res[2:]


def _modulated(x, mod_ref, nw_ref):
    xn = x * _rms(x)
    g1 = nw_ref[...] * (1.0 + mod_ref[1:2, :])
    return xn, g1, xn * g1 + mod_ref[0:1, :]


def even_in_forward(x, mod, nw, w_in_t, gq, gk, qln, kvln, w_uq_t, uk_bd, bd, cos_a, sin_a, cos_t, sin_t):
    s, d = x.shape
    tm = min(ROW_TILE, s)
    n_nope = B_HEADS * B_NOPE

    def body(x_ref, mod_ref, nw_ref, w_ref, gq_ref, gk_ref, qln_ref, kvln_ref, uq_ref, ukbd_ref, bd_ref,
             ca_ref, sa_ref, ct_ref, st_ref,
             qa_o, ka_o, va_o, qb_o, kb_o, kat_o, vat_o, kbt_o, qa_raw_o, ka_raw_o, cq_raw_o, ckv_raw_o, ga_o, gb_o):
        _, _, h = _modulated(x_ref[...], mod_ref, nw_ref)
        h = h.astype(MXU)

        def proj(cols):
            return _mm_nt(h, w_ref[cols[0]:cols[1], :])

        ca, sa, ct, st = ca_ref[...], sa_ref[...], ct_ref[...], st_ref[...]
        wide = lambda t, n: jnp.concatenate([t] * n, axis=1)
        qa = proj(E_QA)
        qa_raw_o[...] = qa
        qr = _rope(_head_norm(qa, gq_ref[...], bd_ref, HD), wide(ca, 4), wide(sa, 4), 32) * SCALE2_A
        for hh in range(A_HEADS):
            qa_o[hh] = qr[:, HD * hh:HD * hh + HD].astype(MXU)
        ka = proj(E_KA)
        ka_raw_o[...] = ka
        kr = _rope(_head_norm(ka, gk_ref[...], bd_ref[0:128, 0:128], HD), ca, sa, 32)
        va = proj(E_VA)
        krt, vat = kr.T, va.T
        for g in range(A_KV):
            ka_o[g] = kr[:, HD * g:HD * g + HD].astype(MXU)
            va_o[g] = va[:, HD * g:HD * g + HD].astype(MXU)
            kat_o[g] = krt[HD * g:HD * g + HD, :].astype(MXU)
            vat_o[g] = vat[HD * g:HD * g + HD, :].astype(MXU)
        ga_o[...] = proj(E_GA).astype(MXU)
        gb_o[...] = proj(E_GB).astype(MXU)
        cq = proj(E_CQ)
        cq_raw_o[...] = cq
        qb = _mm_nt(cq * _rms(cq) * qln_ref[...], uq_ref[...])
        q_lat = _mm(qb[:, 0:n_nope], ukbd_ref[...]) * SCALE2_B
        q_rope = _rope(qb[:, n_nope:], wide(ct, 2), wide(st, 2), 32) * SCALE2_B
        for hh in range(B_HEADS):
            qb_o[hh, :, 0:B_KV_LORA] = q_lat[:, B_KV_LORA * hh:B_KV_LORA * (hh + 1)].astype(MXU)
            qb_o[hh, :, B_KV_LORA:B_QK] = q_rope[:, B_ROPE * hh:B_ROPE * (hh + 1)].astype(MXU)
        ckv = proj(E_CKV)
        ckv_raw_o[...] = ckv
        ckv_n = ckv * _rms(ckv) * kvln_ref[...]
        k_rope = _rope(proj(E_KR), ct[:, 0:B_ROPE], st[:, 0:B_ROPE], 32)
        kb_o[0, :, 0:B_KV_LORA] = ckv_n.astype(MXU)
        kb_o[0, :, B_KV_LORA:B_QK] = k_rope.astype(MXU)
        kbt_o[0, 0:B_KV_LORA, :] = ckv_n.T.astype(MXU)
        kbt_o[0, B_KV_LORA:B_QK, :] = k_rope.T.astype(MXU)

    sd = jax.ShapeDtypeStruct
    outs = (sd((A_HEADS, s, HD), MXU), sd((A_KV, s, HD), MXU), sd((A_KV, s, HD), MXU),
            sd((B_HEADS, s, B_QK), MXU), sd((1, s, B_QK), MXU),
            sd((A_KV, HD, s), MXU), sd((A_KV, HD, s), MXU), sd((1, B_QK, s), MXU),
            sd((s, 512), F32), sd((s, 128), F32), sd((s, B_Q_LORA), F32), sd((s, B_KV_LORA), F32),
            sd((s, 512), MXU), sd((s, 512), MXU))
    out_specs = (_head_spec(A_HEADS, tm, HD), _head_spec(A_KV, tm, HD), _head_spec(A_KV, tm, HD),
                 _head_spec(B_HEADS, tm, B_QK), _head_spec(1, tm, B_QK),
                 _headt_spec(A_KV, HD, tm), _headt_spec(A_KV, HD, tm), _headt_spec(1, B_QK, tm),
                 _row_spec(tm, 512), _row_spec(tm, 128), _row_spec(tm, B_Q_LORA), _row_spec(tm, B_KV_LORA),
                 _row_spec(tm, 512), _row_spec(tm, 512))
    consts = [mod, nw, w_in_t, gq, gk, qln, kvln, w_uq_t, uk_bd, bd]
    return pl.pallas_call(
        body, name="even_in_forward", grid=(s // tm,), out_shape=outs,
        in_specs=[_row_spec(tm, d)] + [_full_spec(a.shape) for a in consts] + [_row_spec(tm, 128)] * 4,
        out_specs=out_specs, compiler_params=_params(("parallel",)),
    )(x, *consts, cos_a, sin_a, cos_t, sin_t)


def odd_in_forward(x, mod, nw, w_in):
    s, d = x.shape
    tm = min(ROW_TILE, s)

    def body(x_ref, mod_ref, nw_ref, w_ref, q_o, k_o, v_o, kt_o, vt_o, g_o):
        _, _, h = _modulated(x_ref[...], mod_ref, nw_ref)
        h = h.astype(MXU)

        def proj(cols):
            return _mm_nt(h, w_ref[cols[0]:cols[1], :])

        q = proj(O_Q) * SCALE2_A
        for hh in range(C_HEADS):
            q_o[hh] = q[:, HD * hh:HD * hh + HD].astype(MXU)
        k = proj(O_K)
        v = proj(O_V)
        for g in range(C_KV):
            kh = k[:, HD * g:HD * g + HD]
            vh = v[:, HD * g:HD * g + HD]
            k_o[g] = kh.astype(MXU)
            v_o[g] = vh.astype(MXU)
            kt_o[g] = kh.T.astype(MXU)
            vt_o[g] = vh.T.astype(MXU)
        g_o[...] = proj(O_G).astype(MXU)

    sd = jax.ShapeDtypeStruct
    return pl.pallas_call(
        body, name="odd_in_forward", grid=(s // tm,),
        out_shape=(sd((C_HEADS, s, HD), MXU), sd((C_KV, s, HD), MXU), sd((C_KV, s, HD), MXU),
                   sd((C_KV, HD, s), MXU), sd((C_KV, HD, s), MXU), sd((s, 1024), MXU)),
        in_specs=[_row_spec(tm, d), _full_spec(mod.shape), _full_spec(nw.shape), _full_spec(w_in.shape)],
        out_specs=(_head_spec(C_HEADS, tm, HD), _head_spec(C_KV, tm, HD), _head_spec(C_KV, tm, HD),
                   _headt_spec(C_KV, HD, tm), _headt_spec(C_KV, HD, tm), _row_spec(tm, 1024)),
        compiler_params=_params(("parallel",)),
    )(x, mod, nw, w_in)


def latent_out_forward(o_lat, w_uv):
    s = o_lat.shape[0]
    tm = min(ROW_TILE, s)

    def body(o_ref, uv_ref, out_ref):
        for hh in range(B_HEADS):
            out_ref[:, HD * hh:HD * hh + HD] = _mm(o_ref[:, B_KV_LORA * hh:B_KV_LORA * (hh + 1)],
                                                   uv_ref[hh]).astype(MXU)

    return pl.pallas_call(
        body, name="latent_out_forward", grid=(s // tm,),
        out_shape=jax.ShapeDtypeStruct((s, B_HEADS * HD), MXU),
        in_specs=[_row_spec(tm, o_lat.shape[1]), _full_spec(w_uv.shape)],
        out_specs=_row_spec(tm, B_HEADS * HD),
        compiler_params=_params(("parallel",)),
    )(o_lat, w_uv)


def mixer_out_forward(x, mod, pairs, w_out, name, loss=None):
    s, d = x.shape
    tm = min(ROW_TILE, s)
    n = len(pairs)
    widths = [o.shape[1] for o, _ in pairs]
    head = loss is not None

    def body(*refs):
        x_ref, mod_ref, w_ref = refs[:3]
        pr = refs[3:3 + 2 * n]
        rest = refs[3 + 2 * n:]
        y = jnp.zeros((tm, d), F32)
        r0 = 0
        for i in range(n):
            mix = pr[2 * i][...].astype(F32) * _silu(pr[2 * i + 1][...].astype(F32))
            y = y + _mm(mix, w_ref[r0:r0 + widths[i], :])
            r0 += widths[i]
        x_out = x_ref[...] + mod_ref[2:3, :] * y
        if not head:
            xo_ref, y_ref = rest
            xo_ref[...] = x_out
        else:
            t_ref, fn_ref, dx_ref, y_ref, lp_ref, dw_ref = rest

            @pl.when(pl.program_id(0) == 0)
            def _():
                lp_ref[...] = jnp.zeros(lp_ref.shape, F32)
                dw_ref[...] = jnp.zeros(dw_ref.shape, F32)

            g = fn_ref[...]
            err = x_out * _rms(x_out) * g - t_ref[...]
            lp_ref[...] += jnp.sum(err * err, axis=0, keepdims=True)
            dx, dg = _rms_bwd(err * (1.0 / d), x_out, g)
            dx_ref[...] = dx
            dw_ref[...] += jnp.sum(dg, axis=0, keepdims=True)
        y_ref[...] = y.astype(y_ref.dtype)

    flat = [a for p in pairs for a in p]
    sd = jax.ShapeDtypeStruct
    in_specs = [_row_spec(tm, d), _full_spec(mod.shape), _full_spec(w_out.shape)]
    in_specs += [_row_spec(tm, a.shape[1]) for a in flat]
    out_shape = (sd((s, d), F32), sd((s, d), MXU))
    out_specs = (_row_spec(tm, d), _row_spec(tm, d))
    if head:
        in_specs += [_row_spec(tm, d), _full_spec(loss[1].shape)]
        out_shape += (sd((1, d), F32), sd((1, d), F32))
        out_specs += (_full_spec((1, d)), _full_spec((1, d)))
    return pl.pallas_call(
        body, name=name, grid=(s // tm,), out_shape=out_shape, in_specs=in_specs, out_specs=out_specs,
        compiler_params=_params(("arbitrary",) if head else ("parallel",)),
    )(x, mod, w_out, *flat, *(loss if head else ()))


ONES_ROWS = 16
AHEAD = 2


def _col_max8(s3):
    m8 = jnp.max(s3, axis=0)
    return jnp.broadcast_to(jnp.max(m8, axis=0, keepdims=True), m8.shape)


def _with_ones(vt, n):
    return jnp.concatenate([vt, jnp.ones((ONES_ROWS, n), vt.dtype)], axis=0)


def _grid_edges(grid):
    ids = [pl.program_id(a) for a in range(len(grid))]
    first = functools.reduce(jnp.logical_and, [i == 0 for i in ids])
    last = functools.reduce(jnp.logical_and, [i == n - 1 for i, n in zip(ids, grid)])
    return first, last


def flash_forward(q, k, vt, *, dv, tq, tk, nsub, name, exchange=None):
    hq, s, dq = q.shape
    g_kv = k.shape[0]
    hpg = hq // g_kv
    nq = s // tq
    tkk = tk * nsub
    nk = s // tkk
    grid = (g_kv, nq, nk)
    hosted = exchange is not None
    m_cols = hpg * tq
    dvp = dv + ONES_ROWS

    def body(*refs):
        nx = exchange.n if hosted else 0
        q_ref, k_ref, vt_ref = refs[:3]
        xs_refs = refs[3:3 + nx]
        o_ref, lse_ref = refs[3 + nx:5 + nx]
        land_refs = refs[5 + nx:5 + 2 * nx]
        m_s, acc_s = refs[5 + 2 * nx:7 + 2 * nx]
        sems = refs[7 + 2 * nx:]
        if hosted:
            first, last = _grid_edges(grid)
            pl.when(first)(lambda: exchange.start(xs_refs, land_refs, sems))
        j = pl.program_id(2)

        @pl.when(j == 0)
        def _():
            m_s[...] = jnp.full((8, m_cols), -jnp.inf, F32)
            acc_s[...] = jnp.zeros((dvp, m_cols), F32)

        qq = q_ref[...].reshape(m_cols, dq)
        score = lambda u: _mm_nt(k_ref[0, tk * u:tk * (u + 1), :], qq).reshape(tk // 8, 8, m_cols)
        sts = {u: score(u) for u in range(min(AHEAD, nsub))}
        m_run = m_s[...]
        acc = acc_s[...]
        for u in range(nsub):
            if u + AHEAD < nsub:
                sts[u + AHEAD] = score(u + AHEAD)
            st = sts.pop(u)
            m_new = jnp.maximum(m_run, _col_max8(st))
            p = jnp.exp2(st - m_new[None])
            alpha = jnp.exp2(m_run - m_new)
            pv = _mm(_with_ones(vt_ref[0, 0:dv, tk * u:tk * (u + 1)], tk), p.reshape(tk, m_cols))
            acc = (acc.reshape(dvp // 8, 8, m_cols) * alpha[None]).reshape(dvp, m_cols) + pv
            m_run = m_new
        acc_s[...] = acc
        m_s[...] = m_run

        @pl.when(j == nk - 1)
        def _():
            l = acc_s[dv:dv + 1, :]
            ot = acc_s[0:dv, :] / l
            lse = m_s[0:1, :] + jnp.log2(l)
            for hh in range(hpg):
                o_ref[:, dv * hh:dv * hh + dv] = ot[:, tq * hh:tq * hh + tq].T.astype(MXU)
                lse_ref[hh] = lse[:, tq * hh:tq * hh + tq]

        if hosted:
            pl.when(last)(lambda: exchange.wait(xs_refs, land_refs, sems))

    sd = jax.ShapeDtypeStruct
    return pl.pallas_call(
        body, name=name, grid=grid,
        out_shape=(sd((s, hq * dv), MXU), sd((hq, 1, s), F32)) + (exchange.land_shapes if hosted else ()),
        in_specs=[pl.BlockSpec((hpg, tq, dq), lambda g, i, j: (g, i, 0)),
                  pl.BlockSpec((1, tkk, k.shape[2]), lambda g, i, j: (g, j, 0)),
                  pl.BlockSpec((1, dv, tkk), lambda g, i, j: (g, 0, j))] + (exchange.in_specs if hosted else []),
        out_specs=(pl.BlockSpec((tq, hpg * dv), lambda g, i, j: (i, g)),
                   pl.BlockSpec((hpg, 1, tq), lambda g, i, j: (g, 0, i))) + (exchange.out_specs if hosted else ()),
        scratch_shapes=[pltpu.VMEM((8, m_cols), F32), pltpu.VMEM((dvp, m_cols), F32)]
        + (list(exchange.sems) if hosted else []),
        compiler_params=_params(("arbitrary",) * 3 if hosted else ("parallel", "parallel", "arbitrary")),
    )(q, k, vt, *(exchange.srcs if hosted else []))


def _window_bias_t(hpg, slope_ref):
    t = WINDOW
    r = lax.broadcasted_iota(jnp.int32, (3 * t, t), 0)
    cq = lax.broadcasted_iota(jnp.int32, (3 * t, t), 1)
    arel = jnp.abs(r - t - cq)
    base = jnp.where(arel <= WINDOW, arel.astype(F32) * (-LOG2E), -jnp.inf)
    return jnp.concatenate([base * slope_ref[hh] for hh in range(hpg)], axis=1)


def _window_edges_t(bias, no_before, no_after):
    t = WINDOW
    r = lax.broadcasted_iota(jnp.int32, bias.shape, 0)
    out = ((r < t) & no_before) | ((r >= 2 * t) & no_after)
    return jnp.where(out, -jnp.inf, bias)


def _window_specs(kind, nb, nblk, d):
    t = WINDOW
    before = lambda i: jnp.clip(i * nb - 1, 0, nblk - 1)
    after = lambda i: jnp.clip((i + 1) * nb, 0, nblk - 1)
    if kind == "rows":
        return [pl.BlockSpec((1, t, d), lambda g, i: (g, before(i), 0)),
                pl.BlockSpec((1, nb * t, d), lambda g, i: (g, i, 0)),
                pl.BlockSpec((1, t, d), lambda g, i: (g, after(i), 0))]
    return [pl.BlockSpec((1, d, t), lambda g, i: (g, 0, before(i))),
            pl.BlockSpec((1, d, nb * t), lambda g, i: (g, 0, i)),
            pl.BlockSpec((1, d, t), lambda g, i: (g, 0, after(i)))]


def window_forward(q, k, vt, sink2, slopes, nb, name):
    hq, s, d = q.shape
    g_kv = k.shape[0]
    hpg = hq // g_kv
    t = WINDOW
    nblk = s // t
    steps = nblk // nb
    m_cols = hpg * t

    def body(q_ref, kp, ko, kn, vp, vo, vn, sink_ref, slope_ref, o_ref, lse_ref):
        i = pl.program_id(1)
        kk_all = jnp.concatenate([kp[0], ko[0], kn[0]], axis=0)
        vt_all = jnp.concatenate([vp[0], vo[0], vn[0]], axis=1)
        bias = _window_bias_t(hpg, slope_ref)
        sink_row = jnp.concatenate([jnp.broadcast_to(sink_ref[hh], (8, t)) for hh in range(hpg)], axis=1)
        sts = {}

        def score(u):
            qq = q_ref[:, t * u:t * (u + 1), :].reshape(m_cols, d)
            b_u = bias
            if u == 0 or u == nb - 1:
                b_u = _window_edges_t(bias, (i == 0) if u == 0 else False,
                                      (i == steps - 1) if u == nb - 1 else False)
            sts[u] = _mm_nt(kk_all[t * u:t * (u + 3), :], qq) + b_u

        for u in range(min(AHEAD, nb)):
            score(u)
        for u in range(nb):
            if u + AHEAD < nb:
                score(u + AHEAD)
            s3 = sts.pop(u).reshape(3 * t // 8, 8, m_cols)
            m8 = jnp.maximum(_col_max8(s3), sink_row)
            p = jnp.exp2(s3 - m8[None]).reshape(3 * t, m_cols)
            acc = _mm(_with_ones(vt_all[:, t * u:t * (u + 3)], 3 * t), p)
            l = acc[d:d + 1, :] + jnp.exp2(sink_row[0:1, :] - m8[0:1, :])
            ot = acc[0:d, :] / l
            lse = m8[0:1, :] + jnp.log2(l)
            for hh in range(hpg):
                o_ref[t * u:t * (u + 1), d * hh:d * hh + d] = ot[:, t * hh:t * hh + t].T.astype(MXU)
                lse_ref[hh, :, t * u:t * (u + 1)] = lse[:, t * hh:t * hh + t]

    sd = jax.ShapeDtypeStruct
    return pl.pallas_call(
        body, name=name, grid=(g_kv, steps),
        out_shape=(sd((s, hq * d), MXU), sd((hq, 1, s), F32)),
        in_specs=[pl.BlockSpec((hpg, nb * t, d), lambda g, i: (g, i, 0))]
        + _window_specs("rows", nb, nblk, d) + _window_specs("cols", nb, nblk, d)
        + [pl.BlockSpec((hpg, 1, 1), lambda g, i: (g, 0, 0))] * 2,
        out_specs=(pl.BlockSpec((nb * t, hpg * d), lambda g, i: (i, g)),
                   pl.BlockSpec((hpg, 1, nb * t), lambda g, i: (g, 0, i))),
        compiler_params=_params(("parallel", "parallel")),
    )(q, k, k, k, vt, vt, vt, sink2, slopes)


def window_backward(q, k, kt, v, do, lse, delta, slopes, nb, name):
    hq, s, d = q.shape
    g_kv = k.shape[0]
    hpg = hq // g_kv
    t = WINDOW
    nblk = s // t
    steps = nblk // nb
    m_cols = hpg * t

    def body(q_ref, kp, ko, kn, ktp, kto, ktn, vp, vo, vn, do_ref, lse_ref, dl_ref, slope_ref,
             dq_ref, dk_ref, dv_ref, dk_s, dv_s):
        i = pl.program_id(1)

        @pl.when(i == 0)
        def _():
            dk_ref[...] = jnp.zeros(dk_ref.shape, F32)
            dv_ref[...] = jnp.zeros(dv_ref.shape, F32)

        dk_s[...] = jnp.zeros(dk_s.shape, F32)
        dv_s[...] = jnp.zeros(dv_s.shape, F32)
        kk_all = jnp.concatenate([kp[0], ko[0], kn[0]], axis=0)
        vv_all = jnp.concatenate([vp[0], vo[0], vn[0]], axis=0)
        kkt_all = jnp.concatenate([ktp[0], kto[0], ktn[0]], axis=1)
        bias = _window_bias_t(hpg, slope_ref)
        qqs, dds, sts, dps = {}, {}, {}, {}

        def issue(u):
            rows = slice(t * u, t * (u + 1))
            keys = slice(t * u, t * (u + 3))
            qqs[u] = q_ref[:, rows, :].reshape(m_cols, d)
            dds[u] = jnp.concatenate([do_ref[rows, d * hh:d * hh + d] for hh in range(hpg)], axis=0)
            b_u = bias
            if u == 0 or u == nb - 1:
                b_u = _window_edges_t(bias, (i == 0) if u == 0 else False,
                                      (i == steps - 1) if u == nb - 1 else False)
            sts[u] = _mm_nt(kk_all[keys, :], qqs[u]) + b_u
            dps[u] = _mm_nt(vv_all[keys, :], dds[u])

        for u in range(min(AHEAD, nb)):
            issue(u)
        for u in range(nb):
            if u + AHEAD < nb:
                issue(u + AHEAD)
            rows = slice(t * u, t * (u + 1))
            keys = slice(t * u, t * (u + 3))
            lse_row = jnp.concatenate([lse_ref[hh, :, rows] for hh in range(hpg)], axis=1)
            dl_row = jnp.concatenate([dl_ref[hh, :, rows] for hh in range(hpg)], axis=1)
            p = jnp.exp2(sts[u] - lse_row)
            ds = p * (dps[u] - dl_row) * SCALE_A
            dv_s[keys, :] += _mm(p, dds[u])
            dk_s[keys, :] += _mm(ds, qqs[u])
            dqt = _mm(kkt_all[:, keys], ds)
            for hh in range(hpg):
                dq_ref[rows, d * hh:d * hh + d] = dqt[:, t * hh:t * hh + t].T.astype(dq_ref.dtype)
        tq = nb * t
        for src, r0, n in ((0, jnp.clip(i * nb - 1, 0, nblk - 1) * t, t), (t, i * tq, tq),
                           (t + tq, jnp.clip((i + 1) * nb, 0, nblk - 1) * t, t)):
            dst = pl.ds(pl.multiple_of(r0, t), n)
            dk_ref[0, dst, :] += dk_s[src:src + n, :] * (1.0 / SCALE2_A)
            dv_ref[0, dst, :] += dv_s[src:src + n, :]

    row_map = lambda g, i: (g, 0, i)
    sd = jax.ShapeDtypeStruct
    return pl.pallas_call(
        body, name=name, grid=(g_kv, steps),
        out_shape=(sd((s, hq * d), MXU), sd((g_kv, s, d), F32), sd((g_kv, s, d), F32)),
        in_specs=[pl.BlockSpec((hpg, nb * t, d), lambda g, i: (g, i, 0))]
        + _window_specs("rows", nb, nblk, d) + _window_specs("cols", nb, nblk, d) + _window_specs("rows", nb, nblk, d)
        + [pl.BlockSpec((nb * t, hpg * d), lambda g, i: (i, g)), pl.BlockSpec((hpg, 1, nb * t), row_map),
           pl.BlockSpec((hpg, 1, nb * t), row_map), pl.BlockSpec((hpg, 1, 1), lambda g, i: (g, 0, 0))],
        out_specs=(pl.BlockSpec((nb * t, hpg * d), lambda g, i: (i, g)),
                   pl.BlockSpec((1, s, d), lambda g, i: (g, 0, 0)),
                   pl.BlockSpec((1, s, d), lambda g, i: (g, 0, 0))),
        scratch_shapes=[pltpu.VMEM(((nb + 2) * t, d), F32), pltpu.VMEM(((nb + 2) * t, d), F32)],
        compiler_params=_params(("parallel", "arbitrary")),
    )(q, k, k, k, kt, kt, kt, v, v, v, do, lse, delta, slopes)


def flash_backward(q, k, kt, v, do, lse, delta, *, scale, dv, tq, tk, nsub, gq, name, split=None, exchange=None):
    hq, s, dq = q.shape
    g_kv = k.shape[0]
    hpg = hq // gq
    nq = s // tq
    tqq = tq * nsub
    nqs = s // tqq
    nkb = s // tk
    grid = (gq, nkb, nqs)
    hosted = exchange is not None
    m_cols = hpg * tq
    c = scale * LOG2E
    has_v = v is not None

    def body(*refs):
        it = iter(refs)
        q_ref, k_ref, kt_ref = next(it), next(it), next(it)
        v_ref = next(it) if has_v else None
        do_ref, lse_ref, dl_ref = next(it), next(it), next(it)
        nx = exchange.n if hosted else 0
        xs_refs = [next(it) for _ in range(nx)]
        dq_ref, dk_ref, dv_ref = next(it), next(it), next(it)
        land_refs = [next(it) for _ in range(nx)]
        dqt_s = next(it)
        sems = list(it)
        kj = pl.program_id(1)
        qi = pl.program_id(2)
        if hosted:
            first, last = _grid_edges(grid)
            pl.when(first)(lambda: exchange.start(xs_refs, land_refs, sems))

        @pl.when((kj == 0) & (qi == 0))
        def _():
            dqt_s[...] = jnp.zeros(dqt_s.shape, F32)

        @pl.when(qi == 0)
        def _():
            dk_ref[...] = jnp.zeros(dk_ref.shape, F32)
            dv_ref[...] = jnp.zeros(dv_ref.shape, F32)

        kk = k_ref[0]
        vv = v_ref[0] if has_v else kk[:, :dv]
        qqs, dds, sts, dps = {}, {}, {}, {}

        def issue(u):
            rows = slice(tq * u, tq * (u + 1))
            qqs[u] = q_ref[:, rows, :].reshape(m_cols, dq)
            dds[u] = jnp.concatenate([do_ref[rows, dv * hh:dv * hh + dv] for hh in range(hpg)], axis=0)
            sts[u] = _mm_nt(kk, qqs[u])
            dps[u] = _mm_nt(vv, dds[u])

        for u in range(min(AHEAD, nsub)):
            issue(u)
        dv_acc = dv_ref[0]
        dk_acc = dk_ref[0]
        for u in range(nsub):
            if u + AHEAD < nsub:
                issue(u + AHEAD)
            rows = slice(tq * u, tq * (u + 1))
            lse_row = jnp.concatenate([lse_ref[hh, :, rows] for hh in range(hpg)], axis=1)
            dl_row = jnp.concatenate([dl_ref[hh, :, rows] for hh in range(hpg)], axis=1)
            p = jnp.exp2(sts[u] - lse_row)
            ds = p * (dps[u] - dl_row) * scale
            dv_acc = dv_acc + _mm(p, dds[u])
            dk_acc = dk_acc + _mm(ds, qqs[u])
            dqt = _mm(kt_ref[0], ds)
            for hh in range(hpg):
                dqt_s[qi * nsub + u, dq * hh:dq * hh + dq, :] += dqt[:, tq * hh:tq * hh + tq]
        dv_ref[0] = dv_acc
        dk_ref[0] = jnp.where(qi == nqs - 1, dk_acc * (1.0 / c), dk_acc)

        @pl.when((kj == nkb - 1) & (qi == nqs - 1))
        def _():
            def emit(t, carry):
                r0 = pl.multiple_of(t * tq, tq)
                for hh in range(hpg):
                    blk = dqt_s[t, dq * hh:dq * hh + dq, :].T
                    if split is None:
                        dq_ref[pl.ds(r0, tq), dq * hh:dq * hh + dq] = blk
                    else:
                        rest = dq - split
                        dq_ref[pl.ds(r0, tq), split * hh:split * (hh + 1)] = blk[:, 0:split]
                        dq_ref[pl.ds(r0, tq), hpg * split + rest * hh:hpg * split + rest * (hh + 1)] = blk[:, split:]
                return carry

            lax.fori_loop(0, nq, emit, 0)

        if hosted:
            pl.when(last)(lambda: exchange.wait(xs_refs, land_refs, sems))

    kv_of = lambda g: g * g_kv // gq
    in_specs = [pl.BlockSpec((hpg, tqq, dq), lambda g, kj, qi: (g, qi, 0)),
                pl.BlockSpec((1, tk, dq), lambda g, kj, qi: (kv_of(g), kj, 0)),
                pl.BlockSpec((1, dq, tk), lambda g, kj, qi: (kv_of(g), 0, kj))]
    args = [q, k, kt]
    if has_v:
        in_specs.append(pl.BlockSpec((1, tk, dv), lambda g, kj, qi: (kv_of(g), kj, 0)))
        args.append(v)
    row_map = lambda g, kj, qi: (g, 0, qi)
    in_specs += [pl.BlockSpec((tqq, hpg * dv), lambda g, kj, qi: (qi, g)),
                 pl.BlockSpec((hpg, 1, tqq), row_map), pl.BlockSpec((hpg, 1, tqq), row_map)]
    args += [do, lse, delta]
    if hosted:
        in_specs += exchange.in_specs
        args += exchange.srcs
    sd = jax.ShapeDtypeStruct
    return pl.pallas_call(
        body, name=name, grid=grid,
        out_shape=(sd((s, hq * dq), F32), sd((gq, s, dq), F32), sd((gq, s, dv), F32))
        + (exchange.land_shapes if hosted else ()),
        in_specs=in_specs,
        out_specs=(pl.BlockSpec((s, hpg * dq), lambda g, kj, qi: (0, g)),
                   pl.BlockSpec((1, tk, dq), lambda g, kj, qi: (g, kj, 0)),
                   pl.BlockSpec((1, tk, dv), lambda g, kj, qi: (g, kj, 0))) + (exchange.out_specs if hosted else ()),
        scratch_shapes=[pltpu.VMEM((nq, hpg * dq, tq), F32)] + (list(exchange.sems) if hosted else []),
        compiler_params=_params(("arbitrary",) * 3 if hosted else ("parallel", "arbitrary", "arbitrary")),
    )(*args)


def mixer_out_backward(dx, y, mod, pairs, w_out, delta_heads, name, lse=None, sink=None):
    s, d = dx.shape
    tm = min(ROW_TILE, s)
    n = len(pairs)
    widths = [o.shape[1] for o, _ in pairs]
    n_delta = sum(1 for h in delta_heads if h)
    with_sink = lse is not None

    def body(*refs):
        it = iter(refs)
        dx_ref, y_ref, mod_ref, wt_ref = next(it), next(it), next(it), next(it)
        pr = [next(it) for _ in range(2 * n)]
        lse_ref = next(it) if with_sink else None
        sink_ref = next(it) if with_sink else None
        outs = [next(it) for _ in range(2 * n)]
        dl_refs = [next(it) for _ in range(n_delta)]
        dgate_ref, dw_ref = next(it), next(it)
        dsink_ref = next(it) if with_sink else None

        @pl.when(pl.program_id(0) == 0)
        def _():
            dgate_ref[...] = jnp.zeros(dgate_ref.shape, F32)
            dw_ref[...] = jnp.zeros(dw_ref.shape, F32)
            if with_sink:
                dsink_ref[...] = jnp.zeros(dsink_ref.shape, F32)

        dxo = dx_ref[...]
        dgate_ref[...] += jnp.sum(dxo * y_ref[...].astype(F32), axis=0, keepdims=True)
        dy = (dxo * mod_ref[2:3, :]).astype(MXU)
        dmix = _mm_nt(dy, wt_ref[...])
        r0 = 0
        di = 0
        for i in range(n):
            o = pr[2 * i][...].astype(F32)
            g = pr[2 * i + 1][...].astype(F32)
            dm = dmix[:, r0:r0 + widths[i]]
            sg = _sigmoid(g)
            act = g * sg
            do = dm * act
            outs[2 * i][...] = do.astype(MXU)
            outs[2 * i + 1][...] = (dm * o * (sg * (1.0 + g * (1.0 - sg)))).astype(MXU)
            dw_ref[r0:r0 + widths[i], :] += _mm_tn(o * act, dy)
            if delta_heads[i]:
                dlt = _group_sums_t(do * o, HD)[0:delta_heads[i], :]
                dl_refs[di][...] = dlt
                if with_sink:
                    ps = jnp.exp2(sink_ref[...] - lse_ref[...])
                    dsink_ref[...] += -jnp.sum(ps * dlt, axis=1, keepdims=True)
                di += 1
            r0 += widths[i]

    flat = [a for p in pairs for a in p]
    sd = jax.ShapeDtypeStruct
    in_specs = [_row_spec(tm, d), _row_spec(tm, d), _full_spec(mod.shape), _full_spec(w_out.shape)]
    in_specs += [_row_spec(tm, a.shape[1]) for a in flat]
    args = [dx, y, mod, w_out] + flat
    if with_sink:
        nh = lse.shape[0]
        in_specs += [_rows_spec(nh, tm), _full_spec(sink.shape)]
        args += [lse, sink]
    out_shape = [sd((s, a.shape[1]), MXU) for a in flat]
    out_specs = [_row_spec(tm, a.shape[1]) for a in flat]
    for h in delta_heads:
        if h:
            out_shape.append(sd((h, s), F32))
            out_specs.append(_rows_spec(h, tm))
    out_shape += [sd((1, d), F32), sd((sum(widths), d), F32)]
    out_specs += [_full_spec((1, d)), _full_spec((sum(widths), d))]
    if with_sink:
        out_shape.append(sd((lse.shape[0], 1), F32))
        out_specs.append(_full_spec((lse.shape[0], 1)))
    return pl.pallas_call(
        body, name=name, grid=(s // tm,), out_shape=tuple(out_shape), in_specs=in_specs, out_specs=tuple(out_specs),
        compiler_params=_params(("arbitrary",)),
    )(*args)


def latent_out_backward(d_ob, o_lat, w_uv):
    s = o_lat.shape[0]
    tm = min(ROW_TILE, s)

    def body(d_ref, o_ref, uv_ref, dol_ref, dl_ref, duv_ref, prod_s):
        @pl.when(pl.program_id(0) == 0)
        def _():
            duv_ref[...] = jnp.zeros(duv_ref.shape, F32)

        for hh in range(B_HEADS):
            dh = d_ref[:, HD * hh:HD * hh + HD]
            ol = o_ref[:, B_KV_LORA * hh:B_KV_LORA * (hh + 1)].astype(F32)
            dol = _mm_nt(dh, uv_ref[hh])
            dol_ref[:, B_KV_LORA * hh:B_KV_LORA * (hh + 1)] = dol.astype(MXU)
            prod_s[:, B_KV_LORA * hh:B_KV_LORA * (hh + 1)] = dol * ol
            duv_ref[:, HD * hh:HD * hh + HD] += _mm_tn(ol, dh)
        dl_ref[...] = _group_sums_t(prod_s[...], B_KV_LORA)[0:B_HEADS, :]

    sd = jax.ShapeDtypeStruct
    duv_shape = (B_KV_LORA, B_HEADS * HD)
    return pl.pallas_call(
        body, name="latent_out_backward", grid=(s // tm,),
        out_shape=(sd(o_lat.shape, MXU), sd((B_HEADS, s), F32), sd(duv_shape, F32)),
        in_specs=[_row_spec(tm, d_ob.shape[1]), _row_spec(tm, o_lat.shape[1]), _full_spec(w_uv.shape)],
        out_specs=(_row_spec(tm, o_lat.shape[1]), _rows_spec(B_HEADS, tm), _full_spec(duv_shape)),
        scratch_shapes=[pltpu.VMEM((tm, o_lat.shape[1]), F32)],
        compiler_params=_params(("arbitrary",)),
    )(d_ob, o_lat, w_uv)


def even_prep_backward(dqa, dka, dva, dqb, dkb, dvb, qa_raw, ka_raw, cq_raw, ckv_raw,
                       gq, gk, qln, kvln, w_uq_t, uk_bd, bd, cos_a, sin_a, cos_t, sin_t):
    s = qa_raw.shape[0]
    tm = min(ROW_TILE, s)
    half_lat = B_KV_LORA * B_HEADS // 2
    half_w = dqb.shape[1] // 2

    def body(dqa_ref, dka_ref, dva_ref, dqb_ref, dkb_ref, dvb_ref, qa_ref, ka_ref, cq_ref, ckv_ref,
             gq_ref, gk_ref, qln_ref, kvln_ref, uqt_ref, ukbd_ref, bd_ref, ca_ref, sa_ref, ct_ref, st_ref,
             pqa, pka, pva, pcq, pckv, pkr, gqn, gkn, gqln, gkvln, guq, guk):
        @pl.when(pl.program_id(0) == 0)
        def _():
            for r in (gqn, gkn, gqln, gkvln, guq, guk):
                r[...] = jnp.zeros(r.shape, F32)

        ca, sa, ct, st = ca_ref[...], sa_ref[...], ct_ref[...], st_ref[...]
        wide = lambda t, n: jnp.concatenate([t] * n, axis=1)
        rows = lambda a: jnp.sum(a, axis=0, keepdims=True)
        dx, dg = _head_norm_bwd(_rope_t(dqa_ref[...], wide(ca, 4), wide(sa, 4), 32), qa_ref[...], gq_ref[...],
                                bd_ref, HD)
        pqa[...] = dx.astype(MXU)
        gqn[...] += rows(dg)
        dk_all = jnp.concatenate([dka_ref[g] for g in range(A_KV)], axis=1)
        dx, dg = _head_norm_bwd(_rope_t(dk_all, ca, sa, 32), ka_ref[...], gk_ref[...], bd_ref[0:128, 0:128], HD)
        pka[...] = dx.astype(MXU)
        gkn[...] += rows(dg)
        pva[...] = jnp.concatenate([dva_ref[g] for g in range(A_KV)], axis=1).astype(MXU)
        cq_raw = cq_ref[...]
        cq_n = cq_raw * _rms(cq_raw) * qln_ref[...]
        qb = _mm_nt(cq_n, uqt_ref[...])
        d_lat = jnp.concatenate([dqb_ref[:, 0:half_lat], dqb_ref[:, half_w:half_w + half_lat]], axis=1)
        d_rope = jnp.concatenate([dqb_ref[:, half_lat:half_w], dqb_ref[:, half_w + half_lat:]], axis=1)
        for hh in range(B_HEADS):
            guk[:, B_NOPE * hh:B_NOPE * (hh + 1)] += _mm_tn(d_lat[:, B_KV_LORA * hh:B_KV_LORA * (hh + 1)],
                                                            qb[:, B_NOPE * hh:B_NOPE * (hh + 1)])
        dqb_all = jnp.concatenate([_mm_nt(d_lat, ukbd_ref[...]),
                                   _rope_t(d_rope, wide(ct, 2), wide(st, 2), 32)], axis=1)
        guq[...] += _mm_tn(dqb_all, cq_n)
        dx, dg = _rms_bwd(_mm(dqb_all, uqt_ref[...]), cq_raw, qln_ref[...])
        pcq[...] = dx.astype(MXU)
        gqln[...] += rows(dg)
        dkb_sum = dkb_ref[0] + dkb_ref[1]
        dckv = dkb_sum[:, 0:B_KV_LORA] + dvb_ref[0] + dvb_ref[1]
        dx, dg = _rms_bwd(dckv, ckv_ref[...], kvln_ref[...])
        pckv[...] = dx.astype(MXU)
        gkvln[...] += rows(dg)
        pkr[...] = _rope_t(dkb_sum[:, B_KV_LORA:B_QK], ct[:, 0:B_ROPE], st[:, 0:B_ROPE], 32).astype(MXU)

    sd = jax.ShapeDtypeStruct
    consts = [gq, gk, qln, kvln, w_uq_t, uk_bd, bd]
    in_specs = [_row_spec(tm, 512), _head_spec(A_KV, tm, HD), _head_spec(A_KV, tm, HD),
                _row_spec(tm, dqb.shape[1]), _head_spec(2, tm, B_QK), _head_spec(2, tm, B_KV_LORA),
                _row_spec(tm, 512), _row_spec(tm, 128), _row_spec(tm, B_Q_LORA), _row_spec(tm, B_KV_LORA)]
    in_specs += [_full_spec(a.shape) for a in consts] + [_row_spec(tm, 128)] * 4
    small = [sd(gq.shape, F32), sd(gk.shape, F32), sd(qln.shape, F32), sd(kvln.shape, F32), sd(w_uq_t.shape, F32),
             sd((B_KV_LORA, B_HEADS * B_NOPE), F32)]
    out_shape = (sd((s, 512), MXU), sd((s, 128), MXU), sd((s, 128), MXU), sd((s, B_Q_LORA), MXU),
                 sd((s, B_KV_LORA), MXU), sd((s, B_ROPE), MXU), *small)
    out_specs = (_row_spec(tm, 512), _row_spec(tm, 128), _row_spec(tm, 128), _row_spec(tm, B_Q_LORA),
                 _row_spec(tm, B_KV_LORA), _row_spec(tm, B_ROPE), *[_full_spec(a.shape) for a in small])
    return pl.pallas_call(
        body, name="even_prep_backward", grid=(s // tm,), out_shape=out_shape, in_specs=in_specs, out_specs=out_specs,
        compiler_params=_params(("arbitrary",)),
    )(dqa, dka, dva, dqb, dkb, dvb, qa_raw, ka_raw, cq_raw, ckv_raw, *consts, cos_a, sin_a, cos_t, sin_t)


def in_proj_backward(x, mod, nw, pieces, name, *, dx_out=None, w_in_t=None, dw_rows=None, exchange=None):
    s, d = x.shape
    tm = min(ROW_TILE, s)
    grid = (s // tm,)
    n = len(pieces)
    cols = [c for _, c in pieces]
    want_dx = w_in_t is not None
    want_dw = dw_rows is not None
    n_cols = sum(c1 - c0 for c0, c1 in cols)
    hosted = exchange is not None
    nx = exchange.n if hosted else 0

    def body(*refs):
        it = iter(refs)
        x_ref, mod_ref, nw_ref = next(it), next(it), next(it)
        dxo_ref, wt_ref = (next(it), next(it)) if want_dx else (None, None)
        p_refs = [next(it) for _ in range(n)]
        xs_refs = [next(it) for _ in range(nx)]
        dx_ref, dv_ref = (next(it), next(it)) if want_dx else (None, None)
        dw_ref = next(it) if want_dw else None
        land_refs = [next(it) for _ in range(nx)]
        acc_ref = next(it) if want_dx else None
        dw_acc = next(it) if want_dw else None
        sems = list(it)
        first, last = _grid_edges(grid)
        if hosted:
            pl.when(first)(lambda: exchange.start(xs_refs, land_refs, sems))

        @pl.when(first)
        def _():
            if want_dw:
                dw_acc[...] = jnp.zeros(dw_acc.shape, F32)
            if want_dx:
                acc_ref[...] = jnp.zeros(acc_ref.shape, F32)

        xn, g1, h = _modulated(x_ref[...], mod_ref, nw_ref)
        hb = h.astype(MXU)
        dh = jnp.zeros((tm, d), F32)
        for k, (pr, (c0, c1)) in enumerate(zip(p_refs, cols)):
            if len(pr.shape) == 3:
                pc = jnp.concatenate([pr[g] for g in range(pr.shape[0])], axis=1).astype(MXU)
            else:
                pc = pr[...].astype(MXU)
            if want_dx:
                dh = dh + jnp.dot(pc, wt_ref[c0:c1, :], preferred_element_type=F32)
            if want_dw:
                r0, r1 = dw_rows[k]
                dw_acc[r0:r1, :] += _mm_tn(pc, hb)
        if want_dx:
            acc_ref[0:1, :] += jnp.sum(dh, axis=0, keepdims=True)
            acc_ref[1:2, :] += jnp.sum(dh * xn, axis=0, keepdims=True)
            dxn = dh * g1
            x = x_ref[...]
            dx_ref[...] = dxo_ref[...] + _rms(x) * (dxn - xn * jnp.mean(dxn * xn, axis=-1, keepdims=True))

        @pl.when(last)
        def _():
            if want_dx:
                dg1 = acc_ref[1:2, :]
                dv_ref[0:1, :] = acc_ref[0:1, :]
                dv_ref[1:2, :] = dg1 * nw_ref[...]
                dv_ref[2:3, :] = dg1 * (1.0 + mod_ref[1:2, :])
                dv_ref[3:4, :] = jnp.zeros((1, d), F32)
            if want_dw:
                dw_ref[...] = dw_acc[...].astype(MXU)

        if hosted:
            pl.when(last)(lambda: exchange.wait(xs_refs, land_refs, sems))

    arrs = [a for a, _ in pieces]
    sd = jax.ShapeDtypeStruct
    args = [x, mod, nw] + ([dx_out, w_in_t] if want_dx else []) + arrs + (exchange.srcs if hosted else [])
    in_specs = [_row_spec(tm, d), _full_spec(mod.shape), _full_spec(nw.shape)]
    in_specs += [_row_spec(tm, d), _full_spec(w_in_t.shape)] if want_dx else []
    in_specs += [_row_spec(tm, a.shape[1]) if a.ndim == 2 else _head_spec(a.shape[0], tm, a.shape[2]) for a in arrs]
    in_specs += exchange.in_specs if hosted else []
    out_shape, out_specs, scratch = [], [], []
    if want_dx:
        out_shape += [sd((s, d), F32), sd((4, d), F32)]
        out_specs += [_row_spec(tm, d), _full_spec((4, d))]
        scratch.append(pltpu.VMEM((8, d), F32))
    if want_dw:
        out_shape.append(sd((n_cols, d), MXU))
        out_specs.append(_full_spec((n_cols, d)))
        scratch.append(pltpu.VMEM((n_cols, d), F32))
    if hosted:
        out_shape += list(exchange.land_shapes)
        out_specs += list(exchange.out_specs)
        scratch += list(exchange.sems)
    return pl.pallas_call(
        body, name=name, grid=grid, out_shape=tuple(out_shape), in_specs=in_specs, out_specs=tuple(out_specs),
        scratch_shapes=scratch, compiler_params=_params(("arbitrary",)),
    )(*args)


def ada_weight_grad(c_all, dmod_cols):
    d = c_all.shape[1]
    w = dmod_cols.shape[2]

    def body(c_ref, dm_ref, out_ref):
        ca = _silu(c_ref[...])
        for l in range(2):
            out_ref[l] = _mm_tn(ca, dm_ref[l])

    return pl.pallas_call(
        body, name="ada_weight_grad",
        out_shape=jax.ShapeDtypeStruct((2, d, w), F32),
        compiler_params=pltpu.CompilerParams(vmem_limit_bytes=VMEM_LIMIT),
    )(c_all, dmod_cols)


def _slot_sum(g_ref):
    g = g_ref[0].astype(F32)
    for k in range(1, g_ref.shape[0]):
        g = g + g_ref[k].astype(F32)
    return g


def _adamw_math(g, w, m, v):
    m_new = ADAM_B1 * m + (1.0 - ADAM_B1) * g
    v_new = ADAM_B2 * v + (1.0 - ADAM_B2) * (g * g)
    m_hat = m_new / (1.0 - ADAM_B1 ** ADAM_STEP)
    v_hat = v_new / (1.0 - ADAM_B2 ** ADAM_STEP)
    return -ADAM_LR * (m_hat / (jnp.sqrt(v_hat) + ADAM_EPS) + ADAM_WD * w), m_new, v_new


def adamw_small(g_alls, ws, ms, vs, loss_all):
    n = len(ws)

    def body(*refs):
        g_refs, w_refs, m_refs, v_refs = (refs[i * n:(i + 1) * n] for i in range(4))
        loss_ref = refs[4 * n]
        outs = refs[4 * n + 1:]
        for i in range(n):
            g = _slot_sum(g_refs[i])
            outs[i][...] = g
            outs[n + i][...], outs[2 * n + i][...], outs[3 * n + i][...] = _adamw_math(
                g, w_refs[i][...], m_refs[i][...], v_refs[i][...])
        outs[4 * n][...] = _slot_sum(loss_ref)

    sds = [jax.ShapeDtypeStruct(w.shape, F32) for w in ws]
    res = pl.pallas_call(
        body, name="adamw_small", out_shape=tuple(sds * 4) + (jax.ShapeDtypeStruct(loss_all.shape[1:], F32),),
        compiler_params=pltpu.CompilerParams(vmem_limit_bytes=VMEM_LIMIT),
    )(*g_alls, *ws, *ms, *vs, loss_all)
    return [res[i * n:(i + 1) * n] for i in range(4)], res[4 * n]


def adamw_rows(g_slots, w, m, v, name):
    n, r, lanes = g_slots.shape
    fits = [t for t in range(16, r + 1, 16) if r % t == 0 and t * lanes <= ADAM_TILE]
    tr = max(fits) if fits else r
    def body(g_ref, w_ref, m_ref, v_ref, go, do, mo, vo):
        g = _slot_sum(g_ref)
        go[...] = g
        do[...], mo[...], vo[...] = _adamw_math(g, w_ref[...], m_ref[...], v_ref[...])

    row = pl.BlockSpec((tr, lanes), lambda i: (i, 0))
    sd = jax.ShapeDtypeStruct((r, lanes), F32)
    return pl.pallas_call(
        body, name=name, grid=(r // tr,), out_shape=(sd, sd, sd, sd),
        in_specs=[pl.BlockSpec((n, tr, lanes), lambda i: (0, i, 0)), row, row, row],
        out_specs=(row, row, row, row),
        compiler_params=_params(("parallel",)),
    )(g_slots, w, m, v)


def _rope_tables(s):
    def cs(pos, dim):
        inv = ROPE_THETA ** (-np.arange(0, dim, 2, dtype=np.float32) / dim)
        ang = pos.astype(np.float32)[:, None] * inv.astype(np.float32)[None, :]
        return np.cos(ang), np.sin(ang)

    rows = s // GRID_W
    row = np.repeat(np.arange(rows), GRID_W)
    col = np.tile(np.arange(GRID_W), rows)
    cr, sr = cs(row, HD // 2)
    cc, sc = cs(col, HD // 2)
    ct, st = cs(np.arange(s), B_ROPE)
    tables = (np.concatenate([cr, cr, cc, cc] * 2, axis=-1), np.concatenate([-sr, sr, -sc, sc] * 2, axis=-1),
              np.concatenate([ct, ct] * 4, axis=-1), np.concatenate([-st, st] * 4, axis=-1))
    return tuple(jnp.asarray(t, F32) for t in tables)


def _even_rows_to_kernel(wt):
    return jnp.concatenate([wt[:1664], wt[1696:], wt[1664:1696]], axis=0)


def _uq_rows_to_kernel(wt):
    r = wt.reshape(B_HEADS, B_NOPE + B_ROPE, -1)
    return jnp.concatenate([r[:, :B_NOPE].reshape(B_HEADS * B_NOPE, -1), r[:, B_NOPE:].reshape(B_HEADS * B_ROPE, -1)])


def _uq_rows_to_reference(wt):
    nope = wt[:B_HEADS * B_NOPE].reshape(B_HEADS, B_NOPE, -1)
    rope = wt[B_HEADS * B_NOPE:].reshape(B_HEADS, B_ROPE, -1)
    return jnp.concatenate([nope, rope], axis=1).reshape(B_HEADS * (B_NOPE + B_ROPE), -1)


def _shard_t(w):
    return jnp.transpose(w[0])


def _unshard_t(wt, like):
    return jnp.transpose(wt)[None].reshape(like.shape)


def kernel(x, c, norm_w, ada_w, ada_b, even_w_in, a_q_norm, a_k_norm, b_q_lora_norm, b_kv_lora_norm, b_w_uq, b_w_uk, b_w_uv, even_w_out, odd_w_in, c_sink, odd_w_out, final_norm, loss_target, m_norm_w, m_ada_w, m_ada_b, m_even_w_in, m_a_q_norm, m_a_k_norm, m_b_q_lora_norm, m_b_kv_lora_norm, m_b_w_uq, m_b_w_uk, m_b_w_uv, m_even_w_out, m_odd_w_in, m_c_sink, m_odd_w_out, m_final_norm, v_norm_w, v_ada_w, v_ada_b, v_even_w_in, v_a_q_norm, v_a_k_norm, v_b_q_lora_norm, v_b_kv_lora_norm, v_b_w_uq, v_b_w_uk, v_b_w_uv, v_even_w_out, v_odd_w_in, v_c_sink, v_odd_w_out, v_final_norm):
    s, d = x.shape[1], x.shape[2]
    x0 = x[0]
    target = loss_target[0]
    me_flat = 4 * lax.axis_index("x") + 2 * lax.axis_index("y") + lax.axis_index("c")

    wcols = ada_w.shape[2]
    bias_cols = lax.dynamic_slice_in_dim(ada_b.reshape(2, N_DEV, wcols), me_flat, 1, axis=1)
    call, modp, (g_in_e, g_uq) = ada_forward(
        jnp.broadcast_to(c, (8, d)), ada_w, bias_cols,
        Gather([_shard_t(even_w_in).astype(MXU), _shard_t(b_w_uq).astype(MXU)]))
    wt_in_e = _even_rows_to_kernel(g_in_e.reshape(-1, d))
    wt_uq = _uq_rows_to_kernel(g_uq.reshape(-1, B_Q_LORA))
    later_exchange = Exchange([_shard_t(odd_w_in).astype(MXU), even_w_out[0].astype(MXU),
                               odd_w_out[0].astype(MXU)], scatter=False)
    uk_bd = (jnp.eye(B_HEADS, dtype=F32)[:, None, :, None] * jnp.transpose(b_w_uk[0], (1, 2, 0))[:, :, None, :]
             ).reshape(B_HEADS * B_NOPE, B_HEADS * B_KV_LORA).astype(MXU)
    head_bd = jnp.asarray(np.kron(np.eye(A_HEADS), np.ones((HD, HD))), MXU)
    gq_full, gk_full = jnp.tile(a_q_norm, (1, A_HEADS)), jnp.tile(a_k_norm, (1, A_KV))
    w_uv = jnp.transpose(b_w_uv[0], (1, 0, 2)).astype(MXU)

    c_all = call[:, 0, :]
    mod = jnp.transpose(modp[:, :, 0, :], (1, 0, 2)).reshape(2, 3, d)
    mod_e, mod_o = mod[0], mod[1]
    nw_e, nw_o = norm_w[0:1], norm_w[1:2]

    cos_a, sin_a, cos_t, sin_t = _rope_tables(s)
    slopes = (2.0 ** (-8.0 * jnp.arange(1, C_HEADS + 1, dtype=F32) / C_HEADS)).reshape(C_HEADS, 1, 1)
    sink2 = c_sink.reshape(C_HEADS, 1, 1) * LOG2E

    (qa, ka, va, qb, kb, kat, vat, kbt, qa_raw, ka_raw, cq_raw, ckv_raw, ga, gb) = even_in_forward(
        x0, mod_e, nw_e, wt_in_e, gq_full, gk_full, b_q_lora_norm, b_kv_lora_norm, wt_uq, uk_bd, head_bd,
        cos_a, sin_a, cos_t, sin_t)
    tk_dense = min(512, s)
    tq_dense = min(256, s)
    fwd_sub = min(8, s // tk_dense)
    bwd_sub_a = min(16, s // tq_dense)
    bwd_sub_b = min(8, s // tq_dense)
    oa, lse_a, g_in_o, g_out_e, g_out_o = flash_forward(
        qa, ka, vat, dv=HD, tq=tq_dense, tk=tk_dense, nsub=fwd_sub, name="attn_a_fwd",
        exchange=later_exchange)
    wt_in_o = g_in_o.reshape(-1, d)
    w_out_e = g_out_e.reshape(-1, d)
    w_out_o = g_out_o.reshape(-1, d)
    o_lat, lse_b = flash_forward(qb, kb, kbt, dv=B_KV_LORA, tq=min(128, s), tk=tk_dense, nsub=fwd_sub,
                                 name="attn_b_fwd")
    ob = latent_out_forward(o_lat, w_uv)
    x1, y_e = mixer_out_forward(x0, mod_e, [(oa, ga), (ob, gb)], w_out_e, "even_out_fwd")

    qc, kc, vc, kct, vct, gc = odd_in_forward(x1, mod_o, nw_o, wt_in_o)
    win_sub = min(8, s // WINDOW)
    oc, lse_c = window_forward(qc, kc, vct, sink2, slopes, win_sub, "attn_c_fwd")
    dx2, y_o, loss_lanes, d_final = mixer_out_forward(x1, mod_o, [(oc, gc)], w_out_o, "odd_out_fwd_loss",
                                                      loss=(target, final_norm.reshape(1, d)))

    loss_part = (0.5 / d) * jnp.sum(loss_lanes)

    doc, dgc, delta_c, dgate_o, dw_out_o, dsink = mixer_out_backward(
        dx2, y_o, mod_o, [(oc, gc)], w_out_o, [C_HEADS], "odd_out_bwd", lse=lse_c.reshape(C_HEADS, s),
        sink=sink2.reshape(C_HEADS, 1))
    rows3 = lambda t: t.reshape(t.shape[0], 1, s)
    dqc, dkc, dvc = window_backward(qc, kc, kct, vc, doc, lse_c, rows3(delta_c), slopes, win_sub, "attn_c_bwd")
    dx1, dvec_o, dwt_in_o = in_proj_backward(
        x1, mod_o, nw_o, [(dqc, O_Q), (dkc, O_K), (dvc, O_V), (dgc, O_G)], "odd_in_bwd",
        dx_out=dx2, w_in_t=wt_in_o, dw_rows=[O_Q, O_K, O_V, O_G])

    doa, dga, dob, dgb, delta_a, dgate_e, dw_out_e = mixer_out_backward(
        dx1, y_e, mod_e, [(oa, ga), (ob, gb)], w_out_e, [A_HEADS, 0], "even_out_bwd")
    d_olat, delta_b, dw_uv = latent_out_backward(dob, o_lat, w_uv)
    blocks = lambda g: g.astype(MXU).reshape(N_DEV, g.shape[0] // N_DEV, g.shape[1])
    even_pieces = lambda: [(pqa, E_QA), (pka, E_KA), (pva, E_VA), (dga, E_GA), (pcq, E_CQ), (pckv, E_CKV),
                           (dgb, E_GB), (pkr, E_KR)]
    scatter_odd = Exchange([blocks(dwt_in_o), blocks(dw_out_o)], True)
    scatter_out_e = Exchange([blocks(dw_out_e)], True)
    dqb, dkb, dvb, l_in_o, l_out_o = flash_backward(
        qb, kb, kbt, None, d_olat, lse_b, rows3(delta_b), scale=SCALE_B, dv=B_KV_LORA,
        tq=tq_dense, tk=tk_dense, nsub=bwd_sub_b, gq=2, name="attn_b_bwd", split=B_KV_LORA, exchange=scatter_odd)
    dqa, dka, dva, l_out_e = flash_backward(
        qa, ka, kat, va, doa, lse_a, rows3(delta_a), scale=SCALE_A, dv=HD,
        tq=tq_dense, tk=tk_dense, nsub=bwd_sub_a, gq=A_KV, name="attn_a_bwd", exchange=scatter_out_e)
    (pqa, pka, pva, pcq, pckv, pkr, g_qn, g_kn, g_qln, g_kvln, dwt_uq, dw_uk) = even_prep_backward(
        dqa, dka, dva, dqb, dkb, dvb, qa_raw, ka_raw, cq_raw, ckv_raw,
        gq_full, gk_full, b_q_lora_norm, b_kv_lora_norm, wt_uq, uk_bd, head_bd, cos_a, sin_a, cos_t, sin_t)
    g_qn = jnp.sum(g_qn.reshape(A_HEADS, HD), axis=0)
    g_kn = jnp.sum(g_kn.reshape(A_KV, HD), axis=0)
    dwt_in_e, l_uk, l_uv = in_proj_backward(
        x0, mod_e, nw_e, even_pieces(), "even_in_bwd_dw",
        dw_rows=[E_QA, E_KA, E_VA, E_GA, E_CQ, E_CKV, (1696, 2208), (1664, 1696)],
        exchange=Exchange([dw_uk.astype(MXU), dw_uv.astype(MXU)], scatter=False))
    dx0, dvec_e, l_in_e, l_uq = in_proj_backward(
        x0, mod_e, nw_e, even_pieces(), "even_in_bwd_dx", dx_out=dx1, w_in_t=wt_in_e,
        exchange=Exchange([blocks(dwt_in_e), blocks(_uq_rows_to_reference(dwt_uq))], True))

    dmod = jnp.stack([jnp.concatenate([dvec_e[0], dvec_e[1], dgate_e[0]]),
                      jnp.concatenate([dvec_o[0], dvec_o[1], dgate_o[0]])])
    d_norm_w = jnp.stack([dvec_e[2], dvec_o[2]])
    small_names = ["norm_w", "ada_b", "a_q_norm", "a_k_norm", "b_q_lora_norm", "b_kv_lora_norm", "b_w_uk", "b_w_uv",
                   "c_sink", "final_norm"]
    small_w = [norm_w, ada_b, a_q_norm, a_k_norm, b_q_lora_norm, b_kv_lora_norm, b_w_uk, b_w_uv, c_sink, final_norm]
    small_m = [m_norm_w, m_ada_b, m_a_q_norm, m_a_k_norm, m_b_q_lora_norm, m_b_kv_lora_norm, m_b_w_uk, m_b_w_uv,
               m_c_sink, m_final_norm]
    small_v = [v_norm_w, v_ada_b, v_a_q_norm, v_a_k_norm, v_b_q_lora_norm, v_b_kv_lora_norm, v_b_w_uk, v_b_w_uv,
               v_c_sink, v_final_norm]
    small_g = [d_norm_w, dmod, g_qn, g_kn, g_qln, g_kvln, None, None, dsink, d_final]
    flat2 = lambda a: a.reshape((1, -1)) if a.size == a.shape[-1] else a.reshape(a.shape[-3:] if a.ndim > 3 else a.shape)
    kshape = [flat2(w).shape for w in small_w]
    late = [i for i, g in enumerate(small_g) if g is not None]
    gathered = all_gather_slots(
        Gather([small_g[i].reshape(kshape[i]) for i in late] + [jnp.full((8, 128), loss_part, F32)]),
        "gather_small_grads")
    g_all = [None] * len(small_g)
    for i, g in zip(late, gathered):
        g_all[i] = g
    g_all[6], g_all[7] = (l.reshape((N_DEV,) + kshape[6]) for l in (l_uk, l_uv))
    sm_out, loss_sum = adamw_small(g_all, [flat2(a) for a in small_w], [flat2(a) for a in small_m],
                                   [flat2(a) for a in small_v], gathered[-1])
    loss = loss_sum[0, 0]
    sm = [{nm: p.reshape(w.shape) for nm, w, p in zip(small_names, small_w, outs)} for outs in sm_out]

    dmod_all = g_all[1].reshape(N_DEV, 2, N_DEV, wcols)
    dmod_cols = lax.dynamic_slice_in_dim(dmod_all, me_flat, 1, axis=2)[:, :, 0, :]
    pad16 = lambda a: jnp.concatenate([a, jnp.zeros_like(a)], axis=0)
    g_ada_w = ada_weight_grad(pad16(c_all), jnp.transpose(pad16(dmod_cols), (1, 0, 2)))
    rows_of = lambda a: a.reshape(-1, wcols)
    ada = adamw_rows(rows_of(g_ada_w)[None], rows_of(ada_w), rows_of(m_ada_w), rows_of(v_ada_w), "adamw_ada_w")
    ada = [p.reshape(ada_w.shape) for p in ada]

    bg = [{}, {}, {}, {}]
    for nm, landed, w, m, v, transposed in (
            ("even_w_in", l_in_e, even_w_in, m_even_w_in, v_even_w_in, True),
            ("b_w_uq", l_uq, b_w_uq, m_b_w_uq, v_b_w_uq, True),
            ("odd_w_in", l_in_o, odd_w_in, m_odd_w_in, v_odd_w_in, True),
            ("even_w_out", l_out_e, even_w_out, m_even_w_out, v_even_w_out, False),
            ("odd_w_out", l_out_o, odd_w_out, m_odd_w_out, v_odd_w_out, False)):
        view = _shard_t if transposed else (lambda a: a[0])
        res = adamw_rows(landed, view(w), view(m), view(v), "adamw_" + nm)
        for kind, p in enumerate(res):
            bg[kind][nm] = _unshard_t(p, w) if transposed else p[None]
    big_names = ["even_w_in", "odd_w_in", "even_w_out", "odd_w_out", "b_w_uq"]

    order = ["norm_w", "ada_w", "ada_b", "even_w_in", "a_q_norm", "a_k_norm", "b_q_lora_norm", "b_kv_lora_norm",
             "b_w_uq", "b_w_uk", "b_w_uv", "even_w_out", "odd_w_in", "c_sink", "odd_w_out", "final_norm"]

    def pick(kind):
        out = []
        for nm in order:
            if nm == "ada_w":
                out.append(ada[kind])
            elif nm in big_names:
                out.append(bg[kind][nm])
            else:
                out.append(sm[kind][nm])
        return out

    return (loss, dx0[None], *pick(0), *pick(1), *pick(2), *pick(3))
```

```python
import functools

import jax
import jax.numpy as jnp
import numpy as np
from jax import lax
from jax.experimental import pallas as pl
from jax.experimental.pallas import tpu as pltpu

F32 = jnp.float32
MXU = jnp.bfloat16
EPS = 1e-6
ROPE_THETA = 10000.0
GRID_W = 64
HD = 64
N_DEV = 8

A_HEADS, A_KV = 8, 2
B_HEADS, B_NOPE, B_ROPE, B_Q_LORA, B_KV_LORA = 8, 64, 32, 256, 128
B_QK = B_KV_LORA + B_ROPE
C_HEADS, C_KV = 16, 4
WINDOW = 128

ADAM_LR, ADAM_B1, ADAM_B2, ADAM_EPS, ADAM_WD, ADAM_STEP = 0.001, 0.9, 0.999, 1e-08, 0.01, 10

ROW_TILE = 512
ADAM_TILE = 2048 * 128
ADAM_COLS = 256

LOG2E = 1.4426950408889634
SCALE_A = HD ** -0.5
SCALE_B = (B_NOPE + B_ROPE) ** -0.5
SCALE2_A, SCALE2_B = SCALE_A * LOG2E, SCALE_B * LOG2E
VMEM_LIMIT = 56 * 1024 * 1024

E_QA, E_KA, E_VA, E_GA, E_CQ, E_CKV, E_GB, E_KR = (
    (0, 512), (512, 640), (640, 768), (768, 1280), (1280, 1536), (1536, 1664), (1664, 2176), (2176, 2208))
O_Q, O_K, O_V, O_G = (0, 1024), (1024, 1280), (1280, 1536), (1536, 2560)


def _mm(a, b):
    return jnp.dot(a.astype(MXU), b.astype(MXU), preferred_element_type=F32)


def _mm_nt(a, b):
    return lax.dot_general(a.astype(MXU), b.astype(MXU), (((1,), (1,)), ((), ())), preferred_element_type=F32)


def _mm_tn(a, b):
    return lax.dot_general(a.astype(MXU), b.astype(MXU), (((0,), (0,)), ((), ())), preferred_element_type=F32)


def _group_sums_t(prod, group):
    tm, w = prod.shape
    sel = (lax.broadcasted_iota(jnp.int32, (w, 128), 0) // group
           == lax.broadcasted_iota(jnp.int32, (w, 128), 1)).astype(MXU)
    hi = prod.astype(MXU)
    lo = prod - hi.astype(F32)
    return (_mm(hi, sel) + _mm(lo, sel)).T


def _sigmoid(z):
    return 1.0 / (1.0 + jnp.exp(-z))


def _silu(z):
    return z * _sigmoid(z)


def _rms(x):
    return lax.rsqrt(jnp.mean(x * x, axis=-1, keepdims=True) + EPS)


def _swap_halves(y, group):
    n = y.shape[-1]
    half = group // 2
    fwd = pltpu.roll(y, half, 1)
    if n == group:
        return fwd
    back = pltpu.roll(y, n - half, 1)
    lane = lax.broadcasted_iota(jnp.int32, y.shape, 1)
    return jnp.where((lane % group) < half, back, fwd)


def _rope(y, cos, sin, group):
    return y * cos + _swap_halves(y, group) * sin


def _rope_t(d, cos, sin, group):
    return d * cos - _swap_halves(d, group) * sin


def _rms_bwd(dy, x, g):
    r = _rms(x)
    xhat = x * r
    dxhat = dy * g
    dx = r * (dxhat - xhat * jnp.mean(dxhat * xhat, axis=-1, keepdims=True))
    return dx, dy * xhat


def _group_mean(v, bd, group):
    hi = v.astype(MXU)
    lo = v - hi.astype(F32)
    return (_mm(hi, bd[...]) + _mm(lo, bd[...])) * (1.0 / group)


def _head_norm(x, g, bd, group):
    return x * lax.rsqrt(_group_mean(x * x, bd, group) + EPS) * g


def _head_norm_bwd(dy, x, g, bd, group):
    r = lax.rsqrt(_group_mean(x * x, bd, group) + EPS)
    xhat = x * r
    dxhat = dy * g
    dx = r * (dxhat - xhat * _group_mean(dxhat * xhat, bd, group))
    return dx, dy * xhat


def _params(sem, vmem=VMEM_LIMIT):
    return pltpu.CompilerParams(dimension_semantics=sem, vmem_limit_bytes=vmem)


def _row_spec(tm, w):
    return pl.BlockSpec((tm, w), lambda i: (i, 0))


def _full_spec(shape):
    nd = len(shape)
    return pl.BlockSpec(shape, lambda i: (0,) * nd)


def _head_spec(h, tm, w):
    return pl.BlockSpec((h, tm, w), lambda i: (0, i, 0))


def _headt_spec(h, w, tm):
    return pl.BlockSpec((h, w, tm), lambda i: (0, 0, i))


def _rows_spec(h, tm):
    return pl.BlockSpec((h, tm), lambda i: (0, i))


def _me():
    return lax.axis_index("x"), lax.axis_index("y"), lax.axis_index("c")


def _flat(p):
    return 4 * p[0] + 2 * p[1] + p[2]


def _peer(me, k):
    x, y, c = me
    return (1 - x if k & 4 else x, 1 - y if k & 2 else y, 1 - c if k & 1 else c)


MESH_ID = pl.DeviceIdType.MESH


class Gather:
    VMEM = pl.BlockSpec(memory_space=pltpu.VMEM)

    def __init__(self, shards):
        self.shards = list(shards)
        self.n = len(self.shards)
        self.out_shapes = tuple(jax.ShapeDtypeStruct((N_DEV,) + a.shape, a.dtype) for a in self.shards)
        self.in_specs = [Gather.VMEM] * self.n
        self.out_specs = (Gather.VMEM,) * self.n
        self.sems = [pltpu.SemaphoreType.DMA((7 * self.n,)), pltpu.SemaphoreType.DMA((7 * self.n,)),
                     pltpu.SemaphoreType.DMA((self.n,))]

    def _plan(self, x_refs, out_refs, sems):
        send_sems, recv_sems, local_sems = sems
        me = _me()
        x, y, c = me
        chips = [(1 - x, y), (x, 1 - y), (1 - x, 1 - y)]

        def copy(a, k, block, to, src=None):
            slot = out_refs[a].at[_flat(block)]
            return pltpu.make_async_remote_copy(
                src_ref=slot if src is None else src, dst_ref=slot, send_sem=send_sems.at[7 * a + k],
                recv_sem=recv_sems.at[7 * a + k], device_id=to, device_id_type=MESH_ID)

        mine = [pltpu.make_async_copy(x_refs[a], out_refs[a].at[_flat(me)], local_sems.at[a]) for a in range(self.n)]
        first = [copy(a, 0, me, (x, y, 1 - c), src=x_refs[a]) for a in range(self.n)]
        first += [copy(a, 1 + j, me, (*chip, c), src=x_refs[a]) for a in range(self.n) for j, chip in enumerate(chips)]
        return me, chips, copy, mine, first

    def start(self, x_refs, out_refs, sems):
        _, _, _, mine, first = self._plan(x_refs, out_refs, sems)
        for cp in mine + first:
            cp.start()

    def forward(self, x_refs, out_refs, sems):
        me, chips, copy, _, _ = self._plan(x_refs, out_refs, sems)
        x, y, c = me
        for a in range(self.n):
            for j, chip in enumerate(chips):
                copy(a, 1 + j, (*chip, c), me).wait_recv()
                copy(a, 4 + j, (*chip, c), (x, y, 1 - c)).start()

    def drain(self, x_refs, out_refs, sems):
        me, chips, copy, mine, first = self._plan(x_refs, out_refs, sems)
        x, y, c = me
        sibling = (x, y, 1 - c)
        for a in range(self.n):
            copy(a, 0, sibling, me).wait_recv()
            for j, chip in enumerate(chips):
                copy(a, 4 + j, (*chip, 1 - c), me).wait_recv()
        for cp in first + [copy(a, 4 + j, (*chip, c), sibling) for a in range(self.n) for j, chip in enumerate(chips)]:
            cp.wait_send()
        for cp in mine:
            cp.wait()

    def finish(self, x_refs, out_refs, sems):
        self.forward(x_refs, out_refs, sems)
        self.drain(x_refs, out_refs, sems)


def all_gather_slots(gather, name):
    def body(*refs):
        x_refs, out_refs, sems = refs[:gather.n], refs[gather.n:2 * gather.n], refs[2 * gather.n:]
        gather.start(x_refs, out_refs, sems)
        gather.finish(x_refs, out_refs, sems)

    return pl.pallas_call(
        body, name=name, out_shape=gather.out_shapes, in_specs=gather.in_specs, out_specs=gather.out_specs,
        scratch_shapes=list(gather.sems), compiler_params=pltpu.CompilerParams(vmem_limit_bytes=VMEM_LIMIT),
    )(*gather.shards)


class Exchange:
    HBM = pl.BlockSpec(memory_space=pl.ANY)

    def __init__(self, srcs, scatter):
        self.srcs = list(srcs)
        self.scatter = scatter
        self.n = len(self.srcs)
        self.land_shapes = tuple(jax.ShapeDtypeStruct((N_DEV,) + tuple(a.shape[-2:]), a.dtype) for a in self.srcs)
        self.in_specs = [Exchange.HBM] * self.n
        self.out_specs = (Exchange.HBM,) * self.n
        self.sems = [pltpu.SemaphoreType.DMA((N_DEV - 1,)), pltpu.SemaphoreType.DMA((N_DEV - 1,)),
                     pltpu.SemaphoreType.DMA] * self.n

    def _copies(self, src_refs, land_refs, sems):
        me = _me()
        mi = _flat(me)
        local, sends, recvs = [], [], []
        for a, (src_ref, land_ref) in enumerate(zip(src_refs, land_refs)):
            send_sems, recv_sems, local_sem = sems[3 * a:3 * a + 3]
            pick = (lambda p, r=src_ref: r.at[_flat(p)]) if self.scatter else (lambda p, r=src_ref: r)
            local.append(pltpu.make_async_copy(pick(me), land_ref.at[mi], local_sem))
            for k in range(1, N_DEV):
                peer = _peer(me, k)
                pair = dict(send_sem=send_sems.at[k - 1], recv_sem=recv_sems.at[k - 1], device_id=peer,
                            device_id_type=MESH_ID)
                sends.append(pltpu.make_async_remote_copy(src_ref=pick(peer), dst_ref=land_ref.at[mi], **pair))
                recvs.append(pltpu.make_async_remote_copy(src_ref=pick(peer), dst_ref=land_ref.at[_flat(peer)],
                                                          **pair))
        return local, sends, recvs

    def start(self, src_refs, land_refs, sems):
        local, sends, _ = self._copies(src_refs, land_refs, sems)
        for cp in local + sends:
            cp.start()

    def wait(self, src_refs, land_refs, sems):
        local, sends, recvs = self._copies(src_refs, land_refs, sems)
        for cp in recvs:
            cp.wait_recv()
        for cp in sends:
            cp.wait_send()
        for cp in local:
            cp.wait()


def ada_forward(c8, ada_w, bias_cols, gather):
    d = c8.shape[1]
    w = ada_w.shape[2]
    ng = gather.n

    def body(*refs):
        c_ref, w_ref, b_ref = refs[:3]
        gx_refs = refs[3:3 + ng]
        call_ref, modp_ref = refs[3 + ng:5 + ng]
        gout_refs = refs[5 + ng:5 + 2 * ng]
        part_ref, s1, r1, s2, r2 = refs[5 + 2 * ng:10 + 2 * ng]
        g_sems = refs[10 + 2 * ng:]
        me = _me()
        mi = _flat(me)
        call_ref[mi] = c_ref[...]
        rows_out = []
        for k in range(1, N_DEV):
            rows_out.append(pltpu.make_async_remote_copy(
                src_ref=c_ref, dst_ref=call_ref.at[mi], send_sem=s1.at[k - 1], recv_sem=r1.at[k - 1],
                device_id=_peer(me, k), device_id_type=MESH_ID))
        for cp in rows_out:
            cp.start()
        gather.start(gx_refs, gout_refs, g_sems)
        for k in range(1, N_DEV):
            pltpu.make_async_remote_copy(
                src_ref=c_ref, dst_ref=call_ref.at[_flat(_peer(me, k))], send_sem=s1.at[k - 1],
                recv_sem=r1.at[k - 1], device_id=_peer(me, k), device_id_type=MESH_ID).wait_recv()
        ca = _silu(call_ref[...].reshape(N_DEV * 8, d))
        for l in range(2):
            part = _mm(ca, w_ref[l]) + b_ref[l]
            for b in range(N_DEV):
                part_ref[b, l] = part[8 * b:8 * b + 8, :]
        modp_ref[mi] = part_ref[mi]
        spread = []
        for k in range(1, N_DEV):
            peer = _peer(me, k)
            spread.append(pltpu.make_async_remote_copy(
                src_ref=part_ref.at[_flat(peer)], dst_ref=modp_ref.at[mi], send_sem=s2.at[k - 1],
                recv_sem=r2.at[k - 1], device_id=peer, device_id_type=MESH_ID))
        for cp in spread:
            cp.start()
        gather.forward(gx_refs, gout_refs, g_sems)
        for k in range(1, N_DEV):
            pi = _flat(_peer(me, k))
            pltpu.make_async_remote_copy(
                src_ref=part_ref.at[pi], dst_ref=modp_ref.at[pi], send_sem=s2.at[k - 1],
                recv_sem=r2.at[k - 1], device_id=_peer(me, k), device_id_type=MESH_ID).wait_recv()
        for cp in rows_out + spread:
            cp.wait_send()
        gather.drain(gx_refs, gout_refs, g_sems)

    vm = pl.BlockSpec(memory_space=pltpu.VMEM)
    res = pl.pallas_call(
        body, name="ada_forward",
        out_shape=(jax.ShapeDtypeStruct((N_DEV, 8, d), F32), jax.ShapeDtypeStruct((N_DEV, 2, 8, w), F32))
        + gather.out_shapes,
        in_specs=[vm, vm, vm] + gather.in_specs, out_specs=(vm, vm) + gather.out_specs,
        scratch_shapes=[pltpu.VMEM((N_DEV, 2, 8, w), F32)] + [pltpu.SemaphoreType.DMA((7,))] * 4 + list(gather.sems),
        compiler_params=pltpu.CompilerParams(vmem_limit_bytes=VMEM_LIMIT),
    )(c8, ada_w, bias_cols, *gather.shards)
    return res[0], res[1], res[2:]


def _modulated(x, mod_ref, nw_ref):
    xn = x * _rms(x)
    g1 = nw_ref[...] * (1.0 + mod_ref[1:2, :])
    return xn, g1, xn * g1 + mod_ref[0:1, :]


def even_in_forward(x, mod, nw, w_in_t, gq, gk, qln, kvln, w_uq_t, uk_bd, bd, cos_a, sin_a, cos_t, sin_t):
    s, d = x.shape
    tm = min(ROW_TILE, s)
    n_nope = B_HEADS * B_NOPE

    def body(x_ref, mod_ref, nw_ref, w_ref, gq_ref, gk_ref, qln_ref, kvln_ref, uq_ref, ukbd_ref, bd_ref,
             ca_ref, sa_ref, ct_ref, st_ref,
             qa_o, ka_o, va_o, qb_o, kb_o, kat_o, vat_o, kbt_o, qa_raw_o, ka_raw_o, cq_raw_o, ckv_raw_o, ga_o, gb_o):
        _, _, h = _modulated(x_ref[...], mod_ref, nw_ref)
        h = h.astype(MXU)

        def proj(cols):
            return _mm_nt(h, w_ref[cols[0]:cols[1], :])

        ca, sa, ct, st = ca_ref[...], sa_ref[...], ct_ref[...], st_ref[...]
        wide = lambda t, n: jnp.concatenate([t] * n, axis=1)
        qa = proj(E_QA)
        qa_raw_o[...] = qa
        qr = _rope(_head_norm(qa, gq_ref[...], bd_ref, HD), wide(ca, 4), wide(sa, 4), 32) * SCALE2_A
        for hh in range(A_HEADS):
            qa_o[hh] = qr[:, HD * hh:HD * hh + HD].astype(MXU)
        ka = proj(E_KA)
        ka_raw_o[...] = ka
        kr = _rope(_head_norm(ka, gk_ref[...], bd_ref[0:128, 0:128], HD), ca, sa, 32)
        va = proj(E_VA)
        krt, vat = kr.T, va.T
        for g in range(A_KV):
            ka_o[g] = kr[:, HD * g:HD * g + HD].astype(MXU)
            va_o[g] = va[:, HD * g:HD * g + HD].astype(MXU)
            kat_o[g] = krt[HD * g:HD * g + HD, :].astype(MXU)
            vat_o[g] = vat[HD * g:HD * g + HD, :].astype(MXU)
        ga_o[...] = proj(E_GA).astype(MXU)
        gb_o[...] = proj(E_GB).astype(MXU)
        cq = proj(E_CQ)
        cq_raw_o[...] = cq
        qb = _mm_nt(cq * _rms(cq) * qln_ref[...], uq_ref[...])
        q_lat = _mm(qb[:, 0:n_nope], ukbd_ref[...]) * SCALE2_B
        q_rope = _rope(qb[:, n_nope:], wide(ct, 2), wide(st, 2), 32) * SCALE2_B
        for hh in range(B_HEADS):
            qb_o[hh, :, 0:B_KV_LORA] = q_lat[:, B_KV_LORA * hh:B_KV_LORA * (hh + 1)].astype(MXU)
            qb_o[hh, :, B_KV_LORA:B_QK] = q_rope[:, B_ROPE * hh:B_ROPE * (hh + 1)].astype(MXU)
        ckv = proj(E_CKV)
        ckv_raw_o[...] = ckv
        ckv_n = ckv * _rms(ckv) * kvln_ref[...]
        k_rope = _rope(proj(E_KR), ct[:, 0:B_ROPE], st[:, 0:B_ROPE], 32)
        kb_o[0, :, 0:B_KV_LORA] = ckv_n.astype(MXU)
        kb_o[0, :, B_KV_LORA:B_QK] = k_rope.astype(MXU)
        kbt_o[0, 0:B_KV_LORA, :] = ckv_n.T.astype(MXU)
        kbt_o[0, B_KV_LORA:B_QK, :] = k_rope.T.astype(MXU)

    sd = jax.ShapeDtypeStruct
    outs = (sd((A_HEADS, s, HD), MXU), sd((A_KV, s, HD), MXU), sd((A_KV, s, HD), MXU),
            sd((B_HEADS, s, B_QK), MXU), sd((1, s, B_QK), MXU),
            sd((A_KV, HD, s), MXU), sd((A_KV, HD, s), MXU), sd((1, B_QK, s), MXU),
            sd((s, 512), F32), sd((s, 128), F32), sd((s, B_Q_LORA), F32), sd((s, B_KV_LORA), F32),
            sd((s, 512), MXU), sd((s, 512), MXU))
    out_specs = (_head_spec(A_HEADS, tm, HD), _head_spec(A_KV, tm, HD), _head_spec(A_KV, tm, HD),
                 _head_spec(B_HEADS, tm, B_QK), _head_spec(1, tm, B_QK),
                 _headt_spec(A_KV, HD, tm), _headt_spec(A_KV, HD, tm), _headt_spec(1, B_QK, tm),
                 _row_spec(tm, 512), _row_spec(tm, 128), _row_spec(tm, B_Q_LORA), _row_spec(tm, B_KV_LORA),
                 _row_spec(tm, 512), _row_spec(tm, 512))
    consts = [mod, nw, w_in_t, gq, gk, qln, kvln, w_uq_t, uk_bd, bd]
    return pl.pallas_call(
        body, name="even_in_forward", grid=(s // tm,), out_shape=outs,
        in_specs=[_row_spec(tm, d)] + [_full_spec(a.shape) for a in consts] + [_row_spec(tm, 128)] * 4,
        out_specs=out_specs, compiler_params=_params(("parallel",)),
    )(x, *consts, cos_a, sin_a, cos_t, sin_t)


def odd_in_forward(x, mod, nw, w_in):
    s, d = x.shape
    tm = min(ROW_TILE, s)

    def body(x_ref, mod_ref, nw_ref, w_ref, q_o, k_o, v_o, kt_o, vt_o, g_o):
        _, _, h = _modulated(x_ref[...], mod_ref, nw_ref)
        h = h.astype(MXU)

        def proj(cols):
            return _mm_nt(h, w_ref[cols[0]:cols[1], :])

        q = proj(O_Q) * SCALE2_A
        for hh in range(C_HEADS):
            q_o[hh] = q[:, HD * hh:HD * hh + HD].astype(MXU)
        k = proj(O_K)
        v = proj(O_V)
        for g in range(C_KV):
            kh = k[:, HD * g:HD * g + HD]
            vh = v[:, HD * g:HD * g + HD]
            k_o[g] = kh.astype(MXU)
            v_o[g] = vh.astype(MXU)
            kt_o[g] = kh.T.astype(MXU)
            vt_o[g] = vh.T.astype(MXU)
        g_o[...] = proj(O_G).astype(MXU)

    sd = jax.ShapeDtypeStruct
    return pl.pallas_call(
        body, name="odd_in_forward", grid=(s // tm,),
        out_shape=(sd((C_HEADS, s, HD), MXU), sd((C_KV, s, HD), MXU), sd((C_KV, s, HD), MXU),
                   sd((C_KV, HD, s), MXU), sd((C_KV, HD, s), MXU), sd((s, 1024), MXU)),
        in_specs=[_row_spec(tm, d), _full_spec(mod.shape), _full_spec(nw.shape), _full_spec(w_in.shape)],
        out_specs=(_head_spec(C_HEADS, tm, HD), _head_spec(C_KV, tm, HD), _head_spec(C_KV, tm, HD),
                   _headt_spec(C_KV, HD, tm), _headt_spec(C_KV, HD, tm), _row_spec(tm, 1024)),
        compiler_params=_params(("parallel",)),
    )(x, mod, nw, w_in)


def latent_out_forward(o_lat, w_uv):
    s = o_lat.shape[0]
    tm = min(ROW_TILE, s)

    def body(o_ref, uv_ref, out_ref):
        for hh in range(B_HEADS):
            out_ref[:, HD * hh:HD * hh + HD] = _mm(o_ref[:, B_KV_LORA * hh:B_KV_LORA * (hh + 1)],
                                                   uv_ref[hh]).astype(MXU)

    return pl.pallas_call(
        body, name="latent_out_forward", grid=(s // tm,),
        out_shape=jax.ShapeDtypeStruct((s, B_HEADS * HD), MXU),
        in_specs=[_row_spec(tm, o_lat.shape[1]), _full_spec(w_uv.shape)],
        out_specs=_row_spec(tm, B_HEADS * HD),
        compiler_params=_params(("parallel",)),
    )(o_lat, w_uv)


def mixer_out_forward(x, mod, pairs, w_out, name, loss=None):
    s, d = x.shape
    tm = min(ROW_TILE, s)
    n = len(pairs)
    widths = [o.shape[1] for o, _ in pairs]
    head = loss is not None

    def body(*refs):
        x_ref, mod_ref, w_ref = refs[:3]
        pr = refs[3:3 + 2 * n]
        rest = refs[3 + 2 * n:]
        y = jnp.zeros((tm, d), F32)
        r0 = 0
        for i in range(n):
            mix = pr[2 * i][...].astype(F32) * _silu(pr[2 * i + 1][...].astype(F32))
            y = y + _mm(mix, w_ref[r0:r0 + widths[i], :])
            r0 += widths[i]
        x_out = x_ref[...] + mod_ref[2:3, :] * y
        if not head:
            xo_ref, y_ref = rest
            xo_ref[...] = x_out
        else:
            t_ref, fn_ref, dx_ref, y_ref, lp_ref, dw_ref = rest

            @pl.when(pl.program_id(0) == 0)
            def _():
                lp_ref[...] = jnp.zeros(lp_ref.shape, F32)
                dw_ref[...] = jnp.zeros(dw_ref.shape, F32)

            g = fn_ref[...]
            err = x_out * _rms(x_out) * g - t_ref[...]
            lp_ref[...] += jnp.sum(err * err, axis=0, keepdims=True)
            dx, dg = _rms_bwd(err * (1.0 / d), x_out, g)
            dx_ref[...] = dx
            dw_ref[...] += jnp.sum(dg, axis=0, keepdims=True)
        y_ref[...] = y.astype(y_ref.dtype)

    flat = [a for p in pairs for a in p]
    sd = jax.ShapeDtypeStruct
    in_specs = [_row_spec(tm, d), _full_spec(mod.shape), _full_spec(w_out.shape)]
    in_specs += [_row_spec(tm, a.shape[1]) for a in flat]
    out_shape = (sd((s, d), F32), sd((s, d), MXU))
    out_specs = (_row_spec(tm, d), _row_spec(tm, d))
    if head:
        in_specs += [_row_spec(tm, d), _full_spec(loss[1].shape)]
        out_shape += (sd((1, d), F32), sd((1, d), F32))
        out_specs += (_full_spec((1, d)), _full_spec((1, d)))
    return pl.pallas_call(
        body, name=name, grid=(s // tm,), out_shape=out_shape, in_specs=in_specs, out_specs=out_specs,
        compiler_params=_params(("arbitrary",) if head else ("parallel",)),
    )(x, mod, w_out, *flat, *(loss if head else ()))


ONES_ROWS = 16
AHEAD = 2


def _col_max8(s3):
    m8 = jnp.max(s3, axis=0)
    return jnp.broadcast_to(jnp.max(m8, axis=0, keepdims=True), m8.shape)


def _with_ones(vt, n):
    return jnp.concatenate([vt, jnp.ones((ONES_ROWS, n), vt.dtype)], axis=0)


def _grid_edges(grid):
    ids = [pl.program_id(a) for a in range(len(grid))]
    first = functools.reduce(jnp.logical_and, [i == 0 for i in ids])
    last = functools.reduce(jnp.logical_and, [i == n - 1 for i, n in zip(ids, grid)])
    return first, last


def flash_forward(q, k, vt, *, dv, tq, tk, nsub, name, exchange=None):
    hq, s, dq = q.shape
    g_kv = k.shape[0]
    hpg = hq // g_kv
    nq = s // tq
    tkk = tk * nsub
    nk = s // tkk
    grid = (g_kv, nq, nk)
    hosted = exchange is not None
    m_cols = hpg * tq
    dvp = dv + ONES_ROWS

    def body(*refs):
        nx = exchange.n if hosted else 0
        q_ref, k_ref, vt_ref = refs[:3]
        xs_refs = refs[3:3 + nx]
        o_ref, lse_ref = refs[3 + nx:5 + nx]
        land_refs = refs[5 + nx:5 + 2 * nx]
        m_s, acc_s = refs[5 + 2 * nx:7 + 2 * nx]
        sems = refs[7 + 2 * nx:]
        if hosted:
            first, last = _grid_edges(grid)
            pl.when(first)(lambda: exchange.start(xs_refs, land_refs, sems))
        j = pl.program_id(2)

        @pl.when(j == 0)
        def _():
            m_s[...] = jnp.full((8, m_cols), -jnp.inf, F32)
            acc_s[...] = jnp.zeros((dvp, m_cols), F32)

        qq = q_ref[...].reshape(m_cols, dq)
        score = lambda u: _mm_nt(k_ref[0, tk * u:tk * (u + 1), :], qq).reshape(tk // 8, 8, m_cols)
        sts = {u: score(u) for u in range(min(AHEAD, nsub))}
        m_run = m_s[...]
        acc = acc_s[...]
        for u in range(nsub):
            if u + AHEAD < nsub:
                sts[u + AHEAD] = score(u + AHEAD)
            st = sts.pop(u)
            m_new = jnp.maximum(m_run, _col_max8(st))
            p = jnp.exp2(st - m_new[None])
            alpha = jnp.exp2(m_run - m_new)
            pv = _mm(_with_ones(vt_ref[0, 0:dv, tk * u:tk * (u + 1)], tk), p.reshape(tk, m_cols))
            acc = (acc.reshape(dvp // 8, 8, m_cols) * alpha[None]).reshape(dvp, m_cols) + pv
            m_run = m_new
        acc_s[...] = acc
        m_s[...] = m_run

        @pl.when(j == nk - 1)
        def _():
            l = acc_s[dv:dv + 1, :]
            ot = acc_s[0:dv, :] / l
            lse = m_s[0:1, :] + jnp.log2(l)
            for hh in range(hpg):
                o_ref[:, dv * hh:dv * hh + dv] = ot[:, tq * hh:tq * hh + tq].T.astype(MXU)
                lse_ref[hh] = lse[:, tq * hh:tq * hh + tq]

        if hosted:
            pl.when(last)(lambda: exchange.wait(xs_refs, land_refs, sems))

    sd = jax.ShapeDtypeStruct
    return pl.pallas_call(
        body, name=name, grid=grid,
        out_shape=(sd((s, hq * dv), MXU), sd((hq, 1, s), F32)) + (exchange.land_shapes if hosted else ()),
        in_specs=[pl.BlockSpec((hpg, tq, dq), lambda g, i, j: (g, i, 0)),
                  pl.BlockSpec((1, tkk, k.shape[2]), lambda g, i, j: (g, j, 0)),
                  pl.BlockSpec((1, dv, tkk), lambda g, i, j: (g, 0, j))] + (exchange.in_specs if hosted else []),
        out_specs=(pl.BlockSpec((tq, hpg * dv), lambda g, i, j: (i, g)),
                   pl.BlockSpec((hpg, 1, tq), lambda g, i, j: (g, 0, i))) + (exchange.out_specs if hosted else ()),
        scratch_shapes=[pltpu.VMEM((8, m_cols), F32), pltpu.VMEM((dvp, m_cols), F32)]
        + (list(exchange.sems) if hosted else []),
        compiler_params=_params(("arbitrary",) * 3 if hosted else ("parallel", "parallel", "arbitrary")),
    )(q, k, vt, *(exchange.srcs if hosted else []))


def _window_bias_t(hpg, slope_ref):
    t = WINDOW
    r = lax.broadcasted_iota(jnp.int32, (3 * t, t), 0)
    cq = lax.broadcasted_iota(jnp.int32, (3 * t, t), 1)
    arel = jnp.abs(r - t - cq)
    base = jnp.where(arel <= WINDOW, arel.astype(F32) * (-LOG2E), -jnp.inf)
    return jnp.concatenate([base * slope_ref[hh] for hh in range(hpg)], axis=1)


def _window_edges_t(bias, no_before, no_after):
    t = WINDOW
    r = lax.broadcasted_iota(jnp.int32, bias.shape, 0)
    out = ((r < t) & no_before) | ((r >= 2 * t) & no_after)
    return jnp.where(out, -jnp.inf, bias)


def _window_specs(kind, nb, nblk, d):
    t = WINDOW
    before = lambda i: jnp.clip(i * nb - 1, 0, nblk - 1)
    after = lambda i: jnp.clip((i + 1) * nb, 0, nblk - 1)
    if kind == "rows":
        return [pl.BlockSpec((1, t, d), lambda g, i: (g, before(i), 0)),
                pl.BlockSpec((1, nb * t, d), lambda g, i: (g, i, 0)),
                pl.BlockSpec((1, t, d), lambda g, i: (g, after(i), 0))]
    return [pl.BlockSpec((1, d, t), lambda g, i: (g, 0, before(i))),
            pl.BlockSpec((1, d, nb * t), lambda g, i: (g, 0, i)),
            pl.BlockSpec((1, d, t), lambda g, i: (g, 0, after(i)))]


def window_forward(q, k, vt, sink2, slopes, nb, name):
    hq, s, d = q.shape
    g_kv = k.shape[0]
    hpg = hq // g_kv
    t = WINDOW
    nblk = s // t
    steps = nblk // nb
    m_cols = hpg * t

    def body(q_ref, kp, ko, kn, vp, vo, vn, sink_ref, slope_ref, o_ref, lse_ref):
        i = pl.program_id(1)
        kk_all = jnp.concatenate([kp[0], ko[0], kn[0]], axis=0)
        vt_all = jnp.concatenate([vp[0], vo[0], vn[0]], axis=1)
        bias = _window_bias_t(hpg, slope_ref)
        sink_row = jnp.concatenate([jnp.broadcast_to(sink_ref[hh], (8, t)) for hh in range(hpg)], axis=1)
        sts = {}

        def score(u):
            qq = q_ref[:, t * u:t * (u + 1), :].reshape(m_cols, d)
            b_u = bias
            if u == 0 or u == nb - 1:
                b_u = _window_edges_t(bias, (i == 0) if u == 0 else False,
                                      (i == steps - 1) if u == nb - 1 else False)
            sts[u] = _mm_nt(kk_all[t * u:t * (u + 3), :], qq) + b_u

        for u in range(min(AHEAD, nb)):
            score(u)
        for u in range(nb):
            if u + AHEAD < nb:
                score(u + AHEAD)
            s3 = sts.pop(u).reshape(3 * t // 8, 8, m_cols)
            m8 = jnp.maximum(_col_max8(s3), sink_row)
            p = jnp.exp2(s3 - m8[None]).reshape(3 * t, m_cols)
            acc = _mm(_with_ones(vt_all[:, t * u:t * (u + 3)], 3 * t), p)
            l = acc[d:d + 1, :] + jnp.exp2(sink_row[0:1, :] - m8[0:1, :])
            ot = acc[0:d, :] / l
            lse = m8[0:1, :] + jnp.log2(l)
            for hh in range(hpg):
                o_ref[t * u:t * (u + 1), d * hh:d * hh + d] = ot[:, t * hh:t * hh + t].T.astype(MXU)
                lse_ref[hh, :, t * u:t * (u + 1)] = lse[:, t * hh:t * hh + t]

    sd = jax.ShapeDtypeStruct
    return pl.pallas_call(
        body, name=name, grid=(g_kv, steps),
        out_shape=(sd((s, hq * d), MXU), sd((hq, 1, s), F32)),
        in_specs=[pl.BlockSpec((hpg, nb * t, d), lambda g, i: (g, i, 0))]
        + _window_specs("rows", nb, nblk, d) + _window_specs("cols", nb, nblk, d)
        + [pl.BlockSpec((hpg, 1, 1), lambda g, i: (g, 0, 0))] * 2,
        out_specs=(pl.BlockSpec((nb * t, hpg * d), lambda g, i: (i, g)),
                   pl.BlockSpec((hpg, 1, nb * t), lambda g, i: (g, 0, i))),
        compiler_params=_params(("parallel", "parallel")),
    )(q, k, k, k, vt, vt, vt, sink2, slopes)


def window_backward(q, k, kt, v, do, lse, delta, slopes, nb, name):
    hq, s, d = q.shape
    g_kv = k.shape[0]
    hpg = hq // g_kv
    t = WINDOW
    nblk = s // t
    steps = nblk // nb
    m_cols = hpg * t

    def body(q_ref, kp, ko, kn, ktp, kto, ktn, vp, vo, vn, do_ref, lse_ref, dl_ref, slope_ref,
             dq_ref, dk_ref, dv_ref, dk_s, dv_s):
        i = pl.program_id(1)

        @pl.when(i == 0)
        def _():
            dk_ref[...] = jnp.zeros(dk_ref.shape, F32)
            dv_ref[...] = jnp.zeros(dv_ref.shape, F32)

        dk_s[...] = jnp.zeros(dk_s.shape, F32)
        dv_s[...] = jnp.zeros(dv_s.shape, F32)
        kk_all = jnp.concatenate([kp[0], ko[0], kn[0]], axis=0)
        vv_all = jnp.concatenate([vp[0], vo[0], vn[0]], axis=0)
        kkt_all = jnp.concatenate([ktp[0], kto[0], ktn[0]], axis=1)
        bias = _window_bias_t(hpg, slope_ref)
        qqs, dds, sts, dps = {}, {}, {}, {}

        def issue(u):
            rows = slice(t * u, t * (u + 1))
            keys = slice(t * u, t * (u + 3))
            qqs[u] = q_ref[:, rows, :].reshape(m_cols, d)
            dds[u] = jnp.concatenate([do_ref[rows, d * hh:d * hh + d] for hh in range(hpg)], axis=0)
            b_u = bias
            if u == 0 or u == nb - 1:
                b_u = _window_edges_t(bias, (i == 0) if u == 0 else False,
                                      (i == steps - 1) if u == nb - 1 else False)
            sts[u] = _mm_nt(kk_all[keys, :], qqs[u]) + b_u
            dps[u] = _mm_nt(vv_all[keys, :], dds[u])

        for u in range(min(AHEAD, nb)):
            issue(u)
        for u in range(nb):
            if u + AHEAD < nb:
                issue(u + AHEAD)
            rows = slice(t * u, t * (u + 1))
            keys = slice(t * u, t * (u + 3))
            lse_row = jnp.concatenate([lse_ref[hh, :, rows] for hh in range(hpg)], axis=1)
            dl_row = jnp.concatenate([dl_ref[hh, :, rows] for hh in range(hpg)], axis=1)
            p = jnp.exp2(sts[u] - lse_row)
            ds = p * (dps[u] - dl_row) * SCALE_A
            dv_s[keys, :] += _mm(p, dds[u])
            dk_s[keys, :] += _mm(ds, qqs[u])
            dqt = _mm(kkt_all[:, keys], ds)
            for hh in range(hpg):
                dq_ref[rows, d * hh:d * hh + d] = dqt[:, t * hh:t * hh + t].T.astype(dq_ref.dtype)
        tq = nb * t
        for src, r0, n in ((0, jnp.clip(i * nb - 1, 0, nblk - 1) * t, t), (t, i * tq, tq),
                           (t + tq, jnp.clip((i + 1) * nb, 0, nblk - 1) * t, t)):
            dst = pl.ds(pl.multiple_of(r0, t), n)
            dk_ref[0, dst, :] += dk_s[src:src + n, :] * (1.0 / SCALE2_A)
            dv_ref[0, dst, :] += dv_s[src:src + n, :]

    row_map = lambda g, i: (g, 0, i)
    sd = jax.ShapeDtypeStruct
    return pl.pallas_call(
        body, name=name, grid=(g_kv, steps),
        out_shape=(sd((s, hq * d), MXU), sd((g_kv, s, d), F32), sd((g_kv, s, d), F32)),
        in_specs=[pl.BlockSpec((hpg, nb * t, d), lambda g, i: (g, i, 0))]
        + _window_specs("rows", nb, nblk, d) + _window_specs("cols", nb, nblk, d) + _window_specs("rows", nb, nblk, d)
        + [pl.BlockSpec((nb * t, hpg * d), lambda g, i: (i, g)), pl.BlockSpec((hpg, 1, nb * t), row_map),
           pl.BlockSpec((hpg, 1, nb * t), row_map), pl.BlockSpec((hpg, 1, 1), lambda g, i: (g, 0, 0))],
        out_specs=(pl.BlockSpec((nb * t, hpg * d), lambda g, i: (i, g)),
                   pl.BlockSpec((1, s, d), lambda g, i: (g, 0, 0)),
                   pl.BlockSpec((1, s, d), lambda g, i: (g, 0, 0))),
        scratch_shapes=[pltpu.VMEM(((nb + 2) * t, d), F32), pltpu.VMEM(((nb + 2) * t, d), F32)],
        compiler_params=_params(("parallel", "arbitrary")),
    )(q, k, k, k, kt, kt, kt, v, v, v, do, lse, delta, slopes)


def flash_backward(q, k, kt, v, do, lse, delta, *, scale, dv, tq, tk, nsub, gq, name, split=None, exchange=None):
    hq, s, dq = q.shape
    g_kv = k.shape[0]
    hpg = hq // gq
    nq = s // tq
    tqq = tq * nsub
    nqs = s // tqq
    nkb = s // tk
    grid = (gq, nkb, nqs)
    hosted = exchange is not None
    m_cols = hpg * tq
    c = scale * LOG2E
    has_v = v is not None

    def body(*refs):
        it = iter(refs)
        q_ref, k_ref, kt_ref = next(it), next(it), next(it)
        v_ref = next(it) if has_v else None
        do_ref, lse_ref, dl_ref = next(it), next(it), next(it)
        nx = exchange.n if hosted else 0
        xs_refs = [next(it) for _ in range(nx)]
        dq_ref, dk_ref, dv_ref = next(it), next(it), next(it)
        land_refs = [next(it) for _ in range(nx)]
        dqt_s = next(it)
        sems = list(it)
        kj = pl.program_id(1)
        qi = pl.program_id(2)
        if hosted:
            first, last = _grid_edges(grid)
            pl.when(first)(lambda: exchange.start(xs_refs, land_refs, sems))

        @pl.when((kj == 0) & (qi == 0))
        def _():
            dqt_s[...] = jnp.zeros(dqt_s.shape, F32)

        @pl.when(qi == 0)
        def _():
            dk_ref[...] = jnp.zeros(dk_ref.shape, F32)
            dv_ref[...] = jnp.zeros(dv_ref.shape, F32)

        kk = k_ref[0]
        vv = v_ref[0] if has_v else kk[:, :dv]
        qqs, dds, sts, dps = {}, {}, {}, {}

        def issue(u):
            rows = slice(tq * u, tq * (u + 1))
            qqs[u] = q_ref[:, rows, :].reshape(m_cols, dq)
            dds[u] = jnp.concatenate([do_ref[rows, dv * hh:dv * hh + dv] for hh in range(hpg)], axis=0)
            sts[u] = _mm_nt(kk, qqs[u])
            dps[u] = _mm_nt(vv, dds[u])

        for u in range(min(AHEAD, nsub)):
            issue(u)
        dv_acc = dv_ref[0]
        dk_acc = dk_ref[0]
        for u in range(nsub):
            if u + AHEAD < nsub:
                issue(u + AHEAD)
            rows = slice(tq * u, tq * (u + 1))
            lse_row = jnp.concatenate([lse_ref[hh, :, rows] for hh in range(hpg)], axis=1)
            dl_row = jnp.concatenate([dl_ref[hh, :, rows] for hh in range(hpg)], axis=1)
            p = jnp.exp2(sts[u] - lse_row)
            ds = p * (dps[u] - dl_row) * scale
            dv_acc = dv_acc + _mm(p, dds[u])
            dk_acc = dk_acc + _mm(ds, qqs[u])
            dqt = _mm(kt_ref[0], ds)
            for hh in range(hpg):
                dqt_s[qi * nsub + u, dq * hh:dq * hh + dq, :] += dqt[:, tq * hh:tq * hh + tq]
        dv_ref[0] = dv_acc
        dk_ref[0] = jnp.where(qi == nqs - 1, dk_acc * (1.0 / c), dk_acc)

        @pl.when((kj == nkb - 1) & (qi == nqs - 1))
        def _():
            def emit(t, carry):
                r0 = pl.multiple_of(t * tq, tq)
                for hh in range(hpg):
                    blk = dqt_s[t, dq * hh:dq * hh + dq, :].T
                    if split is None:
                        dq_ref[pl.ds(r0, tq), dq * hh:dq * hh + dq] = blk
                    else:
                        rest = dq - split
                        dq_ref[pl.ds(r0, tq), split * hh:split * (hh + 1)] = blk[:, 0:split]
                        dq_ref[pl.ds(r0, tq), hpg * split + rest * hh:hpg * split + rest * (hh + 1)] = blk[:, split:]
                return carry

            lax.fori_loop(0, nq, emit, 0)

        if hosted:
            pl.when(last)(lambda: exchange.wait(xs_refs, land_refs, sems))

    kv_of = lambda g: g * g_kv // gq
    in_specs = [pl.BlockSpec((hpg, tqq, dq), lambda g, kj, qi: (g, qi, 0)),
                pl.BlockSpec((1, tk, dq), lambda g, kj, qi: (kv_of(g), kj, 0)),
                pl.BlockSpec((1, dq, tk), lambda g, kj, qi: (kv_of(g), 0, kj))]
    args = [q, k, kt]
    if has_v:
        in_specs.append(pl.BlockSpec((1, tk, dv), lambda g, kj, qi: (kv_of(g), kj, 0)))
        args.append(v)
    row_map = lambda g, kj, qi: (g, 0, qi)
    in_specs += [pl.BlockSpec((tqq, hpg * dv), lambda g, kj, qi: (qi, g)),
                 pl.BlockSpec((hpg, 1, tqq), row_map), pl.BlockSpec((hpg, 1, tqq), row_map)]
    args += [do, lse, delta]
    if hosted:
        in_specs += exchange.in_specs
        args += exchange.srcs
    sd = jax.ShapeDtypeStruct
    return pl.pallas_call(
        body, name=name, grid=grid,
        out_shape=(sd((s, hq * dq), F32), sd((gq, s, dq), F32), sd((gq, s, dv), F32))
        + (exchange.land_shapes if hosted else ()),
        in_specs=in_specs,
        out_specs=(pl.BlockSpec((s, hpg * dq), lambda g, kj, qi: (0, g)),
                   pl.BlockSpec((1, tk, dq), lambda g, kj, qi: (g, kj, 0)),
                   pl.BlockSpec((1, tk, dv), lambda g, kj, qi: (g, kj, 0))) + (exchange.out_specs if hosted else ()),
        scratch_shapes=[pltpu.VMEM((nq, hpg * dq, tq), F32)] + (list(exchange.sems) if hosted else []),
        compiler_params=_params(("arbitrary",) * 3 if hosted else ("parallel", "arbitrary", "arbitrary")),
    )(*args)


def mixer_out_backward(dx, y, mod, pairs, w_out, delta_heads, name, lse=None, sink=None):
    s, d = dx.shape
    tm = min(ROW_TILE, s)
    n = len(pairs)
    widths = [o.shape[1] for o, _ in pairs]
    n_delta = sum(1 for h in delta_heads if h)
    with_sink = lse is not None

    def body(*refs):
        it = iter(refs)
        dx_ref, y_ref, mod_ref, wt_ref = next(it), next(it), next(it), next(it)
        pr = [next(it) for _ in range(2 * n)]
        lse_ref = next(it) if with_sink else None
        sink_ref = next(it) if with_sink else None
        outs = [next(it) for _ in range(2 * n)]
        dl_refs = [next(it) for _ in range(n_delta)]
        dgate_ref, dw_ref = next(it), next(it)
        dsink_ref = next(it) if with_sink else None

        @pl.when(pl.program_id(0) == 0)
        def _():
            dgate_ref[...] = jnp.zeros(dgate_ref.shape, F32)
            dw_ref[...] = jnp.zeros(dw_ref.shape, F32)
            if with_sink:
                dsink_ref[...] = jnp.zeros(dsink_ref.shape, F32)

        dxo = dx_ref[...]
        dgate_ref[...] += jnp.sum(dxo * y_ref[...].astype(F32), axis=0, keepdims=True)
        dy = (dxo * mod_ref[2:3, :]).astype(MXU)
        dmix = _mm_nt(dy, wt_ref[...])
        r0 = 0
        di = 0
        for i in range(n):
            o = pr[2 * i][...].astype(F32)
            g = pr[2 * i + 1][...].astype(F32)
            dm = dmix[:, r0:r0 + widths[i]]
            sg = _sigmoid(g)
            act = g * sg
            do = dm * act
            outs[2 * i][...] = do.astype(MXU)
            outs[2 * i + 1][...] = (dm * o * (sg * (1.0 + g * (1.0 - sg)))).astype(MXU)
            dw_ref[r0:r0 + widths[i], :] += _mm_tn(o * act, dy)
            if delta_heads[i]:
                dlt = _group_sums_t(do * o, HD)[0:delta_heads[i], :]
                dl_refs[di][...] = dlt
                if with_sink:
                    ps = jnp.exp2(sink_ref[...] - lse_ref[...])
                    dsink_ref[...] += -jnp.sum(ps * dlt, axis=1, keepdims=True)
                di += 1
            r0 += widths[i]

    flat = [a for p in pairs for a in p]
    sd = jax.ShapeDtypeStruct
    in_specs = [_row_spec(tm, d), _row_spec(tm, d), _full_spec(mod.shape), _full_spec(w_out.shape)]
    in_specs += [_row_spec(tm, a.shape[1]) for a in flat]
    args = [dx, y, mod, w_out] + flat
    if with_sink:
        nh = lse.shape[0]
        in_specs += [_rows_spec(nh, tm), _full_spec(sink.shape)]
        args += [lse, sink]
    out_shape = [sd((s, a.shape[1]), MXU) for a in flat]
    out_specs = [_row_spec(tm, a.shape[1]) for a in flat]
    for h in delta_heads:
        if h:
            out_shape.append(sd((h, s), F32))
            out_specs.append(_rows_spec(h, tm))
    out_shape += [sd((1, d), F32), sd((sum(widths), d), F32)]
    out_specs += [_full_spec((1, d)), _full_spec((sum(widths), d))]
    if with_sink:
        out_shape.append(sd((lse.shape[0], 1), F32))
        out_specs.append(_full_spec((lse.shape[0], 1)))
    return pl.pallas_call(
        body, name=name, grid=(s // tm,), out_shape=tuple(out_shape), in_specs=in_specs, out_specs=tuple(out_specs),
        compiler_params=_params(("arbitrary",)),
    )(*args)


def latent_out_backward(d_ob, o_lat, w_uv):
    s = o_lat.shape[0]
    tm = min(ROW_TILE, s)

    def body(d_ref, o_ref, uv_ref, dol_ref, dl_ref, duv_ref, prod_s):
        @pl.when(pl.program_id(0) == 0)
        def _():
            duv_ref[...] = jnp.zeros(duv_ref.shape, F32)

        for hh in range(B_HEADS):
            dh = d_ref[:, HD * hh:HD * hh + HD]
            ol = o_ref[:, B_KV_LORA * hh:B_KV_LORA * (hh + 1)].astype(F32)
            dol = _mm_nt(dh, uv_ref[hh])
            dol_ref[:, B_KV_LORA * hh:B_KV_LORA * (hh + 1)] = dol.astype(MXU)
            prod_s[:, B_KV_LORA * hh:B_KV_LORA * (hh + 1)] = dol * ol
            duv_ref[:, HD * hh:HD * hh + HD] += _mm_tn(ol, dh)
        dl_ref[...] = _group_sums_t(prod_s[...], B_KV_LORA)[0:B_HEADS, :]

    sd = jax.ShapeDtypeStruct
    duv_shape = (B_KV_LORA, B_HEADS * HD)
    return pl.pallas_call(
        body, name="latent_out_backward", grid=(s // tm,),
        out_shape=(sd(o_lat.shape, MXU), sd((B_HEADS, s), F32), sd(duv_shape, F32)),
        in_specs=[_row_spec(tm, d_ob.shape[1]), _row_spec(tm, o_lat.shape[1]), _full_spec(w_uv.shape)],
        out_specs=(_row_spec(tm, o_lat.shape[1]), _rows_spec(B_HEADS, tm), _full_spec(duv_shape)),
        scratch_shapes=[pltpu.VMEM((tm, o_lat.shape[1]), F32)],
        compiler_params=_params(("arbitrary",)),
    )(d_ob, o_lat, w_uv)


def even_prep_backward(dqa, dka, dva, dqb, dkb, dvb, qa_raw, ka_raw, cq_raw, ckv_raw,
                       gq, gk, qln, kvln, w_uq_t, uk_bd, bd, cos_a, sin_a, cos_t, sin_t):
    s = qa_raw.shape[0]
    tm = min(ROW_TILE, s)
    half_lat = B_KV_LORA * B_HEADS // 2
    half_w = dqb.shape[1] // 2

    def body(dqa_ref, dka_ref, dva_ref, dqb_ref, dkb_ref, dvb_ref, qa_ref, ka_ref, cq_ref, ckv_ref,
             gq_ref, gk_ref, qln_ref, kvln_ref, uqt_ref, ukbd_ref, bd_ref, ca_ref, sa_ref, ct_ref, st_ref,
             pqa, pka, pva, pcq, pckv, pkr, gqn, gkn, gqln, gkvln, guq, guk):
        @pl.when(pl.program_id(0) == 0)
        def _():
            for r in (gqn, gkn, gqln, gkvln, guq, guk):
                r[...] = jnp.zeros(r.shape, F32)

        ca, sa, ct, st = ca_ref[...], sa_ref[...], ct_ref[...], st_ref[...]
        wide = lambda t, n: jnp.concatenate([t] * n, axis=1)
        rows = lambda a: jnp.sum(a, axis=0, keepdims=True)
        dx, dg = _head_norm_bwd(_rope_t(dqa_ref[...], wide(ca, 4), wide(sa, 4), 32), qa_ref[...], gq_ref[...],
                                bd_ref, HD)
        pqa[...] = dx.astype(MXU)
        gqn[...] += rows(dg)
        dk_all = jnp.concatenate([dka_ref[g] for g in range(A_KV)], axis=1)
        dx, dg = _head_norm_bwd(_rope_t(dk_all, ca, sa, 32), ka_ref[...], gk_ref[...], bd_ref[0:128, 0:128], HD)
        pka[...] = dx.astype(MXU)
        gkn[...] += rows(dg)
        pva[...] = jnp.concatenate([dva_ref[g] for g in range(A_KV)], axis=1).astype(MXU)
        cq_raw = cq_ref[...]
        cq_n = cq_raw * _rms(cq_raw) * qln_ref[...]
        qb = _mm_nt(cq_n, uqt_ref[...])
        d_lat = jnp.concatenate([dqb_ref[:, 0:half_lat], dqb_ref[:, half_w:half_w + half_lat]], axis=1)
        d_rope = jnp.concatenate([dqb_ref[:, half_lat:half_w], dqb_ref[:, half_w + half_lat:]], axis=1)
        for hh in range(B_HEADS):
            guk[:, B_NOPE * hh:B_NOPE * (hh + 1)] += _mm_tn(d_lat[:, B_KV_LORA * hh:B_KV_LORA * (hh + 1)],
                                                            qb[:, B_NOPE * hh:B_NOPE * (hh + 1)])
        dqb_all = jnp.concatenate([_mm_nt(d_lat, ukbd_ref[...]),
                                   _rope_t(d_rope, wide(ct, 2), wide(st, 2), 32)], axis=1)
        guq[...] += _mm_tn(dqb_all, cq_n)
        dx, dg = _rms_bwd(_mm(dqb_all, uqt_ref[...]), cq_raw, qln_ref[...])
        pcq[...] = dx.astype(MXU)
        gqln[...] += rows(dg)
        dkb_sum = dkb_ref[0] + dkb_ref[1]
        dckv = dkb_sum[:, 0:B_KV_LORA] + dvb_ref[0] + dvb_ref[1]
        dx, dg = _rms_bwd(dckv, ckv_ref[...], kvln_ref[...])
        pckv[...] = dx.astype(MXU)
        gkvln[...] += rows(dg)
        pkr[...] = _rope_t(dkb_sum[:, B_KV_LORA:B_QK], ct[:, 0:B_ROPE], st[:, 0:B_ROPE], 32).astype(MXU)

    sd = jax.ShapeDtypeStruct
    consts = [gq, gk, qln, kvln, w_uq_t, uk_bd, bd]
    in_specs = [_row_spec(tm, 512), _head_spec(A_KV, tm, HD), _head_spec(A_KV, tm, HD),
                _row_spec(tm, dqb.shape[1]), _head_spec(2, tm, B_QK), _head_spec(2, tm, B_KV_LORA),
                _row_spec(tm, 512), _row_spec(tm, 128), _row_spec(tm, B_Q_LORA), _row_spec(tm, B_KV_LORA)]
    in_specs += [_full_spec(a.shape) for a in consts] + [_row_spec(tm, 128)] * 4
    small = [sd(gq.shape, F32), sd(gk.shape, F32), sd(qln.shape, F32), sd(kvln.shape, F32), sd(w_uq_t.shape, F32),
             sd((B_KV_LORA, B_HEADS * B_NOPE), F32)]
    out_shape = (sd((s, 512), MXU), sd((s, 128), MXU), sd((s, 128), MXU), sd((s, B_Q_LORA), MXU),
                 sd((s, B_KV_LORA), MXU), sd((s, B_ROPE), MXU), *small)
    out_specs = (_row_spec(tm, 512), _row_spec(tm, 128), _row_spec(tm, 128), _row_spec(tm, B_Q_LORA),
                 _row_spec(tm, B_KV_LORA), _row_spec(tm, B_ROPE), *[_full_spec(a.shape) for a in small])
    return pl.pallas_call(
        body, name="even_prep_backward", grid=(s // tm,), out_shape=out_shape, in_specs=in_specs, out_specs=out_specs,
        compiler_params=_params(("arbitrary",)),
    )(dqa, dka, dva, dqb, dkb, dvb, qa_raw, ka_raw, cq_raw, ckv_raw, *consts, cos_a, sin_a, cos_t, sin_t)


def in_proj_backward(x, mod, nw, pieces, name, *, dx_out=None, w_in_t=None, dw_rows=None, exchange=None):
    s, d = x.shape
    tm = min(ROW_TILE, s)
    grid = (s // tm,)
    n = len(pieces)
    cols = [c for _, c in pieces]
    want_dx = w_in_t is not None
    want_dw = dw_rows is not None
    n_cols = sum(c1 - c0 for c0, c1 in cols)
    dw_block = n_cols // N_DEV
    hosted = exchange is not None
    nx = exchange.n if hosted else 0

    def body(*refs):
        it = iter(refs)
        x_ref, mod_ref, nw_ref = next(it), next(it), next(it)
        dxo_ref, wt_ref = (next(it), next(it)) if want_dx else (None, None)
        p_refs = [next(it) for _ in range(n)]
        xs_refs = [next(it) for _ in range(nx)]
        dx_ref, dv_ref = (next(it), next(it)) if want_dx else (None, None)
        dw_ref = next(it) if want_dw else None
        land_refs = [next(it) for _ in range(nx)]
        acc_ref = next(it) if want_dx else None
        dw_acc = next(it) if want_dw else None
        sems = list(it)
        first, last = _grid_edges(grid)
        if hosted:
            pl.when(first)(lambda: exchange.start(xs_refs, land_refs, sems))

        @pl.when(first)
        def _():
            if want_dw:
                dw_acc[...] = jnp.zeros(dw_acc.shape, F32)
            if want_dx:
                acc_ref[...] = jnp.zeros(acc_ref.shape, F32)

        xn, g1, h = _modulated(x_ref[...], mod_ref, nw_ref)
        hb = h.astype(MXU)
        dh = jnp.zeros((tm, d), F32)
        for k, (pr, (c0, c1)) in enumerate(zip(p_refs, cols)):
            if len(pr.shape) == 3:
                pc = jnp.concatenate([pr[g] for g in range(pr.shape[0])], axis=1).astype(MXU)
            else:
                pc = pr[...].astype(MXU)
            if want_dx:
                dh = dh + jnp.dot(pc, wt_ref[c0:c1, :], preferred_element_type=F32)
            if want_dw:
                r0, r1 = dw_rows[k]
                dw_acc[r0:r1, :] += _mm_tn(pc, hb)
        if want_dx:
            acc_ref[0:1, :] += jnp.sum(dh, axis=0, keepdims=True)
            acc_ref[1:2, :] += jnp.sum(dh * xn, axis=0, keepdims=True)
            dxn = dh * g1
            x = x_ref[...]
            dx_ref[...] = dxo_ref[...] + _rms(x) * (dxn - xn * jnp.mean(dxn * xn, axis=-1, keepdims=True))

        @pl.when(last)
        def _():
            if want_dx:
                dg1 = acc_ref[1:2, :]
                dv_ref[0:1, :] = acc_ref[0:1, :]
                dv_ref[1:2, :] = dg1 * nw_ref[...]
                dv_ref[2:3, :] = dg1 * (1.0 + mod_ref[1:2, :])
                dv_ref[3:4, :] = jnp.zeros((1, d), F32)
            if want_dw:
                for j in range(N_DEV):
                    dw_ref[j] = dw_acc[j * dw_block:(j + 1) * dw_block, :].astype(MXU)

        if hosted:
            pl.when(last)(lambda: exchange.wait(xs_refs, land_refs, sems))

    arrs = [a for a, _ in pieces]
    sd = jax.ShapeDtypeStruct
    args = [x, mod, nw] + ([dx_out, w_in_t] if want_dx else []) + arrs + (exchange.srcs if hosted else [])
    in_specs = [_row_spec(tm, d), _full_spec(mod.shape), _full_spec(nw.shape)]
    in_specs += [_row_spec(tm, d), _full_spec(w_in_t.shape)] if want_dx else []
    in_specs += [_row_spec(tm, a.shape[1]) if a.ndim == 2 else _head_spec(a.shape[0], tm, a.shape[2]) for a in arrs]
    in_specs += exchange.in_specs if hosted else []
    out_shape, out_specs, scratch = [], [], []
    if want_dx:
        out_shape += [sd((s, d), F32), sd((4, d), F32)]
        out_specs += [_row_spec(tm, d), _full_spec((4, d))]
        scratch.append(pltpu.VMEM((8, d), F32))
    if want_dw:
        out_shape.append(sd((N_DEV, dw_block, d), MXU))
        out_specs.append(_full_spec((N_DEV, dw_block, d)))
        scratch.append(pltpu.VMEM((n_cols, d), F32))
    if hosted:
        out_shape += list(exchange.land_shapes)
        out_specs += list(exchange.out_specs)
        scratch += list(exchange.sems)
    return pl.pallas_call(
        body, name=name, grid=grid, out_shape=tuple(out_shape), in_specs=in_specs, out_specs=tuple(out_specs),
        scratch_shapes=scratch, compiler_params=_params(("arbitrary",)),
    )(*args)


def ada_weight_grad(c_all, dmod_cols):
    d = c_all.shape[1]
    w = dmod_cols.shape[2]

    def body(c_ref, dm_ref, out_ref):
        ca = _silu(c_ref[...])
        for l in range(2):
            out_ref[l] = _mm_tn(ca, dm_ref[l])

    return pl.pallas_call(
        body, name="ada_weight_grad",
        out_shape=jax.ShapeDtypeStruct((2, d, w), F32),
        compiler_params=pltpu.CompilerParams(vmem_limit_bytes=VMEM_LIMIT),
    )(c_all, dmod_cols)


def _slot_sum(g_ref):
    g = g_ref[0].astype(F32)
    for k in range(1, g_ref.shape[0]):
        g = g + g_ref[k].astype(F32)
    return g


def _adamw_math(g, w, m, v):
    m_new = ADAM_B1 * m + (1.0 - ADAM_B1) * g
    v_new = ADAM_B2 * v + (1.0 - ADAM_B2) * (g * g)
    m_hat = m_new / (1.0 - ADAM_B1 ** ADAM_STEP)
    v_hat = v_new / (1.0 - ADAM_B2 ** ADAM_STEP)
    return -ADAM_LR * (m_hat / (jnp.sqrt(v_hat) + ADAM_EPS) + ADAM_WD * w), m_new, v_new


def adamw_small(g_alls, ws, ms, vs, loss_all):
    n = len(ws)

    def body(*refs):
        g_refs, w_refs, m_refs, v_refs = (refs[i * n:(i + 1) * n] for i in range(4))
        loss_ref = refs[4 * n]
        outs = refs[4 * n + 1:]
        for i in range(n):
            g = _slot_sum(g_refs[i])
            outs[i][...] = g
            outs[n + i][...], outs[2 * n + i][...], outs[3 * n + i][...] = _adamw_math(
                g, w_refs[i][...], m_refs[i][...], v_refs[i][...])
        outs[4 * n][...] = _slot_sum(loss_ref)

    sds = [jax.ShapeDtypeStruct(w.shape, F32) for w in ws]
    res = pl.pallas_call(
        body, name="adamw_small", out_shape=tuple(sds * 4) + (jax.ShapeDtypeStruct(loss_all.shape[1:], F32),),
        compiler_params=pltpu.CompilerParams(vmem_limit_bytes=VMEM_LIMIT),
    )(*g_alls, *ws, *ms, *vs, loss_all)
    return [res[i * n:(i + 1) * n] for i in range(4)], res[4 * n]


def adamw_rows(g_slots, w, m, v, name):
    n, r, lanes = g_slots.shape
    fits = [t for t in range(16, r + 1, 16) if r % t == 0 and t * lanes <= ADAM_TILE]
    tr = max(fits) if fits else r
    def body(g_ref, w_ref, m_ref, v_ref, go, do, mo, vo):
        g = _slot_sum(g_ref)
        go[...] = g
        do[...], mo[...], vo[...] = _adamw_math(g, w_ref[...], m_ref[...], v_ref[...])

    row = pl.BlockSpec((tr, lanes), lambda i: (i, 0))
    sd = jax.ShapeDtypeStruct((r, lanes), F32)
    return pl.pallas_call(
        body, name=name, grid=(r // tr,), out_shape=(sd, sd, sd, sd),
        in_specs=[pl.BlockSpec((n, tr, lanes), lambda i: (0, i, 0)), row, row, row],
        out_specs=(row, row, row, row),
        compiler_params=_params(("parallel",)),
    )(g_slots, w, m, v)


def adamw_cols(g_slots, w, m, v, name):
    n, r, c = g_slots.shape
    tc = min(ADAM_COLS, c)

    def body(g_ref, w_ref, m_ref, v_ref, go, do, mo, vo):
        g = _slot_sum(g_ref)
        eye = (lax.broadcasted_iota(jnp.int32, (tc, tc), 0) == lax.broadcasted_iota(jnp.int32, (tc, tc), 1)).astype(MXU)
        gt = jnp.zeros((tc, r), F32)
        for _ in range(3):
            part = g.astype(MXU)
            gt = gt + _mm_nt(eye, part)
            g = g - part.astype(F32)
        go[...] = gt
        do[...], mo[...], vo[...] = _adamw_math(gt, w_ref[...], m_ref[...], v_ref[...])

    col = pl.BlockSpec((tc, r), lambda i: (i, 0))
    sd = jax.ShapeDtypeStruct((c, r), F32)
    return pl.pallas_call(
        body, name=name, grid=(c // tc,), out_shape=(sd, sd, sd, sd),
        in_specs=[pl.BlockSpec((n, r, tc), lambda i: (0, 0, i)), col, col, col],
        out_specs=(col, col, col, col),
        compiler_params=_params(("parallel",)),
    )(g_slots, w, m, v)


def _rope_tables(s):
    def cs(pos, dim):
        inv = ROPE_THETA ** (-np.arange(0, dim, 2, dtype=np.float32) / dim)
        ang = pos.astype(np.float32)[:, None] * inv.astype(np.float32)[None, :]
        return np.cos(ang), np.sin(ang)

    rows = s // GRID_W
    row = np.repeat(np.arange(rows), GRID_W)
    col = np.tile(np.arange(GRID_W), rows)
    cr, sr = cs(row, HD // 2)
    cc, sc = cs(col, HD // 2)
    ct, st = cs(np.arange(s), B_ROPE)
    tables = (np.concatenate([cr, cr, cc, cc] * 2, axis=-1), np.concatenate([-sr, sr, -sc, sc] * 2, axis=-1),
              np.concatenate([ct, ct] * 4, axis=-1), np.concatenate([-st, st] * 4, axis=-1))
    return tuple(jnp.asarray(t, F32) for t in tables)


def _even_rows_to_kernel(wt):
    return jnp.concatenate([wt[:1664], wt[1696:], wt[1664:1696]], axis=0)


def _uq_rows_to_kernel(wt):
    r = wt.reshape(B_HEADS, B_NOPE + B_ROPE, -1)
    return jnp.concatenate([r[:, :B_NOPE].reshape(B_HEADS * B_NOPE, -1), r[:, B_NOPE:].reshape(B_HEADS * B_ROPE, -1)])


def _uq_rows_to_reference(wt):
    nope = wt[:B_HEADS * B_NOPE].reshape(B_HEADS, B_NOPE, -1)
    rope = wt[B_HEADS * B_NOPE:].reshape(B_HEADS, B_ROPE, -1)
    return jnp.concatenate([nope, rope], axis=1).reshape(B_HEADS * (B_NOPE + B_ROPE), -1)


def _shard_t(w):
    return jnp.transpose(w[0])


def kernel(x, c, norm_w, ada_w, ada_b, even_w_in, a_q_norm, a_k_norm, b_q_lora_norm, b_kv_lora_norm, b_w_uq, b_w_uk, b_w_uv, even_w_out, odd_w_in, c_sink, odd_w_out, final_norm, loss_target, m_norm_w, m_ada_w, m_ada_b, m_even_w_in, m_a_q_norm, m_a_k_norm, m_b_q_lora_norm, m_b_kv_lora_norm, m_b_w_uq, m_b_w_uk, m_b_w_uv, m_even_w_out, m_odd_w_in, m_c_sink, m_odd_w_out, m_final_norm, v_norm_w, v_ada_w, v_ada_b, v_even_w_in, v_a_q_norm, v_a_k_norm, v_b_q_lora_norm, v_b_kv_lora_norm, v_b_w_uq, v_b_w_uk, v_b_w_uv, v_even_w_out, v_odd_w_in, v_c_sink, v_odd_w_out, v_final_norm):
    s, d = x.shape[1], x.shape[2]
    x0 = x[0]
    target = loss_target[0]
    me_flat = 4 * lax.axis_index("x") + 2 * lax.axis_index("y") + lax.axis_index("c")

    wcols = ada_w.shape[2]
    bias_cols = lax.dynamic_slice_in_dim(ada_b.reshape(2, N_DEV, wcols), me_flat, 1, axis=1)
    call, modp, (g_in_e, g_uq) = ada_forward(
        jnp.broadcast_to(c, (8, d)), ada_w, bias_cols,
        Gather([_shard_t(even_w_in).astype(MXU), _shard_t(b_w_uq).astype(MXU)]))
    wt_in_e = _even_rows_to_kernel(g_in_e.reshape(-1, d))
    wt_uq = _uq_rows_to_kernel(g_uq.reshape(-1, B_Q_LORA))
    later_exchange = Exchange([_shard_t(odd_w_in).astype(MXU), even_w_out[0].astype(MXU),
                               odd_w_out[0].astype(MXU)], scatter=False)
    uk_bd = (jnp.eye(B_HEADS, dtype=F32)[:, None, :, None] * jnp.transpose(b_w_uk[0], (1, 2, 0))[:, :, None, :]
             ).reshape(B_HEADS * B_NOPE, B_HEADS * B_KV_LORA).astype(MXU)
    head_bd = jnp.asarray(np.kron(np.eye(A_HEADS), np.ones((HD, HD))), MXU)
    gq_full, gk_full = jnp.tile(a_q_norm, (1, A_HEADS)), jnp.tile(a_k_norm, (1, A_KV))
    w_uv = jnp.transpose(b_w_uv[0], (1, 0, 2)).astype(MXU)

    c_all = call[:, 0, :]
    mod = jnp.transpose(modp[:, :, 0, :], (1, 0, 2)).reshape(2, 3, d)
    mod_e, mod_o = mod[0], mod[1]
    nw_e, nw_o = norm_w[0:1], norm_w[1:2]

    cos_a, sin_a, cos_t, sin_t = _rope_tables(s)
    slopes = (2.0 ** (-8.0 * jnp.arange(1, C_HEADS + 1, dtype=F32) / C_HEADS)).reshape(C_HEADS, 1, 1)
    sink2 = c_sink.reshape(C_HEADS, 1, 1) * LOG2E

    (qa, ka, va, qb, kb, kat, vat, kbt, qa_raw, ka_raw, cq_raw, ckv_raw, ga, gb) = even_in_forward(
        x0, mod_e, nw_e, wt_in_e, gq_full, gk_full, b_q_lora_norm, b_kv_lora_norm, wt_uq, uk_bd, head_bd,
        cos_a, sin_a, cos_t, sin_t)
    tk_dense = min(512, s)
    tq_dense = min(256, s)
    fwd_sub = min(8, s // tk_dense)
    bwd_sub_a = min(16, s // tq_dense)
    bwd_sub_b = min(8, s // tq_dense)
    oa, lse_a, g_in_o, g_out_e, g_out_o = flash_forward(
        qa, ka, vat, dv=HD, tq=tq_dense, tk=tk_dense, nsub=fwd_sub, name="attn_a_fwd",
        exchange=later_exchange)
    wt_in_o = g_in_o.reshape(-1, d)
    w_out_e = g_out_e.reshape(-1, d)
    w_out_o = g_out_o.reshape(-1, d)
    o_lat, lse_b = flash_forward(qb, kb, kbt, dv=B_KV_LORA, tq=min(128, s), tk=tk_dense, nsub=fwd_sub,
                                 name="attn_b_fwd")
    ob = latent_out_forward(o_lat, w_uv)
    x1, y_e = mixer_out_forward(x0, mod_e, [(oa, ga), (ob, gb)], w_out_e, "even_out_fwd")

    qc, kc, vc, kct, vct, gc = odd_in_forward(x1, mod_o, nw_o, wt_in_o)
    win_sub = min(8, s // WINDOW)
    oc, lse_c = window_forward(qc, kc, vct, sink2, slopes, win_sub, "attn_c_fwd")
    dx2, y_o, loss_lanes, d_final = mixer_out_forward(x1, mod_o, [(oc, gc)], w_out_o, "odd_out_fwd_loss",
                                                      loss=(target, final_norm.reshape(1, d)))

    loss_part = (0.5 / d) * jnp.sum(loss_lanes)

    doc, dgc, delta_c, dgate_o, dw_out_o, dsink = mixer_out_backward(
        dx2, y_o, mod_o, [(oc, gc)], w_out_o, [C_HEADS], "odd_out_bwd", lse=lse_c.reshape(C_HEADS, s),
        sink=sink2.reshape(C_HEADS, 1))
    rows3 = lambda t: t.reshape(t.shape[0], 1, s)
    dqc, dkc, dvc = window_backward(qc, kc, kct, vc, doc, lse_c, rows3(delta_c), slopes, win_sub, "attn_c_bwd")
    dx1, dvec_o, dwt_in_o = in_proj_backward(
        x1, mod_o, nw_o, [(dqc, O_Q), (dkc, O_K), (dvc, O_V), (dgc, O_G)], "odd_in_bwd",
        dx_out=dx2, w_in_t=wt_in_o, dw_rows=[O_Q, O_K, O_V, O_G])

    doa, dga, dob, dgb, delta_a, dgate_e, dw_out_e = mixer_out_backward(
        dx1, y_e, mod_e, [(oa, ga), (ob, gb)], w_out_e, [A_HEADS, 0], "even_out_bwd")
    d_olat, delta_b, dw_uv = latent_out_backward(dob, o_lat, w_uv)
    blocks = lambda g: g.astype(MXU).reshape(N_DEV, g.shape[0] // N_DEV, g.shape[1])
    even_pieces = lambda: [(pqa, E_QA), (pka, E_KA), (pva, E_VA), (dga, E_GA), (pcq, E_CQ), (pckv, E_CKV),
                           (dgb, E_GB), (pkr, E_KR)]
    scatter_odd = Exchange([dwt_in_o, blocks(dw_out_o)], True)
    scatter_out_e = Exchange([blocks(dw_out_e)], True)
    dqb, dkb, dvb, l_in_o, l_out_o = flash_backward(
        qb, kb, kbt, None, d_olat, lse_b, rows3(delta_b), scale=SCALE_B, dv=B_KV_LORA,
        tq=tq_dense, tk=tk_dense, nsub=bwd_sub_b, gq=2, name="attn_b_bwd", split=B_KV_LORA, exchange=scatter_odd)
    dqa, dka, dva, l_out_e = flash_backward(
        qa, ka, kat, va, doa, lse_a, rows3(delta_a), scale=SCALE_A, dv=HD,
        tq=tq_dense, tk=tk_dense, nsub=bwd_sub_a, gq=A_KV, name="attn_a_bwd", exchange=scatter_out_e)
    (pqa, pka, pva, pcq, pckv, pkr, g_qn, g_kn, g_qln, g_kvln, dwt_uq, dw_uk) = even_prep_backward(
        dqa, dka, dva, dqb, dkb, dvb, qa_raw, ka_raw, cq_raw, ckv_raw,
        gq_full, gk_full, b_q_lora_norm, b_kv_lora_norm, wt_uq, uk_bd, head_bd, cos_a, sin_a, cos_t, sin_t)
    g_qn = jnp.sum(g_qn.reshape(A_HEADS, HD), axis=0)
    g_kn = jnp.sum(g_kn.reshape(A_KV, HD), axis=0)
    dwt_in_e, l_uk, l_uv = in_proj_backward(
        x0, mod_e, nw_e, even_pieces(), "even_in_bwd_dw",
        dw_rows=[E_QA, E_KA, E_VA, E_GA, E_CQ, E_CKV, (1696, 2208), (1664, 1696)],
        exchange=Exchange([dw_uk.astype(MXU), dw_uv.astype(MXU)], scatter=False))
    dx0, dvec_e, l_in_e, l_uq = in_proj_backward(
        x0, mod_e, nw_e, even_pieces(), "even_in_bwd_dx", dx_out=dx1, w_in_t=wt_in_e,
        exchange=Exchange([dwt_in_e, blocks(_uq_rows_to_reference(dwt_uq))], True))

    dmod = jnp.stack([jnp.concatenate([dvec_e[0], dvec_e[1], dgate_e[0]]),
                      jnp.concatenate([dvec_o[0], dvec_o[1], dgate_o[0]])])
    d_norm_w = jnp.stack([dvec_e[2], dvec_o[2]])
    small_names = ["norm_w", "ada_b", "a_q_norm", "a_k_norm", "b_q_lora_norm", "b_kv_lora_norm", "b_w_uk", "b_w_uv",
                   "c_sink", "final_norm"]
    small_w = [norm_w, ada_b, a_q_norm, a_k_norm, b_q_lora_norm, b_kv_lora_norm, b_w_uk, b_w_uv, c_sink, final_norm]
    small_m = [m_norm_w, m_ada_b, m_a_q_norm, m_a_k_norm, m_b_q_lora_norm, m_b_kv_lora_norm, m_b_w_uk, m_b_w_uv,
               m_c_sink, m_final_norm]
    small_v = [v_norm_w, v_ada_b, v_a_q_norm, v_a_k_norm, v_b_q_lora_norm, v_b_kv_lora_norm, v_b_w_uk, v_b_w_uv,
               v_c_sink, v_final_norm]
    small_g = [d_norm_w, dmod, g_qn, g_kn, g_qln, g_kvln, None, None, dsink, d_final]
    flat2 = lambda a: a.reshape((1, -1)) if a.size == a.shape[-1] else a.reshape(a.shape[-3:] if a.ndim > 3 else a.shape)
    kshape = [flat2(w).shape for w in small_w]
    late = [i for i, g in enumerate(small_g) if g is not None]
    gathered = all_gather_slots(
        Gather([small_g[i].reshape(kshape[i]) for i in late] + [jnp.full((8, 128), loss_part, F32)]),
        "gather_small_grads")
    g_all = [None] * len(small_g)
    for i, g in zip(late, gathered):
        g_all[i] = g
    g_all[6], g_all[7] = (l.reshape((N_DEV,) + kshape[6]) for l in (l_uk, l_uv))
    sm_out, loss_sum = adamw_small(g_all, [flat2(a) for a in small_w], [flat2(a) for a in small_m],
                                   [flat2(a) for a in small_v], gathered[-1])
    loss = loss_sum[0, 0]
    sm = [{nm: p.reshape(w.shape) for nm, w, p in zip(small_names, small_w, outs)} for outs in sm_out]

    dmod_all = g_all[1].reshape(N_DEV, 2, N_DEV, wcols)
    dmod_cols = lax.dynamic_slice_in_dim(dmod_all, me_flat, 1, axis=2)[:, :, 0, :]
    pad16 = lambda a: jnp.concatenate([a, jnp.zeros_like(a)], axis=0)
    g_ada_w = ada_weight_grad(pad16(c_all), jnp.transpose(pad16(dmod_cols), (1, 0, 2)))
    rows_of = lambda a: a.reshape(-1, wcols)
    ada = adamw_rows(rows_of(g_ada_w)[None], rows_of(ada_w), rows_of(m_ada_w), rows_of(v_ada_w), "adamw_ada_w")
    ada = [p.reshape(ada_w.shape) for p in ada]

    bg = [{}, {}, {}, {}]
    for nm, landed, w, m, v, transposed in (
            ("even_w_in", l_in_e, even_w_in, m_even_w_in, v_even_w_in, True),
            ("b_w_uq", l_uq, b_w_uq, m_b_w_uq, v_b_w_uq, True),
            ("odd_w_in", l_in_o, odd_w_in, m_odd_w_in, v_odd_w_in, True),
            ("even_w_out", l_out_e, even_w_out, m_even_w_out, v_even_w_out, False),
            ("odd_w_out", l_out_o, odd_w_out, m_odd_w_out, v_odd_w_out, False)):
        res = (adamw_cols if transposed else adamw_rows)(landed, w[0], m[0], v[0], "adamw_" + nm)
        for kind, p in enumerate(res):
            bg[kind][nm] = p[None]
    big_names = ["even_w_in", "odd_w_in", "even_w_out", "odd_w_out", "b_w_uq"]

    order = ["norm_w", "ada_w", "ada_b", "even_w_in", "a_q_norm", "a_k_norm", "b_q_lora_norm", "b_kv_lora_norm",
             "b_w_uq", "b_w_uk", "b_w_uv", "even_w_out", "odd_w_in", "c_sink", "odd_w_out", "final_norm"]

    def pick(kind):
        out = []
        for nm in order:
            if nm == "ada_w":
                out.append(ada[kind])
            elif nm in big_names:
                out.append(bg[kind][nm])
            else:
                out.append(sm[kind][nm])
        return out

    return (loss, dx0[None], *pick(0), *pick(1), *pick(2), *pick(3))
```

```python
import functools

import jax
import jax.numpy as jnp
import numpy as np
from jax import lax
from jax.experimental import pallas as pl
from jax.experimental.pallas import tpu as pltpu

F32 = jnp.float32
MXU = jnp.bfloat16
EPS = 1e-6
ROPE_THETA = 10000.0
GRID_W = 64
HD = 64
N_DEV = 8

A_HEADS, A_KV = 8, 2
B_HEADS, B_NOPE, B_ROPE, B_Q_LORA, B_KV_LORA = 8, 64, 32, 256, 128
B_QK = B_KV_LORA + B_ROPE
C_HEADS, C_KV = 16, 4
WINDOW = 128

ADAM_LR, ADAM_B1, ADAM_B2, ADAM_EPS, ADAM_WD, ADAM_STEP = 0.001, 0.9, 0.999, 1e-08, 0.01, 10

ROW_TILE = 512
ADAM_TILE = 2048 * 128
LOG2E = 1.4426950408889634
SCALE_A = HD ** -0.5
SCALE_B = (B_NOPE + B_ROPE) ** -0.5
SCALE2_A, SCALE2_B = SCALE_A * LOG2E, SCALE_B * LOG2E
VMEM_LIMIT = 56 * 1024 * 1024

E_QA, E_KA, E_VA, E_GA, E_CQ, E_CKV, E_GB, E_KR = (
    (0, 512), (512, 640), (640, 768), (768, 1280), (1280, 1536), (1536, 1664), (1664, 2176), (2176, 2208))
O_Q, O_K, O_V, O_G = (0, 1024), (1024, 1280), (1280, 1536), (1536, 2560)


def _mm(a, b):
    return jnp.dot(a.astype(MXU), b.astype(MXU), preferred_element_type=F32)


def _mm_nt(a, b):
    return lax.dot_general(a.astype(MXU), b.astype(MXU), (((1,), (1,)), ((), ())), preferred_element_type=F32)


def _mm_tn(a, b):
    return lax.dot_general(a.astype(MXU), b.astype(MXU), (((0,), (0,)), ((), ())), preferred_element_type=F32)


def _group_sums_t(prod, group):
    tm, w = prod.shape
    sel = (lax.broadcasted_iota(jnp.int32, (w, 128), 0) // group
           == lax.broadcasted_iota(jnp.int32, (w, 128), 1)).astype(MXU)
    hi = prod.astype(MXU)
    lo = prod - hi.astype(F32)
    return (_mm(hi, sel) + _mm(lo, sel)).T


def _sigmoid(z):
    return 1.0 / (1.0 + jnp.exp(-z))


def _silu(z):
    return z * _sigmoid(z)


def _rms(x):
    return lax.rsqrt(jnp.mean(x * x, axis=-1, keepdims=True) + EPS)


def _swap_halves(y, group):
    n = y.shape[-1]
    half = group // 2
    fwd = pltpu.roll(y, half, 1)
    if n == group:
        return fwd
    back = pltpu.roll(y, n - half, 1)
    lane = lax.broadcasted_iota(jnp.int32, y.shape, 1)
    return jnp.where((lane % group) < half, back, fwd)


def _rope(y, cos, sin, group):
    return y * cos + _swap_halves(y, group) * sin


def _rope_t(d, cos, sin, group):
    return d * cos - _swap_halves(d, group) * sin


def _rms_bwd(dy, x, g):
    r = _rms(x)
    xhat = x * r
    dxhat = dy * g
    dx = r * (dxhat - xhat * jnp.mean(dxhat * xhat, axis=-1, keepdims=True))
    return dx, dy * xhat


def _group_mean(v, bd, group):
    hi = v.astype(MXU)
    lo = v - hi.astype(F32)
    return (_mm(hi, bd[...]) + _mm(lo, bd[...])) * (1.0 / group)


def _head_norm(x, g, bd, group):
    return x * lax.rsqrt(_group_mean(x * x, bd, group) + EPS) * g


def _head_norm_bwd(dy, x, g, bd, group):
    r = lax.rsqrt(_group_mean(x * x, bd, group) + EPS)
    xhat = x * r
    dxhat = dy * g
    dx = r * (dxhat - xhat * _group_mean(dxhat * xhat, bd, group))
    return dx, dy * xhat


def _params(sem, vmem=VMEM_LIMIT):
    return pltpu.CompilerParams(dimension_semantics=sem, vmem_limit_bytes=vmem)


def _row_spec(tm, w):
    return pl.BlockSpec((tm, w), lambda i: (i, 0))


def _full_spec(shape):
    nd = len(shape)
    return pl.BlockSpec(shape, lambda i: (0,) * nd)


def _head_spec(h, tm, w):
    return pl.BlockSpec((h, tm, w), lambda i: (0, i, 0))


def _headt_spec(h, w, tm):
    return pl.BlockSpec((h, w, tm), lambda i: (0, 0, i))


def _rows_spec(h, tm):
    return pl.BlockSpec((h, tm), lambda i: (0, i))


def _me():
    return lax.axis_index("x"), lax.axis_index("y"), lax.axis_index("c")


def _flat(p):
    return 4 * p[0] + 2 * p[1] + p[2]


def _peer(me, k):
    x, y, c = me
    return (1 - x if k & 4 else x, 1 - y if k & 2 else y, 1 - c if k & 1 else c)


MESH_ID = pl.DeviceIdType.MESH


class Gather:
    VMEM = pl.BlockSpec(memory_space=pltpu.VMEM)

    def __init__(self, shards):
        self.shards = list(shards)
        self.n = len(self.shards)
        self.out_shapes = tuple(jax.ShapeDtypeStruct((N_DEV,) + a.shape, a.dtype) for a in self.shards)
        self.in_specs = [Gather.VMEM] * self.n
        self.out_specs = (Gather.VMEM,) * self.n
        self.sems = [pltpu.SemaphoreType.DMA((7 * self.n,)), pltpu.SemaphoreType.DMA((7 * self.n,)),
                     pltpu.SemaphoreType.DMA((self.n,))]

    def _plan(self, x_refs, out_refs, sems):
        send_sems, recv_sems, local_sems = sems
        me = _me()
        x, y, c = me
        chips = [(1 - x, y), (x, 1 - y), (1 - x, 1 - y)]

        def copy(a, k, block, to, src=None):
            slot = out_refs[a].at[_flat(block)]
            return pltpu.make_async_remote_copy(
                src_ref=slot if src is None else src, dst_ref=slot, send_sem=send_sems.at[7 * a + k],
                recv_sem=recv_sems.at[7 * a + k], device_id=to, device_id_type=MESH_ID)

        mine = [pltpu.make_async_copy(x_refs[a], out_refs[a].at[_flat(me)], local_sems.at[a]) for a in range(self.n)]
        first = [copy(a, 0, me, (x, y, 1 - c), src=x_refs[a]) for a in range(self.n)]
        first += [copy(a, 1 + j, me, (*chip, c), src=x_refs[a]) for a in range(self.n) for j, chip in enumerate(chips)]
        return me, chips, copy, mine, first

    def start(self, x_refs, out_refs, sems):
        _, _, _, mine, first = self._plan(x_refs, out_refs, sems)
        for cp in mine + first:
            cp.start()

    def forward(self, x_refs, out_refs, sems):
        me, chips, copy, _, _ = self._plan(x_refs, out_refs, sems)
        x, y, c = me
        for a in range(self.n):
            for j, chip in enumerate(chips):
                copy(a, 1 + j, (*chip, c), me).wait_recv()
                copy(a, 4 + j, (*chip, c), (x, y, 1 - c)).start()

    def drain(self, x_refs, out_refs, sems):
        me, chips, copy, mine, first = self._plan(x_refs, out_refs, sems)
        x, y, c = me
        sibling = (x, y, 1 - c)
        for a in range(self.n):
            copy(a, 0, sibling, me).wait_recv()
            for j, chip in enumerate(chips):
                copy(a, 4 + j, (*chip, 1 - c), me).wait_recv()
        for cp in first + [copy(a, 4 + j, (*chip, c), sibling) for a in range(self.n) for j, chip in enumerate(chips)]:
            cp.wait_send()
        for cp in mine:
            cp.wait()

    def finish(self, x_refs, out_refs, sems):
        self.forward(x_refs, out_refs, sems)
        self.drain(x_refs, out_refs, sems)


def all_gather_slots(gather, name):
    def body(*refs):
        x_refs, out_refs, sems = refs[:gather.n], refs[gather.n:2 * gather.n], refs[2 * gather.n:]
        gather.start(x_refs, out_refs, sems)
        gather.finish(x_refs, out_refs, sems)

    return pl.pallas_call(
        body, name=name, out_shape=gather.out_shapes, in_specs=gather.in_specs, out_specs=gather.out_specs,
        scratch_shapes=list(gather.sems), compiler_params=pltpu.CompilerParams(vmem_limit_bytes=VMEM_LIMIT),
    )(*gather.shards)


class Exchange:
    HBM = pl.BlockSpec(memory_space=pl.ANY)

    def __init__(self, srcs, scatter):
        self.srcs = list(srcs)
        self.scatter = scatter
        self.n = len(self.srcs)
        self.land_shapes = tuple(jax.ShapeDtypeStruct((N_DEV,) + tuple(a.shape[-2:]), a.dtype) for a in self.srcs)
        self.in_specs = [Exchange.HBM] * self.n
        self.out_specs = (Exchange.HBM,) * self.n
        self.sems = [pltpu.SemaphoreType.DMA((N_DEV - 1,)), pltpu.SemaphoreType.DMA((N_DEV - 1,)),
                     pltpu.SemaphoreType.DMA] * self.n

    def _copies(self, src_refs, land_refs, sems):
        me = _me()
        mi = _flat(me)
        local, sends, recvs = [], [], []
        for a, (src_ref, land_ref) in enumerate(zip(src_refs, land_refs)):
            send_sems, recv_sems, local_sem = sems[3 * a:3 * a + 3]
            pick = (lambda p, r=src_ref: r.at[_flat(p)]) if self.scatter else (lambda p, r=src_ref: r)
            local.append(pltpu.make_async_copy(pick(me), land_ref.at[mi], local_sem))
            for k in range(1, N_DEV):
                peer = _peer(me, k)
                pair = dict(send_sem=send_sems.at[k - 1], recv_sem=recv_sems.at[k - 1], device_id=peer,
                            device_id_type=MESH_ID)
                sends.append(pltpu.make_async_remote_copy(src_ref=pick(peer), dst_ref=land_ref.at[mi], **pair))
                recvs.append(pltpu.make_async_remote_copy(src_ref=pick(peer), dst_ref=land_ref.at[_flat(peer)],
                                                          **pair))
        return local, sends, recvs

    def start(self, src_refs, land_refs, sems):
        local, sends, _ = self._copies(src_refs, land_refs, sems)
        for cp in local + sends:
            cp.start()

    def wait(self, src_refs, land_refs, sems):
        local, sends, recvs = self._copies(src_refs, land_refs, sems)
        for cp in recvs:
            cp.wait_recv()
        for cp in sends:
            cp.wait_send()
        for cp in local:
            cp.wait()


def ada_forward(c8, ada_w, bias_cols, gather):
    d = c8.shape[1]
    w = ada_w.shape[2]
    ng = gather.n

    def body(*refs):
        c_ref, w_ref, b_ref = refs[:3]
        gx_refs = refs[3:3 + ng]
        call_ref, modp_ref = refs[3 + ng:5 + ng]
        gout_refs = refs[5 + ng:5 + 2 * ng]
        part_ref, s1, r1, s2, r2 = refs[5 + 2 * ng:10 + 2 * ng]
        g_sems = refs[10 + 2 * ng:]
        me = _me()
        mi = _flat(me)
        call_ref[mi] = c_ref[...]
        rows_out = []
        for k in range(1, N_DEV):
            rows_out.append(pltpu.make_async_remote_copy(
                src_ref=c_ref, dst_ref=call_ref.at[mi], send_sem=s1.at[k - 1], recv_sem=r1.at[k - 1],
                device_id=_peer(me, k), device_id_type=MESH_ID))
        for cp in rows_out:
            cp.start()
        gather.start(gx_refs, gout_refs, g_sems)
        for k in range(1, N_DEV):
            pltpu.make_async_remote_copy(
                src_ref=c_ref, dst_ref=call_ref.at[_flat(_peer(me, k))], send_sem=s1.at[k - 1],
                recv_sem=r1.at[k - 1], device_id=_peer(me, k), device_id_type=MESH_ID).wait_recv()
        ca = _silu(call_ref[...].reshape(N_DEV * 8, d))
        for l in range(2):
            part = _mm(ca, w_ref[l]) + b_ref[l]
            for b in range(N_DEV):
                part_ref[b, l] = part[8 * b:8 * b + 8, :]
        modp_ref[mi] = part_ref[mi]
        spread = []
        for k in range(1, N_DEV):
            peer = _peer(me, k)
            spread.append(pltpu.make_async_remote_copy(
                src_ref=part_ref.at[_flat(peer)], dst_ref=modp_ref.at[mi], send_sem=s2.at[k - 1],
                recv_sem=r2.at[k - 1], device_id=peer, device_id_type=MESH_ID))
        for cp in spread:
            cp.start()
        gather.forward(gx_refs, gout_refs, g_sems)
        for k in range(1, N_DEV):
            pi = _flat(_peer(me, k))
            pltpu.make_async_remote_copy(
                src_ref=part_ref.at[pi], dst_ref=modp_ref.at[pi], send_sem=s2.at[k - 1],
                recv_sem=r2.at[k - 1], device_id=_peer(me, k), device_id_type=MESH_ID).wait_recv()
        for cp in rows_out + spread:
            cp.wait_send()
        gather.drain(gx_refs, gout_refs, g_sems)

    vm = pl.BlockSpec(memory_space=pltpu.VMEM)
    res = pl.pallas_call(
        body, name="ada_forward",
        out_shape=(jax.ShapeDtypeStruct((N_DEV, 8, d), F32), jax.ShapeDtypeStruct((N_DEV, 2, 8, w), F32))
        + gather.out_shapes,
        in_specs=[vm, vm, vm] + gather.in_specs, out_specs=(vm, vm) + gather.out_specs,
        scratch_shapes=[pltpu.VMEM((N_DEV, 2, 8, w), F32)] + [pltpu.SemaphoreType.DMA((7,))] * 4 + list(gather.sems),
        compiler_params=pltpu.CompilerParams(vmem_limit_bytes=VMEM_LIMIT),
    )(c8, ada_w, bias_cols, *gather.shards)
    return res[0], res[1], res[2:]


def _modulated(x, mod_ref, nw_ref):
    xn = x * _rms(x)
    g1 = nw_ref[...] * (1.0 + mod_ref[1:2, :])
    return xn, g1, xn * g1 + mod_ref[0:1, :]


def even_in_forward(x, mod, nw, w_in_t, gq, gk, qln, kvln, w_uq_t, uk_bd, bd, cos_a, sin_a, cos_t, sin_t):
    s, d = x.shape
    tm = min(ROW_TILE, s)
    n_nope = B_HEADS * B_NOPE

    def body(x_ref, mod_ref, nw_ref, w_ref, gq_ref, gk_ref, qln_ref, kvln_ref, uq_ref, ukbd_ref, bd_ref,
             ca_ref, sa_ref, ct_ref, st_ref,
             qa_o, ka_o, va_o, qb_o, kb_o, kat_o, vat_o, kbt_o, qa_raw_o, ka_raw_o, cq_raw_o, ckv_raw_o, ga_o, gb_o):
        _, _, h = _modulated(x_ref[...], mod_ref, nw_ref)
        h = h.astype(MXU)

        def proj(cols):
            return _mm_nt(h, w_ref[cols[0]:cols[1], :])

        ca, sa, ct, st = ca_ref[...], sa_ref[...], ct_ref[...], st_ref[...]
        wide = lambda t, n: jnp.concatenate([t] * n, axis=1)
        qa = proj(E_QA)
        qa_raw_o[...] = qa
        qr = _rope(_head_norm(qa, gq_ref[...], bd_ref, HD), wide(ca, 4), wide(sa, 4), 32) * SCALE2_A
        for hh in range(A_HEADS):
            qa_o[hh] = qr[:, HD * hh:HD * hh + HD].astype(MXU)
        ka = proj(E_KA)
        ka_raw_o[...] = ka
        kr = _rope(_head_norm(ka, gk_ref[...], bd_ref[0:128, 0:128], HD), ca, sa, 32)
        va = proj(E_VA)
        krt, vat = kr.T, va.T
        for g in range(A_KV):
            ka_o[g] = kr[:, HD * g:HD * g + HD].astype(MXU)
            va_o[g] = va[:, HD * g:HD * g + HD].astype(MXU)
            kat_o[g] = krt[HD * g:HD * g + HD, :].astype(MXU)
            vat_o[g] = vat[HD * g:HD * g + HD, :].astype(MXU)
        ga_o[...] = proj(E_GA).astype(MXU)
        gb_o[...] = proj(E_GB).astype(MXU)
        cq = proj(E_CQ)
        cq_raw_o[...] = cq
        qb = _mm_nt(cq * _rms(cq) * qln_ref[...], uq_ref[...])
        q_lat = _mm(qb[:, 0:n_nope], ukbd_ref[...]) * SCALE2_B
        q_rope = _rope(qb[:, n_nope:], wide(ct, 2), wide(st, 2), 32) * SCALE2_B
        for hh in range(B_HEADS):
            qb_o[hh, :, 0:B_KV_LORA] = q_lat[:, B_KV_LORA * hh:B_KV_LORA * (hh + 1)].astype(MXU)
            qb_o[hh, :, B_KV_LORA:B_QK] = q_rope[:, B_ROPE * hh:B_ROPE * (hh + 1)].astype(MXU)
        ckv = proj(E_CKV)
        ckv_raw_o[...] = ckv
        ckv_n = ckv * _rms(ckv) * kvln_ref[...]
        k_rope = _rope(proj(E_KR), ct[:, 0:B_ROPE], st[:, 0:B_ROPE], 32)
        kb_o[0, :, 0:B_KV_LORA] = ckv_n.astype(MXU)
        kb_o[0, :, B_KV_LORA:B_QK] = k_rope.astype(MXU)
        kbt_o[0, 0:B_KV_LORA, :] = ckv_n.T.astype(MXU)
        kbt_o[0, B_KV_LORA:B_QK, :] = k_rope.T.astype(MXU)

    sd = jax.ShapeDtypeStruct
    outs = (sd((A_HEADS, s, HD), MXU), sd((A_KV, s, HD), MXU), sd((A_KV, s, HD), MXU),
            sd((B_HEADS, s, B_QK), MXU), sd((1, s, B_QK), MXU),
            sd((A_KV, HD, s), MXU), sd((A_KV, HD, s), MXU), sd((1, B_QK, s), MXU),
            sd((s, 512), F32), sd((s, 128), F32), sd((s, B_Q_LORA), F32), sd((s, B_KV_LORA), F32),
            sd((s, 512), MXU), sd((s, 512), MXU))
    out_specs = (_head_spec(A_HEADS, tm, HD), _head_spec(A_KV, tm, HD), _head_spec(A_KV, tm, HD),
                 _head_spec(B_HEADS, tm, B_QK), _head_spec(1, tm, B_QK),
                 _headt_spec(A_KV, HD, tm), _headt_spec(A_KV, HD, tm), _headt_spec(1, B_QK, tm),
                 _row_spec(tm, 512), _row_spec(tm, 128), _row_spec(tm, B_Q_LORA), _row_spec(tm, B_KV_LORA),
                 _row_spec(tm, 512), _row_spec(tm, 512))
    consts = [mod, nw, w_in_t, gq, gk, qln, kvln, w_uq_t, uk_bd, bd]
    return pl.pallas_call(
        body, name="even_in_forward", grid=(s // tm,), out_shape=outs,
        in_specs=[_row_spec(tm, d)] + [_full_spec(a.shape) for a in consts] + [_row_spec(tm, 128)] * 4,
        out_specs=out_specs, compiler_params=_params(("parallel",)),
    )(x, *consts, cos_a, sin_a, cos_t, sin_t)


def odd_in_forward(x, mod, nw, w_in):
    s, d = x.shape
    tm = min(ROW_TILE, s)

    def body(x_ref, mod_ref, nw_ref, w_ref, q_o, k_o, v_o, kt_o, vt_o, g_o):
        _, _, h = _modulated(x_ref[...], mod_ref, nw_ref)
        h = h.astype(MXU)

        def proj(cols):
            return _mm_nt(h, w_ref[cols[0]:cols[1], :])

        q = proj(O_Q) * SCALE2_A
        for hh in range(C_HEADS):
            q_o[hh] = q[:, HD * hh:HD * hh + HD].astype(MXU)
        k = proj(O_K)
        v = proj(O_V)
        for g in range(C_KV):
            kh = k[:, HD * g:HD * g + HD]
            vh = v[:, HD * g:HD * g + HD]
            k_o[g] = kh.astype(MXU)
            v_o[g] = vh.astype(MXU)
            kt_o[g] = kh.T.astype(MXU)
            vt_o[g] = vh.T.astype(MXU)
        g_o[...] = proj(O_G).astype(MXU)

    sd = jax.ShapeDtypeStruct
    return pl.pallas_call(
        body, name="odd_in_forward", grid=(s // tm,),
        out_shape=(sd((C_HEADS, s, HD), MXU), sd((C_KV, s, HD), MXU), sd((C_KV, s, HD), MXU),
                   sd((C_KV, HD, s), MXU), sd((C_KV, HD, s), MXU), sd((s, 1024), MXU)),
        in_specs=[_row_spec(tm, d), _full_spec(mod.shape), _full_spec(nw.shape), _full_spec(w_in.shape)],
        out_specs=(_head_spec(C_HEADS, tm, HD), _head_spec(C_KV, tm, HD), _head_spec(C_KV, tm, HD),
                   _headt_spec(C_KV, HD, tm), _headt_spec(C_KV, HD, tm), _row_spec(tm, 1024)),
        compiler_params=_params(("parallel",)),
    )(x, mod, nw, w_in)


def latent_out_forward(o_lat, w_uv):
    s = o_lat.shape[0]
    tm = min(ROW_TILE, s)

    def body(o_ref, uv_ref, out_ref):
        for hh in range(B_HEADS):
            out_ref[:, HD * hh:HD * hh + HD] = _mm(o_ref[:, B_KV_LORA * hh:B_KV_LORA * (hh + 1)],
                                                   uv_ref[hh]).astype(MXU)

    return pl.pallas_call(
        body, name="latent_out_forward", grid=(s // tm,),
        out_shape=jax.ShapeDtypeStruct((s, B_HEADS * HD), MXU),
        in_specs=[_row_spec(tm, o_lat.shape[1]), _full_spec(w_uv.shape)],
        out_specs=_row_spec(tm, B_HEADS * HD),
        compiler_params=_params(("parallel",)),
    )(o_lat, w_uv)


def mixer_out_forward(x, mod, pairs, w_out, name, loss=None):
    s, d = x.shape
    tm = min(ROW_TILE, s)
    n = len(pairs)
    widths = [o.shape[1] for o, _ in pairs]
    head = loss is not None

    def body(*refs):
        x_ref, mod_ref, w_ref = refs[:3]
        pr = refs[3:3 + 2 * n]
        rest = refs[3 + 2 * n:]
        y = jnp.zeros((tm, d), F32)
        r0 = 0
        for i in range(n):
            mix = pr[2 * i][...].astype(F32) * _silu(pr[2 * i + 1][...].astype(F32))
            y = y + _mm(mix, w_ref[r0:r0 + widths[i], :])
            r0 += widths[i]
        x_out = x_ref[...] + mod_ref[2:3, :] * y
        if not head:
            xo_ref, y_ref = rest
            xo_ref[...] = x_out
        else:
            t_ref, fn_ref, dx_ref, y_ref, lp_ref, dw_ref = rest

            @pl.when(pl.program_id(0) == 0)
            def _():
                lp_ref[...] = jnp.zeros(lp_ref.shape, F32)
                dw_ref[...] = jnp.zeros(dw_ref.shape, F32)

            g = fn_ref[...]
            err = x_out * _rms(x_out) * g - t_ref[...]
            lp_ref[...] += jnp.sum(err * err, axis=0, keepdims=True)
            dx, dg = _rms_bwd(err * (1.0 / d), x_out, g)
            dx_ref[...] = dx
            dw_ref[...] += jnp.sum(dg, axis=0, keepdims=True)
        y_ref[...] = y.astype(y_ref.dtype)

    flat = [a for p in pairs for a in p]
    sd = jax.ShapeDtypeStruct
    in_specs = [_row_spec(tm, d), _full_spec(mod.shape), _full_spec(w_out.shape)]
    in_specs += [_row_spec(tm, a.shape[1]) for a in flat]
    out_shape = (sd((s, d), F32), sd((s, d), MXU))
    out_specs = (_row_spec(tm, d), _row_spec(tm, d))
    if head:
        in_specs += [_row_spec(tm, d), _full_spec(loss[1].shape)]
        out_shape += (sd((1, d), F32), sd((1, d), F32))
        out_specs += (_full_spec((1, d)), _full_spec((1, d)))
    return pl.pallas_call(
        body, name=name, grid=(s // tm,), out_shape=out_shape, in_specs=in_specs, out_specs=out_specs,
        compiler_params=_params(("arbitrary",) if head else ("parallel",)),
    )(x, mod, w_out, *flat, *(loss if head else ()))


ONES_ROWS = 16
AHEAD = 2


def _col_max8(s3):
    m8 = jnp.max(s3, axis=0)
    return jnp.broadcast_to(jnp.max(m8, axis=0, keepdims=True), m8.shape)


def _with_ones(vt, n):
    return jnp.concatenate([vt, jnp.ones((ONES_ROWS, n), vt.dtype)], axis=0)


def _grid_edges(grid):
    ids = [pl.program_id(a) for a in range(len(grid))]
    first = functools.reduce(jnp.logical_and, [i == 0 for i in ids])
    last = functools.reduce(jnp.logical_and, [i == n - 1 for i, n in zip(ids, grid)])
    return first, last


def flash_forward(q, k, vt, *, dv, tq, tk, nsub, name, exchange=None):
    hq, s, dq = q.shape
    g_kv = k.shape[0]
    hpg = hq // g_kv
    nq = s // tq
    tkk = tk * nsub
    nk = s // tkk
    grid = (g_kv, nq, nk)
    hosted = exchange is not None
    m_cols = hpg * tq
    dvp = dv + ONES_ROWS

    def body(*refs):
        nx = exchange.n if hosted else 0
        q_ref, k_ref, vt_ref = refs[:3]
        xs_refs = refs[3:3 + nx]
        o_ref, lse_ref = refs[3 + nx:5 + nx]
        land_refs = refs[5 + nx:5 + 2 * nx]
        m_s, acc_s = refs[5 + 2 * nx:7 + 2 * nx]
        sems = refs[7 + 2 * nx:]
        if hosted:
            first, last = _grid_edges(grid)
            pl.when(first)(lambda: exchange.start(xs_refs, land_refs, sems))
        j = pl.program_id(2)

        @pl.when(j == 0)
        def _():
            m_s[...] = jnp.full((8, m_cols), -jnp.inf, F32)
            acc_s[...] = jnp.zeros((dvp, m_cols), F32)

        qq = q_ref[...].reshape(m_cols, dq)
        score = lambda u: _mm_nt(k_ref[0, tk * u:tk * (u + 1), :], qq).reshape(tk // 8, 8, m_cols)
        sts = {u: score(u) for u in range(min(AHEAD, nsub))}
        m_run = m_s[...]
        acc = acc_s[...]
        for u in range(nsub):
            if u + AHEAD < nsub:
                sts[u + AHEAD] = score(u + AHEAD)
            st = sts.pop(u)
            m_new = jnp.maximum(m_run, _col_max8(st))
            p = jnp.exp2(st - m_new[None])
            alpha = jnp.exp2(m_run - m_new)
            pv = _mm(_with_ones(vt_ref[0, 0:dv, tk * u:tk * (u + 1)], tk), p.reshape(tk, m_cols))
            acc = (acc.reshape(dvp // 8, 8, m_cols) * alpha[None]).reshape(dvp, m_cols) + pv
            m_run = m_new
        acc_s[...] = acc
        m_s[...] = m_run

        @pl.when(j == nk - 1)
        def _():
            l = acc_s[dv:dv + 1, :]
            ot = acc_s[0:dv, :] / l
            lse = m_s[0:1, :] + jnp.log2(l)
            for hh in range(hpg):
                o_ref[:, dv * hh:dv * hh + dv] = ot[:, tq * hh:tq * hh + tq].T.astype(MXU)
                lse_ref[hh] = lse[:, tq * hh:tq * hh + tq]

        if hosted:
            pl.when(last)(lambda: exchange.wait(xs_refs, land_refs, sems))

    sd = jax.ShapeDtypeStruct
    return pl.pallas_call(
        body, name=name, grid=grid,
        out_shape=(sd((s, hq * dv), MXU), sd((hq, 1, s), F32)) + (exchange.land_shapes if hosted else ()),
        in_specs=[pl.BlockSpec((hpg, tq, dq), lambda g, i, j: (g, i, 0)),
                  pl.BlockSpec((1, tkk, k.shape[2]), lambda g, i, j: (g, j, 0)),
                  pl.BlockSpec((1, dv, tkk), lambda g, i, j: (g, 0, j))] + (exchange.in_specs if hosted else []),
        out_specs=(pl.BlockSpec((tq, hpg * dv), lambda g, i, j: (i, g)),
                   pl.BlockSpec((hpg, 1, tq), lambda g, i, j: (g, 0, i))) + (exchange.out_specs if hosted else ()),
        scratch_shapes=[pltpu.VMEM((8, m_cols), F32), pltpu.VMEM((dvp, m_cols), F32)]
        + (list(exchange.sems) if hosted else []),
        compiler_params=_params(("arbitrary",) * 3 if hosted else ("parallel", "parallel", "arbitrary")),
    )(q, k, vt, *(exchange.srcs if hosted else []))


def _window_bias_t(hpg, slope_ref):
    t = WINDOW
    r = lax.broadcasted_iota(jnp.int32, (3 * t, t), 0)
    cq = lax.broadcasted_iota(jnp.int32, (3 * t, t), 1)
    arel = jnp.abs(r - t - cq)
    base = jnp.where(arel <= WINDOW, arel.astype(F32) * (-LOG2E), -jnp.inf)
    return jnp.concatenate([base * slope_ref[hh] for hh in range(hpg)], axis=1)


def _window_edges_t(bias, no_before, no_after):
    t = WINDOW
    r = lax.broadcasted_iota(jnp.int32, bias.shape, 0)
    out = ((r < t) & no_before) | ((r >= 2 * t) & no_after)
    return jnp.where(out, -jnp.inf, bias)


def _window_specs(kind, nb, nblk, d):
    t = WINDOW
    before = lambda i: jnp.clip(i * nb - 1, 0, nblk - 1)
    after = lambda i: jnp.clip((i + 1) * nb, 0, nblk - 1)
    if kind == "rows":
        return [pl.BlockSpec((1, t, d), lambda g, i: (g, before(i), 0)),
                pl.BlockSpec((1, nb * t, d), lambda g, i: (g, i, 0)),
                pl.BlockSpec((1, t, d), lambda g, i: (g, after(i), 0))]
    return [pl.BlockSpec((1, d, t), lambda g, i: (g, 0, before(i))),
            pl.BlockSpec((1, d, nb * t), lambda g, i: (g, 0, i)),
            pl.BlockSpec((1, d, t), lambda g, i: (g, 0, after(i)))]


def window_forward(q, k, vt, sink2, slopes, nb, name):
    hq, s, d = q.shape
    g_kv = k.shape[0]
    hpg = hq // g_kv
    t = WINDOW
    nblk = s // t
    steps = nblk // nb
    m_cols = hpg * t

    def body(q_ref, kp, ko, kn, vp, vo, vn, sink_ref, slope_ref, o_ref, lse_ref):
        i = pl.program_id(1)
        kk_all = jnp.concatenate([kp[0], ko[0], kn[0]], axis=0)
        vt_all = jnp.concatenate([vp[0], vo[0], vn[0]], axis=1)
        bias = _window_bias_t(hpg, slope_ref)
        sink_row = jnp.concatenate([jnp.broadcast_to(sink_ref[hh], (8, t)) for hh in range(hpg)], axis=1)
        sts = {}

        def score(u):
            qq = q_ref[:, t * u:t * (u + 1), :].reshape(m_cols, d)
            b_u = bias
            if u == 0 or u == nb - 1:
                b_u = _window_edges_t(bias, (i == 0) if u == 0 else False,
                                      (i == steps - 1) if u == nb - 1 else False)
            sts[u] = _mm_nt(kk_all[t * u:t * (u + 3), :], qq) + b_u

        for u in range(min(AHEAD, nb)):
            score(u)
        for u in range(nb):
            if u + AHEAD < nb:
                score(u + AHEAD)
            s3 = sts.pop(u).reshape(3 * t // 8, 8, m_cols)
            m8 = jnp.maximum(_col_max8(s3), sink_row)
            p = jnp.exp2(s3 - m8[None]).reshape(3 * t, m_cols)
            acc = _mm(_with_ones(vt_all[:, t * u:t * (u + 3)], 3 * t), p)
            l = acc[d:d + 1, :] + jnp.exp2(sink_row[0:1, :] - m8[0:1, :])
            ot = acc[0:d, :] / l
            lse = m8[0:1, :] + jnp.log2(l)
            for hh in range(hpg):
                o_ref[t * u:t * (u + 1), d * hh:d * hh + d] = ot[:, t * hh:t * hh + t].T.astype(MXU)
                lse_ref[hh, :, t * u:t * (u + 1)] = lse[:, t * hh:t * hh + t]

    sd = jax.ShapeDtypeStruct
    return pl.pallas_call(
        body, name=name, grid=(g_kv, steps),
        out_shape=(sd((s, hq * d), MXU), sd((hq, 1, s), F32)),
        in_specs=[pl.BlockSpec((hpg, nb * t, d), lambda g, i: (g, i, 0))]
        + _window_specs("rows", nb, nblk, d) + _window_specs("cols", nb, nblk, d)
        + [pl.BlockSpec((hpg, 1, 1), lambda g, i: (g, 0, 0))] * 2,
        out_specs=(pl.BlockSpec((nb * t, hpg * d), lambda g, i: (i, g)),
                   pl.BlockSpec((hpg, 1, nb * t), lambda g, i: (g, 0, i))),
        compiler_params=_params(("parallel", "parallel")),
    )(q, k, k, k, vt, vt, vt, sink2, slopes)


def window_backward(q, k, kt, v, do, lse, delta, slopes, nb, name):
    hq, s, d = q.shape
    g_kv = k.shape[0]
    hpg = hq // g_kv
    t = WINDOW
    nblk = s // t
    steps = nblk // nb
    m_cols = hpg * t

    def body(q_ref, kp, ko, kn, ktp, kto, ktn, vp, vo, vn, do_ref, lse_ref, dl_ref, slope_ref,
             dq_ref, dk_ref, dv_ref, dk_s, dv_s):
        i = pl.program_id(1)

        @pl.when(i == 0)
        def _():
            dk_ref[...] = jnp.zeros(dk_ref.shape, F32)
            dv_ref[...] = jnp.zeros(dv_ref.shape, F32)

        dk_s[...] = jnp.zeros(dk_s.shape, F32)
        dv_s[...] = jnp.zeros(dv_s.shape, F32)
        kk_all = jnp.concatenate([kp[0], ko[0], kn[0]], axis=0)
        vv_all = jnp.concatenate([vp[0], vo[0], vn[0]], axis=0)
        kkt_all = jnp.concatenate([ktp[0], kto[0], ktn[0]], axis=1)
        bias = _window_bias_t(hpg, slope_ref)
        qqs, dds, sts, dps = {}, {}, {}, {}

        def issue(u):
            rows = slice(t * u, t * (u + 1))
            keys = slice(t * u, t * (u + 3))
            qqs[u] = q_ref[:, rows, :].reshape(m_cols, d)
            dds[u] = jnp.concatenate([do_ref[rows, d * hh:d * hh + d] for hh in range(hpg)], axis=0)
            b_u = bias
            if u == 0 or u == nb - 1:
                b_u = _window_edges_t(bias, (i == 0) if u == 0 else False,
                                      (i == steps - 1) if u == nb - 1 else False)
            sts[u] = _mm_nt(kk_all[keys, :], qqs[u]) + b_u
            dps[u] = _mm_nt(vv_all[keys, :], dds[u])

        for u in range(min(AHEAD, nb)):
            issue(u)
        for u in range(nb):
            if u + AHEAD < nb:
                issue(u + AHEAD)
            rows = slice(t * u, t * (u + 1))
            keys = slice(t * u, t * (u + 3))
            lse_row = jnp.concatenate([lse_ref[hh, :, rows] for hh in range(hpg)], axis=1)
            dl_row = jnp.concatenate([dl_ref[hh, :, rows] for hh in range(hpg)], axis=1)
            p = jnp.exp2(sts[u] - lse_row)
            ds = p * (dps[u] - dl_row) * SCALE_A
            dv_s[keys, :] += _mm(p, dds[u])
            dk_s[keys, :] += _mm(ds, qqs[u])
            dqt = _mm(kkt_all[:, keys], ds)
            for hh in range(hpg):
                dq_ref[rows, d * hh:d * hh + d] = dqt[:, t * hh:t * hh + t].T.astype(dq_ref.dtype)
        tq = nb * t
        for src, r0, n in ((0, jnp.clip(i * nb - 1, 0, nblk - 1) * t, t), (t, i * tq, tq),
                           (t + tq, jnp.clip((i + 1) * nb, 0, nblk - 1) * t, t)):
            dst = pl.ds(pl.multiple_of(r0, t), n)
            dk_ref[0, dst, :] += dk_s[src:src + n, :] * (1.0 / SCALE2_A)
            dv_ref[0, dst, :] += dv_s[src:src + n, :]

    row_map = lambda g, i: (g, 0, i)
    sd = jax.ShapeDtypeStruct
    return pl.pallas_call(
        body, name=name, grid=(g_kv, steps),
        out_shape=(sd((s, hq * d), MXU), sd((g_kv, s, d), F32), sd((g_kv, s, d), F32)),
        in_specs=[pl.BlockSpec((hpg, nb * t, d), lambda g, i: (g, i, 0))]
        + _window_specs("rows", nb, nblk, d) + _window_specs("cols", nb, nblk, d) + _window_specs("rows", nb, nblk, d)
        + [pl.BlockSpec((nb * t, hpg * d), lambda g, i: (i, g)), pl.BlockSpec((hpg, 1, nb * t), row_map),
           pl.BlockSpec((hpg, 1, nb * t), row_map), pl.BlockSpec((hpg, 1, 1), lambda g, i: (g, 0, 0))],
        out_specs=(pl.BlockSpec((nb * t, hpg * d), lambda g, i: (i, g)),
                   pl.BlockSpec((1, s, d), lambda g, i: (g, 0, 0)),
                   pl.BlockSpec((1, s, d), lambda g, i: (g, 0, 0))),
        scratch_shapes=[pltpu.VMEM(((nb + 2) * t, d), F32), pltpu.VMEM(((nb + 2) * t, d), F32)],
        compiler_params=_params(("parallel", "arbitrary")),
    )(q, k, k, k, kt, kt, kt, v, v, v, do, lse, delta, slopes)


def flash_backward(q, k, kt, v, do, lse, delta, *, scale, dv, tq, tk, nsub, gq, name, split=None, exchange=None):
    hq, s, dq = q.shape
    g_kv = k.shape[0]
    hpg = hq // gq
    nq = s // tq
    tqq = tq * nsub
    nqs = s // tqq
    nkb = s // tk
    grid = (gq, nkb, nqs)
    hosted = exchange is not None
    m_cols = hpg * tq
    c = scale * LOG2E
    has_v = v is not None

    def body(*refs):
        it = iter(refs)
        q_ref, k_ref, kt_ref = next(it), next(it), next(it)
        v_ref = next(it) if has_v else None
        do_ref, lse_ref, dl_ref = next(it), next(it), next(it)
        nx = exchange.n if hosted else 0
        xs_refs = [next(it) for _ in range(nx)]
        dq_ref, dk_ref, dv_ref = next(it), next(it), next(it)
        land_refs = [next(it) for _ in range(nx)]
        dqt_s = next(it)
        sems = list(it)
        kj = pl.program_id(1)
        qi = pl.program_id(2)
        if hosted:
            first, last = _grid_edges(grid)
            pl.when(first)(lambda: exchange.start(xs_refs, land_refs, sems))

        @pl.when((kj == 0) & (qi == 0))
        def _():
            dqt_s[...] = jnp.zeros(dqt_s.shape, F32)

        @pl.when(qi == 0)
        def _():
            dk_ref[...] = jnp.zeros(dk_ref.shape, F32)
            dv_ref[...] = jnp.zeros(dv_ref.shape, F32)

        kk = k_ref[0]
        vv = v_ref[0] if has_v else kk[:, :dv]
        qqs, dds, sts, dps = {}, {}, {}, {}

        def issue(u):
            rows = slice(tq * u, tq * (u + 1))
            qqs[u] = q_ref[:, rows, :].reshape(m_cols, dq)
            dds[u] = jnp.concatenate([do_ref[rows, dv * hh:dv * hh + dv] for hh in range(hpg)], axis=0)
            sts[u] = _mm_nt(kk, qqs[u])
            dps[u] = _mm_nt(vv, dds[u])

        for u in range(min(AHEAD, nsub)):
            issue(u)
        dv_acc = dv_ref[0]
        dk_acc = dk_ref[0]
        for u in range(nsub):
            if u + AHEAD < nsub:
                issue(u + AHEAD)
            rows = slice(tq * u, tq * (u + 1))
            lse_row = jnp.concatenate([lse_ref[hh, :, rows] for hh in range(hpg)], axis=1)
            dl_row = jnp.concatenate([dl_ref[hh, :, rows] for hh in range(hpg)], axis=1)
            p = jnp.exp2(sts[u] - lse_row)
            ds = p * (dps[u] - dl_row) * scale
            dv_acc = dv_acc + _mm(p, dds[u])
            dk_acc = dk_acc + _mm(ds, qqs[u])
            dqt = _mm(kt_ref[0], ds)
            for hh in range(hpg):
                dqt_s[qi * nsub + u, dq * hh:dq * hh + dq, :] += dqt[:, tq * hh:tq * hh + tq]
        dv_ref[0] = dv_acc
        dk_ref[0] = jnp.where(qi == nqs - 1, dk_acc * (1.0 / c), dk_acc)

        @pl.when((kj == nkb - 1) & (qi == nqs - 1))
        def _():
            def emit(t, carry):
                r0 = pl.multiple_of(t * tq, tq)
                for hh in range(hpg):
                    blk = dqt_s[t, dq * hh:dq * hh + dq, :].T
                    if split is None:
                        dq_ref[pl.ds(r0, tq), dq * hh:dq * hh + dq] = blk
                    else:
                        rest = dq - split
                        dq_ref[pl.ds(r0, tq), split * hh:split * (hh + 1)] = blk[:, 0:split]
                        dq_ref[pl.ds(r0, tq), hpg * split + rest * hh:hpg * split + rest * (hh + 1)] = blk[:, split:]
                return carry

            lax.fori_loop(0, nq, emit, 0)

        if hosted:
            pl.when(last)(lambda: exchange.wait(xs_refs, land_refs, sems))

    kv_of = lambda g: g * g_kv // gq
    in_specs = [pl.BlockSpec((hpg, tqq, dq), lambda g, kj, qi: (g, qi, 0)),
                pl.BlockSpec((1, tk, dq), lambda g, kj, qi: (kv_of(g), kj, 0)),
                pl.BlockSpec((1, dq, tk), lambda g, kj, qi: (kv_of(g), 0, kj))]
    args = [q, k, kt]
    if has_v:
        in_specs.append(pl.BlockSpec((1, tk, dv), lambda g, kj, qi: (kv_of(g), kj, 0)))
        args.append(v)
    row_map = lambda g, kj, qi: (g, 0, qi)
    in_specs += [pl.BlockSpec((tqq, hpg * dv), lambda g, kj, qi: (qi, g)),
                 pl.BlockSpec((hpg, 1, tqq), row_map), pl.BlockSpec((hpg, 1, tqq), row_map)]
    args += [do, lse, delta]
    if hosted:
        in_specs += exchange.in_specs
        args += exchange.srcs
    sd = jax.ShapeDtypeStruct
    return pl.pallas_call(
        body, name=name, grid=grid,
        out_shape=(sd((s, hq * dq), F32), sd((gq, s, dq), F32), sd((gq, s, dv), F32))
        + (exchange.land_shapes if hosted else ()),
        in_specs=in_specs,
        out_specs=(pl.BlockSpec((s, hpg * dq), lambda g, kj, qi: (0, g)),
                   pl.BlockSpec((1, tk, dq), lambda g, kj, qi: (g, kj, 0)),
                   pl.BlockSpec((1, tk, dv), lambda g, kj, qi: (g, kj, 0))) + (exchange.out_specs if hosted else ()),
        scratch_shapes=[pltpu.VMEM((nq, hpg * dq, tq), F32)] + (list(exchange.sems) if hosted else []),
        compiler_params=_params(("arbitrary",) * 3 if hosted else ("parallel", "arbitrary", "arbitrary")),
    )(*args)


def mixer_out_backward(dx, y, mod, pairs, w_out, delta_heads, name, lse=None, sink=None):
    s, d = dx.shape
    tm = min(ROW_TILE, s)
    n = len(pairs)
    widths = [o.shape[1] for o, _ in pairs]
    n_delta = sum(1 for h in delta_heads if h)
    with_sink = lse is not None

    def body(*refs):
        it = iter(refs)
        dx_ref, y_ref, mod_ref, wt_ref = next(it), next(it), next(it), next(it)
        pr = [next(it) for _ in range(2 * n)]
        lse_ref = next(it) if with_sink else None
        sink_ref = next(it) if with_sink else None
        outs = [next(it) for _ in range(2 * n)]
        dl_refs = [next(it) for _ in range(n_delta)]
        dgate_ref, dw_ref = next(it), next(it)
        dsink_ref = next(it) if with_sink else None
        dw_acc = next(it)

        @pl.when(pl.program_id(0) == 0)
        def _():
            dgate_ref[...] = jnp.zeros(dgate_ref.shape, F32)
            dw_acc[...] = jnp.zeros(dw_acc.shape, F32)
            if with_sink:
                dsink_ref[...] = jnp.zeros(dsink_ref.shape, F32)

        dxo = dx_ref[...]
        dgate_ref[...] += jnp.sum(dxo * y_ref[...].astype(F32), axis=0, keepdims=True)
        dy = (dxo * mod_ref[2:3, :]).astype(MXU)
        dmix = _mm_nt(dy, wt_ref[...])
        r0 = 0
        di = 0
        for i in range(n):
            o = pr[2 * i][...].astype(F32)
            g = pr[2 * i + 1][...].astype(F32)
            dm = dmix[:, r0:r0 + widths[i]]
            sg = _sigmoid(g)
            act = g * sg
            do = dm * act
            outs[2 * i][...] = do.astype(MXU)
            outs[2 * i + 1][...] = (dm * o * (sg * (1.0 + g * (1.0 - sg)))).astype(MXU)
            dw_acc[r0:r0 + widths[i], :] += _mm_tn(o * act, dy)
            if delta_heads[i]:
                dlt = _group_sums_t(do * o, HD)[0:delta_heads[i], :]
                dl_refs[di][...] = dlt
                if with_sink:
                    ps = jnp.exp2(sink_ref[...] - lse_ref[...])
                    dsink_ref[...] += -jnp.sum(ps * dlt, axis=1, keepdims=True)
                di += 1
            r0 += widths[i]

        @pl.when(pl.program_id(0) == pl.num_programs(0) - 1)
        def _():
            for j in range(N_DEV):
                dw_ref[j] = dw_acc[j * dw_block:(j + 1) * dw_block, :].astype(MXU)

    dw_block = sum(widths) // N_DEV
    flat = [a for p in pairs for a in p]
    sd = jax.ShapeDtypeStruct
    in_specs = [_row_spec(tm, d), _row_spec(tm, d), _full_spec(mod.shape), _full_spec(w_out.shape)]
    in_specs += [_row_spec(tm, a.shape[1]) for a in flat]
    args = [dx, y, mod, w_out] + flat
    if with_sink:
        nh = lse.shape[0]
        in_specs += [_rows_spec(nh, tm), _full_spec(sink.shape)]
        args += [lse, sink]
    out_shape = [sd((s, a.shape[1]), MXU) for a in flat]
    out_specs = [_row_spec(tm, a.shape[1]) for a in flat]
    for h in delta_heads:
        if h:
            out_shape.append(sd((h, s), F32))
            out_specs.append(_rows_spec(h, tm))
    out_shape += [sd((1, d), F32), sd((N_DEV, dw_block, d), MXU)]
    out_specs += [_full_spec((1, d)), _full_spec((N_DEV, dw_block, d))]
    if with_sink:
        out_shape.append(sd((lse.shape[0], 1), F32))
        out_specs.append(_full_spec((lse.shape[0], 1)))
    return pl.pallas_call(
        body, name=name, grid=(s // tm,), out_shape=tuple(out_shape), in_specs=in_specs, out_specs=tuple(out_specs),
        scratch_shapes=[pltpu.VMEM((sum(widths), d), F32)], compiler_params=_params(("arbitrary",)),
    )(*args)


def latent_out_backward(d_ob, o_lat, w_uv):
    s = o_lat.shape[0]
    tm = min(ROW_TILE, s)

    def body(d_ref, o_ref, uv_ref, dol_ref, dl_ref, duv_ref, prod_s):
        @pl.when(pl.program_id(0) == 0)
        def _():
            duv_ref[...] = jnp.zeros(duv_ref.shape, F32)

        for hh in range(B_HEADS):
            dh = d_ref[:, HD * hh:HD * hh + HD]
            ol = o_ref[:, B_KV_LORA * hh:B_KV_LORA * (hh + 1)].astype(F32)
            dol = _mm_nt(dh, uv_ref[hh])
            dol_ref[:, B_KV_LORA * hh:B_KV_LORA * (hh + 1)] = dol.astype(MXU)
            prod_s[:, B_KV_LORA * hh:B_KV_LORA * (hh + 1)] = dol * ol
            duv_ref[:, HD * hh:HD * hh + HD] += _mm_tn(ol, dh)
        dl_ref[...] = _group_sums_t(prod_s[...], B_KV_LORA)[0:B_HEADS, :]

    sd = jax.ShapeDtypeStruct
    duv_shape = (B_KV_LORA, B_HEADS * HD)
    return pl.pallas_call(
        body, name="latent_out_backward", grid=(s // tm,),
        out_shape=(sd(o_lat.shape, MXU), sd((B_HEADS, s), F32), sd(duv_shape, F32)),
        in_specs=[_row_spec(tm, d_ob.shape[1]), _row_spec(tm, o_lat.shape[1]), _full_spec(w_uv.shape)],
        out_specs=(_row_spec(tm, o_lat.shape[1]), _rows_spec(B_HEADS, tm), _full_spec(duv_shape)),
        scratch_shapes=[pltpu.VMEM((tm, o_lat.shape[1]), F32)],
        compiler_params=_params(("arbitrary",)),
    )(d_ob, o_lat, w_uv)


def even_prep_backward(dqa, dka, dva, dqb, dkb, dvb, qa_raw, ka_raw, cq_raw, ckv_raw,
                       gq, gk, qln, kvln, w_uq_t, uk_bd, bd, cos_a, sin_a, cos_t, sin_t):
    s = qa_raw.shape[0]
    tm = min(ROW_TILE, s)
    half_lat = B_KV_LORA * B_HEADS // 2
    half_w = dqb.shape[1] // 2

    def body(dqa_ref, dka_ref, dva_ref, dqb_ref, dkb_ref, dvb_ref, qa_ref, ka_ref, cq_ref, ckv_ref,
             gq_ref, gk_ref, qln_ref, kvln_ref, uqt_ref, ukbd_ref, bd_ref, ca_ref, sa_ref, ct_ref, st_ref,
             pqa, pka, pva, pcq, pckv, pkr, gqn, gkn, gqln, gkvln, guq, guk):
        @pl.when(pl.program_id(0) == 0)
        def _():
            for r in (gqn, gkn, gqln, gkvln, guq, guk):
                r[...] = jnp.zeros(r.shape, F32)

        ca, sa, ct, st = ca_ref[...], sa_ref[...], ct_ref[...], st_ref[...]
        wide = lambda t, n: jnp.concatenate([t] * n, axis=1)
        rows = lambda a: jnp.sum(a, axis=0, keepdims=True)
        dx, dg = _head_norm_bwd(_rope_t(dqa_ref[...], wide(ca, 4), wide(sa, 4), 32), qa_ref[...], gq_ref[...],
                                bd_ref, HD)
        pqa[...] = dx.astype(MXU)
        gqn[...] += rows(dg)
        dk_all = jnp.concatenate([dka_ref[g] for g in range(A_KV)], axis=1)
        dx, dg = _head_norm_bwd(_rope_t(dk_all, ca, sa, 32), ka_ref[...], gk_ref[...], bd_ref[0:128, 0:128], HD)
        pka[...] = dx.astype(MXU)
        gkn[...] += rows(dg)
        pva[...] = jnp.concatenate([dva_ref[g] for g in range(A_KV)], axis=1).astype(MXU)
        cq_raw = cq_ref[...]
        cq_n = cq_raw * _rms(cq_raw) * qln_ref[...]
        qb = _mm_nt(cq_n, uqt_ref[...])
        d_lat = jnp.concatenate([dqb_ref[:, 0:half_lat], dqb_ref[:, half_w:half_w + half_lat]], axis=1)
        d_rope = jnp.concatenate([dqb_ref[:, half_lat:half_w], dqb_ref[:, half_w + half_lat:]], axis=1)
        for hh in range(B_HEADS):
            guk[:, B_NOPE * hh:B_NOPE * (hh + 1)] += _mm_tn(d_lat[:, B_KV_LORA * hh:B_KV_LORA * (hh + 1)],
                                                            qb[:, B_NOPE * hh:B_NOPE * (hh + 1)])
        dqb_all = jnp.concatenate([_mm_nt(d_lat, ukbd_ref[...]),
                                   _rope_t(d_rope, wide(ct, 2), wide(st, 2), 32)], axis=1)
        guq[...] += _mm_tn(dqb_all, cq_n)
        dx, dg = _rms_bwd(_mm(dqb_all, uqt_ref[...]), cq_raw, qln_ref[...])
        pcq[...] = dx.astype(MXU)
        gqln[...] += rows(dg)
        dkb_sum = dkb_ref[0] + dkb_ref[1]
        dckv = dkb_sum[:, 0:B_KV_LORA] + dvb_ref[0] + dvb_ref[1]
        dx, dg = _rms_bwd(dckv, ckv_ref[...], kvln_ref[...])
        pckv[...] = dx.astype(MXU)
        gkvln[...] += rows(dg)
        pkr[...] = _rope_t(dkb_sum[:, B_KV_LORA:B_QK], ct[:, 0:B_ROPE], st[:, 0:B_ROPE], 32).astype(MXU)

    sd = jax.ShapeDtypeStruct
    consts = [gq, gk, qln, kvln, w_uq_t, uk_bd, bd]
    in_specs = [_row_spec(tm, 512), _head_spec(A_KV, tm, HD), _head_spec(A_KV, tm, HD),
                _row_spec(tm, dqb.shape[1]), _head_spec(2, tm, B_QK), _head_spec(2, tm, B_KV_LORA),
                _row_spec(tm, 512), _row_spec(tm, 128), _row_spec(tm, B_Q_LORA), _row_spec(tm, B_KV_LORA)]
    in_specs += [_full_spec(a.shape) for a in consts] + [_row_spec(tm, 128)] * 4
    small = [sd(gq.shape, F32), sd(gk.shape, F32), sd(qln.shape, F32), sd(kvln.shape, F32), sd(w_uq_t.shape, F32),
             sd((B_KV_LORA, B_HEADS * B_NOPE), F32)]
    out_shape = (sd((s, 512), MXU), sd((s, 128), MXU), sd((s, 128), MXU), sd((s, B_Q_LORA), MXU),
                 sd((s, B_KV_LORA), MXU), sd((s, B_ROPE), MXU), *small)
    out_specs = (_row_spec(tm, 512), _row_spec(tm, 128), _row_spec(tm, 128), _row_spec(tm, B_Q_LORA),
                 _row_spec(tm, B_KV_LORA), _row_spec(tm, B_ROPE), *[_full_spec(a.shape) for a in small])
    return pl.pallas_call(
        body, name="even_prep_backward", grid=(s // tm,), out_shape=out_shape, in_specs=in_specs, out_specs=out_specs,
        compiler_params=_params(("arbitrary",)),
    )(dqa, dka, dva, dqb, dkb, dvb, qa_raw, ka_raw, cq_raw, ckv_raw, *consts, cos_a, sin_a, cos_t, sin_t)


def in_proj_backward(x, mod, nw, pieces, name, *, dx_out=None, w_in_t=None, dw_rows=None, exchange=None):
    s, d = x.shape
    tm = min(ROW_TILE, s)
    grid = (s // tm,)
    n = len(pieces)
    cols = [c for _, c in pieces]
    want_dx = w_in_t is not None
    want_dw = dw_rows is not None
    n_cols = sum(c1 - c0 for c0, c1 in cols)
    dw_block = n_cols // N_DEV
    hosted = exchange is not None
    nx = exchange.n if hosted else 0

    def body(*refs):
        it = iter(refs)
        x_ref, mod_ref, nw_ref = next(it), next(it), next(it)
        dxo_ref, wt_ref = (next(it), next(it)) if want_dx else (None, None)
        p_refs = [next(it) for _ in range(n)]
        xs_refs = [next(it) for _ in range(nx)]
        dx_ref, dv_ref = (next(it), next(it)) if want_dx else (None, None)
        dw_ref = next(it) if want_dw else None
        land_refs = [next(it) for _ in range(nx)]
        acc_ref = next(it) if want_dx else None
        dw_acc = next(it) if want_dw else None
        sems = list(it)
        first, last = _grid_edges(grid)
        if hosted:
            pl.when(first)(lambda: exchange.start(xs_refs, land_refs, sems))

        @pl.when(first)
        def _():
            if want_dw:
                dw_acc[...] = jnp.zeros(dw_acc.shape, F32)
            if want_dx:
                acc_ref[...] = jnp.zeros(acc_ref.shape, F32)

        xn, g1, h = _modulated(x_ref[...], mod_ref, nw_ref)
        hb = h.astype(MXU)
        dh = jnp.zeros((tm, d), F32)
        for k, (pr, (c0, c1)) in enumerate(zip(p_refs, cols)):
            if len(pr.shape) == 3:
                pc = jnp.concatenate([pr[g] for g in range(pr.shape[0])], axis=1).astype(MXU)
            else:
                pc = pr[...].astype(MXU)
            if want_dx:
                dh = dh + jnp.dot(pc, wt_ref[c0:c1, :], preferred_element_type=F32)
            if want_dw:
                r0, r1 = dw_rows[k]
                dw_acc[r0:r1, :] += _mm_tn(pc, hb)
        if want_dx:
            acc_ref[0:1, :] += jnp.sum(dh, axis=0, keepdims=True)
            acc_ref[1:2, :] += jnp.sum(dh * xn, axis=0, keepdims=True)
            dxn = dh * g1
            x = x_ref[...]
            dx_ref[...] = dxo_ref[...] + _rms(x) * (dxn - xn * jnp.mean(dxn * xn, axis=-1, keepdims=True))

        @pl.when(last)
        def _():
            if want_dx:
                dg1 = acc_ref[1:2, :]
                dv_ref[0:1, :] = acc_ref[0:1, :]
                dv_ref[1:2, :] = dg1 * nw_ref[...]
                dv_ref[2:3, :] = dg1 * (1.0 + mod_ref[1:2, :])
                dv_ref[3:4, :] = jnp.zeros((1, d), F32)
            if want_dw:
                for j in range(N_DEV):
                    dw_ref[j] = dw_acc[j * dw_block:(j + 1) * dw_block, :].astype(MXU)

        if hosted:
            pl.when(last)(lambda: exchange.wait(xs_refs, land_refs, sems))

    arrs = [a for a, _ in pieces]
    sd = jax.ShapeDtypeStruct
    args = [x, mod, nw] + ([dx_out, w_in_t] if want_dx else []) + arrs + (exchange.srcs if hosted else [])
    in_specs = [_row_spec(tm, d), _full_spec(mod.shape), _full_spec(nw.shape)]
    in_specs += [_row_spec(tm, d), _full_spec(w_in_t.shape)] if want_dx else []
    in_specs += [_row_spec(tm, a.shape[1]) if a.ndim == 2 else _head_spec(a.shape[0], tm, a.shape[2]) for a in arrs]
    in_specs += exchange.in_specs if hosted else []
    out_shape, out_specs, scratch = [], [], []
    if want_dx:
        out_shape += [sd((s, d), F32), sd((4, d), F32)]
        out_specs += [_row_spec(tm, d), _full_spec((4, d))]
        scratch.append(pltpu.VMEM((8, d), F32))
    if want_dw:
        out_shape.append(sd((N_DEV, dw_block, d), MXU))
        out_specs.append(_full_spec((N_DEV, dw_block, d)))
        scratch.append(pltpu.VMEM((n_cols, d), F32))
    if hosted:
        out_shape += list(exchange.land_shapes)
        out_specs += list(exchange.out_specs)
        scratch += list(exchange.sems)
    return pl.pallas_call(
        body, name=name, grid=grid, out_shape=tuple(out_shape), in_specs=in_specs, out_specs=tuple(out_specs),
        scratch_shapes=scratch, compiler_params=_params(("arbitrary",)),
    )(*args)


def ada_weight_grad(c_all, dmod_cols):
    d = c_all.shape[1]
    w = dmod_cols.shape[2]

    def body(c_ref, dm_ref, out_ref):
        ca = _silu(c_ref[...])
        for l in range(2):
            out_ref[l] = _mm_tn(ca, dm_ref[l])

    return pl.pallas_call(
        body, name="ada_weight_grad",
        out_shape=jax.ShapeDtypeStruct((2, d, w), F32),
        compiler_params=pltpu.CompilerParams(vmem_limit_bytes=VMEM_LIMIT),
    )(c_all, dmod_cols)


def _slot_sum(g_ref):
    g = g_ref[0].astype(F32)
    for k in range(1, g_ref.shape[0]):
        g = g + g_ref[k].astype(F32)
    return g


def _adamw_math(g, w, m, v):
    m_new = ADAM_B1 * m + (1.0 - ADAM_B1) * g
    v_new = ADAM_B2 * v + (1.0 - ADAM_B2) * (g * g)
    m_hat = m_new / (1.0 - ADAM_B1 ** ADAM_STEP)
    v_hat = v_new / (1.0 - ADAM_B2 ** ADAM_STEP)
    return -ADAM_LR * (m_hat / (jnp.sqrt(v_hat) + ADAM_EPS) + ADAM_WD * w), m_new, v_new


def adamw_small(g_alls, ws, ms, vs, loss_all):
    n = len(ws)

    def body(*refs):
        g_refs, w_refs, m_refs, v_refs = (refs[i * n:(i + 1) * n] for i in range(4))
        loss_ref = refs[4 * n]
        outs = refs[4 * n + 1:]
        for i in range(n):
            g = _slot_sum(g_refs[i])
            outs[i][...] = g
            outs[n + i][...], outs[2 * n + i][...], outs[3 * n + i][...] = _adamw_math(
                g, w_refs[i][...], m_refs[i][...], v_refs[i][...])
        outs[4 * n][...] = _slot_sum(loss_ref)

    sds = [jax.ShapeDtypeStruct(w.shape, F32) for w in ws]
    res = pl.pallas_call(
        body, name="adamw_small", out_shape=tuple(sds * 4) + (jax.ShapeDtypeStruct(loss_all.shape[1:], F32),),
        compiler_params=pltpu.CompilerParams(vmem_limit_bytes=VMEM_LIMIT),
    )(*g_alls, *ws, *ms, *vs, loss_all)
    return [res[i * n:(i + 1) * n] for i in range(4)], res[4 * n]


def adamw_rows(g_slots, w, m, v, name):
    n, r, lanes = g_slots.shape
    fits = [t for t in range(16, r + 1, 16) if r % t == 0 and t * lanes <= ADAM_TILE]
    tr = max(fits) if fits else r
    def body(g_ref, w_ref, m_ref, v_ref, go, do, mo, vo):
        g = _slot_sum(g_ref)
        go[...] = g
        do[...], mo[...], vo[...] = _adamw_math(g, w_ref[...], m_ref[...], v_ref[...])

    row = pl.BlockSpec((tr, lanes), lambda i: (i, 0))
    sd = jax.ShapeDtypeStruct((r, lanes), F32)
    return pl.pallas_call(
        body, name=name, grid=(r // tr,), out_shape=(sd, sd, sd, sd),
        in_specs=[pl.BlockSpec((n, tr, lanes), lambda i: (0, i, 0)), row, row, row],
        out_specs=(row, row, row, row),
        compiler_params=_params(("parallel",)),
    )(g_slots, w, m, v)


def _rope_tables(s):
    def cs(pos, dim):
        inv = ROPE_THETA ** (-np.arange(0, dim, 2, dtype=np.float32) / dim)
        ang = pos.astype(np.float32)[:, None] * inv.astype(np.float32)[None, :]
        return np.cos(ang), np.sin(ang)

    rows = s // GRID_W
    row = np.repeat(np.arange(rows), GRID_W)
    col = np.tile(np.arange(GRID_W), rows)
    cr, sr = cs(row, HD // 2)
    cc, sc = cs(col, HD // 2)
    ct, st = cs(np.arange(s), B_ROPE)
    tables = (np.concatenate([cr, cr, cc, cc] * 2, axis=-1), np.concatenate([-sr, sr, -sc, sc] * 2, axis=-1),
              np.concatenate([ct, ct] * 4, axis=-1), np.concatenate([-st, st] * 4, axis=-1))
    return tuple(jnp.asarray(t, F32) for t in tables)


def _even_rows_to_kernel(wt):
    return jnp.concatenate([wt[:1664], wt[1696:], wt[1664:1696]], axis=0)


def _uq_rows_to_kernel(wt):
    r = wt.reshape(B_HEADS, B_NOPE + B_ROPE, -1)
    return jnp.concatenate([r[:, :B_NOPE].reshape(B_HEADS * B_NOPE, -1), r[:, B_NOPE:].reshape(B_HEADS * B_ROPE, -1)])


def _uq_rows_to_reference(wt):
    nope = wt[:B_HEADS * B_NOPE].reshape(B_HEADS, B_NOPE, -1)
    rope = wt[B_HEADS * B_NOPE:].reshape(B_HEADS, B_ROPE, -1)
    return jnp.concatenate([nope, rope], axis=1).reshape(B_HEADS * (B_NOPE + B_ROPE), -1)


def _shard_t(w):
    return jnp.transpose(w[0])


def _unshard_t(wt, like):
    return jnp.transpose(wt)[None].reshape(like.shape)


def kernel(x, c, norm_w, ada_w, ada_b, even_w_in, a_q_norm, a_k_norm, b_q_lora_norm, b_kv_lora_norm, b_w_uq, b_w_uk, b_w_uv, even_w_out, odd_w_in, c_sink, odd_w_out, final_norm, loss_target, m_norm_w, m_ada_w, m_ada_b, m_even_w_in, m_a_q_norm, m_a_k_norm, m_b_q_lora_norm, m_b_kv_lora_norm, m_b_w_uq, m_b_w_uk, m_b_w_uv, m_even_w_out, m_odd_w_in, m_c_sink, m_odd_w_out, m_final_norm, v_norm_w, v_ada_w, v_ada_b, v_even_w_in, v_a_q_norm, v_a_k_norm, v_b_q_lora_norm, v_b_kv_lora_norm, v_b_w_uq, v_b_w_uk, v_b_w_uv, v_even_w_out, v_odd_w_in, v_c_sink, v_odd_w_out, v_final_norm):
    s, d = x.shape[1], x.shape[2]
    x0 = x[0]
    target = loss_target[0]
    me_flat = 4 * lax.axis_index("x") + 2 * lax.axis_index("y") + lax.axis_index("c")

    wcols = ada_w.shape[2]
    bias_cols = lax.dynamic_slice_in_dim(ada_b.reshape(2, N_DEV, wcols), me_flat, 1, axis=1)
    call, modp, (g_in_e, g_uq) = ada_forward(
        jnp.broadcast_to(c, (8, d)), ada_w, bias_cols,
        Gather([_shard_t(even_w_in).astype(MXU), _shard_t(b_w_uq).astype(MXU)]))
    wt_in_e = _even_rows_to_kernel(g_in_e.reshape(-1, d))
    wt_uq = _uq_rows_to_kernel(g_uq.reshape(-1, B_Q_LORA))
    later_exchange = Exchange([_shard_t(odd_w_in).astype(MXU), even_w_out[0].astype(MXU),
                               odd_w_out[0].astype(MXU)], scatter=False)
    uk_bd = (jnp.eye(B_HEADS, dtype=F32)[:, None, :, None] * jnp.transpose(b_w_uk[0], (1, 2, 0))[:, :, None, :]
             ).reshape(B_HEADS * B_NOPE, B_HEADS * B_KV_LORA).astype(MXU)
    head_bd = jnp.asarray(np.kron(np.eye(A_HEADS), np.ones((HD, HD))), MXU)
    gq_full, gk_full = jnp.tile(a_q_norm, (1, A_HEADS)), jnp.tile(a_k_norm, (1, A_KV))
    w_uv = jnp.transpose(b_w_uv[0], (1, 0, 2)).astype(MXU)

    c_all = call[:, 0, :]
    mod = jnp.transpose(modp[:, :, 0, :], (1, 0, 2)).reshape(2, 3, d)
    mod_e, mod_o = mod[0], mod[1]
    nw_e, nw_o = norm_w[0:1], norm_w[1:2]

    cos_a, sin_a, cos_t, sin_t = _rope_tables(s)
    slopes = (2.0 ** (-8.0 * jnp.arange(1, C_HEADS + 1, dtype=F32) / C_HEADS)).reshape(C_HEADS, 1, 1)
    sink2 = c_sink.reshape(C_HEADS, 1, 1) * LOG2E

    (qa, ka, va, qb, kb, kat, vat, kbt, qa_raw, ka_raw, cq_raw, ckv_raw, ga, gb) = even_in_forward(
        x0, mod_e, nw_e, wt_in_e, gq_full, gk_full, b_q_lora_norm, b_kv_lora_norm, wt_uq, uk_bd, head_bd,
        cos_a, sin_a, cos_t, sin_t)
    tk_dense = min(512, s)
    tq_dense = min(256, s)
    fwd_sub = min(8, s // tk_dense)
    bwd_sub_a = min(16, s // tq_dense)
    bwd_sub_b = min(8, s // tq_dense)
    oa, lse_a, g_in_o, g_out_e, g_out_o = flash_forward(
        qa, ka, vat, dv=HD, tq=tq_dense, tk=tk_dense, nsub=fwd_sub, name="attn_a_fwd",
        exchange=later_exchange)
    wt_in_o = g_in_o.reshape(-1, d)
    w_out_e = g_out_e.reshape(-1, d)
    w_out_o = g_out_o.reshape(-1, d)
    o_lat, lse_b = flash_forward(qb, kb, kbt, dv=B_KV_LORA, tq=min(128, s), tk=tk_dense, nsub=fwd_sub,
                                 name="attn_b_fwd")
    ob = latent_out_forward(o_lat, w_uv)
    x1, y_e = mixer_out_forward(x0, mod_e, [(oa, ga), (ob, gb)], w_out_e, "even_out_fwd")

    qc, kc, vc, kct, vct, gc = odd_in_forward(x1, mod_o, nw_o, wt_in_o)
    win_sub = min(8, s // WINDOW)
    oc, lse_c = window_forward(qc, kc, vct, sink2, slopes, win_sub, "attn_c_fwd")
    dx2, y_o, loss_lanes, d_final = mixer_out_forward(x1, mod_o, [(oc, gc)], w_out_o, "odd_out_fwd_loss",
                                                      loss=(target, final_norm.reshape(1, d)))

    loss_part = (0.5 / d) * jnp.sum(loss_lanes)

    doc, dgc, delta_c, dgate_o, dw_out_o, dsink = mixer_out_backward(
        dx2, y_o, mod_o, [(oc, gc)], w_out_o, [C_HEADS], "odd_out_bwd", lse=lse_c.reshape(C_HEADS, s),
        sink=sink2.reshape(C_HEADS, 1))
    rows3 = lambda t: t.reshape(t.shape[0], 1, s)
    dqc, dkc, dvc = window_backward(qc, kc, kct, vc, doc, lse_c, rows3(delta_c), slopes, win_sub, "attn_c_bwd")
    dx1, dvec_o, dwt_in_o = in_proj_backward(
        x1, mod_o, nw_o, [(dqc, O_Q), (dkc, O_K), (dvc, O_V), (dgc, O_G)], "odd_in_bwd",
        dx_out=dx2, w_in_t=wt_in_o, dw_rows=[O_Q, O_K, O_V, O_G])

    doa, dga, dob, dgb, delta_a, dgate_e, dw_out_e = mixer_out_backward(
        dx1, y_e, mod_e, [(oa, ga), (ob, gb)], w_out_e, [A_HEADS, 0], "even_out_bwd")
    d_olat, delta_b, dw_uv = latent_out_backward(dob, o_lat, w_uv)
    blocks = lambda g: g.astype(MXU).reshape(N_DEV, g.shape[0] // N_DEV, g.shape[1])
    even_pieces = lambda: [(pqa, E_QA), (pka, E_KA), (pva, E_VA), (dga, E_GA), (pcq, E_CQ), (pckv, E_CKV),
                           (dgb, E_GB), (pkr, E_KR)]
    scatter_odd = Exchange([dwt_in_o, dw_out_o], True)
    scatter_out_e = Exchange([dw_out_e], True)
    dqb, dkb, dvb, l_in_o, l_out_o = flash_backward(
        qb, kb, kbt, None, d_olat, lse_b, rows3(delta_b), scale=SCALE_B, dv=B_KV_LORA,
        tq=tq_dense, tk=tk_dense, nsub=bwd_sub_b, gq=2, name="attn_b_bwd", split=B_KV_LORA, exchange=scatter_odd)
    dqa, dka, dva, l_out_e = flash_backward(
        qa, ka, kat, va, doa, lse_a, rows3(delta_a), scale=SCALE_A, dv=HD,
        tq=tq_dense, tk=tk_dense, nsub=bwd_sub_a, gq=A_KV, name="attn_a_bwd", exchange=scatter_out_e)
    (pqa, pka, pva, pcq, pckv, pkr, g_qn, g_kn, g_qln, g_kvln, dwt_uq, dw_uk) = even_prep_backward(
        dqa, dka, dva, dqb, dkb, dvb, qa_raw, ka_raw, cq_raw, ckv_raw,
        gq_full, gk_full, b_q_lora_norm, b_kv_lora_norm, wt_uq, uk_bd, head_bd, cos_a, sin_a, cos_t, sin_t)
    g_qn = jnp.sum(g_qn.reshape(A_HEADS, HD), axis=0)
    g_kn = jnp.sum(g_kn.reshape(A_KV, HD), axis=0)
    dwt_in_e, l_uk, l_uv = in_proj_backward(
        x0, mod_e, nw_e, even_pieces(), "even_in_bwd_dw",
        dw_rows=[E_QA, E_KA, E_VA, E_GA, E_CQ, E_CKV, (1696, 2208), (1664, 1696)],
        exchange=Exchange([dw_uk.astype(MXU), dw_uv.astype(MXU)], scatter=False))
    dx0, dvec_e, l_in_e, l_uq = in_proj_backward(
        x0, mod_e, nw_e, even_pieces(), "even_in_bwd_dx", dx_out=dx1, w_in_t=wt_in_e,
        exchange=Exchange([dwt_in_e, blocks(_uq_rows_to_reference(dwt_uq))], True))

    dmod = jnp.stack([jnp.concatenate([dvec_e[0], dvec_e[1], dgate_e[0]]),
                      jnp.concatenate([dvec_o[0], dvec_o[1], dgate_o[0]])])
    d_norm_w = jnp.stack([dvec_e[2], dvec_o[2]])
    small_names = ["norm_w", "ada_b", "a_q_norm", "a_k_norm", "b_q_lora_norm", "b_kv_lora_norm", "b_w_uk", "b_w_uv",
                   "c_sink", "final_norm"]
    small_w = [norm_w, ada_b, a_q_norm, a_k_norm, b_q_lora_norm, b_kv_lora_norm, b_w_uk, b_w_uv, c_sink, final_norm]
    small_m = [m_norm_w, m_ada_b, m_a_q_norm, m_a_k_norm, m_b_q_lora_norm, m_b_kv_lora_norm, m_b_w_uk, m_b_w_uv,
               m_c_sink, m_final_norm]
    small_v = [v_norm_w, v_ada_b, v_a_q_norm, v_a_k_norm, v_b_q_lora_norm, v_b_kv_lora_norm, v_b_w_uk, v_b_w_uv,
               v_c_sink, v_final_norm]
    small_g = [d_norm_w, dmod, g_qn, g_kn, g_qln, g_kvln, None, None, dsink, d_final]
    flat2 = lambda a: a.reshape((1, -1)) if a.size == a.shape[-1] else a.reshape(a.shape[-3:] if a.ndim > 3 else a.shape)
    kshape = [flat2(w).shape for w in small_w]
    late = [i for i, g in enumerate(small_g) if g is not None]
    gathered = all_gather_slots(
        Gather([small_g[i].reshape(kshape[i]) for i in late] + [jnp.full((8, 128), loss_part, F32)]),
        "gather_small_grads")
    g_all = [None] * len(small_g)
    for i, g in zip(late, gathered):
        g_all[i] = g
    g_all[6], g_all[7] = (l.reshape((N_DEV,) + kshape[6]) for l in (l_uk, l_uv))
    sm_out, loss_sum = adamw_small(g_all, [flat2(a) for a in small_w], [flat2(a) for a in small_m],
                                   [flat2(a) for a in small_v], gathered[-1])
    loss = loss_sum[0, 0]
    sm = [{nm: p.reshape(w.shape) for nm, w, p in zip(small_names, small_w, outs)} for outs in sm_out]

    dmod_all = g_all[1].reshape(N_DEV, 2, N_DEV, wcols)
    dmod_cols = lax.dynamic_slice_in_dim(dmod_all, me_flat, 1, axis=2)[:, :, 0, :]
    pad16 = lambda a: jnp.concatenate([a, jnp.zeros_like(a)], axis=0)
    g_ada_w = ada_weight_grad(pad16(c_all), jnp.transpose(pad16(dmod_cols), (1, 0, 2)))
    rows_of = lambda a: a.reshape(-1, wcols)
    ada = adamw_rows(rows_of(g_ada_w)[None], rows_of(ada_w), rows_of(m_ada_w), rows_of(v_ada_w), "adamw_ada_w")
    ada = [p.reshape(ada_w.shape) for p in ada]

    bg = [{}, {}, {}, {}]
    for nm, landed, w, m, v, transposed in (
            ("even_w_in", l_in_e, even_w_in, m_even_w_in, v_even_w_in, True),
            ("b_w_uq", l_uq, b_w_uq, m_b_w_uq, v_b_w_uq, True),
            ("odd_w_in", l_in_o, odd_w_in, m_odd_w_in, v_odd_w_in, True),
            ("even_w_out", l_out_e, even_w_out, m_even_w_out, v_even_w_out, False),
            ("odd_w_out", l_out_o, odd_w_out, m_odd_w_out, v_odd_w_out, False)):
        view = _shard_t if transposed else (lambda a: a[0])
        res = adamw_rows(landed, view(w), view(m), view(v), "adamw_" + nm)
        for kind, p in enumerate(res):
            bg[kind][nm] = _unshard_t(p, w) if transposed else p[None]
    big_names = ["even_w_in", "odd_w_in", "even_w_out", "odd_w_out", "b_w_uq"]

    order = ["norm_w", "ada_w", "ada_b", "even_w_in", "a_q_norm", "a_k_norm", "b_q_lora_norm", "b_kv_lora_norm",
             "b_w_uq", "b_w_uk", "b_w_uv", "even_w_out", "odd_w_in", "c_sink", "odd_w_out", "final_norm"]

    def pick(kind):
        out = []
        for nm in order:
            if nm == "ada_w":
                out.append(ada[kind])
            elif nm in big_names:
                out.append(bg[kind][nm])
            else:
                out.append(sm[kind][nm])
        return out

    return (loss, dx0[None], *pick(0), *pick(1), *pick(2), *pick(3))
```

```python
import functools

import jax
import jax.numpy as jnp
import numpy as np
from jax import lax
from jax.experimental import pallas as pl
from jax.experimental.pallas import tpu as pltpu

F32 = jnp.float32
MXU = jnp.bfloat16
EPS = 1e-6
ROPE_THETA = 10000.0
GRID_W = 64
HD = 64
N_DEV = 8

A_HEADS, A_KV = 8, 2
B_HEADS, B_NOPE, B_ROPE, B_Q_LORA, B_KV_LORA = 8, 64, 32, 256, 128
B_QK = B_KV_LORA + B_ROPE
C_HEADS, C_KV = 16, 4
WINDOW = 128

ADAM_LR, ADAM_B1, ADAM_B2, ADAM_EPS, ADAM_WD, ADAM_STEP = 0.001, 0.9, 0.999, 1e-08, 0.01, 10

ROW_TILE = 512
ADAM_TILE = 2048 * 128
LOG2E = 1.4426950408889634
SCALE_A = HD ** -0.5
SCALE_B = (B_NOPE + B_ROPE) ** -0.5
SCALE2_A, SCALE2_B = SCALE_A * LOG2E, SCALE_B * LOG2E
VMEM_LIMIT = 56 * 1024 * 1024

E_QA, E_KA, E_VA, E_GA, E_CQ, E_CKV, E_GB, E_KR = (
    (0, 512), (512, 640), (640, 768), (768, 1280), (1280, 1536), (1536, 1664), (1664, 2176), (2176, 2208))
O_Q, O_K, O_V, O_G = (0, 1024), (1024, 1280), (1280, 1536), (1536, 2560)


def _mm(a, b):
    return jnp.dot(a.astype(MXU), b.astype(MXU), preferred_element_type=F32)


def _mm_nt(a, b):
    return lax.dot_general(a.astype(MXU), b.astype(MXU), (((1,), (1,)), ((), ())), preferred_element_type=F32)


def _mm_tn(a, b):
    return lax.dot_general(a.astype(MXU), b.astype(MXU), (((0,), (0,)), ((), ())), preferred_element_type=F32)


def _group_sums_t(prod, group):
    tm, w = prod.shape
    sel = (lax.broadcasted_iota(jnp.int32, (w, 128), 0) // group
           == lax.broadcasted_iota(jnp.int32, (w, 128), 1)).astype(MXU)
    hi = prod.astype(MXU)
    lo = prod - hi.astype(F32)
    return (_mm(hi, sel) + _mm(lo, sel)).T


def _sigmoid(z):
    return 1.0 / (1.0 + jnp.exp(-z))


def _silu(z):
    return z * _sigmoid(z)


def _rms(x):
    return lax.rsqrt(jnp.mean(x * x, axis=-1, keepdims=True) + EPS)


def _swap_halves(y, group):
    n = y.shape[-1]
    half = group // 2
    fwd = pltpu.roll(y, half, 1)
    if n == group:
        return fwd
    back = pltpu.roll(y, n - half, 1)
    lane = lax.broadcasted_iota(jnp.int32, y.shape, 1)
    return jnp.where((lane % group) < half, back, fwd)


def _rope(y, cos, sin, group):
    return y * cos + _swap_halves(y, group) * sin


def _rope_t(d, cos, sin, group):
    return d * cos - _swap_halves(d, group) * sin


def _rms_bwd(dy, x, g):
    r = _rms(x)
    xhat = x * r
    dxhat = dy * g
    dx = r * (dxhat - xhat * jnp.mean(dxhat * xhat, axis=-1, keepdims=True))
    return dx, dy * xhat


def _group_mean(v, bd, group):
    hi = v.astype(MXU)
    lo = v - hi.astype(F32)
    return (_mm(hi, bd[...]) + _mm(lo, bd[...])) * (1.0 / group)


def _head_norm(x, g, bd, group):
    return x * lax.rsqrt(_group_mean(x * x, bd, group) + EPS) * g


def _head_norm_bwd(dy, x, g, bd, group):
    r = lax.rsqrt(_group_mean(x * x, bd, group) + EPS)
    xhat = x * r
    dxhat = dy * g
    dx = r * (dxhat - xhat * _group_mean(dxhat * xhat, bd, group))
    return dx, dy * xhat


def _params(sem, vmem=VMEM_LIMIT):
    return pltpu.CompilerParams(dimension_semantics=sem, vmem_limit_bytes=vmem)


def _row_spec(tm, w):
    return pl.BlockSpec((tm, w), lambda i: (i, 0))


def _full_spec(shape):
    nd = len(shape)
    return pl.BlockSpec(shape, lambda i: (0,) * nd)


def _head_spec(h, tm, w):
    return pl.BlockSpec((h, tm, w), lambda i: (0, i, 0))


def _headt_spec(h, w, tm):
    return pl.BlockSpec((h, w, tm), lambda i: (0, 0, i))


def _rows_spec(h, tm):
    return pl.BlockSpec((h, tm), lambda i: (0, i))


def _me():
    return lax.axis_index("x"), lax.axis_index("y"), lax.axis_index("c")


def _flat(p):
    return 4 * p[0] + 2 * p[1] + p[2]


def _peer(me, k):
    x, y, c = me
    return (1 - x if k & 4 else x, 1 - y if k & 2 else y, 1 - c if k & 1 else c)


MESH_ID = pl.DeviceIdType.MESH


class Gather:
    VMEM = pl.BlockSpec(memory_space=pltpu.VMEM)

    def __init__(self, shards):
        self.shards = list(shards)
        self.n = len(self.shards)
        self.out_shapes = tuple(jax.ShapeDtypeStruct((N_DEV,) + a.shape, a.dtype) for a in self.shards)
        self.in_specs = [Gather.VMEM] * self.n
        self.out_specs = (Gather.VMEM,) * self.n
        self.sems = [pltpu.SemaphoreType.DMA((7 * self.n,)), pltpu.SemaphoreType.DMA((7 * self.n,)),
                     pltpu.SemaphoreType.DMA((self.n,))]

    def _plan(self, x_refs, out_refs, sems):
        send_sems, recv_sems, local_sems = sems
        me = _me()
        x, y, c = me
        chips = [(1 - x, y), (x, 1 - y), (1 - x, 1 - y)]

        def copy(a, k, block, to, src=None):
            slot = out_refs[a].at[_flat(block)]
            return pltpu.make_async_remote_copy(
                src_ref=slot if src is None else src, dst_ref=slot, send_sem=send_sems.at[7 * a + k],
                recv_sem=recv_sems.at[7 * a + k], device_id=to, device_id_type=MESH_ID)

        mine = [pltpu.make_async_copy(x_refs[a], out_refs[a].at[_flat(me)], local_sems.at[a]) for a in range(self.n)]
        first = [copy(a, 0, me, (x, y, 1 - c), src=x_refs[a]) for a in range(self.n)]
        first += [copy(a, 1 + j, me, (*chip, c), src=x_refs[a]) for a in range(self.n) for j, chip in enumerate(chips)]
        return me, chips, copy, mine, first

    def start(self, x_refs, out_refs, sems):
        _, _, _, mine, first = self._plan(x_refs, out_refs, sems)
        for cp in mine + first:
            cp.start()

    def forward(self, x_refs, out_refs, sems):
        me, chips, copy, _, _ = self._plan(x_refs, out_refs, sems)
        x, y, c = me
        for a in range(self.n):
            for j, chip in enumerate(chips):
                copy(a, 1 + j, (*chip, c), me).wait_recv()
                copy(a, 4 + j, (*chip, c), (x, y, 1 - c)).start()

    def drain(self, x_refs, out_refs, sems):
        me, chips, copy, mine, first = self._plan(x_refs, out_refs, sems)
        x, y, c = me
        sibling = (x, y, 1 - c)
        for a in range(self.n):
            copy(a, 0, sibling, me).wait_recv()
            for j, chip in enumerate(chips):
                copy(a, 4 + j, (*chip, 1 - c), me).wait_recv()
        for cp in first + [copy(a, 4 + j, (*chip, c), sibling) for a in range(self.n) for j, chip in enumerate(chips)]:
            cp.wait_send()
        for cp in mine:
            cp.wait()

    def finish(self, x_refs, out_refs, sems):
        self.forward(x_refs, out_refs, sems)
        self.drain(x_refs, out_refs, sems)


def all_gather_slots(gather, name):
    def body(*refs):
        x_refs, out_refs, sems = refs[:gather.n], refs[gather.n:2 * gather.n], refs[2 * gather.n:]
        gather.start(x_refs, out_refs, sems)
        gather.finish(x_refs, out_refs, sems)

    return pl.pallas_call(
        body, name=name, out_shape=gather.out_shapes, in_specs=gather.in_specs, out_specs=gather.out_specs,
        scratch_shapes=list(gather.sems), compiler_params=pltpu.CompilerParams(vmem_limit_bytes=VMEM_LIMIT),
    )(*gather.shards)


class Exchange:
    HBM = pl.BlockSpec(memory_space=pl.ANY)

    def __init__(self, srcs, scatter):
        self.srcs = list(srcs)
        self.scatter = scatter
        self.n = len(self.srcs)
        self.land_shapes = tuple(jax.ShapeDtypeStruct((N_DEV,) + tuple(a.shape[-2:]), a.dtype) for a in self.srcs)
        self.in_specs = [Exchange.HBM] * self.n
        self.out_specs = (Exchange.HBM,) * self.n
        self.sems = [pltpu.SemaphoreType.DMA((N_DEV - 1,)), pltpu.SemaphoreType.DMA((N_DEV - 1,)),
                     pltpu.SemaphoreType.DMA] * self.n

    def _copies(self, src_refs, land_refs, sems):
        me = _me()
        mi = _flat(me)
        local, sends, recvs = [], [], []
        for a, (src_ref, land_ref) in enumerate(zip(src_refs, land_refs)):
            send_sems, recv_sems, local_sem = sems[3 * a:3 * a + 3]
            pick = (lambda p, r=src_ref: r.at[_flat(p)]) if self.scatter else (lambda p, r=src_ref: r)
            local.append(pltpu.make_async_copy(pick(me), land_ref.at[mi], local_sem))
            for k in range(1, N_DEV):
                peer = _peer(me, k)
                pair = dict(send_sem=send_sems.at[k - 1], recv_sem=recv_sems.at[k - 1], device_id=peer,
                            device_id_type=MESH_ID)
                sends.append(pltpu.make_async_remote_copy(src_ref=pick(peer), dst_ref=land_ref.at[mi], **pair))
                recvs.append(pltpu.make_async_remote_copy(src_ref=pick(peer), dst_ref=land_ref.at[_flat(peer)],
                                                          **pair))
        return local, sends, recvs

    def start(self, src_refs, land_refs, sems):
        local, sends, _ = self._copies(src_refs, land_refs, sems)
        for cp in local + sends:
            cp.start()

    def wait(self, src_refs, land_refs, sems):
        local, sends, recvs = self._copies(src_refs, land_refs, sems)
        for cp in recvs:
            cp.wait_recv()
        for cp in sends:
            cp.wait_send()
        for cp in local:
            cp.wait()


def ada_forward(c8, ada_w, bias_cols, gather):
    d = c8.shape[1]
    w = ada_w.shape[2]
    ng = gather.n

    def body(*refs):
        c_ref, w_ref, b_ref = refs[:3]
        gx_refs = refs[3:3 + ng]
        call_ref, modp_ref = refs[3 + ng:5 + ng]
        gout_refs = refs[5 + ng:5 + 2 * ng]
        part_ref, s1, r1, s2, r2 = refs[5 + 2 * ng:10 + 2 * ng]
        g_sems = refs[10 + 2 * ng:]
        me = _me()
        mi = _flat(me)
        call_ref[mi] = c_ref[...]
        rows_out = []
        for k in range(1, N_DEV):
            rows_out.append(pltpu.make_async_remote_copy(
                src_ref=c_ref, dst_ref=call_ref.at[mi], send_sem=s1.at[k - 1], recv_sem=r1.at[k - 1],
                device_id=_peer(me, k), device_id_type=MESH_ID))
        for cp in rows_out:
            cp.start()
        gather.start(gx_refs, gout_refs, g_sems)
        for k in range(1, N_DEV):
            pltpu.make_async_remote_copy(
                src_ref=c_ref, dst_ref=call_ref.at[_flat(_peer(me, k))], send_sem=s1.at[k - 1],
                recv_sem=r1.at[k - 1], device_id=_peer(me, k), device_id_type=MESH_ID).wait_recv()
        ca = _silu(call_ref[...].reshape(N_DEV * 8, d))
        for l in range(2):
            part = _mm(ca, w_ref[l]) + b_ref[l]
            for b in range(N_DEV):
                part_ref[b, l] = part[8 * b:8 * b + 8, :]
        modp_ref[mi] = part_ref[mi]
        spread = []
        for k in range(1, N_DEV):
            peer = _peer(me, k)
            spread.append(pltpu.make_async_remote_copy(
                src_ref=part_ref.at[_flat(peer)], dst_ref=modp_ref.at[mi], send_sem=s2.at[k - 1],
                recv_sem=r2.at[k - 1], device_id=peer, device_id_type=MESH_ID))
        for cp in spread:
            cp.start()
        gather.forward(gx_refs, gout_refs, g_sems)
        for k in range(1, N_DEV):
            pi = _flat(_peer(me, k))
            pltpu.make_async_remote_copy(
                src_ref=part_ref.at[pi], dst_ref=modp_ref.at[pi], send_sem=s2.at[k - 1],
                recv_sem=r2.at[k - 1], device_id=_peer(me, k), device_id_type=MESH_ID).wait_recv()
        for cp in rows_out + spread:
            cp.wait_send()
        gather.drain(gx_refs, gout_refs, g_sems)

    vm = pl.BlockSpec(memory_space=pltpu.VMEM)
    res = pl.pallas_call(
        body, name="ada_forward",
        out_shape=(jax.ShapeDtypeStruct((N_DEV, 8, d), F32), jax.ShapeDtypeStruct((N_DEV, 2, 8, w), F32))
        + gather.out_shapes,
        in_specs=[vm, vm, vm] + gather.in_specs, out_specs=(vm, vm) + gather.out_specs,
        scratch_shapes=[pltpu.VMEM((N_DEV, 2, 8, w), F32)] + [pltpu.SemaphoreType.DMA((7,))] * 4 + list(gather.sems),
        compiler_params=pltpu.CompilerParams(vmem_limit_bytes=VMEM_LIMIT),
    )(c8, ada_w, bias_cols, *gather.shards)
    return res[0], res[1], res[2:]


def _modulated(x, mod_ref, nw_ref):
    xn = x * _rms(x)
    g1 = nw_ref[...] * (1.0 + mod_ref[1:2, :])
    return xn, g1, xn * g1 + mod_ref[0:1, :]


def even_in_forward(x, mod, nw, w_in_t, gq, gk, qln, kvln, w_uq_t, uk_bd, bd, cos_a, sin_a, cos_t, sin_t):
    s, d = x.shape
    tm = min(ROW_TILE, s)
    n_nope = B_HEADS * B_NOPE

    def body(x_ref, mod_ref, nw_ref, w_ref, gq_ref, gk_ref, qln_ref, kvln_ref, uq_ref, ukbd_ref, bd_ref,
             ca_ref, sa_ref, ct_ref, st_ref,
             qa_o, ka_o, va_o, qb_o, kb_o, kat_o, vat_o, kbt_o, qa_raw_o, ka_raw_o, cq_raw_o, ckv_raw_o, ga_o, gb_o):
        _, _, h = _modulated(x_ref[...], mod_ref, nw_ref)
        h = h.astype(MXU)

        def proj(cols):
            return _mm_nt(h, w_ref[cols[0]:cols[1], :])

        ca, sa, ct, st = ca_ref[...], sa_ref[...], ct_ref[...], st_ref[...]
        wide = lambda t, n: jnp.concatenate([t] * n, axis=1)
        qa = proj(E_QA)
        qa_raw_o[...] = qa
        qr = _rope(_head_norm(qa, gq_ref[...], bd_ref, HD), wide(ca, 4), wide(sa, 4), 32) * SCALE2_A
        for hh in range(A_HEADS):
            qa_o[hh] = qr[:, HD * hh:HD * hh + HD].astype(MXU)
        ka = proj(E_KA)
        ka_raw_o[...] = ka
        kr = _rope(_head_norm(ka, gk_ref[...], bd_ref[0:128, 0:128], HD), ca, sa, 32)
        va = proj(E_VA)
        krt, vat = kr.T, va.T
        for g in range(A_KV):
            ka_o[g] = kr[:, HD * g:HD * g + HD].astype(MXU)
            va_o[g] = va[:, HD * g:HD * g + HD].astype(MXU)
            kat_o[g] = krt[HD * g:HD * g + HD, :].astype(MXU)
            vat_o[g] = vat[HD * g:HD * g + HD, :].astype(MXU)
        ga_o[...] = proj(E_GA).astype(MXU)
        gb_o[...] = proj(E_GB).astype(MXU)
        cq = proj(E_CQ)
        cq_raw_o[...] = cq
        qb = _mm_nt(cq * _rms(cq) * qln_ref[...], uq_ref[...])
        q_lat = _mm(qb[:, 0:n_nope], ukbd_ref[...]) * SCALE2_B
        q_rope = _rope(qb[:, n_nope:], wide(ct, 2), wide(st, 2), 32) * SCALE2_B
        for hh in range(B_HEADS):
            qb_o[hh, :, 0:B_KV_LORA] = q_lat[:, B_KV_LORA * hh:B_KV_LORA * (hh + 1)].astype(MXU)
            qb_o[hh, :, B_KV_LORA:B_QK] = q_rope[:, B_ROPE * hh:B_ROPE * (hh + 1)].astype(MXU)
        ckv = proj(E_CKV)
        ckv_raw_o[...] = ckv
        ckv_n = ckv * _rms(ckv) * kvln_ref[...]
        k_rope = _rope(proj(E_KR), ct[:, 0:B_ROPE], st[:, 0:B_ROPE], 32)
        kb_o[0, :, 0:B_KV_LORA] = ckv_n.astype(MXU)
        kb_o[0, :, B_KV_LORA:B_QK] = k_rope.astype(MXU)
        kbt_o[0, 0:B_KV_LORA, :] = ckv_n.T.astype(MXU)
        kbt_o[0, B_KV_LORA:B_QK, :] = k_rope.T.astype(MXU)

    sd = jax.ShapeDtypeStruct
    outs = (sd((A_HEADS, s, HD), MXU), sd((A_KV, s, HD), MXU), sd((A_KV, s, HD), MXU),
            sd((B_HEADS, s, B_QK), MXU), sd((1, s, B_QK), MXU),
            sd((A_KV, HD, s), MXU), sd((A_KV, HD, s), MXU), sd((1, B_QK, s), MXU),
            sd((s, 512), F32), sd((s, 128), F32), sd((s, B_Q_LORA), F32), sd((s, B_KV_LORA), F32),
            sd((s, 512), MXU), sd((s, 512), MXU))
    out_specs = (_head_spec(A_HEADS, tm, HD), _head_spec(A_KV, tm, HD), _head_spec(A_KV, tm, HD),
                 _head_spec(B_HEADS, tm, B_QK), _head_spec(1, tm, B_QK),
                 _headt_spec(A_KV, HD, tm), _headt_spec(A_KV, HD, tm), _headt_spec(1, B_QK, tm),
                 _row_spec(tm, 512), _row_spec(tm, 128), _row_spec(tm, B_Q_LORA), _row_spec(tm, B_KV_LORA),
                 _row_spec(tm, 512), _row_spec(tm, 512))
    consts = [mod, nw, w_in_t, gq, gk, qln, kvln, w_uq_t, uk_bd, bd]
    return pl.pallas_call(
        body, name="even_in_forward", grid=(s // tm,), out_shape=outs,
        in_specs=[_row_spec(tm, d)] + [_full_spec(a.shape) for a in consts] + [_row_spec(tm, 128)] * 4,
        out_specs=out_specs, compiler_params=_params(("parallel",)),
    )(x, *consts, cos_a, sin_a, cos_t, sin_t)


def odd_in_forward(x, mod, nw, w_in):
    s, d = x.shape
    tm = min(ROW_TILE, s)

    def body(x_ref, mod_ref, nw_ref, w_ref, q_o, k_o, v_o, kt_o, vt_o, g_o):
        _, _, h = _modulated(x_ref[...], mod_ref, nw_ref)
        h = h.astype(MXU)

        def proj(cols):
            return _mm_nt(h, w_ref[cols[0]:cols[1], :])

        q = proj(O_Q) * SCALE2_A
        for hh in range(C_HEADS):
            q_o[hh] = q[:, HD * hh:HD * hh + HD].astype(MXU)
        k = proj(O_K)
        v = proj(O_V)
        for g in range(C_KV):
            kh = k[:, HD * g:HD * g + HD]
            vh = v[:, HD * g:HD * g + HD]
            k_o[g] = kh.astype(MXU)
            v_o[g] = vh.astype(MXU)
            kt_o[g] = kh.T.astype(MXU)
            vt_o[g] = vh.T.astype(MXU)
        g_o[...] = proj(O_G).astype(MXU)

    sd = jax.ShapeDtypeStruct
    return pl.pallas_call(
        body, name="odd_in_forward", grid=(s // tm,),
        out_shape=(sd((C_HEADS, s, HD), MXU), sd((C_KV, s, HD), MXU), sd((C_KV, s, HD), MXU),
                   sd((C_KV, HD, s), MXU), sd((C_KV, HD, s), MXU), sd((s, 1024), MXU)),
        in_specs=[_row_spec(tm, d), _full_spec(mod.shape), _full_spec(nw.shape), _full_spec(w_in.shape)],
        out_specs=(_head_spec(C_HEADS, tm, HD), _head_spec(C_KV, tm, HD), _head_spec(C_KV, tm, HD),
                   _headt_spec(C_KV, HD, tm), _headt_spec(C_KV, HD, tm), _row_spec(tm, 1024)),
        compiler_params=_params(("parallel",)),
    )(x, mod, nw, w_in)


def mixer_out_forward(x, mod, pairs, w_out, name, loss=None, latent=None):
    s, d = x.shape
    tm = min(ROW_TILE, s)
    n = len(pairs)
    widths = [g.shape[1] for _, g in pairs]
    head = loss is not None
    up = latent is not None

    def body(*refs):
        x_ref, mod_ref, w_ref = refs[:3]
        pr = refs[3:3 + 2 * n]
        rest = refs[3 + 2 * n:]
        if up:
            uv_ref, rest, ob_ref = rest[0], rest[1:-1], rest[-1]
        y = jnp.zeros((tm, d), F32)
        r0 = 0
        for i in range(n):
            o = pr[2 * i][...]
            if up and i == n - 1:
                o = _mm(o, uv_ref[...]).astype(MXU)
                ob_ref[...] = o
            mix = o.astype(F32) * _silu(pr[2 * i + 1][...].astype(F32))
            y = y + _mm(mix, w_ref[r0:r0 + widths[i], :])
            r0 += widths[i]
        x_out = x_ref[...] + mod_ref[2:3, :] * y
        if not head:
            xo_ref, y_ref = rest
            xo_ref[...] = x_out
        else:
            t_ref, fn_ref, dx_ref, y_ref, lp_ref, dw_ref = rest

            @pl.when(pl.program_id(0) == 0)
            def _():
                lp_ref[...] = jnp.zeros(lp_ref.shape, F32)
                dw_ref[...] = jnp.zeros(dw_ref.shape, F32)

            g = fn_ref[...]
            err = x_out * _rms(x_out) * g - t_ref[...]
            lp_ref[...] += jnp.sum(err * err, axis=0, keepdims=True)
            dx, dg = _rms_bwd(err * (1.0 / d), x_out, g)
            dx_ref[...] = dx
            dw_ref[...] += jnp.sum(dg, axis=0, keepdims=True)
        y_ref[...] = y.astype(y_ref.dtype)

    flat = [a for p in pairs for a in p]
    sd = jax.ShapeDtypeStruct
    in_specs = [_row_spec(tm, d), _full_spec(mod.shape), _full_spec(w_out.shape)]
    in_specs += [_row_spec(tm, a.shape[1]) for a in flat]
    out_shape = (sd((s, d), F32), sd((s, d), MXU))
    out_specs = (_row_spec(tm, d), _row_spec(tm, d))
    if up:
        in_specs.append(_full_spec(latent.shape))
    if head:
        in_specs += [_row_spec(tm, d), _full_spec(loss[1].shape)]
        out_shape += (sd((1, d), F32), sd((1, d), F32))
        out_specs += (_full_spec((1, d)), _full_spec((1, d)))
    if up:
        out_shape += (sd((s, widths[-1]), MXU),)
        out_specs += (_row_spec(tm, widths[-1]),)
    return pl.pallas_call(
        body, name=name, grid=(s // tm,), out_shape=out_shape, in_specs=in_specs, out_specs=out_specs,
        compiler_params=_params(("arbitrary",) if head else ("parallel",)),
    )(x, mod, w_out, *flat, *((latent,) if up else ()), *(loss if head else ()))


ONES_ROWS = 16
AHEAD = 2


def _col_max8(s3):
    m8 = jnp.max(s3, axis=0)
    return jnp.broadcast_to(jnp.max(m8, axis=0, keepdims=True), m8.shape)


def _with_ones(vt, n):
    return jnp.concatenate([vt, jnp.ones((ONES_ROWS, n), vt.dtype)], axis=0)


def _grid_edges(grid):
    ids = [pl.program_id(a) for a in range(len(grid))]
    first = functools.reduce(jnp.logical_and, [i == 0 for i in ids])
    last = functools.reduce(jnp.logical_and, [i == n - 1 for i, n in zip(ids, grid)])
    return first, last


def flash_forward(q, k, vt, *, dv, tq, tk, nsub, name, exchange=None):
    hq, s, dq = q.shape
    g_kv = k.shape[0]
    hpg = hq // g_kv
    nq = s // tq
    tkk = tk * nsub
    nk = s // tkk
    grid = (g_kv, nq, nk)
    hosted = exchange is not None
    m_cols = hpg * tq
    dvp = dv + ONES_ROWS

    def body(*refs):
        nx = exchange.n if hosted else 0
        q_ref, k_ref, vt_ref = refs[:3]
        xs_refs = refs[3:3 + nx]
        o_ref, lse_ref = refs[3 + nx:5 + nx]
        land_refs = refs[5 + nx:5 + 2 * nx]
        m_s, acc_s = refs[5 + 2 * nx:7 + 2 * nx]
        sems = refs[7 + 2 * nx:]
        if hosted:
            first, last = _grid_edges(grid)
            pl.when(first)(lambda: exchange.start(xs_refs, land_refs, sems))
        j = pl.program_id(2)

        @pl.when(j == 0)
        def _():
            m_s[...] = jnp.full((8, m_cols), -jnp.inf, F32)
            acc_s[...] = jnp.zeros((dvp, m_cols), F32)

        qq = q_ref[...].reshape(m_cols, dq)
        score = lambda u: _mm_nt(k_ref[0, tk * u:tk * (u + 1), :], qq).reshape(tk // 8, 8, m_cols)
        sts = {u: score(u) for u in range(min(AHEAD, nsub))}
        m_run = m_s[...]
        acc = acc_s[...]
        for u in range(nsub):
            if u + AHEAD < nsub:
                sts[u + AHEAD] = score(u + AHEAD)
            st = sts.pop(u)
            m_new = jnp.maximum(m_run, _col_max8(st))
            p = jnp.exp2(st - m_new[None])
            alpha = jnp.exp2(m_run - m_new)
            pv = _mm(_with_ones(vt_ref[0, 0:dv, tk * u:tk * (u + 1)], tk), p.reshape(tk, m_cols))
            acc = (acc.reshape(dvp // 8, 8, m_cols) * alpha[None]).reshape(dvp, m_cols) + pv
            m_run = m_new
        acc_s[...] = acc
        m_s[...] = m_run

        @pl.when(j == nk - 1)
        def _():
            l = acc_s[dv:dv + 1, :]
            ot = acc_s[0:dv, :] / l
            lse = m_s[0:1, :] + jnp.log2(l)
            for hh in range(hpg):
                o_ref[:, dv * hh:dv * hh + dv] = ot[:, tq * hh:tq * hh + tq].T.astype(MXU)
                lse_ref[hh] = lse[:, tq * hh:tq * hh + tq]

        if hosted:
            pl.when(last)(lambda: exchange.wait(xs_refs, land_refs, sems))

    sd = jax.ShapeDtypeStruct
    return pl.pallas_call(
        body, name=name, grid=grid,
        out_shape=(sd((s, hq * dv), MXU), sd((hq, 1, s), F32)) + (exchange.land_shapes if hosted else ()),
        in_specs=[pl.BlockSpec((hpg, tq, dq), lambda g, i, j: (g, i, 0)),
                  pl.BlockSpec((1, tkk, k.shape[2]), lambda g, i, j: (g, j, 0)),
                  pl.BlockSpec((1, dv, tkk), lambda g, i, j: (g, 0, j))] + (exchange.in_specs if hosted else []),
        out_specs=(pl.BlockSpec((tq, hpg * dv), lambda g, i, j: (i, g)),
                   pl.BlockSpec((hpg, 1, tq), lambda g, i, j: (g, 0, i))) + (exchange.out_specs if hosted else ()),
        scratch_shapes=[pltpu.VMEM((8, m_cols), F32), pltpu.VMEM((dvp, m_cols), F32)]
        + (list(exchange.sems) if hosted else []),
        compiler_params=_params(("arbitrary",) * 3 if hosted else ("parallel", "parallel", "arbitrary")),
    )(q, k, vt, *(exchange.srcs if hosted else []))


def _window_bias_t(hpg, slope_ref):
    t = WINDOW
    r = lax.broadcasted_iota(jnp.int32, (3 * t, t), 0)
    cq = lax.broadcasted_iota(jnp.int32, (3 * t, t), 1)
    arel = jnp.abs(r - t - cq)
    base = jnp.where(arel <= WINDOW, arel.astype(F32) * (-LOG2E), -jnp.inf)
    return jnp.concatenate([base * slope_ref[hh] for hh in range(hpg)], axis=1)


def _window_edges_t(bias, no_before, no_after):
    t = WINDOW
    r = lax.broadcasted_iota(jnp.int32, bias.shape, 0)
    out = ((r < t) & no_before) | ((r >= 2 * t) & no_after)
    return jnp.where(out, -jnp.inf, bias)


def _window_specs(kind, nb, nblk, d):
    t = WINDOW
    before = lambda i: jnp.clip(i * nb - 1, 0, nblk - 1)
    after = lambda i: jnp.clip((i + 1) * nb, 0, nblk - 1)
    if kind == "rows":
        return [pl.BlockSpec((1, t, d), lambda g, i: (g, before(i), 0)),
                pl.BlockSpec((1, nb * t, d), lambda g, i: (g, i, 0)),
                pl.BlockSpec((1, t, d), lambda g, i: (g, after(i), 0))]
    return [pl.BlockSpec((1, d, t), lambda g, i: (g, 0, before(i))),
            pl.BlockSpec((1, d, nb * t), lambda g, i: (g, 0, i)),
            pl.BlockSpec((1, d, t), lambda g, i: (g, 0, after(i)))]


def window_forward(q, k, vt, sink2, slopes, nb, name):
    hq, s, d = q.shape
    g_kv = k.shape[0]
    hpg = hq // g_kv
    t = WINDOW
    nblk = s // t
    steps = nblk // nb
    m_cols = hpg * t

    def body(q_ref, kp, ko, kn, vp, vo, vn, sink_ref, slope_ref, o_ref, lse_ref):
        i = pl.program_id(1)
        kk_all = jnp.concatenate([kp[0], ko[0], kn[0]], axis=0)
        vt_all = jnp.concatenate([vp[0], vo[0], vn[0]], axis=1)
        bias = _window_bias_t(hpg, slope_ref)
        sink_row = jnp.concatenate([jnp.broadcast_to(sink_ref[hh], (8, t)) for hh in range(hpg)], axis=1)
        sts = {}

        def score(u):
            qq = q_ref[:, t * u:t * (u + 1), :].reshape(m_cols, d)
            b_u = bias
            if u == 0 or u == nb - 1:
                b_u = _window_edges_t(bias, (i == 0) if u == 0 else False,
                                      (i == steps - 1) if u == nb - 1 else False)
            sts[u] = _mm_nt(kk_all[t * u:t * (u + 3), :], qq) + b_u

        for u in range(min(AHEAD, nb)):
            score(u)
        for u in range(nb):
            if u + AHEAD < nb:
                score(u + AHEAD)
            s3 = sts.pop(u).reshape(3 * t // 8, 8, m_cols)
            m8 = jnp.maximum(_col_max8(s3), sink_row)
            p = jnp.exp2(s3 - m8[None]).reshape(3 * t, m_cols)
            acc = _mm(_with_ones(vt_all[:, t * u:t * (u + 3)], 3 * t), p)
            l = acc[d:d + 1, :] + jnp.exp2(sink_row[0:1, :] - m8[0:1, :])
            ot = acc[0:d, :] / l
            lse = m8[0:1, :] + jnp.log2(l)
            for hh in range(hpg):
                o_ref[t * u:t * (u + 1), d * hh:d * hh + d] = ot[:, t * hh:t * hh + t].T.astype(MXU)
                lse_ref[hh, :, t * u:t * (u + 1)] = lse[:, t * hh:t * hh + t]

    sd = jax.ShapeDtypeStruct
    return pl.pallas_call(
        body, name=name, grid=(g_kv, steps),
        out_shape=(sd((s, hq * d), MXU), sd((hq, 1, s), F32)),
        in_specs=[pl.BlockSpec((hpg, nb * t, d), lambda g, i: (g, i, 0))]
        + _window_specs("rows", nb, nblk, d) + _window_specs("cols", nb, nblk, d)
        + [pl.BlockSpec((hpg, 1, 1), lambda g, i: (g, 0, 0))] * 2,
        out_specs=(pl.BlockSpec((nb * t, hpg * d), lambda g, i: (i, g)),
                   pl.BlockSpec((hpg, 1, nb * t), lambda g, i: (g, 0, i))),
        compiler_params=_params(("parallel", "parallel")),
    )(q, k, k, k, vt, vt, vt, sink2, slopes)


def window_backward(q, k, kt, v, do, lse, delta, slopes, nb, name):
    hq, s, d = q.shape
    g_kv = k.shape[0]
    hpg = hq // g_kv
    t = WINDOW
    nblk = s // t
    steps = nblk // nb
    m_cols = hpg * t

    def body(q_ref, kp, ko, kn, ktp, kto, ktn, vp, vo, vn, do_ref, lse_ref, dl_ref, slope_ref,
             dq_ref, dk_ref, dv_ref, dk_s, dv_s):
        i = pl.program_id(1)

        @pl.when(i == 0)
        def _():
            dk_ref[...] = jnp.zeros(dk_ref.shape, F32)
            dv_ref[...] = jnp.zeros(dv_ref.shape, F32)

        dk_s[...] = jnp.zeros(dk_s.shape, F32)
        dv_s[...] = jnp.zeros(dv_s.shape, F32)
        kk_all = jnp.concatenate([kp[0], ko[0], kn[0]], axis=0)
        vv_all = jnp.concatenate([vp[0], vo[0], vn[0]], axis=0)
        kkt_all = jnp.concatenate([ktp[0], kto[0], ktn[0]], axis=1)
        bias = _window_bias_t(hpg, slope_ref)
        qqs, dds, sts, dps = {}, {}, {}, {}

        def issue(u):
            rows = slice(t * u, t * (u + 1))
            keys = slice(t * u, t * (u + 3))
            qqs[u] = q_ref[:, rows, :].reshape(m_cols, d)
            dds[u] = jnp.concatenate([do_ref[rows, d * hh:d * hh + d] for hh in range(hpg)], axis=0)
            b_u = bias
            if u == 0 or u == nb - 1:
                b_u = _window_edges_t(bias, (i == 0) if u == 0 else False,
                                      (i == steps - 1) if u == nb - 1 else False)
            sts[u] = _mm_nt(kk_all[keys, :], qqs[u]) + b_u
            dps[u] = _mm_nt(vv_all[keys, :], dds[u])

        for u in range(min(AHEAD, nb)):
            issue(u)
        for u in range(nb):
            if u + AHEAD < nb:
                issue(u + AHEAD)
            rows = slice(t * u, t * (u + 1))
            keys = slice(t * u, t * (u + 3))
            lse_row = jnp.concatenate([lse_ref[hh, :, rows] for hh in range(hpg)], axis=1)
            dl_row = jnp.concatenate([dl_ref[hh, :, rows] for hh in range(hpg)], axis=1)
            p = jnp.exp2(sts[u] - lse_row)
            ds = p * (dps[u] - dl_row) * SCALE_A
            dv_s[keys, :] += _mm(p, dds[u])
            dk_s[keys, :] += _mm(ds, qqs[u])
            dqt = _mm(kkt_all[:, keys], ds)
            for hh in range(hpg):
                dq_ref[rows, d * hh:d * hh + d] = dqt[:, t * hh:t * hh + t].T.astype(dq_ref.dtype)
        tq = nb * t
        for src, r0, n in ((0, jnp.clip(i * nb - 1, 0, nblk - 1) * t, t), (t, i * tq, tq),
                           (t + tq, jnp.clip((i + 1) * nb, 0, nblk - 1) * t, t)):
            dst = pl.ds(pl.multiple_of(r0, t), n)
            dk_ref[0, dst, :] += dk_s[src:src + n, :] * (1.0 / SCALE2_A)
            dv_ref[0, dst, :] += dv_s[src:src + n, :]

    row_map = lambda g, i: (g, 0, i)
    sd = jax.ShapeDtypeStruct
    return pl.pallas_call(
        body, name=name, grid=(g_kv, steps),
        out_shape=(sd((s, hq * d), MXU), sd((g_kv, s, d), F32), sd((g_kv, s, d), F32)),
        in_specs=[pl.BlockSpec((hpg, nb * t, d), lambda g, i: (g, i, 0))]
        + _window_specs("rows", nb, nblk, d) + _window_specs("cols", nb, nblk, d) + _window_specs("rows", nb, nblk, d)
        + [pl.BlockSpec((nb * t, hpg * d), lambda g, i: (i, g)), pl.BlockSpec((hpg, 1, nb * t), row_map),
           pl.BlockSpec((hpg, 1, nb * t), row_map), pl.BlockSpec((hpg, 1, 1), lambda g, i: (g, 0, 0))],
        out_specs=(pl.BlockSpec((nb * t, hpg * d), lambda g, i: (i, g)),
                   pl.BlockSpec((1, s, d), lambda g, i: (g, 0, 0)),
                   pl.BlockSpec((1, s, d), lambda g, i: (g, 0, 0))),
        scratch_shapes=[pltpu.VMEM(((nb + 2) * t, d), F32), pltpu.VMEM(((nb + 2) * t, d), F32)],
        compiler_params=_params(("parallel", "arbitrary")),
    )(q, k, k, k, kt, kt, kt, v, v, v, do, lse, delta, slopes)


def flash_backward(q, k, kt, v, do, lse, delta, *, scale, dv, tq, tk, nsub, gq, name, split=None, exchange=None):
    hq, s, dq = q.shape
    g_kv = k.shape[0]
    hpg = hq // gq
    nq = s // tq
    tqq = tq * nsub
    nqs = s // tqq
    nkb = s // tk
    grid = (gq, nkb, nqs)
    hosted = exchange is not None
    m_cols = hpg * tq
    c = scale * LOG2E
    has_v = v is not None

    def body(*refs):
        it = iter(refs)
        q_ref, k_ref, kt_ref = next(it), next(it), next(it)
        v_ref = next(it) if has_v else None
        do_ref, lse_ref, dl_ref = next(it), next(it), next(it)
        nx = exchange.n if hosted else 0
        xs_refs = [next(it) for _ in range(nx)]
        dq_ref, dk_ref, dv_ref = next(it), next(it), next(it)
        land_refs = [next(it) for _ in range(nx)]
        dqt_s = next(it)
        sems = list(it)
        kj = pl.program_id(1)
        qi = pl.program_id(2)
        if hosted:
            first, last = _grid_edges(grid)
            pl.when(first)(lambda: exchange.start(xs_refs, land_refs, sems))

        @pl.when((kj == 0) & (qi == 0))
        def _():
            dqt_s[...] = jnp.zeros(dqt_s.shape, F32)

        @pl.when(qi == 0)
        def _():
            dk_ref[...] = jnp.zeros(dk_ref.shape, F32)
            dv_ref[...] = jnp.zeros(dv_ref.shape, F32)

        kk = k_ref[0]
        vv = v_ref[0] if has_v else kk[:, :dv]
        qqs, dds, sts, dps = {}, {}, {}, {}

        def issue(u):
            rows = slice(tq * u, tq * (u + 1))
            qqs[u] = q_ref[:, rows, :].reshape(m_cols, dq)
            dds[u] = jnp.concatenate([do_ref[rows, dv * hh:dv * hh + dv] for hh in range(hpg)], axis=0)
            sts[u] = _mm_nt(kk, qqs[u])
            dps[u] = _mm_nt(vv, dds[u])

        for u in range(min(AHEAD, nsub)):
            issue(u)
        dv_acc = dv_ref[0]
        dk_acc = dk_ref[0]
        for u in range(nsub):
            if u + AHEAD < nsub:
                issue(u + AHEAD)
            rows = slice(tq * u, tq * (u + 1))
            lse_row = jnp.concatenate([lse_ref[hh, :, rows] for hh in range(hpg)], axis=1)
            dl_row = jnp.concatenate([dl_ref[hh, :, rows] for hh in range(hpg)], axis=1)
            p = jnp.exp2(sts[u] - lse_row)
            ds = p * (dps[u] - dl_row) * scale
            dv_acc = dv_acc + _mm(p, dds[u])
            dk_acc = dk_acc + _mm(ds, qqs[u])
            dqt = _mm(kt_ref[0], ds)
            for hh in range(hpg):
                dqt_s[qi * nsub + u, dq * hh:dq * hh + dq, :] += dqt[:, tq * hh:tq * hh + tq]
        dv_ref[0] = dv_acc
        dk_ref[0] = jnp.where(qi == nqs - 1, dk_acc * (1.0 / c), dk_acc)

        @pl.when((kj == nkb - 1) & (qi == nqs - 1))
        def _():
            def emit(t, carry):
                r0 = pl.multiple_of(t * tq, tq)
                for hh in range(hpg):
                    blk = dqt_s[t, dq * hh:dq * hh + dq, :].T
                    if split is None:
                        dq_ref[pl.ds(r0, tq), dq * hh:dq * hh + dq] = blk
                    else:
                        rest = dq - split
                        dq_ref[pl.ds(r0, tq), split * hh:split * (hh + 1)] = blk[:, 0:split]
                        dq_ref[pl.ds(r0, tq), hpg * split + rest * hh:hpg * split + rest * (hh + 1)] = blk[:, split:]
                return carry

            lax.fori_loop(0, nq, emit, 0)

        if hosted:
            pl.when(last)(lambda: exchange.wait(xs_refs, land_refs, sems))

    kv_of = lambda g: g * g_kv // gq
    in_specs = [pl.BlockSpec((hpg, tqq, dq), lambda g, kj, qi: (g, qi, 0)),
                pl.BlockSpec((1, tk, dq), lambda g, kj, qi: (kv_of(g), kj, 0)),
                pl.BlockSpec((1, dq, tk), lambda g, kj, qi: (kv_of(g), 0, kj))]
    args = [q, k, kt]
    if has_v:
        in_specs.append(pl.BlockSpec((1, tk, dv), lambda g, kj, qi: (kv_of(g), kj, 0)))
        args.append(v)
    row_map = lambda g, kj, qi: (g, 0, qi)
    in_specs += [pl.BlockSpec((tqq, hpg * dv), lambda g, kj, qi: (qi, g)),
                 pl.BlockSpec((hpg, 1, tqq), row_map), pl.BlockSpec((hpg, 1, tqq), row_map)]
    args += [do, lse, delta]
    if hosted:
        in_specs += exchange.in_specs
        args += exchange.srcs
    sd = jax.ShapeDtypeStruct
    return pl.pallas_call(
        body, name=name, grid=grid,
        out_shape=(sd((s, hq * dq), F32), sd((gq, s, dq), F32), sd((gq, s, dv), F32))
        + (exchange.land_shapes if hosted else ()),
        in_specs=in_specs,
        out_specs=(pl.BlockSpec((s, hpg * dq), lambda g, kj, qi: (0, g)),
                   pl.BlockSpec((1, tk, dq), lambda g, kj, qi: (g, kj, 0)),
                   pl.BlockSpec((1, tk, dv), lambda g, kj, qi: (g, kj, 0))) + (exchange.out_specs if hosted else ()),
        scratch_shapes=[pltpu.VMEM((nq, hpg * dq, tq), F32)] + (list(exchange.sems) if hosted else []),
        compiler_params=_params(("arbitrary",) * 3 if hosted else ("parallel", "arbitrary", "arbitrary")),
    )(*args)


def mixer_out_backward(dx, y, mod, pairs, w_out, delta_heads, name, lse=None, sink=None):
    s, d = dx.shape
    tm = min(ROW_TILE, s)
    n = len(pairs)
    widths = [o.shape[1] for o, _ in pairs]
    n_delta = sum(1 for h in delta_heads if h)
    with_sink = lse is not None

    def body(*refs):
        it = iter(refs)
        dx_ref, y_ref, mod_ref, wt_ref = next(it), next(it), next(it), next(it)
        pr = [next(it) for _ in range(2 * n)]
        lse_ref = next(it) if with_sink else None
        sink_ref = next(it) if with_sink else None
        outs = [next(it) for _ in range(2 * n)]
        dl_refs = [next(it) for _ in range(n_delta)]
        dgate_ref, dw_ref = next(it), next(it)
        dsink_ref = next(it) if with_sink else None
        dw_acc = next(it)

        @pl.when(pl.program_id(0) == 0)
        def _():
            dgate_ref[...] = jnp.zeros(dgate_ref.shape, F32)
            dw_acc[...] = jnp.zeros(dw_acc.shape, F32)
            if with_sink:
                dsink_ref[...] = jnp.zeros(dsink_ref.shape, F32)

        dxo = dx_ref[...]
        dgate_ref[...] += jnp.sum(dxo * y_ref[...].astype(F32), axis=0, keepdims=True)
        dy = (dxo * mod_ref[2:3, :]).astype(MXU)
        dmix = _mm_nt(dy, wt_ref[...])
        r0 = 0
        di = 0
        for i in range(n):
            o = pr[2 * i][...].astype(F32)
            g = pr[2 * i + 1][...].astype(F32)
            dm = dmix[:, r0:r0 + widths[i]]
            sg = _sigmoid(g)
            act = g * sg
            do = dm * act
            outs[2 * i][...] = do.astype(MXU)
            outs[2 * i + 1][...] = (dm * o * (sg * (1.0 + g * (1.0 - sg)))).astype(MXU)
            dw_acc[r0:r0 + widths[i], :] += _mm_tn(o * act, dy)
            if delta_heads[i]:
                dlt = _group_sums_t(do * o, HD)[0:delta_heads[i], :]
                dl_refs[di][...] = dlt
                if with_sink:
                    ps = jnp.exp2(sink_ref[...] - lse_ref[...])
                    dsink_ref[...] += -jnp.sum(ps * dlt, axis=1, keepdims=True)
                di += 1
            r0 += widths[i]

        @pl.when(pl.program_id(0) == pl.num_programs(0) - 1)
        def _():
            for j in range(N_DEV):
                dw_ref[j] = dw_acc[j * dw_block:(j + 1) * dw_block, :].astype(MXU)

    dw_block = sum(widths) // N_DEV
    flat = [a for p in pairs for a in p]
    sd = jax.ShapeDtypeStruct
    in_specs = [_row_spec(tm, d), _row_spec(tm, d), _full_spec(mod.shape), _full_spec(w_out.shape)]
    in_specs += [_row_spec(tm, a.shape[1]) for a in flat]
    args = [dx, y, mod, w_out] + flat
    if with_sink:
        nh = lse.shape[0]
        in_specs += [_rows_spec(nh, tm), _full_spec(sink.shape)]
        args += [lse, sink]
    out_shape = [sd((s, a.shape[1]), MXU) for a in flat]
    out_specs = [_row_spec(tm, a.shape[1]) for a in flat]
    for h in delta_heads:
        if h:
            out_shape.append(sd((h, s), F32))
            out_specs.append(_rows_spec(h, tm))
    out_shape += [sd((1, d), F32), sd((N_DEV, dw_block, d), MXU)]
    out_specs += [_full_spec((1, d)), _full_spec((N_DEV, dw_block, d))]
    if with_sink:
        out_shape.append(sd((lse.shape[0], 1), F32))
        out_specs.append(_full_spec((lse.shape[0], 1)))
    return pl.pallas_call(
        body, name=name, grid=(s // tm,), out_shape=tuple(out_shape), in_specs=in_specs, out_specs=tuple(out_specs),
        scratch_shapes=[pltpu.VMEM((sum(widths), d), F32)], compiler_params=_params(("arbitrary",)),
    )(*args)


def latent_out_backward(d_ob, o_lat, w_uv):
    s = o_lat.shape[0]
    tm = min(ROW_TILE, s)

    def body(d_ref, o_ref, uv_ref, dol_ref, dl_ref, duv_ref, prod_s):
        @pl.when(pl.program_id(0) == 0)
        def _():
            duv_ref[...] = jnp.zeros(duv_ref.shape, F32)

        for hh in range(B_HEADS):
            dh = d_ref[:, HD * hh:HD * hh + HD]
            ol = o_ref[:, B_KV_LORA * hh:B_KV_LORA * (hh + 1)].astype(F32)
            dol = _mm_nt(dh, uv_ref[hh])
            dol_ref[:, B_KV_LORA * hh:B_KV_LORA * (hh + 1)] = dol.astype(MXU)
            prod_s[:, B_KV_LORA * hh:B_KV_LORA * (hh + 1)] = dol * ol
            duv_ref[:, HD * hh:HD * hh + HD] += _mm_tn(ol, dh)
        dl_ref[...] = _group_sums_t(prod_s[...], B_KV_LORA)[0:B_HEADS, :]

    sd = jax.ShapeDtypeStruct
    duv_shape = (B_KV_LORA, B_HEADS * HD)
    return pl.pallas_call(
        body, name="latent_out_backward", grid=(s // tm,),
        out_shape=(sd(o_lat.shape, MXU), sd((B_HEADS, s), F32), sd(duv_shape, F32)),
        in_specs=[_row_spec(tm, d_ob.shape[1]), _row_spec(tm, o_lat.shape[1]), _full_spec(w_uv.shape)],
        out_specs=(_row_spec(tm, o_lat.shape[1]), _rows_spec(B_HEADS, tm), _full_spec(duv_shape)),
        scratch_shapes=[pltpu.VMEM((tm, o_lat.shape[1]), F32)],
        compiler_params=_params(("arbitrary",)),
    )(d_ob, o_lat, w_uv)


def even_prep_backward(dqa, dka, dva, dqb, dkb, dvb, qa_raw, ka_raw, cq_raw, ckv_raw,
                       gq, gk, qln, kvln, w_uq_t, uk_bd, bd, cos_a, sin_a, cos_t, sin_t):
    s = qa_raw.shape[0]
    tm = min(ROW_TILE, s)
    half_lat = B_KV_LORA * B_HEADS // 2
    half_w = dqb.shape[1] // 2

    def body(dqa_ref, dka_ref, dva_ref, dqb_ref, dkb_ref, dvb_ref, qa_ref, ka_ref, cq_ref, ckv_ref,
             gq_ref, gk_ref, qln_ref, kvln_ref, uqt_ref, ukbd_ref, bd_ref, ca_ref, sa_ref, ct_ref, st_ref,
             pqa, pka, pva, pcq, pckv, pkr, gqn, gkn, gqln, gkvln, guq, guk):
        @pl.when(pl.program_id(0) == 0)
        def _():
            for r in (gqn, gkn, gqln, gkvln, guq, guk):
                r[...] = jnp.zeros(r.shape, F32)

        ca, sa, ct, st = ca_ref[...], sa_ref[...], ct_ref[...], st_ref[...]
        wide = lambda t, n: jnp.concatenate([t] * n, axis=1)
        rows = lambda a: jnp.sum(a, axis=0, keepdims=True)
        dx, dg = _head_norm_bwd(_rope_t(dqa_ref[...], wide(ca, 4), wide(sa, 4), 32), qa_ref[...], gq_ref[...],
                                bd_ref, HD)
        pqa[...] = dx.astype(MXU)
        gqn[...] += rows(dg)
        dk_all = jnp.concatenate([dka_ref[g] for g in range(A_KV)], axis=1)
        dx, dg = _head_norm_bwd(_rope_t(dk_all, ca, sa, 32), ka_ref[...], gk_ref[...], bd_ref[0:128, 0:128], HD)
        pka[...] = dx.astype(MXU)
        gkn[...] += rows(dg)
        pva[...] = jnp.concatenate([dva_ref[g] for g in range(A_KV)], axis=1).astype(MXU)
        cq_raw = cq_ref[...]
        cq_n = cq_raw * _rms(cq_raw) * qln_ref[...]
        qb = _mm_nt(cq_n, uqt_ref[...])
        d_lat = jnp.concatenate([dqb_ref[:, 0:half_lat], dqb_ref[:, half_w:half_w + half_lat]], axis=1)
        d_rope = jnp.concatenate([dqb_ref[:, half_lat:half_w], dqb_ref[:, half_w + half_lat:]], axis=1)
        for hh in range(B_HEADS):
            guk[:, B_NOPE * hh:B_NOPE * (hh + 1)] += _mm_tn(d_lat[:, B_KV_LORA * hh:B_KV_LORA * (hh + 1)],
                                                            qb[:, B_NOPE * hh:B_NOPE * (hh + 1)])
        dqb_all = jnp.concatenate([_mm_nt(d_lat, ukbd_ref[...]),
                                   _rope_t(d_rope, wide(ct, 2), wide(st, 2), 32)], axis=1)
        guq[...] += _mm_tn(dqb_all, cq_n)
        dx, dg = _rms_bwd(_mm(dqb_all, uqt_ref[...]), cq_raw, qln_ref[...])
        pcq[...] = dx.astype(MXU)
        gqln[...] += rows(dg)
        dkb_sum = dkb_ref[0] + dkb_ref[1]
        dckv = dkb_sum[:, 0:B_KV_LORA] + dvb_ref[0] + dvb_ref[1]
        dx, dg = _rms_bwd(dckv, ckv_ref[...], kvln_ref[...])
        pckv[...] = dx.astype(MXU)
        gkvln[...] += rows(dg)
        pkr[...] = _rope_t(dkb_sum[:, B_KV_LORA:B_QK], ct[:, 0:B_ROPE], st[:, 0:B_ROPE], 32).astype(MXU)

    sd = jax.ShapeDtypeStruct
    consts = [gq, gk, qln, kvln, w_uq_t, uk_bd, bd]
    in_specs = [_row_spec(tm, 512), _head_spec(A_KV, tm, HD), _head_spec(A_KV, tm, HD),
                _row_spec(tm, dqb.shape[1]), _head_spec(2, tm, B_QK), _head_spec(2, tm, B_KV_LORA),
                _row_spec(tm, 512), _row_spec(tm, 128), _row_spec(tm, B_Q_LORA), _row_spec(tm, B_KV_LORA)]
    in_specs += [_full_spec(a.shape) for a in consts] + [_row_spec(tm, 128)] * 4
    small = [sd(gq.shape, F32), sd(gk.shape, F32), sd(qln.shape, F32), sd(kvln.shape, F32), sd(w_uq_t.shape, F32),
             sd((B_KV_LORA, B_HEADS * B_NOPE), F32)]
    out_shape = (sd((s, 512), MXU), sd((s, 128), MXU), sd((s, 128), MXU), sd((s, B_Q_LORA), MXU),
                 sd((s, B_KV_LORA), MXU), sd((s, B_ROPE), MXU), *small)
    out_specs = (_row_spec(tm, 512), _row_spec(tm, 128), _row_spec(tm, 128), _row_spec(tm, B_Q_LORA),
                 _row_spec(tm, B_KV_LORA), _row_spec(tm, B_ROPE), *[_full_spec(a.shape) for a in small])
    return pl.pallas_call(
        body, name="even_prep_backward", grid=(s // tm,), out_shape=out_shape, in_specs=in_specs, out_specs=out_specs,
        compiler_params=_params(("arbitrary",)),
    )(dqa, dka, dva, dqb, dkb, dvb, qa_raw, ka_raw, cq_raw, ckv_raw, *consts, cos_a, sin_a, cos_t, sin_t)


def in_proj_backward(x, mod, nw, pieces, name, *, dx_out=None, w_in_t=None, dw_rows=None, exchange=None):
    s, d = x.shape
    tm = min(ROW_TILE, s)
    grid = (s // tm,)
    n = len(pieces)
    cols = [c for _, c in pieces]
    want_dx = w_in_t is not None
    want_dw = dw_rows is not None
    n_cols = sum(c1 - c0 for c0, c1 in cols)
    dw_block = n_cols // N_DEV
    hosted = exchange is not None
    nx = exchange.n if hosted else 0

    def body(*refs):
        it = iter(refs)
        x_ref, mod_ref, nw_ref = next(it), next(it), next(it)
        dxo_ref, wt_ref = (next(it), next(it)) if want_dx else (None, None)
        p_refs = [next(it) for _ in range(n)]
        xs_refs = [next(it) for _ in range(nx)]
        dx_ref, dv_ref = (next(it), next(it)) if want_dx else (None, None)
        dw_ref = next(it) if want_dw else None
        land_refs = [next(it) for _ in range(nx)]
        acc_ref = next(it) if want_dx else None
        dw_acc = next(it) if want_dw else None
        sems = list(it)
        first, last = _grid_edges(grid)
        if hosted:
            pl.when(first)(lambda: exchange.start(xs_refs, land_refs, sems))

        @pl.when(first)
        def _():
            if want_dw:
                dw_acc[...] = jnp.zeros(dw_acc.shape, F32)
            if want_dx:
                acc_ref[...] = jnp.zeros(acc_ref.shape, F32)

        xn, g1, h = _modulated(x_ref[...], mod_ref, nw_ref)
        hb = h.astype(MXU)
        dh = jnp.zeros((tm, d), F32)
        for k, (pr, (c0, c1)) in enumerate(zip(p_refs, cols)):
            if len(pr.shape) == 3:
                pc = jnp.concatenate([pr[g] for g in range(pr.shape[0])], axis=1).astype(MXU)
            else:
                pc = pr[...].astype(MXU)
            if want_dx:
                dh = dh + jnp.dot(pc, wt_ref[c0:c1, :], preferred_element_type=F32)
            if want_dw:
                r0, r1 = dw_rows[k]
                dw_acc[r0:r1, :] += _mm_tn(pc, hb)
        if want_dx:
            acc_ref[0:1, :] += jnp.sum(dh, axis=0, keepdims=True)
            acc_ref[1:2, :] += jnp.sum(dh * xn, axis=0, keepdims=True)
            dxn = dh * g1
            x = x_ref[...]
            dx_ref[...] = dxo_ref[...] + _rms(x) * (dxn - xn * jnp.mean(dxn * xn, axis=-1, keepdims=True))

        @pl.when(last)
        def _():
            if want_dx:
                dg1 = acc_ref[1:2, :]
                dv_ref[0:1, :] = acc_ref[0:1, :]
                dv_ref[1:2, :] = dg1 * nw_ref[...]
                dv_ref[2:3, :] = dg1 * (1.0 + mod_ref[1:2, :])
                dv_ref[3:4, :] = jnp.zeros((1, d), F32)
            if want_dw:
                for j in range(N_DEV):
                    dw_ref[j] = dw_acc[j * dw_block:(j + 1) * dw_block, :].astype(MXU)

        if hosted:
            pl.when(last)(lambda: exchange.wait(xs_refs, land_refs, sems))

    arrs = [a for a, _ in pieces]
    sd = jax.ShapeDtypeStruct
    args = [x, mod, nw] + ([dx_out, w_in_t] if want_dx else []) + arrs + (exchange.srcs if hosted else [])
    in_specs = [_row_spec(tm, d), _full_spec(mod.shape), _full_spec(nw.shape)]
    in_specs += [_row_spec(tm, d), _full_spec(w_in_t.shape)] if want_dx else []
    in_specs += [_row_spec(tm, a.shape[1]) if a.ndim == 2 else _head_spec(a.shape[0], tm, a.shape[2]) for a in arrs]
    in_specs += exchange.in_specs if hosted else []
    out_shape, out_specs, scratch = [], [], []
    if want_dx:
        out_shape += [sd((s, d), F32), sd((4, d), F32)]
        out_specs += [_row_spec(tm, d), _full_spec((4, d))]
        scratch.append(pltpu.VMEM((8, d), F32))
    if want_dw:
        out_shape.append(sd((N_DEV, dw_block, d), MXU))
        out_specs.append(_full_spec((N_DEV, dw_block, d)))
        scratch.append(pltpu.VMEM((n_cols, d), F32))
    if hosted:
        out_shape += list(exchange.land_shapes)
        out_specs += list(exchange.out_specs)
        scratch += list(exchange.sems)
    return pl.pallas_call(
        body, name=name, grid=grid, out_shape=tuple(out_shape), in_specs=in_specs, out_specs=tuple(out_specs),
        scratch_shapes=scratch, compiler_params=_params(("arbitrary",)),
    )(*args)


def ada_weight_grad(c_all, dmod_cols):
    d = c_all.shape[1]
    w = dmod_cols.shape[2]

    def body(c_ref, dm_ref, out_ref):
        ca = _silu(c_ref[...])
        for l in range(2):
            out_ref[l] = _mm_tn(ca, dm_ref[l])

    return pl.pallas_call(
        body, name="ada_weight_grad",
        out_shape=jax.ShapeDtypeStruct((2, d, w), F32),
        compiler_params=pltpu.CompilerParams(vmem_limit_bytes=VMEM_LIMIT),
    )(c_all, dmod_cols)


def _slot_sum(g_ref):
    g = g_ref[0].astype(F32)
    for k in range(1, g_ref.shape[0]):
        g = g + g_ref[k].astype(F32)
    return g


def _adamw_math(g, w, m, v):
    m_new = ADAM_B1 * m + (1.0 - ADAM_B1) * g
    v_new = ADAM_B2 * v + (1.0 - ADAM_B2) * (g * g)
    m_hat = m_new / (1.0 - ADAM_B1 ** ADAM_STEP)
    v_hat = v_new / (1.0 - ADAM_B2 ** ADAM_STEP)
    return -ADAM_LR * (m_hat / (jnp.sqrt(v_hat) + ADAM_EPS) + ADAM_WD * w), m_new, v_new


def adamw_small(g_alls, ws, ms, vs, loss_all):
    n = len(ws)

    def body(*refs):
        g_refs, w_refs, m_refs, v_refs = (refs[i * n:(i + 1) * n] for i in range(4))
        loss_ref = refs[4 * n]
        outs = refs[4 * n + 1:]
        for i in range(n):
            g = _slot_sum(g_refs[i])
            outs[i][...] = g
            outs[n + i][...], outs[2 * n + i][...], outs[3 * n + i][...] = _adamw_math(
                g, w_refs[i][...], m_refs[i][...], v_refs[i][...])
        outs[4 * n][...] = _slot_sum(loss_ref)

    sds = [jax.ShapeDtypeStruct(w.shape, F32) for w in ws]
    res = pl.pallas_call(
        body, name="adamw_small", out_shape=tuple(sds * 4) + (jax.ShapeDtypeStruct(loss_all.shape[1:], F32),),
        compiler_params=pltpu.CompilerParams(vmem_limit_bytes=VMEM_LIMIT),
    )(*g_alls, *ws, *ms, *vs, loss_all)
    return [res[i * n:(i + 1) * n] for i in range(4)], res[4 * n]


def adamw_rows(g_slots, w, m, v, name):
    n, r, lanes = g_slots.shape
    fits = [t for t in range(16, r + 1, 16) if r % t == 0 and t * lanes <= ADAM_TILE]
    tr = max(fits) if fits else r
    def body(g_ref, w_ref, m_ref, v_ref, go, do, mo, vo):
        g = _slot_sum(g_ref)
        go[...] = g
        do[...], mo[...], vo[...] = _adamw_math(g, w_ref[...], m_ref[...], v_ref[...])

    row = pl.BlockSpec((tr, lanes), lambda i: (i, 0))
    sd = jax.ShapeDtypeStruct((r, lanes), F32)
    return pl.pallas_call(
        body, name=name, grid=(r // tr,), out_shape=(sd, sd, sd, sd),
        in_specs=[pl.BlockSpec((n, tr, lanes), lambda i: (0, i, 0)), row, row, row],
        out_specs=(row, row, row, row),
        compiler_params=_params(("parallel",)),
    )(g_slots, w, m, v)


def _rope_tables(s):
    def cs(pos, dim):
        inv = ROPE_THETA ** (-np.arange(0, dim, 2, dtype=np.float32) / dim)
        ang = pos.astype(np.float32)[:, None] * inv.astype(np.float32)[None, :]
        return np.cos(ang), np.sin(ang)

    rows = s // GRID_W
    row = np.repeat(np.arange(rows), GRID_W)
    col = np.tile(np.arange(GRID_W), rows)
    cr, sr = cs(row, HD // 2)
    cc, sc = cs(col, HD // 2)
    ct, st = cs(np.arange(s), B_ROPE)
    tables = (np.concatenate([cr, cr, cc, cc] * 2, axis=-1), np.concatenate([-sr, sr, -sc, sc] * 2, axis=-1),
              np.concatenate([ct, ct] * 4, axis=-1), np.concatenate([-st, st] * 4, axis=-1))
    return tuple(jnp.asarray(t, F32) for t in tables)


def _even_rows_to_kernel(wt):
    return jnp.concatenate([wt[:1664], wt[1696:], wt[1664:1696]], axis=0)


def _uq_rows_to_kernel(wt):
    r = wt.reshape(B_HEADS, B_NOPE + B_ROPE, -1)
    return jnp.concatenate([r[:, :B_NOPE].reshape(B_HEADS * B_NOPE, -1), r[:, B_NOPE:].reshape(B_HEADS * B_ROPE, -1)])


def _uq_rows_to_reference(wt):
    nope = wt[:B_HEADS * B_NOPE].reshape(B_HEADS, B_NOPE, -1)
    rope = wt[B_HEADS * B_NOPE:].reshape(B_HEADS, B_ROPE, -1)
    return jnp.concatenate([nope, rope], axis=1).reshape(B_HEADS * (B_NOPE + B_ROPE), -1)


def _shard_t(w):
    return jnp.transpose(w[0])


def _unshard_t(wt, like):
    return jnp.transpose(wt)[None].reshape(like.shape)


def kernel(x, c, norm_w, ada_w, ada_b, even_w_in, a_q_norm, a_k_norm, b_q_lora_norm, b_kv_lora_norm, b_w_uq, b_w_uk, b_w_uv, even_w_out, odd_w_in, c_sink, odd_w_out, final_norm, loss_target, m_norm_w, m_ada_w, m_ada_b, m_even_w_in, m_a_q_norm, m_a_k_norm, m_b_q_lora_norm, m_b_kv_lora_norm, m_b_w_uq, m_b_w_uk, m_b_w_uv, m_even_w_out, m_odd_w_in, m_c_sink, m_odd_w_out, m_final_norm, v_norm_w, v_ada_w, v_ada_b, v_even_w_in, v_a_q_norm, v_a_k_norm, v_b_q_lora_norm, v_b_kv_lora_norm, v_b_w_uq, v_b_w_uk, v_b_w_uv, v_even_w_out, v_odd_w_in, v_c_sink, v_odd_w_out, v_final_norm):
    s, d = x.shape[1], x.shape[2]
    x0 = x[0]
    target = loss_target[0]
    me_flat = 4 * lax.axis_index("x") + 2 * lax.axis_index("y") + lax.axis_index("c")

    wcols = ada_w.shape[2]
    bias_cols = lax.dynamic_slice_in_dim(ada_b.reshape(2, N_DEV, wcols), me_flat, 1, axis=1)
    call, modp, (g_in_e, g_uq) = ada_forward(
        jnp.broadcast_to(c, (8, d)), ada_w, bias_cols,
        Gather([_shard_t(even_w_in).astype(MXU), _shard_t(b_w_uq).astype(MXU)]))
    wt_in_e = _even_rows_to_kernel(g_in_e.reshape(-1, d))
    wt_uq = _uq_rows_to_kernel(g_uq.reshape(-1, B_Q_LORA))
    later_exchange = Exchange([_shard_t(odd_w_in).astype(MXU), even_w_out[0].astype(MXU),
                               odd_w_out[0].astype(MXU)], scatter=False)
    uk_bd = (jnp.eye(B_HEADS, dtype=F32)[:, None, :, None] * jnp.transpose(b_w_uk[0], (1, 2, 0))[:, :, None, :]
             ).reshape(B_HEADS * B_NOPE, B_HEADS * B_KV_LORA).astype(MXU)
    head_bd = jnp.asarray(np.kron(np.eye(A_HEADS), np.ones((HD, HD))), MXU)
    gq_full, gk_full = jnp.tile(a_q_norm, (1, A_HEADS)), jnp.tile(a_k_norm, (1, A_KV))
    w_uv = jnp.transpose(b_w_uv[0], (1, 0, 2)).astype(MXU)
    uv_bd = (jnp.eye(B_HEADS, dtype=MXU)[:, None, :, None] * w_uv[:, :, None, :]
             ).reshape(B_HEADS * B_KV_LORA, B_HEADS * HD)

    c_all = call[:, 0, :]
    mod = jnp.transpose(modp[:, :, 0, :], (1, 0, 2)).reshape(2, 3, d)
    mod_e, mod_o = mod[0], mod[1]
    nw_e, nw_o = norm_w[0:1], norm_w[1:2]

    cos_a, sin_a, cos_t, sin_t = _rope_tables(s)
    slopes = (2.0 ** (-8.0 * jnp.arange(1, C_HEADS + 1, dtype=F32) / C_HEADS)).reshape(C_HEADS, 1, 1)
    sink2 = c_sink.reshape(C_HEADS, 1, 1) * LOG2E

    (qa, ka, va, qb, kb, kat, vat, kbt, qa_raw, ka_raw, cq_raw, ckv_raw, ga, gb) = even_in_forward(
        x0, mod_e, nw_e, wt_in_e, gq_full, gk_full, b_q_lora_norm, b_kv_lora_norm, wt_uq, uk_bd, head_bd,
        cos_a, sin_a, cos_t, sin_t)
    tk_dense = min(512, s)
    tq_dense = min(256, s)
    fwd_sub = min(8, s // tk_dense)
    bwd_sub_a = min(16, s // tq_dense)
    bwd_sub_b = min(8, s // tq_dense)
    oa, lse_a, g_in_o, g_out_e, g_out_o = flash_forward(
        qa, ka, vat, dv=HD, tq=tq_dense, tk=tk_dense, nsub=fwd_sub, name="attn_a_fwd",
        exchange=later_exchange)
    wt_in_o = g_in_o.reshape(-1, d)
    w_out_e = g_out_e.reshape(-1, d)
    w_out_o = g_out_o.reshape(-1, d)
    o_lat, lse_b = flash_forward(qb, kb, kbt, dv=B_KV_LORA, tq=min(128, s), tk=tk_dense, nsub=fwd_sub,
                                 name="attn_b_fwd")
    x1, y_e, ob = mixer_out_forward(x0, mod_e, [(oa, ga), (o_lat, gb)], w_out_e, "even_out_fwd", latent=uv_bd)

    qc, kc, vc, kct, vct, gc = odd_in_forward(x1, mod_o, nw_o, wt_in_o)
    win_sub = min(8, s // WINDOW)
    oc, lse_c = window_forward(qc, kc, vct, sink2, slopes, win_sub, "attn_c_fwd")
    dx2, y_o, loss_lanes, d_final = mixer_out_forward(x1, mod_o, [(oc, gc)], w_out_o, "odd_out_fwd_loss",
                                                      loss=(target, final_norm.reshape(1, d)))

    loss_part = (0.5 / d) * jnp.sum(loss_lanes)

    doc, dgc, delta_c, dgate_o, dw_out_o, dsink = mixer_out_backward(
        dx2, y_o, mod_o, [(oc, gc)], w_out_o, [C_HEADS], "odd_out_bwd", lse=lse_c.reshape(C_HEADS, s),
        sink=sink2.reshape(C_HEADS, 1))
    rows3 = lambda t: t.reshape(t.shape[0], 1, s)
    dqc, dkc, dvc = window_backward(qc, kc, kct, vc, doc, lse_c, rows3(delta_c), slopes, win_sub, "attn_c_bwd")
    dx1, dvec_o, dwt_in_o = in_proj_backward(
        x1, mod_o, nw_o, [(dqc, O_Q), (dkc, O_K), (dvc, O_V), (dgc, O_G)], "odd_in_bwd",
        dx_out=dx2, w_in_t=wt_in_o, dw_rows=[O_Q, O_K, O_V, O_G])

    doa, dga, dob, dgb, delta_a, dgate_e, dw_out_e = mixer_out_backward(
        dx1, y_e, mod_e, [(oa, ga), (ob, gb)], w_out_e, [A_HEADS, 0], "even_out_bwd")
    d_olat, delta_b, dw_uv = latent_out_backward(dob, o_lat, w_uv)
    blocks = lambda g: g.astype(MXU).reshape(N_DEV, g.shape[0] // N_DEV, g.shape[1])
    even_pieces = lambda: [(pqa, E_QA), (pka, E_KA), (pva, E_VA), (dga, E_GA), (pcq, E_CQ), (pckv, E_CKV),
                           (dgb, E_GB), (pkr, E_KR)]
    scatter_odd = Exchange([dwt_in_o, dw_out_o], True)
    scatter_out_e = Exchange([dw_out_e], True)
    dqb, dkb, dvb, l_in_o, l_out_o = flash_backward(
        qb, kb, kbt, None, d_olat, lse_b, rows3(delta_b), scale=SCALE_B, dv=B_KV_LORA,
        tq=tq_dense, tk=tk_dense, nsub=bwd_sub_b, gq=2, name="attn_b_bwd", split=B_KV_LORA, exchange=scatter_odd)
    dqa, dka, dva, l_out_e = flash_backward(
        qa, ka, kat, va, doa, lse_a, rows3(delta_a), scale=SCALE_A, dv=HD,
        tq=tq_dense, tk=tk_dense, nsub=bwd_sub_a, gq=A_KV, name="attn_a_bwd", exchange=scatter_out_e)
    (pqa, pka, pva, pcq, pckv, pkr, g_qn, g_kn, g_qln, g_kvln, dwt_uq, dw_uk) = even_prep_backward(
        dqa, dka, dva, dqb, dkb, dvb, qa_raw, ka_raw, cq_raw, ckv_raw,
        gq_full, gk_full, b_q_lora_norm, b_kv_lora_norm, wt_uq, uk_bd, head_bd, cos_a, sin_a, cos_t, sin_t)
    g_qn = jnp.sum(g_qn.reshape(A_HEADS, HD), axis=0)
    g_kn = jnp.sum(g_kn.reshape(A_KV, HD), axis=0)
    dwt_in_e, l_uk, l_uv = in_proj_backward(
        x0, mod_e, nw_e, even_pieces(), "even_in_bwd_dw",
        dw_rows=[E_QA, E_KA, E_VA, E_GA, E_CQ, E_CKV, (1696, 2208), (1664, 1696)],
        exchange=Exchange([dw_uk.astype(MXU), dw_uv.astype(MXU)], scatter=False))
    dx0, dvec_e, l_in_e, l_uq = in_proj_backward(
        x0, mod_e, nw_e, even_pieces(), "even_in_bwd_dx", dx_out=dx1, w_in_t=wt_in_e,
        exchange=Exchange([dwt_in_e, blocks(_uq_rows_to_reference(dwt_uq))], True))

    dmod = jnp.stack([jnp.concatenate([dvec_e[0], dvec_e[1], dgate_e[0]]),
                      jnp.concatenate([dvec_o[0], dvec_o[1], dgate_o[0]])])
    d_norm_w = jnp.stack([dvec_e[2], dvec_o[2]])
    small_names = ["norm_w", "ada_b", "a_q_norm", "a_k_norm", "b_q_lora_norm", "b_kv_lora_norm", "b_w_uk", "b_w_uv",
                   "c_sink", "final_norm"]
    small_w = [norm_w, ada_b, a_q_norm, a_k_norm, b_q_lora_norm, b_kv_lora_norm, b_w_uk, b_w_uv, c_sink, final_norm]
    small_m = [m_norm_w, m_ada_b, m_a_q_norm, m_a_k_norm, m_b_q_lora_norm, m_b_kv_lora_norm, m_b_w_uk, m_b_w_uv,
               m_c_sink, m_final_norm]
    small_v = [v_norm_w, v_ada_b, v_a_q_norm, v_a_k_norm, v_b_q_lora_norm, v_b_kv_lora_norm, v_b_w_uk, v_b_w_uv,
               v_c_sink, v_final_norm]
    small_g = [d_norm_w, dmod, g_qn, g_kn, g_qln, g_kvln, None, None, dsink, d_final]
    flat2 = lambda a: a.reshape((1, -1)) if a.size == a.shape[-1] else a.reshape(a.shape[-3:] if a.ndim > 3 else a.shape)
    kshape = [flat2(w).shape for w in small_w]
    late = [i for i, g in enumerate(small_g) if g is not None]
    gathered = all_gather_slots(
        Gather([small_g[i].reshape(kshape[i]) for i in late] + [jnp.full((8, 128), loss_part, F32)]),
        "gather_small_grads")
    g_all = [None] * len(small_g)
    for i, g in zip(late, gathered):
        g_all[i] = g
    g_all[6], g_all[7] = (l.reshape((N_DEV,) + kshape[6]) for l in (l_uk, l_uv))
    sm_out, loss_sum = adamw_small(g_all, [flat2(a) for a in small_w], [flat2(a) for a in small_m],
                                   [flat2(a) for a in small_v], gathered[-1])
    loss = loss_sum[0, 0]
    sm = [{nm: p.reshape(w.shape) for nm, w, p in zip(small_names, small_w, outs)} for outs in sm_out]

    dmod_all = g_all[1].reshape(N_DEV, 2, N_DEV, wcols)
    dmod_cols = lax.dynamic_slice_in_dim(dmod_all, me_flat, 1, axis=2)[:, :, 0, :]
    pad16 = lambda a: jnp.concatenate([a, jnp.zeros_like(a)], axis=0)
    g_ada_w = ada_weight_grad(pad16(c_all), jnp.transpose(pad16(dmod_cols), (1, 0, 2)))
    rows_of = lambda a: a.reshape(-1, wcols)
    ada = adamw_rows(rows_of(g_ada_w)[None], rows_of(ada_w), rows_of(m_ada_w), rows_of(v_ada_w), "adamw_ada_w")
    ada = [p.reshape(ada_w.shape) for p in ada]

    bg = [{}, {}, {}, {}]
    for nm, landed, w, m, v, transposed in (
            ("even_w_in", l_in_e, even_w_in, m_even_w_in, v_even_w_in, True),
            ("b_w_uq", l_uq, b_w_uq, m_b_w_uq, v_b_w_uq, True),
            ("odd_w_in", l_in_o, odd_w_in, m_odd_w_in, v_odd_w_in, True),
            ("even_w_out", l_out_e, even_w_out, m_even_w_out, v_even_w_out, False),
            ("odd_w_out", l_out_o, odd_w_out, m_odd_w_out, v_odd_w_out, False)):
        view = _shard_t if transposed else (lambda a: a[0])
        res = adamw_rows(landed, view(w), view(m), view(v), "adamw_" + nm)
        for kind, p in enumerate(res):
            bg[kind][nm] = _unshard_t(p, w) if transposed else p[None]
    big_names = ["even_w_in", "odd_w_in", "even_w_out", "odd_w_out", "b_w_uq"]

    order = ["norm_w", "ada_w", "ada_b", "even_w_in", "a_q_norm", "a_k_norm", "b_q_lora_norm", "b_kv_lora_norm",
             "b_w_uq", "b_w_uk", "b_w_uv", "even_w_out", "odd_w_in", "c_sink", "odd_w_out", "final_norm"]

    def pick(kind):
        out = []
        for nm in order:
            if nm == "ada_w":
                out.append(ada[kind])
            elif nm in big_names:
                out.append(bg[kind][nm])
            else:
                out.append(sm[kind][nm])
        return out

    return (loss, dx0[None], *pick(0), *pick(1), *pick(2), *pick(3))
```

```python
import functools

import jax
import jax.numpy as jnp
import numpy as np
from jax import lax
from jax.experimental import pallas as pl
from jax.experimental.pallas import tpu as pltpu

F32 = jnp.float32
MXU = jnp.bfloat16
EPS = 1e-6
ROPE_THETA = 10000.0
GRID_W = 64
HD = 64
N_DEV = 8

A_HEADS, A_KV = 8, 2
B_HEADS, B_NOPE, B_ROPE, B_Q_LORA, B_KV_LORA = 8, 64, 32, 256, 128
B_QK = B_KV_LORA + B_ROPE
C_HEADS, C_KV = 16, 4
WINDOW = 128

ADAM_LR, ADAM_B1, ADAM_B2, ADAM_EPS, ADAM_WD, ADAM_STEP = 0.001, 0.9, 0.999, 1e-08, 0.01, 10

ROW_TILE = 512
ADAM_TILE = 2048 * 128
LOG2E = 1.4426950408889634
SCALE_A = HD ** -0.5
SCALE_B = (B_NOPE + B_ROPE) ** -0.5
SCALE2_A, SCALE2_B = SCALE_A * LOG2E, SCALE_B * LOG2E
VMEM_LIMIT = 56 * 1024 * 1024

E_QA, E_KA, E_VA, E_GA, E_CQ, E_CKV, E_GB, E_KR = (
    (0, 512), (512, 640), (640, 768), (768, 1280), (1280, 1536), (1536, 1664), (1664, 2176), (2176, 2208))
O_Q, O_K, O_V, O_G = (0, 1024), (1024, 1280), (1280, 1536), (1536, 2560)


def _mm(a, b):
    return jnp.dot(a.astype(MXU), b.astype(MXU), preferred_element_type=F32)


def _mm_nt(a, b):
    return lax.dot_general(a.astype(MXU), b.astype(MXU), (((1,), (1,)), ((), ())), preferred_element_type=F32)


def _mm_tn(a, b):
    return lax.dot_general(a.astype(MXU), b.astype(MXU), (((0,), (0,)), ((), ())), preferred_element_type=F32)


def _group_sums_t(prod, group):
    tm, w = prod.shape
    sel = (lax.broadcasted_iota(jnp.int32, (w, 128), 0) // group
           == lax.broadcasted_iota(jnp.int32, (w, 128), 1)).astype(MXU)
    hi = prod.astype(MXU)
    lo = prod - hi.astype(F32)
    return (_mm(hi, sel) + _mm(lo, sel)).T


def _sigmoid(z):
    return 1.0 / (1.0 + jnp.exp(-z))


def _silu(z):
    return z * _sigmoid(z)


def _rms(x):
    return lax.rsqrt(jnp.mean(x * x, axis=-1, keepdims=True) + EPS)


def _swap_halves(y, group):
    n = y.shape[-1]
    half = group // 2
    fwd = pltpu.roll(y, half, 1)
    if n == group:
        return fwd
    back = pltpu.roll(y, n - half, 1)
    lane = lax.broadcasted_iota(jnp.int32, y.shape, 1)
    return jnp.where((lane % group) < half, back, fwd)


def _rope(y, cos, sin, group):
    return y * cos + _swap_halves(y, group) * sin


def _rope_t(d, cos, sin, group):
    return d * cos - _swap_halves(d, group) * sin


def _rms_bwd(dy, x, g):
    r = _rms(x)
    xhat = x * r
    dxhat = dy * g
    dx = r * (dxhat - xhat * jnp.mean(dxhat * xhat, axis=-1, keepdims=True))
    return dx, dy * xhat


def _group_mean(v, bd, group):
    hi = v.astype(MXU)
    lo = v - hi.astype(F32)
    return (_mm(hi, bd[...]) + _mm(lo, bd[...])) * (1.0 / group)


def _head_norm(x, g, bd, group):
    return x * lax.rsqrt(_group_mean(x * x, bd, group) + EPS) * g


def _head_norm_bwd(dy, x, g, bd, group):
    r = lax.rsqrt(_group_mean(x * x, bd, group) + EPS)
    xhat = x * r
    dxhat = dy * g
    dx = r * (dxhat - xhat * _group_mean(dxhat * xhat, bd, group))
    return dx, dy * xhat


def _params(sem, vmem=VMEM_LIMIT):
    return pltpu.CompilerParams(dimension_semantics=sem, vmem_limit_bytes=vmem)


def _row_spec(tm, w):
    return pl.BlockSpec((tm, w), lambda i: (i, 0))


def _full_spec(shape):
    nd = len(shape)
    return pl.BlockSpec(shape, lambda i: (0,) * nd)


def _head_spec(h, tm, w):
    return pl.BlockSpec((h, tm, w), lambda i: (0, i, 0))


def _headt_spec(h, w, tm):
    return pl.BlockSpec((h, w, tm), lambda i: (0, 0, i))


def _rows_spec(h, tm):
    return pl.BlockSpec((h, tm), lambda i: (0, i))


def _me():
    return lax.axis_index("x"), lax.axis_index("y"), lax.axis_index("c")


def _flat(p):
    return 4 * p[0] + 2 * p[1] + p[2]


def _peer(me, k):
    x, y, c = me
    return (1 - x if k & 4 else x, 1 - y if k & 2 else y, 1 - c if k & 1 else c)


MESH_ID = pl.DeviceIdType.MESH


class Gather:
    VMEM = pl.BlockSpec(memory_space=pltpu.VMEM)

    def __init__(self, shards):
        self.shards = list(shards)
        self.n = len(self.shards)
        self.out_shapes = tuple(jax.ShapeDtypeStruct((N_DEV,) + a.shape, a.dtype) for a in self.shards)
        self.in_specs = [Gather.VMEM] * self.n
        self.out_specs = (Gather.VMEM,) * self.n
        self.sems = [pltpu.SemaphoreType.DMA((7 * self.n,)), pltpu.SemaphoreType.DMA((7 * self.n,)),
                     pltpu.SemaphoreType.DMA((self.n,))]

    def _plan(self, x_refs, out_refs, sems):
        send_sems, recv_sems, local_sems = sems
        me = _me()
        x, y, c = me
        chips = [(1 - x, y), (x, 1 - y), (1 - x, 1 - y)]

        def copy(a, k, block, to, src=None):
            slot = out_refs[a].at[_flat(block)]
            return pltpu.make_async_remote_copy(
                src_ref=slot if src is None else src, dst_ref=slot, send_sem=send_sems.at[7 * a + k],
                recv_sem=recv_sems.at[7 * a + k], device_id=to, device_id_type=MESH_ID)

        mine = [pltpu.make_async_copy(x_refs[a], out_refs[a].at[_flat(me)], local_sems.at[a]) for a in range(self.n)]
        first = [copy(a, 0, me, (x, y, 1 - c), src=x_refs[a]) for a in range(self.n)]
        first += [copy(a, 1 + j, me, (*chip, c), src=x_refs[a]) for a in range(self.n) for j, chip in enumerate(chips)]
        return me, chips, copy, mine, first

    def start(self, x_refs, out_refs, sems):
        _, _, _, mine, first = self._plan(x_refs, out_refs, sems)
        for cp in mine + first:
            cp.start()

    def forward(self, x_refs, out_refs, sems):
        me, chips, copy, _, _ = self._plan(x_refs, out_refs, sems)
        x, y, c = me
        for a in range(self.n):
            for j, chip in enumerate(chips):
                copy(a, 1 + j, (*chip, c), me).wait_recv()
                copy(a, 4 + j, (*chip, c), (x, y, 1 - c)).start()

    def drain(self, x_refs, out_refs, sems):
        me, chips, copy, mine, first = self._plan(x_refs, out_refs, sems)
        x, y, c = me
        sibling = (x, y, 1 - c)
        for a in range(self.n):
            copy(a, 0, sibling, me).wait_recv()
            for j, chip in enumerate(chips):
                copy(a, 4 + j, (*chip, 1 - c), me).wait_recv()
        for cp in first + [copy(a, 4 + j, (*chip, c), sibling) for a in range(self.n) for j, chip in enumerate(chips)]:
            cp.wait_send()
        for cp in mine:
            cp.wait()

    def finish(self, x_refs, out_refs, sems):
        self.forward(x_refs, out_refs, sems)
        self.drain(x_refs, out_refs, sems)


def all_gather_slots(gather, name):
    def body(*refs):
        x_refs, out_refs, sems = refs[:gather.n], refs[gather.n:2 * gather.n], refs[2 * gather.n:]
        gather.start(x_refs, out_refs, sems)
        gather.finish(x_refs, out_refs, sems)

    return pl.pallas_call(
        body, name=name, out_shape=gather.out_shapes, in_specs=gather.in_specs, out_specs=gather.out_specs,
        scratch_shapes=list(gather.sems), compiler_params=pltpu.CompilerParams(vmem_limit_bytes=VMEM_LIMIT),
    )(*gather.shards)


class Exchange:
    HBM = pl.BlockSpec(memory_space=pl.ANY)

    def __init__(self, srcs, scatter):
        self.srcs = list(srcs)
        self.scatter = scatter
        self.n = len(self.srcs)
        self.land_shapes = tuple(jax.ShapeDtypeStruct((N_DEV,) + tuple(a.shape[-2:]), a.dtype) for a in self.srcs)
        self.in_specs = [Exchange.HBM] * self.n
        self.out_specs = (Exchange.HBM,) * self.n
        self.sems = [pltpu.SemaphoreType.DMA((N_DEV - 1,)), pltpu.SemaphoreType.DMA((N_DEV - 1,)),
                     pltpu.SemaphoreType.DMA] * self.n

    def _copies(self, src_refs, land_refs, sems):
        me = _me()
        mi = _flat(me)
        local, sends, recvs = [], [], []
        for a, (src_ref, land_ref) in enumerate(zip(src_refs, land_refs)):
            send_sems, recv_sems, local_sem = sems[3 * a:3 * a + 3]
            pick = (lambda p, r=src_ref: r.at[_flat(p)]) if self.scatter else (lambda p, r=src_ref: r)
            local.append(pltpu.make_async_copy(pick(me), land_ref.at[mi], local_sem))
            for k in range(1, N_DEV):
                peer = _peer(me, k)
                pair = dict(send_sem=send_sems.at[k - 1], recv_sem=recv_sems.at[k - 1], device_id=peer,
                            device_id_type=MESH_ID)
                sends.append(pltpu.make_async_remote_copy(src_ref=pick(peer), dst_ref=land_ref.at[mi], **pair))
                recvs.append(pltpu.make_async_remote_copy(src_ref=pick(peer), dst_ref=land_ref.at[_flat(peer)],
                                                          **pair))
        return local, sends, recvs

    def start(self, src_refs, land_refs, sems):
        local, sends, _ = self._copies(src_refs, land_refs, sems)
        for cp in local + sends:
            cp.start()

    def wait(self, src_refs, land_refs, sems):
        local, sends, recvs = self._copies(src_refs, land_refs, sems)
        for cp in recvs:
            cp.wait_recv()
        for cp in sends:
            cp.wait_send()
        for cp in local:
            cp.wait()


def ada_forward(c8, ada_w, bias_cols, gather):
    d = c8.shape[1]
    w = ada_w.shape[2]
    ng = gather.n

    def body(*refs):
        c_ref, w_ref, b_ref = refs[:3]
        gx_refs = refs[3:3 + ng]
        call_ref, modp_ref = refs[3 + ng:5 + ng]
        gout_refs = refs[5 + ng:5 + 2 * ng]
        part_ref, s1, r1, s2, r2 = refs[5 + 2 * ng:10 + 2 * ng]
        g_sems = refs[10 + 2 * ng:]
        me = _me()
        mi = _flat(me)
        call_ref[mi] = c_ref[...]
        rows_out = []
        for k in range(1, N_DEV):
            rows_out.append(pltpu.make_async_remote_copy(
                src_ref=c_ref, dst_ref=call_ref.at[mi], send_sem=s1.at[k - 1], recv_sem=r1.at[k - 1],
                device_id=_peer(me, k), device_id_type=MESH_ID))
        for cp in rows_out:
            cp.start()
        gather.start(gx_refs, gout_refs, g_sems)
        for k in range(1, N_DEV):
            pltpu.make_async_remote_copy(
                src_ref=c_ref, dst_ref=call_ref.at[_flat(_peer(me, k))], send_sem=s1.at[k - 1],
                recv_sem=r1.at[k - 1], device_id=_peer(me, k), device_id_type=MESH_ID).wait_recv()
        ca = _silu(call_ref[...].reshape(N_DEV * 8, d))
        for l in range(2):
            part = _mm(ca, w_ref[l]) + b_ref[l]
            for b in range(N_DEV):
                part_ref[b, l] = part[8 * b:8 * b + 8, :]
        modp_ref[mi] = part_ref[mi]
        spread = []
        for k in range(1, N_DEV):
            peer = _peer(me, k)
            spread.append(pltpu.make_async_remote_copy(
                src_ref=part_ref.at[_flat(peer)], dst_ref=modp_ref.at[mi], send_sem=s2.at[k - 1],
                recv_sem=r2.at[k - 1], device_id=peer, device_id_type=MESH_ID))
        for cp in spread:
            cp.start()
        gather.forward(gx_refs, gout_refs, g_sems)
        for k in range(1, N_DEV):
            pi = _flat(_peer(me, k))
            pltpu.make_async_remote_copy(
                src_ref=part_ref.at[pi], dst_ref=modp_ref.at[pi], send_sem=s2.at[k - 1],
                recv_sem=r2.at[k - 1], device_id=_peer(me, k), device_id_type=MESH_ID).wait_recv()
        for cp in rows_out + spread:
            cp.wait_send()
        gather.drain(gx_refs, gout_refs, g_sems)

    vm = pl.BlockSpec(memory_space=pltpu.VMEM)
    res = pl.pallas_call(
        body, name="ada_forward",
        out_shape=(jax.ShapeDtypeStruct((N_DEV, 8, d), F32), jax.ShapeDtypeStruct((N_DEV, 2, 8, w), F32))
        + gather.out_shapes,
        in_specs=[vm, vm, vm] + gather.in_specs, out_specs=(vm, vm) + gather.out_specs,
        scratch_shapes=[pltpu.VMEM((N_DEV, 2, 8, w), F32)] + [pltpu.SemaphoreType.DMA((7,))] * 4 + list(gather.sems),
        compiler_params=pltpu.CompilerParams(vmem_limit_bytes=VMEM_LIMIT),
    )(c8, ada_w, bias_cols, *gather.shards)
    return res[0], res[1], res[2:]


def _modulated(x, mod_ref, nw_ref):
    xn = x * _rms(x)
    g1 = nw_ref[...] * (1.0 + mod_ref[1:2, :])
    return xn, g1, xn * g1 + mod_ref[0:1, :]


def even_in_forward(x, mod, nw, w_in_t, gq, gk, qln, kvln, w_uq_t, uk_bd, bd, cos_a, sin_a, cos_t, sin_t):
    s, d = x.shape
    tm = min(ROW_TILE, s)
    n_nope = B_HEADS * B_NOPE

    def body(x_ref, mod_ref, nw_ref, w_ref, gq_ref, gk_ref, qln_ref, kvln_ref, uq_ref, ukbd_ref, bd_ref,
             ca_ref, sa_ref, ct_ref, st_ref,
             qa_o, ka_o, va_o, qb_o, kb_o, kat_o, vat_o, kbt_o, qa_raw_o, ka_raw_o, cq_raw_o, ckv_raw_o, ga_o, gb_o):
        _, _, h = _modulated(x_ref[...], mod_ref, nw_ref)
        h = h.astype(MXU)

        def proj(cols):
            return _mm_nt(h, w_ref[cols[0]:cols[1], :])

        ca, sa, ct, st = ca_ref[...], sa_ref[...], ct_ref[...], st_ref[...]
        wide = lambda t, n: jnp.concatenate([t] * n, axis=1)
        qa = proj(E_QA)
        qa_raw_o[...] = qa
        qr = _rope(_head_norm(qa, gq_ref[...], bd_ref, HD), wide(ca, 4), wide(sa, 4), 32) * SCALE2_A
        for hh in range(A_HEADS):
            qa_o[hh] = qr[:, HD * hh:HD * hh + HD].astype(MXU)
        ka = proj(E_KA)
        ka_raw_o[...] = ka
        kr = _rope(_head_norm(ka, gk_ref[...], bd_ref[0:128, 0:128], HD), ca, sa, 32)
        va = proj(E_VA)
        krt, vat = kr.T, va.T
        for g in range(A_KV):
            ka_o[g] = kr[:, HD * g:HD * g + HD].astype(MXU)
            va_o[g] = va[:, HD * g:HD * g + HD].astype(MXU)
            kat_o[g] = krt[HD * g:HD * g + HD, :].astype(MXU)
            vat_o[g] = vat[HD * g:HD * g + HD, :].astype(MXU)
        ga_o[...] = proj(E_GA).astype(MXU)
        gb_o[...] = proj(E_GB).astype(MXU)
        cq = proj(E_CQ)
        cq_raw_o[...] = cq
        qb = _mm_nt(cq * _rms(cq) * qln_ref[...], uq_ref[...])
        q_lat = _mm(qb[:, 0:n_nope], ukbd_ref[...]) * SCALE2_B
        q_rope = _rope(qb[:, n_nope:], wide(ct, 2), wide(st, 2), 32) * SCALE2_B
        for hh in range(B_HEADS):
            qb_o[hh, :, 0:B_KV_LORA] = q_lat[:, B_KV_LORA * hh:B_KV_LORA * (hh + 1)].astype(MXU)
            qb_o[hh, :, B_KV_LORA:B_QK] = q_rope[:, B_ROPE * hh:B_ROPE * (hh + 1)].astype(MXU)
        ckv = proj(E_CKV)
        ckv_raw_o[...] = ckv
        ckv_n = ckv * _rms(ckv) * kvln_ref[...]
        k_rope = _rope(proj(E_KR), ct[:, 0:B_ROPE], st[:, 0:B_ROPE], 32)
        kb_o[0, :, 0:B_KV_LORA] = ckv_n.astype(MXU)
        kb_o[0, :, B_KV_LORA:B_QK] = k_rope.astype(MXU)
        kbt_o[0, 0:B_KV_LORA, :] = ckv_n.T.astype(MXU)
        kbt_o[0, B_KV_LORA:B_QK, :] = k_rope.T.astype(MXU)

    sd = jax.ShapeDtypeStruct
    outs = (sd((A_HEADS, s, HD), MXU), sd((A_KV, s, HD), MXU), sd((A_KV, s, HD), MXU),
            sd((B_HEADS, s, B_QK), MXU), sd((1, s, B_QK), MXU),
            sd((A_KV, HD, s), MXU), sd((A_KV, HD, s), MXU), sd((1, B_QK, s), MXU),
            sd((s, 512), F32), sd((s, 128), F32), sd((s, B_Q_LORA), F32), sd((s, B_KV_LORA), F32),
            sd((s, 512), MXU), sd((s, 512), MXU))
    out_specs = (_head_spec(A_HEADS, tm, HD), _head_spec(A_KV, tm, HD), _head_spec(A_KV, tm, HD),
                 _head_spec(B_HEADS, tm, B_QK), _head_spec(1, tm, B_QK),
                 _headt_spec(A_KV, HD, tm), _headt_spec(A_KV, HD, tm), _headt_spec(1, B_QK, tm),
                 _row_spec(tm, 512), _row_spec(tm, 128), _row_spec(tm, B_Q_LORA), _row_spec(tm, B_KV_LORA),
                 _row_spec(tm, 512), _row_spec(tm, 512))
    consts = [mod, nw, w_in_t, gq, gk, qln, kvln, w_uq_t, uk_bd, bd]
    return pl.pallas_call(
        body, name="even_in_forward", grid=(s // tm,), out_shape=outs,
        in_specs=[_row_spec(tm, d)] + [_full_spec(a.shape) for a in consts] + [_row_spec(tm, 128)] * 4,
        out_specs=out_specs, compiler_params=_params(("parallel",)),
    )(x, *consts, cos_a, sin_a, cos_t, sin_t)


def odd_in_forward(x, mod, nw, w_in):
    s, d = x.shape
    tm = min(ROW_TILE, s)

    def body(x_ref, mod_ref, nw_ref, w_ref, q_o, k_o, v_o, kt_o, vt_o, g_o):
        _, _, h = _modulated(x_ref[...], mod_ref, nw_ref)
        h = h.astype(MXU)

        def proj(cols):
            return _mm_nt(h, w_ref[cols[0]:cols[1], :])

        q = proj(O_Q) * SCALE2_A
        for hh in range(C_HEADS):
            q_o[hh] = q[:, HD * hh:HD * hh + HD].astype(MXU)
        k = proj(O_K)
        v = proj(O_V)
        for g in range(C_KV):
            kh = k[:, HD * g:HD * g + HD]
            vh = v[:, HD * g:HD * g + HD]
            k_o[g] = kh.astype(MXU)
            v_o[g] = vh.astype(MXU)
            kt_o[g] = kh.T.astype(MXU)
            vt_o[g] = vh.T.astype(MXU)
        g_o[...] = proj(O_G).astype(MXU)

    sd = jax.ShapeDtypeStruct
    return pl.pallas_call(
        body, name="odd_in_forward", grid=(s // tm,),
        out_shape=(sd((C_HEADS, s, HD), MXU), sd((C_KV, s, HD), MXU), sd((C_KV, s, HD), MXU),
                   sd((C_KV, HD, s), MXU), sd((C_KV, HD, s), MXU), sd((s, 1024), MXU)),
        in_specs=[_row_spec(tm, d), _full_spec(mod.shape), _full_spec(nw.shape), _full_spec(w_in.shape)],
        out_specs=(_head_spec(C_HEADS, tm, HD), _head_spec(C_KV, tm, HD), _head_spec(C_KV, tm, HD),
                   _headt_spec(C_KV, HD, tm), _headt_spec(C_KV, HD, tm), _row_spec(tm, 1024)),
        compiler_params=_params(("parallel",)),
    )(x, mod, nw, w_in)


def latent_out_forward(o_lat, w_uv):
    s = o_lat.shape[0]
    tm = min(ROW_TILE, s)

    def body(o_ref, uv_ref, out_ref):
        for hh in range(B_HEADS):
            out_ref[:, HD * hh:HD * hh + HD] = _mm(o_ref[:, B_KV_LORA * hh:B_KV_LORA * (hh + 1)],
                                                   uv_ref[hh]).astype(MXU)

    return pl.pallas_call(
        body, name="latent_out_forward", grid=(s // tm,),
        out_shape=jax.ShapeDtypeStruct((s, B_HEADS * HD), MXU),
        in_specs=[_row_spec(tm, o_lat.shape[1]), _full_spec(w_uv.shape)],
        out_specs=_row_spec(tm, B_HEADS * HD),
        compiler_params=_params(("parallel",)),
    )(o_lat, w_uv)


def mixer_out_forward(x, mod, pairs, w_out, name, loss=None):
    s, d = x.shape
    tm = min(ROW_TILE, s)
    n = len(pairs)
    widths = [o.shape[1] for o, _ in pairs]
    head = loss is not None

    def body(*refs):
        x_ref, mod_ref, w_ref = refs[:3]
        pr = refs[3:3 + 2 * n]
        rest = refs[3 + 2 * n:]
        y = jnp.zeros((tm, d), F32)
        r0 = 0
        for i in range(n):
            mix = pr[2 * i][...].astype(F32) * _silu(pr[2 * i + 1][...].astype(F32))
            y = y + _mm(mix, w_ref[r0:r0 + widths[i], :])
            r0 += widths[i]
        x_out = x_ref[...] + mod_ref[2:3, :] * y
        if not head:
            xo_ref, y_ref = rest
            xo_ref[...] = x_out
        else:
            t_ref, fn_ref, dx_ref, y_ref, lp_ref, dw_ref = rest

            @pl.when(pl.program_id(0) == 0)
            def _():
                lp_ref[...] = jnp.zeros(lp_ref.shape, F32)
                dw_ref[...] = jnp.zeros(dw_ref.shape, F32)

            g = fn_ref[...]
            err = x_out * _rms(x_out) * g - t_ref[...]
            lp_ref[...] += jnp.sum(err * err, axis=0, keepdims=True)
            dx, dg = _rms_bwd(err * (1.0 / d), x_out, g)
            dx_ref[...] = dx
            dw_ref[...] += jnp.sum(dg, axis=0, keepdims=True)
        y_ref[...] = y.astype(y_ref.dtype)

    flat = [a for p in pairs for a in p]
    sd = jax.ShapeDtypeStruct
    in_specs = [_row_spec(tm, d), _full_spec(mod.shape), _full_spec(w_out.shape)]
    in_specs += [_row_spec(tm, a.shape[1]) for a in flat]
    out_shape = (sd((s, d), F32), sd((s, d), MXU))
    out_specs = (_row_spec(tm, d), _row_spec(tm, d))
    if head:
        in_specs += [_row_spec(tm, d), _full_spec(loss[1].shape)]
        out_shape += (sd((1, d), F32), sd((1, d), F32))
        out_specs += (_full_spec((1, d)), _full_spec((1, d)))
    return pl.pallas_call(
        body, name=name, grid=(s // tm,), out_shape=out_shape, in_specs=in_specs, out_specs=out_specs,
        compiler_params=_params(("arbitrary",) if head else ("parallel",)),
    )(x, mod, w_out, *flat, *(loss if head else ()))


ONES_ROWS = 16
AHEAD = 2


def _col_max8(s3):
    m8 = jnp.max(s3, axis=0)
    return jnp.broadcast_to(jnp.max(m8, axis=0, keepdims=True), m8.shape)


def _with_ones(vt, n):
    return jnp.concatenate([vt, jnp.ones((ONES_ROWS, n), vt.dtype)], axis=0)


def _grid_edges(grid):
    ids = [pl.program_id(a) for a in range(len(grid))]
    first = functools.reduce(jnp.logical_and, [i == 0 for i in ids])
    last = functools.reduce(jnp.logical_and, [i == n - 1 for i, n in zip(ids, grid)])
    return first, last


def flash_forward(q, k, vt, *, dv, tq, tk, nsub, name, exchange=None):
    hq, s, dq = q.shape
    g_kv = k.shape[0]
    hpg = hq // g_kv
    nq = s // tq
    tkk = tk * nsub
    nk = s // tkk
    grid = (g_kv, nq, nk)
    hosted = exchange is not None
    m_cols = hpg * tq
    dvp = dv + ONES_ROWS

    def body(*refs):
        nx = exchange.n if hosted else 0
        q_ref, k_ref, vt_ref = refs[:3]
        xs_refs = refs[3:3 + nx]
        o_ref, lse_ref = refs[3 + nx:5 + nx]
        land_refs = refs[5 + nx:5 + 2 * nx]
        m_s, acc_s = refs[5 + 2 * nx:7 + 2 * nx]
        sems = refs[7 + 2 * nx:]
        if hosted:
            first, last = _grid_edges(grid)
            pl.when(first)(lambda: exchange.start(xs_refs, land_refs, sems))
        j = pl.program_id(2)

        @pl.when(j == 0)
        def _():
            m_s[...] = jnp.full((8, m_cols), -jnp.inf, F32)
            acc_s[...] = jnp.zeros((dvp, m_cols), F32)

        qq = q_ref[...].reshape(m_cols, dq)
        score = lambda u: _mm_nt(k_ref[0, tk * u:tk * (u + 1), :], qq).reshape(tk // 8, 8, m_cols)
        sts = {u: score(u) for u in range(min(AHEAD, nsub))}
        m_run = m_s[...]
        acc = acc_s[...]
        for u in range(nsub):
            if u + AHEAD < nsub:
                sts[u + AHEAD] = score(u + AHEAD)
            st = sts.pop(u)
            m_new = jnp.maximum(m_run, _col_max8(st))
            p = jnp.exp2(st - m_new[None])
            alpha = jnp.exp2(m_run - m_new)
            pv = _mm(_with_ones(vt_ref[0, 0:dv, tk * u:tk * (u + 1)], tk), p.reshape(tk, m_cols))
            acc = (acc.reshape(dvp // 8, 8, m_cols) * alpha[None]).reshape(dvp, m_cols) + pv
            m_run = m_new
        acc_s[...] = acc
        m_s[...] = m_run

        @pl.when(j == nk - 1)
        def _():
            l = acc_s[dv:dv + 1, :]
            ot = acc_s[0:dv, :] / l
            lse = m_s[0:1, :] + jnp.log2(l)
            for hh in range(hpg):
                o_ref[:, dv * hh:dv * hh + dv] = ot[:, tq * hh:tq * hh + tq].T.astype(MXU)
                lse_ref[hh] = lse[:, tq * hh:tq * hh + tq]

        if hosted:
            pl.when(last)(lambda: exchange.wait(xs_refs, land_refs, sems))

    sd = jax.ShapeDtypeStruct
    return pl.pallas_call(
        body, name=name, grid=grid,
        out_shape=(sd((s, hq * dv), MXU), sd((hq, 1, s), F32)) + (exchange.land_shapes if hosted else ()),
        in_specs=[pl.BlockSpec((hpg, tq, dq), lambda g, i, j: (g, i, 0)),
                  pl.BlockSpec((1, tkk, k.shape[2]), lambda g, i, j: (g, j, 0)),
                  pl.BlockSpec((1, dv, tkk), lambda g, i, j: (g, 0, j))] + (exchange.in_specs if hosted else []),
        out_specs=(pl.BlockSpec((tq, hpg * dv), lambda g, i, j: (i, g)),
                   pl.BlockSpec((hpg, 1, tq), lambda g, i, j: (g, 0, i))) + (exchange.out_specs if hosted else ()),
        scratch_shapes=[pltpu.VMEM((8, m_cols), F32), pltpu.VMEM((dvp, m_cols), F32)]
        + (list(exchange.sems) if hosted else []),
        compiler_params=_params(("arbitrary",) * 3 if hosted else ("parallel", "parallel", "arbitrary")),
    )(q, k, vt, *(exchange.srcs if hosted else []))


def _window_bias_t(hpg, slope_ref):
    t = WINDOW
    r = lax.broadcasted_iota(jnp.int32, (3 * t, t), 0)
    cq = lax.broadcasted_iota(jnp.int32, (3 * t, t), 1)
    arel = jnp.abs(r - t - cq)
    base = jnp.where(arel <= WINDOW, arel.astype(F32) * (-LOG2E), -jnp.inf)
    return jnp.concatenate([base * slope_ref[hh] for hh in range(hpg)], axis=1)


def _window_edges_t(bias, no_before, no_after):
    t = WINDOW
    r = lax.broadcasted_iota(jnp.int32, bias.shape, 0)
    out = ((r < t) & no_before) | ((r >= 2 * t) & no_after)
    return jnp.where(out, -jnp.inf, bias)


def _window_specs(kind, nb, nblk, d):
    t = WINDOW
    before = lambda i: jnp.clip(i * nb - 1, 0, nblk - 1)
    after = lambda i: jnp.clip((i + 1) * nb, 0, nblk - 1)
    if kind == "rows":
        return [pl.BlockSpec((1, t, d), lambda g, i: (g, before(i), 0)),
                pl.BlockSpec((1, nb * t, d), lambda g, i: (g, i, 0)),
                pl.BlockSpec((1, t, d), lambda g, i: (g, after(i), 0))]
    return [pl.BlockSpec((1, d, t), lambda g, i: (g, 0, before(i))),
            pl.BlockSpec((1, d, nb * t), lambda g, i: (g, 0, i)),
            pl.BlockSpec((1, d, t), lambda g, i: (g, 0, after(i)))]


def window_forward(q, k, vt, sink2, slopes, nb, name):
    hq, s, d = q.shape
    g_kv = k.shape[0]
    hpg = hq // g_kv
    t = WINDOW
    nblk = s // t
    steps = nblk // nb
    m_cols = hpg * t

    def body(q_ref, kp, ko, kn, vp, vo, vn, sink_ref, slope_ref, o_ref, lse_ref):
        i = pl.program_id(1)
        kk_all = jnp.concatenate([kp[0], ko[0], kn[0]], axis=0)
        vt_all = jnp.concatenate([vp[0], vo[0], vn[0]], axis=1)
        bias = _window_bias_t(hpg, slope_ref)
        sink_row = jnp.concatenate([jnp.broadcast_to(sink_ref[hh], (8, t)) for hh in range(hpg)], axis=1)
        sts = {}

        def score(u):
            qq = q_ref[:, t * u:t * (u + 1), :].reshape(m_cols, d)
            b_u = bias
            if u == 0 or u == nb - 1:
                b_u = _window_edges_t(bias, (i == 0) if u == 0 else False,
                                      (i == steps - 1) if u == nb - 1 else False)
            sts[u] = _mm_nt(kk_all[t * u:t * (u + 3), :], qq) + b_u

        for u in range(min(AHEAD, nb)):
            score(u)
        for u in range(nb):
            if u + AHEAD < nb:
                score(u + AHEAD)
            s3 = sts.pop(u).reshape(3 * t // 8, 8, m_cols)
            m8 = jnp.maximum(_col_max8(s3), sink_row)
            p = jnp.exp2(s3 - m8[None]).reshape(3 * t, m_cols)
            acc = _mm(_with_ones(vt_all[:, t * u:t * (u + 3)], 3 * t), p)
            l = acc[d:d + 1, :] + jnp.exp2(sink_row[0:1, :] - m8[0:1, :])
            ot = acc[0:d, :] / l
            lse = m8[0:1, :] + jnp.log2(l)
            for hh in range(hpg):
                o_ref[t * u:t * (u + 1), d * hh:d * hh + d] = ot[:, t * hh:t * hh + t].T.astype(MXU)
                lse_ref[hh, :, t * u:t * (u + 1)] = lse[:, t * hh:t * hh + t]

    sd = jax.ShapeDtypeStruct
    return pl.pallas_call(
        body, name=name, grid=(g_kv, steps),
        out_shape=(sd((s, hq * d), MXU), sd((hq, 1, s), F32)),
        in_specs=[pl.BlockSpec((hpg, nb * t, d), lambda g, i: (g, i, 0))]
        + _window_specs("rows", nb, nblk, d) + _window_specs("cols", nb, nblk, d)
        + [pl.BlockSpec((hpg, 1, 1), lambda g, i: (g, 0, 0))] * 2,
        out_specs=(pl.BlockSpec((nb * t, hpg * d), lambda g, i: (i, g)),
                   pl.BlockSpec((hpg, 1, nb * t), lambda g, i: (g, 0, i))),
        compiler_params=_params(("parallel", "parallel")),
    )(q, k, k, k, vt, vt, vt, sink2, slopes)


def window_backward(q, k, kt, v, do, lse, delta, slopes, nb, name):
    hq, s, d = q.shape
    g_kv = k.shape[0]
    hpg = hq // g_kv
    t = WINDOW
    nblk = s // t
    steps = nblk // nb
    m_cols = hpg * t

    def body(q_ref, kp, ko, kn, ktp, kto, ktn, vp, vo, vn, do_ref, lse_ref, dl_ref, slope_ref,
             dq_ref, dk_ref, dv_ref, dk_s, dv_s):
        i = pl.program_id(1)

        @pl.when(i == 0)
        def _():
            dk_ref[...] = jnp.zeros(dk_ref.shape, F32)
            dv_ref[...] = jnp.zeros(dv_ref.shape, F32)

        dk_s[...] = jnp.zeros(dk_s.shape, F32)
        dv_s[...] = jnp.zeros(dv_s.shape, F32)
        kk_all = jnp.concatenate([kp[0], ko[0], kn[0]], axis=0)
        vv_all = jnp.concatenate([vp[0], vo[0], vn[0]], axis=0)
        kkt_all = jnp.concatenate([ktp[0], kto[0], ktn[0]], axis=1)
        bias = _window_bias_t(hpg, slope_ref)
        qqs, dds, sts, dps = {}, {}, {}, {}

        def issue(u):
            rows = slice(t * u, t * (u + 1))
            keys = slice(t * u, t * (u + 3))
            qqs[u] = q_ref[:, rows, :].reshape(m_cols, d)
            dds[u] = jnp.concatenate([do_ref[rows, d * hh:d * hh + d] for hh in range(hpg)], axis=0)
            b_u = bias
            if u == 0 or u == nb - 1:
                b_u = _window_edges_t(bias, (i == 0) if u == 0 else False,
                                      (i == steps - 1) if u == nb - 1 else False)
            sts[u] = _mm_nt(kk_all[keys, :], qqs[u]) + b_u
            dps[u] = _mm_nt(vv_all[keys, :], dds[u])

        for u in range(min(AHEAD, nb)):
            issue(u)
        for u in range(nb):
            if u + AHEAD < nb:
                issue(u + AHEAD)
            rows = slice(t * u, t * (u + 1))
            keys = slice(t * u, t * (u + 3))
            lse_row = jnp.concatenate([lse_ref[hh, :, rows] for hh in range(hpg)], axis=1)
            dl_row = jnp.concatenate([dl_ref[hh, :, rows] for hh in range(hpg)], axis=1)
            p = jnp.exp2(sts[u] - lse_row)
            ds = p * (dps[u] - dl_row) * SCALE_A
            dv_s[keys, :] += _mm(p, dds[u])
            dk_s[keys, :] += _mm(ds, qqs[u])
            dqt = _mm(kkt_all[:, keys], ds)
            for hh in range(hpg):
                dq_ref[rows, d * hh:d * hh + d] = dqt[:, t * hh:t * hh + t].T.astype(dq_ref.dtype)
        tq = nb * t
        for src, r0, n in ((0, jnp.clip(i * nb - 1, 0, nblk - 1) * t, t), (t, i * tq, tq),
                           (t + tq, jnp.clip((i + 1) * nb, 0, nblk - 1) * t, t)):
            dst = pl.ds(pl.multiple_of(r0, t), n)
            dk_ref[0, dst, :] += dk_s[src:src + n, :] * (1.0 / SCALE2_A)
            dv_ref[0, dst, :] += dv_s[src:src + n, :]

    row_map = lambda g, i: (g, 0, i)
    sd = jax.ShapeDtypeStruct
    return pl.pallas_call(
        body, name=name, grid=(g_kv, steps),
        out_shape=(sd((s, hq * d), MXU), sd((g_kv, s, d), F32), sd((g_kv, s, d), F32)),
        in_specs=[pl.BlockSpec((hpg, nb * t, d), lambda g, i: (g, i, 0))]
        + _window_specs("rows", nb, nblk, d) + _window_specs("cols", nb, nblk, d) + _window_specs("rows", nb, nblk, d)
        + [pl.BlockSpec((nb * t, hpg * d), lambda g, i: (i, g)), pl.BlockSpec((hpg, 1, nb * t), row_map),
           pl.BlockSpec((hpg, 1, nb * t), row_map), pl.BlockSpec((hpg, 1, 1), lambda g, i: (g, 0, 0))],
        out_specs=(pl.BlockSpec((nb * t, hpg * d), lambda g, i: (i, g)),
                   pl.BlockSpec((1, s, d), lambda g, i: (g, 0, 0)),
                   pl.BlockSpec((1, s, d), lambda g, i: (g, 0, 0))),
        scratch_shapes=[pltpu.VMEM(((nb + 2) * t, d), F32), pltpu.VMEM(((nb + 2) * t, d), F32)],
        compiler_params=_params(("parallel", "arbitrary")),
    )(q, k, k, k, kt, kt, kt, v, v, v, do, lse, delta, slopes)


def flash_backward(q, k, kt, v, do, lse, delta, *, scale, dv, tq, tk, nsub, gq, name, split=None, exchange=None):
    hq, s, dq = q.shape
    g_kv = k.shape[0]
    hpg = hq // gq
    nq = s // tq
    tqq = tq * nsub
    nqs = s // tqq
    nkb = s // tk
    grid = (gq, nkb, nqs)
    hosted = exchange is not None
    m_cols = hpg * tq
    c = scale * LOG2E
    has_v = v is not None

    def body(*refs):
        it = iter(refs)
        q_ref, k_ref, kt_ref = next(it), next(it), next(it)
        v_ref = next(it) if has_v else None
        do_ref, lse_ref, dl_ref = next(it), next(it), next(it)
        nx = exchange.n if hosted else 0
        xs_refs = [next(it) for _ in range(nx)]
        dq_ref, dk_ref, dv_ref = next(it), next(it), next(it)
        land_refs = [next(it) for _ in range(nx)]
        dqt_s = next(it)
        sems = list(it)
        kj = pl.program_id(1)
        qi = pl.program_id(2)
        if hosted:
            first, last = _grid_edges(grid)
            pl.when(first)(lambda: exchange.start(xs_refs, land_refs, sems))

        @pl.when((kj == 0) & (qi == 0))
        def _():
            dqt_s[...] = jnp.zeros(dqt_s.shape, F32)

        @pl.when(qi == 0)
        def _():
            dk_ref[...] = jnp.zeros(dk_ref.shape, F32)
            dv_ref[...] = jnp.zeros(dv_ref.shape, F32)

        kk = k_ref[0]
        vv = v_ref[0] if has_v else kk[:, :dv]
        qqs, dds, sts, dps = {}, {}, {}, {}

        def issue(u):
            rows = slice(tq * u, tq * (u + 1))
            qqs[u] = q_ref[:, rows, :].reshape(m_cols, dq)
            dds[u] = jnp.concatenate([do_ref[rows, dv * hh:dv * hh + dv] for hh in range(hpg)], axis=0)
            sts[u] = _mm_nt(kk, qqs[u])
            dps[u] = _mm_nt(vv, dds[u])

        for u in range(min(AHEAD, nsub)):
            issue(u)
        dv_acc = dv_ref[0]
        dk_acc = dk_ref[0]
        for u in range(nsub):
            if u + AHEAD < nsub:
                issue(u + AHEAD)
            rows = slice(tq * u, tq * (u + 1))
            lse_row = jnp.concatenate([lse_ref[hh, :, rows] for hh in range(hpg)], axis=1)
            dl_row = jnp.concatenate([dl_ref[hh, :, rows] for hh in range(hpg)], axis=1)
            p = jnp.exp2(sts[u] - lse_row)
            ds = p * (dps[u] - dl_row) * scale
            dv_acc = dv_acc + _mm(p, dds[u])
            dk_acc = dk_acc + _mm(ds, qqs[u])
            dqt = _mm(kt_ref[0], ds)
            for hh in range(hpg):
                dqt_s[qi * nsub + u, dq * hh:dq * hh + dq, :] += dqt[:, tq * hh:tq * hh + tq]
        dv_ref[0] = dv_acc
        dk_ref[0] = jnp.where(qi == nqs - 1, dk_acc * (1.0 / c), dk_acc)

        @pl.when((kj == nkb - 1) & (qi == nqs - 1))
        def _():
            def emit(t, carry):
                r0 = pl.multiple_of(t * tq, tq)
                for hh in range(hpg):
                    blk = dqt_s[t, dq * hh:dq * hh + dq, :].T
                    if split is None:
                        dq_ref[pl.ds(r0, tq), dq * hh:dq * hh + dq] = blk
                    else:
                        rest = dq - split
                        dq_ref[pl.ds(r0, tq), split * hh:split * (hh + 1)] = blk[:, 0:split]
                        dq_ref[pl.ds(r0, tq), hpg * split + rest * hh:hpg * split + rest * (hh + 1)] = blk[:, split:]
                return carry

            lax.fori_loop(0, nq, emit, 0)

        if hosted:
            pl.when(last)(lambda: exchange.wait(xs_refs, land_refs, sems))

    kv_of = lambda g: g * g_kv // gq
    in_specs = [pl.BlockSpec((hpg, tqq, dq), lambda g, kj, qi: (g, qi, 0)),
                pl.BlockSpec((1, tk, dq), lambda g, kj, qi: (kv_of(g), kj, 0)),
                pl.BlockSpec((1, dq, tk), lambda g, kj, qi: (kv_of(g), 0, kj))]
    args = [q, k, kt]
    if has_v:
        in_specs.append(pl.BlockSpec((1, tk, dv), lambda g, kj, qi: (kv_of(g), kj, 0)))
        args.append(v)
    row_map = lambda g, kj, qi: (g, 0, qi)
    in_specs += [pl.BlockSpec((tqq, hpg * dv), lambda g, kj, qi: (qi, g)),
                 pl.BlockSpec((hpg, 1, tqq), row_map), pl.BlockSpec((hpg, 1, tqq), row_map)]
    args += [do, lse, delta]
    if hosted:
        in_specs += exchange.in_specs
        args += exchange.srcs
    sd = jax.ShapeDtypeStruct
    return pl.pallas_call(
        body, name=name, grid=grid,
        out_shape=(sd((s, hq * dq), F32), sd((gq, s, dq), F32), sd((gq, s, dv), F32))
        + (exchange.land_shapes if hosted else ()),
        in_specs=in_specs,
        out_specs=(pl.BlockSpec((s, hpg * dq), lambda g, kj, qi: (0, g)),
                   pl.BlockSpec((1, tk, dq), lambda g, kj, qi: (g, kj, 0)),
                   pl.BlockSpec((1, tk, dv), lambda g, kj, qi: (g, kj, 0))) + (exchange.out_specs if hosted else ()),
        scratch_shapes=[pltpu.VMEM((nq, hpg * dq, tq), F32)] + (list(exchange.sems) if hosted else []),
        compiler_params=_params(("arbitrary",) * 3 if hosted else ("parallel", "arbitrary", "arbitrary")),
    )(*args)


def mixer_out_backward(dx, y, mod, pairs, w_out, delta_heads, name, lse=None, sink=None):
    s, d = dx.shape
    tm = min(ROW_TILE, s)
    n = len(pairs)
    widths = [o.shape[1] for o, _ in pairs]
    n_delta = sum(1 for h in delta_heads if h)
    with_sink = lse is not None

    def body(*refs):
        it = iter(refs)
        dx_ref, y_ref, mod_ref, wt_ref = next(it), next(it), next(it), next(it)
        pr = [next(it) for _ in range(2 * n)]
        lse_ref = next(it) if with_sink else None
        sink_ref = next(it) if with_sink else None
        outs = [next(it) for _ in range(2 * n)]
        dl_refs = [next(it) for _ in range(n_delta)]
        dgate_ref, dw_ref = next(it), next(it)
        dsink_ref = next(it) if with_sink else None
        dw_acc = next(it)

        @pl.when(pl.program_id(0) == 0)
        def _():
            dgate_ref[...] = jnp.zeros(dgate_ref.shape, F32)
            dw_acc[...] = jnp.zeros(dw_acc.shape, F32)
            if with_sink:
                dsink_ref[...] = jnp.zeros(dsink_ref.shape, F32)

        dxo = dx_ref[...]
        dgate_ref[...] += jnp.sum(dxo * y_ref[...].astype(F32), axis=0, keepdims=True)
        dy = (dxo * mod_ref[2:3, :]).astype(MXU)
        dmix = _mm_nt(dy, wt_ref[...])
        r0 = 0
        di = 0
        for i in range(n):
            o = pr[2 * i][...].astype(F32)
            g = pr[2 * i + 1][...].astype(F32)
            dm = dmix[:, r0:r0 + widths[i]]
            sg = _sigmoid(g)
            act = g * sg
            do = dm * act
            outs[2 * i][...] = do.astype(MXU)
            outs[2 * i + 1][...] = (dm * o * (sg * (1.0 + g * (1.0 - sg)))).astype(MXU)
            dw_acc[r0:r0 + widths[i], :] += _mm_tn(o * act, dy)
            if delta_heads[i]:
                dlt = _group_sums_t(do * o, HD)[0:delta_heads[i], :]
                dl_refs[di][...] = dlt
                if with_sink:
                    ps = jnp.exp2(sink_ref[...] - lse_ref[...])
                    dsink_ref[...] += -jnp.sum(ps * dlt, axis=1, keepdims=True)
                di += 1
            r0 += widths[i]

        @pl.when(pl.program_id(0) == pl.num_programs(0) - 1)
        def _():
            for j in range(N_DEV):
                dw_ref[j] = dw_acc[j * dw_block:(j + 1) * dw_block, :].astype(MXU)

    dw_block = sum(widths) // N_DEV
    flat = [a for p in pairs for a in p]
    sd = jax.ShapeDtypeStruct
    in_specs = [_row_spec(tm, d), _row_spec(tm, d), _full_spec(mod.shape), _full_spec(w_out.shape)]
    in_specs += [_row_spec(tm, a.shape[1]) for a in flat]
    args = [dx, y, mod, w_out] + flat
    if with_sink:
        nh = lse.shape[0]
        in_specs += [_rows_spec(nh, tm), _full_spec(sink.shape)]
        args += [lse, sink]
    out_shape = [sd((s, a.shape[1]), MXU) for a in flat]
    out_specs = [_row_spec(tm, a.shape[1]) for a in flat]
    for h in delta_heads:
        if h:
            out_shape.append(sd((h, s), F32))
            out_specs.append(_rows_spec(h, tm))
    out_shape += [sd((1, d), F32), sd((N_DEV, dw_block, d), MXU)]
    out_specs += [_full_spec((1, d)), _full_spec((N_DEV, dw_block, d))]
    if with_sink:
        out_shape.append(sd((lse.shape[0], 1), F32))
        out_specs.append(_full_spec((lse.shape[0], 1)))
    return pl.pallas_call(
        body, name=name, grid=(s // tm,), out_shape=tuple(out_shape), in_specs=in_specs, out_specs=tuple(out_specs),
        scratch_shapes=[pltpu.VMEM((sum(widths), d), F32)], compiler_params=_params(("arbitrary",)),
    )(*args)


def latent_out_backward(d_ob, o_lat, w_uv):
    s = o_lat.shape[0]
    tm = min(ROW_TILE, s)

    def body(d_ref, o_ref, uv_ref, dol_ref, dl_ref, duv_ref, prod_s):
        @pl.when(pl.program_id(0) == 0)
        def _():
            duv_ref[...] = jnp.zeros(duv_ref.shape, F32)

        for hh in range(B_HEADS):
            dh = d_ref[:, HD * hh:HD * hh + HD]
            ol = o_ref[:, B_KV_LORA * hh:B_KV_LORA * (hh + 1)].astype(F32)
            dol = _mm_nt(dh, uv_ref[hh])
            dol_ref[:, B_KV_LORA * hh:B_KV_LORA * (hh + 1)] = dol.astype(MXU)
            prod_s[:, B_KV_LORA * hh:B_KV_LORA * (hh + 1)] = dol * ol
            duv_ref[:, HD * hh:HD * hh + HD] += _mm_tn(ol, dh)
        dl_ref[...] = _group_sums_t(prod_s[...], B_KV_LORA)[0:B_HEADS, :]

    sd = jax.ShapeDtypeStruct
    duv_shape = (B_KV_LORA, B_HEADS * HD)
    return pl.pallas_call(
        body, name="latent_out_backward", grid=(s // tm,),
        out_shape=(sd(o_lat.shape, MXU), sd((B_HEADS, s), F32), sd(duv_shape, F32)),
        in_specs=[_row_spec(tm, d_ob.shape[1]), _row_spec(tm, o_lat.shape[1]), _full_spec(w_uv.shape)],
        out_specs=(_row_spec(tm, o_lat.shape[1]), _rows_spec(B_HEADS, tm), _full_spec(duv_shape)),
        scratch_shapes=[pltpu.VMEM((tm, o_lat.shape[1]), F32)],
        compiler_params=_params(("arbitrary",)),
    )(d_ob, o_lat, w_uv)


def even_prep_backward(dqa, dka, dva, dqb, dkb, dvb, qa_raw, ka_raw, cq_raw, ckv_raw,
                       gq, gk, qln, kvln, w_uq_t, uk_bd, bd, cos_a, sin_a, cos_t, sin_t):
    s = qa_raw.shape[0]
    tm = min(ROW_TILE, s)
    half_lat = B_KV_LORA * B_HEADS // 2
    half_w = dqb.shape[1] // 2

    def body(dqa_ref, dka_ref, dva_ref, dqb_ref, dkb_ref, dvb_ref, qa_ref, ka_ref, cq_ref, ckv_ref,
             gq_ref, gk_ref, qln_ref, kvln_ref, uqt_ref, ukbd_ref, bd_ref, ca_ref, sa_ref, ct_ref, st_ref,
             pqa, pka, pva, pcq, pckv, pkr, gqn, gkn, gqln, gkvln, guq, guk):
        @pl.when(pl.program_id(0) == 0)
        def _():
            for r in (gqn, gkn, gqln, gkvln, guq, guk):
                r[...] = jnp.zeros(r.shape, F32)

        ca, sa, ct, st = ca_ref[...], sa_ref[...], ct_ref[...], st_ref[...]
        wide = lambda t, n: jnp.concatenate([t] * n, axis=1)
        rows = lambda a: jnp.sum(a, axis=0, keepdims=True)
        dx, dg = _head_norm_bwd(_rope_t(dqa_ref[...], wide(ca, 4), wide(sa, 4), 32), qa_ref[...], gq_ref[...],
                                bd_ref, HD)
        pqa[...] = dx.astype(MXU)
        gqn[...] += rows(dg)
        dk_all = jnp.concatenate([dka_ref[g] for g in range(A_KV)], axis=1)
        dx, dg = _head_norm_bwd(_rope_t(dk_all, ca, sa, 32), ka_ref[...], gk_ref[...], bd_ref[0:128, 0:128], HD)
        pka[...] = dx.astype(MXU)
        gkn[...] += rows(dg)
        pva[...] = jnp.concatenate([dva_ref[g] for g in range(A_KV)], axis=1).astype(MXU)
        cq_raw = cq_ref[...]
        cq_n = cq_raw * _rms(cq_raw) * qln_ref[...]
        qb = _mm_nt(cq_n, uqt_ref[...])
        d_lat = jnp.concatenate([dqb_ref[:, 0:half_lat], dqb_ref[:, half_w:half_w + half_lat]], axis=1)
        d_rope = jnp.concatenate([dqb_ref[:, half_lat:half_w], dqb_ref[:, half_w + half_lat:]], axis=1)
        for hh in range(B_HEADS):
            guk[:, B_NOPE * hh:B_NOPE * (hh + 1)] += _mm_tn(d_lat[:, B_KV_LORA * hh:B_KV_LORA * (hh + 1)],
                                                            qb[:, B_NOPE * hh:B_NOPE * (hh + 1)])
        dqb_all = jnp.concatenate([_mm_nt(d_lat, ukbd_ref[...]),
                                   _rope_t(d_rope, wide(ct, 2), wide(st, 2), 32)], axis=1)
        guq[...] += _mm_tn(dqb_all, cq_n)
        dx, dg = _rms_bwd(_mm(dqb_all, uqt_ref[...]), cq_raw, qln_ref[...])
        pcq[...] = dx.astype(MXU)
        gqln[...] += rows(dg)
        dkb_sum = dkb_ref[0] + dkb_ref[1]
        dckv = dkb_sum[:, 0:B_KV_LORA] + dvb_ref[0] + dvb_ref[1]
        dx, dg = _rms_bwd(dckv, ckv_ref[...], kvln_ref[...])
        pckv[...] = dx.astype(MXU)
        gkvln[...] += rows(dg)
        pkr[...] = _rope_t(dkb_sum[:, B_KV_LORA:B_QK], ct[:, 0:B_ROPE], st[:, 0:B_ROPE], 32).astype(MXU)

    sd = jax.ShapeDtypeStruct
    consts = [gq, gk, qln, kvln, w_uq_t, uk_bd, bd]
    in_specs = [_row_spec(tm, 512), _head_spec(A_KV, tm, HD), _head_spec(A_KV, tm, HD),
                _row_spec(tm, dqb.shape[1]), _head_spec(2, tm, B_QK), _head_spec(2, tm, B_KV_LORA),
                _row_spec(tm, 512), _row_spec(tm, 128), _row_spec(tm, B_Q_LORA), _row_spec(tm, B_KV_LORA)]
    in_specs += [_full_spec(a.shape) for a in consts] + [_row_spec(tm, 128)] * 4
    small = [sd(gq.shape, F32), sd(gk.shape, F32), sd(qln.shape, F32), sd(kvln.shape, F32), sd(w_uq_t.shape, F32),
             sd((B_KV_LORA, B_HEADS * B_NOPE), F32)]
    out_shape = (sd((s, 512), MXU), sd((s, 128), MXU), sd((s, 128), MXU), sd((s, B_Q_LORA), MXU),
                 sd((s, B_KV_LORA), MXU), sd((s, B_ROPE), MXU), *small)
    out_specs = (_row_spec(tm, 512), _row_spec(tm, 128), _row_spec(tm, 128), _row_spec(tm, B_Q_LORA),
                 _row_spec(tm, B_KV_LORA), _row_spec(tm, B_ROPE), *[_full_spec(a.shape) for a in small])
    return pl.pallas_call(
        body, name="even_prep_backward", grid=(s // tm,), out_shape=out_shape, in_specs=in_specs, out_specs=out_specs,
        compiler_params=_params(("arbitrary",)),
    )(dqa, dka, dva, dqb, dkb, dvb, qa_raw, ka_raw, cq_raw, ckv_raw, *consts, cos_a, sin_a, cos_t, sin_t)


def in_proj_backward(x, mod, nw, pieces, name, *, dx_out=None, w_in_t=None, dw_rows=None, dw_span=None,
                     exchange=None):
    s, d = x.shape
    d0, d1 = dw_span if dw_span is not None else (0, d)
    tm = min(ROW_TILE, s)
    grid = (s // tm,)
    n = len(pieces)
    cols = [c for _, c in pieces]
    want_dx = w_in_t is not None
    want_dw = dw_rows is not None
    n_cols = sum(c1 - c0 for c0, c1 in cols)
    dw_block = n_cols // N_DEV
    hosted = exchange is not None
    nx = exchange.n if hosted else 0

    def body(*refs):
        it = iter(refs)
        x_ref, mod_ref, nw_ref = next(it), next(it), next(it)
        dxo_ref, wt_ref = (next(it), next(it)) if want_dx else (None, None)
        p_refs = [next(it) for _ in range(n)]
        xs_refs = [next(it) for _ in range(nx)]
        dx_ref, dv_ref = (next(it), next(it)) if want_dx else (None, None)
        dw_ref = next(it) if want_dw else None
        land_refs = [next(it) for _ in range(nx)]
        acc_ref = next(it) if want_dx else None
        dw_acc = next(it) if want_dw else None
        sems = list(it)
        first, last = _grid_edges(grid)
        if hosted:
            pl.when(first)(lambda: exchange.start(xs_refs, land_refs, sems))

        @pl.when(first)
        def _():
            if want_dw:
                dw_acc[...] = jnp.zeros(dw_acc.shape, F32)
            if want_dx:
                acc_ref[...] = jnp.zeros(acc_ref.shape, F32)

        xn, g1, h = _modulated(x_ref[...], mod_ref, nw_ref)
        hb = h.astype(MXU)
        dh = jnp.zeros((tm, d), F32)
        for k, (pr, (c0, c1)) in enumerate(zip(p_refs, cols)):
            if len(pr.shape) == 3:
                pc = jnp.concatenate([pr[g] for g in range(pr.shape[0])], axis=1).astype(MXU)
            else:
                pc = pr[...].astype(MXU)
            if want_dx:
                dh = dh + jnp.dot(pc, wt_ref[c0:c1, :], preferred_element_type=F32)
            if want_dw:
                r0, r1 = dw_rows[k]
                dw_acc[r0:r1, :] += _mm_tn(pc, hb[:, d0:d1])
        if want_dx:
            acc_ref[0:1, :] += jnp.sum(dh, axis=0, keepdims=True)
            acc_ref[1:2, :] += jnp.sum(dh * xn, axis=0, keepdims=True)
            dxn = dh * g1
            x = x_ref[...]
            dx_ref[...] = dxo_ref[...] + _rms(x) * (dxn - xn * jnp.mean(dxn * xn, axis=-1, keepdims=True))

        @pl.when(last)
        def _():
            if want_dx:
                dg1 = acc_ref[1:2, :]
                dv_ref[0:1, :] = acc_ref[0:1, :]
                dv_ref[1:2, :] = dg1 * nw_ref[...]
                dv_ref[2:3, :] = dg1 * (1.0 + mod_ref[1:2, :])
                dv_ref[3:4, :] = jnp.zeros((1, d), F32)
            if want_dw:
                for j in range(N_DEV):
                    dw_ref[j] = dw_acc[j * dw_block:(j + 1) * dw_block, :].astype(MXU)

        if hosted:
            pl.when(last)(lambda: exchange.wait(xs_refs, land_refs, sems))

    arrs = [a for a, _ in pieces]
    sd = jax.ShapeDtypeStruct
    args = [x, mod, nw] + ([dx_out, w_in_t] if want_dx else []) + arrs + (exchange.srcs if hosted else [])
    in_specs = [_row_spec(tm, d), _full_spec(mod.shape), _full_spec(nw.shape)]
    in_specs += [_row_spec(tm, d), _full_spec(w_in_t.shape)] if want_dx else []
    in_specs += [_row_spec(tm, a.shape[1]) if a.ndim == 2 else _head_spec(a.shape[0], tm, a.shape[2]) for a in arrs]
    in_specs += exchange.in_specs if hosted else []
    out_shape, out_specs, scratch = [], [], []
    if want_dx:
        out_shape += [sd((s, d), F32), sd((4, d), F32)]
        out_specs += [_row_spec(tm, d), _full_spec((4, d))]
        scratch.append(pltpu.VMEM((8, d), F32))
    if want_dw:
        out_shape.append(sd((N_DEV, dw_block, d1 - d0), MXU))
        out_specs.append(_full_spec((N_DEV, dw_block, d1 - d0)))
        scratch.append(pltpu.VMEM((n_cols, d1 - d0), F32))
    if hosted:
        out_shape += list(exchange.land_shapes)
        out_specs += list(exchange.out_specs)
        scratch += list(exchange.sems)
    return pl.pallas_call(
        body, name=name, grid=grid, out_shape=tuple(out_shape), in_specs=in_specs, out_specs=tuple(out_specs),
        scratch_shapes=scratch, compiler_params=_params(("arbitrary",)),
    )(*args)


def ada_weight_grad(c_all, dmod_cols):
    d = c_all.shape[1]
    w = dmod_cols.shape[2]

    def body(c_ref, dm_ref, out_ref):
        ca = _silu(c_ref[...])
        for l in range(2):
            out_ref[l] = _mm_tn(ca, dm_ref[l])

    return pl.pallas_call(
        body, name="ada_weight_grad",
        out_shape=jax.ShapeDtypeStruct((2, d, w), F32),
        compiler_params=pltpu.CompilerParams(vmem_limit_bytes=VMEM_LIMIT),
    )(c_all, dmod_cols)


def _slot_sum(g_ref):
    g = g_ref[0].astype(F32)
    for k in range(1, g_ref.shape[0]):
        g = g + g_ref[k].astype(F32)
    return g


def _adamw_math(g, w, m, v):
    m_new = ADAM_B1 * m + (1.0 - ADAM_B1) * g
    v_new = ADAM_B2 * v + (1.0 - ADAM_B2) * (g * g)
    m_hat = m_new / (1.0 - ADAM_B1 ** ADAM_STEP)
    v_hat = v_new / (1.0 - ADAM_B2 ** ADAM_STEP)
    return -ADAM_LR * (m_hat / (jnp.sqrt(v_hat) + ADAM_EPS) + ADAM_WD * w), m_new, v_new


def adamw_small(g_alls, ws, ms, vs, loss_all):
    n = len(ws)

    def body(*refs):
        g_refs, w_refs, m_refs, v_refs = (refs[i * n:(i + 1) * n] for i in range(4))
        loss_ref = refs[4 * n]
        outs = refs[4 * n + 1:]
        for i in range(n):
            g = _slot_sum(g_refs[i])
            outs[i][...] = g
            outs[n + i][...], outs[2 * n + i][...], outs[3 * n + i][...] = _adamw_math(
                g, w_refs[i][...], m_refs[i][...], v_refs[i][...])
        outs[4 * n][...] = _slot_sum(loss_ref)

    sds = [jax.ShapeDtypeStruct(w.shape, F32) for w in ws]
    res = pl.pallas_call(
        body, name="adamw_small", out_shape=tuple(sds * 4) + (jax.ShapeDtypeStruct(loss_all.shape[1:], F32),),
        compiler_params=pltpu.CompilerParams(vmem_limit_bytes=VMEM_LIMIT),
    )(*g_alls, *ws, *ms, *vs, loss_all)
    return [res[i * n:(i + 1) * n] for i in range(4)], res[4 * n]


def adamw_rows(g_slots, w, m, v, name):
    parts = list(g_slots) if isinstance(g_slots, (list, tuple)) else [g_slots]
    n, r = parts[0].shape[:2]
    lanes = sum(p.shape[2] for p in parts)
    fits = [t for t in range(16, r + 1, 16) if r % t == 0 and t * lanes <= ADAM_TILE]
    tr = max(fits) if fits else r

    def body(*refs):
        g_refs, (w_ref, m_ref, v_ref, go, do, mo, vo) = refs[:len(parts)], refs[len(parts):]
        g = jnp.concatenate([_slot_sum(g_ref) for g_ref in g_refs], axis=1)
        go[...] = g
        do[...], mo[...], vo[...] = _adamw_math(g, w_ref[...], m_ref[...], v_ref[...])

    row = pl.BlockSpec((tr, lanes), lambda i: (i, 0))
    sd = jax.ShapeDtypeStruct((r, lanes), F32)
    return pl.pallas_call(
        body, name=name, grid=(r // tr,), out_shape=(sd, sd, sd, sd),
        in_specs=[pl.BlockSpec((n, tr, p.shape[2]), lambda i: (0, i, 0)) for p in parts] + [row, row, row],
        out_specs=(row, row, row, row),
        compiler_params=_params(("parallel",)),
    )(*parts, w, m, v)


def _rope_tables(s):
    def cs(pos, dim):
        inv = ROPE_THETA ** (-np.arange(0, dim, 2, dtype=np.float32) / dim)
        ang = pos.astype(np.float32)[:, None] * inv.astype(np.float32)[None, :]
        return np.cos(ang), np.sin(ang)

    rows = s // GRID_W
    row = np.repeat(np.arange(rows), GRID_W)
    col = np.tile(np.arange(GRID_W), rows)
    cr, sr = cs(row, HD // 2)
    cc, sc = cs(col, HD // 2)
    ct, st = cs(np.arange(s), B_ROPE)
    tables = (np.concatenate([cr, cr, cc, cc] * 2, axis=-1), np.concatenate([-sr, sr, -sc, sc] * 2, axis=-1),
              np.concatenate([ct, ct] * 4, axis=-1), np.concatenate([-st, st] * 4, axis=-1))
    return tuple(jnp.asarray(t, F32) for t in tables)


def _even_rows_to_kernel(wt):
    return jnp.concatenate([wt[:1664], wt[1696:], wt[1664:1696]], axis=0)


def _uq_rows_to_kernel(wt):
    r = wt.reshape(B_HEADS, B_NOPE + B_ROPE, -1)
    return jnp.concatenate([r[:, :B_NOPE].reshape(B_HEADS * B_NOPE, -1), r[:, B_NOPE:].reshape(B_HEADS * B_ROPE, -1)])


def _uq_rows_to_reference(wt):
    nope = wt[:B_HEADS * B_NOPE].reshape(B_HEADS, B_NOPE, -1)
    rope = wt[B_HEADS * B_NOPE:].reshape(B_HEADS, B_ROPE, -1)
    return jnp.concatenate([nope, rope], axis=1).reshape(B_HEADS * (B_NOPE + B_ROPE), -1)


def _shard_t(w):
    return jnp.transpose(w[0])


def _unshard_t(wt, like):
    return jnp.transpose(wt)[None].reshape(like.shape)


def kernel(x, c, norm_w, ada_w, ada_b, even_w_in, a_q_norm, a_k_norm, b_q_lora_norm, b_kv_lora_norm, b_w_uq, b_w_uk, b_w_uv, even_w_out, odd_w_in, c_sink, odd_w_out, final_norm, loss_target, m_norm_w, m_ada_w, m_ada_b, m_even_w_in, m_a_q_norm, m_a_k_norm, m_b_q_lora_norm, m_b_kv_lora_norm, m_b_w_uq, m_b_w_uk, m_b_w_uv, m_even_w_out, m_odd_w_in, m_c_sink, m_odd_w_out, m_final_norm, v_norm_w, v_ada_w, v_ada_b, v_even_w_in, v_a_q_norm, v_a_k_norm, v_b_q_lora_norm, v_b_kv_lora_norm, v_b_w_uq, v_b_w_uk, v_b_w_uv, v_even_w_out, v_odd_w_in, v_c_sink, v_odd_w_out, v_final_norm):
    s, d = x.shape[1], x.shape[2]
    x0 = x[0]
    target = loss_target[0]
    me_flat = 4 * lax.axis_index("x") + 2 * lax.axis_index("y") + lax.axis_index("c")

    wcols = ada_w.shape[2]
    bias_cols = lax.dynamic_slice_in_dim(ada_b.reshape(2, N_DEV, wcols), me_flat, 1, axis=1)
    call, modp, (g_in_e, g_uq) = ada_forward(
        jnp.broadcast_to(c, (8, d)), ada_w, bias_cols,
        Gather([_shard_t(even_w_in).astype(MXU), _shard_t(b_w_uq).astype(MXU)]))
    wt_in_e = _even_rows_to_kernel(g_in_e.reshape(-1, d))
    wt_uq = _uq_rows_to_kernel(g_uq.reshape(-1, B_Q_LORA))
    later_exchange = Exchange([_shard_t(odd_w_in).astype(MXU), even_w_out[0].astype(MXU),
                               odd_w_out[0].astype(MXU)], scatter=False)
    uk_bd = (jnp.eye(B_HEADS, dtype=F32)[:, None, :, None] * jnp.transpose(b_w_uk[0], (1, 2, 0))[:, :, None, :]
             ).reshape(B_HEADS * B_NOPE, B_HEADS * B_KV_LORA).astype(MXU)
    head_bd = jnp.asarray(np.kron(np.eye(A_HEADS), np.ones((HD, HD))), MXU)
    gq_full, gk_full = jnp.tile(a_q_norm, (1, A_HEADS)), jnp.tile(a_k_norm, (1, A_KV))
    w_uv = jnp.transpose(b_w_uv[0], (1, 0, 2)).astype(MXU)

    c_all = call[:, 0, :]
    mod = jnp.transpose(modp[:, :, 0, :], (1, 0, 2)).reshape(2, 3, d)
    mod_e, mod_o = mod[0], mod[1]
    nw_e, nw_o = norm_w[0:1], norm_w[1:2]

    cos_a, sin_a, cos_t, sin_t = _rope_tables(s)
    slopes = (2.0 ** (-8.0 * jnp.arange(1, C_HEADS + 1, dtype=F32) / C_HEADS)).reshape(C_HEADS, 1, 1)
    sink2 = c_sink.reshape(C_HEADS, 1, 1) * LOG2E

    (qa, ka, va, qb, kb, kat, vat, kbt, qa_raw, ka_raw, cq_raw, ckv_raw, ga, gb) = even_in_forward(
        x0, mod_e, nw_e, wt_in_e, gq_full, gk_full, b_q_lora_norm, b_kv_lora_norm, wt_uq, uk_bd, head_bd,
        cos_a, sin_a, cos_t, sin_t)
    tk_dense = min(512, s)
    tq_dense = min(256, s)
    fwd_sub = min(8, s // tk_dense)
    bwd_sub_a = min(16, s // tq_dense)
    bwd_sub_b = min(8, s // tq_dense)
    oa, lse_a, g_in_o, g_out_e, g_out_o = flash_forward(
        qa, ka, vat, dv=HD, tq=tq_dense, tk=tk_dense, nsub=fwd_sub, name="attn_a_fwd",
        exchange=later_exchange)
    wt_in_o = g_in_o.reshape(-1, d)
    w_out_e = g_out_e.reshape(-1, d)
    w_out_o = g_out_o.reshape(-1, d)
    o_lat, lse_b = flash_forward(qb, kb, kbt, dv=B_KV_LORA, tq=min(128, s), tk=tk_dense, nsub=fwd_sub,
                                 name="attn_b_fwd")
    ob = latent_out_forward(o_lat, w_uv)
    x1, y_e = mixer_out_forward(x0, mod_e, [(oa, ga), (ob, gb)], w_out_e, "even_out_fwd")

    qc, kc, vc, kct, vct, gc = odd_in_forward(x1, mod_o, nw_o, wt_in_o)
    win_sub = min(8, s // WINDOW)
    oc, lse_c = window_forward(qc, kc, vct, sink2, slopes, win_sub, "attn_c_fwd")
    dx2, y_o, loss_lanes, d_final = mixer_out_forward(x1, mod_o, [(oc, gc)], w_out_o, "odd_out_fwd_loss",
                                                      loss=(target, final_norm.reshape(1, d)))

    loss_part = (0.5 / d) * jnp.sum(loss_lanes)

    doc, dgc, delta_c, dgate_o, dw_out_o, dsink = mixer_out_backward(
        dx2, y_o, mod_o, [(oc, gc)], w_out_o, [C_HEADS], "odd_out_bwd", lse=lse_c.reshape(C_HEADS, s),
        sink=sink2.reshape(C_HEADS, 1))
    rows3 = lambda t: t.reshape(t.shape[0], 1, s)
    dqc, dkc, dvc = window_backward(qc, kc, kct, vc, doc, lse_c, rows3(delta_c), slopes, win_sub, "attn_c_bwd")
    dx1, dvec_o, dwt_in_o = in_proj_backward(
        x1, mod_o, nw_o, [(dqc, O_Q), (dkc, O_K), (dvc, O_V), (dgc, O_G)], "odd_in_bwd",
        dx_out=dx2, w_in_t=wt_in_o, dw_rows=[O_Q, O_K, O_V, O_G])

    doa, dga, dob, dgb, delta_a, dgate_e, dw_out_e = mixer_out_backward(
        dx1, y_e, mod_e, [(oa, ga), (ob, gb)], w_out_e, [A_HEADS, 0], "even_out_bwd")
    d_olat, delta_b, dw_uv = latent_out_backward(dob, o_lat, w_uv)
    blocks = lambda g: g.astype(MXU).reshape(N_DEV, g.shape[0] // N_DEV, g.shape[1])
    even_pieces = lambda: [(pqa, E_QA), (pka, E_KA), (pva, E_VA), (dga, E_GA), (pcq, E_CQ), (pckv, E_CKV),
                           (dgb, E_GB), (pkr, E_KR)]
    scatter_odd = Exchange([dwt_in_o, dw_out_o], True)
    scatter_out_e = Exchange([dw_out_e], True)
    dqb, dkb, dvb, l_in_o, l_out_o = flash_backward(
        qb, kb, kbt, None, d_olat, lse_b, rows3(delta_b), scale=SCALE_B, dv=B_KV_LORA,
        tq=tq_dense, tk=tk_dense, nsub=bwd_sub_b, gq=2, name="attn_b_bwd", split=B_KV_LORA, exchange=scatter_odd)
    dqa, dka, dva, l_out_e = flash_backward(
        qa, ka, kat, va, doa, lse_a, rows3(delta_a), scale=SCALE_A, dv=HD,
        tq=tq_dense, tk=tk_dense, nsub=bwd_sub_a, gq=A_KV, name="attn_a_bwd", exchange=scatter_out_e)
    (pqa, pka, pva, pcq, pckv, pkr, g_qn, g_kn, g_qln, g_kvln, dwt_uq, dw_uk) = even_prep_backward(
        dqa, dka, dva, dqb, dkb, dvb, qa_raw, ka_raw, cq_raw, ckv_raw,
        gq_full, gk_full, b_q_lora_norm, b_kv_lora_norm, wt_uq, uk_bd, head_bd, cos_a, sin_a, cos_t, sin_t)
    g_qn = jnp.sum(g_qn.reshape(A_HEADS, HD), axis=0)
    g_kn = jnp.sum(g_kn.reshape(A_KV, HD), axis=0)
    even_rows = [E_QA, E_KA, E_VA, E_GA, E_CQ, E_CKV, (1696, 2208), (1664, 1696)]
    dwt_lo, l_uk, l_uv = in_proj_backward(
        x0, mod_e, nw_e, even_pieces(), "even_in_bwd_dw_lo", dw_rows=even_rows, dw_span=(0, d // 2),
        exchange=Exchange([dw_uk.astype(MXU), dw_uv.astype(MXU)], scatter=False))
    dwt_hi, l_in_lo, l_uq = in_proj_backward(
        x0, mod_e, nw_e, even_pieces(), "even_in_bwd_dw_hi", dw_rows=even_rows, dw_span=(d // 2, d),
        exchange=Exchange([dwt_lo, blocks(_uq_rows_to_reference(dwt_uq))], True))
    dx0, dvec_e, l_in_hi = in_proj_backward(
        x0, mod_e, nw_e, even_pieces(), "even_in_bwd_dx", dx_out=dx1, w_in_t=wt_in_e,
        exchange=Exchange([dwt_hi], True))
    l_in_e = [l_in_lo, l_in_hi]

    dmod = jnp.stack([jnp.concatenate([dvec_e[0], dvec_e[1], dgate_e[0]]),
                      jnp.concatenate([dvec_o[0], dvec_o[1], dgate_o[0]])])
    d_norm_w = jnp.stack([dvec_e[2], dvec_o[2]])
    small_names = ["norm_w", "ada_b", "a_q_norm", "a_k_norm", "b_q_lora_norm", "b_kv_lora_norm", "b_w_uk", "b_w_uv",
                   "c_sink", "final_norm"]
    small_w = [norm_w, ada_b, a_q_norm, a_k_norm, b_q_lora_norm, b_kv_lora_norm, b_w_uk, b_w_uv, c_sink, final_norm]
    small_m = [m_norm_w, m_ada_b, m_a_q_norm, m_a_k_norm, m_b_q_lora_norm, m_b_kv_lora_norm, m_b_w_uk, m_b_w_uv,
               m_c_sink, m_final_norm]
    small_v = [v_norm_w, v_ada_b, v_a_q_norm, v_a_k_norm, v_b_q_lora_norm, v_b_kv_lora_norm, v_b_w_uk, v_b_w_uv,
               v_c_sink, v_final_norm]
    small_g = [d_norm_w, dmod, g_qn, g_kn, g_qln, g_kvln, None, None, dsink, d_final]
    flat2 = lambda a: a.reshape((1, -1)) if a.size == a.shape[-1] else a.reshape(a.shape[-3:] if a.ndim > 3 else a.shape)
    kshape = [flat2(w).shape for w in small_w]
    late = [i for i, g in enumerate(small_g) if g is not None]
    gathered = all_gather_slots(
        Gather([small_g[i].reshape(kshape[i]) for i in late] + [jnp.full((8, 128), loss_part, F32)]),
        "gather_small_grads")
    g_all = [None] * len(small_g)
    for i, g in zip(late, gathered):
        g_all[i] = g
    g_all[6], g_all[7] = (l.reshape((N_DEV,) + kshape[6]) for l in (l_uk, l_uv))
    sm_out, loss_sum = adamw_small(g_all, [flat2(a) for a in small_w], [flat2(a) for a in small_m],
                                   [flat2(a) for a in small_v], gathered[-1])
    loss = loss_sum[0, 0]
    sm = [{nm: p.reshape(w.shape) for nm, w, p in zip(small_names, small_w, outs)} for outs in sm_out]

    dmod_all = g_all[1].reshape(N_DEV, 2, N_DEV, wcols)
    dmod_cols = lax.dynamic_slice_in_dim(dmod_all, me_flat, 1, axis=2)[:, :, 0, :]
    pad16 = lambda a: jnp.concatenate([a, jnp.zeros_like(a)], axis=0)
    g_ada_w = ada_weight_grad(pad16(c_all), jnp.transpose(pad16(dmod_cols), (1, 0, 2)))
    rows_of = lambda a: a.reshape(-1, wcols)
    ada = adamw_rows(rows_of(g_ada_w)[None], rows_of(ada_w), rows_of(m_ada_w), rows_of(v_ada_w), "adamw_ada_w")
    ada = [p.reshape(ada_w.shape) for p in ada]

    bg = [{}, {}, {}, {}]
    for nm, landed, w, m, v, transposed in (
            ("even_w_in", l_in_e, even_w_in, m_even_w_in, v_even_w_in, True),
            ("b_w_uq", l_uq, b_w_uq, m_b_w_uq, v_b_w_uq, True),
            ("odd_w_in", l_in_o, odd_w_in, m_odd_w_in, v_odd_w_in, True),
            ("even_w_out", l_out_e, even_w_out, m_even_w_out, v_even_w_out, False),
            ("odd_w_out", l_out_o, odd_w_out, m_odd_w_out, v_odd_w_out, False)):
        view = _shard_t if transposed else (lambda a: a[0])
        res = adamw_rows(landed, view(w), view(m), view(v), "adamw_" + nm)
        for kind, p in enumerate(res):
            bg[kind][nm] = _unshard_t(p, w) if transposed else p[None]
    big_names = ["even_w_in", "odd_w_in", "even_w_out", "odd_w_out", "b_w_uq"]

    order = ["norm_w", "ada_w", "ada_b", "even_w_in", "a_q_norm", "a_k_norm", "b_q_lora_norm", "b_kv_lora_norm",
             "b_w_uq", "b_w_uk", "b_w_uv", "even_w_out", "odd_w_in", "c_sink", "odd_w_out", "final_norm"]

    def pick(kind):
        out = []
        for nm in order:
            if nm == "ada_w":
                out.append(ada[kind])
            elif nm in big_names:
                out.append(bg[kind][nm])
            else:
                out.append(sm[kind][nm])
        return out

    return (loss, dx0[None], *pick(0), *pick(1), *pick(2), *pick(3))
```

```python
import functools

import jax
import jax.numpy as jnp
import numpy as np
from jax import lax
from jax.experimental import pallas as pl
from jax.experimental.pallas import tpu as pltpu

F32 = jnp.float32
MXU = jnp.bfloat16
EPS = 1e-6
ROPE_THETA = 10000.0
GRID_W = 64
HD = 64
N_DEV = 8

A_HEADS, A_KV = 8, 2
B_HEADS, B_NOPE, B_ROPE, B_Q_LORA, B_KV_LORA = 8, 64, 32, 256, 128
B_QK = B_KV_LORA + B_ROPE
C_HEADS, C_KV = 16, 4
WINDOW = 128

ADAM_LR, ADAM_B1, ADAM_B2, ADAM_EPS, ADAM_WD, ADAM_STEP = 0.001, 0.9, 0.999, 1e-08, 0.01, 10

ROW_TILE = 512
ADAM_TILE = 2048 * 128
LOG2E = 1.4426950408889634
SCALE_A = HD ** -0.5
SCALE_B = (B_NOPE + B_ROPE) ** -0.5
SCALE2_A, SCALE2_B = SCALE_A * LOG2E, SCALE_B * LOG2E
VMEM_LIMIT = 56 * 1024 * 1024

E_QA, E_KA, E_VA, E_GA, E_CQ, E_CKV, E_GB, E_KR = (
    (0, 512), (512, 640), (640, 768), (768, 1280), (1280, 1536), (1536, 1664), (1664, 2176), (2176, 2208))
O_Q, O_K, O_V, O_G = (0, 1024), (1024, 1280), (1280, 1536), (1536, 2560)


def _mm(a, b):
    return jnp.dot(a.astype(MXU), b.astype(MXU), preferred_element_type=F32)


def _mm_nt(a, b):
    return lax.dot_general(a.astype(MXU), b.astype(MXU), (((1,), (1,)), ((), ())), preferred_element_type=F32)


def _mm_tn(a, b):
    return lax.dot_general(a.astype(MXU), b.astype(MXU), (((0,), (0,)), ((), ())), preferred_element_type=F32)


def _group_sums_t(prod, group):
    tm, w = prod.shape
    sel = (lax.broadcasted_iota(jnp.int32, (w, 128), 0) // group
           == lax.broadcasted_iota(jnp.int32, (w, 128), 1)).astype(MXU)
    hi = prod.astype(MXU)
    lo = prod - hi.astype(F32)
    return (_mm(hi, sel) + _mm(lo, sel)).T


def _sigmoid(z):
    return 1.0 / (1.0 + jnp.exp(-z))


def _silu(z):
    return z * _sigmoid(z)


def _rms(x):
    return lax.rsqrt(jnp.mean(x * x, axis=-1, keepdims=True) + EPS)


def _swap_halves(y, group):
    n = y.shape[-1]
    half = group // 2
    fwd = pltpu.roll(y, half, 1)
    if n == group:
        return fwd
    back = pltpu.roll(y, n - half, 1)
    lane = lax.broadcasted_iota(jnp.int32, y.shape, 1)
    return jnp.where((lane % group) < half, back, fwd)


def _rope(y, cos, sin, group):
    return y * cos + _swap_halves(y, group) * sin


def _rope_t(d, cos, sin, group):
    return d * cos - _swap_halves(d, group) * sin


def _rms_bwd(dy, x, g):
    r = _rms(x)
    xhat = x * r
    dxhat = dy * g
    dx = r * (dxhat - xhat * jnp.mean(dxhat * xhat, axis=-1, keepdims=True))
    return dx, dy * xhat


def _group_mean(v, bd, group):
    hi = v.astype(MXU)
    lo = v - hi.astype(F32)
    return (_mm(hi, bd[...]) + _mm(lo, bd[...])) * (1.0 / group)


def _head_norm(x, g, bd, group):
    return x * lax.rsqrt(_group_mean(x * x, bd, group) + EPS) * g


def _head_norm_bwd(dy, x, g, bd, group):
    r = lax.rsqrt(_group_mean(x * x, bd, group) + EPS)
    xhat = x * r
    dxhat = dy * g
    dx = r * (dxhat - xhat * _group_mean(dxhat * xhat, bd, group))
    return dx, dy * xhat


def _params(sem, vmem=VMEM_LIMIT):
    return pltpu.CompilerParams(dimension_semantics=sem, vmem_limit_bytes=vmem)


def _row_spec(tm, w):
    return pl.BlockSpec((tm, w), lambda i: (i, 0))


def _full_spec(shape):
    nd = len(shape)
    return pl.BlockSpec(shape, lambda i: (0,) * nd)


def _head_spec(h, tm, w):
    return pl.BlockSpec((h, tm, w), lambda i: (0, i, 0))


def _headt_spec(h, w, tm):
    return pl.BlockSpec((h, w, tm), lambda i: (0, 0, i))


def _rows_spec(h, tm):
    return pl.BlockSpec((h, tm), lambda i: (0, i))


def _me():
    return lax.axis_index("x"), lax.axis_index("y"), lax.axis_index("c")


def _flat(p):
    return 4 * p[0] + 2 * p[1] + p[2]


def _peer(me, k):
    x, y, c = me
    return (1 - x if k & 4 else x, 1 - y if k & 2 else y, 1 - c if k & 1 else c)


MESH_ID = pl.DeviceIdType.MESH


class Gather:
    VMEM = pl.BlockSpec(memory_space=pltpu.VMEM)

    def __init__(self, shards):
        self.shards = list(shards)
        self.n = len(self.shards)
        self.out_shapes = tuple(jax.ShapeDtypeStruct((N_DEV,) + a.shape, a.dtype) for a in self.shards)
        self.in_specs = [Gather.VMEM] * self.n
        self.out_specs = (Gather.VMEM,) * self.n
        self.sems = [pltpu.SemaphoreType.DMA((7 * self.n,)), pltpu.SemaphoreType.DMA((7 * self.n,)),
                     pltpu.SemaphoreType.DMA((self.n,))]

    def _plan(self, x_refs, out_refs, sems):
        send_sems, recv_sems, local_sems = sems
        me = _me()
        x, y, c = me
        chips = [(1 - x, y), (x, 1 - y), (1 - x, 1 - y)]

        def copy(a, k, block, to, src=None):
            slot = out_refs[a].at[_flat(block)]
            return pltpu.make_async_remote_copy(
                src_ref=slot if src is None else src, dst_ref=slot, send_sem=send_sems.at[7 * a + k],
                recv_sem=recv_sems.at[7 * a + k], device_id=to, device_id_type=MESH_ID)

        mine = [pltpu.make_async_copy(x_refs[a], out_refs[a].at[_flat(me)], local_sems.at[a]) for a in range(self.n)]
        first = [copy(a, 0, me, (x, y, 1 - c), src=x_refs[a]) for a in range(self.n)]
        first += [copy(a, 1 + j, me, (*chip, c), src=x_refs[a]) for a in range(self.n) for j, chip in enumerate(chips)]
        return me, chips, copy, mine, first

    def start(self, x_refs, out_refs, sems):
        _, _, _, mine, first = self._plan(x_refs, out_refs, sems)
        for cp in mine + first:
            cp.start()

    def forward(self, x_refs, out_refs, sems):
        me, chips, copy, _, _ = self._plan(x_refs, out_refs, sems)
        x, y, c = me
        for a in range(self.n):
            for j, chip in enumerate(chips):
                copy(a, 1 + j, (*chip, c), me).wait_recv()
                copy(a, 4 + j, (*chip, c), (x, y, 1 - c)).start()

    def drain(self, x_refs, out_refs, sems):
        me, chips, copy, mine, first = self._plan(x_refs, out_refs, sems)
        x, y, c = me
        sibling = (x, y, 1 - c)
        for a in range(self.n):
            copy(a, 0, sibling, me).wait_recv()
            for j, chip in enumerate(chips):
                copy(a, 4 + j, (*chip, 1 - c), me).wait_recv()
        for cp in first + [copy(a, 4 + j, (*chip, c), sibling) for a in range(self.n) for j, chip in enumerate(chips)]:
            cp.wait_send()
        for cp in mine:
            cp.wait()

    def finish(self, x_refs, out_refs, sems):
        self.forward(x_refs, out_refs, sems)
        self.drain(x_refs, out_refs, sems)


def all_gather_slots(gather, name):
    def body(*refs):
        x_refs, out_refs, sems = refs[:gather.n], refs[gather.n:2 * gather.n], refs[2 * gather.n:]
        gather.start(x_refs, out_refs, sems)
        gather.finish(x_refs, out_refs, sems)

    return pl.pallas_call(
        body, name=name, out_shape=gather.out_shapes, in_specs=gather.in_specs, out_specs=gather.out_specs,
        scratch_shapes=list(gather.sems), compiler_params=pltpu.CompilerParams(vmem_limit_bytes=VMEM_LIMIT),
    )(*gather.shards)


class Exchange:
    HBM = pl.BlockSpec(memory_space=pl.ANY)

    def __init__(self, srcs, scatter):
        self.srcs = list(srcs)
        self.n = len(self.srcs)
        self.scatter = list(scatter) if isinstance(scatter, (list, tuple)) else [scatter] * self.n
        self.land_shapes = tuple(jax.ShapeDtypeStruct((N_DEV,) + tuple(a.shape[-2:]), a.dtype) for a in self.srcs)
        self.in_specs = [Exchange.HBM] * self.n
        self.out_specs = (Exchange.HBM,) * self.n
        self.sems = [pltpu.SemaphoreType.DMA((N_DEV - 1,)), pltpu.SemaphoreType.DMA((N_DEV - 1,)),
                     pltpu.SemaphoreType.DMA] * self.n

    def _copies(self, src_refs, land_refs, sems):
        me = _me()
        mi = _flat(me)
        local, sends, recvs = [], [], []
        for a, (src_ref, land_ref) in enumerate(zip(src_refs, land_refs)):
            send_sems, recv_sems, local_sem = sems[3 * a:3 * a + 3]
            pick = (lambda p, r=src_ref: r.at[_flat(p)]) if self.scatter[a] else (lambda p, r=src_ref: r)
            local.append(pltpu.make_async_copy(pick(me), land_ref.at[mi], local_sem))
            for k in range(1, N_DEV):
                peer = _peer(me, k)
                pair = dict(send_sem=send_sems.at[k - 1], recv_sem=recv_sems.at[k - 1], device_id=peer,
                            device_id_type=MESH_ID)
                sends.append(pltpu.make_async_remote_copy(src_ref=pick(peer), dst_ref=land_ref.at[mi], **pair))
                recvs.append(pltpu.make_async_remote_copy(src_ref=pick(peer), dst_ref=land_ref.at[_flat(peer)],
                                                          **pair))
        return local, sends, recvs

    def start(self, src_refs, land_refs, sems):
        local, sends, _ = self._copies(src_refs, land_refs, sems)
        for cp in local + sends:
            cp.start()

    def wait(self, src_refs, land_refs, sems):
        local, sends, recvs = self._copies(src_refs, land_refs, sems)
        for cp in recvs:
            cp.wait_recv()
        for cp in sends:
            cp.wait_send()
        for cp in local:
            cp.wait()


def ada_forward(c8, ada_w, bias_cols, gather):
    d = c8.shape[1]
    w = ada_w.shape[2]
    ng = gather.n

    def body(*refs):
        c_ref, w_ref, b_ref = refs[:3]
        gx_refs = refs[3:3 + ng]
        call_ref, modp_ref = refs[3 + ng:5 + ng]
        gout_refs = refs[5 + ng:5 + 2 * ng]
        part_ref, s1, r1, s2, r2 = refs[5 + 2 * ng:10 + 2 * ng]
        g_sems = refs[10 + 2 * ng:]
        me = _me()
        mi = _flat(me)
        call_ref[mi] = c_ref[...]
        rows_out = []
        for k in range(1, N_DEV):
            rows_out.append(pltpu.make_async_remote_copy(
                src_ref=c_ref, dst_ref=call_ref.at[mi], send_sem=s1.at[k - 1], recv_sem=r1.at[k - 1],
                device_id=_peer(me, k), device_id_type=MESH_ID))
        for cp in rows_out:
            cp.start()
        gather.start(gx_refs, gout_refs, g_sems)
        for k in range(1, N_DEV):
            pltpu.make_async_remote_copy(
                src_ref=c_ref, dst_ref=call_ref.at[_flat(_peer(me, k))], send_sem=s1.at[k - 1],
                recv_sem=r1.at[k - 1], device_id=_peer(me, k), device_id_type=MESH_ID).wait_recv()
        ca = _silu(call_ref[...].reshape(N_DEV * 8, d))
        for l in range(2):
            part = _mm(ca, w_ref[l]) + b_ref[l]
            for b in range(N_DEV):
                part_ref[b, l] = part[8 * b:8 * b + 8, :]
        modp_ref[mi] = part_ref[mi]
        spread = []
        for k in range(1, N_DEV):
            peer = _peer(me, k)
            spread.append(pltpu.make_async_remote_copy(
                src_ref=part_ref.at[_flat(peer)], dst_ref=modp_ref.at[mi], send_sem=s2.at[k - 1],
                recv_sem=r2.at[k - 1], device_id=peer, device_id_type=MESH_ID))
        for cp in spread:
            cp.start()
        gather.forward(gx_refs, gout_refs, g_sems)
        for k in range(1, N_DEV):
            pi = _flat(_peer(me, k))
            pltpu.make_async_remote_copy(
                src_ref=part_ref.at[pi], dst_ref=modp_ref.at[pi], send_sem=s2.at[k - 1],
                recv_sem=r2.at[k - 1], device_id=_peer(me, k), device_id_type=MESH_ID).wait_recv()
        for cp in rows_out + spread:
            cp.wait_send()
        gather.drain(gx_refs, gout_refs, g_sems)

    vm = pl.BlockSpec(memory_space=pltpu.VMEM)
    res = pl.pallas_call(
        body, name="ada_forward",
        out_shape=(jax.ShapeDtypeStruct((N_DEV, 8, d), F32), jax.ShapeDtypeStruct((N_DEV, 2, 8, w), F32))
        + gather.out_shapes,
        in_specs=[vm, vm, vm] + gather.in_specs, out_specs=(vm, vm) + gather.out_specs,
        scratch_shapes=[pltpu.VMEM((N_DEV, 2, 8, w), F32)] + [pltpu.SemaphoreType.DMA((7,))] * 4 + list(gather.sems),
        compiler_params=pltpu.CompilerParams(vmem_limit_bytes=VMEM_LIMIT),
    )(c8, ada_w, bias_cols, *gather.shards)
    return res[0], res[1], res[2:]


def _modulated(x, mod_ref, nw_ref):
    xn = x * _rms(x)
    g1 = nw_ref[...] * (1.0 + mod_ref[1:2, :])
    return xn, g1, xn * g1 + mod_ref[0:1, :]


def even_in_forward(x, mod, nw, w_in_t, gq, gk, qln, kvln, w_uq_t, uk_bd, bd, cos_a, sin_a, cos_t, sin_t):
    s, d = x.shape
    tm = min(ROW_TILE, s)
    n_nope = B_HEADS * B_NOPE

    def body(x_ref, mod_ref, nw_ref, w_ref, gq_ref, gk_ref, qln_ref, kvln_ref, uq_ref, ukbd_ref, bd_ref,
             ca_ref, sa_ref, ct_ref, st_ref,
             qa_o, ka_o, va_o, qb_o, kb_o, kat_o, vat_o, kbt_o, qa_raw_o, ka_raw_o, cq_raw_o, ckv_raw_o, ga_o, gb_o):
        _, _, h = _modulated(x_ref[...], mod_ref, nw_ref)
        h = h.astype(MXU)

        def proj(cols):
            return _mm_nt(h, w_ref[cols[0]:cols[1], :])

        ca, sa, ct, st = ca_ref[...], sa_ref[...], ct_ref[...], st_ref[...]
        wide = lambda t, n: jnp.concatenate([t] * n, axis=1)
        qa = proj(E_QA)
        qa_raw_o[...] = qa
        qr = _rope(_head_norm(qa, gq_ref[...], bd_ref, HD), wide(ca, 4), wide(sa, 4), 32) * SCALE2_A
        for hh in range(A_HEADS):
            qa_o[hh] = qr[:, HD * hh:HD * hh + HD].astype(MXU)
        ka = proj(E_KA)
        ka_raw_o[...] = ka
        kr = _rope(_head_norm(ka, gk_ref[...], bd_ref[0:128, 0:128], HD), ca, sa, 32)
        va = proj(E_VA)
        krt, vat = kr.T, va.T
        for g in range(A_KV):
            ka_o[g] = kr[:, HD * g:HD * g + HD].astype(MXU)
            va_o[g] = va[:, HD * g:HD * g + HD].astype(MXU)
            kat_o[g] = krt[HD * g:HD * g + HD, :].astype(MXU)
            vat_o[g] = vat[HD * g:HD * g + HD, :].astype(MXU)
        ga_o[...] = proj(E_GA).astype(MXU)
        gb_o[...] = proj(E_GB).astype(MXU)
        cq = proj(E_CQ)
        cq_raw_o[...] = cq
        qb = _mm_nt(cq * _rms(cq) * qln_ref[...], uq_ref[...])
        q_lat = _mm(qb[:, 0:n_nope], ukbd_ref[...]) * SCALE2_B
        q_rope = _rope(qb[:, n_nope:], wide(ct, 2), wide(st, 2), 32) * SCALE2_B
        for hh in range(B_HEADS):
            qb_o[hh, :, 0:B_KV_LORA] = q_lat[:, B_KV_LORA * hh:B_KV_LORA * (hh + 1)].astype(MXU)
            qb_o[hh, :, B_KV_LORA:B_QK] = q_rope[:, B_ROPE * hh:B_ROPE * (hh + 1)].astype(MXU)
        ckv = proj(E_CKV)
        ckv_raw_o[...] = ckv
        ckv_n = ckv * _rms(ckv) * kvln_ref[...]
        k_rope = _rope(proj(E_KR), ct[:, 0:B_ROPE], st[:, 0:B_ROPE], 32)
        kb_o[0, :, 0:B_KV_LORA] = ckv_n.astype(MXU)
        kb_o[0, :, B_KV_LORA:B_QK] = k_rope.astype(MXU)
        kbt_o[0, 0:B_KV_LORA, :] = ckv_n.T.astype(MXU)
        kbt_o[0, B_KV_LORA:B_QK, :] = k_rope.T.astype(MXU)

    sd = jax.ShapeDtypeStruct
    outs = (sd((A_HEADS, s, HD), MXU), sd((A_KV, s, HD), MXU), sd((A_KV, s, HD), MXU),
            sd((B_HEADS, s, B_QK), MXU), sd((1, s, B_QK), MXU),
            sd((A_KV, HD, s), MXU), sd((A_KV, HD, s), MXU), sd((1, B_QK, s), MXU),
            sd((s, 512), F32), sd((s, 128), F32), sd((s, B_Q_LORA), F32), sd((s, B_KV_LORA), F32),
            sd((s, 512), MXU), sd((s, 512), MXU))
    out_specs = (_head_spec(A_HEADS, tm, HD), _head_spec(A_KV, tm, HD), _head_spec(A_KV, tm, HD),
                 _head_spec(B_HEADS, tm, B_QK), _head_spec(1, tm, B_QK),
                 _headt_spec(A_KV, HD, tm), _headt_spec(A_KV, HD, tm), _headt_spec(1, B_QK, tm),
                 _row_spec(tm, 512), _row_spec(tm, 128), _row_spec(tm, B_Q_LORA), _row_spec(tm, B_KV_LORA),
                 _row_spec(tm, 512), _row_spec(tm, 512))
    consts = [mod, nw, w_in_t, gq, gk, qln, kvln, w_uq_t, uk_bd, bd]
    return pl.pallas_call(
        body, name="even_in_forward", grid=(s // tm,), out_shape=outs,
        in_specs=[_row_spec(tm, d)] + [_full_spec(a.shape) for a in consts] + [_row_spec(tm, 128)] * 4,
        out_specs=out_specs, compiler_params=_params(("parallel",)),
    )(x, *consts, cos_a, sin_a, cos_t, sin_t)


def odd_in_forward(x, mod, nw, w_in):
    s, d = x.shape
    tm = min(ROW_TILE, s)

    def body(x_ref, mod_ref, nw_ref, w_ref, q_o, k_o, v_o, kt_o, vt_o, g_o):
        _, _, h = _modulated(x_ref[...], mod_ref, nw_ref)
        h = h.astype(MXU)

        def proj(cols):
            return _mm_nt(h, w_ref[cols[0]:cols[1], :])

        q = proj(O_Q) * SCALE2_A
        for hh in range(C_HEADS):
            q_o[hh] = q[:, HD * hh:HD * hh + HD].astype(MXU)
        k = proj(O_K)
        v = proj(O_V)
        for g in range(C_KV):
            kh = k[:, HD * g:HD * g + HD]
            vh = v[:, HD * g:HD * g + HD]
            k_o[g] = kh.astype(MXU)
            v_o[g] = vh.astype(MXU)
            kt_o[g] = kh.T.astype(MXU)
            vt_o[g] = vh.T.astype(MXU)
        g_o[...] = proj(O_G).astype(MXU)

    sd = jax.ShapeDtypeStruct
    return pl.pallas_call(
        body, name="odd_in_forward", grid=(s // tm,),
        out_shape=(sd((C_HEADS, s, HD), MXU), sd((C_KV, s, HD), MXU), sd((C_KV, s, HD), MXU),
                   sd((C_KV, HD, s), MXU), sd((C_KV, HD, s), MXU), sd((s, 1024), MXU)),
        in_specs=[_row_spec(tm, d), _full_spec(mod.shape), _full_spec(nw.shape), _full_spec(w_in.shape)],
        out_specs=(_head_spec(C_HEADS, tm, HD), _head_spec(C_KV, tm, HD), _head_spec(C_KV, tm, HD),
                   _headt_spec(C_KV, HD, tm), _headt_spec(C_KV, HD, tm), _row_spec(tm, 1024)),
        compiler_params=_params(("parallel",)),
    )(x, mod, nw, w_in)


def latent_out_forward(o_lat, w_uv):
    s = o_lat.shape[0]
    tm = min(ROW_TILE, s)

    def body(o_ref, uv_ref, out_ref):
        for hh in range(B_HEADS):
            out_ref[:, HD * hh:HD * hh + HD] = _mm(o_ref[:, B_KV_LORA * hh:B_KV_LORA * (hh + 1)],
                                                   uv_ref[hh]).astype(MXU)

    return pl.pallas_call(
        body, name="latent_out_forward", grid=(s // tm,),
        out_shape=jax.ShapeDtypeStruct((s, B_HEADS * HD), MXU),
        in_specs=[_row_spec(tm, o_lat.shape[1]), _full_spec(w_uv.shape)],
        out_specs=_row_spec(tm, B_HEADS * HD),
        compiler_params=_params(("parallel",)),
    )(o_lat, w_uv)


def mixer_out_forward(x, mod, pairs, w_out, name, loss=None):
    s, d = x.shape
    tm = min(ROW_TILE, s)
    n = len(pairs)
    widths = [o.shape[1] for o, _ in pairs]
    head = loss is not None

    def body(*refs):
        x_ref, mod_ref, w_ref = refs[:3]
        pr = refs[3:3 + 2 * n]
        rest = refs[3 + 2 * n:]
        y = jnp.zeros((tm, d), F32)
        r0 = 0
        for i in range(n):
            mix = pr[2 * i][...].astype(F32) * _silu(pr[2 * i + 1][...].astype(F32))
            y = y + _mm(mix, w_ref[r0:r0 + widths[i], :])
            r0 += widths[i]
        x_out = x_ref[...] + mod_ref[2:3, :] * y
        if not head:
            xo_ref, y_ref = rest
            xo_ref[...] = x_out
        else:
            t_ref, fn_ref, dx_ref, y_ref, lp_ref, dw_ref = rest

            @pl.when(pl.program_id(0) == 0)
            def _():
                lp_ref[...] = jnp.zeros(lp_ref.shape, F32)
                dw_ref[...] = jnp.zeros(dw_ref.shape, F32)

            g = fn_ref[...]
            err = x_out * _rms(x_out) * g - t_ref[...]
            lp_ref[...] += jnp.sum(err * err, axis=0, keepdims=True)
            dx, dg = _rms_bwd(err * (1.0 / d), x_out, g)
            dx_ref[...] = dx
            dw_ref[...] += jnp.sum(dg, axis=0, keepdims=True)
        y_ref[...] = y.astype(y_ref.dtype)

    flat = [a for p in pairs for a in p]
    sd = jax.ShapeDtypeStruct
    in_specs = [_row_spec(tm, d), _full_spec(mod.shape), _full_spec(w_out.shape)]
    in_specs += [_row_spec(tm, a.shape[1]) for a in flat]
    out_shape = (sd((s, d), F32), sd((s, d), MXU))
    out_specs = (_row_spec(tm, d), _row_spec(tm, d))
    if head:
        in_specs += [_row_spec(tm, d), _full_spec(loss[1].shape)]
        out_shape += (sd((1, d), F32), sd((1, d), F32))
        out_specs += (_full_spec((1, d)), _full_spec((1, d)))
    return pl.pallas_call(
        body, name=name, grid=(s // tm,), out_shape=out_shape, in_specs=in_specs, out_specs=out_specs,
        compiler_params=_params(("arbitrary",) if head else ("parallel",)),
    )(x, mod, w_out, *flat, *(loss if head else ()))


ONES_ROWS = 16
AHEAD = 2


def _col_max8(s3):
    m8 = jnp.max(s3, axis=0)
    return jnp.broadcast_to(jnp.max(m8, axis=0, keepdims=True), m8.shape)


def _with_ones(vt, n):
    return jnp.concatenate([vt, jnp.ones((ONES_ROWS, n), vt.dtype)], axis=0)


def _grid_edges(grid):
    ids = [pl.program_id(a) for a in range(len(grid))]
    first = functools.reduce(jnp.logical_and, [i == 0 for i in ids])
    last = functools.reduce(jnp.logical_and, [i == n - 1 for i, n in zip(ids, grid)])
    return first, last


def flash_forward(q, k, vt, *, dv, tq, tk, nsub, name, exchange=None):
    hq, s, dq = q.shape
    g_kv = k.shape[0]
    hpg = hq // g_kv
    nq = s // tq
    tkk = tk * nsub
    nk = s // tkk
    grid = (g_kv, nq, nk)
    hosted = exchange is not None
    m_cols = hpg * tq
    dvp = dv + ONES_ROWS

    def body(*refs):
        nx = exchange.n if hosted else 0
        q_ref, k_ref, vt_ref = refs[:3]
        xs_refs = refs[3:3 + nx]
        o_ref, lse_ref = refs[3 + nx:5 + nx]
        land_refs = refs[5 + nx:5 + 2 * nx]
        m_s, acc_s = refs[5 + 2 * nx:7 + 2 * nx]
        sems = refs[7 + 2 * nx:]
        if hosted:
            first, last = _grid_edges(grid)
            pl.when(first)(lambda: exchange.start(xs_refs, land_refs, sems))
        j = pl.program_id(2)

        @pl.when(j == 0)
        def _():
            m_s[...] = jnp.full((8, m_cols), -jnp.inf, F32)
            acc_s[...] = jnp.zeros((dvp, m_cols), F32)

        qq = q_ref[...].reshape(m_cols, dq)
        score = lambda u: _mm_nt(k_ref[0, tk * u:tk * (u + 1), :], qq).reshape(tk // 8, 8, m_cols)
        sts = {u: score(u) for u in range(min(AHEAD, nsub))}
        m_run = m_s[...]
        acc = acc_s[...]
        for u in range(nsub):
            if u + AHEAD < nsub:
                sts[u + AHEAD] = score(u + AHEAD)
            st = sts.pop(u)
            m_new = jnp.maximum(m_run, _col_max8(st))
            p = jnp.exp2(st - m_new[None])
            alpha = jnp.exp2(m_run - m_new)
            pv = _mm(_with_ones(vt_ref[0, 0:dv, tk * u:tk * (u + 1)], tk), p.reshape(tk, m_cols))
            acc = (acc.reshape(dvp // 8, 8, m_cols) * alpha[None]).reshape(dvp, m_cols) + pv
            m_run = m_new
        acc_s[...] = acc
        m_s[...] = m_run

        @pl.when(j == nk - 1)
        def _():
            l = acc_s[dv:dv + 1, :]
            ot = acc_s[0:dv, :] / l
            lse = m_s[0:1, :] + jnp.log2(l)
            for hh in range(hpg):
                o_ref[:, dv * hh:dv * hh + dv] = ot[:, tq * hh:tq * hh + tq].T.astype(MXU)
                lse_ref[hh] = lse[:, tq * hh:tq * hh + tq]

        if hosted:
            pl.when(last)(lambda: exchange.wait(xs_refs, land_refs, sems))

    sd = jax.ShapeDtypeStruct
    return pl.pallas_call(
        body, name=name, grid=grid,
        out_shape=(sd((s, hq * dv), MXU), sd((hq, 1, s), F32)) + (exchange.land_shapes if hosted else ()),
        in_specs=[pl.BlockSpec((hpg, tq, dq), lambda g, i, j: (g, i, 0)),
                  pl.BlockSpec((1, tkk, k.shape[2]), lambda g, i, j: (g, j, 0)),
                  pl.BlockSpec((1, dv, tkk), lambda g, i, j: (g, 0, j))] + (exchange.in_specs if hosted else []),
        out_specs=(pl.BlockSpec((tq, hpg * dv), lambda g, i, j: (i, g)),
                   pl.BlockSpec((hpg, 1, tq), lambda g, i, j: (g, 0, i))) + (exchange.out_specs if hosted else ()),
        scratch_shapes=[pltpu.VMEM((8, m_cols), F32), pltpu.VMEM((dvp, m_cols), F32)]
        + (list(exchange.sems) if hosted else []),
        compiler_params=_params(("arbitrary",) * 3 if hosted else ("parallel", "parallel", "arbitrary")),
    )(q, k, vt, *(exchange.srcs if hosted else []))


def _window_bias_t(hpg, slope_ref):
    t = WINDOW
    r = lax.broadcasted_iota(jnp.int32, (3 * t, t), 0)
    cq = lax.broadcasted_iota(jnp.int32, (3 * t, t), 1)
    arel = jnp.abs(r - t - cq)
    base = jnp.where(arel <= WINDOW, arel.astype(F32) * (-LOG2E), -jnp.inf)
    return jnp.concatenate([base * slope_ref[hh] for hh in range(hpg)], axis=1)


def _window_edges_t(bias, no_before, no_after):
    t = WINDOW
    r = lax.broadcasted_iota(jnp.int32, bias.shape, 0)
    out = ((r < t) & no_before) | ((r >= 2 * t) & no_after)
    return jnp.where(out, -jnp.inf, bias)


def _window_specs(kind, nb, nblk, d):
    t = WINDOW
    before = lambda i: jnp.clip(i * nb - 1, 0, nblk - 1)
    after = lambda i: jnp.clip((i + 1) * nb, 0, nblk - 1)
    if kind == "rows":
        return [pl.BlockSpec((1, t, d), lambda g, i: (g, before(i), 0)),
                pl.BlockSpec((1, nb * t, d), lambda g, i: (g, i, 0)),
                pl.BlockSpec((1, t, d), lambda g, i: (g, after(i), 0))]
    return [pl.BlockSpec((1, d, t), lambda g, i: (g, 0, before(i))),
            pl.BlockSpec((1, d, nb * t), lambda g, i: (g, 0, i)),
            pl.BlockSpec((1, d, t), lambda g, i: (g, 0, after(i)))]


def window_forward(q, k, vt, sink2, slopes, nb, name):
    hq, s, d = q.shape
    g_kv = k.shape[0]
    hpg = hq // g_kv
    t = WINDOW
    nblk = s // t
    steps = nblk // nb
    m_cols = hpg * t

    def body(q_ref, kp, ko, kn, vp, vo, vn, sink_ref, slope_ref, o_ref, lse_ref):
        i = pl.program_id(1)
        kk_all = jnp.concatenate([kp[0], ko[0], kn[0]], axis=0)
        vt_all = jnp.concatenate([vp[0], vo[0], vn[0]], axis=1)
        bias = _window_bias_t(hpg, slope_ref)
        sink_row = jnp.concatenate([jnp.broadcast_to(sink_ref[hh], (8, t)) for hh in range(hpg)], axis=1)
        sts = {}

        def score(u):
            qq = q_ref[:, t * u:t * (u + 1), :].reshape(m_cols, d)
            b_u = bias
            if u == 0 or u == nb - 1:
                b_u = _window_edges_t(bias, (i == 0) if u == 0 else False,
                                      (i == steps - 1) if u == nb - 1 else False)
            sts[u] = _mm_nt(kk_all[t * u:t * (u + 3), :], qq) + b_u

        for u in range(min(AHEAD, nb)):
            score(u)
        for u in range(nb):
            if u + AHEAD < nb:
                score(u + AHEAD)
            s3 = sts.pop(u).reshape(3 * t // 8, 8, m_cols)
            m8 = jnp.maximum(_col_max8(s3), sink_row)
            p = jnp.exp2(s3 - m8[None]).reshape(3 * t, m_cols)
            acc = _mm(_with_ones(vt_all[:, t * u:t * (u + 3)], 3 * t), p)
            l = acc[d:d + 1, :] + jnp.exp2(sink_row[0:1, :] - m8[0:1, :])
            ot = acc[0:d, :] / l
            lse = m8[0:1, :] + jnp.log2(l)
            for hh in range(hpg):
                o_ref[t * u:t * (u + 1), d * hh:d * hh + d] = ot[:, t * hh:t * hh + t].T.astype(MXU)
                lse_ref[hh, :, t * u:t * (u + 1)] = lse[:, t * hh:t * hh + t]

    sd = jax.ShapeDtypeStruct
    return pl.pallas_call(
        body, name=name, grid=(g_kv, steps),
        out_shape=(sd((s, hq * d), MXU), sd((hq, 1, s), F32)),
        in_specs=[pl.BlockSpec((hpg, nb * t, d), lambda g, i: (g, i, 0))]
        + _window_specs("rows", nb, nblk, d) + _window_specs("cols", nb, nblk, d)
        + [pl.BlockSpec((hpg, 1, 1), lambda g, i: (g, 0, 0))] * 2,
        out_specs=(pl.BlockSpec((nb * t, hpg * d), lambda g, i: (i, g)),
                   pl.BlockSpec((hpg, 1, nb * t), lambda g, i: (g, 0, i))),
        compiler_params=_params(("parallel", "parallel")),
    )(q, k, k, k, vt, vt, vt, sink2, slopes)


def window_backward(q, k, kt, v, do, lse, delta, slopes, nb, name):
    hq, s, d = q.shape
    g_kv = k.shape[0]
    hpg = hq // g_kv
    t = WINDOW
    nblk = s // t
    steps = nblk // nb
    m_cols = hpg * t

    def body(q_ref, kp, ko, kn, ktp, kto, ktn, vp, vo, vn, do_ref, lse_ref, dl_ref, slope_ref,
             dq_ref, dk_ref, dv_ref, dk_s, dv_s):
        i = pl.program_id(1)

        @pl.when(i == 0)
        def _():
            dk_ref[...] = jnp.zeros(dk_ref.shape, F32)
            dv_ref[...] = jnp.zeros(dv_ref.shape, F32)

        dk_s[...] = jnp.zeros(dk_s.shape, F32)
        dv_s[...] = jnp.zeros(dv_s.shape, F32)
        kk_all = jnp.concatenate([kp[0], ko[0], kn[0]], axis=0)
        vv_all = jnp.concatenate([vp[0], vo[0], vn[0]], axis=0)
        kkt_all = jnp.concatenate([ktp[0], kto[0], ktn[0]], axis=1)
        bias = _window_bias_t(hpg, slope_ref)
        qqs, dds, sts, dps = {}, {}, {}, {}

        def issue(u):
            rows = slice(t * u, t * (u + 1))
            keys = slice(t * u, t * (u + 3))
            qqs[u] = q_ref[:, rows, :].reshape(m_cols, d)
            dds[u] = jnp.concatenate([do_ref[rows, d * hh:d * hh + d] for hh in range(hpg)], axis=0)
            b_u = bias
            if u == 0 or u == nb - 1:
                b_u = _window_edges_t(bias, (i == 0) if u == 0 else False,
                                      (i == steps - 1) if u == nb - 1 else False)
            sts[u] = _mm_nt(kk_all[keys, :], qqs[u]) + b_u
            dps[u] = _mm_nt(vv_all[keys, :], dds[u])

        for u in range(min(AHEAD, nb)):
            issue(u)
        for u in range(nb):
            if u + AHEAD < nb:
                issue(u + AHEAD)
            rows = slice(t * u, t * (u + 1))
            keys = slice(t * u, t * (u + 3))
            lse_row = jnp.concatenate([lse_ref[hh, :, rows] for hh in range(hpg)], axis=1)
            dl_row = jnp.concatenate([dl_ref[hh, :, rows] for hh in range(hpg)], axis=1)
            p = jnp.exp2(sts[u] - lse_row)
            ds = p * (dps[u] - dl_row) * SCALE_A
            dv_s[keys, :] += _mm(p, dds[u])
            dk_s[keys, :] += _mm(ds, qqs[u])
            dqt = _mm(kkt_all[:, keys], ds)
            for hh in range(hpg):
                dq_ref[rows, d * hh:d * hh + d] = dqt[:, t * hh:t * hh + t].T.astype(dq_ref.dtype)
        tq = nb * t
        for src, r0, n in ((0, jnp.clip(i * nb - 1, 0, nblk - 1) * t, t), (t, i * tq, tq),
                           (t + tq, jnp.clip((i + 1) * nb, 0, nblk - 1) * t, t)):
            dst = pl.ds(pl.multiple_of(r0, t), n)
            dk_ref[0, dst, :] += dk_s[src:src + n, :] * (1.0 / SCALE2_A)
            dv_ref[0, dst, :] += dv_s[src:src + n, :]

    row_map = lambda g, i: (g, 0, i)
    sd = jax.ShapeDtypeStruct
    return pl.pallas_call(
        body, name=name, grid=(g_kv, steps),
        out_shape=(sd((s, hq * d), MXU), sd((g_kv, s, d), F32), sd((g_kv, s, d), F32)),
        in_specs=[pl.BlockSpec((hpg, nb * t, d), lambda g, i: (g, i, 0))]
        + _window_specs("rows", nb, nblk, d) + _window_specs("cols", nb, nblk, d) + _window_specs("rows", nb, nblk, d)
        + [pl.BlockSpec((nb * t, hpg * d), lambda g, i: (i, g)), pl.BlockSpec((hpg, 1, nb * t), row_map),
           pl.BlockSpec((hpg, 1, nb * t), row_map), pl.BlockSpec((hpg, 1, 1), lambda g, i: (g, 0, 0))],
        out_specs=(pl.BlockSpec((nb * t, hpg * d), lambda g, i: (i, g)),
                   pl.BlockSpec((1, s, d), lambda g, i: (g, 0, 0)),
                   pl.BlockSpec((1, s, d), lambda g, i: (g, 0, 0))),
        scratch_shapes=[pltpu.VMEM(((nb + 2) * t, d), F32), pltpu.VMEM(((nb + 2) * t, d), F32)],
        compiler_params=_params(("parallel", "arbitrary")),
    )(q, k, k, k, kt, kt, kt, v, v, v, do, lse, delta, slopes)


def flash_backward(q, k, kt, v, do, lse, delta, *, scale, dv, tq, tk, nsub, gq, name, split=None, exchange=None):
    hq, s, dq = q.shape
    g_kv = k.shape[0]
    hpg = hq // gq
    nq = s // tq
    tqq = tq * nsub
    nqs = s // tqq
    nkb = s // tk
    grid = (gq, nkb, nqs)
    hosted = exchange is not None
    m_cols = hpg * tq
    c = scale * LOG2E
    has_v = v is not None

    def body(*refs):
        it = iter(refs)
        q_ref, k_ref, kt_ref = next(it), next(it), next(it)
        v_ref = next(it) if has_v else None
        do_ref, lse_ref, dl_ref = next(it), next(it), next(it)
        nx = exchange.n if hosted else 0
        xs_refs = [next(it) for _ in range(nx)]
        dq_ref, dk_ref, dv_ref = next(it), next(it), next(it)
        land_refs = [next(it) for _ in range(nx)]
        dqt_s = next(it)
        sems = list(it)
        kj = pl.program_id(1)
        qi = pl.program_id(2)
        if hosted:
            first, last = _grid_edges(grid)
            pl.when(first)(lambda: exchange.start(xs_refs, land_refs, sems))

        @pl.when((kj == 0) & (qi == 0))
        def _():
            dqt_s[...] = jnp.zeros(dqt_s.shape, F32)

        @pl.when(qi == 0)
        def _():
            dk_ref[...] = jnp.zeros(dk_ref.shape, F32)
            dv_ref[...] = jnp.zeros(dv_ref.shape, F32)

        kk = k_ref[0]
        vv = v_ref[0] if has_v else kk[:, :dv]
        qqs, dds, sts, dps = {}, {}, {}, {}

        def issue(u):
            rows = slice(tq * u, tq * (u + 1))
            qqs[u] = q_ref[:, rows, :].reshape(m_cols, dq)
            dds[u] = jnp.concatenate([do_ref[rows, dv * hh:dv * hh + dv] for hh in range(hpg)], axis=0)
            sts[u] = _mm_nt(kk, qqs[u])
            dps[u] = _mm_nt(vv, dds[u])

        for u in range(min(AHEAD, nsub)):
            issue(u)
        dv_acc = dv_ref[0]
        dk_acc = dk_ref[0]
        for u in range(nsub):
            if u + AHEAD < nsub:
                issue(u + AHEAD)
            rows = slice(tq * u, tq * (u + 1))
            lse_row = jnp.concatenate([lse_ref[hh, :, rows] for hh in range(hpg)], axis=1)
            dl_row = jnp.concatenate([dl_ref[hh, :, rows] for hh in range(hpg)], axis=1)
            p = jnp.exp2(sts[u] - lse_row)
            ds = p * (dps[u] - dl_row) * scale
            dv_acc = dv_acc + _mm(p, dds[u])
            dk_acc = dk_acc + _mm(ds, qqs[u])
            dqt = _mm(kt_ref[0], ds)
            for hh in range(hpg):
                dqt_s[qi * nsub + u, dq * hh:dq * hh + dq, :] += dqt[:, tq * hh:tq * hh + tq]
        dv_ref[0] = dv_acc
        dk_ref[0] = jnp.where(qi == nqs - 1, dk_acc * (1.0 / c), dk_acc)

        @pl.when((kj == nkb - 1) & (qi == nqs - 1))
        def _():
            def emit(t, carry):
                r0 = pl.multiple_of(t * tq, tq)
                for hh in range(hpg):
                    blk = dqt_s[t, dq * hh:dq * hh + dq, :].T
                    if split is None:
                        dq_ref[pl.ds(r0, tq), dq * hh:dq * hh + dq] = blk
                    else:
                        rest = dq - split
                        dq_ref[pl.ds(r0, tq), split * hh:split * (hh + 1)] = blk[:, 0:split]
                        dq_ref[pl.ds(r0, tq), hpg * split + rest * hh:hpg * split + rest * (hh + 1)] = blk[:, split:]
                return carry

            lax.fori_loop(0, nq, emit, 0)

        if hosted:
            pl.when(last)(lambda: exchange.wait(xs_refs, land_refs, sems))

    kv_of = lambda g: g * g_kv // gq
    in_specs = [pl.BlockSpec((hpg, tqq, dq), lambda g, kj, qi: (g, qi, 0)),
                pl.BlockSpec((1, tk, dq), lambda g, kj, qi: (kv_of(g), kj, 0)),
                pl.BlockSpec((1, dq, tk), lambda g, kj, qi: (kv_of(g), 0, kj))]
    args = [q, k, kt]
    if has_v:
        in_specs.append(pl.BlockSpec((1, tk, dv), lambda g, kj, qi: (kv_of(g), kj, 0)))
        args.append(v)
    row_map = lambda g, kj, qi: (g, 0, qi)
    in_specs += [pl.BlockSpec((tqq, hpg * dv), lambda g, kj, qi: (qi, g)),
                 pl.BlockSpec((hpg, 1, tqq), row_map), pl.BlockSpec((hpg, 1, tqq), row_map)]
    args += [do, lse, delta]
    if hosted:
        in_specs += exchange.in_specs
        args += exchange.srcs
    sd = jax.ShapeDtypeStruct
    return pl.pallas_call(
        body, name=name, grid=grid,
        out_shape=(sd((s, hq * dq), F32), sd((gq, s, dq), F32), sd((gq, s, dv), F32))
        + (exchange.land_shapes if hosted else ()),
        in_specs=in_specs,
        out_specs=(pl.BlockSpec((s, hpg * dq), lambda g, kj, qi: (0, g)),
                   pl.BlockSpec((1, tk, dq), lambda g, kj, qi: (g, kj, 0)),
                   pl.BlockSpec((1, tk, dv), lambda g, kj, qi: (g, kj, 0))) + (exchange.out_specs if hosted else ()),
        scratch_shapes=[pltpu.VMEM((nq, hpg * dq, tq), F32)] + (list(exchange.sems) if hosted else []),
        compiler_params=_params(("arbitrary",) * 3 if hosted else ("parallel", "arbitrary", "arbitrary")),
    )(*args)


def mixer_out_backward(dx, y, mod, pairs, w_out, delta_heads, name, lse=None, sink=None):
    s, d = dx.shape
    tm = min(ROW_TILE, s)
    n = len(pairs)
    widths = [o.shape[1] for o, _ in pairs]
    n_delta = sum(1 for h in delta_heads if h)
    with_sink = lse is not None

    def body(*refs):
        it = iter(refs)
        dx_ref, y_ref, mod_ref, wt_ref = next(it), next(it), next(it), next(it)
        pr = [next(it) for _ in range(2 * n)]
        lse_ref = next(it) if with_sink else None
        sink_ref = next(it) if with_sink else None
        outs = [next(it) for _ in range(2 * n)]
        dl_refs = [next(it) for _ in range(n_delta)]
        dgate_ref, dw_ref = next(it), next(it)
        dsink_ref = next(it) if with_sink else None
        dw_acc = next(it)

        @pl.when(pl.program_id(0) == 0)
        def _():
            dgate_ref[...] = jnp.zeros(dgate_ref.shape, F32)
            dw_acc[...] = jnp.zeros(dw_acc.shape, F32)
            if with_sink:
                dsink_ref[...] = jnp.zeros(dsink_ref.shape, F32)

        dxo = dx_ref[...]
        dgate_ref[...] += jnp.sum(dxo * y_ref[...].astype(F32), axis=0, keepdims=True)
        dy = (dxo * mod_ref[2:3, :]).astype(MXU)
        dmix = _mm_nt(dy, wt_ref[...])
        r0 = 0
        di = 0
        for i in range(n):
            o = pr[2 * i][...].astype(F32)
            g = pr[2 * i + 1][...].astype(F32)
            dm = dmix[:, r0:r0 + widths[i]]
            sg = _sigmoid(g)
            act = g * sg
            do = dm * act
            outs[2 * i][...] = do.astype(MXU)
            outs[2 * i + 1][...] = (dm * o * (sg * (1.0 + g * (1.0 - sg)))).astype(MXU)
            dw_acc[r0:r0 + widths[i], :] += _mm_tn(o * act, dy)
            if delta_heads[i]:
                dlt = _group_sums_t(do * o, HD)[0:delta_heads[i], :]
                dl_refs[di][...] = dlt
                if with_sink:
                    ps = jnp.exp2(sink_ref[...] - lse_ref[...])
                    dsink_ref[...] += -jnp.sum(ps * dlt, axis=1, keepdims=True)
                di += 1
            r0 += widths[i]

        @pl.when(pl.program_id(0) == pl.num_programs(0) - 1)
        def _():
            for j in range(N_DEV):
                dw_ref[j] = dw_acc[j * dw_block:(j + 1) * dw_block, :].astype(MXU)

    dw_block = sum(widths) // N_DEV
    flat = [a for p in pairs for a in p]
    sd = jax.ShapeDtypeStruct
    in_specs = [_row_spec(tm, d), _row_spec(tm, d), _full_spec(mod.shape), _full_spec(w_out.shape)]
    in_specs += [_row_spec(tm, a.shape[1]) for a in flat]
    args = [dx, y, mod, w_out] + flat
    if with_sink:
        nh = lse.shape[0]
        in_specs += [_rows_spec(nh, tm), _full_spec(sink.shape)]
        args += [lse, sink]
    out_shape = [sd((s, a.shape[1]), MXU) for a in flat]
    out_specs = [_row_spec(tm, a.shape[1]) for a in flat]
    for h in delta_heads:
        if h:
            out_shape.append(sd((h, s), F32))
            out_specs.append(_rows_spec(h, tm))
    out_shape += [sd((1, d), F32), sd((N_DEV, dw_block, d), MXU)]
    out_specs += [_full_spec((1, d)), _full_spec((N_DEV, dw_block, d))]
    if with_sink:
        out_shape.append(sd((lse.shape[0], 1), F32))
        out_specs.append(_full_spec((lse.shape[0], 1)))
    return pl.pallas_call(
        body, name=name, grid=(s // tm,), out_shape=tuple(out_shape), in_specs=in_specs, out_specs=tuple(out_specs),
        scratch_shapes=[pltpu.VMEM((sum(widths), d), F32)], compiler_params=_params(("arbitrary",)),
    )(*args)


def latent_out_backward(d_ob, o_lat, w_uv):
    s = o_lat.shape[0]
    tm = min(ROW_TILE, s)

    def body(d_ref, o_ref, uv_ref, dol_ref, dl_ref, duv_ref, prod_s):
        @pl.when(pl.program_id(0) == 0)
        def _():
            duv_ref[...] = jnp.zeros(duv_ref.shape, F32)

        for hh in range(B_HEADS):
            dh = d_ref[:, HD * hh:HD * hh + HD]
            ol = o_ref[:, B_KV_LORA * hh:B_KV_LORA * (hh + 1)].astype(F32)
            dol = _mm_nt(dh, uv_ref[hh])
            dol_ref[:, B_KV_LORA * hh:B_KV_LORA * (hh + 1)] = dol.astype(MXU)
            prod_s[:, B_KV_LORA * hh:B_KV_LORA * (hh + 1)] = dol * ol
            duv_ref[:, HD * hh:HD * hh + HD] += _mm_tn(ol, dh)
        dl_ref[...] = _group_sums_t(prod_s[...], B_KV_LORA)[0:B_HEADS, :]

    sd = jax.ShapeDtypeStruct
    duv_shape = (B_KV_LORA, B_HEADS * HD)
    return pl.pallas_call(
        body, name="latent_out_backward", grid=(s // tm,),
        out_shape=(sd(o_lat.shape, MXU), sd((B_HEADS, s), F32), sd(duv_shape, F32)),
        in_specs=[_row_spec(tm, d_ob.shape[1]), _row_spec(tm, o_lat.shape[1]), _full_spec(w_uv.shape)],
        out_specs=(_row_spec(tm, o_lat.shape[1]), _rows_spec(B_HEADS, tm), _full_spec(duv_shape)),
        scratch_shapes=[pltpu.VMEM((tm, o_lat.shape[1]), F32)],
        compiler_params=_params(("arbitrary",)),
    )(d_ob, o_lat, w_uv)


def even_prep_backward(dqa, dka, dva, dqb, dkb, dvb, qa_raw, ka_raw, cq_raw, ckv_raw,
                       gq, gk, qln, kvln, w_uq_t, uk_bd, bd, cos_a, sin_a, cos_t, sin_t):
    s = qa_raw.shape[0]
    tm = min(ROW_TILE, s)
    half_lat = B_KV_LORA * B_HEADS // 2
    half_w = dqb.shape[1] // 2

    def body(dqa_ref, dka_ref, dva_ref, dqb_ref, dkb_ref, dvb_ref, qa_ref, ka_ref, cq_ref, ckv_ref,
             gq_ref, gk_ref, qln_ref, kvln_ref, uqt_ref, ukbd_ref, bd_ref, ca_ref, sa_ref, ct_ref, st_ref,
             pqa, pka, pva, pcq, pckv, pkr, gqn, gkn, gqln, gkvln, guq, guk):
        @pl.when(pl.program_id(0) == 0)
        def _():
            for r in (gqn, gkn, gqln, gkvln, guq, guk):
                r[...] = jnp.zeros(r.shape, F32)

        ca, sa, ct, st = ca_ref[...], sa_ref[...], ct_ref[...], st_ref[...]
        wide = lambda t, n: jnp.concatenate([t] * n, axis=1)
        rows = lambda a: jnp.sum(a, axis=0, keepdims=True)
        dx, dg = _head_norm_bwd(_rope_t(dqa_ref[...], wide(ca, 4), wide(sa, 4), 32), qa_ref[...], gq_ref[...],
                                bd_ref, HD)
        pqa[...] = dx.astype(MXU)
        gqn[...] += rows(dg)
        dk_all = jnp.concatenate([dka_ref[g] for g in range(A_KV)], axis=1)
        dx, dg = _head_norm_bwd(_rope_t(dk_all, ca, sa, 32), ka_ref[...], gk_ref[...], bd_ref[0:128, 0:128], HD)
        pka[...] = dx.astype(MXU)
        gkn[...] += rows(dg)
        pva[...] = jnp.concatenate([dva_ref[g] for g in range(A_KV)], axis=1).astype(MXU)
        cq_raw = cq_ref[...]
        cq_n = cq_raw * _rms(cq_raw) * qln_ref[...]
        qb = _mm_nt(cq_n, uqt_ref[...])
        d_lat = jnp.concatenate([dqb_ref[:, 0:half_lat], dqb_ref[:, half_w:half_w + half_lat]], axis=1)
        d_rope = jnp.concatenate([dqb_ref[:, half_lat:half_w], dqb_ref[:, half_w + half_lat:]], axis=1)
        for hh in range(B_HEADS):
            guk[:, B_NOPE * hh:B_NOPE * (hh + 1)] += _mm_tn(d_lat[:, B_KV_LORA * hh:B_KV_LORA * (hh + 1)],
                                                            qb[:, B_NOPE * hh:B_NOPE * (hh + 1)])
        dqb_all = jnp.concatenate([_mm_nt(d_lat, ukbd_ref[...]),
                                   _rope_t(d_rope, wide(ct, 2), wide(st, 2), 32)], axis=1)
        guq[...] += _mm_tn(dqb_all, cq_n)
        dx, dg = _rms_bwd(_mm(dqb_all, uqt_ref[...]), cq_raw, qln_ref[...])
        pcq[...] = dx.astype(MXU)
        gqln[...] += rows(dg)
        dkb_sum = dkb_ref[0] + dkb_ref[1]
        dckv = dkb_sum[:, 0:B_KV_LORA] + dvb_ref[0] + dvb_ref[1]
        dx, dg = _rms_bwd(dckv, ckv_ref[...], kvln_ref[...])
        pckv[...] = dx.astype(MXU)
        gkvln[...] += rows(dg)
        pkr[...] = _rope_t(dkb_sum[:, B_KV_LORA:B_QK], ct[:, 0:B_ROPE], st[:, 0:B_ROPE], 32).astype(MXU)

    sd = jax.ShapeDtypeStruct
    consts = [gq, gk, qln, kvln, w_uq_t, uk_bd, bd]
    in_specs = [_row_spec(tm, 512), _head_spec(A_KV, tm, HD), _head_spec(A_KV, tm, HD),
                _row_spec(tm, dqb.shape[1]), _head_spec(2, tm, B_QK), _head_spec(2, tm, B_KV_LORA),
                _row_spec(tm, 512), _row_spec(tm, 128), _row_spec(tm, B_Q_LORA), _row_spec(tm, B_KV_LORA)]
    in_specs += [_full_spec(a.shape) for a in consts] + [_row_spec(tm, 128)] * 4
    small = [sd(gq.shape, F32), sd(gk.shape, F32), sd(qln.shape, F32), sd(kvln.shape, F32), sd(w_uq_t.shape, F32),
             sd((B_KV_LORA, B_HEADS * B_NOPE), F32)]
    out_shape = (sd((s, 512), MXU), sd((s, 128), MXU), sd((s, 128), MXU), sd((s, B_Q_LORA), MXU),
                 sd((s, B_KV_LORA), MXU), sd((s, B_ROPE), MXU), *small)
    out_specs = (_row_spec(tm, 512), _row_spec(tm, 128), _row_spec(tm, 128), _row_spec(tm, B_Q_LORA),
                 _row_spec(tm, B_KV_LORA), _row_spec(tm, B_ROPE), *[_full_spec(a.shape) for a in small])
    return pl.pallas_call(
        body, name="even_prep_backward", grid=(s // tm,), out_shape=out_shape, in_specs=in_specs, out_specs=out_specs,
        compiler_params=_params(("arbitrary",)),
    )(dqa, dka, dva, dqb, dkb, dvb, qa_raw, ka_raw, cq_raw, ckv_raw, *consts, cos_a, sin_a, cos_t, sin_t)


def in_proj_backward(x, mod, nw, pieces, name, *, dx_out=None, w_in_t=None, dw_rows=None, exchange=None):
    s, d = x.shape
    tm = min(ROW_TILE, s)
    grid = (s // tm,)
    n = len(pieces)
    cols = [c for _, c in pieces]
    want_dx = w_in_t is not None
    want_dw = dw_rows is not None
    n_cols = sum(c1 - c0 for c0, c1 in cols)
    dw_block = n_cols // N_DEV
    hosted = exchange is not None
    nx = exchange.n if hosted else 0

    def body(*refs):
        it = iter(refs)
        x_ref, mod_ref, nw_ref = next(it), next(it), next(it)
        dxo_ref, wt_ref = (next(it), next(it)) if want_dx else (None, None)
        p_refs = [next(it) for _ in range(n)]
        xs_refs = [next(it) for _ in range(nx)]
        dx_ref, dv_ref = (next(it), next(it)) if want_dx else (None, None)
        dw_ref = next(it) if want_dw else None
        land_refs = [next(it) for _ in range(nx)]
        acc_ref = next(it) if want_dx else None
        dw_acc = next(it) if want_dw else None
        sems = list(it)
        first, last = _grid_edges(grid)
        if hosted:
            pl.when(first)(lambda: exchange.start(xs_refs, land_refs, sems))

        @pl.when(first)
        def _():
            if want_dw:
                dw_acc[...] = jnp.zeros(dw_acc.shape, F32)
            if want_dx:
                acc_ref[...] = jnp.zeros(acc_ref.shape, F32)

        xn, g1, h = _modulated(x_ref[...], mod_ref, nw_ref)
        hb = h.astype(MXU)
        dh = jnp.zeros((tm, d), F32)
        for k, (pr, (c0, c1)) in enumerate(zip(p_refs, cols)):
            if len(pr.shape) == 3:
                pc = jnp.concatenate([pr[g] for g in range(pr.shape[0])], axis=1).astype(MXU)
            else:
                pc = pr[...].astype(MXU)
            if want_dx:
                dh = dh + jnp.dot(pc, wt_ref[c0:c1, :], preferred_element_type=F32)
            if want_dw:
                r0, r1 = dw_rows[k]
                dw_acc[r0:r1, :] += _mm_tn(pc, hb)
        if want_dx:
            acc_ref[0:1, :] += jnp.sum(dh, axis=0, keepdims=True)
            acc_ref[1:2, :] += jnp.sum(dh * xn, axis=0, keepdims=True)
            dxn = dh * g1
            x = x_ref[...]
            dx_ref[...] = dxo_ref[...] + _rms(x) * (dxn - xn * jnp.mean(dxn * xn, axis=-1, keepdims=True))

        @pl.when(last)
        def _():
            if want_dx:
                dg1 = acc_ref[1:2, :]
                dv_ref[0:1, :] = acc_ref[0:1, :]
                dv_ref[1:2, :] = dg1 * nw_ref[...]
                dv_ref[2:3, :] = dg1 * (1.0 + mod_ref[1:2, :])
                dv_ref[3:4, :] = jnp.zeros((1, d), F32)
            if want_dw:
                for j in range(N_DEV):
                    dw_ref[j] = dw_acc[j * dw_block:(j + 1) * dw_block, :].astype(MXU)

        if hosted:
            pl.when(last)(lambda: exchange.wait(xs_refs, land_refs, sems))

    arrs = [a for a, _ in pieces]
    sd = jax.ShapeDtypeStruct
    args = [x, mod, nw] + ([dx_out, w_in_t] if want_dx else []) + arrs + (exchange.srcs if hosted else [])
    in_specs = [_row_spec(tm, d), _full_spec(mod.shape), _full_spec(nw.shape)]
    in_specs += [_row_spec(tm, d), _full_spec(w_in_t.shape)] if want_dx else []
    in_specs += [_row_spec(tm, a.shape[1]) if a.ndim == 2 else _head_spec(a.shape[0], tm, a.shape[2]) for a in arrs]
    in_specs += exchange.in_specs if hosted else []
    out_shape, out_specs, scratch = [], [], []
    if want_dx:
        out_shape += [sd((s, d), F32), sd((4, d), F32)]
        out_specs += [_row_spec(tm, d), _full_spec((4, d))]
        scratch.append(pltpu.VMEM((8, d), F32))
    if want_dw:
        out_shape.append(sd((N_DEV, dw_block, d), MXU))
        out_specs.append(_full_spec((N_DEV, dw_block, d)))
        scratch.append(pltpu.VMEM((n_cols, d), F32))
    if hosted:
        out_shape += list(exchange.land_shapes)
        out_specs += list(exchange.out_specs)
        scratch += list(exchange.sems)
    return pl.pallas_call(
        body, name=name, grid=grid, out_shape=tuple(out_shape), in_specs=in_specs, out_specs=tuple(out_specs),
        scratch_shapes=scratch, compiler_params=_params(("arbitrary",)),
    )(*args)


def ada_weight_grad(c_all, dmod_cols):
    d = c_all.shape[1]
    w = dmod_cols.shape[2]

    def body(c_ref, dm_ref, out_ref):
        ca = _silu(c_ref[...])
        for l in range(2):
            out_ref[l] = _mm_tn(ca, dm_ref[l])

    return pl.pallas_call(
        body, name="ada_weight_grad",
        out_shape=jax.ShapeDtypeStruct((2, d, w), F32),
        compiler_params=pltpu.CompilerParams(vmem_limit_bytes=VMEM_LIMIT),
    )(c_all, dmod_cols)


def _slot_sum(g_ref):
    g = g_ref[0].astype(F32)
    for k in range(1, g_ref.shape[0]):
        g = g + g_ref[k].astype(F32)
    return g


def _adamw_math(g, w, m, v):
    m_new = ADAM_B1 * m + (1.0 - ADAM_B1) * g
    v_new = ADAM_B2 * v + (1.0 - ADAM_B2) * (g * g)
    m_hat = m_new / (1.0 - ADAM_B1 ** ADAM_STEP)
    v_hat = v_new / (1.0 - ADAM_B2 ** ADAM_STEP)
    return -ADAM_LR * (m_hat / (jnp.sqrt(v_hat) + ADAM_EPS) + ADAM_WD * w), m_new, v_new


def adamw_small(g_alls, ws, ms, vs, loss_all):
    n = len(ws)

    def body(*refs):
        g_refs, w_refs, m_refs, v_refs = (refs[i * n:(i + 1) * n] for i in range(4))
        loss_ref = refs[4 * n]
        outs = refs[4 * n + 1:]
        for i in range(n):
            g = _slot_sum(g_refs[i])
            outs[i][...] = g
            outs[n + i][...], outs[2 * n + i][...], outs[3 * n + i][...] = _adamw_math(
                g, w_refs[i][...], m_refs[i][...], v_refs[i][...])
        outs[4 * n][...] = _slot_sum(loss_ref)

    sds = [jax.ShapeDtypeStruct(w.shape, F32) for w in ws]
    res = pl.pallas_call(
        body, name="adamw_small", out_shape=tuple(sds * 4) + (jax.ShapeDtypeStruct(loss_all.shape[1:], F32),),
        compiler_params=pltpu.CompilerParams(vmem_limit_bytes=VMEM_LIMIT),
    )(*g_alls, *ws, *ms, *vs, loss_all)
    return [res[i * n:(i + 1) * n] for i in range(4)], res[4 * n]


def adamw_rows(g_slots, w, m, v, name):
    n, r, lanes = g_slots.shape
    fits = [t for t in range(16, r + 1, 16) if r % t == 0 and t * lanes <= ADAM_TILE]
    tr = max(fits) if fits else r
    def body(g_ref, w_ref, m_ref, v_ref, go, do, mo, vo):
        g = _slot_sum(g_ref)
        go[...] = g
        do[...], mo[...], vo[...] = _adamw_math(g, w_ref[...], m_ref[...], v_ref[...])

    row = pl.BlockSpec((tr, lanes), lambda i: (i, 0))
    sd = jax.ShapeDtypeStruct((r, lanes), F32)
    return pl.pallas_call(
        body, name=name, grid=(r // tr,), out_shape=(sd, sd, sd, sd),
        in_specs=[pl.BlockSpec((n, tr, lanes), lambda i: (0, i, 0)), row, row, row],
        out_specs=(row, row, row, row),
        compiler_params=_params(("parallel",)),
    )(g_slots, w, m, v)


def _rope_tables(s):
    def cs(pos, dim):
        inv = ROPE_THETA ** (-np.arange(0, dim, 2, dtype=np.float32) / dim)
        ang = pos.astype(np.float32)[:, None] * inv.astype(np.float32)[None, :]
        return np.cos(ang), np.sin(ang)

    rows = s // GRID_W
    row = np.repeat(np.arange(rows), GRID_W)
    col = np.tile(np.arange(GRID_W), rows)
    cr, sr = cs(row, HD // 2)
    cc, sc = cs(col, HD // 2)
    ct, st = cs(np.arange(s), B_ROPE)
    tables = (np.concatenate([cr, cr, cc, cc] * 2, axis=-1), np.concatenate([-sr, sr, -sc, sc] * 2, axis=-1),
              np.concatenate([ct, ct] * 4, axis=-1), np.concatenate([-st, st] * 4, axis=-1))
    return tuple(jnp.asarray(t, F32) for t in tables)


def _even_rows_to_kernel(wt):
    return jnp.concatenate([wt[:1664], wt[1696:], wt[1664:1696]], axis=0)


def _uq_rows_to_kernel(wt):
    r = wt.reshape(B_HEADS, B_NOPE + B_ROPE, -1)
    return jnp.concatenate([r[:, :B_NOPE].reshape(B_HEADS * B_NOPE, -1), r[:, B_NOPE:].reshape(B_HEADS * B_ROPE, -1)])


def _uq_rows_to_reference(wt):
    nope = wt[:B_HEADS * B_NOPE].reshape(B_HEADS, B_NOPE, -1)
    rope = wt[B_HEADS * B_NOPE:].reshape(B_HEADS, B_ROPE, -1)
    return jnp.concatenate([nope, rope], axis=1).reshape(B_HEADS * (B_NOPE + B_ROPE), -1)


def _shard_t(w):
    return jnp.transpose(w[0])


def _unshard_t(wt, like):
    return jnp.transpose(wt)[None].reshape(like.shape)


def kernel(x, c, norm_w, ada_w, ada_b, even_w_in, a_q_norm, a_k_norm, b_q_lora_norm, b_kv_lora_norm, b_w_uq, b_w_uk, b_w_uv, even_w_out, odd_w_in, c_sink, odd_w_out, final_norm, loss_target, m_norm_w, m_ada_w, m_ada_b, m_even_w_in, m_a_q_norm, m_a_k_norm, m_b_q_lora_norm, m_b_kv_lora_norm, m_b_w_uq, m_b_w_uk, m_b_w_uv, m_even_w_out, m_odd_w_in, m_c_sink, m_odd_w_out, m_final_norm, v_norm_w, v_ada_w, v_ada_b, v_even_w_in, v_a_q_norm, v_a_k_norm, v_b_q_lora_norm, v_b_kv_lora_norm, v_b_w_uq, v_b_w_uk, v_b_w_uv, v_even_w_out, v_odd_w_in, v_c_sink, v_odd_w_out, v_final_norm):
    s, d = x.shape[1], x.shape[2]
    x0 = x[0]
    target = loss_target[0]
    me_flat = 4 * lax.axis_index("x") + 2 * lax.axis_index("y") + lax.axis_index("c")

    wcols = ada_w.shape[2]
    bias_cols = lax.dynamic_slice_in_dim(ada_b.reshape(2, N_DEV, wcols), me_flat, 1, axis=1)
    call, modp, (g_in_e, g_uq) = ada_forward(
        jnp.broadcast_to(c, (8, d)), ada_w, bias_cols,
        Gather([_shard_t(even_w_in).astype(MXU), _shard_t(b_w_uq).astype(MXU)]))
    wt_in_e = _even_rows_to_kernel(g_in_e.reshape(-1, d))
    wt_uq = _uq_rows_to_kernel(g_uq.reshape(-1, B_Q_LORA))
    later_exchange = Exchange([_shard_t(odd_w_in).astype(MXU), even_w_out[0].astype(MXU),
                               odd_w_out[0].astype(MXU)], scatter=False)
    uk_bd = (jnp.eye(B_HEADS, dtype=F32)[:, None, :, None] * jnp.transpose(b_w_uk[0], (1, 2, 0))[:, :, None, :]
             ).reshape(B_HEADS * B_NOPE, B_HEADS * B_KV_LORA).astype(MXU)
    head_bd = jnp.asarray(np.kron(np.eye(A_HEADS), np.ones((HD, HD))), MXU)
    gq_full, gk_full = jnp.tile(a_q_norm, (1, A_HEADS)), jnp.tile(a_k_norm, (1, A_KV))
    w_uv = jnp.transpose(b_w_uv[0], (1, 0, 2)).astype(MXU)

    c_all = call[:, 0, :]
    mod = jnp.transpose(modp[:, :, 0, :], (1, 0, 2)).reshape(2, 3, d)
    mod_e, mod_o = mod[0], mod[1]
    nw_e, nw_o = norm_w[0:1], norm_w[1:2]

    cos_a, sin_a, cos_t, sin_t = _rope_tables(s)
    slopes = (2.0 ** (-8.0 * jnp.arange(1, C_HEADS + 1, dtype=F32) / C_HEADS)).reshape(C_HEADS, 1, 1)
    sink2 = c_sink.reshape(C_HEADS, 1, 1) * LOG2E

    (qa, ka, va, qb, kb, kat, vat, kbt, qa_raw, ka_raw, cq_raw, ckv_raw, ga, gb) = even_in_forward(
        x0, mod_e, nw_e, wt_in_e, gq_full, gk_full, b_q_lora_norm, b_kv_lora_norm, wt_uq, uk_bd, head_bd,
        cos_a, sin_a, cos_t, sin_t)
    tk_dense = min(512, s)
    tq_dense = min(256, s)
    fwd_sub = min(8, s // tk_dense)
    bwd_sub_a = min(16, s // tq_dense)
    bwd_sub_b = min(8, s // tq_dense)
    oa, lse_a, g_in_o, g_out_e, g_out_o = flash_forward(
        qa, ka, vat, dv=HD, tq=tq_dense, tk=tk_dense, nsub=fwd_sub, name="attn_a_fwd",
        exchange=later_exchange)
    wt_in_o = g_in_o.reshape(-1, d)
    w_out_e = g_out_e.reshape(-1, d)
    w_out_o = g_out_o.reshape(-1, d)
    o_lat, lse_b = flash_forward(qb, kb, kbt, dv=B_KV_LORA, tq=min(128, s), tk=tk_dense, nsub=fwd_sub,
                                 name="attn_b_fwd")
    ob = latent_out_forward(o_lat, w_uv)
    x1, y_e = mixer_out_forward(x0, mod_e, [(oa, ga), (ob, gb)], w_out_e, "even_out_fwd")

    qc, kc, vc, kct, vct, gc = odd_in_forward(x1, mod_o, nw_o, wt_in_o)
    win_sub = min(8, s // WINDOW)
    oc, lse_c = window_forward(qc, kc, vct, sink2, slopes, win_sub, "attn_c_fwd")
    dx2, y_o, loss_lanes, d_final = mixer_out_forward(x1, mod_o, [(oc, gc)], w_out_o, "odd_out_fwd_loss",
                                                      loss=(target, final_norm.reshape(1, d)))

    loss_part = (0.5 / d) * jnp.sum(loss_lanes)

    doc, dgc, delta_c, dgate_o, dw_out_o, dsink = mixer_out_backward(
        dx2, y_o, mod_o, [(oc, gc)], w_out_o, [C_HEADS], "odd_out_bwd", lse=lse_c.reshape(C_HEADS, s),
        sink=sink2.reshape(C_HEADS, 1))
    rows3 = lambda t: t.reshape(t.shape[0], 1, s)
    dqc, dkc, dvc = window_backward(qc, kc, kct, vc, doc, lse_c, rows3(delta_c), slopes, win_sub, "attn_c_bwd")
    dx1, dvec_o, dwt_in_o = in_proj_backward(
        x1, mod_o, nw_o, [(dqc, O_Q), (dkc, O_K), (dvc, O_V), (dgc, O_G)], "odd_in_bwd",
        dx_out=dx2, w_in_t=wt_in_o, dw_rows=[O_Q, O_K, O_V, O_G])

    doa, dga, dob, dgb, delta_a, dgate_e, dw_out_e = mixer_out_backward(
        dx1, y_e, mod_e, [(oa, ga), (ob, gb)], w_out_e, [A_HEADS, 0], "even_out_bwd")
    d_olat, delta_b, dw_uv = latent_out_backward(dob, o_lat, w_uv)
    blocks = lambda g: g.astype(MXU).reshape(N_DEV, g.shape[0] // N_DEV, g.shape[1])
    even_pieces = lambda: [(pqa, E_QA), (pka, E_KA), (pva, E_VA), (dga, E_GA), (pcq, E_CQ), (pckv, E_CKV),
                           (dgb, E_GB), (pkr, E_KR)]
    scatter_odd = Exchange([dwt_in_o, dw_out_o], True)
    scatter_out_e = Exchange([dw_out_e, dw_uv.astype(MXU)], [True, False])
    dqb, dkb, dvb, l_in_o, l_out_o = flash_backward(
        qb, kb, kbt, None, d_olat, lse_b, rows3(delta_b), scale=SCALE_B, dv=B_KV_LORA,
        tq=tq_dense, tk=tk_dense, nsub=bwd_sub_b, gq=2, name="attn_b_bwd", split=B_KV_LORA, exchange=scatter_odd)
    dqa, dka, dva, l_out_e, l_uv = flash_backward(
        qa, ka, kat, va, doa, lse_a, rows3(delta_a), scale=SCALE_A, dv=HD,
        tq=tq_dense, tk=tk_dense, nsub=bwd_sub_a, gq=A_KV, name="attn_a_bwd", exchange=scatter_out_e)
    (pqa, pka, pva, pcq, pckv, pkr, g_qn, g_kn, g_qln, g_kvln, dwt_uq, dw_uk) = even_prep_backward(
        dqa, dka, dva, dqb, dkb, dvb, qa_raw, ka_raw, cq_raw, ckv_raw,
        gq_full, gk_full, b_q_lora_norm, b_kv_lora_norm, wt_uq, uk_bd, head_bd, cos_a, sin_a, cos_t, sin_t)
    g_qn = jnp.sum(g_qn.reshape(A_HEADS, HD), axis=0)
    g_kn = jnp.sum(g_kn.reshape(A_KV, HD), axis=0)
    dwt_in_e, l_uk = in_proj_backward(
        x0, mod_e, nw_e, even_pieces(), "even_in_bwd_dw",
        dw_rows=[E_QA, E_KA, E_VA, E_GA, E_CQ, E_CKV, (1696, 2208), (1664, 1696)],
        exchange=Exchange([dw_uk.astype(MXU)], scatter=False))
    dx0, dvec_e, l_in_e, l_uq = in_proj_backward(
        x0, mod_e, nw_e, even_pieces(), "even_in_bwd_dx", dx_out=dx1, w_in_t=wt_in_e,
        exchange=Exchange([dwt_in_e, blocks(_uq_rows_to_reference(dwt_uq))], True))

    dmod = jnp.stack([jnp.concatenate([dvec_e[0], dvec_e[1], dgate_e[0]]),
                      jnp.concatenate([dvec_o[0], dvec_o[1], dgate_o[0]])])
    d_norm_w = jnp.stack([dvec_e[2], dvec_o[2]])
    small_names = ["norm_w", "ada_b", "a_q_norm", "a_k_norm", "b_q_lora_norm", "b_kv_lora_norm", "b_w_uk", "b_w_uv",
                   "c_sink", "final_norm"]
    small_w = [norm_w, ada_b, a_q_norm, a_k_norm, b_q_lora_norm, b_kv_lora_norm, b_w_uk, b_w_uv, c_sink, final_norm]
    small_m = [m_norm_w, m_ada_b, m_a_q_norm, m_a_k_norm, m_b_q_lora_norm, m_b_kv_lora_norm, m_b_w_uk, m_b_w_uv,
               m_c_sink, m_final_norm]
    small_v = [v_norm_w, v_ada_b, v_a_q_norm, v_a_k_norm, v_b_q_lora_norm, v_b_kv_lora_norm, v_b_w_uk, v_b_w_uv,
               v_c_sink, v_final_norm]
    small_g = [d_norm_w, dmod, g_qn, g_kn, g_qln, g_kvln, None, None, dsink, d_final]
    flat2 = lambda a: a.reshape((1, -1)) if a.size == a.shape[-1] else a.reshape(a.shape[-3:] if a.ndim > 3 else a.shape)
    kshape = [flat2(w).shape for w in small_w]
    late = [i for i, g in enumerate(small_g) if g is not None]
    gathered = all_gather_slots(
        Gather([small_g[i].reshape(kshape[i]) for i in late] + [jnp.full((8, 128), loss_part, F32)]),
        "gather_small_grads")
    g_all = [None] * len(small_g)
    for i, g in zip(late, gathered):
        g_all[i] = g
    g_all[6], g_all[7] = (l.reshape((N_DEV,) + kshape[6]) for l in (l_uk, l_uv))
    sm_out, loss_sum = adamw_small(g_all, [flat2(a) for a in small_w], [flat2(a) for a in small_m],
                                   [flat2(a) for a in small_v], gathered[-1])
    loss = loss_sum[0, 0]
    sm = [{nm: p.reshape(w.shape) for nm, w, p in zip(small_names, small_w, outs)} for outs in sm_out]

    dmod_all = g_all[1].reshape(N_DEV, 2, N_DEV, wcols)
    dmod_cols = lax.dynamic_slice_in_dim(dmod_all, me_flat, 1, axis=2)[:, :, 0, :]
    pad16 = lambda a: jnp.concatenate([a, jnp.zeros_like(a)], axis=0)
    g_ada_w = ada_weight_grad(pad16(c_all), jnp.transpose(pad16(dmod_cols), (1, 0, 2)))
    rows_of = lambda a: a.reshape(-1, wcols)
    ada = adamw_rows(rows_of(g_ada_w)[None], rows_of(ada_w), rows_of(m_ada_w), rows_of(v_ada_w), "adamw_ada_w")
    ada = [p.reshape(ada_w.shape) for p in ada]

    bg = [{}, {}, {}, {}]
    for nm, landed, w, m, v, transposed in (
            ("even_w_in", l_in_e, even_w_in, m_even_w_in, v_even_w_in, True),
            ("b_w_uq", l_uq, b_w_uq, m_b_w_uq, v_b_w_uq, True),
            ("odd_w_in", l_in_o, odd_w_in, m_odd_w_in, v_odd_w_in, True),
            ("even_w_out", l_out_e, even_w_out, m_even_w_out, v_even_w_out, False),
            ("odd_w_out", l_out_o, odd_w_out, m_odd_w_out, v_odd_w_out, False)):
        view = _shard_t if transposed else (lambda a: a[0])
        res = adamw_rows(landed, view(w), view(m), view(v), "adamw_" + nm)
        for kind, p in enumerate(res):
            bg[kind][nm] = _unshard_t(p, w) if transposed else p[None]
    big_names = ["even_w_in", "odd_w_in", "even_w_out", "odd_w_out", "b_w_uq"]

    order = ["norm_w", "ada_w", "ada_b", "even_w_in", "a_q_norm", "a_k_norm", "b_q_lora_norm", "b_kv_lora_norm",
             "b_w_uq", "b_w_uk", "b_w_uv", "even_w_out", "odd_w_in", "c_sink", "odd_w_out", "final_norm"]

    def pick(kind):
        out = []
        for nm in order:
            if nm == "ada_w":
                out.append(ada[kind])
            elif nm in big_names:
                out.append(bg[kind][nm])
            else:
                out.append(sm[kind][nm])
        return out

    return (loss, dx0[None], *pick(0), *pick(1), *pick(2), *pick(3))
```

```python
import functools

import jax
import jax.numpy as jnp
import numpy as np
from jax import lax
from jax.experimental import pallas as pl
from jax.experimental.pallas import tpu as pltpu

F32 = jnp.float32
MXU = jnp.bfloat16
EPS = 1e-6
ROPE_THETA = 10000.0
GRID_W = 64
HD = 64
N_DEV = 8

A_HEADS, A_KV = 8, 2
B_HEADS, B_NOPE, B_ROPE, B_Q_LORA, B_KV_LORA = 8, 64, 32, 256, 128
B_QK = B_KV_LORA + B_ROPE
C_HEADS, C_KV = 16, 4
WINDOW = 128

ADAM_LR, ADAM_B1, ADAM_B2, ADAM_EPS, ADAM_WD, ADAM_STEP = 0.001, 0.9, 0.999, 1e-08, 0.01, 10

ROW_TILE = 512
ADAM_TILE = 2048 * 128
X_SLOTS = 3
LOG2E = 1.4426950408889634
SCALE_A = HD ** -0.5
SCALE_B = (B_NOPE + B_ROPE) ** -0.5
SCALE2_A, SCALE2_B = SCALE_A * LOG2E, SCALE_B * LOG2E
VMEM_LIMIT = 56 * 1024 * 1024

E_QA, E_KA, E_VA, E_GA, E_CQ, E_CKV, E_GB, E_KR = (
    (0, 512), (512, 640), (640, 768), (768, 1280), (1280, 1536), (1536, 1664), (1664, 2176), (2176, 2208))
O_Q, O_K, O_V, O_G = (0, 1024), (1024, 1280), (1280, 1536), (1536, 2560)


def _mm(a, b):
    return jnp.dot(a.astype(MXU), b.astype(MXU), preferred_element_type=F32)


def _mm_nt(a, b):
    return lax.dot_general(a.astype(MXU), b.astype(MXU), (((1,), (1,)), ((), ())), preferred_element_type=F32)


def _mm_tn(a, b):
    return lax.dot_general(a.astype(MXU), b.astype(MXU), (((0,), (0,)), ((), ())), preferred_element_type=F32)


def _group_sums_t(prod, group):
    tm, w = prod.shape
    sel = (lax.broadcasted_iota(jnp.int32, (w, 128), 0) // group
           == lax.broadcasted_iota(jnp.int32, (w, 128), 1)).astype(MXU)
    hi = prod.astype(MXU)
    lo = prod - hi.astype(F32)
    return (_mm(hi, sel) + _mm(lo, sel)).T


def _sigmoid(z):
    return 1.0 / (1.0 + jnp.exp(-z))


def _silu(z):
    return z * _sigmoid(z)


def _rms(x):
    return lax.rsqrt(jnp.mean(x * x, axis=-1, keepdims=True) + EPS)


def _swap_halves(y, group):
    n = y.shape[-1]
    half = group // 2
    fwd = pltpu.roll(y, half, 1)
    if n == group:
        return fwd
    back = pltpu.roll(y, n - half, 1)
    lane = lax.broadcasted_iota(jnp.int32, y.shape, 1)
    return jnp.where((lane % group) < half, back, fwd)


def _rope(y, cos, sin, group):
    return y * cos + _swap_halves(y, group) * sin


def _rope_t(d, cos, sin, group):
    return d * cos - _swap_halves(d, group) * sin


def _rms_bwd(dy, x, g):
    r = _rms(x)
    xhat = x * r
    dxhat = dy * g
    dx = r * (dxhat - xhat * jnp.mean(dxhat * xhat, axis=-1, keepdims=True))
    return dx, dy * xhat


def _group_mean(v, bd, group):
    hi = v.astype(MXU)
    lo = v - hi.astype(F32)
    return (_mm(hi, bd[...]) + _mm(lo, bd[...])) * (1.0 / group)


def _head_norm(x, g, bd, group):
    return x * lax.rsqrt(_group_mean(x * x, bd, group) + EPS) * g


def _head_norm_bwd(dy, x, g, bd, group):
    r = lax.rsqrt(_group_mean(x * x, bd, group) + EPS)
    xhat = x * r
    dxhat = dy * g
    dx = r * (dxhat - xhat * _group_mean(dxhat * xhat, bd, group))
    return dx, dy * xhat


def _params(sem, vmem=VMEM_LIMIT):
    return pltpu.CompilerParams(dimension_semantics=sem, vmem_limit_bytes=vmem)


def _row_spec(tm, w):
    return pl.BlockSpec((tm, w), lambda i: (i, 0))


def _full_spec(shape):
    nd = len(shape)
    return pl.BlockSpec(shape, lambda i: (0,) * nd)


def _head_spec(h, tm, w):
    return pl.BlockSpec((h, tm, w), lambda i: (0, i, 0))


def _headt_spec(h, w, tm):
    return pl.BlockSpec((h, w, tm), lambda i: (0, 0, i))


def _rows_spec(h, tm):
    return pl.BlockSpec((h, tm), lambda i: (0, i))


def _me():
    return lax.axis_index("x"), lax.axis_index("y"), lax.axis_index("c")


def _flat(p):
    return 4 * p[0] + 2 * p[1] + p[2]


def _peer(me, k):
    x, y, c = me
    return (1 - x if k & 4 else x, 1 - y if k & 2 else y, 1 - c if k & 1 else c)


MESH_ID = pl.DeviceIdType.MESH


class Gather:
    VMEM = pl.BlockSpec(memory_space=pltpu.VMEM)

    def __init__(self, shards):
        self.shards = list(shards)
        self.n = len(self.shards)
        self.out_shapes = tuple(jax.ShapeDtypeStruct((N_DEV,) + a.shape, a.dtype) for a in self.shards)
        self.in_specs = [Gather.VMEM] * self.n
        self.out_specs = (Gather.VMEM,) * self.n
        self.sems = [pltpu.SemaphoreType.DMA((7 * self.n,)), pltpu.SemaphoreType.DMA((7 * self.n,)),
                     pltpu.SemaphoreType.DMA((self.n,))]

    def _plan(self, x_refs, out_refs, sems):
        send_sems, recv_sems, local_sems = sems
        me = _me()
        x, y, c = me
        chips = [(1 - x, y), (x, 1 - y), (1 - x, 1 - y)]

        def copy(a, k, block, to, src=None):
            slot = out_refs[a].at[_flat(block)]
            return pltpu.make_async_remote_copy(
                src_ref=slot if src is None else src, dst_ref=slot, send_sem=send_sems.at[7 * a + k],
                recv_sem=recv_sems.at[7 * a + k], device_id=to, device_id_type=MESH_ID)

        mine = [pltpu.make_async_copy(x_refs[a], out_refs[a].at[_flat(me)], local_sems.at[a]) for a in range(self.n)]
        first = [copy(a, 0, me, (x, y, 1 - c), src=x_refs[a]) for a in range(self.n)]
        first += [copy(a, 1 + j, me, (*chip, c), src=x_refs[a]) for a in range(self.n) for j, chip in enumerate(chips)]
        return me, chips, copy, mine, first

    def start(self, x_refs, out_refs, sems):
        _, _, _, mine, first = self._plan(x_refs, out_refs, sems)
        for cp in mine + first:
            cp.start()

    def forward(self, x_refs, out_refs, sems):
        me, chips, copy, _, _ = self._plan(x_refs, out_refs, sems)
        x, y, c = me
        for a in range(self.n):
            for j, chip in enumerate(chips):
                copy(a, 1 + j, (*chip, c), me).wait_recv()
                copy(a, 4 + j, (*chip, c), (x, y, 1 - c)).start()

    def drain(self, x_refs, out_refs, sems):
        me, chips, copy, mine, first = self._plan(x_refs, out_refs, sems)
        x, y, c = me
        sibling = (x, y, 1 - c)
        for a in range(self.n):
            copy(a, 0, sibling, me).wait_recv()
            for j, chip in enumerate(chips):
                copy(a, 4 + j, (*chip, 1 - c), me).wait_recv()
        for cp in first + [copy(a, 4 + j, (*chip, c), sibling) for a in range(self.n) for j, chip in enumerate(chips)]:
            cp.wait_send()
        for cp in mine:
            cp.wait()

    def finish(self, x_refs, out_refs, sems):
        self.forward(x_refs, out_refs, sems)
        self.drain(x_refs, out_refs, sems)


def all_gather_slots(gather, name):
    def body(*refs):
        x_refs, out_refs, sems = refs[:gather.n], refs[gather.n:2 * gather.n], refs[2 * gather.n:]
        gather.start(x_refs, out_refs, sems)
        gather.finish(x_refs, out_refs, sems)

    return pl.pallas_call(
        body, name=name, out_shape=gather.out_shapes, in_specs=gather.in_specs, out_specs=gather.out_specs,
        scratch_shapes=list(gather.sems), compiler_params=pltpu.CompilerParams(vmem_limit_bytes=VMEM_LIMIT),
    )(*gather.shards)


class Exchange:
    HBM = pl.BlockSpec(memory_space=pl.ANY)

    def __init__(self, srcs, scatter):
        self.srcs = list(srcs)
        self.scatter = scatter
        self.n = len(self.srcs)
        self.land_shapes = tuple(jax.ShapeDtypeStruct((N_DEV,) + tuple(a.shape[-2:]), a.dtype) for a in self.srcs)
        self.in_specs = [Exchange.HBM] * self.n
        self.out_specs = (Exchange.HBM,) * self.n
        self.sems = [pltpu.SemaphoreType.DMA((N_DEV - 1,)), pltpu.SemaphoreType.DMA((N_DEV - 1,)),
                     pltpu.SemaphoreType.DMA] * self.n

    def _copies(self, src_refs, land_refs, sems):
        me = _me()
        mi = _flat(me)
        local, sends, recvs = [], [], []
        for a, (src_ref, land_ref) in enumerate(zip(src_refs, land_refs)):
            send_sems, recv_sems, local_sem = sems[3 * a:3 * a + 3]
            pick = (lambda p, r=src_ref: r.at[_flat(p)]) if self.scatter else (lambda p, r=src_ref: r)
            local.append(pltpu.make_async_copy(pick(me), land_ref.at[mi], local_sem))
            for k in range(1, N_DEV):
                peer = _peer(me, k)
                pair = dict(send_sem=send_sems.at[k - 1], recv_sem=recv_sems.at[k - 1], device_id=peer,
                            device_id_type=MESH_ID)
                sends.append(pltpu.make_async_remote_copy(src_ref=pick(peer), dst_ref=land_ref.at[mi], **pair))
                recvs.append(pltpu.make_async_remote_copy(src_ref=pick(peer), dst_ref=land_ref.at[_flat(peer)],
                                                          **pair))
        return local, sends, recvs

    def start(self, src_refs, land_refs, sems):
        local, sends, _ = self._copies(src_refs, land_refs, sems)
        for cp in local + sends:
            cp.start()

    def wait(self, src_refs, land_refs, sems):
        local, sends, recvs = self._copies(src_refs, land_refs, sems)
        for cp in recvs:
            cp.wait_recv()
        for cp in sends:
            cp.wait_send()
        for cp in local:
            cp.wait()


def ada_forward(c8, ada_w, bias_cols, gather):
    d = c8.shape[1]
    w = ada_w.shape[2]
    ng = gather.n

    def body(*refs):
        c_ref, w_ref, b_ref = refs[:3]
        gx_refs = refs[3:3 + ng]
        call_ref, modp_ref = refs[3 + ng:5 + ng]
        gout_refs = refs[5 + ng:5 + 2 * ng]
        part_ref, s1, r1, s2, r2 = refs[5 + 2 * ng:10 + 2 * ng]
        g_sems = refs[10 + 2 * ng:]
        me = _me()
        mi = _flat(me)
        call_ref[mi] = c_ref[...]
        rows_out = []
        for k in range(1, N_DEV):
            rows_out.append(pltpu.make_async_remote_copy(
                src_ref=c_ref, dst_ref=call_ref.at[mi], send_sem=s1.at[k - 1], recv_sem=r1.at[k - 1],
                device_id=_peer(me, k), device_id_type=MESH_ID))
        for cp in rows_out:
            cp.start()
        gather.start(gx_refs, gout_refs, g_sems)
        for k in range(1, N_DEV):
            pltpu.make_async_remote_copy(
                src_ref=c_ref, dst_ref=call_ref.at[_flat(_peer(me, k))], send_sem=s1.at[k - 1],
                recv_sem=r1.at[k - 1], device_id=_peer(me, k), device_id_type=MESH_ID).wait_recv()
        ca = _silu(call_ref[...].reshape(N_DEV * 8, d))
        for l in range(2):
            part = _mm(ca, w_ref[l]) + b_ref[l]
            for b in range(N_DEV):
                part_ref[b, l] = part[8 * b:8 * b + 8, :]
        modp_ref[mi] = part_ref[mi]
        spread = []
        for k in range(1, N_DEV):
            peer = _peer(me, k)
            spread.append(pltpu.make_async_remote_copy(
                src_ref=part_ref.at[_flat(peer)], dst_ref=modp_ref.at[mi], send_sem=s2.at[k - 1],
                recv_sem=r2.at[k - 1], device_id=peer, device_id_type=MESH_ID))
        for cp in spread:
            cp.start()
        gather.forward(gx_refs, gout_refs, g_sems)
        for k in range(1, N_DEV):
            pi = _flat(_peer(me, k))
            pltpu.make_async_remote_copy(
                src_ref=part_ref.at[pi], dst_ref=modp_ref.at[pi], send_sem=s2.at[k - 1],
                recv_sem=r2.at[k - 1], device_id=_peer(me, k), device_id_type=MESH_ID).wait_recv()
        for cp in rows_out + spread:
            cp.wait_send()
        gather.drain(gx_refs, gout_refs, g_sems)

    vm = pl.BlockSpec(memory_space=pltpu.VMEM)
    res = pl.pallas_call(
        body, name="ada_forward",
        out_shape=(jax.ShapeDtypeStruct((N_DEV, 8, d), F32), jax.ShapeDtypeStruct((N_DEV, 2, 8, w), F32))
        + gather.out_shapes,
        in_specs=[vm, vm, vm] + gather.in_specs, out_specs=(vm, vm) + gather.out_specs,
        scratch_shapes=[pltpu.VMEM((N_DEV, 2, 8, w), F32)] + [pltpu.SemaphoreType.DMA((7,))] * 4 + list(gather.sems),
        compiler_params=pltpu.CompilerParams(vmem_limit_bytes=VMEM_LIMIT),
    )(c8, ada_w, bias_cols, *gather.shards)
    return res[0], res[1], res[2:]


def _modulated(x, mod_ref, nw_ref):
    xn = x * _rms(x)
    g1 = nw_ref[...] * (1.0 + mod_ref[1:2, :])
    return xn, g1, xn * g1 + mod_ref[0:1, :]


def even_in_forward(x, mod, nw, w_in_t, gq, gk, qln, kvln, w_uq_t, uk_bd, bd, cos_a, sin_a, cos_t, sin_t):
    s, d = x.shape
    tm = min(ROW_TILE, s)
    n_nope = B_HEADS * B_NOPE

    def body(x_ref, mod_ref, nw_ref, w_ref, gq_ref, gk_ref, qln_ref, kvln_ref, uq_ref, ukbd_ref, bd_ref,
             ca_ref, sa_ref, ct_ref, st_ref,
             qa_o, ka_o, va_o, qb_o, kb_o, kat_o, vat_o, kbt_o, qa_raw_o, ka_raw_o, cq_raw_o, ckv_raw_o, ga_o, gb_o):
        _, _, h = _modulated(x_ref[...], mod_ref, nw_ref)
        h = h.astype(MXU)

        def proj(cols):
            return _mm_nt(h, w_ref[cols[0]:cols[1], :])

        ca, sa, ct, st = ca_ref[...], sa_ref[...], ct_ref[...], st_ref[...]
        wide = lambda t, n: jnp.concatenate([t] * n, axis=1)
        qa = proj(E_QA)
        qa_raw_o[...] = qa
        qr = _rope(_head_norm(qa, gq_ref[...], bd_ref, HD), wide(ca, 4), wide(sa, 4), 32) * SCALE2_A
        for hh in range(A_HEADS):
            qa_o[hh] = qr[:, HD * hh:HD * hh + HD].astype(MXU)
        ka = proj(E_KA)
        ka_raw_o[...] = ka
        kr = _rope(_head_norm(ka, gk_ref[...], bd_ref[0:128, 0:128], HD), ca, sa, 32)
        va = proj(E_VA)
        krt, vat = kr.T, va.T
        for g in range(A_KV):
            ka_o[g] = kr[:, HD * g:HD * g + HD].astype(MXU)
            va_o[g] = va[:, HD * g:HD * g + HD].astype(MXU)
            kat_o[g] = krt[HD * g:HD * g + HD, :].astype(MXU)
            vat_o[g] = vat[HD * g:HD * g + HD, :].astype(MXU)
        ga_o[...] = proj(E_GA).astype(MXU)
        gb_o[...] = proj(E_GB).astype(MXU)
        cq = proj(E_CQ)
        cq_raw_o[...] = cq
        qb = _mm_nt(cq * _rms(cq) * qln_ref[...], uq_ref[...])
        q_lat = _mm(qb[:, 0:n_nope], ukbd_ref[...]) * SCALE2_B
        q_rope = _rope(qb[:, n_nope:], wide(ct, 2), wide(st, 2), 32) * SCALE2_B
        for hh in range(B_HEADS):
            qb_o[hh, :, 0:B_KV_LORA] = q_lat[:, B_KV_LORA * hh:B_KV_LORA * (hh + 1)].astype(MXU)
            qb_o[hh, :, B_KV_LORA:B_QK] = q_rope[:, B_ROPE * hh:B_ROPE * (hh + 1)].astype(MXU)
        ckv = proj(E_CKV)
        ckv_raw_o[...] = ckv
        ckv_n = ckv * _rms(ckv) * kvln_ref[...]
        k_rope = _rope(proj(E_KR), ct[:, 0:B_ROPE], st[:, 0:B_ROPE], 32)
        kb_o[0, :, 0:B_KV_LORA] = ckv_n.astype(MXU)
        kb_o[0, :, B_KV_LORA:B_QK] = k_rope.astype(MXU)
        kbt_o[0, 0:B_KV_LORA, :] = ckv_n.T.astype(MXU)
        kbt_o[0, B_KV_LORA:B_QK, :] = k_rope.T.astype(MXU)

    sd = jax.ShapeDtypeStruct
    outs = (sd((A_HEADS, s, HD), MXU), sd((A_KV, s, HD), MXU), sd((A_KV, s, HD), MXU),
            sd((B_HEADS, s, B_QK), MXU), sd((1, s, B_QK), MXU),
            sd((A_KV, HD, s), MXU), sd((A_KV, HD, s), MXU), sd((1, B_QK, s), MXU),
            sd((s, 512), F32), sd((s, 128), F32), sd((s, B_Q_LORA), F32), sd((s, B_KV_LORA), F32),
            sd((s, 512), MXU), sd((s, 512), MXU))
    out_specs = (_head_spec(A_HEADS, tm, HD), _head_spec(A_KV, tm, HD), _head_spec(A_KV, tm, HD),
                 _head_spec(B_HEADS, tm, B_QK), _head_spec(1, tm, B_QK),
                 _headt_spec(A_KV, HD, tm), _headt_spec(A_KV, HD, tm), _headt_spec(1, B_QK, tm),
                 _row_spec(tm, 512), _row_spec(tm, 128), _row_spec(tm, B_Q_LORA), _row_spec(tm, B_KV_LORA),
                 _row_spec(tm, 512), _row_spec(tm, 512))
    consts = [mod, nw, w_in_t, gq, gk, qln, kvln, w_uq_t, uk_bd, bd]
    return pl.pallas_call(
        body, name="even_in_forward", grid=(s // tm,), out_shape=outs,
        in_specs=[_row_spec(tm, d)] + [_full_spec(a.shape) for a in consts] + [_row_spec(tm, 128)] * 4,
        out_specs=out_specs, compiler_params=_params(("parallel",)),
    )(x, *consts, cos_a, sin_a, cos_t, sin_t)


def odd_in_forward(x, mod, nw, w_in):
    s, d = x.shape
    tm = min(ROW_TILE, s)

    def body(x_ref, mod_ref, nw_ref, w_ref, q_o, k_o, v_o, kt_o, vt_o, g_o):
        _, _, h = _modulated(x_ref[...], mod_ref, nw_ref)
        h = h.astype(MXU)

        def proj(cols):
            return _mm_nt(h, w_ref[cols[0]:cols[1], :])

        q = proj(O_Q) * SCALE2_A
        for hh in range(C_HEADS):
            q_o[hh] = q[:, HD * hh:HD * hh + HD].astype(MXU)
        k = proj(O_K)
        v = proj(O_V)
        for g in range(C_KV):
            kh = k[:, HD * g:HD * g + HD]
            vh = v[:, HD * g:HD * g + HD]
            k_o[g] = kh.astype(MXU)
            v_o[g] = vh.astype(MXU)
            kt_o[g] = kh.T.astype(MXU)
            vt_o[g] = vh.T.astype(MXU)
        g_o[...] = proj(O_G).astype(MXU)

    sd = jax.ShapeDtypeStruct
    return pl.pallas_call(
        body, name="odd_in_forward", grid=(s // tm,),
        out_shape=(sd((C_HEADS, s, HD), MXU), sd((C_KV, s, HD), MXU), sd((C_KV, s, HD), MXU),
                   sd((C_KV, HD, s), MXU), sd((C_KV, HD, s), MXU), sd((s, 1024), MXU)),
        in_specs=[_row_spec(tm, d), _full_spec(mod.shape), _full_spec(nw.shape), _full_spec(w_in.shape)],
        out_specs=(_head_spec(C_HEADS, tm, HD), _head_spec(C_KV, tm, HD), _head_spec(C_KV, tm, HD),
                   _headt_spec(C_KV, HD, tm), _headt_spec(C_KV, HD, tm), _row_spec(tm, 1024)),
        compiler_params=_params(("parallel",)),
    )(x, mod, nw, w_in)


def latent_out_forward(o_lat, w_uv):
    s = o_lat.shape[0]
    tm = min(ROW_TILE, s)

    def body(o_ref, uv_ref, out_ref):
        for hh in range(B_HEADS):
            out_ref[:, HD * hh:HD * hh + HD] = _mm(o_ref[:, B_KV_LORA * hh:B_KV_LORA * (hh + 1)],
                                                   uv_ref[hh]).astype(MXU)

    return pl.pallas_call(
        body, name="latent_out_forward", grid=(s // tm,),
        out_shape=jax.ShapeDtypeStruct((s, B_HEADS * HD), MXU),
        in_specs=[_row_spec(tm, o_lat.shape[1]), _full_spec(w_uv.shape)],
        out_specs=_row_spec(tm, B_HEADS * HD),
        compiler_params=_params(("parallel",)),
    )(o_lat, w_uv)


def mixer_out_forward(x, mod, pairs, w_out, name, loss=None):
    s, d = x.shape
    tm = min(ROW_TILE, s)
    n = len(pairs)
    widths = [o.shape[1] for o, _ in pairs]
    head = loss is not None

    def body(*refs):
        x_ref, mod_ref, w_ref = refs[:3]
        pr = refs[3:3 + 2 * n]
        rest = refs[3 + 2 * n:]
        y = jnp.zeros((tm, d), F32)
        r0 = 0
        for i in range(n):
            mix = pr[2 * i][...].astype(F32) * _silu(pr[2 * i + 1][...].astype(F32))
            y = y + _mm(mix, w_ref[r0:r0 + widths[i], :])
            r0 += widths[i]
        x_out = x_ref[...] + mod_ref[2:3, :] * y
        if not head:
            xo_ref, y_ref = rest
            xo_ref[...] = x_out
        else:
            t_ref, fn_ref, dx_ref, y_ref, lp_ref, dw_ref = rest

            @pl.when(pl.program_id(0) == 0)
            def _():
                lp_ref[...] = jnp.zeros(lp_ref.shape, F32)
                dw_ref[...] = jnp.zeros(dw_ref.shape, F32)

            g = fn_ref[...]
            err = x_out * _rms(x_out) * g - t_ref[...]
            lp_ref[...] += jnp.sum(err * err, axis=0, keepdims=True)
            dx, dg = _rms_bwd(err * (1.0 / d), x_out, g)
            dx_ref[...] = dx
            dw_ref[...] += jnp.sum(dg, axis=0, keepdims=True)
        y_ref[...] = y.astype(y_ref.dtype)

    flat = [a for p in pairs for a in p]
    sd = jax.ShapeDtypeStruct
    in_specs = [_row_spec(tm, d), _full_spec(mod.shape), _full_spec(w_out.shape)]
    in_specs += [_row_spec(tm, a.shape[1]) for a in flat]
    out_shape = (sd((s, d), F32), sd((s, d), MXU))
    out_specs = (_row_spec(tm, d), _row_spec(tm, d))
    if head:
        in_specs += [_row_spec(tm, d), _full_spec(loss[1].shape)]
        out_shape += (sd((1, d), F32), sd((1, d), F32))
        out_specs += (_full_spec((1, d)), _full_spec((1, d)))
    return pl.pallas_call(
        body, name=name, grid=(s // tm,), out_shape=out_shape, in_specs=in_specs, out_specs=out_specs,
        compiler_params=_params(("arbitrary",) if head else ("parallel",)),
    )(x, mod, w_out, *flat, *(loss if head else ()))


ONES_ROWS = 16
AHEAD = 2


def _col_max8(s3):
    m8 = jnp.max(s3, axis=0)
    return jnp.broadcast_to(jnp.max(m8, axis=0, keepdims=True), m8.shape)


def _with_ones(vt, n):
    return jnp.concatenate([vt, jnp.ones((ONES_ROWS, n), vt.dtype)], axis=0)


def _grid_edges(grid):
    ids = [pl.program_id(a) for a in range(len(grid))]
    first = functools.reduce(jnp.logical_and, [i == 0 for i in ids])
    last = functools.reduce(jnp.logical_and, [i == n - 1 for i, n in zip(ids, grid)])
    return first, last


def flash_forward(q, k, vt, *, dv, tq, tk, nsub, name, exchange=None):
    hq, s, dq = q.shape
    g_kv = k.shape[0]
    hpg = hq // g_kv
    nq = s // tq
    tkk = tk * nsub
    nk = s // tkk
    grid = (g_kv, nq, nk)
    hosted = exchange is not None
    m_cols = hpg * tq
    dvp = dv + ONES_ROWS

    def body(*refs):
        nx = exchange.n if hosted else 0
        q_ref, k_ref, vt_ref = refs[:3]
        xs_refs = refs[3:3 + nx]
        o_ref, lse_ref = refs[3 + nx:5 + nx]
        land_refs = refs[5 + nx:5 + 2 * nx]
        m_s, acc_s = refs[5 + 2 * nx:7 + 2 * nx]
        sems = refs[7 + 2 * nx:]
        if hosted:
            first, last = _grid_edges(grid)
            pl.when(first)(lambda: exchange.start(xs_refs, land_refs, sems))
        j = pl.program_id(2)

        @pl.when(j == 0)
        def _():
            m_s[...] = jnp.full((8, m_cols), -jnp.inf, F32)
            acc_s[...] = jnp.zeros((dvp, m_cols), F32)

        qq = q_ref[...].reshape(m_cols, dq)
        score = lambda u: _mm_nt(k_ref[0, tk * u:tk * (u + 1), :], qq).reshape(tk // 8, 8, m_cols)
        sts = {u: score(u) for u in range(min(AHEAD, nsub))}
        m_run = m_s[...]
        acc = acc_s[...]
        for u in range(nsub):
            if u + AHEAD < nsub:
                sts[u + AHEAD] = score(u + AHEAD)
            st = sts.pop(u)
            m_new = jnp.maximum(m_run, _col_max8(st))
            p = jnp.exp2(st - m_new[None])
            alpha = jnp.exp2(m_run - m_new)
            pv = _mm(_with_ones(vt_ref[0, 0:dv, tk * u:tk * (u + 1)], tk), p.reshape(tk, m_cols))
            acc = (acc.reshape(dvp // 8, 8, m_cols) * alpha[None]).reshape(dvp, m_cols) + pv
            m_run = m_new
        acc_s[...] = acc
        m_s[...] = m_run

        @pl.when(j == nk - 1)
        def _():
            l = acc_s[dv:dv + 1, :]
            ot = acc_s[0:dv, :] / l
            lse = m_s[0:1, :] + jnp.log2(l)
            for hh in range(hpg):
                o_ref[:, dv * hh:dv * hh + dv] = ot[:, tq * hh:tq * hh + tq].T.astype(MXU)
                lse_ref[hh] = lse[:, tq * hh:tq * hh + tq]

        if hosted:
            pl.when(last)(lambda: exchange.wait(xs_refs, land_refs, sems))

    sd = jax.ShapeDtypeStruct
    return pl.pallas_call(
        body, name=name, grid=grid,
        out_shape=(sd((s, hq * dv), MXU), sd((hq, 1, s), F32)) + (exchange.land_shapes if hosted else ()),
        in_specs=[pl.BlockSpec((hpg, tq, dq), lambda g, i, j: (g, i, 0)),
                  pl.BlockSpec((1, tkk, k.shape[2]), lambda g, i, j: (g, j, 0)),
                  pl.BlockSpec((1, dv, tkk), lambda g, i, j: (g, 0, j))] + (exchange.in_specs if hosted else []),
        out_specs=(pl.BlockSpec((tq, hpg * dv), lambda g, i, j: (i, g)),
                   pl.BlockSpec((hpg, 1, tq), lambda g, i, j: (g, 0, i))) + (exchange.out_specs if hosted else ()),
        scratch_shapes=[pltpu.VMEM((8, m_cols), F32), pltpu.VMEM((dvp, m_cols), F32)]
        + (list(exchange.sems) if hosted else []),
        compiler_params=_params(("arbitrary",) * 3 if hosted else ("parallel", "parallel", "arbitrary")),
    )(q, k, vt, *(exchange.srcs if hosted else []))


def _window_bias_t(hpg, slope_ref):
    t = WINDOW
    r = lax.broadcasted_iota(jnp.int32, (3 * t, t), 0)
    cq = lax.broadcasted_iota(jnp.int32, (3 * t, t), 1)
    arel = jnp.abs(r - t - cq)
    base = jnp.where(arel <= WINDOW, arel.astype(F32) * (-LOG2E), -jnp.inf)
    return jnp.concatenate([base * slope_ref[hh] for hh in range(hpg)], axis=1)


def _window_edges_t(bias, no_before, no_after):
    t = WINDOW
    r = lax.broadcasted_iota(jnp.int32, bias.shape, 0)
    out = ((r < t) & no_before) | ((r >= 2 * t) & no_after)
    return jnp.where(out, -jnp.inf, bias)


def _window_specs(kind, nb, nblk, d):
    t = WINDOW
    before = lambda i: jnp.clip(i * nb - 1, 0, nblk - 1)
    after = lambda i: jnp.clip((i + 1) * nb, 0, nblk - 1)
    if kind == "rows":
        return [pl.BlockSpec((1, t, d), lambda g, i: (g, before(i), 0)),
                pl.BlockSpec((1, nb * t, d), lambda g, i: (g, i, 0)),
                pl.BlockSpec((1, t, d), lambda g, i: (g, after(i), 0))]
    return [pl.BlockSpec((1, d, t), lambda g, i: (g, 0, before(i))),
            pl.BlockSpec((1, d, nb * t), lambda g, i: (g, 0, i)),
            pl.BlockSpec((1, d, t), lambda g, i: (g, 0, after(i)))]


def window_forward(q, k, vt, sink2, slopes, nb, name):
    hq, s, d = q.shape
    g_kv = k.shape[0]
    hpg = hq // g_kv
    t = WINDOW
    nblk = s // t
    steps = nblk // nb
    m_cols = hpg * t

    def body(q_ref, kp, ko, kn, vp, vo, vn, sink_ref, slope_ref, o_ref, lse_ref):
        i = pl.program_id(1)
        kk_all = jnp.concatenate([kp[0], ko[0], kn[0]], axis=0)
        vt_all = jnp.concatenate([vp[0], vo[0], vn[0]], axis=1)
        bias = _window_bias_t(hpg, slope_ref)
        sink_row = jnp.concatenate([jnp.broadcast_to(sink_ref[hh], (8, t)) for hh in range(hpg)], axis=1)
        sts = {}

        def score(u):
            qq = q_ref[:, t * u:t * (u + 1), :].reshape(m_cols, d)
            b_u = bias
            if u == 0 or u == nb - 1:
                b_u = _window_edges_t(bias, (i == 0) if u == 0 else False,
                                      (i == steps - 1) if u == nb - 1 else False)
            sts[u] = _mm_nt(kk_all[t * u:t * (u + 3), :], qq) + b_u

        for u in range(min(AHEAD, nb)):
            score(u)
        for u in range(nb):
            if u + AHEAD < nb:
                score(u + AHEAD)
            s3 = sts.pop(u).reshape(3 * t // 8, 8, m_cols)
            m8 = jnp.maximum(_col_max8(s3), sink_row)
            p = jnp.exp2(s3 - m8[None]).reshape(3 * t, m_cols)
            acc = _mm(_with_ones(vt_all[:, t * u:t * (u + 3)], 3 * t), p)
            l = acc[d:d + 1, :] + jnp.exp2(sink_row[0:1, :] - m8[0:1, :])
            ot = acc[0:d, :] / l
            lse = m8[0:1, :] + jnp.log2(l)
            for hh in range(hpg):
                o_ref[t * u:t * (u + 1), d * hh:d * hh + d] = ot[:, t * hh:t * hh + t].T.astype(MXU)
                lse_ref[hh, :, t * u:t * (u + 1)] = lse[:, t * hh:t * hh + t]

    sd = jax.ShapeDtypeStruct
    return pl.pallas_call(
        body, name=name, grid=(g_kv, steps),
        out_shape=(sd((s, hq * d), MXU), sd((hq, 1, s), F32)),
        in_specs=[pl.BlockSpec((hpg, nb * t, d), lambda g, i: (g, i, 0))]
        + _window_specs("rows", nb, nblk, d) + _window_specs("cols", nb, nblk, d)
        + [pl.BlockSpec((hpg, 1, 1), lambda g, i: (g, 0, 0))] * 2,
        out_specs=(pl.BlockSpec((nb * t, hpg * d), lambda g, i: (i, g)),
                   pl.BlockSpec((hpg, 1, nb * t), lambda g, i: (g, 0, i))),
        compiler_params=_params(("parallel", "parallel")),
    )(q, k, k, k, vt, vt, vt, sink2, slopes)


def window_backward(q, k, kt, v, do, lse, delta, slopes, nb, name):
    hq, s, d = q.shape
    g_kv = k.shape[0]
    hpg = hq // g_kv
    t = WINDOW
    nblk = s // t
    steps = nblk // nb
    m_cols = hpg * t

    def body(q_ref, kp, ko, kn, ktp, kto, ktn, vp, vo, vn, do_ref, lse_ref, dl_ref, slope_ref,
             dq_ref, dk_ref, dv_ref, dk_s, dv_s):
        i = pl.program_id(1)

        @pl.when(i == 0)
        def _():
            dk_ref[...] = jnp.zeros(dk_ref.shape, F32)
            dv_ref[...] = jnp.zeros(dv_ref.shape, F32)

        dk_s[...] = jnp.zeros(dk_s.shape, F32)
        dv_s[...] = jnp.zeros(dv_s.shape, F32)
        kk_all = jnp.concatenate([kp[0], ko[0], kn[0]], axis=0)
        vv_all = jnp.concatenate([vp[0], vo[0], vn[0]], axis=0)
        kkt_all = jnp.concatenate([ktp[0], kto[0], ktn[0]], axis=1)
        bias = _window_bias_t(hpg, slope_ref)
        qqs, dds, sts, dps = {}, {}, {}, {}

        def issue(u):
            rows = slice(t * u, t * (u + 1))
            keys = slice(t * u, t * (u + 3))
            qqs[u] = q_ref[:, rows, :].reshape(m_cols, d)
            dds[u] = jnp.concatenate([do_ref[rows, d * hh:d * hh + d] for hh in range(hpg)], axis=0)
            b_u = bias
            if u == 0 or u == nb - 1:
                b_u = _window_edges_t(bias, (i == 0) if u == 0 else False,
                                      (i == steps - 1) if u == nb - 1 else False)
            sts[u] = _mm_nt(kk_all[keys, :], qqs[u]) + b_u
            dps[u] = _mm_nt(vv_all[keys, :], dds[u])

        for u in range(min(AHEAD, nb)):
            issue(u)
        for u in range(nb):
            if u + AHEAD < nb:
                issue(u + AHEAD)
            rows = slice(t * u, t * (u + 1))
            keys = slice(t * u, t * (u + 3))
            lse_row = jnp.concatenate([lse_ref[hh, :, rows] for hh in range(hpg)], axis=1)
            dl_row = jnp.concatenate([dl_ref[hh, :, rows] for hh in range(hpg)], axis=1)
            p = jnp.exp2(sts[u] - lse_row)
            ds = p * (dps[u] - dl_row) * SCALE_A
            dv_s[keys, :] += _mm(p, dds[u])
            dk_s[keys, :] += _mm(ds, qqs[u])
            dqt = _mm(kkt_all[:, keys], ds)
            for hh in range(hpg):
                dq_ref[rows, d * hh:d * hh + d] = dqt[:, t * hh:t * hh + t].T.astype(dq_ref.dtype)
        tq = nb * t
        for src, r0, n in ((0, jnp.clip(i * nb - 1, 0, nblk - 1) * t, t), (t, i * tq, tq),
                           (t + tq, jnp.clip((i + 1) * nb, 0, nblk - 1) * t, t)):
            dst = pl.ds(pl.multiple_of(r0, t), n)
            dk_ref[0, dst, :] += dk_s[src:src + n, :] * (1.0 / SCALE2_A)
            dv_ref[0, dst, :] += dv_s[src:src + n, :]

    row_map = lambda g, i: (g, 0, i)
    sd = jax.ShapeDtypeStruct
    return pl.pallas_call(
        body, name=name, grid=(g_kv, steps),
        out_shape=(sd((s, hq * d), MXU), sd((g_kv, s, d), F32), sd((g_kv, s, d), F32)),
        in_specs=[pl.BlockSpec((hpg, nb * t, d), lambda g, i: (g, i, 0))]
        + _window_specs("rows", nb, nblk, d) + _window_specs("cols", nb, nblk, d) + _window_specs("rows", nb, nblk, d)
        + [pl.BlockSpec((nb * t, hpg * d), lambda g, i: (i, g)), pl.BlockSpec((hpg, 1, nb * t), row_map),
           pl.BlockSpec((hpg, 1, nb * t), row_map), pl.BlockSpec((hpg, 1, 1), lambda g, i: (g, 0, 0))],
        out_specs=(pl.BlockSpec((nb * t, hpg * d), lambda g, i: (i, g)),
                   pl.BlockSpec((1, s, d), lambda g, i: (g, 0, 0)),
                   pl.BlockSpec((1, s, d), lambda g, i: (g, 0, 0))),
        scratch_shapes=[pltpu.VMEM(((nb + 2) * t, d), F32), pltpu.VMEM(((nb + 2) * t, d), F32)],
        compiler_params=_params(("parallel", "arbitrary")),
    )(q, k, k, k, kt, kt, kt, v, v, v, do, lse, delta, slopes)


def flash_backward(q, k, kt, v, do, lse, delta, *, scale, dv, tq, tk, nsub, gq, name, split=None, exchange=None):
    hq, s, dq = q.shape
    g_kv = k.shape[0]
    hpg = hq // gq
    nq = s // tq
    tqq = tq * nsub
    nqs = s // tqq
    nkb = s // tk
    grid = (gq, nkb, nqs)
    hosted = exchange is not None
    m_cols = hpg * tq
    c = scale * LOG2E
    has_v = v is not None

    def body(*refs):
        it = iter(refs)
        q_ref, k_ref, kt_ref = next(it), next(it), next(it)
        v_ref = next(it) if has_v else None
        do_ref, lse_ref, dl_ref = next(it), next(it), next(it)
        nx = exchange.n if hosted else 0
        xs_refs = [next(it) for _ in range(nx)]
        dq_ref, dk_ref, dv_ref = next(it), next(it), next(it)
        land_refs = [next(it) for _ in range(nx)]
        dqt_s = next(it)
        sems = list(it)
        kj = pl.program_id(1)
        qi = pl.program_id(2)
        if hosted:
            first, last = _grid_edges(grid)
            pl.when(first)(lambda: exchange.start(xs_refs, land_refs, sems))

        @pl.when((kj == 0) & (qi == 0))
        def _():
            dqt_s[...] = jnp.zeros(dqt_s.shape, F32)

        @pl.when(qi == 0)
        def _():
            dk_ref[...] = jnp.zeros(dk_ref.shape, F32)
            dv_ref[...] = jnp.zeros(dv_ref.shape, F32)

        kk = k_ref[0]
        vv = v_ref[0] if has_v else kk[:, :dv]
        qqs, dds, sts, dps = {}, {}, {}, {}

        def issue(u):
            rows = slice(tq * u, tq * (u + 1))
            qqs[u] = q_ref[:, rows, :].reshape(m_cols, dq)
            dds[u] = jnp.concatenate([do_ref[rows, dv * hh:dv * hh + dv] for hh in range(hpg)], axis=0)
            sts[u] = _mm_nt(kk, qqs[u])
            dps[u] = _mm_nt(vv, dds[u])

        for u in range(min(AHEAD, nsub)):
            issue(u)
        dv_acc = dv_ref[0]
        dk_acc = dk_ref[0]
        for u in range(nsub):
            if u + AHEAD < nsub:
                issue(u + AHEAD)
            rows = slice(tq * u, tq * (u + 1))
            lse_row = jnp.concatenate([lse_ref[hh, :, rows] for hh in range(hpg)], axis=1)
            dl_row = jnp.concatenate([dl_ref[hh, :, rows] for hh in range(hpg)], axis=1)
            p = jnp.exp2(sts[u] - lse_row)
            ds = p * (dps[u] - dl_row) * scale
            dv_acc = dv_acc + _mm(p, dds[u])
            dk_acc = dk_acc + _mm(ds, qqs[u])
            dqt = _mm(kt_ref[0], ds)
            for hh in range(hpg):
                dqt_s[qi * nsub + u, dq * hh:dq * hh + dq, :] += dqt[:, tq * hh:tq * hh + tq]
        dv_ref[0] = dv_acc
        dk_ref[0] = jnp.where(qi == nqs - 1, dk_acc * (1.0 / c), dk_acc)

        @pl.when((kj == nkb - 1) & (qi == nqs - 1))
        def _():
            def emit(t, carry):
                r0 = pl.multiple_of(t * tq, tq)
                for hh in range(hpg):
                    blk = dqt_s[t, dq * hh:dq * hh + dq, :].T
                    if split is None:
                        dq_ref[pl.ds(r0, tq), dq * hh:dq * hh + dq] = blk
                    else:
                        rest = dq - split
                        dq_ref[pl.ds(r0, tq), split * hh:split * (hh + 1)] = blk[:, 0:split]
                        dq_ref[pl.ds(r0, tq), hpg * split + rest * hh:hpg * split + rest * (hh + 1)] = blk[:, split:]
                return carry

            lax.fori_loop(0, nq, emit, 0)

        if hosted:
            pl.when(last)(lambda: exchange.wait(xs_refs, land_refs, sems))

    kv_of = lambda g: g * g_kv // gq
    in_specs = [pl.BlockSpec((hpg, tqq, dq), lambda g, kj, qi: (g, qi, 0)),
                pl.BlockSpec((1, tk, dq), lambda g, kj, qi: (kv_of(g), kj, 0)),
                pl.BlockSpec((1, dq, tk), lambda g, kj, qi: (kv_of(g), 0, kj))]
    args = [q, k, kt]
    if has_v:
        in_specs.append(pl.BlockSpec((1, tk, dv), lambda g, kj, qi: (kv_of(g), kj, 0)))
        args.append(v)
    row_map = lambda g, kj, qi: (g, 0, qi)
    in_specs += [pl.BlockSpec((tqq, hpg * dv), lambda g, kj, qi: (qi, g)),
                 pl.BlockSpec((hpg, 1, tqq), row_map), pl.BlockSpec((hpg, 1, tqq), row_map)]
    args += [do, lse, delta]
    if hosted:
        in_specs += exchange.in_specs
        args += exchange.srcs
    sd = jax.ShapeDtypeStruct
    return pl.pallas_call(
        body, name=name, grid=grid,
        out_shape=(sd((s, hq * dq), F32), sd((gq, s, dq), F32), sd((gq, s, dv), F32))
        + (exchange.land_shapes if hosted else ()),
        in_specs=in_specs,
        out_specs=(pl.BlockSpec((s, hpg * dq), lambda g, kj, qi: (0, g)),
                   pl.BlockSpec((1, tk, dq), lambda g, kj, qi: (g, kj, 0)),
                   pl.BlockSpec((1, tk, dv), lambda g, kj, qi: (g, kj, 0))) + (exchange.out_specs if hosted else ()),
        scratch_shapes=[pltpu.VMEM((nq, hpg * dq, tq), F32)] + (list(exchange.sems) if hosted else []),
        compiler_params=_params(("arbitrary",) * 3 if hosted else ("parallel", "arbitrary", "arbitrary")),
    )(*args)


def mixer_out_backward(dx, y, mod, pairs, w_out, delta_heads, name, lse=None, sink=None):
    s, d = dx.shape
    tm = min(ROW_TILE, s)
    n = len(pairs)
    widths = [o.shape[1] for o, _ in pairs]
    n_delta = sum(1 for h in delta_heads if h)
    with_sink = lse is not None

    def body(*refs):
        it = iter(refs)
        dx_ref, y_ref, mod_ref, wt_ref = next(it), next(it), next(it), next(it)
        pr = [next(it) for _ in range(2 * n)]
        lse_ref = next(it) if with_sink else None
        sink_ref = next(it) if with_sink else None
        outs = [next(it) for _ in range(2 * n)]
        dl_refs = [next(it) for _ in range(n_delta)]
        dgate_ref, dw_ref = next(it), next(it)
        dsink_ref = next(it) if with_sink else None
        dw_acc = next(it)

        @pl.when(pl.program_id(0) == 0)
        def _():
            dgate_ref[...] = jnp.zeros(dgate_ref.shape, F32)
            dw_acc[...] = jnp.zeros(dw_acc.shape, F32)
            if with_sink:
                dsink_ref[...] = jnp.zeros(dsink_ref.shape, F32)

        dxo = dx_ref[...]
        dgate_ref[...] += jnp.sum(dxo * y_ref[...].astype(F32), axis=0, keepdims=True)
        dy = (dxo * mod_ref[2:3, :]).astype(MXU)
        dmix = _mm_nt(dy, wt_ref[...])
        r0 = 0
        di = 0
        for i in range(n):
            o = pr[2 * i][...].astype(F32)
            g = pr[2 * i + 1][...].astype(F32)
            dm = dmix[:, r0:r0 + widths[i]]
            sg = _sigmoid(g)
            act = g * sg
            do = dm * act
            outs[2 * i][...] = do.astype(MXU)
            outs[2 * i + 1][...] = (dm * o * (sg * (1.0 + g * (1.0 - sg)))).astype(MXU)
            dw_acc[r0:r0 + widths[i], :] += _mm_tn(o * act, dy)
            if delta_heads[i]:
                dlt = _group_sums_t(do * o, HD)[0:delta_heads[i], :]
                dl_refs[di][...] = dlt
                if with_sink:
                    ps = jnp.exp2(sink_ref[...] - lse_ref[...])
                    dsink_ref[...] += -jnp.sum(ps * dlt, axis=1, keepdims=True)
                di += 1
            r0 += widths[i]

        @pl.when(pl.program_id(0) == pl.num_programs(0) - 1)
        def _():
            for j in range(N_DEV):
                dw_ref[j] = dw_acc[j * dw_block:(j + 1) * dw_block, :].astype(MXU)

    dw_block = sum(widths) // N_DEV
    flat = [a for p in pairs for a in p]
    sd = jax.ShapeDtypeStruct
    in_specs = [_row_spec(tm, d), _row_spec(tm, d), _full_spec(mod.shape), _full_spec(w_out.shape)]
    in_specs += [_row_spec(tm, a.shape[1]) for a in flat]
    args = [dx, y, mod, w_out] + flat
    if with_sink:
        nh = lse.shape[0]
        in_specs += [_rows_spec(nh, tm), _full_spec(sink.shape)]
        args += [lse, sink]
    out_shape = [sd((s, a.shape[1]), MXU) for a in flat]
    out_specs = [_row_spec(tm, a.shape[1]) for a in flat]
    for h in delta_heads:
        if h:
            out_shape.append(sd((h, s), F32))
            out_specs.append(_rows_spec(h, tm))
    out_shape += [sd((1, d), F32), sd((N_DEV, dw_block, d), MXU)]
    out_specs += [_full_spec((1, d)), _full_spec((N_DEV, dw_block, d))]
    if with_sink:
        out_shape.append(sd((lse.shape[0], 1), F32))
        out_specs.append(_full_spec((lse.shape[0], 1)))
    return pl.pallas_call(
        body, name=name, grid=(s // tm,), out_shape=tuple(out_shape), in_specs=in_specs, out_specs=tuple(out_specs),
        scratch_shapes=[pltpu.VMEM((sum(widths), d), F32)], compiler_params=_params(("arbitrary",)),
    )(*args)


def latent_out_backward(d_ob, o_lat, w_uv):
    s = o_lat.shape[0]
    tm = min(ROW_TILE, s)

    def body(d_ref, o_ref, uv_ref, dol_ref, dl_ref, duv_ref, prod_s):
        @pl.when(pl.program_id(0) == 0)
        def _():
            duv_ref[...] = jnp.zeros(duv_ref.shape, F32)

        for hh in range(B_HEADS):
            dh = d_ref[:, HD * hh:HD * hh + HD]
            ol = o_ref[:, B_KV_LORA * hh:B_KV_LORA * (hh + 1)].astype(F32)
            dol = _mm_nt(dh, uv_ref[hh])
            dol_ref[:, B_KV_LORA * hh:B_KV_LORA * (hh + 1)] = dol.astype(MXU)
            prod_s[:, B_KV_LORA * hh:B_KV_LORA * (hh + 1)] = dol * ol
            duv_ref[:, HD * hh:HD * hh + HD] += _mm_tn(ol, dh)
        dl_ref[...] = _group_sums_t(prod_s[...], B_KV_LORA)[0:B_HEADS, :]

    sd = jax.ShapeDtypeStruct
    duv_shape = (B_KV_LORA, B_HEADS * HD)
    return pl.pallas_call(
        body, name="latent_out_backward", grid=(s // tm,),
        out_shape=(sd(o_lat.shape, MXU), sd((B_HEADS, s), F32), sd(duv_shape, F32)),
        in_specs=[_row_spec(tm, d_ob.shape[1]), _row_spec(tm, o_lat.shape[1]), _full_spec(w_uv.shape)],
        out_specs=(_row_spec(tm, o_lat.shape[1]), _rows_spec(B_HEADS, tm), _full_spec(duv_shape)),
        scratch_shapes=[pltpu.VMEM((tm, o_lat.shape[1]), F32)],
        compiler_params=_params(("arbitrary",)),
    )(d_ob, o_lat, w_uv)


def even_prep_backward(dqa, dka, dva, dqb, dkb, dvb, qa_raw, ka_raw, cq_raw, ckv_raw,
                       gq, gk, qln, kvln, w_uq_t, uk_bd, bd, cos_a, sin_a, cos_t, sin_t):
    s = qa_raw.shape[0]
    tm = min(ROW_TILE, s)
    half_lat = B_KV_LORA * B_HEADS // 2
    half_w = dqb.shape[1] // 2

    def body(dqa_ref, dka_ref, dva_ref, dqb_ref, dkb_ref, dvb_ref, qa_ref, ka_ref, cq_ref, ckv_ref,
             gq_ref, gk_ref, qln_ref, kvln_ref, uqt_ref, ukbd_ref, bd_ref, ca_ref, sa_ref, ct_ref, st_ref,
             pqa, pka, pva, pcq, pckv, pkr, gqn, gkn, gqln, gkvln, guq, guk):
        @pl.when(pl.program_id(0) == 0)
        def _():
            for r in (gqn, gkn, gqln, gkvln, guq, guk):
                r[...] = jnp.zeros(r.shape, F32)

        ca, sa, ct, st = ca_ref[...], sa_ref[...], ct_ref[...], st_ref[...]
        wide = lambda t, n: jnp.concatenate([t] * n, axis=1)
        rows = lambda a: jnp.sum(a, axis=0, keepdims=True)
        dx, dg = _head_norm_bwd(_rope_t(dqa_ref[...], wide(ca, 4), wide(sa, 4), 32), qa_ref[...], gq_ref[...],
                                bd_ref, HD)
        pqa[...] = dx.astype(MXU)
        gqn[...] += rows(dg)
        dk_all = jnp.concatenate([dka_ref[g] for g in range(A_KV)], axis=1)
        dx, dg = _head_norm_bwd(_rope_t(dk_all, ca, sa, 32), ka_ref[...], gk_ref[...], bd_ref[0:128, 0:128], HD)
        pka[...] = dx.astype(MXU)
        gkn[...] += rows(dg)
        pva[...] = jnp.concatenate([dva_ref[g] for g in range(A_KV)], axis=1).astype(MXU)
        cq_raw = cq_ref[...]
        cq_n = cq_raw * _rms(cq_raw) * qln_ref[...]
        qb = _mm_nt(cq_n, uqt_ref[...])
        d_lat = jnp.concatenate([dqb_ref[:, 0:half_lat], dqb_ref[:, half_w:half_w + half_lat]], axis=1)
        d_rope = jnp.concatenate([dqb_ref[:, half_lat:half_w], dqb_ref[:, half_w + half_lat:]], axis=1)
        for hh in range(B_HEADS):
            guk[:, B_NOPE * hh:B_NOPE * (hh + 1)] += _mm_tn(d_lat[:, B_KV_LORA * hh:B_KV_LORA * (hh + 1)],
                                                            qb[:, B_NOPE * hh:B_NOPE * (hh + 1)])
        dqb_all = jnp.concatenate([_mm_nt(d_lat, ukbd_ref[...]),
                                   _rope_t(d_rope, wide(ct, 2), wide(st, 2), 32)], axis=1)
        guq[...] += _mm_tn(dqb_all, cq_n)
        dx, dg = _rms_bwd(_mm(dqb_all, uqt_ref[...]), cq_raw, qln_ref[...])
        pcq[...] = dx.astype(MXU)
        gqln[...] += rows(dg)
        dkb_sum = dkb_ref[0] + dkb_ref[1]
        dckv = dkb_sum[:, 0:B_KV_LORA] + dvb_ref[0] + dvb_ref[1]
        dx, dg = _rms_bwd(dckv, ckv_ref[...], kvln_ref[...])
        pckv[...] = dx.astype(MXU)
        gkvln[...] += rows(dg)
        pkr[...] = _rope_t(dkb_sum[:, B_KV_LORA:B_QK], ct[:, 0:B_ROPE], st[:, 0:B_ROPE], 32).astype(MXU)

    sd = jax.ShapeDtypeStruct
    consts = [gq, gk, qln, kvln, w_uq_t, uk_bd, bd]
    in_specs = [_row_spec(tm, 512), _head_spec(A_KV, tm, HD), _head_spec(A_KV, tm, HD),
                _row_spec(tm, dqb.shape[1]), _head_spec(2, tm, B_QK), _head_spec(2, tm, B_KV_LORA),
                _row_spec(tm, 512), _row_spec(tm, 128), _row_spec(tm, B_Q_LORA), _row_spec(tm, B_KV_LORA)]
    in_specs += [_full_spec(a.shape) for a in consts] + [_row_spec(tm, 128)] * 4
    small = [sd(gq.shape, F32), sd(gk.shape, F32), sd(qln.shape, F32), sd(kvln.shape, F32), sd(w_uq_t.shape, F32),
             sd((B_KV_LORA, B_HEADS * B_NOPE), F32)]
    out_shape = (sd((s, 512), MXU), sd((s, 128), MXU), sd((s, 128), MXU), sd((s, B_Q_LORA), MXU),
                 sd((s, B_KV_LORA), MXU), sd((s, B_ROPE), MXU), *small)
    out_specs = (_row_spec(tm, 512), _row_spec(tm, 128), _row_spec(tm, 128), _row_spec(tm, B_Q_LORA),
                 _row_spec(tm, B_KV_LORA), _row_spec(tm, B_ROPE), *[_full_spec(a.shape) for a in small])
    return pl.pallas_call(
        body, name="even_prep_backward", grid=(s // tm,), out_shape=out_shape, in_specs=in_specs, out_specs=out_specs,
        compiler_params=_params(("arbitrary",)),
    )(dqa, dka, dva, dqb, dkb, dvb, qa_raw, ka_raw, cq_raw, ckv_raw, *consts, cos_a, sin_a, cos_t, sin_t)


def in_proj_backward(x, mod, nw, pieces, name, *, dx_out=None, w_in_t=None, dw_rows=None, exchange=None):
    s, d = x.shape
    tm = min(ROW_TILE, s)
    grid = (s // tm,)
    n = len(pieces)
    cols = [c for _, c in pieces]
    want_dx = w_in_t is not None
    want_dw = dw_rows is not None
    n_cols = sum(c1 - c0 for c0, c1 in cols)
    dw_block = n_cols // N_DEV
    hosted = exchange is not None
    nx = exchange.n if hosted else 0

    def body(*refs):
        it = iter(refs)
        x_ref, mod_ref, nw_ref = next(it), next(it), next(it)
        dxo_ref, wt_ref = (next(it), next(it)) if want_dx else (None, None)
        p_refs = [next(it) for _ in range(n)]
        xs_refs = [next(it) for _ in range(nx)]
        dx_ref, dv_ref = (next(it), next(it)) if want_dx else (None, None)
        dw_ref = next(it) if want_dw else None
        land_refs = [next(it) for _ in range(nx)]
        xbuf, xsem = next(it), next(it)
        acc_ref = next(it) if want_dx else None
        dw_acc = next(it) if want_dw else None
        sems = list(it)
        first, last = _grid_edges(grid)
        if hosted:
            pl.when(first)(lambda: exchange.start(xs_refs, land_refs, sems))

        @pl.when(first)
        def _():
            if want_dw:
                dw_acc[...] = jnp.zeros(dw_acc.shape, F32)
            if want_dx:
                acc_ref[...] = jnp.zeros(acc_ref.shape, F32)

        step = pl.program_id(0)

        def fetch(t):
            slot = t % X_SLOTS
            return pltpu.make_async_copy(x_ref.at[pl.ds(pl.multiple_of(t * tm, tm), tm), :], xbuf.at[slot],
                                         xsem.at[slot])

        @pl.when(first)
        def _():
            for t in range(min(X_SLOTS - 1, grid[0])):
                fetch(t).start()

        @pl.when(step + (X_SLOTS - 1) < grid[0])
        def _():
            fetch(step + (X_SLOTS - 1)).start()

        fetch(step).wait()
        x = xbuf[step % X_SLOTS]
        xn, g1, h = _modulated(x, mod_ref, nw_ref)
        hb = h.astype(MXU)
        dh = jnp.zeros((tm, d), F32)
        for k, (pr, (c0, c1)) in enumerate(zip(p_refs, cols)):
            if len(pr.shape) == 3:
                pc = jnp.concatenate([pr[g] for g in range(pr.shape[0])], axis=1).astype(MXU)
            else:
                pc = pr[...].astype(MXU)
            if want_dx:
                dh = dh + jnp.dot(pc, wt_ref[c0:c1, :], preferred_element_type=F32)
            if want_dw:
                r0, r1 = dw_rows[k]
                dw_acc[r0:r1, :] += _mm_tn(pc, hb)
        if want_dx:
            acc_ref[0:1, :] += jnp.sum(dh, axis=0, keepdims=True)
            acc_ref[1:2, :] += jnp.sum(dh * xn, axis=0, keepdims=True)
            dxn = dh * g1
            dx_ref[...] = dxo_ref[...] + _rms(x) * (dxn - xn * jnp.mean(dxn * xn, axis=-1, keepdims=True))

        @pl.when(last)
        def _():
            if want_dx:
                dg1 = acc_ref[1:2, :]
                dv_ref[0:1, :] = acc_ref[0:1, :]
                dv_ref[1:2, :] = dg1 * nw_ref[...]
                dv_ref[2:3, :] = dg1 * (1.0 + mod_ref[1:2, :])
                dv_ref[3:4, :] = jnp.zeros((1, d), F32)
            if want_dw:
                for j in range(N_DEV):
                    dw_ref[j] = dw_acc[j * dw_block:(j + 1) * dw_block, :].astype(MXU)

        if hosted:
            pl.when(last)(lambda: exchange.wait(xs_refs, land_refs, sems))

    arrs = [a for a, _ in pieces]
    sd = jax.ShapeDtypeStruct
    args = [x, mod, nw] + ([dx_out, w_in_t] if want_dx else []) + arrs + (exchange.srcs if hosted else [])
    in_specs = [pl.BlockSpec(memory_space=pl.ANY), _full_spec(mod.shape), _full_spec(nw.shape)]
    in_specs += [_row_spec(tm, d), _full_spec(w_in_t.shape)] if want_dx else []
    in_specs += [_row_spec(tm, a.shape[1]) if a.ndim == 2 else _head_spec(a.shape[0], tm, a.shape[2]) for a in arrs]
    in_specs += exchange.in_specs if hosted else []
    out_shape, out_specs = [], []
    scratch = [pltpu.VMEM((X_SLOTS, tm, d), F32), pltpu.SemaphoreType.DMA((X_SLOTS,))]
    if want_dx:
        out_shape += [sd((s, d), F32), sd((4, d), F32)]
        out_specs += [_row_spec(tm, d), _full_spec((4, d))]
        scratch.append(pltpu.VMEM((8, d), F32))
    if want_dw:
        out_shape.append(sd((N_DEV, dw_block, d), MXU))
        out_specs.append(_full_spec((N_DEV, dw_block, d)))
        scratch.append(pltpu.VMEM((n_cols, d), F32))
    if hosted:
        out_shape += list(exchange.land_shapes)
        out_specs += list(exchange.out_specs)
        scratch += list(exchange.sems)
    return pl.pallas_call(
        body, name=name, grid=grid, out_shape=tuple(out_shape), in_specs=in_specs, out_specs=tuple(out_specs),
        scratch_shapes=scratch, compiler_params=_params(("arbitrary",)),
    )(*args)


def ada_weight_grad(c_all, dmod_cols):
    d = c_all.shape[1]
    w = dmod_cols.shape[2]

    def body(c_ref, dm_ref, out_ref):
        ca = _silu(c_ref[...])
        for l in range(2):
            out_ref[l] = _mm_tn(ca, dm_ref[l])

    return pl.pallas_call(
        body, name="ada_weight_grad",
        out_shape=jax.ShapeDtypeStruct((2, d, w), F32),
        compiler_params=pltpu.CompilerParams(vmem_limit_bytes=VMEM_LIMIT),
    )(c_all, dmod_cols)


def _slot_sum(g_ref):
    g = g_ref[0].astype(F32)
    for k in range(1, g_ref.shape[0]):
        g = g + g_ref[k].astype(F32)
    return g


def _adamw_math(g, w, m, v):
    m_new = ADAM_B1 * m + (1.0 - ADAM_B1) * g
    v_new = ADAM_B2 * v + (1.0 - ADAM_B2) * (g * g)
    m_hat = m_new / (1.0 - ADAM_B1 ** ADAM_STEP)
    v_hat = v_new / (1.0 - ADAM_B2 ** ADAM_STEP)
    return -ADAM_LR * (m_hat / (jnp.sqrt(v_hat) + ADAM_EPS) + ADAM_WD * w), m_new, v_new


def adamw_small(g_alls, ws, ms, vs, loss_all):
    n = len(ws)

    def body(*refs):
        g_refs, w_refs, m_refs, v_refs = (refs[i * n:(i + 1) * n] for i in range(4))
        loss_ref = refs[4 * n]
        outs = refs[4 * n + 1:]
        for i in range(n):
            g = _slot_sum(g_refs[i])
            outs[i][...] = g
            outs[n + i][...], outs[2 * n + i][...], outs[3 * n + i][...] = _adamw_math(
                g, w_refs[i][...], m_refs[i][...], v_refs[i][...])
        outs[4 * n][...] = _slot_sum(loss_ref)

    sds = [jax.ShapeDtypeStruct(w.shape, F32) for w in ws]
    res = pl.pallas_call(
        body, name="adamw_small", out_shape=tuple(sds * 4) + (jax.ShapeDtypeStruct(loss_all.shape[1:], F32),),
        compiler_params=pltpu.CompilerParams(vmem_limit_bytes=VMEM_LIMIT),
    )(*g_alls, *ws, *ms, *vs, loss_all)
    return [res[i * n:(i + 1) * n] for i in range(4)], res[4 * n]


def adamw_rows(g_slots, w, m, v, name):
    n, r, lanes = g_slots.shape
    fits = [t for t in range(16, r + 1, 16) if r % t == 0 and t * lanes <= ADAM_TILE]
    tr = max(fits) if fits else r
    def body(g_ref, w_ref, m_ref, v_ref, go, do, mo, vo):
        g = _slot_sum(g_ref)
        go[...] = g
        do[...], mo[...], vo[...] = _adamw_math(g, w_ref[...], m_ref[...], v_ref[...])

    row = pl.BlockSpec((tr, lanes), lambda i: (i, 0))
    sd = jax.ShapeDtypeStruct((r, lanes), F32)
    return pl.pallas_call(
        body, name=name, grid=(r // tr,), out_shape=(sd, sd, sd, sd),
        in_specs=[pl.BlockSpec((n, tr, lanes), lambda i: (0, i, 0)), row, row, row],
        out_specs=(row, row, row, row),
        compiler_params=_params(("parallel",)),
    )(g_slots, w, m, v)


def _rope_tables(s):
    def cs(pos, dim):
        inv = ROPE_THETA ** (-np.arange(0, dim, 2, dtype=np.float32) / dim)
        ang = pos.astype(np.float32)[:, None] * inv.astype(np.float32)[None, :]
        return np.cos(ang), np.sin(ang)

    rows = s // GRID_W
    row = np.repeat(np.arange(rows), GRID_W)
    col = np.tile(np.arange(GRID_W), rows)
    cr, sr = cs(row, HD // 2)
    cc, sc = cs(col, HD // 2)
    ct, st = cs(np.arange(s), B_ROPE)
    tables = (np.concatenate([cr, cr, cc, cc] * 2, axis=-1), np.concatenate([-sr, sr, -sc, sc] * 2, axis=-1),
              np.concatenate([ct, ct] * 4, axis=-1), np.concatenate([-st, st] * 4, axis=-1))
    return tuple(jnp.asarray(t, F32) for t in tables)


def _even_rows_to_kernel(wt):
    return jnp.concatenate([wt[:1664], wt[1696:], wt[1664:1696]], axis=0)


def _uq_rows_to_kernel(wt):
    r = wt.reshape(B_HEADS, B_NOPE + B_ROPE, -1)
    return jnp.concatenate([r[:, :B_NOPE].reshape(B_HEADS * B_NOPE, -1), r[:, B_NOPE:].reshape(B_HEADS * B_ROPE, -1)])


def _uq_rows_to_reference(wt):
    nope = wt[:B_HEADS * B_NOPE].reshape(B_HEADS, B_NOPE, -1)
    rope = wt[B_HEADS * B_NOPE:].reshape(B_HEADS, B_ROPE, -1)
    return jnp.concatenate([nope, rope], axis=1).reshape(B_HEADS * (B_NOPE + B_ROPE), -1)


def _shard_t(w):
    return jnp.transpose(w[0])


def _unshard_t(wt, like):
    return jnp.transpose(wt)[None].reshape(like.shape)


def kernel(x, c, norm_w, ada_w, ada_b, even_w_in, a_q_norm, a_k_norm, b_q_lora_norm, b_kv_lora_norm, b_w_uq, b_w_uk, b_w_uv, even_w_out, odd_w_in, c_sink, odd_w_out, final_norm, loss_target, m_norm_w, m_ada_w, m_ada_b, m_even_w_in, m_a_q_norm, m_a_k_norm, m_b_q_lora_norm, m_b_kv_lora_norm, m_b_w_uq, m_b_w_uk, m_b_w_uv, m_even_w_out, m_odd_w_in, m_c_sink, m_odd_w_out, m_final_norm, v_norm_w, v_ada_w, v_ada_b, v_even_w_in, v_a_q_norm, v_a_k_norm, v_b_q_lora_norm, v_b_kv_lora_norm, v_b_w_uq, v_b_w_uk, v_b_w_uv, v_even_w_out, v_odd_w_in, v_c_sink, v_odd_w_out, v_final_norm):
    s, d = x.shape[1], x.shape[2]
    x0 = x[0]
    target = loss_target[0]
    me_flat = 4 * lax.axis_index("x") + 2 * lax.axis_index("y") + lax.axis_index("c")

    wcols = ada_w.shape[2]
    bias_cols = lax.dynamic_slice_in_dim(ada_b.reshape(2, N_DEV, wcols), me_flat, 1, axis=1)
    call, modp, (g_in_e, g_uq) = ada_forward(
        jnp.broadcast_to(c, (8, d)), ada_w, bias_cols,
        Gather([_shard_t(even_w_in).astype(MXU), _shard_t(b_w_uq).astype(MXU)]))
    wt_in_e = _even_rows_to_kernel(g_in_e.reshape(-1, d))
    wt_uq = _uq_rows_to_kernel(g_uq.reshape(-1, B_Q_LORA))
    later_exchange = Exchange([_shard_t(odd_w_in).astype(MXU), even_w_out[0].astype(MXU),
                               odd_w_out[0].astype(MXU)], scatter=False)
    uk_bd = (jnp.eye(B_HEADS, dtype=F32)[:, None, :, None] * jnp.transpose(b_w_uk[0], (1, 2, 0))[:, :, None, :]
             ).reshape(B_HEADS * B_NOPE, B_HEADS * B_KV_LORA).astype(MXU)
    head_bd = jnp.asarray(np.kron(np.eye(A_HEADS), np.ones((HD, HD))), MXU)
    gq_full, gk_full = jnp.tile(a_q_norm, (1, A_HEADS)), jnp.tile(a_k_norm, (1, A_KV))
    w_uv = jnp.transpose(b_w_uv[0], (1, 0, 2)).astype(MXU)

    c_all = call[:, 0, :]
    mod = jnp.transpose(modp[:, :, 0, :], (1, 0, 2)).reshape(2, 3, d)
    mod_e, mod_o = mod[0], mod[1]
    nw_e, nw_o = norm_w[0:1], norm_w[1:2]

    cos_a, sin_a, cos_t, sin_t = _rope_tables(s)
    slopes = (2.0 ** (-8.0 * jnp.arange(1, C_HEADS + 1, dtype=F32) / C_HEADS)).reshape(C_HEADS, 1, 1)
    sink2 = c_sink.reshape(C_HEADS, 1, 1) * LOG2E

    (qa, ka, va, qb, kb, kat, vat, kbt, qa_raw, ka_raw, cq_raw, ckv_raw, ga, gb) = even_in_forward(
        x0, mod_e, nw_e, wt_in_e, gq_full, gk_full, b_q_lora_norm, b_kv_lora_norm, wt_uq, uk_bd, head_bd,
        cos_a, sin_a, cos_t, sin_t)
    tk_dense = min(512, s)
    tq_dense = min(256, s)
    fwd_sub = min(8, s // tk_dense)
    bwd_sub_a = min(16, s // tq_dense)
    bwd_sub_b = min(8, s // tq_dense)
    oa, lse_a, g_in_o, g_out_e, g_out_o = flash_forward(
        qa, ka, vat, dv=HD, tq=tq_dense, tk=tk_dense, nsub=fwd_sub, name="attn_a_fwd",
        exchange=later_exchange)
    wt_in_o = g_in_o.reshape(-1, d)
    w_out_e = g_out_e.reshape(-1, d)
    w_out_o = g_out_o.reshape(-1, d)
    o_lat, lse_b = flash_forward(qb, kb, kbt, dv=B_KV_LORA, tq=min(128, s), tk=tk_dense, nsub=fwd_sub,
                                 name="attn_b_fwd")
    ob = latent_out_forward(o_lat, w_uv)
    x1, y_e = mixer_out_forward(x0, mod_e, [(oa, ga), (ob, gb)], w_out_e, "even_out_fwd")

    qc, kc, vc, kct, vct, gc = odd_in_forward(x1, mod_o, nw_o, wt_in_o)
    win_sub = min(8, s // WINDOW)
    oc, lse_c = window_forward(qc, kc, vct, sink2, slopes, win_sub, "attn_c_fwd")
    dx2, y_o, loss_lanes, d_final = mixer_out_forward(x1, mod_o, [(oc, gc)], w_out_o, "odd_out_fwd_loss",
                                                      loss=(target, final_norm.reshape(1, d)))

    loss_part = (0.5 / d) * jnp.sum(loss_lanes)

    doc, dgc, delta_c, dgate_o, dw_out_o, dsink = mixer_out_backward(
        dx2, y_o, mod_o, [(oc, gc)], w_out_o, [C_HEADS], "odd_out_bwd", lse=lse_c.reshape(C_HEADS, s),
        sink=sink2.reshape(C_HEADS, 1))
    rows3 = lambda t: t.reshape(t.shape[0], 1, s)
    dqc, dkc, dvc = window_backward(qc, kc, kct, vc, doc, lse_c, rows3(delta_c), slopes, win_sub, "attn_c_bwd")
    dx1, dvec_o, dwt_in_o = in_proj_backward(
        x1, mod_o, nw_o, [(dqc, O_Q), (dkc, O_K), (dvc, O_V), (dgc, O_G)], "odd_in_bwd",
        dx_out=dx2, w_in_t=wt_in_o, dw_rows=[O_Q, O_K, O_V, O_G])

    doa, dga, dob, dgb, delta_a, dgate_e, dw_out_e = mixer_out_backward(
        dx1, y_e, mod_e, [(oa, ga), (ob, gb)], w_out_e, [A_HEADS, 0], "even_out_bwd")
    d_olat, delta_b, dw_uv = latent_out_backward(dob, o_lat, w_uv)
    blocks = lambda g: g.astype(MXU).reshape(N_DEV, g.shape[0] // N_DEV, g.shape[1])
    even_pieces = lambda: [(pqa, E_QA), (pka, E_KA), (pva, E_VA), (dga, E_GA), (pcq, E_CQ), (pckv, E_CKV),
                           (dgb, E_GB), (pkr, E_KR)]
    scatter_odd = Exchange([dwt_in_o, dw_out_o], True)
    scatter_out_e = Exchange([dw_out_e], True)
    dqb, dkb, dvb, l_in_o, l_out_o = flash_backward(
        qb, kb, kbt, None, d_olat, lse_b, rows3(delta_b), scale=SCALE_B, dv=B_KV_LORA,
        tq=tq_dense, tk=tk_dense, nsub=bwd_sub_b, gq=2, name="attn_b_bwd", split=B_KV_LORA, exchange=scatter_odd)
    dqa, dka, dva, l_out_e = flash_backward(
        qa, ka, kat, va, doa, lse_a, rows3(delta_a), scale=SCALE_A, dv=HD,
        tq=tq_dense, tk=tk_dense, nsub=bwd_sub_a, gq=A_KV, name="attn_a_bwd", exchange=scatter_out_e)
    (pqa, pka, pva, pcq, pckv, pkr, g_qn, g_kn, g_qln, g_kvln, dwt_uq, dw_uk) = even_prep_backward(
        dqa, dka, dva, dqb, dkb, dvb, qa_raw, ka_raw, cq_raw, ckv_raw,
        gq_full, gk_full, b_q_lora_norm, b_kv_lora_norm, wt_uq, uk_bd, head_bd, cos_a, sin_a, cos_t, sin_t)
    g_qn = jnp.sum(g_qn.reshape(A_HEADS, HD), axis=0)
    g_kn = jnp.sum(g_kn.reshape(A_KV, HD), axis=0)
    dwt_in_e, l_uk, l_uv = in_proj_backward(
        x0, mod_e, nw_e, even_pieces(), "even_in_bwd_dw",
        dw_rows=[E_QA, E_KA, E_VA, E_GA, E_CQ, E_CKV, (1696, 2208), (1664, 1696)],
        exchange=Exchange([dw_uk.astype(MXU), dw_uv.astype(MXU)], scatter=False))
    dx0, dvec_e, l_in_e, l_uq = in_proj_backward(
        x0, mod_e, nw_e, even_pieces(), "even_in_bwd_dx", dx_out=dx1, w_in_t=wt_in_e,
        exchange=Exchange([dwt_in_e, blocks(_uq_rows_to_reference(dwt_uq))], True))

    dmod = jnp.stack([jnp.concatenate([dvec_e[0], dvec_e[1], dgate_e[0]]),
                      jnp.concatenate([dvec_o[0], dvec_o[1], dgate_o[0]])])
    d_norm_w = jnp.stack([dvec_e[2], dvec_o[2]])
    small_names = ["norm_w", "ada_b", "a_q_norm", "a_k_norm", "b_q_lora_norm", "b_kv_lora_norm", "b_w_uk", "b_w_uv",
                   "c_sink", "final_norm"]
    small_w = [norm_w, ada_b, a_q_norm, a_k_norm, b_q_lora_norm, b_kv_lora_norm, b_w_uk, b_w_uv, c_sink, final_norm]
    small_m = [m_norm_w, m_ada_b, m_a_q_norm, m_a_k_norm, m_b_q_lora_norm, m_b_kv_lora_norm, m_b_w_uk, m_b_w_uv,
               m_c_sink, m_final_norm]
    small_v = [v_norm_w, v_ada_b, v_a_q_norm, v_a_k_norm, v_b_q_lora_norm, v_b_kv_lora_norm, v_b_w_uk, v_b_w_uv,
               v_c_sink, v_final_norm]
    small_g = [d_norm_w, dmod, g_qn, g_kn, g_qln, g_kvln, None, None, dsink, d_final]
    flat2 = lambda a: a.reshape((1, -1)) if a.size == a.shape[-1] else a.reshape(a.shape[-3:] if a.ndim > 3 else a.shape)
    kshape = [flat2(w).shape for w in small_w]
    late = [i for i, g in enumerate(small_g) if g is not None]
    gathered = all_gather_slots(
        Gather([small_g[i].reshape(kshape[i]) for i in late] + [jnp.full((8, 128), loss_part, F32)]),
        "gather_small_grads")
    g_all = [None] * len(small_g)
    for i, g in zip(late, gathered):
        g_all[i] = g
    g_all[6], g_all[7] = (l.reshape((N_DEV,) + kshape[6]) for l in (l_uk, l_uv))
    sm_out, loss_sum = adamw_small(g_all, [flat2(a) for a in small_w], [flat2(a) for a in small_m],
                                   [flat2(a) for a in small_v], gathered[-1])
    loss = loss_sum[0, 0]
    sm = [{nm: p.reshape(w.shape) for nm, w, p in zip(small_names, small_w, outs)} for outs in sm_out]

    dmod_all = g_all[1].reshape(N_DEV, 2, N_DEV, wcols)
    dmod_cols = lax.dynamic_slice_in_dim(dmod_all, me_flat, 1, axis=2)[:, :, 0, :]
    pad16 = lambda a: jnp.concatenate([a, jnp.zeros_like(a)], axis=0)
    g_ada_w = ada_weight_grad(pad16(c_all), jnp.transpose(pad16(dmod_cols), (1, 0, 2)))
    rows_of = lambda a: a.reshape(-1, wcols)
    ada = adamw_rows(rows_of(g_ada_w)[None], rows_of(ada_w), rows_of(m_ada_w), rows_of(v_ada_w), "adamw_ada_w")
    ada = [p.reshape(ada_w.shape) for p in ada]

    bg = [{}, {}, {}, {}]
    for nm, landed, w, m, v, transposed in (
            ("even_w_in", l_in_e, even_w_in, m_even_w_in, v_even_w_in, True),
            ("b_w_uq", l_uq, b_w_uq, m_b_w_uq, v_b_w_uq, True),
            ("odd_w_in", l_in_o, odd_w_in, m_odd_w_in, v_odd_w_in, True),
            ("even_w_out", l_out_e, even_w_out, m_even_w_out, v_even_w_out, False),
            ("odd_w_out", l_out_o, odd_w_out, m_odd_w_out, v_odd_w_out, False)):
        view = _shard_t if transposed else (lambda a: a[0])
        res = adamw_rows(landed, view(w), view(m), view(v), "adamw_" + nm)
        for kind, p in enumerate(res):
            bg[kind][nm] = _unshard_t(p, w) if transposed else p[None]
    big_names = ["even_w_in", "odd_w_in", "even_w_out", "odd_w_out", "b_w_uq"]

    order = ["norm_w", "ada_w", "ada_b", "even_w_in", "a_q_norm", "a_k_norm", "b_q_lora_norm", "b_kv_lora_norm",
             "b_w_uq", "b_w_uk", "b_w_uv", "even_w_out", "odd_w_in", "c_sink", "odd_w_out", "final_norm"]

    def pick(kind):
        out = []
        for nm in order:
            if nm == "ada_w":
                out.append(ada[kind])
            elif nm in big_names:
                out.append(bg[kind][nm])
            else:
                out.append(sm[kind][nm])
        return out

    return (loss, dx0[None], *pick(0), *pick(1), *pick(2), *pick(3))
```
